```python
import jax, jax.numpy as jnp
from jax import lax
import numpy as np

D_MODEL = 2048
BATCH = 8
SEQ = 2048
DEPTH = 1

ATT_HEADS = 16
ATT_KV_HEADS = 2
HEAD_DIM = 64
ATT_WIDTH = ATT_HEADS * HEAD_DIM
KV_WIDTH = ATT_KV_HEADS * HEAD_DIM
GROUP = ATT_HEADS // ATT_KV_HEADS
WINDOW = 128
BLOCK = 128
ROT_DIM = HEAD_DIM // 4
ROPE_THETA = 500000.0
HG_HEADS = 8
HG_EXPAND = 128
HG_HEAD_V = 128
HG_F_WIDTH = HG_HEADS * HG_EXPAND
HG_V_WIDTH = HG_HEADS * HG_HEAD_V
CHUNK = 64
N_LB = DEPTH + 1
FFN_HIDDEN = ((8 * D_MODEL // 3 + 255) // 256) * 256
N_MOD = 6
IN_COLS = ATT_WIDTH + 2 * KV_WIDTH + 2 * HG_F_WIDTH + 2 * HG_V_WIDTH + 2 * D_MODEL
EPS = 1e-6

kernel_name = "hybrid_swa_sink_hgrn2_gated_block"


def rmsnorm(t, gain):
    t32 = t.astype(jnp.float32)
    y = t32 * lax.rsqrt(jnp.mean(t32 * t32, axis=-1, keepdims=True) + EPS)
    return (y * gain.astype(jnp.float32)).astype(t.dtype)


def rope_partial(t, cos, sin):
    half = ROT_DIM // 2
    t1 = t[..., :half].astype(jnp.float32)
    t2 = t[..., half:ROT_DIM].astype(jnp.float32)
    rot = jnp.concatenate([t1 * cos - t2 * sin, t2 * cos + t1 * sin], axis=-1)
    return jnp.concatenate([rot.astype(t.dtype), t[..., ROT_DIM:]], axis=-1)


def sliding_window_attention(q, k, v, sinks):
    B, S = q.shape[0], q.shape[1]
    nb = S // BLOCK
    qb = q.reshape(B, nb, BLOCK, ATT_KV_HEADS, GROUP, HEAD_DIM).astype(jnp.float32)
    kb = k.reshape(B, nb, BLOCK, ATT_KV_HEADS, HEAD_DIM).astype(jnp.float32)
    vb = v.reshape(B, nb, BLOCK, ATT_KV_HEADS, HEAD_DIM).astype(jnp.float32)
    prev = lambda t: jnp.concatenate([jnp.zeros_like(t[:, :1]), t[:, :-1]], axis=1)
    kk = jnp.concatenate([prev(kb), kb], axis=2)
    vv = jnp.concatenate([prev(vb), vb], axis=2)
    logits = jnp.einsum('bnqhgd,bnkhd->bnhgqk', qb, kk) * (HEAD_DIM ** -0.5)
    qi = jnp.arange(BLOCK)[:, None]
    kj = jnp.arange(2 * BLOCK)[None, :]
    rel = BLOCK + qi - kj
    band = (rel >= 0) & (rel < WINDOW)
    has_prev = (jnp.arange(nb) > 0)[:, None, None] | (kj >= BLOCK)[None]
    mask = band[None] & has_prev
    logits = jnp.where(mask[None, :, None, None], logits, -jnp.inf)
    sink = sinks.astype(jnp.float32).reshape(ATT_KV_HEADS, GROUP)[None, None, :, :, None, None]
    m = jnp.maximum(jnp.max(logits, axis=-1, keepdims=True), sink)
    p = jnp.exp(logits - m)
    denom = jnp.sum(p, axis=-1, keepdims=True) + jnp.exp(sink - m)
    out = jnp.einsum('bnhgqk,bnkhd->bnqhgd', p / denom, vv)
    return out.reshape(B, S, ATT_WIDTH).astype(q.dtype)


def hgrn2_recurrence(q, f_raw, i, lb):
    B, S = q.shape[0], q.shape[1]
    nc = S // CHUNK
    f = lb + (1.0 - lb) * jax.nn.sigmoid(f_raw.astype(jnp.float32))
    key = 1.0 - f
    logf = jnp.log(f)

    def chunks(t, d):
        return t.astype(jnp.float32).reshape(B, nc, CHUNK, HG_HEADS, d).transpose(1, 0, 3, 2, 4)

    qc = chunks(jax.nn.silu(q.astype(jnp.float32)), HG_EXPAND)
    kc = chunks(key, HG_EXPAND)
    vc = chunks(i, HG_HEAD_V)
    bc = jnp.cumsum(chunks(logf, HG_EXPAND), axis=3)
    causal = jnp.tril(jnp.ones((CHUNK, CHUNK), dtype=bool))[:, :, None]

    def step(state, xs):
        qt, kt, vt, bt = xs
        o_inter = jnp.einsum('bhck,bhkv->bhcv', qt * jnp.exp(bt), state)
        diff = bt[:, :, :, None, :] - bt[:, :, None, :, :]
        decay = jnp.where(causal, jnp.exp(jnp.where(causal, diff, 0.0)), 0.0)
        scores = jnp.einsum('bhtk,bhtsk,bhsk->bhts', qt, decay, kt)
        o = o_inter + jnp.einsum('bhts,bhsv->bhtv', scores, vt)
        b_last = bt[:, :, -1:, :]
        new_state = (jnp.exp(b_last[:, :, 0, :])[..., None] * state
                     + jnp.einsum('bhsk,bhsv->bhkv', kt * jnp.exp(b_last - bt), vt))
        return new_state, o

    state0 = jnp.zeros((B, HG_HEADS, HG_EXPAND, HG_HEAD_V), jnp.float32)
    _, o = lax.scan(step, state0, (qc, kc, vc, bc))
    return o.transpose(1, 0, 3, 2, 4).reshape(B, S, HG_HEADS, HG_HEAD_V)


def _fwd_setup_inputs(seed: int = 0) -> dict:
    key = jax.random.key(seed)
    ks = jax.random.split(key, 20)
    f32 = jnp.float32
    nrm = lambda k, shape, scale: (jax.random.normal(k, shape, f32) * scale)
    gain = lambda k, shape: 1.0 + 0.05 * jax.random.normal(k, shape, f32)
    offsets = jax.random.randint(ks[2], (BATCH, 1), 0, 4096, dtype=jnp.int32)
    positions = (offsets + jnp.arange(SEQ, dtype=jnp.int32)[None, :]).astype(jnp.int32)
    return {
        "x": nrm(ks[0], (BATCH, SEQ, D_MODEL), 1.0),
        "c": nrm(ks[1], (BATCH, D_MODEL), 1.0),
        "positions": positions,
        "w_ada": nrm(ks[3], (DEPTH, D_MODEL, N_MOD * D_MODEL), 0.5 * D_MODEL ** -0.5),
        "b_ada": nrm(ks[4], (DEPTH, N_MOD * D_MODEL), 0.02),
        "g_pre_mix": gain(ks[5], (DEPTH, D_MODEL)),
        "g_post_mix": gain(ks[6], (DEPTH, D_MODEL)),
        "g_pre_ffn": gain(ks[7], (DEPTH, D_MODEL)),
        "g_post_ffn": gain(ks[8], (DEPTH, D_MODEL)),
        "w_in": nrm(ks[9], (DEPTH, D_MODEL, IN_COLS), D_MODEL ** -0.5),
        "attn_sinks": nrm(ks[10], (DEPTH, ATT_HEADS), 1.0),
        "w_attn_proj": nrm(ks[11], (DEPTH, ATT_WIDTH, D_MODEL), ATT_WIDTH ** -0.5),
        "hg_lower_bounds": nrm(ks[12], (N_LB, HG_F_WIDTH), 0.5),
        "hg_norm": gain(ks[13], (DEPTH, HG_HEAD_V)),
        "w_hgrn_proj": nrm(ks[14], (DEPTH, HG_V_WIDTH, D_MODEL), HG_V_WIDTH ** -0.5),
        "w_out": nrm(ks[15], (DEPTH, D_MODEL, D_MODEL), D_MODEL ** -0.5),
        "w_ffn_in": nrm(ks[16], (DEPTH, D_MODEL, 2 * FFN_HIDDEN), D_MODEL ** -0.5),
        "w_ffn_out": nrm(ks[17], (DEPTH, FFN_HIDDEN, D_MODEL), FFN_HIDDEN ** -0.5),
    }


def _fwd_reference(x, c, positions, w_ada, b_ada, g_pre_mix, g_post_mix, g_pre_ffn, g_post_ffn,
              w_in, attn_sinks, w_attn_proj, hg_lower_bounds, hg_norm, w_hgrn_proj, w_out,
              w_ffn_in, w_ffn_out):
    B, S = x.shape[0], x.shape[1]
    inv_freq = ROPE_THETA ** (-jnp.arange(0, ROT_DIM, 2, dtype=jnp.float32) / ROT_DIM)
    ang = positions.astype(jnp.float32)[..., None] * inv_freq
    cos, sin = jnp.cos(ang)[:, :, None, :], jnp.sin(ang)[:, :, None, :]
    lb_table = jnp.cumsum(jax.nn.softmax(hg_lower_bounds.astype(jnp.float32), axis=0), axis=0)
    splits = [ATT_WIDTH, ATT_WIDTH + KV_WIDTH, ATT_WIDTH + 2 * KV_WIDTH,
              ATT_WIDTH + 2 * KV_WIDTH + HG_F_WIDTH,
              ATT_WIDTH + 2 * KV_WIDTH + 2 * HG_F_WIDTH,
              ATT_WIDTH + 2 * KV_WIDTH + 2 * HG_F_WIDTH + HG_V_WIDTH,
              ATT_WIDTH + 2 * KV_WIDTH + 2 * HG_F_WIDTH + 2 * HG_V_WIDTH,
              ATT_WIDTH + 2 * KV_WIDTH + 2 * HG_F_WIDTH + 2 * HG_V_WIDTH + D_MODEL]
    for l in range(DEPTH):
        mod = (c @ w_ada[l] + b_ada[l])[:, None, :]
        shift1, scale1, gate1, shift2, scale2, gate2 = jnp.split(mod, N_MOD, axis=-1)

        h = rmsnorm(x, g_pre_mix[l]) * (1.0 + scale1) + shift1
        proj = h @ w_in[l]
        q_a, k_a, v_a, q_h, f_h, i_h, g_h, gate_a, gate_h = jnp.split(proj, splits, axis=-1)
        qa = rope_partial(q_a.reshape(B, S, ATT_HEADS, HEAD_DIM), cos, sin)
        ka = rope_partial(k_a.reshape(B, S, ATT_KV_HEADS, HEAD_DIM), cos, sin)
        va = v_a.reshape(B, S, ATT_KV_HEADS, HEAD_DIM)
        y_a = sliding_window_attention(qa, ka, va, attn_sinks[l]) @ w_attn_proj[l]
        o_h = hgrn2_recurrence(q_h, f_h, i_h, lb_table[l])
        o_h = rmsnorm(o_h, hg_norm[l]).reshape(B, S, HG_V_WIDTH).astype(x.dtype)
        y_h = (o_h * jax.nn.sigmoid(g_h)) @ w_hgrn_proj[l]
        merged = jax.nn.sigmoid(gate_a) * y_a + jax.nn.sigmoid(gate_h) * y_h
        y = merged @ w_out[l]
        x = x + gate1 * rmsnorm(y, g_post_mix[l])

        h = rmsnorm(x, g_pre_ffn[l]) * (1.0 + scale2) + shift2
        gu = h @ w_ffn_in[l]
        g_ffn, u_ffn = jnp.split(gu, 2, axis=-1)
        y = (jax.nn.silu(g_ffn) * u_ffn) @ w_ffn_out[l]
        x = x + gate2 * rmsnorm(y, g_post_ffn[l])
    return x


import jax as _jax
import jax.numpy as _jnp

TWIN_FORMAT = 'train_step'
FWD_PARAMS = ['x', 'c', 'positions', 'w_ada', 'b_ada', 'g_pre_mix', 'g_post_mix', 'g_pre_ffn', 'g_post_ffn', 'w_in', 'attn_sinks', 'w_attn_proj', 'hg_lower_bounds', 'hg_norm', 'w_hgrn_proj', 'w_out', 'w_ffn_in', 'w_ffn_out']
TWIN_WEIGHTS = ['w_ada', 'b_ada', 'g_pre_mix', 'g_post_mix', 'g_pre_ffn', 'g_post_ffn', 'w_in', 'attn_sinks', 'w_attn_proj', 'hg_lower_bounds', 'hg_norm', 'w_hgrn_proj', 'w_out', 'w_ffn_in', 'w_ffn_out']
TWIN_DIFF_INPUT = 'x'
TWIN_INPUTS = ['x', 'c', 'positions', 'w_ada', 'b_ada', 'g_pre_mix', 'g_post_mix', 'g_pre_ffn', 'g_post_ffn', 'w_in', 'attn_sinks', 'w_attn_proj', 'hg_lower_bounds', 'hg_norm', 'w_hgrn_proj', 'w_out', 'w_ffn_in', 'w_ffn_out', 'loss_target', 'm_w_ada', 'm_b_ada', 'm_g_pre_mix', 'm_g_post_mix', 'm_g_pre_ffn', 'm_g_post_ffn', 'm_w_in', 'm_attn_sinks', 'm_w_attn_proj', 'm_hg_lower_bounds', 'm_hg_norm', 'm_w_hgrn_proj', 'm_w_out', 'm_w_ffn_in', 'm_w_ffn_out', 'v_w_ada', 'v_b_ada', 'v_g_pre_mix', 'v_g_post_mix', 'v_g_pre_ffn', 'v_g_post_ffn', 'v_w_in', 'v_attn_sinks', 'v_w_attn_proj', 'v_hg_lower_bounds', 'v_hg_norm', 'v_w_hgrn_proj', 'v_w_out', 'v_w_ffn_in', 'v_w_ffn_out']
TWIN_OUTPUTS = ['loss', 'grad_x', 'grad_w_ada', 'grad_b_ada', 'grad_g_pre_mix', 'grad_g_post_mix', 'grad_g_pre_ffn', 'grad_g_post_ffn', 'grad_w_in', 'grad_attn_sinks', 'grad_w_attn_proj', 'grad_hg_lower_bounds', 'grad_hg_norm', 'grad_w_hgrn_proj', 'grad_w_out', 'grad_w_ffn_in', 'grad_w_ffn_out', 'delta_w_ada', 'delta_b_ada', 'delta_g_pre_mix', 'delta_g_post_mix', 'delta_g_pre_ffn', 'delta_g_post_ffn', 'delta_w_in', 'delta_attn_sinks', 'delta_w_attn_proj', 'delta_hg_lower_bounds', 'delta_hg_norm', 'delta_w_hgrn_proj', 'delta_w_out', 'delta_w_ffn_in', 'delta_w_ffn_out', 'new_m_w_ada', 'new_m_b_ada', 'new_m_g_pre_mix', 'new_m_g_post_mix', 'new_m_g_pre_ffn', 'new_m_g_post_ffn', 'new_m_w_in', 'new_m_attn_sinks', 'new_m_w_attn_proj', 'new_m_hg_lower_bounds', 'new_m_hg_norm', 'new_m_w_hgrn_proj', 'new_m_w_out', 'new_m_w_ffn_in', 'new_m_w_ffn_out', 'new_v_w_ada', 'new_v_b_ada', 'new_v_g_pre_mix', 'new_v_g_post_mix', 'new_v_g_pre_ffn', 'new_v_g_post_ffn', 'new_v_w_in', 'new_v_attn_sinks', 'new_v_w_attn_proj', 'new_v_hg_lower_bounds', 'new_v_hg_norm', 'new_v_w_hgrn_proj', 'new_v_w_out', 'new_v_w_ffn_in', 'new_v_w_ffn_out']
TWIN_LEAF_KINDS = {'loss': 'loss', 'grad_x': 'grad_x', 'grad_w_ada': 'grad_w', 'grad_b_ada': 'grad_w', 'grad_g_pre_mix': 'grad_w', 'grad_g_post_mix': 'grad_w', 'grad_g_pre_ffn': 'grad_w', 'grad_g_post_ffn': 'grad_w', 'grad_w_in': 'grad_w', 'grad_attn_sinks': 'grad_w', 'grad_w_attn_proj': 'grad_w', 'grad_hg_lower_bounds': 'grad_w', 'grad_hg_norm': 'grad_w', 'grad_w_hgrn_proj': 'grad_w', 'grad_w_out': 'grad_w', 'grad_w_ffn_in': 'grad_w', 'grad_w_ffn_out': 'grad_w', 'delta_w_ada': 'delta_w', 'delta_b_ada': 'delta_w', 'delta_g_pre_mix': 'delta_w', 'delta_g_post_mix': 'delta_w', 'delta_g_pre_ffn': 'delta_w', 'delta_g_post_ffn': 'delta_w', 'delta_w_in': 'delta_w', 'delta_attn_sinks': 'delta_w', 'delta_w_attn_proj': 'delta_w', 'delta_hg_lower_bounds': 'delta_w', 'delta_hg_norm': 'delta_w', 'delta_w_hgrn_proj': 'delta_w', 'delta_w_out': 'delta_w', 'delta_w_ffn_in': 'delta_w', 'delta_w_ffn_out': 'delta_w', 'new_m_w_ada': 'new_m', 'new_m_b_ada': 'new_m', 'new_m_g_pre_mix': 'new_m', 'new_m_g_post_mix': 'new_m', 'new_m_g_pre_ffn': 'new_m', 'new_m_g_post_ffn': 'new_m', 'new_m_w_in': 'new_m', 'new_m_attn_sinks': 'new_m', 'new_m_w_attn_proj': 'new_m', 'new_m_hg_lower_bounds': 'new_m', 'new_m_hg_norm': 'new_m', 'new_m_w_hgrn_proj': 'new_m', 'new_m_w_out': 'new_m', 'new_m_w_ffn_in': 'new_m', 'new_m_w_ffn_out': 'new_m', 'new_v_w_ada': 'new_v', 'new_v_b_ada': 'new_v', 'new_v_g_pre_mix': 'new_v', 'new_v_g_post_mix': 'new_v', 'new_v_g_pre_ffn': 'new_v', 'new_v_g_post_ffn': 'new_v', 'new_v_w_in': 'new_v', 'new_v_attn_sinks': 'new_v', 'new_v_w_attn_proj': 'new_v', 'new_v_hg_lower_bounds': 'new_v', 'new_v_hg_norm': 'new_v', 'new_v_w_hgrn_proj': 'new_v', 'new_v_w_out': 'new_v', 'new_v_w_ffn_in': 'new_v', 'new_v_w_ffn_out': 'new_v'}


def _forward(args):
    return _fwd_reference(*[args[k] for k in FWD_PARAMS])


def _output_shape():
    out = _jax.eval_shape(lambda: _forward(_fwd_setup_inputs(0)))
    return out.shape, out.dtype

N_MICROBATCH = 1
ADAM_LR = 0.001
ADAM_B1 = 0.9
ADAM_B2 = 0.999
ADAM_EPS = 1e-08
ADAM_WD = 0.01
ADAM_STEP = 10
PER_EXAMPLE_BATCH_AXIS = {'x': 0, 'c': 0, 'positions': 0, 'loss_target': 0}
SHARED_INPUTS = []
_WEIGHT_DTYPES = {'w_ada': _jnp.float32, 'b_ada': _jnp.float32, 'g_pre_mix': _jnp.float32, 'g_post_mix': _jnp.float32, 'g_pre_ffn': _jnp.float32, 'g_post_ffn': _jnp.float32, 'w_in': _jnp.float32, 'attn_sinks': _jnp.float32, 'w_attn_proj': _jnp.float32, 'hg_lower_bounds': _jnp.float32, 'hg_norm': _jnp.float32, 'w_hgrn_proj': _jnp.float32, 'w_out': _jnp.float32, 'w_ffn_in': _jnp.float32, 'w_ffn_out': _jnp.float32}
MOMENT_SCALE = {'w_ada': 1.570729e+00, 'b_ada': 1.581449e+00, 'g_pre_mix': 7.497865e-02, 'g_post_mix': 2.899228e+00, 'g_pre_ffn': 1.701020e-01, 'g_post_ffn': 2.408049e+00, 'w_in': 4.210011e-01, 'attn_sinks': 3.049057e-02, 'w_attn_proj': 7.059840e-01, 'hg_lower_bounds': 1.243794e-02, 'hg_norm': 2.182959e+00, 'w_hgrn_proj': 5.402395e-01, 'w_out': 8.984631e-01, 'w_ffn_in': 1.263375e-01, 'w_ffn_out': 2.328649e-01}


def _to_microbatches(a, axis):
    t = _jnp.moveaxis(a, axis, 0)
    t = t.reshape((N_MICROBATCH, t.shape[0] // N_MICROBATCH) + t.shape[1:])
    return _jnp.moveaxis(t, 1, axis + 1)


def setup_inputs(seed: int = 0) -> dict:
    inp = _fwd_setup_inputs(seed)
    key = _jax.random.fold_in(_jax.random.key(seed), 7919)
    shape, _ = _output_shape()
    out = dict(inp)
    out["loss_target"] = _jax.random.normal(_jax.random.fold_in(key, 0), shape, _jnp.float32)
    for i, name in enumerate(TWIN_WEIGHTS):
        w = inp[name].astype(_jnp.float32)
        if MOMENT_SCALE is None:
            s = _jnp.sqrt(_jnp.mean(_jnp.square(w)) + 1e-30)
        else:
            s = MOMENT_SCALE[name]
        km, kv = _jax.random.split(_jax.random.fold_in(key, i + 1))
        out[name] = w
        out["m_" + name] = s * _jax.random.normal(km, w.shape, _jnp.float32)
        out["v_" + name] = (s * s) * _jax.random.uniform(kv, w.shape, _jnp.float32, 0.5, 1.5)
    if N_MICROBATCH > 1:
        for name, axis in PER_EXAMPLE_BATCH_AXIS.items():
            out[name] = _to_microbatches(out[name], axis)
    return {'x': out['x'], 'c': out['c'], 'positions': out['positions'], 'w_ada': out['w_ada'], 'b_ada': out['b_ada'], 'g_pre_mix': out['g_pre_mix'], 'g_post_mix': out['g_post_mix'], 'g_pre_ffn': out['g_pre_ffn'], 'g_post_ffn': out['g_post_ffn'], 'w_in': out['w_in'], 'attn_sinks': out['attn_sinks'], 'w_attn_proj': out['w_attn_proj'], 'hg_lower_bounds': out['hg_lower_bounds'], 'hg_norm': out['hg_norm'], 'w_hgrn_proj': out['w_hgrn_proj'], 'w_out': out['w_out'], 'w_ffn_in': out['w_ffn_in'], 'w_ffn_out': out['w_ffn_out'], 'loss_target': out['loss_target'], 'm_w_ada': out['m_w_ada'], 'm_b_ada': out['m_b_ada'], 'm_g_pre_mix': out['m_g_pre_mix'], 'm_g_post_mix': out['m_g_post_mix'], 'm_g_pre_ffn': out['m_g_pre_ffn'], 'm_g_post_ffn': out['m_g_post_ffn'], 'm_w_in': out['m_w_in'], 'm_attn_sinks': out['m_attn_sinks'], 'm_w_attn_proj': out['m_w_attn_proj'], 'm_hg_lower_bounds': out['m_hg_lower_bounds'], 'm_hg_norm': out['m_hg_norm'], 'm_w_hgrn_proj': out['m_w_hgrn_proj'], 'm_w_out': out['m_w_out'], 'm_w_ffn_in': out['m_w_ffn_in'], 'm_w_ffn_out': out['m_w_ffn_out'], 'v_w_ada': out['v_w_ada'], 'v_b_ada': out['v_b_ada'], 'v_g_pre_mix': out['v_g_pre_mix'], 'v_g_post_mix': out['v_g_post_mix'], 'v_g_pre_ffn': out['v_g_pre_ffn'], 'v_g_post_ffn': out['v_g_post_ffn'], 'v_w_in': out['v_w_in'], 'v_attn_sinks': out['v_attn_sinks'], 'v_w_attn_proj': out['v_w_attn_proj'], 'v_hg_lower_bounds': out['v_hg_lower_bounds'], 'v_hg_norm': out['v_hg_norm'], 'v_w_hgrn_proj': out['v_w_hgrn_proj'], 'v_w_out': out['v_w_out'], 'v_w_ffn_in': out['v_w_ffn_in'], 'v_w_ffn_out': out['v_w_ffn_out']}


def _loss(weights, diff, rest, loss_target):
    with _jax.named_scope("forward"):
        args = {**rest, TWIN_DIFF_INPUT: diff, **{k: w.astype(_WEIGHT_DTYPES[k]) for k, w in weights.items()}}
        y = _forward(args)
    with _jax.named_scope("loss_head"):
        err = _jnp.square(y.astype(_jnp.float32) - loss_target)
        return 0.5 * _jnp.sum(_jnp.mean(err, axis=-1)) if err.ndim else 0.5 * err


def _adamw(w, g, m, v):
    m = ADAM_B1 * m + (1.0 - ADAM_B1) * g
    v = ADAM_B2 * v + (1.0 - ADAM_B2) * _jnp.square(g)
    m_hat = m / (1.0 - ADAM_B1 ** ADAM_STEP)
    v_hat = v / (1.0 - ADAM_B2 ** ADAM_STEP)
    delta = -ADAM_LR * (m_hat / (_jnp.sqrt(v_hat) + ADAM_EPS) + ADAM_WD * w)
    return delta, m, v


def reference(x, c, positions, w_ada, b_ada, g_pre_mix, g_post_mix, g_pre_ffn, g_post_ffn, w_in, attn_sinks, w_attn_proj, hg_lower_bounds, hg_norm, w_hgrn_proj, w_out, w_ffn_in, w_ffn_out, loss_target, m_w_ada, m_b_ada, m_g_pre_mix, m_g_post_mix, m_g_pre_ffn, m_g_post_ffn, m_w_in, m_attn_sinks, m_w_attn_proj, m_hg_lower_bounds, m_hg_norm, m_w_hgrn_proj, m_w_out, m_w_ffn_in, m_w_ffn_out, v_w_ada, v_b_ada, v_g_pre_mix, v_g_post_mix, v_g_pre_ffn, v_g_post_ffn, v_w_in, v_attn_sinks, v_w_attn_proj, v_hg_lower_bounds, v_hg_norm, v_w_hgrn_proj, v_w_out, v_w_ffn_in, v_w_ffn_out):
    given = dict(x=x, c=c, positions=positions, w_ada=w_ada, b_ada=b_ada, g_pre_mix=g_pre_mix, g_post_mix=g_post_mix, g_pre_ffn=g_pre_ffn, g_post_ffn=g_post_ffn, w_in=w_in, attn_sinks=attn_sinks, w_attn_proj=w_attn_proj, hg_lower_bounds=hg_lower_bounds, hg_norm=hg_norm, w_hgrn_proj=w_hgrn_proj, w_out=w_out, w_ffn_in=w_ffn_in, w_ffn_out=w_ffn_out, loss_target=loss_target, m_w_ada=m_w_ada, m_b_ada=m_b_ada, m_g_pre_mix=m_g_pre_mix, m_g_post_mix=m_g_post_mix, m_g_pre_ffn=m_g_pre_ffn, m_g_post_ffn=m_g_post_ffn, m_w_in=m_w_in, m_attn_sinks=m_attn_sinks, m_w_attn_proj=m_w_attn_proj, m_hg_lower_bounds=m_hg_lower_bounds, m_hg_norm=m_hg_norm, m_w_hgrn_proj=m_w_hgrn_proj, m_w_out=m_w_out, m_w_ffn_in=m_w_ffn_in, m_w_ffn_out=m_w_ffn_out, v_w_ada=v_w_ada, v_b_ada=v_b_ada, v_g_pre_mix=v_g_pre_mix, v_g_post_mix=v_g_post_mix, v_g_pre_ffn=v_g_pre_ffn, v_g_post_ffn=v_g_post_ffn, v_w_in=v_w_in, v_attn_sinks=v_attn_sinks, v_w_attn_proj=v_w_attn_proj, v_hg_lower_bounds=v_hg_lower_bounds, v_hg_norm=v_hg_norm, v_w_hgrn_proj=v_w_hgrn_proj, v_w_out=v_w_out, v_w_ffn_in=v_w_ffn_in, v_w_ffn_out=v_w_ffn_out)
    weights = {n: given[n] for n in TWIN_WEIGHTS}
    shared = {n: given[n] for n in SHARED_INPUTS}
    per_example = {n: given[n] for n in ['x', 'c', 'positions']}
    grad_fn = _jax.value_and_grad(_loss, argnums=(0, 1))

    def one_microbatch(ex, loss_target):
        ex = dict(ex)
        diff = ex.pop(TWIN_DIFF_INPUT)
        return grad_fn(weights, diff, {**shared, **ex}, loss_target)

    if N_MICROBATCH == 1:
        loss, (grad_w, grad_x) = one_microbatch(per_example, given["loss_target"])
    else:
        def body(carry, xs):
            loss_sum, grad_sum = carry
            l_k, (gw_k, gx_k) = one_microbatch(xs[0], xs[1])
            with _jax.named_scope("update"):
                return (loss_sum + l_k, _jax.tree.map(_jnp.add, grad_sum, gw_k)), gx_k

        init = (_jnp.zeros((), _jnp.float32), _jax.tree.map(_jnp.zeros_like, weights))
        (loss, grad_w), grad_x = _jax.lax.scan(body, init, (per_example, given["loss_target"]))
    with _jax.named_scope("update"):
        delta_w, new_m, new_v = {}, {}, {}
        for n in TWIN_WEIGHTS:
            delta_w[n], new_m[n], new_v[n] = _adamw(weights[n], grad_w[n], given["m_" + n], given["v_" + n])
    return (loss, grad_x, *[grad_w[n] for n in TWIN_WEIGHTS], *[delta_w[n] for n in TWIN_WEIGHTS],
            *[new_m[n] for n in TWIN_WEIGHTS], *[new_v[n] for n in TWIN_WEIGHTS])
```

```python
import functools

import jax
import jax.numpy as jnp
from jax import lax
from jax.experimental import pallas as pl
from jax.experimental.pallas import tpu as pltpu

F32 = jnp.float32
BF16 = jnp.bfloat16

N_DEV = 8
D = 2048
ATT_HEADS = 16
KV_HEADS = 2
HEAD_DIM = 64
GROUP = ATT_HEADS // KV_HEADS
ATT_W = ATT_HEADS * HEAD_DIM
BLK = 128
ROT = HEAD_DIM // 4
ROPE_THETA = 500000.0
HG_HEADS = 8
HG_K = 128
HG_W = HG_HEADS * HG_K
CHUNK = 64
FFN = 5632
N_MOD = 6
EPS = 1e-6
LANE = 128
Q_A, K_A, V_A, Q_H, F_H, I_H, G_H, GT_A, GT_H, IN_COLS = 0, 1024, 1152, 1280, 2304, 3328, 4352, 5376, 7424, 9472

ADAM_LR, ADAM_B1, ADAM_B2, ADAM_EPS, ADAM_WD, ADAM_STEP = 0.001, 0.9, 0.999, 1e-08, 0.01, 10

TR = 256
HG_TB = 512
VMEM_BIG = 56 << 20
MESH = pl.DeviceIdType.MESH


def _sds(shape, dtype):
    return jax.ShapeDtypeStruct(shape, dtype)


def _params(n_axes, vmem=None):
    return pltpu.CompilerParams(dimension_semantics=("arbitrary",) * n_axes, vmem_limit_bytes=vmem)


def _sig(t):
    return 1.0 / (1.0 + jnp.exp(-t))


def _dot(a, b, dims):
    return lax.dot_general(a, b, (dims, ((), ())), preferred_element_type=F32)


NN = ((1,), (0,))
NT = ((1,), (1,))
TN = ((0,), (0,))


def _matmul(a, b, a_spec, b_spec, o_spec, out_shape, grid, dims, acc_shape, name):
    nk = grid[2]

    def body(a_ref, b_ref, o_ref, *scratch):
        part = _dot(a_ref[...], b_ref[...], dims)
        if nk == 1:
            o_ref[...] = part.astype(o_ref.dtype)
        else:
            acc = scratch[0]
            k = pl.program_id(2)

            @pl.when(k == 0)
            def _():
                acc[...] = part

            @pl.when(k > 0)
            def _():
                acc[...] += part

            @pl.when(k == nk - 1)
            def _():
                o_ref[...] = acc[...].astype(o_ref.dtype)

    return pl.pallas_call(
        body, grid=grid, in_specs=[a_spec, b_spec], out_specs=o_spec, out_shape=out_shape,
        scratch_shapes=[pltpu.VMEM(acc_shape, F32)] if nk > 1 else [],
        compiler_params=_params(3, VMEM_BIG), name=name)(a, b)


def _mm_nn(a, b, tm, tn, tk, out_dtype, name):
    m, k = a.shape
    n = b.shape[1]
    return _matmul(a, b, pl.BlockSpec((tm, tk), lambda j, i, kk: (i, kk)), pl.BlockSpec((tk, tn), lambda j, i, kk: (kk, j)),
                   pl.BlockSpec((tm, tn), lambda j, i, kk: (i, j)), _sds((m, n), out_dtype),
                   (n // tn, m // tm, k // tk), NN, (tm, tn), name)


def _mm_nn_dm(a, b, tm, out_dtype, name):
    m, k = a.shape
    n = b.shape[2]
    return _matmul(a, b, pl.BlockSpec((tm, k), lambda j, i, kk: (i, 0)), pl.BlockSpec((None, k, n), lambda j, i, kk: (j, 0, 0)),
                   pl.BlockSpec((tm, n), lambda j, i, kk: (i, j)), _sds((m, N_DEV * n), out_dtype),
                   (N_DEV, m // tm, 1), NN, (tm, n), name)


def _mm_nt(a, b, tm, tn, tk, out_dtype, name):
    m, k = a.shape
    n = b.shape[0]
    return _matmul(a, b, pl.BlockSpec((tm, tk), lambda j, i, kk: (i, kk)), pl.BlockSpec((tn, tk), lambda j, i, kk: (j, kk)),
                   pl.BlockSpec((tm, tn), lambda j, i, kk: (i, j)), _sds((m, n), out_dtype),
                   (n // tn, m // tm, k // tk), NT, (tm, tn), name)


def _mm_nt_dm(a, b, tm, tn, out_dtype, name):
    m = a.shape[0]
    n_out, n = b.shape[1], b.shape[2]
    return _matmul(a, b, pl.BlockSpec((tm, n), lambda j, i, kk: (i, kk)), pl.BlockSpec((None, tn, n), lambda j, i, kk: (kk, j, 0)),
                   pl.BlockSpec((tm, tn), lambda j, i, kk: (i, j)), _sds((m, n_out), out_dtype),
                   (n_out // tn, m // tm, N_DEV), NT, (tm, tn), name)


def _mm_nt_dm2(a, b, tm, tn, out_dtype, name):
    m = a.shape[1]
    n_out, n = b.shape[1], b.shape[2]
    return _matmul(a, b, pl.BlockSpec((None, tm, n), lambda j, i, kk: (kk, i, 0)), pl.BlockSpec((None, tn, n), lambda j, i, kk: (kk, j, 0)),
                   pl.BlockSpec((tm, tn), lambda j, i, kk: (i, j)), _sds((m, n_out), out_dtype),
                   (n_out // tn, m // tm, N_DEV), NT, (tm, tn), name)


def _mm_tn(a, b, tm, tn, out_dtype, name):
    s, m = a.shape
    n = b.shape[1]
    return _matmul(a, b, pl.BlockSpec((s, tm), lambda j, i, kk: (0, i)), pl.BlockSpec((s, tn), lambda j, i, kk: (0, j)),
                   pl.BlockSpec((tm, tn), lambda j, i, kk: (i, j)), _sds((m, n), out_dtype),
                   (n // tn, m // tm, 1), TN, (tm, tn), name)


def _mm_tn_dm(a, b, tm, out_dtype, name, b_dm=False):
    s, m = a.shape
    if b_dm:
        n = b.shape[2]
        b_spec = pl.BlockSpec((None, s, n), lambda j, i, kk: (j, 0, 0))
    else:
        n = b.shape[1] // N_DEV
        b_spec = pl.BlockSpec((s, n), lambda j, i, kk: (0, j))
    return _matmul(a, b, pl.BlockSpec((s, tm), lambda j, i, kk: (0, i)), b_spec,
                   pl.BlockSpec((None, tm, n), lambda j, i, kk: (j, i, 0)), _sds((N_DEV, m, n), out_dtype),
                   (N_DEV, m // tm, 1), TN, (tm, n), name)


def _row_spec():
    return pl.BlockSpec((TR, D), lambda i: (i, 0))


def _vec_spec(k=0):
    return pl.BlockSpec((1, D), lambda i: (0, k))


def _acc_rows(ref, first, val):
    @pl.when(first)
    def _():
        ref[...] = val

    @pl.when(jnp.logical_not(first))
    def _():
        ref[...] += val


def _pre_fwd(x, g, mod, k_scale, k_shift, name):
    s = x.shape[0]

    def body(x_ref, g_ref, sc_ref, sh_ref, h_ref):
        xv = x_ref[...]
        r = lax.rsqrt(jnp.mean(xv * xv, axis=-1, keepdims=True) + EPS)
        n = xv * r * g_ref[...]
        h_ref[...] = (n * (1.0 + sc_ref[...]) + sh_ref[...]).astype(h_ref.dtype)

    return pl.pallas_call(body, grid=(s // TR,), in_specs=[_row_spec(), _vec_spec(), _vec_spec(k_scale), _vec_spec(k_shift)],
                          out_specs=_row_spec(), out_shape=_sds((s, D), BF16), compiler_params=_params(1), name=name)(x, g, mod, mod)


def _post_fwd(x, y, g, mod, k_gate, name):
    s = x.shape[0]

    def body(x_ref, y_ref, g_ref, gt_ref, o_ref):
        yv = y_ref[...]
        r = lax.rsqrt(jnp.mean(yv * yv, axis=-1, keepdims=True) + EPS)
        o_ref[...] = x_ref[...] + gt_ref[...] * (yv * r * g_ref[...])

    return pl.pallas_call(body, grid=(s // TR,), in_specs=[_row_spec(), _row_spec(), _vec_spec(), _vec_spec(k_gate)],
                          out_specs=_row_spec(), out_shape=_sds((s, D), F32), compiler_params=_params(1), name=name)(x, y, g, mod)


def _post_fwd_loss(x, y, g, mod, k_gate, tgt, name):
    s = x.shape[0]

    def body(x_ref, y_ref, g_ref, gt_ref, t_ref, e_ref, loss_ref):
        i = pl.program_id(0)
        yv = y_ref[...]
        r = lax.rsqrt(jnp.mean(yv * yv, axis=-1, keepdims=True) + EPS)
        err = x_ref[...] + gt_ref[...] * (yv * r * g_ref[...]) - t_ref[...]
        e_ref[...] = err * (1.0 / D)
        part = 0.5 * jnp.sum(jnp.mean(err * err, axis=-1, keepdims=True), axis=0, keepdims=True)
        _acc_rows(loss_ref, i == 0, part)

    return pl.pallas_call(body, grid=(s // TR,),
                          in_specs=[_row_spec(), _row_spec(), _vec_spec(), _vec_spec(k_gate), _row_spec()],
                          out_specs=[_row_spec(), pl.BlockSpec((1, 1), lambda i: (0, 0))],
                          out_shape=[_sds((s, D), F32), _sds((1, 1), F32)], compiler_params=_params(1), name=name)(x, y, g, mod, tgt)


def _pre_bwd(dh, x, res, g, mod, k_scale, name):
    s = x.shape[0]

    def body(dh_ref, x_ref, res_ref, g_ref, sc_ref, dx_ref, dsh_ref, dsc_ref, dg_ref):
        first = pl.program_id(0) == 0
        xv, dh_v, gv = x_ref[...], dh_ref[...], g_ref[...]
        r = lax.rsqrt(jnp.mean(xv * xv, axis=-1, keepdims=True) + EPS)
        xh = xv * r
        dn = dh_v * (1.0 + sc_ref[...])
        dgn = dn * gv
        dx_ref[...] = res_ref[...] + r * (dgn - xh * jnp.mean(dgn * xh, axis=-1, keepdims=True))
        _acc_rows(dsh_ref, first, jnp.sum(dh_v, axis=0, keepdims=True))
        _acc_rows(dsc_ref, first, jnp.sum(dh_v * (xh * gv), axis=0, keepdims=True))
        _acc_rows(dg_ref, first, jnp.sum(dn * xh, axis=0, keepdims=True))

    return pl.pallas_call(body, grid=(s // TR,),
                          in_specs=[_row_spec(), _row_spec(), _row_spec(), _vec_spec(), _vec_spec(k_scale)],
                          out_specs=[_row_spec(), _vec_spec(), _vec_spec(), _vec_spec()],
                          out_shape=[_sds((s, D), F32)] + [_sds((1, D), F32)] * 3,
                          compiler_params=_params(1), name=name)(dh, x, res, g, mod)


def _post_bwd(dx, y, g, mod, k_gate, name):
    s = y.shape[0]

    def body(dx_ref, y_ref, g_ref, gt_ref, dy_ref, dgt_ref, dg_ref):
        first = pl.program_id(0) == 0
        yv, dxv, gv = y_ref[...], dx_ref[...], g_ref[...]
        r = lax.rsqrt(jnp.mean(yv * yv, axis=-1, keepdims=True) + EPS)
        yh = yv * r
        dn = dxv * gt_ref[...]
        dgn = dn * gv
        dy_ref[...] = (r * (dgn - yh * jnp.mean(dgn * yh, axis=-1, keepdims=True))).astype(dy_ref.dtype)
        _acc_rows(dgt_ref, first, jnp.sum(dxv * (yh * gv), axis=0, keepdims=True))
        _acc_rows(dg_ref, first, jnp.sum(dn * yh, axis=0, keepdims=True))

    return pl.pallas_call(body, grid=(s // TR,), in_specs=[_row_spec(), _row_spec(), _vec_spec(), _vec_spec(k_gate)],
                          out_specs=[_row_spec(), _vec_spec(), _vec_spec()],
                          out_shape=[_sds((s, D), BF16), _sds((1, D), F32), _sds((1, D), F32)],
                          compiler_params=_params(1), name=name)(dx, y, g, mod)


SW_TN = 512


def _swiglu_fwd(gu):
    s = gu.shape[0]
    nb = FFN // SW_TN

    def body(g_ref, u_ref, a_ref):
        gv = g_ref[...]
        a_ref[...] = (gv * _sig(gv) * u_ref[...]).astype(a_ref.dtype)

    return pl.pallas_call(body, grid=(s // TR, nb),
                          in_specs=[pl.BlockSpec((TR, SW_TN), lambda i, j: (i, j)), pl.BlockSpec((TR, SW_TN), lambda i, j: (i, j + nb))],
                          out_specs=pl.BlockSpec((TR, SW_TN), lambda i, j: (i, j)), out_shape=_sds((s, FFN), BF16),
                          compiler_params=_params(2), name="swiglu_fwd")(gu, gu)


def _swiglu_bwd(dact, gu):
    s = gu.shape[0]
    nb = FFN // SW_TN

    def body(da_ref, g_ref, u_ref, o_ref):
        half = pl.program_id(2)
        gv, da = g_ref[...], da_ref[...]
        sg = _sig(gv)
        d_gate = da * u_ref[...] * (sg * (1.0 + gv * (1.0 - sg)))
        d_up = da * (gv * sg)
        o_ref[...] = jnp.where(half == 0, d_gate, d_up).astype(o_ref.dtype)

    blk = lambda f: pl.BlockSpec((TR, SW_TN), f)
    return pl.pallas_call(body, grid=(s // TR, nb, 2),
                          in_specs=[blk(lambda i, j, h: (i, j)), blk(lambda i, j, h: (i, j)), blk(lambda i, j, h: (i, j + nb))],
                          out_specs=blk(lambda i, j, h: (i, j + nb * h)), out_shape=_sds((s, 2 * FFN), BF16),
                          compiler_params=_params(3), name="swiglu_bwd")(dact, gu, gu)


MG_TN = 256


def _merge_fwd(y_a, y_h, proj):
    s = y_a.shape[0]
    tn = MG_TN
    ba, bh = GT_A // tn, GT_H // tn

    def body(ya_ref, yh_ref, ga_ref, gh_ref, m_ref):
        m_ref[...] = (_sig(ga_ref[...]) * ya_ref[...] + _sig(gh_ref[...]) * yh_ref[...]).astype(m_ref.dtype)

    blk = lambda f: pl.BlockSpec((TR, tn), f)
    return pl.pallas_call(body, grid=(s // TR, D // tn),
                          in_specs=[blk(lambda i, j: (i, j)), blk(lambda i, j: (i, j)), blk(lambda i, j: (i, j + ba)), blk(lambda i, j: (i, j + bh))],
                          out_specs=blk(lambda i, j: (i, j)), out_shape=_sds((s, D), BF16),
                          compiler_params=_params(2), name="merge_fwd")(y_a, y_h, proj, proj)


def _merge_bwd(dm, y_a, y_h, proj):
    s = y_a.shape[0]
    tn = MG_TN
    ba, bh = GT_A // tn, GT_H // tn

    def body(dm_ref, ya_ref, yh_ref, ga_ref, gh_ref, dya_ref, dyh_ref, dga_ref, dgh_ref):
        dmv = dm_ref[...]
        sa, sh = _sig(ga_ref[...]), _sig(gh_ref[...])
        dya_ref[...] = (dmv * sa).astype(BF16)
        dyh_ref[...] = (dmv * sh).astype(BF16)
        dga_ref[...] = (dmv * ya_ref[...] * (sa * (1.0 - sa))).astype(BF16)
        dgh_ref[...] = (dmv * yh_ref[...] * (sh * (1.0 - sh))).astype(BF16)

    blk = lambda f: pl.BlockSpec((TR, tn), f)
    nat = blk(lambda i, j: (i, j))
    return pl.pallas_call(body, grid=(s // TR, D // tn),
                          in_specs=[nat, nat, nat, blk(lambda i, j: (i, j + ba)), blk(lambda i, j: (i, j + bh))],
                          out_specs=[nat] * 4, out_shape=[_sds((s, D), BF16)] * 4,
                          compiler_params=_params(2), name="merge_bwd")(dm, y_a, y_h, proj, proj)


def _hgout_fwd(o_raw, proj, hg_norm):
    s = o_raw.shape[0]
    bg = G_H // LANE

    def body(o_ref, g_ref, n_ref, out_ref):
        ov = o_ref[...]
        r = lax.rsqrt(jnp.mean(ov * ov, axis=-1, keepdims=True) + EPS)
        out_ref[...] = (ov * r * n_ref[...] * _sig(g_ref[...])).astype(out_ref.dtype)

    blk = lambda f: pl.BlockSpec((TR, LANE), f)
    return pl.pallas_call(body, grid=(s // TR, HG_HEADS),
                          in_specs=[blk(lambda i, h: (i, h)), blk(lambda i, h: (i, h + bg)), pl.BlockSpec((1, LANE), lambda i, h: (0, 0))],
                          out_specs=blk(lambda i, h: (i, h)), out_shape=_sds((s, HG_W), BF16),
                          compiler_params=_params(2), name="hgout_fwd")(o_raw, proj, hg_norm)


def _hgout_bwd(d_out, o_raw, proj, hg_norm):
    s = o_raw.shape[0]
    bg = G_H // LANE

    def body(d_ref, o_ref, g_ref, n_ref, do_ref, dg_ref, dn_ref):
        first = jnp.logical_and(pl.program_id(0) == 0, pl.program_id(1) == 0)
        ov, dv, nv = o_ref[...], d_ref[...], n_ref[...]
        sg = _sig(g_ref[...])
        r = lax.rsqrt(jnp.mean(ov * ov, axis=-1, keepdims=True) + EPS)
        oh = ov * r
        d_on = dv * sg
        dg_ref[...] = (dv * (oh * nv) * (sg * (1.0 - sg))).astype(dg_ref.dtype)
        t = d_on * nv
        do_ref[...] = r * (t - oh * jnp.mean(t * oh, axis=-1, keepdims=True))
        _acc_rows(dn_ref, first, jnp.sum(d_on * oh, axis=0, keepdims=True))

    blk = lambda f: pl.BlockSpec((TR, LANE), f)
    vec = pl.BlockSpec((1, LANE), lambda i, h: (0, 0))
    return pl.pallas_call(body, grid=(s // TR, HG_HEADS),
                          in_specs=[blk(lambda i, h: (i, h)), blk(lambda i, h: (i, h)), blk(lambda i, h: (i, h + bg)), vec],
                          out_specs=[blk(lambda i, h: (i, h)), blk(lambda i, h: (i, h)), vec],
                          out_shape=[_sds((s, HG_W), F32), _sds((s, HG_W), BF16), _sds((1, LANE), F32)],
                          compiler_params=_params(2), name="hgout_bwd")(d_out, o_raw, proj, hg_norm)


def _rope(t, cos, s_lo, s_hi):
    return t * cos + pltpu.roll(t, LANE - ROT // 2, 1) * s_lo + pltpu.roll(t, ROT // 2, 1) * s_hi


def _rope_wide(t, cos, s_lo, s_hi):
    return jnp.concatenate([_rope(t[:, k * LANE:(k + 1) * LANE], cos, s_lo, s_hi) for k in range(t.shape[1] // LANE)], axis=1)


def _attn_mask(has_prev):
    qi = lax.broadcasted_iota(jnp.int32, (BLK, 2 * BLK), 0)
    kj = lax.broadcasted_iota(jnp.int32, (BLK, 2 * BLK), 1)
    rel = BLK + qi - kj
    band = jnp.logical_and(rel >= 0, rel < BLK)
    return jnp.logical_and(band, jnp.logical_or(has_prev, kj >= BLK))


def _attn_specs():
    prev = lambda i: jnp.maximum(i - 1, 0)
    kb, vb = K_A // LANE, V_A // LANE
    blk = lambda f: pl.BlockSpec((BLK, LANE), f)
    tabs = [blk(lambda i: (i, 0))] * 3 + [blk(lambda i: (prev(i), 0))] * 3
    return [pl.BlockSpec((BLK, ATT_W), lambda i: (i, 0)), blk(lambda i: (i, kb)), blk(lambda i: (prev(i), kb)),
            blk(lambda i: (i, vb)), blk(lambda i: (prev(i), vb))] + tabs + [pl.BlockSpec((1, LANE), lambda i: (0, 0))]


def _attn_probs(qh, kg, mask, sk):
    logits = _dot(qh, kg, NT) * (HEAD_DIM ** -0.5)
    logits = jnp.where(mask, logits, -jnp.inf)
    m = jnp.maximum(jnp.max(logits, axis=-1, keepdims=True), sk)
    p = jnp.exp(logits - m)
    e_sink = jnp.exp(sk - m)
    inv = 1.0 / (jnp.sum(p, axis=-1, keepdims=True) + e_sink)
    return p * inv, e_sink * inv


def _attn_fwd(proj, tabs, sinks):
    s = proj.shape[0]

    def body(q_ref, kc_ref, kp_ref, vc_ref, vp_ref, c0, l0, h0, c1, l1, h1, sk_ref, o_ref):
        i = pl.program_id(0)
        mask = _attn_mask(i > 0)
        q = _rope_wide(q_ref[...], c0[...], l0[...], h0[...]).astype(BF16)
        kk = jnp.concatenate([_rope(kp_ref[...], c1[...], l1[...], h1[...]), _rope(kc_ref[...], c0[...], l0[...], h0[...])], axis=0).astype(BF16)
        vv = jnp.concatenate([vp_ref[...], vc_ref[...]], axis=0).astype(BF16)
        outs = []
        for h in range(ATT_HEADS):
            g = h // GROUP
            prob, _ = _attn_probs(q[:, h * HEAD_DIM:(h + 1) * HEAD_DIM], kk[:, g * HEAD_DIM:(g + 1) * HEAD_DIM], mask, sk_ref[:, h:h + 1])
            outs.append(_dot(prob.astype(BF16), vv[:, g * HEAD_DIM:(g + 1) * HEAD_DIM], NN))
        o_ref[...] = jnp.concatenate(outs, axis=1).astype(o_ref.dtype)

    return pl.pallas_call(body, grid=(s // BLK,), in_specs=_attn_specs(),
                          out_specs=pl.BlockSpec((BLK, ATT_W), lambda i: (i, 0)), out_shape=_sds((s, ATT_W), BF16),
                          compiler_params=_params(1), name="attn_fwd")(proj, proj, proj, proj, proj, *tabs, *tabs, sinks)


def _attn_bwd(proj, tabs, sinks, d_att):
    s = proj.shape[0]

    def body(q_ref, kc_ref, kp_ref, vc_ref, vp_ref, c0, l0, h0, c1, l1, h1, sk_ref, do_ref, dq_ref, dk_ref, dv_ref, ds_ref):
        i = pl.program_id(0)

        @pl.when(i == 0)
        def _():
            dk_ref[...] = jnp.zeros_like(dk_ref)
            dv_ref[...] = jnp.zeros_like(dv_ref)
            ds_ref[...] = jnp.zeros_like(ds_ref)

        mask = _attn_mask(i > 0)
        q = _rope_wide(q_ref[...], c0[...], l0[...], h0[...]).astype(BF16)
        kk = jnp.concatenate([_rope(kp_ref[...], c1[...], l1[...], h1[...]), _rope(kc_ref[...], c0[...], l0[...], h0[...])], axis=0).astype(BF16)
        vv = jnp.concatenate([vp_ref[...], vc_ref[...]], axis=0).astype(BF16)
        d_o = do_ref[...].astype(BF16)
        lane = lax.broadcasted_iota(jnp.int32, (1, LANE), 1)
        dqs, dks, dvs = [], [], []
        d_sink = jnp.zeros((1, LANE), F32)
        for g in range(KV_HEADS):
            kg, vg = kk[:, g * HEAD_DIM:(g + 1) * HEAD_DIM], vv[:, g * HEAD_DIM:(g + 1) * HEAD_DIM]
            dkg = jnp.zeros((2 * BLK, HEAD_DIM), F32)
            dvg = jnp.zeros((2 * BLK, HEAD_DIM), F32)
            for j in range(GROUP):
                h = g * GROUP + j
                qh, doh = q[:, h * HEAD_DIM:(h + 1) * HEAD_DIM], d_o[:, h * HEAD_DIM:(h + 1) * HEAD_DIM]
                prob, p_sink = _attn_probs(qh, kg, mask, sk_ref[:, h:h + 1])
                d_p = _dot(doh, vg, NT)
                dd = jnp.sum(prob * d_p, axis=-1, keepdims=True)
                d_s = (prob * (d_p - dd)).astype(BF16)
                d_sink = d_sink + jnp.where(lane == h, -jnp.sum(p_sink * dd, axis=0, keepdims=True), 0.0)
                dqs.append(_dot(d_s, kg, NN) * (HEAD_DIM ** -0.5))
                dkg = dkg + _dot(d_s, qh, TN) * (HEAD_DIM ** -0.5)
                dvg = dvg + _dot(prob.astype(BF16), doh, TN)
            dks.append(dkg)
            dvs.append(dvg)
        dq_ref[...] = _rope_wide(jnp.concatenate(dqs, axis=1), c0[...], -l0[...], -h0[...]).astype(dq_ref.dtype)
        d_k = jnp.concatenate(dks, axis=1)
        d_v = jnp.concatenate(dvs, axis=1)
        cur = pl.ds(pl.multiple_of(i * BLK, BLK), BLK)
        prv = pl.ds(pl.multiple_of(jnp.maximum(i - 1, 0) * BLK, BLK), BLK)
        dk_ref[prv, :] += _rope(d_k[:BLK], c1[...], -l1[...], -h1[...])
        dk_ref[cur, :] += _rope(d_k[BLK:], c0[...], -l0[...], -h0[...])
        dv_ref[prv, :] += d_v[:BLK]
        dv_ref[cur, :] += d_v[BLK:]
        ds_ref[...] += d_sink

    full = pl.BlockSpec((s, LANE), lambda i: (0, 0))
    return pl.pallas_call(body, grid=(s // BLK,), in_specs=_attn_specs() + [pl.BlockSpec((BLK, ATT_W), lambda i: (i, 0))],
                          out_specs=[pl.BlockSpec((BLK, ATT_W), lambda i: (i, 0)), full, full, pl.BlockSpec((1, LANE), lambda i: (0, 0))],
                          out_shape=[_sds((s, ATT_W), BF16), _sds((s, LANE), F32), _sds((s, LANE), F32), _sds((1, LANE), F32)],
                          compiler_params=_params(1), name="attn_bwd")(proj, proj, proj, proj, proj, *tabs, *tabs, sinks, d_att)


def _tri_matmul(tri, t):
    hi = t.astype(BF16)
    r1 = t - hi.astype(F32)
    mid = r1.astype(BF16)
    lo = (r1 - mid.astype(F32)).astype(BF16)
    return _dot(tri, hi, NN) + _dot(tri, mid, NN) + _dot(tri, lo, NN)


def _lower_bound(hl):
    a, b = hl[0:1, :], hl[1:2, :]
    mx = jnp.maximum(a, b)
    ea, eb = jnp.exp(a - mx), jnp.exp(b - mx)
    return ea / (ea + eb)


def _hg_gates(q_raw, f_raw, lb, tri_lower):
    sg = _sig(f_raw)
    f = lb + (1.0 - lb) * sg
    sq = _sig(q_raw)
    b = _tri_matmul(tri_lower, jnp.log(f))
    return sg, f, 1.0 - f, sq, q_raw * sq, b


def _hg_specs(n_map):
    blk = lambda off: pl.BlockSpec((HG_TB, LANE), lambda h, n: (n_map(n), off // LANE + h))
    return [blk(Q_H), blk(F_H), blk(I_H), pl.BlockSpec((2, LANE), lambda h, n: (0, h))]


def _hgrn_fwd(proj, hl):
    s = proj.shape[0]
    n_chunk = HG_TB // CHUNK

    def body(q_ref, f_ref, i_ref, hl_ref, o_ref, st_out_ref, st_ref, b_s, k_s):
        @pl.when(pl.program_id(1) == 0)
        def _():
            st_ref[...] = jnp.zeros_like(st_ref)

        lb = _lower_bound(hl_ref[...])
        r_i = lax.broadcasted_iota(jnp.int32, (CHUNK, CHUNK), 0)
        c_i = lax.broadcasted_iota(jnp.int32, (CHUNK, CHUNK), 1)
        tri_lower = (r_i >= c_i).astype(BF16)
        row = lax.broadcasted_iota(jnp.int32, (CHUNK, 1), 0)
        lane = lax.broadcasted_iota(jnp.int32, (CHUNK, LANE), 1)

        def chunk(c, carry):
            rows = pl.ds(pl.multiple_of(c * CHUNK, CHUNK), CHUNK)
            v = i_ref[rows, :]
            _, _, kk, _, qs, b = _hg_gates(q_ref[rows, :], f_ref[rows, :], lb, tri_lower)
            b_s[...] = b
            k_s[...] = kk

            def column(sx, a):
                e = jnp.exp(jnp.minimum(b - b_s[pl.ds(sx, 1), :], 0.0))
                col = jnp.sum(qs * e * k_s[pl.ds(sx, 1), :], axis=-1, keepdims=True)
                return jnp.where(lane == sx, jnp.where(row >= sx, col, 0.0), a)

            a = lax.fori_loop(0, CHUNK, column, jnp.zeros((CHUNK, LANE), F32))
            st = st_ref[...]
            st_out_ref[c] = st.astype(BF16)
            o_ref[rows, :] = _dot((qs * jnp.exp(b)).astype(BF16), st.astype(BF16), NT) + _dot(a[:, :CHUNK].astype(BF16), v.astype(BF16), NN)
            b_last = b[CHUNK - 1:CHUNK, :]
            st_ref[...] = st * jnp.exp(b_last) + _dot(v.astype(BF16), (kk * jnp.exp(b_last - b)).astype(BF16), TN)
            return carry

        lax.fori_loop(0, n_chunk, chunk, 0)

    return pl.pallas_call(
        body, grid=(HG_HEADS, s // HG_TB), in_specs=_hg_specs(lambda n: n),
        out_specs=[pl.BlockSpec((HG_TB, LANE), lambda h, n: (n, h)), pl.BlockSpec((None, n_chunk, HG_K, HG_K), lambda h, n: (h, n, 0, 0))],
        out_shape=[_sds((s, HG_W), F32), _sds((HG_HEADS, s // CHUNK, HG_K, HG_K), BF16)],
        scratch_shapes=[pltpu.VMEM((HG_K, HG_K), F32), pltpu.VMEM((CHUNK, LANE), F32), pltpu.VMEM((CHUNK, LANE), F32)],
        compiler_params=_params(2), name="hgrn_fwd")(proj, proj, proj, hl)


def _hgrn_bwd(proj, hl, states, d_o):
    s = proj.shape[0]
    n_chunk = HG_TB // CHUNK
    n_blk = s // HG_TB
    rev = lambda n: n_blk - 1 - n

    def body(q_ref, f_ref, i_ref, hl_ref, st_in_ref, do_ref, dq_ref, df_ref, di_ref, dhl_ref, dst_ref, dlb_ref, b_s, k_s, dk_s):
        n = pl.program_id(1)

        @pl.when(n == 0)
        def _():
            dst_ref[...] = jnp.zeros_like(dst_ref)
            dlb_ref[...] = jnp.zeros_like(dlb_ref)

        lb = _lower_bound(hl_ref[...])
        r_i = lax.broadcasted_iota(jnp.int32, (CHUNK, CHUNK), 0)
        c_i = lax.broadcasted_iota(jnp.int32, (CHUNK, CHUNK), 1)
        tri_lower = (r_i >= c_i).astype(BF16)
        tri_upper = (r_i <= c_i).astype(BF16)
        row = lax.broadcasted_iota(jnp.int32, (CHUNK, 1), 0)
        lane = lax.broadcasted_iota(jnp.int32, (CHUNK, LANE), 1)

        def chunk(cc, carry):
            c = n_chunk - 1 - cc
            rows = pl.ds(pl.multiple_of(c * CHUNK, CHUNK), CHUNK)
            q_raw, v = q_ref[rows, :], i_ref[rows, :]
            sg, f, kk, sq, qs, b = _hg_gates(q_raw, f_ref[rows, :], lb, tri_lower)
            b_s[...] = b
            k_s[...] = kk
            e_b = jnp.exp(b)
            qe = qs * e_b
            b_last = b[CHUNK - 1:CHUNK, :]
            e_last = jnp.exp(b_last)
            e_kd = jnp.exp(b_last - b)
            kd = kk * e_kd
            st0 = st_in_ref[c]
            d_ob = do_ref[rows, :].astype(BF16)
            dst = dst_ref[...]
            dst_b = dst.astype(BF16)
            vb = v.astype(BF16)
            d_qe = _dot(d_ob, st0, NN)
            d_a = jnp.where(r_i >= c_i, _dot(d_ob, vb, NT), 0.0)
            d_a = jnp.concatenate([d_a, jnp.zeros((CHUNK, LANE - CHUNK), F32)], axis=1)

            def column(sx, state):
                a, dqs = state
                ks = k_s[pl.ds(sx, 1), :]
                e = jnp.where(row >= sx, jnp.exp(jnp.minimum(b - b_s[pl.ds(sx, 1), :], 0.0)), 0.0)
                pe = qs * e
                a = jnp.where(lane == sx, jnp.sum(pe * ks, axis=-1, keepdims=True), a)
                g_col = jnp.sum(jnp.where(lane == sx, d_a, 0.0), axis=-1, keepdims=True)
                dqs = dqs + g_col * (e * ks)
                dk_s[pl.ds(sx, 1), :] = jnp.sum(g_col * pe, axis=0, keepdims=True)
                return a, dqs

            zero = jnp.zeros((CHUNK, LANE), F32)
            a, dqs = lax.fori_loop(0, CHUNK, column, (zero, zero))
            d_v = _dot(a[:, :CHUNK].astype(BF16), d_ob, TN) + _dot(kd.astype(BF16), dst_b, NT)
            d_kd = _dot(vb, dst_b, NN)
            dqs = dqs + d_qe * e_b
            dkk = dk_s[...] + d_kd * e_kd
            d_b = qs * dqs - kk * dkk
            d_b_last = jnp.sum(d_kd * kd, axis=0, keepdims=True) + jnp.sum(dst * st0.astype(F32), axis=0, keepdims=True) * e_last
            d_b = d_b + jnp.where(row == CHUNK - 1, d_b_last, 0.0)
            d_g = _tri_matmul(tri_upper, d_b)
            dst_ref[...] = _dot(d_ob, qe.astype(BF16), TN) + dst * e_last
            d_f = d_g / f - dkk
            dlb_ref[...] += jnp.sum(d_f * (1.0 - sg), axis=0, keepdims=True)
            dq_ref[rows, :] = (dqs * (sq * (1.0 + q_raw * (1.0 - sq)))).astype(dq_ref.dtype)
            df_ref[rows, :] = (d_f * (1.0 - lb) * (sg * (1.0 - sg))).astype(df_ref.dtype)
            di_ref[rows, :] = d_v.astype(di_ref.dtype)
            return carry

        lax.fori_loop(0, n_chunk, chunk, 0)

        @pl.when(n == n_blk - 1)
        def _():
            d_hl0 = dlb_ref[...] * (lb * (1.0 - lb))
            dhl_ref[...] = jnp.concatenate([d_hl0, -d_hl0], axis=0)

    out_blk = pl.BlockSpec((HG_TB, LANE), lambda h, n: (rev(n), h))
    return pl.pallas_call(
        body, grid=(HG_HEADS, n_blk),
        in_specs=_hg_specs(rev) + [pl.BlockSpec((None, n_chunk, HG_K, HG_K), lambda h, n: (h, rev(n), 0, 0)), out_blk],
        out_specs=[out_blk, out_blk, out_blk, pl.BlockSpec((2, LANE), lambda h, n: (0, h))],
        out_shape=[_sds((s, HG_W), BF16)] * 3 + [_sds((2, HG_W), F32)],
        scratch_shapes=[pltpu.VMEM((HG_K, HG_K), F32), pltpu.VMEM((1, LANE), F32)] + [pltpu.VMEM((CHUNK, LANE), F32)] * 3,
        compiler_params=_params(2), name="hgrn_bwd")(proj, proj, proj, hl, states, d_o)


def _mod_part(c_all, w_shard, b_shard):
    n = w_shard.shape[1]
    tn = 512

    def body(c_ref, w_ref, b_ref, o_ref):
        o_ref[...] = _dot(c_ref[...].astype(BF16), w_ref[...].astype(BF16), NN) + b_ref[...]

    return pl.pallas_call(body, grid=(n // tn,),
                          in_specs=[pl.BlockSpec((N_DEV, D), lambda j: (0, 0)), pl.BlockSpec((D, tn), lambda j: (0, j)), pl.BlockSpec((1, tn), lambda j: (0, j))],
                          out_specs=pl.BlockSpec((N_DEV, tn), lambda j: (0, j)), out_shape=_sds((N_DEV, n), F32),
                          compiler_params=_params(1, 32 << 20), name="mod_part")(c_all, w_shard, b_shard)


def _grad_w_ada(c_all_t, dmod_cols):
    n = dmod_cols.shape[1]
    tn = 512

    def body(c_ref, d_ref, o_ref):
        cv = c_ref[...].astype(BF16).astype(F32)
        dv = d_ref[...].astype(BF16).astype(F32)
        acc = cv[:, 0:1] * dv[0:1, :]
        for k in range(1, N_DEV):
            acc = acc + cv[:, k:k + 1] * dv[k:k + 1, :]
        o_ref[...] = acc

    return pl.pallas_call(body, grid=(n // tn,),
                          in_specs=[pl.BlockSpec((D, N_DEV), lambda j: (0, 0)), pl.BlockSpec((N_DEV, tn), lambda j: (0, j))],
                          out_specs=pl.BlockSpec((D, tn), lambda j: (0, j)), out_shape=_sds((D, n), F32),
                          compiler_params=_params(1, 32 << 20), name="grad_w_ada")(c_all_t, dmod_cols)


def _row_tile(r, c):
    if r * c * 4 <= (1 << 20) or r % 8:
        return r
    best = 8
    for t in range(8, r + 1, 8):
        if r % t == 0 and t * c * 4 <= (1 << 20):
            best = t
    return best


def _adamw(pieces, w, m, v, name, emit_grad=True):
    p, r, c = pieces.shape
    tr = _row_tile(r, c)
    c1 = 1.0 / (1.0 - ADAM_B1 ** ADAM_STEP)
    c2 = 1.0 / (1.0 - ADAM_B2 ** ADAM_STEP)

    def body(p_ref, w_ref, m_ref, v_ref, *outs):
        g = p_ref[0].astype(F32)
        for k in range(1, p):
            g = g + p_ref[k].astype(F32)
        m2 = ADAM_B1 * m_ref[...] + (1.0 - ADAM_B1) * g
        v2 = ADAM_B2 * v_ref[...] + (1.0 - ADAM_B2) * (g * g)
        delta = -ADAM_LR * ((m2 * c1) / (jnp.sqrt(v2 * c2) + ADAM_EPS) + ADAM_WD * w_ref[...])
        if emit_grad:
            outs[0][...] = g
        outs[-3][...] = delta
        outs[-2][...] = m2
        outs[-1][...] = v2

    blk = pl.BlockSpec((tr, c), lambda i: (i, 0))
    n_out = 4 if emit_grad else 3
    return pl.pallas_call(body, grid=(r // tr,), in_specs=[pl.BlockSpec((p, tr, c), lambda i: (0, i, 0)), blk, blk, blk],
                          out_specs=[blk] * n_out, out_shape=[_sds((r, c), F32)] * n_out,
                          compiler_params=_params(1, 48 << 20), name=name)(pieces, w, m, v)


def _my_coords():
    return lax.axis_index("x"), lax.axis_index("y"), lax.axis_index("c")


def _flip(coords, k):
    x, y, c = coords
    return (1 - x if k & 4 else x, 1 - y if k & 2 else y, 1 - c if k & 1 else c)


def _lin(coords):
    return 4 * coords[0] + 2 * coords[1] + coords[2]


def _exchange_small(x3, bcast, name):
    n = x3.shape[2]

    def body(x_ref, o_ref, send_sems, recv_sems):
        me = _my_coords()
        my_id = _lin(me)
        o_ref[pl.ds(my_id, 1)] = x_ref[pl.ds(0 if bcast else my_id, 1)]
        copies = []
        for k in range(1, N_DEV):
            peer = _flip(me, k)
            src = x_ref.at[0 if bcast else _lin(peer)]
            cp = pltpu.make_async_remote_copy(src_ref=src, dst_ref=o_ref.at[my_id], send_sem=send_sems.at[k], recv_sem=recv_sems.at[k],
                                              device_id=peer, device_id_type=MESH)
            cp.start()
            copies.append(cp)
        for k in range(1, N_DEV):
            peer = _flip(me, k)
            pltpu.make_async_remote_copy(src_ref=x_ref.at[0], dst_ref=o_ref.at[_lin(peer)], send_sem=send_sems.at[k], recv_sem=recv_sems.at[k],
                                         device_id=peer, device_id_type=MESH).wait_recv()
        for cp in copies:
            cp.wait_send()

    vm = pl.BlockSpec(memory_space=pltpu.VMEM)
    return pl.pallas_call(body, in_specs=[vm], out_specs=vm, out_shape=_sds((N_DEV, 1, n), F32),
                          scratch_shapes=[pltpu.SemaphoreType.DMA((N_DEV,)), pltpu.SemaphoreType.DMA((N_DEV,))], name=name)(x3)


def _allgather_weights(shards):
    nw = len(shards)

    def body(*refs):
        ins, outs = refs[:nw], refs[nw:2 * nw]
        send_sems, recv_sems, local_sems = refs[2 * nw:]
        me = _my_coords()
        x, y, c = me
        sibling = (x, y, 1 - c)
        chips = [(1 - x, y), (x, 1 - y), (1 - x, 1 - y)]

        def copy(w, k, block, to, src=None):
            dst = outs[w].at[_lin(block)]
            return pltpu.make_async_remote_copy(src_ref=dst if src is None else src, dst_ref=dst, send_sem=send_sems.at[w, k],
                                                recv_sem=recv_sems.at[w, k], device_id=to, device_id_type=MESH)

        mine = [pltpu.make_async_copy(ins[w], outs[w].at[_lin(me)], local_sems.at[w]) for w in range(nw)]
        for cp in mine:
            cp.start()
        sent = []
        for w in range(nw):
            sent.append(copy(w, 0, me, sibling, src=ins[w]))
            sent += [copy(w, 1 + j, me, (*chip, c), src=ins[w]) for j, chip in enumerate(chips)]
        for cp in sent:
            cp.start()
        for w in range(nw):
            for j, chip in enumerate(chips):
                copy(w, 1 + j, (*chip, c), me).wait_recv()
                passed = copy(w, 4 + j, (*chip, c), sibling)
                passed.start()
                sent.append(passed)
        for w in range(nw):
            copy(w, 0, sibling, me).wait_recv()
            for j, chip in enumerate(chips):
                copy(w, 4 + j, (*chip, 1 - c), me).wait_recv()
        for cp in sent:
            cp.wait_send()
        for cp in mine:
            cp.wait()

    hbm = pl.BlockSpec(memory_space=pl.ANY)
    return pl.pallas_call(body, in_specs=[hbm] * nw, out_specs=[hbm] * nw,
                          out_shape=[_sds((N_DEV,) + s.shape, s.dtype) for s in shards],
                          scratch_shapes=[pltpu.SemaphoreType.DMA((nw, 7)), pltpu.SemaphoreType.DMA((nw, 7)), pltpu.SemaphoreType.DMA((nw,))],
                          name="allgather_weights")(*shards)


def _scatter_grads(grads):
    nw = len(grads)

    def body(*refs):
        ins, outs = refs[:nw], refs[nw:2 * nw]
        send_sems, recv_sems, local_sems = refs[2 * nw:]
        me = _my_coords()
        my_id = _lin(me)
        mine = [pltpu.make_async_copy(ins[w].at[my_id], outs[w].at[my_id], local_sems.at[w]) for w in range(nw)]
        for cp in mine:
            cp.start()
        sent = []
        for w in range(nw):
            for k in range(1, N_DEV):
                peer = _flip(me, k)
                cp = pltpu.make_async_remote_copy(src_ref=ins[w].at[_lin(peer)], dst_ref=outs[w].at[my_id], send_sem=send_sems.at[w, k],
                                                  recv_sem=recv_sems.at[w, k], device_id=peer, device_id_type=MESH)
                cp.start()
                sent.append(cp)
        for w in range(nw):
            for k in range(1, N_DEV):
                peer = _flip(me, k)
                pltpu.make_async_remote_copy(src_ref=ins[w].at[my_id], dst_ref=outs[w].at[_lin(peer)], send_sem=send_sems.at[w, k],
                                             recv_sem=recv_sems.at[w, k], device_id=peer, device_id_type=MESH).wait_recv()
        for cp in sent:
            cp.wait_send()
        for cp in mine:
            cp.wait()

    hbm = pl.BlockSpec(memory_space=pl.ANY)
    return pl.pallas_call(body, in_specs=[hbm] * nw, out_specs=[hbm] * nw, out_shape=[_sds(g.shape, g.dtype) for g in grads],
                          scratch_shapes=[pltpu.SemaphoreType.DMA((nw, N_DEV)), pltpu.SemaphoreType.DMA((nw, N_DEV)), pltpu.SemaphoreType.DMA((nw,))],
                          name="scatter_grads")(*grads)


def _rope_tables(positions):
    half = ROT // 2
    inv_freq = ROPE_THETA ** (-jnp.arange(0, ROT, 2, dtype=F32) / ROT)
    ang = positions.astype(F32).reshape(-1, 1) * inv_freq
    cos, sin = jnp.cos(ang), jnp.sin(ang)
    s = ang.shape[0]
    pad = jnp.zeros((s, HEAD_DIM - ROT), F32)
    zero = jnp.zeros((s, half), F32)
    two = lambda t: jnp.concatenate([t, t], axis=1)
    return (two(jnp.concatenate([cos, cos, pad + 1.0], axis=1)), two(jnp.concatenate([-sin, zero, pad], axis=1)),
            two(jnp.concatenate([zero, sin, pad], axis=1)))


def _local_step(x, tgt, tabs, mod, sinks_pad, hl, hg_norm, g_pre_mix, g_post_mix, g_pre_ffn, g_post_ffn,
                w_in, w_in_dm, w_attn_dm, w_hgrn_dm, w_out, w_ffn_in_dm, w_ffn_out):
    s = x.shape[0]
    h1 = _pre_fwd(x, g_pre_mix, mod, 1, 0, "pre_mix_fwd")
    proj = _mm_nn(h1, w_in, s, 256, D, F32, "proj_mm")
    att = _attn_fwd(proj, tabs, sinks_pad)
    o_raw, states = _hgrn_fwd(proj, hl)
    ohg = _hgout_fwd(o_raw, proj, hg_norm)
    y_a = _mm_nn_dm(att, w_attn_dm, s, F32, "attn_proj_mm")
    y_h = _mm_nn_dm(ohg, w_hgrn_dm, s, F32, "hgrn_proj_mm")
    merged = _merge_fwd(y_a, y_h, proj)
    y = _mm_nn(merged, w_out, s, 512, D, F32, "out_mm")
    x1 = _post_fwd(x, y, g_post_mix, mod, 2, "post_mix_fwd")
    h2 = _pre_fwd(x1, g_pre_ffn, mod, 4, 3, "pre_ffn_fwd")
    gu = _mm_nn_dm(h2, w_ffn_in_dm, s // 2, F32, "ffn_in_mm")
    act = _swiglu_fwd(gu)
    y2 = _mm_nn(act, w_ffn_out, s, 512, FFN // 4, F32, "ffn_out_mm")
    err, loss = _post_fwd_loss(x1, y2, g_post_ffn, mod, 5, tgt, "post_ffn_loss")
    dy2, d_gate2, dg_post_ffn = _post_bwd(err, y2, g_post_ffn, mod, 5, "post_ffn_bwd")
    d_act = _mm_nt(dy2, w_ffn_out, s, 512, D, F32, "ffn_out_dx")
    gw_ffn_out = _mm_tn(act, dy2, 512, D, BF16, "ffn_out_dw")
    dgu = _swiglu_bwd(d_act, gu)
    dh2 = _mm_nt_dm(dgu, w_ffn_in_dm, s, 512, F32, "ffn_in_dx")
    gw_ffn_in = _mm_tn_dm(h2, dgu, 512, BF16, "ffn_in_dw")
    dx1, d_shift2, d_scale2, dg_pre_ffn = _pre_bwd(dh2, x1, err, g_pre_ffn, mod, 4, "pre_ffn_bwd")
    dy, d_gate1, dg_post_mix = _post_bwd(dx1, y, g_post_mix, mod, 2, "post_mix_bwd")
    d_merged = _mm_nt(dy, w_out, s, 512, D, F32, "out_dx")
    gw_out = _mm_tn(merged, dy, 512, D, BF16, "out_dw")
    dy_a, dy_h, d_gate_a, d_gate_h = _merge_bwd(d_merged, y_a, y_h, proj)
    d_att = _mm_nt_dm(dy_a, w_attn_dm, s, 512, F32, "attn_proj_dx")
    gw_attn = _mm_tn_dm(att, dy_a, 512, BF16, "attn_proj_dw")
    d_ohg = _mm_nt_dm(dy_h, w_hgrn_dm, s, 512, F32, "hgrn_proj_dx")
    gw_hgrn = _mm_tn_dm(ohg, dy_h, 512, BF16, "hgrn_proj_dw")
    d_o, d_gh, d_hg_norm = _hgout_bwd(d_ohg, o_raw, proj, hg_norm)
    d_qh, d_fh, d_ih, d_hl = _hgrn_bwd(proj, hl, states, d_o)
    d_qa, d_ka, d_va, d_sinks = _attn_bwd(proj, tabs, sinks_pad, d_att)
    d_proj = jnp.concatenate([d_qa, d_ka.astype(BF16), d_va.astype(BF16), d_qh, d_fh, d_ih, d_gh, d_gate_a, d_gate_h], axis=1)
    d_proj_dm = d_proj.reshape(s, N_DEV, IN_COLS // N_DEV).transpose(1, 0, 2)
    dh1 = _mm_nt_dm2(d_proj_dm, w_in_dm, s // 2, 512, F32, "proj_dx")
    gw_in = _mm_tn_dm(h1, d_proj_dm, 512, BF16, "proj_dw", b_dm=True)
    grad_x, d_shift1, d_scale1, dg_pre_mix = _pre_bwd(dh1, x, dx1, g_pre_mix, mod, 1, "pre_mix_bwd")
    d_mod = jnp.concatenate([d_shift1, d_scale1, d_gate1, d_shift2, d_scale2, d_gate2], axis=1)
    small = [d_mod, dg_pre_mix, dg_post_mix, dg_pre_ffn, dg_post_ffn, d_hl.reshape(1, 2 * HG_W), d_hg_norm, d_sinks]
    big = [gw_in, gw_attn, gw_hgrn, gw_out.reshape(N_DEV, D // N_DEV, D), gw_ffn_in, gw_ffn_out.reshape(N_DEV, FFN // N_DEV, D)]
    return loss, grad_x, small, big


def kernel(x, c, positions, w_ada, b_ada, g_pre_mix, g_post_mix, g_pre_ffn, g_post_ffn, w_in, attn_sinks, w_attn_proj, hg_lower_bounds, hg_norm, w_hgrn_proj, w_out, w_ffn_in, w_ffn_out, loss_target, m_w_ada, m_b_ada, m_g_pre_mix, m_g_post_mix, m_g_pre_ffn, m_g_post_ffn, m_w_in, m_attn_sinks, m_w_attn_proj, m_hg_lower_bounds, m_hg_norm, m_w_hgrn_proj, m_w_out, m_w_ffn_in, m_w_ffn_out, v_w_ada, v_b_ada, v_g_pre_mix, v_g_post_mix, v_g_pre_ffn, v_g_post_ffn, v_w_in, v_attn_sinks, v_w_attn_proj, v_hg_lower_bounds, v_hg_norm, v_w_hgrn_proj, v_w_out, v_w_ffn_in, v_w_ffn_out):
    my_id = _lin(_my_coords())
    s = x.shape[1]
    n_ada = w_ada.shape[2]

    big_w = [w_in[0], w_attn_proj[0], w_hgrn_proj[0], w_out[0], w_ffn_in[0], w_ffn_out[0]]
    w_in_dm, w_attn_dm, w_hgrn_dm, w_out_g, w_ffn_in_dm, w_ffn_out_g = _allgather_weights([w.astype(BF16) for w in big_w])
    w_in_full = w_in_dm.transpose(1, 0, 2).reshape(D, IN_COLS)

    c_all = _exchange_small(c.reshape(1, 1, D), True, "gather_c").reshape(N_DEV, D)
    b_cols = lax.dynamic_slice(b_ada, (0, my_id * n_ada), (1, n_ada))
    mod_part = _mod_part(c_all, w_ada[0], b_cols)
    mod = _exchange_small(mod_part.reshape(N_DEV, 1, n_ada), False, "scatter_mod").reshape(1, N_MOD * D)

    sinks_pad = jnp.pad(attn_sinks, ((0, 0), (0, LANE - ATT_HEADS)))
    loss, grad_x, small, big = _local_step(
        x[0], loss_target[0], _rope_tables(positions), mod, sinks_pad, hg_lower_bounds, hg_norm, g_pre_mix, g_post_mix, g_pre_ffn, g_post_ffn,
        w_in_full, w_in_dm, w_attn_dm, w_hgrn_dm, w_out_g.reshape(D, D), w_ffn_in_dm, w_ffn_out_g.reshape(FFN, D))
    loss = lax.psum(loss[0, 0], ("x", "y", "c"))

    sizes = [t.shape[1] for t in small]
    parts = _exchange_small(jnp.concatenate(small, axis=1).reshape(1, 1, sum(sizes)), True, "gather_small_grads")
    offs = [sum(sizes[:k]) for k in range(len(sizes))]
    piece = lambda k, n=None: parts[:, :, offs[k]:offs[k] + (sizes[k] if n is None else n)]
    small_w = [(piece(0), b_ada, m_b_ada, v_b_ada), (piece(1), g_pre_mix, m_g_pre_mix, v_g_pre_mix),
               (piece(2), g_post_mix, m_g_post_mix, v_g_post_mix), (piece(3), g_pre_ffn, m_g_pre_ffn, v_g_pre_ffn),
               (piece(4), g_post_ffn, m_g_post_ffn, v_g_post_ffn),
               (piece(5).reshape(N_DEV, 2, HG_W), hg_lower_bounds, m_hg_lower_bounds, v_hg_lower_bounds),
               (piece(6), hg_norm, m_hg_norm, v_hg_norm), (piece(7, ATT_HEADS), attn_sinks, m_attn_sinks, v_attn_sinks)]
    names = ["b_ada", "g_pre_mix", "g_post_mix", "g_pre_ffn", "g_post_ffn", "hg_lower_bounds", "hg_norm", "attn_sinks"]
    res = {n: _adamw(p, w, m, v, "adamw_" + n) for n, (p, w, m, v) in zip(names, small_w)}

    dmod_cols = lax.dynamic_slice(parts.reshape(N_DEV, -1), (0, my_id * n_ada), (N_DEV, n_ada))
    g_w_ada = _grad_w_ada(c_all.T, dmod_cols)
    res["w_ada"] = [g_w_ada] + list(_adamw(g_w_ada[None], w_ada[0], m_w_ada[0], v_w_ada[0], "adamw_w_ada", emit_grad=False))

    recv = _scatter_grads(big)
    big_names = ["w_in", "w_attn_proj", "w_hgrn_proj", "w_out", "w_ffn_in", "w_ffn_out"]
    big_m = [m_w_in, m_w_attn_proj, m_w_hgrn_proj, m_w_out, m_w_ffn_in, m_w_ffn_out]
    big_v = [v_w_in, v_w_attn_proj, v_w_hgrn_proj, v_w_out, v_w_ffn_in, v_w_ffn_out]
    for n, p, w, m, v in zip(big_names, recv, big_w, big_m, big_v):
        res[n] = _adamw(p, w, m[0], v[0], "adamw_" + n)

    order = ["w_ada", "b_ada", "g_pre_mix", "g_post_mix", "g_pre_ffn", "g_post_ffn", "w_in", "attn_sinks", "w_attn_proj",
             "hg_lower_bounds", "hg_norm", "w_hgrn_proj", "w_out", "w_ffn_in", "w_ffn_out"]
    lead = {"w_ada", "w_in", "w_attn_proj", "w_hgrn_proj", "w_out", "w_ffn_in", "w_ffn_out"}
    outs = [loss, grad_x[None]]
    for k in range(4):
        outs += [res[n][k][None] if n in lead else res[n][k] for n in order]
    return tuple(outs)
```

```python
import functools

import jax
import jax.numpy as jnp
from jax import lax
from jax.experimental import pallas as pl
from jax.experimental.pallas import tpu as pltpu

F32 = jnp.float32
BF16 = jnp.bfloat16

N_DEV = 8
D = 2048
ATT_HEADS = 16
KV_HEADS = 2
HEAD_DIM = 64
GROUP = ATT_HEADS // KV_HEADS
ATT_W = ATT_HEADS * HEAD_DIM
BLK = 128
ROT = HEAD_DIM // 4
ROPE_THETA = 500000.0
HG_HEADS = 8
HG_K = 128
HG_W = HG_HEADS * HG_K
CHUNK = 64
SUB = 16
FFN = 5632
N_MOD = 6
EPS = 1e-6
LANE = 128
Q_A, K_A, V_A, Q_H, F_H, I_H, G_H, GT_A, GT_H, IN_COLS = 0, 1024, 1152, 1280, 2304, 3328, 4352, 5376, 7424, 9472

ADAM_LR, ADAM_B1, ADAM_B2, ADAM_EPS, ADAM_WD, ADAM_STEP = 0.001, 0.9, 0.999, 1e-08, 0.01, 10

TR = 256
HG_TB = 512
VMEM_BIG = 56 << 20
MESH = pl.DeviceIdType.MESH


def _sds(shape, dtype):
    return jax.ShapeDtypeStruct(shape, dtype)


def _params(n_axes, vmem=None):
    return pltpu.CompilerParams(dimension_semantics=("arbitrary",) * n_axes, vmem_limit_bytes=vmem)


def _sig(t):
    return 1.0 / (1.0 + jnp.exp(-t))


def _dot(a, b, dims):
    return lax.dot_general(a, b, (dims, ((), ())), preferred_element_type=F32)


NN = ((1,), (0,))
NT = ((1,), (1,))
TN = ((0,), (0,))


def _matmul(a, b, a_spec, b_spec, o_spec, out_shape, grid, dims, acc_shape, name):
    nk = grid[2]

    def body(a_ref, b_ref, o_ref, *scratch):
        part = _dot(a_ref[...], b_ref[...], dims)
        if nk == 1:
            o_ref[...] = part.astype(o_ref.dtype)
        else:
            acc = scratch[0]
            k = pl.program_id(2)

            @pl.when(k == 0)
            def _():
                acc[...] = part

            @pl.when(k > 0)
            def _():
                acc[...] += part

            @pl.when(k == nk - 1)
            def _():
                o_ref[...] = acc[...].astype(o_ref.dtype)

    return pl.pallas_call(
        body, grid=grid, in_specs=[a_spec, b_spec], out_specs=o_spec, out_shape=out_shape,
        scratch_shapes=[pltpu.VMEM(acc_shape, F32)] if nk > 1 else [],
        compiler_params=_params(3, VMEM_BIG), name=name)(a, b)


def _mm_nn(a, b, tm, tn, tk, out_dtype, name):
    m, k = a.shape
    n = b.shape[1]
    return _matmul(a, b, pl.BlockSpec((tm, tk), lambda j, i, kk: (i, kk)), pl.BlockSpec((tk, tn), lambda j, i, kk: (kk, j)),
                   pl.BlockSpec((tm, tn), lambda j, i, kk: (i, j)), _sds((m, n), out_dtype),
                   (n // tn, m // tm, k // tk), NN, (tm, tn), name)


def _mm_nn_dm(a, b, tm, out_dtype, name):
    m, k = a.shape
    n = b.shape[2]
    return _matmul(a, b, pl.BlockSpec((tm, k), lambda j, i, kk: (i, 0)), pl.BlockSpec((None, k, n), lambda j, i, kk: (j, 0, 0)),
                   pl.BlockSpec((tm, n), lambda j, i, kk: (i, j)), _sds((m, N_DEV * n), out_dtype),
                   (N_DEV, m // tm, 1), NN, (tm, n), name)


def _mm_nt(a, b, tm, tn, tk, out_dtype, name):
    m, k = a.shape
    n = b.shape[0]
    return _matmul(a, b, pl.BlockSpec((tm, tk), lambda j, i, kk: (i, kk)), pl.BlockSpec((tn, tk), lambda j, i, kk: (j, kk)),
                   pl.BlockSpec((tm, tn), lambda j, i, kk: (i, j)), _sds((m, n), out_dtype),
                   (n // tn, m // tm, k // tk), NT, (tm, tn), name)


def _mm_nt_dm(a, b, tm, tn, out_dtype, name):
    m = a.shape[0]
    n_out, n = b.shape[1], b.shape[2]
    return _matmul(a, b, pl.BlockSpec((tm, n), lambda j, i, kk: (i, kk)), pl.BlockSpec((None, tn, n), lambda j, i, kk: (kk, j, 0)),
                   pl.BlockSpec((tm, tn), lambda j, i, kk: (i, j)), _sds((m, n_out), out_dtype),
                   (n_out // tn, m // tm, N_DEV), NT, (tm, tn), name)


def _mm_nt_dm2(a, b, tm, tn, out_dtype, name):
    m = a.shape[1]
    n_out, n = b.shape[1], b.shape[2]
    return _matmul(a, b, pl.BlockSpec((None, tm, n), lambda j, i, kk: (kk, i, 0)), pl.BlockSpec((None, tn, n), lambda j, i, kk: (kk, j, 0)),
                   pl.BlockSpec((tm, tn), lambda j, i, kk: (i, j)), _sds((m, n_out), out_dtype),
                   (n_out // tn, m // tm, N_DEV), NT, (tm, tn), name)


def _mm_tn(a, b, tm, tn, out_dtype, name):
    s, m = a.shape
    n = b.shape[1]
    return _matmul(a, b, pl.BlockSpec((s, tm), lambda j, i, kk: (0, i)), pl.BlockSpec((s, tn), lambda j, i, kk: (0, j)),
                   pl.BlockSpec((tm, tn), lambda j, i, kk: (i, j)), _sds((m, n), out_dtype),
                   (n // tn, m // tm, 1), TN, (tm, tn), name)


def _mm_tn_dm(a, b, tm, out_dtype, name, b_dm=False):
    s, m = a.shape
    if b_dm:
        n = b.shape[2]
        b_spec = pl.BlockSpec((None, s, n), lambda j, i, kk: (j, 0, 0))
    else:
        n = b.shape[1] // N_DEV
        b_spec = pl.BlockSpec((s, n), lambda j, i, kk: (0, j))
    return _matmul(a, b, pl.BlockSpec((s, tm), lambda j, i, kk: (0, i)), b_spec,
                   pl.BlockSpec((None, tm, n), lambda j, i, kk: (j, i, 0)), _sds((N_DEV, m, n), out_dtype),
                   (N_DEV, m // tm, 1), TN, (tm, n), name)


def _row_spec():
    return pl.BlockSpec((TR, D), lambda i: (i, 0))


def _vec_spec(k=0):
    return pl.BlockSpec((1, D), lambda i: (0, k))


def _acc_rows(ref, first, val):
    @pl.when(first)
    def _():
        ref[...] = val

    @pl.when(jnp.logical_not(first))
    def _():
        ref[...] += val


def _pre_fwd(x, g, mod, k_scale, k_shift, name):
    s = x.shape[0]

    def body(x_ref, g_ref, sc_ref, sh_ref, h_ref):
        xv = x_ref[...]
        r = lax.rsqrt(jnp.mean(xv * xv, axis=-1, keepdims=True) + EPS)
        n = xv * r * g_ref[...]
        h_ref[...] = (n * (1.0 + sc_ref[...]) + sh_ref[...]).astype(h_ref.dtype)

    return pl.pallas_call(body, grid=(s // TR,), in_specs=[_row_spec(), _vec_spec(), _vec_spec(k_scale), _vec_spec(k_shift)],
                          out_specs=_row_spec(), out_shape=_sds((s, D), BF16), compiler_params=_params(1), name=name)(x, g, mod, mod)


def _post_fwd(x, y, g, mod, k_gate, name):
    s = x.shape[0]

    def body(x_ref, y_ref, g_ref, gt_ref, o_ref):
        yv = y_ref[...]
        r = lax.rsqrt(jnp.mean(yv * yv, axis=-1, keepdims=True) + EPS)
        o_ref[...] = x_ref[...] + gt_ref[...] * (yv * r * g_ref[...])

    return pl.pallas_call(body, grid=(s // TR,), in_specs=[_row_spec(), _row_spec(), _vec_spec(), _vec_spec(k_gate)],
                          out_specs=_row_spec(), out_shape=_sds((s, D), F32), compiler_params=_params(1), name=name)(x, y, g, mod)


def _post_fwd_loss(x, y, g, mod, k_gate, tgt, name):
    s = x.shape[0]

    def body(x_ref, y_ref, g_ref, gt_ref, t_ref, e_ref, loss_ref):
        i = pl.program_id(0)
        yv = y_ref[...]
        r = lax.rsqrt(jnp.mean(yv * yv, axis=-1, keepdims=True) + EPS)
        err = x_ref[...] + gt_ref[...] * (yv * r * g_ref[...]) - t_ref[...]
        e_ref[...] = err * (1.0 / D)
        part = 0.5 * jnp.sum(jnp.mean(err * err, axis=-1, keepdims=True), axis=0, keepdims=True)
        _acc_rows(loss_ref, i == 0, part)

    return pl.pallas_call(body, grid=(s // TR,),
                          in_specs=[_row_spec(), _row_spec(), _vec_spec(), _vec_spec(k_gate), _row_spec()],
                          out_specs=[_row_spec(), pl.BlockSpec((1, 1), lambda i: (0, 0))],
                          out_shape=[_sds((s, D), F32), _sds((1, 1), F32)], compiler_params=_params(1), name=name)(x, y, g, mod, tgt)


def _pre_bwd(dh, x, res, g, mod, k_scale, name):
    s = x.shape[0]

    def body(dh_ref, x_ref, res_ref, g_ref, sc_ref, dx_ref, dsh_ref, dsc_ref, dg_ref):
        first = pl.program_id(0) == 0
        xv, dh_v, gv = x_ref[...], dh_ref[...], g_ref[...]
        r = lax.rsqrt(jnp.mean(xv * xv, axis=-1, keepdims=True) + EPS)
        xh = xv * r
        dn = dh_v * (1.0 + sc_ref[...])
        dgn = dn * gv
        dx_ref[...] = res_ref[...] + r * (dgn - xh * jnp.mean(dgn * xh, axis=-1, keepdims=True))
        _acc_rows(dsh_ref, first, jnp.sum(dh_v, axis=0, keepdims=True))
        _acc_rows(dsc_ref, first, jnp.sum(dh_v * (xh * gv), axis=0, keepdims=True))
        _acc_rows(dg_ref, first, jnp.sum(dn * xh, axis=0, keepdims=True))

    return pl.pallas_call(body, grid=(s // TR,),
                          in_specs=[_row_spec(), _row_spec(), _row_spec(), _vec_spec(), _vec_spec(k_scale)],
                          out_specs=[_row_spec(), _vec_spec(), _vec_spec(), _vec_spec()],
                          out_shape=[_sds((s, D), F32)] + [_sds((1, D), F32)] * 3,
                          compiler_params=_params(1), name=name)(dh, x, res, g, mod)


def _post_bwd(dx, y, g, mod, k_gate, name):
    s = y.shape[0]

    def body(dx_ref, y_ref, g_ref, gt_ref, dy_ref, dgt_ref, dg_ref):
        first = pl.program_id(0) == 0
        yv, dxv, gv = y_ref[...], dx_ref[...], g_ref[...]
        r = lax.rsqrt(jnp.mean(yv * yv, axis=-1, keepdims=True) + EPS)
        yh = yv * r
        dn = dxv * gt_ref[...]
        dgn = dn * gv
        dy_ref[...] = (r * (dgn - yh * jnp.mean(dgn * yh, axis=-1, keepdims=True))).astype(dy_ref.dtype)
        _acc_rows(dgt_ref, first, jnp.sum(dxv * (yh * gv), axis=0, keepdims=True))
        _acc_rows(dg_ref, first, jnp.sum(dn * yh, axis=0, keepdims=True))

    return pl.pallas_call(body, grid=(s // TR,), in_specs=[_row_spec(), _row_spec(), _vec_spec(), _vec_spec(k_gate)],
                          out_specs=[_row_spec(), _vec_spec(), _vec_spec()],
                          out_shape=[_sds((s, D), BF16), _sds((1, D), F32), _sds((1, D), F32)],
                          compiler_params=_params(1), name=name)(dx, y, g, mod)


SW_TN = 512


def _swiglu_fwd(gu):
    s = gu.shape[0]
    nb = FFN // SW_TN

    def body(g_ref, u_ref, a_ref):
        gv = g_ref[...]
        a_ref[...] = (gv * _sig(gv) * u_ref[...]).astype(a_ref.dtype)

    return pl.pallas_call(body, grid=(s // TR, nb),
                          in_specs=[pl.BlockSpec((TR, SW_TN), lambda i, j: (i, j)), pl.BlockSpec((TR, SW_TN), lambda i, j: (i, j + nb))],
                          out_specs=pl.BlockSpec((TR, SW_TN), lambda i, j: (i, j)), out_shape=_sds((s, FFN), BF16),
                          compiler_params=_params(2), name="swiglu_fwd")(gu, gu)


def _swiglu_bwd(dact, gu):
    s = gu.shape[0]
    nb = FFN // SW_TN

    def body(da_ref, g_ref, u_ref, o_ref):
        half = pl.program_id(2)
        gv, da = g_ref[...], da_ref[...]
        sg = _sig(gv)
        d_gate = da * u_ref[...] * (sg * (1.0 + gv * (1.0 - sg)))
        d_up = da * (gv * sg)
        o_ref[...] = jnp.where(half == 0, d_gate, d_up).astype(o_ref.dtype)

    blk = lambda f: pl.BlockSpec((TR, SW_TN), f)
    return pl.pallas_call(body, grid=(s // TR, nb, 2),
                          in_specs=[blk(lambda i, j, h: (i, j)), blk(lambda i, j, h: (i, j)), blk(lambda i, j, h: (i, j + nb))],
                          out_specs=blk(lambda i, j, h: (i, j + nb * h)), out_shape=_sds((s, 2 * FFN), BF16),
                          compiler_params=_params(3), name="swiglu_bwd")(dact, gu, gu)


MG_TN = 256


def _merge_fwd(y_a, y_h, proj):
    s = y_a.shape[0]
    tn = MG_TN
    ba, bh = GT_A // tn, GT_H // tn

    def body(ya_ref, yh_ref, ga_ref, gh_ref, m_ref):
        m_ref[...] = (_sig(ga_ref[...]) * ya_ref[...] + _sig(gh_ref[...]) * yh_ref[...]).astype(m_ref.dtype)

    blk = lambda f: pl.BlockSpec((TR, tn), f)
    return pl.pallas_call(body, grid=(s // TR, D // tn),
                          in_specs=[blk(lambda i, j: (i, j)), blk(lambda i, j: (i, j)), blk(lambda i, j: (i, j + ba)), blk(lambda i, j: (i, j + bh))],
                          out_specs=blk(lambda i, j: (i, j)), out_shape=_sds((s, D), BF16),
                          compiler_params=_params(2), name="merge_fwd")(y_a, y_h, proj, proj)


def _merge_bwd(dm, y_a, y_h, proj):
    s = y_a.shape[0]
    tn = MG_TN
    ba, bh = GT_A // tn, GT_H // tn

    def body(dm_ref, ya_ref, yh_ref, ga_ref, gh_ref, dya_ref, dyh_ref, dga_ref, dgh_ref):
        dmv = dm_ref[...]
        sa, sh = _sig(ga_ref[...]), _sig(gh_ref[...])
        dya_ref[...] = (dmv * sa).astype(BF16)
        dyh_ref[...] = (dmv * sh).astype(BF16)
        dga_ref[...] = (dmv * ya_ref[...] * (sa * (1.0 - sa))).astype(BF16)
        dgh_ref[...] = (dmv * yh_ref[...] * (sh * (1.0 - sh))).astype(BF16)

    blk = lambda f: pl.BlockSpec((TR, tn), f)
    nat = blk(lambda i, j: (i, j))
    return pl.pallas_call(body, grid=(s // TR, D // tn),
                          in_specs=[nat, nat, nat, blk(lambda i, j: (i, j + ba)), blk(lambda i, j: (i, j + bh))],
                          out_specs=[nat] * 4, out_shape=[_sds((s, D), BF16)] * 4,
                          compiler_params=_params(2), name="merge_bwd")(dm, y_a, y_h, proj, proj)


def _hgout_fwd(o_raw, proj, hg_norm):
    s = o_raw.shape[0]
    bg = G_H // LANE

    def body(o_ref, g_ref, n_ref, out_ref):
        ov = o_ref[...]
        r = lax.rsqrt(jnp.mean(ov * ov, axis=-1, keepdims=True) + EPS)
        out_ref[...] = (ov * r * n_ref[...] * _sig(g_ref[...])).astype(out_ref.dtype)

    blk = lambda f: pl.BlockSpec((TR, LANE), f)
    return pl.pallas_call(body, grid=(s // TR, HG_HEADS),
                          in_specs=[blk(lambda i, h: (i, h)), blk(lambda i, h: (i, h + bg)), pl.BlockSpec((1, LANE), lambda i, h: (0, 0))],
                          out_specs=blk(lambda i, h: (i, h)), out_shape=_sds((s, HG_W), BF16),
                          compiler_params=_params(2), name="hgout_fwd")(o_raw, proj, hg_norm)


def _hgout_bwd(d_out, o_raw, proj, hg_norm):
    s = o_raw.shape[0]
    bg = G_H // LANE

    def body(d_ref, o_ref, g_ref, n_ref, do_ref, dg_ref, dn_ref):
        first = jnp.logical_and(pl.program_id(0) == 0, pl.program_id(1) == 0)
        ov, dv, nv = o_ref[...], d_ref[...], n_ref[...]
        sg = _sig(g_ref[...])
        r = lax.rsqrt(jnp.mean(ov * ov, axis=-1, keepdims=True) + EPS)
        oh = ov * r
        d_on = dv * sg
        dg_ref[...] = (dv * (oh * nv) * (sg * (1.0 - sg))).astype(dg_ref.dtype)
        t = d_on * nv
        do_ref[...] = r * (t - oh * jnp.mean(t * oh, axis=-1, keepdims=True))
        _acc_rows(dn_ref, first, jnp.sum(d_on * oh, axis=0, keepdims=True))

    blk = lambda f: pl.BlockSpec((TR, LANE), f)
    vec = pl.BlockSpec((1, LANE), lambda i, h: (0, 0))
    return pl.pallas_call(body, grid=(s // TR, HG_HEADS),
                          in_specs=[blk(lambda i, h: (i, h)), blk(lambda i, h: (i, h)), blk(lambda i, h: (i, h + bg)), vec],
                          out_specs=[blk(lambda i, h: (i, h)), blk(lambda i, h: (i, h)), vec],
                          out_shape=[_sds((s, HG_W), F32), _sds((s, HG_W), BF16), _sds((1, LANE), F32)],
                          compiler_params=_params(2), name="hgout_bwd")(d_out, o_raw, proj, hg_norm)


def _rope(t, cos, s_lo, s_hi):
    return t * cos + pltpu.roll(t, LANE - ROT // 2, 1) * s_lo + pltpu.roll(t, ROT // 2, 1) * s_hi


def _rope_wide(t, cos, s_lo, s_hi):
    return jnp.concatenate([_rope(t[:, k * LANE:(k + 1) * LANE], cos, s_lo, s_hi) for k in range(t.shape[1] // LANE)], axis=1)


def _attn_mask(has_prev):
    qi = lax.broadcasted_iota(jnp.int32, (BLK, 2 * BLK), 0)
    kj = lax.broadcasted_iota(jnp.int32, (BLK, 2 * BLK), 1)
    rel = BLK + qi - kj
    band = jnp.logical_and(rel >= 0, rel < BLK)
    return jnp.logical_and(band, jnp.logical_or(has_prev, kj >= BLK))


def _attn_specs():
    prev = lambda i: jnp.maximum(i - 1, 0)
    kb, vb = K_A // LANE, V_A // LANE
    blk = lambda f: pl.BlockSpec((BLK, LANE), f)
    tabs = [blk(lambda i: (i, 0))] * 3 + [blk(lambda i: (prev(i), 0))] * 3
    return [pl.BlockSpec((BLK, ATT_W), lambda i: (i, 0)), blk(lambda i: (i, kb)), blk(lambda i: (prev(i), kb)),
            blk(lambda i: (i, vb)), blk(lambda i: (prev(i), vb))] + tabs + [pl.BlockSpec((1, LANE), lambda i: (0, 0))]


def _attn_probs(qh, kg, mask, sk):
    logits = _dot(qh, kg, NT) * (HEAD_DIM ** -0.5)
    logits = jnp.where(mask, logits, -jnp.inf)
    m = jnp.maximum(jnp.max(logits, axis=-1, keepdims=True), sk)
    p = jnp.exp(logits - m)
    e_sink = jnp.exp(sk - m)
    inv = 1.0 / (jnp.sum(p, axis=-1, keepdims=True) + e_sink)
    return p * inv, e_sink * inv


def _attn_fwd(proj, tabs, sinks):
    s = proj.shape[0]

    def body(q_ref, kc_ref, kp_ref, vc_ref, vp_ref, c0, l0, h0, c1, l1, h1, sk_ref, o_ref):
        i = pl.program_id(0)
        mask = _attn_mask(i > 0)
        q = _rope_wide(q_ref[...], c0[...], l0[...], h0[...]).astype(BF16)
        kk = jnp.concatenate([_rope(kp_ref[...], c1[...], l1[...], h1[...]), _rope(kc_ref[...], c0[...], l0[...], h0[...])], axis=0).astype(BF16)
        vv = jnp.concatenate([vp_ref[...], vc_ref[...]], axis=0).astype(BF16)
        outs = []
        for h in range(ATT_HEADS):
            g = h // GROUP
            prob, _ = _attn_probs(q[:, h * HEAD_DIM:(h + 1) * HEAD_DIM], kk[:, g * HEAD_DIM:(g + 1) * HEAD_DIM], mask, sk_ref[:, h:h + 1])
            outs.append(_dot(prob.astype(BF16), vv[:, g * HEAD_DIM:(g + 1) * HEAD_DIM], NN))
        o_ref[...] = jnp.concatenate(outs, axis=1).astype(o_ref.dtype)

    return pl.pallas_call(body, grid=(s // BLK,), in_specs=_attn_specs(),
                          out_specs=pl.BlockSpec((BLK, ATT_W), lambda i: (i, 0)), out_shape=_sds((s, ATT_W), BF16),
                          compiler_params=_params(1), name="attn_fwd")(proj, proj, proj, proj, proj, *tabs, *tabs, sinks)


def _attn_bwd(proj, tabs, sinks, d_att):
    s = proj.shape[0]

    def body(q_ref, kc_ref, kp_ref, vc_ref, vp_ref, c0, l0, h0, c1, l1, h1, sk_ref, do_ref, dq_ref, dk_ref, dv_ref, ds_ref):
        i = pl.program_id(0)

        @pl.when(i == 0)
        def _():
            dk_ref[...] = jnp.zeros_like(dk_ref)
            dv_ref[...] = jnp.zeros_like(dv_ref)
            ds_ref[...] = jnp.zeros_like(ds_ref)

        mask = _attn_mask(i > 0)
        q = _rope_wide(q_ref[...], c0[...], l0[...], h0[...]).astype(BF16)
        kk = jnp.concatenate([_rope(kp_ref[...], c1[...], l1[...], h1[...]), _rope(kc_ref[...], c0[...], l0[...], h0[...])], axis=0).astype(BF16)
        vv = jnp.concatenate([vp_ref[...], vc_ref[...]], axis=0).astype(BF16)
        d_o = do_ref[...].astype(BF16)
        lane = lax.broadcasted_iota(jnp.int32, (1, LANE), 1)
        dqs, dks, dvs = [], [], []
        d_sink = jnp.zeros((1, LANE), F32)
        for g in range(KV_HEADS):
            kg, vg = kk[:, g * HEAD_DIM:(g + 1) * HEAD_DIM], vv[:, g * HEAD_DIM:(g + 1) * HEAD_DIM]
            dkg = jnp.zeros((2 * BLK, HEAD_DIM), F32)
            dvg = jnp.zeros((2 * BLK, HEAD_DIM), F32)
            for j in range(GROUP):
                h = g * GROUP + j
                qh, doh = q[:, h * HEAD_DIM:(h + 1) * HEAD_DIM], d_o[:, h * HEAD_DIM:(h + 1) * HEAD_DIM]
                prob, p_sink = _attn_probs(qh, kg, mask, sk_ref[:, h:h + 1])
                d_p = _dot(doh, vg, NT)
                dd = jnp.sum(prob * d_p, axis=-1, keepdims=True)
                d_s = (prob * (d_p - dd)).astype(BF16)
                d_sink = d_sink + jnp.where(lane == h, -jnp.sum(p_sink * dd, axis=0, keepdims=True), 0.0)
                dqs.append(_dot(d_s, kg, NN) * (HEAD_DIM ** -0.5))
                dkg = dkg + _dot(d_s, qh, TN) * (HEAD_DIM ** -0.5)
                dvg = dvg + _dot(prob.astype(BF16), doh, TN)
            dks.append(dkg)
            dvs.append(dvg)
        dq_ref[...] = _rope_wide(jnp.concatenate(dqs, axis=1), c0[...], -l0[...], -h0[...]).astype(dq_ref.dtype)
        d_k = jnp.concatenate(dks, axis=1)
        d_v = jnp.concatenate(dvs, axis=1)
        cur = pl.ds(pl.multiple_of(i * BLK, BLK), BLK)
        prv = pl.ds(pl.multiple_of(jnp.maximum(i - 1, 0) * BLK, BLK), BLK)
        dk_ref[prv, :] += _rope(d_k[:BLK], c1[...], -l1[...], -h1[...])
        dk_ref[cur, :] += _rope(d_k[BLK:], c0[...], -l0[...], -h0[...])
        dv_ref[prv, :] += d_v[:BLK]
        dv_ref[cur, :] += d_v[BLK:]
        ds_ref[...] += d_sink

    full = pl.BlockSpec((s, LANE), lambda i: (0, 0))
    return pl.pallas_call(body, grid=(s // BLK,), in_specs=_attn_specs() + [pl.BlockSpec((BLK, ATT_W), lambda i: (i, 0))],
                          out_specs=[pl.BlockSpec((BLK, ATT_W), lambda i: (i, 0)), full, full, pl.BlockSpec((1, LANE), lambda i: (0, 0))],
                          out_shape=[_sds((s, ATT_W), BF16), _sds((s, LANE), F32), _sds((s, LANE), F32), _sds((1, LANE), F32)],
                          compiler_params=_params(1), name="attn_bwd")(proj, proj, proj, proj, proj, *tabs, *tabs, sinks, d_att)


def _tri_matmul(tri, t):
    hi = t.astype(BF16)
    r1 = t - hi.astype(F32)
    mid = r1.astype(BF16)
    lo = (r1 - mid.astype(F32)).astype(BF16)
    return _dot(tri, hi, NN) + _dot(tri, mid, NN) + _dot(tri, lo, NN)


def _lower_bound(hl):
    a, b = hl[0:1, :], hl[1:2, :]
    mx = jnp.maximum(a, b)
    ea, eb = jnp.exp(a - mx), jnp.exp(b - mx)
    return ea / (ea + eb)


def _hg_gates(q_raw, f_raw, lb, tri_lower):
    sg = _sig(f_raw)
    f = lb + (1.0 - lb) * sg
    sq = _sig(q_raw)
    b = _tri_matmul(tri_lower, jnp.log(f))
    return sg, f, 1.0 - f, sq, q_raw * sq, b


def _hg_specs(n_map):
    blk = lambda off: pl.BlockSpec((HG_TB, LANE), lambda h, n: (n_map(n), off // LANE + h))
    return [blk(Q_H), blk(F_H), blk(I_H), pl.BlockSpec((2, LANE), lambda h, n: (0, h))]


def _hg_intra(qs, kk, b, d_a=None):
    lane = lax.broadcasted_iota(jnp.int32, (SUB, CHUNK), 1)
    row1 = lax.broadcasted_iota(jnp.int32, (SUB, 1), 0)
    rowk = lax.broadcasted_iota(jnp.int32, (SUB, LANE), 0)
    grad = d_a is not None
    a_blocks, dq_blocks, dk_blocks = [], [], []
    dk_left = None
    for j in range(CHUNK // SUB):
        lo = j * SUB
        q_j, k_j, b_j = qs[lo:lo + SUB], kk[lo:lo + SUB], b[lo:lo + SUB]
        a_j = jnp.zeros((SUB, CHUNK), F32)
        if grad:
            da_j = d_a[lo:lo + SUB]
            dq_j = jnp.zeros((SUB, LANE), F32)
            dk_j = jnp.zeros((SUB, LANE), F32)
        for sx in range(SUB):
            e = jnp.where(row1 >= sx, jnp.exp(jnp.minimum(b_j - b_j[sx:sx + 1], 0.0)), 0.0)
            pe = q_j * e
            a_j = jnp.where(lane == lo + sx, jnp.sum(pe * k_j[sx:sx + 1], axis=-1, keepdims=True), a_j)
            if grad:
                g_col = da_j[:, lo + sx:lo + sx + 1]
                dq_j = dq_j + g_col * (e * k_j[sx:sx + 1])
                dk_j = jnp.where(rowk == sx, jnp.sum(g_col * pe, axis=0, keepdims=True), dk_j)
        if j > 0:
            ref = b[lo - 1:lo]
            sc_q = jnp.exp(b_j - ref)
            sc_k = jnp.exp(jnp.minimum(ref - b, 0.0))
            qt = (q_j * sc_q).astype(BF16)
            kt = (kk * sc_k).astype(BF16)
            a_j = a_j + jnp.where(lane < lo, _dot(qt, kt, NT), 0.0)
            if grad:
                da_left = jnp.where(lane < lo, da_j, 0.0).astype(BF16)
                dq_j = dq_j + _dot(da_left, kt, NN) * sc_q
                t = _dot(da_left, qt, TN) * sc_k
                dk_left = t if dk_left is None else dk_left + t
        a_blocks.append(a_j)
        if grad:
            dq_blocks.append(dq_j)
            dk_blocks.append(dk_j)
    a = jnp.concatenate(a_blocks, axis=0)
    if not grad:
        return a
    return a, jnp.concatenate(dq_blocks, axis=0), jnp.concatenate(dk_blocks, axis=0) + dk_left


def _hgrn_fwd(proj, hl):
    s = proj.shape[0]
    n_chunk = HG_TB // CHUNK

    def body(q_ref, f_ref, i_ref, hl_ref, o_ref, st_out_ref, st_ref):
        @pl.when(pl.program_id(1) == 0)
        def _():
            st_ref[...] = jnp.zeros_like(st_ref)

        lb = _lower_bound(hl_ref[...])
        r_i = lax.broadcasted_iota(jnp.int32, (CHUNK, CHUNK), 0)
        c_i = lax.broadcasted_iota(jnp.int32, (CHUNK, CHUNK), 1)
        tri_lower = (r_i >= c_i).astype(BF16)

        def chunk(c, carry):
            rows = pl.ds(pl.multiple_of(c * CHUNK, CHUNK), CHUNK)
            v = i_ref[rows, :].astype(BF16)
            _, _, kk, _, qs, b = _hg_gates(q_ref[rows, :], f_ref[rows, :], lb, tri_lower)
            a = _hg_intra(qs, kk, b)
            st = st_ref[...]
            st_b = st.astype(BF16)
            st_out_ref[c] = st_b
            o_ref[rows, :] = _dot((qs * jnp.exp(b)).astype(BF16), st_b, NT) + _dot(a.astype(BF16), v, NN)
            b_last = b[CHUNK - 1:CHUNK, :]
            st_ref[...] = st * jnp.exp(b_last) + _dot(v, (kk * jnp.exp(b_last - b)).astype(BF16), TN)
            return carry

        lax.fori_loop(0, n_chunk, chunk, 0)

    return pl.pallas_call(
        body, grid=(HG_HEADS, s // HG_TB), in_specs=_hg_specs(lambda n: n),
        out_specs=[pl.BlockSpec((HG_TB, LANE), lambda h, n: (n, h)), pl.BlockSpec((None, n_chunk, HG_K, HG_K), lambda h, n: (h, n, 0, 0))],
        out_shape=[_sds((s, HG_W), F32), _sds((HG_HEADS, s // CHUNK, HG_K, HG_K), BF16)],
        scratch_shapes=[pltpu.VMEM((HG_K, HG_K), F32)],
        compiler_params=_params(2), name="hgrn_fwd")(proj, proj, proj, hl)


def _hgrn_bwd(proj, hl, states, d_o):
    s = proj.shape[0]
    n_chunk = HG_TB // CHUNK
    n_blk = s // HG_TB
    rev = lambda n: n_blk - 1 - n

    def body(q_ref, f_ref, i_ref, hl_ref, st_in_ref, do_ref, dq_ref, df_ref, di_ref, dhl_ref, dst_ref, dlb_ref):
        n = pl.program_id(1)

        @pl.when(n == 0)
        def _():
            dst_ref[...] = jnp.zeros_like(dst_ref)
            dlb_ref[...] = jnp.zeros_like(dlb_ref)

        lb = _lower_bound(hl_ref[...])
        r_i = lax.broadcasted_iota(jnp.int32, (CHUNK, CHUNK), 0)
        c_i = lax.broadcasted_iota(jnp.int32, (CHUNK, CHUNK), 1)
        tri_lower = (r_i >= c_i).astype(BF16)
        tri_upper = (r_i <= c_i).astype(BF16)
        row = lax.broadcasted_iota(jnp.int32, (CHUNK, 1), 0)

        def chunk(cc, carry):
            c = n_chunk - 1 - cc
            rows = pl.ds(pl.multiple_of(c * CHUNK, CHUNK), CHUNK)
            q_raw = q_ref[rows, :]
            vb = i_ref[rows, :].astype(BF16)
            sg, f, kk, sq, qs, b = _hg_gates(q_raw, f_ref[rows, :], lb, tri_lower)
            e_b = jnp.exp(b)
            qe = qs * e_b
            b_last = b[CHUNK - 1:CHUNK, :]
            e_last = jnp.exp(b_last)
            e_kd = jnp.exp(b_last - b)
            kd = kk * e_kd
            st0 = st_in_ref[c]
            d_ob = do_ref[rows, :].astype(BF16)
            dst = dst_ref[...]
            dst_b = dst.astype(BF16)
            d_a = jnp.where(r_i >= c_i, _dot(d_ob, vb, NT), 0.0)
            a, dqs, dkk = _hg_intra(qs, kk, b, d_a)
            d_v = _dot(a.astype(BF16), d_ob, TN) + _dot(kd.astype(BF16), dst_b, NT)
            d_kd = _dot(vb, dst_b, NN)
            dqs = dqs + _dot(d_ob, st0, NN) * e_b
            dkk = dkk + d_kd * e_kd
            d_b_last = jnp.sum(d_kd * kd, axis=0, keepdims=True) + jnp.sum(dst * st0.astype(F32), axis=0, keepdims=True) * e_last
            d_b = qs * dqs - kk * dkk + jnp.where(row == CHUNK - 1, d_b_last, 0.0)
            d_g = _tri_matmul(tri_upper, d_b)
            dst_ref[...] = _dot(d_ob, qe.astype(BF16), TN) + dst * e_last
            d_f = d_g / f - dkk
            dlb_ref[...] += jnp.sum(d_f * (1.0 - sg), axis=0, keepdims=True)
            dq_ref[rows, :] = (dqs * (sq * (1.0 + q_raw * (1.0 - sq)))).astype(dq_ref.dtype)
            df_ref[rows, :] = (d_f * (1.0 - lb) * (sg * (1.0 - sg))).astype(df_ref.dtype)
            di_ref[rows, :] = d_v.astype(di_ref.dtype)
            return carry

        lax.fori_loop(0, n_chunk, chunk, 0)

        @pl.when(n == n_blk - 1)
        def _():
            d_hl0 = dlb_ref[...] * (lb * (1.0 - lb))
            dhl_ref[...] = jnp.concatenate([d_hl0, -d_hl0], axis=0)

    out_blk = pl.BlockSpec((HG_TB, LANE), lambda h, n: (rev(n), h))
    return pl.pallas_call(
        body, grid=(HG_HEADS, n_blk),
        in_specs=_hg_specs(rev) + [pl.BlockSpec((None, n_chunk, HG_K, HG_K), lambda h, n: (h, rev(n), 0, 0)), out_blk],
        out_specs=[out_blk, out_blk, out_blk, pl.BlockSpec((2, LANE), lambda h, n: (0, h))],
        out_shape=[_sds((s, HG_W), BF16)] * 3 + [_sds((2, HG_W), F32)],
        scratch_shapes=[pltpu.VMEM((HG_K, HG_K), F32), pltpu.VMEM((1, LANE), F32)],
        compiler_params=_params(2), name="hgrn_bwd")(proj, proj, proj, hl, states, d_o)


def _mod_part(c_all, w_shard, b_shard):
    n = w_shard.shape[1]
    tn = 512

    def body(c_ref, w_ref, b_ref, o_ref):
        o_ref[...] = _dot(c_ref[...].astype(BF16), w_ref[...].astype(BF16), NN) + b_ref[...]

    return pl.pallas_call(body, grid=(n // tn,),
                          in_specs=[pl.BlockSpec((N_DEV, D), lambda j: (0, 0)), pl.BlockSpec((D, tn), lambda j: (0, j)), pl.BlockSpec((1, tn), lambda j: (0, j))],
                          out_specs=pl.BlockSpec((N_DEV, tn), lambda j: (0, j)), out_shape=_sds((N_DEV, n), F32),
                          compiler_params=_params(1, 32 << 20), name="mod_part")(c_all, w_shard, b_shard)


def _grad_w_ada(c_all_t, dmod_cols):
    n = dmod_cols.shape[1]
    tn = 512

    def body(c_ref, d_ref, o_ref):
        cv = c_ref[...].astype(BF16).astype(F32)
        dv = d_ref[...].astype(BF16).astype(F32)
        acc = cv[:, 0:1] * dv[0:1, :]
        for k in range(1, N_DEV):
            acc = acc + cv[:, k:k + 1] * dv[k:k + 1, :]
        o_ref[...] = acc

    return pl.pallas_call(body, grid=(n // tn,),
                          in_specs=[pl.BlockSpec((D, N_DEV), lambda j: (0, 0)), pl.BlockSpec((N_DEV, tn), lambda j: (0, j))],
                          out_specs=pl.BlockSpec((D, tn), lambda j: (0, j)), out_shape=_sds((D, n), F32),
                          compiler_params=_params(1, 32 << 20), name="grad_w_ada")(c_all_t, dmod_cols)


def _row_tile(r, c):
    if r * c * 4 <= (1 << 20) or r % 8:
        return r
    best = 8
    for t in range(8, r + 1, 8):
        if r % t == 0 and t * c * 4 <= (1 << 20):
            best = t
    return best


def _adamw(pieces, w, m, v, name, emit_grad=True):
    p, r, c = pieces.shape
    tr = _row_tile(r, c)
    c1 = 1.0 / (1.0 - ADAM_B1 ** ADAM_STEP)
    c2 = 1.0 / (1.0 - ADAM_B2 ** ADAM_STEP)

    def body(p_ref, w_ref, m_ref, v_ref, *outs):
        g = p_ref[0].astype(F32)
        for k in range(1, p):
            g = g + p_ref[k].astype(F32)
        m2 = ADAM_B1 * m_ref[...] + (1.0 - ADAM_B1) * g
        v2 = ADAM_B2 * v_ref[...] + (1.0 - ADAM_B2) * (g * g)
        delta = -ADAM_LR * ((m2 * c1) / (jnp.sqrt(v2 * c2) + ADAM_EPS) + ADAM_WD * w_ref[...])
        if emit_grad:
            outs[0][...] = g
        outs[-3][...] = delta
        outs[-2][...] = m2
        outs[-1][...] = v2

    blk = pl.BlockSpec((tr, c), lambda i: (i, 0))
    n_out = 4 if emit_grad else 3
    return pl.pallas_call(body, grid=(r // tr,), in_specs=[pl.BlockSpec((p, tr, c), lambda i: (0, i, 0)), blk, blk, blk],
                          out_specs=[blk] * n_out, out_shape=[_sds((r, c), F32)] * n_out,
                          compiler_params=_params(1, 48 << 20), name=name)(pieces, w, m, v)


def _my_coords():
    return lax.axis_index("x"), lax.axis_index("y"), lax.axis_index("c")


def _flip(coords, k):
    x, y, c = coords
    return (1 - x if k & 4 else x, 1 - y if k & 2 else y, 1 - c if k & 1 else c)


def _lin(coords):
    return 4 * coords[0] + 2 * coords[1] + coords[2]


def _exchange_small(x3, bcast, name):
    n = x3.shape[2]

    def body(x_ref, o_ref, send_sems, recv_sems):
        me = _my_coords()
        my_id = _lin(me)
        o_ref[pl.ds(my_id, 1)] = x_ref[pl.ds(0 if bcast else my_id, 1)]
        copies = []
        for k in range(1, N_DEV):
            peer = _flip(me, k)
            src = x_ref.at[0 if bcast else _lin(peer)]
            cp = pltpu.make_async_remote_copy(src_ref=src, dst_ref=o_ref.at[my_id], send_sem=send_sems.at[k], recv_sem=recv_sems.at[k],
                                              device_id=peer, device_id_type=MESH)
            cp.start()
            copies.append(cp)
        for k in range(1, N_DEV):
            peer = _flip(me, k)
            pltpu.make_async_remote_copy(src_ref=x_ref.at[0], dst_ref=o_ref.at[_lin(peer)], send_sem=send_sems.at[k], recv_sem=recv_sems.at[k],
                                         device_id=peer, device_id_type=MESH).wait_recv()
        for cp in copies:
            cp.wait_send()

    vm = pl.BlockSpec(memory_space=pltpu.VMEM)
    return pl.pallas_call(body, in_specs=[vm], out_specs=vm, out_shape=_sds((N_DEV, 1, n), F32),
                          scratch_shapes=[pltpu.SemaphoreType.DMA((N_DEV,)), pltpu.SemaphoreType.DMA((N_DEV,))], name=name)(x3)


def _allgather_weights(shards):
    nw = len(shards)

    def body(*refs):
        ins, outs = refs[:nw], refs[nw:2 * nw]
        send_sems, recv_sems, local_sems = refs[2 * nw:]
        me = _my_coords()
        x, y, c = me
        sibling = (x, y, 1 - c)
        chips = [(1 - x, y), (x, 1 - y), (1 - x, 1 - y)]

        def copy(w, k, block, to, src=None):
            dst = outs[w].at[_lin(block)]
            return pltpu.make_async_remote_copy(src_ref=dst if src is None else src, dst_ref=dst, send_sem=send_sems.at[w, k],
                                                recv_sem=recv_sems.at[w, k], device_id=to, device_id_type=MESH)

        mine = [pltpu.make_async_copy(ins[w], outs[w].at[_lin(me)], local_sems.at[w]) for w in range(nw)]
        for cp in mine:
            cp.start()
        sent = []
        for w in range(nw):
            sent.append(copy(w, 0, me, sibling, src=ins[w]))
            sent += [copy(w, 1 + j, me, (*chip, c), src=ins[w]) for j, chip in enumerate(chips)]
        for cp in sent:
            cp.start()
        for w in range(nw):
            for j, chip in enumerate(chips):
                copy(w, 1 + j, (*chip, c), me).wait_recv()
                passed = copy(w, 4 + j, (*chip, c), sibling)
                passed.start()
                sent.append(passed)
        for w in range(nw):
            copy(w, 0, sibling, me).wait_recv()
            for j, chip in enumerate(chips):
                copy(w, 4 + j, (*chip, 1 - c), me).wait_recv()
        for cp in sent:
            cp.wait_send()
        for cp in mine:
            cp.wait()

    hbm = pl.BlockSpec(memory_space=pl.ANY)
    return pl.pallas_call(body, in_specs=[hbm] * nw, out_specs=[hbm] * nw,
                          out_shape=[_sds((N_DEV,) + s.shape, s.dtype) for s in shards],
                          scratch_shapes=[pltpu.SemaphoreType.DMA((nw, 7)), pltpu.SemaphoreType.DMA((nw, 7)), pltpu.SemaphoreType.DMA((nw,))],
                          name="allgather_weights")(*shards)


def _scatter_grads(grads):
    nw = len(grads)

    def body(*refs):
        ins, outs = refs[:nw], refs[nw:2 * nw]
        send_sems, recv_sems, local_sems = refs[2 * nw:]
        me = _my_coords()
        my_id = _lin(me)
        mine = [pltpu.make_async_copy(ins[w].at[my_id], outs[w].at[my_id], local_sems.at[w]) for w in range(nw)]
        for cp in mine:
            cp.start()
        sent = []
        for w in range(nw):
            for k in range(1, N_DEV):
                peer = _flip(me, k)
                cp = pltpu.make_async_remote_copy(src_ref=ins[w].at[_lin(peer)], dst_ref=outs[w].at[my_id], send_sem=send_sems.at[w, k],
                                                  recv_sem=recv_sems.at[w, k], device_id=peer, device_id_type=MESH)
                cp.start()
                sent.append(cp)
        for w in range(nw):
            for k in range(1, N_DEV):
                peer = _flip(me, k)
                pltpu.make_async_remote_copy(src_ref=ins[w].at[my_id], dst_ref=outs[w].at[_lin(peer)], send_sem=send_sems.at[w, k],
                                             recv_sem=recv_sems.at[w, k], device_id=peer, device_id_type=MESH).wait_recv()
        for cp in sent:
            cp.wait_send()
        for cp in mine:
            cp.wait()

    hbm = pl.BlockSpec(memory_space=pl.ANY)
    return pl.pallas_call(body, in_specs=[hbm] * nw, out_specs=[hbm] * nw, out_shape=[_sds(g.shape, g.dtype) for g in grads],
                          scratch_shapes=[pltpu.SemaphoreType.DMA((nw, N_DEV)), pltpu.SemaphoreType.DMA((nw, N_DEV)), pltpu.SemaphoreType.DMA((nw,))],
                          name="scatter_grads")(*grads)


def _rope_tables(positions):
    half = ROT // 2
    inv_freq = ROPE_THETA ** (-jnp.arange(0, ROT, 2, dtype=F32) / ROT)
    ang = positions.astype(F32).reshape(-1, 1) * inv_freq
    cos, sin = jnp.cos(ang), jnp.sin(ang)
    s = ang.shape[0]
    pad = jnp.zeros((s, HEAD_DIM - ROT), F32)
    zero = jnp.zeros((s, half), F32)
    two = lambda t: jnp.concatenate([t, t], axis=1)
    return (two(jnp.concatenate([cos, cos, pad + 1.0], axis=1)), two(jnp.concatenate([-sin, zero, pad], axis=1)),
            two(jnp.concatenate([zero, sin, pad], axis=1)))


def _local_step(x, tgt, tabs, mod, sinks_pad, hl, hg_norm, g_pre_mix, g_post_mix, g_pre_ffn, g_post_ffn,
                w_in, w_in_dm, w_attn_dm, w_hgrn_dm, w_out, w_ffn_in_dm, w_ffn_out):
    s = x.shape[0]
    h1 = _pre_fwd(x, g_pre_mix, mod, 1, 0, "pre_mix_fwd")
    proj = _mm_nn(h1, w_in, s, 256, D, F32, "proj_mm")
    att = _attn_fwd(proj, tabs, sinks_pad)
    o_raw, states = _hgrn_fwd(proj, hl)
    ohg = _hgout_fwd(o_raw, proj, hg_norm)
    y_a = _mm_nn_dm(att, w_attn_dm, s, F32, "attn_proj_mm")
    y_h = _mm_nn_dm(ohg, w_hgrn_dm, s, F32, "hgrn_proj_mm")
    merged = _merge_fwd(y_a, y_h, proj)
    y = _mm_nn(merged, w_out, s, 512, D, F32, "out_mm")
    x1 = _post_fwd(x, y, g_post_mix, mod, 2, "post_mix_fwd")
    h2 = _pre_fwd(x1, g_pre_ffn, mod, 4, 3, "pre_ffn_fwd")
    gu = _mm_nn_dm(h2, w_ffn_in_dm, s // 2, F32, "ffn_in_mm")
    act = _swiglu_fwd(gu)
    y2 = _mm_nn(act, w_ffn_out, s, 512, FFN // 4, F32, "ffn_out_mm")
    err, loss = _post_fwd_loss(x1, y2, g_post_ffn, mod, 5, tgt, "post_ffn_loss")
    dy2, d_gate2, dg_post_ffn = _post_bwd(err, y2, g_post_ffn, mod, 5, "post_ffn_bwd")
    d_act = _mm_nt(dy2, w_ffn_out, s, 512, D, F32, "ffn_out_dx")
    gw_ffn_out = _mm_tn(act, dy2, 512, D, BF16, "ffn_out_dw")
    dgu = _swiglu_bwd(d_act, gu)
    dh2 = _mm_nt_dm(dgu, w_ffn_in_dm, s, 512, F32, "ffn_in_dx")
    gw_ffn_in = _mm_tn_dm(h2, dgu, 512, BF16, "ffn_in_dw")
    dx1, d_shift2, d_scale2, dg_pre_ffn = _pre_bwd(dh2, x1, err, g_pre_ffn, mod, 4, "pre_ffn_bwd")
    dy, d_gate1, dg_post_mix = _post_bwd(dx1, y, g_post_mix, mod, 2, "post_mix_bwd")
    d_merged = _mm_nt(dy, w_out, s, 512, D, F32, "out_dx")
    gw_out = _mm_tn(merged, dy, 512, D, BF16, "out_dw")
    dy_a, dy_h, d_gate_a, d_gate_h = _merge_bwd(d_merged, y_a, y_h, proj)
    d_att = _mm_nt_dm(dy_a, w_attn_dm, s, 512, F32, "attn_proj_dx")
    gw_attn = _mm_tn_dm(att, dy_a, 512, BF16, "attn_proj_dw")
    d_ohg = _mm_nt_dm(dy_h, w_hgrn_dm, s, 512, F32, "hgrn_proj_dx")
    gw_hgrn = _mm_tn_dm(ohg, dy_h, 512, BF16, "hgrn_proj_dw")
    d_o, d_gh, d_hg_norm = _hgout_bwd(d_ohg, o_raw, proj, hg_norm)
    d_qh, d_fh, d_ih, d_hl = _hgrn_bwd(proj, hl, states, d_o)
    d_qa, d_ka, d_va, d_sinks = _attn_bwd(proj, tabs, sinks_pad, d_att)
    d_proj = jnp.concatenate([d_qa, d_ka.astype(BF16), d_va.astype(BF16), d_qh, d_fh, d_ih, d_gh, d_gate_a, d_gate_h], axis=1)
    d_proj_dm = d_proj.reshape(s, N_DEV, IN_COLS // N_DEV).transpose(1, 0, 2)
    dh1 = _mm_nt_dm2(d_proj_dm, w_in_dm, s // 2, 512, F32, "proj_dx")
    gw_in = _mm_tn_dm(h1, d_proj_dm, 512, BF16, "proj_dw", b_dm=True)
    grad_x, d_shift1, d_scale1, dg_pre_mix = _pre_bwd(dh1, x, dx1, g_pre_mix, mod, 1, "pre_mix_bwd")
    d_mod = jnp.concatenate([d_shift1, d_scale1, d_gate1, d_shift2, d_scale2, d_gate2], axis=1)
    small = [d_mod, dg_pre_mix, dg_post_mix, dg_pre_ffn, dg_post_ffn, d_hl.reshape(1, 2 * HG_W), d_hg_norm, d_sinks]
    big = [gw_in, gw_attn, gw_hgrn, gw_out.reshape(N_DEV, D // N_DEV, D), gw_ffn_in, gw_ffn_out.reshape(N_DEV, FFN // N_DEV, D)]
    return loss, grad_x, small, big


def kernel(x, c, positions, w_ada, b_ada, g_pre_mix, g_post_mix, g_pre_ffn, g_post_ffn, w_in, attn_sinks, w_attn_proj, hg_lower_bounds, hg_norm, w_hgrn_proj, w_out, w_ffn_in, w_ffn_out, loss_target, m_w_ada, m_b_ada, m_g_pre_mix, m_g_post_mix, m_g_pre_ffn, m_g_post_ffn, m_w_in, m_attn_sinks, m_w_attn_proj, m_hg_lower_bounds, m_hg_norm, m_w_hgrn_proj, m_w_out, m_w_ffn_in, m_w_ffn_out, v_w_ada, v_b_ada, v_g_pre_mix, v_g_post_mix, v_g_pre_ffn, v_g_post_ffn, v_w_in, v_attn_sinks, v_w_attn_proj, v_hg_lower_bounds, v_hg_norm, v_w_hgrn_proj, v_w_out, v_w_ffn_in, v_w_ffn_out):
    my_id = _lin(_my_coords())
    s = x.shape[1]
    n_ada = w_ada.shape[2]

    big_w = [w_in[0], w_attn_proj[0], w_hgrn_proj[0], w_out[0], w_ffn_in[0], w_ffn_out[0]]
    w_in_dm, w_attn_dm, w_hgrn_dm, w_out_g, w_ffn_in_dm, w_ffn_out_g = _allgather_weights([w.astype(BF16) for w in big_w])
    w_in_full = w_in_dm.transpose(1, 0, 2).reshape(D, IN_COLS)

    c_all = _exchange_small(c.reshape(1, 1, D), True, "gather_c").reshape(N_DEV, D)
    b_cols = lax.dynamic_slice(b_ada, (0, my_id * n_ada), (1, n_ada))
    mod_part = _mod_part(c_all, w_ada[0], b_cols)
    mod = _exchange_small(mod_part.reshape(N_DEV, 1, n_ada), False, "scatter_mod").reshape(1, N_MOD * D)

    sinks_pad = jnp.pad(attn_sinks, ((0, 0), (0, LANE - ATT_HEADS)))
    loss, grad_x, small, big = _local_step(
        x[0], loss_target[0], _rope_tables(positions), mod, sinks_pad, hg_lower_bounds, hg_norm, g_pre_mix, g_post_mix, g_pre_ffn, g_post_ffn,
        w_in_full, w_in_dm, w_attn_dm, w_hgrn_dm, w_out_g.reshape(D, D), w_ffn_in_dm, w_ffn_out_g.reshape(FFN, D))
    loss = lax.psum(loss[0, 0], ("x", "y", "c"))

    sizes = [t.shape[1] for t in small]
    parts = _exchange_small(jnp.concatenate(small, axis=1).reshape(1, 1, sum(sizes)), True, "gather_small_grads")
    offs = [sum(sizes[:k]) for k in range(len(sizes))]
    piece = lambda k, n=None: parts[:, :, offs[k]:offs[k] + (sizes[k] if n is None else n)]
    small_w = [(piece(0), b_ada, m_b_ada, v_b_ada), (piece(1), g_pre_mix, m_g_pre_mix, v_g_pre_mix),
               (piece(2), g_post_mix, m_g_post_mix, v_g_post_mix), (piece(3), g_pre_ffn, m_g_pre_ffn, v_g_pre_ffn),
               (piece(4), g_post_ffn, m_g_post_ffn, v_g_post_ffn),
               (piece(5).reshape(N_DEV, 2, HG_W), hg_lower_bounds, m_hg_lower_bounds, v_hg_lower_bounds),
               (piece(6), hg_norm, m_hg_norm, v_hg_norm), (piece(7, ATT_HEADS), attn_sinks, m_attn_sinks, v_attn_sinks)]
    names = ["b_ada", "g_pre_mix", "g_post_mix", "g_pre_ffn", "g_post_ffn", "hg_lower_bounds", "hg_norm", "attn_sinks"]
    res = {n: _adamw(p, w, m, v, "adamw_" + n) for n, (p, w, m, v) in zip(names, small_w)}

    dmod_cols = lax.dynamic_slice(parts.reshape(N_DEV, -1), (0, my_id * n_ada), (N_DEV, n_ada))
    g_w_ada = _grad_w_ada(c_all.T, dmod_cols)
    res["w_ada"] = [g_w_ada] + list(_adamw(g_w_ada[None], w_ada[0], m_w_ada[0], v_w_ada[0], "adamw_w_ada", emit_grad=False))

    recv = _scatter_grads(big)
    big_names = ["w_in", "w_attn_proj", "w_hgrn_proj", "w_out", "w_ffn_in", "w_ffn_out"]
    big_m = [m_w_in, m_w_attn_proj, m_w_hgrn_proj, m_w_out, m_w_ffn_in, m_w_ffn_out]
    big_v = [v_w_in, v_w_attn_proj, v_w_hgrn_proj, v_w_out, v_w_ffn_in, v_w_ffn_out]
    for n, p, w, m, v in zip(big_names, recv, big_w, big_m, big_v):
        res[n] = _adamw(p, w, m[0], v[0], "adamw_" + n)

    order = ["w_ada", "b_ada", "g_pre_mix", "g_post_mix", "g_pre_ffn", "g_post_ffn", "w_in", "attn_sinks", "w_attn_proj",
             "hg_lower_bounds", "hg_norm", "w_hgrn_proj", "w_out", "w_ffn_in", "w_ffn_out"]
    lead = {"w_ada", "w_in", "w_attn_proj", "w_hgrn_proj", "w_out", "w_ffn_in", "w_ffn_out"}
    outs = [loss, grad_x[None]]
    for k in range(4):
        outs += [res[n][k][None] if n in lead else res[n][k] for n in order]
    return tuple(outs)
```

```python
import functools

import jax
import jax.numpy as jnp
from jax import lax
from jax.experimental import pallas as pl
from jax.experimental.pallas import tpu as pltpu

F32 = jnp.float32
BF16 = jnp.bfloat16

N_DEV = 8
D = 2048
ATT_HEADS = 16
KV_HEADS = 2
HEAD_DIM = 64
GROUP = ATT_HEADS // KV_HEADS
ATT_W = ATT_HEADS * HEAD_DIM
BLK = 128
ROT = HEAD_DIM // 4
ROPE_THETA = 500000.0
HG_HEADS = 8
HG_K = 128
HG_W = HG_HEADS * HG_K
CHUNK = 64
SUB = 16
FFN = 5632
N_MOD = 6
EPS = 1e-6
LANE = 128
Q_A, K_A, V_A, Q_H, F_H, I_H, G_H, GT_A, GT_H, IN_COLS = 0, 1024, 1152, 1280, 2304, 3328, 4352, 5376, 7424, 9472

ADAM_LR, ADAM_B1, ADAM_B2, ADAM_EPS, ADAM_WD, ADAM_STEP = 0.001, 0.9, 0.999, 1e-08, 0.01, 10

TR = 256
HG_TB = 512
VMEM_BIG = 56 << 20
MESH = pl.DeviceIdType.MESH


def _sds(shape, dtype):
    return jax.ShapeDtypeStruct(shape, dtype)


def _params(n_axes, vmem=None):
    return pltpu.CompilerParams(dimension_semantics=("arbitrary",) * n_axes, vmem_limit_bytes=vmem)


def _sig(t):
    return 1.0 / (1.0 + jnp.exp(-t))


def _dot(a, b, dims):
    return lax.dot_general(a, b, (dims, ((), ())), preferred_element_type=F32)


NN = ((1,), (0,))
NT = ((1,), (1,))
TN = ((0,), (0,))


def _matmul(a, b, a_spec, b_spec, o_spec, out_shape, grid, dims, acc_shape, name):
    nk = grid[2]

    def body(a_ref, b_ref, o_ref, *scratch):
        part = _dot(a_ref[...], b_ref[...], dims)
        if nk == 1:
            o_ref[...] = part.astype(o_ref.dtype)
        else:
            acc = scratch[0]
            k = pl.program_id(2)

            @pl.when(k == 0)
            def _():
                acc[...] = part

            @pl.when(k > 0)
            def _():
                acc[...] += part

            @pl.when(k == nk - 1)
            def _():
                o_ref[...] = acc[...].astype(o_ref.dtype)

    return pl.pallas_call(
        body, grid=grid, in_specs=[a_spec, b_spec], out_specs=o_spec, out_shape=out_shape,
        scratch_shapes=[pltpu.VMEM(acc_shape, F32)] if nk > 1 else [],
        compiler_params=_params(3, VMEM_BIG), name=name)(a, b)


def _mm_nn(a, b, tm, tn, tk, out_dtype, name):
    m, k = a.shape
    n = b.shape[1]
    return _matmul(a, b, pl.BlockSpec((tm, tk), lambda j, i, kk: (i, kk)), pl.BlockSpec((tk, tn), lambda j, i, kk: (kk, j)),
                   pl.BlockSpec((tm, tn), lambda j, i, kk: (i, j)), _sds((m, n), out_dtype),
                   (n // tn, m // tm, k // tk), NN, (tm, tn), name)


def _mm_nn_dm(a, b, tm, out_dtype, name):
    m, k = a.shape
    n = b.shape[2]
    return _matmul(a, b, pl.BlockSpec((tm, k), lambda j, i, kk: (i, 0)), pl.BlockSpec((None, k, n), lambda j, i, kk: (j, 0, 0)),
                   pl.BlockSpec((tm, n), lambda j, i, kk: (i, j)), _sds((m, N_DEV * n), out_dtype),
                   (N_DEV, m // tm, 1), NN, (tm, n), name)


def _mm_nt(a, b, tm, tn, tk, out_dtype, name):
    m, k = a.shape
    n = b.shape[0]
    return _matmul(a, b, pl.BlockSpec((tm, tk), lambda j, i, kk: (i, kk)), pl.BlockSpec((tn, tk), lambda j, i, kk: (j, kk)),
                   pl.BlockSpec((tm, tn), lambda j, i, kk: (i, j)), _sds((m, n), out_dtype),
                   (n // tn, m // tm, k // tk), NT, (tm, tn), name)


def _mm_nt_dm(a, b, tm, tn, out_dtype, name):
    m = a.shape[0]
    n_out, n = b.shape[1], b.shape[2]
    return _matmul(a, b, pl.BlockSpec((tm, n), lambda j, i, kk: (i, kk)), pl.BlockSpec((None, tn, n), lambda j, i, kk: (kk, j, 0)),
                   pl.BlockSpec((tm, tn), lambda j, i, kk: (i, j)), _sds((m, n_out), out_dtype),
                   (n_out // tn, m // tm, N_DEV), NT, (tm, tn), name)


def _mm_nt_dm2(a, b, tm, tn, out_dtype, name):
    m = a.shape[1]
    n_out, n = b.shape[1], b.shape[2]
    return _matmul(a, b, pl.BlockSpec((None, tm, n), lambda j, i, kk: (kk, i, 0)), pl.BlockSpec((None, tn, n), lambda j, i, kk: (kk, j, 0)),
                   pl.BlockSpec((tm, tn), lambda j, i, kk: (i, j)), _sds((m, n_out), out_dtype),
                   (n_out // tn, m // tm, N_DEV), NT, (tm, tn), name)


def _mm_tn(a, b, tm, tn, out_dtype, name):
    s, m = a.shape
    n = b.shape[1]
    return _matmul(a, b, pl.BlockSpec((s, tm), lambda j, i, kk: (0, i)), pl.BlockSpec((s, tn), lambda j, i, kk: (0, j)),
                   pl.BlockSpec((tm, tn), lambda j, i, kk: (i, j)), _sds((m, n), out_dtype),
                   (n // tn, m // tm, 1), TN, (tm, tn), name)


def _mm_tn_dm(a, b, tm, out_dtype, name, b_dm=False):
    s, m = a.shape
    if b_dm:
        n = b.shape[2]
        b_spec = pl.BlockSpec((None, s, n), lambda j, i, kk: (j, 0, 0))
    else:
        n = b.shape[1] // N_DEV
        b_spec = pl.BlockSpec((s, n), lambda j, i, kk: (0, j))
    return _matmul(a, b, pl.BlockSpec((s, tm), lambda j, i, kk: (0, i)), b_spec,
                   pl.BlockSpec((None, tm, n), lambda j, i, kk: (j, i, 0)), _sds((N_DEV, m, n), out_dtype),
                   (N_DEV, m // tm, 1), TN, (tm, n), name)


def _row_spec():
    return pl.BlockSpec((TR, D), lambda i: (i, 0))


def _vec_spec(k=0):
    return pl.BlockSpec((1, D), lambda i: (0, k))


def _acc_rows(ref, first, val):
    @pl.when(first)
    def _():
        ref[...] = val

    @pl.when(jnp.logical_not(first))
    def _():
        ref[...] += val


def _pre_fwd(x, g, mod, k_scale, k_shift, name):
    s = x.shape[0]

    def body(x_ref, g_ref, sc_ref, sh_ref, h_ref):
        xv = x_ref[...]
        r = lax.rsqrt(jnp.mean(xv * xv, axis=-1, keepdims=True) + EPS)
        n = xv * r * g_ref[...]
        h_ref[...] = (n * (1.0 + sc_ref[...]) + sh_ref[...]).astype(h_ref.dtype)

    return pl.pallas_call(body, grid=(s // TR,), in_specs=[_row_spec(), _vec_spec(), _vec_spec(k_scale), _vec_spec(k_shift)],
                          out_specs=_row_spec(), out_shape=_sds((s, D), BF16), compiler_params=_params(1), name=name)(x, g, mod, mod)


def _post_fwd(x, y, g, mod, k_gate, name):
    s = x.shape[0]

    def body(x_ref, y_ref, g_ref, gt_ref, o_ref):
        yv = y_ref[...]
        r = lax.rsqrt(jnp.mean(yv * yv, axis=-1, keepdims=True) + EPS)
        o_ref[...] = x_ref[...] + gt_ref[...] * (yv * r * g_ref[...])

    return pl.pallas_call(body, grid=(s // TR,), in_specs=[_row_spec(), _row_spec(), _vec_spec(), _vec_spec(k_gate)],
                          out_specs=_row_spec(), out_shape=_sds((s, D), F32), compiler_params=_params(1), name=name)(x, y, g, mod)


def _post_fwd_loss(x, y, g, mod, k_gate, tgt, name):
    s = x.shape[0]

    def body(x_ref, y_ref, g_ref, gt_ref, t_ref, e_ref, loss_ref):
        i = pl.program_id(0)
        yv = y_ref[...]
        r = lax.rsqrt(jnp.mean(yv * yv, axis=-1, keepdims=True) + EPS)
        err = x_ref[...] + gt_ref[...] * (yv * r * g_ref[...]) - t_ref[...]
        e_ref[...] = err * (1.0 / D)
        part = 0.5 * jnp.sum(jnp.mean(err * err, axis=-1, keepdims=True), axis=0, keepdims=True)
        _acc_rows(loss_ref, i == 0, part)

    return pl.pallas_call(body, grid=(s // TR,),
                          in_specs=[_row_spec(), _row_spec(), _vec_spec(), _vec_spec(k_gate), _row_spec()],
                          out_specs=[_row_spec(), pl.BlockSpec((1, 1), lambda i: (0, 0))],
                          out_shape=[_sds((s, D), F32), _sds((1, 1), F32)], compiler_params=_params(1), name=name)(x, y, g, mod, tgt)


def _pre_bwd(dh, x, res, g, mod, k_scale, name):
    s = x.shape[0]

    def body(dh_ref, x_ref, res_ref, g_ref, sc_ref, dx_ref, dsh_ref, dsc_ref, dg_ref):
        first = pl.program_id(0) == 0
        xv, dh_v, gv = x_ref[...], dh_ref[...], g_ref[...]
        r = lax.rsqrt(jnp.mean(xv * xv, axis=-1, keepdims=True) + EPS)
        xh = xv * r
        dn = dh_v * (1.0 + sc_ref[...])
        dgn = dn * gv
        dx_ref[...] = res_ref[...] + r * (dgn - xh * jnp.mean(dgn * xh, axis=-1, keepdims=True))
        _acc_rows(dsh_ref, first, jnp.sum(dh_v, axis=0, keepdims=True))
        _acc_rows(dsc_ref, first, jnp.sum(dh_v * (xh * gv), axis=0, keepdims=True))
        _acc_rows(dg_ref, first, jnp.sum(dn * xh, axis=0, keepdims=True))

    return pl.pallas_call(body, grid=(s // TR,),
                          in_specs=[_row_spec(), _row_spec(), _row_spec(), _vec_spec(), _vec_spec(k_scale)],
                          out_specs=[_row_spec(), _vec_spec(), _vec_spec(), _vec_spec()],
                          out_shape=[_sds((s, D), F32)] + [_sds((1, D), F32)] * 3,
                          compiler_params=_params(1), name=name)(dh, x, res, g, mod)


def _post_bwd(dx, y, g, mod, k_gate, name):
    s = y.shape[0]

    def body(dx_ref, y_ref, g_ref, gt_ref, dy_ref, dgt_ref, dg_ref):
        first = pl.program_id(0) == 0
        yv, dxv, gv = y_ref[...], dx_ref[...], g_ref[...]
        r = lax.rsqrt(jnp.mean(yv * yv, axis=-1, keepdims=True) + EPS)
        yh = yv * r
        dn = dxv * gt_ref[...]
        dgn = dn * gv
        dy_ref[...] = (r * (dgn - yh * jnp.mean(dgn * yh, axis=-1, keepdims=True))).astype(dy_ref.dtype)
        _acc_rows(dgt_ref, first, jnp.sum(dxv * (yh * gv), axis=0, keepdims=True))
        _acc_rows(dg_ref, first, jnp.sum(dn * yh, axis=0, keepdims=True))

    return pl.pallas_call(body, grid=(s // TR,), in_specs=[_row_spec(), _row_spec(), _vec_spec(), _vec_spec(k_gate)],
                          out_specs=[_row_spec(), _vec_spec(), _vec_spec()],
                          out_shape=[_sds((s, D), BF16), _sds((1, D), F32), _sds((1, D), F32)],
                          compiler_params=_params(1), name=name)(dx, y, g, mod)


SW_TN = 512


def _swiglu_fwd(gu):
    s = gu.shape[0]
    nb = FFN // SW_TN

    def body(g_ref, u_ref, a_ref):
        gv = g_ref[...]
        a_ref[...] = (gv * _sig(gv) * u_ref[...]).astype(a_ref.dtype)

    return pl.pallas_call(body, grid=(s // TR, nb),
                          in_specs=[pl.BlockSpec((TR, SW_TN), lambda i, j: (i, j)), pl.BlockSpec((TR, SW_TN), lambda i, j: (i, j + nb))],
                          out_specs=pl.BlockSpec((TR, SW_TN), lambda i, j: (i, j)), out_shape=_sds((s, FFN), BF16),
                          compiler_params=_params(2), name="swiglu_fwd")(gu, gu)


def _swiglu_bwd(dact, gu):
    s = gu.shape[0]
    nb = FFN // SW_TN

    def body(da_ref, g_ref, u_ref, o_ref):
        half = pl.program_id(2)
        gv, da = g_ref[...], da_ref[...]
        sg = _sig(gv)
        d_gate = da * u_ref[...] * (sg * (1.0 + gv * (1.0 - sg)))
        d_up = da * (gv * sg)
        o_ref[...] = jnp.where(half == 0, d_gate, d_up).astype(o_ref.dtype)

    blk = lambda f: pl.BlockSpec((TR, SW_TN), f)
    return pl.pallas_call(body, grid=(s // TR, nb, 2),
                          in_specs=[blk(lambda i, j, h: (i, j)), blk(lambda i, j, h: (i, j)), blk(lambda i, j, h: (i, j + nb))],
                          out_specs=blk(lambda i, j, h: (i, j + nb * h)), out_shape=_sds((s, 2 * FFN), BF16),
                          compiler_params=_params(3), name="swiglu_bwd")(dact, gu, gu)


MG_TN = 256


def _merge_fwd(y_a, y_h, proj):
    s = y_a.shape[0]
    tn = MG_TN
    ba, bh = GT_A // tn, GT_H // tn

    def body(ya_ref, yh_ref, ga_ref, gh_ref, m_ref):
        m_ref[...] = (_sig(ga_ref[...]) * ya_ref[...] + _sig(gh_ref[...]) * yh_ref[...]).astype(m_ref.dtype)

    blk = lambda f: pl.BlockSpec((TR, tn), f)
    return pl.pallas_call(body, grid=(s // TR, D // tn),
                          in_specs=[blk(lambda i, j: (i, j)), blk(lambda i, j: (i, j)), blk(lambda i, j: (i, j + ba)), blk(lambda i, j: (i, j + bh))],
                          out_specs=blk(lambda i, j: (i, j)), out_shape=_sds((s, D), BF16),
                          compiler_params=_params(2), name="merge_fwd")(y_a, y_h, proj, proj)


def _merge_bwd(dm, y_a, y_h, proj):
    s = y_a.shape[0]
    tn = MG_TN
    ba, bh = GT_A // tn, GT_H // tn

    def body(dm_ref, ya_ref, yh_ref, ga_ref, gh_ref, dya_ref, dyh_ref, dga_ref, dgh_ref):
        dmv = dm_ref[...]
        sa, sh = _sig(ga_ref[...]), _sig(gh_ref[...])
        dya_ref[...] = (dmv * sa).astype(BF16)
        dyh_ref[...] = (dmv * sh).astype(BF16)
        dga_ref[...] = (dmv * ya_ref[...] * (sa * (1.0 - sa))).astype(BF16)
        dgh_ref[...] = (dmv * yh_ref[...] * (sh * (1.0 - sh))).astype(BF16)

    blk = lambda f: pl.BlockSpec((TR, tn), f)
    nat = blk(lambda i, j: (i, j))
    return pl.pallas_call(body, grid=(s // TR, D // tn),
                          in_specs=[nat, nat, nat, blk(lambda i, j: (i, j + ba)), blk(lambda i, j: (i, j + bh))],
                          out_specs=[nat] * 4, out_shape=[_sds((s, D), BF16)] * 4,
                          compiler_params=_params(2), name="merge_bwd")(dm, y_a, y_h, proj, proj)


def _hgout_fwd(o_raw, proj, hg_norm):
    s = o_raw.shape[0]
    bg = G_H // LANE

    def body(o_ref, g_ref, n_ref, out_ref):
        ov = o_ref[...]
        r = lax.rsqrt(jnp.mean(ov * ov, axis=-1, keepdims=True) + EPS)
        out_ref[...] = (ov * r * n_ref[...] * _sig(g_ref[...])).astype(out_ref.dtype)

    blk = lambda f: pl.BlockSpec((TR, LANE), f)
    return pl.pallas_call(body, grid=(s // TR, HG_HEADS),
                          in_specs=[blk(lambda i, h: (i, h)), blk(lambda i, h: (i, h + bg)), pl.BlockSpec((1, LANE), lambda i, h: (0, 0))],
                          out_specs=blk(lambda i, h: (i, h)), out_shape=_sds((s, HG_W), BF16),
                          compiler_params=_params(2), name="hgout_fwd")(o_raw, proj, hg_norm)


def _hgout_bwd(d_out, o_raw, proj, hg_norm):
    s = o_raw.shape[0]
    bg = G_H // LANE

    def body(d_ref, o_ref, g_ref, n_ref, do_ref, dg_ref, dn_ref):
        first = jnp.logical_and(pl.program_id(0) == 0, pl.program_id(1) == 0)
        ov, dv, nv = o_ref[...], d_ref[...], n_ref[...]
        sg = _sig(g_ref[...])
        r = lax.rsqrt(jnp.mean(ov * ov, axis=-1, keepdims=True) + EPS)
        oh = ov * r
        d_on = dv * sg
        dg_ref[...] = (dv * (oh * nv) * (sg * (1.0 - sg))).astype(dg_ref.dtype)
        t = d_on * nv
        do_ref[...] = r * (t - oh * jnp.mean(t * oh, axis=-1, keepdims=True))
        _acc_rows(dn_ref, first, jnp.sum(d_on * oh, axis=0, keepdims=True))

    blk = lambda f: pl.BlockSpec((TR, LANE), f)
    vec = pl.BlockSpec((1, LANE), lambda i, h: (0, 0))
    return pl.pallas_call(body, grid=(s // TR, HG_HEADS),
                          in_specs=[blk(lambda i, h: (i, h)), blk(lambda i, h: (i, h)), blk(lambda i, h: (i, h + bg)), vec],
                          out_specs=[blk(lambda i, h: (i, h)), blk(lambda i, h: (i, h)), vec],
                          out_shape=[_sds((s, HG_W), F32), _sds((s, HG_W), BF16), _sds((1, LANE), F32)],
                          compiler_params=_params(2), name="hgout_bwd")(d_out, o_raw, proj, hg_norm)


def _rope(t, cos, s_lo, s_hi):
    return t * cos + pltpu.roll(t, LANE - ROT // 2, 1) * s_lo + pltpu.roll(t, ROT // 2, 1) * s_hi


def _rope_wide(t, cos, s_lo, s_hi):
    return jnp.concatenate([_rope(t[:, k * LANE:(k + 1) * LANE], cos, s_lo, s_hi) for k in range(t.shape[1] // LANE)], axis=1)


def _attn_mask(has_prev):
    qi = lax.broadcasted_iota(jnp.int32, (BLK, 2 * BLK), 0)
    kj = lax.broadcasted_iota(jnp.int32, (BLK, 2 * BLK), 1)
    rel = BLK + qi - kj
    band = jnp.logical_and(rel >= 0, rel < BLK)
    return jnp.logical_and(band, jnp.logical_or(has_prev, kj >= BLK))


def _attn_specs():
    prev = lambda i: jnp.maximum(i - 1, 0)
    kb, vb = K_A // LANE, V_A // LANE
    blk = lambda f: pl.BlockSpec((BLK, LANE), f)
    tabs = [blk(lambda i: (i, 0))] * 3 + [blk(lambda i: (prev(i), 0))] * 3
    return [pl.BlockSpec((BLK, ATT_W), lambda i: (i, 0)), blk(lambda i: (i, kb)), blk(lambda i: (prev(i), kb)),
            blk(lambda i: (i, vb)), blk(lambda i: (prev(i), vb))] + tabs + [pl.BlockSpec((1, LANE), lambda i: (0, 0))]


def _attn_probs(qh, kg, mask, sk):
    logits = _dot(qh, kg, NT) * (HEAD_DIM ** -0.5)
    logits = jnp.where(mask, logits, -jnp.inf)
    m = jnp.maximum(jnp.max(logits, axis=-1, keepdims=True), sk)
    p = jnp.exp(logits - m)
    e_sink = jnp.exp(sk - m)
    inv = 1.0 / (jnp.sum(p, axis=-1, keepdims=True) + e_sink)
    return p * inv, e_sink * inv


def _attn_fwd(proj, tabs, sinks):
    s = proj.shape[0]

    def body(q_ref, kc_ref, kp_ref, vc_ref, vp_ref, c0, l0, h0, c1, l1, h1, sk_ref, o_ref):
        i = pl.program_id(0)
        mask = _attn_mask(i > 0)
        q = _rope_wide(q_ref[...], c0[...], l0[...], h0[...]).astype(BF16)
        kk = jnp.concatenate([_rope(kp_ref[...], c1[...], l1[...], h1[...]), _rope(kc_ref[...], c0[...], l0[...], h0[...])], axis=0).astype(BF16)
        vv = jnp.concatenate([vp_ref[...], vc_ref[...]], axis=0).astype(BF16)
        outs = []
        for h in range(ATT_HEADS):
            g = h // GROUP
            prob, _ = _attn_probs(q[:, h * HEAD_DIM:(h + 1) * HEAD_DIM], kk[:, g * HEAD_DIM:(g + 1) * HEAD_DIM], mask, sk_ref[:, h:h + 1])
            outs.append(_dot(prob.astype(BF16), vv[:, g * HEAD_DIM:(g + 1) * HEAD_DIM], NN))
        o_ref[...] = jnp.concatenate(outs, axis=1).astype(o_ref.dtype)

    return pl.pallas_call(body, grid=(s // BLK,), in_specs=_attn_specs(),
                          out_specs=pl.BlockSpec((BLK, ATT_W), lambda i: (i, 0)), out_shape=_sds((s, ATT_W), BF16),
                          compiler_params=_params(1), name="attn_fwd")(proj, proj, proj, proj, proj, *tabs, *tabs, sinks)


def _attn_bwd(proj, tabs, sinks, d_att):
    s = proj.shape[0]

    def body(q_ref, kc_ref, kp_ref, vc_ref, vp_ref, c0, l0, h0, c1, l1, h1, sk_ref, do_ref, dq_ref, dk_ref, dv_ref, ds_ref):
        i = pl.program_id(0)

        @pl.when(i == 0)
        def _():
            dk_ref[...] = jnp.zeros_like(dk_ref)
            dv_ref[...] = jnp.zeros_like(dv_ref)
            ds_ref[...] = jnp.zeros_like(ds_ref)

        mask = _attn_mask(i > 0)
        q = _rope_wide(q_ref[...], c0[...], l0[...], h0[...]).astype(BF16)
        kk = jnp.concatenate([_rope(kp_ref[...], c1[...], l1[...], h1[...]), _rope(kc_ref[...], c0[...], l0[...], h0[...])], axis=0).astype(BF16)
        vv = jnp.concatenate([vp_ref[...], vc_ref[...]], axis=0).astype(BF16)
        d_o = do_ref[...].astype(BF16)
        lane = lax.broadcasted_iota(jnp.int32, (1, LANE), 1)
        dqs, dks, dvs = [], [], []
        d_sink = jnp.zeros((1, LANE), F32)
        for g in range(KV_HEADS):
            kg, vg = kk[:, g * HEAD_DIM:(g + 1) * HEAD_DIM], vv[:, g * HEAD_DIM:(g + 1) * HEAD_DIM]
            dkg = jnp.zeros((2 * BLK, HEAD_DIM), F32)
            dvg = jnp.zeros((2 * BLK, HEAD_DIM), F32)
            for j in range(GROUP):
                h = g * GROUP + j
                qh, doh = q[:, h * HEAD_DIM:(h + 1) * HEAD_DIM], d_o[:, h * HEAD_DIM:(h + 1) * HEAD_DIM]
                prob, p_sink = _attn_probs(qh, kg, mask, sk_ref[:, h:h + 1])
                d_p = _dot(doh, vg, NT)
                dd = jnp.sum(prob * d_p, axis=-1, keepdims=True)
                d_s = (prob * (d_p - dd)).astype(BF16)
                d_sink = d_sink + jnp.where(lane == h, -jnp.sum(p_sink * dd, axis=0, keepdims=True), 0.0)
                dqs.append(_dot(d_s, kg, NN) * (HEAD_DIM ** -0.5))
                dkg = dkg + _dot(d_s, qh, TN) * (HEAD_DIM ** -0.5)
                dvg = dvg + _dot(prob.astype(BF16), doh, TN)
            dks.append(dkg)
            dvs.append(dvg)
        dq_ref[...] = _rope_wide(jnp.concatenate(dqs, axis=1), c0[...], -l0[...], -h0[...]).astype(dq_ref.dtype)
        d_k = jnp.concatenate(dks, axis=1)
        d_v = jnp.concatenate(dvs, axis=1)
        cur = pl.ds(pl.multiple_of(i * BLK, BLK), BLK)
        prv = pl.ds(pl.multiple_of(jnp.maximum(i - 1, 0) * BLK, BLK), BLK)
        dk_ref[prv, :] += _rope(d_k[:BLK], c1[...], -l1[...], -h1[...])
        dk_ref[cur, :] += _rope(d_k[BLK:], c0[...], -l0[...], -h0[...])
        dv_ref[prv, :] += d_v[:BLK]
        dv_ref[cur, :] += d_v[BLK:]
        ds_ref[...] += d_sink

    full = pl.BlockSpec((s, LANE), lambda i: (0, 0))
    return pl.pallas_call(body, grid=(s // BLK,), in_specs=_attn_specs() + [pl.BlockSpec((BLK, ATT_W), lambda i: (i, 0))],
                          out_specs=[pl.BlockSpec((BLK, ATT_W), lambda i: (i, 0)), full, full, pl.BlockSpec((1, LANE), lambda i: (0, 0))],
                          out_shape=[_sds((s, ATT_W), BF16), _sds((s, LANE), F32), _sds((s, LANE), F32), _sds((1, LANE), F32)],
                          compiler_params=_params(1), name="attn_bwd")(proj, proj, proj, proj, proj, *tabs, *tabs, sinks, d_att)


def _tri_matmul(tri, t):
    hi = t.astype(BF16)
    r1 = t - hi.astype(F32)
    mid = r1.astype(BF16)
    lo = (r1 - mid.astype(F32)).astype(BF16)
    return _dot(tri, hi, NN) + _dot(tri, mid, NN) + _dot(tri, lo, NN)


def _lower_bound(hl):
    a, b = hl[0:1, :], hl[1:2, :]
    mx = jnp.maximum(a, b)
    ea, eb = jnp.exp(a - mx), jnp.exp(b - mx)
    return ea / (ea + eb)


def _hg_gates(q_raw, f_raw, lb, tri_lower):
    sg = _sig(f_raw)
    f = lb + (1.0 - lb) * sg
    sq = _sig(q_raw)
    b = _tri_matmul(tri_lower, jnp.log(f))
    return sg, f, 1.0 - f, sq, q_raw * sq, b


def _hg_specs(n_map):
    blk = lambda off: pl.BlockSpec((HG_TB, LANE), lambda h, n: (n_map(n), off // LANE + h))
    return [blk(Q_H), blk(F_H), blk(I_H), pl.BlockSpec((2, LANE), lambda h, n: (0, h))]


def _hg_intra(qs, kk, b, d_a=None):
    lane = lax.broadcasted_iota(jnp.int32, (SUB, CHUNK), 1)
    row1 = lax.broadcasted_iota(jnp.int32, (SUB, 1), 0)
    rowk = lax.broadcasted_iota(jnp.int32, (SUB, LANE), 0)
    grad = d_a is not None
    a_blocks, dq_blocks, dk_blocks = [], [], []
    dk_left = None
    for j in range(CHUNK // SUB):
        lo = j * SUB
        q_j, k_j, b_j = qs[lo:lo + SUB], kk[lo:lo + SUB], b[lo:lo + SUB]
        a_j = jnp.zeros((SUB, CHUNK), F32)
        if grad:
            da_j = d_a[lo:lo + SUB]
            dq_j = jnp.zeros((SUB, LANE), F32)
            dk_j = jnp.zeros((SUB, LANE), F32)
        for sx in range(SUB):
            e = jnp.where(row1 >= sx, jnp.exp(jnp.minimum(b_j - b_j[sx:sx + 1], 0.0)), 0.0)
            pe = q_j * e
            a_j = jnp.where(lane == lo + sx, jnp.sum(pe * k_j[sx:sx + 1], axis=-1, keepdims=True), a_j)
            if grad:
                g_col = da_j[:, lo + sx:lo + sx + 1]
                dq_j = dq_j + g_col * (e * k_j[sx:sx + 1])
                dk_j = jnp.where(rowk == sx, jnp.sum(g_col * pe, axis=0, keepdims=True), dk_j)
        if j > 0:
            ref = b[lo - 1:lo]
            sc_q = jnp.exp(b_j - ref)
            sc_k = jnp.exp(jnp.minimum(ref - b, 0.0))
            qt = (q_j * sc_q).astype(BF16)
            kt = (kk * sc_k).astype(BF16)
            a_j = a_j + jnp.where(lane < lo, _dot(qt, kt, NT), 0.0)
            if grad:
                da_left = jnp.where(lane < lo, da_j, 0.0).astype(BF16)
                dq_j = dq_j + _dot(da_left, kt, NN) * sc_q
                t = _dot(da_left, qt, TN) * sc_k
                dk_left = t if dk_left is None else dk_left + t
        a_blocks.append(a_j)
        if grad:
            dq_blocks.append(dq_j)
            dk_blocks.append(dk_j)
    a = jnp.concatenate(a_blocks, axis=0)
    if not grad:
        return a
    return a, jnp.concatenate(dq_blocks, axis=0), jnp.concatenate(dk_blocks, axis=0) + dk_left


def _hgrn_fwd(proj, hl):
    s = proj.shape[0]
    n_chunk = HG_TB // CHUNK

    def body(q_ref, f_ref, i_ref, hl_ref, o_ref, st_out_ref, st_ref):
        @pl.when(pl.program_id(1) == 0)
        def _():
            st_ref[...] = jnp.zeros_like(st_ref)

        lb = _lower_bound(hl_ref[...])
        r_i = lax.broadcasted_iota(jnp.int32, (CHUNK, CHUNK), 0)
        c_i = lax.broadcasted_iota(jnp.int32, (CHUNK, CHUNK), 1)
        tri_lower = (r_i >= c_i).astype(BF16)

        def chunk(c, carry):
            rows = pl.ds(pl.multiple_of(c * CHUNK, CHUNK), CHUNK)
            v = i_ref[rows, :].astype(BF16)
            _, _, kk, _, qs, b = _hg_gates(q_ref[rows, :], f_ref[rows, :], lb, tri_lower)
            a = _hg_intra(qs, kk, b)
            st = st_ref[...]
            st_b = st.astype(BF16)
            st_out_ref[c] = st_b
            o_ref[rows, :] = _dot((qs * jnp.exp(b)).astype(BF16), st_b, NT) + _dot(a.astype(BF16), v, NN)
            b_last = b[CHUNK - 1:CHUNK, :]
            st_ref[...] = st * jnp.exp(b_last) + _dot(v, (kk * jnp.exp(b_last - b)).astype(BF16), TN)
            return carry

        lax.fori_loop(0, n_chunk, chunk, 0)

    return pl.pallas_call(
        body, grid=(HG_HEADS, s // HG_TB), in_specs=_hg_specs(lambda n: n),
        out_specs=[pl.BlockSpec((HG_TB, LANE), lambda h, n: (n, h)), pl.BlockSpec((None, n_chunk, HG_K, HG_K), lambda h, n: (h, n, 0, 0))],
        out_shape=[_sds((s, HG_W), F32), _sds((HG_HEADS, s // CHUNK, HG_K, HG_K), BF16)],
        scratch_shapes=[pltpu.VMEM((HG_K, HG_K), F32)],
        compiler_params=_params(2), name="hgrn_fwd")(proj, proj, proj, hl)


def _hgrn_bwd(proj, hl, states, d_o):
    s = proj.shape[0]
    n_chunk = HG_TB // CHUNK
    n_blk = s // HG_TB
    rev = lambda n: n_blk - 1 - n

    def body(q_ref, f_ref, i_ref, hl_ref, st_in_ref, do_ref, dq_ref, df_ref, di_ref, dhl_ref, dst_ref, dlb_ref):
        n = pl.program_id(1)

        @pl.when(n == 0)
        def _():
            dst_ref[...] = jnp.zeros_like(dst_ref)
            dlb_ref[...] = jnp.zeros_like(dlb_ref)

        lb = _lower_bound(hl_ref[...])
        r_i = lax.broadcasted_iota(jnp.int32, (CHUNK, CHUNK), 0)
        c_i = lax.broadcasted_iota(jnp.int32, (CHUNK, CHUNK), 1)
        tri_lower = (r_i >= c_i).astype(BF16)
        tri_upper = (r_i <= c_i).astype(BF16)
        row = lax.broadcasted_iota(jnp.int32, (CHUNK, 1), 0)

        def chunk(cc, carry):
            c = n_chunk - 1 - cc
            rows = pl.ds(pl.multiple_of(c * CHUNK, CHUNK), CHUNK)
            q_raw = q_ref[rows, :]
            vb = i_ref[rows, :].astype(BF16)
            sg, f, kk, sq, qs, b = _hg_gates(q_raw, f_ref[rows, :], lb, tri_lower)
            e_b = jnp.exp(b)
            qe = qs * e_b
            b_last = b[CHUNK - 1:CHUNK, :]
            e_last = jnp.exp(b_last)
            e_kd = jnp.exp(b_last - b)
            kd = kk * e_kd
            st0 = st_in_ref[c]
            d_ob = do_ref[rows, :].astype(BF16)
            dst = dst_ref[...]
            dst_b = dst.astype(BF16)
            d_a = jnp.where(r_i >= c_i, _dot(d_ob, vb, NT), 0.0)
            a, dqs, dkk = _hg_intra(qs, kk, b, d_a)
            d_v = _dot(a.astype(BF16), d_ob, TN) + _dot(kd.astype(BF16), dst_b, NT)
            d_kd = _dot(vb, dst_b, NN)
            dqs = dqs + _dot(d_ob, st0, NN) * e_b
            dkk = dkk + d_kd * e_kd
            d_b_last = jnp.sum(d_kd * kd, axis=0, keepdims=True) + jnp.sum(dst * st0.astype(F32), axis=0, keepdims=True) * e_last
            d_b = qs * dqs - kk * dkk + jnp.where(row == CHUNK - 1, d_b_last, 0.0)
            d_g = _tri_matmul(tri_upper, d_b)
            dst_ref[...] = _dot(d_ob, qe.astype(BF16), TN) + dst * e_last
            d_f = d_g / f - dkk
            dlb_ref[...] += jnp.sum(d_f * (1.0 - sg), axis=0, keepdims=True)
            dq_ref[rows, :] = (dqs * (sq * (1.0 + q_raw * (1.0 - sq)))).astype(dq_ref.dtype)
            df_ref[rows, :] = (d_f * (1.0 - lb) * (sg * (1.0 - sg))).astype(df_ref.dtype)
            di_ref[rows, :] = d_v.astype(di_ref.dtype)
            return carry

        lax.fori_loop(0, n_chunk, chunk, 0)

        @pl.when(n == n_blk - 1)
        def _():
            d_hl0 = dlb_ref[...] * (lb * (1.0 - lb))
            dhl_ref[...] = jnp.concatenate([d_hl0, -d_hl0], axis=0)

    out_blk = pl.BlockSpec((HG_TB, LANE), lambda h, n: (rev(n), h))
    return pl.pallas_call(
        body, grid=(HG_HEADS, n_blk),
        in_specs=_hg_specs(rev) + [pl.BlockSpec((None, n_chunk, HG_K, HG_K), lambda h, n: (h, rev(n), 0, 0)), out_blk],
        out_specs=[out_blk, out_blk, out_blk, pl.BlockSpec((2, LANE), lambda h, n: (0, h))],
        out_shape=[_sds((s, HG_W), BF16)] * 3 + [_sds((2, HG_W), F32)],
        scratch_shapes=[pltpu.VMEM((HG_K, HG_K), F32), pltpu.VMEM((1, LANE), F32)],
        compiler_params=_params(2), name="hgrn_bwd")(proj, proj, proj, hl, states, d_o)


def _mod_part(c_all, w_shard, b_shard):
    n = w_shard.shape[1]
    tn = 512

    def body(c_ref, w_ref, b_ref, o_ref):
        o_ref[...] = _dot(c_ref[...].astype(BF16), w_ref[...].astype(BF16), NN) + b_ref[...]

    return pl.pallas_call(body, grid=(n // tn,),
                          in_specs=[pl.BlockSpec((N_DEV, D), lambda j: (0, 0)), pl.BlockSpec((D, tn), lambda j: (0, j)), pl.BlockSpec((1, tn), lambda j: (0, j))],
                          out_specs=pl.BlockSpec((N_DEV, tn), lambda j: (0, j)), out_shape=_sds((N_DEV, n), F32),
                          compiler_params=_params(1, 32 << 20), name="mod_part")(c_all, w_shard, b_shard)


def _grad_w_ada(c_all_t, dmod_cols):
    n = dmod_cols.shape[1]
    tn = 512

    def body(c_ref, d_ref, o_ref):
        cv = c_ref[...].astype(BF16).astype(F32)
        dv = d_ref[...].astype(BF16).astype(F32)
        acc = cv[:, 0:1] * dv[0:1, :]
        for k in range(1, N_DEV):
            acc = acc + cv[:, k:k + 1] * dv[k:k + 1, :]
        o_ref[...] = acc

    return pl.pallas_call(body, grid=(n // tn,),
                          in_specs=[pl.BlockSpec((D, N_DEV), lambda j: (0, 0)), pl.BlockSpec((N_DEV, tn), lambda j: (0, j))],
                          out_specs=pl.BlockSpec((D, tn), lambda j: (0, j)), out_shape=_sds((D, n), F32),
                          compiler_params=_params(1, 32 << 20), name="grad_w_ada")(c_all_t, dmod_cols)


def _row_tile(r, c):
    if r * c * 4 <= (1 << 20) or r % 8:
        return r
    best = 8
    for t in range(8, r + 1, 8):
        if r % t == 0 and t * c * 4 <= (1 << 20):
            best = t
    return best


def _adamw(pieces, w, m, v, name, emit_grad=True, own=None):
    p, r, c = pieces.shape
    tr = _row_tile(r, c)
    c1 = 1.0 / (1.0 - ADAM_B1 ** ADAM_STEP)
    c2 = 1.0 / (1.0 - ADAM_B2 ** ADAM_STEP)

    def body(*refs):
        if own is None:
            p_ref, w_ref, m_ref, v_ref, *outs = refs
            g = p_ref[0].astype(F32)
        else:
            o_ref, p_ref, w_ref, m_ref, v_ref, *outs = refs
            g = o_ref[...].astype(F32) + p_ref[0].astype(F32)
        for k in range(1, p):
            g = g + p_ref[k].astype(F32)
        m2 = ADAM_B1 * m_ref[...] + (1.0 - ADAM_B1) * g
        v2 = ADAM_B2 * v_ref[...] + (1.0 - ADAM_B2) * (g * g)
        delta = -ADAM_LR * ((m2 * c1) / (jnp.sqrt(v2 * c2) + ADAM_EPS) + ADAM_WD * w_ref[...])
        if emit_grad:
            outs[0][...] = g
        outs[-3][...] = delta
        outs[-2][...] = m2
        outs[-1][...] = v2

    blk = pl.BlockSpec((tr, c), lambda i: (i, 0))
    n_out = 4 if emit_grad else 3
    lead = [] if own is None else [own]
    return pl.pallas_call(body, grid=(r // tr,), in_specs=[blk] * len(lead) + [pl.BlockSpec((p, tr, c), lambda i: (0, i, 0)), blk, blk, blk],
                          out_specs=[blk] * n_out, out_shape=[_sds((r, c), F32)] * n_out,
                          compiler_params=_params(1, 48 << 20), name=name)(*lead, pieces, w, m, v)


def _my_coords():
    return lax.axis_index("x"), lax.axis_index("y"), lax.axis_index("c")


def _flip(coords, k):
    x, y, c = coords
    return (1 - x if k & 4 else x, 1 - y if k & 2 else y, 1 - c if k & 1 else c)


def _lin(coords):
    return 4 * coords[0] + 2 * coords[1] + coords[2]


def _exchange_small(x3, bcast, name):
    n = x3.shape[2]

    def body(x_ref, o_ref, send_sems, recv_sems):
        me = _my_coords()
        my_id = _lin(me)
        o_ref[pl.ds(my_id, 1)] = x_ref[pl.ds(0 if bcast else my_id, 1)]
        copies = []
        for k in range(1, N_DEV):
            peer = _flip(me, k)
            src = x_ref.at[0 if bcast else _lin(peer)]
            cp = pltpu.make_async_remote_copy(src_ref=src, dst_ref=o_ref.at[my_id], send_sem=send_sems.at[k], recv_sem=recv_sems.at[k],
                                              device_id=peer, device_id_type=MESH)
            cp.start()
            copies.append(cp)
        for k in range(1, N_DEV):
            peer = _flip(me, k)
            pltpu.make_async_remote_copy(src_ref=x_ref.at[0], dst_ref=o_ref.at[_lin(peer)], send_sem=send_sems.at[k], recv_sem=recv_sems.at[k],
                                         device_id=peer, device_id_type=MESH).wait_recv()
        for cp in copies:
            cp.wait_send()

    vm = pl.BlockSpec(memory_space=pltpu.VMEM)
    return pl.pallas_call(body, in_specs=[vm], out_specs=vm, out_shape=_sds((N_DEV, 1, n), F32),
                          scratch_shapes=[pltpu.SemaphoreType.DMA((N_DEV,)), pltpu.SemaphoreType.DMA((N_DEV,))], name=name)(x3)


HBM_SPEC = pl.BlockSpec(memory_space=pltpu.HBM)
SEM_SPEC = pl.BlockSpec(memory_space=pltpu.SEMAPHORE)
ANY_SPEC = pl.BlockSpec(memory_space=pl.ANY)
DATAFLOW = pltpu.SideEffectType.DATAFLOW_SIDE_EFFECTING
GATHER_FLIPS = (1, 2, 4, 6)
PASS_FLIPS = (2, 4, 6)
TOKEN = (8, LANE)


def _hbm(t):
    return pltpu.with_memory_space_constraint(t, pltpu.HBM)


def _hbm_like(ts):
    return [pltpu.HBM(t.shape, t.dtype) for t in ts]


def _split_start(body, srcs, lands, n_sem, name):
    n = len(srcs)
    outs = pl.pallas_call(
        body, name=name,
        out_shape=(pltpu.SemaphoreType.DMA((n_sem,)), pltpu.SemaphoreType.DMA((n_sem,)), *_hbm_like(srcs), *_hbm_like(lands), _sds(TOKEN, F32)),
        in_specs=[HBM_SPEC] * (2 * n), out_specs=(SEM_SPEC, SEM_SPEC, *[HBM_SPEC] * (2 * n), pl.BlockSpec(memory_space=pltpu.VMEM)),
        input_output_aliases={i: 2 + i for i in range(2 * n)},
        compiler_params=pltpu.CompilerParams(has_side_effects=DATAFLOW))(*[_hbm(t) for t in srcs], *[_hbm(t) for t in lands])
    return dict(sems=outs[:2], thru=list(outs[2:2 + 2 * n]), token=outs[-1], n=n)


def _split_wait(body, handle, after, name):
    n = handle["n"]
    thru = handle["thru"]
    outs = pl.pallas_call(
        body, name=name, out_shape=_hbm_like(thru), in_specs=[HBM_SPEC] * (2 * n) + [SEM_SPEC, SEM_SPEC, ANY_SPEC],
        out_specs=[HBM_SPEC] * (2 * n), input_output_aliases={i: i for i in range(2 * n)},
        compiler_params=pltpu.CompilerParams(has_side_effects=DATAFLOW))(*thru, *handle["sems"], after)
    return list(outs[:n]), list(outs[n:])


def _gather_start(shards, name):
    n = len(shards)
    my_id = _lin(_my_coords())
    lands = [lax.dynamic_update_slice(lax.empty((N_DEV,) + t.shape, t.dtype), t[None], (my_id, 0, 0)) for t in shards]

    def body(*refs):
        src, land = refs[:n], refs[n:2 * n]
        send_sems, recv_sems = refs[2 * n], refs[2 * n + 1]
        token = refs[-1]
        me = _my_coords()
        for w in range(n):
            for j, k in enumerate(GATHER_FLIPS):
                q = len(GATHER_FLIPS) * w + j
                pltpu.make_async_remote_copy(src_ref=src[w], dst_ref=land[w].at[_lin(me)], send_sem=send_sems.at[q], recv_sem=recv_sems.at[q],
                                             device_id=_flip(me, k), device_id_type=MESH).start()
        token[...] = jnp.zeros_like(token)

    return _split_start(body, shards, lands, len(GATHER_FLIPS) * n, name)


def _gather_wait(handle, after, name):
    n = handle["n"]

    def body(*refs):
        src, land = refs[:n], refs[n:2 * n]
        send_sems, recv_sems = refs[2 * n], refs[2 * n + 1]
        me = _my_coords()
        for w in range(n):
            for j, k in enumerate(GATHER_FLIPS):
                q = len(GATHER_FLIPS) * w + j
                peer = _flip(me, k)
                cp = pltpu.make_async_remote_copy(src_ref=src[w], dst_ref=land[w].at[_lin(peer)], send_sem=send_sems.at[q], recv_sem=recv_sems.at[q],
                                                  device_id=peer, device_id_type=MESH)
                cp.wait_send()
                cp.wait_recv()

    return _split_wait(body, handle, after, name)[1]


def _gather_pass(lands, name):
    n = len(lands)
    n_p = len(PASS_FLIPS)

    def body(*refs):
        land = refs[n:2 * n]
        send_sems, recv_sems = refs[2 * n:]
        me = _my_coords()
        sibling = _flip(me, 1)
        sent = []
        for w in range(n):
            for j, k in enumerate(PASS_FLIPS):
                blk = land[w].at[_lin(_flip(me, k))]
                cp = pltpu.make_async_remote_copy(src_ref=blk, dst_ref=blk, send_sem=send_sems.at[n_p * w + j], recv_sem=recv_sems.at[n_p * w + j],
                                                  device_id=sibling, device_id_type=MESH)
                cp.start()
                sent.append(cp)
        for w in range(n):
            for j, k in enumerate(PASS_FLIPS):
                blk = land[w].at[_lin(_flip(me, k + 1))]
                pltpu.make_async_remote_copy(src_ref=blk, dst_ref=blk, send_sem=send_sems.at[n_p * w + j], recv_sem=recv_sems.at[n_p * w + j],
                                             device_id=sibling, device_id_type=MESH).wait_recv()
        for cp in sent:
            cp.wait_send()

    return pl.pallas_call(body, in_specs=[ANY_SPEC] * n, out_specs=[ANY_SPEC] * n, out_shape=[_sds(t.shape, t.dtype) for t in lands],
                          input_output_aliases={i: i for i in range(n)},
                          scratch_shapes=[pltpu.SemaphoreType.DMA((n_p * n,)), pltpu.SemaphoreType.DMA((n_p * n,))], name=name)(*lands)


def _scatter_start(grads, name):
    n = len(grads)
    lands = [lax.empty((N_DEV - 1,) + g.shape[1:], g.dtype) for g in grads]

    def body(*refs):
        src, land = refs[:n], refs[n:2 * n]
        send_sems, recv_sems = refs[2 * n], refs[2 * n + 1]
        token = refs[-1]
        me = _my_coords()
        for w in range(n):
            for k in range(1, N_DEV):
                q = (N_DEV - 1) * w + k - 1
                peer = _flip(me, k)
                pltpu.make_async_remote_copy(src_ref=src[w].at[_lin(peer)], dst_ref=land[w].at[k - 1], send_sem=send_sems.at[q], recv_sem=recv_sems.at[q],
                                             device_id=peer, device_id_type=MESH).start()
        token[...] = jnp.zeros_like(token)

    return _split_start(body, grads, lands, (N_DEV - 1) * n, name)


def _scatter_wait(handle, after, name):
    n = handle["n"]

    def body(*refs):
        src, land = refs[:n], refs[n:2 * n]
        send_sems, recv_sems = refs[2 * n], refs[2 * n + 1]
        me = _my_coords()
        for w in range(n):
            for k in range(1, N_DEV):
                q = (N_DEV - 1) * w + k - 1
                peer = _flip(me, k)
                cp = pltpu.make_async_remote_copy(src_ref=src[w].at[_lin(peer)], dst_ref=land[w].at[k - 1], send_sem=send_sems.at[q], recv_sem=recv_sems.at[q],
                                                  device_id=peer, device_id_type=MESH)
                cp.wait_send()
                cp.wait_recv()

    return _split_wait(body, handle, after, name)


def _after(t, *tokens):
    for tok in tokens:
        t = t + tok[0:1, 0:1]
    return t


def _rope_tables(positions):
    half = ROT // 2
    inv_freq = ROPE_THETA ** (-jnp.arange(0, ROT, 2, dtype=F32) / ROT)
    ang = positions.astype(F32).reshape(-1, 1) * inv_freq
    cos, sin = jnp.cos(ang), jnp.sin(ang)
    s = ang.shape[0]
    pad = jnp.zeros((s, HEAD_DIM - ROT), F32)
    zero = jnp.zeros((s, half), F32)
    two = lambda t: jnp.concatenate([t, t], axis=1)
    return (two(jnp.concatenate([cos, cos, pad + 1.0], axis=1)), two(jnp.concatenate([-sin, zero, pad], axis=1)),
            two(jnp.concatenate([zero, sin, pad], axis=1)))


def _local_step(x, tgt, tabs, mod, sinks_pad, hl, hg_norm, g_pre_mix, g_post_mix, g_pre_ffn, g_post_ffn, weights, scatter):
    s = x.shape[0]
    h1 = _pre_fwd(x, g_pre_mix, mod, 1, 0, "pre_mix_fwd")
    w_in, w_in_dm = weights("in", h1)
    proj = _mm_nn(h1, w_in, s, 256, D, F32, "proj_mm")
    att = _attn_fwd(proj, tabs, sinks_pad)
    o_raw, states = _hgrn_fwd(proj, hl)
    ohg = _hgout_fwd(o_raw, proj, hg_norm)
    w_attn_dm, w_hgrn_dm, w_out = weights("mix", ohg)
    y_a = _mm_nn_dm(att, w_attn_dm, s, F32, "attn_proj_mm")
    y_h = _mm_nn_dm(ohg, w_hgrn_dm, s, F32, "hgrn_proj_mm")
    merged = _merge_fwd(y_a, y_h, proj)
    y = _mm_nn(merged, w_out, s, 512, D, F32, "out_mm")
    x1 = _post_fwd(x, y, g_post_mix, mod, 2, "post_mix_fwd")
    h2 = _pre_fwd(x1, g_pre_ffn, mod, 4, 3, "pre_ffn_fwd")
    w_ffn_in_dm, w_ffn_out = weights("ffn", h2)
    gu = _mm_nn_dm(h2, w_ffn_in_dm, s // 2, F32, "ffn_in_mm")
    act = _swiglu_fwd(gu)
    y2 = _mm_nn(act, w_ffn_out, s, 512, FFN // 4, F32, "ffn_out_mm")
    err, loss = _post_fwd_loss(x1, y2, g_post_ffn, mod, 5, tgt, "post_ffn_loss")
    dy2, d_gate2, dg_post_ffn = _post_bwd(err, y2, g_post_ffn, mod, 5, "post_ffn_bwd")
    d_act = _mm_nt(dy2, w_ffn_out, s, 512, D, F32, "ffn_out_dx")
    gw_ffn_out = _mm_tn(act, dy2, 512, D, BF16, "ffn_out_dw")
    t_out = scatter([gw_ffn_out.reshape(N_DEV, FFN // N_DEV, D)], "ffn_out")
    dgu = _swiglu_bwd(d_act, gu)
    dh2 = _mm_nt_dm(dgu, w_ffn_in_dm, s, 512, F32, "ffn_in_dx")
    gw_ffn_in = _mm_tn_dm(h2, dgu, 512, BF16, "ffn_in_dw")
    t_in = scatter([gw_ffn_in], "ffn_in")
    mod = _after(mod, t_out, t_in)
    dx1, d_shift2, d_scale2, dg_pre_ffn = _pre_bwd(dh2, x1, err, g_pre_ffn, mod, 4, "pre_ffn_bwd")
    dy, d_gate1, dg_post_mix = _post_bwd(dx1, y, g_post_mix, mod, 2, "post_mix_bwd")
    d_merged = _mm_nt(dy, w_out, s, 512, D, F32, "out_dx")
    gw_out = _mm_tn(merged, dy, 512, D, BF16, "out_dw")
    dy_a, dy_h, d_gate_a, d_gate_h = _merge_bwd(d_merged, y_a, y_h, proj)
    d_att = _mm_nt_dm(dy_a, w_attn_dm, s, 512, F32, "attn_proj_dx")
    gw_attn = _mm_tn_dm(att, dy_a, 512, BF16, "attn_proj_dw")
    d_ohg = _mm_nt_dm(dy_h, w_hgrn_dm, s, 512, F32, "hgrn_proj_dx")
    gw_hgrn = _mm_tn_dm(ohg, dy_h, 512, BF16, "hgrn_proj_dw")
    t_mix = scatter([gw_attn, gw_hgrn, gw_out.reshape(N_DEV, D // N_DEV, D)], "mix")
    d_o, d_gh, d_hg_norm = _hgout_bwd(d_ohg, o_raw, proj, _after(hg_norm, t_mix))
    d_qh, d_fh, d_ih, d_hl = _hgrn_bwd(proj, hl, states, d_o)
    d_qa, d_ka, d_va, d_sinks = _attn_bwd(proj, tabs, sinks_pad, d_att)
    d_proj = jnp.concatenate([d_qa, d_ka.astype(BF16), d_va.astype(BF16), d_qh, d_fh, d_ih, d_gh, d_gate_a, d_gate_h], axis=1)
    d_proj_dm = d_proj.reshape(s, N_DEV, IN_COLS // N_DEV).transpose(1, 0, 2)
    gw_in = _mm_tn_dm(h1, d_proj_dm, 512, BF16, "proj_dw", b_dm=True)
    t_proj = scatter([gw_in], "in")
    dh1 = _mm_nt_dm2(d_proj_dm, w_in_dm, s // 2, 512, F32, "proj_dx")
    grad_x, d_shift1, d_scale1, dg_pre_mix = _pre_bwd(dh1, x, dx1, g_pre_mix, _after(mod, t_proj), 1, "pre_mix_bwd")
    d_mod = jnp.concatenate([d_shift1, d_scale1, d_gate1, d_shift2, d_scale2, d_gate2], axis=1)
    small = [d_mod, dg_pre_mix, dg_post_mix, dg_pre_ffn, dg_post_ffn, d_hl.reshape(1, 2 * HG_W), d_hg_norm, d_sinks]
    return loss, grad_x, small


def kernel(x, c, positions, w_ada, b_ada, g_pre_mix, g_post_mix, g_pre_ffn, g_post_ffn, w_in, attn_sinks, w_attn_proj, hg_lower_bounds, hg_norm, w_hgrn_proj, w_out, w_ffn_in, w_ffn_out, loss_target, m_w_ada, m_b_ada, m_g_pre_mix, m_g_post_mix, m_g_pre_ffn, m_g_post_ffn, m_w_in, m_attn_sinks, m_w_attn_proj, m_hg_lower_bounds, m_hg_norm, m_w_hgrn_proj, m_w_out, m_w_ffn_in, m_w_ffn_out, v_w_ada, v_b_ada, v_g_pre_mix, v_g_post_mix, v_g_pre_ffn, v_g_post_ffn, v_w_in, v_attn_sinks, v_w_attn_proj, v_hg_lower_bounds, v_hg_norm, v_w_hgrn_proj, v_w_out, v_w_ffn_in, v_w_ffn_out):
    my_id = _lin(_my_coords())
    s = x.shape[1]
    n_ada = w_ada.shape[2]

    groups = {"in": [w_in], "mix": [w_attn_proj, w_hgrn_proj, w_out], "ffn": [w_ffn_in, w_ffn_out]}
    gathers = {g: _gather_start([w[0].astype(BF16) for w in ws], "gather_start_" + g) for g, ws in groups.items()}

    def weights(group, after):
        lands = _gather_pass(_gather_wait(gathers[group], after, "gather_wait_" + group), "gather_pass_" + group)
        if group == "in":
            return lands[0].transpose(1, 0, 2).reshape(D, IN_COLS), lands[0]
        if group == "mix":
            return lands[0], lands[1], lands[2].reshape(D, D)
        return lands[0], lands[1].reshape(FFN, D)

    scatters = {}

    def scatter(grads, group):
        scatters[group] = _scatter_start(grads, "scatter_start_" + group)
        return scatters[group]["token"]

    c_dep = _after(c, *[h["token"] for h in gathers.values()])
    c_all = _exchange_small(c_dep.reshape(1, 1, D), True, "gather_c").reshape(N_DEV, D)
    b_cols = lax.dynamic_slice(b_ada, (0, my_id * n_ada), (1, n_ada))
    mod_part = _mod_part(c_all, w_ada[0], b_cols)
    mod = _exchange_small(mod_part.reshape(N_DEV, 1, n_ada), False, "scatter_mod").reshape(1, N_MOD * D)

    sinks_pad = jnp.pad(attn_sinks, ((0, 0), (0, LANE - ATT_HEADS)))
    loss, grad_x, small = _local_step(
        x[0], loss_target[0], _rope_tables(positions), mod, sinks_pad, hg_lower_bounds, hg_norm, g_pre_mix, g_post_mix, g_pre_ffn, g_post_ffn,
        weights, scatter)
    loss = lax.psum(loss[0, 0], ("x", "y", "c"))

    sizes = [t.shape[1] for t in small]
    parts = _exchange_small(jnp.concatenate(small, axis=1).reshape(1, 1, sum(sizes)), True, "gather_small_grads")
    offs = [sum(sizes[:k]) for k in range(len(sizes))]
    piece = lambda k, n=None: parts[:, :, offs[k]:offs[k] + (sizes[k] if n is None else n)]
    small_w = [(piece(0), b_ada, m_b_ada, v_b_ada), (piece(1), g_pre_mix, m_g_pre_mix, v_g_pre_mix),
               (piece(2), g_post_mix, m_g_post_mix, v_g_post_mix), (piece(3), g_pre_ffn, m_g_pre_ffn, v_g_pre_ffn),
               (piece(4), g_post_ffn, m_g_post_ffn, v_g_post_ffn),
               (piece(5).reshape(N_DEV, 2, HG_W), hg_lower_bounds, m_hg_lower_bounds, v_hg_lower_bounds),
               (piece(6), hg_norm, m_hg_norm, v_hg_norm), (piece(7, ATT_HEADS), attn_sinks, m_attn_sinks, v_attn_sinks)]
    names = ["b_ada", "g_pre_mix", "g_post_mix", "g_pre_ffn", "g_post_ffn", "hg_lower_bounds", "hg_norm", "attn_sinks"]
    res = {n: _adamw(p, w, m, v, "adamw_" + n) for n, (p, w, m, v) in zip(names, small_w)}

    dmod_cols = lax.dynamic_slice(parts.reshape(N_DEV, -1), (0, my_id * n_ada), (N_DEV, n_ada))
    g_w_ada = _grad_w_ada(c_all.T, dmod_cols)
    res["w_ada"] = [g_w_ada] + list(_adamw(g_w_ada[None], w_ada[0], m_w_ada[0], v_w_ada[0], "adamw_w_ada", emit_grad=False))

    big = {"ffn_out": [("w_ffn_out", w_ffn_out, m_w_ffn_out, v_w_ffn_out)], "ffn_in": [("w_ffn_in", w_ffn_in, m_w_ffn_in, v_w_ffn_in)],
           "mix": [("w_attn_proj", w_attn_proj, m_w_attn_proj, v_w_attn_proj), ("w_hgrn_proj", w_hgrn_proj, m_w_hgrn_proj, v_w_hgrn_proj),
                   ("w_out", w_out, m_w_out, v_w_out)],
           "in": [("w_in", w_in, m_w_in, v_w_in)]}
    after = res["w_ada"][1]
    for group, members in big.items():
        local, lands = _scatter_wait(scatters[group], after, "scatter_wait_" + group)
        for (n, w, m, v), g_local, land in zip(members, local, lands):
            own = lax.dynamic_index_in_dim(g_local, my_id, 0, keepdims=False)
            res[n] = _adamw(land, w[0], m[0], v[0], "adamw_" + n, own=own)
            after = res[n][1]

    order = ["w_ada", "b_ada", "g_pre_mix", "g_post_mix", "g_pre_ffn", "g_post_ffn", "w_in", "attn_sinks", "w_attn_proj",
             "hg_lower_bounds", "hg_norm", "w_hgrn_proj", "w_out", "w_ffn_in", "w_ffn_out"]
    lead = {"w_ada", "w_in", "w_attn_proj", "w_hgrn_proj", "w_out", "w_ffn_in", "w_ffn_out"}
    outs = [loss, grad_x[None]]
    for k in range(4):
        outs += [res[n][k][None] if n in lead else res[n][k] for n in order]
    return tuple(outs)
```

```python
import functools

import jax
import jax.numpy as jnp
from jax import lax
from jax.experimental import pallas as pl
from jax.experimental.pallas import tpu as pltpu

F32 = jnp.float32
BF16 = jnp.bfloat16

N_DEV = 8
D = 2048
ATT_HEADS = 16
KV_HEADS = 2
HEAD_DIM = 64
GROUP = ATT_HEADS // KV_HEADS
ATT_W = ATT_HEADS * HEAD_DIM
BLK = 128
ROT = HEAD_DIM // 4
ROPE_THETA = 500000.0
HG_HEADS = 8
HG_K = 128
HG_W = HG_HEADS * HG_K
CHUNK = 64
SUB = 16
FFN = 5632
N_MOD = 6
EPS = 1e-6
LANE = 128
Q_A, K_A, V_A, Q_H, F_H, I_H, G_H, GT_A, GT_H, IN_COLS = 0, 1024, 1152, 1280, 2304, 3328, 4352, 5376, 7424, 9472

ADAM_LR, ADAM_B1, ADAM_B2, ADAM_EPS, ADAM_WD, ADAM_STEP = 0.001, 0.9, 0.999, 1e-08, 0.01, 10

TR = 256
HG_TB = 512
VMEM_BIG = 56 << 20
MESH = pl.DeviceIdType.MESH


def _sds(shape, dtype):
    return jax.ShapeDtypeStruct(shape, dtype)


def _params(n_axes, vmem=None):
    return pltpu.CompilerParams(dimension_semantics=("arbitrary",) * n_axes, vmem_limit_bytes=vmem)


def _sig(t):
    return 1.0 / (1.0 + jnp.exp(-t))


def _dot(a, b, dims):
    return lax.dot_general(a, b, (dims, ((), ())), preferred_element_type=F32)


NN = ((1,), (0,))
NT = ((1,), (1,))
TN = ((0,), (0,))


def _matmul(a, b, a_spec, b_spec, o_spec, out_shape, grid, dims, acc_shape, name, deps=()):
    nk = grid[2]
    nd = len(deps)

    def body(a_ref, b_ref, *rest):
        o_ref, scratch = rest[nd], rest[nd + 1:]
        part = _dot(a_ref[...], b_ref[...], dims)
        if nk == 1:
            o_ref[...] = part.astype(o_ref.dtype)
        else:
            acc = scratch[0]
            k = pl.program_id(2)

            @pl.when(k == 0)
            def _():
                acc[...] = part

            @pl.when(k > 0)
            def _():
                acc[...] += part

            @pl.when(k == nk - 1)
            def _():
                o_ref[...] = acc[...].astype(o_ref.dtype)

    return pl.pallas_call(
        body, grid=grid, in_specs=[a_spec, b_spec] + [pl.BlockSpec(memory_space=pl.ANY)] * nd, out_specs=o_spec, out_shape=out_shape,
        scratch_shapes=[pltpu.VMEM(acc_shape, F32)] if nk > 1 else [],
        compiler_params=_params(3, VMEM_BIG), name=name)(a, b, *deps)


def _mm_nn(a, b, tm, tn, tk, out_dtype, name):
    m, k = a.shape
    n = b.shape[1]
    return _matmul(a, b, pl.BlockSpec((tm, tk), lambda j, i, kk: (i, kk)), pl.BlockSpec((tk, tn), lambda j, i, kk: (kk, j)),
                   pl.BlockSpec((tm, tn), lambda j, i, kk: (i, j)), _sds((m, n), out_dtype),
                   (n // tn, m // tm, k // tk), NN, (tm, tn), name)


def _mm_nn_dm(a, b, tm, out_dtype, name):
    m, k = a.shape
    n = b.shape[2]
    return _matmul(a, b, pl.BlockSpec((tm, k), lambda j, i, kk: (i, 0)), pl.BlockSpec((None, k, n), lambda j, i, kk: (j, 0, 0)),
                   pl.BlockSpec((tm, n), lambda j, i, kk: (i, j)), _sds((m, N_DEV * n), out_dtype),
                   (N_DEV, m // tm, 1), NN, (tm, n), name)


def _mm_nt(a, b, tm, tn, tk, out_dtype, name):
    m, k = a.shape
    n = b.shape[0]
    return _matmul(a, b, pl.BlockSpec((tm, tk), lambda j, i, kk: (i, kk)), pl.BlockSpec((tn, tk), lambda j, i, kk: (j, kk)),
                   pl.BlockSpec((tm, tn), lambda j, i, kk: (i, j)), _sds((m, n), out_dtype),
                   (n // tn, m // tm, k // tk), NT, (tm, tn), name)


def _mm_nt_dm(a, b, tm, tn, out_dtype, name):
    m = a.shape[0]
    n_out, n = b.shape[1], b.shape[2]
    return _matmul(a, b, pl.BlockSpec((tm, n), lambda j, i, kk: (i, kk)), pl.BlockSpec((None, tn, n), lambda j, i, kk: (kk, j, 0)),
                   pl.BlockSpec((tm, tn), lambda j, i, kk: (i, j)), _sds((m, n_out), out_dtype),
                   (n_out // tn, m // tm, N_DEV), NT, (tm, tn), name)


def _mm_nt_dm2(a, b, tm, tn, out_dtype, name):
    m = a.shape[1]
    n_out, n = b.shape[1], b.shape[2]
    return _matmul(a, b, pl.BlockSpec((None, tm, n), lambda j, i, kk: (kk, i, 0)), pl.BlockSpec((None, tn, n), lambda j, i, kk: (kk, j, 0)),
                   pl.BlockSpec((tm, tn), lambda j, i, kk: (i, j)), _sds((m, n_out), out_dtype),
                   (n_out // tn, m // tm, N_DEV), NT, (tm, tn), name)


def _mm_tn(a, b, tm, tn, out_dtype, name):
    s, m = a.shape
    n = b.shape[1]
    return _matmul(a, b, pl.BlockSpec((s, tm), lambda j, i, kk: (0, i)), pl.BlockSpec((s, tn), lambda j, i, kk: (0, j)),
                   pl.BlockSpec((tm, tn), lambda j, i, kk: (i, j)), _sds((m, n), out_dtype),
                   (n // tn, m // tm, 1), TN, (tm, tn), name)


def _mm_tn_dm(a, b, tm, out_dtype, name, b_dm=False, deps=()):
    s, m = a.shape
    if b_dm:
        n = b.shape[2]
        b_spec = pl.BlockSpec((None, s, n), lambda j, i, kk: (j, 0, 0))
    else:
        n = b.shape[1] // N_DEV
        b_spec = pl.BlockSpec((s, n), lambda j, i, kk: (0, j))
    return _matmul(a, b, pl.BlockSpec((s, tm), lambda j, i, kk: (0, i)), b_spec,
                   pl.BlockSpec((None, tm, n), lambda j, i, kk: (j, i, 0)), _sds((N_DEV, m, n), out_dtype),
                   (N_DEV, m // tm, 1), TN, (tm, n), name, deps)


def _row_spec():
    return pl.BlockSpec((TR, D), lambda i: (i, 0))


def _vec_spec(k=0):
    return pl.BlockSpec((1, D), lambda i: (0, k))


def _acc_rows(ref, first, val):
    @pl.when(first)
    def _():
        ref[...] = val

    @pl.when(jnp.logical_not(first))
    def _():
        ref[...] += val


def _pre_fwd(x, g, mod, k_scale, k_shift, name):
    s = x.shape[0]

    def body(x_ref, g_ref, sc_ref, sh_ref, h_ref):
        xv = x_ref[...]
        r = lax.rsqrt(jnp.mean(xv * xv, axis=-1, keepdims=True) + EPS)
        n = xv * r * g_ref[...]
        h_ref[...] = (n * (1.0 + sc_ref[...]) + sh_ref[...]).astype(h_ref.dtype)

    return pl.pallas_call(body, grid=(s // TR,), in_specs=[_row_spec(), _vec_spec(), _vec_spec(k_scale), _vec_spec(k_shift)],
                          out_specs=_row_spec(), out_shape=_sds((s, D), BF16), compiler_params=_params(1), name=name)(x, g, mod, mod)


def _post_fwd(x, y, g, mod, k_gate, name):
    s = x.shape[0]

    def body(x_ref, y_ref, g_ref, gt_ref, o_ref):
        yv = y_ref[...]
        r = lax.rsqrt(jnp.mean(yv * yv, axis=-1, keepdims=True) + EPS)
        o_ref[...] = x_ref[...] + gt_ref[...] * (yv * r * g_ref[...])

    return pl.pallas_call(body, grid=(s // TR,), in_specs=[_row_spec(), _row_spec(), _vec_spec(), _vec_spec(k_gate)],
                          out_specs=_row_spec(), out_shape=_sds((s, D), F32), compiler_params=_params(1), name=name)(x, y, g, mod)


def _post_fwd_loss(x, y, g, mod, k_gate, tgt, name):
    s = x.shape[0]

    def body(x_ref, y_ref, g_ref, gt_ref, t_ref, e_ref, loss_ref):
        i = pl.program_id(0)
        yv = y_ref[...]
        r = lax.rsqrt(jnp.mean(yv * yv, axis=-1, keepdims=True) + EPS)
        err = x_ref[...] + gt_ref[...] * (yv * r * g_ref[...]) - t_ref[...]
        e_ref[...] = err * (1.0 / D)
        part = 0.5 * jnp.sum(jnp.mean(err * err, axis=-1, keepdims=True), axis=0, keepdims=True)
        _acc_rows(loss_ref, i == 0, part)

    return pl.pallas_call(body, grid=(s // TR,),
                          in_specs=[_row_spec(), _row_spec(), _vec_spec(), _vec_spec(k_gate), _row_spec()],
                          out_specs=[_row_spec(), pl.BlockSpec((1, 1), lambda i: (0, 0))],
                          out_shape=[_sds((s, D), F32), _sds((1, 1), F32)], compiler_params=_params(1), name=name)(x, y, g, mod, tgt)


def _pre_bwd(dh, x, res, g, mod, k_scale, name):
    s = x.shape[0]

    def body(dh_ref, x_ref, res_ref, g_ref, sc_ref, dx_ref, dsh_ref, dsc_ref, dg_ref):
        first = pl.program_id(0) == 0
        xv, dh_v, gv = x_ref[...], dh_ref[...], g_ref[...]
        r = lax.rsqrt(jnp.mean(xv * xv, axis=-1, keepdims=True) + EPS)
        xh = xv * r
        dn = dh_v * (1.0 + sc_ref[...])
        dgn = dn * gv
        dx_ref[...] = res_ref[...] + r * (dgn - xh * jnp.mean(dgn * xh, axis=-1, keepdims=True))
        _acc_rows(dsh_ref, first, jnp.sum(dh_v, axis=0, keepdims=True))
        _acc_rows(dsc_ref, first, jnp.sum(dh_v * (xh * gv), axis=0, keepdims=True))
        _acc_rows(dg_ref, first, jnp.sum(dn * xh, axis=0, keepdims=True))

    return pl.pallas_call(body, grid=(s // TR,),
                          in_specs=[_row_spec(), _row_spec(), _row_spec(), _vec_spec(), _vec_spec(k_scale)],
                          out_specs=[_row_spec(), _vec_spec(), _vec_spec(), _vec_spec()],
                          out_shape=[_sds((s, D), F32)] + [_sds((1, D), F32)] * 3,
                          compiler_params=_params(1), name=name)(dh, x, res, g, mod)


def _post_bwd(dx, y, g, mod, k_gate, name):
    s = y.shape[0]

    def body(dx_ref, y_ref, g_ref, gt_ref, dy_ref, dgt_ref, dg_ref):
        first = pl.program_id(0) == 0
        yv, dxv, gv = y_ref[...], dx_ref[...], g_ref[...]
        r = lax.rsqrt(jnp.mean(yv * yv, axis=-1, keepdims=True) + EPS)
        yh = yv * r
        dn = dxv * gt_ref[...]
        dgn = dn * gv
        dy_ref[...] = (r * (dgn - yh * jnp.mean(dgn * yh, axis=-1, keepdims=True))).astype(dy_ref.dtype)
        _acc_rows(dgt_ref, first, jnp.sum(dxv * (yh * gv), axis=0, keepdims=True))
        _acc_rows(dg_ref, first, jnp.sum(dn * yh, axis=0, keepdims=True))

    return pl.pallas_call(body, grid=(s // TR,), in_specs=[_row_spec(), _row_spec(), _vec_spec(), _vec_spec(k_gate)],
                          out_specs=[_row_spec(), _vec_spec(), _vec_spec()],
                          out_shape=[_sds((s, D), BF16), _sds((1, D), F32), _sds((1, D), F32)],
                          compiler_params=_params(1), name=name)(dx, y, g, mod)


SW_TN = 1408
SW_TR = 512
TALL = 1024


def _swiglu_fwd(gu):
    s = gu.shape[0]
    nb = FFN // SW_TN

    def body(g_ref, u_ref, a_ref):
        gv = g_ref[...]
        a_ref[...] = (gv * _sig(gv) * u_ref[...]).astype(a_ref.dtype)

    return pl.pallas_call(body, grid=(s // SW_TR, nb),
                          in_specs=[pl.BlockSpec((SW_TR, SW_TN), lambda i, j: (i, j)), pl.BlockSpec((SW_TR, SW_TN), lambda i, j: (i, j + nb))],
                          out_specs=pl.BlockSpec((SW_TR, SW_TN), lambda i, j: (i, j)), out_shape=_sds((s, FFN), BF16),
                          compiler_params=_params(2, 48 << 20), name="swiglu_fwd")(gu, gu)


def _swiglu_bwd(dact, gu):
    s = gu.shape[0]
    nb = FFN // SW_TN

    def body(da_ref, g_ref, u_ref, o_ref):
        half = pl.program_id(2)
        gv, da = g_ref[...], da_ref[...]
        sg = _sig(gv)
        d_gate = da * u_ref[...] * (sg * (1.0 + gv * (1.0 - sg)))
        d_up = da * (gv * sg)
        o_ref[...] = jnp.where(half == 0, d_gate, d_up).astype(o_ref.dtype)

    blk = lambda f: pl.BlockSpec((SW_TR, SW_TN), f)
    return pl.pallas_call(body, grid=(s // SW_TR, nb, 2),
                          in_specs=[blk(lambda i, j, h: (i, j)), blk(lambda i, j, h: (i, j)), blk(lambda i, j, h: (i, j + nb))],
                          out_specs=blk(lambda i, j, h: (i, j + nb * h)), out_shape=_sds((s, 2 * FFN), BF16),
                          compiler_params=_params(3, 48 << 20), name="swiglu_bwd")(dact, gu, gu)


MG_TN = 256


def _merge_fwd(y_a, y_h, proj):
    s = y_a.shape[0]
    tn = MG_TN
    ba, bh = GT_A // tn, GT_H // tn

    def body(ya_ref, yh_ref, ga_ref, gh_ref, m_ref):
        m_ref[...] = (_sig(ga_ref[...]) * ya_ref[...] + _sig(gh_ref[...]) * yh_ref[...]).astype(m_ref.dtype)

    tr = min(s, TALL)
    blk = lambda f: pl.BlockSpec((tr, tn), f)
    return pl.pallas_call(body, grid=(s // tr, D // tn),
                          in_specs=[blk(lambda i, j: (i, j)), blk(lambda i, j: (i, j)), blk(lambda i, j: (i, j + ba)), blk(lambda i, j: (i, j + bh))],
                          out_specs=blk(lambda i, j: (i, j)), out_shape=_sds((s, D), BF16),
                          compiler_params=_params(2), name="merge_fwd")(y_a, y_h, proj, proj)


def _merge_bwd(dm, y_a, y_h, proj):
    s = y_a.shape[0]
    tn = MG_TN
    ba, bh = GT_A // tn, GT_H // tn

    def body(dm_ref, ya_ref, yh_ref, ga_ref, gh_ref, dya_ref, dyh_ref, dga_ref, dgh_ref):
        dmv = dm_ref[...]
        sa, sh = _sig(ga_ref[...]), _sig(gh_ref[...])
        dya_ref[...] = (dmv * sa).astype(BF16)
        dyh_ref[...] = (dmv * sh).astype(BF16)
        dga_ref[...] = (dmv * ya_ref[...] * (sa * (1.0 - sa))).astype(BF16)
        dgh_ref[...] = (dmv * yh_ref[...] * (sh * (1.0 - sh))).astype(BF16)

    tr = min(s, TALL)
    blk = lambda f: pl.BlockSpec((tr, tn), f)
    nat = blk(lambda i, j: (i, j))
    return pl.pallas_call(body, grid=(s // tr, D // tn),
                          in_specs=[nat, nat, nat, blk(lambda i, j: (i, j + ba)), blk(lambda i, j: (i, j + bh))],
                          out_specs=[nat] * 4, out_shape=[_sds((s, D), BF16)] * 4,
                          compiler_params=_params(2), name="merge_bwd")(dm, y_a, y_h, proj, proj)


def _hgout_fwd(o_raw, proj, hg_norm):
    s = o_raw.shape[0]
    bg = G_H // LANE

    def body(o_ref, g_ref, n_ref, out_ref):
        ov = o_ref[...]
        r = lax.rsqrt(jnp.mean(ov * ov, axis=-1, keepdims=True) + EPS)
        out_ref[...] = (ov * r * n_ref[...] * _sig(g_ref[...])).astype(out_ref.dtype)

    tr = min(s, TALL)
    blk = lambda f: pl.BlockSpec((tr, LANE), f)
    return pl.pallas_call(body, grid=(s // tr, HG_HEADS),
                          in_specs=[blk(lambda i, h: (i, h)), blk(lambda i, h: (i, h + bg)), pl.BlockSpec((1, LANE), lambda i, h: (0, 0))],
                          out_specs=blk(lambda i, h: (i, h)), out_shape=_sds((s, HG_W), BF16),
                          compiler_params=_params(2), name="hgout_fwd")(o_raw, proj, hg_norm)


def _hgout_bwd(d_out, o_raw, proj, hg_norm):
    s = o_raw.shape[0]
    bg = G_H // LANE

    def body(d_ref, o_ref, g_ref, n_ref, do_ref, dg_ref, dn_ref):
        first = jnp.logical_and(pl.program_id(0) == 0, pl.program_id(1) == 0)
        ov, dv, nv = o_ref[...], d_ref[...], n_ref[...]
        sg = _sig(g_ref[...])
        r = lax.rsqrt(jnp.mean(ov * ov, axis=-1, keepdims=True) + EPS)
        oh = ov * r
        d_on = dv * sg
        dg_ref[...] = (dv * (oh * nv) * (sg * (1.0 - sg))).astype(dg_ref.dtype)
        t = d_on * nv
        do_ref[...] = r * (t - oh * jnp.mean(t * oh, axis=-1, keepdims=True))
        _acc_rows(dn_ref, first, jnp.sum(d_on * oh, axis=0, keepdims=True))

    tr = min(s, TALL)
    blk = lambda f: pl.BlockSpec((tr, LANE), f)
    vec = pl.BlockSpec((1, LANE), lambda i, h: (0, 0))
    return pl.pallas_call(body, grid=(s // tr, HG_HEADS),
                          in_specs=[blk(lambda i, h: (i, h)), blk(lambda i, h: (i, h)), blk(lambda i, h: (i, h + bg)), vec],
                          out_specs=[blk(lambda i, h: (i, h)), blk(lambda i, h: (i, h)), vec],
                          out_shape=[_sds((s, HG_W), F32), _sds((s, HG_W), BF16), _sds((1, LANE), F32)],
                          compiler_params=_params(2), name="hgout_bwd")(d_out, o_raw, proj, hg_norm)


def _rope(t, cos, s_lo, s_hi):
    return t * cos + pltpu.roll(t, LANE - ROT // 2, 1) * s_lo + pltpu.roll(t, ROT // 2, 1) * s_hi


def _rope_wide(t, cos, s_lo, s_hi):
    return jnp.concatenate([_rope(t[:, k * LANE:(k + 1) * LANE], cos, s_lo, s_hi) for k in range(t.shape[1] // LANE)], axis=1)


def _attn_mask(has_prev):
    qi = lax.broadcasted_iota(jnp.int32, (BLK, 2 * BLK), 0)
    kj = lax.broadcasted_iota(jnp.int32, (BLK, 2 * BLK), 1)
    rel = BLK + qi - kj
    band = jnp.logical_and(rel >= 0, rel < BLK)
    return jnp.logical_and(band, jnp.logical_or(has_prev, kj >= BLK))


def _attn_specs():
    prev = lambda i: jnp.maximum(i - 1, 0)
    kb, vb = K_A // LANE, V_A // LANE
    blk = lambda f: pl.BlockSpec((BLK, LANE), f)
    tabs = [blk(lambda i: (i, 0))] * 3 + [blk(lambda i: (prev(i), 0))] * 3
    return [pl.BlockSpec((BLK, ATT_W), lambda i: (i, 0)), blk(lambda i: (i, kb)), blk(lambda i: (prev(i), kb)),
            blk(lambda i: (i, vb)), blk(lambda i: (prev(i), vb))] + tabs + [pl.BlockSpec((1, LANE), lambda i: (0, 0))]


def _attn_probs(qh, kg, mask, sk):
    logits = _dot(qh, kg, NT) * (HEAD_DIM ** -0.5)
    logits = jnp.where(mask, logits, -jnp.inf)
    m = jnp.maximum(jnp.max(logits, axis=-1, keepdims=True), sk)
    p = jnp.exp(logits - m)
    e_sink = jnp.exp(sk - m)
    inv = 1.0 / (jnp.sum(p, axis=-1, keepdims=True) + e_sink)
    return p * inv, e_sink * inv


def _attn_fwd(proj, tabs, sinks):
    s = proj.shape[0]

    def body(q_ref, kc_ref, kp_ref, vc_ref, vp_ref, c0, l0, h0, c1, l1, h1, sk_ref, o_ref):
        i = pl.program_id(0)
        mask = _attn_mask(i > 0)
        q = _rope_wide(q_ref[...], c0[...], l0[...], h0[...]).astype(BF16)
        kk = jnp.concatenate([_rope(kp_ref[...], c1[...], l1[...], h1[...]), _rope(kc_ref[...], c0[...], l0[...], h0[...])], axis=0).astype(BF16)
        vv = jnp.concatenate([vp_ref[...], vc_ref[...]], axis=0).astype(BF16)
        outs = []
        for h in range(ATT_HEADS):
            g = h // GROUP
            prob, _ = _attn_probs(q[:, h * HEAD_DIM:(h + 1) * HEAD_DIM], kk[:, g * HEAD_DIM:(g + 1) * HEAD_DIM], mask, sk_ref[:, h:h + 1])
            outs.append(_dot(prob.astype(BF16), vv[:, g * HEAD_DIM:(g + 1) * HEAD_DIM], NN))
        o_ref[...] = jnp.concatenate(outs, axis=1).astype(o_ref.dtype)

    return pl.pallas_call(body, grid=(s // BLK,), in_specs=_attn_specs(),
                          out_specs=pl.BlockSpec((BLK, ATT_W), lambda i: (i, 0)), out_shape=_sds((s, ATT_W), BF16),
                          compiler_params=_params(1), name="attn_fwd")(proj, proj, proj, proj, proj, *tabs, *tabs, sinks)


def _attn_bwd(proj, tabs, sinks, d_att):
    s = proj.shape[0]

    def body(q_ref, kc_ref, kp_ref, vc_ref, vp_ref, c0, l0, h0, c1, l1, h1, sk_ref, do_ref, dq_ref, dk_ref, dv_ref, ds_ref):
        i = pl.program_id(0)

        @pl.when(i == 0)
        def _():
            dk_ref[...] = jnp.zeros_like(dk_ref)
            dv_ref[...] = jnp.zeros_like(dv_ref)
            ds_ref[...] = jnp.zeros_like(ds_ref)

        mask = _attn_mask(i > 0)
        q = _rope_wide(q_ref[...], c0[...], l0[...], h0[...]).astype(BF16)
        kk = jnp.concatenate([_rope(kp_ref[...], c1[...], l1[...], h1[...]), _rope(kc_ref[...], c0[...], l0[...], h0[...])], axis=0).astype(BF16)
        vv = jnp.concatenate([vp_ref[...], vc_ref[...]], axis=0).astype(BF16)
        d_o = do_ref[...].astype(BF16)
        lane = lax.broadcasted_iota(jnp.int32, (1, LANE), 1)
        dqs, dks, dvs = [], [], []
        d_sink = jnp.zeros((1, LANE), F32)
        for g in range(KV_HEADS):
            kg, vg = kk[:, g * HEAD_DIM:(g + 1) * HEAD_DIM], vv[:, g * HEAD_DIM:(g + 1) * HEAD_DIM]
            dkg = jnp.zeros((2 * BLK, HEAD_DIM), F32)
            dvg = jnp.zeros((2 * BLK, HEAD_DIM), F32)
            for j in range(GROUP):
                h = g * GROUP + j
                qh, doh = q[:, h * HEAD_DIM:(h + 1) * HEAD_DIM], d_o[:, h * HEAD_DIM:(h + 1) * HEAD_DIM]
                prob, p_sink = _attn_probs(qh, kg, mask, sk_ref[:, h:h + 1])
                d_p = _dot(doh, vg, NT)
                dd = jnp.sum(prob * d_p, axis=-1, keepdims=True)
                d_s = (prob * (d_p - dd)).astype(BF16)
                d_sink = d_sink + jnp.where(lane == h, -jnp.sum(p_sink * dd, axis=0, keepdims=True), 0.0)
                dqs.append(_dot(d_s, kg, NN) * (HEAD_DIM ** -0.5))
                dkg = dkg + _dot(d_s, qh, TN) * (HEAD_DIM ** -0.5)
                dvg = dvg + _dot(prob.astype(BF16), doh, TN)
            dks.append(dkg)
            dvs.append(dvg)
        dq_ref[...] = _rope_wide(jnp.concatenate(dqs, axis=1), c0[...], -l0[...], -h0[...]).astype(dq_ref.dtype)
        d_k = jnp.concatenate(dks, axis=1)
        d_v = jnp.concatenate(dvs, axis=1)
        cur = pl.ds(pl.multiple_of(i * BLK, BLK), BLK)
        prv = pl.ds(pl.multiple_of(jnp.maximum(i - 1, 0) * BLK, BLK), BLK)
        dk_ref[prv, :] += _rope(d_k[:BLK], c1[...], -l1[...], -h1[...])
        dk_ref[cur, :] += _rope(d_k[BLK:], c0[...], -l0[...], -h0[...])
        dv_ref[prv, :] += d_v[:BLK]
        dv_ref[cur, :] += d_v[BLK:]
        ds_ref[...] += d_sink

    full = pl.BlockSpec((s, LANE), lambda i: (0, 0))
    return pl.pallas_call(body, grid=(s // BLK,), in_specs=_attn_specs() + [pl.BlockSpec((BLK, ATT_W), lambda i: (i, 0))],
                          out_specs=[pl.BlockSpec((BLK, ATT_W), lambda i: (i, 0)), full, full, pl.BlockSpec((1, LANE), lambda i: (0, 0))],
                          out_shape=[_sds((s, ATT_W), BF16), _sds((s, LANE), F32), _sds((s, LANE), F32), _sds((1, LANE), F32)],
                          compiler_params=_params(1), name="attn_bwd")(proj, proj, proj, proj, proj, *tabs, *tabs, sinks, d_att)


def _tri_matmul(tri, t):
    hi = t.astype(BF16)
    r1 = t - hi.astype(F32)
    mid = r1.astype(BF16)
    lo = (r1 - mid.astype(F32)).astype(BF16)
    return _dot(tri, hi, NN) + _dot(tri, mid, NN) + _dot(tri, lo, NN)


def _lower_bound(hl):
    a, b = hl[0:1, :], hl[1:2, :]
    mx = jnp.maximum(a, b)
    ea, eb = jnp.exp(a - mx), jnp.exp(b - mx)
    return ea / (ea + eb)


def _hg_gates(q_raw, f_raw, lb, tri_lower):
    sg = _sig(f_raw)
    f = lb + (1.0 - lb) * sg
    sq = _sig(q_raw)
    b = _tri_matmul(tri_lower, jnp.log(f))
    return sg, f, 1.0 - f, sq, q_raw * sq, b


def _hg_specs(n_map):
    blk = lambda off: pl.BlockSpec((HG_TB, LANE), lambda h, n: (n_map(n), off // LANE + h))
    return [blk(Q_H), blk(F_H), blk(I_H), pl.BlockSpec((2, LANE), lambda h, n: (0, h))]


def _hg_intra(qs, kk, b, d_a=None):
    lane = lax.broadcasted_iota(jnp.int32, (SUB, CHUNK), 1)
    row1 = lax.broadcasted_iota(jnp.int32, (SUB, 1), 0)
    rowk = lax.broadcasted_iota(jnp.int32, (SUB, LANE), 0)
    grad = d_a is not None
    a_blocks, dq_blocks, dk_blocks = [], [], []
    dk_left = None
    for j in range(CHUNK // SUB):
        lo = j * SUB
        q_j, k_j, b_j = qs[lo:lo + SUB], kk[lo:lo + SUB], b[lo:lo + SUB]
        a_j = jnp.zeros((SUB, CHUNK), F32)
        if grad:
            da_j = d_a[lo:lo + SUB]
            dq_j = jnp.zeros((SUB, LANE), F32)
            dk_j = jnp.zeros((SUB, LANE), F32)
        for sx in range(SUB):
            e = jnp.where(row1 >= sx, jnp.exp(jnp.minimum(b_j - b_j[sx:sx + 1], 0.0)), 0.0)
            pe = q_j * e
            a_j = jnp.where(lane == lo + sx, jnp.sum(pe * k_j[sx:sx + 1], axis=-1, keepdims=True), a_j)
            if grad:
                g_col = da_j[:, lo + sx:lo + sx + 1]
                dq_j = dq_j + g_col * (e * k_j[sx:sx + 1])
                dk_j = jnp.where(rowk == sx, jnp.sum(g_col * pe, axis=0, keepdims=True), dk_j)
        if j > 0:
            ref = b[lo - 1:lo]
            sc_q = jnp.exp(b_j - ref)
            sc_k = jnp.exp(jnp.minimum(ref - b, 0.0))
            qt = (q_j * sc_q).astype(BF16)
            kt = (kk * sc_k).astype(BF16)
            a_j = a_j + jnp.where(lane < lo, _dot(qt, kt, NT), 0.0)
            if grad:
                da_left = jnp.where(lane < lo, da_j, 0.0).astype(BF16)
                dq_j = dq_j + _dot(da_left, kt, NN) * sc_q
                t = _dot(da_left, qt, TN) * sc_k
                dk_left = t if dk_left is None else dk_left + t
        a_blocks.append(a_j)
        if grad:
            dq_blocks.append(dq_j)
            dk_blocks.append(dk_j)
    a = jnp.concatenate(a_blocks, axis=0)
    if not grad:
        return a
    return a, jnp.concatenate(dq_blocks, axis=0), jnp.concatenate(dk_blocks, axis=0) + dk_left


def _hgrn_fwd(proj, hl):
    s = proj.shape[0]
    n_chunk = HG_TB // CHUNK

    def body(q_ref, f_ref, i_ref, hl_ref, o_ref, st_out_ref, st_ref):
        @pl.when(pl.program_id(1) == 0)
        def _():
            st_ref[...] = jnp.zeros_like(st_ref)

        lb = _lower_bound(hl_ref[...])
        r_i = lax.broadcasted_iota(jnp.int32, (CHUNK, CHUNK), 0)
        c_i = lax.broadcasted_iota(jnp.int32, (CHUNK, CHUNK), 1)
        tri_lower = (r_i >= c_i).astype(BF16)

        def chunk(c, carry):
            rows = pl.ds(pl.multiple_of(c * CHUNK, CHUNK), CHUNK)
            v = i_ref[rows, :].astype(BF16)
            _, _, kk, _, qs, b = _hg_gates(q_ref[rows, :], f_ref[rows, :], lb, tri_lower)
            a = _hg_intra(qs, kk, b)
            st = st_ref[...]
            st_b = st.astype(BF16)
            st_out_ref[c] = st_b
            o_ref[rows, :] = _dot((qs * jnp.exp(b)).astype(BF16), st_b, NT) + _dot(a.astype(BF16), v, NN)
            b_last = b[CHUNK - 1:CHUNK, :]
            st_ref[...] = st * jnp.exp(b_last) + _dot(v, (kk * jnp.exp(b_last - b)).astype(BF16), TN)
            return carry

        lax.fori_loop(0, n_chunk, chunk, 0)

    return pl.pallas_call(
        body, grid=(HG_HEADS, s // HG_TB), in_specs=_hg_specs(lambda n: n),
        out_specs=[pl.BlockSpec((HG_TB, LANE), lambda h, n: (n, h)), pl.BlockSpec((None, n_chunk, HG_K, HG_K), lambda h, n: (h, n, 0, 0))],
        out_shape=[_sds((s, HG_W), F32), _sds((HG_HEADS, s // CHUNK, HG_K, HG_K), BF16)],
        scratch_shapes=[pltpu.VMEM((HG_K, HG_K), F32)],
        compiler_params=_params(2), name="hgrn_fwd")(proj, proj, proj, hl)


def _hgrn_bwd(proj, hl, states, d_o):
    s = proj.shape[0]
    n_chunk = HG_TB // CHUNK
    n_blk = s // HG_TB
    rev = lambda n: n_blk - 1 - n

    def body(q_ref, f_ref, i_ref, hl_ref, st_in_ref, do_ref, dq_ref, df_ref, di_ref, dhl_ref, dst_ref, dlb_ref):
        n = pl.program_id(1)

        @pl.when(n == 0)
        def _():
            dst_ref[...] = jnp.zeros_like(dst_ref)
            dlb_ref[...] = jnp.zeros_like(dlb_ref)

        lb = _lower_bound(hl_ref[...])
        r_i = lax.broadcasted_iota(jnp.int32, (CHUNK, CHUNK), 0)
        c_i = lax.broadcasted_iota(jnp.int32, (CHUNK, CHUNK), 1)
        tri_lower = (r_i >= c_i).astype(BF16)
        tri_upper = (r_i <= c_i).astype(BF16)
        row = lax.broadcasted_iota(jnp.int32, (CHUNK, 1), 0)

        def chunk(cc, carry):
            c = n_chunk - 1 - cc
            rows = pl.ds(pl.multiple_of(c * CHUNK, CHUNK), CHUNK)
            q_raw = q_ref[rows, :]
            vb = i_ref[rows, :].astype(BF16)
            sg, f, kk, sq, qs, b = _hg_gates(q_raw, f_ref[rows, :], lb, tri_lower)
            e_b = jnp.exp(b)
            qe = qs * e_b
            b_last = b[CHUNK - 1:CHUNK, :]
            e_last = jnp.exp(b_last)
            e_kd = jnp.exp(b_last - b)
            kd = kk * e_kd
            st0 = st_in_ref[c]
            d_ob = do_ref[rows, :].astype(BF16)
            dst = dst_ref[...]
            dst_b = dst.astype(BF16)
            d_a = jnp.where(r_i >= c_i, _dot(d_ob, vb, NT), 0.0)
            a, dqs, dkk = _hg_intra(qs, kk, b, d_a)
            d_v = _dot(a.astype(BF16), d_ob, TN) + _dot(kd.astype(BF16), dst_b, NT)
            d_kd = _dot(vb, dst_b, NN)
            dqs = dqs + _dot(d_ob, st0, NN) * e_b
            dkk = dkk + d_kd * e_kd
            d_b_last = jnp.sum(d_kd * kd, axis=0, keepdims=True) + jnp.sum(dst * st0.astype(F32), axis=0, keepdims=True) * e_last
            d_b = qs * dqs - kk * dkk + jnp.where(row == CHUNK - 1, d_b_last, 0.0)
            d_g = _tri_matmul(tri_upper, d_b)
            dst_ref[...] = _dot(d_ob, qe.astype(BF16), TN) + dst * e_last
            d_f = d_g / f - dkk
            dlb_ref[...] += jnp.sum(d_f * (1.0 - sg), axis=0, keepdims=True)
            dq_ref[rows, :] = (dqs * (sq * (1.0 + q_raw * (1.0 - sq)))).astype(dq_ref.dtype)
            df_ref[rows, :] = (d_f * (1.0 - lb) * (sg * (1.0 - sg))).astype(df_ref.dtype)
            di_ref[rows, :] = d_v.astype(di_ref.dtype)
            return carry

        lax.fori_loop(0, n_chunk, chunk, 0)

        @pl.when(n == n_blk - 1)
        def _():
            d_hl0 = dlb_ref[...] * (lb * (1.0 - lb))
            dhl_ref[...] = jnp.concatenate([d_hl0, -d_hl0], axis=0)

    out_blk = pl.BlockSpec((HG_TB, LANE), lambda h, n: (rev(n), h))
    return pl.pallas_call(
        body, grid=(HG_HEADS, n_blk),
        in_specs=_hg_specs(rev) + [pl.BlockSpec((None, n_chunk, HG_K, HG_K), lambda h, n: (h, rev(n), 0, 0)), out_blk],
        out_specs=[out_blk, out_blk, out_blk, pl.BlockSpec((2, LANE), lambda h, n: (0, h))],
        out_shape=[_sds((s, HG_W), BF16)] * 3 + [_sds((2, HG_W), F32)],
        scratch_shapes=[pltpu.VMEM((HG_K, HG_K), F32), pltpu.VMEM((1, LANE), F32)],
        compiler_params=_params(2), name="hgrn_bwd")(proj, proj, proj, hl, states, d_o)


def _mod_part(c_all, w_shard, b_shard):
    n = w_shard.shape[1]
    tn = 512

    def body(c_ref, w_ref, b_ref, o_ref):
        o_ref[...] = _dot(c_ref[...].astype(BF16), w_ref[...].astype(BF16), NN) + b_ref[...]

    return pl.pallas_call(body, grid=(n // tn,),
                          in_specs=[pl.BlockSpec((N_DEV, D), lambda j: (0, 0)), pl.BlockSpec((D, tn), lambda j: (0, j)), pl.BlockSpec((1, tn), lambda j: (0, j))],
                          out_specs=pl.BlockSpec((N_DEV, tn), lambda j: (0, j)), out_shape=_sds((N_DEV, n), F32),
                          compiler_params=_params(1, 32 << 20), name="mod_part")(c_all, w_shard, b_shard)


def _grad_w_ada(c_all_t, dmod_cols):
    n = dmod_cols.shape[1]
    tn = 512

    def body(c_ref, d_ref, o_ref):
        cv = c_ref[...].astype(BF16).astype(F32)
        dv = d_ref[...].astype(BF16).astype(F32)
        acc = cv[:, 0:1] * dv[0:1, :]
        for k in range(1, N_DEV):
            acc = acc + cv[:, k:k + 1] * dv[k:k + 1, :]
        o_ref[...] = acc

    return pl.pallas_call(body, grid=(n // tn,),
                          in_specs=[pl.BlockSpec((D, N_DEV), lambda j: (0, 0)), pl.BlockSpec((N_DEV, tn), lambda j: (0, j))],
                          out_specs=pl.BlockSpec((D, tn), lambda j: (0, j)), out_shape=_sds((D, n), F32),
                          compiler_params=_params(1, 32 << 20), name="grad_w_ada")(c_all_t, dmod_cols)


def _row_tile(r, c):
    if r * c * 4 <= (1 << 20) or r % 8:
        return r
    best = 8
    for t in range(8, r + 1, 8):
        if r % t == 0 and t * c * 4 <= (1 << 20):
            best = t
    return best


def _adamw(pieces, w, m, v, name, emit_grad=True, own=None):
    p, r, c = pieces.shape
    tr = _row_tile(r, c)
    c1 = 1.0 / (1.0 - ADAM_B1 ** ADAM_STEP)
    c2 = 1.0 / (1.0 - ADAM_B2 ** ADAM_STEP)

    def body(*refs):
        if own is None:
            p_ref, w_ref, m_ref, v_ref, *outs = refs
            g = p_ref[0].astype(F32)
        else:
            o_ref, p_ref, w_ref, m_ref, v_ref, *outs = refs
            g = o_ref[...].astype(F32) + p_ref[0].astype(F32)
        for k in range(1, p):
            g = g + p_ref[k].astype(F32)
        m2 = ADAM_B1 * m_ref[...] + (1.0 - ADAM_B1) * g
        v2 = ADAM_B2 * v_ref[...] + (1.0 - ADAM_B2) * (g * g)
        delta = -ADAM_LR * ((m2 * c1) / (jnp.sqrt(v2 * c2) + ADAM_EPS) + ADAM_WD * w_ref[...])
        if emit_grad:
            outs[0][...] = g
        outs[-3][...] = delta
        outs[-2][...] = m2
        outs[-1][...] = v2

    blk = pl.BlockSpec((tr, c), lambda i: (i, 0))
    n_out = 4 if emit_grad else 3
    lead = [] if own is None else [own]
    return pl.pallas_call(body, grid=(r // tr,), in_specs=[blk] * len(lead) + [pl.BlockSpec((p, tr, c), lambda i: (0, i, 0)), blk, blk, blk],
                          out_specs=[blk] * n_out, out_shape=[_sds((r, c), F32)] * n_out,
                          compiler_params=_params(1, 48 << 20), name=name)(*lead, pieces, w, m, v)


def _my_coords():
    return lax.axis_index("x"), lax.axis_index("y"), lax.axis_index("c")


def _flip(coords, k):
    x, y, c = coords
    return (1 - x if k & 4 else x, 1 - y if k & 2 else y, 1 - c if k & 1 else c)


def _lin(coords):
    return 4 * coords[0] + 2 * coords[1] + coords[2]


def _exchange_small(x3, bcast, name):
    n = x3.shape[2]

    def body(x_ref, o_ref, send_sems, recv_sems):
        me = _my_coords()
        my_id = _lin(me)
        o_ref[pl.ds(my_id, 1)] = x_ref[pl.ds(0 if bcast else my_id, 1)]
        copies = []
        for k in range(1, N_DEV):
            peer = _flip(me, k)
            src = x_ref.at[0 if bcast else _lin(peer)]
            cp = pltpu.make_async_remote_copy(src_ref=src, dst_ref=o_ref.at[my_id], send_sem=send_sems.at[k], recv_sem=recv_sems.at[k],
                                              device_id=peer, device_id_type=MESH)
            cp.start()
            copies.append(cp)
        for k in range(1, N_DEV):
            peer = _flip(me, k)
            pltpu.make_async_remote_copy(src_ref=x_ref.at[0], dst_ref=o_ref.at[_lin(peer)], send_sem=send_sems.at[k], recv_sem=recv_sems.at[k],
                                         device_id=peer, device_id_type=MESH).wait_recv()
        for cp in copies:
            cp.wait_send()

    vm = pl.BlockSpec(memory_space=pltpu.VMEM)
    return pl.pallas_call(body, in_specs=[vm], out_specs=vm, out_shape=_sds((N_DEV, 1, n), F32),
                          scratch_shapes=[pltpu.SemaphoreType.DMA((N_DEV,)), pltpu.SemaphoreType.DMA((N_DEV,))], name=name)(x3)


HBM_SPEC = pl.BlockSpec(memory_space=pltpu.HBM)
SEM_SPEC = pl.BlockSpec(memory_space=pltpu.SEMAPHORE)
ANY_SPEC = pl.BlockSpec(memory_space=pl.ANY)
DATAFLOW = pltpu.SideEffectType.DATAFLOW_SIDE_EFFECTING
GATHER_FLIPS = (1, 2, 4, 6)
PASS_FLIPS = (2, 4, 6)
TOKEN = (8, LANE)


def _hbm(t):
    return pltpu.with_memory_space_constraint(t, pltpu.HBM)


def _hbm_like(ts):
    return [pltpu.HBM(t.shape, t.dtype) for t in ts]


def _split_start(issue, srcs, lands, n_sem, name, deps=()):
    n, nd = len(srcs), len(deps)

    def body(*refs):
        issue(refs[:n], refs[n:2 * n], refs[2 * n + nd], refs[2 * n + nd + 1])
        refs[-1][...] = jnp.zeros(TOKEN, F32)

    outs = pl.pallas_call(
        body, name=name,
        out_shape=(pltpu.SemaphoreType.DMA((n_sem,)), pltpu.SemaphoreType.DMA((n_sem,)), *_hbm_like(srcs), *_hbm_like(lands), _sds(TOKEN, F32)),
        in_specs=[HBM_SPEC] * (2 * n) + [ANY_SPEC] * nd,
        out_specs=(SEM_SPEC, SEM_SPEC, *[HBM_SPEC] * (2 * n), pl.BlockSpec(memory_space=pltpu.VMEM)),
        input_output_aliases={i: 2 + i for i in range(2 * n)},
        compiler_params=pltpu.CompilerParams(has_side_effects=DATAFLOW))(*[_hbm(t) for t in srcs], *[_hbm(t) for t in lands], *deps)
    return dict(sems=outs[:2], thru=list(outs[2:2 + 2 * n]), token=outs[-1], n=n)


def _split_wait(finish, handle, after, name):
    n = handle["n"]
    thru = handle["thru"]

    def body(*refs):
        finish(refs[:n], refs[n:2 * n], refs[2 * n], refs[2 * n + 1])

    outs = pl.pallas_call(
        body, name=name, out_shape=_hbm_like(thru), in_specs=[HBM_SPEC] * (2 * n) + [SEM_SPEC, SEM_SPEC] + [ANY_SPEC] * len(after),
        out_specs=[HBM_SPEC] * (2 * n), input_output_aliases={i: i for i in range(2 * n)},
        compiler_params=pltpu.CompilerParams(has_side_effects=DATAFLOW))(*thru, *handle["sems"], *after)
    return list(outs[:n]), list(outs[n:])


def _gather_start(shards, name, deps=()):
    n = len(shards)
    my_id = _lin(_my_coords())
    lands = [lax.dynamic_update_slice(lax.empty((N_DEV,) + t.shape, t.dtype), t[None], (my_id, 0, 0)) for t in shards]

    def issue(src, land, send_sems, recv_sems):
        me = _my_coords()
        for w in range(n):
            for j, k in enumerate(GATHER_FLIPS):
                q = len(GATHER_FLIPS) * w + j
                pltpu.make_async_remote_copy(src_ref=src[w], dst_ref=land[w].at[_lin(me)], send_sem=send_sems.at[q], recv_sem=recv_sems.at[q],
                                             device_id=_flip(me, k), device_id_type=MESH).start()

    return _split_start(issue, shards, lands, len(GATHER_FLIPS) * n, name, deps)


def _gather_wait(handle, after, name):
    n = handle["n"]

    def finish(src, land, send_sems, recv_sems):
        me = _my_coords()
        for w in range(n):
            for j, k in enumerate(GATHER_FLIPS):
                q = len(GATHER_FLIPS) * w + j
                peer = _flip(me, k)
                cp = pltpu.make_async_remote_copy(src_ref=src[w], dst_ref=land[w].at[_lin(peer)], send_sem=send_sems.at[q], recv_sem=recv_sems.at[q],
                                                  device_id=peer, device_id_type=MESH)
                cp.wait_send()
                cp.wait_recv()

    return _split_wait(finish, handle, after, name)[1]


def _gather_pass(lands, name):
    n = len(lands)
    n_p = len(PASS_FLIPS)

    def body(*refs):
        land = refs[n:2 * n]
        send_sems, recv_sems = refs[2 * n:]
        me = _my_coords()
        sibling = _flip(me, 1)
        sent = []
        for w in range(n):
            for j, k in enumerate(PASS_FLIPS):
                blk = land[w].at[_lin(_flip(me, k))]
                cp = pltpu.make_async_remote_copy(src_ref=blk, dst_ref=blk, send_sem=send_sems.at[n_p * w + j], recv_sem=recv_sems.at[n_p * w + j],
                                                  device_id=sibling, device_id_type=MESH)
                cp.start()
                sent.append(cp)
        for w in range(n):
            for j, k in enumerate(PASS_FLIPS):
                blk = land[w].at[_lin(_flip(me, k + 1))]
                pltpu.make_async_remote_copy(src_ref=blk, dst_ref=blk, send_sem=send_sems.at[n_p * w + j], recv_sem=recv_sems.at[n_p * w + j],
                                             device_id=sibling, device_id_type=MESH).wait_recv()
        for cp in sent:
            cp.wait_send()

    return pl.pallas_call(body, in_specs=[ANY_SPEC] * n, out_specs=[ANY_SPEC] * n, out_shape=[_sds(t.shape, t.dtype) for t in lands],
                          input_output_aliases={i: i for i in range(n)},
                          scratch_shapes=[pltpu.SemaphoreType.DMA((n_p * n,)), pltpu.SemaphoreType.DMA((n_p * n,))], name=name)(*lands)


def _scatter_start(grads, name, deps=()):
    n = len(grads)
    lands = [lax.empty((N_DEV - 1,) + g.shape[1:], g.dtype) for g in grads]

    def issue(src, land, send_sems, recv_sems):
        me = _my_coords()
        for w in range(n):
            for k in range(1, N_DEV):
                q = (N_DEV - 1) * w + k - 1
                peer = _flip(me, k)
                pltpu.make_async_remote_copy(src_ref=src[w].at[_lin(peer)], dst_ref=land[w].at[k - 1], send_sem=send_sems.at[q], recv_sem=recv_sems.at[q],
                                             device_id=peer, device_id_type=MESH).start()

    return _split_start(issue, grads, lands, (N_DEV - 1) * n, name, deps)


def _scatter_wait(handle, after, name):
    n = handle["n"]

    def finish(src, land, send_sems, recv_sems):
        me = _my_coords()
        for w in range(n):
            for k in range(1, N_DEV):
                q = (N_DEV - 1) * w + k - 1
                peer = _flip(me, k)
                cp = pltpu.make_async_remote_copy(src_ref=src[w].at[_lin(peer)], dst_ref=land[w].at[k - 1], send_sem=send_sems.at[q], recv_sem=recv_sems.at[q],
                                                  device_id=peer, device_id_type=MESH)
                cp.wait_send()
                cp.wait_recv()

    return _split_wait(finish, handle, after, name)


def _after(t, *tokens):
    for tok in tokens:
        t = t + tok[0:1, 0:1]
    return t


def _rope_tables(positions):
    half = ROT // 2
    inv_freq = ROPE_THETA ** (-jnp.arange(0, ROT, 2, dtype=F32) / ROT)
    ang = positions.astype(F32).reshape(-1, 1) * inv_freq
    cos, sin = jnp.cos(ang), jnp.sin(ang)
    s = ang.shape[0]
    pad = jnp.zeros((s, HEAD_DIM - ROT), F32)
    zero = jnp.zeros((s, half), F32)
    two = lambda t: jnp.concatenate([t, t], axis=1)
    return (two(jnp.concatenate([cos, cos, pad + 1.0], axis=1)), two(jnp.concatenate([-sin, zero, pad], axis=1)),
            two(jnp.concatenate([zero, sin, pad], axis=1)))


def _local_step(x, tgt, tabs, mod, sinks_pad, hl, hg_norm, g_pre_mix, g_post_mix, g_pre_ffn, g_post_ffn, weights, scatter):
    s = x.shape[0]
    h1 = _pre_fwd(x, g_pre_mix, mod, 1, 0, "pre_mix_fwd")
    w_in, w_in_dm = weights("in", h1)
    proj = _mm_nn(h1, w_in, s, 256, D, F32, "proj_mm")
    att = _attn_fwd(proj, tabs, sinks_pad)
    o_raw, states = _hgrn_fwd(proj, hl)
    ohg = _hgout_fwd(o_raw, proj, hg_norm)
    w_attn_dm, w_hgrn_dm, w_out = weights("mix", ohg)
    y_a = _mm_nn_dm(att, w_attn_dm, s, F32, "attn_proj_mm")
    y_h = _mm_nn_dm(ohg, w_hgrn_dm, s, F32, "hgrn_proj_mm")
    merged = _merge_fwd(y_a, y_h, proj)
    y = _mm_nn(merged, w_out, s, 512, D, F32, "out_mm")
    x1 = _post_fwd(x, y, g_post_mix, mod, 2, "post_mix_fwd")
    h2 = _pre_fwd(x1, g_pre_ffn, mod, 4, 3, "pre_ffn_fwd")
    w_ffn_in_dm, w_ffn_out = weights("ffn", h2)
    gu = _mm_nn_dm(h2, w_ffn_in_dm, s // 2, F32, "ffn_in_mm")
    act = _swiglu_fwd(gu)
    y2 = _mm_nn(act, w_ffn_out, s, 512, FFN // 4, F32, "ffn_out_mm")
    err, loss = _post_fwd_loss(x1, y2, g_post_ffn, mod, 5, tgt, "post_ffn_loss")
    dy2, d_gate2, dg_post_ffn = _post_bwd(err, y2, g_post_ffn, mod, 5, "post_ffn_bwd")
    d_act = _mm_nt(dy2, w_ffn_out, s, 512, D, F32, "ffn_out_dx")
    gw_ffn_out = _mm_tn(act, dy2, 512, D, BF16, "ffn_out_dw")
    t_out = scatter([gw_ffn_out.reshape(N_DEV, FFN // N_DEV, D)], "ffn_out")
    dgu = _swiglu_bwd(d_act, gu)
    dh2 = _mm_nt_dm(dgu, w_ffn_in_dm, s, 512, F32, "ffn_in_dx")
    gw_ffn_in = _mm_tn_dm(h2, dgu, 512, BF16, "ffn_in_dw")
    t_in = scatter([gw_ffn_in], "ffn_in")
    mod = _after(mod, t_out, t_in)
    dx1, d_shift2, d_scale2, dg_pre_ffn = _pre_bwd(dh2, x1, err, g_pre_ffn, mod, 4, "pre_ffn_bwd")
    dy, d_gate1, dg_post_mix = _post_bwd(dx1, y, g_post_mix, mod, 2, "post_mix_bwd")
    d_merged = _mm_nt(dy, w_out, s, 512, D, F32, "out_dx")
    gw_out = _mm_tn(merged, dy, 512, D, BF16, "out_dw")
    dy_a, dy_h, d_gate_a, d_gate_h = _merge_bwd(d_merged, y_a, y_h, proj)
    d_att = _mm_nt_dm(dy_a, w_attn_dm, s, 512, F32, "attn_proj_dx")
    gw_attn = _mm_tn_dm(att, dy_a, 512, BF16, "attn_proj_dw")
    d_ohg = _mm_nt_dm(dy_h, w_hgrn_dm, s, 512, F32, "hgrn_proj_dx")
    gw_hgrn = _mm_tn_dm(ohg, dy_h, 512, BF16, "hgrn_proj_dw")
    t_mix = scatter([gw_attn, gw_hgrn, gw_out.reshape(N_DEV, D // N_DEV, D)], "mix")
    d_o, d_gh, d_hg_norm = _hgout_bwd(d_ohg, o_raw, proj, _after(hg_norm, t_mix))
    d_qh, d_fh, d_ih, d_hl = _hgrn_bwd(proj, hl, states, d_o)
    d_qa, d_ka, d_va, d_sinks = _attn_bwd(proj, tabs, sinks_pad, d_att)
    d_proj = jnp.concatenate([d_qa, d_ka.astype(BF16), d_va.astype(BF16), d_qh, d_fh, d_ih, d_gh, d_gate_a, d_gate_h], axis=1)
    d_proj_dm = d_proj.reshape(s, N_DEV, IN_COLS // N_DEV).transpose(1, 0, 2)
    dh1 = _mm_nt_dm2(d_proj_dm, w_in_dm, s // 2, 512, F32, "proj_dx")
    grad_x, d_shift1, d_scale1, dg_pre_mix = _pre_bwd(dh1, x, dx1, g_pre_mix, mod, 1, "pre_mix_bwd")
    d_mod = jnp.concatenate([d_shift1, d_scale1, d_gate1, d_shift2, d_scale2, d_gate2], axis=1)
    small = [d_mod, dg_pre_mix, dg_post_mix, dg_pre_ffn, dg_post_ffn, d_hl.reshape(1, 2 * HG_W), d_hg_norm, d_sinks]
    return loss, grad_x, small, h1, d_proj_dm


def kernel(x, c, positions, w_ada, b_ada, g_pre_mix, g_post_mix, g_pre_ffn, g_post_ffn, w_in, attn_sinks, w_attn_proj, hg_lower_bounds, hg_norm, w_hgrn_proj, w_out, w_ffn_in, w_ffn_out, loss_target, m_w_ada, m_b_ada, m_g_pre_mix, m_g_post_mix, m_g_pre_ffn, m_g_post_ffn, m_w_in, m_attn_sinks, m_w_attn_proj, m_hg_lower_bounds, m_hg_norm, m_w_hgrn_proj, m_w_out, m_w_ffn_in, m_w_ffn_out, v_w_ada, v_b_ada, v_g_pre_mix, v_g_post_mix, v_g_pre_ffn, v_g_post_ffn, v_w_in, v_attn_sinks, v_w_attn_proj, v_hg_lower_bounds, v_hg_norm, v_w_hgrn_proj, v_w_out, v_w_ffn_in, v_w_ffn_out):
    my_id = _lin(_my_coords())
    s = x.shape[1]
    n_ada = w_ada.shape[2]

    c_all = _exchange_small(c.reshape(1, 1, D), True, "gather_c").reshape(N_DEV, D)
    groups = {"in": [w_in], "mix": [w_attn_proj, w_hgrn_proj, w_out], "ffn": [w_ffn_in, w_ffn_out]}
    start = lambda g, dep: _gather_start([w[0].astype(BF16) for w in groups[g]], "gather_start_" + g, deps=[dep])
    gathers = {"in": start("in", c_all)}
    b_cols = _after(lax.dynamic_slice(b_ada, (0, my_id * n_ada), (1, n_ada)), gathers["in"]["token"])
    mod_part = _mod_part(c_all, w_ada[0], b_cols)
    mod = _exchange_small(mod_part.reshape(N_DEV, 1, n_ada), False, "scatter_mod").reshape(1, N_MOD * D)
    gathers["mix"] = start("mix", mod)
    gathers["ffn"] = start("ffn", gathers["mix"]["token"])
    mod = _after(mod, gathers["ffn"]["token"])

    def weights(group, after):
        lands = _gather_pass(_gather_wait(gathers[group], [after], "gather_wait_" + group), "gather_pass_" + group)
        if group == "in":
            return lands[0].transpose(1, 0, 2).reshape(D, IN_COLS), lands[0]
        if group == "mix":
            return lands[0], lands[1], lands[2].reshape(D, D)
        return lands[0], lands[1].reshape(FFN, D)

    scatters = {}

    def scatter(grads, group):
        scatters[group] = _scatter_start(grads, "scatter_start_" + group)
        return scatters[group]["token"]

    sinks_pad = jnp.pad(attn_sinks, ((0, 0), (0, LANE - ATT_HEADS)))
    loss, grad_x, small, h1, d_proj_dm = _local_step(
        x[0], loss_target[0], _rope_tables(positions), mod, sinks_pad, hg_lower_bounds, hg_norm, g_pre_mix, g_post_mix, g_pre_ffn, g_post_ffn,
        weights, scatter)
    loss = lax.psum(loss[0, 0], ("x", "y", "c"))

    sizes = [t.shape[1] for t in small]
    parts = _exchange_small(jnp.concatenate(small, axis=1).reshape(1, 1, sum(sizes)), True, "gather_small_grads")
    gw_in = _mm_tn_dm(h1, d_proj_dm, 512, BF16, "proj_dw", b_dm=True, deps=[parts])
    scatter([gw_in], "in")
    offs = [sum(sizes[:k]) for k in range(len(sizes))]
    piece = lambda k, n=None: parts[:, :, offs[k]:offs[k] + (sizes[k] if n is None else n)]
    small_w = [(piece(0), b_ada, m_b_ada, v_b_ada), (piece(1), g_pre_mix, m_g_pre_mix, v_g_pre_mix),
               (piece(2), g_post_mix, m_g_post_mix, v_g_post_mix), (piece(3), g_pre_ffn, m_g_pre_ffn, v_g_pre_ffn),
               (piece(4), g_post_ffn, m_g_post_ffn, v_g_post_ffn),
               (piece(5).reshape(N_DEV, 2, HG_W), hg_lower_bounds, m_hg_lower_bounds, v_hg_lower_bounds),
               (piece(6), hg_norm, m_hg_norm, v_hg_norm), (piece(7, ATT_HEADS), attn_sinks, m_attn_sinks, v_attn_sinks)]
    names = ["b_ada", "g_pre_mix", "g_post_mix", "g_pre_ffn", "g_post_ffn", "hg_lower_bounds", "hg_norm", "attn_sinks"]
    res = {n: _adamw(p, w, m, v, "adamw_" + n) for n, (p, w, m, v) in zip(names, small_w)}

    dmod_cols = lax.dynamic_slice(parts.reshape(N_DEV, -1), (0, my_id * n_ada), (N_DEV, n_ada))
    g_w_ada = _grad_w_ada(c_all.T, dmod_cols)
    res["w_ada"] = [g_w_ada] + list(_adamw(g_w_ada[None], w_ada[0], m_w_ada[0], v_w_ada[0], "adamw_w_ada", emit_grad=False))

    big = {"ffn_out": [("w_ffn_out", w_ffn_out, m_w_ffn_out, v_w_ffn_out)], "ffn_in": [("w_ffn_in", w_ffn_in, m_w_ffn_in, v_w_ffn_in)],
           "mix": [("w_attn_proj", w_attn_proj, m_w_attn_proj, v_w_attn_proj), ("w_hgrn_proj", w_hgrn_proj, m_w_hgrn_proj, v_w_hgrn_proj),
                   ("w_out", w_out, m_w_out, v_w_out)],
           "in": [("w_in", w_in, m_w_in, v_w_in)]}
    after = [scatters["in"]["token"]]
    for group, members in big.items():
        if group == "in":
            after = after + [res["w_ada"][1], res["b_ada"][1]]
        local, lands = _scatter_wait(scatters[group], after, "scatter_wait_" + group)
        for (n, w, m, v), g_local, land in zip(members, local, lands):
            own = lax.dynamic_index_in_dim(g_local, my_id, 0, keepdims=False)
            res[n] = _adamw(land, w[0], m[0], v[0], "adamw_" + n, own=own)
            after = [res[n][1]]

    order = ["w_ada", "b_ada", "g_pre_mix", "g_post_mix", "g_pre_ffn", "g_post_ffn", "w_in", "attn_sinks", "w_attn_proj",
             "hg_lower_bounds", "hg_norm", "w_hgrn_proj", "w_out", "w_ffn_in", "w_ffn_out"]
    lead = {"w_ada", "w_in", "w_attn_proj", "w_hgrn_proj", "w_out", "w_ffn_in", "w_ffn_out"}
    outs = [loss, grad_x[None]]
    for k in range(4):
        outs += [res[n][k][None] if n in lead else res[n][k] for n in order]
    return tuple(outs)
```

```python
import functools

import jax
import jax.numpy as jnp
from jax import lax
from jax.experimental import pallas as pl
from jax.experimental.pallas import tpu as pltpu

F32 = jnp.float32
BF16 = jnp.bfloat16

N_DEV = 8
D = 2048
ATT_HEADS = 16
KV_HEADS = 2
HEAD_DIM = 64
GROUP = ATT_HEADS // KV_HEADS
ATT_W = ATT_HEADS * HEAD_DIM
BLK = 128
ROT = HEAD_DIM // 4
ROPE_THETA = 500000.0
HG_HEADS = 8
HG_K = 128
HG_W = HG_HEADS * HG_K
CHUNK = 64
SUB = 16
FFN = 5632
N_MOD = 6
EPS = 1e-6
LANE = 128
Q_A, K_A, V_A, Q_H, F_H, I_H, G_H, GT_A, GT_H, IN_COLS = 0, 1024, 1152, 1280, 2304, 3328, 4352, 5376, 7424, 9472

ADAM_LR, ADAM_B1, ADAM_B2, ADAM_EPS, ADAM_WD, ADAM_STEP = 0.001, 0.9, 0.999, 1e-08, 0.01, 10

TR = 256
HG_TB = 512
VMEM_BIG = 56 << 20
MESH = pl.DeviceIdType.MESH


def _sds(shape, dtype):
    return jax.ShapeDtypeStruct(shape, dtype)


def _params(n_axes, vmem=None):
    return pltpu.CompilerParams(dimension_semantics=("arbitrary",) * n_axes, vmem_limit_bytes=vmem)


def _sig(t):
    return 1.0 / (1.0 + jnp.exp(-t))


def _dot(a, b, dims):
    return lax.dot_general(a, b, (dims, ((), ())), preferred_element_type=F32)


NN = ((1,), (0,))
NT = ((1,), (1,))
TN = ((0,), (0,))


def _matmul(a, b, a_spec, b_spec, o_spec, out_shape, grid, dims, acc_shape, name, deps=()):
    nk = grid[2]
    nd = len(deps)

    def body(a_ref, b_ref, *rest):
        o_ref, scratch = rest[nd], rest[nd + 1:]
        part = _dot(a_ref[...], b_ref[...], dims)
        if nk == 1:
            o_ref[...] = part.astype(o_ref.dtype)
        else:
            acc = scratch[0]
            k = pl.program_id(2)

            @pl.when(k == 0)
            def _():
                acc[...] = part

            @pl.when(k > 0)
            def _():
                acc[...] += part

            @pl.when(k == nk - 1)
            def _():
                o_ref[...] = acc[...].astype(o_ref.dtype)

    return pl.pallas_call(
        body, grid=grid, in_specs=[a_spec, b_spec] + [pl.BlockSpec(memory_space=pl.ANY)] * nd, out_specs=o_spec, out_shape=out_shape,
        scratch_shapes=[pltpu.VMEM(acc_shape, F32)] if nk > 1 else [],
        compiler_params=_params(3, VMEM_BIG), name=name)(a, b, *deps)


def _mm_nn(a, b, tm, tn, tk, out_dtype, name):
    m, k = a.shape
    n = b.shape[1]
    return _matmul(a, b, pl.BlockSpec((tm, tk), lambda j, i, kk: (i, kk)), pl.BlockSpec((tk, tn), lambda j, i, kk: (kk, j)),
                   pl.BlockSpec((tm, tn), lambda j, i, kk: (i, j)), _sds((m, n), out_dtype),
                   (n // tn, m // tm, k // tk), NN, (tm, tn), name)


def _mm_nn_dm(a, b, tm, out_dtype, name):
    m, k = a.shape
    n = b.shape[2]
    return _matmul(a, b, pl.BlockSpec((tm, k), lambda j, i, kk: (i, 0)), pl.BlockSpec((None, k, n), lambda j, i, kk: (j, 0, 0)),
                   pl.BlockSpec((tm, n), lambda j, i, kk: (i, j)), _sds((m, N_DEV * n), out_dtype),
                   (N_DEV, m // tm, 1), NN, (tm, n), name)


def _mm_nt(a, b, tm, tn, tk, out_dtype, name):
    m, k = a.shape
    n = b.shape[0]
    return _matmul(a, b, pl.BlockSpec((tm, tk), lambda j, i, kk: (i, kk)), pl.BlockSpec((tn, tk), lambda j, i, kk: (j, kk)),
                   pl.BlockSpec((tm, tn), lambda j, i, kk: (i, j)), _sds((m, n), out_dtype),
                   (n // tn, m // tm, k // tk), NT, (tm, tn), name)


def _mm_nt_dm(a, b, tm, tn, out_dtype, name):
    m = a.shape[0]
    n_out, n = b.shape[1], b.shape[2]
    return _matmul(a, b, pl.BlockSpec((tm, n), lambda j, i, kk: (i, kk)), pl.BlockSpec((None, tn, n), lambda j, i, kk: (kk, j, 0)),
                   pl.BlockSpec((tm, tn), lambda j, i, kk: (i, j)), _sds((m, n_out), out_dtype),
                   (n_out // tn, m // tm, N_DEV), NT, (tm, tn), name)


def _mm_tn(a, b, tm, tn, out_dtype, name, deps=()):
    s, m = a.shape
    n = b.shape[1]
    return _matmul(a, b, pl.BlockSpec((s, tm), lambda j, i, kk: (0, i)), pl.BlockSpec((s, tn), lambda j, i, kk: (0, j)),
                   pl.BlockSpec((tm, tn), lambda j, i, kk: (i, j)), _sds((m, n), out_dtype),
                   (n // tn, m // tm, 1), TN, (tm, tn), name, deps)


def _mm_tn_dm(a, b, tm, out_dtype, name):
    s, m = a.shape
    n = b.shape[1] // N_DEV
    return _matmul(a, b, pl.BlockSpec((s, tm), lambda j, i, kk: (0, i)), pl.BlockSpec((s, n), lambda j, i, kk: (0, j)),
                   pl.BlockSpec((None, tm, n), lambda j, i, kk: (j, i, 0)), _sds((N_DEV, m, n), out_dtype),
                   (N_DEV, m // tm, 1), TN, (tm, n), name)


def _row_spec():
    return pl.BlockSpec((TR, D), lambda i: (i, 0))


def _vec_spec(k=0):
    return pl.BlockSpec((1, D), lambda i: (0, k))


def _acc_rows(ref, first, val):
    @pl.when(first)
    def _():
        ref[...] = val

    @pl.when(jnp.logical_not(first))
    def _():
        ref[...] += val


def _pre_fwd(x, g, mod, k_scale, k_shift, name):
    s = x.shape[0]

    def body(x_ref, g_ref, sc_ref, sh_ref, h_ref):
        xv = x_ref[...]
        r = lax.rsqrt(jnp.mean(xv * xv, axis=-1, keepdims=True) + EPS)
        n = xv * r * g_ref[...]
        h_ref[...] = (n * (1.0 + sc_ref[...]) + sh_ref[...]).astype(h_ref.dtype)

    return pl.pallas_call(body, grid=(s // TR,), in_specs=[_row_spec(), _vec_spec(), _vec_spec(k_scale), _vec_spec(k_shift)],
                          out_specs=_row_spec(), out_shape=_sds((s, D), BF16), compiler_params=_params(1), name=name)(x, g, mod, mod)


def _post_fwd(x, y, g, mod, k_gate, name):
    s = x.shape[0]

    def body(x_ref, y_ref, g_ref, gt_ref, o_ref):
        yv = y_ref[...]
        r = lax.rsqrt(jnp.mean(yv * yv, axis=-1, keepdims=True) + EPS)
        o_ref[...] = x_ref[...] + gt_ref[...] * (yv * r * g_ref[...])

    return pl.pallas_call(body, grid=(s // TR,), in_specs=[_row_spec(), _row_spec(), _vec_spec(), _vec_spec(k_gate)],
                          out_specs=_row_spec(), out_shape=_sds((s, D), F32), compiler_params=_params(1), name=name)(x, y, g, mod)


def _post_fwd_loss(x, y, g, mod, k_gate, tgt, name):
    s = x.shape[0]

    def body(x_ref, y_ref, g_ref, gt_ref, t_ref, e_ref, loss_ref):
        i = pl.program_id(0)
        yv = y_ref[...]
        r = lax.rsqrt(jnp.mean(yv * yv, axis=-1, keepdims=True) + EPS)
        err = x_ref[...] + gt_ref[...] * (yv * r * g_ref[...]) - t_ref[...]
        e_ref[...] = err * (1.0 / D)
        part = 0.5 * jnp.sum(jnp.mean(err * err, axis=-1, keepdims=True), axis=0, keepdims=True)
        _acc_rows(loss_ref, i == 0, part)

    return pl.pallas_call(body, grid=(s // TR,),
                          in_specs=[_row_spec(), _row_spec(), _vec_spec(), _vec_spec(k_gate), _row_spec()],
                          out_specs=[_row_spec(), pl.BlockSpec((1, 1), lambda i: (0, 0))],
                          out_shape=[_sds((s, D), F32), _sds((1, 1), F32)], compiler_params=_params(1), name=name)(x, y, g, mod, tgt)


def _pre_bwd(dh, x, res, g, mod, k_scale, name):
    s = x.shape[0]

    def body(dh_ref, x_ref, res_ref, g_ref, sc_ref, dx_ref, dsh_ref, dsc_ref, dg_ref):
        first = pl.program_id(0) == 0
        xv, dh_v, gv = x_ref[...], dh_ref[...], g_ref[...]
        r = lax.rsqrt(jnp.mean(xv * xv, axis=-1, keepdims=True) + EPS)
        xh = xv * r
        dn = dh_v * (1.0 + sc_ref[...])
        dgn = dn * gv
        dx_ref[...] = res_ref[...] + r * (dgn - xh * jnp.mean(dgn * xh, axis=-1, keepdims=True))
        _acc_rows(dsh_ref, first, jnp.sum(dh_v, axis=0, keepdims=True))
        _acc_rows(dsc_ref, first, jnp.sum(dh_v * (xh * gv), axis=0, keepdims=True))
        _acc_rows(dg_ref, first, jnp.sum(dn * xh, axis=0, keepdims=True))

    return pl.pallas_call(body, grid=(s // TR,),
                          in_specs=[_row_spec(), _row_spec(), _row_spec(), _vec_spec(), _vec_spec(k_scale)],
                          out_specs=[_row_spec(), _vec_spec(), _vec_spec(), _vec_spec()],
                          out_shape=[_sds((s, D), F32)] + [_sds((1, D), F32)] * 3,
                          compiler_params=_params(1), name=name)(dh, x, res, g, mod)


def _post_bwd(dx, y, g, mod, k_gate, name):
    s = y.shape[0]

    def body(dx_ref, y_ref, g_ref, gt_ref, dy_ref, dgt_ref, dg_ref):
        first = pl.program_id(0) == 0
        yv, dxv, gv = y_ref[...], dx_ref[...], g_ref[...]
        r = lax.rsqrt(jnp.mean(yv * yv, axis=-1, keepdims=True) + EPS)
        yh = yv * r
        dn = dxv * gt_ref[...]
        dgn = dn * gv
        dy_ref[...] = (r * (dgn - yh * jnp.mean(dgn * yh, axis=-1, keepdims=True))).astype(dy_ref.dtype)
        _acc_rows(dgt_ref, first, jnp.sum(dxv * (yh * gv), axis=0, keepdims=True))
        _acc_rows(dg_ref, first, jnp.sum(dn * yh, axis=0, keepdims=True))

    return pl.pallas_call(body, grid=(s // TR,), in_specs=[_row_spec(), _row_spec(), _vec_spec(), _vec_spec(k_gate)],
                          out_specs=[_row_spec(), _vec_spec(), _vec_spec()],
                          out_shape=[_sds((s, D), BF16), _sds((1, D), F32), _sds((1, D), F32)],
                          compiler_params=_params(1), name=name)(dx, y, g, mod)


SW_TN = 1408
SW_TR = 512
TALL = 1024


def _swiglu_fwd(gu):
    s = gu.shape[0]
    nb = FFN // SW_TN

    def body(g_ref, u_ref, a_ref):
        gv = g_ref[...]
        a_ref[...] = (gv * _sig(gv) * u_ref[...]).astype(a_ref.dtype)

    return pl.pallas_call(body, grid=(s // SW_TR, nb),
                          in_specs=[pl.BlockSpec((SW_TR, SW_TN), lambda i, j: (i, j)), pl.BlockSpec((SW_TR, SW_TN), lambda i, j: (i, j + nb))],
                          out_specs=pl.BlockSpec((SW_TR, SW_TN), lambda i, j: (i, j)), out_shape=_sds((s, FFN), BF16),
                          compiler_params=_params(2, 48 << 20), name="swiglu_fwd")(gu, gu)


def _swiglu_bwd(dact, gu):
    s = gu.shape[0]
    nb = FFN // SW_TN

    def body(da_ref, g_ref, u_ref, o_ref):
        half = pl.program_id(2)
        gv, da = g_ref[...], da_ref[...]
        sg = _sig(gv)
        d_gate = da * u_ref[...] * (sg * (1.0 + gv * (1.0 - sg)))
        d_up = da * (gv * sg)
        o_ref[...] = jnp.where(half == 0, d_gate, d_up).astype(o_ref.dtype)

    blk = lambda f: pl.BlockSpec((SW_TR, SW_TN), f)
    return pl.pallas_call(body, grid=(s // SW_TR, nb, 2),
                          in_specs=[blk(lambda i, j, h: (i, j)), blk(lambda i, j, h: (i, j)), blk(lambda i, j, h: (i, j + nb))],
                          out_specs=blk(lambda i, j, h: (i, j + nb * h)), out_shape=_sds((s, 2 * FFN), BF16),
                          compiler_params=_params(3, 48 << 20), name="swiglu_bwd")(dact, gu, gu)


MG_TN = 256


def _merge_fwd(y_a, y_h, proj):
    s = y_a.shape[0]
    tn = MG_TN
    ba, bh = GT_A // tn, GT_H // tn

    def body(ya_ref, yh_ref, ga_ref, gh_ref, m_ref):
        m_ref[...] = (_sig(ga_ref[...]) * ya_ref[...] + _sig(gh_ref[...]) * yh_ref[...]).astype(m_ref.dtype)

    tr = min(s, TALL)
    blk = lambda f: pl.BlockSpec((tr, tn), f)
    return pl.pallas_call(body, grid=(s // tr, D // tn),
                          in_specs=[blk(lambda i, j: (i, j)), blk(lambda i, j: (i, j)), blk(lambda i, j: (i, j + ba)), blk(lambda i, j: (i, j + bh))],
                          out_specs=blk(lambda i, j: (i, j)), out_shape=_sds((s, D), BF16),
                          compiler_params=_params(2), name="merge_fwd")(y_a, y_h, proj, proj)


def _merge_bwd(dm, y_a, y_h, proj):
    s = y_a.shape[0]
    tn = MG_TN
    ba, bh = GT_A // tn, GT_H // tn

    def body(dm_ref, ya_ref, yh_ref, ga_ref, gh_ref, dya_ref, dyh_ref, dga_ref, dgh_ref):
        dmv = dm_ref[...]
        sa, sh = _sig(ga_ref[...]), _sig(gh_ref[...])
        dya_ref[...] = (dmv * sa).astype(BF16)
        dyh_ref[...] = (dmv * sh).astype(BF16)
        dga_ref[...] = (dmv * ya_ref[...] * (sa * (1.0 - sa))).astype(BF16)
        dgh_ref[...] = (dmv * yh_ref[...] * (sh * (1.0 - sh))).astype(BF16)

    tr = min(s, TALL)
    blk = lambda f: pl.BlockSpec((tr, tn), f)
    nat = blk(lambda i, j: (i, j))
    return pl.pallas_call(body, grid=(s // tr, D // tn),
                          in_specs=[nat, nat, nat, blk(lambda i, j: (i, j + ba)), blk(lambda i, j: (i, j + bh))],
                          out_specs=[nat] * 4, out_shape=[_sds((s, D), BF16)] * 4,
                          compiler_params=_params(2), name="merge_bwd")(dm, y_a, y_h, proj, proj)


def _hgout_fwd(o_raw, proj, hg_norm):
    s = o_raw.shape[0]
    bg = G_H // LANE

    def body(o_ref, g_ref, n_ref, out_ref):
        ov = o_ref[...]
        r = lax.rsqrt(jnp.mean(ov * ov, axis=-1, keepdims=True) + EPS)
        out_ref[...] = (ov * r * n_ref[...] * _sig(g_ref[...])).astype(out_ref.dtype)

    tr = min(s, TALL)
    blk = lambda f: pl.BlockSpec((tr, LANE), f)
    return pl.pallas_call(body, grid=(s // tr, HG_HEADS),
                          in_specs=[blk(lambda i, h: (i, h)), blk(lambda i, h: (i, h + bg)), pl.BlockSpec((1, LANE), lambda i, h: (0, 0))],
                          out_specs=blk(lambda i, h: (i, h)), out_shape=_sds((s, HG_W), BF16),
                          compiler_params=_params(2), name="hgout_fwd")(o_raw, proj, hg_norm)


def _hgout_bwd(d_out, o_raw, proj, hg_norm):
    s = o_raw.shape[0]
    bg = G_H // LANE

    def body(d_ref, o_ref, g_ref, n_ref, do_ref, dg_ref, dn_ref):
        first = jnp.logical_and(pl.program_id(0) == 0, pl.program_id(1) == 0)
        ov, dv, nv = o_ref[...], d_ref[...], n_ref[...]
        sg = _sig(g_ref[...])
        r = lax.rsqrt(jnp.mean(ov * ov, axis=-1, keepdims=True) + EPS)
        oh = ov * r
        d_on = dv * sg
        dg_ref[...] = (dv * (oh * nv) * (sg * (1.0 - sg))).astype(dg_ref.dtype)
        t = d_on * nv
        do_ref[...] = r * (t - oh * jnp.mean(t * oh, axis=-1, keepdims=True))
        _acc_rows(dn_ref, first, jnp.sum(d_on * oh, axis=0, keepdims=True))

    tr = min(s, TALL)
    blk = lambda f: pl.BlockSpec((tr, LANE), f)
    vec = pl.BlockSpec((1, LANE), lambda i, h: (0, 0))
    return pl.pallas_call(body, grid=(s // tr, HG_HEADS),
                          in_specs=[blk(lambda i, h: (i, h)), blk(lambda i, h: (i, h)), blk(lambda i, h: (i, h + bg)), vec],
                          out_specs=[blk(lambda i, h: (i, h)), blk(lambda i, h: (i, h)), vec],
                          out_shape=[_sds((s, HG_W), F32), _sds((s, HG_W), BF16), _sds((1, LANE), F32)],
                          compiler_params=_params(2), name="hgout_bwd")(d_out, o_raw, proj, hg_norm)


def _rope(t, cos, s_lo, s_hi):
    return t * cos + pltpu.roll(t, LANE - ROT // 2, 1) * s_lo + pltpu.roll(t, ROT // 2, 1) * s_hi


def _rope_wide(t, cos, s_lo, s_hi):
    return jnp.concatenate([_rope(t[:, k * LANE:(k + 1) * LANE], cos, s_lo, s_hi) for k in range(t.shape[1] // LANE)], axis=1)


def _attn_mask(has_prev):
    qi = lax.broadcasted_iota(jnp.int32, (BLK, 2 * BLK), 0)
    kj = lax.broadcasted_iota(jnp.int32, (BLK, 2 * BLK), 1)
    rel = BLK + qi - kj
    band = jnp.logical_and(rel >= 0, rel < BLK)
    return jnp.logical_and(band, jnp.logical_or(has_prev, kj >= BLK))


def _attn_specs():
    prev = lambda i: jnp.maximum(i - 1, 0)
    kb, vb = K_A // LANE, V_A // LANE
    blk = lambda f: pl.BlockSpec((BLK, LANE), f)
    tabs = [blk(lambda i: (i, 0))] * 3 + [blk(lambda i: (prev(i), 0))] * 3
    return [pl.BlockSpec((BLK, ATT_W), lambda i: (i, 0)), blk(lambda i: (i, kb)), blk(lambda i: (prev(i), kb)),
            blk(lambda i: (i, vb)), blk(lambda i: (prev(i), vb))] + tabs + [pl.BlockSpec((1, LANE), lambda i: (0, 0))]


def _attn_probs(qh, kg, mask, sk):
    logits = _dot(qh, kg, NT) * (HEAD_DIM ** -0.5)
    logits = jnp.where(mask, logits, -jnp.inf)
    m = jnp.maximum(jnp.max(logits, axis=-1, keepdims=True), sk)
    p = jnp.exp(logits - m)
    e_sink = jnp.exp(sk - m)
    inv = 1.0 / (jnp.sum(p, axis=-1, keepdims=True) + e_sink)
    return p * inv, e_sink * inv


def _attn_fwd(proj, tabs, sinks):
    s = proj.shape[0]

    def body(q_ref, kc_ref, kp_ref, vc_ref, vp_ref, c0, l0, h0, c1, l1, h1, sk_ref, o_ref):
        i = pl.program_id(0)
        mask = _attn_mask(i > 0)
        q = _rope_wide(q_ref[...], c0[...], l0[...], h0[...]).astype(BF16)
        kk = jnp.concatenate([_rope(kp_ref[...], c1[...], l1[...], h1[...]), _rope(kc_ref[...], c0[...], l0[...], h0[...])], axis=0).astype(BF16)
        vv = jnp.concatenate([vp_ref[...], vc_ref[...]], axis=0).astype(BF16)
        outs = []
        for h in range(ATT_HEADS):
            g = h // GROUP
            prob, _ = _attn_probs(q[:, h * HEAD_DIM:(h + 1) * HEAD_DIM], kk[:, g * HEAD_DIM:(g + 1) * HEAD_DIM], mask, sk_ref[:, h:h + 1])
            outs.append(_dot(prob.astype(BF16), vv[:, g * HEAD_DIM:(g + 1) * HEAD_DIM], NN))
        o_ref[...] = jnp.concatenate(outs, axis=1).astype(o_ref.dtype)

    return pl.pallas_call(body, grid=(s // BLK,), in_specs=_attn_specs(),
                          out_specs=pl.BlockSpec((BLK, ATT_W), lambda i: (i, 0)), out_shape=_sds((s, ATT_W), BF16),
                          compiler_params=_params(1), name="attn_fwd")(proj, proj, proj, proj, proj, *tabs, *tabs, sinks)


def _attn_bwd(proj, tabs, sinks, d_att):
    s = proj.shape[0]

    def body(q_ref, kc_ref, kp_ref, vc_ref, vp_ref, c0, l0, h0, c1, l1, h1, sk_ref, do_ref, dq_ref, dk_ref, dv_ref, ds_ref):
        i = pl.program_id(0)

        @pl.when(i == 0)
        def _():
            dk_ref[...] = jnp.zeros_like(dk_ref)
            dv_ref[...] = jnp.zeros_like(dv_ref)
            ds_ref[...] = jnp.zeros_like(ds_ref)

        mask = _attn_mask(i > 0)
        q = _rope_wide(q_ref[...], c0[...], l0[...], h0[...]).astype(BF16)
        kk = jnp.concatenate([_rope(kp_ref[...], c1[...], l1[...], h1[...]), _rope(kc_ref[...], c0[...], l0[...], h0[...])], axis=0).astype(BF16)
        vv = jnp.concatenate([vp_ref[...], vc_ref[...]], axis=0).astype(BF16)
        d_o = do_ref[...].astype(BF16)
        lane = lax.broadcasted_iota(jnp.int32, (1, LANE), 1)
        dqs, dks, dvs = [], [], []
        d_sink = jnp.zeros((1, LANE), F32)
        for g in range(KV_HEADS):
            kg, vg = kk[:, g * HEAD_DIM:(g + 1) * HEAD_DIM], vv[:, g * HEAD_DIM:(g + 1) * HEAD_DIM]
            dkg = jnp.zeros((2 * BLK, HEAD_DIM), F32)
            dvg = jnp.zeros((2 * BLK, HEAD_DIM), F32)
            for j in range(GROUP):
                h = g * GROUP + j
                qh, doh = q[:, h * HEAD_DIM:(h + 1) * HEAD_DIM], d_o[:, h * HEAD_DIM:(h + 1) * HEAD_DIM]
                prob, p_sink = _attn_probs(qh, kg, mask, sk_ref[:, h:h + 1])
                d_p = _dot(doh, vg, NT)
                dd = jnp.sum(prob * d_p, axis=-1, keepdims=True)
                d_s = (prob * (d_p - dd)).astype(BF16)
                d_sink = d_sink + jnp.where(lane == h, -jnp.sum(p_sink * dd, axis=0, keepdims=True), 0.0)
                dqs.append(_dot(d_s, kg, NN) * (HEAD_DIM ** -0.5))
                dkg = dkg + _dot(d_s, qh, TN) * (HEAD_DIM ** -0.5)
                dvg = dvg + _dot(prob.astype(BF16), doh, TN)
            dks.append(dkg)
            dvs.append(dvg)
        dq_ref[...] = _rope_wide(jnp.concatenate(dqs, axis=1), c0[...], -l0[...], -h0[...]).astype(dq_ref.dtype)
        d_k = jnp.concatenate(dks, axis=1)
        d_v = jnp.concatenate(dvs, axis=1)
        cur = pl.ds(pl.multiple_of(i * BLK, BLK), BLK)
        prv = pl.ds(pl.multiple_of(jnp.maximum(i - 1, 0) * BLK, BLK), BLK)
        dk_ref[prv, :] += _rope(d_k[:BLK], c1[...], -l1[...], -h1[...])
        dk_ref[cur, :] += _rope(d_k[BLK:], c0[...], -l0[...], -h0[...])
        dv_ref[prv, :] += d_v[:BLK]
        dv_ref[cur, :] += d_v[BLK:]
        ds_ref[...] += d_sink

    full = pl.BlockSpec((s, LANE), lambda i: (0, 0))
    return pl.pallas_call(body, grid=(s // BLK,), in_specs=_attn_specs() + [pl.BlockSpec((BLK, ATT_W), lambda i: (i, 0))],
                          out_specs=[pl.BlockSpec((BLK, ATT_W), lambda i: (i, 0)), full, full, pl.BlockSpec((1, LANE), lambda i: (0, 0))],
                          out_shape=[_sds((s, ATT_W), BF16), _sds((s, LANE), F32), _sds((s, LANE), F32), _sds((1, LANE), F32)],
                          compiler_params=_params(1), name="attn_bwd")(proj, proj, proj, proj, proj, *tabs, *tabs, sinks, d_att)


def _tri_matmul(tri, t):
    hi = t.astype(BF16)
    r1 = t - hi.astype(F32)
    mid = r1.astype(BF16)
    lo = (r1 - mid.astype(F32)).astype(BF16)
    return _dot(tri, hi, NN) + _dot(tri, mid, NN) + _dot(tri, lo, NN)


def _lower_bound(hl):
    a, b = hl[0:1, :], hl[1:2, :]
    mx = jnp.maximum(a, b)
    ea, eb = jnp.exp(a - mx), jnp.exp(b - mx)
    return ea / (ea + eb)


def _hg_gates(q_raw, f_raw, lb, tri_lower):
    sg = _sig(f_raw)
    f = lb + (1.0 - lb) * sg
    sq = _sig(q_raw)
    b = _tri_matmul(tri_lower, jnp.log(f))
    return sg, f, 1.0 - f, sq, q_raw * sq, b


HG_PAIR_FWD = 2
HG_PAIR_BWD = 1


def _hg_specs(n_map, pair):
    blk = lambda off: pl.BlockSpec((HG_TB, pair * LANE), lambda h, n: (n_map(n), off // (pair * LANE) + h))
    return [blk(Q_H), blk(F_H), blk(I_H), pl.BlockSpec((2, pair * LANE), lambda h, n: (0, h))]


def _hg_intra(qs, kk, b, d_a=None):
    lane = lax.broadcasted_iota(jnp.int32, (SUB, CHUNK), 1)
    row1 = lax.broadcasted_iota(jnp.int32, (SUB, 1), 0)
    rowk = lax.broadcasted_iota(jnp.int32, (SUB, LANE), 0)
    grad = d_a is not None
    a_blocks, dq_blocks, dk_blocks = [], [], []
    dk_left = None
    for j in range(CHUNK // SUB):
        lo = j * SUB
        q_j, k_j, b_j = qs[lo:lo + SUB], kk[lo:lo + SUB], b[lo:lo + SUB]
        a_j = jnp.zeros((SUB, CHUNK), F32)
        if grad:
            da_j = d_a[lo:lo + SUB]
            dq_j = jnp.zeros((SUB, LANE), F32)
            dk_j = jnp.zeros((SUB, LANE), F32)
        for sx in range(SUB):
            e = jnp.where(row1 >= sx, jnp.exp(jnp.minimum(b_j - b_j[sx:sx + 1], 0.0)), 0.0)
            pe = q_j * e
            a_j = jnp.where(lane == lo + sx, jnp.sum(pe * k_j[sx:sx + 1], axis=-1, keepdims=True), a_j)
            if grad:
                g_col = da_j[:, lo + sx:lo + sx + 1]
                dq_j = dq_j + g_col * (e * k_j[sx:sx + 1])
                dk_j = jnp.where(rowk == sx, jnp.sum(g_col * pe, axis=0, keepdims=True), dk_j)
        if j > 0:
            ref = b[lo - 1:lo]
            sc_q = jnp.exp(b_j - ref)
            sc_k = jnp.exp(jnp.minimum(ref - b, 0.0))
            qt = (q_j * sc_q).astype(BF16)
            kt = (kk * sc_k).astype(BF16)
            a_j = a_j + jnp.where(lane < lo, _dot(qt, kt, NT), 0.0)
            if grad:
                da_left = jnp.where(lane < lo, da_j, 0.0).astype(BF16)
                dq_j = dq_j + _dot(da_left, kt, NN) * sc_q
                t = _dot(da_left, qt, TN) * sc_k
                dk_left = t if dk_left is None else dk_left + t
        a_blocks.append(a_j)
        if grad:
            dq_blocks.append(dq_j)
            dk_blocks.append(dk_j)
    a = jnp.concatenate(a_blocks, axis=0)
    if not grad:
        return a
    return a, jnp.concatenate(dq_blocks, axis=0), jnp.concatenate(dk_blocks, axis=0) + dk_left


def _hgrn_fwd(proj, hl):
    s = proj.shape[0]
    n_chunk = HG_TB // CHUNK
    pair = HG_PAIR_FWD

    def body(q_ref, f_ref, i_ref, hl_ref, o_ref, st_out_ref, st_ref):
        @pl.when(pl.program_id(1) == 0)
        def _():
            st_ref[...] = jnp.zeros_like(st_ref)

        r_i = lax.broadcasted_iota(jnp.int32, (CHUNK, CHUNK), 0)
        c_i = lax.broadcasted_iota(jnp.int32, (CHUNK, CHUNK), 1)
        tri_lower = (r_i >= c_i).astype(BF16)

        def chunk(c, carry):
            rows = pl.ds(pl.multiple_of(c * CHUNK, CHUNK), CHUNK)
            for p in range(pair):
                cols = slice(p * LANE, (p + 1) * LANE)
                lb = _lower_bound(hl_ref[:, cols])
                v = i_ref[rows, cols].astype(BF16)
                _, _, kk, _, qs, b = _hg_gates(q_ref[rows, cols], f_ref[rows, cols], lb, tri_lower)
                a = _hg_intra(qs, kk, b)
                st = st_ref[p]
                st_b = st.astype(BF16)
                st_out_ref[p, c] = st_b
                o_ref[rows, cols] = _dot((qs * jnp.exp(b)).astype(BF16), st_b, NT) + _dot(a.astype(BF16), v, NN)
                b_last = b[CHUNK - 1:CHUNK, :]
                st_ref[p] = st * jnp.exp(b_last) + _dot(v, (kk * jnp.exp(b_last - b)).astype(BF16), TN)
            return carry

        lax.fori_loop(0, n_chunk, chunk, 0)

    return pl.pallas_call(
        body, grid=(HG_HEADS // pair, s // HG_TB), in_specs=_hg_specs(lambda n: n, pair),
        out_specs=[pl.BlockSpec((HG_TB, pair * LANE), lambda h, n: (n, h)), pl.BlockSpec((pair, n_chunk, HG_K, HG_K), lambda h, n: (h, n, 0, 0))],
        out_shape=[_sds((s, HG_W), F32), _sds((HG_HEADS, s // CHUNK, HG_K, HG_K), BF16)],
        scratch_shapes=[pltpu.VMEM((pair, HG_K, HG_K), F32)],
        compiler_params=_params(2), name="hgrn_fwd")(proj, proj, proj, hl)


def _hgrn_bwd(proj, hl, states, d_o):
    s = proj.shape[0]
    n_chunk = HG_TB // CHUNK
    n_blk = s // HG_TB
    pair = HG_PAIR_BWD
    rev = lambda n: n_blk - 1 - n

    def body(q_ref, f_ref, i_ref, hl_ref, st_in_ref, do_ref, dq_ref, df_ref, di_ref, dhl_ref, dst_ref, dlb_ref):
        n = pl.program_id(1)

        @pl.when(n == 0)
        def _():
            dst_ref[...] = jnp.zeros_like(dst_ref)
            dlb_ref[...] = jnp.zeros_like(dlb_ref)

        r_i = lax.broadcasted_iota(jnp.int32, (CHUNK, CHUNK), 0)
        c_i = lax.broadcasted_iota(jnp.int32, (CHUNK, CHUNK), 1)
        tri_lower = (r_i >= c_i).astype(BF16)
        tri_upper = (r_i <= c_i).astype(BF16)
        row = lax.broadcasted_iota(jnp.int32, (CHUNK, 1), 0)

        def chunk(cc, carry):
            c = n_chunk - 1 - cc
            rows = pl.ds(pl.multiple_of(c * CHUNK, CHUNK), CHUNK)
            for p in range(pair):
                cols = slice(p * LANE, (p + 1) * LANE)
                lb = _lower_bound(hl_ref[:, cols])
                q_raw = q_ref[rows, cols]
                vb = i_ref[rows, cols].astype(BF16)
                sg, f, kk, sq, qs, b = _hg_gates(q_raw, f_ref[rows, cols], lb, tri_lower)
                e_b = jnp.exp(b)
                qe = qs * e_b
                b_last = b[CHUNK - 1:CHUNK, :]
                e_last = jnp.exp(b_last)
                e_kd = jnp.exp(b_last - b)
                kd = kk * e_kd
                st0 = st_in_ref[p, c]
                d_ob = do_ref[rows, cols].astype(BF16)
                dst = dst_ref[p]
                dst_b = dst.astype(BF16)
                d_a = jnp.where(r_i >= c_i, _dot(d_ob, vb, NT), 0.0)
                a, dqs, dkk = _hg_intra(qs, kk, b, d_a)
                d_v = _dot(a.astype(BF16), d_ob, TN) + _dot(kd.astype(BF16), dst_b, NT)
                d_kd = _dot(vb, dst_b, NN)
                dqs = dqs + _dot(d_ob, st0, NN) * e_b
                dkk = dkk + d_kd * e_kd
                d_b_last = jnp.sum(d_kd * kd, axis=0, keepdims=True) + jnp.sum(dst * st0.astype(F32), axis=0, keepdims=True) * e_last
                d_b = qs * dqs - kk * dkk + jnp.where(row == CHUNK - 1, d_b_last, 0.0)
                d_g = _tri_matmul(tri_upper, d_b)
                dst_ref[p] = _dot(d_ob, qe.astype(BF16), TN) + dst * e_last
                d_f = d_g / f - dkk
                dlb_ref[:, cols] += jnp.sum(d_f * (1.0 - sg), axis=0, keepdims=True)
                dq_ref[rows, cols] = (dqs * (sq * (1.0 + q_raw * (1.0 - sq)))).astype(dq_ref.dtype)
                df_ref[rows, cols] = (d_f * (1.0 - lb) * (sg * (1.0 - sg))).astype(df_ref.dtype)
                di_ref[rows, cols] = d_v.astype(di_ref.dtype)
            return carry

        lax.fori_loop(0, n_chunk, chunk, 0)

        @pl.when(n == n_blk - 1)
        def _():
            lb = _lower_bound(hl_ref[...])
            d_hl0 = dlb_ref[...] * (lb * (1.0 - lb))
            dhl_ref[...] = jnp.concatenate([d_hl0, -d_hl0], axis=0)

    out_blk = pl.BlockSpec((HG_TB, pair * LANE), lambda h, n: (rev(n), h))
    return pl.pallas_call(
        body, grid=(HG_HEADS // pair, n_blk),
        in_specs=_hg_specs(rev, pair) + [pl.BlockSpec((pair, n_chunk, HG_K, HG_K), lambda h, n: (h, rev(n), 0, 0)), out_blk],
        out_specs=[out_blk, out_blk, out_blk, pl.BlockSpec((2, pair * LANE), lambda h, n: (0, h))],
        out_shape=[_sds((s, HG_W), BF16)] * 3 + [_sds((2, HG_W), F32)],
        scratch_shapes=[pltpu.VMEM((pair, HG_K, HG_K), F32), pltpu.VMEM((1, pair * LANE), F32)],
        compiler_params=_params(2), name="hgrn_bwd")(proj, proj, proj, hl, states, d_o)


def _mod_part(c_all, w_shard, b_shard):
    n = w_shard.shape[1]
    tn = 512

    def body(c_ref, w_ref, b_ref, o_ref):
        o_ref[...] = _dot(c_ref[...].astype(BF16), w_ref[...].astype(BF16), NN) + b_ref[...]

    return pl.pallas_call(body, grid=(n // tn,),
                          in_specs=[pl.BlockSpec((N_DEV, D), lambda j: (0, 0)), pl.BlockSpec((D, tn), lambda j: (0, j)), pl.BlockSpec((1, tn), lambda j: (0, j))],
                          out_specs=pl.BlockSpec((N_DEV, tn), lambda j: (0, j)), out_shape=_sds((N_DEV, n), F32),
                          compiler_params=_params(1, 32 << 20), name="mod_part")(c_all, w_shard, b_shard)


def _grad_w_ada(c_all_t, dmod_cols):
    n = dmod_cols.shape[1]
    tn = 512

    def body(c_ref, d_ref, o_ref):
        cv = c_ref[...].astype(BF16).astype(F32)
        dv = d_ref[...].astype(BF16).astype(F32)
        acc = cv[:, 0:1] * dv[0:1, :]
        for k in range(1, N_DEV):
            acc = acc + cv[:, k:k + 1] * dv[k:k + 1, :]
        o_ref[...] = acc

    return pl.pallas_call(body, grid=(n // tn,),
                          in_specs=[pl.BlockSpec((D, N_DEV), lambda j: (0, 0)), pl.BlockSpec((N_DEV, tn), lambda j: (0, j))],
                          out_specs=pl.BlockSpec((D, tn), lambda j: (0, j)), out_shape=_sds((D, n), F32),
                          compiler_params=_params(1, 32 << 20), name="grad_w_ada")(c_all_t, dmod_cols)


def _row_tile(r, c):
    if r * c * 4 <= (1 << 20) or r % 8:
        return r
    best = 8
    for t in range(8, r + 1, 8):
        if r % t == 0 and t * c * 4 <= (1 << 20):
            best = t
    return best


def _sum_pieces(pieces, own, name):
    p, r, c = pieces.shape
    tr = _row_tile(r, c)

    def body(o_ref, p_ref, g_ref):
        g = o_ref[...].astype(F32)
        for k in range(p):
            g = g + p_ref[k].astype(F32)
        g_ref[...] = g

    blk = pl.BlockSpec((tr, c), lambda i: (i, 0))
    return pl.pallas_call(body, grid=(r // tr,), in_specs=[blk, pl.BlockSpec((p, tr, c), lambda i: (0, i, 0))], out_specs=blk,
                          out_shape=_sds((r, c), F32), compiler_params=_params(1), name=name)(own, pieces)


def _adamw(pieces, w, m, v, name, emit_grad=True, own=None):
    p, r, c = pieces.shape
    tr = _row_tile(r, c)
    c1 = 1.0 / (1.0 - ADAM_B1 ** ADAM_STEP)
    c2 = 1.0 / (1.0 - ADAM_B2 ** ADAM_STEP)

    def body(*refs):
        if own is None:
            p_ref, w_ref, m_ref, v_ref, *outs = refs
            g = p_ref[0].astype(F32)
        else:
            o_ref, p_ref, w_ref, m_ref, v_ref, *outs = refs
            g = o_ref[...].astype(F32) + p_ref[0].astype(F32)
        for k in range(1, p):
            g = g + p_ref[k].astype(F32)
        m2 = ADAM_B1 * m_ref[...] + (1.0 - ADAM_B1) * g
        v2 = ADAM_B2 * v_ref[...] + (1.0 - ADAM_B2) * (g * g)
        delta = -ADAM_LR * ((m2 * c1) / (jnp.sqrt(v2 * c2) + ADAM_EPS) + ADAM_WD * w_ref[...])
        if emit_grad:
            outs[0][...] = g
        outs[-3][...] = delta
        outs[-2][...] = m2
        outs[-1][...] = v2

    blk = pl.BlockSpec((tr, c), lambda i: (i, 0))
    n_out = 4 if emit_grad else 3
    lead = [] if own is None else [own]
    return pl.pallas_call(body, grid=(r // tr,), in_specs=[blk] * len(lead) + [pl.BlockSpec((p, tr, c), lambda i: (0, i, 0)), blk, blk, blk],
                          out_specs=[blk] * n_out, out_shape=[_sds((r, c), F32)] * n_out,
                          compiler_params=_params(1, 48 << 20), name=name)(*lead, pieces, w, m, v)


def _my_coords():
    return lax.axis_index("x"), lax.axis_index("y"), lax.axis_index("c")


def _flip(coords, k):
    x, y, c = coords
    return (1 - x if k & 4 else x, 1 - y if k & 2 else y, 1 - c if k & 1 else c)


def _lin(coords):
    return 4 * coords[0] + 2 * coords[1] + coords[2]


def _exchange_small(x3, bcast, name):
    n = x3.shape[2]

    def body(x_ref, o_ref, send_sems, recv_sems):
        me = _my_coords()
        my_id = _lin(me)
        o_ref[pl.ds(my_id, 1)] = x_ref[pl.ds(0 if bcast else my_id, 1)]
        copies = []
        for k in range(1, N_DEV):
            peer = _flip(me, k)
            src = x_ref.at[0 if bcast else _lin(peer)]
            cp = pltpu.make_async_remote_copy(src_ref=src, dst_ref=o_ref.at[my_id], send_sem=send_sems.at[k], recv_sem=recv_sems.at[k],
                                              device_id=peer, device_id_type=MESH)
            cp.start()
            copies.append(cp)
        for k in range(1, N_DEV):
            peer = _flip(me, k)
            pltpu.make_async_remote_copy(src_ref=x_ref.at[0], dst_ref=o_ref.at[_lin(peer)], send_sem=send_sems.at[k], recv_sem=recv_sems.at[k],
                                         device_id=peer, device_id_type=MESH).wait_recv()
        for cp in copies:
            cp.wait_send()

    vm = pl.BlockSpec(memory_space=pltpu.VMEM)
    return pl.pallas_call(body, in_specs=[vm], out_specs=vm, out_shape=_sds((N_DEV, 1, n), F32),
                          scratch_shapes=[pltpu.SemaphoreType.DMA((N_DEV,)), pltpu.SemaphoreType.DMA((N_DEV,))], name=name)(x3)


HBM_SPEC = pl.BlockSpec(memory_space=pltpu.HBM)
SEM_SPEC = pl.BlockSpec(memory_space=pltpu.SEMAPHORE)
ANY_SPEC = pl.BlockSpec(memory_space=pl.ANY)
DATAFLOW = pltpu.SideEffectType.DATAFLOW_SIDE_EFFECTING
GATHER_FLIPS = (1, 2, 4, 6)
PASS_FLIPS = (2, 4, 6)
TOKEN = (8, LANE)


def _hbm(t):
    return pltpu.with_memory_space_constraint(t, pltpu.HBM)


def _hbm_like(ts):
    return [pltpu.HBM(t.shape, t.dtype) for t in ts]


def _split_start(issue, srcs, lands, n_sem, name, deps=()):
    n, nd = len(srcs), len(deps)

    def body(*refs):
        issue(refs[:n], refs[n:2 * n], refs[2 * n + nd], refs[2 * n + nd + 1])
        refs[-1][...] = jnp.zeros(TOKEN, F32)

    outs = pl.pallas_call(
        body, name=name,
        out_shape=(pltpu.SemaphoreType.DMA((n_sem,)), pltpu.SemaphoreType.DMA((n_sem,)), *_hbm_like(srcs), *_hbm_like(lands), _sds(TOKEN, F32)),
        in_specs=[HBM_SPEC] * (2 * n) + [ANY_SPEC] * nd,
        out_specs=(SEM_SPEC, SEM_SPEC, *[HBM_SPEC] * (2 * n), pl.BlockSpec(memory_space=pltpu.VMEM)),
        input_output_aliases={i: 2 + i for i in range(2 * n)},
        compiler_params=pltpu.CompilerParams(has_side_effects=DATAFLOW))(*[_hbm(t) for t in srcs], *[_hbm(t) for t in lands], *deps)
    return dict(sems=outs[:2], thru=list(outs[2:2 + 2 * n]), token=outs[-1], n=n)


def _split_wait(finish, handle, after, name):
    n = handle["n"]
    thru = handle["thru"]

    def body(*refs):
        finish(refs[:n], refs[n:2 * n], refs[2 * n], refs[2 * n + 1])

    outs = pl.pallas_call(
        body, name=name, out_shape=_hbm_like(thru), in_specs=[HBM_SPEC] * (2 * n) + [SEM_SPEC, SEM_SPEC] + [ANY_SPEC] * len(after),
        out_specs=[HBM_SPEC] * (2 * n), input_output_aliases={i: i for i in range(2 * n)},
        compiler_params=pltpu.CompilerParams(has_side_effects=DATAFLOW))(*thru, *handle["sems"], *after)
    return list(outs[:n]), list(outs[n:])


def _gather_start(shards, name, deps=()):
    n = len(shards)
    my_id = _lin(_my_coords())
    lands = [lax.dynamic_update_slice(lax.empty((N_DEV,) + t.shape, t.dtype), t[None], (my_id, 0, 0)) for t in shards]

    def issue(src, land, send_sems, recv_sems):
        me = _my_coords()
        for w in range(n):
            for j, k in enumerate(GATHER_FLIPS):
                q = len(GATHER_FLIPS) * w + j
                pltpu.make_async_remote_copy(src_ref=src[w], dst_ref=land[w].at[_lin(me)], send_sem=send_sems.at[q], recv_sem=recv_sems.at[q],
                                             device_id=_flip(me, k), device_id_type=MESH).start()

    return _split_start(issue, shards, lands, len(GATHER_FLIPS) * n, name, deps)


def _gather_wait(handle, after, name):
    n = handle["n"]

    def finish(src, land, send_sems, recv_sems):
        me = _my_coords()
        for w in range(n):
            for j, k in enumerate(GATHER_FLIPS):
                q = len(GATHER_FLIPS) * w + j
                peer = _flip(me, k)
                cp = pltpu.make_async_remote_copy(src_ref=src[w], dst_ref=land[w].at[_lin(peer)], send_sem=send_sems.at[q], recv_sem=recv_sems.at[q],
                                                  device_id=peer, device_id_type=MESH)
                cp.wait_send()
                cp.wait_recv()

    return _split_wait(finish, handle, after, name)[1]


def _gather_pass(lands, name):
    n = len(lands)
    n_p = len(PASS_FLIPS)

    def body(*refs):
        land = refs[n:2 * n]
        send_sems, recv_sems = refs[2 * n:]
        me = _my_coords()
        sibling = _flip(me, 1)
        sent = []
        for w in range(n):
            for j, k in enumerate(PASS_FLIPS):
                blk = land[w].at[_lin(_flip(me, k))]
                cp = pltpu.make_async_remote_copy(src_ref=blk, dst_ref=blk, send_sem=send_sems.at[n_p * w + j], recv_sem=recv_sems.at[n_p * w + j],
                                                  device_id=sibling, device_id_type=MESH)
                cp.start()
                sent.append(cp)
        for w in range(n):
            for j, k in enumerate(PASS_FLIPS):
                blk = land[w].at[_lin(_flip(me, k + 1))]
                pltpu.make_async_remote_copy(src_ref=blk, dst_ref=blk, send_sem=send_sems.at[n_p * w + j], recv_sem=recv_sems.at[n_p * w + j],
                                             device_id=sibling, device_id_type=MESH).wait_recv()
        for cp in sent:
            cp.wait_send()

    return pl.pallas_call(body, in_specs=[ANY_SPEC] * n, out_specs=[ANY_SPEC] * n, out_shape=[_sds(t.shape, t.dtype) for t in lands],
                          input_output_aliases={i: i for i in range(n)},
                          scratch_shapes=[pltpu.SemaphoreType.DMA((n_p * n,)), pltpu.SemaphoreType.DMA((n_p * n,))], name=name)(*lands)


def _scatter_start(grads, name, deps=()):
    n = len(grads)
    lands = [lax.empty((N_DEV - 1,) + g.shape[1:], g.dtype) for g in grads]

    def issue(src, land, send_sems, recv_sems):
        me = _my_coords()
        for w in range(n):
            for k in range(1, N_DEV):
                q = (N_DEV - 1) * w + k - 1
                peer = _flip(me, k)
                pltpu.make_async_remote_copy(src_ref=src[w].at[_lin(peer)], dst_ref=land[w].at[k - 1], send_sem=send_sems.at[q], recv_sem=recv_sems.at[q],
                                             device_id=peer, device_id_type=MESH).start()

    return _split_start(issue, grads, lands, (N_DEV - 1) * n, name, deps)


def _scatter_wait(handle, after, name):
    n = handle["n"]

    def finish(src, land, send_sems, recv_sems):
        me = _my_coords()
        for w in range(n):
            for k in range(1, N_DEV):
                q = (N_DEV - 1) * w + k - 1
                peer = _flip(me, k)
                cp = pltpu.make_async_remote_copy(src_ref=src[w].at[_lin(peer)], dst_ref=land[w].at[k - 1], send_sem=send_sems.at[q], recv_sem=recv_sems.at[q],
                                                  device_id=peer, device_id_type=MESH)
                cp.wait_send()
                cp.wait_recv()

    return _split_wait(finish, handle, after, name)


CHIP_FLIPS = (0, 2, 4, 6)


def _pair_exchange(grads, name):
    n = len(grads)
    n_c = len(CHIP_FLIPS)

    def body(*refs):
        src, mine, theirs = refs[:n], refs[n:2 * n], refs[2 * n:3 * n]
        send_sems, recv_sems, local_sems = refs[3 * n:]
        me = _my_coords()
        sibling = _flip(me, 1)
        pending = []
        for w in range(n):
            for j, k in enumerate(CHIP_FLIPS):
                q = n_c * w + j
                keep = pltpu.make_async_copy(src[w].at[_lin(_flip(me, k))], mine[w].at[j], local_sems.at[q])
                keep.start()
                give = pltpu.make_async_remote_copy(src_ref=src[w].at[_lin(_flip(me, k + 1))], dst_ref=theirs[w].at[j], send_sem=send_sems.at[q],
                                                    recv_sem=recv_sems.at[q], device_id=sibling, device_id_type=MESH)
                give.start()
                pending.append((keep, give))
        for keep, give in pending:
            give.wait_recv()
        for keep, give in pending:
            give.wait_send()
            keep.wait()

    shapes = [_sds((n_c,) + g.shape[1:], g.dtype) for g in grads]
    outs = pl.pallas_call(body, in_specs=[ANY_SPEC] * n, out_specs=[ANY_SPEC] * (2 * n), out_shape=shapes + shapes,
                          scratch_shapes=[pltpu.SemaphoreType.DMA((n_c * n,))] * 3, name=name)(*grads)
    return list(outs[:n]), list(outs[n:])


def _pair_add(mine, theirs, name):
    p, r, c = mine.shape
    tr = _row_tile(p * r, c)

    def body(a_ref, b_ref, o_ref):
        o_ref[...] = (a_ref[...].astype(F32) + b_ref[...].astype(F32)).astype(o_ref.dtype)

    blk = pl.BlockSpec((tr, c), lambda i: (i, 0))
    out = pl.pallas_call(body, grid=(p * r // tr,), in_specs=[blk, blk], out_specs=blk, out_shape=_sds((p * r, c), mine.dtype),
                         compiler_params=_params(1), name=name)(mine.reshape(p * r, c), theirs.reshape(p * r, c))
    return out.reshape(p, r, c)


def _chips_start(parts, name, deps=()):
    n = len(parts)
    n_c = len(CHIP_FLIPS) - 1
    lands = [lax.empty((n_c,) + t.shape[1:], t.dtype) for t in parts]

    def issue(src, land, send_sems, recv_sems):
        me = _my_coords()
        for w in range(n):
            for j in range(1, n_c + 1):
                q = n_c * w + j - 1
                pltpu.make_async_remote_copy(src_ref=src[w].at[j], dst_ref=land[w].at[j - 1], send_sem=send_sems.at[q], recv_sem=recv_sems.at[q],
                                             device_id=_flip(me, CHIP_FLIPS[j]), device_id_type=MESH).start()

    return _split_start(issue, parts, lands, n_c * n, name, deps)


def _chips_wait(handle, after, name):
    n = handle["n"]
    n_c = len(CHIP_FLIPS) - 1

    def finish(src, land, send_sems, recv_sems):
        me = _my_coords()
        for w in range(n):
            for j in range(1, n_c + 1):
                q = n_c * w + j - 1
                cp = pltpu.make_async_remote_copy(src_ref=src[w].at[j], dst_ref=land[w].at[j - 1], send_sem=send_sems.at[q], recv_sem=recv_sems.at[q],
                                                  device_id=_flip(me, CHIP_FLIPS[j]), device_id_type=MESH)
                cp.wait_send()
                cp.wait_recv()

    return _split_wait(finish, handle, after, name)


def _after(t, *tokens):
    for tok in tokens:
        t = t + tok[0:1, 0:1]
    return t


def _rope_tables(positions):
    half = ROT // 2
    inv_freq = ROPE_THETA ** (-jnp.arange(0, ROT, 2, dtype=F32) / ROT)
    ang = positions.astype(F32).reshape(-1, 1) * inv_freq
    cos, sin = jnp.cos(ang), jnp.sin(ang)
    s = ang.shape[0]
    pad = jnp.zeros((s, HEAD_DIM - ROT), F32)
    zero = jnp.zeros((s, half), F32)
    two = lambda t: jnp.concatenate([t, t], axis=1)
    return (two(jnp.concatenate([cos, cos, pad + 1.0], axis=1)), two(jnp.concatenate([-sin, zero, pad], axis=1)),
            two(jnp.concatenate([zero, sin, pad], axis=1)))


def _local_step(x, tgt, tabs, mod, sinks_pad, hl, hg_norm, g_pre_mix, g_post_mix, g_pre_ffn, g_post_ffn, weights, scatter):
    s = x.shape[0]
    h1 = _pre_fwd(x, g_pre_mix, mod, 1, 0, "pre_mix_fwd")
    (w_in_t,) = weights("in", h1)
    proj = _mm_nt(h1, w_in_t, s, 256, D, F32, "proj_mm")
    att = _attn_fwd(proj, tabs, sinks_pad)
    o_raw, states = _hgrn_fwd(proj, hl)
    ohg = _hgout_fwd(o_raw, proj, hg_norm)
    w_attn_dm, w_hgrn_dm, w_out = weights("mix", ohg)
    y_a = _mm_nn_dm(att, w_attn_dm, s, F32, "attn_proj_mm")
    y_h = _mm_nn_dm(ohg, w_hgrn_dm, s, F32, "hgrn_proj_mm")
    merged = _merge_fwd(y_a, y_h, proj)
    y = _mm_nn(merged, w_out, s, 512, D, F32, "out_mm")
    x1 = _post_fwd(x, y, g_post_mix, mod, 2, "post_mix_fwd")
    h2 = _pre_fwd(x1, g_pre_ffn, mod, 4, 3, "pre_ffn_fwd")
    w_ffn_in_dm, w_ffn_out = weights("ffn", h2)
    gu = _mm_nn_dm(h2, w_ffn_in_dm, s // 2, F32, "ffn_in_mm")
    act = _swiglu_fwd(gu)
    y2 = _mm_nn(act, w_ffn_out, s, 512, FFN // 4, F32, "ffn_out_mm")
    err, loss = _post_fwd_loss(x1, y2, g_post_ffn, mod, 5, tgt, "post_ffn_loss")
    dy2, d_gate2, dg_post_ffn = _post_bwd(err, y2, g_post_ffn, mod, 5, "post_ffn_bwd")
    d_act = _mm_nt(dy2, w_ffn_out, s, 512, D, F32, "ffn_out_dx")
    gw_ffn_out = _mm_tn(act, dy2, 512, D, BF16, "ffn_out_dw")
    t_out = scatter([gw_ffn_out.reshape(N_DEV, FFN // N_DEV, D)], "ffn_out")
    dgu = _swiglu_bwd(d_act, gu)
    dh2 = _mm_nt_dm(dgu, w_ffn_in_dm, s, 512, F32, "ffn_in_dx")
    gw_ffn_in = _mm_tn_dm(h2, dgu, 512, BF16, "ffn_in_dw")
    t_in = scatter([gw_ffn_in], "ffn_in")
    mod = _after(mod, t_out, t_in)
    dx1, d_shift2, d_scale2, dg_pre_ffn = _pre_bwd(dh2, x1, err, g_pre_ffn, mod, 4, "pre_ffn_bwd")
    dy, d_gate1, dg_post_mix = _post_bwd(dx1, y, g_post_mix, mod, 2, "post_mix_bwd")
    d_merged = _mm_nt(dy, w_out, s, 512, D, F32, "out_dx")
    gw_out = _mm_tn(merged, dy, 512, D, BF16, "out_dw")
    dy_a, dy_h, d_gate_a, d_gate_h = _merge_bwd(d_merged, y_a, y_h, proj)
    d_att = _mm_nt_dm(dy_a, w_attn_dm, s, 512, F32, "attn_proj_dx")
    gw_attn = _mm_tn_dm(att, dy_a, 512, BF16, "attn_proj_dw")
    d_ohg = _mm_nt_dm(dy_h, w_hgrn_dm, s, 512, F32, "hgrn_proj_dx")
    gw_hgrn = _mm_tn_dm(ohg, dy_h, 512, BF16, "hgrn_proj_dw")
    t_mix = scatter([gw_attn, gw_hgrn, gw_out.reshape(N_DEV, D // N_DEV, D)], "mix")
    d_o, d_gh, d_hg_norm = _hgout_bwd(d_ohg, o_raw, proj, _after(hg_norm, t_mix))
    d_qh, d_fh, d_ih, d_hl = _hgrn_bwd(proj, hl, states, d_o)
    d_qa, d_ka, d_va, d_sinks = _attn_bwd(proj, tabs, sinks_pad, d_att)
    d_proj = jnp.concatenate([d_qa, d_ka.astype(BF16), d_va.astype(BF16), d_qh, d_fh, d_ih, d_gh, d_gate_a, d_gate_h], axis=1)
    dh1 = _mm_nn(d_proj, w_in_t, s // 2, 512, IN_COLS // 2, F32, "proj_dx")
    grad_x, d_shift1, d_scale1, dg_pre_mix = _pre_bwd(dh1, x, dx1, g_pre_mix, mod, 1, "pre_mix_bwd")
    d_mod = jnp.concatenate([d_shift1, d_scale1, d_gate1, d_shift2, d_scale2, d_gate2], axis=1)
    small = [d_mod, dg_pre_mix, dg_post_mix, dg_pre_ffn, dg_post_ffn, d_hl.reshape(1, 2 * HG_W), d_hg_norm, d_sinks]
    return loss, grad_x, small, h1, d_proj


def kernel(x, c, positions, w_ada, b_ada, g_pre_mix, g_post_mix, g_pre_ffn, g_post_ffn, w_in, attn_sinks, w_attn_proj, hg_lower_bounds, hg_norm, w_hgrn_proj, w_out, w_ffn_in, w_ffn_out, loss_target, m_w_ada, m_b_ada, m_g_pre_mix, m_g_post_mix, m_g_pre_ffn, m_g_post_ffn, m_w_in, m_attn_sinks, m_w_attn_proj, m_hg_lower_bounds, m_hg_norm, m_w_hgrn_proj, m_w_out, m_w_ffn_in, m_w_ffn_out, v_w_ada, v_b_ada, v_g_pre_mix, v_g_post_mix, v_g_pre_ffn, v_g_post_ffn, v_w_in, v_attn_sinks, v_w_attn_proj, v_hg_lower_bounds, v_hg_norm, v_w_hgrn_proj, v_w_out, v_w_ffn_in, v_w_ffn_out):
    my_id = _lin(_my_coords())
    s = x.shape[1]
    n_ada = w_ada.shape[2]

    c_all = _exchange_small(c.reshape(1, 1, D), True, "gather_c").reshape(N_DEV, D)
    groups = {"in": [w_in[0].T], "mix": [w_attn_proj[0], w_hgrn_proj[0], w_out[0]], "ffn": [w_ffn_in[0], w_ffn_out[0]]}

    def start(group, dep):
        shards, dep = lax.optimization_barrier((groups[group], dep))
        return _gather_start([t.astype(BF16) for t in shards], "gather_start_" + group, deps=[dep])

    gathers = {"in": start("in", c_all)}
    b_cols = _after(lax.dynamic_slice(b_ada, (0, my_id * n_ada), (1, n_ada)), gathers["in"]["token"])
    mod_part = _mod_part(c_all, w_ada[0], b_cols)
    mod = _exchange_small(mod_part.reshape(N_DEV, 1, n_ada), False, "scatter_mod").reshape(1, N_MOD * D)
    gathers["mix"] = start("mix", mod)
    gathers["ffn"] = start("ffn", gathers["mix"]["token"])
    mod = _after(mod, gathers["ffn"]["token"])

    def weights(group, after):
        lands = _gather_pass(_gather_wait(gathers[group], [after], "gather_wait_" + group), "gather_pass_" + group)
        if group == "in":
            return (lands[0].reshape(IN_COLS, D),)
        if group == "mix":
            return lands[0], lands[1], lands[2].reshape(D, D)
        return lands[0], lands[1].reshape(FFN, D)

    scatters = {}

    def scatter(grads, group):
        scatters[group] = _scatter_start(grads, "scatter_start_" + group)
        return scatters[group]["token"]

    sinks_pad = jnp.pad(attn_sinks, ((0, 0), (0, LANE - ATT_HEADS)))
    loss, grad_x, small, h1, d_proj = _local_step(
        x[0], loss_target[0], _rope_tables(positions), mod, sinks_pad, hg_lower_bounds, hg_norm, g_pre_mix, g_post_mix, g_pre_ffn, g_post_ffn,
        weights, scatter)
    loss = lax.psum(loss[0, 0], ("x", "y", "c"))

    sizes = [t.shape[1] for t in small]
    parts = _exchange_small(jnp.concatenate(small, axis=1).reshape(1, 1, sum(sizes)), True, "gather_small_grads")
    gw_in = _mm_tn(d_proj, h1, 256, D, BF16, "proj_dw", deps=[parts]).reshape(N_DEV, IN_COLS // N_DEV, D)
    mine, theirs = _pair_exchange([gw_in], "scatter_pair_in")
    scatters["in"] = _chips_start([_pair_add(mine[0], theirs[0], "scatter_pair_add_in")], "scatter_start_in")
    offs = [sum(sizes[:k]) for k in range(len(sizes))]
    piece = lambda k, n=None: parts[:, :, offs[k]:offs[k] + (sizes[k] if n is None else n)]
    small_w = [(piece(0), b_ada, m_b_ada, v_b_ada), (piece(1), g_pre_mix, m_g_pre_mix, v_g_pre_mix),
               (piece(2), g_post_mix, m_g_post_mix, v_g_post_mix), (piece(3), g_pre_ffn, m_g_pre_ffn, v_g_pre_ffn),
               (piece(4), g_post_ffn, m_g_post_ffn, v_g_post_ffn),
               (piece(5).reshape(N_DEV, 2, HG_W), hg_lower_bounds, m_hg_lower_bounds, v_hg_lower_bounds),
               (piece(6), hg_norm, m_hg_norm, v_hg_norm), (piece(7, ATT_HEADS), attn_sinks, m_attn_sinks, v_attn_sinks)]
    names = ["b_ada", "g_pre_mix", "g_post_mix", "g_pre_ffn", "g_post_ffn", "hg_lower_bounds", "hg_norm", "attn_sinks"]
    res = {n: _adamw(p, w, m, v, "adamw_" + n) for n, (p, w, m, v) in zip(names, small_w)}

    dmod_cols = lax.dynamic_slice(parts.reshape(N_DEV, -1), (0, my_id * n_ada), (N_DEV, n_ada))
    g_w_ada = _grad_w_ada(c_all.T, dmod_cols)
    res["w_ada"] = [g_w_ada] + list(_adamw(g_w_ada[None], w_ada[0], m_w_ada[0], v_w_ada[0], "adamw_w_ada", emit_grad=False))

    big = {"ffn_out": [("w_ffn_out", w_ffn_out, m_w_ffn_out, v_w_ffn_out)], "ffn_in": [("w_ffn_in", w_ffn_in, m_w_ffn_in, v_w_ffn_in)],
           "mix": [("w_attn_proj", w_attn_proj, m_w_attn_proj, v_w_attn_proj), ("w_hgrn_proj", w_hgrn_proj, m_w_hgrn_proj, v_w_hgrn_proj),
                   ("w_out", w_out, m_w_out, v_w_out)],
           "in": [("w_in", w_in, m_w_in, v_w_in)]}
    after = [scatters["in"]["token"]]
    for group, members in big.items():
        if group == "in":
            local, lands = _chips_wait(scatters[group], after + [res["w_ada"][1], res["b_ada"][1]], "scatter_wait_" + group)
            own = [t[0] for t in local]
        else:
            local, lands = _scatter_wait(scatters[group], after, "scatter_wait_" + group)
            own = [lax.dynamic_index_in_dim(t, my_id, 0, keepdims=False) for t in local]
        for (n, w, m, v), g_own, land in zip(members, own, lands):
            if group == "in":
                g_w = _sum_pieces(land, g_own, "sum_" + n).T
                res[n] = [g_w] + list(_adamw(g_w[None], w[0], m[0], v[0], "adamw_" + n, emit_grad=False))
            else:
                res[n] = _adamw(land, w[0], m[0], v[0], "adamw_" + n, own=g_own)
            after = [res[n][1]]

    order = ["w_ada", "b_ada", "g_pre_mix", "g_post_mix", "g_pre_ffn", "g_post_ffn", "w_in", "attn_sinks", "w_attn_proj",
             "hg_lower_bounds", "hg_norm", "w_hgrn_proj", "w_out", "w_ffn_in", "w_ffn_out"]
    lead = {"w_ada", "w_in", "w_attn_proj", "w_hgrn_proj", "w_out", "w_ffn_in", "w_ffn_out"}
    outs = [loss, grad_x[None]]
    for k in range(4):
        outs += [res[n][k][None] if n in lead else res[n][k] for n in order]
    return tuple(outs)
```

```python
import functools

import jax
import jax.numpy as jnp
from jax import lax
from jax.experimental import pallas as pl
from jax.experimental.pallas import tpu as pltpu

F32 = jnp.float32
BF16 = jnp.bfloat16

N_DEV = 8
D = 2048
ATT_HEADS = 16
KV_HEADS = 2
HEAD_DIM = 64
GROUP = ATT_HEADS // KV_HEADS
ATT_W = ATT_HEADS * HEAD_DIM
BLK = 128
ROT = HEAD_DIM // 4
ROPE_THETA = 500000.0
HG_HEADS = 8
HG_K = 128
HG_W = HG_HEADS * HG_K
CHUNK = 64
SUB = 16
FFN = 5632
N_MOD = 6
EPS = 1e-6
LANE = 128
Q_A, K_A, V_A, Q_H, F_H, I_H, G_H, GT_A, GT_H, IN_COLS = 0, 1024, 1152, 1280, 2304, 3328, 4352, 5376, 7424, 9472

ADAM_LR, ADAM_B1, ADAM_B2, ADAM_EPS, ADAM_WD, ADAM_STEP = 0.001, 0.9, 0.999, 1e-08, 0.01, 10

TR = 256
HG_TB = 512
VMEM_BIG = 56 << 20
MESH = pl.DeviceIdType.MESH


def _sds(shape, dtype):
    return jax.ShapeDtypeStruct(shape, dtype)


def _params(n_axes, vmem=None):
    return pltpu.CompilerParams(dimension_semantics=("arbitrary",) * n_axes, vmem_limit_bytes=vmem)


def _sig(t):
    return 1.0 / (1.0 + jnp.exp(-t))


def _dot(a, b, dims):
    return lax.dot_general(a, b, (dims, ((), ())), preferred_element_type=F32)


NN = ((1,), (0,))
NT = ((1,), (1,))
TN = ((0,), (0,))


def _matmul(a, b, a_spec, b_spec, o_spec, out_shape, grid, dims, acc_shape, name, deps=()):
    nk = grid[2]
    nd = len(deps)

    def body(a_ref, b_ref, *rest):
        o_ref, scratch = rest[nd], rest[nd + 1:]
        part = _dot(a_ref[...], b_ref[...], dims)
        if nk == 1:
            o_ref[...] = part.astype(o_ref.dtype)
        else:
            acc = scratch[0]
            k = pl.program_id(2)

            @pl.when(k == 0)
            def _():
                acc[...] = part

            @pl.when(k > 0)
            def _():
                acc[...] += part

            @pl.when(k == nk - 1)
            def _():
                o_ref[...] = acc[...].astype(o_ref.dtype)

    return pl.pallas_call(
        body, grid=grid, in_specs=[a_spec, b_spec] + [pl.BlockSpec(memory_space=pl.ANY)] * nd, out_specs=o_spec, out_shape=out_shape,
        scratch_shapes=[pltpu.VMEM(acc_shape, F32)] if nk > 1 else [],
        compiler_params=_params(3, VMEM_BIG), name=name)(a, b, *deps)


def _mm_nn(a, b, tm, tn, tk, out_dtype, name):
    m, k = a.shape
    n = b.shape[1]
    return _matmul(a, b, pl.BlockSpec((tm, tk), lambda j, i, kk: (i, kk)), pl.BlockSpec((tk, tn), lambda j, i, kk: (kk, j)),
                   pl.BlockSpec((tm, tn), lambda j, i, kk: (i, j)), _sds((m, n), out_dtype),
                   (n // tn, m // tm, k // tk), NN, (tm, tn), name)


def _mm_nn_dm(a, b, tm, out_dtype, name):
    m, k = a.shape
    n = b.shape[2]
    return _matmul(a, b, pl.BlockSpec((tm, k), lambda j, i, kk: (i, 0)), pl.BlockSpec((None, k, n), lambda j, i, kk: (j, 0, 0)),
                   pl.BlockSpec((tm, n), lambda j, i, kk: (i, j)), _sds((m, N_DEV * n), out_dtype),
                   (N_DEV, m // tm, 1), NN, (tm, n), name)


def _mm_nt(a, b, tm, tn, tk, out_dtype, name):
    m, k = a.shape
    n = b.shape[0]
    return _matmul(a, b, pl.BlockSpec((tm, tk), lambda j, i, kk: (i, kk)), pl.BlockSpec((tn, tk), lambda j, i, kk: (j, kk)),
                   pl.BlockSpec((tm, tn), lambda j, i, kk: (i, j)), _sds((m, n), out_dtype),
                   (n // tn, m // tm, k // tk), NT, (tm, tn), name)


def _mm_nt_dm(a, b, tm, tn, out_dtype, name):
    m = a.shape[0]
    n_out, n = b.shape[1], b.shape[2]
    return _matmul(a, b, pl.BlockSpec((tm, n), lambda j, i, kk: (i, kk)), pl.BlockSpec((None, tn, n), lambda j, i, kk: (kk, j, 0)),
                   pl.BlockSpec((tm, tn), lambda j, i, kk: (i, j)), _sds((m, n_out), out_dtype),
                   (n_out // tn, m // tm, N_DEV), NT, (tm, tn), name)


def _mm_tn(a, b, tm, tn, out_dtype, name, deps=()):
    s, m = a.shape
    n = b.shape[1]
    return _matmul(a, b, pl.BlockSpec((s, tm), lambda j, i, kk: (0, i)), pl.BlockSpec((s, tn), lambda j, i, kk: (0, j)),
                   pl.BlockSpec((tm, tn), lambda j, i, kk: (i, j)), _sds((m, n), out_dtype),
                   (n // tn, m // tm, 1), TN, (tm, tn), name, deps)


def _mm_tn_dm(a, b, tm, out_dtype, name):
    s, m = a.shape
    n = b.shape[1] // N_DEV
    return _matmul(a, b, pl.BlockSpec((s, tm), lambda j, i, kk: (0, i)), pl.BlockSpec((s, n), lambda j, i, kk: (0, j)),
                   pl.BlockSpec((None, tm, n), lambda j, i, kk: (j, i, 0)), _sds((N_DEV, m, n), out_dtype),
                   (N_DEV, m // tm, 1), TN, (tm, n), name)


def _row_spec():
    return pl.BlockSpec((TR, D), lambda i: (i, 0))


def _vec_spec(k=0):
    return pl.BlockSpec((1, D), lambda i: (0, k))


def _acc_rows(ref, first, val):
    @pl.when(first)
    def _():
        ref[...] = val

    @pl.when(jnp.logical_not(first))
    def _():
        ref[...] += val


def _pre_fwd(x, g, mod, k_scale, k_shift, name):
    s = x.shape[0]

    def body(x_ref, g_ref, sc_ref, sh_ref, h_ref):
        xv = x_ref[...]
        r = lax.rsqrt(jnp.mean(xv * xv, axis=-1, keepdims=True) + EPS)
        n = xv * r * g_ref[...]
        h_ref[...] = (n * (1.0 + sc_ref[...]) + sh_ref[...]).astype(h_ref.dtype)

    return pl.pallas_call(body, grid=(s // TR,), in_specs=[_row_spec(), _vec_spec(), _vec_spec(k_scale), _vec_spec(k_shift)],
                          out_specs=_row_spec(), out_shape=_sds((s, D), BF16), compiler_params=_params(1), name=name)(x, g, mod, mod)


def _post_fwd(x, y, g, mod, k_gate, name):
    s = x.shape[0]

    def body(x_ref, y_ref, g_ref, gt_ref, o_ref):
        yv = y_ref[...]
        r = lax.rsqrt(jnp.mean(yv * yv, axis=-1, keepdims=True) + EPS)
        o_ref[...] = x_ref[...] + gt_ref[...] * (yv * r * g_ref[...])

    return pl.pallas_call(body, grid=(s // TR,), in_specs=[_row_spec(), _row_spec(), _vec_spec(), _vec_spec(k_gate)],
                          out_specs=_row_spec(), out_shape=_sds((s, D), F32), compiler_params=_params(1), name=name)(x, y, g, mod)


def _post_fwd_loss(x, y, g, mod, k_gate, tgt, name):
    s = x.shape[0]

    def body(x_ref, y_ref, g_ref, gt_ref, t_ref, e_ref, loss_ref):
        i = pl.program_id(0)
        yv = y_ref[...]
        r = lax.rsqrt(jnp.mean(yv * yv, axis=-1, keepdims=True) + EPS)
        err = x_ref[...] + gt_ref[...] * (yv * r * g_ref[...]) - t_ref[...]
        e_ref[...] = err * (1.0 / D)
        part = 0.5 * jnp.sum(jnp.mean(err * err, axis=-1, keepdims=True), axis=0, keepdims=True)
        _acc_rows(loss_ref, i == 0, part)

    return pl.pallas_call(body, grid=(s // TR,),
                          in_specs=[_row_spec(), _row_spec(), _vec_spec(), _vec_spec(k_gate), _row_spec()],
                          out_specs=[_row_spec(), pl.BlockSpec((1, 1), lambda i: (0, 0))],
                          out_shape=[_sds((s, D), F32), _sds((1, 1), F32)], compiler_params=_params(1), name=name)(x, y, g, mod, tgt)


def _pre_bwd(dh, x, res, g, mod, k_scale, name):
    s = x.shape[0]

    def body(dh_ref, x_ref, res_ref, g_ref, sc_ref, dx_ref, dsh_ref, dsc_ref, dg_ref):
        first = pl.program_id(0) == 0
        xv, dh_v, gv = x_ref[...], dh_ref[...], g_ref[...]
        r = lax.rsqrt(jnp.mean(xv * xv, axis=-1, keepdims=True) + EPS)
        xh = xv * r
        dn = dh_v * (1.0 + sc_ref[...])
        dgn = dn * gv
        dx_ref[...] = res_ref[...] + r * (dgn - xh * jnp.mean(dgn * xh, axis=-1, keepdims=True))
        _acc_rows(dsh_ref, first, jnp.sum(dh_v, axis=0, keepdims=True))
        _acc_rows(dsc_ref, first, jnp.sum(dh_v * (xh * gv), axis=0, keepdims=True))
        _acc_rows(dg_ref, first, jnp.sum(dn * xh, axis=0, keepdims=True))

    return pl.pallas_call(body, grid=(s // TR,),
                          in_specs=[_row_spec(), _row_spec(), _row_spec(), _vec_spec(), _vec_spec(k_scale)],
                          out_specs=[_row_spec(), _vec_spec(), _vec_spec(), _vec_spec()],
                          out_shape=[_sds((s, D), F32)] + [_sds((1, D), F32)] * 3,
                          compiler_params=_params(1), name=name)(dh, x, res, g, mod)


def _post_bwd(dx, y, g, mod, k_gate, name):
    s = y.shape[0]

    def body(dx_ref, y_ref, g_ref, gt_ref, dy_ref, dgt_ref, dg_ref):
        first = pl.program_id(0) == 0
        yv, dxv, gv = y_ref[...], dx_ref[...], g_ref[...]
        r = lax.rsqrt(jnp.mean(yv * yv, axis=-1, keepdims=True) + EPS)
        yh = yv * r
        dn = dxv * gt_ref[...]
        dgn = dn * gv
        dy_ref[...] = (r * (dgn - yh * jnp.mean(dgn * yh, axis=-1, keepdims=True))).astype(dy_ref.dtype)
        _acc_rows(dgt_ref, first, jnp.sum(dxv * (yh * gv), axis=0, keepdims=True))
        _acc_rows(dg_ref, first, jnp.sum(dn * yh, axis=0, keepdims=True))

    return pl.pallas_call(body, grid=(s // TR,), in_specs=[_row_spec(), _row_spec(), _vec_spec(), _vec_spec(k_gate)],
                          out_specs=[_row_spec(), _vec_spec(), _vec_spec()],
                          out_shape=[_sds((s, D), BF16), _sds((1, D), F32), _sds((1, D), F32)],
                          compiler_params=_params(1), name=name)(dx, y, g, mod)


SW_TN = 1408
SW_TR = 512
TALL = 1024


def _swiglu_fwd(gu):
    s = gu.shape[0]
    nb = FFN // SW_TN

    def body(g_ref, u_ref, a_ref):
        gv = g_ref[...]
        a_ref[...] = (gv * _sig(gv) * u_ref[...]).astype(a_ref.dtype)

    return pl.pallas_call(body, grid=(s // SW_TR, nb),
                          in_specs=[pl.BlockSpec((SW_TR, SW_TN), lambda i, j: (i, j)), pl.BlockSpec((SW_TR, SW_TN), lambda i, j: (i, j + nb))],
                          out_specs=pl.BlockSpec((SW_TR, SW_TN), lambda i, j: (i, j)), out_shape=_sds((s, FFN), BF16),
                          compiler_params=_params(2, 48 << 20), name="swiglu_fwd")(gu, gu)


def _swiglu_bwd(dact, gu):
    s = gu.shape[0]
    nb = FFN // SW_TN

    def body(da_ref, g_ref, u_ref, o_ref):
        half = pl.program_id(2)
        gv, da = g_ref[...], da_ref[...]
        sg = _sig(gv)
        d_gate = da * u_ref[...] * (sg * (1.0 + gv * (1.0 - sg)))
        d_up = da * (gv * sg)
        o_ref[...] = jnp.where(half == 0, d_gate, d_up).astype(o_ref.dtype)

    blk = lambda f: pl.BlockSpec((SW_TR, SW_TN), f)
    return pl.pallas_call(body, grid=(s // SW_TR, nb, 2),
                          in_specs=[blk(lambda i, j, h: (i, j)), blk(lambda i, j, h: (i, j)), blk(lambda i, j, h: (i, j + nb))],
                          out_specs=blk(lambda i, j, h: (i, j + nb * h)), out_shape=_sds((s, 2 * FFN), BF16),
                          compiler_params=_params(3, 48 << 20), name="swiglu_bwd")(dact, gu, gu)


MG_TN = 256


def _merge_fwd(y_a, y_h, proj):
    s = y_a.shape[0]
    tn = MG_TN
    ba, bh = GT_A // tn, GT_H // tn

    def body(ya_ref, yh_ref, ga_ref, gh_ref, m_ref):
        m_ref[...] = (_sig(ga_ref[...]) * ya_ref[...] + _sig(gh_ref[...]) * yh_ref[...]).astype(m_ref.dtype)

    tr = min(s, TALL)
    blk = lambda f: pl.BlockSpec((tr, tn), f)
    return pl.pallas_call(body, grid=(s // tr, D // tn),
                          in_specs=[blk(lambda i, j: (i, j)), blk(lambda i, j: (i, j)), blk(lambda i, j: (i, j + ba)), blk(lambda i, j: (i, j + bh))],
                          out_specs=blk(lambda i, j: (i, j)), out_shape=_sds((s, D), BF16),
                          compiler_params=_params(2), name="merge_fwd")(y_a, y_h, proj, proj)


def _merge_bwd(dm, y_a, y_h, proj):
    s = y_a.shape[0]
    tn = MG_TN
    ba, bh = GT_A // tn, GT_H // tn

    def body(dm_ref, ya_ref, yh_ref, ga_ref, gh_ref, dya_ref, dyh_ref, dga_ref, dgh_ref):
        dmv = dm_ref[...]
        sa, sh = _sig(ga_ref[...]), _sig(gh_ref[...])
        dya_ref[...] = (dmv * sa).astype(BF16)
        dyh_ref[...] = (dmv * sh).astype(BF16)
        dga_ref[...] = (dmv * ya_ref[...] * (sa * (1.0 - sa))).astype(BF16)
        dgh_ref[...] = (dmv * yh_ref[...] * (sh * (1.0 - sh))).astype(BF16)

    tr = min(s, TALL)
    blk = lambda f: pl.BlockSpec((tr, tn), f)
    nat = blk(lambda i, j: (i, j))
    return pl.pallas_call(body, grid=(s // tr, D // tn),
                          in_specs=[nat, nat, nat, blk(lambda i, j: (i, j + ba)), blk(lambda i, j: (i, j + bh))],
                          out_specs=[nat] * 4, out_shape=[_sds((s, D), BF16)] * 4,
                          compiler_params=_params(2), name="merge_bwd")(dm, y_a, y_h, proj, proj)


def _hgout_fwd(o_raw, proj, hg_norm):
    s = o_raw.shape[0]
    bg = G_H // LANE

    def body(o_ref, g_ref, n_ref, out_ref):
        ov = o_ref[...]
        r = lax.rsqrt(jnp.mean(ov * ov, axis=-1, keepdims=True) + EPS)
        out_ref[...] = (ov * r * n_ref[...] * _sig(g_ref[...])).astype(out_ref.dtype)

    tr = min(s, TALL)
    blk = lambda f: pl.BlockSpec((tr, LANE), f)
    return pl.pallas_call(body, grid=(s // tr, HG_HEADS),
                          in_specs=[blk(lambda i, h: (i, h)), blk(lambda i, h: (i, h + bg)), pl.BlockSpec((1, LANE), lambda i, h: (0, 0))],
                          out_specs=blk(lambda i, h: (i, h)), out_shape=_sds((s, HG_W), BF16),
                          compiler_params=_params(2), name="hgout_fwd")(o_raw, proj, hg_norm)


def _hgout_bwd(d_out, o_raw, proj, hg_norm):
    s = o_raw.shape[0]
    bg = G_H // LANE

    def body(d_ref, o_ref, g_ref, n_ref, do_ref, dg_ref, dn_ref):
        first = jnp.logical_and(pl.program_id(0) == 0, pl.program_id(1) == 0)
        ov, dv, nv = o_ref[...], d_ref[...], n_ref[...]
        sg = _sig(g_ref[...])
        r = lax.rsqrt(jnp.mean(ov * ov, axis=-1, keepdims=True) + EPS)
        oh = ov * r
        d_on = dv * sg
        dg_ref[...] = (dv * (oh * nv) * (sg * (1.0 - sg))).astype(dg_ref.dtype)
        t = d_on * nv
        do_ref[...] = r * (t - oh * jnp.mean(t * oh, axis=-1, keepdims=True))
        _acc_rows(dn_ref, first, jnp.sum(d_on * oh, axis=0, keepdims=True))

    tr = min(s, TALL)
    blk = lambda f: pl.BlockSpec((tr, LANE), f)
    vec = pl.BlockSpec((1, LANE), lambda i, h: (0, 0))
    return pl.pallas_call(body, grid=(s // tr, HG_HEADS),
                          in_specs=[blk(lambda i, h: (i, h)), blk(lambda i, h: (i, h)), blk(lambda i, h: (i, h + bg)), vec],
                          out_specs=[blk(lambda i, h: (i, h)), blk(lambda i, h: (i, h)), vec],
                          out_shape=[_sds((s, HG_W), F32), _sds((s, HG_W), BF16), _sds((1, LANE), F32)],
                          compiler_params=_params(2), name="hgout_bwd")(d_out, o_raw, proj, hg_norm)


def _rope(t, cos, s_lo, s_hi):
    return t * cos + pltpu.roll(t, LANE - ROT // 2, 1) * s_lo + pltpu.roll(t, ROT // 2, 1) * s_hi


def _rope_wide(t, cos, s_lo, s_hi):
    return jnp.concatenate([_rope(t[:, k * LANE:(k + 1) * LANE], cos, s_lo, s_hi) for k in range(t.shape[1] // LANE)], axis=1)


def _attn_mask(has_prev):
    qi = lax.broadcasted_iota(jnp.int32, (BLK, 2 * BLK), 0)
    kj = lax.broadcasted_iota(jnp.int32, (BLK, 2 * BLK), 1)
    rel = BLK + qi - kj
    band = jnp.logical_and(rel >= 0, rel < BLK)
    return jnp.logical_and(band, jnp.logical_or(has_prev, kj >= BLK))


def _attn_specs():
    prev = lambda i: jnp.maximum(i - 1, 0)
    kb, vb = K_A // LANE, V_A // LANE
    blk = lambda f: pl.BlockSpec((BLK, LANE), f)
    tabs = [blk(lambda i: (i, 0))] * 3 + [blk(lambda i: (prev(i), 0))] * 3
    return [pl.BlockSpec((BLK, ATT_W), lambda i: (i, 0)), blk(lambda i: (i, kb)), blk(lambda i: (prev(i), kb)),
            blk(lambda i: (i, vb)), blk(lambda i: (prev(i), vb))] + tabs + [pl.BlockSpec((1, LANE), lambda i: (0, 0))]


def _attn_logits(qh, kg):
    return _dot(qh, kg, NT)


def _attn_probs(raw, mask, sk):
    logits = jnp.where(mask, raw * (HEAD_DIM ** -0.5), -jnp.inf)
    m = jnp.maximum(jnp.max(logits, axis=-1, keepdims=True), sk)
    p = jnp.exp(logits - m)
    e_sink = jnp.exp(sk - m)
    inv = 1.0 / (jnp.sum(p, axis=-1, keepdims=True) + e_sink)
    return p * inv, e_sink * inv


def _attn_fwd(proj, tabs, sinks):
    s = proj.shape[0]

    def body(q_ref, kc_ref, kp_ref, vc_ref, vp_ref, c0, l0, h0, c1, l1, h1, sk_ref, o_ref):
        i = pl.program_id(0)
        mask = _attn_mask(i > 0)
        q = _rope_wide(q_ref[...], c0[...], l0[...], h0[...]).astype(BF16)
        kk = jnp.concatenate([_rope(kp_ref[...], c1[...], l1[...], h1[...]), _rope(kc_ref[...], c0[...], l0[...], h0[...])], axis=0).astype(BF16)
        vv = jnp.concatenate([vp_ref[...], vc_ref[...]], axis=0).astype(BF16)
        head = lambda t, h: t[:, h * HEAD_DIM:(h + 1) * HEAD_DIM]
        raw = [_attn_logits(head(q, h), head(kk, h // GROUP)) for h in range(ATT_HEADS)]
        probs = [_attn_probs(raw[h], mask, sk_ref[:, h:h + 1])[0].astype(BF16) for h in range(ATT_HEADS)]
        outs = [_dot(probs[h], head(vv, h // GROUP), NN) for h in range(ATT_HEADS)]
        o_ref[...] = jnp.concatenate(outs, axis=1).astype(o_ref.dtype)

    return pl.pallas_call(body, grid=(s // BLK,), in_specs=_attn_specs(),
                          out_specs=pl.BlockSpec((BLK, ATT_W), lambda i: (i, 0)), out_shape=_sds((s, ATT_W), BF16),
                          compiler_params=_params(1), name="attn_fwd")(proj, proj, proj, proj, proj, *tabs, *tabs, sinks)


def _attn_bwd(proj, tabs, sinks, d_att):
    s = proj.shape[0]

    def body(q_ref, kc_ref, kp_ref, vc_ref, vp_ref, c0, l0, h0, c1, l1, h1, sk_ref, do_ref, dq_ref, dk_ref, dv_ref, ds_ref):
        i = pl.program_id(0)

        @pl.when(i == 0)
        def _():
            dk_ref[...] = jnp.zeros_like(dk_ref)
            dv_ref[...] = jnp.zeros_like(dv_ref)
            ds_ref[...] = jnp.zeros_like(ds_ref)

        mask = _attn_mask(i > 0)
        q = _rope_wide(q_ref[...], c0[...], l0[...], h0[...]).astype(BF16)
        kk = jnp.concatenate([_rope(kp_ref[...], c1[...], l1[...], h1[...]), _rope(kc_ref[...], c0[...], l0[...], h0[...])], axis=0).astype(BF16)
        vv = jnp.concatenate([vp_ref[...], vc_ref[...]], axis=0).astype(BF16)
        d_o = do_ref[...].astype(BF16)
        lane = lax.broadcasted_iota(jnp.int32, (1, LANE), 1)
        dqs, dks, dvs = [], [], []
        d_sink = jnp.zeros((1, LANE), F32)
        for g in range(KV_HEADS):
            kg, vg = kk[:, g * HEAD_DIM:(g + 1) * HEAD_DIM], vv[:, g * HEAD_DIM:(g + 1) * HEAD_DIM]
            dkg = jnp.zeros((2 * BLK, HEAD_DIM), F32)
            dvg = jnp.zeros((2 * BLK, HEAD_DIM), F32)
            for j in range(GROUP):
                h = g * GROUP + j
                qh, doh = q[:, h * HEAD_DIM:(h + 1) * HEAD_DIM], d_o[:, h * HEAD_DIM:(h + 1) * HEAD_DIM]
                prob, p_sink = _attn_probs(_attn_logits(qh, kg), mask, sk_ref[:, h:h + 1])
                d_p = _dot(doh, vg, NT)
                dd = jnp.sum(prob * d_p, axis=-1, keepdims=True)
                d_s = (prob * (d_p - dd)).astype(BF16)
                d_sink = d_sink + jnp.where(lane == h, -jnp.sum(p_sink * dd, axis=0, keepdims=True), 0.0)
                dqs.append(_dot(d_s, kg, NN) * (HEAD_DIM ** -0.5))
                dkg = dkg + _dot(d_s, qh, TN) * (HEAD_DIM ** -0.5)
                dvg = dvg + _dot(prob.astype(BF16), doh, TN)
            dks.append(dkg)
            dvs.append(dvg)
        dq_ref[...] = _rope_wide(jnp.concatenate(dqs, axis=1), c0[...], -l0[...], -h0[...]).astype(dq_ref.dtype)
        d_k = jnp.concatenate(dks, axis=1)
        d_v = jnp.concatenate(dvs, axis=1)
        cur = pl.ds(pl.multiple_of(i * BLK, BLK), BLK)
        prv = pl.ds(pl.multiple_of(jnp.maximum(i - 1, 0) * BLK, BLK), BLK)
        dk_ref[prv, :] += _rope(d_k[:BLK], c1[...], -l1[...], -h1[...])
        dk_ref[cur, :] += _rope(d_k[BLK:], c0[...], -l0[...], -h0[...])
        dv_ref[prv, :] += d_v[:BLK]
        dv_ref[cur, :] += d_v[BLK:]
        ds_ref[...] += d_sink

    full = pl.BlockSpec((s, LANE), lambda i: (0, 0))
    return pl.pallas_call(body, grid=(s // BLK,), in_specs=_attn_specs() + [pl.BlockSpec((BLK, ATT_W), lambda i: (i, 0))],
                          out_specs=[pl.BlockSpec((BLK, ATT_W), lambda i: (i, 0)), full, full, pl.BlockSpec((1, LANE), lambda i: (0, 0))],
                          out_shape=[_sds((s, ATT_W), BF16), _sds((s, LANE), F32), _sds((s, LANE), F32), _sds((1, LANE), F32)],
                          compiler_params=_params(1), name="attn_bwd")(proj, proj, proj, proj, proj, *tabs, *tabs, sinks, d_att)


def _tri_matmul(tri, t):
    hi = t.astype(BF16)
    r1 = t - hi.astype(F32)
    mid = r1.astype(BF16)
    lo = (r1 - mid.astype(F32)).astype(BF16)
    return _dot(tri, hi, NN) + _dot(tri, mid, NN) + _dot(tri, lo, NN)


def _lower_bound(hl):
    a, b = hl[0:1, :], hl[1:2, :]
    mx = jnp.maximum(a, b)
    ea, eb = jnp.exp(a - mx), jnp.exp(b - mx)
    return ea / (ea + eb)


def _hg_gates(q_raw, f_raw, lb, tri_lower):
    sg = _sig(f_raw)
    f = lb + (1.0 - lb) * sg
    sq = _sig(q_raw)
    b = _tri_matmul(tri_lower, jnp.log(f))
    return sg, f, 1.0 - f, sq, q_raw * sq, b


HG_PAIR_FWD = 2
HG_PAIR_BWD = 1


def _hg_specs(n_map, pair):
    blk = lambda off: pl.BlockSpec((HG_TB, pair * LANE), lambda h, n: (n_map(n), off // (pair * LANE) + h))
    return [blk(Q_H), blk(F_H), blk(I_H), pl.BlockSpec((2, pair * LANE), lambda h, n: (0, h))]


def _hg_spread():
    c = lax.broadcasted_iota(jnp.int32, (CHUNK, SUB * SUB), 0)
    l = lax.broadcasted_iota(jnp.int32, (CHUNK, SUB * SUB), 1)
    r = lax.broadcasted_iota(jnp.int32, (SUB, SUB * SUB), 0)
    lr = lax.broadcasted_iota(jnp.int32, (SUB, SUB * SUB), 1)
    cols = [(c == lo + (l >> 4)).astype(BF16) for lo in range(0, CHUNK, SUB)]
    tile = [(c == lo + (l & (SUB - 1))).astype(BF16) for lo in range(0, CHUNK, SUB)]
    return cols, tile, (lr & (SUB - 1)) == r, (lr >> 4) == r


def _hg_intra(qs, kk, b, grad=None):
    lane = lax.broadcasted_iota(jnp.int32, (SUB, CHUNK), 1)
    row1 = lax.broadcasted_iota(jnp.int32, (SUB, 1), 0)
    kk_b = kk.astype(BF16)
    if grad is not None:
        d_a, d_at, (cols, tile, diag, block) = grad
    a_blocks, dq_blocks, dk_blocks, db_blocks = [], [], [], []
    dk_left = None
    for j in range(CHUNK // SUB):
        lo = j * SUB
        q_j, k_j, b_j = qs[lo:lo + SUB], kk[lo:lo + SUB], b[lo:lo + SUB]
        es = [jnp.where(row1 >= sx, jnp.exp(jnp.minimum(b_j - b_j[sx:sx + 1], 0.0)), 0.0) for sx in range(SUB)]
        pes = [q_j * e for e in es]
        pe = jnp.concatenate(pes, axis=0).astype(BF16)
        pairs = _dot(pe, kk_b, NT)
        a_j = jnp.zeros((SUB, CHUNK), F32)
        for sx in range(SUB):
            a_j = jnp.where(lane == lo + sx, pairs[sx * SUB:(sx + 1) * SUB], a_j)
        if grad is not None:
            da_j = d_a[lo:lo + SUB]
            ek = jnp.concatenate([e * k_j[sx:sx + 1] for sx, e in enumerate(es)], axis=0).astype(BF16)
            sel_t = jnp.where(diag, _dot(da_j.astype(BF16), cols[j], NN), 0.0).astype(BF16)
            sel_s = jnp.where(block, _dot(d_at[lo:lo + SUB].astype(BF16), tile[j], NN), 0.0).astype(BF16)
            pek = jnp.concatenate([p * k_j[sx:sx + 1] for sx, p in enumerate(pes)], axis=0).astype(BF16)
            dq_j = _dot(sel_t, ek, NN)
            dk_j = _dot(sel_s, pe, NN)
            db_j = _dot(sel_t, pek, NN) - _dot(sel_s, pek, NN)
        if j > 0:
            ref = b[lo - 1:lo]
            sc_q = jnp.exp(b_j - ref)
            sc_k = jnp.exp(jnp.minimum(ref - b, 0.0))
            qt = (q_j * sc_q).astype(BF16)
            kt = (kk * sc_k).astype(BF16)
            a_j = a_j + jnp.where(lane < lo, _dot(qt, kt, NT), 0.0)
            if grad is not None:
                da_left = jnp.where(lane < lo, da_j, 0.0).astype(BF16)
                dq_left = _dot(da_left, kt, NN) * sc_q
                dq_j = dq_j + dq_left
                db_j = db_j + q_j * dq_left
                t = _dot(da_left, qt, TN) * sc_k
                dk_left = t if dk_left is None else dk_left + t
        a_blocks.append(a_j)
        if grad is not None:
            dq_blocks.append(dq_j)
            dk_blocks.append(dk_j)
            db_blocks.append(db_j)
    a = jnp.concatenate(a_blocks, axis=0)
    if grad is None:
        return a
    return a, jnp.concatenate(dq_blocks, axis=0), jnp.concatenate(dk_blocks, axis=0) + dk_left, jnp.concatenate(db_blocks, axis=0) - kk * dk_left


def _hgrn_fwd(proj, hl):
    s = proj.shape[0]
    n_chunk = HG_TB // CHUNK
    pair = HG_PAIR_FWD

    def body(q_ref, f_ref, i_ref, hl_ref, o_ref, st_out_ref, st_ref):
        @pl.when(pl.program_id(1) == 0)
        def _():
            st_ref[...] = jnp.zeros_like(st_ref)

        r_i = lax.broadcasted_iota(jnp.int32, (CHUNK, CHUNK), 0)
        c_i = lax.broadcasted_iota(jnp.int32, (CHUNK, CHUNK), 1)
        tri_lower = (r_i >= c_i).astype(BF16)

        def chunk(c, carry):
            rows = pl.ds(pl.multiple_of(c * CHUNK, CHUNK), CHUNK)
            for p in range(pair):
                cols = slice(p * LANE, (p + 1) * LANE)
                lb = _lower_bound(hl_ref[:, cols])
                v = i_ref[rows, cols].astype(BF16)
                _, _, kk, _, qs, b = _hg_gates(q_ref[rows, cols], f_ref[rows, cols], lb, tri_lower)
                a = _hg_intra(qs, kk, b)
                st = st_ref[p]
                st_b = st.astype(BF16)
                st_out_ref[p, c] = st_b
                o_ref[rows, cols] = _dot((qs * jnp.exp(b)).astype(BF16), st_b, NT) + _dot(a.astype(BF16), v, NN)
                b_last = b[CHUNK - 1:CHUNK, :]
                st_ref[p] = st * jnp.exp(b_last) + _dot(v, (kk * jnp.exp(b_last - b)).astype(BF16), TN)
            return carry

        lax.fori_loop(0, n_chunk, chunk, 0)

    return pl.pallas_call(
        body, grid=(HG_HEADS // pair, s // HG_TB), in_specs=_hg_specs(lambda n: n, pair),
        out_specs=[pl.BlockSpec((HG_TB, pair * LANE), lambda h, n: (n, h)), pl.BlockSpec((pair, n_chunk, HG_K, HG_K), lambda h, n: (h, n, 0, 0))],
        out_shape=[_sds((s, HG_W), F32), _sds((HG_HEADS, s // CHUNK, HG_K, HG_K), BF16)],
        scratch_shapes=[pltpu.VMEM((pair, HG_K, HG_K), F32)],
        compiler_params=_params(2), name="hgrn_fwd")(proj, proj, proj, hl)


def _hgrn_bwd(proj, hl, states, d_o):
    s = proj.shape[0]
    n_chunk = HG_TB // CHUNK
    n_blk = s // HG_TB
    pair = HG_PAIR_BWD
    rev = lambda n: n_blk - 1 - n

    def body(q_ref, f_ref, i_ref, hl_ref, st_in_ref, do_ref, dq_ref, df_ref, di_ref, dhl_ref, dst_ref, dlb_ref):
        n = pl.program_id(1)

        @pl.when(n == 0)
        def _():
            dst_ref[...] = jnp.zeros_like(dst_ref)
            dlb_ref[...] = jnp.zeros_like(dlb_ref)

        r_i = lax.broadcasted_iota(jnp.int32, (CHUNK, CHUNK), 0)
        c_i = lax.broadcasted_iota(jnp.int32, (CHUNK, CHUNK), 1)
        tri_lower = (r_i >= c_i).astype(BF16)
        tri_upper = (r_i <= c_i).astype(BF16)
        row = lax.broadcasted_iota(jnp.int32, (CHUNK, 1), 0)
        spread = _hg_spread()

        def chunk(cc, carry):
            c = n_chunk - 1 - cc
            rows = pl.ds(pl.multiple_of(c * CHUNK, CHUNK), CHUNK)
            for p in range(pair):
                cols = slice(p * LANE, (p + 1) * LANE)
                lb = _lower_bound(hl_ref[:, cols])
                q_raw = q_ref[rows, cols]
                vb = i_ref[rows, cols].astype(BF16)
                sg, f, kk, sq, qs, b = _hg_gates(q_raw, f_ref[rows, cols], lb, tri_lower)
                e_b = jnp.exp(b)
                qe = qs * e_b
                b_last = b[CHUNK - 1:CHUNK, :]
                e_last = jnp.exp(b_last)
                e_kd = jnp.exp(b_last - b)
                kd = kk * e_kd
                st0 = st_in_ref[p, c]
                d_ob = do_ref[rows, cols].astype(BF16)
                dst = dst_ref[p]
                dst_b = dst.astype(BF16)
                d_a = jnp.where(r_i >= c_i, _dot(d_ob, vb, NT), 0.0)
                d_at = jnp.where(r_i <= c_i, _dot(vb, d_ob, NT), 0.0)
                a, dqs, dkk, d_b = _hg_intra(qs, kk, b, (d_a, d_at, spread))
                d_v = _dot(a.astype(BF16), d_ob, TN) + _dot(kd.astype(BF16), dst_b, NT)
                d_kd = _dot(vb, dst_b, NN)
                dqs_st = _dot(d_ob, st0, NN) * e_b
                dkk_st = d_kd * e_kd
                dqs = dqs + dqs_st
                dkk = dkk + dkk_st
                d_b_last = jnp.sum(d_kd * kd, axis=0, keepdims=True) + jnp.sum(dst * st0.astype(F32), axis=0, keepdims=True) * e_last
                d_b = d_b + qs * dqs_st - kk * dkk_st + jnp.where(row == CHUNK - 1, d_b_last, 0.0)
                d_g = _tri_matmul(tri_upper, d_b)
                dst_ref[p] = _dot(d_ob, qe.astype(BF16), TN) + dst * e_last
                d_f = d_g / f - dkk
                dlb_ref[:, cols] += jnp.sum(d_f * (1.0 - sg), axis=0, keepdims=True)
                dq_ref[rows, cols] = (dqs * (sq * (1.0 + q_raw * (1.0 - sq)))).astype(dq_ref.dtype)
                df_ref[rows, cols] = (d_f * (1.0 - lb) * (sg * (1.0 - sg))).astype(df_ref.dtype)
                di_ref[rows, cols] = d_v.astype(di_ref.dtype)
            return carry

        lax.fori_loop(0, n_chunk, chunk, 0)

        @pl.when(n == n_blk - 1)
        def _():
            lb = _lower_bound(hl_ref[...])
            d_hl0 = dlb_ref[...] * (lb * (1.0 - lb))
            dhl_ref[...] = jnp.concatenate([d_hl0, -d_hl0], axis=0)

    out_blk = pl.BlockSpec((HG_TB, pair * LANE), lambda h, n: (rev(n), h))
    return pl.pallas_call(
        body, grid=(HG_HEADS // pair, n_blk),
        in_specs=_hg_specs(rev, pair) + [pl.BlockSpec((pair, n_chunk, HG_K, HG_K), lambda h, n: (h, rev(n), 0, 0)), out_blk],
        out_specs=[out_blk, out_blk, out_blk, pl.BlockSpec((2, pair * LANE), lambda h, n: (0, h))],
        out_shape=[_sds((s, HG_W), BF16)] * 3 + [_sds((2, HG_W), F32)],
        scratch_shapes=[pltpu.VMEM((pair, HG_K, HG_K), F32), pltpu.VMEM((1, pair * LANE), F32)],
        compiler_params=_params(2), name="hgrn_bwd")(proj, proj, proj, hl, states, d_o)


def _mod_part(c_all, w_shard, b_shard):
    n = w_shard.shape[1]
    tn = 512

    def body(c_ref, w_ref, b_ref, o_ref):
        o_ref[...] = _dot(c_ref[...].astype(BF16), w_ref[...].astype(BF16), NN) + b_ref[...]

    return pl.pallas_call(body, grid=(n // tn,),
                          in_specs=[pl.BlockSpec((N_DEV, D), lambda j: (0, 0)), pl.BlockSpec((D, tn), lambda j: (0, j)), pl.BlockSpec((1, tn), lambda j: (0, j))],
                          out_specs=pl.BlockSpec((N_DEV, tn), lambda j: (0, j)), out_shape=_sds((N_DEV, n), F32),
                          compiler_params=_params(1, 32 << 20), name="mod_part")(c_all, w_shard, b_shard)


def _grad_w_ada(c_all_t, dmod_cols):
    n = dmod_cols.shape[1]
    tn = 512

    def body(c_ref, d_ref, o_ref):
        cv = c_ref[...].astype(BF16).astype(F32)
        dv = d_ref[...].astype(BF16).astype(F32)
        acc = cv[:, 0:1] * dv[0:1, :]
        for k in range(1, N_DEV):
            acc = acc + cv[:, k:k + 1] * dv[k:k + 1, :]
        o_ref[...] = acc

    return pl.pallas_call(body, grid=(n // tn,),
                          in_specs=[pl.BlockSpec((D, N_DEV), lambda j: (0, 0)), pl.BlockSpec((N_DEV, tn), lambda j: (0, j))],
                          out_specs=pl.BlockSpec((D, tn), lambda j: (0, j)), out_shape=_sds((D, n), F32),
                          compiler_params=_params(1, 32 << 20), name="grad_w_ada")(c_all_t, dmod_cols)


def _row_tile(r, c):
    if r * c * 4 <= (1 << 20) or r % 8:
        return r
    best = 8
    for t in range(8, r + 1, 8):
        if r % t == 0 and t * c * 4 <= (1 << 20):
            best = t
    return best


def _sum_pieces(pieces, own, name):
    p, r, c = pieces.shape
    tr = _row_tile(r, c)

    def body(o_ref, p_ref, g_ref):
        g = o_ref[...].astype(F32)
        for k in range(p):
            g = g + p_ref[k].astype(F32)
        g_ref[...] = g

    blk = pl.BlockSpec((tr, c), lambda i: (i, 0))
    return pl.pallas_call(body, grid=(r // tr,), in_specs=[blk, pl.BlockSpec((p, tr, c), lambda i: (0, i, 0))], out_specs=blk,
                          out_shape=_sds((r, c), F32), compiler_params=_params(1), name=name)(own, pieces)


def _adamw(pieces, w, m, v, name, emit_grad=True, own=None):
    p, r, c = pieces.shape
    tr = _row_tile(r, c)
    c1 = 1.0 / (1.0 - ADAM_B1 ** ADAM_STEP)
    c2 = 1.0 / (1.0 - ADAM_B2 ** ADAM_STEP)

    def body(*refs):
        if own is None:
            p_ref, w_ref, m_ref, v_ref, *outs = refs
            g = p_ref[0].astype(F32)
        else:
            o_ref, p_ref, w_ref, m_ref, v_ref, *outs = refs
            g = o_ref[...].astype(F32) + p_ref[0].astype(F32)
        for k in range(1, p):
            g = g + p_ref[k].astype(F32)
        m2 = ADAM_B1 * m_ref[...] + (1.0 - ADAM_B1) * g
        v2 = ADAM_B2 * v_ref[...] + (1.0 - ADAM_B2) * (g * g)
        delta = -ADAM_LR * ((m2 * c1) / (jnp.sqrt(v2 * c2) + ADAM_EPS) + ADAM_WD * w_ref[...])
        if emit_grad:
            outs[0][...] = g
        outs[-3][...] = delta
        outs[-2][...] = m2
        outs[-1][...] = v2

    blk = pl.BlockSpec((tr, c), lambda i: (i, 0))
    n_out = 4 if emit_grad else 3
    lead = [] if own is None else [own]
    return pl.pallas_call(body, grid=(r // tr,), in_specs=[blk] * len(lead) + [pl.BlockSpec((p, tr, c), lambda i: (0, i, 0)), blk, blk, blk],
                          out_specs=[blk] * n_out, out_shape=[_sds((r, c), F32)] * n_out,
                          compiler_params=_params(1, 48 << 20), name=name)(*lead, pieces, w, m, v)


def _my_coords():
    return lax.axis_index("x"), lax.axis_index("y"), lax.axis_index("c")


def _flip(coords, k):
    x, y, c = coords
    return (1 - x if k & 4 else x, 1 - y if k & 2 else y, 1 - c if k & 1 else c)


def _lin(coords):
    return 4 * coords[0] + 2 * coords[1] + coords[2]


def _exchange_small(x3, bcast, name):
    n = x3.shape[2]

    def body(x_ref, o_ref, send_sems, recv_sems):
        me = _my_coords()
        my_id = _lin(me)
        o_ref[pl.ds(my_id, 1)] = x_ref[pl.ds(0 if bcast else my_id, 1)]
        copies = []
        for k in range(1, N_DEV):
            peer = _flip(me, k)
            src = x_ref.at[0 if bcast else _lin(peer)]
            cp = pltpu.make_async_remote_copy(src_ref=src, dst_ref=o_ref.at[my_id], send_sem=send_sems.at[k], recv_sem=recv_sems.at[k],
                                              device_id=peer, device_id_type=MESH)
            cp.start()
            copies.append(cp)
        for k in range(1, N_DEV):
            peer = _flip(me, k)
            pltpu.make_async_remote_copy(src_ref=x_ref.at[0], dst_ref=o_ref.at[_lin(peer)], send_sem=send_sems.at[k], recv_sem=recv_sems.at[k],
                                         device_id=peer, device_id_type=MESH).wait_recv()
        for cp in copies:
            cp.wait_send()

    vm = pl.BlockSpec(memory_space=pltpu.VMEM)
    return pl.pallas_call(body, in_specs=[vm], out_specs=vm, out_shape=_sds((N_DEV, 1, n), F32),
                          scratch_shapes=[pltpu.SemaphoreType.DMA((N_DEV,)), pltpu.SemaphoreType.DMA((N_DEV,))], name=name)(x3)


HBM_SPEC = pl.BlockSpec(memory_space=pltpu.HBM)
SEM_SPEC = pl.BlockSpec(memory_space=pltpu.SEMAPHORE)
ANY_SPEC = pl.BlockSpec(memory_space=pl.ANY)
DATAFLOW = pltpu.SideEffectType.DATAFLOW_SIDE_EFFECTING
GATHER_FLIPS = (1, 2, 4, 6)
PASS_FLIPS = (2, 4, 6)
TOKEN = (8, LANE)


def _hbm(t):
    return pltpu.with_memory_space_constraint(t, pltpu.HBM)


def _hbm_like(ts):
    return [pltpu.HBM(t.shape, t.dtype) for t in ts]


def _split_start(issue, srcs, lands, n_sem, name, deps=()):
    n, nd = len(srcs), len(deps)

    def body(*refs):
        issue(refs[:n], refs[n:2 * n], refs[2 * n + nd], refs[2 * n + nd + 1])
        refs[-1][...] = jnp.zeros(TOKEN, F32)

    outs = pl.pallas_call(
        body, name=name,
        out_shape=(pltpu.SemaphoreType.DMA((n_sem,)), pltpu.SemaphoreType.DMA((n_sem,)), *_hbm_like(srcs), *_hbm_like(lands), _sds(TOKEN, F32)),
        in_specs=[HBM_SPEC] * (2 * n) + [ANY_SPEC] * nd,
        out_specs=(SEM_SPEC, SEM_SPEC, *[HBM_SPEC] * (2 * n), pl.BlockSpec(memory_space=pltpu.VMEM)),
        input_output_aliases={i: 2 + i for i in range(2 * n)},
        compiler_params=pltpu.CompilerParams(has_side_effects=DATAFLOW))(*[_hbm(t) for t in srcs], *[_hbm(t) for t in lands], *deps)
    return dict(sems=outs[:2], thru=list(outs[2:2 + 2 * n]), token=outs[-1], n=n)


def _split_wait(finish, handle, after, name):
    n = handle["n"]
    thru = handle["thru"]

    def body(*refs):
        finish(refs[:n], refs[n:2 * n], refs[2 * n], refs[2 * n + 1])

    outs = pl.pallas_call(
        body, name=name, out_shape=_hbm_like(thru), in_specs=[HBM_SPEC] * (2 * n) + [SEM_SPEC, SEM_SPEC] + [ANY_SPEC] * len(after),
        out_specs=[HBM_SPEC] * (2 * n), input_output_aliases={i: i for i in range(2 * n)},
        compiler_params=pltpu.CompilerParams(has_side_effects=DATAFLOW))(*thru, *handle["sems"], *after)
    return list(outs[:n]), list(outs[n:])


def _gather_start(shards, name, deps=()):
    n = len(shards)
    my_id = _lin(_my_coords())
    lands = [lax.dynamic_update_slice(lax.empty((N_DEV,) + t.shape, t.dtype), t[None], (my_id, 0, 0)) for t in shards]

    def issue(src, land, send_sems, recv_sems):
        me = _my_coords()
        for w in range(n):
            for j, k in enumerate(GATHER_FLIPS):
                q = len(GATHER_FLIPS) * w + j
                pltpu.make_async_remote_copy(src_ref=src[w], dst_ref=land[w].at[_lin(me)], send_sem=send_sems.at[q], recv_sem=recv_sems.at[q],
                                             device_id=_flip(me, k), device_id_type=MESH).start()

    return _split_start(issue, shards, lands, len(GATHER_FLIPS) * n, name, deps)


def _gather_wait(handle, after, name):
    n = handle["n"]

    def finish(src, land, send_sems, recv_sems):
        me = _my_coords()
        for w in range(n):
            for j, k in enumerate(GATHER_FLIPS):
                q = len(GATHER_FLIPS) * w + j
                peer = _flip(me, k)
                cp = pltpu.make_async_remote_copy(src_ref=src[w], dst_ref=land[w].at[_lin(peer)], send_sem=send_sems.at[q], recv_sem=recv_sems.at[q],
                                                  device_id=peer, device_id_type=MESH)
                cp.wait_send()
                cp.wait_recv()

    return _split_wait(finish, handle, after, name)[1]


def _gather_pass(lands, name):
    n = len(lands)
    n_p = len(PASS_FLIPS)

    def body(*refs):
        land = refs[n:2 * n]
        send_sems, recv_sems = refs[2 * n:]
        me = _my_coords()
        sibling = _flip(me, 1)
        sent = []
        for w in range(n):
            for j, k in enumerate(PASS_FLIPS):
                blk = land[w].at[_lin(_flip(me, k))]
                cp = pltpu.make_async_remote_copy(src_ref=blk, dst_ref=blk, send_sem=send_sems.at[n_p * w + j], recv_sem=recv_sems.at[n_p * w + j],
                                                  device_id=sibling, device_id_type=MESH)
                cp.start()
                sent.append(cp)
        for w in range(n):
            for j, k in enumerate(PASS_FLIPS):
                blk = land[w].at[_lin(_flip(me, k + 1))]
                pltpu.make_async_remote_copy(src_ref=blk, dst_ref=blk, send_sem=send_sems.at[n_p * w + j], recv_sem=recv_sems.at[n_p * w + j],
                                             device_id=sibling, device_id_type=MESH).wait_recv()
        for cp in sent:
            cp.wait_send()

    return pl.pallas_call(body, in_specs=[ANY_SPEC] * n, out_specs=[ANY_SPEC] * n, out_shape=[_sds(t.shape, t.dtype) for t in lands],
                          input_output_aliases={i: i for i in range(n)},
                          scratch_shapes=[pltpu.SemaphoreType.DMA((n_p * n,)), pltpu.SemaphoreType.DMA((n_p * n,))], name=name)(*lands)


def _scatter_start(grads, name, deps=()):
    n = len(grads)
    lands = [lax.empty((N_DEV - 1,) + g.shape[1:], g.dtype) for g in grads]

    def issue(src, land, send_sems, recv_sems):
        me = _my_coords()
        for w in range(n):
            for k in range(1, N_DEV):
                q = (N_DEV - 1) * w + k - 1
                peer = _flip(me, k)
                pltpu.make_async_remote_copy(src_ref=src[w].at[_lin(peer)], dst_ref=land[w].at[k - 1], send_sem=send_sems.at[q], recv_sem=recv_sems.at[q],
                                             device_id=peer, device_id_type=MESH).start()

    return _split_start(issue, grads, lands, (N_DEV - 1) * n, name, deps)


def _scatter_wait(handle, after, name):
    n = handle["n"]

    def finish(src, land, send_sems, recv_sems):
        me = _my_coords()
        for w in range(n):
            for k in range(1, N_DEV):
                q = (N_DEV - 1) * w + k - 1
                peer = _flip(me, k)
                cp = pltpu.make_async_remote_copy(src_ref=src[w].at[_lin(peer)], dst_ref=land[w].at[k - 1], send_sem=send_sems.at[q], recv_sem=recv_sems.at[q],
                                                  device_id=peer, device_id_type=MESH)
                cp.wait_send()
                cp.wait_recv()

    return _split_wait(finish, handle, after, name)


CHIP_FLIPS = (0, 2, 4, 6)


def _pair_exchange(grads, name):
    n = len(grads)
    n_c = len(CHIP_FLIPS)

    def body(*refs):
        src, theirs = refs[:n], refs[n:2 * n]
        send_sems, recv_sems = refs[2 * n:]
        me = _my_coords()
        sibling = _flip(me, 1)
        sent = []
        for w in range(n):
            for j, k in enumerate(CHIP_FLIPS):
                q = n_c * w + j
                cp = pltpu.make_async_remote_copy(src_ref=src[w].at[_lin(_flip(me, k + 1))], dst_ref=theirs[w].at[j], send_sem=send_sems.at[q],
                                                  recv_sem=recv_sems.at[q], device_id=sibling, device_id_type=MESH)
                cp.start()
                sent.append(cp)
        for cp in sent:
            cp.wait_recv()
        for cp in sent:
            cp.wait_send()

    outs = pl.pallas_call(body, in_specs=[ANY_SPEC] * n, out_specs=[ANY_SPEC] * n, out_shape=[_sds((n_c,) + g.shape[1:], g.dtype) for g in grads],
                          scratch_shapes=[pltpu.SemaphoreType.DMA((n_c * n,))] * 2, name=name)(*grads)
    return list(outs)


def _pair_add(grad, theirs, name):
    p, r, c = theirs.shape
    tr = _row_tile(r, c)
    me = _my_coords()
    ids = jnp.stack([_lin(_flip(me, k)) for k in CHIP_FLIPS]).astype(jnp.int32)

    def body(ids_ref, a_ref, b_ref, o_ref):
        o_ref[...] = (a_ref[...].astype(F32) + b_ref[...].astype(F32)).astype(o_ref.dtype)

    blk = pl.BlockSpec((None, tr, c), lambda j, i, ids_ref: (j, i, 0))
    return pl.pallas_call(
        body, out_shape=_sds((p, r, c), theirs.dtype), compiler_params=_params(2), name=name,
        grid_spec=pltpu.PrefetchScalarGridSpec(
            num_scalar_prefetch=1, grid=(p, r // tr),
            in_specs=[pl.BlockSpec((None, tr, c), lambda j, i, ids_ref: (ids_ref[j], i, 0)), blk], out_specs=blk))(ids, grad, theirs)


def _chips_start(parts, name, deps=()):
    n = len(parts)
    n_c = len(CHIP_FLIPS) - 1
    lands = [lax.empty((n_c,) + t.shape[1:], t.dtype) for t in parts]

    def issue(src, land, send_sems, recv_sems):
        me = _my_coords()
        for w in range(n):
            for j in range(1, n_c + 1):
                q = n_c * w + j - 1
                pltpu.make_async_remote_copy(src_ref=src[w].at[j], dst_ref=land[w].at[j - 1], send_sem=send_sems.at[q], recv_sem=recv_sems.at[q],
                                             device_id=_flip(me, CHIP_FLIPS[j]), device_id_type=MESH).start()

    return _split_start(issue, parts, lands, n_c * n, name, deps)


def _chips_wait(handle, after, name):
    n = handle["n"]
    n_c = len(CHIP_FLIPS) - 1

    def finish(src, land, send_sems, recv_sems):
        me = _my_coords()
        for w in range(n):
            for j in range(1, n_c + 1):
                q = n_c * w + j - 1
                cp = pltpu.make_async_remote_copy(src_ref=src[w].at[j], dst_ref=land[w].at[j - 1], send_sem=send_sems.at[q], recv_sem=recv_sems.at[q],
                                                  device_id=_flip(me, CHIP_FLIPS[j]), device_id_type=MESH)
                cp.wait_send()
                cp.wait_recv()

    return _split_wait(finish, handle, after, name)


def _after(t, *tokens):
    for tok in tokens:
        t = t + tok[0:1, 0:1]
    return t


def _rope_tables(positions):
    half = ROT // 2
    inv_freq = ROPE_THETA ** (-jnp.arange(0, ROT, 2, dtype=F32) / ROT)
    ang = positions.astype(F32).reshape(-1, 1) * inv_freq
    cos, sin = jnp.cos(ang), jnp.sin(ang)
    s = ang.shape[0]
    pad = jnp.zeros((s, HEAD_DIM - ROT), F32)
    zero = jnp.zeros((s, half), F32)
    two = lambda t: jnp.concatenate([t, t], axis=1)
    return (two(jnp.concatenate([cos, cos, pad + 1.0], axis=1)), two(jnp.concatenate([-sin, zero, pad], axis=1)),
            two(jnp.concatenate([zero, sin, pad], axis=1)))


def _local_step(x, tgt, tabs, mod, sinks_pad, hl, hg_norm, g_pre_mix, g_post_mix, g_pre_ffn, g_post_ffn, weights, scatter):
    s = x.shape[0]
    h1 = _pre_fwd(x, g_pre_mix, mod, 1, 0, "pre_mix_fwd")
    (w_in_t,) = weights("in", h1)
    proj = _mm_nt(h1, w_in_t, s, 256, D, F32, "proj_mm")
    att = _attn_fwd(proj, tabs, sinks_pad)
    o_raw, states = _hgrn_fwd(proj, hl)
    ohg = _hgout_fwd(o_raw, proj, hg_norm)
    w_attn_dm, w_hgrn_dm, w_out = weights("mix", ohg)
    y_a = _mm_nn_dm(att, w_attn_dm, s, F32, "attn_proj_mm")
    y_h = _mm_nn_dm(ohg, w_hgrn_dm, s, F32, "hgrn_proj_mm")
    merged = _merge_fwd(y_a, y_h, proj)
    y = _mm_nn(merged, w_out, s, 512, D, F32, "out_mm")
    x1 = _post_fwd(x, y, g_post_mix, mod, 2, "post_mix_fwd")
    h2 = _pre_fwd(x1, g_pre_ffn, mod, 4, 3, "pre_ffn_fwd")
    w_ffn_in_dm, w_ffn_out = weights("ffn", h2)
    gu = _mm_nn_dm(h2, w_ffn_in_dm, s // 2, F32, "ffn_in_mm")
    act = _swiglu_fwd(gu)
    y2 = _mm_nn(act, w_ffn_out, s, 512, FFN // 4, F32, "ffn_out_mm")
    err, loss = _post_fwd_loss(x1, y2, g_post_ffn, mod, 5, tgt, "post_ffn_loss")
    dy2, d_gate2, dg_post_ffn = _post_bwd(err, y2, g_post_ffn, mod, 5, "post_ffn_bwd")
    d_act = _mm_nt(dy2, w_ffn_out, s, 512, D, F32, "ffn_out_dx")
    gw_ffn_out = _mm_tn(act, dy2, 512, D, BF16, "ffn_out_dw")
    t_out = scatter([gw_ffn_out.reshape(N_DEV, FFN // N_DEV, D)], "ffn_out")
    dgu = _swiglu_bwd(d_act, gu)
    dh2 = _mm_nt_dm(dgu, w_ffn_in_dm, s, 512, F32, "ffn_in_dx")
    gw_ffn_in = _mm_tn_dm(h2, dgu, 512, BF16, "ffn_in_dw")
    t_in = scatter([gw_ffn_in], "ffn_in")
    mod = _after(mod, t_out, t_in)
    dx1, d_shift2, d_scale2, dg_pre_ffn = _pre_bwd(dh2, x1, err, g_pre_ffn, mod, 4, "pre_ffn_bwd")
    dy, d_gate1, dg_post_mix = _post_bwd(dx1, y, g_post_mix, mod, 2, "post_mix_bwd")
    d_merged = _mm_nt(dy, w_out, s, 512, D, F32, "out_dx")
    gw_out = _mm_tn(merged, dy, 512, D, BF16, "out_dw")
    dy_a, dy_h, d_gate_a, d_gate_h = _merge_bwd(d_merged, y_a, y_h, proj)
    d_att = _mm_nt_dm(dy_a, w_attn_dm, s, 512, F32, "attn_proj_dx")
    gw_attn = _mm_tn_dm(att, dy_a, 512, BF16, "attn_proj_dw")
    d_ohg = _mm_nt_dm(dy_h, w_hgrn_dm, s, 512, F32, "hgrn_proj_dx")
    gw_hgrn = _mm_tn_dm(ohg, dy_h, 512, BF16, "hgrn_proj_dw")
    t_mix = scatter([gw_attn, gw_hgrn, gw_out.reshape(N_DEV, D // N_DEV, D)], "mix")
    d_o, d_gh, d_hg_norm = _hgout_bwd(d_ohg, o_raw, proj, _after(hg_norm, t_mix))
    d_qh, d_fh, d_ih, d_hl = _hgrn_bwd(proj, hl, states, d_o)
    d_qa, d_ka, d_va, d_sinks = _attn_bwd(proj, tabs, sinks_pad, d_att)
    d_proj = jnp.concatenate([d_qa, d_ka.astype(BF16), d_va.astype(BF16), d_qh, d_fh, d_ih, d_gh, d_gate_a, d_gate_h], axis=1)
    dh1 = _mm_nn(d_proj, w_in_t, s // 2, 512, IN_COLS // 2, F32, "proj_dx")
    grad_x, d_shift1, d_scale1, dg_pre_mix = _pre_bwd(dh1, x, dx1, g_pre_mix, mod, 1, "pre_mix_bwd")
    d_mod = jnp.concatenate([d_shift1, d_scale1, d_gate1, d_shift2, d_scale2, d_gate2], axis=1)
    small = [d_mod, dg_pre_mix, dg_post_mix, dg_pre_ffn, dg_post_ffn, d_hl.reshape(1, 2 * HG_W), d_hg_norm, d_sinks]
    return loss, grad_x, small, h1, d_proj


def kernel(x, c, positions, w_ada, b_ada, g_pre_mix, g_post_mix, g_pre_ffn, g_post_ffn, w_in, attn_sinks, w_attn_proj, hg_lower_bounds, hg_norm, w_hgrn_proj, w_out, w_ffn_in, w_ffn_out, loss_target, m_w_ada, m_b_ada, m_g_pre_mix, m_g_post_mix, m_g_pre_ffn, m_g_post_ffn, m_w_in, m_attn_sinks, m_w_attn_proj, m_hg_lower_bounds, m_hg_norm, m_w_hgrn_proj, m_w_out, m_w_ffn_in, m_w_ffn_out, v_w_ada, v_b_ada, v_g_pre_mix, v_g_post_mix, v_g_pre_ffn, v_g_post_ffn, v_w_in, v_attn_sinks, v_w_attn_proj, v_hg_lower_bounds, v_hg_norm, v_w_hgrn_proj, v_w_out, v_w_ffn_in, v_w_ffn_out):
    my_id = _lin(_my_coords())
    s = x.shape[1]
    n_ada = w_ada.shape[2]

    c_all = _exchange_small(c.reshape(1, 1, D), True, "gather_c").reshape(N_DEV, D)
    b_cols = lax.dynamic_slice(b_ada, (0, my_id * n_ada), (1, n_ada))
    mod_part = _mod_part(c_all, w_ada[0], b_cols)
    mod = _exchange_small(mod_part.reshape(N_DEV, 1, n_ada), False, "scatter_mod").reshape(1, N_MOD * D)
    groups = {"in": [w_in[0].T], "mix": [w_attn_proj[0], w_hgrn_proj[0], w_out[0]], "ffn": [w_ffn_in[0], w_ffn_out[0]]}

    def start(group, dep):
        shards, dep = lax.optimization_barrier((groups[group], dep))
        return _gather_start([t.astype(BF16) for t in shards], "gather_start_" + group, deps=[dep])

    gathers = {"in": start("in", mod)}
    gathers["mix"] = start("mix", gathers["in"]["token"])
    gathers["ffn"] = start("ffn", gathers["mix"]["token"])

    def weights(group, after):
        after = [after, gathers["ffn"]["token"]]
        lands = _gather_pass(_gather_wait(gathers[group], after, "gather_wait_" + group), "gather_pass_" + group)
        if group == "in":
            return (lands[0].reshape(IN_COLS, D),)
        if group == "mix":
            return lands[0], lands[1], lands[2].reshape(D, D)
        return lands[0], lands[1].reshape(FFN, D)

    scatters = {}

    def scatter(grads, group):
        scatters[group] = _scatter_start(grads, "scatter_start_" + group)
        return scatters[group]["token"]

    sinks_pad = jnp.pad(attn_sinks, ((0, 0), (0, LANE - ATT_HEADS)))
    loss, grad_x, small, h1, d_proj = _local_step(
        x[0], loss_target[0], _rope_tables(positions), mod, sinks_pad, hg_lower_bounds, hg_norm, g_pre_mix, g_post_mix, g_pre_ffn, g_post_ffn,
        weights, scatter)
    loss = lax.psum(loss[0, 0], ("x", "y", "c"))

    sizes = [t.shape[1] for t in small]
    parts = _exchange_small(jnp.concatenate(small, axis=1).reshape(1, 1, sum(sizes)), True, "gather_small_grads")
    gw_in = _mm_tn(d_proj, h1, 256, D, BF16, "proj_dw", deps=[parts]).reshape(N_DEV, IN_COLS // N_DEV, D)
    theirs = _pair_exchange([gw_in], "scatter_pair_in")
    scatters["in"] = _chips_start([_pair_add(gw_in, theirs[0], "scatter_pair_add_in")], "scatter_start_in")
    offs = [sum(sizes[:k]) for k in range(len(sizes))]
    piece = lambda k, n=None: parts[:, :, offs[k]:offs[k] + (sizes[k] if n is None else n)]
    small_w = [(piece(0), b_ada, m_b_ada, v_b_ada), (piece(1), g_pre_mix, m_g_pre_mix, v_g_pre_mix),
               (piece(2), g_post_mix, m_g_post_mix, v_g_post_mix), (piece(3), g_pre_ffn, m_g_pre_ffn, v_g_pre_ffn),
               (piece(4), g_post_ffn, m_g_post_ffn, v_g_post_ffn),
               (piece(5).reshape(N_DEV, 2, HG_W), hg_lower_bounds, m_hg_lower_bounds, v_hg_lower_bounds),
               (piece(6), hg_norm, m_hg_norm, v_hg_norm), (piece(7, ATT_HEADS), attn_sinks, m_attn_sinks, v_attn_sinks)]
    names = ["b_ada", "g_pre_mix", "g_post_mix", "g_pre_ffn", "g_post_ffn", "hg_lower_bounds", "hg_norm", "attn_sinks"]
    res = {n: _adamw(p, w, m, v, "adamw_" + n) for n, (p, w, m, v) in zip(names, small_w)}

    dmod_cols = lax.dynamic_slice(parts.reshape(N_DEV, -1), (0, my_id * n_ada), (N_DEV, n_ada))
    g_w_ada = _grad_w_ada(c_all.T, dmod_cols)
    res["w_ada"] = [g_w_ada] + list(_adamw(g_w_ada[None], w_ada[0], m_w_ada[0], v_w_ada[0], "adamw_w_ada", emit_grad=False))

    big = {"ffn_out": [("w_ffn_out", w_ffn_out, m_w_ffn_out, v_w_ffn_out)], "ffn_in": [("w_ffn_in", w_ffn_in, m_w_ffn_in, v_w_ffn_in)],
           "mix": [("w_attn_proj", w_attn_proj, m_w_attn_proj, v_w_attn_proj), ("w_hgrn_proj", w_hgrn_proj, m_w_hgrn_proj, v_w_hgrn_proj),
                   ("w_out", w_out, m_w_out, v_w_out)],
           "in": [("w_in", w_in, m_w_in, v_w_in)]}
    after = [scatters["in"]["token"]]
    for group, members in big.items():
        if group == "in":
            local, lands = _chips_wait(scatters[group], after + [res["w_ada"][1], res["b_ada"][1]], "scatter_wait_" + group)
            own = [t[0] for t in local]
        else:
            local, lands = _scatter_wait(scatters[group], after, "scatter_wait_" + group)
            own = [lax.dynamic_index_in_dim(t, my_id, 0, keepdims=False) for t in local]
        for (n, w, m, v), g_own, land in zip(members, own, lands):
            if group == "in":
                g_w = _sum_pieces(land, g_own, "sum_" + n).T
                res[n] = [g_w] + list(_adamw(g_w[None], w[0], m[0], v[0], "adamw_" + n, emit_grad=False))
            else:
                res[n] = _adamw(land, w[0], m[0], v[0], "adamw_" + n, own=g_own)
            after = [res[n][1]]

    order = ["w_ada", "b_ada", "g_pre_mix", "g_post_mix", "g_pre_ffn", "g_post_ffn", "w_in", "attn_sinks", "w_attn_proj",
             "hg_lower_bounds", "hg_norm", "w_hgrn_proj", "w_out", "w_ffn_in", "w_ffn_out"]
    lead = {"w_ada", "w_in", "w_attn_proj", "w_hgrn_proj", "w_out", "w_ffn_in", "w_ffn_out"}
    outs = [loss, grad_x[None]]
    for k in range(4):
        outs += [res[n][k][None] if n in lead else res[n][k] for n in order]
    return tuple(outs)
```

```python
import functools

import jax
import jax.numpy as jnp
from jax import lax
from jax.experimental import pallas as pl
from jax.experimental.pallas import tpu as pltpu

F32 = jnp.float32
BF16 = jnp.bfloat16

N_DEV = 8
D = 2048
ATT_HEADS = 16
KV_HEADS = 2
HEAD_DIM = 64
GROUP = ATT_HEADS // KV_HEADS
ATT_W = ATT_HEADS * HEAD_DIM
BLK = 128
ROT = HEAD_DIM // 4
ROPE_THETA = 500000.0
HG_HEADS = 8
HG_K = 128
HG_W = HG_HEADS * HG_K
CHUNK = 64
SUB = 16
FFN = 5632
N_MOD = 6
EPS = 1e-6
LANE = 128
Q_A, K_A, V_A, Q_H, F_H, I_H, G_H, GT_A, GT_H, IN_COLS = 0, 1024, 1152, 1280, 2304, 3328, 4352, 5376, 7424, 9472

ADAM_LR, ADAM_B1, ADAM_B2, ADAM_EPS, ADAM_WD, ADAM_STEP = 0.001, 0.9, 0.999, 1e-08, 0.01, 10

TR = 256
HG_TB = 512
VMEM_BIG = 56 << 20
MESH = pl.DeviceIdType.MESH


def _sds(shape, dtype):
    return jax.ShapeDtypeStruct(shape, dtype)


def _params(n_axes, vmem=None):
    return pltpu.CompilerParams(dimension_semantics=("arbitrary",) * n_axes, vmem_limit_bytes=vmem)


def _sig(t):
    return 1.0 / (1.0 + jnp.exp(-t))


def _dot(a, b, dims):
    return lax.dot_general(a, b, (dims, ((), ())), preferred_element_type=F32)


NN = ((1,), (0,))
NT = ((1,), (1,))
TN = ((0,), (0,))


def _matmul(a, b, a_spec, b_spec, o_spec, out_shape, grid, dims, acc_shape, name, deps=()):
    nk = grid[2]
    nd = len(deps)

    def body(a_ref, b_ref, *rest):
        o_ref, scratch = rest[nd], rest[nd + 1:]
        part = _dot(a_ref[...], b_ref[...], dims)
        if nk == 1:
            o_ref[...] = part.astype(o_ref.dtype)
        else:
            acc = scratch[0]
            k = pl.program_id(2)

            @pl.when(k == 0)
            def _():
                acc[...] = part

            @pl.when(k > 0)
            def _():
                acc[...] += part

            @pl.when(k == nk - 1)
            def _():
                o_ref[...] = acc[...].astype(o_ref.dtype)

    return pl.pallas_call(
        body, grid=grid, in_specs=[a_spec, b_spec] + [pl.BlockSpec(memory_space=pl.ANY)] * nd, out_specs=o_spec, out_shape=out_shape,
        scratch_shapes=[pltpu.VMEM(acc_shape, F32)] if nk > 1 else [],
        compiler_params=_params(3, VMEM_BIG), name=name)(a, b, *deps)


def _mm_nn(a, b, tm, tn, tk, out_dtype, name):
    m, k = a.shape
    n = b.shape[1]
    return _matmul(a, b, pl.BlockSpec((tm, tk), lambda j, i, kk: (i, kk)), pl.BlockSpec((tk, tn), lambda j, i, kk: (kk, j)),
                   pl.BlockSpec((tm, tn), lambda j, i, kk: (i, j)), _sds((m, n), out_dtype),
                   (n // tn, m // tm, k // tk), NN, (tm, tn), name)


def _mm_nn_dm(a, b, tm, out_dtype, name):
    m, k = a.shape
    n = b.shape[2]
    return _matmul(a, b, pl.BlockSpec((tm, k), lambda j, i, kk: (i, 0)), pl.BlockSpec((None, k, n), lambda j, i, kk: (j, 0, 0)),
                   pl.BlockSpec((tm, n), lambda j, i, kk: (i, j)), _sds((m, N_DEV * n), out_dtype),
                   (N_DEV, m // tm, 1), NN, (tm, n), name)


def _mm_nt(a, b, tm, tn, tk, out_dtype, name):
    m, k = a.shape
    n = b.shape[0]
    return _matmul(a, b, pl.BlockSpec((tm, tk), lambda j, i, kk: (i, kk)), pl.BlockSpec((tn, tk), lambda j, i, kk: (j, kk)),
                   pl.BlockSpec((tm, tn), lambda j, i, kk: (i, j)), _sds((m, n), out_dtype),
                   (n // tn, m // tm, k // tk), NT, (tm, tn), name)


def _mm_nt_dm(a, b, tm, tn, out_dtype, name):
    m = a.shape[0]
    n_out, n = b.shape[1], b.shape[2]
    return _matmul(a, b, pl.BlockSpec((tm, n), lambda j, i, kk: (i, kk)), pl.BlockSpec((None, tn, n), lambda j, i, kk: (kk, j, 0)),
                   pl.BlockSpec((tm, tn), lambda j, i, kk: (i, j)), _sds((m, n_out), out_dtype),
                   (n_out // tn, m // tm, N_DEV), NT, (tm, tn), name)


def _mm_tn(a, b, tm, tn, out_dtype, name, deps=()):
    s, m = a.shape
    n = b.shape[1]
    return _matmul(a, b, pl.BlockSpec((s, tm), lambda j, i, kk: (0, i)), pl.BlockSpec((s, tn), lambda j, i, kk: (0, j)),
                   pl.BlockSpec((tm, tn), lambda j, i, kk: (i, j)), _sds((m, n), out_dtype),
                   (n // tn, m // tm, 1), TN, (tm, tn), name, deps)


def _mm_tn_dm(a, b, tm, out_dtype, name):
    s, m = a.shape
    n = b.shape[1] // N_DEV
    return _matmul(a, b, pl.BlockSpec((s, tm), lambda j, i, kk: (0, i)), pl.BlockSpec((s, n), lambda j, i, kk: (0, j)),
                   pl.BlockSpec((None, tm, n), lambda j, i, kk: (j, i, 0)), _sds((N_DEV, m, n), out_dtype),
                   (N_DEV, m // tm, 1), TN, (tm, n), name)


def _row_spec():
    return pl.BlockSpec((TR, D), lambda i: (i, 0))


def _vec_spec(k=0):
    return pl.BlockSpec((1, D), lambda i: (0, k))


def _acc_rows(ref, first, val):
    @pl.when(first)
    def _():
        ref[...] = val

    @pl.when(jnp.logical_not(first))
    def _():
        ref[...] += val


def _pre_fwd(x, g, mod, k_scale, k_shift, name):
    s = x.shape[0]

    def body(x_ref, g_ref, sc_ref, sh_ref, h_ref):
        xv = x_ref[...]
        r = lax.rsqrt(jnp.mean(xv * xv, axis=-1, keepdims=True) + EPS)
        n = xv * r * g_ref[...]
        h_ref[...] = (n * (1.0 + sc_ref[...]) + sh_ref[...]).astype(h_ref.dtype)

    return pl.pallas_call(body, grid=(s // TR,), in_specs=[_row_spec(), _vec_spec(), _vec_spec(k_scale), _vec_spec(k_shift)],
                          out_specs=_row_spec(), out_shape=_sds((s, D), BF16), compiler_params=_params(1), name=name)(x, g, mod, mod)


def _post_fwd(x, y, g, mod, k_gate, name):
    s = x.shape[0]

    def body(x_ref, y_ref, g_ref, gt_ref, o_ref):
        yv = y_ref[...]
        r = lax.rsqrt(jnp.mean(yv * yv, axis=-1, keepdims=True) + EPS)
        o_ref[...] = x_ref[...] + gt_ref[...] * (yv * r * g_ref[...])

    return pl.pallas_call(body, grid=(s // TR,), in_specs=[_row_spec(), _row_spec(), _vec_spec(), _vec_spec(k_gate)],
                          out_specs=_row_spec(), out_shape=_sds((s, D), F32), compiler_params=_params(1), name=name)(x, y, g, mod)


def _post_fwd_loss(x, y, g, mod, k_gate, tgt, name):
    s = x.shape[0]

    def body(x_ref, y_ref, g_ref, gt_ref, t_ref, e_ref, loss_ref):
        i = pl.program_id(0)
        yv = y_ref[...]
        r = lax.rsqrt(jnp.mean(yv * yv, axis=-1, keepdims=True) + EPS)
        err = x_ref[...] + gt_ref[...] * (yv * r * g_ref[...]) - t_ref[...]
        e_ref[...] = err * (1.0 / D)
        part = 0.5 * jnp.sum(jnp.mean(err * err, axis=-1, keepdims=True), axis=0, keepdims=True)
        _acc_rows(loss_ref, i == 0, part)

    return pl.pallas_call(body, grid=(s // TR,),
                          in_specs=[_row_spec(), _row_spec(), _vec_spec(), _vec_spec(k_gate), _row_spec()],
                          out_specs=[_row_spec(), pl.BlockSpec((1, 1), lambda i: (0, 0))],
                          out_shape=[_sds((s, D), F32), _sds((1, 1), F32)], compiler_params=_params(1), name=name)(x, y, g, mod, tgt)


def _pre_bwd(dh, x, res, g, mod, k_scale, name):
    s = x.shape[0]

    def body(dh_ref, x_ref, res_ref, g_ref, sc_ref, dx_ref, dsh_ref, dsc_ref, dg_ref):
        first = pl.program_id(0) == 0
        xv, dh_v, gv = x_ref[...], dh_ref[...], g_ref[...]
        r = lax.rsqrt(jnp.mean(xv * xv, axis=-1, keepdims=True) + EPS)
        xh = xv * r
        dn = dh_v * (1.0 + sc_ref[...])
        dgn = dn * gv
        dx_ref[...] = res_ref[...] + r * (dgn - xh * jnp.mean(dgn * xh, axis=-1, keepdims=True))
        _acc_rows(dsh_ref, first, jnp.sum(dh_v, axis=0, keepdims=True))
        _acc_rows(dsc_ref, first, jnp.sum(dh_v * (xh * gv), axis=0, keepdims=True))
        _acc_rows(dg_ref, first, jnp.sum(dn * xh, axis=0, keepdims=True))

    return pl.pallas_call(body, grid=(s // TR,),
                          in_specs=[_row_spec(), _row_spec(), _row_spec(), _vec_spec(), _vec_spec(k_scale)],
                          out_specs=[_row_spec(), _vec_spec(), _vec_spec(), _vec_spec()],
                          out_shape=[_sds((s, D), F32)] + [_sds((1, D), F32)] * 3,
                          compiler_params=_params(1), name=name)(dh, x, res, g, mod)


def _post_bwd(dx, y, g, mod, k_gate, name):
    s = y.shape[0]

    def body(dx_ref, y_ref, g_ref, gt_ref, dy_ref, dgt_ref, dg_ref):
        first = pl.program_id(0) == 0
        yv, dxv, gv = y_ref[...], dx_ref[...], g_ref[...]
        r = lax.rsqrt(jnp.mean(yv * yv, axis=-1, keepdims=True) + EPS)
        yh = yv * r
        dn = dxv * gt_ref[...]
        dgn = dn * gv
        dy_ref[...] = (r * (dgn - yh * jnp.mean(dgn * yh, axis=-1, keepdims=True))).astype(dy_ref.dtype)
        _acc_rows(dgt_ref, first, jnp.sum(dxv * (yh * gv), axis=0, keepdims=True))
        _acc_rows(dg_ref, first, jnp.sum(dn * yh, axis=0, keepdims=True))

    return pl.pallas_call(body, grid=(s // TR,), in_specs=[_row_spec(), _row_spec(), _vec_spec(), _vec_spec(k_gate)],
                          out_specs=[_row_spec(), _vec_spec(), _vec_spec()],
                          out_shape=[_sds((s, D), BF16), _sds((1, D), F32), _sds((1, D), F32)],
                          compiler_params=_params(1), name=name)(dx, y, g, mod)


SW_TN = 1408
SW_TR = 512
TALL = 1024


def _swiglu_fwd(gu):
    s = gu.shape[0]
    nb = FFN // SW_TN

    def body(g_ref, u_ref, a_ref):
        gv = g_ref[...]
        a_ref[...] = (gv * _sig(gv) * u_ref[...]).astype(a_ref.dtype)

    return pl.pallas_call(body, grid=(s // SW_TR, nb),
                          in_specs=[pl.BlockSpec((SW_TR, SW_TN), lambda i, j: (i, j)), pl.BlockSpec((SW_TR, SW_TN), lambda i, j: (i, j + nb))],
                          out_specs=pl.BlockSpec((SW_TR, SW_TN), lambda i, j: (i, j)), out_shape=_sds((s, FFN), BF16),
                          compiler_params=_params(2, 48 << 20), name="swiglu_fwd")(gu, gu)


def _swiglu_bwd(dact, gu):
    s = gu.shape[0]
    nb = FFN // SW_TN

    def body(da_ref, g_ref, u_ref, o_ref):
        half = pl.program_id(2)
        gv, da = g_ref[...], da_ref[...]
        sg = _sig(gv)
        d_gate = da * u_ref[...] * (sg * (1.0 + gv * (1.0 - sg)))
        d_up = da * (gv * sg)
        o_ref[...] = jnp.where(half == 0, d_gate, d_up).astype(o_ref.dtype)

    blk = lambda f: pl.BlockSpec((SW_TR, SW_TN), f)
    return pl.pallas_call(body, grid=(s // SW_TR, nb, 2),
                          in_specs=[blk(lambda i, j, h: (i, j)), blk(lambda i, j, h: (i, j)), blk(lambda i, j, h: (i, j + nb))],
                          out_specs=blk(lambda i, j, h: (i, j + nb * h)), out_shape=_sds((s, 2 * FFN), BF16),
                          compiler_params=_params(3, 48 << 20), name="swiglu_bwd")(dact, gu, gu)


MG_TN = 256


def _merge_fwd(y_a, y_h, proj):
    s = y_a.shape[0]
    tn = MG_TN
    ba, bh = GT_A // tn, GT_H // tn

    def body(ya_ref, yh_ref, ga_ref, gh_ref, m_ref):
        m_ref[...] = (_sig(ga_ref[...]) * ya_ref[...] + _sig(gh_ref[...]) * yh_ref[...]).astype(m_ref.dtype)

    tr = min(s, TALL)
    blk = lambda f: pl.BlockSpec((tr, tn), f)
    return pl.pallas_call(body, grid=(s // tr, D // tn),
                          in_specs=[blk(lambda i, j: (i, j)), blk(lambda i, j: (i, j)), blk(lambda i, j: (i, j + ba)), blk(lambda i, j: (i, j + bh))],
                          out_specs=blk(lambda i, j: (i, j)), out_shape=_sds((s, D), BF16),
                          compiler_params=_params(2), name="merge_fwd")(y_a, y_h, proj, proj)


def _merge_bwd(dm, y_a, y_h, proj):
    s = y_a.shape[0]
    tn = MG_TN
    ba, bh = GT_A // tn, GT_H // tn

    def body(dm_ref, ya_ref, yh_ref, ga_ref, gh_ref, dya_ref, dyh_ref, dga_ref, dgh_ref):
        dmv = dm_ref[...]
        sa, sh = _sig(ga_ref[...]), _sig(gh_ref[...])
        dya_ref[...] = (dmv * sa).astype(BF16)
        dyh_ref[...] = (dmv * sh).astype(BF16)
        dga_ref[...] = (dmv * ya_ref[...] * (sa * (1.0 - sa))).astype(BF16)
        dgh_ref[...] = (dmv * yh_ref[...] * (sh * (1.0 - sh))).astype(BF16)

    tr = min(s, TALL)
    blk = lambda f: pl.BlockSpec((tr, tn), f)
    nat = blk(lambda i, j: (i, j))
    return pl.pallas_call(body, grid=(s // tr, D // tn),
                          in_specs=[nat, nat, nat, blk(lambda i, j: (i, j + ba)), blk(lambda i, j: (i, j + bh))],
                          out_specs=[nat] * 4, out_shape=[_sds((s, D), BF16)] * 4,
                          compiler_params=_params(2), name="merge_bwd")(dm, y_a, y_h, proj, proj)


def _hgout_fwd(o_raw, proj, hg_norm):
    s = o_raw.shape[0]
    bg = G_H // LANE

    def body(o_ref, g_ref, n_ref, out_ref):
        ov = o_ref[...]
        r = lax.rsqrt(jnp.mean(ov * ov, axis=-1, keepdims=True) + EPS)
        out_ref[...] = (ov * r * n_ref[...] * _sig(g_ref[...])).astype(out_ref.dtype)

    tr = min(s, TALL)
    blk = lambda f: pl.BlockSpec((tr, LANE), f)
    return pl.pallas_call(body, grid=(s // tr, HG_HEADS),
                          in_specs=[blk(lambda i, h: (i, h)), blk(lambda i, h: (i, h + bg)), pl.BlockSpec((1, LANE), lambda i, h: (0, 0))],
                          out_specs=blk(lambda i, h: (i, h)), out_shape=_sds((s, HG_W), BF16),
                          compiler_params=_params(2), name="hgout_fwd")(o_raw, proj, hg_norm)


def _hgout_bwd(d_out, o_raw, proj, hg_norm):
    s = o_raw.shape[0]
    bg = G_H // LANE

    def body(d_ref, o_ref, g_ref, n_ref, do_ref, dg_ref, dn_ref):
        first = jnp.logical_and(pl.program_id(0) == 0, pl.program_id(1) == 0)
        ov, dv, nv = o_ref[...], d_ref[...], n_ref[...]
        sg = _sig(g_ref[...])
        r = lax.rsqrt(jnp.mean(ov * ov, axis=-1, keepdims=True) + EPS)
        oh = ov * r
        d_on = dv * sg
        dg_ref[...] = (dv * (oh * nv) * (sg * (1.0 - sg))).astype(dg_ref.dtype)
        t = d_on * nv
        do_ref[...] = r * (t - oh * jnp.mean(t * oh, axis=-1, keepdims=True))
        _acc_rows(dn_ref, first, jnp.sum(d_on * oh, axis=0, keepdims=True))

    tr = min(s, TALL)
    blk = lambda f: pl.BlockSpec((tr, LANE), f)
    vec = pl.BlockSpec((1, LANE), lambda i, h: (0, 0))
    return pl.pallas_call(body, grid=(s // tr, HG_HEADS),
                          in_specs=[blk(lambda i, h: (i, h)), blk(lambda i, h: (i, h)), blk(lambda i, h: (i, h + bg)), vec],
                          out_specs=[blk(lambda i, h: (i, h)), blk(lambda i, h: (i, h)), vec],
                          out_shape=[_sds((s, HG_W), F32), _sds((s, HG_W), BF16), _sds((1, LANE), F32)],
                          compiler_params=_params(2), name="hgout_bwd")(d_out, o_raw, proj, hg_norm)


def _rope(t, cos, s_lo, s_hi):
    return t * cos + pltpu.roll(t, LANE - ROT // 2, 1) * s_lo + pltpu.roll(t, ROT // 2, 1) * s_hi


def _rope_wide(t, cos, s_lo, s_hi):
    return jnp.concatenate([_rope(t[:, k * LANE:(k + 1) * LANE], cos, s_lo, s_hi) for k in range(t.shape[1] // LANE)], axis=1)


def _attn_mask(has_prev):
    qi = lax.broadcasted_iota(jnp.int32, (BLK, 2 * BLK), 0)
    kj = lax.broadcasted_iota(jnp.int32, (BLK, 2 * BLK), 1)
    rel = BLK + qi - kj
    band = jnp.logical_and(rel >= 0, rel < BLK)
    return jnp.logical_and(band, jnp.logical_or(has_prev, kj >= BLK))


def _attn_specs():
    prev = lambda i: jnp.maximum(i - 1, 0)
    kb, vb = K_A // LANE, V_A // LANE
    blk = lambda f: pl.BlockSpec((BLK, LANE), f)
    tabs = [blk(lambda i: (i, 0))] * 3 + [blk(lambda i: (prev(i), 0))] * 3
    return [pl.BlockSpec((BLK, ATT_W), lambda i: (i, 0)), blk(lambda i: (i, kb)), blk(lambda i: (prev(i), kb)),
            blk(lambda i: (i, vb)), blk(lambda i: (prev(i), vb))] + tabs + [pl.BlockSpec((1, LANE), lambda i: (0, 0))]


def _attn_logits(qh, kg):
    return _dot(qh, kg, NT)


def _attn_probs(raw, mask, sk):
    logits = jnp.where(mask, raw * (HEAD_DIM ** -0.5), -jnp.inf)
    m = jnp.maximum(jnp.max(logits, axis=-1, keepdims=True), sk)
    p = jnp.exp(logits - m)
    e_sink = jnp.exp(sk - m)
    inv = 1.0 / (jnp.sum(p, axis=-1, keepdims=True) + e_sink)
    return p * inv, e_sink * inv


def _attn_fwd(proj, tabs, sinks):
    s = proj.shape[0]

    def body(q_ref, kc_ref, kp_ref, vc_ref, vp_ref, c0, l0, h0, c1, l1, h1, sk_ref, o_ref):
        i = pl.program_id(0)
        mask = _attn_mask(i > 0)
        q = _rope_wide(q_ref[...], c0[...], l0[...], h0[...]).astype(BF16)
        kk = jnp.concatenate([_rope(kp_ref[...], c1[...], l1[...], h1[...]), _rope(kc_ref[...], c0[...], l0[...], h0[...])], axis=0).astype(BF16)
        vv = jnp.concatenate([vp_ref[...], vc_ref[...]], axis=0).astype(BF16)
        part = lambda t, h: t[:, h * HEAD_DIM:(h + 1) * HEAD_DIM]

        def head(h):
            raw = _attn_logits(part(q, h), part(kk, h // GROUP))
            yield
            sk = sk_ref[:, h:h + 1]
            logits = jnp.where(mask, raw * (HEAD_DIM ** -0.5), -jnp.inf)
            m = jnp.maximum(jnp.max(logits, axis=-1, keepdims=True), sk)
            yield
            p = jnp.exp(logits - m)
            den = jnp.sum(p, axis=-1, keepdims=True) + jnp.exp(sk - m)
            yield
            out = _dot((p * (1.0 / den)).astype(BF16), part(vv, h // GROUP), NN)
            yield
            return out

        o_ref[...] = jnp.concatenate(_interleave([head(h) for h in range(ATT_HEADS)]), axis=1).astype(o_ref.dtype)

    return pl.pallas_call(body, grid=(s // BLK,), in_specs=_attn_specs(),
                          out_specs=pl.BlockSpec((BLK, ATT_W), lambda i: (i, 0)), out_shape=_sds((s, ATT_W), BF16),
                          compiler_params=_params(1), name="attn_fwd")(proj, proj, proj, proj, proj, *tabs, *tabs, sinks)


def _attn_bwd(proj, tabs, sinks, d_att):
    s = proj.shape[0]

    def body(q_ref, kc_ref, kp_ref, vc_ref, vp_ref, c0, l0, h0, c1, l1, h1, sk_ref, do_ref, dq_ref, dk_ref, dv_ref, ds_ref):
        i = pl.program_id(0)

        @pl.when(i == 0)
        def _():
            dk_ref[...] = jnp.zeros_like(dk_ref)
            dv_ref[...] = jnp.zeros_like(dv_ref)
            ds_ref[...] = jnp.zeros_like(ds_ref)

        mask = _attn_mask(i > 0)
        q = _rope_wide(q_ref[...], c0[...], l0[...], h0[...]).astype(BF16)
        kk = jnp.concatenate([_rope(kp_ref[...], c1[...], l1[...], h1[...]), _rope(kc_ref[...], c0[...], l0[...], h0[...])], axis=0).astype(BF16)
        vv = jnp.concatenate([vp_ref[...], vc_ref[...]], axis=0).astype(BF16)
        d_o = do_ref[...].astype(BF16)
        lane = lax.broadcasted_iota(jnp.int32, (1, LANE), 1)
        part = lambda t, h: t[:, h * HEAD_DIM:(h + 1) * HEAD_DIM]

        def head(h):
            kg, vg = part(kk, h // GROUP), part(vv, h // GROUP)
            qh, doh = part(q, h), part(d_o, h)
            raw = _attn_logits(qh, kg)
            d_p = _dot(doh, vg, NT)
            yield
            prob, p_sink = _attn_probs(raw, mask, sk_ref[:, h:h + 1])
            yield
            dd = jnp.sum(prob * d_p, axis=-1, keepdims=True)
            yield
            d_s = (prob * (d_p - dd)).astype(BF16)
            d_sink = jnp.where(lane == h, -jnp.sum(p_sink * dd, axis=0, keepdims=True), 0.0)
            dq = _dot(d_s, kg, NN)
            dk = _dot(d_s, qh, TN)
            dv = _dot(prob.astype(BF16), doh, TN)
            yield
            return dq * (HEAD_DIM ** -0.5), dk * (HEAD_DIM ** -0.5), dv, d_sink

        per_head = _interleave([head(h) for h in range(ATT_HEADS)])
        dqs = [t[0] for t in per_head]
        group_sum = lambda k, g: functools.reduce(jnp.add, [t[k] for t in per_head[g * GROUP:(g + 1) * GROUP]])
        dks = [group_sum(1, g) for g in range(KV_HEADS)]
        dvs = [group_sum(2, g) for g in range(KV_HEADS)]
        d_sink = functools.reduce(jnp.add, [t[3] for t in per_head])
        dq_ref[...] = _rope_wide(jnp.concatenate(dqs, axis=1), c0[...], -l0[...], -h0[...]).astype(dq_ref.dtype)
        d_k = jnp.concatenate(dks, axis=1)
        d_v = jnp.concatenate(dvs, axis=1)
        cur = pl.ds(pl.multiple_of(i * BLK, BLK), BLK)
        prv = pl.ds(pl.multiple_of(jnp.maximum(i - 1, 0) * BLK, BLK), BLK)
        dk_ref[prv, :] += _rope(d_k[:BLK], c1[...], -l1[...], -h1[...])
        dk_ref[cur, :] += _rope(d_k[BLK:], c0[...], -l0[...], -h0[...])
        dv_ref[prv, :] += d_v[:BLK]
        dv_ref[cur, :] += d_v[BLK:]
        ds_ref[...] += d_sink

    full = pl.BlockSpec((s, LANE), lambda i: (0, 0))
    return pl.pallas_call(body, grid=(s // BLK,), in_specs=_attn_specs() + [pl.BlockSpec((BLK, ATT_W), lambda i: (i, 0))],
                          out_specs=[pl.BlockSpec((BLK, ATT_W), lambda i: (i, 0)), full, full, pl.BlockSpec((1, LANE), lambda i: (0, 0))],
                          out_shape=[_sds((s, ATT_W), BF16), _sds((s, LANE), F32), _sds((s, LANE), F32), _sds((1, LANE), F32)],
                          compiler_params=_params(1), name="attn_bwd")(proj, proj, proj, proj, proj, *tabs, *tabs, sinks, d_att)


def _tri_matmul(tri, t):
    hi = t.astype(BF16)
    r1 = t - hi.astype(F32)
    mid = r1.astype(BF16)
    lo = (r1 - mid.astype(F32)).astype(BF16)
    return _dot(tri, hi, NN) + _dot(tri, mid, NN) + _dot(tri, lo, NN)


def _lower_bound(hl):
    a, b = hl[0:1, :], hl[1:2, :]
    mx = jnp.maximum(a, b)
    ea, eb = jnp.exp(a - mx), jnp.exp(b - mx)
    return ea / (ea + eb)


def _hg_gates(q_raw, f_raw, lb, tri_lower):
    sg = _sig(f_raw)
    f = lb + (1.0 - lb) * sg
    sq = _sig(q_raw)
    b = _tri_matmul(tri_lower, jnp.log(f))
    return sg, f, 1.0 - f, sq, q_raw * sq, b


HG_PAIR_FWD = 8
HG_PAIR_BWD = 8


def _hg_specs(n_map, pair):
    blk = lambda off, p: pl.BlockSpec((HG_TB, LANE), lambda h, n: (n_map(n), off // LANE + pair * h + p))
    return [blk(off, p) for off in (Q_H, F_H, I_H) for p in range(pair)] + [pl.BlockSpec((2, pair * LANE), lambda h, n: (0, h))]


def _interleave(gens):
    out = [None] * len(gens)
    live = list(range(len(gens)))
    while live:
        for k in list(live):
            try:
                next(gens[k])
            except StopIteration as stop:
                out[k] = stop.value
                live.remove(k)
    return out


def _hg_spread():
    c = lax.broadcasted_iota(jnp.int32, (CHUNK, SUB * SUB), 0)
    l = lax.broadcasted_iota(jnp.int32, (CHUNK, SUB * SUB), 1)
    r = lax.broadcasted_iota(jnp.int32, (SUB, SUB * SUB), 0)
    lr = lax.broadcasted_iota(jnp.int32, (SUB, SUB * SUB), 1)
    cols = [(c == lo + (l >> 4)).astype(BF16) for lo in range(0, CHUNK, SUB)]
    tile = [(c == lo + (l & (SUB - 1))).astype(BF16) for lo in range(0, CHUNK, SUB)]
    return cols, tile, (lr & (SUB - 1)) == r, (lr >> 4) == r


def _hg_intra(qs, kk, b, grad=None):
    lane = lax.broadcasted_iota(jnp.int32, (SUB, CHUNK), 1)
    row1 = lax.broadcasted_iota(jnp.int32, (SUB, 1), 0)
    kk_b = kk.astype(BF16)
    if grad is not None:
        d_a, d_at, (cols, tile, diag, block) = grad
    a_blocks, dq_blocks, dk_blocks, db_blocks = [], [], [], []
    dk_left = None
    for j in range(CHUNK // SUB):
        lo = j * SUB
        q_j, k_j, b_j = qs[lo:lo + SUB], kk[lo:lo + SUB], b[lo:lo + SUB]
        es = [jnp.where(row1 >= sx, jnp.exp(jnp.minimum(b_j - b_j[sx:sx + 1], 0.0)), 0.0) for sx in range(SUB)]
        pes = [q_j * e for e in es]
        pe = jnp.concatenate(pes, axis=0).astype(BF16)
        pairs = _dot(pe, kk_b, NT)
        yield
        a_j = jnp.zeros((SUB, CHUNK), F32)
        for sx in range(SUB):
            a_j = jnp.where(lane == lo + sx, pairs[sx * SUB:(sx + 1) * SUB], a_j)
        if grad is not None:
            da_j = d_a[lo:lo + SUB]
            ek = jnp.concatenate([e * k_j[sx:sx + 1] for sx, e in enumerate(es)], axis=0).astype(BF16)
            sel_t = jnp.where(diag, _dot(da_j.astype(BF16), cols[j], NN), 0.0).astype(BF16)
            sel_s = jnp.where(block, _dot(d_at[lo:lo + SUB].astype(BF16), tile[j], NN), 0.0).astype(BF16)
            pek = jnp.concatenate([p * k_j[sx:sx + 1] for sx, p in enumerate(pes)], axis=0).astype(BF16)
            yield
            dq_j = _dot(sel_t, ek, NN)
            dk_j = _dot(sel_s, pe, NN)
            db_j = _dot(sel_t, pek, NN) - _dot(sel_s, pek, NN)
            yield
        if j > 0:
            ref = b[lo - 1:lo]
            sc_q = jnp.exp(b_j - ref)
            sc_k = jnp.exp(jnp.minimum(ref - b, 0.0))
            qt = (q_j * sc_q).astype(BF16)
            kt = (kk * sc_k).astype(BF16)
            left = _dot(qt, kt, NT)
            yield
            a_j = a_j + jnp.where(lane < lo, left, 0.0)
            if grad is not None:
                da_left = jnp.where(lane < lo, da_j, 0.0).astype(BF16)
                dq_left = _dot(da_left, kt, NN) * sc_q
                dq_j = dq_j + dq_left
                db_j = db_j + q_j * dq_left
                t = _dot(da_left, qt, TN)
                yield
                t = t * sc_k
                dk_left = t if dk_left is None else dk_left + t
        a_blocks.append(a_j)
        if grad is not None:
            dq_blocks.append(dq_j)
            dk_blocks.append(dk_j)
            db_blocks.append(db_j)
    a = jnp.concatenate(a_blocks, axis=0)
    if grad is None:
        return a
    return a, jnp.concatenate(dq_blocks, axis=0), jnp.concatenate(dk_blocks, axis=0) + dk_left, jnp.concatenate(db_blocks, axis=0) - kk * dk_left


def _hgrn_fwd(proj, hl):
    s = proj.shape[0]
    n_chunk = HG_TB // CHUNK
    pair = HG_PAIR_FWD

    def body(*refs):
        q_refs, f_refs, i_refs = refs[:pair], refs[pair:2 * pair], refs[2 * pair:3 * pair]
        hl_ref, o_ref, st_out_ref, st_ref = refs[3 * pair:]

        @pl.when(pl.program_id(1) == 0)
        def _():
            st_ref[...] = jnp.zeros_like(st_ref)

        r_i = lax.broadcasted_iota(jnp.int32, (CHUNK, CHUNK), 0)
        c_i = lax.broadcasted_iota(jnp.int32, (CHUNK, CHUNK), 1)
        tri_lower = (r_i >= c_i).astype(BF16)

        def chunk(c, carry):
            rows = pl.ds(pl.multiple_of(c * CHUNK, CHUNK), CHUNK)
            def head(p):
                cols = slice(p * LANE, (p + 1) * LANE)
                lb = _lower_bound(hl_ref[:, cols])
                v = i_refs[p][rows, :].astype(BF16)
                _, _, kk, _, qs, b = _hg_gates(q_refs[p][rows, :], f_refs[p][rows, :], lb, tri_lower)
                yield
                st = st_ref[p]
                st_b = st.astype(BF16)
                st_out_ref[p, c] = st_b
                o_state = _dot((qs * jnp.exp(b)).astype(BF16), st_b, NT)
                b_last = b[CHUNK - 1:CHUNK, :]
                st_new = _dot(v, (kk * jnp.exp(b_last - b)).astype(BF16), TN)
                a = yield from _hg_intra(qs, kk, b)
                st_ref[p] = st * jnp.exp(b_last) + st_new
                o_ref[rows, cols] = o_state + _dot(a.astype(BF16), v, NN)

            _interleave([head(p) for p in range(pair)])
            return carry

        lax.fori_loop(0, n_chunk, chunk, 0)

    return pl.pallas_call(
        body, grid=(HG_HEADS // pair, s // HG_TB), in_specs=_hg_specs(lambda n: n, pair),
        out_specs=[pl.BlockSpec((HG_TB, pair * LANE), lambda h, n: (n, h)), pl.BlockSpec((pair, n_chunk, HG_K, HG_K), lambda h, n: (h, n, 0, 0))],
        out_shape=[_sds((s, HG_W), F32), _sds((HG_HEADS, s // CHUNK, HG_K, HG_K), BF16)],
        scratch_shapes=[pltpu.VMEM((pair, HG_K, HG_K), F32)],
        compiler_params=_params(2), name="hgrn_fwd")(*[proj] * (3 * pair), hl)


def _hgrn_bwd(proj, hl, states, d_o):
    s = proj.shape[0]
    n_chunk = HG_TB // CHUNK
    n_blk = s // HG_TB
    pair = HG_PAIR_BWD
    rev = lambda n: n_blk - 1 - n

    def body(*refs):
        q_refs, f_refs, i_refs = refs[:pair], refs[pair:2 * pair], refs[2 * pair:3 * pair]
        hl_ref, st_in_ref, do_ref, dq_ref, df_ref, di_ref, dhl_ref, dst_ref, dlb_ref = refs[3 * pair:]
        n = pl.program_id(1)

        @pl.when(n == 0)
        def _():
            dst_ref[...] = jnp.zeros_like(dst_ref)
            dlb_ref[...] = jnp.zeros_like(dlb_ref)

        r_i = lax.broadcasted_iota(jnp.int32, (CHUNK, CHUNK), 0)
        c_i = lax.broadcasted_iota(jnp.int32, (CHUNK, CHUNK), 1)
        tri_lower = (r_i >= c_i).astype(BF16)
        tri_upper = (r_i <= c_i).astype(BF16)
        row = lax.broadcasted_iota(jnp.int32, (CHUNK, 1), 0)
        spread = _hg_spread()

        def chunk(cc, carry):
            c = n_chunk - 1 - cc
            rows = pl.ds(pl.multiple_of(c * CHUNK, CHUNK), CHUNK)
            def head(p):
                cols = slice(p * LANE, (p + 1) * LANE)
                lb = _lower_bound(hl_ref[:, cols])
                q_raw = q_refs[p][rows, :]
                vb = i_refs[p][rows, :].astype(BF16)
                sg, f, kk, sq, qs, b = _hg_gates(q_raw, f_refs[p][rows, :], lb, tri_lower)
                yield
                e_b = jnp.exp(b)
                qe = qs * e_b
                b_last = b[CHUNK - 1:CHUNK, :]
                e_last = jnp.exp(b_last)
                e_kd = jnp.exp(b_last - b)
                kd = kk * e_kd
                st0 = st_in_ref[p, c]
                d_ob = do_ref[rows, cols].astype(BF16)
                dst = dst_ref[p]
                dst_b = dst.astype(BF16)
                d_a = jnp.where(r_i >= c_i, _dot(d_ob, vb, NT), 0.0)
                d_at = jnp.where(r_i <= c_i, _dot(vb, d_ob, NT), 0.0)
                d_v_st = _dot(kd.astype(BF16), dst_b, NT)
                d_kd = _dot(vb, dst_b, NN)
                d_qe = _dot(d_ob, st0, NN)
                dst_new = _dot(d_ob, qe.astype(BF16), TN)
                yield
                a, dqs, dkk, d_b = yield from _hg_intra(qs, kk, b, (d_a, d_at, spread))
                d_v = _dot(a.astype(BF16), d_ob, TN) + d_v_st
                dqs_st = d_qe * e_b
                dkk_st = d_kd * e_kd
                dqs = dqs + dqs_st
                dkk = dkk + dkk_st
                d_b_last = jnp.sum(d_kd * kd, axis=0, keepdims=True) + jnp.sum(dst * st0.astype(F32), axis=0, keepdims=True) * e_last
                d_b = d_b + qs * dqs_st - kk * dkk_st + jnp.where(row == CHUNK - 1, d_b_last, 0.0)
                d_g = _tri_matmul(tri_upper, d_b)
                dst_ref[p] = dst_new + dst * e_last
                yield
                d_f = d_g / f - dkk
                dlb_ref[:, cols] += jnp.sum(d_f * (1.0 - sg), axis=0, keepdims=True)
                dq_ref[rows, cols] = (dqs * (sq * (1.0 + q_raw * (1.0 - sq)))).astype(dq_ref.dtype)
                df_ref[rows, cols] = (d_f * (1.0 - lb) * (sg * (1.0 - sg))).astype(df_ref.dtype)
                di_ref[rows, cols] = d_v.astype(di_ref.dtype)

            _interleave([head(p) for p in range(pair)])
            return carry

        lax.fori_loop(0, n_chunk, chunk, 0)

        @pl.when(n == n_blk - 1)
        def _():
            lb = _lower_bound(hl_ref[...])
            d_hl0 = dlb_ref[...] * (lb * (1.0 - lb))
            dhl_ref[...] = jnp.concatenate([d_hl0, -d_hl0], axis=0)

    out_blk = pl.BlockSpec((HG_TB, pair * LANE), lambda h, n: (rev(n), h))
    return pl.pallas_call(
        body, grid=(HG_HEADS // pair, n_blk),
        in_specs=_hg_specs(rev, pair) + [pl.BlockSpec((pair, n_chunk, HG_K, HG_K), lambda h, n: (h, rev(n), 0, 0)), out_blk],
        out_specs=[out_blk, out_blk, out_blk, pl.BlockSpec((2, pair * LANE), lambda h, n: (0, h))],
        out_shape=[_sds((s, HG_W), BF16)] * 3 + [_sds((2, HG_W), F32)],
        scratch_shapes=[pltpu.VMEM((pair, HG_K, HG_K), F32), pltpu.VMEM((1, pair * LANE), F32)],
        compiler_params=_params(2), name="hgrn_bwd")(*[proj] * (3 * pair), hl, states, d_o)


def _mod_part(c_all, w_shard, b_shard):
    n = w_shard.shape[1]
    tn = 512

    def body(c_ref, w_ref, b_ref, o_ref):
        o_ref[...] = _dot(c_ref[...].astype(BF16), w_ref[...].astype(BF16), NN) + b_ref[...]

    return pl.pallas_call(body, grid=(n // tn,),
                          in_specs=[pl.BlockSpec((N_DEV, D), lambda j: (0, 0)), pl.BlockSpec((D, tn), lambda j: (0, j)), pl.BlockSpec((1, tn), lambda j: (0, j))],
                          out_specs=pl.BlockSpec((N_DEV, tn), lambda j: (0, j)), out_shape=_sds((N_DEV, n), F32),
                          compiler_params=_params(1, 32 << 20), name="mod_part")(c_all, w_shard, b_shard)


def _grad_w_ada(c_all_t, dmod_cols):
    n = dmod_cols.shape[1]
    tn = 512

    def body(c_ref, d_ref, o_ref):
        cv = c_ref[...].astype(BF16).astype(F32)
        dv = d_ref[...].astype(BF16).astype(F32)
        acc = cv[:, 0:1] * dv[0:1, :]
        for k in range(1, N_DEV):
            acc = acc + cv[:, k:k + 1] * dv[k:k + 1, :]
        o_ref[...] = acc

    return pl.pallas_call(body, grid=(n // tn,),
                          in_specs=[pl.BlockSpec((D, N_DEV), lambda j: (0, 0)), pl.BlockSpec((N_DEV, tn), lambda j: (0, j))],
                          out_specs=pl.BlockSpec((D, tn), lambda j: (0, j)), out_shape=_sds((D, n), F32),
                          compiler_params=_params(1, 32 << 20), name="grad_w_ada")(c_all_t, dmod_cols)


def _row_tile(r, c, max_elems=1 << 18):
    if r * c <= max_elems or r % 8:
        return r
    best = 8
    for t in range(8, r + 1, 8):
        if r % t == 0 and t * c <= max_elems:
            best = t
    return best


WIDE_TILE = 5 << 17


def _sum_pieces(pieces, own, name):
    p, r, c = pieces.shape
    tr = _row_tile(r, c, WIDE_TILE)

    def body(o_ref, p_ref, g_ref):
        g = o_ref[...].astype(F32)
        for k in range(p):
            g = g + p_ref[k].astype(F32)
        g_ref[...] = g

    blk = pl.BlockSpec((tr, c), lambda i: (i, 0))
    return pl.pallas_call(body, grid=(r // tr,), in_specs=[blk, pl.BlockSpec((p, tr, c), lambda i: (0, i, 0))], out_specs=blk,
                          out_shape=_sds((r, c), F32), compiler_params=_params(1), name=name)(own, pieces)


def _adamw(pieces, w, m, v, name, emit_grad=True, own=None):
    p, r, c = pieces.shape
    tr = _row_tile(r, c)
    c1 = 1.0 / (1.0 - ADAM_B1 ** ADAM_STEP)
    c2 = 1.0 / (1.0 - ADAM_B2 ** ADAM_STEP)

    def body(*refs):
        if own is None:
            p_ref, w_ref, m_ref, v_ref, *outs = refs
            g = p_ref[0].astype(F32)
        else:
            o_ref, p_ref, w_ref, m_ref, v_ref, *outs = refs
            g = o_ref[...].astype(F32) + p_ref[0].astype(F32)
        for k in range(1, p):
            g = g + p_ref[k].astype(F32)
        m2 = ADAM_B1 * m_ref[...] + (1.0 - ADAM_B1) * g
        v2 = ADAM_B2 * v_ref[...] + (1.0 - ADAM_B2) * (g * g)
        delta = -ADAM_LR * ((m2 * c1) / (jnp.sqrt(v2 * c2) + ADAM_EPS) + ADAM_WD * w_ref[...])
        if emit_grad:
            outs[0][...] = g
        outs[-3][...] = delta
        outs[-2][...] = m2
        outs[-1][...] = v2

    blk = pl.BlockSpec((tr, c), lambda i: (i, 0))
    n_out = 4 if emit_grad else 3
    lead = [] if own is None else [own]
    return pl.pallas_call(body, grid=(r // tr,), in_specs=[blk] * len(lead) + [pl.BlockSpec((p, tr, c), lambda i: (0, i, 0)), blk, blk, blk],
                          out_specs=[blk] * n_out, out_shape=[_sds((r, c), F32)] * n_out,
                          compiler_params=_params(1, 48 << 20), name=name)(*lead, pieces, w, m, v)


def _my_coords():
    return lax.axis_index("x"), lax.axis_index("y"), lax.axis_index("c")


def _flip(coords, k):
    x, y, c = coords
    return (1 - x if k & 4 else x, 1 - y if k & 2 else y, 1 - c if k & 1 else c)


def _lin(coords):
    return 4 * coords[0] + 2 * coords[1] + coords[2]


def _exchange_small(x3, bcast, name):
    n = x3.shape[2]

    def body(x_ref, o_ref, send_sems, recv_sems):
        me = _my_coords()
        my_id = _lin(me)
        o_ref[pl.ds(my_id, 1)] = x_ref[pl.ds(0 if bcast else my_id, 1)]
        copies = []
        for k in range(1, N_DEV):
            peer = _flip(me, k)
            src = x_ref.at[0 if bcast else _lin(peer)]
            cp = pltpu.make_async_remote_copy(src_ref=src, dst_ref=o_ref.at[my_id], send_sem=send_sems.at[k], recv_sem=recv_sems.at[k],
                                              device_id=peer, device_id_type=MESH)
            cp.start()
            copies.append(cp)
        for k in range(1, N_DEV):
            peer = _flip(me, k)
            pltpu.make_async_remote_copy(src_ref=x_ref.at[0], dst_ref=o_ref.at[_lin(peer)], send_sem=send_sems.at[k], recv_sem=recv_sems.at[k],
                                         device_id=peer, device_id_type=MESH).wait_recv()
        for cp in copies:
            cp.wait_send()

    vm = pl.BlockSpec(memory_space=pltpu.VMEM)
    return pl.pallas_call(body, in_specs=[vm], out_specs=vm, out_shape=_sds((N_DEV, 1, n), F32),
                          scratch_shapes=[pltpu.SemaphoreType.DMA((N_DEV,)), pltpu.SemaphoreType.DMA((N_DEV,))], name=name)(x3)


HBM_SPEC = pl.BlockSpec(memory_space=pltpu.HBM)
SEM_SPEC = pl.BlockSpec(memory_space=pltpu.SEMAPHORE)
ANY_SPEC = pl.BlockSpec(memory_space=pl.ANY)
DATAFLOW = pltpu.SideEffectType.DATAFLOW_SIDE_EFFECTING
GATHER_FLIPS = (1, 2, 4, 6)
PASS_FLIPS = (2, 4, 6)
TOKEN = (8, LANE)


def _hbm(t):
    return pltpu.with_memory_space_constraint(t, pltpu.HBM)


def _hbm_like(ts):
    return [pltpu.HBM(t.shape, t.dtype) for t in ts]


def _split_start(issue, srcs, lands, n_sem, name, deps=()):
    n, nd = len(srcs), len(deps)

    def body(*refs):
        issue(refs[:n], refs[n:2 * n], refs[2 * n + nd], refs[2 * n + nd + 1])
        refs[-1][...] = jnp.zeros(TOKEN, F32)

    outs = pl.pallas_call(
        body, name=name,
        out_shape=(pltpu.SemaphoreType.DMA((n_sem,)), pltpu.SemaphoreType.DMA((n_sem,)), *_hbm_like(srcs), *_hbm_like(lands), _sds(TOKEN, F32)),
        in_specs=[HBM_SPEC] * (2 * n) + [ANY_SPEC] * nd,
        out_specs=(SEM_SPEC, SEM_SPEC, *[HBM_SPEC] * (2 * n), pl.BlockSpec(memory_space=pltpu.VMEM)),
        input_output_aliases={i: 2 + i for i in range(2 * n)},
        compiler_params=pltpu.CompilerParams(has_side_effects=DATAFLOW))(*[_hbm(t) for t in srcs], *[_hbm(t) for t in lands], *deps)
    return dict(sems=outs[:2], thru=list(outs[2:2 + 2 * n]), token=outs[-1], n=n)


def _split_wait(finish, handle, after, name):
    n = handle["n"]
    thru = handle["thru"]

    def body(*refs):
        finish(refs[:n], refs[n:2 * n], refs[2 * n], refs[2 * n + 1])

    outs = pl.pallas_call(
        body, name=name, out_shape=_hbm_like(thru), in_specs=[HBM_SPEC] * (2 * n) + [SEM_SPEC, SEM_SPEC] + [ANY_SPEC] * len(after),
        out_specs=[HBM_SPEC] * (2 * n), input_output_aliases={i: i for i in range(2 * n)},
        compiler_params=pltpu.CompilerParams(has_side_effects=DATAFLOW))(*thru, *handle["sems"], *after)
    return list(outs[:n]), list(outs[n:])


def _gather_start(shards, name, deps=()):
    n = len(shards)
    my_id = _lin(_my_coords())
    lands = [lax.dynamic_update_slice(lax.empty((N_DEV,) + t.shape, t.dtype), t[None], (my_id, 0, 0)) for t in shards]

    def issue(src, land, send_sems, recv_sems):
        me = _my_coords()
        for w in range(n):
            for j, k in enumerate(GATHER_FLIPS):
                q = len(GATHER_FLIPS) * w + j
                pltpu.make_async_remote_copy(src_ref=src[w], dst_ref=land[w].at[_lin(me)], send_sem=send_sems.at[q], recv_sem=recv_sems.at[q],
                                             device_id=_flip(me, k), device_id_type=MESH).start()

    return _split_start(issue, shards, lands, len(GATHER_FLIPS) * n, name, deps)


def _gather_wait(handle, after, name):
    n = handle["n"]

    def finish(src, land, send_sems, recv_sems):
        me = _my_coords()
        for w in range(n):
            for j, k in enumerate(GATHER_FLIPS):
                q = len(GATHER_FLIPS) * w + j
                peer = _flip(me, k)
                cp = pltpu.make_async_remote_copy(src_ref=src[w], dst_ref=land[w].at[_lin(peer)], send_sem=send_sems.at[q], recv_sem=recv_sems.at[q],
                                                  device_id=peer, device_id_type=MESH)
                cp.wait_send()
                cp.wait_recv()

    return _split_wait(finish, handle, after, name)[1]


def _gather_pass(lands, name):
    n = len(lands)
    n_p = len(PASS_FLIPS)

    def body(*refs):
        land = refs[n:2 * n]
        send_sems, recv_sems = refs[2 * n:]
        me = _my_coords()
        sibling = _flip(me, 1)
        sent = []
        for w in range(n):
            for j, k in enumerate(PASS_FLIPS):
                blk = land[w].at[_lin(_flip(me, k))]
                cp = pltpu.make_async_remote_copy(src_ref=blk, dst_ref=blk, send_sem=send_sems.at[n_p * w + j], recv_sem=recv_sems.at[n_p * w + j],
                                                  device_id=sibling, device_id_type=MESH)
                cp.start()
                sent.append(cp)
        for w in range(n):
            for j, k in enumerate(PASS_FLIPS):
                blk = land[w].at[_lin(_flip(me, k + 1))]
                pltpu.make_async_remote_copy(src_ref=blk, dst_ref=blk, send_sem=send_sems.at[n_p * w + j], recv_sem=recv_sems.at[n_p * w + j],
                                             device_id=sibling, device_id_type=MESH).wait_recv()
        for cp in sent:
            cp.wait_send()

    return pl.pallas_call(body, in_specs=[ANY_SPEC] * n, out_specs=[ANY_SPEC] * n, out_shape=[_sds(t.shape, t.dtype) for t in lands],
                          input_output_aliases={i: i for i in range(n)},
                          scratch_shapes=[pltpu.SemaphoreType.DMA((n_p * n,)), pltpu.SemaphoreType.DMA((n_p * n,))], name=name)(*lands)


def _scatter_start(grads, name, deps=()):
    n = len(grads)
    lands = [lax.empty((N_DEV - 1,) + g.shape[1:], g.dtype) for g in grads]

    def issue(src, land, send_sems, recv_sems):
        me = _my_coords()
        for w in range(n):
            for k in range(1, N_DEV):
                q = (N_DEV - 1) * w + k - 1
                peer = _flip(me, k)
                pltpu.make_async_remote_copy(src_ref=src[w].at[_lin(peer)], dst_ref=land[w].at[k - 1], send_sem=send_sems.at[q], recv_sem=recv_sems.at[q],
                                             device_id=peer, device_id_type=MESH).start()

    return _split_start(issue, grads, lands, (N_DEV - 1) * n, name, deps)


def _scatter_wait(handle, after, name):
    n = handle["n"]

    def finish(src, land, send_sems, recv_sems):
        me = _my_coords()
        for w in range(n):
            for k in range(1, N_DEV):
                q = (N_DEV - 1) * w + k - 1
                peer = _flip(me, k)
                cp = pltpu.make_async_remote_copy(src_ref=src[w].at[_lin(peer)], dst_ref=land[w].at[k - 1], send_sem=send_sems.at[q], recv_sem=recv_sems.at[q],
                                                  device_id=peer, device_id_type=MESH)
                cp.wait_send()
                cp.wait_recv()

    return _split_wait(finish, handle, after, name)


CHIP_FLIPS = (0, 2, 4, 6)


def _pair_exchange(grads, name):
    n = len(grads)
    n_c = len(CHIP_FLIPS)

    def body(*refs):
        src, theirs = refs[:n], refs[n:2 * n]
        send_sems, recv_sems = refs[2 * n:]
        me = _my_coords()
        sibling = _flip(me, 1)
        sent = []
        for w in range(n):
            for j, k in enumerate(CHIP_FLIPS):
                q = n_c * w + j
                cp = pltpu.make_async_remote_copy(src_ref=src[w].at[_lin(_flip(me, k + 1))], dst_ref=theirs[w].at[j], send_sem=send_sems.at[q],
                                                  recv_sem=recv_sems.at[q], device_id=sibling, device_id_type=MESH)
                cp.start()
                sent.append(cp)
        for cp in sent:
            cp.wait_recv()
        for cp in sent:
            cp.wait_send()

    outs = pl.pallas_call(body, in_specs=[ANY_SPEC] * n, out_specs=[ANY_SPEC] * n, out_shape=[_sds((n_c,) + g.shape[1:], g.dtype) for g in grads],
                          scratch_shapes=[pltpu.SemaphoreType.DMA((n_c * n,))] * 2, name=name)(*grads)
    return list(outs)


def _pair_add(grad, theirs, name):
    p, r, c = theirs.shape
    tr = _row_tile(r, c, WIDE_TILE)
    me = _my_coords()
    ids = jnp.stack([_lin(_flip(me, k)) for k in CHIP_FLIPS]).astype(jnp.int32)

    def body(ids_ref, a_ref, b_ref, o_ref):
        o_ref[...] = (a_ref[...].astype(F32) + b_ref[...].astype(F32)).astype(o_ref.dtype)

    blk = pl.BlockSpec((None, tr, c), lambda j, i, ids_ref: (j, i, 0))
    return pl.pallas_call(
        body, out_shape=_sds((p, r, c), theirs.dtype), compiler_params=_params(2), name=name,
        grid_spec=pltpu.PrefetchScalarGridSpec(
            num_scalar_prefetch=1, grid=(p, r // tr),
            in_specs=[pl.BlockSpec((None, tr, c), lambda j, i, ids_ref: (ids_ref[j], i, 0)), blk], out_specs=blk))(ids, grad, theirs)


def _chips_start(parts, name, deps=()):
    n = len(parts)
    n_c = len(CHIP_FLIPS) - 1
    lands = [lax.empty((n_c,) + t.shape[1:], t.dtype) for t in parts]

    def issue(src, land, send_sems, recv_sems):
        me = _my_coords()
        for w in range(n):
            for j in range(1, n_c + 1):
                q = n_c * w + j - 1
                pltpu.make_async_remote_copy(src_ref=src[w].at[j], dst_ref=land[w].at[j - 1], send_sem=send_sems.at[q], recv_sem=recv_sems.at[q],
                                             device_id=_flip(me, CHIP_FLIPS[j]), device_id_type=MESH).start()

    return _split_start(issue, parts, lands, n_c * n, name, deps)


def _chips_wait(handle, after, name):
    n = handle["n"]
    n_c = len(CHIP_FLIPS) - 1

    def finish(src, land, send_sems, recv_sems):
        me = _my_coords()
        for w in range(n):
            for j in range(1, n_c + 1):
                q = n_c * w + j - 1
                cp = pltpu.make_async_remote_copy(src_ref=src[w].at[j], dst_ref=land[w].at[j - 1], send_sem=send_sems.at[q], recv_sem=recv_sems.at[q],
                                                  device_id=_flip(me, CHIP_FLIPS[j]), device_id_type=MESH)
                cp.wait_send()
                cp.wait_recv()

    return _split_wait(finish, handle, after, name)


def _after(t, *tokens):
    for tok in tokens:
        t = t + tok[0:1, 0:1]
    return t


def _rope_tables(positions):
    half = ROT // 2
    inv_freq = ROPE_THETA ** (-jnp.arange(0, ROT, 2, dtype=F32) / ROT)
    ang = positions.astype(F32).reshape(-1, 1) * inv_freq
    cos, sin = jnp.cos(ang), jnp.sin(ang)
    s = ang.shape[0]
    pad = jnp.zeros((s, HEAD_DIM - ROT), F32)
    zero = jnp.zeros((s, half), F32)
    two = lambda t: jnp.concatenate([t, t], axis=1)
    return (two(jnp.concatenate([cos, cos, pad + 1.0], axis=1)), two(jnp.concatenate([-sin, zero, pad], axis=1)),
            two(jnp.concatenate([zero, sin, pad], axis=1)))


def _local_step(x, tgt, tabs, mod, sinks_pad, hl, hg_norm, g_pre_mix, g_post_mix, g_pre_ffn, g_post_ffn, weights, scatter):
    s = x.shape[0]
    h1 = _pre_fwd(x, g_pre_mix, mod, 1, 0, "pre_mix_fwd")
    (w_in_t,) = weights("in", h1)
    proj = _mm_nt(h1, w_in_t, s, 256, D, F32, "proj_mm")
    att = _attn_fwd(proj, tabs, sinks_pad)
    o_raw, states = _hgrn_fwd(proj, hl)
    ohg = _hgout_fwd(o_raw, proj, hg_norm)
    w_attn_dm, w_hgrn_dm, w_out = weights("mix", ohg)
    y_a = _mm_nn_dm(att, w_attn_dm, s, F32, "attn_proj_mm")
    y_h = _mm_nn_dm(ohg, w_hgrn_dm, s, F32, "hgrn_proj_mm")
    merged = _merge_fwd(y_a, y_h, proj)
    y = _mm_nn(merged, w_out, s, 512, D, F32, "out_mm")
    x1 = _post_fwd(x, y, g_post_mix, mod, 2, "post_mix_fwd")
    h2 = _pre_fwd(x1, g_pre_ffn, mod, 4, 3, "pre_ffn_fwd")
    w_ffn_in_dm, w_ffn_out = weights("ffn", h2)
    gu = _mm_nn_dm(h2, w_ffn_in_dm, s // 2, F32, "ffn_in_mm")
    act = _swiglu_fwd(gu)
    y2 = _mm_nn(act, w_ffn_out, s, 512, FFN // 4, F32, "ffn_out_mm")
    err, loss = _post_fwd_loss(x1, y2, g_post_ffn, mod, 5, tgt, "post_ffn_loss")
    dy2, d_gate2, dg_post_ffn = _post_bwd(err, y2, g_post_ffn, mod, 5, "post_ffn_bwd")
    d_act = _mm_nt(dy2, w_ffn_out, s, 512, D, F32, "ffn_out_dx")
    gw_ffn_out = _mm_tn(act, dy2, 512, D, BF16, "ffn_out_dw")
    t_out = scatter([gw_ffn_out.reshape(N_DEV, FFN // N_DEV, D)], "ffn_out")
    dgu = _swiglu_bwd(d_act, gu)
    dh2 = _mm_nt_dm(dgu, w_ffn_in_dm, s, 1024, F32, "ffn_in_dx")
    gw_ffn_in = _mm_tn_dm(h2, dgu, 512, BF16, "ffn_in_dw")
    t_in = scatter([gw_ffn_in], "ffn_in")
    mod = _after(mod, t_out, t_in)
    dx1, d_shift2, d_scale2, dg_pre_ffn = _pre_bwd(dh2, x1, err, g_pre_ffn, mod, 4, "pre_ffn_bwd")
    dy, d_gate1, dg_post_mix = _post_bwd(dx1, y, g_post_mix, mod, 2, "post_mix_bwd")
    d_merged = _mm_nt(dy, w_out, s, 512, D, F32, "out_dx")
    gw_out = _mm_tn(merged, dy, 512, D, BF16, "out_dw")
    dy_a, dy_h, d_gate_a, d_gate_h = _merge_bwd(d_merged, y_a, y_h, proj)
    d_att = _mm_nt_dm(dy_a, w_attn_dm, s, ATT_W, F32, "attn_proj_dx")
    gw_attn = _mm_tn_dm(att, dy_a, ATT_W, BF16, "attn_proj_dw")
    d_ohg = _mm_nt_dm(dy_h, w_hgrn_dm, s, HG_W, F32, "hgrn_proj_dx")
    gw_hgrn = _mm_tn_dm(ohg, dy_h, HG_W, BF16, "hgrn_proj_dw")
    t_mix = scatter([gw_attn, gw_hgrn, gw_out.reshape(N_DEV, D // N_DEV, D)], "mix")
    d_o, d_gh, d_hg_norm = _hgout_bwd(d_ohg, o_raw, proj, _after(hg_norm, t_mix))
    d_qh, d_fh, d_ih, d_hl = _hgrn_bwd(proj, hl, states, d_o)
    d_qa, d_ka, d_va, d_sinks = _attn_bwd(proj, tabs, sinks_pad, d_att)
    d_proj = jnp.concatenate([d_qa, d_ka.astype(BF16), d_va.astype(BF16), d_qh, d_fh, d_ih, d_gh, d_gate_a, d_gate_h], axis=1)
    dh1 = _mm_nn(d_proj, w_in_t, s // 2, 512, IN_COLS // 2, F32, "proj_dx")
    grad_x, d_shift1, d_scale1, dg_pre_mix = _pre_bwd(dh1, x, dx1, g_pre_mix, mod, 1, "pre_mix_bwd")
    d_mod = jnp.concatenate([d_shift1, d_scale1, d_gate1, d_shift2, d_scale2, d_gate2], axis=1)
    small = [d_mod, dg_pre_mix, dg_post_mix, dg_pre_ffn, dg_post_ffn, d_hl.reshape(1, 2 * HG_W), d_hg_norm, d_sinks]
    return loss, grad_x, small, h1, d_proj


def kernel(x, c, positions, w_ada, b_ada, g_pre_mix, g_post_mix, g_pre_ffn, g_post_ffn, w_in, attn_sinks, w_attn_proj, hg_lower_bounds, hg_norm, w_hgrn_proj, w_out, w_ffn_in, w_ffn_out, loss_target, m_w_ada, m_b_ada, m_g_pre_mix, m_g_post_mix, m_g_pre_ffn, m_g_post_ffn, m_w_in, m_attn_sinks, m_w_attn_proj, m_hg_lower_bounds, m_hg_norm, m_w_hgrn_proj, m_w_out, m_w_ffn_in, m_w_ffn_out, v_w_ada, v_b_ada, v_g_pre_mix, v_g_post_mix, v_g_pre_ffn, v_g_post_ffn, v_w_in, v_attn_sinks, v_w_attn_proj, v_hg_lower_bounds, v_hg_norm, v_w_hgrn_proj, v_w_out, v_w_ffn_in, v_w_ffn_out):
    my_id = _lin(_my_coords())
    s = x.shape[1]
    n_ada = w_ada.shape[2]

    c_all = _exchange_small(c.reshape(1, 1, D), True, "gather_c").reshape(N_DEV, D)
    b_cols = lax.dynamic_slice(b_ada, (0, my_id * n_ada), (1, n_ada))
    mod_part = _mod_part(c_all, w_ada[0], b_cols)
    mod = _exchange_small(mod_part.reshape(N_DEV, 1, n_ada), False, "scatter_mod").reshape(1, N_MOD * D)
    groups = {"in": [w_in[0].T], "mix": [w_attn_proj[0], w_hgrn_proj[0], w_out[0]], "ffn": [w_ffn_in[0], w_ffn_out[0]]}

    def start(group, dep):
        shards, dep = lax.optimization_barrier((groups[group], dep))
        return _gather_start([t.astype(BF16) for t in shards], "gather_start_" + group, deps=[dep])

    gathers = {"in": start("in", mod)}
    gathers["mix"] = start("mix", gathers["in"]["token"])
    gathers["ffn"] = start("ffn", gathers["mix"]["token"])

    def weights(group, after):
        after = [after, gathers["ffn"]["token"]]
        lands = _gather_pass(_gather_wait(gathers[group], after, "gather_wait_" + group), "gather_pass_" + group)
        if group == "in":
            return (lands[0].reshape(IN_COLS, D),)
        if group == "mix":
            return lands[0], lands[1], lands[2].reshape(D, D)
        return lands[0], lands[1].reshape(FFN, D)

    scatters = {}

    def scatter(grads, group):
        scatters[group] = _scatter_start(grads, "scatter_start_" + group)
        return scatters[group]["token"]

    sinks_pad = jnp.pad(attn_sinks, ((0, 0), (0, LANE - ATT_HEADS)))
    loss, grad_x, small, h1, d_proj = _local_step(
        x[0], loss_target[0], _rope_tables(positions), mod, sinks_pad, hg_lower_bounds, hg_norm, g_pre_mix, g_post_mix, g_pre_ffn, g_post_ffn,
        weights, scatter)
    loss = lax.psum(loss[0, 0], ("x", "y", "c"))

    sizes = [t.shape[1] for t in small]
    parts = _exchange_small(jnp.concatenate(small, axis=1).reshape(1, 1, sum(sizes)), True, "gather_small_grads")
    gw_in = _mm_tn(d_proj, h1, 256, D, BF16, "proj_dw", deps=[parts]).reshape(N_DEV, IN_COLS // N_DEV, D)
    theirs = _pair_exchange([gw_in], "scatter_pair_in")
    scatters["in"] = _chips_start([_pair_add(gw_in, theirs[0], "scatter_pair_add_in")], "scatter_start_in")
    offs = [sum(sizes[:k]) for k in range(len(sizes))]
    piece = lambda k, n=None: parts[:, :, offs[k]:offs[k] + (sizes[k] if n is None else n)]
    small_w = [(piece(0), b_ada, m_b_ada, v_b_ada), (piece(1), g_pre_mix, m_g_pre_mix, v_g_pre_mix),
               (piece(2), g_post_mix, m_g_post_mix, v_g_post_mix), (piece(3), g_pre_ffn, m_g_pre_ffn, v_g_pre_ffn),
               (piece(4), g_post_ffn, m_g_post_ffn, v_g_post_ffn),
               (piece(5).reshape(N_DEV, 2, HG_W), hg_lower_bounds, m_hg_lower_bounds, v_hg_lower_bounds),
               (piece(6), hg_norm, m_hg_norm, v_hg_norm), (piece(7, ATT_HEADS), attn_sinks, m_attn_sinks, v_attn_sinks)]
    names = ["b_ada", "g_pre_mix", "g_post_mix", "g_pre_ffn", "g_post_ffn", "hg_lower_bounds", "hg_norm", "attn_sinks"]
    res = {n: _adamw(p, w, m, v, "adamw_" + n) for n, (p, w, m, v) in zip(names, small_w)}

    dmod_cols = lax.dynamic_slice(parts.reshape(N_DEV, -1), (0, my_id * n_ada), (N_DEV, n_ada))
    g_w_ada = _grad_w_ada(c_all.T, dmod_cols)
    res["w_ada"] = [g_w_ada] + list(_adamw(g_w_ada[None], w_ada[0], m_w_ada[0], v_w_ada[0], "adamw_w_ada", emit_grad=False))

    big = {"ffn_out": [("w_ffn_out", w_ffn_out, m_w_ffn_out, v_w_ffn_out)], "ffn_in": [("w_ffn_in", w_ffn_in, m_w_ffn_in, v_w_ffn_in)],
           "mix": [("w_attn_proj", w_attn_proj, m_w_attn_proj, v_w_attn_proj), ("w_hgrn_proj", w_hgrn_proj, m_w_hgrn_proj, v_w_hgrn_proj),
                   ("w_out", w_out, m_w_out, v_w_out)],
           "in": [("w_in", w_in, m_w_in, v_w_in)]}
    after = [scatters["in"]["token"]]
    for group, members in big.items():
        if group == "in":
            local, lands = _chips_wait(scatters[group], after + [res["w_ada"][1], res["b_ada"][1]], "scatter_wait_" + group)
            own = [t[0] for t in local]
        else:
            local, lands = _scatter_wait(scatters[group], after, "scatter_wait_" + group)
            own = [lax.dynamic_index_in_dim(t, my_id, 0, keepdims=False) for t in local]
        for (n, w, m, v), g_own, land in zip(members, own, lands):
            if group == "in":
                g_w = _sum_pieces(land, g_own, "sum_" + n).T
                res[n] = [g_w] + list(_adamw(g_w[None], w[0], m[0], v[0], "adamw_" + n, emit_grad=False))
            else:
                res[n] = _adamw(land, w[0], m[0], v[0], "adamw_" + n, own=g_own)
            after = [res[n][1]]

    order = ["w_ada", "b_ada", "g_pre_mix", "g_post_mix", "g_pre_ffn", "g_post_ffn", "w_in", "attn_sinks", "w_attn_proj",
             "hg_lower_bounds", "hg_norm", "w_hgrn_proj", "w_out", "w_ffn_in", "w_ffn_out"]
    lead = {"w_ada", "w_in", "w_attn_proj", "w_hgrn_proj", "w_out", "w_ffn_in", "w_ffn_out"}
    outs = [loss, grad_x[None]]
    for k in range(4):
        outs += [res[n][k][None] if n in lead else res[n][k] for n in order]
    return tuple(outs)
```

```python
import functools

import jax
import jax.numpy as jnp
from jax import lax
from jax.experimental import pallas as pl
from jax.experimental.pallas import tpu as pltpu

F32 = jnp.float32
BF16 = jnp.bfloat16

N_DEV = 8
D = 2048
ATT_HEADS = 16
KV_HEADS = 2
HEAD_DIM = 64
GROUP = ATT_HEADS // KV_HEADS
ATT_W = ATT_HEADS * HEAD_DIM
BLK = 128
ROT = HEAD_DIM // 4
ROPE_THETA = 500000.0
HG_HEADS = 8
HG_K = 128
HG_W = HG_HEADS * HG_K
CHUNK = 64
SUB = 16
FFN = 5632
N_MOD = 6
EPS = 1e-6
LANE = 128
Q_A, K_A, V_A, Q_H, F_H, I_H, G_H, GT_A, GT_H, IN_COLS = 0, 1024, 1152, 1280, 2304, 3328, 4352, 5376, 7424, 9472

ADAM_LR, ADAM_B1, ADAM_B2, ADAM_EPS, ADAM_WD, ADAM_STEP = 0.001, 0.9, 0.999, 1e-08, 0.01, 10

TR = 256
HG_TB = 512
VMEM_BIG = 56 << 20
MESH = pl.DeviceIdType.MESH


def _sds(shape, dtype):
    return jax.ShapeDtypeStruct(shape, dtype)


def _params(n_axes, vmem=None):
    return pltpu.CompilerParams(dimension_semantics=("arbitrary",) * n_axes, vmem_limit_bytes=vmem)


def _sig(t):
    return 1.0 / (1.0 + jnp.exp(-t))


def _dot(a, b, dims):
    return lax.dot_general(a, b, (dims, ((), ())), preferred_element_type=F32)


NN = ((1,), (0,))
NT = ((1,), (1,))
TN = ((0,), (0,))


def _matmul(a, b, a_spec, b_spec, o_spec, out_shape, grid, dims, acc_shape, name, deps=()):
    nk = grid[2]
    nd = len(deps)

    def body(a_ref, b_ref, *rest):
        o_ref, scratch = rest[nd], rest[nd + 1:]
        part = _dot(a_ref[...], b_ref[...], dims)
        if nk == 1:
            o_ref[...] = part.astype(o_ref.dtype)
        else:
            acc = scratch[0]
            k = pl.program_id(2)

            @pl.when(k == 0)
            def _():
                acc[...] = part

            @pl.when(k > 0)
            def _():
                acc[...] += part

            @pl.when(k == nk - 1)
            def _():
                o_ref[...] = acc[...].astype(o_ref.dtype)

    return pl.pallas_call(
        body, grid=grid, in_specs=[a_spec, b_spec] + [pl.BlockSpec(memory_space=pl.ANY)] * nd, out_specs=o_spec, out_shape=out_shape,
        scratch_shapes=[pltpu.VMEM(acc_shape, F32)] if nk > 1 else [],
        compiler_params=_params(3, VMEM_BIG), name=name)(a, b, *deps)


def _mm_nn(a, b, tm, tn, tk, out_dtype, name):
    m, k = a.shape
    n = b.shape[1]
    return _matmul(a, b, pl.BlockSpec((tm, tk), lambda j, i, kk: (i, kk)), pl.BlockSpec((tk, tn), lambda j, i, kk: (kk, j)),
                   pl.BlockSpec((tm, tn), lambda j, i, kk: (i, j)), _sds((m, n), out_dtype),
                   (n // tn, m // tm, k // tk), NN, (tm, tn), name)


def _mm_nn_dm(a, b, tm, out_dtype, name):
    m, k = a.shape
    n = b.shape[2]
    return _matmul(a, b, pl.BlockSpec((tm, k), lambda j, i, kk: (i, 0)), pl.BlockSpec((None, k, n), lambda j, i, kk: (j, 0, 0)),
                   pl.BlockSpec((tm, n), lambda j, i, kk: (i, j)), _sds((m, N_DEV * n), out_dtype),
                   (N_DEV, m // tm, 1), NN, (tm, n), name)


def _mm_nt(a, b, tm, tn, tk, out_dtype, name):
    m, k = a.shape
    n = b.shape[0]
    return _matmul(a, b, pl.BlockSpec((tm, tk), lambda j, i, kk: (i, kk)), pl.BlockSpec((tn, tk), lambda j, i, kk: (j, kk)),
                   pl.BlockSpec((tm, tn), lambda j, i, kk: (i, j)), _sds((m, n), out_dtype),
                   (n // tn, m // tm, k // tk), NT, (tm, tn), name)


def _mm_nt_dm(a, b, tm, tn, out_dtype, name):
    m = a.shape[0]
    n_out, n = b.shape[1], b.shape[2]
    return _matmul(a, b, pl.BlockSpec((tm, n), lambda j, i, kk: (i, kk)), pl.BlockSpec((None, tn, n), lambda j, i, kk: (kk, j, 0)),
                   pl.BlockSpec((tm, tn), lambda j, i, kk: (i, j)), _sds((m, n_out), out_dtype),
                   (n_out // tn, m // tm, N_DEV), NT, (tm, tn), name)


def _mm_tn(a, b, tm, tn, out_dtype, name, deps=()):
    s, m = a.shape
    n = b.shape[1]
    return _matmul(a, b, pl.BlockSpec((s, tm), lambda j, i, kk: (0, i)), pl.BlockSpec((s, tn), lambda j, i, kk: (0, j)),
                   pl.BlockSpec((tm, tn), lambda j, i, kk: (i, j)), _sds((m, n), out_dtype),
                   (n // tn, m // tm, 1), TN, (tm, tn), name, deps)


def _mm_tn_dm(a, b, tm, out_dtype, name):
    s, m = a.shape
    n = b.shape[1] // N_DEV
    return _matmul(a, b, pl.BlockSpec((s, tm), lambda j, i, kk: (0, i)), pl.BlockSpec((s, n), lambda j, i, kk: (0, j)),
                   pl.BlockSpec((None, tm, n), lambda j, i, kk: (j, i, 0)), _sds((N_DEV, m, n), out_dtype),
                   (N_DEV, m // tm, 1), TN, (tm, n), name)


def _row_spec():
    return pl.BlockSpec((TR, D), lambda i: (i, 0))


def _vec_spec(k=0):
    return pl.BlockSpec((1, D), lambda i: (0, k))


def _acc_rows(ref, first, val):
    @pl.when(first)
    def _():
        ref[...] = val

    @pl.when(jnp.logical_not(first))
    def _():
        ref[...] += val


def _pre_fwd(x, g, mod, k_scale, k_shift, name):
    s = x.shape[0]

    def body(x_ref, g_ref, sc_ref, sh_ref, h_ref):
        xv = x_ref[...]
        r = lax.rsqrt(jnp.mean(xv * xv, axis=-1, keepdims=True) + EPS)
        n = xv * r * g_ref[...]
        h_ref[...] = (n * (1.0 + sc_ref[...]) + sh_ref[...]).astype(h_ref.dtype)

    return pl.pallas_call(body, grid=(s // TR,), in_specs=[_row_spec(), _vec_spec(), _vec_spec(k_scale), _vec_spec(k_shift)],
                          out_specs=_row_spec(), out_shape=_sds((s, D), BF16), compiler_params=_params(1), name=name)(x, g, mod, mod)


def _post_fwd(x, y, g, mod, k_gate, name):
    s = x.shape[0]

    def body(x_ref, y_ref, g_ref, gt_ref, o_ref):
        yv = y_ref[...]
        r = lax.rsqrt(jnp.mean(yv * yv, axis=-1, keepdims=True) + EPS)
        o_ref[...] = x_ref[...] + gt_ref[...] * (yv * r * g_ref[...])

    return pl.pallas_call(body, grid=(s // TR,), in_specs=[_row_spec(), _row_spec(), _vec_spec(), _vec_spec(k_gate)],
                          out_specs=_row_spec(), out_shape=_sds((s, D), F32), compiler_params=_params(1), name=name)(x, y, g, mod)


def _post_fwd_loss(x, y, g, mod, k_gate, tgt, name):
    s = x.shape[0]

    def body(x_ref, y_ref, g_ref, gt_ref, t_ref, e_ref, loss_ref):
        i = pl.program_id(0)
        yv = y_ref[...]
        r = lax.rsqrt(jnp.mean(yv * yv, axis=-1, keepdims=True) + EPS)
        err = x_ref[...] + gt_ref[...] * (yv * r * g_ref[...]) - t_ref[...]
        e_ref[...] = err * (1.0 / D)
        part = 0.5 * jnp.sum(jnp.mean(err * err, axis=-1, keepdims=True), axis=0, keepdims=True)
        _acc_rows(loss_ref, i == 0, part)

    return pl.pallas_call(body, grid=(s // TR,),
                          in_specs=[_row_spec(), _row_spec(), _vec_spec(), _vec_spec(k_gate), _row_spec()],
                          out_specs=[_row_spec(), pl.BlockSpec((1, 1), lambda i: (0, 0))],
                          out_shape=[_sds((s, D), F32), _sds((1, 1), F32)], compiler_params=_params(1), name=name)(x, y, g, mod, tgt)


def _pre_bwd(dh, x, res, g, mod, k_scale, name):
    s = x.shape[0]

    def body(dh_ref, x_ref, res_ref, g_ref, sc_ref, dx_ref, dsh_ref, dsc_ref, dg_ref):
        first = pl.program_id(0) == 0
        xv, dh_v, gv = x_ref[...], dh_ref[...], g_ref[...]
        r = lax.rsqrt(jnp.mean(xv * xv, axis=-1, keepdims=True) + EPS)
        xh = xv * r
        dn = dh_v * (1.0 + sc_ref[...])
        dgn = dn * gv
        dx_ref[...] = res_ref[...] + r * (dgn - xh * jnp.mean(dgn * xh, axis=-1, keepdims=True))
        _acc_rows(dsh_ref, first, jnp.sum(dh_v, axis=0, keepdims=True))
        _acc_rows(dsc_ref, first, jnp.sum(dh_v * (xh * gv), axis=0, keepdims=True))
        _acc_rows(dg_ref, first, jnp.sum(dn * xh, axis=0, keepdims=True))

    return pl.pallas_call(body, grid=(s // TR,),
                          in_specs=[_row_spec(), _row_spec(), _row_spec(), _vec_spec(), _vec_spec(k_scale)],
                          out_specs=[_row_spec(), _vec_spec(), _vec_spec(), _vec_spec()],
                          out_shape=[_sds((s, D), F32)] + [_sds((1, D), F32)] * 3,
                          compiler_params=_params(1), name=name)(dh, x, res, g, mod)


def _post_bwd(dx, y, g, mod, k_gate, name):
    s = y.shape[0]

    def body(dx_ref, y_ref, g_ref, gt_ref, dy_ref, dgt_ref, dg_ref):
        first = pl.program_id(0) == 0
        yv, dxv, gv = y_ref[...], dx_ref[...], g_ref[...]
        r = lax.rsqrt(jnp.mean(yv * yv, axis=-1, keepdims=True) + EPS)
        yh = yv * r
        dn = dxv * gt_ref[...]
        dgn = dn * gv
        dy_ref[...] = (r * (dgn - yh * jnp.mean(dgn * yh, axis=-1, keepdims=True))).astype(dy_ref.dtype)
        _acc_rows(dgt_ref, first, jnp.sum(dxv * (yh * gv), axis=0, keepdims=True))
        _acc_rows(dg_ref, first, jnp.sum(dn * yh, axis=0, keepdims=True))

    return pl.pallas_call(body, grid=(s // TR,), in_specs=[_row_spec(), _row_spec(), _vec_spec(), _vec_spec(k_gate)],
                          out_specs=[_row_spec(), _vec_spec(), _vec_spec()],
                          out_shape=[_sds((s, D), BF16), _sds((1, D), F32), _sds((1, D), F32)],
                          compiler_params=_params(1), name=name)(dx, y, g, mod)


SW_TN = 1408
SW_TR = 512
TALL = 1024


def _swiglu_fwd(gu):
    s = gu.shape[0]
    nb = FFN // SW_TN

    def body(g_ref, u_ref, a_ref):
        gv = g_ref[...]
        a_ref[...] = (gv * _sig(gv) * u_ref[...]).astype(a_ref.dtype)

    return pl.pallas_call(body, grid=(s // SW_TR, nb),
                          in_specs=[pl.BlockSpec((SW_TR, SW_TN), lambda i, j: (i, j)), pl.BlockSpec((SW_TR, SW_TN), lambda i, j: (i, j + nb))],
                          out_specs=pl.BlockSpec((SW_TR, SW_TN), lambda i, j: (i, j)), out_shape=_sds((s, FFN), BF16),
                          compiler_params=_params(2, 48 << 20), name="swiglu_fwd")(gu, gu)


def _swiglu_bwd(dact, gu):
    s = gu.shape[0]
    nb = FFN // SW_TN

    def body(da_ref, g_ref, u_ref, o_ref):
        half = pl.program_id(2)
        gv, da = g_ref[...], da_ref[...]
        sg = _sig(gv)
        d_gate = da * u_ref[...] * (sg * (1.0 + gv * (1.0 - sg)))
        d_up = da * (gv * sg)
        o_ref[...] = jnp.where(half == 0, d_gate, d_up).astype(o_ref.dtype)

    blk = lambda f: pl.BlockSpec((SW_TR, SW_TN), f)
    return pl.pallas_call(body, grid=(s // SW_TR, nb, 2),
                          in_specs=[blk(lambda i, j, h: (i, j)), blk(lambda i, j, h: (i, j)), blk(lambda i, j, h: (i, j + nb))],
                          out_specs=blk(lambda i, j, h: (i, j + nb * h)), out_shape=_sds((s, 2 * FFN), BF16),
                          compiler_params=_params(3, 48 << 20), name="swiglu_bwd")(dact, gu, gu)


MG_TN = 256


def _merge_fwd(y_a, y_h, proj):
    s = y_a.shape[0]
    tn = MG_TN
    ba, bh = GT_A // tn, GT_H // tn

    def body(ya_ref, yh_ref, ga_ref, gh_ref, m_ref):
        m_ref[...] = (_sig(ga_ref[...]) * ya_ref[...] + _sig(gh_ref[...]) * yh_ref[...]).astype(m_ref.dtype)

    tr = min(s, TALL)
    blk = lambda f: pl.BlockSpec((tr, tn), f)
    return pl.pallas_call(body, grid=(s // tr, D // tn),
                          in_specs=[blk(lambda i, j: (i, j)), blk(lambda i, j: (i, j)), blk(lambda i, j: (i, j + ba)), blk(lambda i, j: (i, j + bh))],
                          out_specs=blk(lambda i, j: (i, j)), out_shape=_sds((s, D), BF16),
                          compiler_params=_params(2), name="merge_fwd")(y_a, y_h, proj, proj)


def _merge_bwd(dm, y_a, y_h, proj):
    s = y_a.shape[0]
    tn = MG_TN
    ba, bh = GT_A // tn, GT_H // tn

    def body(dm_ref, ya_ref, yh_ref, ga_ref, gh_ref, dya_ref, dyh_ref, dga_ref, dgh_ref):
        dmv = dm_ref[...]
        sa, sh = _sig(ga_ref[...]), _sig(gh_ref[...])
        dya_ref[...] = (dmv * sa).astype(BF16)
        dyh_ref[...] = (dmv * sh).astype(BF16)
        dga_ref[...] = (dmv * ya_ref[...] * (sa * (1.0 - sa))).astype(BF16)
        dgh_ref[...] = (dmv * yh_ref[...] * (sh * (1.0 - sh))).astype(BF16)

    tr = min(s, TALL)
    blk = lambda f: pl.BlockSpec((tr, tn), f)
    nat = blk(lambda i, j: (i, j))
    return pl.pallas_call(body, grid=(s // tr, D // tn),
                          in_specs=[nat, nat, nat, blk(lambda i, j: (i, j + ba)), blk(lambda i, j: (i, j + bh))],
                          out_specs=[nat] * 4, out_shape=[_sds((s, D), BF16)] * 4,
                          compiler_params=_params(2), name="merge_bwd")(dm, y_a, y_h, proj, proj)


def _hgout_fwd(o_raw, proj, hg_norm):
    s = o_raw.shape[0]
    bg = G_H // LANE

    def body(o_ref, g_ref, n_ref, out_ref):
        ov = o_ref[...]
        r = lax.rsqrt(jnp.mean(ov * ov, axis=-1, keepdims=True) + EPS)
        out_ref[...] = (ov * r * n_ref[...] * _sig(g_ref[...])).astype(out_ref.dtype)

    tr = min(s, TALL)
    blk = lambda f: pl.BlockSpec((tr, LANE), f)
    return pl.pallas_call(body, grid=(s // tr, HG_HEADS),
                          in_specs=[blk(lambda i, h: (i, h)), blk(lambda i, h: (i, h + bg)), pl.BlockSpec((1, LANE), lambda i, h: (0, 0))],
                          out_specs=blk(lambda i, h: (i, h)), out_shape=_sds((s, HG_W), BF16),
                          compiler_params=_params(2), name="hgout_fwd")(o_raw, proj, hg_norm)


def _hgout_bwd(d_out, o_raw, proj, hg_norm):
    s = o_raw.shape[0]
    bg = G_H // LANE

    def body(d_ref, o_ref, g_ref, n_ref, do_ref, dg_ref, dn_ref):
        first = jnp.logical_and(pl.program_id(0) == 0, pl.program_id(1) == 0)
        ov, dv, nv = o_ref[...], d_ref[...], n_ref[...]
        sg = _sig(g_ref[...])
        r = lax.rsqrt(jnp.mean(ov * ov, axis=-1, keepdims=True) + EPS)
        oh = ov * r
        d_on = dv * sg
        dg_ref[...] = (dv * (oh * nv) * (sg * (1.0 - sg))).astype(dg_ref.dtype)
        t = d_on * nv
        do_ref[...] = r * (t - oh * jnp.mean(t * oh, axis=-1, keepdims=True))
        _acc_rows(dn_ref, first, jnp.sum(d_on * oh, axis=0, keepdims=True))

    tr = min(s, TALL)
    blk = lambda f: pl.BlockSpec((tr, LANE), f)
    vec = pl.BlockSpec((1, LANE), lambda i, h: (0, 0))
    return pl.pallas_call(body, grid=(s // tr, HG_HEADS),
                          in_specs=[blk(lambda i, h: (i, h)), blk(lambda i, h: (i, h)), blk(lambda i, h: (i, h + bg)), vec],
                          out_specs=[blk(lambda i, h: (i, h)), blk(lambda i, h: (i, h)), vec],
                          out_shape=[_sds((s, HG_W), F32), _sds((s, HG_W), BF16), _sds((1, LANE), F32)],
                          compiler_params=_params(2), name="hgout_bwd")(d_out, o_raw, proj, hg_norm)


def _rope(t, cos, s_lo, s_hi):
    return t * cos + pltpu.roll(t, LANE - ROT // 2, 1) * s_lo + pltpu.roll(t, ROT // 2, 1) * s_hi


def _rope_wide(t, cos, s_lo, s_hi):
    return jnp.concatenate([_rope(t[:, k * LANE:(k + 1) * LANE], cos, s_lo, s_hi) for k in range(t.shape[1] // LANE)], axis=1)


def _attn_mask(has_prev):
    qi = lax.broadcasted_iota(jnp.int32, (BLK, 2 * BLK), 0)
    kj = lax.broadcasted_iota(jnp.int32, (BLK, 2 * BLK), 1)
    rel = BLK + qi - kj
    band = jnp.logical_and(rel >= 0, rel < BLK)
    return jnp.logical_and(band, jnp.logical_or(has_prev, kj >= BLK))


def _attn_specs():
    prev = lambda i: jnp.maximum(i - 1, 0)
    kb, vb = K_A // LANE, V_A // LANE
    blk = lambda f: pl.BlockSpec((BLK, LANE), f)
    tabs = [blk(lambda i: (i, 0))] * 3 + [blk(lambda i: (prev(i), 0))] * 3
    return [pl.BlockSpec((BLK, ATT_W), lambda i: (i, 0)), blk(lambda i: (i, kb)), blk(lambda i: (prev(i), kb)),
            blk(lambda i: (i, vb)), blk(lambda i: (prev(i), vb))] + tabs + [pl.BlockSpec((1, LANE), lambda i: (0, 0))]


def _attn_logits(qh, kg):
    return _dot(qh, kg, NT)


def _attn_probs(raw, mask, sk):
    logits = jnp.where(mask, raw * (HEAD_DIM ** -0.5), -jnp.inf)
    m = jnp.maximum(jnp.max(logits, axis=-1, keepdims=True), sk)
    p = jnp.exp(logits - m)
    e_sink = jnp.exp(sk - m)
    inv = 1.0 / (jnp.sum(p, axis=-1, keepdims=True) + e_sink)
    return p * inv, e_sink * inv


def _attn_fwd(proj, tabs, sinks):
    s = proj.shape[0]

    def body(q_ref, kc_ref, kp_ref, vc_ref, vp_ref, c0, l0, h0, c1, l1, h1, sk_ref, o_ref):
        i = pl.program_id(0)
        mask = _attn_mask(i > 0)
        q = _rope_wide(q_ref[...], c0[...], l0[...], h0[...]).astype(BF16)
        kk = jnp.concatenate([_rope(kp_ref[...], c1[...], l1[...], h1[...]), _rope(kc_ref[...], c0[...], l0[...], h0[...])], axis=0).astype(BF16)
        vv = jnp.concatenate([vp_ref[...], vc_ref[...]], axis=0).astype(BF16)
        part = lambda t, h: t[:, h * HEAD_DIM:(h + 1) * HEAD_DIM]

        def head(h):
            raw = _attn_logits(part(q, h), part(kk, h // GROUP))
            yield
            sk = sk_ref[:, h:h + 1]
            logits = jnp.where(mask, raw * (HEAD_DIM ** -0.5), -jnp.inf)
            m = jnp.maximum(jnp.max(logits, axis=-1, keepdims=True), sk)
            yield
            p = jnp.exp(logits - m)
            den = jnp.sum(p, axis=-1, keepdims=True) + jnp.exp(sk - m)
            yield
            out = _dot((p * (1.0 / den)).astype(BF16), part(vv, h // GROUP), NN)
            yield
            return out

        o_ref[...] = jnp.concatenate(_interleave([head(h) for h in range(ATT_HEADS)]), axis=1).astype(o_ref.dtype)

    return pl.pallas_call(body, grid=(s // BLK,), in_specs=_attn_specs(),
                          out_specs=pl.BlockSpec((BLK, ATT_W), lambda i: (i, 0)), out_shape=_sds((s, ATT_W), BF16),
                          compiler_params=_params(1), name="attn_fwd")(proj, proj, proj, proj, proj, *tabs, *tabs, sinks)


def _attn_bwd(proj, tabs, sinks, d_att):
    s = proj.shape[0]

    def body(q_ref, kc_ref, kp_ref, vc_ref, vp_ref, c0, l0, h0, c1, l1, h1, sk_ref, do_ref, dq_ref, dk_ref, dv_ref, ds_ref):
        i = pl.program_id(0)

        @pl.when(i == 0)
        def _():
            dk_ref[...] = jnp.zeros_like(dk_ref)
            dv_ref[...] = jnp.zeros_like(dv_ref)
            ds_ref[...] = jnp.zeros_like(ds_ref)

        mask = _attn_mask(i > 0)
        q = _rope_wide(q_ref[...], c0[...], l0[...], h0[...]).astype(BF16)
        kk = jnp.concatenate([_rope(kp_ref[...], c1[...], l1[...], h1[...]), _rope(kc_ref[...], c0[...], l0[...], h0[...])], axis=0).astype(BF16)
        vv = jnp.concatenate([vp_ref[...], vc_ref[...]], axis=0).astype(BF16)
        d_o = do_ref[...].astype(BF16)
        lane = lax.broadcasted_iota(jnp.int32, (1, LANE), 1)
        part = lambda t, h: t[:, h * HEAD_DIM:(h + 1) * HEAD_DIM]

        def head(h):
            kg, vg = part(kk, h // GROUP), part(vv, h // GROUP)
            qh, doh = part(q, h), part(d_o, h)
            raw = _attn_logits(qh, kg)
            d_p = _dot(doh, vg, NT)
            yield
            prob, p_sink = _attn_probs(raw, mask, sk_ref[:, h:h + 1])
            yield
            dd = jnp.sum(prob * d_p, axis=-1, keepdims=True)
            yield
            d_s = (prob * (d_p - dd)).astype(BF16)
            d_sink = jnp.where(lane == h, -jnp.sum(p_sink * dd, axis=0, keepdims=True), 0.0)
            dq = _dot(d_s, kg, NN)
            dk = _dot(d_s, qh, TN)
            dv = _dot(prob.astype(BF16), doh, TN)
            yield
            return dq * (HEAD_DIM ** -0.5), dk * (HEAD_DIM ** -0.5), dv, d_sink

        per_head = _interleave([head(h) for h in range(ATT_HEADS)])
        dqs = [t[0] for t in per_head]
        group_sum = lambda k, g: functools.reduce(jnp.add, [t[k] for t in per_head[g * GROUP:(g + 1) * GROUP]])
        dks = [group_sum(1, g) for g in range(KV_HEADS)]
        dvs = [group_sum(2, g) for g in range(KV_HEADS)]
        d_sink = functools.reduce(jnp.add, [t[3] for t in per_head])
        dq_ref[...] = _rope_wide(jnp.concatenate(dqs, axis=1), c0[...], -l0[...], -h0[...]).astype(dq_ref.dtype)
        d_k = jnp.concatenate(dks, axis=1)
        d_v = jnp.concatenate(dvs, axis=1)
        cur = pl.ds(pl.multiple_of(i * BLK, BLK), BLK)
        prv = pl.ds(pl.multiple_of(jnp.maximum(i - 1, 0) * BLK, BLK), BLK)
        dk_ref[prv, :] += _rope(d_k[:BLK], c1[...], -l1[...], -h1[...])
        dk_ref[cur, :] += _rope(d_k[BLK:], c0[...], -l0[...], -h0[...])
        dv_ref[prv, :] += d_v[:BLK]
        dv_ref[cur, :] += d_v[BLK:]
        ds_ref[...] += d_sink

    full = pl.BlockSpec((s, LANE), lambda i: (0, 0))
    return pl.pallas_call(body, grid=(s // BLK,), in_specs=_attn_specs() + [pl.BlockSpec((BLK, ATT_W), lambda i: (i, 0))],
                          out_specs=[pl.BlockSpec((BLK, ATT_W), lambda i: (i, 0)), full, full, pl.BlockSpec((1, LANE), lambda i: (0, 0))],
                          out_shape=[_sds((s, ATT_W), BF16), _sds((s, LANE), F32), _sds((s, LANE), F32), _sds((1, LANE), F32)],
                          compiler_params=_params(1), name="attn_bwd")(proj, proj, proj, proj, proj, *tabs, *tabs, sinks, d_att)


def _tri_matmul(tri, t):
    hi = t.astype(BF16)
    r1 = t - hi.astype(F32)
    mid = r1.astype(BF16)
    lo = (r1 - mid.astype(F32)).astype(BF16)
    return _dot(tri, hi, NN) + _dot(tri, mid, NN) + _dot(tri, lo, NN)


def _lower_bound(hl):
    a, b = hl[0:1, :], hl[1:2, :]
    mx = jnp.maximum(a, b)
    ea, eb = jnp.exp(a - mx), jnp.exp(b - mx)
    return ea / (ea + eb)


def _hg_gates(q_raw, f_raw, lb, tri_lower):
    sg = _sig(f_raw)
    f = lb + (1.0 - lb) * sg
    sq = _sig(q_raw)
    b = _tri_matmul(tri_lower, jnp.log(f))
    return sg, f, 1.0 - f, sq, q_raw * sq, b


HG_PAIR_FWD = 8
HG_PAIR_BWD = 8


def _hg_specs(n_map, pair):
    blk = lambda off, p: pl.BlockSpec((HG_TB, LANE), lambda h, n: (n_map(n), off // LANE + pair * h + p))
    return [blk(off, p) for off in (Q_H, F_H, I_H) for p in range(pair)] + [pl.BlockSpec((2, pair * LANE), lambda h, n: (0, h))]


def _interleave(gens):
    out = [None] * len(gens)
    live = list(range(len(gens)))
    while live:
        for k in list(live):
            try:
                next(gens[k])
            except StopIteration as stop:
                out[k] = stop.value
                live.remove(k)
    return out


def _hg_spread():
    c = lax.broadcasted_iota(jnp.int32, (CHUNK, SUB * SUB), 0)
    l = lax.broadcasted_iota(jnp.int32, (CHUNK, SUB * SUB), 1)
    r = lax.broadcasted_iota(jnp.int32, (SUB, SUB * SUB), 0)
    lr = lax.broadcasted_iota(jnp.int32, (SUB, SUB * SUB), 1)
    cols = [(c == lo + (l >> 4)).astype(BF16) for lo in range(0, CHUNK, SUB)]
    tile = [(c == lo + (l & (SUB - 1))).astype(BF16) for lo in range(0, CHUNK, SUB)]
    return cols, tile, (lr & (SUB - 1)) == r, (lr >> 4) == r


def _hg_intra(qs, kk, b, grad=None):
    lane = lax.broadcasted_iota(jnp.int32, (SUB, CHUNK), 1)
    row1 = lax.broadcasted_iota(jnp.int32, (SUB, 1), 0)
    kk_b = kk.astype(BF16)
    if grad is not None:
        d_a, d_at, (cols, tile, diag, block) = grad
    a_blocks, dq_blocks, dk_blocks, db_blocks = [], [], [], []
    dk_left = None
    for j in range(CHUNK // SUB):
        lo = j * SUB
        q_j, k_j, b_j = qs[lo:lo + SUB], kk[lo:lo + SUB], b[lo:lo + SUB]
        es = [jnp.where(row1 >= sx, jnp.exp(jnp.minimum(b_j - b_j[sx:sx + 1], 0.0)), 0.0) for sx in range(SUB)]
        pes = [q_j * e for e in es]
        pe = jnp.concatenate(pes, axis=0).astype(BF16)
        pairs = _dot(pe, kk_b, NT)
        yield
        a_j = jnp.zeros((SUB, CHUNK), F32)
        for sx in range(SUB):
            a_j = jnp.where(lane == lo + sx, pairs[sx * SUB:(sx + 1) * SUB], a_j)
        if grad is not None:
            da_j = d_a[lo:lo + SUB]
            ek = jnp.concatenate([e * k_j[sx:sx + 1] for sx, e in enumerate(es)], axis=0).astype(BF16)
            sel_t = jnp.where(diag, _dot(da_j.astype(BF16), cols[j], NN), 0.0).astype(BF16)
            sel_s = jnp.where(block, _dot(d_at[lo:lo + SUB].astype(BF16), tile[j], NN), 0.0).astype(BF16)
            pek = jnp.concatenate([p * k_j[sx:sx + 1] for sx, p in enumerate(pes)], axis=0).astype(BF16)
            yield
            dq_j = _dot(sel_t, ek, NN)
            dk_j = _dot(sel_s, pe, NN)
            db_j = _dot(sel_t, pek, NN) - _dot(sel_s, pek, NN)
            yield
        if j > 0:
            ref = b[lo - 1:lo]
            sc_q = jnp.exp(b_j - ref)
            sc_k = jnp.exp(jnp.minimum(ref - b, 0.0))
            qt = (q_j * sc_q).astype(BF16)
            kt = (kk * sc_k).astype(BF16)
            left = _dot(qt, kt, NT)
            yield
            a_j = a_j + jnp.where(lane < lo, left, 0.0)
            if grad is not None:
                da_left = jnp.where(lane < lo, da_j, 0.0).astype(BF16)
                dq_left = _dot(da_left, kt, NN) * sc_q
                dq_j = dq_j + dq_left
                db_j = db_j + q_j * dq_left
                t = _dot(da_left, qt, TN)
                yield
                t = t * sc_k
                dk_left = t if dk_left is None else dk_left + t
        a_blocks.append(a_j)
        if grad is not None:
            dq_blocks.append(dq_j)
            dk_blocks.append(dk_j)
            db_blocks.append(db_j)
    a = jnp.concatenate(a_blocks, axis=0)
    if grad is None:
        return a
    return a, jnp.concatenate(dq_blocks, axis=0), jnp.concatenate(dk_blocks, axis=0) + dk_left, jnp.concatenate(db_blocks, axis=0) - kk * dk_left


def _hgrn_fwd(proj, hl):
    s = proj.shape[0]
    n_chunk = HG_TB // CHUNK
    pair = HG_PAIR_FWD

    def body(*refs):
        q_refs, f_refs, i_refs = refs[:pair], refs[pair:2 * pair], refs[2 * pair:3 * pair]
        hl_ref, o_ref, st_out_ref, st_ref = refs[3 * pair:]

        @pl.when(pl.program_id(1) == 0)
        def _():
            st_ref[...] = jnp.zeros_like(st_ref)

        r_i = lax.broadcasted_iota(jnp.int32, (CHUNK, CHUNK), 0)
        c_i = lax.broadcasted_iota(jnp.int32, (CHUNK, CHUNK), 1)
        tri_lower = (r_i >= c_i).astype(BF16)

        def chunk(c, carry):
            rows = pl.ds(pl.multiple_of(c * CHUNK, CHUNK), CHUNK)
            def head(p):
                cols = slice(p * LANE, (p + 1) * LANE)
                lb = _lower_bound(hl_ref[:, cols])
                v = i_refs[p][rows, :].astype(BF16)
                _, _, kk, _, qs, b = _hg_gates(q_refs[p][rows, :], f_refs[p][rows, :], lb, tri_lower)
                yield
                st = st_ref[p]
                st_b = st.astype(BF16)
                st_out_ref[p, c] = st_b
                o_state = _dot((qs * jnp.exp(b)).astype(BF16), st_b, NT)
                b_last = b[CHUNK - 1:CHUNK, :]
                st_new = _dot(v, (kk * jnp.exp(b_last - b)).astype(BF16), TN)
                a = yield from _hg_intra(qs, kk, b)
                st_ref[p] = st * jnp.exp(b_last) + st_new
                o_ref[rows, cols] = o_state + _dot(a.astype(BF16), v, NN)

            _interleave([head(p) for p in range(pair)])
            return carry

        lax.fori_loop(0, n_chunk, chunk, 0)

    return pl.pallas_call(
        body, grid=(HG_HEADS // pair, s // HG_TB), in_specs=_hg_specs(lambda n: n, pair),
        out_specs=[pl.BlockSpec((HG_TB, pair * LANE), lambda h, n: (n, h)), pl.BlockSpec((pair, n_chunk, HG_K, HG_K), lambda h, n: (h, n, 0, 0))],
        out_shape=[_sds((s, HG_W), F32), _sds((HG_HEADS, s // CHUNK, HG_K, HG_K), BF16)],
        scratch_shapes=[pltpu.VMEM((pair, HG_K, HG_K), F32)],
        compiler_params=_params(2), name="hgrn_fwd")(*[proj] * (3 * pair), hl)


def _hgrn_bwd(proj, hl, states, d_o):
    s = proj.shape[0]
    n_chunk = HG_TB // CHUNK
    n_blk = s // HG_TB
    pair = HG_PAIR_BWD
    rev = lambda n: n_blk - 1 - n

    def body(*refs):
        q_refs, f_refs, i_refs = refs[:pair], refs[pair:2 * pair], refs[2 * pair:3 * pair]
        hl_ref, st_in_ref, do_ref, dq_ref, df_ref, di_ref, dhl_ref, dst_ref, dlb_ref = refs[3 * pair:]
        n = pl.program_id(1)

        @pl.when(n == 0)
        def _():
            dst_ref[...] = jnp.zeros_like(dst_ref)
            dlb_ref[...] = jnp.zeros_like(dlb_ref)

        r_i = lax.broadcasted_iota(jnp.int32, (CHUNK, CHUNK), 0)
        c_i = lax.broadcasted_iota(jnp.int32, (CHUNK, CHUNK), 1)
        tri_lower = (r_i >= c_i).astype(BF16)
        tri_upper = (r_i <= c_i).astype(BF16)
        row = lax.broadcasted_iota(jnp.int32, (CHUNK, 1), 0)
        spread = _hg_spread()

        def chunk(cc, carry):
            c = n_chunk - 1 - cc
            rows = pl.ds(pl.multiple_of(c * CHUNK, CHUNK), CHUNK)
            def head(p):
                cols = slice(p * LANE, (p + 1) * LANE)
                lb = _lower_bound(hl_ref[:, cols])
                q_raw = q_refs[p][rows, :]
                vb = i_refs[p][rows, :].astype(BF16)
                sg, f, kk, sq, qs, b = _hg_gates(q_raw, f_refs[p][rows, :], lb, tri_lower)
                yield
                e_b = jnp.exp(b)
                qe = qs * e_b
                b_last = b[CHUNK - 1:CHUNK, :]
                e_last = jnp.exp(b_last)
                e_kd = jnp.exp(b_last - b)
                kd = kk * e_kd
                st0 = st_in_ref[p, c]
                d_ob = do_ref[rows, cols].astype(BF16)
                dst = dst_ref[p]
                dst_b = dst.astype(BF16)
                d_a = jnp.where(r_i >= c_i, _dot(d_ob, vb, NT), 0.0)
                d_at = jnp.where(r_i <= c_i, _dot(vb, d_ob, NT), 0.0)
                d_v_st = _dot(kd.astype(BF16), dst_b, NT)
                d_kd = _dot(vb, dst_b, NN)
                d_qe = _dot(d_ob, st0, NN)
                dst_new = _dot(d_ob, qe.astype(BF16), TN)
                yield
                a, dqs, dkk, d_b = yield from _hg_intra(qs, kk, b, (d_a, d_at, spread))
                d_v = _dot(a.astype(BF16), d_ob, TN) + d_v_st
                dqs_st = d_qe * e_b
                dkk_st = d_kd * e_kd
                dqs = dqs + dqs_st
                dkk = dkk + dkk_st
                d_b_last = jnp.sum(d_kd * kd, axis=0, keepdims=True) + jnp.sum(dst * st0.astype(F32), axis=0, keepdims=True) * e_last
                d_b = d_b + qs * dqs_st - kk * dkk_st + jnp.where(row == CHUNK - 1, d_b_last, 0.0)
                d_g = _tri_matmul(tri_upper, d_b)
                dst_ref[p] = dst_new + dst * e_last
                yield
                d_f = d_g / f - dkk
                dlb_ref[:, cols] += jnp.sum(d_f * (1.0 - sg), axis=0, keepdims=True)
                dq_ref[rows, cols] = (dqs * (sq * (1.0 + q_raw * (1.0 - sq)))).astype(dq_ref.dtype)
                df_ref[rows, cols] = (d_f * (1.0 - lb) * (sg * (1.0 - sg))).astype(df_ref.dtype)
                di_ref[rows, cols] = d_v.astype(di_ref.dtype)

            _interleave([head(p) for p in range(pair)])
            return carry

        lax.fori_loop(0, n_chunk, chunk, 0)

        @pl.when(n == n_blk - 1)
        def _():
            lb = _lower_bound(hl_ref[...])
            d_hl0 = dlb_ref[...] * (lb * (1.0 - lb))
            dhl_ref[...] = jnp.concatenate([d_hl0, -d_hl0], axis=0)

    out_blk = pl.BlockSpec((HG_TB, pair * LANE), lambda h, n: (rev(n), h))
    return pl.pallas_call(
        body, grid=(HG_HEADS // pair, n_blk),
        in_specs=_hg_specs(rev, pair) + [pl.BlockSpec((pair, n_chunk, HG_K, HG_K), lambda h, n: (h, rev(n), 0, 0)), out_blk],
        out_specs=[out_blk, out_blk, out_blk, pl.BlockSpec((2, pair * LANE), lambda h, n: (0, h))],
        out_shape=[_sds((s, HG_W), BF16)] * 3 + [_sds((2, HG_W), F32)],
        scratch_shapes=[pltpu.VMEM((pair, HG_K, HG_K), F32), pltpu.VMEM((1, pair * LANE), F32)],
        compiler_params=_params(2), name="hgrn_bwd")(*[proj] * (3 * pair), hl, states, d_o)


def _mod_part(c_all, w_shard, b_shard):
    n = w_shard.shape[1]
    tn = 512

    def body(c_ref, w_ref, b_ref, o_ref):
        o_ref[...] = _dot(c_ref[...].astype(BF16), w_ref[...].astype(BF16), NN) + b_ref[...]

    return pl.pallas_call(body, grid=(n // tn,),
                          in_specs=[pl.BlockSpec((N_DEV, D), lambda j: (0, 0)), pl.BlockSpec((D, tn), lambda j: (0, j)), pl.BlockSpec((1, tn), lambda j: (0, j))],
                          out_specs=pl.BlockSpec((N_DEV, tn), lambda j: (0, j)), out_shape=_sds((N_DEV, n), F32),
                          compiler_params=_params(1, 32 << 20), name="mod_part")(c_all, w_shard, b_shard)


def _grad_w_ada(c_all_t, dmod_cols):
    n = dmod_cols.shape[1]
    tn = 512

    def body(c_ref, d_ref, o_ref):
        cv = c_ref[...].astype(BF16).astype(F32)
        dv = d_ref[...].astype(BF16).astype(F32)
        acc = cv[:, 0:1] * dv[0:1, :]
        for k in range(1, N_DEV):
            acc = acc + cv[:, k:k + 1] * dv[k:k + 1, :]
        o_ref[...] = acc

    return pl.pallas_call(body, grid=(n // tn,),
                          in_specs=[pl.BlockSpec((D, N_DEV), lambda j: (0, 0)), pl.BlockSpec((N_DEV, tn), lambda j: (0, j))],
                          out_specs=pl.BlockSpec((D, tn), lambda j: (0, j)), out_shape=_sds((D, n), F32),
                          compiler_params=_params(1, 32 << 20), name="grad_w_ada")(c_all_t, dmod_cols)


def _row_tile(r, c, max_elems=1 << 18):
    if r * c <= max_elems or r % 8:
        return r
    best = 8
    for t in range(8, r + 1, 8):
        if r % t == 0 and t * c <= max_elems:
            best = t
    return best


WIDE_TILE = 5 << 17


def _sum_pieces(pieces, own, name):
    p, r, c = pieces.shape
    tr = _row_tile(r, c, WIDE_TILE)

    def body(o_ref, p_ref, g_ref):
        g = o_ref[...].astype(F32)
        for k in range(p):
            g = g + p_ref[k].astype(F32)
        g_ref[...] = g

    blk = pl.BlockSpec((tr, c), lambda i: (i, 0))
    return pl.pallas_call(body, grid=(r // tr,), in_specs=[blk, pl.BlockSpec((p, tr, c), lambda i: (0, i, 0))], out_specs=blk,
                          out_shape=_sds((r, c), F32), compiler_params=_params(1), name=name)(own, pieces)


def _adamw(pieces, w, m, v, name, emit_grad=True, own=None):
    p, r, c = pieces.shape
    tr = _row_tile(r, c)
    c1 = 1.0 / (1.0 - ADAM_B1 ** ADAM_STEP)
    c2 = 1.0 / (1.0 - ADAM_B2 ** ADAM_STEP)

    def body(*refs):
        if own is None:
            p_ref, w_ref, m_ref, v_ref, *outs = refs
            g = p_ref[0].astype(F32)
        else:
            o_ref, p_ref, w_ref, m_ref, v_ref, *outs = refs
            g = o_ref[...].astype(F32) + p_ref[0].astype(F32)
        for k in range(1, p):
            g = g + p_ref[k].astype(F32)
        m2 = ADAM_B1 * m_ref[...] + (1.0 - ADAM_B1) * g
        v2 = ADAM_B2 * v_ref[...] + (1.0 - ADAM_B2) * (g * g)
        delta = -ADAM_LR * ((m2 * c1) / (jnp.sqrt(v2 * c2) + ADAM_EPS) + ADAM_WD * w_ref[...])
        if emit_grad:
            outs[0][...] = g
        outs[-3][...] = delta
        outs[-2][...] = m2
        outs[-1][...] = v2

    blk = pl.BlockSpec((tr, c), lambda i: (i, 0))
    n_out = 4 if emit_grad else 3
    lead = [] if own is None else [own]
    return pl.pallas_call(body, grid=(r // tr,), in_specs=[blk] * len(lead) + [pl.BlockSpec((p, tr, c), lambda i: (0, i, 0)), blk, blk, blk],
                          out_specs=[blk] * n_out, out_shape=[_sds((r, c), F32)] * n_out,
                          compiler_params=_params(1, 48 << 20), name=name)(*lead, pieces, w, m, v)


def _my_coords():
    return lax.axis_index("x"), lax.axis_index("y"), lax.axis_index("c")


def _flip(coords, k):
    x, y, c = coords
    return (1 - x if k & 4 else x, 1 - y if k & 2 else y, 1 - c if k & 1 else c)


def _lin(coords):
    return 4 * coords[0] + 2 * coords[1] + coords[2]


def _exchange_small(x3, bcast, name):
    n = x3.shape[2]

    def body(x_ref, o_ref, send_sems, recv_sems):
        me = _my_coords()
        my_id = _lin(me)
        o_ref[pl.ds(my_id, 1)] = x_ref[pl.ds(0 if bcast else my_id, 1)]
        copies = []
        for k in range(1, N_DEV):
            peer = _flip(me, k)
            src = x_ref.at[0 if bcast else _lin(peer)]
            cp = pltpu.make_async_remote_copy(src_ref=src, dst_ref=o_ref.at[my_id], send_sem=send_sems.at[k], recv_sem=recv_sems.at[k],
                                              device_id=peer, device_id_type=MESH)
            cp.start()
            copies.append(cp)
        for k in range(1, N_DEV):
            peer = _flip(me, k)
            pltpu.make_async_remote_copy(src_ref=x_ref.at[0], dst_ref=o_ref.at[_lin(peer)], send_sem=send_sems.at[k], recv_sem=recv_sems.at[k],
                                         device_id=peer, device_id_type=MESH).wait_recv()
        for cp in copies:
            cp.wait_send()

    vm = pl.BlockSpec(memory_space=pltpu.VMEM)
    return pl.pallas_call(body, in_specs=[vm], out_specs=vm, out_shape=_sds((N_DEV, 1, n), F32),
                          scratch_shapes=[pltpu.SemaphoreType.DMA((N_DEV,)), pltpu.SemaphoreType.DMA((N_DEV,))], name=name)(x3)


HBM_SPEC = pl.BlockSpec(memory_space=pltpu.HBM)
SEM_SPEC = pl.BlockSpec(memory_space=pltpu.SEMAPHORE)
ANY_SPEC = pl.BlockSpec(memory_space=pl.ANY)
DATAFLOW = pltpu.SideEffectType.DATAFLOW_SIDE_EFFECTING
GATHER_FLIPS = (1, 2, 4, 6)
PASS_FLIPS = (2, 4, 6)
TOKEN = (8, LANE)


def _hbm(t):
    return pltpu.with_memory_space_constraint(t, pltpu.HBM)


def _hbm_like(ts):
    return [pltpu.HBM(t.shape, t.dtype) for t in ts]


def _split_start(issue, srcs, lands, n_sem, name, deps=()):
    n, nd = len(srcs), len(deps)

    def body(*refs):
        issue(refs[:n], refs[n:2 * n], refs[2 * n + nd], refs[2 * n + nd + 1])
        refs[-1][...] = jnp.zeros(TOKEN, F32)

    outs = pl.pallas_call(
        body, name=name,
        out_shape=(pltpu.SemaphoreType.DMA((n_sem,)), pltpu.SemaphoreType.DMA((n_sem,)), *_hbm_like(srcs), *_hbm_like(lands), _sds(TOKEN, F32)),
        in_specs=[HBM_SPEC] * (2 * n) + [ANY_SPEC] * nd,
        out_specs=(SEM_SPEC, SEM_SPEC, *[HBM_SPEC] * (2 * n), pl.BlockSpec(memory_space=pltpu.VMEM)),
        input_output_aliases={i: 2 + i for i in range(2 * n)},
        compiler_params=pltpu.CompilerParams(has_side_effects=DATAFLOW))(*[_hbm(t) for t in srcs], *[_hbm(t) for t in lands], *deps)
    return dict(sems=outs[:2], thru=list(outs[2:2 + 2 * n]), token=outs[-1], n=n)


def _split_wait(finish, handle, after, name):
    n = handle["n"]
    thru = handle["thru"]

    def body(*refs):
        finish(refs[:n], refs[n:2 * n], refs[2 * n], refs[2 * n + 1])

    outs = pl.pallas_call(
        body, name=name, out_shape=_hbm_like(thru), in_specs=[HBM_SPEC] * (2 * n) + [SEM_SPEC, SEM_SPEC] + [ANY_SPEC] * len(after),
        out_specs=[HBM_SPEC] * (2 * n), input_output_aliases={i: i for i in range(2 * n)},
        compiler_params=pltpu.CompilerParams(has_side_effects=DATAFLOW))(*thru, *handle["sems"], *after)
    return list(outs[:n]), list(outs[n:])


def _gather_start(shards, name, deps=()):
    n = len(shards)
    my_id = _lin(_my_coords())
    lands = [lax.dynamic_update_slice(lax.empty((N_DEV,) + t.shape, t.dtype), t[None], (my_id, 0, 0)) for t in shards]

    def issue(src, land, send_sems, recv_sems):
        me = _my_coords()
        for w in range(n):
            for j, k in enumerate(GATHER_FLIPS):
                q = len(GATHER_FLIPS) * w + j
                pltpu.make_async_remote_copy(src_ref=src[w], dst_ref=land[w].at[_lin(me)], send_sem=send_sems.at[q], recv_sem=recv_sems.at[q],
                                             device_id=_flip(me, k), device_id_type=MESH).start()

    return _split_start(issue, shards, lands, len(GATHER_FLIPS) * n, name, deps)


def _gather_wait(handle, after, name):
    n = handle["n"]

    def finish(src, land, send_sems, recv_sems):
        me = _my_coords()
        for w in range(n):
            for j, k in enumerate(GATHER_FLIPS):
                q = len(GATHER_FLIPS) * w + j
                peer = _flip(me, k)
                cp = pltpu.make_async_remote_copy(src_ref=src[w], dst_ref=land[w].at[_lin(peer)], send_sem=send_sems.at[q], recv_sem=recv_sems.at[q],
                                                  device_id=peer, device_id_type=MESH)
                cp.wait_send()
                cp.wait_recv()

    return _split_wait(finish, handle, after, name)[1]


def _gather_pass(lands, name):
    n = len(lands)
    n_p = len(PASS_FLIPS)

    def body(*refs):
        land = refs[n:2 * n]
        send_sems, recv_sems = refs[2 * n:]
        me = _my_coords()
        sibling = _flip(me, 1)
        sent = []
        for w in range(n):
            for j, k in enumerate(PASS_FLIPS):
                blk = land[w].at[_lin(_flip(me, k))]
                cp = pltpu.make_async_remote_copy(src_ref=blk, dst_ref=blk, send_sem=send_sems.at[n_p * w + j], recv_sem=recv_sems.at[n_p * w + j],
                                                  device_id=sibling, device_id_type=MESH)
                cp.start()
                sent.append(cp)
        for w in range(n):
            for j, k in enumerate(PASS_FLIPS):
                blk = land[w].at[_lin(_flip(me, k + 1))]
                pltpu.make_async_remote_copy(src_ref=blk, dst_ref=blk, send_sem=send_sems.at[n_p * w + j], recv_sem=recv_sems.at[n_p * w + j],
                                             device_id=sibling, device_id_type=MESH).wait_recv()
        for cp in sent:
            cp.wait_send()

    return pl.pallas_call(body, in_specs=[ANY_SPEC] * n, out_specs=[ANY_SPEC] * n, out_shape=[_sds(t.shape, t.dtype) for t in lands],
                          input_output_aliases={i: i for i in range(n)},
                          scratch_shapes=[pltpu.SemaphoreType.DMA((n_p * n,)), pltpu.SemaphoreType.DMA((n_p * n,))], name=name)(*lands)


CHIP_FLIPS = (0, 2, 4, 6)


def _pair_exchange(grads, name):
    n = len(grads)
    n_c = len(CHIP_FLIPS)

    def body(*refs):
        src, theirs = refs[:n], refs[n:2 * n]
        send_sems, recv_sems = refs[2 * n:]
        me = _my_coords()
        sibling = _flip(me, 1)
        sent = []
        for w in range(n):
            for j, k in enumerate(CHIP_FLIPS):
                q = n_c * w + j
                cp = pltpu.make_async_remote_copy(src_ref=src[w].at[_lin(_flip(me, k + 1))], dst_ref=theirs[w].at[j], send_sem=send_sems.at[q],
                                                  recv_sem=recv_sems.at[q], device_id=sibling, device_id_type=MESH)
                cp.start()
                sent.append(cp)
        for cp in sent:
            cp.wait_recv()
        for cp in sent:
            cp.wait_send()

    outs = pl.pallas_call(body, in_specs=[ANY_SPEC] * n, out_specs=[ANY_SPEC] * n, out_shape=[_sds((n_c,) + g.shape[1:], g.dtype) for g in grads],
                          scratch_shapes=[pltpu.SemaphoreType.DMA((n_c * n,))] * 2, name=name)(*grads)
    return list(outs)


def _pair_add(grad, theirs, name):
    p, r, c = theirs.shape
    tr = _row_tile(r, c, WIDE_TILE)
    me = _my_coords()
    ids = jnp.stack([_lin(_flip(me, k)) for k in CHIP_FLIPS]).astype(jnp.int32)

    def body(ids_ref, a_ref, b_ref, o_ref):
        o_ref[...] = (a_ref[...].astype(F32) + b_ref[...].astype(F32)).astype(o_ref.dtype)

    blk = pl.BlockSpec((None, tr, c), lambda j, i, ids_ref: (j, i, 0))
    return pl.pallas_call(
        body, out_shape=_sds((p, r, c), theirs.dtype), compiler_params=_params(2), name=name,
        grid_spec=pltpu.PrefetchScalarGridSpec(
            num_scalar_prefetch=1, grid=(p, r // tr),
            in_specs=[pl.BlockSpec((None, tr, c), lambda j, i, ids_ref: (ids_ref[j], i, 0)), blk], out_specs=blk))(ids, grad, theirs)


def _chips_start(parts, name, deps=()):
    n = len(parts)
    n_c = len(CHIP_FLIPS) - 1
    lands = [lax.empty((n_c,) + t.shape[1:], t.dtype) for t in parts]

    def issue(src, land, send_sems, recv_sems):
        me = _my_coords()
        for w in range(n):
            for j in range(1, n_c + 1):
                q = n_c * w + j - 1
                pltpu.make_async_remote_copy(src_ref=src[w].at[j], dst_ref=land[w].at[j - 1], send_sem=send_sems.at[q], recv_sem=recv_sems.at[q],
                                             device_id=_flip(me, CHIP_FLIPS[j]), device_id_type=MESH).start()

    return _split_start(issue, parts, lands, n_c * n, name, deps)


def _chips_wait(handle, after, name):
    n = handle["n"]
    n_c = len(CHIP_FLIPS) - 1

    def finish(src, land, send_sems, recv_sems):
        me = _my_coords()
        for w in range(n):
            for j in range(1, n_c + 1):
                q = n_c * w + j - 1
                cp = pltpu.make_async_remote_copy(src_ref=src[w].at[j], dst_ref=land[w].at[j - 1], send_sem=send_sems.at[q], recv_sem=recv_sems.at[q],
                                                  device_id=_flip(me, CHIP_FLIPS[j]), device_id_type=MESH)
                cp.wait_send()
                cp.wait_recv()

    return _split_wait(finish, handle, after, name)


def _after(t, *tokens):
    for tok in tokens:
        t = t + tok[0:1, 0:1]
    return t


def _rope_tables(positions):
    half = ROT // 2
    inv_freq = ROPE_THETA ** (-jnp.arange(0, ROT, 2, dtype=F32) / ROT)
    ang = positions.astype(F32).reshape(-1, 1) * inv_freq
    cos, sin = jnp.cos(ang), jnp.sin(ang)
    s = ang.shape[0]
    pad = jnp.zeros((s, HEAD_DIM - ROT), F32)
    zero = jnp.zeros((s, half), F32)
    two = lambda t: jnp.concatenate([t, t], axis=1)
    return (two(jnp.concatenate([cos, cos, pad + 1.0], axis=1)), two(jnp.concatenate([-sin, zero, pad], axis=1)),
            two(jnp.concatenate([zero, sin, pad], axis=1)))


def _local_step(x, tgt, tabs, mod, sinks_pad, hl, hg_norm, g_pre_mix, g_post_mix, g_pre_ffn, g_post_ffn, weights, scatter):
    s = x.shape[0]
    h1 = _pre_fwd(x, g_pre_mix, mod, 1, 0, "pre_mix_fwd")
    (w_in_t,) = weights("in", h1)
    proj = _mm_nt(h1, w_in_t, s, 256, D, F32, "proj_mm")
    att = _attn_fwd(proj, tabs, sinks_pad)
    o_raw, states = _hgrn_fwd(proj, hl)
    ohg = _hgout_fwd(o_raw, proj, hg_norm)
    w_attn_dm, w_hgrn_dm, w_out = weights("mix", ohg)
    y_a = _mm_nn_dm(att, w_attn_dm, s, F32, "attn_proj_mm")
    y_h = _mm_nn_dm(ohg, w_hgrn_dm, s, F32, "hgrn_proj_mm")
    merged = _merge_fwd(y_a, y_h, proj)
    y = _mm_nn(merged, w_out, s, 512, D, F32, "out_mm")
    x1 = _post_fwd(x, y, g_post_mix, mod, 2, "post_mix_fwd")
    h2 = _pre_fwd(x1, g_pre_ffn, mod, 4, 3, "pre_ffn_fwd")
    w_ffn_in_dm, w_ffn_out = weights("ffn", h2)
    gu = _mm_nn_dm(h2, w_ffn_in_dm, s // 2, F32, "ffn_in_mm")
    act = _swiglu_fwd(gu)
    y2 = _mm_nn(act, w_ffn_out, s, 512, FFN // 4, F32, "ffn_out_mm")
    err, loss = _post_fwd_loss(x1, y2, g_post_ffn, mod, 5, tgt, "post_ffn_loss")
    dy2, d_gate2, dg_post_ffn = _post_bwd(err, y2, g_post_ffn, mod, 5, "post_ffn_bwd")
    d_act = _mm_nt(dy2, w_ffn_out, s, 512, D, F32, "ffn_out_dx")
    gw_ffn_out = _mm_tn(act, dy2, 512, D, BF16, "ffn_out_dw")
    t_out = scatter([gw_ffn_out.reshape(N_DEV, FFN // N_DEV, D)], "ffn_out")
    dgu = _swiglu_bwd(d_act, gu)
    dh2 = _mm_nt_dm(dgu, w_ffn_in_dm, s, 1024, F32, "ffn_in_dx")
    gw_ffn_in = _mm_tn_dm(h2, dgu, 512, BF16, "ffn_in_dw")
    t_in = scatter([gw_ffn_in], "ffn_in")
    mod = _after(mod, t_out, t_in)
    dx1, d_shift2, d_scale2, dg_pre_ffn = _pre_bwd(dh2, x1, err, g_pre_ffn, mod, 4, "pre_ffn_bwd")
    dy, d_gate1, dg_post_mix = _post_bwd(dx1, y, g_post_mix, mod, 2, "post_mix_bwd")
    d_merged = _mm_nt(dy, w_out, s, 512, D, F32, "out_dx")
    gw_out = _mm_tn(merged, dy, 512, D, BF16, "out_dw")
    dy_a, dy_h, d_gate_a, d_gate_h = _merge_bwd(d_merged, y_a, y_h, proj)
    d_att = _mm_nt_dm(dy_a, w_attn_dm, s, ATT_W, F32, "attn_proj_dx")
    gw_attn = _mm_tn_dm(att, dy_a, ATT_W, BF16, "attn_proj_dw")
    d_ohg = _mm_nt_dm(dy_h, w_hgrn_dm, s, HG_W, F32, "hgrn_proj_dx")
    gw_hgrn = _mm_tn_dm(ohg, dy_h, HG_W, BF16, "hgrn_proj_dw")
    t_mix = scatter([gw_attn, gw_hgrn, gw_out.reshape(N_DEV, D // N_DEV, D)], "mix")
    d_o, d_gh, d_hg_norm = _hgout_bwd(d_ohg, o_raw, proj, _after(hg_norm, t_mix))
    d_qh, d_fh, d_ih, d_hl = _hgrn_bwd(proj, hl, states, d_o)
    d_qa, d_ka, d_va, d_sinks = _attn_bwd(proj, tabs, sinks_pad, d_att)
    d_proj = jnp.concatenate([d_qa, d_ka.astype(BF16), d_va.astype(BF16), d_qh, d_fh, d_ih, d_gh, d_gate_a, d_gate_h], axis=1)
    dh1 = _mm_nn(d_proj, w_in_t, s // 2, 512, IN_COLS // 2, F32, "proj_dx")
    grad_x, d_shift1, d_scale1, dg_pre_mix = _pre_bwd(dh1, x, dx1, g_pre_mix, mod, 1, "pre_mix_bwd")
    d_mod = jnp.concatenate([d_shift1, d_scale1, d_gate1, d_shift2, d_scale2, d_gate2], axis=1)
    small = [d_mod, dg_pre_mix, dg_post_mix, dg_pre_ffn, dg_post_ffn, d_hl.reshape(1, 2 * HG_W), d_hg_norm, d_sinks]
    return loss, grad_x, small, h1, d_proj


def kernel(x, c, positions, w_ada, b_ada, g_pre_mix, g_post_mix, g_pre_ffn, g_post_ffn, w_in, attn_sinks, w_attn_proj, hg_lower_bounds, hg_norm, w_hgrn_proj, w_out, w_ffn_in, w_ffn_out, loss_target, m_w_ada, m_b_ada, m_g_pre_mix, m_g_post_mix, m_g_pre_ffn, m_g_post_ffn, m_w_in, m_attn_sinks, m_w_attn_proj, m_hg_lower_bounds, m_hg_norm, m_w_hgrn_proj, m_w_out, m_w_ffn_in, m_w_ffn_out, v_w_ada, v_b_ada, v_g_pre_mix, v_g_post_mix, v_g_pre_ffn, v_g_post_ffn, v_w_in, v_attn_sinks, v_w_attn_proj, v_hg_lower_bounds, v_hg_norm, v_w_hgrn_proj, v_w_out, v_w_ffn_in, v_w_ffn_out):
    my_id = _lin(_my_coords())
    s = x.shape[1]
    n_ada = w_ada.shape[2]

    c_all = _exchange_small(c.reshape(1, 1, D), True, "gather_c").reshape(N_DEV, D)
    b_cols = lax.dynamic_slice(b_ada, (0, my_id * n_ada), (1, n_ada))
    mod_part = _mod_part(c_all, w_ada[0], b_cols)
    mod = _exchange_small(mod_part.reshape(N_DEV, 1, n_ada), False, "scatter_mod").reshape(1, N_MOD * D)
    groups = {"in": [w_in[0].T], "mix": [w_attn_proj[0], w_hgrn_proj[0], w_out[0]], "ffn": [w_ffn_in[0], w_ffn_out[0]]}

    def start(group, dep):
        shards, dep = lax.optimization_barrier((groups[group], dep))
        return _gather_start([t.astype(BF16) for t in shards], "gather_start_" + group, deps=[dep])

    gathers = {"in": start("in", mod)}
    gathers["mix"] = start("mix", gathers["in"]["token"])
    gathers["ffn"] = start("ffn", gathers["mix"]["token"])

    def weights(group, after):
        after = [after, gathers["ffn"]["token"]]
        lands = _gather_pass(_gather_wait(gathers[group], after, "gather_wait_" + group), "gather_pass_" + group)
        if group == "in":
            return (lands[0].reshape(IN_COLS, D),)
        if group == "mix":
            return lands[0], lands[1], lands[2].reshape(D, D)
        return lands[0], lands[1].reshape(FFN, D)

    scatters = {}

    def scatter(grads, group):
        theirs = _pair_exchange(grads, "scatter_pair_" + group)
        parts = [_pair_add(g, t, "scatter_pair_add_%s_%d" % (group, k)) for k, (g, t) in enumerate(zip(grads, theirs))]
        scatters[group] = _chips_start(parts, "scatter_start_" + group)
        return scatters[group]["token"]

    sinks_pad = jnp.pad(attn_sinks, ((0, 0), (0, LANE - ATT_HEADS)))
    loss, grad_x, small, h1, d_proj = _local_step(
        x[0], loss_target[0], _rope_tables(positions), mod, sinks_pad, hg_lower_bounds, hg_norm, g_pre_mix, g_post_mix, g_pre_ffn, g_post_ffn,
        weights, scatter)
    loss = lax.psum(loss[0, 0], ("x", "y", "c"))

    sizes = [t.shape[1] for t in small]
    parts = _exchange_small(jnp.concatenate(small, axis=1).reshape(1, 1, sum(sizes)), True, "gather_small_grads")
    gw_in = _mm_tn(d_proj, h1, 256, D, BF16, "proj_dw", deps=[parts]).reshape(N_DEV, IN_COLS // N_DEV, D)
    scatter([gw_in], "in")
    offs = [sum(sizes[:k]) for k in range(len(sizes))]
    piece = lambda k, n=None: parts[:, :, offs[k]:offs[k] + (sizes[k] if n is None else n)]
    small_w = [(piece(0), b_ada, m_b_ada, v_b_ada), (piece(1), g_pre_mix, m_g_pre_mix, v_g_pre_mix),
               (piece(2), g_post_mix, m_g_post_mix, v_g_post_mix), (piece(3), g_pre_ffn, m_g_pre_ffn, v_g_pre_ffn),
               (piece(4), g_post_ffn, m_g_post_ffn, v_g_post_ffn),
               (piece(5).reshape(N_DEV, 2, HG_W), hg_lower_bounds, m_hg_lower_bounds, v_hg_lower_bounds),
               (piece(6), hg_norm, m_hg_norm, v_hg_norm), (piece(7, ATT_HEADS), attn_sinks, m_attn_sinks, v_attn_sinks)]
    names = ["b_ada", "g_pre_mix", "g_post_mix", "g_pre_ffn", "g_post_ffn", "hg_lower_bounds", "hg_norm", "attn_sinks"]
    res = {n: _adamw(p, w, m, v, "adamw_" + n) for n, (p, w, m, v) in zip(names, small_w)}

    dmod_cols = lax.dynamic_slice(parts.reshape(N_DEV, -1), (0, my_id * n_ada), (N_DEV, n_ada))
    g_w_ada = _grad_w_ada(c_all.T, dmod_cols)
    res["w_ada"] = [g_w_ada] + list(_adamw(g_w_ada[None], w_ada[0], m_w_ada[0], v_w_ada[0], "adamw_w_ada", emit_grad=False))

    big = {"ffn_out": [("w_ffn_out", w_ffn_out, m_w_ffn_out, v_w_ffn_out)], "ffn_in": [("w_ffn_in", w_ffn_in, m_w_ffn_in, v_w_ffn_in)],
           "mix": [("w_attn_proj", w_attn_proj, m_w_attn_proj, v_w_attn_proj), ("w_hgrn_proj", w_hgrn_proj, m_w_hgrn_proj, v_w_hgrn_proj),
                   ("w_out", w_out, m_w_out, v_w_out)],
           "in": [("w_in", w_in, m_w_in, v_w_in)]}
    after = [scatters["in"]["token"]]
    for group, members in big.items():
        if group == "in":
            after = after + [res["w_ada"][1], res["b_ada"][1]]
        local, lands = _chips_wait(scatters[group], after, "scatter_wait_" + group)
        own = [t[0] for t in local]
        for (n, w, m, v), g_own, land in zip(members, own, lands):
            if group == "in":
                g_w = _sum_pieces(land, g_own, "sum_" + n).T
                res[n] = [g_w] + list(_adamw(g_w[None], w[0], m[0], v[0], "adamw_" + n, emit_grad=False))
            else:
                res[n] = _adamw(land, w[0], m[0], v[0], "adamw_" + n, own=g_own)
            after = [res[n][1]]

    order = ["w_ada", "b_ada", "g_pre_mix", "g_post_mix", "g_pre_ffn", "g_post_ffn", "w_in", "attn_sinks", "w_attn_proj",
             "hg_lower_bounds", "hg_norm", "w_hgrn_proj", "w_out", "w_ffn_in", "w_ffn_out"]
    lead = {"w_ada", "w_in", "w_attn_proj", "w_hgrn_proj", "w_out", "w_ffn_in", "w_ffn_out"}
    outs = [loss, grad_x[None]]
    for k in range(4):
        outs += [res[n][k][None] if n in lead else res[n][k] for n in order]
    return tuple(outs)
```

```python
import functools

import jax
import jax.numpy as jnp
from jax import lax
from jax.experimental import pallas as pl
from jax.experimental.pallas import tpu as pltpu

F32 = jnp.float32
BF16 = jnp.bfloat16

N_DEV = 8
D = 2048
ATT_HEADS = 16
KV_HEADS = 2
HEAD_DIM = 64
GROUP = ATT_HEADS // KV_HEADS
ATT_W = ATT_HEADS * HEAD_DIM
BLK = 128
ROT = HEAD_DIM // 4
ROPE_THETA = 500000.0
HG_HEADS = 8
HG_K = 128
HG_W = HG_HEADS * HG_K
CHUNK = 64
SUB = 16
FFN = 5632
N_MOD = 6
EPS = 1e-6
LANE = 128
Q_A, K_A, V_A, Q_H, F_H, I_H, G_H, GT_A, GT_H, IN_COLS = 0, 1024, 1152, 1280, 2304, 3328, 4352, 5376, 7424, 9472

ADAM_LR, ADAM_B1, ADAM_B2, ADAM_EPS, ADAM_WD, ADAM_STEP = 0.001, 0.9, 0.999, 1e-08, 0.01, 10

TR = 256
HG_TB = 512
VMEM_BIG = 56 << 20
MESH = pl.DeviceIdType.MESH


def _sds(shape, dtype):
    return jax.ShapeDtypeStruct(shape, dtype)


def _params(n_axes, vmem=None):
    return pltpu.CompilerParams(dimension_semantics=("arbitrary",) * n_axes, vmem_limit_bytes=vmem)


def _sig(t):
    return 1.0 / (1.0 + jnp.exp(-t))


def _dot(a, b, dims):
    return lax.dot_general(a, b, (dims, ((), ())), preferred_element_type=F32)


NN = ((1,), (0,))
NT = ((1,), (1,))
TN = ((0,), (0,))


def _matmul(a, b, a_spec, b_spec, o_spec, out_shape, grid, dims, acc_shape, name, deps=()):
    nk = grid[2]
    nd = len(deps)

    def body(a_ref, b_ref, *rest):
        o_ref, scratch = rest[nd], rest[nd + 1:]
        part = _dot(a_ref[...], b_ref[...], dims)
        if nk == 1:
            o_ref[...] = part.astype(o_ref.dtype)
        else:
            acc = scratch[0]
            k = pl.program_id(2)

            @pl.when(k == 0)
            def _():
                acc[...] = part

            @pl.when(k > 0)
            def _():
                acc[...] += part

            @pl.when(k == nk - 1)
            def _():
                o_ref[...] = acc[...].astype(o_ref.dtype)

    return pl.pallas_call(
        body, grid=grid, in_specs=[a_spec, b_spec] + [pl.BlockSpec(memory_space=pl.ANY)] * nd, out_specs=o_spec, out_shape=out_shape,
        scratch_shapes=[pltpu.VMEM(acc_shape, F32)] if nk > 1 else [],
        compiler_params=_params(3, VMEM_BIG), name=name)(a, b, *deps)


def _mm_nn(a, b, tm, tn, tk, out_dtype, name):
    m, k = a.shape
    n = b.shape[1]
    return _matmul(a, b, pl.BlockSpec((tm, tk), lambda j, i, kk: (i, kk)), pl.BlockSpec((tk, tn), lambda j, i, kk: (kk, j)),
                   pl.BlockSpec((tm, tn), lambda j, i, kk: (i, j)), _sds((m, n), out_dtype),
                   (n // tn, m // tm, k // tk), NN, (tm, tn), name)


def _mm_nn_dm(a, b, tm, out_dtype, name):
    m, k = a.shape
    n = b.shape[2]
    return _matmul(a, b, pl.BlockSpec((tm, k), lambda j, i, kk: (i, 0)), pl.BlockSpec((None, k, n), lambda j, i, kk: (j, 0, 0)),
                   pl.BlockSpec((tm, n), lambda j, i, kk: (i, j)), _sds((m, N_DEV * n), out_dtype),
                   (N_DEV, m // tm, 1), NN, (tm, n), name)


def _mm_nt(a, b, tm, tn, tk, out_dtype, name, deps=()):
    m, k = a.shape
    n = b.shape[0]
    return _matmul(a, b, pl.BlockSpec((tm, tk), lambda j, i, kk: (i, kk)), pl.BlockSpec((tn, tk), lambda j, i, kk: (j, kk)),
                   pl.BlockSpec((tm, tn), lambda j, i, kk: (i, j)), _sds((m, n), out_dtype),
                   (n // tn, m // tm, k // tk), NT, (tm, tn), name, deps)


def _mm_nt_dm(a, b, tm, tn, out_dtype, name, deps=()):
    m = a.shape[0]
    n_out, n = b.shape[1], b.shape[2]
    return _matmul(a, b, pl.BlockSpec((tm, n), lambda j, i, kk: (i, kk)), pl.BlockSpec((None, tn, n), lambda j, i, kk: (kk, j, 0)),
                   pl.BlockSpec((tm, tn), lambda j, i, kk: (i, j)), _sds((m, n_out), out_dtype),
                   (n_out // tn, m // tm, N_DEV), NT, (tm, tn), name, deps)


def _mm_tn(a, b, tm, tn, out_dtype, name, deps=()):
    s, m = a.shape
    n = b.shape[1]
    return _matmul(a, b, pl.BlockSpec((s, tm), lambda j, i, kk: (0, i)), pl.BlockSpec((s, tn), lambda j, i, kk: (0, j)),
                   pl.BlockSpec((tm, tn), lambda j, i, kk: (i, j)), _sds((m, n), out_dtype),
                   (n // tn, m // tm, 1), TN, (tm, tn), name, deps)


def _mm_tn_dm(a, b, tm, out_dtype, name):
    s, m = a.shape
    n = b.shape[1] // N_DEV
    return _matmul(a, b, pl.BlockSpec((s, tm), lambda j, i, kk: (0, i)), pl.BlockSpec((s, n), lambda j, i, kk: (0, j)),
                   pl.BlockSpec((None, tm, n), lambda j, i, kk: (j, i, 0)), _sds((N_DEV, m, n), out_dtype),
                   (N_DEV, m // tm, 1), TN, (tm, n), name)


def _row_spec():
    return pl.BlockSpec((TR, D), lambda i: (i, 0))


def _vec_spec(k=0):
    return pl.BlockSpec((1, D), lambda i: (0, k))


def _acc_rows(ref, first, val):
    @pl.when(first)
    def _():
        ref[...] = val

    @pl.when(jnp.logical_not(first))
    def _():
        ref[...] += val


def _pre_fwd(x, g, mod, k_scale, k_shift, name):
    s = x.shape[0]

    def body(x_ref, g_ref, sc_ref, sh_ref, h_ref):
        xv = x_ref[...]
        r = lax.rsqrt(jnp.mean(xv * xv, axis=-1, keepdims=True) + EPS)
        n = xv * r * g_ref[...]
        h_ref[...] = (n * (1.0 + sc_ref[...]) + sh_ref[...]).astype(h_ref.dtype)

    return pl.pallas_call(body, grid=(s // TR,), in_specs=[_row_spec(), _vec_spec(), _vec_spec(k_scale), _vec_spec(k_shift)],
                          out_specs=_row_spec(), out_shape=_sds((s, D), BF16), compiler_params=_params(1), name=name)(x, g, mod, mod)


def _post_fwd(x, y, g, mod, k_gate, name):
    s = x.shape[0]

    def body(x_ref, y_ref, g_ref, gt_ref, o_ref):
        yv = y_ref[...]
        r = lax.rsqrt(jnp.mean(yv * yv, axis=-1, keepdims=True) + EPS)
        o_ref[...] = x_ref[...] + gt_ref[...] * (yv * r * g_ref[...])

    return pl.pallas_call(body, grid=(s // TR,), in_specs=[_row_spec(), _row_spec(), _vec_spec(), _vec_spec(k_gate)],
                          out_specs=_row_spec(), out_shape=_sds((s, D), F32), compiler_params=_params(1), name=name)(x, y, g, mod)


def _post_fwd_loss(x, y, g, mod, k_gate, tgt, name):
    s = x.shape[0]

    def body(x_ref, y_ref, g_ref, gt_ref, t_ref, e_ref, loss_ref):
        i = pl.program_id(0)
        yv = y_ref[...]
        r = lax.rsqrt(jnp.mean(yv * yv, axis=-1, keepdims=True) + EPS)
        err = x_ref[...] + gt_ref[...] * (yv * r * g_ref[...]) - t_ref[...]
        e_ref[...] = err * (1.0 / D)
        part = 0.5 * jnp.sum(jnp.mean(err * err, axis=-1, keepdims=True), axis=0, keepdims=True)
        _acc_rows(loss_ref, i == 0, part)

    return pl.pallas_call(body, grid=(s // TR,),
                          in_specs=[_row_spec(), _row_spec(), _vec_spec(), _vec_spec(k_gate), _row_spec()],
                          out_specs=[_row_spec(), pl.BlockSpec((1, 1), lambda i: (0, 0))],
                          out_shape=[_sds((s, D), F32), _sds((1, 1), F32)], compiler_params=_params(1), name=name)(x, y, g, mod, tgt)


def _pre_bwd(dh, x, res, g, mod, k_scale, name):
    s = x.shape[0]

    def body(dh_ref, x_ref, res_ref, g_ref, sc_ref, dx_ref, dsh_ref, dsc_ref, dg_ref):
        first = pl.program_id(0) == 0
        xv, dh_v, gv = x_ref[...], dh_ref[...], g_ref[...]
        r = lax.rsqrt(jnp.mean(xv * xv, axis=-1, keepdims=True) + EPS)
        xh = xv * r
        dn = dh_v * (1.0 + sc_ref[...])
        dgn = dn * gv
        dx_ref[...] = res_ref[...] + r * (dgn - xh * jnp.mean(dgn * xh, axis=-1, keepdims=True))
        _acc_rows(dsh_ref, first, jnp.sum(dh_v, axis=0, keepdims=True))
        _acc_rows(dsc_ref, first, jnp.sum(dh_v * (xh * gv), axis=0, keepdims=True))
        _acc_rows(dg_ref, first, jnp.sum(dn * xh, axis=0, keepdims=True))

    return pl.pallas_call(body, grid=(s // TR,),
                          in_specs=[_row_spec(), _row_spec(), _row_spec(), _vec_spec(), _vec_spec(k_scale)],
                          out_specs=[_row_spec(), _vec_spec(), _vec_spec(), _vec_spec()],
                          out_shape=[_sds((s, D), F32)] + [_sds((1, D), F32)] * 3,
                          compiler_params=_params(1), name=name)(dh, x, res, g, mod)


def _post_bwd(dx, y, g, mod, k_gate, name):
    s = y.shape[0]

    def body(dx_ref, y_ref, g_ref, gt_ref, dy_ref, dgt_ref, dg_ref):
        first = pl.program_id(0) == 0
        yv, dxv, gv = y_ref[...], dx_ref[...], g_ref[...]
        r = lax.rsqrt(jnp.mean(yv * yv, axis=-1, keepdims=True) + EPS)
        yh = yv * r
        dn = dxv * gt_ref[...]
        dgn = dn * gv
        dy_ref[...] = (r * (dgn - yh * jnp.mean(dgn * yh, axis=-1, keepdims=True))).astype(dy_ref.dtype)
        _acc_rows(dgt_ref, first, jnp.sum(dxv * (yh * gv), axis=0, keepdims=True))
        _acc_rows(dg_ref, first, jnp.sum(dn * yh, axis=0, keepdims=True))

    return pl.pallas_call(body, grid=(s // TR,), in_specs=[_row_spec(), _row_spec(), _vec_spec(), _vec_spec(k_gate)],
                          out_specs=[_row_spec(), _vec_spec(), _vec_spec()],
                          out_shape=[_sds((s, D), BF16), _sds((1, D), F32), _sds((1, D), F32)],
                          compiler_params=_params(1), name=name)(dx, y, g, mod)


SW_TN = 1408
SW_TR = 512
TALL = 1024


def _swiglu_fwd(gu):
    s = gu.shape[0]
    nb = FFN // SW_TN

    def body(g_ref, u_ref, a_ref):
        gv = g_ref[...]
        a_ref[...] = (gv * _sig(gv) * u_ref[...]).astype(a_ref.dtype)

    return pl.pallas_call(body, grid=(s // SW_TR, nb),
                          in_specs=[pl.BlockSpec((SW_TR, SW_TN), lambda i, j: (i, j)), pl.BlockSpec((SW_TR, SW_TN), lambda i, j: (i, j + nb))],
                          out_specs=pl.BlockSpec((SW_TR, SW_TN), lambda i, j: (i, j)), out_shape=_sds((s, FFN), BF16),
                          compiler_params=_params(2, 48 << 20), name="swiglu_fwd")(gu, gu)


def _swiglu_bwd(dact, gu):
    s = gu.shape[0]
    nb = FFN // SW_TN

    def body(da_ref, g_ref, u_ref, o_ref):
        half = pl.program_id(2)
        gv, da = g_ref[...], da_ref[...]
        sg = _sig(gv)

        @pl.when(half == 0)
        def _():
            o_ref[...] = (da * u_ref[...] * (sg * (1.0 + gv * (1.0 - sg)))).astype(o_ref.dtype)

        @pl.when(half == 1)
        def _():
            o_ref[...] = (da * (gv * sg)).astype(o_ref.dtype)

    blk = lambda f: pl.BlockSpec((SW_TR, SW_TN), f)
    return pl.pallas_call(body, grid=(s // SW_TR, nb, 2),
                          in_specs=[blk(lambda i, j, h: (i, j)), blk(lambda i, j, h: (i, j)), blk(lambda i, j, h: (i, j + nb))],
                          out_specs=blk(lambda i, j, h: (i, j + nb * h)), out_shape=_sds((s, 2 * FFN), BF16),
                          compiler_params=_params(3, 48 << 20), name="swiglu_bwd")(dact, gu, gu)


MG_TN = 256


def _merge_fwd(y_a, y_h, proj):
    s = y_a.shape[0]
    tn = MG_TN
    ba, bh = GT_A // tn, GT_H // tn

    def body(ya_ref, yh_ref, ga_ref, gh_ref, m_ref):
        m_ref[...] = (_sig(ga_ref[...]) * ya_ref[...] + _sig(gh_ref[...]) * yh_ref[...]).astype(m_ref.dtype)

    tr = min(s, TALL)
    blk = lambda f: pl.BlockSpec((tr, tn), f)
    return pl.pallas_call(body, grid=(s // tr, D // tn),
                          in_specs=[blk(lambda i, j: (i, j)), blk(lambda i, j: (i, j)), blk(lambda i, j: (i, j + ba)), blk(lambda i, j: (i, j + bh))],
                          out_specs=blk(lambda i, j: (i, j)), out_shape=_sds((s, D), BF16),
                          compiler_params=_params(2), name="merge_fwd")(y_a, y_h, proj, proj)


def _merge_bwd(dm, y_a, y_h, proj):
    s = y_a.shape[0]
    tn = MG_TN
    ba, bh = GT_A // tn, GT_H // tn

    def body(dm_ref, ya_ref, yh_ref, ga_ref, gh_ref, dya_ref, dyh_ref, dga_ref, dgh_ref):
        dmv = dm_ref[...]
        sa, sh = _sig(ga_ref[...]), _sig(gh_ref[...])
        dya_ref[...] = (dmv * sa).astype(BF16)
        dyh_ref[...] = (dmv * sh).astype(BF16)
        dga_ref[...] = (dmv * ya_ref[...] * (sa * (1.0 - sa))).astype(BF16)
        dgh_ref[...] = (dmv * yh_ref[...] * (sh * (1.0 - sh))).astype(BF16)

    tr = min(s, TALL)
    blk = lambda f: pl.BlockSpec((tr, tn), f)
    nat = blk(lambda i, j: (i, j))
    return pl.pallas_call(body, grid=(s // tr, D // tn),
                          in_specs=[nat, nat, nat, blk(lambda i, j: (i, j + ba)), blk(lambda i, j: (i, j + bh))],
                          out_specs=[nat] * 4, out_shape=[_sds((s, D), BF16)] * 4,
                          compiler_params=_params(2), name="merge_bwd")(dm, y_a, y_h, proj, proj)


def _hgout_fwd(o_raw, proj, hg_norm):
    s = o_raw.shape[0]
    bg = G_H // LANE

    def body(o_ref, g_ref, n_ref, out_ref):
        ov = o_ref[...]
        r = lax.rsqrt(jnp.mean(ov * ov, axis=-1, keepdims=True) + EPS)
        out_ref[...] = (ov * r * n_ref[...] * _sig(g_ref[...])).astype(out_ref.dtype)

    tr = min(s, TALL)
    blk = lambda f: pl.BlockSpec((tr, LANE), f)
    return pl.pallas_call(body, grid=(s // tr, HG_HEADS),
                          in_specs=[blk(lambda i, h: (i, h)), blk(lambda i, h: (i, h + bg)), pl.BlockSpec((1, LANE), lambda i, h: (0, 0))],
                          out_specs=blk(lambda i, h: (i, h)), out_shape=_sds((s, HG_W), BF16),
                          compiler_params=_params(2), name="hgout_fwd")(o_raw, proj, hg_norm)


def _hgout_bwd(d_out, o_raw, proj, hg_norm):
    s = o_raw.shape[0]
    bg = G_H // LANE

    def body(d_ref, o_ref, g_ref, n_ref, do_ref, dg_ref, dn_ref):
        first = jnp.logical_and(pl.program_id(0) == 0, pl.program_id(1) == 0)
        ov, dv, nv = o_ref[...], d_ref[...], n_ref[...]
        sg = _sig(g_ref[...])
        r = lax.rsqrt(jnp.mean(ov * ov, axis=-1, keepdims=True) + EPS)
        oh = ov * r
        d_on = dv * sg
        dg_ref[...] = (dv * (oh * nv) * (sg * (1.0 - sg))).astype(dg_ref.dtype)
        t = d_on * nv
        do_ref[...] = r * (t - oh * jnp.mean(t * oh, axis=-1, keepdims=True))
        _acc_rows(dn_ref, first, jnp.sum(d_on * oh, axis=0, keepdims=True))

    tr = min(s, TALL)
    blk = lambda f: pl.BlockSpec((tr, LANE), f)
    vec = pl.BlockSpec((1, LANE), lambda i, h: (0, 0))
    return pl.pallas_call(body, grid=(s // tr, HG_HEADS),
                          in_specs=[blk(lambda i, h: (i, h)), blk(lambda i, h: (i, h)), blk(lambda i, h: (i, h + bg)), vec],
                          out_specs=[blk(lambda i, h: (i, h)), blk(lambda i, h: (i, h)), vec],
                          out_shape=[_sds((s, HG_W), F32), _sds((s, HG_W), BF16), _sds((1, LANE), F32)],
                          compiler_params=_params(2), name="hgout_bwd")(d_out, o_raw, proj, hg_norm)


def _rope(t, cos, s_lo, s_hi):
    return t * cos + pltpu.roll(t, LANE - ROT // 2, 1) * s_lo + pltpu.roll(t, ROT // 2, 1) * s_hi


def _rope_wide(t, cos, s_lo, s_hi):
    return jnp.concatenate([_rope(t[:, k * LANE:(k + 1) * LANE], cos, s_lo, s_hi) for k in range(t.shape[1] // LANE)], axis=1)


def _attn_mask(has_prev):
    qi = lax.broadcasted_iota(jnp.int32, (BLK, 2 * BLK), 0)
    kj = lax.broadcasted_iota(jnp.int32, (BLK, 2 * BLK), 1)
    rel = BLK + qi - kj
    band = jnp.logical_and(rel >= 0, rel < BLK)
    return jnp.logical_and(band, jnp.logical_or(has_prev, kj >= BLK))


def _attn_specs():
    prev = lambda i: jnp.maximum(i - 1, 0)
    kb, vb = K_A // LANE, V_A // LANE
    blk = lambda f: pl.BlockSpec((BLK, LANE), f)
    tabs = [blk(lambda i: (i, 0))] * 3 + [blk(lambda i: (prev(i), 0))] * 3
    return [pl.BlockSpec((BLK, ATT_W), lambda i: (i, 0)), blk(lambda i: (i, kb)), blk(lambda i: (prev(i), kb)),
            blk(lambda i: (i, vb)), blk(lambda i: (prev(i), vb))] + tabs + [pl.BlockSpec((1, LANE), lambda i: (0, 0))]


def _attn_logits(qh, kg):
    return _dot(qh, kg, NT)


def _attn_probs(raw, mask, sk):
    logits = jnp.where(mask, raw * (HEAD_DIM ** -0.5), -jnp.inf)
    m = jnp.maximum(jnp.max(logits, axis=-1, keepdims=True), sk)
    p = jnp.exp(logits - m)
    e_sink = jnp.exp(sk - m)
    inv = 1.0 / (jnp.sum(p, axis=-1, keepdims=True) + e_sink)
    return p * inv, e_sink * inv


def _attn_fwd(proj, tabs, sinks):
    s = proj.shape[0]

    def body(q_ref, kc_ref, kp_ref, vc_ref, vp_ref, c0, l0, h0, c1, l1, h1, sk_ref, o_ref):
        i = pl.program_id(0)
        mask = _attn_mask(i > 0)
        q = _rope_wide(q_ref[...], c0[...], l0[...], h0[...]).astype(BF16)
        kk = jnp.concatenate([_rope(kp_ref[...], c1[...], l1[...], h1[...]), _rope(kc_ref[...], c0[...], l0[...], h0[...])], axis=0).astype(BF16)
        vv = jnp.concatenate([vp_ref[...], vc_ref[...]], axis=0).astype(BF16)
        part = lambda t, h: t[:, h * HEAD_DIM:(h + 1) * HEAD_DIM]

        def head(h):
            raw = _attn_logits(part(q, h), part(kk, h // GROUP))
            yield
            sk = sk_ref[:, h:h + 1]
            logits = jnp.where(mask, raw * (HEAD_DIM ** -0.5), -jnp.inf)
            m = jnp.maximum(jnp.max(logits, axis=-1, keepdims=True), sk)
            yield
            p = jnp.exp(logits - m)
            den = jnp.sum(p, axis=-1, keepdims=True) + jnp.exp(sk - m)
            yield
            out = _dot((p * (1.0 / den)).astype(BF16), part(vv, h // GROUP), NN)
            yield
            return out

        o_ref[...] = jnp.concatenate(_interleave([head(h) for h in range(ATT_HEADS)]), axis=1).astype(o_ref.dtype)

    return pl.pallas_call(body, grid=(s // BLK,), in_specs=_attn_specs(),
                          out_specs=pl.BlockSpec((BLK, ATT_W), lambda i: (i, 0)), out_shape=_sds((s, ATT_W), BF16),
                          compiler_params=_params(1), name="attn_fwd")(proj, proj, proj, proj, proj, *tabs, *tabs, sinks)


def _attn_bwd(proj, tabs, sinks, d_att):
    s = proj.shape[0]

    def body(q_ref, kc_ref, kp_ref, vc_ref, vp_ref, c0, l0, h0, c1, l1, h1, sk_ref, do_ref, dq_ref, dk_ref, dv_ref, ds_ref):
        i = pl.program_id(0)

        @pl.when(i == 0)
        def _():
            dk_ref[...] = jnp.zeros_like(dk_ref)
            dv_ref[...] = jnp.zeros_like(dv_ref)
            ds_ref[...] = jnp.zeros_like(ds_ref)

        mask = _attn_mask(i > 0)
        q = _rope_wide(q_ref[...], c0[...], l0[...], h0[...]).astype(BF16)
        kk = jnp.concatenate([_rope(kp_ref[...], c1[...], l1[...], h1[...]), _rope(kc_ref[...], c0[...], l0[...], h0[...])], axis=0).astype(BF16)
        vv = jnp.concatenate([vp_ref[...], vc_ref[...]], axis=0).astype(BF16)
        d_o = do_ref[...].astype(BF16)
        lane = lax.broadcasted_iota(jnp.int32, (1, LANE), 1)
        part = lambda t, h: t[:, h * HEAD_DIM:(h + 1) * HEAD_DIM]

        def head(h):
            kg, vg = part(kk, h // GROUP), part(vv, h // GROUP)
            qh, doh = part(q, h), part(d_o, h)
            raw = _attn_logits(qh, kg)
            d_p = _dot(doh, vg, NT)
            yield
            prob, p_sink = _attn_probs(raw, mask, sk_ref[:, h:h + 1])
            yield
            dd = jnp.sum(prob * d_p, axis=-1, keepdims=True)
            yield
            d_s = (prob * (d_p - dd)).astype(BF16)
            d_sink = jnp.where(lane == h, -jnp.sum(p_sink * dd, axis=0, keepdims=True), 0.0)
            dq = _dot(d_s, kg, NN)
            dk = _dot(d_s, qh, TN)
            dv = _dot(prob.astype(BF16), doh, TN)
            yield
            return dq * (HEAD_DIM ** -0.5), dk * (HEAD_DIM ** -0.5), dv, d_sink

        per_head = _interleave([head(h) for h in range(ATT_HEADS)])
        dqs = [t[0] for t in per_head]
        group_sum = lambda k, g: functools.reduce(jnp.add, [t[k] for t in per_head[g * GROUP:(g + 1) * GROUP]])
        dks = [group_sum(1, g) for g in range(KV_HEADS)]
        dvs = [group_sum(2, g) for g in range(KV_HEADS)]
        d_sink = functools.reduce(jnp.add, [t[3] for t in per_head])
        dq_ref[...] = _rope_wide(jnp.concatenate(dqs, axis=1), c0[...], -l0[...], -h0[...]).astype(dq_ref.dtype)
        d_k = jnp.concatenate(dks, axis=1)
        d_v = jnp.concatenate(dvs, axis=1)
        cur = pl.ds(pl.multiple_of(i * BLK, BLK), BLK)
        prv = pl.ds(pl.multiple_of(jnp.maximum(i - 1, 0) * BLK, BLK), BLK)
        dk_ref[prv, :] += _rope(d_k[:BLK], c1[...], -l1[...], -h1[...])
        dk_ref[cur, :] += _rope(d_k[BLK:], c0[...], -l0[...], -h0[...])
        dv_ref[prv, :] += d_v[:BLK]
        dv_ref[cur, :] += d_v[BLK:]
        ds_ref[...] += d_sink

    full = pl.BlockSpec((s, LANE), lambda i: (0, 0))
    return pl.pallas_call(body, grid=(s // BLK,), in_specs=_attn_specs() + [pl.BlockSpec((BLK, ATT_W), lambda i: (i, 0))],
                          out_specs=[pl.BlockSpec((BLK, ATT_W), lambda i: (i, 0)), full, full, pl.BlockSpec((1, LANE), lambda i: (0, 0))],
                          out_shape=[_sds((s, ATT_W), BF16), _sds((s, LANE), F32), _sds((s, LANE), F32), _sds((1, LANE), F32)],
                          compiler_params=_params(1), name="attn_bwd")(proj, proj, proj, proj, proj, *tabs, *tabs, sinks, d_att)


def _tri_matmul(tri, t):
    hi = t.astype(BF16)
    r1 = t - hi.astype(F32)
    mid = r1.astype(BF16)
    lo = (r1 - mid.astype(F32)).astype(BF16)
    return _dot(tri, hi, NN) + _dot(tri, mid, NN) + _dot(tri, lo, NN)


def _lower_bound(hl):
    a, b = hl[0:1, :], hl[1:2, :]
    mx = jnp.maximum(a, b)
    ea, eb = jnp.exp(a - mx), jnp.exp(b - mx)
    return ea / (ea + eb)


def _hg_gates(q_raw, f_raw, lb, tri_lower):
    sg = _sig(f_raw)
    f = lb + (1.0 - lb) * sg
    sq = _sig(q_raw)
    b = _tri_matmul(tri_lower, jnp.log(f))
    return sg, f, 1.0 - f, sq, q_raw * sq, b


HG_PAIR_FWD = 8
HG_PAIR_BWD = 8


def _hg_specs(n_map, pair):
    blk = lambda off, p: pl.BlockSpec((HG_TB, LANE), lambda h, n: (n_map(n), off // LANE + pair * h + p))
    return [blk(off, p) for off in (Q_H, F_H, I_H) for p in range(pair)] + [pl.BlockSpec((2, pair * LANE), lambda h, n: (0, h))]


def _interleave(gens):
    out = [None] * len(gens)
    live = list(range(len(gens)))
    while live:
        for k in list(live):
            try:
                next(gens[k])
            except StopIteration as stop:
                out[k] = stop.value
                live.remove(k)
    return out


def _hg_spread():
    c = lax.broadcasted_iota(jnp.int32, (CHUNK, SUB * SUB), 0)
    l = lax.broadcasted_iota(jnp.int32, (CHUNK, SUB * SUB), 1)
    r = lax.broadcasted_iota(jnp.int32, (SUB, SUB * SUB), 0)
    lr = lax.broadcasted_iota(jnp.int32, (SUB, SUB * SUB), 1)
    cols = [(c == lo + (l >> 4)).astype(BF16) for lo in range(0, CHUNK, SUB)]
    tile = [(c == lo + (l & (SUB - 1))).astype(BF16) for lo in range(0, CHUNK, SUB)]
    return cols, tile, (lr & (SUB - 1)) == r, (lr >> 4) == r


def _hg_intra(qs, kk, b, grad=None):
    lane = lax.broadcasted_iota(jnp.int32, (SUB, CHUNK), 1)
    row1 = lax.broadcasted_iota(jnp.int32, (SUB, 1), 0)
    kk_b = kk.astype(BF16)
    if grad is not None:
        d_a, d_at, (cols, tile, diag, block) = grad
    a_blocks, dq_blocks, dk_blocks, db_blocks = [], [], [], []
    dk_left = None
    for j in range(CHUNK // SUB):
        lo = j * SUB
        q_j, k_j, b_j = qs[lo:lo + SUB], kk[lo:lo + SUB], b[lo:lo + SUB]
        es = [jnp.where(row1 >= sx, jnp.exp(jnp.minimum(b_j - b_j[sx:sx + 1], 0.0)), 0.0) for sx in range(SUB)]
        pes = [q_j * e for e in es]
        pe = jnp.concatenate(pes, axis=0).astype(BF16)
        pairs = _dot(pe, kk_b, NT)
        yield
        a_j = jnp.zeros((SUB, CHUNK), F32)
        for sx in range(SUB):
            a_j = jnp.where(lane == lo + sx, pairs[sx * SUB:(sx + 1) * SUB], a_j)
        if grad is not None:
            da_j = d_a[lo:lo + SUB]
            ek = jnp.concatenate([e * k_j[sx:sx + 1] for sx, e in enumerate(es)], axis=0).astype(BF16)
            sel_t = jnp.where(diag, _dot(da_j.astype(BF16), cols[j], NN), 0.0).astype(BF16)
            sel_s = jnp.where(block, _dot(d_at[lo:lo + SUB].astype(BF16), tile[j], NN), 0.0).astype(BF16)
            pek = jnp.concatenate([p * k_j[sx:sx + 1] for sx, p in enumerate(pes)], axis=0).astype(BF16)
            yield
            dq_j = _dot(sel_t, ek, NN)
            dk_j = _dot(sel_s, pe, NN)
            db_j = _dot(sel_t, pek, NN) - _dot(sel_s, pek, NN)
            yield
        if j > 0:
            ref = b[lo - 1:lo]
            sc_q = jnp.exp(b_j - ref)
            sc_k = jnp.exp(jnp.minimum(ref - b, 0.0))
            qt = (q_j * sc_q).astype(BF16)
            kt = (kk * sc_k).astype(BF16)
            left = _dot(qt, kt, NT)
            yield
            a_j = a_j + jnp.where(lane < lo, left, 0.0)
            if grad is not None:
                da_left = jnp.where(lane < lo, da_j, 0.0).astype(BF16)
                dq_left = _dot(da_left, kt, NN) * sc_q
                dq_j = dq_j + dq_left
                db_j = db_j + q_j * dq_left
                t = _dot(da_left, qt, TN)
                yield
                t = t * sc_k
                dk_left = t if dk_left is None else dk_left + t
        a_blocks.append(a_j)
        if grad is not None:
            dq_blocks.append(dq_j)
            dk_blocks.append(dk_j)
            db_blocks.append(db_j)
    a = jnp.concatenate(a_blocks, axis=0)
    if grad is None:
        return a
    return a, jnp.concatenate(dq_blocks, axis=0), jnp.concatenate(dk_blocks, axis=0) + dk_left, jnp.concatenate(db_blocks, axis=0) - kk * dk_left


def _hgrn_fwd(proj, hl):
    s = proj.shape[0]
    n_chunk = HG_TB // CHUNK
    pair = HG_PAIR_FWD

    def body(*refs):
        q_refs, f_refs, i_refs = refs[:pair], refs[pair:2 * pair], refs[2 * pair:3 * pair]
        hl_ref, o_ref, st_out_ref, st_ref = refs[3 * pair:]

        @pl.when(pl.program_id(1) == 0)
        def _():
            st_ref[...] = jnp.zeros_like(st_ref)

        r_i = lax.broadcasted_iota(jnp.int32, (CHUNK, CHUNK), 0)
        c_i = lax.broadcasted_iota(jnp.int32, (CHUNK, CHUNK), 1)
        tri_lower = (r_i >= c_i).astype(BF16)

        def chunk(c, carry):
            rows = pl.ds(pl.multiple_of(c * CHUNK, CHUNK), CHUNK)
            def head(p):
                cols = slice(p * LANE, (p + 1) * LANE)
                lb = _lower_bound(hl_ref[:, cols])
                v = i_refs[p][rows, :].astype(BF16)
                _, _, kk, _, qs, b = _hg_gates(q_refs[p][rows, :], f_refs[p][rows, :], lb, tri_lower)
                yield
                st = st_ref[p]
                st_b = st.astype(BF16)
                st_out_ref[p, c] = st_b
                o_state = _dot((qs * jnp.exp(b)).astype(BF16), st_b, NT)
                b_last = b[CHUNK - 1:CHUNK, :]
                st_new = _dot(v, (kk * jnp.exp(b_last - b)).astype(BF16), TN)
                a = yield from _hg_intra(qs, kk, b)
                st_ref[p] = st * jnp.exp(b_last) + st_new
                o_ref[rows, cols] = o_state + _dot(a.astype(BF16), v, NN)

            _interleave([head(p) for p in range(pair)])
            return carry

        lax.fori_loop(0, n_chunk, chunk, 0)

    return pl.pallas_call(
        body, grid=(HG_HEADS // pair, s // HG_TB), in_specs=_hg_specs(lambda n: n, pair),
        out_specs=[pl.BlockSpec((HG_TB, pair * LANE), lambda h, n: (n, h)), pl.BlockSpec((pair, n_chunk, HG_K, HG_K), lambda h, n: (h, n, 0, 0))],
        out_shape=[_sds((s, HG_W), F32), _sds((HG_HEADS, s // CHUNK, HG_K, HG_K), BF16)],
        scratch_shapes=[pltpu.VMEM((pair, HG_K, HG_K), F32)],
        compiler_params=_params(2), name="hgrn_fwd")(*[proj] * (3 * pair), hl)


def _hgrn_bwd(proj, hl, states, d_o):
    s = proj.shape[0]
    n_chunk = HG_TB // CHUNK
    n_blk = s // HG_TB
    pair = HG_PAIR_BWD
    rev = lambda n: n_blk - 1 - n

    def body(*refs):
        q_refs, f_refs, i_refs = refs[:pair], refs[pair:2 * pair], refs[2 * pair:3 * pair]
        hl_ref, st_in_ref, do_ref, dq_ref, df_ref, di_ref, dhl_ref, dst_ref, dlb_ref = refs[3 * pair:]
        n = pl.program_id(1)

        @pl.when(n == 0)
        def _():
            dst_ref[...] = jnp.zeros_like(dst_ref)
            dlb_ref[...] = jnp.zeros_like(dlb_ref)

        r_i = lax.broadcasted_iota(jnp.int32, (CHUNK, CHUNK), 0)
        c_i = lax.broadcasted_iota(jnp.int32, (CHUNK, CHUNK), 1)
        tri_lower = (r_i >= c_i).astype(BF16)
        tri_upper = (r_i <= c_i).astype(BF16)
        row = lax.broadcasted_iota(jnp.int32, (CHUNK, 1), 0)
        spread = _hg_spread()

        def chunk(cc, carry):
            c = n_chunk - 1 - cc
            rows = pl.ds(pl.multiple_of(c * CHUNK, CHUNK), CHUNK)
            def head(p):
                cols = slice(p * LANE, (p + 1) * LANE)
                lb = _lower_bound(hl_ref[:, cols])
                q_raw = q_refs[p][rows, :]
                vb = i_refs[p][rows, :].astype(BF16)
                sg, f, kk, sq, qs, b = _hg_gates(q_raw, f_refs[p][rows, :], lb, tri_lower)
                yield
                e_b = jnp.exp(b)
                qe = qs * e_b
                b_last = b[CHUNK - 1:CHUNK, :]
                e_last = jnp.exp(b_last)
                e_kd = jnp.exp(b_last - b)
                kd = kk * e_kd
                st0 = st_in_ref[p, c]
                d_ob = do_ref[rows, cols].astype(BF16)
                dst = dst_ref[p]
                dst_b = dst.astype(BF16)
                d_a = jnp.where(r_i >= c_i, _dot(d_ob, vb, NT), 0.0)
                d_at = jnp.where(r_i <= c_i, _dot(vb, d_ob, NT), 0.0)
                d_v_st = _dot(kd.astype(BF16), dst_b, NT)
                d_kd = _dot(vb, dst_b, NN)
                d_qe = _dot(d_ob, st0, NN)
                dst_new = _dot(d_ob, qe.astype(BF16), TN)
                yield
                a, dqs, dkk, d_b = yield from _hg_intra(qs, kk, b, (d_a, d_at, spread))
                d_v = _dot(a.astype(BF16), d_ob, TN) + d_v_st
                dqs_st = d_qe * e_b
                dkk_st = d_kd * e_kd
                dqs = dqs + dqs_st
                dkk = dkk + dkk_st
                d_b_last = jnp.sum(d_kd * kd, axis=0, keepdims=True) + jnp.sum(dst * st0.astype(F32), axis=0, keepdims=True) * e_last
                d_b = d_b + qs * dqs_st - kk * dkk_st + jnp.where(row == CHUNK - 1, d_b_last, 0.0)
                d_g = _tri_matmul(tri_upper, d_b)
                dst_ref[p] = dst_new + dst * e_last
                yield
                d_f = d_g / f - dkk
                dlb_ref[:, cols] += jnp.sum(d_f * (1.0 - sg), axis=0, keepdims=True)
                dq_ref[rows, cols] = (dqs * (sq * (1.0 + q_raw * (1.0 - sq)))).astype(dq_ref.dtype)
                df_ref[rows, cols] = (d_f * (1.0 - lb) * (sg * (1.0 - sg))).astype(df_ref.dtype)
                di_ref[rows, cols] = d_v.astype(di_ref.dtype)

            _interleave([head(p) for p in range(pair)])
            return carry

        lax.fori_loop(0, n_chunk, chunk, 0)

        @pl.when(n == n_blk - 1)
        def _():
            lb = _lower_bound(hl_ref[...])
            d_hl0 = dlb_ref[...] * (lb * (1.0 - lb))
            dhl_ref[...] = jnp.concatenate([d_hl0, -d_hl0], axis=0)

    out_blk = pl.BlockSpec((HG_TB, pair * LANE), lambda h, n: (rev(n), h))
    return pl.pallas_call(
        body, grid=(HG_HEADS // pair, n_blk),
        in_specs=_hg_specs(rev, pair) + [pl.BlockSpec((pair, n_chunk, HG_K, HG_K), lambda h, n: (h, rev(n), 0, 0)), out_blk],
        out_specs=[out_blk, out_blk, out_blk, pl.BlockSpec((2, pair * LANE), lambda h, n: (0, h))],
        out_shape=[_sds((s, HG_W), BF16)] * 3 + [_sds((2, HG_W), F32)],
        scratch_shapes=[pltpu.VMEM((pair, HG_K, HG_K), F32), pltpu.VMEM((1, pair * LANE), F32)],
        compiler_params=_params(2), name="hgrn_bwd")(*[proj] * (3 * pair), hl, states, d_o)


def _mod_part(c_all, w_shard, b_shard):
    n = w_shard.shape[1]
    tn = 512

    def body(c_ref, w_ref, b_ref, o_ref):
        o_ref[...] = _dot(c_ref[...].astype(BF16), w_ref[...].astype(BF16), NN) + b_ref[...]

    return pl.pallas_call(body, grid=(n // tn,),
                          in_specs=[pl.BlockSpec((N_DEV, D), lambda j: (0, 0)), pl.BlockSpec((D, tn), lambda j: (0, j)), pl.BlockSpec((1, tn), lambda j: (0, j))],
                          out_specs=pl.BlockSpec((N_DEV, tn), lambda j: (0, j)), out_shape=_sds((N_DEV, n), F32),
                          compiler_params=_params(1, 32 << 20), name="mod_part")(c_all, w_shard, b_shard)


def _grad_w_ada(c_all_t, dmod_cols):
    n = dmod_cols.shape[1]
    tn = 512

    def body(c_ref, d_ref, o_ref):
        cv = c_ref[...].astype(BF16).astype(F32)
        dv = d_ref[...].astype(BF16).astype(F32)
        acc = cv[:, 0:1] * dv[0:1, :]
        for k in range(1, N_DEV):
            acc = acc + cv[:, k:k + 1] * dv[k:k + 1, :]
        o_ref[...] = acc

    return pl.pallas_call(body, grid=(n // tn,),
                          in_specs=[pl.BlockSpec((D, N_DEV), lambda j: (0, 0)), pl.BlockSpec((N_DEV, tn), lambda j: (0, j))],
                          out_specs=pl.BlockSpec((D, tn), lambda j: (0, j)), out_shape=_sds((D, n), F32),
                          compiler_params=_params(1, 32 << 20), name="grad_w_ada")(c_all_t, dmod_cols)


def _row_tile(r, c, max_elems=1 << 18):
    if r * c <= max_elems or r % 8:
        return r
    best = 8
    for t in range(8, r + 1, 8):
        if r % t == 0 and t * c <= max_elems:
            best = t
    return best


WIDE_TILE = 5 << 17


def _sum_pieces(pieces, own, name):
    p, r, c = pieces.shape
    tr = _row_tile(r, c, WIDE_TILE)

    def body(o_ref, p_ref, g_ref):
        g = o_ref[...].astype(F32)
        for k in range(p):
            g = g + p_ref[k].astype(F32)
        g_ref[...] = g

    blk = pl.BlockSpec((tr, c), lambda i: (i, 0))
    return pl.pallas_call(body, grid=(r // tr,), in_specs=[blk, pl.BlockSpec((p, tr, c), lambda i: (0, i, 0))], out_specs=blk,
                          out_shape=_sds((r, c), F32), compiler_params=_params(1), name=name)(own, pieces)


def _adamw(pieces, w, m, v, name, emit_grad=True, own=None):
    p, r, c = pieces.shape
    tr = _row_tile(r, c)
    c1 = 1.0 / (1.0 - ADAM_B1 ** ADAM_STEP)
    c2 = 1.0 / (1.0 - ADAM_B2 ** ADAM_STEP)

    def body(*refs):
        if own is None:
            p_ref, w_ref, m_ref, v_ref, *outs = refs
            g = p_ref[0].astype(F32)
        else:
            o_ref, p_ref, w_ref, m_ref, v_ref, *outs = refs
            g = o_ref[...].astype(F32) + p_ref[0].astype(F32)
        for k in range(1, p):
            g = g + p_ref[k].astype(F32)
        m2 = ADAM_B1 * m_ref[...] + (1.0 - ADAM_B1) * g
        v2 = ADAM_B2 * v_ref[...] + (1.0 - ADAM_B2) * (g * g)
        delta = -ADAM_LR * ((m2 * c1) / (jnp.sqrt(v2 * c2) + ADAM_EPS) + ADAM_WD * w_ref[...])
        if emit_grad:
            outs[0][...] = g
        outs[-3][...] = delta
        outs[-2][...] = m2
        outs[-1][...] = v2

    blk = pl.BlockSpec((tr, c), lambda i: (i, 0))
    n_out = 4 if emit_grad else 3
    lead = [] if own is None else [own]
    return pl.pallas_call(body, grid=(r // tr,), in_specs=[blk] * len(lead) + [pl.BlockSpec((p, tr, c), lambda i: (0, i, 0)), blk, blk, blk],
                          out_specs=[blk] * n_out, out_shape=[_sds((r, c), F32)] * n_out,
                          compiler_params=_params(1, 48 << 20), name=name)(*lead, pieces, w, m, v)


def _my_coords():
    return lax.axis_index("x"), lax.axis_index("y"), lax.axis_index("c")


def _flip(coords, k):
    x, y, c = coords
    return (1 - x if k & 4 else x, 1 - y if k & 2 else y, 1 - c if k & 1 else c)


def _lin(coords):
    return 4 * coords[0] + 2 * coords[1] + coords[2]


def _exchange_small(x3, bcast, name):
    n = x3.shape[2]

    def body(x_ref, o_ref, send_sems, recv_sems):
        me = _my_coords()
        my_id = _lin(me)
        o_ref[pl.ds(my_id, 1)] = x_ref[pl.ds(0 if bcast else my_id, 1)]
        copies = []
        for k in range(1, N_DEV):
            peer = _flip(me, k)
            src = x_ref.at[0 if bcast else _lin(peer)]
            cp = pltpu.make_async_remote_copy(src_ref=src, dst_ref=o_ref.at[my_id], send_sem=send_sems.at[k], recv_sem=recv_sems.at[k],
                                              device_id=peer, device_id_type=MESH)
            cp.start()
            copies.append(cp)
        for k in range(1, N_DEV):
            peer = _flip(me, k)
            pltpu.make_async_remote_copy(src_ref=x_ref.at[0], dst_ref=o_ref.at[_lin(peer)], send_sem=send_sems.at[k], recv_sem=recv_sems.at[k],
                                         device_id=peer, device_id_type=MESH).wait_recv()
        for cp in copies:
            cp.wait_send()

    vm = pl.BlockSpec(memory_space=pltpu.VMEM)
    return pl.pallas_call(body, in_specs=[vm], out_specs=vm, out_shape=_sds((N_DEV, 1, n), F32),
                          scratch_shapes=[pltpu.SemaphoreType.DMA((N_DEV,)), pltpu.SemaphoreType.DMA((N_DEV,))], name=name)(x3)


HBM_SPEC = pl.BlockSpec(memory_space=pltpu.HBM)
SEM_SPEC = pl.BlockSpec(memory_space=pltpu.SEMAPHORE)
ANY_SPEC = pl.BlockSpec(memory_space=pl.ANY)
DATAFLOW = pltpu.SideEffectType.DATAFLOW_SIDE_EFFECTING
GATHER_FLIPS = (1, 2, 4, 6)
PASS_FLIPS = (2, 4, 6)
TOKEN = (8, LANE)


def _hbm(t):
    return pltpu.with_memory_space_constraint(t, pltpu.HBM)


def _hbm_like(ts):
    return [pltpu.HBM(t.shape, t.dtype) for t in ts]


def _split_start(issue, srcs, lands, n_sem, name, deps=()):
    n, nd = len(srcs), len(deps)

    def body(*refs):
        issue(refs[:n], refs[n:2 * n], refs[2 * n + nd], refs[2 * n + nd + 1])
        refs[-1][...] = jnp.zeros(TOKEN, F32)

    outs = pl.pallas_call(
        body, name=name,
        out_shape=(pltpu.SemaphoreType.DMA((n_sem,)), pltpu.SemaphoreType.DMA((n_sem,)), *_hbm_like(srcs), *_hbm_like(lands), _sds(TOKEN, F32)),
        in_specs=[HBM_SPEC] * (2 * n) + [ANY_SPEC] * nd,
        out_specs=(SEM_SPEC, SEM_SPEC, *[HBM_SPEC] * (2 * n), pl.BlockSpec(memory_space=pltpu.VMEM)),
        input_output_aliases={i: 2 + i for i in range(2 * n)},
        compiler_params=pltpu.CompilerParams(has_side_effects=DATAFLOW))(*[_hbm(t) for t in srcs], *[_hbm(t) for t in lands], *deps)
    return dict(sems=outs[:2], thru=list(outs[2:2 + 2 * n]), token=outs[-1], n=n)


def _split_wait(finish, handle, after, name):
    n = handle["n"]
    thru = handle["thru"]

    def body(*refs):
        finish(refs[:n], refs[n:2 * n], refs[2 * n], refs[2 * n + 1])

    outs = pl.pallas_call(
        body, name=name, out_shape=_hbm_like(thru), in_specs=[HBM_SPEC] * (2 * n) + [SEM_SPEC, SEM_SPEC] + [ANY_SPEC] * len(after),
        out_specs=[HBM_SPEC] * (2 * n), input_output_aliases={i: i for i in range(2 * n)},
        compiler_params=pltpu.CompilerParams(has_side_effects=DATAFLOW))(*thru, *handle["sems"], *after)
    return list(outs[:n]), list(outs[n:])


def _gather_start(shards, name, deps=()):
    n = len(shards)
    my_id = _lin(_my_coords())
    lands = [lax.dynamic_update_slice(lax.empty((N_DEV,) + t.shape, t.dtype), t[None], (my_id, 0, 0)) for t in shards]

    def issue(src, land, send_sems, recv_sems):
        me = _my_coords()
        for w in range(n):
            for j, k in enumerate(GATHER_FLIPS):
                q = len(GATHER_FLIPS) * w + j
                pltpu.make_async_remote_copy(src_ref=src[w], dst_ref=land[w].at[_lin(me)], send_sem=send_sems.at[q], recv_sem=recv_sems.at[q],
                                             device_id=_flip(me, k), device_id_type=MESH).start()

    return _split_start(issue, shards, lands, len(GATHER_FLIPS) * n, name, deps)


def _gather_wait(handle, after, name):
    n = handle["n"]

    def finish(src, land, send_sems, recv_sems):
        me = _my_coords()
        for w in range(n):
            for j, k in enumerate(GATHER_FLIPS):
                q = len(GATHER_FLIPS) * w + j
                peer = _flip(me, k)
                cp = pltpu.make_async_remote_copy(src_ref=src[w], dst_ref=land[w].at[_lin(peer)], send_sem=send_sems.at[q], recv_sem=recv_sems.at[q],
                                                  device_id=peer, device_id_type=MESH)
                cp.wait_send()
                cp.wait_recv()

    return _split_wait(finish, handle, after, name)[1]


def _gather_pass(lands, name):
    n = len(lands)
    n_p = len(PASS_FLIPS)

    def body(*refs):
        land = refs[n:2 * n]
        send_sems, recv_sems = refs[2 * n:]
        me = _my_coords()
        sibling = _flip(me, 1)
        sent = []
        for w in range(n):
            for j, k in enumerate(PASS_FLIPS):
                blk = land[w].at[_lin(_flip(me, k))]
                cp = pltpu.make_async_remote_copy(src_ref=blk, dst_ref=blk, send_sem=send_sems.at[n_p * w + j], recv_sem=recv_sems.at[n_p * w + j],
                                                  device_id=sibling, device_id_type=MESH)
                cp.start()
                sent.append(cp)
        for w in range(n):
            for j, k in enumerate(PASS_FLIPS):
                blk = land[w].at[_lin(_flip(me, k + 1))]
                pltpu.make_async_remote_copy(src_ref=blk, dst_ref=blk, send_sem=send_sems.at[n_p * w + j], recv_sem=recv_sems.at[n_p * w + j],
                                             device_id=sibling, device_id_type=MESH).wait_recv()
        for cp in sent:
            cp.wait_send()

    return pl.pallas_call(body, in_specs=[ANY_SPEC] * n, out_specs=[ANY_SPEC] * n, out_shape=[_sds(t.shape, t.dtype) for t in lands],
                          input_output_aliases={i: i for i in range(n)},
                          scratch_shapes=[pltpu.SemaphoreType.DMA((n_p * n,)), pltpu.SemaphoreType.DMA((n_p * n,))], name=name)(*lands)


CHIP_FLIPS = (0, 2, 4, 6)


def _pair_copy(src, land, send_sems, recv_sems, w, j):
    me = _my_coords()
    q = len(CHIP_FLIPS) * w + j
    return pltpu.make_async_remote_copy(src_ref=src[w].at[_lin(_flip(me, CHIP_FLIPS[j] + 1))], dst_ref=land[w].at[j], send_sem=send_sems.at[q],
                                        recv_sem=recv_sems.at[q], device_id=_flip(me, 1), device_id_type=MESH)


def _pair_start(grads, name, deps=()):
    n = len(grads)
    lands = [lax.empty((len(CHIP_FLIPS),) + g.shape[1:], g.dtype) for g in grads]

    def issue(src, land, send_sems, recv_sems):
        for w in range(n):
            for j in range(len(CHIP_FLIPS)):
                _pair_copy(src, land, send_sems, recv_sems, w, j).start()

    return _split_start(issue, grads, lands, len(CHIP_FLIPS) * n, name, deps)


def _pair_wait(handle, after, name):
    n = handle["n"]

    def finish(src, land, send_sems, recv_sems):
        for w in range(n):
            for j in range(len(CHIP_FLIPS)):
                cp = _pair_copy(src, land, send_sems, recv_sems, w, j)
                cp.wait_send()
                cp.wait_recv()

    return _split_wait(finish, handle, after, name)


def _pair_add(grad, theirs, name):
    p, r, c = theirs.shape
    tr = _row_tile(r, c, WIDE_TILE)
    me = _my_coords()
    ids = jnp.stack([_lin(_flip(me, k)) for k in CHIP_FLIPS]).astype(jnp.int32)

    def body(ids_ref, a_ref, b_ref, o_ref):
        o_ref[...] = (a_ref[...].astype(F32) + b_ref[...].astype(F32)).astype(o_ref.dtype)

    blk = pl.BlockSpec((None, tr, c), lambda j, i, ids_ref: (j, i, 0))
    return pl.pallas_call(
        body, out_shape=_sds((p, r, c), theirs.dtype), compiler_params=_params(2), name=name,
        grid_spec=pltpu.PrefetchScalarGridSpec(
            num_scalar_prefetch=1, grid=(p, r // tr),
            in_specs=[pl.BlockSpec((None, tr, c), lambda j, i, ids_ref: (ids_ref[j], i, 0)), blk], out_specs=blk))(ids, grad, theirs)


def _chips_start(parts, name, deps=()):
    n = len(parts)
    n_c = len(CHIP_FLIPS) - 1
    lands = [lax.empty((n_c,) + t.shape[1:], t.dtype) for t in parts]

    def issue(src, land, send_sems, recv_sems):
        me = _my_coords()
        for w in range(n):
            for j in range(1, n_c + 1):
                q = n_c * w + j - 1
                pltpu.make_async_remote_copy(src_ref=src[w].at[j], dst_ref=land[w].at[j - 1], send_sem=send_sems.at[q], recv_sem=recv_sems.at[q],
                                             device_id=_flip(me, CHIP_FLIPS[j]), device_id_type=MESH).start()

    return _split_start(issue, parts, lands, n_c * n, name, deps)


def _chips_wait(handle, after, name):
    n = handle["n"]
    n_c = len(CHIP_FLIPS) - 1

    def finish(src, land, send_sems, recv_sems):
        me = _my_coords()
        for w in range(n):
            for j in range(1, n_c + 1):
                q = n_c * w + j - 1
                cp = pltpu.make_async_remote_copy(src_ref=src[w].at[j], dst_ref=land[w].at[j - 1], send_sem=send_sems.at[q], recv_sem=recv_sems.at[q],
                                                  device_id=_flip(me, CHIP_FLIPS[j]), device_id_type=MESH)
                cp.wait_send()
                cp.wait_recv()

    return _split_wait(finish, handle, after, name)


def _after(t, *tokens):
    for tok in tokens:
        t = t + tok[0:1, 0:1]
    return t


def _rope_tables(positions):
    half = ROT // 2
    inv_freq = ROPE_THETA ** (-jnp.arange(0, ROT, 2, dtype=F32) / ROT)
    ang = positions.astype(F32).reshape(-1, 1) * inv_freq
    cos, sin = jnp.cos(ang), jnp.sin(ang)
    s = ang.shape[0]
    pad = jnp.zeros((s, HEAD_DIM - ROT), F32)
    zero = jnp.zeros((s, half), F32)
    two = lambda t: jnp.concatenate([t, t], axis=1)
    return (two(jnp.concatenate([cos, cos, pad + 1.0], axis=1)), two(jnp.concatenate([-sin, zero, pad], axis=1)),
            two(jnp.concatenate([zero, sin, pad], axis=1)))


def _local_step(x, tgt, tabs, mod, sinks_pad, hl, hg_norm, g_pre_mix, g_post_mix, g_pre_ffn, g_post_ffn, weights, scatter, scatter_on):
    s = x.shape[0]
    h1 = _pre_fwd(x, g_pre_mix, mod, 1, 0, "pre_mix_fwd")
    (w_in_t,) = weights("in", h1)
    proj = _mm_nt(h1, w_in_t, s, 256, D, F32, "proj_mm")
    att = _attn_fwd(proj, tabs, sinks_pad)
    o_raw, states = _hgrn_fwd(proj, hl)
    ohg = _hgout_fwd(o_raw, proj, hg_norm)
    w_attn_dm, w_hgrn_dm, w_out = weights("mix", ohg)
    y_a = _mm_nn_dm(att, w_attn_dm, s, F32, "attn_proj_mm")
    y_h = _mm_nn_dm(ohg, w_hgrn_dm, s, F32, "hgrn_proj_mm")
    merged = _merge_fwd(y_a, y_h, proj)
    y = _mm_nn(merged, w_out, s, 512, D, F32, "out_mm")
    x1 = _post_fwd(x, y, g_post_mix, mod, 2, "post_mix_fwd")
    h2 = _pre_fwd(x1, g_pre_ffn, mod, 4, 3, "pre_ffn_fwd")
    w_ffn_in_dm, w_ffn_out = weights("ffn", h2)
    gu = _mm_nn_dm(h2, w_ffn_in_dm, s // 2, F32, "ffn_in_mm")
    act = _swiglu_fwd(gu)
    y2 = _mm_nn(act, w_ffn_out, s, 512, FFN // 4, F32, "ffn_out_mm")
    err, loss = _post_fwd_loss(x1, y2, g_post_ffn, mod, 5, tgt, "post_ffn_loss")
    dy2, d_gate2, dg_post_ffn = _post_bwd(err, y2, g_post_ffn, mod, 5, "post_ffn_bwd")
    gw_ffn_out = _mm_tn(act, dy2, 512, D, BF16, "ffn_out_dw")
    t_pair = scatter([gw_ffn_out.reshape(N_DEV, FFN // N_DEV, D)], "ffn_out")
    d_act = _mm_nt(dy2, w_ffn_out, s, 512, D, F32, "ffn_out_dx", deps=[t_pair])
    dgu = _swiglu_bwd(d_act, gu)
    t_out = scatter_on("ffn_out", dgu)
    gw_ffn_in = _mm_tn_dm(h2, dgu, 1024, BF16, "ffn_in_dw")
    t_pair = scatter([gw_ffn_in], "ffn_in")
    dh2 = _mm_nt_dm(dgu, w_ffn_in_dm, s, 1024, F32, "ffn_in_dx", deps=[t_pair])
    mod = _after(mod, t_out)
    dx1, d_shift2, d_scale2, dg_pre_ffn = _pre_bwd(dh2, x1, err, g_pre_ffn, mod, 4, "pre_ffn_bwd")
    dy, d_gate1, dg_post_mix = _post_bwd(dx1, y, g_post_mix, mod, 2, "post_mix_bwd")
    t_in = scatter_on("ffn_in", dy)
    d_merged = _mm_nt(dy, w_out, s, 512, D, F32, "out_dx")
    gw_out = _mm_tn(merged, dy, 512, D, BF16, "out_dw")
    dy_a, dy_h, d_gate_a, d_gate_h = _merge_bwd(d_merged, y_a, y_h, proj)
    gw_attn = _mm_tn_dm(att, dy_a, ATT_W, BF16, "attn_proj_dw")
    gw_hgrn = _mm_tn_dm(ohg, dy_h, HG_W, BF16, "hgrn_proj_dw")
    t_pair = scatter([gw_attn, gw_hgrn, gw_out.reshape(N_DEV, D // N_DEV, D)], "mix")
    d_att = _mm_nt_dm(dy_a, w_attn_dm, s, ATT_W, F32, "attn_proj_dx", deps=[t_pair])
    d_ohg = _mm_nt_dm(dy_h, w_hgrn_dm, s, HG_W, F32, "hgrn_proj_dx")
    d_o, d_gh, d_hg_norm = _hgout_bwd(d_ohg, o_raw, proj, _after(hg_norm, t_in))
    d_qh, d_fh, d_ih, d_hl = _hgrn_bwd(proj, hl, states, d_o)
    t_mix = scatter_on("mix", d_qh)
    d_qa, d_ka, d_va, d_sinks = _attn_bwd(proj, tabs, _after(sinks_pad, t_mix), d_att)
    d_proj = jnp.concatenate([d_qa, d_ka.astype(BF16), d_va.astype(BF16), d_qh, d_fh, d_ih, d_gh, d_gate_a, d_gate_h], axis=1)
    dh1 = _mm_nn(d_proj, w_in_t, s // 2, 512, IN_COLS // 2, F32, "proj_dx")
    grad_x, d_shift1, d_scale1, dg_pre_mix = _pre_bwd(dh1, x, dx1, g_pre_mix, mod, 1, "pre_mix_bwd")
    d_mod = jnp.concatenate([d_shift1, d_scale1, d_gate1, d_shift2, d_scale2, d_gate2], axis=1)
    small = [d_mod, dg_pre_mix, dg_post_mix, dg_pre_ffn, dg_post_ffn, d_hl.reshape(1, 2 * HG_W), d_hg_norm, d_sinks]
    return loss, grad_x, small, h1, d_proj


def kernel(x, c, positions, w_ada, b_ada, g_pre_mix, g_post_mix, g_pre_ffn, g_post_ffn, w_in, attn_sinks, w_attn_proj, hg_lower_bounds, hg_norm, w_hgrn_proj, w_out, w_ffn_in, w_ffn_out, loss_target, m_w_ada, m_b_ada, m_g_pre_mix, m_g_post_mix, m_g_pre_ffn, m_g_post_ffn, m_w_in, m_attn_sinks, m_w_attn_proj, m_hg_lower_bounds, m_hg_norm, m_w_hgrn_proj, m_w_out, m_w_ffn_in, m_w_ffn_out, v_w_ada, v_b_ada, v_g_pre_mix, v_g_post_mix, v_g_pre_ffn, v_g_post_ffn, v_w_in, v_attn_sinks, v_w_attn_proj, v_hg_lower_bounds, v_hg_norm, v_w_hgrn_proj, v_w_out, v_w_ffn_in, v_w_ffn_out):
    my_id = _lin(_my_coords())
    s = x.shape[1]
    n_ada = w_ada.shape[2]

    c_all = _exchange_small(c.reshape(1, 1, D), True, "gather_c").reshape(N_DEV, D)
    b_cols = lax.dynamic_slice(b_ada, (0, my_id * n_ada), (1, n_ada))
    mod_part = _mod_part(c_all, w_ada[0], b_cols)
    mod = _exchange_small(mod_part.reshape(N_DEV, 1, n_ada), False, "scatter_mod").reshape(1, N_MOD * D)
    groups = {"in": [w_in[0].T], "mix": [w_attn_proj[0], w_hgrn_proj[0], w_out[0]], "ffn": [w_ffn_in[0], w_ffn_out[0]]}

    def start(group, dep):
        shards, dep = lax.optimization_barrier((groups[group], dep))
        return _gather_start([t.astype(BF16) for t in shards], "gather_start_" + group, deps=[dep])

    gathers = {"in": start("in", mod)}
    gathers["mix"] = start("mix", gathers["in"]["token"])
    gathers["ffn"] = start("ffn", gathers["mix"]["token"])

    def weights(group, after):
        after = [after, gathers["ffn"]["token"]]
        lands = _gather_pass(_gather_wait(gathers[group], after, "gather_wait_" + group), "gather_pass_" + group)
        if group == "in":
            return (lands[0].reshape(IN_COLS, D),)
        if group == "mix":
            return lands[0], lands[1], lands[2].reshape(D, D)
        return lands[0], lands[1].reshape(FFN, D)

    pairs, scatters = {}, {}

    def scatter(grads, group):
        pairs[group] = _pair_start(grads, "scatter_pair_" + group)
        return pairs[group]["token"]

    def scatter_on(group, after):
        local, theirs = _pair_wait(pairs[group], [after], "scatter_pair_wait_" + group)
        parts = [_pair_add(g, t, "scatter_pair_add_%s_%d" % (group, k)) for k, (g, t) in enumerate(zip(local, theirs))]
        scatters[group] = _chips_start(parts, "scatter_start_" + group)
        return scatters[group]["token"]

    sinks_pad = jnp.pad(attn_sinks, ((0, 0), (0, LANE - ATT_HEADS)))
    loss, grad_x, small, h1, d_proj = _local_step(
        x[0], loss_target[0], _rope_tables(positions), mod, sinks_pad, hg_lower_bounds, hg_norm, g_pre_mix, g_post_mix, g_pre_ffn, g_post_ffn,
        weights, scatter, scatter_on)
    loss = lax.psum(loss[0, 0], ("x", "y", "c"))

    sizes = [t.shape[1] for t in small]
    parts = _exchange_small(jnp.concatenate(small, axis=1).reshape(1, 1, sum(sizes)), True, "gather_small_grads")
    gw_in = _mm_tn(d_proj, h1, 256, D, BF16, "proj_dw", deps=[parts]).reshape(N_DEV, IN_COLS // N_DEV, D)
    scatter([gw_in], "in")
    scatter_on("in", gw_in)
    offs = [sum(sizes[:k]) for k in range(len(sizes))]
    piece = lambda k, n=None: parts[:, :, offs[k]:offs[k] + (sizes[k] if n is None else n)]
    small_w = [(piece(0), b_ada, m_b_ada, v_b_ada), (piece(1), g_pre_mix, m_g_pre_mix, v_g_pre_mix),
               (piece(2), g_post_mix, m_g_post_mix, v_g_post_mix), (piece(3), g_pre_ffn, m_g_pre_ffn, v_g_pre_ffn),
               (piece(4), g_post_ffn, m_g_post_ffn, v_g_post_ffn),
               (piece(5).reshape(N_DEV, 2, HG_W), hg_lower_bounds, m_hg_lower_bounds, v_hg_lower_bounds),
               (piece(6), hg_norm, m_hg_norm, v_hg_norm), (piece(7, ATT_HEADS), attn_sinks, m_attn_sinks, v_attn_sinks)]
    names = ["b_ada", "g_pre_mix", "g_post_mix", "g_pre_ffn", "g_post_ffn", "hg_lower_bounds", "hg_norm", "attn_sinks"]
    res = {n: _adamw(p, w, m, v, "adamw_" + n) for n, (p, w, m, v) in zip(names, small_w)}

    dmod_cols = lax.dynamic_slice(parts.reshape(N_DEV, -1), (0, my_id * n_ada), (N_DEV, n_ada))
    g_w_ada = _grad_w_ada(c_all.T, dmod_cols)
    res["w_ada"] = [g_w_ada] + list(_adamw(g_w_ada[None], w_ada[0], m_w_ada[0], v_w_ada[0], "adamw_w_ada", emit_grad=False))

    big = {"ffn_out": [("w_ffn_out", w_ffn_out, m_w_ffn_out, v_w_ffn_out)], "ffn_in": [("w_ffn_in", w_ffn_in, m_w_ffn_in, v_w_ffn_in)],
           "mix": [("w_attn_proj", w_attn_proj, m_w_attn_proj, v_w_attn_proj), ("w_hgrn_proj", w_hgrn_proj, m_w_hgrn_proj, v_w_hgrn_proj),
                   ("w_out", w_out, m_w_out, v_w_out)],
           "in": [("w_in", w_in, m_w_in, v_w_in)]}
    after = [scatters["in"]["token"]]
    for group, members in big.items():
        if group == "in":
            after = after + [res["w_ada"][1], res["b_ada"][1]]
        local, lands = _chips_wait(scatters[group], after, "scatter_wait_" + group)
        own = [t[0] for t in local]
        for (n, w, m, v), g_own, land in zip(members, own, lands):
            if group == "in":
                g_w = _sum_pieces(land, g_own, "sum_" + n).T
                res[n] = [g_w] + list(_adamw(g_w[None], w[0], m[0], v[0], "adamw_" + n, emit_grad=False))
            else:
                res[n] = _adamw(land, w[0], m[0], v[0], "adamw_" + n, own=g_own)
            after = [res[n][1]]

    order = ["w_ada", "b_ada", "g_pre_mix", "g_post_mix", "g_pre_ffn", "g_post_ffn", "w_in", "attn_sinks", "w_attn_proj",
             "hg_lower_bounds", "hg_norm", "w_hgrn_proj", "w_out", "w_ffn_in", "w_ffn_out"]
    lead = {"w_ada", "w_in", "w_attn_proj", "w_hgrn_proj", "w_out", "w_ffn_in", "w_ffn_out"}
    outs = [loss, grad_x[None]]
    for k in range(4):
        outs += [res[n][k][None] if n in lead else res[n][k] for n in order]
    return tuple(outs)
```

```python
import functools

import jax
import jax.numpy as jnp
from jax import lax
from jax.experimental import pallas as pl
from jax.experimental.pallas import tpu as pltpu

F32 = jnp.float32
BF16 = jnp.bfloat16

N_DEV = 8
D = 2048
ATT_HEADS = 16
KV_HEADS = 2
HEAD_DIM = 64
GROUP = ATT_HEADS // KV_HEADS
ATT_W = ATT_HEADS * HEAD_DIM
BLK = 128
ROT = HEAD_DIM // 4
ROPE_THETA = 500000.0
HG_HEADS = 8
HG_K = 128
HG_W = HG_HEADS * HG_K
CHUNK = 64
SUB = 16
FFN = 5632
N_MOD = 6
EPS = 1e-6
LANE = 128
Q_A, K_A, V_A, Q_H, F_H, I_H, G_H, GT_A, GT_H, IN_COLS = 0, 1024, 1152, 1280, 2304, 3328, 4352, 5376, 7424, 9472

ADAM_LR, ADAM_B1, ADAM_B2, ADAM_EPS, ADAM_WD, ADAM_STEP = 0.001, 0.9, 0.999, 1e-08, 0.01, 10

TR = 256
HG_TB = 512
VMEM_BIG = 56 << 20
MESH = pl.DeviceIdType.MESH


def _sds(shape, dtype):
    return jax.ShapeDtypeStruct(shape, dtype)


def _params(n_axes, vmem=None):
    return pltpu.CompilerParams(dimension_semantics=("arbitrary",) * n_axes, vmem_limit_bytes=vmem)


def _sig(t):
    return 1.0 / (1.0 + jnp.exp(-t))


def _dot(a, b, dims):
    return lax.dot_general(a, b, (dims, ((), ())), preferred_element_type=F32)


NN = ((1,), (0,))
NT = ((1,), (1,))
TN = ((0,), (0,))


def _matmul(a, b, a_spec, b_spec, o_spec, out_shape, grid, dims, acc_shape, name, deps=()):
    nk = grid[2]
    nd = len(deps)

    def body(a_ref, b_ref, *rest):
        o_ref, scratch = rest[nd], rest[nd + 1:]
        part = _dot(a_ref[...], b_ref[...], dims)
        if nk == 1:
            o_ref[...] = part.astype(o_ref.dtype)
        else:
            acc = scratch[0]
            k = pl.program_id(2)

            @pl.when(k == 0)
            def _():
                acc[...] = part

            @pl.when(k > 0)
            def _():
                acc[...] += part

            @pl.when(k == nk - 1)
            def _():
                o_ref[...] = acc[...].astype(o_ref.dtype)

    return pl.pallas_call(
        body, grid=grid, in_specs=[a_spec, b_spec] + [pl.BlockSpec(memory_space=pl.ANY)] * nd, out_specs=o_spec, out_shape=out_shape,
        scratch_shapes=[pltpu.VMEM(acc_shape, F32)] if nk > 1 else [],
        compiler_params=_params(3, VMEM_BIG), name=name)(a, b, *deps)


def _mm_nn(a, b, tm, tn, tk, out_dtype, name):
    m, k = a.shape
    n = b.shape[1]
    return _matmul(a, b, pl.BlockSpec((tm, tk), lambda j, i, kk: (i, kk)), pl.BlockSpec((tk, tn), lambda j, i, kk: (kk, j)),
                   pl.BlockSpec((tm, tn), lambda j, i, kk: (i, j)), _sds((m, n), out_dtype),
                   (n // tn, m // tm, k // tk), NN, (tm, tn), name)


def _mm_nn_dm(a, b, tm, out_dtype, name):
    m, k = a.shape
    n = b.shape[2]
    return _matmul(a, b, pl.BlockSpec((tm, k), lambda j, i, kk: (i, 0)), pl.BlockSpec((None, k, n), lambda j, i, kk: (j, 0, 0)),
                   pl.BlockSpec((tm, n), lambda j, i, kk: (i, j)), _sds((m, N_DEV * n), out_dtype),
                   (N_DEV, m // tm, 1), NN, (tm, n), name)


def _mm_nt(a, b, tm, tn, tk, out_dtype, name, deps=()):
    m, k = a.shape
    n = b.shape[0]
    return _matmul(a, b, pl.BlockSpec((tm, tk), lambda j, i, kk: (i, kk)), pl.BlockSpec((tn, tk), lambda j, i, kk: (j, kk)),
                   pl.BlockSpec((tm, tn), lambda j, i, kk: (i, j)), _sds((m, n), out_dtype),
                   (n // tn, m // tm, k // tk), NT, (tm, tn), name, deps)


def _mm_nt_dm(a, b, tm, tn, out_dtype, name, deps=()):
    m = a.shape[0]
    n_out, n = b.shape[1], b.shape[2]
    return _matmul(a, b, pl.BlockSpec((tm, n), lambda j, i, kk: (i, kk)), pl.BlockSpec((None, tn, n), lambda j, i, kk: (kk, j, 0)),
                   pl.BlockSpec((tm, tn), lambda j, i, kk: (i, j)), _sds((m, n_out), out_dtype),
                   (n_out // tn, m // tm, N_DEV), NT, (tm, tn), name, deps)


def _mm_tn(a, b, tm, tn, out_dtype, name, deps=()):
    s, m = a.shape
    n = b.shape[1]
    return _matmul(a, b, pl.BlockSpec((s, tm), lambda j, i, kk: (0, i)), pl.BlockSpec((s, tn), lambda j, i, kk: (0, j)),
                   pl.BlockSpec((tm, tn), lambda j, i, kk: (i, j)), _sds((m, n), out_dtype),
                   (n // tn, m // tm, 1), TN, (tm, tn), name, deps)


def _mm_tn_dm(a, b, tm, out_dtype, name):
    s, m = a.shape
    n = b.shape[1] // N_DEV
    return _matmul(a, b, pl.BlockSpec((s, tm), lambda j, i, kk: (0, i)), pl.BlockSpec((s, n), lambda j, i, kk: (0, j)),
                   pl.BlockSpec((None, tm, n), lambda j, i, kk: (j, i, 0)), _sds((N_DEV, m, n), out_dtype),
                   (N_DEV, m // tm, 1), TN, (tm, n), name)


def _row_spec():
    return pl.BlockSpec((TR, D), lambda i: (i, 0))


def _vec_spec(k=0):
    return pl.BlockSpec((1, D), lambda i: (0, k))


def _acc_rows(ref, first, val):
    @pl.when(first)
    def _():
        ref[...] = val

    @pl.when(jnp.logical_not(first))
    def _():
        ref[...] += val


def _pre_fwd(x, g, mod, k_scale, k_shift, name):
    s = x.shape[0]

    def body(x_ref, g_ref, sc_ref, sh_ref, h_ref):
        xv = x_ref[...]
        r = lax.rsqrt(jnp.mean(xv * xv, axis=-1, keepdims=True) + EPS)
        n = xv * r * g_ref[...]
        h_ref[...] = (n * (1.0 + sc_ref[...]) + sh_ref[...]).astype(h_ref.dtype)

    return pl.pallas_call(body, grid=(s // TR,), in_specs=[_row_spec(), _vec_spec(), _vec_spec(k_scale), _vec_spec(k_shift)],
                          out_specs=_row_spec(), out_shape=_sds((s, D), BF16), compiler_params=_params(1), name=name)(x, g, mod, mod)


def _post_fwd(x, y, g, mod, k_gate, name):
    s = x.shape[0]

    def body(x_ref, y_ref, g_ref, gt_ref, o_ref):
        yv = y_ref[...]
        r = lax.rsqrt(jnp.mean(yv * yv, axis=-1, keepdims=True) + EPS)
        o_ref[...] = x_ref[...] + gt_ref[...] * (yv * r * g_ref[...])

    return pl.pallas_call(body, grid=(s // TR,), in_specs=[_row_spec(), _row_spec(), _vec_spec(), _vec_spec(k_gate)],
                          out_specs=_row_spec(), out_shape=_sds((s, D), F32), compiler_params=_params(1), name=name)(x, y, g, mod)


def _post_fwd_loss(x, y, g, mod, k_gate, tgt, name):
    s = x.shape[0]

    def body(x_ref, y_ref, g_ref, gt_ref, t_ref, e_ref, loss_ref):
        i = pl.program_id(0)
        yv = y_ref[...]
        r = lax.rsqrt(jnp.mean(yv * yv, axis=-1, keepdims=True) + EPS)
        err = x_ref[...] + gt_ref[...] * (yv * r * g_ref[...]) - t_ref[...]
        e_ref[...] = err * (1.0 / D)
        part = 0.5 * jnp.sum(jnp.mean(err * err, axis=-1, keepdims=True), axis=0, keepdims=True)
        _acc_rows(loss_ref, i == 0, part)

    return pl.pallas_call(body, grid=(s // TR,),
                          in_specs=[_row_spec(), _row_spec(), _vec_spec(), _vec_spec(k_gate), _row_spec()],
                          out_specs=[_row_spec(), pl.BlockSpec((1, 1), lambda i: (0, 0))],
                          out_shape=[_sds((s, D), F32), _sds((1, 1), F32)], compiler_params=_params(1), name=name)(x, y, g, mod, tgt)


def _pre_bwd(dh, x, res, g, mod, k_scale, name):
    s = x.shape[0]

    def body(dh_ref, x_ref, res_ref, g_ref, sc_ref, dx_ref, dsh_ref, dsc_ref, dg_ref):
        first = pl.program_id(0) == 0
        xv, dh_v, gv = x_ref[...], dh_ref[...], g_ref[...]
        r = lax.rsqrt(jnp.mean(xv * xv, axis=-1, keepdims=True) + EPS)
        xh = xv * r
        dn = dh_v * (1.0 + sc_ref[...])
        dgn = dn * gv
        dx_ref[...] = res_ref[...] + r * (dgn - xh * jnp.mean(dgn * xh, axis=-1, keepdims=True))
        _acc_rows(dsh_ref, first, jnp.sum(dh_v, axis=0, keepdims=True))
        _acc_rows(dsc_ref, first, jnp.sum(dh_v * (xh * gv), axis=0, keepdims=True))
        _acc_rows(dg_ref, first, jnp.sum(dn * xh, axis=0, keepdims=True))

    return pl.pallas_call(body, grid=(s // TR,),
                          in_specs=[_row_spec(), _row_spec(), _row_spec(), _vec_spec(), _vec_spec(k_scale)],
                          out_specs=[_row_spec(), _vec_spec(), _vec_spec(), _vec_spec()],
                          out_shape=[_sds((s, D), F32)] + [_sds((1, D), F32)] * 3,
                          compiler_params=_params(1), name=name)(dh, x, res, g, mod)


def _post_bwd(dx, y, g, mod, k_gate, name):
    s = y.shape[0]

    def body(dx_ref, y_ref, g_ref, gt_ref, dy_ref, dgt_ref, dg_ref):
        first = pl.program_id(0) == 0
        yv, dxv, gv = y_ref[...], dx_ref[...], g_ref[...]
        r = lax.rsqrt(jnp.mean(yv * yv, axis=-1, keepdims=True) + EPS)
        yh = yv * r
        dn = dxv * gt_ref[...]
        dgn = dn * gv
        dy_ref[...] = (r * (dgn - yh * jnp.mean(dgn * yh, axis=-1, keepdims=True))).astype(dy_ref.dtype)
        _acc_rows(dgt_ref, first, jnp.sum(dxv * (yh * gv), axis=0, keepdims=True))
        _acc_rows(dg_ref, first, jnp.sum(dn * yh, axis=0, keepdims=True))

    return pl.pallas_call(body, grid=(s // TR,), in_specs=[_row_spec(), _row_spec(), _vec_spec(), _vec_spec(k_gate)],
                          out_specs=[_row_spec(), _vec_spec(), _vec_spec()],
                          out_shape=[_sds((s, D), BF16), _sds((1, D), F32), _sds((1, D), F32)],
                          compiler_params=_params(1), name=name)(dx, y, g, mod)


SW_TN = 1408
SW_TR = 512
TALL = 1024


def _swiglu_fwd(gu):
    s = gu.shape[0]
    nb = FFN // SW_TN

    def body(g_ref, u_ref, a_ref):
        gv = g_ref[...]
        a_ref[...] = (gv * _sig(gv) * u_ref[...]).astype(a_ref.dtype)

    return pl.pallas_call(body, grid=(s // SW_TR, nb),
                          in_specs=[pl.BlockSpec((SW_TR, SW_TN), lambda i, j: (i, j)), pl.BlockSpec((SW_TR, SW_TN), lambda i, j: (i, j + nb))],
                          out_specs=pl.BlockSpec((SW_TR, SW_TN), lambda i, j: (i, j)), out_shape=_sds((s, FFN), BF16),
                          compiler_params=_params(2, 48 << 20), name="swiglu_fwd")(gu, gu)


def _swiglu_bwd(dact, gu):
    s = gu.shape[0]
    nb = FFN // SW_TN
    n_steps = (s // SW_TR) * nb

    def body(da_ref, g_ref, u_ref, o_ref, buf, sems):
        i, j = pl.program_id(0), pl.program_id(1)
        step = i * nb + j
        slot = step % 2

        def tiles(sl):
            rows = pl.ds(pl.multiple_of(i * SW_TR, SW_TR), SW_TR)
            return [pltpu.make_async_copy(buf.at[sl, h], o_ref.at[rows, pl.ds(pl.multiple_of((j + nb * h) * SW_TN, LANE), SW_TN)], sems.at[sl, h])
                    for h in range(2)]

        @pl.when(step >= 2)
        def _():
            for cp in tiles(slot):
                cp.wait()

        gv, da = g_ref[...], da_ref[...]
        sg = _sig(gv)
        buf[slot, 0] = (da * u_ref[...] * (sg * (1.0 + gv * (1.0 - sg)))).astype(buf.dtype)
        buf[slot, 1] = (da * (gv * sg)).astype(buf.dtype)
        for cp in tiles(slot):
            cp.start()

        @pl.when(step == n_steps - 1)
        def _():
            for cp in tiles(slot) + (tiles(1 - slot) if n_steps > 1 else []):
                cp.wait()

    blk = lambda f: pl.BlockSpec((SW_TR, SW_TN), f)
    return pl.pallas_call(body, grid=(s // SW_TR, nb),
                          in_specs=[blk(lambda i, j: (i, j)), blk(lambda i, j: (i, j)), blk(lambda i, j: (i, j + nb))],
                          out_specs=pl.BlockSpec(memory_space=pl.ANY), out_shape=_sds((s, 2 * FFN), BF16),
                          scratch_shapes=[pltpu.VMEM((2, 2, SW_TR, SW_TN), BF16), pltpu.SemaphoreType.DMA((2, 2))],
                          compiler_params=_params(2, 48 << 20), name="swiglu_bwd")(dact, gu, gu)


MG_TN = 256


def _merge_fwd(y_a, y_h, proj):
    s = y_a.shape[0]
    tn = MG_TN
    ba, bh = GT_A // tn, GT_H // tn

    def body(ya_ref, yh_ref, ga_ref, gh_ref, m_ref):
        m_ref[...] = (_sig(ga_ref[...]) * ya_ref[...] + _sig(gh_ref[...]) * yh_ref[...]).astype(m_ref.dtype)

    tr = min(s, TALL)
    blk = lambda f: pl.BlockSpec((tr, tn), f)
    return pl.pallas_call(body, grid=(s // tr, D // tn),
                          in_specs=[blk(lambda i, j: (i, j)), blk(lambda i, j: (i, j)), blk(lambda i, j: (i, j + ba)), blk(lambda i, j: (i, j + bh))],
                          out_specs=blk(lambda i, j: (i, j)), out_shape=_sds((s, D), BF16),
                          compiler_params=_params(2), name="merge_fwd")(y_a, y_h, proj, proj)


def _merge_bwd(dm, y_a, y_h, proj):
    s = y_a.shape[0]
    tn = MG_TN
    ba, bh = GT_A // tn, GT_H // tn

    def body(dm_ref, ya_ref, yh_ref, ga_ref, gh_ref, dya_ref, dyh_ref, dga_ref, dgh_ref):
        dmv = dm_ref[...]
        sa, sh = _sig(ga_ref[...]), _sig(gh_ref[...])
        dya_ref[...] = (dmv * sa).astype(BF16)
        dyh_ref[...] = (dmv * sh).astype(BF16)
        dga_ref[...] = (dmv * ya_ref[...] * (sa * (1.0 - sa))).astype(BF16)
        dgh_ref[...] = (dmv * yh_ref[...] * (sh * (1.0 - sh))).astype(BF16)

    tr = min(s, TALL)
    blk = lambda f: pl.BlockSpec((tr, tn), f)
    nat = blk(lambda i, j: (i, j))
    return pl.pallas_call(body, grid=(s // tr, D // tn),
                          in_specs=[nat, nat, nat, blk(lambda i, j: (i, j + ba)), blk(lambda i, j: (i, j + bh))],
                          out_specs=[nat] * 4, out_shape=[_sds((s, D), BF16)] * 4,
                          compiler_params=_params(2), name="merge_bwd")(dm, y_a, y_h, proj, proj)


def _hgout_fwd(o_raw, proj, hg_norm):
    s = o_raw.shape[0]
    bg = G_H // LANE

    def body(o_ref, g_ref, n_ref, out_ref):
        ov = o_ref[...]
        r = lax.rsqrt(jnp.mean(ov * ov, axis=-1, keepdims=True) + EPS)
        out_ref[...] = (ov * r * n_ref[...] * _sig(g_ref[...])).astype(out_ref.dtype)

    tr = min(s, TALL)
    blk = lambda f: pl.BlockSpec((tr, LANE), f)
    return pl.pallas_call(body, grid=(s // tr, HG_HEADS),
                          in_specs=[blk(lambda i, h: (i, h)), blk(lambda i, h: (i, h + bg)), pl.BlockSpec((1, LANE), lambda i, h: (0, 0))],
                          out_specs=blk(lambda i, h: (i, h)), out_shape=_sds((s, HG_W), BF16),
                          compiler_params=_params(2), name="hgout_fwd")(o_raw, proj, hg_norm)


def _hgout_bwd(d_out, o_raw, proj, hg_norm):
    s = o_raw.shape[0]
    bg = G_H // LANE

    def body(d_ref, o_ref, g_ref, n_ref, do_ref, dg_ref, dn_ref):
        first = jnp.logical_and(pl.program_id(0) == 0, pl.program_id(1) == 0)
        ov, dv, nv = o_ref[...], d_ref[...], n_ref[...]
        sg = _sig(g_ref[...])
        r = lax.rsqrt(jnp.mean(ov * ov, axis=-1, keepdims=True) + EPS)
        oh = ov * r
        d_on = dv * sg
        dg_ref[...] = (dv * (oh * nv) * (sg * (1.0 - sg))).astype(dg_ref.dtype)
        t = d_on * nv
        do_ref[...] = r * (t - oh * jnp.mean(t * oh, axis=-1, keepdims=True))
        _acc_rows(dn_ref, first, jnp.sum(d_on * oh, axis=0, keepdims=True))

    tr = min(s, TALL)
    blk = lambda f: pl.BlockSpec((tr, LANE), f)
    vec = pl.BlockSpec((1, LANE), lambda i, h: (0, 0))
    return pl.pallas_call(body, grid=(s // tr, HG_HEADS),
                          in_specs=[blk(lambda i, h: (i, h)), blk(lambda i, h: (i, h)), blk(lambda i, h: (i, h + bg)), vec],
                          out_specs=[blk(lambda i, h: (i, h)), blk(lambda i, h: (i, h)), vec],
                          out_shape=[_sds((s, HG_W), F32), _sds((s, HG_W), BF16), _sds((1, LANE), F32)],
                          compiler_params=_params(2), name="hgout_bwd")(d_out, o_raw, proj, hg_norm)


def _rope(t, cos, s_lo, s_hi):
    return t * cos + pltpu.roll(t, LANE - ROT // 2, 1) * s_lo + pltpu.roll(t, ROT // 2, 1) * s_hi


def _rope_wide(t, cos, s_lo, s_hi):
    return jnp.concatenate([_rope(t[:, k * LANE:(k + 1) * LANE], cos, s_lo, s_hi) for k in range(t.shape[1] // LANE)], axis=1)


def _attn_mask(has_prev):
    qi = lax.broadcasted_iota(jnp.int32, (BLK, 2 * BLK), 0)
    kj = lax.broadcasted_iota(jnp.int32, (BLK, 2 * BLK), 1)
    rel = BLK + qi - kj
    band = jnp.logical_and(rel >= 0, rel < BLK)
    return jnp.logical_and(band, jnp.logical_or(has_prev, kj >= BLK))


def _attn_specs():
    prev = lambda i: jnp.maximum(i - 1, 0)
    kb, vb = K_A // LANE, V_A // LANE
    blk = lambda f: pl.BlockSpec((BLK, LANE), f)
    tabs = [blk(lambda i: (i, 0))] * 3 + [blk(lambda i: (prev(i), 0))] * 3
    return [pl.BlockSpec((BLK, ATT_W), lambda i: (i, 0)), blk(lambda i: (i, kb)), blk(lambda i: (prev(i), kb)),
            blk(lambda i: (i, vb)), blk(lambda i: (prev(i), vb))] + tabs + [pl.BlockSpec((1, LANE), lambda i: (0, 0))]


def _attn_logits(qh, kg):
    return _dot(qh, kg, NT)


def _attn_probs(raw, mask, sk):
    logits = jnp.where(mask, raw * (HEAD_DIM ** -0.5), -jnp.inf)
    m = jnp.maximum(jnp.max(logits, axis=-1, keepdims=True), sk)
    p = jnp.exp(logits - m)
    e_sink = jnp.exp(sk - m)
    inv = 1.0 / (jnp.sum(p, axis=-1, keepdims=True) + e_sink)
    return p * inv, e_sink * inv


def _attn_fwd(proj, tabs, sinks):
    s = proj.shape[0]

    def body(q_ref, kc_ref, kp_ref, vc_ref, vp_ref, c0, l0, h0, c1, l1, h1, sk_ref, o_ref):
        i = pl.program_id(0)
        mask = _attn_mask(i > 0)
        q = _rope_wide(q_ref[...], c0[...], l0[...], h0[...]).astype(BF16)
        kk = jnp.concatenate([_rope(kp_ref[...], c1[...], l1[...], h1[...]), _rope(kc_ref[...], c0[...], l0[...], h0[...])], axis=0).astype(BF16)
        vv = jnp.concatenate([vp_ref[...], vc_ref[...]], axis=0).astype(BF16)
        part = lambda t, h: t[:, h * HEAD_DIM:(h + 1) * HEAD_DIM]

        def head(h):
            raw = _attn_logits(part(q, h), part(kk, h // GROUP))
            yield
            sk = sk_ref[:, h:h + 1]
            logits = jnp.where(mask, raw * (HEAD_DIM ** -0.5), -jnp.inf)
            m = jnp.maximum(jnp.max(logits, axis=-1, keepdims=True), sk)
            yield
            p = jnp.exp(logits - m)
            den = jnp.sum(p, axis=-1, keepdims=True) + jnp.exp(sk - m)
            yield
            out = _dot((p * (1.0 / den)).astype(BF16), part(vv, h // GROUP), NN)
            yield
            return out

        o_ref[...] = jnp.concatenate(_interleave([head(h) for h in range(ATT_HEADS)]), axis=1).astype(o_ref.dtype)

    return pl.pallas_call(body, grid=(s // BLK,), in_specs=_attn_specs(),
                          out_specs=pl.BlockSpec((BLK, ATT_W), lambda i: (i, 0)), out_shape=_sds((s, ATT_W), BF16),
                          compiler_params=_params(1), name="attn_fwd")(proj, proj, proj, proj, proj, *tabs, *tabs, sinks)


def _attn_bwd(proj, tabs, sinks, d_att):
    s = proj.shape[0]

    def body(q_ref, kc_ref, kp_ref, vc_ref, vp_ref, c0, l0, h0, c1, l1, h1, sk_ref, do_ref, dq_ref, dk_ref, dv_ref, ds_ref):
        i = pl.program_id(0)

        @pl.when(i == 0)
        def _():
            dk_ref[...] = jnp.zeros_like(dk_ref)
            dv_ref[...] = jnp.zeros_like(dv_ref)
            ds_ref[...] = jnp.zeros_like(ds_ref)

        mask = _attn_mask(i > 0)
        q = _rope_wide(q_ref[...], c0[...], l0[...], h0[...]).astype(BF16)
        kk = jnp.concatenate([_rope(kp_ref[...], c1[...], l1[...], h1[...]), _rope(kc_ref[...], c0[...], l0[...], h0[...])], axis=0).astype(BF16)
        vv = jnp.concatenate([vp_ref[...], vc_ref[...]], axis=0).astype(BF16)
        d_o = do_ref[...].astype(BF16)
        lane = lax.broadcasted_iota(jnp.int32, (1, LANE), 1)
        part = lambda t, h: t[:, h * HEAD_DIM:(h + 1) * HEAD_DIM]

        def head(h):
            kg, vg = part(kk, h // GROUP), part(vv, h // GROUP)
            qh, doh = part(q, h), part(d_o, h)
            raw = _attn_logits(qh, kg)
            d_p = _dot(doh, vg, NT)
            yield
            prob, p_sink = _attn_probs(raw, mask, sk_ref[:, h:h + 1])
            yield
            dd = jnp.sum(prob * d_p, axis=-1, keepdims=True)
            yield
            d_s = (prob * (d_p - dd)).astype(BF16)
            d_sink = jnp.where(lane == h, -jnp.sum(p_sink * dd, axis=0, keepdims=True), 0.0)
            dq = _dot(d_s, kg, NN)
            dk = _dot(d_s, qh, TN)
            dv = _dot(prob.astype(BF16), doh, TN)
            yield
            return dq * (HEAD_DIM ** -0.5), dk * (HEAD_DIM ** -0.5), dv, d_sink

        per_head = _interleave([head(h) for h in range(ATT_HEADS)])
        dqs = [t[0] for t in per_head]
        group_sum = lambda k, g: functools.reduce(jnp.add, [t[k] for t in per_head[g * GROUP:(g + 1) * GROUP]])
        dks = [group_sum(1, g) for g in range(KV_HEADS)]
        dvs = [group_sum(2, g) for g in range(KV_HEADS)]
        d_sink = functools.reduce(jnp.add, [t[3] for t in per_head])
        dq_ref[...] = _rope_wide(jnp.concatenate(dqs, axis=1), c0[...], -l0[...], -h0[...]).astype(dq_ref.dtype)
        d_k = jnp.concatenate(dks, axis=1)
        d_v = jnp.concatenate(dvs, axis=1)
        cur = pl.ds(pl.multiple_of(i * BLK, BLK), BLK)
        prv = pl.ds(pl.multiple_of(jnp.maximum(i - 1, 0) * BLK, BLK), BLK)
        dk_ref[prv, :] += _rope(d_k[:BLK], c1[...], -l1[...], -h1[...])
        dk_ref[cur, :] += _rope(d_k[BLK:], c0[...], -l0[...], -h0[...])
        dv_ref[prv, :] += d_v[:BLK]
        dv_ref[cur, :] += d_v[BLK:]
        ds_ref[...] += d_sink

    full = pl.BlockSpec((s, LANE), lambda i: (0, 0))
    return pl.pallas_call(body, grid=(s // BLK,), in_specs=_attn_specs() + [pl.BlockSpec((BLK, ATT_W), lambda i: (i, 0))],
                          out_specs=[pl.BlockSpec((BLK, ATT_W), lambda i: (i, 0)), full, full, pl.BlockSpec((1, LANE), lambda i: (0, 0))],
                          out_shape=[_sds((s, ATT_W), BF16), _sds((s, LANE), F32), _sds((s, LANE), F32), _sds((1, LANE), F32)],
                          compiler_params=_params(1), name="attn_bwd")(proj, proj, proj, proj, proj, *tabs, *tabs, sinks, d_att)


def _tri_matmul(tri, t):
    hi = t.astype(BF16)
    r1 = t - hi.astype(F32)
    mid = r1.astype(BF16)
    lo = (r1 - mid.astype(F32)).astype(BF16)
    return _dot(tri, hi, NN) + _dot(tri, mid, NN) + _dot(tri, lo, NN)


def _lower_bound(hl):
    a, b = hl[0:1, :], hl[1:2, :]
    mx = jnp.maximum(a, b)
    ea, eb = jnp.exp(a - mx), jnp.exp(b - mx)
    return ea / (ea + eb)


def _hg_gates(q_raw, f_raw, lb, tri_lower):
    sg = _sig(f_raw)
    f = lb + (1.0 - lb) * sg
    sq = _sig(q_raw)
    b = _tri_matmul(tri_lower, jnp.log(f))
    return sg, f, 1.0 - f, sq, q_raw * sq, b


HG_PAIR_FWD = 8
HG_PAIR_BWD = 8


def _hg_specs(n_map, pair):
    blk = lambda off, p: pl.BlockSpec((HG_TB, LANE), lambda h, n: (n_map(n), off // LANE + pair * h + p))
    return [blk(off, p) for off in (Q_H, F_H, I_H) for p in range(pair)] + [pl.BlockSpec((2, pair * LANE), lambda h, n: (0, h))]


def _interleave(gens):
    out = [None] * len(gens)
    live = list(range(len(gens)))
    while live:
        for k in list(live):
            try:
                next(gens[k])
            except StopIteration as stop:
                out[k] = stop.value
                live.remove(k)
    return out


def _hg_spread():
    c = lax.broadcasted_iota(jnp.int32, (CHUNK, SUB * SUB), 0)
    l = lax.broadcasted_iota(jnp.int32, (CHUNK, SUB * SUB), 1)
    r = lax.broadcasted_iota(jnp.int32, (SUB, SUB * SUB), 0)
    lr = lax.broadcasted_iota(jnp.int32, (SUB, SUB * SUB), 1)
    cols = [(c == lo + (l >> 4)).astype(BF16) for lo in range(0, CHUNK, SUB)]
    tile = [(c == lo + (l & (SUB - 1))).astype(BF16) for lo in range(0, CHUNK, SUB)]
    return cols, tile, (lr & (SUB - 1)) == r, (lr >> 4) == r


def _hg_intra(qs, kk, b, grad=None):
    lane = lax.broadcasted_iota(jnp.int32, (SUB, CHUNK), 1)
    row1 = lax.broadcasted_iota(jnp.int32, (SUB, 1), 0)
    kk_b = kk.astype(BF16)
    if grad is not None:
        d_a, d_at, (cols, tile, diag, block) = grad
    a_blocks, dq_blocks, dk_blocks, db_blocks = [], [], [], []
    dk_left = None
    for j in range(CHUNK // SUB):
        lo = j * SUB
        q_j, k_j, b_j = qs[lo:lo + SUB], kk[lo:lo + SUB], b[lo:lo + SUB]
        es = [jnp.where(row1 >= sx, jnp.exp(jnp.minimum(b_j - b_j[sx:sx + 1], 0.0)), 0.0) for sx in range(SUB)]
        pes = [q_j * e for e in es]
        pe = jnp.concatenate(pes, axis=0).astype(BF16)
        pairs = _dot(pe, kk_b, NT)
        yield
        a_j = jnp.zeros((SUB, CHUNK), F32)
        for sx in range(SUB):
            a_j = jnp.where(lane == lo + sx, pairs[sx * SUB:(sx + 1) * SUB], a_j)
        if grad is not None:
            da_j = d_a[lo:lo + SUB]
            ek = jnp.concatenate([e * k_j[sx:sx + 1] for sx, e in enumerate(es)], axis=0).astype(BF16)
            sel_t = jnp.where(diag, _dot(da_j.astype(BF16), cols[j], NN), 0.0).astype(BF16)
            sel_s = jnp.where(block, _dot(d_at[lo:lo + SUB].astype(BF16), tile[j], NN), 0.0).astype(BF16)
            pek = jnp.concatenate([p * k_j[sx:sx + 1] for sx, p in enumerate(pes)], axis=0).astype(BF16)
            yield
            dq_j = _dot(sel_t, ek, NN)
            dk_j = _dot(sel_s, pe, NN)
            db_j = _dot(sel_t, pek, NN) - _dot(sel_s, pek, NN)
            yield
        if j > 0:
            ref = b[lo - 1:lo]
            sc_q = jnp.exp(b_j - ref)
            sc_k = jnp.exp(jnp.minimum(ref - b, 0.0))
            qt = (q_j * sc_q).astype(BF16)
            kt = (kk * sc_k).astype(BF16)
            left = _dot(qt, kt, NT)
            yield
            a_j = a_j + jnp.where(lane < lo, left, 0.0)
            if grad is not None:
                da_left = jnp.where(lane < lo, da_j, 0.0).astype(BF16)
                dq_left = _dot(da_left, kt, NN) * sc_q
                dq_j = dq_j + dq_left
                db_j = db_j + q_j * dq_left
                t = _dot(da_left, qt, TN)
                yield
                t = t * sc_k
                dk_left = t if dk_left is None else dk_left + t
        a_blocks.append(a_j)
        if grad is not None:
            dq_blocks.append(dq_j)
            dk_blocks.append(dk_j)
            db_blocks.append(db_j)
    a = jnp.concatenate(a_blocks, axis=0)
    if grad is None:
        return a
    return a, jnp.concatenate(dq_blocks, axis=0), jnp.concatenate(dk_blocks, axis=0) + dk_left, jnp.concatenate(db_blocks, axis=0) - kk * dk_left


def _hgrn_fwd(proj, hl):
    s = proj.shape[0]
    n_chunk = HG_TB // CHUNK
    pair = HG_PAIR_FWD

    def body(*refs):
        q_refs, f_refs, i_refs = refs[:pair], refs[pair:2 * pair], refs[2 * pair:3 * pair]
        hl_ref, o_ref, st_out_ref, st_ref = refs[3 * pair:]

        @pl.when(pl.program_id(1) == 0)
        def _():
            st_ref[...] = jnp.zeros_like(st_ref)

        r_i = lax.broadcasted_iota(jnp.int32, (CHUNK, CHUNK), 0)
        c_i = lax.broadcasted_iota(jnp.int32, (CHUNK, CHUNK), 1)
        tri_lower = (r_i >= c_i).astype(BF16)

        def chunk(c, carry):
            rows = pl.ds(pl.multiple_of(c * CHUNK, CHUNK), CHUNK)
            def head(p):
                cols = slice(p * LANE, (p + 1) * LANE)
                lb = _lower_bound(hl_ref[:, cols])
                v = i_refs[p][rows, :].astype(BF16)
                _, _, kk, _, qs, b = _hg_gates(q_refs[p][rows, :], f_refs[p][rows, :], lb, tri_lower)
                yield
                st = st_ref[p]
                st_b = st.astype(BF16)
                st_out_ref[p, c] = st_b
                o_state = _dot((qs * jnp.exp(b)).astype(BF16), st_b, NT)
                b_last = b[CHUNK - 1:CHUNK, :]
                st_new = _dot(v, (kk * jnp.exp(b_last - b)).astype(BF16), TN)
                a = yield from _hg_intra(qs, kk, b)
                st_ref[p] = st * jnp.exp(b_last) + st_new
                o_ref[rows, cols] = o_state + _dot(a.astype(BF16), v, NN)

            _interleave([head(p) for p in range(pair)])
            return carry

        lax.fori_loop(0, n_chunk, chunk, 0)

    return pl.pallas_call(
        body, grid=(HG_HEADS // pair, s // HG_TB), in_specs=_hg_specs(lambda n: n, pair),
        out_specs=[pl.BlockSpec((HG_TB, pair * LANE), lambda h, n: (n, h)), pl.BlockSpec((pair, n_chunk, HG_K, HG_K), lambda h, n: (h, n, 0, 0))],
        out_shape=[_sds((s, HG_W), F32), _sds((HG_HEADS, s // CHUNK, HG_K, HG_K), BF16)],
        scratch_shapes=[pltpu.VMEM((pair, HG_K, HG_K), F32)],
        compiler_params=_params(2), name="hgrn_fwd")(*[proj] * (3 * pair), hl)


def _hgrn_bwd(proj, hl, states, d_o):
    s = proj.shape[0]
    n_chunk = HG_TB // CHUNK
    n_blk = s // HG_TB
    pair = HG_PAIR_BWD
    rev = lambda n: n_blk - 1 - n

    def body(*refs):
        q_refs, f_refs, i_refs = refs[:pair], refs[pair:2 * pair], refs[2 * pair:3 * pair]
        hl_ref, st_in_ref, do_ref, dq_ref, df_ref, di_ref, dhl_ref, dst_ref, dlb_ref = refs[3 * pair:]
        n = pl.program_id(1)

        @pl.when(n == 0)
        def _():
            dst_ref[...] = jnp.zeros_like(dst_ref)
            dlb_ref[...] = jnp.zeros_like(dlb_ref)

        r_i = lax.broadcasted_iota(jnp.int32, (CHUNK, CHUNK), 0)
        c_i = lax.broadcasted_iota(jnp.int32, (CHUNK, CHUNK), 1)
        tri_lower = (r_i >= c_i).astype(BF16)
        tri_upper = (r_i <= c_i).astype(BF16)
        row = lax.broadcasted_iota(jnp.int32, (CHUNK, 1), 0)
        spread = _hg_spread()

        def chunk(cc, carry):
            c = n_chunk - 1 - cc
            rows = pl.ds(pl.multiple_of(c * CHUNK, CHUNK), CHUNK)
            def head(p):
                cols = slice(p * LANE, (p + 1) * LANE)
                lb = _lower_bound(hl_ref[:, cols])
                q_raw = q_refs[p][rows, :]
                vb = i_refs[p][rows, :].astype(BF16)
                sg, f, kk, sq, qs, b = _hg_gates(q_raw, f_refs[p][rows, :], lb, tri_lower)
                yield
                e_b = jnp.exp(b)
                qe = qs * e_b
                b_last = b[CHUNK - 1:CHUNK, :]
                e_last = jnp.exp(b_last)
                e_kd = jnp.exp(b_last - b)
                kd = kk * e_kd
                st0 = st_in_ref[p, c]
                d_ob = do_ref[rows, cols].astype(BF16)
                dst = dst_ref[p]
                dst_b = dst.astype(BF16)
                d_a = jnp.where(r_i >= c_i, _dot(d_ob, vb, NT), 0.0)
                d_at = jnp.where(r_i <= c_i, _dot(vb, d_ob, NT), 0.0)
                d_v_st = _dot(kd.astype(BF16), dst_b, NT)
                d_kd = _dot(vb, dst_b, NN)
                d_qe = _dot(d_ob, st0, NN)
                dst_new = _dot(d_ob, qe.astype(BF16), TN)
                yield
                a, dqs, dkk, d_b = yield from _hg_intra(qs, kk, b, (d_a, d_at, spread))
                d_v = _dot(a.astype(BF16), d_ob, TN) + d_v_st
                dqs_st = d_qe * e_b
                dkk_st = d_kd * e_kd
                dqs = dqs + dqs_st
                dkk = dkk + dkk_st
                d_b_last = jnp.sum(d_kd * kd, axis=0, keepdims=True) + jnp.sum(dst * st0.astype(F32), axis=0, keepdims=True) * e_last
                d_b = d_b + qs * dqs_st - kk * dkk_st + jnp.where(row == CHUNK - 1, d_b_last, 0.0)
                d_g = _tri_matmul(tri_upper, d_b)
                dst_ref[p] = dst_new + dst * e_last
                yield
                d_f = d_g / f - dkk
                dlb_ref[:, cols] += jnp.sum(d_f * (1.0 - sg), axis=0, keepdims=True)
                dq_ref[rows, cols] = (dqs * (sq * (1.0 + q_raw * (1.0 - sq)))).astype(dq_ref.dtype)
                df_ref[rows, cols] = (d_f * (1.0 - lb) * (sg * (1.0 - sg))).astype(df_ref.dtype)
                di_ref[rows, cols] = d_v.astype(di_ref.dtype)

            _interleave([head(p) for p in range(pair)])
            return carry

        lax.fori_loop(0, n_chunk, chunk, 0)

        @pl.when(n == n_blk - 1)
        def _():
            lb = _lower_bound(hl_ref[...])
            d_hl0 = dlb_ref[...] * (lb * (1.0 - lb))
            dhl_ref[...] = jnp.concatenate([d_hl0, -d_hl0], axis=0)

    out_blk = pl.BlockSpec((HG_TB, pair * LANE), lambda h, n: (rev(n), h))
    return pl.pallas_call(
        body, grid=(HG_HEADS // pair, n_blk),
        in_specs=_hg_specs(rev, pair) + [pl.BlockSpec((pair, n_chunk, HG_K, HG_K), lambda h, n: (h, rev(n), 0, 0)), out_blk],
        out_specs=[out_blk, out_blk, out_blk, pl.BlockSpec((2, pair * LANE), lambda h, n: (0, h))],
        out_shape=[_sds((s, HG_W), BF16)] * 3 + [_sds((2, HG_W), F32)],
        scratch_shapes=[pltpu.VMEM((pair, HG_K, HG_K), F32), pltpu.VMEM((1, pair * LANE), F32)],
        compiler_params=_params(2), name="hgrn_bwd")(*[proj] * (3 * pair), hl, states, d_o)


def _mod_part(c_all, w_shard, b_shard):
    n = w_shard.shape[1]
    tn = 512

    def body(c_ref, w_ref, b_ref, o_ref):
        o_ref[...] = _dot(c_ref[...].astype(BF16), w_ref[...].astype(BF16), NN) + b_ref[...]

    return pl.pallas_call(body, grid=(n // tn,),
                          in_specs=[pl.BlockSpec((N_DEV, D), lambda j: (0, 0)), pl.BlockSpec((D, tn), lambda j: (0, j)), pl.BlockSpec((1, tn), lambda j: (0, j))],
                          out_specs=pl.BlockSpec((N_DEV, tn), lambda j: (0, j)), out_shape=_sds((N_DEV, n), F32),
                          compiler_params=_params(1, 32 << 20), name="mod_part")(c_all, w_shard, b_shard)


def _grad_w_ada(c_all_t, dmod_cols):
    n = dmod_cols.shape[1]
    tn = 512

    def body(c_ref, d_ref, o_ref):
        cv = c_ref[...].astype(BF16).astype(F32)
        dv = d_ref[...].astype(BF16).astype(F32)
        acc = cv[:, 0:1] * dv[0:1, :]
        for k in range(1, N_DEV):
            acc = acc + cv[:, k:k + 1] * dv[k:k + 1, :]
        o_ref[...] = acc

    return pl.pallas_call(body, grid=(n // tn,),
                          in_specs=[pl.BlockSpec((D, N_DEV), lambda j: (0, 0)), pl.BlockSpec((N_DEV, tn), lambda j: (0, j))],
                          out_specs=pl.BlockSpec((D, tn), lambda j: (0, j)), out_shape=_sds((D, n), F32),
                          compiler_params=_params(1, 32 << 20), name="grad_w_ada")(c_all_t, dmod_cols)


def _row_tile(r, c, max_elems=1 << 18):
    if r * c <= max_elems or r % 8:
        return r
    best = 8
    for t in range(8, r + 1, 8):
        if r % t == 0 and t * c <= max_elems:
            best = t
    return best


WIDE_TILE = 5 << 17


def _sum_pieces(pieces, own, name):
    p, r, c = pieces.shape
    tr = _row_tile(r, c, WIDE_TILE)

    def body(o_ref, p_ref, g_ref):
        g = o_ref[...].astype(F32)
        for k in range(p):
            g = g + p_ref[k].astype(F32)
        g_ref[...] = g

    blk = pl.BlockSpec((tr, c), lambda i: (i, 0))
    return pl.pallas_call(body, grid=(r // tr,), in_specs=[blk, pl.BlockSpec((p, tr, c), lambda i: (0, i, 0))], out_specs=blk,
                          out_shape=_sds((r, c), F32), compiler_params=_params(1), name=name)(own, pieces)


def _adamw(pieces, w, m, v, name, emit_grad=True, own=None):
    p, r, c = pieces.shape
    tr = _row_tile(r, c)
    c1 = 1.0 / (1.0 - ADAM_B1 ** ADAM_STEP)
    c2 = 1.0 / (1.0 - ADAM_B2 ** ADAM_STEP)

    def body(*refs):
        if own is None:
            p_ref, w_ref, m_ref, v_ref, *outs = refs
            g = p_ref[0].astype(F32)
        else:
            o_ref, p_ref, w_ref, m_ref, v_ref, *outs = refs
            g = o_ref[...].astype(F32) + p_ref[0].astype(F32)
        for k in range(1, p):
            g = g + p_ref[k].astype(F32)
        m2 = ADAM_B1 * m_ref[...] + (1.0 - ADAM_B1) * g
        v2 = ADAM_B2 * v_ref[...] + (1.0 - ADAM_B2) * (g * g)
        delta = -ADAM_LR * ((m2 * c1) / (jnp.sqrt(v2 * c2) + ADAM_EPS) + ADAM_WD * w_ref[...])
        if emit_grad:
            outs[0][...] = g
        outs[-3][...] = delta
        outs[-2][...] = m2
        outs[-1][...] = v2

    blk = pl.BlockSpec((tr, c), lambda i: (i, 0))
    n_out = 4 if emit_grad else 3
    lead = [] if own is None else [own]
    return pl.pallas_call(body, grid=(r // tr,), in_specs=[blk] * len(lead) + [pl.BlockSpec((p, tr, c), lambda i: (0, i, 0)), blk, blk, blk],
                          out_specs=[blk] * n_out, out_shape=[_sds((r, c), F32)] * n_out,
                          compiler_params=_params(1, 48 << 20), name=name)(*lead, pieces, w, m, v)


def _my_coords():
    return lax.axis_index("x"), lax.axis_index("y"), lax.axis_index("c")


def _flip(coords, k):
    x, y, c = coords
    return (1 - x if k & 4 else x, 1 - y if k & 2 else y, 1 - c if k & 1 else c)


def _lin(coords):
    return 4 * coords[0] + 2 * coords[1] + coords[2]


def _exchange_small(x3, bcast, name):
    n = x3.shape[2]

    def body(x_ref, o_ref, send_sems, recv_sems):
        me = _my_coords()
        my_id = _lin(me)
        o_ref[pl.ds(my_id, 1)] = x_ref[pl.ds(0 if bcast else my_id, 1)]
        copies = []
        for k in range(1, N_DEV):
            peer = _flip(me, k)
            src = x_ref.at[0 if bcast else _lin(peer)]
            cp = pltpu.make_async_remote_copy(src_ref=src, dst_ref=o_ref.at[my_id], send_sem=send_sems.at[k], recv_sem=recv_sems.at[k],
                                              device_id=peer, device_id_type=MESH)
            cp.start()
            copies.append(cp)
        for k in range(1, N_DEV):
            peer = _flip(me, k)
            pltpu.make_async_remote_copy(src_ref=x_ref.at[0], dst_ref=o_ref.at[_lin(peer)], send_sem=send_sems.at[k], recv_sem=recv_sems.at[k],
                                         device_id=peer, device_id_type=MESH).wait_recv()
        for cp in copies:
            cp.wait_send()

    vm = pl.BlockSpec(memory_space=pltpu.VMEM)
    return pl.pallas_call(body, in_specs=[vm], out_specs=vm, out_shape=_sds((N_DEV, 1, n), F32),
                          scratch_shapes=[pltpu.SemaphoreType.DMA((N_DEV,)), pltpu.SemaphoreType.DMA((N_DEV,))], name=name)(x3)


HBM_SPEC = pl.BlockSpec(memory_space=pltpu.HBM)
SEM_SPEC = pl.BlockSpec(memory_space=pltpu.SEMAPHORE)
ANY_SPEC = pl.BlockSpec(memory_space=pl.ANY)
DATAFLOW = pltpu.SideEffectType.DATAFLOW_SIDE_EFFECTING
GATHER_FLIPS = (1, 2, 4, 6)
PASS_FLIPS = (2, 4, 6)
TOKEN = (8, LANE)


def _hbm(t):
    return pltpu.with_memory_space_constraint(t, pltpu.HBM)


def _hbm_like(ts):
    return [pltpu.HBM(t.shape, t.dtype) for t in ts]


def _split_start(issue, srcs, lands, n_sem, name, deps=()):
    n, nd = len(srcs), len(deps)

    def body(*refs):
        issue(refs[:n], refs[n:2 * n], refs[2 * n + nd], refs[2 * n + nd + 1])
        refs[-1][...] = jnp.zeros(TOKEN, F32)

    outs = pl.pallas_call(
        body, name=name,
        out_shape=(pltpu.SemaphoreType.DMA((n_sem,)), pltpu.SemaphoreType.DMA((n_sem,)), *_hbm_like(srcs), *_hbm_like(lands), _sds(TOKEN, F32)),
        in_specs=[HBM_SPEC] * (2 * n) + [ANY_SPEC] * nd,
        out_specs=(SEM_SPEC, SEM_SPEC, *[HBM_SPEC] * (2 * n), pl.BlockSpec(memory_space=pltpu.VMEM)),
        input_output_aliases={i: 2 + i for i in range(2 * n)},
        compiler_params=pltpu.CompilerParams(has_side_effects=DATAFLOW))(*[_hbm(t) for t in srcs], *[_hbm(t) for t in lands], *deps)
    return dict(sems=outs[:2], thru=list(outs[2:2 + 2 * n]), token=outs[-1], n=n)


def _split_wait(finish, handle, after, name):
    n = handle["n"]
    thru = handle["thru"]

    def body(*refs):
        finish(refs[:n], refs[n:2 * n], refs[2 * n], refs[2 * n + 1])

    outs = pl.pallas_call(
        body, name=name, out_shape=_hbm_like(thru), in_specs=[HBM_SPEC] * (2 * n) + [SEM_SPEC, SEM_SPEC] + [ANY_SPEC] * len(after),
        out_specs=[HBM_SPEC] * (2 * n), input_output_aliases={i: i for i in range(2 * n)},
        compiler_params=pltpu.CompilerParams(has_side_effects=DATAFLOW))(*thru, *handle["sems"], *after)
    return list(outs[:n]), list(outs[n:])


def _gather_start(shards, name, deps=()):
    n = len(shards)
    my_id = _lin(_my_coords())
    lands = [lax.dynamic_update_slice(lax.empty((N_DEV,) + t.shape, t.dtype), t[None], (my_id, 0, 0)) for t in shards]

    def issue(src, land, send_sems, recv_sems):
        me = _my_coords()
        for w in range(n):
            for j, k in enumerate(GATHER_FLIPS):
                q = len(GATHER_FLIPS) * w + j
                pltpu.make_async_remote_copy(src_ref=src[w], dst_ref=land[w].at[_lin(me)], send_sem=send_sems.at[q], recv_sem=recv_sems.at[q],
                                             device_id=_flip(me, k), device_id_type=MESH).start()

    return _split_start(issue, shards, lands, len(GATHER_FLIPS) * n, name, deps)


def _gather_wait(handle, after, name):
    n = handle["n"]

    def finish(src, land, send_sems, recv_sems):
        me = _my_coords()
        for w in range(n):
            for j, k in enumerate(GATHER_FLIPS):
                q = len(GATHER_FLIPS) * w + j
                peer = _flip(me, k)
                cp = pltpu.make_async_remote_copy(src_ref=src[w], dst_ref=land[w].at[_lin(peer)], send_sem=send_sems.at[q], recv_sem=recv_sems.at[q],
                                                  device_id=peer, device_id_type=MESH)
                cp.wait_send()
                cp.wait_recv()

    return _split_wait(finish, handle, after, name)[1]


def _gather_pass(lands, name):
    n = len(lands)
    n_p = len(PASS_FLIPS)

    def body(*refs):
        land = refs[n:2 * n]
        send_sems, recv_sems = refs[2 * n:]
        me = _my_coords()
        sibling = _flip(me, 1)
        sent = []
        for w in range(n):
            for j, k in enumerate(PASS_FLIPS):
                blk = land[w].at[_lin(_flip(me, k))]
                cp = pltpu.make_async_remote_copy(src_ref=blk, dst_ref=blk, send_sem=send_sems.at[n_p * w + j], recv_sem=recv_sems.at[n_p * w + j],
                                                  device_id=sibling, device_id_type=MESH)
                cp.start()
                sent.append(cp)
        for w in range(n):
            for j, k in enumerate(PASS_FLIPS):
                blk = land[w].at[_lin(_flip(me, k + 1))]
                pltpu.make_async_remote_copy(src_ref=blk, dst_ref=blk, send_sem=send_sems.at[n_p * w + j], recv_sem=recv_sems.at[n_p * w + j],
                                             device_id=sibling, device_id_type=MESH).wait_recv()
        for cp in sent:
            cp.wait_send()

    return pl.pallas_call(body, in_specs=[ANY_SPEC] * n, out_specs=[ANY_SPEC] * n, out_shape=[_sds(t.shape, t.dtype) for t in lands],
                          input_output_aliases={i: i for i in range(n)},
                          scratch_shapes=[pltpu.SemaphoreType.DMA((n_p * n,)), pltpu.SemaphoreType.DMA((n_p * n,))], name=name)(*lands)


CHIP_FLIPS = (0, 2, 4, 6)


def _pair_copy(src, land, send_sems, recv_sems, w, j):
    me = _my_coords()
    q = len(CHIP_FLIPS) * w + j
    return pltpu.make_async_remote_copy(src_ref=src[w].at[_lin(_flip(me, CHIP_FLIPS[j] + 1))], dst_ref=land[w].at[j], send_sem=send_sems.at[q],
                                        recv_sem=recv_sems.at[q], device_id=_flip(me, 1), device_id_type=MESH)


def _pair_exchange(grads, name):
    n = len(grads)

    def body(*refs):
        src, land = refs[:n], refs[n:2 * n]
        send_sems, recv_sems = refs[2 * n:]
        sent = [_pair_copy(src, land, send_sems, recv_sems, w, j) for w in range(n) for j in range(len(CHIP_FLIPS))]
        for cp in sent:
            cp.start()
        for cp in sent:
            cp.wait_recv()
        for cp in sent:
            cp.wait_send()

    outs = pl.pallas_call(body, in_specs=[ANY_SPEC] * n, out_specs=[ANY_SPEC] * n,
                          out_shape=[_sds((len(CHIP_FLIPS),) + g.shape[1:], g.dtype) for g in grads],
                          scratch_shapes=[pltpu.SemaphoreType.DMA((len(CHIP_FLIPS) * n,))] * 2, name=name)(*grads)
    return list(outs)


def _pair_start(grads, name, deps=()):
    n = len(grads)
    lands = [lax.empty((len(CHIP_FLIPS),) + g.shape[1:], g.dtype) for g in grads]

    def issue(src, land, send_sems, recv_sems):
        for w in range(n):
            for j in range(len(CHIP_FLIPS)):
                _pair_copy(src, land, send_sems, recv_sems, w, j).start()

    return _split_start(issue, grads, lands, len(CHIP_FLIPS) * n, name, deps)


def _pair_wait(handle, after, name):
    n = handle["n"]

    def finish(src, land, send_sems, recv_sems):
        for w in range(n):
            for j in range(len(CHIP_FLIPS)):
                cp = _pair_copy(src, land, send_sems, recv_sems, w, j)
                cp.wait_send()
                cp.wait_recv()

    return _split_wait(finish, handle, after, name)


def _pair_add(grad, theirs, name):
    p, r, c = theirs.shape
    tr = _row_tile(r, c, WIDE_TILE)
    me = _my_coords()
    ids = jnp.stack([_lin(_flip(me, k)) for k in CHIP_FLIPS]).astype(jnp.int32)

    def body(ids_ref, a_ref, b_ref, o_ref):
        o_ref[...] = (a_ref[...].astype(F32) + b_ref[...].astype(F32)).astype(o_ref.dtype)

    blk = pl.BlockSpec((None, tr, c), lambda j, i, ids_ref: (j, i, 0))
    return pl.pallas_call(
        body, out_shape=_sds((p, r, c), theirs.dtype), compiler_params=_params(2), name=name,
        grid_spec=pltpu.PrefetchScalarGridSpec(
            num_scalar_prefetch=1, grid=(p, r // tr),
            in_specs=[pl.BlockSpec((None, tr, c), lambda j, i, ids_ref: (ids_ref[j], i, 0)), blk], out_specs=blk))(ids, grad, theirs)


def _chips_start(parts, name, deps=()):
    n = len(parts)
    n_c = len(CHIP_FLIPS) - 1
    lands = [lax.empty((n_c,) + t.shape[1:], t.dtype) for t in parts]

    def issue(src, land, send_sems, recv_sems):
        me = _my_coords()
        for w in range(n):
            for j in range(1, n_c + 1):
                q = n_c * w + j - 1
                pltpu.make_async_remote_copy(src_ref=src[w].at[j], dst_ref=land[w].at[j - 1], send_sem=send_sems.at[q], recv_sem=recv_sems.at[q],
                                             device_id=_flip(me, CHIP_FLIPS[j]), device_id_type=MESH).start()

    return _split_start(issue, parts, lands, n_c * n, name, deps)


def _chips_wait(handle, after, name):
    n = handle["n"]
    n_c = len(CHIP_FLIPS) - 1

    def finish(src, land, send_sems, recv_sems):
        me = _my_coords()
        for w in range(n):
            for j in range(1, n_c + 1):
                q = n_c * w + j - 1
                cp = pltpu.make_async_remote_copy(src_ref=src[w].at[j], dst_ref=land[w].at[j - 1], send_sem=send_sems.at[q], recv_sem=recv_sems.at[q],
                                                  device_id=_flip(me, CHIP_FLIPS[j]), device_id_type=MESH)
                cp.wait_send()
                cp.wait_recv()

    return _split_wait(finish, handle, after, name)


def _after(t, *tokens):
    for tok in tokens:
        t = t + tok[0:1, 0:1]
    return t


def _rope_tables(positions):
    half = ROT // 2
    inv_freq = ROPE_THETA ** (-jnp.arange(0, ROT, 2, dtype=F32) / ROT)
    ang = positions.astype(F32).reshape(-1, 1) * inv_freq
    cos, sin = jnp.cos(ang), jnp.sin(ang)
    s = ang.shape[0]
    pad = jnp.zeros((s, HEAD_DIM - ROT), F32)
    zero = jnp.zeros((s, half), F32)
    two = lambda t: jnp.concatenate([t, t], axis=1)
    return (two(jnp.concatenate([cos, cos, pad + 1.0], axis=1)), two(jnp.concatenate([-sin, zero, pad], axis=1)),
            two(jnp.concatenate([zero, sin, pad], axis=1)))


def _local_step(x, tgt, tabs, mod, sinks_pad, hl, hg_norm, g_pre_mix, g_post_mix, g_pre_ffn, g_post_ffn, weights, scatter, scatter_on):
    s = x.shape[0]
    h1 = _pre_fwd(x, g_pre_mix, mod, 1, 0, "pre_mix_fwd")
    (w_in_t,) = weights("in", h1)
    proj = _mm_nt(h1, w_in_t, s, 256, D, F32, "proj_mm")
    att = _attn_fwd(proj, tabs, sinks_pad)
    o_raw, states = _hgrn_fwd(proj, hl)
    ohg = _hgout_fwd(o_raw, proj, hg_norm)
    w_attn_dm, w_hgrn_dm, w_out = weights("mix", ohg)
    y_a = _mm_nn_dm(att, w_attn_dm, s, F32, "attn_proj_mm")
    y_h = _mm_nn_dm(ohg, w_hgrn_dm, s, F32, "hgrn_proj_mm")
    merged = _merge_fwd(y_a, y_h, proj)
    y = _mm_nn(merged, w_out, s, 512, D, F32, "out_mm")
    x1 = _post_fwd(x, y, g_post_mix, mod, 2, "post_mix_fwd")
    h2 = _pre_fwd(x1, g_pre_ffn, mod, 4, 3, "pre_ffn_fwd")
    w_ffn_in_dm, w_ffn_out = weights("ffn", h2)
    gu = _mm_nn_dm(h2, w_ffn_in_dm, s // 2, F32, "ffn_in_mm")
    act = _swiglu_fwd(gu)
    y2 = _mm_nn(act, w_ffn_out, s, 512, FFN // 4, F32, "ffn_out_mm")
    err, loss = _post_fwd_loss(x1, y2, g_post_ffn, mod, 5, tgt, "post_ffn_loss")
    dy2, d_gate2, dg_post_ffn = _post_bwd(err, y2, g_post_ffn, mod, 5, "post_ffn_bwd")
    gw_ffn_out = _mm_tn(act, dy2, 512, D, BF16, "ffn_out_dw")
    t_pair = scatter([gw_ffn_out.reshape(N_DEV, FFN // N_DEV, D)], "ffn_out")
    d_act = _mm_nt(dy2, w_ffn_out, s, 512, D, F32, "ffn_out_dx", deps=[t_pair])
    dgu = _swiglu_bwd(d_act, gu)
    t_out = scatter_on("ffn_out", dgu)
    gw_ffn_in = _mm_tn_dm(h2, dgu, 1024, BF16, "ffn_in_dw")
    t_pair = scatter([gw_ffn_in], "ffn_in")
    dh2 = _mm_nt_dm(dgu, w_ffn_in_dm, s, 1024, F32, "ffn_in_dx", deps=[t_pair])
    mod = _after(mod, t_out)
    dx1, d_shift2, d_scale2, dg_pre_ffn = _pre_bwd(dh2, x1, err, g_pre_ffn, mod, 4, "pre_ffn_bwd")
    dy, d_gate1, dg_post_mix = _post_bwd(dx1, y, g_post_mix, mod, 2, "post_mix_bwd")
    t_in = scatter_on("ffn_in", dy)
    d_merged = _mm_nt(dy, w_out, s, 512, D, F32, "out_dx")
    gw_out = _mm_tn(merged, dy, 512, D, BF16, "out_dw")
    dy_a, dy_h, d_gate_a, d_gate_h = _merge_bwd(d_merged, y_a, y_h, proj)
    gw_attn = _mm_tn_dm(att, dy_a, ATT_W, BF16, "attn_proj_dw")
    gw_hgrn = _mm_tn_dm(ohg, dy_h, HG_W, BF16, "hgrn_proj_dw")
    t_pair = scatter([gw_attn, gw_hgrn, gw_out.reshape(N_DEV, D // N_DEV, D)], "mix")
    d_att = _mm_nt_dm(dy_a, w_attn_dm, s, ATT_W, F32, "attn_proj_dx", deps=[t_pair])
    d_ohg = _mm_nt_dm(dy_h, w_hgrn_dm, s, HG_W, F32, "hgrn_proj_dx")
    d_o, d_gh, d_hg_norm = _hgout_bwd(d_ohg, o_raw, proj, _after(hg_norm, t_in))
    d_qh, d_fh, d_ih, d_hl = _hgrn_bwd(proj, hl, states, d_o)
    t_mix = scatter_on("mix", d_qh)
    d_qa, d_ka, d_va, d_sinks = _attn_bwd(proj, tabs, _after(sinks_pad, t_mix), d_att)
    d_proj = jnp.concatenate([d_qa, d_ka.astype(BF16), d_va.astype(BF16), d_qh, d_fh, d_ih, d_gh, d_gate_a, d_gate_h], axis=1)
    dh1 = _mm_nn(d_proj, w_in_t, s // 2, 512, IN_COLS // 2, F32, "proj_dx")
    grad_x, d_shift1, d_scale1, dg_pre_mix = _pre_bwd(dh1, x, dx1, g_pre_mix, mod, 1, "pre_mix_bwd")
    d_mod = jnp.concatenate([d_shift1, d_scale1, d_gate1, d_shift2, d_scale2, d_gate2], axis=1)
    small = [d_mod, dg_pre_mix, dg_post_mix, dg_pre_ffn, dg_post_ffn, d_hl.reshape(1, 2 * HG_W), d_hg_norm, d_sinks]
    return loss, grad_x, small, h1, d_proj


def kernel(x, c, positions, w_ada, b_ada, g_pre_mix, g_post_mix, g_pre_ffn, g_post_ffn, w_in, attn_sinks, w_attn_proj, hg_lower_bounds, hg_norm, w_hgrn_proj, w_out, w_ffn_in, w_ffn_out, loss_target, m_w_ada, m_b_ada, m_g_pre_mix, m_g_post_mix, m_g_pre_ffn, m_g_post_ffn, m_w_in, m_attn_sinks, m_w_attn_proj, m_hg_lower_bounds, m_hg_norm, m_w_hgrn_proj, m_w_out, m_w_ffn_in, m_w_ffn_out, v_w_ada, v_b_ada, v_g_pre_mix, v_g_post_mix, v_g_pre_ffn, v_g_post_ffn, v_w_in, v_attn_sinks, v_w_attn_proj, v_hg_lower_bounds, v_hg_norm, v_w_hgrn_proj, v_w_out, v_w_ffn_in, v_w_ffn_out):
    my_id = _lin(_my_coords())
    s = x.shape[1]
    n_ada = w_ada.shape[2]

    c_all = _exchange_small(c.reshape(1, 1, D), True, "gather_c").reshape(N_DEV, D)
    b_cols = lax.dynamic_slice(b_ada, (0, my_id * n_ada), (1, n_ada))
    mod_part = _mod_part(c_all, w_ada[0], b_cols)
    mod = _exchange_small(mod_part.reshape(N_DEV, 1, n_ada), False, "scatter_mod").reshape(1, N_MOD * D)
    groups = {"in": [w_in[0].T], "mix": [w_attn_proj[0], w_hgrn_proj[0], w_out[0]], "ffn": [w_ffn_in[0], w_ffn_out[0]]}

    def start(group, dep):
        shards, dep = lax.optimization_barrier((groups[group], dep))
        return _gather_start([t.astype(BF16) for t in shards], "gather_start_" + group, deps=[dep])

    gathers = {"in": start("in", mod)}
    gathers["mix"] = start("mix", gathers["in"]["token"])
    gathers["ffn"] = start("ffn", gathers["mix"]["token"])

    def weights(group, after):
        after = [after, gathers["ffn"]["token"]]
        if group == "in":
            after += [w_in[0], m_w_in[0], v_w_in[0]]
        lands = _gather_pass(_gather_wait(gathers[group], after, "gather_wait_" + group), "gather_pass_" + group)
        if group == "in":
            return (lands[0].reshape(IN_COLS, D),)
        if group == "mix":
            return lands[0], lands[1], lands[2].reshape(D, D)
        return lands[0], lands[1].reshape(FFN, D)

    pairs, scatters = {}, {}

    def scatter(grads, group):
        pairs[group] = _pair_start(grads, "scatter_pair_" + group)
        return pairs[group]["token"]

    def scatter_on(group, after):
        if group in pairs:
            local, theirs = _pair_wait(pairs[group], [after], "scatter_pair_wait_" + group)
        else:
            local, theirs = after, _pair_exchange(after, "scatter_pair_" + group)
        parts = [_pair_add(g, t, "scatter_pair_add_%s_%d" % (group, k)) for k, (g, t) in enumerate(zip(local, theirs))]
        scatters[group] = _chips_start(parts, "scatter_start_" + group)
        return scatters[group]["token"]

    sinks_pad = jnp.pad(attn_sinks, ((0, 0), (0, LANE - ATT_HEADS)))
    loss, grad_x, small, h1, d_proj = _local_step(
        x[0], loss_target[0], _rope_tables(positions), mod, sinks_pad, hg_lower_bounds, hg_norm, g_pre_mix, g_post_mix, g_pre_ffn, g_post_ffn,
        weights, scatter, scatter_on)
    loss = lax.psum(loss[0, 0], ("x", "y", "c"))

    sizes = [t.shape[1] for t in small]
    parts = _exchange_small(jnp.concatenate(small, axis=1).reshape(1, 1, sum(sizes)), True, "gather_small_grads")
    gw_in = _mm_tn(d_proj, h1, 256, D, BF16, "proj_dw", deps=[parts]).reshape(N_DEV, IN_COLS // N_DEV, D)
    scatter_on("in", [gw_in])
    offs = [sum(sizes[:k]) for k in range(len(sizes))]
    piece = lambda k, n=None: parts[:, :, offs[k]:offs[k] + (sizes[k] if n is None else n)]
    small_w = [(piece(0), b_ada, m_b_ada, v_b_ada), (piece(1), g_pre_mix, m_g_pre_mix, v_g_pre_mix),
               (piece(2), g_post_mix, m_g_post_mix, v_g_post_mix), (piece(3), g_pre_ffn, m_g_pre_ffn, v_g_pre_ffn),
               (piece(4), g_post_ffn, m_g_post_ffn, v_g_post_ffn),
               (piece(5).reshape(N_DEV, 2, HG_W), hg_lower_bounds, m_hg_lower_bounds, v_hg_lower_bounds),
               (piece(6), hg_norm, m_hg_norm, v_hg_norm), (piece(7, ATT_HEADS), attn_sinks, m_attn_sinks, v_attn_sinks)]
    names = ["b_ada", "g_pre_mix", "g_post_mix", "g_pre_ffn", "g_post_ffn", "hg_lower_bounds", "hg_norm", "attn_sinks"]
    res = {n: _adamw(p, w, m, v, "adamw_" + n) for n, (p, w, m, v) in zip(names, small_w)}

    dmod_cols = lax.dynamic_slice(parts.reshape(N_DEV, -1), (0, my_id * n_ada), (N_DEV, n_ada))
    g_w_ada = _grad_w_ada(c_all.T, dmod_cols)
    res["w_ada"] = [g_w_ada] + list(_adamw(g_w_ada[None], w_ada[0], m_w_ada[0], v_w_ada[0], "adamw_w_ada", emit_grad=False))

    big = {"ffn_out": [("w_ffn_out", w_ffn_out, m_w_ffn_out, v_w_ffn_out)], "ffn_in": [("w_ffn_in", w_ffn_in, m_w_ffn_in, v_w_ffn_in)],
           "mix": [("w_attn_proj", w_attn_proj, m_w_attn_proj, v_w_attn_proj), ("w_hgrn_proj", w_hgrn_proj, m_w_hgrn_proj, v_w_hgrn_proj),
                   ("w_out", w_out, m_w_out, v_w_out)],
           "in": [("w_in", w_in, m_w_in, v_w_in)]}
    after = [scatters["in"]["token"]]
    for group, members in big.items():
        if group == "in":
            after = [res[n][1] for n in res]
        local, lands = _chips_wait(scatters[group], after, "scatter_wait_" + group)
        own = [t[0] for t in local]
        for (n, w, m, v), g_own, land in zip(members, own, lands):
            if group == "in":
                g_w = _sum_pieces(land, g_own, "sum_" + n).T
                res[n] = [g_w] + list(_adamw(g_w[None], w[0], m[0], v[0], "adamw_" + n, emit_grad=False))
            else:
                res[n] = _adamw(land, w[0], m[0], v[0], "adamw_" + n, own=g_own)
            after = after + [res[n][1]]

    order = ["w_ada", "b_ada", "g_pre_mix", "g_post_mix", "g_pre_ffn", "g_post_ffn", "w_in", "attn_sinks", "w_attn_proj",
             "hg_lower_bounds", "hg_norm", "w_hgrn_proj", "w_out", "w_ffn_in", "w_ffn_out"]
    lead = {"w_ada", "w_in", "w_attn_proj", "w_hgrn_proj", "w_out", "w_ffn_in", "w_ffn_out"}
    outs = [loss, grad_x[None]]
    for k in range(4):
        outs += [res[n][k][None] if n in lead else res[n][k] for n in order]
    return tuple(outs)
```

```python
import functools

import jax
import jax.numpy as jnp
from jax import lax
from jax.experimental import pallas as pl
from jax.experimental.pallas import tpu as pltpu

F32 = jnp.float32
BF16 = jnp.bfloat16

N_DEV = 8
D = 2048
ATT_HEADS = 16
KV_HEADS = 2
HEAD_DIM = 64
GROUP = ATT_HEADS // KV_HEADS
ATT_W = ATT_HEADS * HEAD_DIM
BLK = 128
ROT = HEAD_DIM // 4
ROPE_THETA = 500000.0
HG_HEADS = 8
HG_K = 128
HG_W = HG_HEADS * HG_K
CHUNK = 64
SUB = 16
FFN = 5632
N_MOD = 6
EPS = 1e-6
LANE = 128
Q_A, K_A, V_A, Q_H, F_H, I_H, G_H, GT_A, GT_H, IN_COLS = 0, 1024, 1152, 1280, 2304, 3328, 4352, 5376, 7424, 9472

ADAM_LR, ADAM_B1, ADAM_B2, ADAM_EPS, ADAM_WD, ADAM_STEP = 0.001, 0.9, 0.999, 1e-08, 0.01, 10

TR = 256
HG_TB = 512
VMEM_BIG = 56 << 20
MESH = pl.DeviceIdType.MESH


def _sds(shape, dtype):
    return jax.ShapeDtypeStruct(shape, dtype)


def _params(n_axes, vmem=None):
    return pltpu.CompilerParams(dimension_semantics=("arbitrary",) * n_axes, vmem_limit_bytes=vmem)


def _sig(t):
    return 1.0 / (1.0 + jnp.exp(-t))


def _dot(a, b, dims):
    return lax.dot_general(a, b, (dims, ((), ())), preferred_element_type=F32)


NN = ((1,), (0,))
NT = ((1,), (1,))
TN = ((0,), (0,))


def _matmul(a, b, a_spec, b_spec, o_spec, out_shape, grid, dims, acc_shape, name, deps=()):
    nk = grid[2]
    nd = len(deps)

    def body(a_ref, b_ref, *rest):
        o_ref, scratch = rest[nd], rest[nd + 1:]
        part = _dot(a_ref[...], b_ref[...], dims)
        if nk == 1:
            o_ref[...] = part.astype(o_ref.dtype)
        else:
            acc = scratch[0]
            k = pl.program_id(2)

            @pl.when(k == 0)
            def _():
                acc[...] = part

            @pl.when(k > 0)
            def _():
                acc[...] += part

            @pl.when(k == nk - 1)
            def _():
                o_ref[...] = acc[...].astype(o_ref.dtype)

    return pl.pallas_call(
        body, grid=grid, in_specs=[a_spec, b_spec] + [pl.BlockSpec(memory_space=pl.ANY)] * nd, out_specs=o_spec, out_shape=out_shape,
        scratch_shapes=[pltpu.VMEM(acc_shape, F32)] if nk > 1 else [],
        compiler_params=_params(3, VMEM_BIG), name=name)(a, b, *deps)


def _mm_nn(a, b, tm, tn, tk, out_dtype, name):
    m, k = a.shape
    n = b.shape[1]
    return _matmul(a, b, pl.BlockSpec((tm, tk), lambda j, i, kk: (i, kk)), pl.BlockSpec((tk, tn), lambda j, i, kk: (kk, j)),
                   pl.BlockSpec((tm, tn), lambda j, i, kk: (i, j)), _sds((m, n), out_dtype),
                   (n // tn, m // tm, k // tk), NN, (tm, tn), name)


def _mm_nn_dm(a, b, tm, out_dtype, name):
    m, k = a.shape
    n = b.shape[2]
    return _matmul(a, b, pl.BlockSpec((tm, k), lambda j, i, kk: (i, 0)), pl.BlockSpec((None, k, n), lambda j, i, kk: (j, 0, 0)),
                   pl.BlockSpec((tm, n), lambda j, i, kk: (i, j)), _sds((m, N_DEV * n), out_dtype),
                   (N_DEV, m // tm, 1), NN, (tm, n), name)


def _mm_nt(a, b, tm, tn, tk, out_dtype, name, deps=()):
    m, k = a.shape
    n = b.shape[0]
    return _matmul(a, b, pl.BlockSpec((tm, tk), lambda j, i, kk: (i, kk)), pl.BlockSpec((tn, tk), lambda j, i, kk: (j, kk)),
                   pl.BlockSpec((tm, tn), lambda j, i, kk: (i, j)), _sds((m, n), out_dtype),
                   (n // tn, m // tm, k // tk), NT, (tm, tn), name, deps)


def _mm_nt_dm(a, b, tm, tn, out_dtype, name, deps=()):
    m = a.shape[0]
    n_out, n = b.shape[1], b.shape[2]
    return _matmul(a, b, pl.BlockSpec((tm, n), lambda j, i, kk: (i, kk)), pl.BlockSpec((None, tn, n), lambda j, i, kk: (kk, j, 0)),
                   pl.BlockSpec((tm, tn), lambda j, i, kk: (i, j)), _sds((m, n_out), out_dtype),
                   (n_out // tn, m // tm, N_DEV), NT, (tm, tn), name, deps)


def _mm_tn(a, b, tm, tn, out_dtype, name, deps=()):
    s, m = a.shape
    n = b.shape[1]
    return _matmul(a, b, pl.BlockSpec((s, tm), lambda j, i, kk: (0, i)), pl.BlockSpec((s, tn), lambda j, i, kk: (0, j)),
                   pl.BlockSpec((tm, tn), lambda j, i, kk: (i, j)), _sds((m, n), out_dtype),
                   (n // tn, m // tm, 1), TN, (tm, tn), name, deps)


def _mm_tn_dm(a, b, tm, out_dtype, name):
    s, m = a.shape
    n = b.shape[1] // N_DEV
    return _matmul(a, b, pl.BlockSpec((s, tm), lambda j, i, kk: (0, i)), pl.BlockSpec((s, n), lambda j, i, kk: (0, j)),
                   pl.BlockSpec((None, tm, n), lambda j, i, kk: (j, i, 0)), _sds((N_DEV, m, n), out_dtype),
                   (N_DEV, m // tm, 1), TN, (tm, n), name)


def _row_spec():
    return pl.BlockSpec((TR, D), lambda i: (i, 0))


def _vec_spec(k=0):
    return pl.BlockSpec((1, D), lambda i: (0, k))


def _acc_rows(ref, first, val):
    @pl.when(first)
    def _():
        ref[...] = val

    @pl.when(jnp.logical_not(first))
    def _():
        ref[...] += val


def _pre_fwd(x, g, mod, k_scale, k_shift, name):
    s = x.shape[0]

    def body(x_ref, g_ref, sc_ref, sh_ref, h_ref):
        xv = x_ref[...]
        r = lax.rsqrt(jnp.mean(xv * xv, axis=-1, keepdims=True) + EPS)
        n = xv * r * g_ref[...]
        h_ref[...] = (n * (1.0 + sc_ref[...]) + sh_ref[...]).astype(h_ref.dtype)

    return pl.pallas_call(body, grid=(s // TR,), in_specs=[_row_spec(), _vec_spec(), _vec_spec(k_scale), _vec_spec(k_shift)],
                          out_specs=_row_spec(), out_shape=_sds((s, D), BF16), compiler_params=_params(1), name=name)(x, g, mod, mod)


def _post_fwd(x, y, g, mod, k_gate, name):
    s = x.shape[0]

    def body(x_ref, y_ref, g_ref, gt_ref, o_ref):
        yv = y_ref[...]
        r = lax.rsqrt(jnp.mean(yv * yv, axis=-1, keepdims=True) + EPS)
        o_ref[...] = x_ref[...] + gt_ref[...] * (yv * r * g_ref[...])

    return pl.pallas_call(body, grid=(s // TR,), in_specs=[_row_spec(), _row_spec(), _vec_spec(), _vec_spec(k_gate)],
                          out_specs=_row_spec(), out_shape=_sds((s, D), F32), compiler_params=_params(1), name=name)(x, y, g, mod)


def _post_fwd_loss(x, y, g, mod, k_gate, tgt, name):
    s = x.shape[0]

    def body(x_ref, y_ref, g_ref, gt_ref, t_ref, e_ref, loss_ref):
        i = pl.program_id(0)
        yv = y_ref[...]
        r = lax.rsqrt(jnp.mean(yv * yv, axis=-1, keepdims=True) + EPS)
        err = x_ref[...] + gt_ref[...] * (yv * r * g_ref[...]) - t_ref[...]
        e_ref[...] = err * (1.0 / D)
        part = 0.5 * jnp.sum(jnp.mean(err * err, axis=-1, keepdims=True), axis=0, keepdims=True)
        _acc_rows(loss_ref, i == 0, part)

    return pl.pallas_call(body, grid=(s // TR,),
                          in_specs=[_row_spec(), _row_spec(), _vec_spec(), _vec_spec(k_gate), _row_spec()],
                          out_specs=[_row_spec(), pl.BlockSpec((1, 1), lambda i: (0, 0))],
                          out_shape=[_sds((s, D), F32), _sds((1, 1), F32)], compiler_params=_params(1), name=name)(x, y, g, mod, tgt)


def _pre_bwd(dh, x, res, g, mod, k_scale, name):
    s = x.shape[0]

    def body(dh_ref, x_ref, res_ref, g_ref, sc_ref, dx_ref, dsh_ref, dsc_ref, dg_ref):
        first = pl.program_id(0) == 0
        xv, dh_v, gv = x_ref[...], dh_ref[...], g_ref[...]
        r = lax.rsqrt(jnp.mean(xv * xv, axis=-1, keepdims=True) + EPS)
        xh = xv * r
        dn = dh_v * (1.0 + sc_ref[...])
        dgn = dn * gv
        dx_ref[...] = res_ref[...] + r * (dgn - xh * jnp.mean(dgn * xh, axis=-1, keepdims=True))
        _acc_rows(dsh_ref, first, jnp.sum(dh_v, axis=0, keepdims=True))
        _acc_rows(dsc_ref, first, jnp.sum(dh_v * (xh * gv), axis=0, keepdims=True))
        _acc_rows(dg_ref, first, jnp.sum(dn * xh, axis=0, keepdims=True))

    return pl.pallas_call(body, grid=(s // TR,),
                          in_specs=[_row_spec(), _row_spec(), _row_spec(), _vec_spec(), _vec_spec(k_scale)],
                          out_specs=[_row_spec(), _vec_spec(), _vec_spec(), _vec_spec()],
                          out_shape=[_sds((s, D), F32)] + [_sds((1, D), F32)] * 3,
                          compiler_params=_params(1), name=name)(dh, x, res, g, mod)


def _post_bwd(dx, y, g, mod, k_gate, name):
    s = y.shape[0]

    def body(dx_ref, y_ref, g_ref, gt_ref, dy_ref, dgt_ref, dg_ref):
        first = pl.program_id(0) == 0
        yv, dxv, gv = y_ref[...], dx_ref[...], g_ref[...]
        r = lax.rsqrt(jnp.mean(yv * yv, axis=-1, keepdims=True) + EPS)
        yh = yv * r
        dn = dxv * gt_ref[...]
        dgn = dn * gv
        dy_ref[...] = (r * (dgn - yh * jnp.mean(dgn * yh, axis=-1, keepdims=True))).astype(dy_ref.dtype)
        _acc_rows(dgt_ref, first, jnp.sum(dxv * (yh * gv), axis=0, keepdims=True))
        _acc_rows(dg_ref, first, jnp.sum(dn * yh, axis=0, keepdims=True))

    return pl.pallas_call(body, grid=(s // TR,), in_specs=[_row_spec(), _row_spec(), _vec_spec(), _vec_spec(k_gate)],
                          out_specs=[_row_spec(), _vec_spec(), _vec_spec()],
                          out_shape=[_sds((s, D), BF16), _sds((1, D), F32), _sds((1, D), F32)],
                          compiler_params=_params(1), name=name)(dx, y, g, mod)


SW_TN = 1408
SW_TR = 512
TALL = 1024


def _swiglu_fwd(gu):
    s = gu.shape[0]
    nb = FFN // SW_TN

    def body(g_ref, u_ref, a_ref):
        gv = g_ref[...]
        a_ref[...] = (gv * _sig(gv) * u_ref[...]).astype(a_ref.dtype)

    return pl.pallas_call(body, grid=(s // SW_TR, nb),
                          in_specs=[pl.BlockSpec((SW_TR, SW_TN), lambda i, j: (i, j)), pl.BlockSpec((SW_TR, SW_TN), lambda i, j: (i, j + nb))],
                          out_specs=pl.BlockSpec((SW_TR, SW_TN), lambda i, j: (i, j)), out_shape=_sds((s, FFN), BF16),
                          compiler_params=_params(2, 48 << 20), name="swiglu_fwd")(gu, gu)


def _swiglu_bwd(dact, gu):
    s = gu.shape[0]
    nb = FFN // SW_TN
    n_steps = (s // SW_TR) * nb

    def body(da_ref, g_ref, u_ref, o_ref, buf, sems):
        i, j = pl.program_id(0), pl.program_id(1)
        step = i * nb + j
        slot = step % 2

        def tiles(sl):
            rows = pl.ds(pl.multiple_of(i * SW_TR, SW_TR), SW_TR)
            return [pltpu.make_async_copy(buf.at[sl, h], o_ref.at[rows, pl.ds(pl.multiple_of((j + nb * h) * SW_TN, LANE), SW_TN)], sems.at[sl, h])
                    for h in range(2)]

        @pl.when(step >= 2)
        def _():
            for cp in tiles(slot):
                cp.wait()

        gv, da = g_ref[...], da_ref[...]
        sg = _sig(gv)
        buf[slot, 0] = (da * u_ref[...] * (sg * (1.0 + gv * (1.0 - sg)))).astype(buf.dtype)
        buf[slot, 1] = (da * (gv * sg)).astype(buf.dtype)
        for cp in tiles(slot):
            cp.start()

        @pl.when(step == n_steps - 1)
        def _():
            for cp in tiles(slot) + (tiles(1 - slot) if n_steps > 1 else []):
                cp.wait()

    blk = lambda f: pl.BlockSpec((SW_TR, SW_TN), f)
    return pl.pallas_call(body, grid=(s // SW_TR, nb),
                          in_specs=[blk(lambda i, j: (i, j)), blk(lambda i, j: (i, j)), blk(lambda i, j: (i, j + nb))],
                          out_specs=pl.BlockSpec(memory_space=pl.ANY), out_shape=_sds((s, 2 * FFN), BF16),
                          scratch_shapes=[pltpu.VMEM((2, 2, SW_TR, SW_TN), BF16), pltpu.SemaphoreType.DMA((2, 2))],
                          compiler_params=_params(2, 48 << 20), name="swiglu_bwd")(dact, gu, gu)


MG_TN = 256


def _merge_fwd(y_a, y_h, proj):
    s = y_a.shape[0]
    tn = MG_TN
    ba, bh = GT_A // tn, GT_H // tn

    def body(ya_ref, yh_ref, ga_ref, gh_ref, m_ref):
        m_ref[...] = (_sig(ga_ref[...]) * ya_ref[...] + _sig(gh_ref[...]) * yh_ref[...]).astype(m_ref.dtype)

    tr = min(s, TALL)
    blk = lambda f: pl.BlockSpec((tr, tn), f)
    return pl.pallas_call(body, grid=(s // tr, D // tn),
                          in_specs=[blk(lambda i, j: (i, j)), blk(lambda i, j: (i, j)), blk(lambda i, j: (i, j + ba)), blk(lambda i, j: (i, j + bh))],
                          out_specs=blk(lambda i, j: (i, j)), out_shape=_sds((s, D), BF16),
                          compiler_params=_params(2), name="merge_fwd")(y_a, y_h, proj, proj)


def _merge_bwd(dm, y_a, y_h, proj):
    s = y_a.shape[0]
    tn = MG_TN
    ba, bh = GT_A // tn, GT_H // tn

    def body(dm_ref, ya_ref, yh_ref, ga_ref, gh_ref, dya_ref, dyh_ref, dga_ref, dgh_ref):
        dmv = dm_ref[...]
        sa, sh = _sig(ga_ref[...]), _sig(gh_ref[...])
        dya_ref[...] = (dmv * sa).astype(BF16)
        dyh_ref[...] = (dmv * sh).astype(BF16)
        dga_ref[...] = (dmv * ya_ref[...] * (sa * (1.0 - sa))).astype(BF16)
        dgh_ref[...] = (dmv * yh_ref[...] * (sh * (1.0 - sh))).astype(BF16)

    tr = min(s, TALL)
    blk = lambda f: pl.BlockSpec((tr, tn), f)
    nat = blk(lambda i, j: (i, j))
    return pl.pallas_call(body, grid=(s // tr, D // tn),
                          in_specs=[nat, nat, nat, blk(lambda i, j: (i, j + ba)), blk(lambda i, j: (i, j + bh))],
                          out_specs=[nat] * 4, out_shape=[_sds((s, D), BF16)] * 4,
                          compiler_params=_params(2), name="merge_bwd")(dm, y_a, y_h, proj, proj)


def _hgout_fwd(o_raw, proj, hg_norm):
    s = o_raw.shape[0]
    bg = G_H // LANE

    def body(o_ref, g_ref, n_ref, out_ref):
        ov = o_ref[...]
        r = lax.rsqrt(jnp.mean(ov * ov, axis=-1, keepdims=True) + EPS)
        out_ref[...] = (ov * r * n_ref[...] * _sig(g_ref[...])).astype(out_ref.dtype)

    tr = min(s, TALL)
    blk = lambda f: pl.BlockSpec((tr, LANE), f)
    return pl.pallas_call(body, grid=(s // tr, HG_HEADS),
                          in_specs=[blk(lambda i, h: (i, h)), blk(lambda i, h: (i, h + bg)), pl.BlockSpec((1, LANE), lambda i, h: (0, 0))],
                          out_specs=blk(lambda i, h: (i, h)), out_shape=_sds((s, HG_W), BF16),
                          compiler_params=_params(2), name="hgout_fwd")(o_raw, proj, hg_norm)


def _hgout_bwd(d_out, o_raw, proj, hg_norm):
    s = o_raw.shape[0]
    bg = G_H // LANE

    def body(d_ref, o_ref, g_ref, n_ref, do_ref, dg_ref, dn_ref):
        first = jnp.logical_and(pl.program_id(0) == 0, pl.program_id(1) == 0)
        ov, dv, nv = o_ref[...], d_ref[...], n_ref[...]
        sg = _sig(g_ref[...])
        r = lax.rsqrt(jnp.mean(ov * ov, axis=-1, keepdims=True) + EPS)
        oh = ov * r
        d_on = dv * sg
        dg_ref[...] = (dv * (oh * nv) * (sg * (1.0 - sg))).astype(dg_ref.dtype)
        t = d_on * nv
        do_ref[...] = r * (t - oh * jnp.mean(t * oh, axis=-1, keepdims=True))
        _acc_rows(dn_ref, first, jnp.sum(d_on * oh, axis=0, keepdims=True))

    tr = min(s, TALL)
    blk = lambda f: pl.BlockSpec((tr, LANE), f)
    vec = pl.BlockSpec((1, LANE), lambda i, h: (0, 0))
    return pl.pallas_call(body, grid=(s // tr, HG_HEADS),
                          in_specs=[blk(lambda i, h: (i, h)), blk(lambda i, h: (i, h)), blk(lambda i, h: (i, h + bg)), vec],
                          out_specs=[blk(lambda i, h: (i, h)), blk(lambda i, h: (i, h)), vec],
                          out_shape=[_sds((s, HG_W), F32), _sds((s, HG_W), BF16), _sds((1, LANE), F32)],
                          compiler_params=_params(2), name="hgout_bwd")(d_out, o_raw, proj, hg_norm)


def _rope(t, cos, s_lo, s_hi):
    return t * cos + pltpu.roll(t, LANE - ROT // 2, 1) * s_lo + pltpu.roll(t, ROT // 2, 1) * s_hi


def _rope_wide(t, cos, s_lo, s_hi):
    return jnp.concatenate([_rope(t[:, k * LANE:(k + 1) * LANE], cos, s_lo, s_hi) for k in range(t.shape[1] // LANE)], axis=1)


def _attn_mask(has_prev):
    qi = lax.broadcasted_iota(jnp.int32, (BLK, 2 * BLK), 0)
    kj = lax.broadcasted_iota(jnp.int32, (BLK, 2 * BLK), 1)
    rel = BLK + qi - kj
    band = jnp.logical_and(rel >= 0, rel < BLK)
    return jnp.logical_and(band, jnp.logical_or(has_prev, kj >= BLK))


def _attn_specs():
    prev = lambda i: jnp.maximum(i - 1, 0)
    kb, vb = K_A // LANE, V_A // LANE
    blk = lambda f: pl.BlockSpec((BLK, LANE), f)
    tabs = [blk(lambda i: (i, 0))] * 3 + [blk(lambda i: (prev(i), 0))] * 3
    return [pl.BlockSpec((BLK, ATT_W), lambda i: (i, 0)), blk(lambda i: (i, kb)), blk(lambda i: (prev(i), kb)),
            blk(lambda i: (i, vb)), blk(lambda i: (prev(i), vb))] + tabs + [pl.BlockSpec((1, LANE), lambda i: (0, 0))]


def _attn_logits(qh, kg):
    return _dot(qh, kg, NT)


def _attn_probs(raw, mask, sk):
    logits = jnp.where(mask, raw * (HEAD_DIM ** -0.5), -jnp.inf)
    m = jnp.maximum(jnp.max(logits, axis=-1, keepdims=True), sk)
    p = jnp.exp(logits - m)
    e_sink = jnp.exp(sk - m)
    inv = 1.0 / (jnp.sum(p, axis=-1, keepdims=True) + e_sink)
    return p * inv, e_sink * inv


def _attn_fwd(proj, tabs, sinks):
    s = proj.shape[0]

    def body(q_ref, kc_ref, kp_ref, vc_ref, vp_ref, c0, l0, h0, c1, l1, h1, sk_ref, o_ref):
        i = pl.program_id(0)
        mask = _attn_mask(i > 0)
        q = _rope_wide(q_ref[...], c0[...], l0[...], h0[...]).astype(BF16)
        kk = jnp.concatenate([_rope(kp_ref[...], c1[...], l1[...], h1[...]), _rope(kc_ref[...], c0[...], l0[...], h0[...])], axis=0).astype(BF16)
        vv = jnp.concatenate([vp_ref[...], vc_ref[...]], axis=0).astype(BF16)
        part = lambda t, h: t[:, h * HEAD_DIM:(h + 1) * HEAD_DIM]

        def head(h):
            raw = _attn_logits(part(q, h), part(kk, h // GROUP))
            yield
            sk = sk_ref[:, h:h + 1]
            logits = jnp.where(mask, raw * (HEAD_DIM ** -0.5), -jnp.inf)
            m = jnp.maximum(jnp.max(logits, axis=-1, keepdims=True), sk)
            yield
            p = jnp.exp(logits - m)
            den = jnp.sum(p, axis=-1, keepdims=True) + jnp.exp(sk - m)
            yield
            out = _dot((p * (1.0 / den)).astype(BF16), part(vv, h // GROUP), NN)
            yield
            return out

        o_ref[...] = jnp.concatenate(_interleave([head(h) for h in range(ATT_HEADS)]), axis=1).astype(o_ref.dtype)

    return pl.pallas_call(body, grid=(s // BLK,), in_specs=_attn_specs(),
                          out_specs=pl.BlockSpec((BLK, ATT_W), lambda i: (i, 0)), out_shape=_sds((s, ATT_W), BF16),
                          compiler_params=_params(1), name="attn_fwd")(proj, proj, proj, proj, proj, *tabs, *tabs, sinks)


def _attn_bwd(proj, tabs, sinks, d_att):
    s = proj.shape[0]

    def body(q_ref, kc_ref, kp_ref, vc_ref, vp_ref, c0, l0, h0, c1, l1, h1, sk_ref, do_ref, dq_ref, dk_ref, dv_ref, ds_ref):
        i = pl.program_id(0)

        @pl.when(i == 0)
        def _():
            dk_ref[...] = jnp.zeros_like(dk_ref)
            dv_ref[...] = jnp.zeros_like(dv_ref)
            ds_ref[...] = jnp.zeros_like(ds_ref)

        mask = _attn_mask(i > 0)
        q = _rope_wide(q_ref[...], c0[...], l0[...], h0[...]).astype(BF16)
        kk = jnp.concatenate([_rope(kp_ref[...], c1[...], l1[...], h1[...]), _rope(kc_ref[...], c0[...], l0[...], h0[...])], axis=0).astype(BF16)
        vv = jnp.concatenate([vp_ref[...], vc_ref[...]], axis=0).astype(BF16)
        d_o = do_ref[...].astype(BF16)
        lane = lax.broadcasted_iota(jnp.int32, (1, LANE), 1)
        part = lambda t, h: t[:, h * HEAD_DIM:(h + 1) * HEAD_DIM]

        def head(h):
            kg, vg = part(kk, h // GROUP), part(vv, h // GROUP)
            qh, doh = part(q, h), part(d_o, h)
            raw = _attn_logits(qh, kg)
            d_p = _dot(doh, vg, NT)
            yield
            prob, p_sink = _attn_probs(raw, mask, sk_ref[:, h:h + 1])
            yield
            dd = jnp.sum(prob * d_p, axis=-1, keepdims=True)
            yield
            d_s = (prob * (d_p - dd)).astype(BF16)
            d_sink = jnp.where(lane == h, -jnp.sum(p_sink * dd, axis=0, keepdims=True), 0.0)
            dq = _dot(d_s, kg, NN)
            dk = _dot(d_s, qh, TN)
            dv = _dot(prob.astype(BF16), doh, TN)
            yield
            return dq * (HEAD_DIM ** -0.5), dk * (HEAD_DIM ** -0.5), dv, d_sink

        per_head = _interleave([head(h) for h in range(ATT_HEADS)])
        dqs = [t[0] for t in per_head]
        group_sum = lambda k, g: functools.reduce(jnp.add, [t[k] for t in per_head[g * GROUP:(g + 1) * GROUP]])
        dks = [group_sum(1, g) for g in range(KV_HEADS)]
        dvs = [group_sum(2, g) for g in range(KV_HEADS)]
        d_sink = functools.reduce(jnp.add, [t[3] for t in per_head])
        dq_ref[...] = _rope_wide(jnp.concatenate(dqs, axis=1), c0[...], -l0[...], -h0[...]).astype(dq_ref.dtype)
        d_k = jnp.concatenate(dks, axis=1)
        d_v = jnp.concatenate(dvs, axis=1)
        cur = pl.ds(pl.multiple_of(i * BLK, BLK), BLK)
        prv = pl.ds(pl.multiple_of(jnp.maximum(i - 1, 0) * BLK, BLK), BLK)
        dk_ref[prv, :] += _rope(d_k[:BLK], c1[...], -l1[...], -h1[...])
        dk_ref[cur, :] += _rope(d_k[BLK:], c0[...], -l0[...], -h0[...])
        dv_ref[prv, :] += d_v[:BLK]
        dv_ref[cur, :] += d_v[BLK:]
        ds_ref[...] += d_sink

    full = pl.BlockSpec((s, LANE), lambda i: (0, 0))
    return pl.pallas_call(body, grid=(s // BLK,), in_specs=_attn_specs() + [pl.BlockSpec((BLK, ATT_W), lambda i: (i, 0))],
                          out_specs=[pl.BlockSpec((BLK, ATT_W), lambda i: (i, 0)), full, full, pl.BlockSpec((1, LANE), lambda i: (0, 0))],
                          out_shape=[_sds((s, ATT_W), BF16), _sds((s, LANE), F32), _sds((s, LANE), F32), _sds((1, LANE), F32)],
                          compiler_params=_params(1), name="attn_bwd")(proj, proj, proj, proj, proj, *tabs, *tabs, sinks, d_att)


def _tri_matmul(tri, t):
    hi = t.astype(BF16)
    r1 = t - hi.astype(F32)
    mid = r1.astype(BF16)
    lo = (r1 - mid.astype(F32)).astype(BF16)
    return _dot(tri, hi, NN) + _dot(tri, mid, NN) + _dot(tri, lo, NN)


def _lower_bound(hl):
    a, b = hl[0:1, :], hl[1:2, :]
    mx = jnp.maximum(a, b)
    ea, eb = jnp.exp(a - mx), jnp.exp(b - mx)
    return ea / (ea + eb)


def _hg_gates(q_raw, f_raw, lb, tri_lower):
    sg = _sig(f_raw)
    f = lb + (1.0 - lb) * sg
    sq = _sig(q_raw)
    b = _tri_matmul(tri_lower, jnp.log(f))
    return sg, f, 1.0 - f, sq, q_raw * sq, b


HG_PAIR_FWD = 8
HG_PAIR_BWD = 8


def _hg_specs(n_map, pair):
    blk = lambda off, p: pl.BlockSpec((HG_TB, LANE), lambda h, n: (n_map(n), off // LANE + pair * h + p))
    return [blk(off, p) for off in (Q_H, F_H, I_H) for p in range(pair)] + [pl.BlockSpec((2, pair * LANE), lambda h, n: (0, h))]


def _interleave(gens):
    out = [None] * len(gens)
    live = list(range(len(gens)))
    while live:
        for k in list(live):
            try:
                next(gens[k])
            except StopIteration as stop:
                out[k] = stop.value
                live.remove(k)
    return out


def _hg_spread():
    c = lax.broadcasted_iota(jnp.int32, (CHUNK, SUB * SUB), 0)
    l = lax.broadcasted_iota(jnp.int32, (CHUNK, SUB * SUB), 1)
    r = lax.broadcasted_iota(jnp.int32, (SUB, SUB * SUB), 0)
    lr = lax.broadcasted_iota(jnp.int32, (SUB, SUB * SUB), 1)
    cols = [(c == lo + (l >> 4)).astype(BF16) for lo in range(0, CHUNK, SUB)]
    tile = [(c == lo + (l & (SUB - 1))).astype(BF16) for lo in range(0, CHUNK, SUB)]
    return cols, tile, (lr & (SUB - 1)) == r, (lr >> 4) == r


def _hg_intra(qs, kk, b, grad=None):
    lane = lax.broadcasted_iota(jnp.int32, (SUB, CHUNK), 1)
    row1 = lax.broadcasted_iota(jnp.int32, (SUB, 1), 0)
    kk_b = kk.astype(BF16)
    if grad is not None:
        d_a, d_at, (cols, tile, diag, block) = grad
    a_blocks, dq_blocks, dk_blocks, db_blocks = [], [], [], []
    dk_left = None
    for j in range(CHUNK // SUB):
        lo = j * SUB
        q_j, k_j, b_j = qs[lo:lo + SUB], kk[lo:lo + SUB], b[lo:lo + SUB]
        es = [jnp.where(row1 >= sx, jnp.exp(jnp.minimum(b_j - b_j[sx:sx + 1], 0.0)), 0.0) for sx in range(SUB)]
        pes = [q_j * e for e in es]
        pe = jnp.concatenate(pes, axis=0).astype(BF16)
        pairs = _dot(pe, kk_b, NT)
        yield
        a_j = jnp.zeros((SUB, CHUNK), F32)
        for sx in range(SUB):
            a_j = jnp.where(lane == lo + sx, pairs[sx * SUB:(sx + 1) * SUB], a_j)
        if grad is not None:
            da_j = d_a[lo:lo + SUB]
            ek = jnp.concatenate([e * k_j[sx:sx + 1] for sx, e in enumerate(es)], axis=0).astype(BF16)
            sel_t = jnp.where(diag, _dot(da_j.astype(BF16), cols[j], NN), 0.0).astype(BF16)
            sel_s = jnp.where(block, _dot(d_at[lo:lo + SUB].astype(BF16), tile[j], NN), 0.0).astype(BF16)
            pek = jnp.concatenate([p * k_j[sx:sx + 1] for sx, p in enumerate(pes)], axis=0).astype(BF16)
            yield
            dq_j = _dot(sel_t, ek, NN)
            dk_j = _dot(sel_s, pe, NN)
            db_j = _dot(sel_t, pek, NN) - _dot(sel_s, pek, NN)
            yield
        if j > 0:
            ref = b[lo - 1:lo]
            sc_q = jnp.exp(b_j - ref)
            sc_k = jnp.exp(jnp.minimum(ref - b, 0.0))
            qt = (q_j * sc_q).astype(BF16)
            kt = (kk * sc_k).astype(BF16)
            left = _dot(qt, kt, NT)
            yield
            a_j = a_j + jnp.where(lane < lo, left, 0.0)
            if grad is not None:
                da_left = jnp.where(lane < lo, da_j, 0.0).astype(BF16)
                dq_left = _dot(da_left, kt, NN) * sc_q
                dq_j = dq_j + dq_left
                db_j = db_j + q_j * dq_left
                t = _dot(da_left, qt, TN)
                yield
                t = t * sc_k
                dk_left = t if dk_left is None else dk_left + t
        a_blocks.append(a_j)
        if grad is not None:
            dq_blocks.append(dq_j)
            dk_blocks.append(dk_j)
            db_blocks.append(db_j)
    a = jnp.concatenate(a_blocks, axis=0)
    if grad is None:
        return a
    return a, jnp.concatenate(dq_blocks, axis=0), jnp.concatenate(dk_blocks, axis=0) + dk_left, jnp.concatenate(db_blocks, axis=0) - kk * dk_left


def _hgrn_fwd(proj, hl):
    s = proj.shape[0]
    n_chunk = HG_TB // CHUNK
    pair = HG_PAIR_FWD

    def body(*refs):
        q_refs, f_refs, i_refs = refs[:pair], refs[pair:2 * pair], refs[2 * pair:3 * pair]
        hl_ref, o_ref, st_out_ref, st_ref = refs[3 * pair:]

        @pl.when(pl.program_id(1) == 0)
        def _():
            st_ref[...] = jnp.zeros_like(st_ref)

        r_i = lax.broadcasted_iota(jnp.int32, (CHUNK, CHUNK), 0)
        c_i = lax.broadcasted_iota(jnp.int32, (CHUNK, CHUNK), 1)
        tri_lower = (r_i >= c_i).astype(BF16)

        def chunk(c, carry):
            rows = pl.ds(pl.multiple_of(c * CHUNK, CHUNK), CHUNK)
            def head(p):
                cols = slice(p * LANE, (p + 1) * LANE)
                lb = _lower_bound(hl_ref[:, cols])
                v = i_refs[p][rows, :].astype(BF16)
                _, _, kk, _, qs, b = _hg_gates(q_refs[p][rows, :], f_refs[p][rows, :], lb, tri_lower)
                yield
                st = st_ref[p]
                st_b = st.astype(BF16)
                st_out_ref[p, c] = st_b
                o_state = _dot((qs * jnp.exp(b)).astype(BF16), st_b, NT)
                b_last = b[CHUNK - 1:CHUNK, :]
                st_new = _dot(v, (kk * jnp.exp(b_last - b)).astype(BF16), TN)
                a = yield from _hg_intra(qs, kk, b)
                st_ref[p] = st * jnp.exp(b_last) + st_new
                o_ref[rows, cols] = o_state + _dot(a.astype(BF16), v, NN)

            _interleave([head(p) for p in range(pair)])
            return carry

        lax.fori_loop(0, n_chunk, chunk, 0)

    return pl.pallas_call(
        body, grid=(HG_HEADS // pair, s // HG_TB), in_specs=_hg_specs(lambda n: n, pair),
        out_specs=[pl.BlockSpec((HG_TB, pair * LANE), lambda h, n: (n, h)), pl.BlockSpec((pair, n_chunk, HG_K, HG_K), lambda h, n: (h, n, 0, 0))],
        out_shape=[_sds((s, HG_W), F32), _sds((HG_HEADS, s // CHUNK, HG_K, HG_K), BF16)],
        scratch_shapes=[pltpu.VMEM((pair, HG_K, HG_K), F32)],
        compiler_params=_params(2), name="hgrn_fwd")(*[proj] * (3 * pair), hl)


def _hgrn_bwd(proj, hl, states, d_o):
    s = proj.shape[0]
    n_chunk = HG_TB // CHUNK
    n_blk = s // HG_TB
    pair = HG_PAIR_BWD
    rev = lambda n: n_blk - 1 - n

    def body(*refs):
        q_refs, f_refs, i_refs = refs[:pair], refs[pair:2 * pair], refs[2 * pair:3 * pair]
        hl_ref, st_in_ref, do_ref, dq_ref, df_ref, di_ref, dhl_ref, dst_ref, dlb_ref = refs[3 * pair:]
        n = pl.program_id(1)

        @pl.when(n == 0)
        def _():
            dst_ref[...] = jnp.zeros_like(dst_ref)
            dlb_ref[...] = jnp.zeros_like(dlb_ref)

        r_i = lax.broadcasted_iota(jnp.int32, (CHUNK, CHUNK), 0)
        c_i = lax.broadcasted_iota(jnp.int32, (CHUNK, CHUNK), 1)
        tri_lower = (r_i >= c_i).astype(BF16)
        tri_upper = (r_i <= c_i).astype(BF16)
        row = lax.broadcasted_iota(jnp.int32, (CHUNK, 1), 0)
        spread = _hg_spread()

        def chunk(cc, carry):
            c = n_chunk - 1 - cc
            rows = pl.ds(pl.multiple_of(c * CHUNK, CHUNK), CHUNK)
            def head(p):
                cols = slice(p * LANE, (p + 1) * LANE)
                lb = _lower_bound(hl_ref[:, cols])
                q_raw = q_refs[p][rows, :]
                vb = i_refs[p][rows, :].astype(BF16)
                sg, f, kk, sq, qs, b = _hg_gates(q_raw, f_refs[p][rows, :], lb, tri_lower)
                yield
                e_b = jnp.exp(b)
                qe = qs * e_b
                b_last = b[CHUNK - 1:CHUNK, :]
                e_last = jnp.exp(b_last)
                e_kd = jnp.exp(b_last - b)
                kd = kk * e_kd
                st0 = st_in_ref[p, c]
                d_ob = do_ref[rows, cols].astype(BF16)
                dst = dst_ref[p]
                dst_b = dst.astype(BF16)
                d_a = jnp.where(r_i >= c_i, _dot(d_ob, vb, NT), 0.0)
                d_at = jnp.where(r_i <= c_i, _dot(vb, d_ob, NT), 0.0)
                d_v_st = _dot(kd.astype(BF16), dst_b, NT)
                d_kd = _dot(vb, dst_b, NN)
                d_qe = _dot(d_ob, st0, NN)
                dst_new = _dot(d_ob, qe.astype(BF16), TN)
                yield
                a, dqs, dkk, d_b = yield from _hg_intra(qs, kk, b, (d_a, d_at, spread))
                d_v = _dot(a.astype(BF16), d_ob, TN) + d_v_st
                dqs_st = d_qe * e_b
                dkk_st = d_kd * e_kd
                dqs = dqs + dqs_st
                dkk = dkk + dkk_st
                d_b_last = jnp.sum(d_kd * kd, axis=0, keepdims=True) + jnp.sum(dst * st0.astype(F32), axis=0, keepdims=True) * e_last
                d_b = d_b + qs * dqs_st - kk * dkk_st + jnp.where(row == CHUNK - 1, d_b_last, 0.0)
                d_g = _tri_matmul(tri_upper, d_b)
                dst_ref[p] = dst_new + dst * e_last
                yield
                d_f = d_g / f - dkk
                dlb_ref[:, cols] += jnp.sum(d_f * (1.0 - sg), axis=0, keepdims=True)
                dq_ref[rows, cols] = (dqs * (sq * (1.0 + q_raw * (1.0 - sq)))).astype(dq_ref.dtype)
                df_ref[rows, cols] = (d_f * (1.0 - lb) * (sg * (1.0 - sg))).astype(df_ref.dtype)
                di_ref[rows, cols] = d_v.astype(di_ref.dtype)

            _interleave([head(p) for p in range(pair)])
            return carry

        lax.fori_loop(0, n_chunk, chunk, 0)

        @pl.when(n == n_blk - 1)
        def _():
            lb = _lower_bound(hl_ref[...])
            d_hl0 = dlb_ref[...] * (lb * (1.0 - lb))
            dhl_ref[...] = jnp.concatenate([d_hl0, -d_hl0], axis=0)

    out_blk = pl.BlockSpec((HG_TB, pair * LANE), lambda h, n: (rev(n), h))
    return pl.pallas_call(
        body, grid=(HG_HEADS // pair, n_blk),
        in_specs=_hg_specs(rev, pair) + [pl.BlockSpec((pair, n_chunk, HG_K, HG_K), lambda h, n: (h, rev(n), 0, 0)), out_blk],
        out_specs=[out_blk, out_blk, out_blk, pl.BlockSpec((2, pair * LANE), lambda h, n: (0, h))],
        out_shape=[_sds((s, HG_W), BF16)] * 3 + [_sds((2, HG_W), F32)],
        scratch_shapes=[pltpu.VMEM((pair, HG_K, HG_K), F32), pltpu.VMEM((1, pair * LANE), F32)],
        compiler_params=_params(2), name="hgrn_bwd")(*[proj] * (3 * pair), hl, states, d_o)


def _mod_part(c_all, w_shard, b_shard):
    n = w_shard.shape[1]
    tn = 512

    def body(c_ref, w_ref, b_ref, o_ref):
        o_ref[...] = _dot(c_ref[...].astype(BF16), w_ref[...].astype(BF16), NN) + b_ref[...]

    return pl.pallas_call(body, grid=(n // tn,),
                          in_specs=[pl.BlockSpec((N_DEV, D), lambda j: (0, 0)), pl.BlockSpec((D, tn), lambda j: (0, j)), pl.BlockSpec((1, tn), lambda j: (0, j))],
                          out_specs=pl.BlockSpec((N_DEV, tn), lambda j: (0, j)), out_shape=_sds((N_DEV, n), F32),
                          compiler_params=_params(1, 32 << 20), name="mod_part")(c_all, w_shard, b_shard)


def _grad_w_ada(c_all_t, dmod_cols):
    n = dmod_cols.shape[1]
    tn = 512

    def body(c_ref, d_ref, o_ref):
        cv = c_ref[...].astype(BF16).astype(F32)
        dv = d_ref[...].astype(BF16).astype(F32)
        acc = cv[:, 0:1] * dv[0:1, :]
        for k in range(1, N_DEV):
            acc = acc + cv[:, k:k + 1] * dv[k:k + 1, :]
        o_ref[...] = acc

    return pl.pallas_call(body, grid=(n // tn,),
                          in_specs=[pl.BlockSpec((D, N_DEV), lambda j: (0, 0)), pl.BlockSpec((N_DEV, tn), lambda j: (0, j))],
                          out_specs=pl.BlockSpec((D, tn), lambda j: (0, j)), out_shape=_sds((D, n), F32),
                          compiler_params=_params(1, 32 << 20), name="grad_w_ada")(c_all_t, dmod_cols)


def _row_tile(r, c, max_elems=1 << 18):
    if r * c <= max_elems or r % 8:
        return r
    best = 8
    for t in range(8, r + 1, 8):
        if r % t == 0 and t * c <= max_elems:
            best = t
    return best


WIDE_TILE = 5 << 17


def _adamw(pieces, w, m, v, name, emit_grad=True, own=None):
    p, r, c = pieces.shape
    tr = _row_tile(r, c)
    c1 = 1.0 / (1.0 - ADAM_B1 ** ADAM_STEP)
    c2 = 1.0 / (1.0 - ADAM_B2 ** ADAM_STEP)

    def body(*refs):
        if own is None:
            p_ref, w_ref, m_ref, v_ref, *outs = refs
            g = p_ref[0].astype(F32)
        else:
            o_ref, p_ref, w_ref, m_ref, v_ref, *outs = refs
            g = o_ref[...].astype(F32) + p_ref[0].astype(F32)
        for k in range(1, p):
            g = g + p_ref[k].astype(F32)
        m2 = ADAM_B1 * m_ref[...] + (1.0 - ADAM_B1) * g
        v2 = ADAM_B2 * v_ref[...] + (1.0 - ADAM_B2) * (g * g)
        delta = -ADAM_LR * ((m2 * c1) / (jnp.sqrt(v2 * c2) + ADAM_EPS) + ADAM_WD * w_ref[...])
        if emit_grad:
            outs[0][...] = g
        outs[-3][...] = delta
        outs[-2][...] = m2
        outs[-1][...] = v2

    blk = pl.BlockSpec((tr, c), lambda i: (i, 0))
    n_out = 4 if emit_grad else 3
    lead = [] if own is None else [own]
    return pl.pallas_call(body, grid=(r // tr,), in_specs=[blk] * len(lead) + [pl.BlockSpec((p, tr, c), lambda i: (0, i, 0)), blk, blk, blk],
                          out_specs=[blk] * n_out, out_shape=[_sds((r, c), F32)] * n_out,
                          compiler_params=_params(1, 48 << 20), name=name)(*lead, pieces, w, m, v)


def _my_coords():
    return lax.axis_index("x"), lax.axis_index("y"), lax.axis_index("c")


def _flip(coords, k):
    x, y, c = coords
    return (1 - x if k & 4 else x, 1 - y if k & 2 else y, 1 - c if k & 1 else c)


def _lin(coords):
    return 4 * coords[0] + 2 * coords[1] + coords[2]


def _exchange_small(x3, bcast, name):
    n = x3.shape[2]

    def body(x_ref, o_ref, send_sems, recv_sems):
        me = _my_coords()
        my_id = _lin(me)
        o_ref[pl.ds(my_id, 1)] = x_ref[pl.ds(0 if bcast else my_id, 1)]
        copies = []
        for k in range(1, N_DEV):
            peer = _flip(me, k)
            src = x_ref.at[0 if bcast else _lin(peer)]
            cp = pltpu.make_async_remote_copy(src_ref=src, dst_ref=o_ref.at[my_id], send_sem=send_sems.at[k], recv_sem=recv_sems.at[k],
                                              device_id=peer, device_id_type=MESH)
            cp.start()
            copies.append(cp)
        for k in range(1, N_DEV):
            peer = _flip(me, k)
            pltpu.make_async_remote_copy(src_ref=x_ref.at[0], dst_ref=o_ref.at[_lin(peer)], send_sem=send_sems.at[k], recv_sem=recv_sems.at[k],
                                         device_id=peer, device_id_type=MESH).wait_recv()
        for cp in copies:
            cp.wait_send()

    vm = pl.BlockSpec(memory_space=pltpu.VMEM)
    return pl.pallas_call(body, in_specs=[vm], out_specs=vm, out_shape=_sds((N_DEV, 1, n), F32),
                          scratch_shapes=[pltpu.SemaphoreType.DMA((N_DEV,)), pltpu.SemaphoreType.DMA((N_DEV,))], name=name)(x3)


HBM_SPEC = pl.BlockSpec(memory_space=pltpu.HBM)
SEM_SPEC = pl.BlockSpec(memory_space=pltpu.SEMAPHORE)
ANY_SPEC = pl.BlockSpec(memory_space=pl.ANY)
DATAFLOW = pltpu.SideEffectType.DATAFLOW_SIDE_EFFECTING
GATHER_FLIPS = (1, 2, 4, 6)
PASS_FLIPS = (2, 4, 6)
TOKEN = (8, LANE)


def _hbm(t):
    return pltpu.with_memory_space_constraint(t, pltpu.HBM)


def _hbm_like(ts):
    return [pltpu.HBM(t.shape, t.dtype) for t in ts]


def _split_start(issue, srcs, lands, n_sem, name, deps=()):
    n, nd = len(srcs), len(deps)

    def body(*refs):
        issue(refs[:n], refs[n:2 * n], refs[2 * n + nd], refs[2 * n + nd + 1])
        refs[-1][...] = jnp.zeros(TOKEN, F32)

    outs = pl.pallas_call(
        body, name=name,
        out_shape=(pltpu.SemaphoreType.DMA((n_sem,)), pltpu.SemaphoreType.DMA((n_sem,)), *_hbm_like(srcs), *_hbm_like(lands), _sds(TOKEN, F32)),
        in_specs=[HBM_SPEC] * (2 * n) + [ANY_SPEC] * nd,
        out_specs=(SEM_SPEC, SEM_SPEC, *[HBM_SPEC] * (2 * n), pl.BlockSpec(memory_space=pltpu.VMEM)),
        input_output_aliases={i: 2 + i for i in range(2 * n)},
        compiler_params=pltpu.CompilerParams(has_side_effects=DATAFLOW))(*[_hbm(t) for t in srcs], *[_hbm(t) for t in lands], *deps)
    return dict(sems=outs[:2], thru=list(outs[2:2 + 2 * n]), token=outs[-1], n=n)


def _split_wait(finish, handle, after, name):
    n = handle["n"]
    thru = handle["thru"]

    def body(*refs):
        finish(refs[:n], refs[n:2 * n], refs[2 * n], refs[2 * n + 1])

    outs = pl.pallas_call(
        body, name=name, out_shape=_hbm_like(thru), in_specs=[HBM_SPEC] * (2 * n) + [SEM_SPEC, SEM_SPEC] + [ANY_SPEC] * len(after),
        out_specs=[HBM_SPEC] * (2 * n), input_output_aliases={i: i for i in range(2 * n)},
        compiler_params=pltpu.CompilerParams(has_side_effects=DATAFLOW))(*thru, *handle["sems"], *after)
    return list(outs[:n]), list(outs[n:])


def _gather_start(shards, name, deps=()):
    n = len(shards)
    my_id = _lin(_my_coords())
    lands = [lax.dynamic_update_slice(lax.empty((N_DEV,) + t.shape, t.dtype), t[None], (my_id, 0, 0)) for t in shards]

    def issue(src, land, send_sems, recv_sems):
        me = _my_coords()
        for w in range(n):
            for j, k in enumerate(GATHER_FLIPS):
                q = len(GATHER_FLIPS) * w + j
                pltpu.make_async_remote_copy(src_ref=src[w], dst_ref=land[w].at[_lin(me)], send_sem=send_sems.at[q], recv_sem=recv_sems.at[q],
                                             device_id=_flip(me, k), device_id_type=MESH).start()

    return _split_start(issue, shards, lands, len(GATHER_FLIPS) * n, name, deps)


def _gather_wait(handle, after, name):
    n = handle["n"]

    def finish(src, land, send_sems, recv_sems):
        me = _my_coords()
        for w in range(n):
            for j, k in enumerate(GATHER_FLIPS):
                q = len(GATHER_FLIPS) * w + j
                peer = _flip(me, k)
                cp = pltpu.make_async_remote_copy(src_ref=src[w], dst_ref=land[w].at[_lin(peer)], send_sem=send_sems.at[q], recv_sem=recv_sems.at[q],
                                                  device_id=peer, device_id_type=MESH)
                cp.wait_send()
                cp.wait_recv()

    return _split_wait(finish, handle, after, name)[1]


def _gather_pass(lands, name):
    n = len(lands)
    n_p = len(PASS_FLIPS)

    def body(*refs):
        land = refs[n:2 * n]
        send_sems, recv_sems = refs[2 * n:]
        me = _my_coords()
        sibling = _flip(me, 1)
        sent = []
        for w in range(n):
            for j, k in enumerate(PASS_FLIPS):
                blk = land[w].at[_lin(_flip(me, k))]
                cp = pltpu.make_async_remote_copy(src_ref=blk, dst_ref=blk, send_sem=send_sems.at[n_p * w + j], recv_sem=recv_sems.at[n_p * w + j],
                                                  device_id=sibling, device_id_type=MESH)
                cp.start()
                sent.append(cp)
        for w in range(n):
            for j, k in enumerate(PASS_FLIPS):
                blk = land[w].at[_lin(_flip(me, k + 1))]
                pltpu.make_async_remote_copy(src_ref=blk, dst_ref=blk, send_sem=send_sems.at[n_p * w + j], recv_sem=recv_sems.at[n_p * w + j],
                                             device_id=sibling, device_id_type=MESH).wait_recv()
        for cp in sent:
            cp.wait_send()

    return pl.pallas_call(body, in_specs=[ANY_SPEC] * n, out_specs=[ANY_SPEC] * n, out_shape=[_sds(t.shape, t.dtype) for t in lands],
                          input_output_aliases={i: i for i in range(n)},
                          scratch_shapes=[pltpu.SemaphoreType.DMA((n_p * n,)), pltpu.SemaphoreType.DMA((n_p * n,))], name=name)(*lands)


CHIP_FLIPS = (0, 2, 4, 6)


def _pair_copy(src, land, send_sems, recv_sems, w, j):
    me = _my_coords()
    q = len(CHIP_FLIPS) * w + j
    return pltpu.make_async_remote_copy(src_ref=src[w].at[_lin(_flip(me, CHIP_FLIPS[j] + 1))], dst_ref=land[w].at[j], send_sem=send_sems.at[q],
                                        recv_sem=recv_sems.at[q], device_id=_flip(me, 1), device_id_type=MESH)


def _pair_exchange(grads, name):
    n = len(grads)

    def body(*refs):
        src, land = refs[:n], refs[n:2 * n]
        send_sems, recv_sems = refs[2 * n:]
        sent = [_pair_copy(src, land, send_sems, recv_sems, w, j) for w in range(n) for j in range(len(CHIP_FLIPS))]
        for cp in sent:
            cp.start()
        for cp in sent:
            cp.wait_recv()
        for cp in sent:
            cp.wait_send()

    outs = pl.pallas_call(body, in_specs=[ANY_SPEC] * n, out_specs=[ANY_SPEC] * n,
                          out_shape=[_sds((len(CHIP_FLIPS),) + g.shape[1:], g.dtype) for g in grads],
                          scratch_shapes=[pltpu.SemaphoreType.DMA((len(CHIP_FLIPS) * n,))] * 2, name=name)(*grads)
    return list(outs)


def _pair_start(grads, name, deps=()):
    n = len(grads)
    lands = [lax.empty((len(CHIP_FLIPS),) + g.shape[1:], g.dtype) for g in grads]

    def issue(src, land, send_sems, recv_sems):
        for w in range(n):
            for j in range(len(CHIP_FLIPS)):
                _pair_copy(src, land, send_sems, recv_sems, w, j).start()

    return _split_start(issue, grads, lands, len(CHIP_FLIPS) * n, name, deps)


def _pair_wait(handle, after, name):
    n = handle["n"]

    def finish(src, land, send_sems, recv_sems):
        for w in range(n):
            for j in range(len(CHIP_FLIPS)):
                cp = _pair_copy(src, land, send_sems, recv_sems, w, j)
                cp.wait_send()
                cp.wait_recv()

    return _split_wait(finish, handle, after, name)


def _pair_add(grad, theirs, name):
    p, r, c = theirs.shape
    tr = _row_tile(r, c, WIDE_TILE)
    me = _my_coords()
    ids = jnp.stack([_lin(_flip(me, k)) for k in CHIP_FLIPS]).astype(jnp.int32)

    def body(ids_ref, a_ref, b_ref, o_ref):
        o_ref[...] = (a_ref[...].astype(F32) + b_ref[...].astype(F32)).astype(o_ref.dtype)

    blk = pl.BlockSpec((None, tr, c), lambda j, i, ids_ref: (j, i, 0))
    return pl.pallas_call(
        body, out_shape=_sds((p, r, c), theirs.dtype), compiler_params=_params(2), name=name,
        grid_spec=pltpu.PrefetchScalarGridSpec(
            num_scalar_prefetch=1, grid=(p, r // tr),
            in_specs=[pl.BlockSpec((None, tr, c), lambda j, i, ids_ref: (ids_ref[j], i, 0)), blk], out_specs=blk))(ids, grad, theirs)


def _chips_start(parts, name, deps=()):
    n = len(parts)
    n_c = len(CHIP_FLIPS) - 1
    lands = [lax.empty((n_c,) + t.shape[1:], t.dtype) for t in parts]

    def issue(src, land, send_sems, recv_sems):
        me = _my_coords()
        for w in range(n):
            for j in range(1, n_c + 1):
                q = n_c * w + j - 1
                pltpu.make_async_remote_copy(src_ref=src[w].at[j], dst_ref=land[w].at[j - 1], send_sem=send_sems.at[q], recv_sem=recv_sems.at[q],
                                             device_id=_flip(me, CHIP_FLIPS[j]), device_id_type=MESH).start()

    return _split_start(issue, parts, lands, n_c * n, name, deps)


def _chips_wait(handle, after, name):
    n = handle["n"]
    n_c = len(CHIP_FLIPS) - 1

    def finish(src, land, send_sems, recv_sems):
        me = _my_coords()
        for w in range(n):
            for j in range(1, n_c + 1):
                q = n_c * w + j - 1
                cp = pltpu.make_async_remote_copy(src_ref=src[w].at[j], dst_ref=land[w].at[j - 1], send_sem=send_sems.at[q], recv_sem=recv_sems.at[q],
                                                  device_id=_flip(me, CHIP_FLIPS[j]), device_id_type=MESH)
                cp.wait_send()
                cp.wait_recv()

    return _split_wait(finish, handle, after, name)


def _after(t, *tokens):
    for tok in tokens:
        t = t + tok[0:1, 0:1]
    return t


def _rope_tables(positions):
    half = ROT // 2
    inv_freq = ROPE_THETA ** (-jnp.arange(0, ROT, 2, dtype=F32) / ROT)
    ang = positions.astype(F32).reshape(-1, 1) * inv_freq
    cos, sin = jnp.cos(ang), jnp.sin(ang)
    s = ang.shape[0]
    pad = jnp.zeros((s, HEAD_DIM - ROT), F32)
    zero = jnp.zeros((s, half), F32)
    two = lambda t: jnp.concatenate([t, t], axis=1)
    return (two(jnp.concatenate([cos, cos, pad + 1.0], axis=1)), two(jnp.concatenate([-sin, zero, pad], axis=1)),
            two(jnp.concatenate([zero, sin, pad], axis=1)))


def _local_step(x, tgt, tabs, mod, sinks_pad, hl, hg_norm, g_pre_mix, g_post_mix, g_pre_ffn, g_post_ffn, weights, scatter, scatter_on):
    s = x.shape[0]
    h1 = _pre_fwd(x, g_pre_mix, mod, 1, 0, "pre_mix_fwd")
    (w_in_t,) = weights("in", h1)
    proj = _mm_nt(h1, w_in_t, s, 256, D, F32, "proj_mm")
    att = _attn_fwd(proj, tabs, sinks_pad)
    o_raw, states = _hgrn_fwd(proj, hl)
    ohg = _hgout_fwd(o_raw, proj, hg_norm)
    w_attn_dm, w_hgrn_dm, w_out = weights("mix", ohg)
    y_a = _mm_nn_dm(att, w_attn_dm, s, F32, "attn_proj_mm")
    y_h = _mm_nn_dm(ohg, w_hgrn_dm, s, F32, "hgrn_proj_mm")
    merged = _merge_fwd(y_a, y_h, proj)
    y = _mm_nn(merged, w_out, s, 512, D, F32, "out_mm")
    x1 = _post_fwd(x, y, g_post_mix, mod, 2, "post_mix_fwd")
    h2 = _pre_fwd(x1, g_pre_ffn, mod, 4, 3, "pre_ffn_fwd")
    (w_ffn_in_dm,) = weights("ffn_in", h2)
    gu = _mm_nn_dm(h2, w_ffn_in_dm, s // 2, F32, "ffn_in_mm")
    act = _swiglu_fwd(gu)
    (w_ffn_out,) = weights("ffn_out", act)
    y2 = _mm_nn(act, w_ffn_out, s, 512, FFN // 4, F32, "ffn_out_mm")
    err, loss = _post_fwd_loss(x1, y2, g_post_ffn, mod, 5, tgt, "post_ffn_loss")
    dy2, d_gate2, dg_post_ffn = _post_bwd(err, y2, g_post_ffn, mod, 5, "post_ffn_bwd")
    gw_ffn_out = _mm_tn(act, dy2, 512, D, BF16, "ffn_out_dw")
    t_pair = scatter([gw_ffn_out.reshape(N_DEV, FFN // N_DEV, D)], "ffn_out")
    d_act = _mm_nt(dy2, w_ffn_out, s, 512, D, F32, "ffn_out_dx", deps=[t_pair])
    dgu = _swiglu_bwd(d_act, gu)
    t_out = scatter_on("ffn_out", dgu)
    gw_ffn_in = _mm_tn_dm(h2, dgu, 1024, BF16, "ffn_in_dw")
    t_pair = scatter([gw_ffn_in], "ffn_in")
    dh2 = _mm_nt_dm(dgu, w_ffn_in_dm, s, 1024, F32, "ffn_in_dx", deps=[t_pair])
    mod = _after(mod, t_out)
    dx1, d_shift2, d_scale2, dg_pre_ffn = _pre_bwd(dh2, x1, err, g_pre_ffn, mod, 4, "pre_ffn_bwd")
    dy, d_gate1, dg_post_mix = _post_bwd(dx1, y, g_post_mix, mod, 2, "post_mix_bwd")
    t_in = scatter_on("ffn_in", dy)
    d_merged = _mm_nt(dy, w_out, s, 512, D, F32, "out_dx")
    gw_out = _mm_tn(merged, dy, 512, D, BF16, "out_dw")
    dy_a, dy_h, d_gate_a, d_gate_h = _merge_bwd(d_merged, y_a, y_h, proj)
    gw_attn = _mm_tn_dm(att, dy_a, ATT_W, BF16, "attn_proj_dw")
    gw_hgrn = _mm_tn_dm(ohg, dy_h, HG_W, BF16, "hgrn_proj_dw")
    t_pair = scatter([gw_attn, gw_hgrn, gw_out.reshape(N_DEV, D // N_DEV, D)], "mix")
    d_att = _mm_nt_dm(dy_a, w_attn_dm, s, ATT_W, F32, "attn_proj_dx")
    d_ohg = _mm_nt_dm(dy_h, w_hgrn_dm, s, HG_W, F32, "hgrn_proj_dx", deps=[t_pair])
    d_o, d_gh, d_hg_norm = _hgout_bwd(d_ohg, o_raw, proj, _after(hg_norm, t_in))
    d_qh, d_fh, d_ih, d_hl = _hgrn_bwd(proj, hl, states, d_o)
    t_mix = scatter_on("mix", d_qh)
    d_qa, d_ka, d_va, d_sinks = _attn_bwd(proj, tabs, _after(sinks_pad, t_mix), d_att)
    d_proj = jnp.concatenate([d_qa, d_ka.astype(BF16), d_va.astype(BF16), d_qh, d_fh, d_ih, d_gh, d_gate_a, d_gate_h], axis=1)
    dh1 = _mm_nn(d_proj, w_in_t, s // 2, 512, IN_COLS // 2, F32, "proj_dx")
    grad_x, d_shift1, d_scale1, dg_pre_mix = _pre_bwd(dh1, x, dx1, g_pre_mix, mod, 1, "pre_mix_bwd")
    d_mod = jnp.concatenate([d_shift1, d_scale1, d_gate1, d_shift2, d_scale2, d_gate2], axis=1)
    small = [d_mod, dg_pre_mix, dg_post_mix, dg_pre_ffn, dg_post_ffn, d_hl.reshape(1, 2 * HG_W), d_hg_norm, d_sinks]
    return loss, grad_x, small, h1, d_proj


def kernel(x, c, positions, w_ada, b_ada, g_pre_mix, g_post_mix, g_pre_ffn, g_post_ffn, w_in, attn_sinks, w_attn_proj, hg_lower_bounds, hg_norm, w_hgrn_proj, w_out, w_ffn_in, w_ffn_out, loss_target, m_w_ada, m_b_ada, m_g_pre_mix, m_g_post_mix, m_g_pre_ffn, m_g_post_ffn, m_w_in, m_attn_sinks, m_w_attn_proj, m_hg_lower_bounds, m_hg_norm, m_w_hgrn_proj, m_w_out, m_w_ffn_in, m_w_ffn_out, v_w_ada, v_b_ada, v_g_pre_mix, v_g_post_mix, v_g_pre_ffn, v_g_post_ffn, v_w_in, v_attn_sinks, v_w_attn_proj, v_hg_lower_bounds, v_hg_norm, v_w_hgrn_proj, v_w_out, v_w_ffn_in, v_w_ffn_out):
    my_id = _lin(_my_coords())
    s = x.shape[1]
    n_ada = w_ada.shape[2]

    c_all = _exchange_small(c.reshape(1, 1, D), True, "gather_c").reshape(N_DEV, D)
    b_cols = lax.dynamic_slice(b_ada, (0, my_id * n_ada), (1, n_ada))
    mod_part = _mod_part(c_all, w_ada[0], b_cols)
    mod = _exchange_small(mod_part.reshape(N_DEV, 1, n_ada), False, "scatter_mod").reshape(1, N_MOD * D)
    groups = {"in": [w_in[0].T], "mix": [w_attn_proj[0], w_hgrn_proj[0], w_out[0]], "ffn_in": [w_ffn_in[0]], "ffn_out": [w_ffn_out[0]]}

    def start(group, dep):
        shards, dep = lax.optimization_barrier((groups[group], dep))
        return _gather_start([t.astype(BF16) for t in shards], "gather_start_" + group, deps=[dep])

    gathers = {"in": start("in", mod)}
    gathers["mix"] = start("mix", gathers["in"]["token"])
    gathers["ffn_in"] = start("ffn_in", gathers["mix"]["token"])
    gathers["ffn_out"] = start("ffn_out", gathers["ffn_in"]["token"])

    def weights(group, after):
        after = [after, gathers["ffn_out"]["token"]]
        lands = _gather_pass(_gather_wait(gathers[group], after, "gather_wait_" + group), "gather_pass_" + group)
        if group == "in":
            return (lands[0].reshape(IN_COLS, D),)
        if group == "mix":
            return lands[0], lands[1], lands[2].reshape(D, D)
        return (lands[0],) if group == "ffn_in" else (lands[0].reshape(FFN, D),)

    pairs, scatters = {}, {}

    def scatter(grads, group):
        pairs[group] = _pair_start(grads, "scatter_pair_" + group)
        return pairs[group]["token"]

    def scatter_on(group, after):
        if group in pairs:
            local, theirs = _pair_wait(pairs[group], [after], "scatter_pair_wait_" + group)
        else:
            local, theirs = after, _pair_exchange(after, "scatter_pair_" + group)
        parts = [_pair_add(g, t, "scatter_pair_add_%s_%d" % (group, k)) for k, (g, t) in enumerate(zip(local, theirs))]
        scatters[group] = _chips_start(parts, "scatter_start_" + group)
        return scatters[group]["token"]

    sinks_pad = jnp.pad(attn_sinks, ((0, 0), (0, LANE - ATT_HEADS)))
    loss, grad_x, small, h1, d_proj = _local_step(
        x[0], loss_target[0], _rope_tables(positions), mod, sinks_pad, hg_lower_bounds, hg_norm, g_pre_mix, g_post_mix, g_pre_ffn, g_post_ffn,
        weights, scatter, scatter_on)
    loss = lax.psum(loss[0, 0], ("x", "y", "c"))

    sizes = [t.shape[1] for t in small]
    parts = _exchange_small(jnp.concatenate(small, axis=1).reshape(1, 1, sum(sizes)), True, "gather_small_grads")
    gw_in = _mm_tn(d_proj, h1, 256, D, BF16, "proj_dw", deps=[parts]).reshape(N_DEV, IN_COLS // N_DEV, D)
    scatter_on("in", [gw_in])
    offs = [sum(sizes[:k]) for k in range(len(sizes))]
    piece = lambda k, n=None: parts[:, :, offs[k]:offs[k] + (sizes[k] if n is None else n)]
    small_w = [(piece(0), b_ada, m_b_ada, v_b_ada), (piece(1), g_pre_mix, m_g_pre_mix, v_g_pre_mix),
               (piece(2), g_post_mix, m_g_post_mix, v_g_post_mix), (piece(3), g_pre_ffn, m_g_pre_ffn, v_g_pre_ffn),
               (piece(4), g_post_ffn, m_g_post_ffn, v_g_post_ffn),
               (piece(5).reshape(N_DEV, 2, HG_W), hg_lower_bounds, m_hg_lower_bounds, v_hg_lower_bounds),
               (piece(6), hg_norm, m_hg_norm, v_hg_norm), (piece(7, ATT_HEADS), attn_sinks, m_attn_sinks, v_attn_sinks)]
    names = ["b_ada", "g_pre_mix", "g_post_mix", "g_pre_ffn", "g_post_ffn", "hg_lower_bounds", "hg_norm", "attn_sinks"]
    res = {n: _adamw(p, w, m, v, "adamw_" + n) for n, (p, w, m, v) in zip(names, small_w)}

    dmod_cols = lax.dynamic_slice(parts.reshape(N_DEV, -1), (0, my_id * n_ada), (N_DEV, n_ada))
    g_w_ada = _grad_w_ada(c_all.T, dmod_cols)
    res["w_ada"] = [g_w_ada] + list(_adamw(g_w_ada[None], w_ada[0], m_w_ada[0], v_w_ada[0], "adamw_w_ada", emit_grad=False))

    big = {"ffn_out": [("w_ffn_out", w_ffn_out, m_w_ffn_out, v_w_ffn_out)], "ffn_in": [("w_ffn_in", w_ffn_in, m_w_ffn_in, v_w_ffn_in)],
           "mix": [("w_attn_proj", w_attn_proj, m_w_attn_proj, v_w_attn_proj), ("w_hgrn_proj", w_hgrn_proj, m_w_hgrn_proj, v_w_hgrn_proj),
                   ("w_out", w_out, m_w_out, v_w_out)],
           "in": [("w_in", w_in, m_w_in, v_w_in)]}
    after = [scatters["in"]["token"]]
    for group, members in big.items():
        if group == "in":
            after = [res[n][1] for n in res]
        local, lands = _chips_wait(scatters[group], after, "scatter_wait_" + group)
        own = [t[0] for t in local]
        for (n, w, m, v), g_own, land in zip(members, own, lands):
            if group == "in":
                res[n] = [t.T for t in _adamw(land, w[0].T, m[0].T, v[0].T, "adamw_" + n, own=g_own)]
            else:
                res[n] = _adamw(land, w[0], m[0], v[0], "adamw_" + n, own=g_own)
            after = after + [res[n][1]]

    order = ["w_ada", "b_ada", "g_pre_mix", "g_post_mix", "g_pre_ffn", "g_post_ffn", "w_in", "attn_sinks", "w_attn_proj",
             "hg_lower_bounds", "hg_norm", "w_hgrn_proj", "w_out", "w_ffn_in", "w_ffn_out"]
    lead = {"w_ada", "w_in", "w_attn_proj", "w_hgrn_proj", "w_out", "w_ffn_in", "w_ffn_out"}
    outs = [loss, grad_x[None]]
    for k in range(4):
        outs += [res[n][k][None] if n in lead else res[n][k] for n in order]
    return tuple(outs)
```

```python
import functools

import jax
import jax.numpy as jnp
from jax import lax
from jax.experimental import pallas as pl
from jax.experimental.pallas import tpu as pltpu

F32 = jnp.float32
BF16 = jnp.bfloat16

N_DEV = 8
D = 2048
ATT_HEADS = 16
KV_HEADS = 2
HEAD_DIM = 64
GROUP = ATT_HEADS // KV_HEADS
ATT_W = ATT_HEADS * HEAD_DIM
BLK = 128
ROT = HEAD_DIM // 4
ROPE_THETA = 500000.0
HG_HEADS = 8
HG_K = 128
HG_W = HG_HEADS * HG_K
CHUNK = 64
SUB = 16
FFN = 5632
N_MOD = 6
EPS = 1e-6
LANE = 128
Q_A, K_A, V_A, Q_H, F_H, I_H, G_H, GT_A, GT_H, IN_COLS = 0, 1024, 1152, 1280, 2304, 3328, 4352, 5376, 7424, 9472

ADAM_LR, ADAM_B1, ADAM_B2, ADAM_EPS, ADAM_WD, ADAM_STEP = 0.001, 0.9, 0.999, 1e-08, 0.01, 10

TR = 256
HG_TB = 512
VMEM_BIG = 56 << 20
MESH = pl.DeviceIdType.MESH


def _sds(shape, dtype):
    return jax.ShapeDtypeStruct(shape, dtype)


def _params(n_axes, vmem=None):
    return pltpu.CompilerParams(dimension_semantics=("arbitrary",) * n_axes, vmem_limit_bytes=vmem)


def _sig(t):
    return 1.0 / (1.0 + jnp.exp(-t))


def _dot(a, b, dims):
    return lax.dot_general(a, b, (dims, ((), ())), preferred_element_type=F32)


NN = ((1,), (0,))
NT = ((1,), (1,))
TN = ((0,), (0,))


def _matmul(a, b, a_spec, b_spec, o_spec, out_shape, grid, dims, acc_shape, name, deps=()):
    nk = grid[2]
    nd = len(deps)

    def body(a_ref, b_ref, *rest):
        o_ref, scratch = rest[nd], rest[nd + 1:]
        part = _dot(a_ref[...], b_ref[...], dims)
        if nk == 1:
            o_ref[...] = part.astype(o_ref.dtype)
        else:
            acc = scratch[0]
            k = pl.program_id(2)

            @pl.when(k == 0)
            def _():
                acc[...] = part

            @pl.when(k > 0)
            def _():
                acc[...] += part

            @pl.when(k == nk - 1)
            def _():
                o_ref[...] = acc[...].astype(o_ref.dtype)

    return pl.pallas_call(
        body, grid=grid, in_specs=[a_spec, b_spec] + [pl.BlockSpec(memory_space=pl.ANY)] * nd, out_specs=o_spec, out_shape=out_shape,
        scratch_shapes=[pltpu.VMEM(acc_shape, F32)] if nk > 1 else [],
        compiler_params=_params(3, VMEM_BIG), name=name)(a, b, *deps)


def _mm_nn(a, b, tm, tn, tk, out_dtype, name):
    m, k = a.shape
    n = b.shape[1]
    return _matmul(a, b, pl.BlockSpec((tm, tk), lambda j, i, kk: (i, kk)), pl.BlockSpec((tk, tn), lambda j, i, kk: (kk, j)),
                   pl.BlockSpec((tm, tn), lambda j, i, kk: (i, j)), _sds((m, n), out_dtype),
                   (n // tn, m // tm, k // tk), NN, (tm, tn), name)


def _mm_nn_dm(a, b, tm, out_dtype, name):
    m, k = a.shape
    n = b.shape[2]
    return _matmul(a, b, pl.BlockSpec((tm, k), lambda j, i, kk: (i, 0)), pl.BlockSpec((None, k, n), lambda j, i, kk: (j, 0, 0)),
                   pl.BlockSpec((tm, n), lambda j, i, kk: (i, j)), _sds((m, N_DEV * n), out_dtype),
                   (N_DEV, m // tm, 1), NN, (tm, n), name)


def _mm_nt(a, b, tm, tn, tk, out_dtype, name, deps=()):
    m, k = a.shape
    n = b.shape[0]
    return _matmul(a, b, pl.BlockSpec((tm, tk), lambda j, i, kk: (i, kk)), pl.BlockSpec((tn, tk), lambda j, i, kk: (j, kk)),
                   pl.BlockSpec((tm, tn), lambda j, i, kk: (i, j)), _sds((m, n), out_dtype),
                   (n // tn, m // tm, k // tk), NT, (tm, tn), name, deps)


def _mm_nt_dm(a, b, tm, tn, out_dtype, name, deps=()):
    m = a.shape[0]
    n_out, n = b.shape[1], b.shape[2]
    return _matmul(a, b, pl.BlockSpec((tm, n), lambda j, i, kk: (i, kk)), pl.BlockSpec((None, tn, n), lambda j, i, kk: (kk, j, 0)),
                   pl.BlockSpec((tm, tn), lambda j, i, kk: (i, j)), _sds((m, n_out), out_dtype),
                   (n_out // tn, m // tm, N_DEV), NT, (tm, tn), name, deps)


def _mm_tn(a, b, tm, tn, out_dtype, name, deps=()):
    s, m = a.shape
    n = b.shape[1]
    return _matmul(a, b, pl.BlockSpec((s, tm), lambda j, i, kk: (0, i)), pl.BlockSpec((s, tn), lambda j, i, kk: (0, j)),
                   pl.BlockSpec((tm, tn), lambda j, i, kk: (i, j)), _sds((m, n), out_dtype),
                   (n // tn, m // tm, 1), TN, (tm, tn), name, deps)


def _mm_tn_dm(a, b, tm, out_dtype, name):
    s, m = a.shape
    n = b.shape[1] // N_DEV
    return _matmul(a, b, pl.BlockSpec((s, tm), lambda j, i, kk: (0, i)), pl.BlockSpec((s, n), lambda j, i, kk: (0, j)),
                   pl.BlockSpec((None, tm, n), lambda j, i, kk: (j, i, 0)), _sds((N_DEV, m, n), out_dtype),
                   (N_DEV, m // tm, 1), TN, (tm, n), name)


def _row_spec():
    return pl.BlockSpec((TR, D), lambda i: (i, 0))


def _vec_spec(k=0):
    return pl.BlockSpec((1, D), lambda i: (0, k))


def _acc_rows(ref, first, val):
    @pl.when(first)
    def _():
        ref[...] = val

    @pl.when(jnp.logical_not(first))
    def _():
        ref[...] += val


def _pre_fwd(x, g, mod, k_scale, k_shift, name):
    s = x.shape[0]

    def body(x_ref, g_ref, sc_ref, sh_ref, h_ref):
        xv = x_ref[...]
        r = lax.rsqrt(jnp.mean(xv * xv, axis=-1, keepdims=True) + EPS)
        n = xv * r * g_ref[...]
        h_ref[...] = (n * (1.0 + sc_ref[...]) + sh_ref[...]).astype(h_ref.dtype)

    return pl.pallas_call(body, grid=(s // TR,), in_specs=[_row_spec(), _vec_spec(), _vec_spec(k_scale), _vec_spec(k_shift)],
                          out_specs=_row_spec(), out_shape=_sds((s, D), BF16), compiler_params=_params(1), name=name)(x, g, mod, mod)


def _post_fwd(x, y, g, mod, k_gate, name):
    s = x.shape[0]

    def body(x_ref, y_ref, g_ref, gt_ref, o_ref):
        yv = y_ref[...]
        r = lax.rsqrt(jnp.mean(yv * yv, axis=-1, keepdims=True) + EPS)
        o_ref[...] = x_ref[...] + gt_ref[...] * (yv * r * g_ref[...])

    return pl.pallas_call(body, grid=(s // TR,), in_specs=[_row_spec(), _row_spec(), _vec_spec(), _vec_spec(k_gate)],
                          out_specs=_row_spec(), out_shape=_sds((s, D), F32), compiler_params=_params(1), name=name)(x, y, g, mod)


def _post_fwd_loss(x, y, g, mod, k_gate, tgt, name):
    s = x.shape[0]

    def body(x_ref, y_ref, g_ref, gt_ref, t_ref, e_ref, loss_ref):
        i = pl.program_id(0)
        yv = y_ref[...]
        r = lax.rsqrt(jnp.mean(yv * yv, axis=-1, keepdims=True) + EPS)
        err = x_ref[...] + gt_ref[...] * (yv * r * g_ref[...]) - t_ref[...]
        e_ref[...] = err * (1.0 / D)
        part = 0.5 * jnp.sum(jnp.mean(err * err, axis=-1, keepdims=True), axis=0, keepdims=True)
        _acc_rows(loss_ref, i == 0, part)

    return pl.pallas_call(body, grid=(s // TR,),
                          in_specs=[_row_spec(), _row_spec(), _vec_spec(), _vec_spec(k_gate), _row_spec()],
                          out_specs=[_row_spec(), pl.BlockSpec((1, 1), lambda i: (0, 0))],
                          out_shape=[_sds((s, D), F32), _sds((1, 1), F32)], compiler_params=_params(1), name=name)(x, y, g, mod, tgt)


def _pre_bwd(dh, x, res, g, mod, k_scale, name):
    s = x.shape[0]

    def body(dh_ref, x_ref, res_ref, g_ref, sc_ref, dx_ref, dsh_ref, dsc_ref, dg_ref):
        first = pl.program_id(0) == 0
        xv, dh_v, gv = x_ref[...], dh_ref[...], g_ref[...]
        r = lax.rsqrt(jnp.mean(xv * xv, axis=-1, keepdims=True) + EPS)
        xh = xv * r
        dn = dh_v * (1.0 + sc_ref[...])
        dgn = dn * gv
        dx_ref[...] = res_ref[...] + r * (dgn - xh * jnp.mean(dgn * xh, axis=-1, keepdims=True))
        _acc_rows(dsh_ref, first, jnp.sum(dh_v, axis=0, keepdims=True))
        _acc_rows(dsc_ref, first, jnp.sum(dh_v * (xh * gv), axis=0, keepdims=True))
        _acc_rows(dg_ref, first, jnp.sum(dn * xh, axis=0, keepdims=True))

    return pl.pallas_call(body, grid=(s // TR,),
                          in_specs=[_row_spec(), _row_spec(), _row_spec(), _vec_spec(), _vec_spec(k_scale)],
                          out_specs=[_row_spec(), _vec_spec(), _vec_spec(), _vec_spec()],
                          out_shape=[_sds((s, D), F32)] + [_sds((1, D), F32)] * 3,
                          compiler_params=_params(1), name=name)(dh, x, res, g, mod)


def _post_bwd(dx, y, g, mod, k_gate, name):
    s = y.shape[0]

    def body(dx_ref, y_ref, g_ref, gt_ref, dy_ref, dgt_ref, dg_ref):
        first = pl.program_id(0) == 0
        yv, dxv, gv = y_ref[...], dx_ref[...], g_ref[...]
        r = lax.rsqrt(jnp.mean(yv * yv, axis=-1, keepdims=True) + EPS)
        yh = yv * r
        dn = dxv * gt_ref[...]
        dgn = dn * gv
        dy_ref[...] = (r * (dgn - yh * jnp.mean(dgn * yh, axis=-1, keepdims=True))).astype(dy_ref.dtype)
        _acc_rows(dgt_ref, first, jnp.sum(dxv * (yh * gv), axis=0, keepdims=True))
        _acc_rows(dg_ref, first, jnp.sum(dn * yh, axis=0, keepdims=True))

    return pl.pallas_call(body, grid=(s // TR,), in_specs=[_row_spec(), _row_spec(), _vec_spec(), _vec_spec(k_gate)],
                          out_specs=[_row_spec(), _vec_spec(), _vec_spec()],
                          out_shape=[_sds((s, D), BF16), _sds((1, D), F32), _sds((1, D), F32)],
                          compiler_params=_params(1), name=name)(dx, y, g, mod)


SW_TN = 1408
SW_TR = 512
TALL = 1024


def _swiglu_fwd(gu):
    s = gu.shape[0]
    nb = FFN // SW_TN

    def body(g_ref, u_ref, a_ref):
        gv = g_ref[...]
        a_ref[...] = (gv * _sig(gv) * u_ref[...]).astype(a_ref.dtype)

    return pl.pallas_call(body, grid=(s // SW_TR, nb),
                          in_specs=[pl.BlockSpec((SW_TR, SW_TN), lambda i, j: (i, j)), pl.BlockSpec((SW_TR, SW_TN), lambda i, j: (i, j + nb))],
                          out_specs=pl.BlockSpec((SW_TR, SW_TN), lambda i, j: (i, j)), out_shape=_sds((s, FFN), BF16),
                          compiler_params=_params(2, 48 << 20), name="swiglu_fwd")(gu, gu)


def _swiglu_bwd(dact, gu):
    s = gu.shape[0]
    nb = FFN // SW_TN
    n_steps = (s // SW_TR) * nb

    def body(da_ref, g_ref, u_ref, o_ref, buf, sems):
        i, j = pl.program_id(0), pl.program_id(1)
        step = i * nb + j
        slot = step % 2

        def tiles(sl):
            rows = pl.ds(pl.multiple_of(i * SW_TR, SW_TR), SW_TR)
            return [pltpu.make_async_copy(buf.at[sl, h], o_ref.at[rows, pl.ds(pl.multiple_of((j + nb * h) * SW_TN, LANE), SW_TN)], sems.at[sl, h])
                    for h in range(2)]

        @pl.when(step >= 2)
        def _():
            for cp in tiles(slot):
                cp.wait()

        gv, da = g_ref[...], da_ref[...]
        sg = _sig(gv)
        buf[slot, 0] = (da * u_ref[...] * (sg * (1.0 + gv * (1.0 - sg)))).astype(buf.dtype)
        buf[slot, 1] = (da * (gv * sg)).astype(buf.dtype)
        for cp in tiles(slot):
            cp.start()

        @pl.when(step == n_steps - 1)
        def _():
            for cp in tiles(slot) + (tiles(1 - slot) if n_steps > 1 else []):
                cp.wait()

    blk = lambda f: pl.BlockSpec((SW_TR, SW_TN), f)
    return pl.pallas_call(body, grid=(s // SW_TR, nb),
                          in_specs=[blk(lambda i, j: (i, j)), blk(lambda i, j: (i, j)), blk(lambda i, j: (i, j + nb))],
                          out_specs=pl.BlockSpec(memory_space=pl.ANY), out_shape=_sds((s, 2 * FFN), BF16),
                          scratch_shapes=[pltpu.VMEM((2, 2, SW_TR, SW_TN), BF16), pltpu.SemaphoreType.DMA((2, 2))],
                          compiler_params=_params(2, 48 << 20), name="swiglu_bwd")(dact, gu, gu)


MG_TN = 256


def _merge_fwd(y_a, y_h, proj):
    s = y_a.shape[0]
    tn = MG_TN
    ba, bh = GT_A // tn, GT_H // tn

    def body(ya_ref, yh_ref, ga_ref, gh_ref, m_ref):
        m_ref[...] = (_sig(ga_ref[...]) * ya_ref[...] + _sig(gh_ref[...]) * yh_ref[...]).astype(m_ref.dtype)

    tr = min(s, TALL)
    blk = lambda f: pl.BlockSpec((tr, tn), f)
    return pl.pallas_call(body, grid=(s // tr, D // tn),
                          in_specs=[blk(lambda i, j: (i, j)), blk(lambda i, j: (i, j)), blk(lambda i, j: (i, j + ba)), blk(lambda i, j: (i, j + bh))],
                          out_specs=blk(lambda i, j: (i, j)), out_shape=_sds((s, D), BF16),
                          compiler_params=_params(2), name="merge_fwd")(y_a, y_h, proj, proj)


def _merge_bwd(dm, y_a, y_h, proj):
    s = y_a.shape[0]
    tn = MG_TN
    ba, bh = GT_A // tn, GT_H // tn

    def body(dm_ref, ya_ref, yh_ref, ga_ref, gh_ref, dya_ref, dyh_ref, dga_ref, dgh_ref):
        dmv = dm_ref[...]
        sa, sh = _sig(ga_ref[...]), _sig(gh_ref[...])
        dya_ref[...] = (dmv * sa).astype(BF16)
        dyh_ref[...] = (dmv * sh).astype(BF16)
        dga_ref[...] = (dmv * ya_ref[...] * (sa * (1.0 - sa))).astype(BF16)
        dgh_ref[...] = (dmv * yh_ref[...] * (sh * (1.0 - sh))).astype(BF16)

    tr = min(s, TALL)
    blk = lambda f: pl.BlockSpec((tr, tn), f)
    nat = blk(lambda i, j: (i, j))
    return pl.pallas_call(body, grid=(s // tr, D // tn),
                          in_specs=[nat, nat, nat, blk(lambda i, j: (i, j + ba)), blk(lambda i, j: (i, j + bh))],
                          out_specs=[nat] * 4, out_shape=[_sds((s, D), BF16)] * 4,
                          compiler_params=_params(2), name="merge_bwd")(dm, y_a, y_h, proj, proj)


def _hgout_fwd(o_raw, proj, hg_norm):
    s = o_raw.shape[0]
    bg = G_H // LANE

    def body(o_ref, g_ref, n_ref, out_ref):
        ov = o_ref[...]
        r = lax.rsqrt(jnp.mean(ov * ov, axis=-1, keepdims=True) + EPS)
        out_ref[...] = (ov * r * n_ref[...] * _sig(g_ref[...])).astype(out_ref.dtype)

    tr = min(s, TALL)
    blk = lambda f: pl.BlockSpec((tr, LANE), f)
    return pl.pallas_call(body, grid=(s // tr, HG_HEADS),
                          in_specs=[blk(lambda i, h: (i, h)), blk(lambda i, h: (i, h + bg)), pl.BlockSpec((1, LANE), lambda i, h: (0, 0))],
                          out_specs=blk(lambda i, h: (i, h)), out_shape=_sds((s, HG_W), BF16),
                          compiler_params=_params(2), name="hgout_fwd")(o_raw, proj, hg_norm)


def _hgout_bwd(d_out, o_raw, proj, hg_norm):
    s = o_raw.shape[0]
    bg = G_H // LANE

    def body(d_ref, o_ref, g_ref, n_ref, do_ref, dg_ref, dn_ref):
        first = jnp.logical_and(pl.program_id(0) == 0, pl.program_id(1) == 0)
        ov, dv, nv = o_ref[...], d_ref[...], n_ref[...]
        sg = _sig(g_ref[...])
        r = lax.rsqrt(jnp.mean(ov * ov, axis=-1, keepdims=True) + EPS)
        oh = ov * r
        d_on = dv * sg
        dg_ref[...] = (dv * (oh * nv) * (sg * (1.0 - sg))).astype(dg_ref.dtype)
        t = d_on * nv
        do_ref[...] = r * (t - oh * jnp.mean(t * oh, axis=-1, keepdims=True))
        _acc_rows(dn_ref, first, jnp.sum(d_on * oh, axis=0, keepdims=True))

    tr = min(s, TALL)
    blk = lambda f: pl.BlockSpec((tr, LANE), f)
    vec = pl.BlockSpec((1, LANE), lambda i, h: (0, 0))
    return pl.pallas_call(body, grid=(s // tr, HG_HEADS),
                          in_specs=[blk(lambda i, h: (i, h)), blk(lambda i, h: (i, h)), blk(lambda i, h: (i, h + bg)), vec],
                          out_specs=[blk(lambda i, h: (i, h)), blk(lambda i, h: (i, h)), vec],
                          out_shape=[_sds((s, HG_W), F32), _sds((s, HG_W), BF16), _sds((1, LANE), F32)],
                          compiler_params=_params(2), name="hgout_bwd")(d_out, o_raw, proj, hg_norm)


def _rope(t, cos, s_lo, s_hi):
    return t * cos + pltpu.roll(t, LANE - ROT // 2, 1) * s_lo + pltpu.roll(t, ROT // 2, 1) * s_hi


def _rope_wide(t, cos, s_lo, s_hi):
    return jnp.concatenate([_rope(t[:, k * LANE:(k + 1) * LANE], cos, s_lo, s_hi) for k in range(t.shape[1] // LANE)], axis=1)


def _attn_mask(has_prev):
    kj = lax.broadcasted_iota(jnp.int32, (2 * BLK, BLK), 0)
    qi = lax.broadcasted_iota(jnp.int32, (2 * BLK, BLK), 1)
    rel = BLK + qi - kj
    band = jnp.logical_and(rel >= 0, rel < BLK)
    return jnp.logical_and(band, jnp.logical_or(has_prev, kj >= BLK))


def _attn_specs():
    prev = lambda i: jnp.maximum(i - 1, 0)
    kb, vb = K_A // LANE, V_A // LANE
    blk = lambda f: pl.BlockSpec((BLK, LANE), f)
    tabs = [blk(lambda i: (i, 0))] * 3 + [blk(lambda i: (prev(i), 0))] * 3
    return [pl.BlockSpec((BLK, ATT_W), lambda i: (i, 0)), blk(lambda i: (i, kb)), blk(lambda i: (prev(i), kb)),
            blk(lambda i: (i, vb)), blk(lambda i: (prev(i), vb))] + tabs + [pl.BlockSpec((1, LANE), lambda i: (0, 0))]


def _attn_logits(qh, kg):
    return _dot(kg, qh, NT)


def _attn_probs(raw, mask, sk):
    logits = jnp.where(mask, raw * (HEAD_DIM ** -0.5), -jnp.inf)
    m = jnp.maximum(jnp.max(logits, axis=0, keepdims=True), sk)
    p = jnp.exp(logits - m)
    e_sink = jnp.exp(sk - m)
    inv = 1.0 / (jnp.sum(p, axis=0, keepdims=True) + e_sink)
    return p, inv, e_sink * inv


def _attn_fwd(proj, tabs, sinks):
    s = proj.shape[0]

    def body(q_ref, kc_ref, kp_ref, vc_ref, vp_ref, c0, l0, h0, c1, l1, h1, sk_ref, o_ref):
        i = pl.program_id(0)
        mask = _attn_mask(i > 0)
        q = _rope_wide(q_ref[...], c0[...], l0[...], h0[...]).astype(BF16)
        kk = jnp.concatenate([_rope(kp_ref[...], c1[...], l1[...], h1[...]), _rope(kc_ref[...], c0[...], l0[...], h0[...])], axis=0).astype(BF16)
        v_t = jnp.concatenate([vp_ref[...], vc_ref[...]], axis=0).T.astype(BF16)
        part = lambda t, h: t[:, h * HEAD_DIM:(h + 1) * HEAD_DIM]
        k_heads = [part(kk, g) for g in range(KV_HEADS)]

        def head(h):
            g = h // GROUP
            raw = _attn_logits(part(q, h), k_heads[g])
            yield
            p, inv, _ = _attn_probs(raw, mask, sk_ref[:, h:h + 1])
            yield
            out_t = _dot(v_t[g * HEAD_DIM:(g + 1) * HEAD_DIM], p.astype(BF16), NN)
            yield
            return out_t * inv

        o_ref[...] = jnp.concatenate(_interleave([head(h) for h in range(ATT_HEADS)]), axis=0).T.astype(o_ref.dtype)

    return pl.pallas_call(body, grid=(s // BLK,), in_specs=_attn_specs(),
                          out_specs=pl.BlockSpec((BLK, ATT_W), lambda i: (i, 0)), out_shape=_sds((s, ATT_W), BF16),
                          compiler_params=_params(1), name="attn_fwd")(proj, proj, proj, proj, proj, *tabs, *tabs, sinks)


def _attn_bwd(proj, tabs, sinks, d_att):
    s = proj.shape[0]

    def body(q_ref, kc_ref, kp_ref, vc_ref, vp_ref, c0, l0, h0, c1, l1, h1, sk_ref, do_ref, dq_ref, dk_ref, dv_ref, ds_ref):
        i = pl.program_id(0)

        @pl.when(i == 0)
        def _():
            dk_ref[...] = jnp.zeros_like(dk_ref)
            dv_ref[...] = jnp.zeros_like(dv_ref)
            ds_ref[...] = jnp.zeros_like(ds_ref)

        mask = _attn_mask(i > 0)
        q = _rope_wide(q_ref[...], c0[...], l0[...], h0[...]).astype(BF16)
        kk = jnp.concatenate([_rope(kp_ref[...], c1[...], l1[...], h1[...]), _rope(kc_ref[...], c0[...], l0[...], h0[...])], axis=0).astype(BF16)
        k_f32 = jnp.concatenate([_rope(kp_ref[...], c1[...], l1[...], h1[...]), _rope(kc_ref[...], c0[...], l0[...], h0[...])], axis=0)
        k_t = k_f32.T.astype(BF16)
        vv = jnp.concatenate([vp_ref[...], vc_ref[...]], axis=0).astype(BF16)
        d_o = do_ref[...].astype(BF16)
        lane = lax.broadcasted_iota(jnp.int32, (1, LANE), 1)
        part = lambda t, h: t[:, h * HEAD_DIM:(h + 1) * HEAD_DIM]
        k_heads = [part(kk, g) for g in range(KV_HEADS)]
        v_heads = [part(vv, g) for g in range(KV_HEADS)]

        def head(h):
            g = h // GROUP
            qh, doh = part(q, h), part(d_o, h)
            raw = _attn_logits(qh, k_heads[g])
            d_p = _dot(v_heads[g], doh, NT)
            yield
            p, inv, p_sink = _attn_probs(raw, mask, sk_ref[:, h:h + 1])
            prob = p * inv
            dv = _dot(prob.astype(BF16), doh, NN)
            yield
            dd = jnp.sum(prob * d_p, axis=0, keepdims=True)
            d_s = (prob * (d_p - dd)).astype(BF16)
            d_sink = jnp.where(lane == h, -jnp.sum(p_sink * dd, axis=1, keepdims=True), 0.0)
            dq_t = _dot(k_t[g * HEAD_DIM:(g + 1) * HEAD_DIM], d_s, NN)
            dk = _dot(d_s, qh, NN)
            yield
            return dq_t * (HEAD_DIM ** -0.5), dk * (HEAD_DIM ** -0.5), dv, d_sink

        per_head = _interleave([head(h) for h in range(ATT_HEADS)])
        dqs = [jnp.concatenate([t[0] for t in per_head], axis=0).T]
        group_sum = lambda k, g: functools.reduce(jnp.add, [t[k] for t in per_head[g * GROUP:(g + 1) * GROUP]])
        dks = [group_sum(1, g) for g in range(KV_HEADS)]
        dvs = [group_sum(2, g) for g in range(KV_HEADS)]
        d_sink = functools.reduce(jnp.add, [t[3] for t in per_head])
        dq_ref[...] = _rope_wide(jnp.concatenate(dqs, axis=1), c0[...], -l0[...], -h0[...]).astype(dq_ref.dtype)
        d_k = jnp.concatenate(dks, axis=1)
        d_v = jnp.concatenate(dvs, axis=1)
        cur = pl.ds(pl.multiple_of(i * BLK, BLK), BLK)
        prv = pl.ds(pl.multiple_of(jnp.maximum(i - 1, 0) * BLK, BLK), BLK)
        dk_ref[prv, :] += _rope(d_k[:BLK], c1[...], -l1[...], -h1[...])
        dk_ref[cur, :] += _rope(d_k[BLK:], c0[...], -l0[...], -h0[...])
        dv_ref[prv, :] += d_v[:BLK]
        dv_ref[cur, :] += d_v[BLK:]
        ds_ref[...] += d_sink

    full = pl.BlockSpec((s, LANE), lambda i: (0, 0))
    return pl.pallas_call(body, grid=(s // BLK,), in_specs=_attn_specs() + [pl.BlockSpec((BLK, ATT_W), lambda i: (i, 0))],
                          out_specs=[pl.BlockSpec((BLK, ATT_W), lambda i: (i, 0)), full, full, pl.BlockSpec((1, LANE), lambda i: (0, 0))],
                          out_shape=[_sds((s, ATT_W), BF16), _sds((s, LANE), F32), _sds((s, LANE), F32), _sds((1, LANE), F32)],
                          compiler_params=_params(1), name="attn_bwd")(proj, proj, proj, proj, proj, *tabs, *tabs, sinks, d_att)


def _tri_matmul(tri, t):
    hi = t.astype(BF16)
    r1 = t - hi.astype(F32)
    mid = r1.astype(BF16)
    lo = (r1 - mid.astype(F32)).astype(BF16)
    return _dot(tri, hi, NN) + _dot(tri, mid, NN) + _dot(tri, lo, NN)


def _lower_bound(hl):
    a, b = hl[0:1, :], hl[1:2, :]
    mx = jnp.maximum(a, b)
    ea, eb = jnp.exp(a - mx), jnp.exp(b - mx)
    return ea / (ea + eb)


def _hg_gates(q_raw, f_raw, lb, tri_lower):
    sg = _sig(f_raw)
    f = lb + (1.0 - lb) * sg
    sq = _sig(q_raw)
    b = _tri_matmul(tri_lower, jnp.log(f))
    return sg, f, 1.0 - f, sq, q_raw * sq, b


HG_PAIR_FWD = 8
HG_PAIR_BWD = 8


def _hg_specs(n_map, pair):
    blk = lambda off, p: pl.BlockSpec((HG_TB, LANE), lambda h, n: (n_map(n), off // LANE + pair * h + p))
    return [blk(off, p) for off in (Q_H, F_H, I_H) for p in range(pair)] + [pl.BlockSpec((2, pair * LANE), lambda h, n: (0, h))]


def _interleave(gens):
    out = [None] * len(gens)
    live = list(range(len(gens)))
    while live:
        for k in list(live):
            try:
                next(gens[k])
            except StopIteration as stop:
                out[k] = stop.value
                live.remove(k)
    return out


def _hg_spread():
    c = lax.broadcasted_iota(jnp.int32, (CHUNK, SUB * SUB), 0)
    l = lax.broadcasted_iota(jnp.int32, (CHUNK, SUB * SUB), 1)
    r = lax.broadcasted_iota(jnp.int32, (SUB, SUB * SUB), 0)
    lr = lax.broadcasted_iota(jnp.int32, (SUB, SUB * SUB), 1)
    cols = [(c == lo + (l >> 4)).astype(BF16) for lo in range(0, CHUNK, SUB)]
    tile = [(c == lo + (l & (SUB - 1))).astype(BF16) for lo in range(0, CHUNK, SUB)]
    return cols, tile, (lr & (SUB - 1)) == r, (lr >> 4) == r


def _hg_intra(qs, kk, b, grad=None):
    lane = lax.broadcasted_iota(jnp.int32, (SUB, CHUNK), 1)
    row1 = lax.broadcasted_iota(jnp.int32, (SUB, 1), 0)
    kk_b = kk.astype(BF16)
    if grad is not None:
        d_a, d_at, (cols, tile, diag, block) = grad
    a_blocks, dq_blocks, dk_blocks, db_blocks = [], [], [], []
    dk_left = None
    for j in range(CHUNK // SUB):
        lo = j * SUB
        q_j, k_j, b_j = qs[lo:lo + SUB], kk[lo:lo + SUB], b[lo:lo + SUB]
        es = [jnp.where(row1 >= sx, jnp.exp(jnp.minimum(b_j - b_j[sx:sx + 1], 0.0)), 0.0) for sx in range(SUB)]
        pes = [q_j * e for e in es]
        pe = jnp.concatenate(pes, axis=0).astype(BF16)
        pairs = _dot(pe, kk_b, NT)
        yield
        a_j = jnp.zeros((SUB, CHUNK), F32)
        for sx in range(SUB):
            a_j = jnp.where(lane == lo + sx, pairs[sx * SUB:(sx + 1) * SUB], a_j)
        if grad is not None:
            da_j = d_a[lo:lo + SUB]
            ek = jnp.concatenate([e * k_j[sx:sx + 1] for sx, e in enumerate(es)], axis=0).astype(BF16)
            sel_t = jnp.where(diag, _dot(da_j.astype(BF16), cols[j], NN), 0.0).astype(BF16)
            sel_s = jnp.where(block, _dot(d_at[lo:lo + SUB].astype(BF16), tile[j], NN), 0.0).astype(BF16)
            pek = jnp.concatenate([p * k_j[sx:sx + 1] for sx, p in enumerate(pes)], axis=0).astype(BF16)
            yield
            dq_j = _dot(sel_t, ek, NN)
            dk_j = _dot(sel_s, pe, NN)
            db_j = _dot(sel_t, pek, NN) - _dot(sel_s, pek, NN)
            yield
        if j > 0:
            ref = b[lo - 1:lo]
            sc_q = jnp.exp(b_j - ref)
            sc_k = jnp.exp(jnp.minimum(ref - b, 0.0))
            qt = (q_j * sc_q).astype(BF16)
            kt = (kk * sc_k).astype(BF16)
            left = _dot(qt, kt, NT)
            yield
            a_j = a_j + jnp.where(lane < lo, left, 0.0)
            if grad is not None:
                da_left = jnp.where(lane < lo, da_j, 0.0).astype(BF16)
                dq_left = _dot(da_left, kt, NN) * sc_q
                dq_j = dq_j + dq_left
                db_j = db_j + q_j * dq_left
                t = _dot(da_left, qt, TN)
                yield
                t = t * sc_k
                dk_left = t if dk_left is None else dk_left + t
        a_blocks.append(a_j)
        if grad is not None:
            dq_blocks.append(dq_j)
            dk_blocks.append(dk_j)
            db_blocks.append(db_j)
    a = jnp.concatenate(a_blocks, axis=0)
    if grad is None:
        return a
    return a, jnp.concatenate(dq_blocks, axis=0), jnp.concatenate(dk_blocks, axis=0) + dk_left, jnp.concatenate(db_blocks, axis=0) - kk * dk_left


def _hgrn_fwd(proj, hl):
    s = proj.shape[0]
    n_chunk = HG_TB // CHUNK
    pair = HG_PAIR_FWD

    def body(*refs):
        q_refs, f_refs, i_refs = refs[:pair], refs[pair:2 * pair], refs[2 * pair:3 * pair]
        hl_ref, o_ref, st_out_ref, st_ref = refs[3 * pair:]

        @pl.when(pl.program_id(1) == 0)
        def _():
            st_ref[...] = jnp.zeros_like(st_ref)

        r_i = lax.broadcasted_iota(jnp.int32, (CHUNK, CHUNK), 0)
        c_i = lax.broadcasted_iota(jnp.int32, (CHUNK, CHUNK), 1)
        tri_lower = (r_i >= c_i).astype(BF16)

        def chunk(c, carry):
            rows = pl.ds(pl.multiple_of(c * CHUNK, CHUNK), CHUNK)
            def head(p):
                cols = slice(p * LANE, (p + 1) * LANE)
                lb = _lower_bound(hl_ref[:, cols])
                v = i_refs[p][rows, :].astype(BF16)
                _, _, kk, _, qs, b = _hg_gates(q_refs[p][rows, :], f_refs[p][rows, :], lb, tri_lower)
                yield
                st = st_ref[p]
                st_b = st.astype(BF16)
                st_out_ref[p, c] = st_b
                o_state = _dot((qs * jnp.exp(b)).astype(BF16), st_b, NT)
                b_last = b[CHUNK - 1:CHUNK, :]
                st_new = _dot(v, (kk * jnp.exp(b_last - b)).astype(BF16), TN)
                a = yield from _hg_intra(qs, kk, b)
                st_ref[p] = st * jnp.exp(b_last) + st_new
                o_ref[rows, cols] = o_state + _dot(a.astype(BF16), v, NN)

            _interleave([head(p) for p in range(pair)])
            return carry

        lax.fori_loop(0, n_chunk, chunk, 0)

    return pl.pallas_call(
        body, grid=(HG_HEADS // pair, s // HG_TB), in_specs=_hg_specs(lambda n: n, pair),
        out_specs=[pl.BlockSpec((HG_TB, pair * LANE), lambda h, n: (n, h)), pl.BlockSpec((pair, n_chunk, HG_K, HG_K), lambda h, n: (h, n, 0, 0))],
        out_shape=[_sds((s, HG_W), F32), _sds((HG_HEADS, s // CHUNK, HG_K, HG_K), BF16)],
        scratch_shapes=[pltpu.VMEM((pair, HG_K, HG_K), F32)],
        compiler_params=_params(2), name="hgrn_fwd")(*[proj] * (3 * pair), hl)


def _hgrn_bwd(proj, hl, states, d_o):
    s = proj.shape[0]
    n_chunk = HG_TB // CHUNK
    n_blk = s // HG_TB
    pair = HG_PAIR_BWD
    rev = lambda n: n_blk - 1 - n

    def body(*refs):
        q_refs, f_refs, i_refs = refs[:pair], refs[pair:2 * pair], refs[2 * pair:3 * pair]
        hl_ref, st_in_ref, do_ref, dq_ref, df_ref, di_ref, dhl_ref, dst_ref, dlb_ref = refs[3 * pair:]
        n = pl.program_id(1)

        @pl.when(n == 0)
        def _():
            dst_ref[...] = jnp.zeros_like(dst_ref)
            dlb_ref[...] = jnp.zeros_like(dlb_ref)

        r_i = lax.broadcasted_iota(jnp.int32, (CHUNK, CHUNK), 0)
        c_i = lax.broadcasted_iota(jnp.int32, (CHUNK, CHUNK), 1)
        tri_lower = (r_i >= c_i).astype(BF16)
        tri_upper = (r_i <= c_i).astype(BF16)
        row = lax.broadcasted_iota(jnp.int32, (CHUNK, 1), 0)
        spread = _hg_spread()

        def chunk(cc, carry):
            c = n_chunk - 1 - cc
            rows = pl.ds(pl.multiple_of(c * CHUNK, CHUNK), CHUNK)
            def head(p):
                cols = slice(p * LANE, (p + 1) * LANE)
                lb = _lower_bound(hl_ref[:, cols])
                q_raw = q_refs[p][rows, :]
                vb = i_refs[p][rows, :].astype(BF16)
                sg, f, kk, sq, qs, b = _hg_gates(q_raw, f_refs[p][rows, :], lb, tri_lower)
                yield
                e_b = jnp.exp(b)
                qe = qs * e_b
                b_last = b[CHUNK - 1:CHUNK, :]
                e_last = jnp.exp(b_last)
                e_kd = jnp.exp(b_last - b)
                kd = kk * e_kd
                st0 = st_in_ref[p, c]
                d_ob = do_ref[rows, cols].astype(BF16)
                dst = dst_ref[p]
                dst_b = dst.astype(BF16)
                d_a = jnp.where(r_i >= c_i, _dot(d_ob, vb, NT), 0.0)
                d_at = jnp.where(r_i <= c_i, _dot(vb, d_ob, NT), 0.0)
                d_v_st = _dot(kd.astype(BF16), dst_b, NT)
                d_kd = _dot(vb, dst_b, NN)
                d_qe = _dot(d_ob, st0, NN)
                dst_new = _dot(d_ob, qe.astype(BF16), TN)
                yield
                a, dqs, dkk, d_b = yield from _hg_intra(qs, kk, b, (d_a, d_at, spread))
                d_v = _dot(a.astype(BF16), d_ob, TN) + d_v_st
                dqs_st = d_qe * e_b
                dkk_st = d_kd * e_kd
                dqs = dqs + dqs_st
                dkk = dkk + dkk_st
                d_b_last = jnp.sum(d_kd * kd, axis=0, keepdims=True) + jnp.sum(dst * st0.astype(F32), axis=0, keepdims=True) * e_last
                d_b = d_b + qs * dqs_st - kk * dkk_st + jnp.where(row == CHUNK - 1, d_b_last, 0.0)
                d_g = _tri_matmul(tri_upper, d_b)
                dst_ref[p] = dst_new + dst * e_last
                yield
                d_f = d_g / f - dkk
                dlb_ref[:, cols] += jnp.sum(d_f * (1.0 - sg), axis=0, keepdims=True)
                dq_ref[rows, cols] = (dqs * (sq * (1.0 + q_raw * (1.0 - sq)))).astype(dq_ref.dtype)
                df_ref[rows, cols] = (d_f * (1.0 - lb) * (sg * (1.0 - sg))).astype(df_ref.dtype)
                di_ref[rows, cols] = d_v.astype(di_ref.dtype)

            _interleave([head(p) for p in range(pair)])
            return carry

        lax.fori_loop(0, n_chunk, chunk, 0)

        @pl.when(n == n_blk - 1)
        def _():
            lb = _lower_bound(hl_ref[...])
            d_hl0 = dlb_ref[...] * (lb * (1.0 - lb))
            dhl_ref[...] = jnp.concatenate([d_hl0, -d_hl0], axis=0)

    out_blk = pl.BlockSpec((HG_TB, pair * LANE), lambda h, n: (rev(n), h))
    return pl.pallas_call(
        body, grid=(HG_HEADS // pair, n_blk),
        in_specs=_hg_specs(rev, pair) + [pl.BlockSpec((pair, n_chunk, HG_K, HG_K), lambda h, n: (h, rev(n), 0, 0)), out_blk],
        out_specs=[out_blk, out_blk, out_blk, pl.BlockSpec((2, pair * LANE), lambda h, n: (0, h))],
        out_shape=[_sds((s, HG_W), BF16)] * 3 + [_sds((2, HG_W), F32)],
        scratch_shapes=[pltpu.VMEM((pair, HG_K, HG_K), F32), pltpu.VMEM((1, pair * LANE), F32)],
        compiler_params=_params(2), name="hgrn_bwd")(*[proj] * (3 * pair), hl, states, d_o)


def _mod_part(c_all, w_shard, b_shard):
    n = w_shard.shape[1]
    tn = 512

    def body(c_ref, w_ref, b_ref, o_ref):
        o_ref[...] = _dot(c_ref[...].astype(BF16), w_ref[...].astype(BF16), NN) + b_ref[...]

    return pl.pallas_call(body, grid=(n // tn,),
                          in_specs=[pl.BlockSpec((N_DEV, D), lambda j: (0, 0)), pl.BlockSpec((D, tn), lambda j: (0, j)), pl.BlockSpec((1, tn), lambda j: (0, j))],
                          out_specs=pl.BlockSpec((N_DEV, tn), lambda j: (0, j)), out_shape=_sds((N_DEV, n), F32),
                          compiler_params=_params(1, 32 << 20), name="mod_part")(c_all, w_shard, b_shard)


def _grad_w_ada(c_all_t, dmod_cols):
    n = dmod_cols.shape[1]
    tn = 512

    def body(c_ref, d_ref, o_ref):
        cv = c_ref[...].astype(BF16).astype(F32)
        dv = d_ref[...].astype(BF16).astype(F32)
        acc = cv[:, 0:1] * dv[0:1, :]
        for k in range(1, N_DEV):
            acc = acc + cv[:, k:k + 1] * dv[k:k + 1, :]
        o_ref[...] = acc

    return pl.pallas_call(body, grid=(n // tn,),
                          in_specs=[pl.BlockSpec((D, N_DEV), lambda j: (0, 0)), pl.BlockSpec((N_DEV, tn), lambda j: (0, j))],
                          out_specs=pl.BlockSpec((D, tn), lambda j: (0, j)), out_shape=_sds((D, n), F32),
                          compiler_params=_params(1, 32 << 20), name="grad_w_ada")(c_all_t, dmod_cols)


def _row_tile(r, c, max_elems=1 << 18):
    if r * c <= max_elems or r % 8:
        return r
    best = 8
    for t in range(8, r + 1, 8):
        if r % t == 0 and t * c <= max_elems:
            best = t
    return best


WIDE_TILE = 5 << 17


def _adamw(pieces, w, m, v, name, emit_grad=True, own=None):
    p, r, c = pieces.shape
    tr = _row_tile(r, c)
    c1 = 1.0 / (1.0 - ADAM_B1 ** ADAM_STEP)
    c2 = 1.0 / (1.0 - ADAM_B2 ** ADAM_STEP)

    def body(*refs):
        if own is None:
            p_ref, w_ref, m_ref, v_ref, *outs = refs
            g = p_ref[0].astype(F32)
        else:
            o_ref, p_ref, w_ref, m_ref, v_ref, *outs = refs
            g = o_ref[...].astype(F32) + p_ref[0].astype(F32)
        for k in range(1, p):
            g = g + p_ref[k].astype(F32)
        m2 = ADAM_B1 * m_ref[...] + (1.0 - ADAM_B1) * g
        v2 = ADAM_B2 * v_ref[...] + (1.0 - ADAM_B2) * (g * g)
        delta = -ADAM_LR * ((m2 * c1) / (jnp.sqrt(v2 * c2) + ADAM_EPS) + ADAM_WD * w_ref[...])
        if emit_grad:
            outs[0][...] = g
        outs[-3][...] = delta
        outs[-2][...] = m2
        outs[-1][...] = v2

    blk = pl.BlockSpec((tr, c), lambda i: (i, 0))
    n_out = 4 if emit_grad else 3
    lead = [] if own is None else [own]
    return pl.pallas_call(body, grid=(r // tr,), in_specs=[blk] * len(lead) + [pl.BlockSpec((p, tr, c), lambda i: (0, i, 0)), blk, blk, blk],
                          out_specs=[blk] * n_out, out_shape=[_sds((r, c), F32)] * n_out,
                          compiler_params=_params(1, 48 << 20), name=name)(*lead, pieces, w, m, v)


def _my_coords():
    return lax.axis_index("x"), lax.axis_index("y"), lax.axis_index("c")


def _flip(coords, k):
    x, y, c = coords
    return (1 - x if k & 4 else x, 1 - y if k & 2 else y, 1 - c if k & 1 else c)


def _lin(coords):
    return 4 * coords[0] + 2 * coords[1] + coords[2]


def _exchange_small(x3, bcast, name):
    n = x3.shape[2]

    def body(x_ref, o_ref, send_sems, recv_sems):
        me = _my_coords()
        my_id = _lin(me)
        o_ref[pl.ds(my_id, 1)] = x_ref[pl.ds(0 if bcast else my_id, 1)]
        copies = []
        for k in range(1, N_DEV):
            peer = _flip(me, k)
            src = x_ref.at[0 if bcast else _lin(peer)]
            cp = pltpu.make_async_remote_copy(src_ref=src, dst_ref=o_ref.at[my_id], send_sem=send_sems.at[k], recv_sem=recv_sems.at[k],
                                              device_id=peer, device_id_type=MESH)
            cp.start()
            copies.append(cp)
        for k in range(1, N_DEV):
            peer = _flip(me, k)
            pltpu.make_async_remote_copy(src_ref=x_ref.at[0], dst_ref=o_ref.at[_lin(peer)], send_sem=send_sems.at[k], recv_sem=recv_sems.at[k],
                                         device_id=peer, device_id_type=MESH).wait_recv()
        for cp in copies:
            cp.wait_send()

    vm = pl.BlockSpec(memory_space=pltpu.VMEM)
    return pl.pallas_call(body, in_specs=[vm], out_specs=vm, out_shape=_sds((N_DEV, 1, n), F32),
                          scratch_shapes=[pltpu.SemaphoreType.DMA((N_DEV,)), pltpu.SemaphoreType.DMA((N_DEV,))], name=name)(x3)


HBM_SPEC = pl.BlockSpec(memory_space=pltpu.HBM)
SEM_SPEC = pl.BlockSpec(memory_space=pltpu.SEMAPHORE)
ANY_SPEC = pl.BlockSpec(memory_space=pl.ANY)
DATAFLOW = pltpu.SideEffectType.DATAFLOW_SIDE_EFFECTING
GATHER_FLIPS = (1, 2, 4, 6)
PASS_FLIPS = (2, 4, 6)
TOKEN = (8, LANE)


def _hbm(t):
    return pltpu.with_memory_space_constraint(t, pltpu.HBM)


def _hbm_like(ts):
    return [pltpu.HBM(t.shape, t.dtype) for t in ts]


def _split_start(issue, srcs, lands, n_sem, name, deps=()):
    n, nd = len(srcs), len(deps)

    def body(*refs):
        issue(refs[:n], refs[n:2 * n], refs[2 * n + nd], refs[2 * n + nd + 1])
        refs[-1][...] = jnp.zeros(TOKEN, F32)

    outs = pl.pallas_call(
        body, name=name,
        out_shape=(pltpu.SemaphoreType.DMA((n_sem,)), pltpu.SemaphoreType.DMA((n_sem,)), *_hbm_like(srcs), *_hbm_like(lands), _sds(TOKEN, F32)),
        in_specs=[HBM_SPEC] * (2 * n) + [ANY_SPEC] * nd,
        out_specs=(SEM_SPEC, SEM_SPEC, *[HBM_SPEC] * (2 * n), pl.BlockSpec(memory_space=pltpu.VMEM)),
        input_output_aliases={i: 2 + i for i in range(2 * n)},
        compiler_params=pltpu.CompilerParams(has_side_effects=DATAFLOW))(*[_hbm(t) for t in srcs], *[_hbm(t) for t in lands], *deps)
    return dict(sems=outs[:2], thru=list(outs[2:2 + 2 * n]), token=outs[-1], n=n)


def _split_wait(finish, handle, after, name):
    n = handle["n"]
    thru = handle["thru"]

    def body(*refs):
        finish(refs[:n], refs[n:2 * n], refs[2 * n], refs[2 * n + 1])

    outs = pl.pallas_call(
        body, name=name, out_shape=_hbm_like(thru), in_specs=[HBM_SPEC] * (2 * n) + [SEM_SPEC, SEM_SPEC] + [ANY_SPEC] * len(after),
        out_specs=[HBM_SPEC] * (2 * n), input_output_aliases={i: i for i in range(2 * n)},
        compiler_params=pltpu.CompilerParams(has_side_effects=DATAFLOW))(*thru, *handle["sems"], *after)
    return list(outs[:n]), list(outs[n:])


def _gather_start(shards, name, deps=()):
    n = len(shards)
    my_id = _lin(_my_coords())
    lands = [lax.dynamic_update_slice(lax.empty((N_DEV,) + t.shape, t.dtype), t[None], (my_id, 0, 0)) for t in shards]

    def issue(src, land, send_sems, recv_sems):
        me = _my_coords()
        for w in range(n):
            for j, k in enumerate(GATHER_FLIPS):
                q = len(GATHER_FLIPS) * w + j
                pltpu.make_async_remote_copy(src_ref=src[w], dst_ref=land[w].at[_lin(me)], send_sem=send_sems.at[q], recv_sem=recv_sems.at[q],
                                             device_id=_flip(me, k), device_id_type=MESH).start()

    return _split_start(issue, shards, lands, len(GATHER_FLIPS) * n, name, deps)


def _gather_wait(handle, after, name):
    n = handle["n"]

    def finish(src, land, send_sems, recv_sems):
        me = _my_coords()
        for w in range(n):
            for j, k in enumerate(GATHER_FLIPS):
                q = len(GATHER_FLIPS) * w + j
                peer = _flip(me, k)
                cp = pltpu.make_async_remote_copy(src_ref=src[w], dst_ref=land[w].at[_lin(peer)], send_sem=send_sems.at[q], recv_sem=recv_sems.at[q],
                                                  device_id=peer, device_id_type=MESH)
                cp.wait_send()
                cp.wait_recv()

    return _split_wait(finish, handle, after, name)[1]


def _gather_pass(lands, name):
    n = len(lands)
    n_p = len(PASS_FLIPS)

    def body(*refs):
        land = refs[n:2 * n]
        send_sems, recv_sems = refs[2 * n:]
        me = _my_coords()
        sibling = _flip(me, 1)
        sent = []
        for w in range(n):
            for j, k in enumerate(PASS_FLIPS):
                blk = land[w].at[_lin(_flip(me, k))]
                cp = pltpu.make_async_remote_copy(src_ref=blk, dst_ref=blk, send_sem=send_sems.at[n_p * w + j], recv_sem=recv_sems.at[n_p * w + j],
                                                  device_id=sibling, device_id_type=MESH)
                cp.start()
                sent.append(cp)
        for w in range(n):
            for j, k in enumerate(PASS_FLIPS):
                blk = land[w].at[_lin(_flip(me, k + 1))]
                pltpu.make_async_remote_copy(src_ref=blk, dst_ref=blk, send_sem=send_sems.at[n_p * w + j], recv_sem=recv_sems.at[n_p * w + j],
                                             device_id=sibling, device_id_type=MESH).wait_recv()
        for cp in sent:
            cp.wait_send()

    return pl.pallas_call(body, in_specs=[ANY_SPEC] * n, out_specs=[ANY_SPEC] * n, out_shape=[_sds(t.shape, t.dtype) for t in lands],
                          input_output_aliases={i: i for i in range(n)},
                          scratch_shapes=[pltpu.SemaphoreType.DMA((n_p * n,)), pltpu.SemaphoreType.DMA((n_p * n,))], name=name)(*lands)


CHIP_FLIPS = (0, 2, 4, 6)


def _pair_copy(src, land, send_sems, recv_sems, w, j):
    me = _my_coords()
    q = len(CHIP_FLIPS) * w + j
    return pltpu.make_async_remote_copy(src_ref=src[w].at[_lin(_flip(me, CHIP_FLIPS[j] + 1))], dst_ref=land[w].at[j], send_sem=send_sems.at[q],
                                        recv_sem=recv_sems.at[q], device_id=_flip(me, 1), device_id_type=MESH)


def _pair_exchange(grads, name):
    n = len(grads)

    def body(*refs):
        src, land = refs[:n], refs[n:2 * n]
        send_sems, recv_sems = refs[2 * n:]
        sent = [_pair_copy(src, land, send_sems, recv_sems, w, j) for w in range(n) for j in range(len(CHIP_FLIPS))]
        for cp in sent:
            cp.start()
        for cp in sent:
            cp.wait_recv()
        for cp in sent:
            cp.wait_send()

    outs = pl.pallas_call(body, in_specs=[ANY_SPEC] * n, out_specs=[ANY_SPEC] * n,
                          out_shape=[_sds((len(CHIP_FLIPS),) + g.shape[1:], g.dtype) for g in grads],
                          scratch_shapes=[pltpu.SemaphoreType.DMA((len(CHIP_FLIPS) * n,))] * 2, name=name)(*grads)
    return list(outs)


def _pair_start(grads, name, deps=()):
    n = len(grads)
    lands = [lax.empty((len(CHIP_FLIPS),) + g.shape[1:], g.dtype) for g in grads]

    def issue(src, land, send_sems, recv_sems):
        for w in range(n):
            for j in range(len(CHIP_FLIPS)):
                _pair_copy(src, land, send_sems, recv_sems, w, j).start()

    return _split_start(issue, grads, lands, len(CHIP_FLIPS) * n, name, deps)


def _pair_wait(handle, after, name):
    n = handle["n"]

    def finish(src, land, send_sems, recv_sems):
        for w in range(n):
            for j in range(len(CHIP_FLIPS)):
                cp = _pair_copy(src, land, send_sems, recv_sems, w, j)
                cp.wait_send()
                cp.wait_recv()

    return _split_wait(finish, handle, after, name)


def _pair_add(grad, theirs, name):
    p, r, c = theirs.shape
    tr = _row_tile(r, c, WIDE_TILE)
    me = _my_coords()
    ids = jnp.stack([_lin(_flip(me, k)) for k in CHIP_FLIPS]).astype(jnp.int32)

    def body(ids_ref, a_ref, b_ref, o_ref):
        o_ref[...] = (a_ref[...].astype(F32) + b_ref[...].astype(F32)).astype(o_ref.dtype)

    blk = pl.BlockSpec((None, tr, c), lambda j, i, ids_ref: (j, i, 0))
    return pl.pallas_call(
        body, out_shape=_sds((p, r, c), theirs.dtype), compiler_params=_params(2), name=name,
        grid_spec=pltpu.PrefetchScalarGridSpec(
            num_scalar_prefetch=1, grid=(p, r // tr),
            in_specs=[pl.BlockSpec((None, tr, c), lambda j, i, ids_ref: (ids_ref[j], i, 0)), blk], out_specs=blk))(ids, grad, theirs)


def _chips_start(parts, name, deps=()):
    n = len(parts)
    n_c = len(CHIP_FLIPS) - 1
    lands = [lax.empty((n_c,) + t.shape[1:], t.dtype) for t in parts]

    def issue(src, land, send_sems, recv_sems):
        me = _my_coords()
        for w in range(n):
            for j in range(1, n_c + 1):
                q = n_c * w + j - 1
                pltpu.make_async_remote_copy(src_ref=src[w].at[j], dst_ref=land[w].at[j - 1], send_sem=send_sems.at[q], recv_sem=recv_sems.at[q],
                                             device_id=_flip(me, CHIP_FLIPS[j]), device_id_type=MESH).start()

    return _split_start(issue, parts, lands, n_c * n, name, deps)


def _chips_wait(handle, after, name):
    n = handle["n"]
    n_c = len(CHIP_FLIPS) - 1

    def finish(src, land, send_sems, recv_sems):
        me = _my_coords()
        for w in range(n):
            for j in range(1, n_c + 1):
                q = n_c * w + j - 1
                cp = pltpu.make_async_remote_copy(src_ref=src[w].at[j], dst_ref=land[w].at[j - 1], send_sem=send_sems.at[q], recv_sem=recv_sems.at[q],
                                                  device_id=_flip(me, CHIP_FLIPS[j]), device_id_type=MESH)
                cp.wait_send()
                cp.wait_recv()

    return _split_wait(finish, handle, after, name)


def _after(t, *tokens):
    for tok in tokens:
        t = t + tok[0:1, 0:1]
    return t


def _rope_tables(positions):
    half = ROT // 2
    inv_freq = ROPE_THETA ** (-jnp.arange(0, ROT, 2, dtype=F32) / ROT)
    ang = positions.astype(F32).reshape(-1, 1) * inv_freq
    cos, sin = jnp.cos(ang), jnp.sin(ang)
    s = ang.shape[0]
    pad = jnp.zeros((s, HEAD_DIM - ROT), F32)
    zero = jnp.zeros((s, half), F32)
    two = lambda t: jnp.concatenate([t, t], axis=1)
    return (two(jnp.concatenate([cos, cos, pad + 1.0], axis=1)), two(jnp.concatenate([-sin, zero, pad], axis=1)),
            two(jnp.concatenate([zero, sin, pad], axis=1)))


def _local_step(x, tgt, tabs, mod, sinks_pad, hl, hg_norm, g_pre_mix, g_post_mix, g_pre_ffn, g_post_ffn, weights, scatter, scatter_on):
    s = x.shape[0]
    h1 = _pre_fwd(x, g_pre_mix, mod, 1, 0, "pre_mix_fwd")
    (w_in_t,) = weights("in", h1)
    proj = _mm_nt(h1, w_in_t, s, 256, D, F32, "proj_mm")
    att = _attn_fwd(proj, tabs, sinks_pad)
    o_raw, states = _hgrn_fwd(proj, hl)
    ohg = _hgout_fwd(o_raw, proj, hg_norm)
    w_attn_dm, w_hgrn_dm, w_out = weights("mix", ohg)
    y_a = _mm_nn_dm(att, w_attn_dm, s, F32, "attn_proj_mm")
    y_h = _mm_nn_dm(ohg, w_hgrn_dm, s, F32, "hgrn_proj_mm")
    merged = _merge_fwd(y_a, y_h, proj)
    y = _mm_nn(merged, w_out, s, 512, D, F32, "out_mm")
    x1 = _post_fwd(x, y, g_post_mix, mod, 2, "post_mix_fwd")
    h2 = _pre_fwd(x1, g_pre_ffn, mod, 4, 3, "pre_ffn_fwd")
    (w_ffn_in_dm,) = weights("ffn_in", h2)
    gu = _mm_nn_dm(h2, w_ffn_in_dm, s // 2, F32, "ffn_in_mm")
    act = _swiglu_fwd(gu)
    (w_ffn_out,) = weights("ffn_out", act)
    y2 = _mm_nn(act, w_ffn_out, s, 512, FFN // 4, F32, "ffn_out_mm")
    err, loss = _post_fwd_loss(x1, y2, g_post_ffn, mod, 5, tgt, "post_ffn_loss")
    dy2, d_gate2, dg_post_ffn = _post_bwd(err, y2, g_post_ffn, mod, 5, "post_ffn_bwd")
    gw_ffn_out = _mm_tn(act, dy2, 512, D, BF16, "ffn_out_dw")
    t_pair = scatter([gw_ffn_out.reshape(N_DEV, FFN // N_DEV, D)], "ffn_out")
    d_act = _mm_nt(dy2, w_ffn_out, s, 512, D, F32, "ffn_out_dx", deps=[t_pair])
    dgu = _swiglu_bwd(d_act, gu)
    t_out = scatter_on("ffn_out", dgu)
    gw_ffn_in = _mm_tn_dm(h2, dgu, 1024, BF16, "ffn_in_dw")
    t_pair = scatter([gw_ffn_in], "ffn_in")
    dh2 = _mm_nt_dm(dgu, w_ffn_in_dm, s, 1024, F32, "ffn_in_dx", deps=[t_pair])
    mod = _after(mod, t_out)
    dx1, d_shift2, d_scale2, dg_pre_ffn = _pre_bwd(dh2, x1, err, g_pre_ffn, mod, 4, "pre_ffn_bwd")
    dy, d_gate1, dg_post_mix = _post_bwd(dx1, y, g_post_mix, mod, 2, "post_mix_bwd")
    t_in = scatter_on("ffn_in", dy)
    d_merged = _mm_nt(dy, w_out, s, 512, D, F32, "out_dx")
    gw_out = _mm_tn(merged, dy, 512, D, BF16, "out_dw")
    dy_a, dy_h, d_gate_a, d_gate_h = _merge_bwd(d_merged, y_a, y_h, proj)
    gw_attn = _mm_tn_dm(att, dy_a, ATT_W, BF16, "attn_proj_dw")
    gw_hgrn = _mm_tn_dm(ohg, dy_h, HG_W, BF16, "hgrn_proj_dw")
    t_pair = scatter([gw_attn, gw_hgrn, gw_out.reshape(N_DEV, D // N_DEV, D)], "mix")
    d_att = _mm_nt_dm(dy_a, w_attn_dm, s, ATT_W, F32, "attn_proj_dx")
    d_ohg = _mm_nt_dm(dy_h, w_hgrn_dm, s, HG_W, F32, "hgrn_proj_dx", deps=[t_pair])
    d_o, d_gh, d_hg_norm = _hgout_bwd(d_ohg, o_raw, proj, _after(hg_norm, t_in))
    d_qh, d_fh, d_ih, d_hl = _hgrn_bwd(proj, hl, states, d_o)
    t_mix = scatter_on("mix", d_qh)
    d_qa, d_ka, d_va, d_sinks = _attn_bwd(proj, tabs, _after(sinks_pad, t_mix), d_att)
    d_proj = jnp.concatenate([d_qa, d_ka.astype(BF16), d_va.astype(BF16), d_qh, d_fh, d_ih, d_gh, d_gate_a, d_gate_h], axis=1)
    dh1 = _mm_nn(d_proj, w_in_t, s // 2, 512, IN_COLS // 2, F32, "proj_dx")
    grad_x, d_shift1, d_scale1, dg_pre_mix = _pre_bwd(dh1, x, dx1, g_pre_mix, mod, 1, "pre_mix_bwd")
    d_mod = jnp.concatenate([d_shift1, d_scale1, d_gate1, d_shift2, d_scale2, d_gate2], axis=1)
    small = [d_mod, dg_pre_mix, dg_post_mix, dg_pre_ffn, dg_post_ffn, d_hl.reshape(1, 2 * HG_W), d_hg_norm, d_sinks]
    return loss, grad_x, small, h1, d_proj


def kernel(x, c, positions, w_ada, b_ada, g_pre_mix, g_post_mix, g_pre_ffn, g_post_ffn, w_in, attn_sinks, w_attn_proj, hg_lower_bounds, hg_norm, w_hgrn_proj, w_out, w_ffn_in, w_ffn_out, loss_target, m_w_ada, m_b_ada, m_g_pre_mix, m_g_post_mix, m_g_pre_ffn, m_g_post_ffn, m_w_in, m_attn_sinks, m_w_attn_proj, m_hg_lower_bounds, m_hg_norm, m_w_hgrn_proj, m_w_out, m_w_ffn_in, m_w_ffn_out, v_w_ada, v_b_ada, v_g_pre_mix, v_g_post_mix, v_g_pre_ffn, v_g_post_ffn, v_w_in, v_attn_sinks, v_w_attn_proj, v_hg_lower_bounds, v_hg_norm, v_w_hgrn_proj, v_w_out, v_w_ffn_in, v_w_ffn_out):
    my_id = _lin(_my_coords())
    s = x.shape[1]
    n_ada = w_ada.shape[2]

    c_all = _exchange_small(c.reshape(1, 1, D), True, "gather_c").reshape(N_DEV, D)
    b_cols = lax.dynamic_slice(b_ada, (0, my_id * n_ada), (1, n_ada))
    mod_part = _mod_part(c_all, w_ada[0], b_cols)
    mod = _exchange_small(mod_part.reshape(N_DEV, 1, n_ada), False, "scatter_mod").reshape(1, N_MOD * D)
    groups = {"in": [w_in[0].T], "mix": [w_attn_proj[0], w_hgrn_proj[0], w_out[0]], "ffn_in": [w_ffn_in[0]], "ffn_out": [w_ffn_out[0]]}

    def start(group, dep):
        shards, dep = lax.optimization_barrier((groups[group], dep))
        return _gather_start([t.astype(BF16) for t in shards], "gather_start_" + group, deps=[dep])

    gathers = {"in": start("in", mod)}
    gathers["mix"] = start("mix", gathers["in"]["token"])
    gathers["ffn_in"] = start("ffn_in", gathers["mix"]["token"])
    gathers["ffn_out"] = start("ffn_out", gathers["ffn_in"]["token"])

    def weights(group, after):
        after = [after, gathers["ffn_out"]["token"]]
        lands = _gather_pass(_gather_wait(gathers[group], after, "gather_wait_" + group), "gather_pass_" + group)
        if group == "in":
            return (lands[0].reshape(IN_COLS, D),)
        if group == "mix":
            return lands[0], lands[1], lands[2].reshape(D, D)
        return (lands[0],) if group == "ffn_in" else (lands[0].reshape(FFN, D),)

    pairs, scatters = {}, {}

    def scatter(grads, group):
        pairs[group] = _pair_start(grads, "scatter_pair_" + group)
        return pairs[group]["token"]

    def scatter_on(group, after):
        if group in pairs:
            local, theirs = _pair_wait(pairs[group], [after], "scatter_pair_wait_" + group)
        else:
            local, theirs = after, _pair_exchange(after, "scatter_pair_" + group)
        parts = [_pair_add(g, t, "scatter_pair_add_%s_%d" % (group, k)) for k, (g, t) in enumerate(zip(local, theirs))]
        scatters[group] = _chips_start(parts, "scatter_start_" + group)
        return scatters[group]["token"]

    sinks_pad = jnp.pad(attn_sinks, ((0, 0), (0, LANE - ATT_HEADS)))
    loss, grad_x, small, h1, d_proj = _local_step(
        x[0], loss_target[0], _rope_tables(positions), mod, sinks_pad, hg_lower_bounds, hg_norm, g_pre_mix, g_post_mix, g_pre_ffn, g_post_ffn,
        weights, scatter, scatter_on)
    loss = lax.psum(loss[0, 0], ("x", "y", "c"))

    sizes = [t.shape[1] for t in small]
    parts = _exchange_small(jnp.concatenate(small, axis=1).reshape(1, 1, sum(sizes)), True, "gather_small_grads")
    gw_in = _mm_tn(d_proj, h1, 256, D, BF16, "proj_dw", deps=[parts]).reshape(N_DEV, IN_COLS // N_DEV, D)
    scatter_on("in", [gw_in])
    offs = [sum(sizes[:k]) for k in range(len(sizes))]
    piece = lambda k, n=None: parts[:, :, offs[k]:offs[k] + (sizes[k] if n is None else n)]
    small_w = [(piece(0), b_ada, m_b_ada, v_b_ada), (piece(1), g_pre_mix, m_g_pre_mix, v_g_pre_mix),
               (piece(2), g_post_mix, m_g_post_mix, v_g_post_mix), (piece(3), g_pre_ffn, m_g_pre_ffn, v_g_pre_ffn),
               (piece(4), g_post_ffn, m_g_post_ffn, v_g_post_ffn),
               (piece(5).reshape(N_DEV, 2, HG_W), hg_lower_bounds, m_hg_lower_bounds, v_hg_lower_bounds),
               (piece(6), hg_norm, m_hg_norm, v_hg_norm), (piece(7, ATT_HEADS), attn_sinks, m_attn_sinks, v_attn_sinks)]
    names = ["b_ada", "g_pre_mix", "g_post_mix", "g_pre_ffn", "g_post_ffn", "hg_lower_bounds", "hg_norm", "attn_sinks"]
    res = {n: _adamw(p, w, m, v, "adamw_" + n) for n, (p, w, m, v) in zip(names, small_w)}

    dmod_cols = lax.dynamic_slice(parts.reshape(N_DEV, -1), (0, my_id * n_ada), (N_DEV, n_ada))
    g_w_ada = _grad_w_ada(c_all.T, dmod_cols)
    res["w_ada"] = [g_w_ada] + list(_adamw(g_w_ada[None], w_ada[0], m_w_ada[0], v_w_ada[0], "adamw_w_ada", emit_grad=False))

    big = {"ffn_out": [("w_ffn_out", w_ffn_out, m_w_ffn_out, v_w_ffn_out)], "ffn_in": [("w_ffn_in", w_ffn_in, m_w_ffn_in, v_w_ffn_in)],
           "mix": [("w_attn_proj", w_attn_proj, m_w_attn_proj, v_w_attn_proj), ("w_hgrn_proj", w_hgrn_proj, m_w_hgrn_proj, v_w_hgrn_proj),
                   ("w_out", w_out, m_w_out, v_w_out)],
           "in": [("w_in", w_in, m_w_in, v_w_in)]}
    after = [scatters["in"]["token"]]
    for group, members in big.items():
        if group == "in":
            after = [res[n][1] for n in res]
        local, lands = _chips_wait(scatters[group], after, "scatter_wait_" + group)
        own = [t[0] for t in local]
        for (n, w, m, v), g_own, land in zip(members, own, lands):
            if group == "in":
                res[n] = [t.T for t in _adamw(land, w[0].T, m[0].T, v[0].T, "adamw_" + n, own=g_own)]
            else:
                res[n] = _adamw(land, w[0], m[0], v[0], "adamw_" + n, own=g_own)
            after = after + [res[n][1]]

    order = ["w_ada", "b_ada", "g_pre_mix", "g_post_mix", "g_pre_ffn", "g_post_ffn", "w_in", "attn_sinks", "w_attn_proj",
             "hg_lower_bounds", "hg_norm", "w_hgrn_proj", "w_out", "w_ffn_in", "w_ffn_out"]
    lead = {"w_ada", "w_in", "w_attn_proj", "w_hgrn_proj", "w_out", "w_ffn_in", "w_ffn_out"}
    outs = [loss, grad_x[None]]
    for k in range(4):
        outs += [res[n][k][None] if n in lead else res[n][k] for n in order]
    return tuple(outs)
```

```python
import functools

import jax
import jax.numpy as jnp
from jax import lax
from jax.experimental import pallas as pl
from jax.experimental.pallas import tpu as pltpu

F32 = jnp.float32
BF16 = jnp.bfloat16

N_DEV = 8
D = 2048
ATT_HEADS = 16
KV_HEADS = 2
HEAD_DIM = 64
GROUP = ATT_HEADS // KV_HEADS
ATT_W = ATT_HEADS * HEAD_DIM
BLK = 128
ROT = HEAD_DIM // 4
ROPE_THETA = 500000.0
HG_HEADS = 8
HG_K = 128
HG_W = HG_HEADS * HG_K
CHUNK = 64
SUB = 16
FFN = 5632
N_MOD = 6
EPS = 1e-6
LANE = 128
Q_A, K_A, V_A, Q_H, F_H, I_H, G_H, GT_A, GT_H, IN_COLS = 0, 1024, 1152, 1280, 2304, 3328, 4352, 5376, 7424, 9472

ADAM_LR, ADAM_B1, ADAM_B2, ADAM_EPS, ADAM_WD, ADAM_STEP = 0.001, 0.9, 0.999, 1e-08, 0.01, 10

TR = 256
HG_TB = 512
VMEM_BIG = 56 << 20
MESH = pl.DeviceIdType.MESH


def _sds(shape, dtype):
    return jax.ShapeDtypeStruct(shape, dtype)


def _params(n_axes, vmem=None):
    return pltpu.CompilerParams(dimension_semantics=("arbitrary",) * n_axes, vmem_limit_bytes=vmem)


def _sig(t):
    return 1.0 / (1.0 + jnp.exp(-t))


def _dot(a, b, dims):
    return lax.dot_general(a, b, (dims, ((), ())), preferred_element_type=F32)


NN = ((1,), (0,))
NT = ((1,), (1,))
TN = ((0,), (0,))


def _matmul(a, b, a_spec, b_spec, o_spec, out_shape, grid, dims, acc_shape, name, deps=()):
    nk = grid[2]
    nd = len(deps)

    def body(a_ref, b_ref, *rest):
        o_ref, scratch = rest[nd], rest[nd + 1:]
        part = _dot(a_ref[...], b_ref[...], dims)
        if nk == 1:
            o_ref[...] = part.astype(o_ref.dtype)
        else:
            acc = scratch[0]
            k = pl.program_id(2)

            @pl.when(k == 0)
            def _():
                acc[...] = part

            @pl.when(k > 0)
            def _():
                acc[...] += part

            @pl.when(k == nk - 1)
            def _():
                o_ref[...] = acc[...].astype(o_ref.dtype)

    return pl.pallas_call(
        body, grid=grid, in_specs=[a_spec, b_spec] + [pl.BlockSpec(memory_space=pl.ANY)] * nd, out_specs=o_spec, out_shape=out_shape,
        scratch_shapes=[pltpu.VMEM(acc_shape, F32)] if nk > 1 else [],
        compiler_params=_params(3, VMEM_BIG), name=name)(a, b, *deps)


def _mm_nn(a, b, tm, tn, tk, out_dtype, name):
    m, k = a.shape
    n = b.shape[1]
    return _matmul(a, b, pl.BlockSpec((tm, tk), lambda j, i, kk: (i, kk)), pl.BlockSpec((tk, tn), lambda j, i, kk: (kk, j)),
                   pl.BlockSpec((tm, tn), lambda j, i, kk: (i, j)), _sds((m, n), out_dtype),
                   (n // tn, m // tm, k // tk), NN, (tm, tn), name)


def _mm_nn_dm(a, b, tm, out_dtype, name):
    m, k = a.shape
    n = b.shape[2]
    return _matmul(a, b, pl.BlockSpec((tm, k), lambda j, i, kk: (i, 0)), pl.BlockSpec((None, k, n), lambda j, i, kk: (j, 0, 0)),
                   pl.BlockSpec((tm, n), lambda j, i, kk: (i, j)), _sds((m, N_DEV * n), out_dtype),
                   (N_DEV, m // tm, 1), NN, (tm, n), name)


def _mm_nt(a, b, tm, tn, tk, out_dtype, name, deps=()):
    m, k = a.shape
    n = b.shape[0]
    return _matmul(a, b, pl.BlockSpec((tm, tk), lambda j, i, kk: (i, kk)), pl.BlockSpec((tn, tk), lambda j, i, kk: (j, kk)),
                   pl.BlockSpec((tm, tn), lambda j, i, kk: (i, j)), _sds((m, n), out_dtype),
                   (n // tn, m // tm, k // tk), NT, (tm, tn), name, deps)


def _mm_nt_dm(a, b, tm, tn, out_dtype, name, deps=()):
    m = a.shape[0]
    n_out, n = b.shape[1], b.shape[2]
    return _matmul(a, b, pl.BlockSpec((tm, n), lambda j, i, kk: (i, kk)), pl.BlockSpec((None, tn, n), lambda j, i, kk: (kk, j, 0)),
                   pl.BlockSpec((tm, tn), lambda j, i, kk: (i, j)), _sds((m, n_out), out_dtype),
                   (n_out // tn, m // tm, N_DEV), NT, (tm, tn), name, deps)


def _mm_tn(a, b, tm, tn, out_dtype, name, deps=()):
    s, m = a.shape
    n = b.shape[1]
    return _matmul(a, b, pl.BlockSpec((s, tm), lambda j, i, kk: (0, i)), pl.BlockSpec((s, tn), lambda j, i, kk: (0, j)),
                   pl.BlockSpec((tm, tn), lambda j, i, kk: (i, j)), _sds((m, n), out_dtype),
                   (n // tn, m // tm, 1), TN, (tm, tn), name, deps)


def _mm_tn_dm(a, b, tm, out_dtype, name):
    s, m = a.shape
    n = b.shape[1] // N_DEV
    return _matmul(a, b, pl.BlockSpec((s, tm), lambda j, i, kk: (0, i)), pl.BlockSpec((s, n), lambda j, i, kk: (0, j)),
                   pl.BlockSpec((None, tm, n), lambda j, i, kk: (j, i, 0)), _sds((N_DEV, m, n), out_dtype),
                   (N_DEV, m // tm, 1), TN, (tm, n), name)


def _row_spec():
    return pl.BlockSpec((TR, D), lambda i: (i, 0))


def _vec_spec(k=0):
    return pl.BlockSpec((1, D), lambda i: (0, k))


def _acc_rows(ref, first, val):
    @pl.when(first)
    def _():
        ref[...] = val

    @pl.when(jnp.logical_not(first))
    def _():
        ref[...] += val


def _pre_fwd(x, g, mod, k_scale, k_shift, name):
    s = x.shape[0]

    def body(x_ref, g_ref, sc_ref, sh_ref, h_ref):
        xv = x_ref[...]
        r = lax.rsqrt(jnp.mean(xv * xv, axis=-1, keepdims=True) + EPS)
        n = xv * r * g_ref[...]
        h_ref[...] = (n * (1.0 + sc_ref[...]) + sh_ref[...]).astype(h_ref.dtype)

    return pl.pallas_call(body, grid=(s // TR,), in_specs=[_row_spec(), _vec_spec(), _vec_spec(k_scale), _vec_spec(k_shift)],
                          out_specs=_row_spec(), out_shape=_sds((s, D), BF16), compiler_params=_params(1), name=name)(x, g, mod, mod)


def _post_fwd(x, y, g, mod, k_gate, name):
    s = x.shape[0]

    def body(x_ref, y_ref, g_ref, gt_ref, o_ref):
        yv = y_ref[...]
        r = lax.rsqrt(jnp.mean(yv * yv, axis=-1, keepdims=True) + EPS)
        o_ref[...] = x_ref[...] + gt_ref[...] * (yv * r * g_ref[...])

    return pl.pallas_call(body, grid=(s // TR,), in_specs=[_row_spec(), _row_spec(), _vec_spec(), _vec_spec(k_gate)],
                          out_specs=_row_spec(), out_shape=_sds((s, D), F32), compiler_params=_params(1), name=name)(x, y, g, mod)


def _post_fwd_loss(x, y, g, mod, k_gate, tgt, name):
    s = x.shape[0]

    def body(x_ref, y_ref, g_ref, gt_ref, t_ref, e_ref, loss_ref):
        i = pl.program_id(0)
        yv = y_ref[...]
        r = lax.rsqrt(jnp.mean(yv * yv, axis=-1, keepdims=True) + EPS)
        err = x_ref[...] + gt_ref[...] * (yv * r * g_ref[...]) - t_ref[...]
        e_ref[...] = err * (1.0 / D)
        part = 0.5 * jnp.sum(jnp.mean(err * err, axis=-1, keepdims=True), axis=0, keepdims=True)
        _acc_rows(loss_ref, i == 0, part)

    return pl.pallas_call(body, grid=(s // TR,),
                          in_specs=[_row_spec(), _row_spec(), _vec_spec(), _vec_spec(k_gate), _row_spec()],
                          out_specs=[_row_spec(), pl.BlockSpec((1, 1), lambda i: (0, 0))],
                          out_shape=[_sds((s, D), F32), _sds((1, 1), F32)], compiler_params=_params(1), name=name)(x, y, g, mod, tgt)


def _pre_bwd(dh, x, res, g, mod, k_scale, name):
    s = x.shape[0]

    def body(dh_ref, x_ref, res_ref, g_ref, sc_ref, dx_ref, dsh_ref, dsc_ref, dg_ref):
        first = pl.program_id(0) == 0
        xv, dh_v, gv = x_ref[...], dh_ref[...], g_ref[...]
        r = lax.rsqrt(jnp.mean(xv * xv, axis=-1, keepdims=True) + EPS)
        xh = xv * r
        dn = dh_v * (1.0 + sc_ref[...])
        dgn = dn * gv
        dx_ref[...] = res_ref[...] + r * (dgn - xh * jnp.mean(dgn * xh, axis=-1, keepdims=True))
        _acc_rows(dsh_ref, first, jnp.sum(dh_v, axis=0, keepdims=True))
        _acc_rows(dsc_ref, first, jnp.sum(dh_v * (xh * gv), axis=0, keepdims=True))
        _acc_rows(dg_ref, first, jnp.sum(dn * xh, axis=0, keepdims=True))

    return pl.pallas_call(body, grid=(s // TR,),
                          in_specs=[_row_spec(), _row_spec(), _row_spec(), _vec_spec(), _vec_spec(k_scale)],
                          out_specs=[_row_spec(), _vec_spec(), _vec_spec(), _vec_spec()],
                          out_shape=[_sds((s, D), F32)] + [_sds((1, D), F32)] * 3,
                          compiler_params=_params(1), name=name)(dh, x, res, g, mod)


def _post_bwd(dx, y, g, mod, k_gate, name):
    s = y.shape[0]

    def body(dx_ref, y_ref, g_ref, gt_ref, dy_ref, dgt_ref, dg_ref):
        first = pl.program_id(0) == 0
        yv, dxv, gv = y_ref[...], dx_ref[...], g_ref[...]
        r = lax.rsqrt(jnp.mean(yv * yv, axis=-1, keepdims=True) + EPS)
        yh = yv * r
        dn = dxv * gt_ref[...]
        dgn = dn * gv
        dy_ref[...] = (r * (dgn - yh * jnp.mean(dgn * yh, axis=-1, keepdims=True))).astype(dy_ref.dtype)
        _acc_rows(dgt_ref, first, jnp.sum(dxv * (yh * gv), axis=0, keepdims=True))
        _acc_rows(dg_ref, first, jnp.sum(dn * yh, axis=0, keepdims=True))

    return pl.pallas_call(body, grid=(s // TR,), in_specs=[_row_spec(), _row_spec(), _vec_spec(), _vec_spec(k_gate)],
                          out_specs=[_row_spec(), _vec_spec(), _vec_spec()],
                          out_shape=[_sds((s, D), BF16), _sds((1, D), F32), _sds((1, D), F32)],
                          compiler_params=_params(1), name=name)(dx, y, g, mod)


SW_TN = 1408
SW_TR = 512
TALL = 1024


def _swiglu_fwd(gu):
    s = gu.shape[0]
    nb = FFN // SW_TN

    def body(g_ref, u_ref, a_ref):
        gv = g_ref[...]
        a_ref[...] = (gv * _sig(gv) * u_ref[...]).astype(a_ref.dtype)

    return pl.pallas_call(body, grid=(s // SW_TR, nb),
                          in_specs=[pl.BlockSpec((SW_TR, SW_TN), lambda i, j: (i, j)), pl.BlockSpec((SW_TR, SW_TN), lambda i, j: (i, j + nb))],
                          out_specs=pl.BlockSpec((SW_TR, SW_TN), lambda i, j: (i, j)), out_shape=_sds((s, FFN), BF16),
                          compiler_params=_params(2, 48 << 20), name="swiglu_fwd")(gu, gu)


def _swiglu_bwd(dact, gu):
    s = gu.shape[0]
    nb = FFN // SW_TN
    n_steps = (s // SW_TR) * nb

    def body(da_ref, g_ref, u_ref, o_ref, buf, sems):
        i, j = pl.program_id(0), pl.program_id(1)
        step = i * nb + j
        slot = step % 2

        def tiles(sl):
            rows = pl.ds(pl.multiple_of(i * SW_TR, SW_TR), SW_TR)
            return [pltpu.make_async_copy(buf.at[sl, h], o_ref.at[rows, pl.ds(pl.multiple_of((j + nb * h) * SW_TN, LANE), SW_TN)], sems.at[sl, h])
                    for h in range(2)]

        @pl.when(step >= 2)
        def _():
            for cp in tiles(slot):
                cp.wait()

        gv, da = g_ref[...], da_ref[...]
        sg = _sig(gv)
        buf[slot, 0] = (da * u_ref[...] * (sg * (1.0 + gv * (1.0 - sg)))).astype(buf.dtype)
        buf[slot, 1] = (da * (gv * sg)).astype(buf.dtype)
        for cp in tiles(slot):
            cp.start()

        @pl.when(step == n_steps - 1)
        def _():
            for cp in tiles(slot) + (tiles(1 - slot) if n_steps > 1 else []):
                cp.wait()

    blk = lambda f: pl.BlockSpec((SW_TR, SW_TN), f)
    return pl.pallas_call(body, grid=(s // SW_TR, nb),
                          in_specs=[blk(lambda i, j: (i, j)), blk(lambda i, j: (i, j)), blk(lambda i, j: (i, j + nb))],
                          out_specs=pl.BlockSpec(memory_space=pl.ANY), out_shape=_sds((s, 2 * FFN), BF16),
                          scratch_shapes=[pltpu.VMEM((2, 2, SW_TR, SW_TN), BF16), pltpu.SemaphoreType.DMA((2, 2))],
                          compiler_params=_params(2, 48 << 20), name="swiglu_bwd")(dact, gu, gu)


MG_TN = 256


def _merge_fwd(y_a, y_h, proj):
    s = y_a.shape[0]
    tn = MG_TN
    ba, bh = GT_A // tn, GT_H // tn

    def body(ya_ref, yh_ref, ga_ref, gh_ref, m_ref):
        m_ref[...] = (_sig(ga_ref[...]) * ya_ref[...] + _sig(gh_ref[...]) * yh_ref[...]).astype(m_ref.dtype)

    tr = min(s, TALL)
    blk = lambda f: pl.BlockSpec((tr, tn), f)
    return pl.pallas_call(body, grid=(s // tr, D // tn),
                          in_specs=[blk(lambda i, j: (i, j)), blk(lambda i, j: (i, j)), blk(lambda i, j: (i, j + ba)), blk(lambda i, j: (i, j + bh))],
                          out_specs=blk(lambda i, j: (i, j)), out_shape=_sds((s, D), BF16),
                          compiler_params=_params(2), name="merge_fwd")(y_a, y_h, proj, proj)


def _merge_bwd(dm, y_a, y_h, proj):
    s = y_a.shape[0]
    tn = MG_TN
    ba, bh = GT_A // tn, GT_H // tn

    def body(dm_ref, ya_ref, yh_ref, ga_ref, gh_ref, dya_ref, dyh_ref, dga_ref, dgh_ref):
        dmv = dm_ref[...]
        sa, sh = _sig(ga_ref[...]), _sig(gh_ref[...])
        dya_ref[...] = (dmv * sa).astype(BF16)
        dyh_ref[...] = (dmv * sh).astype(BF16)
        dga_ref[...] = (dmv * ya_ref[...] * (sa * (1.0 - sa))).astype(BF16)
        dgh_ref[...] = (dmv * yh_ref[...] * (sh * (1.0 - sh))).astype(BF16)

    tr = min(s, TALL)
    blk = lambda f: pl.BlockSpec((tr, tn), f)
    nat = blk(lambda i, j: (i, j))
    return pl.pallas_call(body, grid=(s // tr, D // tn),
                          in_specs=[nat, nat, nat, blk(lambda i, j: (i, j + ba)), blk(lambda i, j: (i, j + bh))],
                          out_specs=[nat] * 4, out_shape=[_sds((s, D), BF16)] * 4,
                          compiler_params=_params(2), name="merge_bwd")(dm, y_a, y_h, proj, proj)


def _hgout_fwd(o_raw, proj, hg_norm):
    s = o_raw.shape[0]
    bg = G_H // LANE

    def body(o_ref, g_ref, n_ref, out_ref):
        ov = o_ref[...]
        r = lax.rsqrt(jnp.mean(ov * ov, axis=-1, keepdims=True) + EPS)
        out_ref[...] = (ov * r * n_ref[...] * _sig(g_ref[...])).astype(out_ref.dtype)

    tr = min(s, TALL)
    blk = lambda f: pl.BlockSpec((tr, LANE), f)
    return pl.pallas_call(body, grid=(s // tr, HG_HEADS),
                          in_specs=[blk(lambda i, h: (i, h)), blk(lambda i, h: (i, h + bg)), pl.BlockSpec((1, LANE), lambda i, h: (0, 0))],
                          out_specs=blk(lambda i, h: (i, h)), out_shape=_sds((s, HG_W), BF16),
                          compiler_params=_params(2), name="hgout_fwd")(o_raw, proj, hg_norm)


def _hgout_bwd(d_out, o_raw, proj, hg_norm):
    s = o_raw.shape[0]
    bg = G_H // LANE

    def body(d_ref, o_ref, g_ref, n_ref, do_ref, dg_ref, dn_ref):
        first = jnp.logical_and(pl.program_id(0) == 0, pl.program_id(1) == 0)
        ov, dv, nv = o_ref[...], d_ref[...], n_ref[...]
        sg = _sig(g_ref[...])
        r = lax.rsqrt(jnp.mean(ov * ov, axis=-1, keepdims=True) + EPS)
        oh = ov * r
        d_on = dv * sg
        dg_ref[...] = (dv * (oh * nv) * (sg * (1.0 - sg))).astype(dg_ref.dtype)
        t = d_on * nv
        do_ref[...] = r * (t - oh * jnp.mean(t * oh, axis=-1, keepdims=True))
        _acc_rows(dn_ref, first, jnp.sum(d_on * oh, axis=0, keepdims=True))

    tr = min(s, TALL)
    blk = lambda f: pl.BlockSpec((tr, LANE), f)
    vec = pl.BlockSpec((1, LANE), lambda i, h: (0, 0))
    return pl.pallas_call(body, grid=(s // tr, HG_HEADS),
                          in_specs=[blk(lambda i, h: (i, h)), blk(lambda i, h: (i, h)), blk(lambda i, h: (i, h + bg)), vec],
                          out_specs=[blk(lambda i, h: (i, h)), blk(lambda i, h: (i, h)), vec],
                          out_shape=[_sds((s, HG_W), F32), _sds((s, HG_W), BF16), _sds((1, LANE), F32)],
                          compiler_params=_params(2), name="hgout_bwd")(d_out, o_raw, proj, hg_norm)


def _rope(t, cos, s_lo, s_hi):
    return t * cos + pltpu.roll(t, LANE - ROT // 2, 1) * s_lo + pltpu.roll(t, ROT // 2, 1) * s_hi


def _rope_wide(t, cos, s_lo, s_hi):
    return jnp.concatenate([_rope(t[:, k * LANE:(k + 1) * LANE], cos, s_lo, s_hi) for k in range(t.shape[1] // LANE)], axis=1)


def _attn_mask(has_prev):
    kj = lax.broadcasted_iota(jnp.int32, (2 * BLK, BLK), 0)
    qi = lax.broadcasted_iota(jnp.int32, (2 * BLK, BLK), 1)
    rel = BLK + qi - kj
    band = jnp.logical_and(rel >= 0, rel < BLK)
    return jnp.logical_and(band, jnp.logical_or(has_prev, kj >= BLK))


def _attn_specs():
    prev = lambda i: jnp.maximum(i - 1, 0)
    kb, vb = K_A // LANE, V_A // LANE
    blk = lambda f: pl.BlockSpec((BLK, LANE), f)
    tabs = [blk(lambda i: (i, 0))] * 3 + [blk(lambda i: (prev(i), 0))] * 3
    return [pl.BlockSpec((BLK, ATT_W), lambda i: (i, 0)), blk(lambda i: (i, kb)), blk(lambda i: (prev(i), kb)),
            blk(lambda i: (i, vb)), blk(lambda i: (prev(i), vb))] + tabs + [pl.BlockSpec((1, LANE), lambda i: (0, 0))]


def _attn_logits(qh, kg):
    return _dot(kg, qh, NT)


def _attn_probs(raw, mask, sk):
    logits = jnp.where(mask, raw * (HEAD_DIM ** -0.5), -jnp.inf)
    m = jnp.maximum(jnp.max(logits, axis=0, keepdims=True), sk)
    p = jnp.exp(logits - m)
    e_sink = jnp.exp(sk - m)
    inv = 1.0 / (jnp.sum(p, axis=0, keepdims=True) + e_sink)
    return p, inv, e_sink * inv


def _attn_fwd(proj, tabs, sinks):
    s = proj.shape[0]

    def body(q_ref, kc_ref, kp_ref, vc_ref, vp_ref, c0, l0, h0, c1, l1, h1, sk_ref, o_ref):
        i = pl.program_id(0)
        mask = _attn_mask(i > 0)
        q = _rope_wide(q_ref[...], c0[...], l0[...], h0[...]).astype(BF16)
        kk = jnp.concatenate([_rope(kp_ref[...], c1[...], l1[...], h1[...]), _rope(kc_ref[...], c0[...], l0[...], h0[...])], axis=0).astype(BF16)
        v_t = jnp.concatenate([vp_ref[...], vc_ref[...]], axis=0).T.astype(BF16)
        part = lambda t, h: t[:, h * HEAD_DIM:(h + 1) * HEAD_DIM]
        k_heads = [part(kk, g) for g in range(KV_HEADS)]

        def head(h):
            g = h // GROUP
            raw = _attn_logits(part(q, h), k_heads[g])
            yield
            p, inv, _ = _attn_probs(raw, mask, sk_ref[:, h:h + 1])
            yield
            out_t = _dot(v_t[g * HEAD_DIM:(g + 1) * HEAD_DIM], p.astype(BF16), NN)
            yield
            return out_t * inv

        o_ref[...] = jnp.concatenate(_interleave([head(h) for h in range(ATT_HEADS)]), axis=0).T.astype(o_ref.dtype)

    return pl.pallas_call(body, grid=(s // BLK,), in_specs=_attn_specs(),
                          out_specs=pl.BlockSpec((BLK, ATT_W), lambda i: (i, 0)), out_shape=_sds((s, ATT_W), BF16),
                          compiler_params=_params(1), name="attn_fwd")(proj, proj, proj, proj, proj, *tabs, *tabs, sinks)


def _attn_bwd(proj, tabs, sinks, d_att):
    s = proj.shape[0]

    def body(q_ref, kc_ref, kp_ref, vc_ref, vp_ref, c0, l0, h0, c1, l1, h1, sk_ref, do_ref, dq_ref, dk_ref, dv_ref, ds_ref):
        i = pl.program_id(0)

        @pl.when(i == 0)
        def _():
            dk_ref[...] = jnp.zeros_like(dk_ref)
            dv_ref[...] = jnp.zeros_like(dv_ref)
            ds_ref[...] = jnp.zeros_like(ds_ref)

        mask = _attn_mask(i > 0)
        q = _rope_wide(q_ref[...], c0[...], l0[...], h0[...]).astype(BF16)
        kk = jnp.concatenate([_rope(kp_ref[...], c1[...], l1[...], h1[...]), _rope(kc_ref[...], c0[...], l0[...], h0[...])], axis=0).astype(BF16)
        k_f32 = jnp.concatenate([_rope(kp_ref[...], c1[...], l1[...], h1[...]), _rope(kc_ref[...], c0[...], l0[...], h0[...])], axis=0)
        k_t = k_f32.T.astype(BF16)
        vv = jnp.concatenate([vp_ref[...], vc_ref[...]], axis=0).astype(BF16)
        d_o = do_ref[...].astype(BF16)
        lane = lax.broadcasted_iota(jnp.int32, (1, LANE), 1)
        part = lambda t, h: t[:, h * HEAD_DIM:(h + 1) * HEAD_DIM]
        k_heads = [part(kk, g) for g in range(KV_HEADS)]
        v_heads = [part(vv, g) for g in range(KV_HEADS)]

        def head(h):
            g = h // GROUP
            qh, doh = part(q, h), part(d_o, h)
            raw = _attn_logits(qh, k_heads[g])
            d_p = _dot(v_heads[g], doh, NT)
            yield
            p, inv, p_sink = _attn_probs(raw, mask, sk_ref[:, h:h + 1])
            prob = p * inv
            dv = _dot(prob.astype(BF16), doh, NN)
            yield
            dd = jnp.sum(prob * d_p, axis=0, keepdims=True)
            d_s = (prob * (d_p - dd)).astype(BF16)
            d_sink = jnp.where(lane == h, -jnp.sum(p_sink * dd, axis=1, keepdims=True), 0.0)
            dq_t = _dot(k_t[g * HEAD_DIM:(g + 1) * HEAD_DIM], d_s, NN)
            dk = _dot(d_s, qh, NN)
            yield
            return dq_t * (HEAD_DIM ** -0.5), dk * (HEAD_DIM ** -0.5), dv, d_sink

        per_head = _interleave([head(h) for h in range(ATT_HEADS)])
        dqs = [jnp.concatenate([t[0] for t in per_head], axis=0).T]
        group_sum = lambda k, g: functools.reduce(jnp.add, [t[k] for t in per_head[g * GROUP:(g + 1) * GROUP]])
        dks = [group_sum(1, g) for g in range(KV_HEADS)]
        dvs = [group_sum(2, g) for g in range(KV_HEADS)]
        d_sink = functools.reduce(jnp.add, [t[3] for t in per_head])
        dq_ref[...] = _rope_wide(jnp.concatenate(dqs, axis=1), c0[...], -l0[...], -h0[...]).astype(dq_ref.dtype)
        d_k = jnp.concatenate(dks, axis=1)
        d_v = jnp.concatenate(dvs, axis=1)
        cur = pl.ds(pl.multiple_of(i * BLK, BLK), BLK)
        prv = pl.ds(pl.multiple_of(jnp.maximum(i - 1, 0) * BLK, BLK), BLK)
        dk_ref[prv, :] += _rope(d_k[:BLK], c1[...], -l1[...], -h1[...])
        dk_ref[cur, :] += _rope(d_k[BLK:], c0[...], -l0[...], -h0[...])
        dv_ref[prv, :] += d_v[:BLK]
        dv_ref[cur, :] += d_v[BLK:]
        ds_ref[...] += d_sink

    full = pl.BlockSpec((s, LANE), lambda i: (0, 0))
    return pl.pallas_call(body, grid=(s // BLK,), in_specs=_attn_specs() + [pl.BlockSpec((BLK, ATT_W), lambda i: (i, 0))],
                          out_specs=[pl.BlockSpec((BLK, ATT_W), lambda i: (i, 0)), full, full, pl.BlockSpec((1, LANE), lambda i: (0, 0))],
                          out_shape=[_sds((s, ATT_W), BF16), _sds((s, LANE), F32), _sds((s, LANE), F32), _sds((1, LANE), F32)],
                          compiler_params=_params(1), name="attn_bwd")(proj, proj, proj, proj, proj, *tabs, *tabs, sinks, d_att)


def _tri_matmul(tri, t):
    hi = t.astype(BF16)
    r1 = t - hi.astype(F32)
    mid = r1.astype(BF16)
    lo = (r1 - mid.astype(F32)).astype(BF16)
    return _dot(tri, hi, NN) + _dot(tri, mid, NN) + _dot(tri, lo, NN)


def _lower_bound(hl):
    a, b = hl[0:1, :], hl[1:2, :]
    mx = jnp.maximum(a, b)
    ea, eb = jnp.exp(a - mx), jnp.exp(b - mx)
    return ea / (ea + eb)


def _hg_gates(q_raw, f_raw, lb, tri_lower):
    sg = _sig(f_raw)
    f = lb + (1.0 - lb) * sg
    sq = _sig(q_raw)
    b = _tri_matmul(tri_lower, jnp.log(f))
    return sg, f, 1.0 - f, sq, q_raw * sq, b


HG_PAIR_FWD = 8
HG_PAIR_BWD = 8


def _hg_specs(n_map, pair):
    blk = lambda off, p: pl.BlockSpec((HG_TB, LANE), lambda h, n: (n_map(n), off // LANE + pair * h + p))
    return [blk(off, p) for off in (Q_H, F_H, I_H) for p in range(pair)] + [pl.BlockSpec((2, pair * LANE), lambda h, n: (0, h))]


def _interleave(gens):
    out = [None] * len(gens)
    live = list(range(len(gens)))
    while live:
        for k in list(live):
            try:
                next(gens[k])
            except StopIteration as stop:
                out[k] = stop.value
                live.remove(k)
    return out


def _hg_spread():
    c = lax.broadcasted_iota(jnp.int32, (CHUNK, SUB * SUB), 0)
    l = lax.broadcasted_iota(jnp.int32, (CHUNK, SUB * SUB), 1)
    r = lax.broadcasted_iota(jnp.int32, (SUB, SUB * SUB), 0)
    lr = lax.broadcasted_iota(jnp.int32, (SUB, SUB * SUB), 1)
    cols = [(c == lo + (l >> 4)).astype(BF16) for lo in range(0, CHUNK, SUB)]
    tile = [(c == lo + (l & (SUB - 1))).astype(BF16) for lo in range(0, CHUNK, SUB)]
    return cols, tile, (lr & (SUB - 1)) == r, (lr >> 4) == r


def _hg_intra(qs, kk, b, grad=None):
    lane = lax.broadcasted_iota(jnp.int32, (SUB, CHUNK), 1)
    row1 = lax.broadcasted_iota(jnp.int32, (SUB, 1), 0)
    kk_b = kk.astype(BF16)
    if grad is not None:
        d_a, d_at, (cols, tile, diag, block) = grad
    a_blocks, dq_blocks, dk_blocks, db_blocks = [], [], [], []
    dk_left = None
    for j in range(CHUNK // SUB):
        lo = j * SUB
        q_j, k_j, b_j = qs[lo:lo + SUB], kk[lo:lo + SUB], b[lo:lo + SUB]
        es = [jnp.where(row1 >= sx, jnp.exp(jnp.minimum(b_j - b_j[sx:sx + 1], 0.0)), 0.0) for sx in range(SUB)]
        pes = [q_j * e for e in es]
        pe = jnp.concatenate(pes, axis=0).astype(BF16)
        pairs = _dot(pe, kk_b, NT)
        yield
        a_j = jnp.zeros((SUB, CHUNK), F32)
        for sx in range(SUB):
            a_j = jnp.where(lane == lo + sx, pairs[sx * SUB:(sx + 1) * SUB], a_j)
        if grad is not None:
            da_j = d_a[lo:lo + SUB]
            ek = jnp.concatenate([e * k_j[sx:sx + 1] for sx, e in enumerate(es)], axis=0).astype(BF16)
            sel_t = jnp.where(diag, _dot(da_j.astype(BF16), cols[j], NN), 0.0).astype(BF16)
            sel_s = jnp.where(block, _dot(d_at[lo:lo + SUB].astype(BF16), tile[j], NN), 0.0).astype(BF16)
            pek = jnp.concatenate([p * k_j[sx:sx + 1] for sx, p in enumerate(pes)], axis=0).astype(BF16)
            yield
            dq_j = _dot(sel_t, ek, NN)
            dk_j = _dot(sel_s, pe, NN)
            db_j = _dot(sel_t, pek, NN) - _dot(sel_s, pek, NN)
            yield
        if j > 0:
            ref = b[lo - 1:lo]
            sc_q = jnp.exp(b_j - ref)
            sc_k = jnp.exp(jnp.minimum(ref - b, 0.0))
            qt = (q_j * sc_q).astype(BF16)
            kt = (kk * sc_k).astype(BF16)
            left = _dot(qt, kt, NT)
            yield
            a_j = a_j + jnp.where(lane < lo, left, 0.0)
            if grad is not None:
                da_left = jnp.where(lane < lo, da_j, 0.0).astype(BF16)
                dq_left = _dot(da_left, kt, NN) * sc_q
                dq_j = dq_j + dq_left
                db_j = db_j + q_j * dq_left
                t = _dot(da_left, qt, TN)
                yield
                t = t * sc_k
                dk_left = t if dk_left is None else dk_left + t
        a_blocks.append(a_j)
        if grad is not None:
            dq_blocks.append(dq_j)
            dk_blocks.append(dk_j)
            db_blocks.append(db_j)
    a = jnp.concatenate(a_blocks, axis=0)
    if grad is None:
        return a
    return a, jnp.concatenate(dq_blocks, axis=0), jnp.concatenate(dk_blocks, axis=0) + dk_left, jnp.concatenate(db_blocks, axis=0) - kk * dk_left


def _hgrn_fwd(proj, hl):
    s = proj.shape[0]
    n_chunk = HG_TB // CHUNK
    pair = HG_PAIR_FWD

    def body(*refs):
        q_refs, f_refs, i_refs = refs[:pair], refs[pair:2 * pair], refs[2 * pair:3 * pair]
        hl_ref, o_ref, st_out_ref, st_ref = refs[3 * pair:]

        @pl.when(pl.program_id(1) == 0)
        def _():
            st_ref[...] = jnp.zeros_like(st_ref)

        r_i = lax.broadcasted_iota(jnp.int32, (CHUNK, CHUNK), 0)
        c_i = lax.broadcasted_iota(jnp.int32, (CHUNK, CHUNK), 1)
        tri_lower = (r_i >= c_i).astype(BF16)

        def chunk(c, carry):
            rows = pl.ds(pl.multiple_of(c * CHUNK, CHUNK), CHUNK)
            def head(p):
                cols = slice(p * LANE, (p + 1) * LANE)
                lb = _lower_bound(hl_ref[:, cols])
                v = i_refs[p][rows, :].astype(BF16)
                _, _, kk, _, qs, b = _hg_gates(q_refs[p][rows, :], f_refs[p][rows, :], lb, tri_lower)
                yield
                st = st_ref[p]
                st_b = st.astype(BF16)
                st_out_ref[p, c] = st_b
                o_state = _dot((qs * jnp.exp(b)).astype(BF16), st_b, NT)
                b_last = b[CHUNK - 1:CHUNK, :]
                st_new = _dot(v, (kk * jnp.exp(b_last - b)).astype(BF16), TN)
                a = yield from _hg_intra(qs, kk, b)
                st_ref[p] = st * jnp.exp(b_last) + st_new
                o_ref[rows, cols] = o_state + _dot(a.astype(BF16), v, NN)

            _interleave([head(p) for p in range(pair)])
            return carry

        lax.fori_loop(0, n_chunk, chunk, 0)

    return pl.pallas_call(
        body, grid=(HG_HEADS // pair, s // HG_TB), in_specs=_hg_specs(lambda n: n, pair),
        out_specs=[pl.BlockSpec((HG_TB, pair * LANE), lambda h, n: (n, h)), pl.BlockSpec((pair, n_chunk, HG_K, HG_K), lambda h, n: (h, n, 0, 0))],
        out_shape=[_sds((s, HG_W), F32), _sds((HG_HEADS, s // CHUNK, HG_K, HG_K), BF16)],
        scratch_shapes=[pltpu.VMEM((pair, HG_K, HG_K), F32)],
        compiler_params=_params(2), name="hgrn_fwd")(*[proj] * (3 * pair), hl)


def _hgrn_bwd(proj, hl, states, d_o):
    s = proj.shape[0]
    n_chunk = HG_TB // CHUNK
    n_blk = s // HG_TB
    pair = HG_PAIR_BWD
    rev = lambda n: n_blk - 1 - n

    def body(*refs):
        q_refs, f_refs, i_refs = refs[:pair], refs[pair:2 * pair], refs[2 * pair:3 * pair]
        hl_ref, st_in_ref, do_ref, dq_ref, df_ref, di_ref, dhl_ref, dst_ref, dlb_ref = refs[3 * pair:]
        n = pl.program_id(1)

        @pl.when(n == 0)
        def _():
            dst_ref[...] = jnp.zeros_like(dst_ref)
            dlb_ref[...] = jnp.zeros_like(dlb_ref)

        r_i = lax.broadcasted_iota(jnp.int32, (CHUNK, CHUNK), 0)
        c_i = lax.broadcasted_iota(jnp.int32, (CHUNK, CHUNK), 1)
        tri_lower = (r_i >= c_i).astype(BF16)
        tri_upper = (r_i <= c_i).astype(BF16)
        row = lax.broadcasted_iota(jnp.int32, (CHUNK, 1), 0)
        spread = _hg_spread()

        def chunk(cc, carry):
            c = n_chunk - 1 - cc
            rows = pl.ds(pl.multiple_of(c * CHUNK, CHUNK), CHUNK)
            def head(p):
                cols = slice(p * LANE, (p + 1) * LANE)
                lb = _lower_bound(hl_ref[:, cols])
                q_raw = q_refs[p][rows, :]
                vb = i_refs[p][rows, :].astype(BF16)
                sg, f, kk, sq, qs, b = _hg_gates(q_raw, f_refs[p][rows, :], lb, tri_lower)
                yield
                e_b = jnp.exp(b)
                qe = qs * e_b
                b_last = b[CHUNK - 1:CHUNK, :]
                e_last = jnp.exp(b_last)
                e_kd = jnp.exp(b_last - b)
                kd = kk * e_kd
                st0 = st_in_ref[p, c]
                d_ob = do_ref[rows, cols].astype(BF16)
                dst = dst_ref[p]
                dst_b = dst.astype(BF16)
                d_a = jnp.where(r_i >= c_i, _dot(d_ob, vb, NT), 0.0)
                d_at = jnp.where(r_i <= c_i, _dot(vb, d_ob, NT), 0.0)
                d_v_st = _dot(kd.astype(BF16), dst_b, NT)
                d_kd = _dot(vb, dst_b, NN)
                d_qe = _dot(d_ob, st0, NN)
                dst_new = _dot(d_ob, qe.astype(BF16), TN)
                yield
                a, dqs, dkk, d_b = yield from _hg_intra(qs, kk, b, (d_a, d_at, spread))
                d_v = _dot(a.astype(BF16), d_ob, TN) + d_v_st
                dqs_st = d_qe * e_b
                dkk_st = d_kd * e_kd
                dqs = dqs + dqs_st
                dkk = dkk + dkk_st
                d_b_last = jnp.sum(d_kd * kd, axis=0, keepdims=True) + jnp.sum(dst * st0.astype(F32), axis=0, keepdims=True) * e_last
                d_b = d_b + qs * dqs_st - kk * dkk_st + jnp.where(row == CHUNK - 1, d_b_last, 0.0)
                d_g = _tri_matmul(tri_upper, d_b)
                dst_ref[p] = dst_new + dst * e_last
                yield
                d_f = d_g / f - dkk
                dlb_ref[:, cols] += jnp.sum(d_f * (1.0 - sg), axis=0, keepdims=True)
                dq_ref[rows, cols] = (dqs * (sq * (1.0 + q_raw * (1.0 - sq)))).astype(dq_ref.dtype)
                df_ref[rows, cols] = (d_f * (1.0 - lb) * (sg * (1.0 - sg))).astype(df_ref.dtype)
                di_ref[rows, cols] = d_v.astype(di_ref.dtype)

            _interleave([head(p) for p in range(pair)])
            return carry

        lax.fori_loop(0, n_chunk, chunk, 0)

        @pl.when(n == n_blk - 1)
        def _():
            lb = _lower_bound(hl_ref[...])
            d_hl0 = dlb_ref[...] * (lb * (1.0 - lb))
            dhl_ref[...] = jnp.concatenate([d_hl0, -d_hl0], axis=0)

    out_blk = pl.BlockSpec((HG_TB, pair * LANE), lambda h, n: (rev(n), h))
    return pl.pallas_call(
        body, grid=(HG_HEADS // pair, n_blk),
        in_specs=_hg_specs(rev, pair) + [pl.BlockSpec((pair, n_chunk, HG_K, HG_K), lambda h, n: (h, rev(n), 0, 0)), out_blk],
        out_specs=[out_blk, out_blk, out_blk, pl.BlockSpec((2, pair * LANE), lambda h, n: (0, h))],
        out_shape=[_sds((s, HG_W), BF16)] * 3 + [_sds((2, HG_W), F32)],
        scratch_shapes=[pltpu.VMEM((pair, HG_K, HG_K), F32), pltpu.VMEM((1, pair * LANE), F32)],
        compiler_params=_params(2), name="hgrn_bwd")(*[proj] * (3 * pair), hl, states, d_o)


def _mod_part(c_all, w_shard, b_shard):
    n = w_shard.shape[1]
    tn = 512

    def body(c_ref, w_ref, b_ref, o_ref):
        o_ref[...] = _dot(c_ref[...].astype(BF16), w_ref[...].astype(BF16), NN) + b_ref[...]

    return pl.pallas_call(body, grid=(n // tn,),
                          in_specs=[pl.BlockSpec((N_DEV, D), lambda j: (0, 0)), pl.BlockSpec((D, tn), lambda j: (0, j)), pl.BlockSpec((1, tn), lambda j: (0, j))],
                          out_specs=pl.BlockSpec((N_DEV, tn), lambda j: (0, j)), out_shape=_sds((N_DEV, n), F32),
                          compiler_params=_params(1, 32 << 20), name="mod_part")(c_all, w_shard, b_shard)


def _grad_w_ada(c_all_t, dmod_cols):
    n = dmod_cols.shape[1]
    tn = 512

    def body(c_ref, d_ref, o_ref):
        cv = c_ref[...].astype(BF16).astype(F32)
        dv = d_ref[...].astype(BF16).astype(F32)
        acc = cv[:, 0:1] * dv[0:1, :]
        for k in range(1, N_DEV):
            acc = acc + cv[:, k:k + 1] * dv[k:k + 1, :]
        o_ref[...] = acc

    return pl.pallas_call(body, grid=(n // tn,),
                          in_specs=[pl.BlockSpec((D, N_DEV), lambda j: (0, 0)), pl.BlockSpec((N_DEV, tn), lambda j: (0, j))],
                          out_specs=pl.BlockSpec((D, tn), lambda j: (0, j)), out_shape=_sds((D, n), F32),
                          compiler_params=_params(1, 32 << 20), name="grad_w_ada")(c_all_t, dmod_cols)


def _row_tile(r, c, max_elems=1 << 18):
    if r * c <= max_elems or r % 8:
        return r
    best = 8
    for t in range(8, r + 1, 8):
        if r % t == 0 and t * c <= max_elems:
            best = t
    return best


WIDE_TILE = 5 << 17


def _adamw(pieces, w, m, v, name, emit_grad=True, own=None):
    p, r, c = pieces.shape
    tr = _row_tile(r, c)
    c1 = 1.0 / (1.0 - ADAM_B1 ** ADAM_STEP)
    c2 = 1.0 / (1.0 - ADAM_B2 ** ADAM_STEP)

    def body(*refs):
        if own is None:
            p_ref, w_ref, m_ref, v_ref, *outs = refs
            g = p_ref[0].astype(F32)
        else:
            o_ref, p_ref, w_ref, m_ref, v_ref, *outs = refs
            g = o_ref[...].astype(F32) + p_ref[0].astype(F32)
        for k in range(1, p):
            g = g + p_ref[k].astype(F32)
        m2 = ADAM_B1 * m_ref[...] + (1.0 - ADAM_B1) * g
        v2 = ADAM_B2 * v_ref[...] + (1.0 - ADAM_B2) * (g * g)
        delta = -ADAM_LR * ((m2 * c1) / (jnp.sqrt(v2 * c2) + ADAM_EPS) + ADAM_WD * w_ref[...])
        if emit_grad:
            outs[0][...] = g
        outs[-3][...] = delta
        outs[-2][...] = m2
        outs[-1][...] = v2

    blk = pl.BlockSpec((tr, c), lambda i: (i, 0))
    n_out = 4 if emit_grad else 3
    lead = [] if own is None else [own]
    return pl.pallas_call(body, grid=(r // tr,), in_specs=[blk] * len(lead) + [pl.BlockSpec((p, tr, c), lambda i: (0, i, 0)), blk, blk, blk],
                          out_specs=[blk] * n_out, out_shape=[_sds((r, c), F32)] * n_out,
                          compiler_params=_params(1, 48 << 20), name=name)(*lead, pieces, w, m, v)


def _my_coords():
    return lax.axis_index("x"), lax.axis_index("y"), lax.axis_index("c")


def _flip(coords, k):
    x, y, c = coords
    return (1 - x if k & 4 else x, 1 - y if k & 2 else y, 1 - c if k & 1 else c)


def _lin(coords):
    return 4 * coords[0] + 2 * coords[1] + coords[2]


def _exchange_small(x3, bcast, name):
    n = x3.shape[2]

    def body(x_ref, o_ref, send_sems, recv_sems):
        me = _my_coords()
        my_id = _lin(me)
        o_ref[pl.ds(my_id, 1)] = x_ref[pl.ds(0 if bcast else my_id, 1)]
        copies = []
        for k in range(1, N_DEV):
            peer = _flip(me, k)
            src = x_ref.at[0 if bcast else _lin(peer)]
            cp = pltpu.make_async_remote_copy(src_ref=src, dst_ref=o_ref.at[my_id], send_sem=send_sems.at[k], recv_sem=recv_sems.at[k],
                                              device_id=peer, device_id_type=MESH)
            cp.start()
            copies.append(cp)
        for k in range(1, N_DEV):
            peer = _flip(me, k)
            pltpu.make_async_remote_copy(src_ref=x_ref.at[0], dst_ref=o_ref.at[_lin(peer)], send_sem=send_sems.at[k], recv_sem=recv_sems.at[k],
                                         device_id=peer, device_id_type=MESH).wait_recv()
        for cp in copies:
            cp.wait_send()

    vm = pl.BlockSpec(memory_space=pltpu.VMEM)
    return pl.pallas_call(body, in_specs=[vm], out_specs=vm, out_shape=_sds((N_DEV, 1, n), F32),
                          scratch_shapes=[pltpu.SemaphoreType.DMA((N_DEV,)), pltpu.SemaphoreType.DMA((N_DEV,))], name=name)(x3)


HBM_SPEC = pl.BlockSpec(memory_space=pltpu.HBM)
SEM_SPEC = pl.BlockSpec(memory_space=pltpu.SEMAPHORE)
ANY_SPEC = pl.BlockSpec(memory_space=pl.ANY)
DATAFLOW = pltpu.SideEffectType.DATAFLOW_SIDE_EFFECTING
GATHER_FLIPS = (1, 2, 4, 6)
PASS_FLIPS = (2, 4, 6)
TOKEN = (8, LANE)


def _hbm(t):
    return pltpu.with_memory_space_constraint(t, pltpu.HBM)


def _hbm_like(ts):
    return [pltpu.HBM(t.shape, t.dtype) for t in ts]


def _split_start(issue, srcs, lands, n_sem, name, deps=()):
    n, nd = len(srcs), len(deps)

    def body(*refs):
        issue(refs[:n], refs[n:2 * n], refs[2 * n + nd], refs[2 * n + nd + 1])
        refs[-1][...] = jnp.zeros(TOKEN, F32)

    outs = pl.pallas_call(
        body, name=name,
        out_shape=(pltpu.SemaphoreType.DMA((n_sem,)), pltpu.SemaphoreType.DMA((n_sem,)), *_hbm_like(srcs), *_hbm_like(lands), _sds(TOKEN, F32)),
        in_specs=[HBM_SPEC] * (2 * n) + [ANY_SPEC] * nd,
        out_specs=(SEM_SPEC, SEM_SPEC, *[HBM_SPEC] * (2 * n), pl.BlockSpec(memory_space=pltpu.VMEM)),
        input_output_aliases={i: 2 + i for i in range(2 * n)},
        compiler_params=pltpu.CompilerParams(has_side_effects=DATAFLOW))(*[_hbm(t) for t in srcs], *[_hbm(t) for t in lands], *deps)
    return dict(sems=outs[:2], thru=list(outs[2:2 + 2 * n]), token=outs[-1], n=n)


def _split_wait(finish, handle, after, name):
    n = handle["n"]
    thru = handle["thru"]

    def body(*refs):
        finish(refs[:n], refs[n:2 * n], refs[2 * n], refs[2 * n + 1])

    outs = pl.pallas_call(
        body, name=name, out_shape=_hbm_like(thru), in_specs=[HBM_SPEC] * (2 * n) + [SEM_SPEC, SEM_SPEC] + [ANY_SPEC] * len(after),
        out_specs=[HBM_SPEC] * (2 * n), input_output_aliases={i: i for i in range(2 * n)},
        compiler_params=pltpu.CompilerParams(has_side_effects=DATAFLOW))(*thru, *handle["sems"], *after)
    return list(outs[:n]), list(outs[n:])


def _gather_start(shards, name, deps=()):
    n = len(shards)
    my_id = _lin(_my_coords())
    lands = [lax.dynamic_update_slice(lax.empty((N_DEV,) + t.shape, t.dtype), t[None], (my_id, 0, 0)) for t in shards]

    def issue(src, land, send_sems, recv_sems):
        me = _my_coords()
        for w in range(n):
            for j, k in enumerate(GATHER_FLIPS):
                q = len(GATHER_FLIPS) * w + j
                pltpu.make_async_remote_copy(src_ref=src[w], dst_ref=land[w].at[_lin(me)], send_sem=send_sems.at[q], recv_sem=recv_sems.at[q],
                                             device_id=_flip(me, k), device_id_type=MESH).start()

    return _split_start(issue, shards, lands, len(GATHER_FLIPS) * n, name, deps)


def _gather_wait(handle, after, name):
    n = handle["n"]

    def finish(src, land, send_sems, recv_sems):
        me = _my_coords()
        for w in range(n):
            for j, k in enumerate(GATHER_FLIPS):
                q = len(GATHER_FLIPS) * w + j
                peer = _flip(me, k)
                cp = pltpu.make_async_remote_copy(src_ref=src[w], dst_ref=land[w].at[_lin(peer)], send_sem=send_sems.at[q], recv_sem=recv_sems.at[q],
                                                  device_id=peer, device_id_type=MESH)
                cp.wait_send()
                cp.wait_recv()

    return _split_wait(finish, handle, after, name)[1]


def _gather_pass(lands, name):
    n = len(lands)
    n_p = len(PASS_FLIPS)

    def body(*refs):
        land = refs[n:2 * n]
        send_sems, recv_sems = refs[2 * n:]
        me = _my_coords()
        sibling = _flip(me, 1)
        sent = []
        for w in range(n):
            for j, k in enumerate(PASS_FLIPS):
                blk = land[w].at[_lin(_flip(me, k))]
                cp = pltpu.make_async_remote_copy(src_ref=blk, dst_ref=blk, send_sem=send_sems.at[n_p * w + j], recv_sem=recv_sems.at[n_p * w + j],
                                                  device_id=sibling, device_id_type=MESH)
                cp.start()
                sent.append(cp)
        for w in range(n):
            for j, k in enumerate(PASS_FLIPS):
                blk = land[w].at[_lin(_flip(me, k + 1))]
                pltpu.make_async_remote_copy(src_ref=blk, dst_ref=blk, send_sem=send_sems.at[n_p * w + j], recv_sem=recv_sems.at[n_p * w + j],
                                             device_id=sibling, device_id_type=MESH).wait_recv()
        for cp in sent:
            cp.wait_send()

    return pl.pallas_call(body, in_specs=[ANY_SPEC] * n, out_specs=[ANY_SPEC] * n, out_shape=[_sds(t.shape, t.dtype) for t in lands],
                          input_output_aliases={i: i for i in range(n)},
                          scratch_shapes=[pltpu.SemaphoreType.DMA((n_p * n,)), pltpu.SemaphoreType.DMA((n_p * n,))], name=name)(*lands)


CHIP_FLIPS = (0, 2, 4, 6)


def _pair_copy(src, land, send_sems, recv_sems, w, j):
    me = _my_coords()
    q = len(CHIP_FLIPS) * w + j
    return pltpu.make_async_remote_copy(src_ref=src[w].at[_lin(_flip(me, CHIP_FLIPS[j] + 1))], dst_ref=land[w].at[j], send_sem=send_sems.at[q],
                                        recv_sem=recv_sems.at[q], device_id=_flip(me, 1), device_id_type=MESH)


def _pair_exchange(grads, name):
    n = len(grads)

    def body(*refs):
        src, land = refs[:n], refs[n:2 * n]
        send_sems, recv_sems = refs[2 * n:]
        sent = [_pair_copy(src, land, send_sems, recv_sems, w, j) for w in range(n) for j in range(len(CHIP_FLIPS))]
        for cp in sent:
            cp.start()
        for cp in sent:
            cp.wait_recv()
        for cp in sent:
            cp.wait_send()

    outs = pl.pallas_call(body, in_specs=[ANY_SPEC] * n, out_specs=[ANY_SPEC] * n,
                          out_shape=[_sds((len(CHIP_FLIPS),) + g.shape[1:], g.dtype) for g in grads],
                          scratch_shapes=[pltpu.SemaphoreType.DMA((len(CHIP_FLIPS) * n,))] * 2, name=name)(*grads)
    return list(outs)


def _pair_start(grads, name, deps=()):
    n = len(grads)
    lands = [lax.empty((len(CHIP_FLIPS),) + g.shape[1:], g.dtype) for g in grads]

    def issue(src, land, send_sems, recv_sems):
        for w in range(n):
            for j in range(len(CHIP_FLIPS)):
                _pair_copy(src, land, send_sems, recv_sems, w, j).start()

    return _split_start(issue, grads, lands, len(CHIP_FLIPS) * n, name, deps)


def _pair_wait(handle, after, name):
    n = handle["n"]

    def finish(src, land, send_sems, recv_sems):
        for w in range(n):
            for j in range(len(CHIP_FLIPS)):
                cp = _pair_copy(src, land, send_sems, recv_sems, w, j)
                cp.wait_send()
                cp.wait_recv()

    return _split_wait(finish, handle, after, name)


def _pair_add(grad, theirs, name):
    p, r, c = theirs.shape
    tr = _row_tile(r, c, WIDE_TILE)
    me = _my_coords()
    ids = jnp.stack([_lin(_flip(me, k)) for k in CHIP_FLIPS]).astype(jnp.int32)

    def body(ids_ref, a_ref, b_ref, o_ref):
        o_ref[...] = (a_ref[...].astype(F32) + b_ref[...].astype(F32)).astype(o_ref.dtype)

    blk = pl.BlockSpec((None, tr, c), lambda j, i, ids_ref: (j, i, 0))
    return pl.pallas_call(
        body, out_shape=_sds((p, r, c), theirs.dtype), compiler_params=_params(2), name=name,
        grid_spec=pltpu.PrefetchScalarGridSpec(
            num_scalar_prefetch=1, grid=(p, r // tr),
            in_specs=[pl.BlockSpec((None, tr, c), lambda j, i, ids_ref: (ids_ref[j], i, 0)), blk], out_specs=blk))(ids, grad, theirs)


def _chips_start(parts, name, deps=()):
    n = len(parts)
    n_c = len(CHIP_FLIPS) - 1
    lands = [lax.empty((n_c,) + t.shape[1:], t.dtype) for t in parts]

    def issue(src, land, send_sems, recv_sems):
        me = _my_coords()
        for w in range(n):
            for j in range(1, n_c + 1):
                q = n_c * w + j - 1
                pltpu.make_async_remote_copy(src_ref=src[w].at[j], dst_ref=land[w].at[j - 1], send_sem=send_sems.at[q], recv_sem=recv_sems.at[q],
                                             device_id=_flip(me, CHIP_FLIPS[j]), device_id_type=MESH).start()

    return _split_start(issue, parts, lands, n_c * n, name, deps)


def _chips_wait(handle, after, name):
    n = handle["n"]
    n_c = len(CHIP_FLIPS) - 1

    def finish(src, land, send_sems, recv_sems):
        me = _my_coords()
        for w in range(n):
            for j in range(1, n_c + 1):
                q = n_c * w + j - 1
                cp = pltpu.make_async_remote_copy(src_ref=src[w].at[j], dst_ref=land[w].at[j - 1], send_sem=send_sems.at[q], recv_sem=recv_sems.at[q],
                                                  device_id=_flip(me, CHIP_FLIPS[j]), device_id_type=MESH)
                cp.wait_send()
                cp.wait_recv()

    return _split_wait(finish, handle, after, name)


def _after(t, *tokens):
    for tok in tokens:
        t = t + tok[0:1, 0:1]
    return t


def _rope_tables(positions):
    half = ROT // 2
    inv_freq = ROPE_THETA ** (-jnp.arange(0, ROT, 2, dtype=F32) / ROT)
    ang = positions.astype(F32).reshape(-1, 1) * inv_freq
    cos, sin = jnp.cos(ang), jnp.sin(ang)
    s = ang.shape[0]
    pad = jnp.zeros((s, HEAD_DIM - ROT), F32)
    zero = jnp.zeros((s, half), F32)
    two = lambda t: jnp.concatenate([t, t], axis=1)
    return (two(jnp.concatenate([cos, cos, pad + 1.0], axis=1)), two(jnp.concatenate([-sin, zero, pad], axis=1)),
            two(jnp.concatenate([zero, sin, pad], axis=1)))


def _local_step(x, tgt, tabs, mod, sinks_pad, hl, hg_norm, g_pre_mix, g_post_mix, g_pre_ffn, g_post_ffn, weights, scatter, scatter_on):
    s = x.shape[0]
    h1 = _pre_fwd(x, g_pre_mix, mod, 1, 0, "pre_mix_fwd")
    (w_in_t,) = weights("in", h1)
    proj = _mm_nt(h1, w_in_t, s, 256, D, F32, "proj_mm")
    att = _attn_fwd(proj, tabs, sinks_pad)
    o_raw, states = _hgrn_fwd(proj, hl)
    ohg = _hgout_fwd(o_raw, proj, hg_norm)
    w_attn_dm, w_hgrn_dm, w_out = weights("mix", ohg)
    y_a = _mm_nn_dm(att, w_attn_dm, s, F32, "attn_proj_mm")
    y_h = _mm_nn_dm(ohg, w_hgrn_dm, s, F32, "hgrn_proj_mm")
    merged = _merge_fwd(y_a, y_h, proj)
    y = _mm_nn(merged, w_out, s, 512, D, F32, "out_mm")
    x1 = _post_fwd(x, y, g_post_mix, mod, 2, "post_mix_fwd")
    h2 = _pre_fwd(x1, g_pre_ffn, mod, 4, 3, "pre_ffn_fwd")
    (w_ffn_in_dm,) = weights("ffn_in", h2)
    gu = _mm_nn_dm(h2, w_ffn_in_dm, s // 2, F32, "ffn_in_mm")
    act = _swiglu_fwd(gu)
    (w_ffn_out,) = weights("ffn_out", act)
    y2 = _mm_nn(act, w_ffn_out, 512, 512, FFN, F32, "ffn_out_mm")
    err, loss = _post_fwd_loss(x1, y2, g_post_ffn, mod, 5, tgt, "post_ffn_loss")
    dy2, d_gate2, dg_post_ffn = _post_bwd(err, y2, g_post_ffn, mod, 5, "post_ffn_bwd")
    gw_ffn_out = _mm_tn(act, dy2, 512, D, BF16, "ffn_out_dw")
    t_pair = scatter([gw_ffn_out.reshape(N_DEV, FFN // N_DEV, D)], "ffn_out")
    d_act = _mm_nt(dy2, w_ffn_out, s, 512, D, F32, "ffn_out_dx", deps=[t_pair])
    dgu = _swiglu_bwd(d_act, gu)
    t_out = scatter_on("ffn_out", dgu)
    gw_ffn_in = _mm_tn_dm(h2, dgu, 1024, BF16, "ffn_in_dw")
    t_pair = scatter([gw_ffn_in], "ffn_in")
    dh2 = _mm_nt_dm(dgu, w_ffn_in_dm, s, 1024, F32, "ffn_in_dx", deps=[t_pair])
    mod = _after(mod, t_out)
    dx1, d_shift2, d_scale2, dg_pre_ffn = _pre_bwd(dh2, x1, err, g_pre_ffn, mod, 4, "pre_ffn_bwd")
    dy, d_gate1, dg_post_mix = _post_bwd(dx1, y, g_post_mix, mod, 2, "post_mix_bwd")
    t_in = scatter_on("ffn_in", dy)
    d_merged = _mm_nt(dy, w_out, s, 512, D, F32, "out_dx")
    gw_out = _mm_tn(merged, dy, 512, D, BF16, "out_dw")
    dy_a, dy_h, d_gate_a, d_gate_h = _merge_bwd(d_merged, y_a, y_h, proj)
    gw_attn = _mm_tn_dm(att, dy_a, ATT_W, BF16, "attn_proj_dw")
    gw_hgrn = _mm_tn_dm(ohg, dy_h, HG_W, BF16, "hgrn_proj_dw")
    t_pair = scatter([gw_attn, gw_hgrn, gw_out.reshape(N_DEV, D // N_DEV, D)], "mix")
    d_att = _mm_nt_dm(dy_a, w_attn_dm, s, ATT_W, F32, "attn_proj_dx")
    d_ohg = _mm_nt_dm(dy_h, w_hgrn_dm, s, HG_W, F32, "hgrn_proj_dx", deps=[t_pair])
    d_o, d_gh, d_hg_norm = _hgout_bwd(d_ohg, o_raw, proj, _after(hg_norm, t_in))
    d_qh, d_fh, d_ih, d_hl = _hgrn_bwd(proj, hl, states, d_o)
    t_mix = scatter_on("mix", d_qh)
    d_qa, d_ka, d_va, d_sinks = _attn_bwd(proj, tabs, _after(sinks_pad, t_mix), d_att)
    d_proj = jnp.concatenate([d_qa, d_ka.astype(BF16), d_va.astype(BF16), d_qh, d_fh, d_ih, d_gh, d_gate_a, d_gate_h], axis=1)
    dh1 = _mm_nn(d_proj, w_in_t, s // 2, 512, IN_COLS // 2, F32, "proj_dx")
    grad_x, d_shift1, d_scale1, dg_pre_mix = _pre_bwd(dh1, x, dx1, g_pre_mix, mod, 1, "pre_mix_bwd")
    d_mod = jnp.concatenate([d_shift1, d_scale1, d_gate1, d_shift2, d_scale2, d_gate2], axis=1)
    small = [d_mod, dg_pre_mix, dg_post_mix, dg_pre_ffn, dg_post_ffn, d_hl.reshape(1, 2 * HG_W), d_hg_norm, d_sinks]
    return loss, grad_x, small, h1, d_proj


def kernel(x, c, positions, w_ada, b_ada, g_pre_mix, g_post_mix, g_pre_ffn, g_post_ffn, w_in, attn_sinks, w_attn_proj, hg_lower_bounds, hg_norm, w_hgrn_proj, w_out, w_ffn_in, w_ffn_out, loss_target, m_w_ada, m_b_ada, m_g_pre_mix, m_g_post_mix, m_g_pre_ffn, m_g_post_ffn, m_w_in, m_attn_sinks, m_w_attn_proj, m_hg_lower_bounds, m_hg_norm, m_w_hgrn_proj, m_w_out, m_w_ffn_in, m_w_ffn_out, v_w_ada, v_b_ada, v_g_pre_mix, v_g_post_mix, v_g_pre_ffn, v_g_post_ffn, v_w_in, v_attn_sinks, v_w_attn_proj, v_hg_lower_bounds, v_hg_norm, v_w_hgrn_proj, v_w_out, v_w_ffn_in, v_w_ffn_out):
    my_id = _lin(_my_coords())
    s = x.shape[1]
    n_ada = w_ada.shape[2]

    c_all = _exchange_small(c.reshape(1, 1, D), True, "gather_c").reshape(N_DEV, D)
    b_cols = lax.dynamic_slice(b_ada, (0, my_id * n_ada), (1, n_ada))
    mod_part = _mod_part(c_all, w_ada[0], b_cols)
    mod = _exchange_small(mod_part.reshape(N_DEV, 1, n_ada), False, "scatter_mod").reshape(1, N_MOD * D)
    groups = {"in": [w_in[0].T], "mix": [w_attn_proj[0], w_hgrn_proj[0], w_out[0]], "ffn_in": [w_ffn_in[0]], "ffn_out": [w_ffn_out[0]]}

    def start(group, dep):
        shards, dep = lax.optimization_barrier((groups[group], dep))
        return _gather_start([t.astype(BF16) for t in shards], "gather_start_" + group, deps=[dep])

    gathers = {"in": start("in", mod)}
    gathers["mix"] = start("mix", gathers["in"]["token"])
    gathers["ffn_in"] = start("ffn_in", gathers["mix"]["token"])
    gathers["ffn_out"] = start("ffn_out", gathers["ffn_in"]["token"])

    def weights(group, after):
        after = [after, gathers["ffn_out"]["token"]]
        lands = _gather_pass(_gather_wait(gathers[group], after, "gather_wait_" + group), "gather_pass_" + group)
        if group == "in":
            return (lands[0].reshape(IN_COLS, D),)
        if group == "mix":
            return lands[0], lands[1], lands[2].reshape(D, D)
        return (lands[0],) if group == "ffn_in" else (lands[0].reshape(FFN, D),)

    pairs, scatters = {}, {}

    def scatter(grads, group):
        pairs[group] = _pair_start(grads, "scatter_pair_" + group)
        return pairs[group]["token"]

    def scatter_on(group, after):
        if group in pairs:
            local, theirs = _pair_wait(pairs[group], [after], "scatter_pair_wait_" + group)
        else:
            local, theirs = after, _pair_exchange(after, "scatter_pair_" + group)
        parts = [_pair_add(g, t, "scatter_pair_add_%s_%d" % (group, k)) for k, (g, t) in enumerate(zip(local, theirs))]
        scatters[group] = _chips_start(parts, "scatter_start_" + group)
        return scatters[group]["token"]

    sinks_pad = jnp.pad(attn_sinks, ((0, 0), (0, LANE - ATT_HEADS)))
    loss, grad_x, small, h1, d_proj = _local_step(
        x[0], loss_target[0], _rope_tables(positions), mod, sinks_pad, hg_lower_bounds, hg_norm, g_pre_mix, g_post_mix, g_pre_ffn, g_post_ffn,
        weights, scatter, scatter_on)
    loss = lax.psum(loss[0, 0], ("x", "y", "c"))

    sizes = [t.shape[1] for t in small]
    parts = _exchange_small(jnp.concatenate(small, axis=1).reshape(1, 1, sum(sizes)), True, "gather_small_grads")
    dep = parts
    for half, cols in (("in_a", slice(0, D // 2)), ("in_b", slice(D // 2, D))):
        gw_half = _mm_tn(d_proj, h1[:, cols], 256, D // 2, BF16, "proj_dw_" + half, deps=[dep])
        dep = scatter_on(half, [gw_half.reshape(N_DEV, IN_COLS // N_DEV, D // 2)])
    offs = [sum(sizes[:k]) for k in range(len(sizes))]
    piece = lambda k, n=None: parts[:, :, offs[k]:offs[k] + (sizes[k] if n is None else n)]
    small_w = [(piece(0), b_ada, m_b_ada, v_b_ada), (piece(1), g_pre_mix, m_g_pre_mix, v_g_pre_mix),
               (piece(2), g_post_mix, m_g_post_mix, v_g_post_mix), (piece(3), g_pre_ffn, m_g_pre_ffn, v_g_pre_ffn),
               (piece(4), g_post_ffn, m_g_post_ffn, v_g_post_ffn),
               (piece(5).reshape(N_DEV, 2, HG_W), hg_lower_bounds, m_hg_lower_bounds, v_hg_lower_bounds),
               (piece(6), hg_norm, m_hg_norm, v_hg_norm), (piece(7, ATT_HEADS), attn_sinks, m_attn_sinks, v_attn_sinks)]
    names = ["b_ada", "g_pre_mix", "g_post_mix", "g_pre_ffn", "g_post_ffn", "hg_lower_bounds", "hg_norm", "attn_sinks"]
    res = {n: _adamw(p, w, m, v, "adamw_" + n) for n, (p, w, m, v) in zip(names, small_w)}

    dmod_cols = lax.dynamic_slice(parts.reshape(N_DEV, -1), (0, my_id * n_ada), (N_DEV, n_ada))
    g_w_ada = _grad_w_ada(c_all.T, dmod_cols)
    res["w_ada"] = [g_w_ada] + list(_adamw(g_w_ada[None], w_ada[0], m_w_ada[0], v_w_ada[0], "adamw_w_ada", emit_grad=False))

    big = {"ffn_out": [("w_ffn_out", w_ffn_out, m_w_ffn_out, v_w_ffn_out)], "ffn_in": [("w_ffn_in", w_ffn_in, m_w_ffn_in, v_w_ffn_in)],
           "mix": [("w_attn_proj", w_attn_proj, m_w_attn_proj, v_w_attn_proj), ("w_hgrn_proj", w_hgrn_proj, m_w_hgrn_proj, v_w_hgrn_proj),
                   ("w_out", w_out, m_w_out, v_w_out)]}
    after = [scatters["in_b"]["token"]]
    for group, members in big.items():
        local, lands = _chips_wait(scatters[group], after, "scatter_wait_" + group)
        for (n, w, m, v), mine, land in zip(members, local, lands):
            res[n] = _adamw(land, w[0], m[0], v[0], "adamw_" + n, own=mine[0])
            after = after + [res[n][1]]
    after = [res[n][1] for n in res]
    halves = [_chips_wait(scatters[half], after, "scatter_wait_" + half) for half in ("in_a", "in_b")]
    own = jnp.concatenate([local[0][0] for local, _ in halves], axis=1)
    land = jnp.concatenate([lands[0] for _, lands in halves], axis=2)
    res["w_in"] = [t.T for t in _adamw(land, w_in[0].T, m_w_in[0].T, v_w_in[0].T, "adamw_w_in", own=own)]

    order = ["w_ada", "b_ada", "g_pre_mix", "g_post_mix", "g_pre_ffn", "g_post_ffn", "w_in", "attn_sinks", "w_attn_proj",
             "hg_lower_bounds", "hg_norm", "w_hgrn_proj", "w_out", "w_ffn_in", "w_ffn_out"]
    lead = {"w_ada", "w_in", "w_attn_proj", "w_hgrn_proj", "w_out", "w_ffn_in", "w_ffn_out"}
    outs = [loss, grad_x[None]]
    for k in range(4):
        outs += [res[n][k][None] if n in lead else res[n][k] for n in order]
    return tuple(outs)
```

```python
import functools

import jax
import jax.numpy as jnp
from jax import lax
from jax.experimental import pallas as pl
from jax.experimental.pallas import tpu as pltpu

F32 = jnp.float32
BF16 = jnp.bfloat16

N_DEV = 8
D = 2048
ATT_HEADS = 16
KV_HEADS = 2
HEAD_DIM = 64
GROUP = ATT_HEADS // KV_HEADS
ATT_W = ATT_HEADS * HEAD_DIM
BLK = 128
ROT = HEAD_DIM // 4
ROPE_THETA = 500000.0
HG_HEADS = 8
HG_K = 128
HG_W = HG_HEADS * HG_K
CHUNK = 64
SUB = 16
FFN = 5632
N_MOD = 6
EPS = 1e-6
LANE = 128
Q_A, K_A, V_A, Q_H, F_H, I_H, G_H, GT_A, GT_H, IN_COLS = 0, 1024, 1152, 1280, 2304, 3328, 4352, 5376, 7424, 9472

ADAM_LR, ADAM_B1, ADAM_B2, ADAM_EPS, ADAM_WD, ADAM_STEP = 0.001, 0.9, 0.999, 1e-08, 0.01, 10

TR = 256
HG_TB = 512
VMEM_BIG = 56 << 20
MESH = pl.DeviceIdType.MESH


def _sds(shape, dtype):
    return jax.ShapeDtypeStruct(shape, dtype)


def _params(n_axes, vmem=None):
    return pltpu.CompilerParams(dimension_semantics=("arbitrary",) * n_axes, vmem_limit_bytes=vmem)


def _sig(t):
    return 1.0 / (1.0 + jnp.exp(-t))


def _dot(a, b, dims):
    return lax.dot_general(a, b, (dims, ((), ())), preferred_element_type=F32)


NN = ((1,), (0,))
NT = ((1,), (1,))
TN = ((0,), (0,))


def _matmul(a, b, a_spec, b_spec, o_spec, out_shape, grid, dims, acc_shape, name, deps=(), add=None):
    nk = grid[2]
    nd = len(deps)
    extra = [] if add is None else [add]

    def body(a_ref, b_ref, *rest):
        o_ref, scratch = rest[nd + len(extra)], rest[nd + len(extra) + 1:]
        part = _dot(a_ref[...], b_ref[...], dims)
        if add is not None:
            assert nk == 1
            part = part + rest[nd][...]
        if nk == 1:
            o_ref[...] = part.astype(o_ref.dtype)
        else:
            acc = scratch[0]
            k = pl.program_id(2)

            @pl.when(k == 0)
            def _():
                acc[...] = part

            @pl.when(k > 0)
            def _():
                acc[...] += part

            @pl.when(k == nk - 1)
            def _():
                o_ref[...] = acc[...].astype(o_ref.dtype)

    return pl.pallas_call(
        body, grid=grid, in_specs=[a_spec, b_spec] + [pl.BlockSpec(memory_space=pl.ANY)] * nd + [o_spec] * len(extra),
        out_specs=o_spec, out_shape=out_shape, scratch_shapes=[pltpu.VMEM(acc_shape, F32)] if nk > 1 else [],
        input_output_aliases={2 + nd: 0} if extra else {},
        compiler_params=_params(3, VMEM_BIG), name=name)(a, b, *deps, *extra)


def _mm_nn(a, b, tm, tn, tk, out_dtype, name):
    m, k = a.shape
    n = b.shape[1]
    return _matmul(a, b, pl.BlockSpec((tm, tk), lambda j, i, kk: (i, kk)), pl.BlockSpec((tk, tn), lambda j, i, kk: (kk, j)),
                   pl.BlockSpec((tm, tn), lambda j, i, kk: (i, j)), _sds((m, n), out_dtype),
                   (n // tn, m // tm, k // tk), NN, (tm, tn), name)


def _mm_nn_dm(a, b, tm, out_dtype, name):
    m, k = a.shape
    n = b.shape[2]
    return _matmul(a, b, pl.BlockSpec((tm, k), lambda j, i, kk: (i, 0)), pl.BlockSpec((None, k, n), lambda j, i, kk: (j, 0, 0)),
                   pl.BlockSpec((tm, n), lambda j, i, kk: (i, j)), _sds((m, N_DEV * n), out_dtype),
                   (N_DEV, m // tm, 1), NN, (tm, n), name)


def _mm_nt(a, b, tm, tn, tk, out_dtype, name, deps=(), add=None):
    m, k = a.shape
    n = b.shape[0]
    return _matmul(a, b, pl.BlockSpec((tm, tk), lambda j, i, kk: (i, kk)), pl.BlockSpec((tn, tk), lambda j, i, kk: (j, kk)),
                   pl.BlockSpec((tm, tn), lambda j, i, kk: (i, j)), _sds((m, n), out_dtype),
                   (n // tn, m // tm, k // tk), NT, (tm, tn), name, deps, add)


def _mm_nt_dm(a, b, tm, tn, out_dtype, name, deps=()):
    m = a.shape[0]
    n_out, n = b.shape[1], b.shape[2]
    return _matmul(a, b, pl.BlockSpec((tm, n), lambda j, i, kk: (i, kk)), pl.BlockSpec((None, tn, n), lambda j, i, kk: (kk, j, 0)),
                   pl.BlockSpec((tm, tn), lambda j, i, kk: (i, j)), _sds((m, n_out), out_dtype),
                   (n_out // tn, m // tm, N_DEV), NT, (tm, tn), name, deps)


def _mm_tn(a, b, tm, tn, out_dtype, name, deps=()):
    s, m = a.shape
    n = b.shape[1]
    return _matmul(a, b, pl.BlockSpec((s, tm), lambda j, i, kk: (0, i)), pl.BlockSpec((s, tn), lambda j, i, kk: (0, j)),
                   pl.BlockSpec((tm, tn), lambda j, i, kk: (i, j)), _sds((m, n), out_dtype),
                   (n // tn, m // tm, 1), TN, (tm, tn), name, deps)


def _mm_tn_dm(a, b, tm, out_dtype, name):
    s, m = a.shape
    n = b.shape[1] // N_DEV
    return _matmul(a, b, pl.BlockSpec((s, tm), lambda j, i, kk: (0, i)), pl.BlockSpec((s, n), lambda j, i, kk: (0, j)),
                   pl.BlockSpec((None, tm, n), lambda j, i, kk: (j, i, 0)), _sds((N_DEV, m, n), out_dtype),
                   (N_DEV, m // tm, 1), TN, (tm, n), name)


def _row_spec():
    return pl.BlockSpec((TR, D), lambda i: (i, 0))


def _vec_spec(k=0):
    return pl.BlockSpec((1, D), lambda i: (0, k))


def _acc_rows(ref, first, val):
    @pl.when(first)
    def _():
        ref[...] = val

    @pl.when(jnp.logical_not(first))
    def _():
        ref[...] += val


def _pre_fwd(x, g, mod, k_scale, k_shift, name):
    s = x.shape[0]

    def body(x_ref, g_ref, sc_ref, sh_ref, h_ref):
        xv = x_ref[...]
        r = lax.rsqrt(jnp.mean(xv * xv, axis=-1, keepdims=True) + EPS)
        n = xv * r * g_ref[...]
        h_ref[...] = (n * (1.0 + sc_ref[...]) + sh_ref[...]).astype(h_ref.dtype)

    return pl.pallas_call(body, grid=(s // TR,), in_specs=[_row_spec(), _vec_spec(), _vec_spec(k_scale), _vec_spec(k_shift)],
                          out_specs=_row_spec(), out_shape=_sds((s, D), BF16), compiler_params=_params(1), name=name)(x, g, mod, mod)


def _post_fwd(x, y, g, mod, k_gate, name):
    s = x.shape[0]

    def body(x_ref, y_ref, g_ref, gt_ref, o_ref):
        yv = y_ref[...]
        r = lax.rsqrt(jnp.mean(yv * yv, axis=-1, keepdims=True) + EPS)
        o_ref[...] = x_ref[...] + gt_ref[...] * (yv * r * g_ref[...])

    return pl.pallas_call(body, grid=(s // TR,), in_specs=[_row_spec(), _row_spec(), _vec_spec(), _vec_spec(k_gate)],
                          out_specs=_row_spec(), out_shape=_sds((s, D), F32), compiler_params=_params(1), name=name)(x, y, g, mod)


def _post_fwd_loss(x, y, g, mod, k_gate, tgt, name):
    s = x.shape[0]

    def body(x_ref, y_ref, g_ref, gt_ref, t_ref, e_ref, loss_ref):
        i = pl.program_id(0)
        yv = y_ref[...]
        r = lax.rsqrt(jnp.mean(yv * yv, axis=-1, keepdims=True) + EPS)
        err = x_ref[...] + gt_ref[...] * (yv * r * g_ref[...]) - t_ref[...]
        e_ref[...] = err * (1.0 / D)
        part = 0.5 * jnp.sum(jnp.mean(err * err, axis=-1, keepdims=True), axis=0, keepdims=True)
        _acc_rows(loss_ref, i == 0, part)

    return pl.pallas_call(body, grid=(s // TR,),
                          in_specs=[_row_spec(), _row_spec(), _vec_spec(), _vec_spec(k_gate), _row_spec()],
                          out_specs=[_row_spec(), pl.BlockSpec((1, 1), lambda i: (0, 0))],
                          out_shape=[_sds((s, D), F32), _sds((1, 1), F32)], compiler_params=_params(1), name=name)(x, y, g, mod, tgt)


def _pre_bwd(dh, x, res, g, mod, k_scale, name):
    s = x.shape[0]

    def body(dh_ref, x_ref, res_ref, g_ref, sc_ref, dx_ref, dsh_ref, dsc_ref, dg_ref):
        first = pl.program_id(0) == 0
        xv, dh_v, gv = x_ref[...], dh_ref[...], g_ref[...]
        r = lax.rsqrt(jnp.mean(xv * xv, axis=-1, keepdims=True) + EPS)
        xh = xv * r
        dn = dh_v * (1.0 + sc_ref[...])
        dgn = dn * gv
        dx_ref[...] = res_ref[...] + r * (dgn - xh * jnp.mean(dgn * xh, axis=-1, keepdims=True))
        _acc_rows(dsh_ref, first, jnp.sum(dh_v, axis=0, keepdims=True))
        _acc_rows(dsc_ref, first, jnp.sum(dh_v * (xh * gv), axis=0, keepdims=True))
        _acc_rows(dg_ref, first, jnp.sum(dn * xh, axis=0, keepdims=True))

    return pl.pallas_call(body, grid=(s // TR,),
                          in_specs=[_row_spec(), _row_spec(), _row_spec(), _vec_spec(), _vec_spec(k_scale)],
                          out_specs=[_row_spec(), _vec_spec(), _vec_spec(), _vec_spec()],
                          out_shape=[_sds((s, D), F32)] + [_sds((1, D), F32)] * 3,
                          compiler_params=_params(1), name=name)(dh, x, res, g, mod)


def _post_bwd(dx, y, g, mod, k_gate, name):
    s = y.shape[0]

    def body(dx_ref, y_ref, g_ref, gt_ref, dy_ref, dgt_ref, dg_ref):
        first = pl.program_id(0) == 0
        yv, dxv, gv = y_ref[...], dx_ref[...], g_ref[...]
        r = lax.rsqrt(jnp.mean(yv * yv, axis=-1, keepdims=True) + EPS)
        yh = yv * r
        dn = dxv * gt_ref[...]
        dgn = dn * gv
        dy_ref[...] = (r * (dgn - yh * jnp.mean(dgn * yh, axis=-1, keepdims=True))).astype(dy_ref.dtype)
        _acc_rows(dgt_ref, first, jnp.sum(dxv * (yh * gv), axis=0, keepdims=True))
        _acc_rows(dg_ref, first, jnp.sum(dn * yh, axis=0, keepdims=True))

    return pl.pallas_call(body, grid=(s // TR,), in_specs=[_row_spec(), _row_spec(), _vec_spec(), _vec_spec(k_gate)],
                          out_specs=[_row_spec(), _vec_spec(), _vec_spec()],
                          out_shape=[_sds((s, D), BF16), _sds((1, D), F32), _sds((1, D), F32)],
                          compiler_params=_params(1), name=name)(dx, y, g, mod)


SW_TN = 1408
SW_TR = 512
TALL = 1024


def _swiglu_fwd(gu):
    s = gu.shape[0]
    nb = FFN // SW_TN

    def body(g_ref, u_ref, a_ref):
        gv = g_ref[...]
        a_ref[...] = (gv * _sig(gv) * u_ref[...]).astype(a_ref.dtype)

    return pl.pallas_call(body, grid=(s // SW_TR, nb),
                          in_specs=[pl.BlockSpec((SW_TR, SW_TN), lambda i, j: (i, j)), pl.BlockSpec((SW_TR, SW_TN), lambda i, j: (i, j + nb))],
                          out_specs=pl.BlockSpec((SW_TR, SW_TN), lambda i, j: (i, j)), out_shape=_sds((s, FFN), BF16),
                          compiler_params=_params(2, 48 << 20), name="swiglu_fwd")(gu, gu)


def _swiglu_bwd(dact, gu):
    s = gu.shape[0]
    nb = FFN // SW_TN
    n_steps = (s // SW_TR) * nb

    def body(da_ref, g_ref, u_ref, o_ref, buf, sems):
        i, j = pl.program_id(0), pl.program_id(1)
        step = i * nb + j
        slot = step % 2

        def tiles(sl):
            rows = pl.ds(pl.multiple_of(i * SW_TR, SW_TR), SW_TR)
            return [pltpu.make_async_copy(buf.at[sl, h], o_ref.at[rows, pl.ds(pl.multiple_of((j + nb * h) * SW_TN, LANE), SW_TN)], sems.at[sl, h])
                    for h in range(2)]

        @pl.when(step >= 2)
        def _():
            for cp in tiles(slot):
                cp.wait()

        gv, da = g_ref[...], da_ref[...]
        sg = _sig(gv)
        buf[slot, 0] = (da * u_ref[...] * (sg * (1.0 + gv * (1.0 - sg)))).astype(buf.dtype)
        buf[slot, 1] = (da * (gv * sg)).astype(buf.dtype)
        for cp in tiles(slot):
            cp.start()

        @pl.when(step == n_steps - 1)
        def _():
            for cp in tiles(slot) + (tiles(1 - slot) if n_steps > 1 else []):
                cp.wait()

    blk = lambda f: pl.BlockSpec((SW_TR, SW_TN), f)
    return pl.pallas_call(body, grid=(s // SW_TR, nb),
                          in_specs=[blk(lambda i, j: (i, j)), blk(lambda i, j: (i, j)), blk(lambda i, j: (i, j + nb))],
                          out_specs=pl.BlockSpec(memory_space=pl.ANY), out_shape=_sds((s, 2 * FFN), BF16),
                          scratch_shapes=[pltpu.VMEM((2, 2, SW_TR, SW_TN), BF16), pltpu.SemaphoreType.DMA((2, 2))],
                          compiler_params=_params(2, 48 << 20), name="swiglu_bwd")(dact, gu, gu)


MG_TN = 256


def _merge_fwd(y_a, y_h, proj):
    s = y_a.shape[0]
    tn = MG_TN
    ba, bh = GT_A // tn, GT_H // tn

    def body(ya_ref, yh_ref, ga_ref, gh_ref, m_ref):
        m_ref[...] = (_sig(ga_ref[...]) * ya_ref[...] + _sig(gh_ref[...]) * yh_ref[...]).astype(m_ref.dtype)

    tr = min(s, TALL)
    blk = lambda f: pl.BlockSpec((tr, tn), f)
    return pl.pallas_call(body, grid=(s // tr, D // tn),
                          in_specs=[blk(lambda i, j: (i, j)), blk(lambda i, j: (i, j)), blk(lambda i, j: (i, j + ba)), blk(lambda i, j: (i, j + bh))],
                          out_specs=blk(lambda i, j: (i, j)), out_shape=_sds((s, D), BF16),
                          compiler_params=_params(2), name="merge_fwd")(y_a, y_h, proj, proj)


def _merge_bwd(dm, y_a, y_h, proj):
    s = y_a.shape[0]
    tn = MG_TN
    ba, bh = GT_A // tn, GT_H // tn

    def body(dm_ref, ya_ref, yh_ref, ga_ref, gh_ref, dya_ref, dyh_ref, dga_ref, dgh_ref):
        dmv = dm_ref[...]
        sa, sh = _sig(ga_ref[...]), _sig(gh_ref[...])
        dya_ref[...] = (dmv * sa).astype(BF16)
        dyh_ref[...] = (dmv * sh).astype(BF16)
        dga_ref[...] = (dmv * ya_ref[...] * (sa * (1.0 - sa))).astype(BF16)
        dgh_ref[...] = (dmv * yh_ref[...] * (sh * (1.0 - sh))).astype(BF16)

    tr = min(s, TALL)
    blk = lambda f: pl.BlockSpec((tr, tn), f)
    nat = blk(lambda i, j: (i, j))
    return pl.pallas_call(body, grid=(s // tr, D // tn),
                          in_specs=[nat, nat, nat, blk(lambda i, j: (i, j + ba)), blk(lambda i, j: (i, j + bh))],
                          out_specs=[nat] * 4, out_shape=[_sds((s, D), BF16)] * 4,
                          compiler_params=_params(2), name="merge_bwd")(dm, y_a, y_h, proj, proj)


def _hgout_fwd(o_raw, proj, hg_norm):
    s = o_raw.shape[0]
    bg = G_H // LANE

    def body(o_ref, g_ref, n_ref, out_ref):
        ov = o_ref[...]
        r = lax.rsqrt(jnp.mean(ov * ov, axis=-1, keepdims=True) + EPS)
        out_ref[...] = (ov * r * n_ref[...] * _sig(g_ref[...])).astype(out_ref.dtype)

    tr = min(s, TALL)
    blk = lambda f: pl.BlockSpec((tr, LANE), f)
    return pl.pallas_call(body, grid=(s // tr, HG_HEADS),
                          in_specs=[blk(lambda i, h: (i, h)), blk(lambda i, h: (i, h + bg)), pl.BlockSpec((1, LANE), lambda i, h: (0, 0))],
                          out_specs=blk(lambda i, h: (i, h)), out_shape=_sds((s, HG_W), BF16),
                          compiler_params=_params(2), name="hgout_fwd")(o_raw, proj, hg_norm)


def _hgout_bwd(d_out, o_raw, proj, hg_norm):
    s = o_raw.shape[0]
    bg = G_H // LANE

    def body(d_ref, o_ref, g_ref, n_ref, do_ref, dg_ref, dn_ref):
        first = jnp.logical_and(pl.program_id(0) == 0, pl.program_id(1) == 0)
        ov, dv, nv = o_ref[...], d_ref[...], n_ref[...]
        sg = _sig(g_ref[...])
        r = lax.rsqrt(jnp.mean(ov * ov, axis=-1, keepdims=True) + EPS)
        oh = ov * r
        d_on = dv * sg
        dg_ref[...] = (dv * (oh * nv) * (sg * (1.0 - sg))).astype(dg_ref.dtype)
        t = d_on * nv
        do_ref[...] = r * (t - oh * jnp.mean(t * oh, axis=-1, keepdims=True))
        _acc_rows(dn_ref, first, jnp.sum(d_on * oh, axis=0, keepdims=True))

    tr = min(s, TALL)
    blk = lambda f: pl.BlockSpec((tr, LANE), f)
    vec = pl.BlockSpec((1, LANE), lambda i, h: (0, 0))
    return pl.pallas_call(body, grid=(s // tr, HG_HEADS),
                          in_specs=[blk(lambda i, h: (i, h)), blk(lambda i, h: (i, h)), blk(lambda i, h: (i, h + bg)), vec],
                          out_specs=[blk(lambda i, h: (i, h)), blk(lambda i, h: (i, h)), vec],
                          out_shape=[_sds((s, HG_W), F32), _sds((s, HG_W), BF16), _sds((1, LANE), F32)],
                          compiler_params=_params(2), name="hgout_bwd")(d_out, o_raw, proj, hg_norm)


def _rope(t, cos, s_lo, s_hi):
    return t * cos + pltpu.roll(t, LANE - ROT // 2, 1) * s_lo + pltpu.roll(t, ROT // 2, 1) * s_hi


def _rope_wide(t, cos, s_lo, s_hi):
    return jnp.concatenate([_rope(t[:, k * LANE:(k + 1) * LANE], cos, s_lo, s_hi) for k in range(t.shape[1] // LANE)], axis=1)


def _attn_mask(has_prev):
    kj = lax.broadcasted_iota(jnp.int32, (2 * BLK, BLK), 0)
    qi = lax.broadcasted_iota(jnp.int32, (2 * BLK, BLK), 1)
    rel = BLK + qi - kj
    band = jnp.logical_and(rel >= 0, rel < BLK)
    return jnp.logical_and(band, jnp.logical_or(has_prev, kj >= BLK))


def _attn_specs():
    prev = lambda i: jnp.maximum(i - 1, 0)
    kb, vb = K_A // LANE, V_A // LANE
    blk = lambda f: pl.BlockSpec((BLK, LANE), f)
    tabs = [blk(lambda i: (i, 0))] * 3 + [blk(lambda i: (prev(i), 0))] * 3
    return [pl.BlockSpec((BLK, ATT_W), lambda i: (i, 0)), blk(lambda i: (i, kb)), blk(lambda i: (prev(i), kb)),
            blk(lambda i: (i, vb)), blk(lambda i: (prev(i), vb))] + tabs + [pl.BlockSpec((1, LANE), lambda i: (0, 0))]


def _attn_logits(qh, kg):
    return _dot(kg, qh, NT)


def _attn_probs(raw, mask, sk):
    logits = jnp.where(mask, raw * (HEAD_DIM ** -0.5), -jnp.inf)
    m = jnp.maximum(jnp.max(logits, axis=0, keepdims=True), sk)
    p = jnp.exp(logits - m)
    e_sink = jnp.exp(sk - m)
    inv = 1.0 / (jnp.sum(p, axis=0, keepdims=True) + e_sink)
    return p, inv, e_sink * inv


def _attn_fwd(proj, tabs, sinks):
    s = proj.shape[0]

    def body(q_ref, kc_ref, kp_ref, vc_ref, vp_ref, c0, l0, h0, c1, l1, h1, sk_ref, o_ref):
        i = pl.program_id(0)
        mask = _attn_mask(i > 0)
        q = _rope_wide(q_ref[...], c0[...], l0[...], h0[...]).astype(BF16)
        kk = jnp.concatenate([_rope(kp_ref[...], c1[...], l1[...], h1[...]), _rope(kc_ref[...], c0[...], l0[...], h0[...])], axis=0).astype(BF16)
        v_t = jnp.concatenate([vp_ref[...], vc_ref[...]], axis=0).T.astype(BF16)
        part = lambda t, h: t[:, h * HEAD_DIM:(h + 1) * HEAD_DIM]
        k_heads = [part(kk, g) for g in range(KV_HEADS)]

        def head(h):
            g = h // GROUP
            raw = _attn_logits(part(q, h), k_heads[g])
            yield
            p, inv, _ = _attn_probs(raw, mask, sk_ref[:, h:h + 1])
            yield
            out_t = _dot(v_t[g * HEAD_DIM:(g + 1) * HEAD_DIM], p.astype(BF16), NN)
            yield
            return out_t * inv

        o_ref[...] = jnp.concatenate(_interleave([head(h) for h in range(ATT_HEADS)]), axis=0).T.astype(o_ref.dtype)

    return pl.pallas_call(body, grid=(s // BLK,), in_specs=_attn_specs(),
                          out_specs=pl.BlockSpec((BLK, ATT_W), lambda i: (i, 0)), out_shape=_sds((s, ATT_W), BF16),
                          compiler_params=_params(1), name="attn_fwd")(proj, proj, proj, proj, proj, *tabs, *tabs, sinks)


def _attn_bwd(proj, tabs, sinks, d_att):
    s = proj.shape[0]

    def body(q_ref, kc_ref, kp_ref, vc_ref, vp_ref, c0, l0, h0, c1, l1, h1, sk_ref, do_ref, dq_ref, dk_ref, dv_ref, ds_ref):
        i = pl.program_id(0)

        @pl.when(i == 0)
        def _():
            dk_ref[...] = jnp.zeros_like(dk_ref)
            dv_ref[...] = jnp.zeros_like(dv_ref)
            ds_ref[...] = jnp.zeros_like(ds_ref)

        mask = _attn_mask(i > 0)
        q = _rope_wide(q_ref[...], c0[...], l0[...], h0[...]).astype(BF16)
        kk = jnp.concatenate([_rope(kp_ref[...], c1[...], l1[...], h1[...]), _rope(kc_ref[...], c0[...], l0[...], h0[...])], axis=0).astype(BF16)
        k_f32 = jnp.concatenate([_rope(kp_ref[...], c1[...], l1[...], h1[...]), _rope(kc_ref[...], c0[...], l0[...], h0[...])], axis=0)
        k_t = k_f32.T.astype(BF16)
        vv = jnp.concatenate([vp_ref[...], vc_ref[...]], axis=0).astype(BF16)
        d_o = do_ref[...].astype(BF16)
        lane = lax.broadcasted_iota(jnp.int32, (1, LANE), 1)
        part = lambda t, h: t[:, h * HEAD_DIM:(h + 1) * HEAD_DIM]
        k_heads = [part(kk, g) for g in range(KV_HEADS)]
        v_heads = [part(vv, g) for g in range(KV_HEADS)]

        def head(h):
            g = h // GROUP
            qh, doh = part(q, h), part(d_o, h)
            raw = _attn_logits(qh, k_heads[g])
            d_p = _dot(v_heads[g], doh, NT)
            yield
            p, inv, p_sink = _attn_probs(raw, mask, sk_ref[:, h:h + 1])
            prob = p * inv
            dv = _dot(prob.astype(BF16), doh, NN)
            yield
            dd = jnp.sum(prob * d_p, axis=0, keepdims=True)
            d_s = (prob * (d_p - dd)).astype(BF16)
            d_sink = jnp.where(lane == h, -jnp.sum(p_sink * dd, axis=1, keepdims=True), 0.0)
            dq_t = _dot(k_t[g * HEAD_DIM:(g + 1) * HEAD_DIM], d_s, NN)
            dk = _dot(d_s, qh, NN)
            yield
            return dq_t * (HEAD_DIM ** -0.5), dk * (HEAD_DIM ** -0.5), dv, d_sink

        per_head = _interleave([head(h) for h in range(ATT_HEADS)])
        dqs = [jnp.concatenate([t[0] for t in per_head], axis=0).T]
        group_sum = lambda k, g: functools.reduce(jnp.add, [t[k] for t in per_head[g * GROUP:(g + 1) * GROUP]])
        dks = [group_sum(1, g) for g in range(KV_HEADS)]
        dvs = [group_sum(2, g) for g in range(KV_HEADS)]
        d_sink = functools.reduce(jnp.add, [t[3] for t in per_head])
        dq_ref[...] = _rope_wide(jnp.concatenate(dqs, axis=1), c0[...], -l0[...], -h0[...]).astype(dq_ref.dtype)
        d_k = jnp.concatenate(dks, axis=1)
        d_v = jnp.concatenate(dvs, axis=1)
        cur = pl.ds(pl.multiple_of(i * BLK, BLK), BLK)
        prv = pl.ds(pl.multiple_of(jnp.maximum(i - 1, 0) * BLK, BLK), BLK)
        dk_ref[prv, :] += _rope(d_k[:BLK], c1[...], -l1[...], -h1[...])
        dk_ref[cur, :] += _rope(d_k[BLK:], c0[...], -l0[...], -h0[...])
        dv_ref[prv, :] += d_v[:BLK]
        dv_ref[cur, :] += d_v[BLK:]
        ds_ref[...] += d_sink

    full = pl.BlockSpec((s, LANE), lambda i: (0, 0))
    return pl.pallas_call(body, grid=(s // BLK,), in_specs=_attn_specs() + [pl.BlockSpec((BLK, ATT_W), lambda i: (i, 0))],
                          out_specs=[pl.BlockSpec((BLK, ATT_W), lambda i: (i, 0)), full, full, pl.BlockSpec((1, LANE), lambda i: (0, 0))],
                          out_shape=[_sds((s, ATT_W), BF16), _sds((s, LANE), F32), _sds((s, LANE), F32), _sds((1, LANE), F32)],
                          compiler_params=_params(1), name="attn_bwd")(proj, proj, proj, proj, proj, *tabs, *tabs, sinks, d_att)


def _tri_matmul(tri, t):
    hi = t.astype(BF16)
    r1 = t - hi.astype(F32)
    mid = r1.astype(BF16)
    lo = (r1 - mid.astype(F32)).astype(BF16)
    return _dot(tri, hi, NN) + _dot(tri, mid, NN) + _dot(tri, lo, NN)


def _lower_bound(hl):
    a, b = hl[0:1, :], hl[1:2, :]
    mx = jnp.maximum(a, b)
    ea, eb = jnp.exp(a - mx), jnp.exp(b - mx)
    return ea / (ea + eb)


def _hg_gates(q_raw, f_raw, lb, tri_lower):
    sg = _sig(f_raw)
    f = lb + (1.0 - lb) * sg
    sq = _sig(q_raw)
    b = _tri_matmul(tri_lower, jnp.log(f))
    return sg, f, 1.0 - f, sq, q_raw * sq, b


HG_PAIR_FWD = 8
HG_PAIR_BWD = 8


def _hg_specs(n_map, pair):
    blk = lambda off, p: pl.BlockSpec((HG_TB, LANE), lambda h, n: (n_map(n), off // LANE + pair * h + p))
    return [blk(off, p) for off in (Q_H, F_H, I_H) for p in range(pair)] + [pl.BlockSpec((2, pair * LANE), lambda h, n: (0, h))]


def _interleave(gens):
    out = [None] * len(gens)
    live = list(range(len(gens)))
    while live:
        for k in list(live):
            try:
                next(gens[k])
            except StopIteration as stop:
                out[k] = stop.value
                live.remove(k)
    return out


def _hg_spread():
    c = lax.broadcasted_iota(jnp.int32, (CHUNK, SUB * SUB), 0)
    l = lax.broadcasted_iota(jnp.int32, (CHUNK, SUB * SUB), 1)
    r = lax.broadcasted_iota(jnp.int32, (SUB, SUB * SUB), 0)
    lr = lax.broadcasted_iota(jnp.int32, (SUB, SUB * SUB), 1)
    cols = [(c == lo + (l >> 4)).astype(BF16) for lo in range(0, CHUNK, SUB)]
    tile = [(c == lo + (l & (SUB - 1))).astype(BF16) for lo in range(0, CHUNK, SUB)]
    return cols, tile, (lr & (SUB - 1)) == r, (lr >> 4) == r


def _hg_intra(qs, kk, b, grad=None):
    lane = lax.broadcasted_iota(jnp.int32, (SUB, CHUNK), 1)
    row1 = lax.broadcasted_iota(jnp.int32, (SUB, 1), 0)
    kk_b = kk.astype(BF16)
    if grad is not None:
        d_a, d_at, (cols, tile, diag, block) = grad
    a_blocks, dq_blocks, dk_blocks, db_blocks = [], [], [], []
    dk_left = None
    for j in range(CHUNK // SUB):
        lo = j * SUB
        q_j, k_j, b_j = qs[lo:lo + SUB], kk[lo:lo + SUB], b[lo:lo + SUB]
        es = [jnp.where(row1 >= sx, jnp.exp(jnp.minimum(b_j - b_j[sx:sx + 1], 0.0)), 0.0) for sx in range(SUB)]
        pes = [q_j * e for e in es]
        pe = jnp.concatenate(pes, axis=0).astype(BF16)
        pairs = _dot(pe, kk_b, NT)
        yield
        a_j = jnp.zeros((SUB, CHUNK), F32)
        for sx in range(SUB):
            a_j = jnp.where(lane == lo + sx, pairs[sx * SUB:(sx + 1) * SUB], a_j)
        if grad is not None:
            da_j = d_a[lo:lo + SUB]
            ek = jnp.concatenate([e * k_j[sx:sx + 1] for sx, e in enumerate(es)], axis=0).astype(BF16)
            sel_t = jnp.where(diag, _dot(da_j.astype(BF16), cols[j], NN), 0.0).astype(BF16)
            sel_s = jnp.where(block, _dot(d_at[lo:lo + SUB].astype(BF16), tile[j], NN), 0.0).astype(BF16)
            pek = jnp.concatenate([p * k_j[sx:sx + 1] for sx, p in enumerate(pes)], axis=0).astype(BF16)
            yield
            dq_j = _dot(sel_t, ek, NN)
            dk_j = _dot(sel_s, pe, NN)
            db_j = _dot(sel_t, pek, NN) - _dot(sel_s, pek, NN)
            yield
        if j > 0:
            ref = b[lo - 1:lo]
            sc_q = jnp.exp(b_j - ref)
            sc_k = jnp.exp(jnp.minimum(ref - b, 0.0))
            qt = (q_j * sc_q).astype(BF16)
            kt = (kk * sc_k).astype(BF16)
            left = _dot(qt, kt, NT)
            yield
            a_j = a_j + jnp.where(lane < lo, left, 0.0)
            if grad is not None:
                da_left = jnp.where(lane < lo, da_j, 0.0).astype(BF16)
                dq_left = _dot(da_left, kt, NN) * sc_q
                dq_j = dq_j + dq_left
                db_j = db_j + q_j * dq_left
                t = _dot(da_left, qt, TN)
                yield
                t = t * sc_k
                dk_left = t if dk_left is None else dk_left + t
        a_blocks.append(a_j)
        if grad is not None:
            dq_blocks.append(dq_j)
            dk_blocks.append(dk_j)
            db_blocks.append(db_j)
    a = jnp.concatenate(a_blocks, axis=0)
    if grad is None:
        return a
    return a, jnp.concatenate(dq_blocks, axis=0), jnp.concatenate(dk_blocks, axis=0) + dk_left, jnp.concatenate(db_blocks, axis=0) - kk * dk_left


def _hgrn_fwd(proj, hl):
    s = proj.shape[0]
    n_chunk = HG_TB // CHUNK
    pair = HG_PAIR_FWD

    def body(*refs):
        q_refs, f_refs, i_refs = refs[:pair], refs[pair:2 * pair], refs[2 * pair:3 * pair]
        hl_ref, o_ref, st_out_ref, st_ref = refs[3 * pair:]

        @pl.when(pl.program_id(1) == 0)
        def _():
            st_ref[...] = jnp.zeros_like(st_ref)

        r_i = lax.broadcasted_iota(jnp.int32, (CHUNK, CHUNK), 0)
        c_i = lax.broadcasted_iota(jnp.int32, (CHUNK, CHUNK), 1)
        tri_lower = (r_i >= c_i).astype(BF16)

        def chunk(c, carry):
            rows = pl.ds(pl.multiple_of(c * CHUNK, CHUNK), CHUNK)
            def head(p):
                cols = slice(p * LANE, (p + 1) * LANE)
                lb = _lower_bound(hl_ref[:, cols])
                v = i_refs[p][rows, :].astype(BF16)
                _, _, kk, _, qs, b = _hg_gates(q_refs[p][rows, :], f_refs[p][rows, :], lb, tri_lower)
                yield
                st = st_ref[p]
                st_b = st.astype(BF16)
                st_out_ref[p, c] = st_b
                o_state = _dot((qs * jnp.exp(b)).astype(BF16), st_b, NT)
                b_last = b[CHUNK - 1:CHUNK, :]
                st_new = _dot(v, (kk * jnp.exp(b_last - b)).astype(BF16), TN)
                a = yield from _hg_intra(qs, kk, b)
                st_ref[p] = st * jnp.exp(b_last) + st_new
                o_ref[rows, cols] = o_state + _dot(a.astype(BF16), v, NN)

            _interleave([head(p) for p in range(pair)])
            return carry

        lax.fori_loop(0, n_chunk, chunk, 0)

    return pl.pallas_call(
        body, grid=(HG_HEADS // pair, s // HG_TB), in_specs=_hg_specs(lambda n: n, pair),
        out_specs=[pl.BlockSpec((HG_TB, pair * LANE), lambda h, n: (n, h)), pl.BlockSpec((pair, n_chunk, HG_K, HG_K), lambda h, n: (h, n, 0, 0))],
        out_shape=[_sds((s, HG_W), F32), _sds((HG_HEADS, s // CHUNK, HG_K, HG_K), BF16)],
        scratch_shapes=[pltpu.VMEM((pair, HG_K, HG_K), F32)],
        compiler_params=_params(2), name="hgrn_fwd")(*[proj] * (3 * pair), hl)


def _hgrn_bwd(proj, hl, states, d_o):
    s = proj.shape[0]
    n_chunk = HG_TB // CHUNK
    n_blk = s // HG_TB
    pair = HG_PAIR_BWD
    rev = lambda n: n_blk - 1 - n

    def body(*refs):
        q_refs, f_refs, i_refs = refs[:pair], refs[pair:2 * pair], refs[2 * pair:3 * pair]
        hl_ref, st_in_ref, do_ref, dq_ref, df_ref, di_ref, dhl_ref, dst_ref, dlb_ref = refs[3 * pair:]
        n = pl.program_id(1)

        @pl.when(n == 0)
        def _():
            dst_ref[...] = jnp.zeros_like(dst_ref)
            dlb_ref[...] = jnp.zeros_like(dlb_ref)

        r_i = lax.broadcasted_iota(jnp.int32, (CHUNK, CHUNK), 0)
        c_i = lax.broadcasted_iota(jnp.int32, (CHUNK, CHUNK), 1)
        tri_lower = (r_i >= c_i).astype(BF16)
        tri_upper = (r_i <= c_i).astype(BF16)
        row = lax.broadcasted_iota(jnp.int32, (CHUNK, 1), 0)
        spread = _hg_spread()

        def chunk(cc, carry):
            c = n_chunk - 1 - cc
            rows = pl.ds(pl.multiple_of(c * CHUNK, CHUNK), CHUNK)
            def head(p):
                cols = slice(p * LANE, (p + 1) * LANE)
                lb = _lower_bound(hl_ref[:, cols])
                q_raw = q_refs[p][rows, :]
                vb = i_refs[p][rows, :].astype(BF16)
                sg, f, kk, sq, qs, b = _hg_gates(q_raw, f_refs[p][rows, :], lb, tri_lower)
                yield
                e_b = jnp.exp(b)
                qe = qs * e_b
                b_last = b[CHUNK - 1:CHUNK, :]
                e_last = jnp.exp(b_last)
                e_kd = jnp.exp(b_last - b)
                kd = kk * e_kd
                st0 = st_in_ref[p, c]
                d_ob = do_ref[rows, cols].astype(BF16)
                dst = dst_ref[p]
                dst_b = dst.astype(BF16)
                d_a = jnp.where(r_i >= c_i, _dot(d_ob, vb, NT), 0.0)
                d_at = jnp.where(r_i <= c_i, _dot(vb, d_ob, NT), 0.0)
                d_v_st = _dot(kd.astype(BF16), dst_b, NT)
                d_kd = _dot(vb, dst_b, NN)
                d_qe = _dot(d_ob, st0, NN)
                dst_new = _dot(d_ob, qe.astype(BF16), TN)
                yield
                a, dqs, dkk, d_b = yield from _hg_intra(qs, kk, b, (d_a, d_at, spread))
                d_v = _dot(a.astype(BF16), d_ob, TN) + d_v_st
                dqs_st = d_qe * e_b
                dkk_st = d_kd * e_kd
                dqs = dqs + dqs_st
                dkk = dkk + dkk_st
                d_b_last = jnp.sum(d_kd * kd, axis=0, keepdims=True) + jnp.sum(dst * st0.astype(F32), axis=0, keepdims=True) * e_last
                d_b = d_b + qs * dqs_st - kk * dkk_st + jnp.where(row == CHUNK - 1, d_b_last, 0.0)
                d_g = _tri_matmul(tri_upper, d_b)
                dst_ref[p] = dst_new + dst * e_last
                yield
                d_f = d_g / f - dkk
                dlb_ref[:, cols] += jnp.sum(d_f * (1.0 - sg), axis=0, keepdims=True)
                dq_ref[rows, cols] = (dqs * (sq * (1.0 + q_raw * (1.0 - sq)))).astype(dq_ref.dtype)
                df_ref[rows, cols] = (d_f * (1.0 - lb) * (sg * (1.0 - sg))).astype(df_ref.dtype)
                di_ref[rows, cols] = d_v.astype(di_ref.dtype)

            _interleave([head(p) for p in range(pair)])
            return carry

        lax.fori_loop(0, n_chunk, chunk, 0)

        @pl.when(n == n_blk - 1)
        def _():
            lb = _lower_bound(hl_ref[...])
            d_hl0 = dlb_ref[...] * (lb * (1.0 - lb))
            dhl_ref[...] = jnp.concatenate([d_hl0, -d_hl0], axis=0)

    out_blk = pl.BlockSpec((HG_TB, pair * LANE), lambda h, n: (rev(n), h))
    return pl.pallas_call(
        body, grid=(HG_HEADS // pair, n_blk),
        in_specs=_hg_specs(rev, pair) + [pl.BlockSpec((pair, n_chunk, HG_K, HG_K), lambda h, n: (h, rev(n), 0, 0)), out_blk],
        out_specs=[out_blk, out_blk, out_blk, pl.BlockSpec((2, pair * LANE), lambda h, n: (0, h))],
        out_shape=[_sds((s, HG_W), BF16)] * 3 + [_sds((2, HG_W), F32)],
        scratch_shapes=[pltpu.VMEM((pair, HG_K, HG_K), F32), pltpu.VMEM((1, pair * LANE), F32)],
        compiler_params=_params(2), name="hgrn_bwd")(*[proj] * (3 * pair), hl, states, d_o)


def _mod_part(c_all, w_shard, b_shard):
    n = w_shard.shape[1]
    tn = 512

    def body(c_ref, w_ref, b_ref, o_ref):
        o_ref[...] = _dot(c_ref[...].astype(BF16), w_ref[...].astype(BF16), NN) + b_ref[...]

    return pl.pallas_call(body, grid=(n // tn,),
                          in_specs=[pl.BlockSpec((N_DEV, D), lambda j: (0, 0)), pl.BlockSpec((D, tn), lambda j: (0, j)), pl.BlockSpec((1, tn), lambda j: (0, j))],
                          out_specs=pl.BlockSpec((N_DEV, tn), lambda j: (0, j)), out_shape=_sds((N_DEV, n), F32),
                          compiler_params=_params(1, 32 << 20), name="mod_part")(c_all, w_shard, b_shard)


def _grad_w_ada(c_all_t, dmod_cols):
    n = dmod_cols.shape[1]
    tn = 512

    def body(c_ref, d_ref, o_ref):
        cv = c_ref[...].astype(BF16).astype(F32)
        dv = d_ref[...].astype(BF16).astype(F32)
        acc = cv[:, 0:1] * dv[0:1, :]
        for k in range(1, N_DEV):
            acc = acc + cv[:, k:k + 1] * dv[k:k + 1, :]
        o_ref[...] = acc

    return pl.pallas_call(body, grid=(n // tn,),
                          in_specs=[pl.BlockSpec((D, N_DEV), lambda j: (0, 0)), pl.BlockSpec((N_DEV, tn), lambda j: (0, j))],
                          out_specs=pl.BlockSpec((D, tn), lambda j: (0, j)), out_shape=_sds((D, n), F32),
                          compiler_params=_params(1, 32 << 20), name="grad_w_ada")(c_all_t, dmod_cols)


def _row_tile(r, c, max_elems=1 << 18):
    if r * c <= max_elems or r % 8:
        return r
    best = 8
    for t in range(8, r + 1, 8):
        if r % t == 0 and t * c <= max_elems:
            best = t
    return best


WIDE_TILE = 5 << 17


def _adamw(pieces, w, m, v, name, emit_grad=True, own=None):
    p, r, c = pieces.shape
    tr = _row_tile(r, c)
    c1 = 1.0 / (1.0 - ADAM_B1 ** ADAM_STEP)
    c2 = 1.0 / (1.0 - ADAM_B2 ** ADAM_STEP)

    def body(*refs):
        if own is None:
            p_ref, w_ref, m_ref, v_ref, *outs = refs
            g = p_ref[0].astype(F32)
        else:
            o_ref, p_ref, w_ref, m_ref, v_ref, *outs = refs
            g = o_ref[...].astype(F32) + p_ref[0].astype(F32)
        for k in range(1, p):
            g = g + p_ref[k].astype(F32)
        m2 = ADAM_B1 * m_ref[...] + (1.0 - ADAM_B1) * g
        v2 = ADAM_B2 * v_ref[...] + (1.0 - ADAM_B2) * (g * g)
        delta = -ADAM_LR * ((m2 * c1) / (jnp.sqrt(v2 * c2) + ADAM_EPS) + ADAM_WD * w_ref[...])
        if emit_grad:
            outs[0][...] = g
        outs[-3][...] = delta
        outs[-2][...] = m2
        outs[-1][...] = v2

    blk = pl.BlockSpec((tr, c), lambda i: (i, 0))
    n_out = 4 if emit_grad else 3
    lead = [] if own is None else [own]
    return pl.pallas_call(body, grid=(r // tr,), in_specs=[blk] * len(lead) + [pl.BlockSpec((p, tr, c), lambda i: (0, i, 0)), blk, blk, blk],
                          out_specs=[blk] * n_out, out_shape=[_sds((r, c), F32)] * n_out,
                          compiler_params=_params(1, 48 << 20), name=name)(*lead, pieces, w, m, v)


def _my_coords():
    return lax.axis_index("x"), lax.axis_index("y"), lax.axis_index("c")


def _flip(coords, k):
    x, y, c = coords
    return (1 - x if k & 4 else x, 1 - y if k & 2 else y, 1 - c if k & 1 else c)


def _lin(coords):
    return 4 * coords[0] + 2 * coords[1] + coords[2]


def _exchange_small(x3, bcast, name):
    n = x3.shape[2]

    def body(x_ref, o_ref, send_sems, recv_sems):
        me = _my_coords()
        my_id = _lin(me)
        o_ref[pl.ds(my_id, 1)] = x_ref[pl.ds(0 if bcast else my_id, 1)]
        copies = []
        for k in range(1, N_DEV):
            peer = _flip(me, k)
            src = x_ref.at[0 if bcast else _lin(peer)]
            cp = pltpu.make_async_remote_copy(src_ref=src, dst_ref=o_ref.at[my_id], send_sem=send_sems.at[k], recv_sem=recv_sems.at[k],
                                              device_id=peer, device_id_type=MESH)
            cp.start()
            copies.append(cp)
        for k in range(1, N_DEV):
            peer = _flip(me, k)
            pltpu.make_async_remote_copy(src_ref=x_ref.at[0], dst_ref=o_ref.at[_lin(peer)], send_sem=send_sems.at[k], recv_sem=recv_sems.at[k],
                                         device_id=peer, device_id_type=MESH).wait_recv()
        for cp in copies:
            cp.wait_send()

    vm = pl.BlockSpec(memory_space=pltpu.VMEM)
    return pl.pallas_call(body, in_specs=[vm], out_specs=vm, out_shape=_sds((N_DEV, 1, n), F32),
                          scratch_shapes=[pltpu.SemaphoreType.DMA((N_DEV,)), pltpu.SemaphoreType.DMA((N_DEV,))], name=name)(x3)


HBM_SPEC = pl.BlockSpec(memory_space=pltpu.HBM)
SEM_SPEC = pl.BlockSpec(memory_space=pltpu.SEMAPHORE)
ANY_SPEC = pl.BlockSpec(memory_space=pl.ANY)
DATAFLOW = pltpu.SideEffectType.DATAFLOW_SIDE_EFFECTING
GATHER_FLIPS = (1, 2, 4, 6)
PASS_FLIPS = (2, 4, 6)
TOKEN = (8, LANE)


def _hbm(t):
    return pltpu.with_memory_space_constraint(t, pltpu.HBM)


def _hbm_like(ts):
    return [pltpu.HBM(t.shape, t.dtype) for t in ts]


def _split_start(issue, srcs, lands, n_sem, name, deps=()):
    n, nd = len(srcs), len(deps)

    def body(*refs):
        issue(refs[:n], refs[n:2 * n], refs[2 * n + nd], refs[2 * n + nd + 1])
        refs[-1][...] = jnp.zeros(TOKEN, F32)

    outs = pl.pallas_call(
        body, name=name,
        out_shape=(pltpu.SemaphoreType.DMA((n_sem,)), pltpu.SemaphoreType.DMA((n_sem,)), *_hbm_like(srcs), *_hbm_like(lands), _sds(TOKEN, F32)),
        in_specs=[HBM_SPEC] * (2 * n) + [ANY_SPEC] * nd,
        out_specs=(SEM_SPEC, SEM_SPEC, *[HBM_SPEC] * (2 * n), pl.BlockSpec(memory_space=pltpu.VMEM)),
        input_output_aliases={i: 2 + i for i in range(2 * n)},
        compiler_params=pltpu.CompilerParams(has_side_effects=DATAFLOW))(*[_hbm(t) for t in srcs], *[_hbm(t) for t in lands], *deps)
    return dict(sems=outs[:2], thru=list(outs[2:2 + 2 * n]), token=outs[-1], n=n)


def _split_wait(finish, handle, after, name):
    n = handle["n"]
    thru = handle["thru"]

    def body(*refs):
        finish(refs[:n], refs[n:2 * n], refs[2 * n], refs[2 * n + 1])

    outs = pl.pallas_call(
        body, name=name, out_shape=_hbm_like(thru), in_specs=[HBM_SPEC] * (2 * n) + [SEM_SPEC, SEM_SPEC] + [ANY_SPEC] * len(after),
        out_specs=[HBM_SPEC] * (2 * n), input_output_aliases={i: i for i in range(2 * n)},
        compiler_params=pltpu.CompilerParams(has_side_effects=DATAFLOW))(*thru, *handle["sems"], *after)
    return list(outs[:n]), list(outs[n:])


def _gather_start(shards, name, deps=()):
    n = len(shards)
    my_id = _lin(_my_coords())
    lands = [lax.dynamic_update_slice(lax.empty((N_DEV,) + t.shape, t.dtype), t[None], (my_id, 0, 0)) for t in shards]

    def issue(src, land, send_sems, recv_sems):
        me = _my_coords()
        for w in range(n):
            for j, k in enumerate(GATHER_FLIPS):
                q = len(GATHER_FLIPS) * w + j
                pltpu.make_async_remote_copy(src_ref=src[w], dst_ref=land[w].at[_lin(me)], send_sem=send_sems.at[q], recv_sem=recv_sems.at[q],
                                             device_id=_flip(me, k), device_id_type=MESH).start()

    return _split_start(issue, shards, lands, len(GATHER_FLIPS) * n, name, deps)


def _gather_wait(handle, after, name):
    n = handle["n"]

    def finish(src, land, send_sems, recv_sems):
        me = _my_coords()
        for w in range(n):
            for j, k in enumerate(GATHER_FLIPS):
                q = len(GATHER_FLIPS) * w + j
                peer = _flip(me, k)
                cp = pltpu.make_async_remote_copy(src_ref=src[w], dst_ref=land[w].at[_lin(peer)], send_sem=send_sems.at[q], recv_sem=recv_sems.at[q],
                                                  device_id=peer, device_id_type=MESH)
                cp.wait_send()
                cp.wait_recv()

    return _split_wait(finish, handle, after, name)[1]


def _gather_pass(lands, name):
    n = len(lands)
    n_p = len(PASS_FLIPS)

    def body(*refs):
        land = refs[n:2 * n]
        send_sems, recv_sems = refs[2 * n:]
        me = _my_coords()
        sibling = _flip(me, 1)
        sent = []
        for w in range(n):
            for j, k in enumerate(PASS_FLIPS):
                blk = land[w].at[_lin(_flip(me, k))]
                cp = pltpu.make_async_remote_copy(src_ref=blk, dst_ref=blk, send_sem=send_sems.at[n_p * w + j], recv_sem=recv_sems.at[n_p * w + j],
                                                  device_id=sibling, device_id_type=MESH)
                cp.start()
                sent.append(cp)
        for w in range(n):
            for j, k in enumerate(PASS_FLIPS):
                blk = land[w].at[_lin(_flip(me, k + 1))]
                pltpu.make_async_remote_copy(src_ref=blk, dst_ref=blk, send_sem=send_sems.at[n_p * w + j], recv_sem=recv_sems.at[n_p * w + j],
                                             device_id=sibling, device_id_type=MESH).wait_recv()
        for cp in sent:
            cp.wait_send()

    return pl.pallas_call(body, in_specs=[ANY_SPEC] * n, out_specs=[ANY_SPEC] * n, out_shape=[_sds(t.shape, t.dtype) for t in lands],
                          input_output_aliases={i: i for i in range(n)},
                          scratch_shapes=[pltpu.SemaphoreType.DMA((n_p * n,)), pltpu.SemaphoreType.DMA((n_p * n,))], name=name)(*lands)


CHIP_FLIPS = (0, 2, 4, 6)


def _pair_copy(src, land, send_sems, recv_sems, w, j):
    me = _my_coords()
    q = len(CHIP_FLIPS) * w + j
    return pltpu.make_async_remote_copy(src_ref=src[w].at[_lin(_flip(me, CHIP_FLIPS[j] + 1))], dst_ref=land[w].at[j], send_sem=send_sems.at[q],
                                        recv_sem=recv_sems.at[q], device_id=_flip(me, 1), device_id_type=MESH)


def _pair_exchange(grads, name):
    n = len(grads)

    def body(*refs):
        src, land = refs[:n], refs[n:2 * n]
        send_sems, recv_sems = refs[2 * n:]
        sent = [_pair_copy(src, land, send_sems, recv_sems, w, j) for w in range(n) for j in range(len(CHIP_FLIPS))]
        for cp in sent:
            cp.start()
        for cp in sent:
            cp.wait_recv()
        for cp in sent:
            cp.wait_send()

    outs = pl.pallas_call(body, in_specs=[ANY_SPEC] * n, out_specs=[ANY_SPEC] * n,
                          out_shape=[_sds((len(CHIP_FLIPS),) + g.shape[1:], g.dtype) for g in grads],
                          scratch_shapes=[pltpu.SemaphoreType.DMA((len(CHIP_FLIPS) * n,))] * 2, name=name)(*grads)
    return list(outs)


def _pair_start(grads, name, deps=()):
    n = len(grads)
    lands = [lax.empty((len(CHIP_FLIPS),) + g.shape[1:], g.dtype) for g in grads]

    def issue(src, land, send_sems, recv_sems):
        for w in range(n):
            for j in range(len(CHIP_FLIPS)):
                _pair_copy(src, land, send_sems, recv_sems, w, j).start()

    return _split_start(issue, grads, lands, len(CHIP_FLIPS) * n, name, deps)


def _pair_wait(handle, after, name):
    n = handle["n"]

    def finish(src, land, send_sems, recv_sems):
        for w in range(n):
            for j in range(len(CHIP_FLIPS)):
                cp = _pair_copy(src, land, send_sems, recv_sems, w, j)
                cp.wait_send()
                cp.wait_recv()

    return _split_wait(finish, handle, after, name)


def _pair_add(grad, theirs, name):
    p, r, c = theirs.shape
    tr = _row_tile(r, c, WIDE_TILE)
    me = _my_coords()
    ids = jnp.stack([_lin(_flip(me, k)) for k in CHIP_FLIPS]).astype(jnp.int32)

    def body(ids_ref, a_ref, b_ref, o_ref):
        o_ref[...] = (a_ref[...].astype(F32) + b_ref[...].astype(F32)).astype(o_ref.dtype)

    blk = pl.BlockSpec((None, tr, c), lambda j, i, ids_ref: (j, i, 0))
    return pl.pallas_call(
        body, out_shape=_sds((p, r, c), theirs.dtype), compiler_params=_params(2), name=name,
        grid_spec=pltpu.PrefetchScalarGridSpec(
            num_scalar_prefetch=1, grid=(p, r // tr),
            in_specs=[pl.BlockSpec((None, tr, c), lambda j, i, ids_ref: (ids_ref[j], i, 0)), blk], out_specs=blk))(ids, grad, theirs)


def _chips_start(parts, name, deps=()):
    n = len(parts)
    n_c = len(CHIP_FLIPS) - 1
    lands = [lax.empty((n_c,) + t.shape[1:], t.dtype) for t in parts]

    def issue(src, land, send_sems, recv_sems):
        me = _my_coords()
        for w in range(n):
            for j in range(1, n_c + 1):
                q = n_c * w + j - 1
                pltpu.make_async_remote_copy(src_ref=src[w].at[j], dst_ref=land[w].at[j - 1], send_sem=send_sems.at[q], recv_sem=recv_sems.at[q],
                                             device_id=_flip(me, CHIP_FLIPS[j]), device_id_type=MESH).start()

    return _split_start(issue, parts, lands, n_c * n, name, deps)


def _chips_wait(handle, after, name):
    n = handle["n"]
    n_c = len(CHIP_FLIPS) - 1

    def finish(src, land, send_sems, recv_sems):
        me = _my_coords()
        for w in range(n):
            for j in range(1, n_c + 1):
                q = n_c * w + j - 1
                cp = pltpu.make_async_remote_copy(src_ref=src[w].at[j], dst_ref=land[w].at[j - 1], send_sem=send_sems.at[q], recv_sem=recv_sems.at[q],
                                                  device_id=_flip(me, CHIP_FLIPS[j]), device_id_type=MESH)
                cp.wait_send()
                cp.wait_recv()

    return _split_wait(finish, handle, after, name)


def _after(t, *tokens):
    for tok in tokens:
        t = t + tok[0:1, 0:1]
    return t


def _rope_tables(positions):
    half = ROT // 2
    inv_freq = ROPE_THETA ** (-jnp.arange(0, ROT, 2, dtype=F32) / ROT)
    ang = positions.astype(F32).reshape(-1, 1) * inv_freq
    cos, sin = jnp.cos(ang), jnp.sin(ang)
    s = ang.shape[0]
    pad = jnp.zeros((s, HEAD_DIM - ROT), F32)
    zero = jnp.zeros((s, half), F32)
    two = lambda t: jnp.concatenate([t, t], axis=1)
    return (two(jnp.concatenate([cos, cos, pad + 1.0], axis=1)), two(jnp.concatenate([-sin, zero, pad], axis=1)),
            two(jnp.concatenate([zero, sin, pad], axis=1)))


def _local_step(x, tgt, tabs, mod, sinks_pad, hl, hg_norm, g_pre_mix, g_post_mix, g_pre_ffn, g_post_ffn, weights, scatter, scatter_on):
    s = x.shape[0]
    h1 = _pre_fwd(x, g_pre_mix, mod, 1, 0, "pre_mix_fwd")
    (w_in_a,) = weights("in_a", h1)
    proj = _mm_nt(h1[:, :D // 2], w_in_a, s, 256, D // 2, F32, "proj_mm_a")
    (w_in_b,) = weights("in_b", proj)
    proj = _mm_nt(h1[:, D // 2:], w_in_b, s, 256, D // 2, F32, "proj_mm_b", add=proj)
    att = _attn_fwd(proj, tabs, sinks_pad)
    o_raw, states = _hgrn_fwd(proj, hl)
    ohg = _hgout_fwd(o_raw, proj, hg_norm)
    w_attn_dm, w_hgrn_dm, w_out = weights("mix", ohg)
    y_a = _mm_nn_dm(att, w_attn_dm, s, F32, "attn_proj_mm")
    y_h = _mm_nn_dm(ohg, w_hgrn_dm, s, F32, "hgrn_proj_mm")
    merged = _merge_fwd(y_a, y_h, proj)
    y = _mm_nn(merged, w_out, s, 512, D, F32, "out_mm")
    x1 = _post_fwd(x, y, g_post_mix, mod, 2, "post_mix_fwd")
    h2 = _pre_fwd(x1, g_pre_ffn, mod, 4, 3, "pre_ffn_fwd")
    (w_ffn_in_dm,) = weights("ffn_in", h2)
    gu = _mm_nn_dm(h2, w_ffn_in_dm, s // 2, F32, "ffn_in_mm")
    act = _swiglu_fwd(gu)
    (w_ffn_out,) = weights("ffn_out", act)
    y2 = _mm_nn(act, w_ffn_out, 512, 512, FFN, F32, "ffn_out_mm")
    err, loss = _post_fwd_loss(x1, y2, g_post_ffn, mod, 5, tgt, "post_ffn_loss")
    dy2, d_gate2, dg_post_ffn = _post_bwd(err, y2, g_post_ffn, mod, 5, "post_ffn_bwd")
    gw_ffn_out = _mm_tn(act, dy2, 512, D, BF16, "ffn_out_dw")
    t_pair = scatter([gw_ffn_out.reshape(N_DEV, FFN // N_DEV, D)], "ffn_out")
    d_act = _mm_nt(dy2, w_ffn_out, s, 512, D, F32, "ffn_out_dx", deps=[t_pair])
    dgu = _swiglu_bwd(d_act, gu)
    t_out = scatter_on("ffn_out", dgu)
    gw_ffn_in = _mm_tn_dm(h2, dgu, 1024, BF16, "ffn_in_dw")
    t_pair = scatter([gw_ffn_in], "ffn_in")
    dh2 = _mm_nt_dm(dgu, w_ffn_in_dm, s, 1024, F32, "ffn_in_dx", deps=[t_pair])
    mod = _after(mod, t_out)
    dx1, d_shift2, d_scale2, dg_pre_ffn = _pre_bwd(dh2, x1, err, g_pre_ffn, mod, 4, "pre_ffn_bwd")
    dy, d_gate1, dg_post_mix = _post_bwd(dx1, y, g_post_mix, mod, 2, "post_mix_bwd")
    t_in = scatter_on("ffn_in", dy)
    d_merged = _mm_nt(dy, w_out, s, 512, D, F32, "out_dx")
    gw_out = _mm_tn(merged, dy, 512, D, BF16, "out_dw")
    dy_a, dy_h, d_gate_a, d_gate_h = _merge_bwd(d_merged, y_a, y_h, proj)
    gw_attn = _mm_tn_dm(att, dy_a, ATT_W, BF16, "attn_proj_dw")
    gw_hgrn = _mm_tn_dm(ohg, dy_h, HG_W, BF16, "hgrn_proj_dw")
    t_pair = scatter([gw_attn, gw_hgrn, gw_out.reshape(N_DEV, D // N_DEV, D)], "mix")
    d_att = _mm_nt_dm(dy_a, w_attn_dm, s, ATT_W, F32, "attn_proj_dx")
    d_ohg = _mm_nt_dm(dy_h, w_hgrn_dm, s, HG_W, F32, "hgrn_proj_dx", deps=[t_pair])
    d_o, d_gh, d_hg_norm = _hgout_bwd(d_ohg, o_raw, proj, _after(hg_norm, t_in))
    d_qh, d_fh, d_ih, d_hl = _hgrn_bwd(proj, hl, states, d_o)
    t_mix = scatter_on("mix", d_qh)
    d_qa, d_ka, d_va, d_sinks = _attn_bwd(proj, tabs, _after(sinks_pad, t_mix), d_att)
    d_proj = jnp.concatenate([d_qa, d_ka.astype(BF16), d_va.astype(BF16), d_qh, d_fh, d_ih, d_gh, d_gate_a, d_gate_h], axis=1)
    dh1 = jnp.concatenate([_mm_nn(d_proj, w_half, s // 2, 512, IN_COLS // 2, F32, "proj_dx_" + tag)
                           for tag, w_half in (("a", w_in_a), ("b", w_in_b))], axis=1)
    grad_x, d_shift1, d_scale1, dg_pre_mix = _pre_bwd(dh1, x, dx1, g_pre_mix, mod, 1, "pre_mix_bwd")
    d_mod = jnp.concatenate([d_shift1, d_scale1, d_gate1, d_shift2, d_scale2, d_gate2], axis=1)
    small = [d_mod, dg_pre_mix, dg_post_mix, dg_pre_ffn, dg_post_ffn, d_hl.reshape(1, 2 * HG_W), d_hg_norm, d_sinks]
    return loss, grad_x, small, h1, d_proj


def kernel(x, c, positions, w_ada, b_ada, g_pre_mix, g_post_mix, g_pre_ffn, g_post_ffn, w_in, attn_sinks, w_attn_proj, hg_lower_bounds, hg_norm, w_hgrn_proj, w_out, w_ffn_in, w_ffn_out, loss_target, m_w_ada, m_b_ada, m_g_pre_mix, m_g_post_mix, m_g_pre_ffn, m_g_post_ffn, m_w_in, m_attn_sinks, m_w_attn_proj, m_hg_lower_bounds, m_hg_norm, m_w_hgrn_proj, m_w_out, m_w_ffn_in, m_w_ffn_out, v_w_ada, v_b_ada, v_g_pre_mix, v_g_post_mix, v_g_pre_ffn, v_g_post_ffn, v_w_in, v_attn_sinks, v_w_attn_proj, v_hg_lower_bounds, v_hg_norm, v_w_hgrn_proj, v_w_out, v_w_ffn_in, v_w_ffn_out):
    my_id = _lin(_my_coords())
    s = x.shape[1]
    n_ada = w_ada.shape[2]

    c_all = _exchange_small(c.reshape(1, 1, D), True, "gather_c").reshape(N_DEV, D)
    b_cols = lax.dynamic_slice(b_ada, (0, my_id * n_ada), (1, n_ada))
    mod_part = _mod_part(c_all, w_ada[0], b_cols)
    mod = _exchange_small(mod_part.reshape(N_DEV, 1, n_ada), False, "scatter_mod").reshape(1, N_MOD * D)
    groups = {"in_a": [w_in[0].T[:, :D // 2]], "in_b": [w_in[0].T[:, D // 2:]], "mix": [w_attn_proj[0], w_hgrn_proj[0], w_out[0]],
              "ffn_in": [w_ffn_in[0]], "ffn_out": [w_ffn_out[0]]}

    def start(group, dep):
        shards, dep = lax.optimization_barrier((groups[group], dep))
        return _gather_start([t.astype(BF16) for t in shards], "gather_start_" + group, deps=[dep])

    gathers = {"in_a": start("in_a", mod)}
    gathers["in_b"] = start("in_b", gathers["in_a"]["token"])
    gathers["mix"] = start("mix", gathers["in_b"]["token"])
    gathers["ffn_in"] = start("ffn_in", gathers["mix"]["token"])
    gathers["ffn_out"] = start("ffn_out", gathers["ffn_in"]["token"])

    def weights(group, after):
        after = [after, gathers["ffn_out"]["token"]]
        lands = _gather_pass(_gather_wait(gathers[group], after, "gather_wait_" + group), "gather_pass_" + group)
        if group in ("in_a", "in_b"):
            return (lands[0].reshape(IN_COLS, D // 2),)
        if group == "mix":
            return lands[0], lands[1], lands[2].reshape(D, D)
        return (lands[0],) if group == "ffn_in" else (lands[0].reshape(FFN, D),)

    pairs, scatters = {}, {}

    def scatter(grads, group):
        pairs[group] = _pair_start(grads, "scatter_pair_" + group)
        return pairs[group]["token"]

    def scatter_on(group, after):
        if group in pairs:
            local, theirs = _pair_wait(pairs[group], [after], "scatter_pair_wait_" + group)
        else:
            local, theirs = after, _pair_exchange(after, "scatter_pair_" + group)
        parts = [_pair_add(g, t, "scatter_pair_add_%s_%d" % (group, k)) for k, (g, t) in enumerate(zip(local, theirs))]
        scatters[group] = _chips_start(parts, "scatter_start_" + group)
        return scatters[group]["token"]

    sinks_pad = jnp.pad(attn_sinks, ((0, 0), (0, LANE - ATT_HEADS)))
    loss, grad_x, small, h1, d_proj = _local_step(
        x[0], loss_target[0], _rope_tables(positions), mod, sinks_pad, hg_lower_bounds, hg_norm, g_pre_mix, g_post_mix, g_pre_ffn, g_post_ffn,
        weights, scatter, scatter_on)
    loss = lax.psum(loss[0, 0], ("x", "y", "c"))

    sizes = [t.shape[1] for t in small]
    parts = _exchange_small(jnp.concatenate(small, axis=1).reshape(1, 1, sum(sizes)), True, "gather_small_grads")
    dep = parts
    for half, cols in (("in_a", slice(0, D // 2)), ("in_b", slice(D // 2, D))):
        gw_half = _mm_tn(d_proj, h1[:, cols], 256, D // 2, BF16, "proj_dw_" + half, deps=[dep])
        dep = scatter_on(half, [gw_half.reshape(N_DEV, IN_COLS // N_DEV, D // 2)])
    offs = [sum(sizes[:k]) for k in range(len(sizes))]
    piece = lambda k, n=None: parts[:, :, offs[k]:offs[k] + (sizes[k] if n is None else n)]
    small_w = [(piece(0), b_ada, m_b_ada, v_b_ada), (piece(1), g_pre_mix, m_g_pre_mix, v_g_pre_mix),
               (piece(2), g_post_mix, m_g_post_mix, v_g_post_mix), (piece(3), g_pre_ffn, m_g_pre_ffn, v_g_pre_ffn),
               (piece(4), g_post_ffn, m_g_post_ffn, v_g_post_ffn),
               (piece(5).reshape(N_DEV, 2, HG_W), hg_lower_bounds, m_hg_lower_bounds, v_hg_lower_bounds),
               (piece(6), hg_norm, m_hg_norm, v_hg_norm), (piece(7, ATT_HEADS), attn_sinks, m_attn_sinks, v_attn_sinks)]
    names = ["b_ada", "g_pre_mix", "g_post_mix", "g_pre_ffn", "g_post_ffn", "hg_lower_bounds", "hg_norm", "attn_sinks"]
    res = {n: _adamw(p, w, m, v, "adamw_" + n) for n, (p, w, m, v) in zip(names, small_w)}

    dmod_cols = lax.dynamic_slice(parts.reshape(N_DEV, -1), (0, my_id * n_ada), (N_DEV, n_ada))
    g_w_ada = _grad_w_ada(c_all.T, dmod_cols)
    res["w_ada"] = [g_w_ada] + list(_adamw(g_w_ada[None], w_ada[0], m_w_ada[0], v_w_ada[0], "adamw_w_ada", emit_grad=False))

    big = {"ffn_out": [("w_ffn_out", w_ffn_out, m_w_ffn_out, v_w_ffn_out)], "ffn_in": [("w_ffn_in", w_ffn_in, m_w_ffn_in, v_w_ffn_in)],
           "mix": [("w_attn_proj", w_attn_proj, m_w_attn_proj, v_w_attn_proj), ("w_hgrn_proj", w_hgrn_proj, m_w_hgrn_proj, v_w_hgrn_proj),
                   ("w_out", w_out, m_w_out, v_w_out)]}
    after = [scatters["in_b"]["token"]]
    for group, members in big.items():
        local, lands = _chips_wait(scatters[group], after, "scatter_wait_" + group)
        for (n, w, m, v), mine, land in zip(members, local, lands):
            res[n] = _adamw(land, w[0], m[0], v[0], "adamw_" + n, own=mine[0])
            after = after + [res[n][1]]
    after = [res[n][1] for n in res]
    halves = [_chips_wait(scatters[half], after, "scatter_wait_" + half) for half in ("in_a", "in_b")]
    own = jnp.concatenate([local[0][0] for local, _ in halves], axis=1)
    land = jnp.concatenate([lands[0] for _, lands in halves], axis=2)
    res["w_in"] = [t.T for t in _adamw(land, w_in[0].T, m_w_in[0].T, v_w_in[0].T, "adamw_w_in", own=own)]

    order = ["w_ada", "b_ada", "g_pre_mix", "g_post_mix", "g_pre_ffn", "g_post_ffn", "w_in", "attn_sinks", "w_attn_proj",
             "hg_lower_bounds", "hg_norm", "w_hgrn_proj", "w_out", "w_ffn_in", "w_ffn_out"]
    lead = {"w_ada", "w_in", "w_attn_proj", "w_hgrn_proj", "w_out", "w_ffn_in", "w_ffn_out"}
    outs = [loss, grad_x[None]]
    for k in range(4):
        outs += [res[n][k][None] if n in lead else res[n][k] for n in order]
    return tuple(outs)
```

```python
import functools

import jax
import jax.numpy as jnp
from jax import lax
from jax.experimental import pallas as pl
from jax.experimental.pallas import tpu as pltpu

F32 = jnp.float32
BF16 = jnp.bfloat16

N_DEV = 8
D = 2048
ATT_HEADS = 16
KV_HEADS = 2
HEAD_DIM = 64
GROUP = ATT_HEADS // KV_HEADS
ATT_W = ATT_HEADS * HEAD_DIM
BLK = 128
ROT = HEAD_DIM // 4
ROPE_THETA = 500000.0
HG_HEADS = 8
HG_K = 128
HG_W = HG_HEADS * HG_K
CHUNK = 64
SUB = 16
FFN = 5632
N_MOD = 6
EPS = 1e-6
LANE = 128
Q_A, K_A, V_A, Q_H, F_H, I_H, G_H, GT_A, GT_H, IN_COLS = 0, 1024, 1152, 1280, 2304, 3328, 4352, 5376, 7424, 9472

ADAM_LR, ADAM_B1, ADAM_B2, ADAM_EPS, ADAM_WD, ADAM_STEP = 0.001, 0.9, 0.999, 1e-08, 0.01, 10

TR = 256
HG_TB = 512
VMEM_BIG = 56 << 20
MESH = pl.DeviceIdType.MESH


def _sds(shape, dtype):
    return jax.ShapeDtypeStruct(shape, dtype)


def _params(n_axes, vmem=None):
    return pltpu.CompilerParams(dimension_semantics=("arbitrary",) * n_axes, vmem_limit_bytes=vmem)


def _sig(t):
    return 1.0 / (1.0 + jnp.exp(-t))


def _dot(a, b, dims):
    return lax.dot_general(a, b, (dims, ((), ())), preferred_element_type=F32)


NN = ((1,), (0,))
NT = ((1,), (1,))
TN = ((0,), (0,))


def _matmul(a, b, a_spec, b_spec, o_spec, out_shape, grid, dims, acc_shape, name, deps=(), add=None):
    nk = grid[2]
    nd = len(deps)
    extra = [] if add is None else [add]

    def body(a_ref, b_ref, *rest):
        o_ref, scratch = rest[nd + len(extra)], rest[nd + len(extra) + 1:]
        part = _dot(a_ref[...], b_ref[...], dims)
        if add is not None:
            assert nk == 1
            part = part + rest[nd][...]
        if nk == 1:
            o_ref[...] = part.astype(o_ref.dtype)
        else:
            acc = scratch[0]
            k = pl.program_id(2)

            @pl.when(k == 0)
            def _():
                acc[...] = part

            @pl.when(k > 0)
            def _():
                acc[...] += part

            @pl.when(k == nk - 1)
            def _():
                o_ref[...] = acc[...].astype(o_ref.dtype)

    return pl.pallas_call(
        body, grid=grid, in_specs=[a_spec, b_spec] + [pl.BlockSpec(memory_space=pl.ANY)] * nd + [o_spec] * len(extra),
        out_specs=o_spec, out_shape=out_shape, scratch_shapes=[pltpu.VMEM(acc_shape, F32)] if nk > 1 else [],
        input_output_aliases={2 + nd: 0} if extra else {},
        compiler_params=_params(3, VMEM_BIG), name=name)(a, b, *deps, *extra)


def _mm_nn(a, b, tm, tn, tk, out_dtype, name):
    m, k = a.shape
    n = b.shape[1]
    return _matmul(a, b, pl.BlockSpec((tm, tk), lambda j, i, kk: (i, kk)), pl.BlockSpec((tk, tn), lambda j, i, kk: (kk, j)),
                   pl.BlockSpec((tm, tn), lambda j, i, kk: (i, j)), _sds((m, n), out_dtype),
                   (n // tn, m // tm, k // tk), NN, (tm, tn), name)


def _mm_nn_dm(a, b, tm, out_dtype, name):
    m, k = a.shape
    n = b.shape[2]
    return _matmul(a, b, pl.BlockSpec((tm, k), lambda j, i, kk: (i, 0)), pl.BlockSpec((None, k, n), lambda j, i, kk: (j, 0, 0)),
                   pl.BlockSpec((tm, n), lambda j, i, kk: (i, j)), _sds((m, N_DEV * n), out_dtype),
                   (N_DEV, m // tm, 1), NN, (tm, n), name)


def _mm_nt(a, b, tm, tn, tk, out_dtype, name, deps=(), add=None):
    m, k = a.shape
    n = b.shape[0]
    return _matmul(a, b, pl.BlockSpec((tm, tk), lambda j, i, kk: (i, kk)), pl.BlockSpec((tn, tk), lambda j, i, kk: (j, kk)),
                   pl.BlockSpec((tm, tn), lambda j, i, kk: (i, j)), _sds((m, n), out_dtype),
                   (n // tn, m // tm, k // tk), NT, (tm, tn), name, deps, add)


def _mm_nt_dm(a, b, tm, tn, out_dtype, name, deps=()):
    m = a.shape[0]
    n_out, n = b.shape[1], b.shape[2]
    return _matmul(a, b, pl.BlockSpec((tm, n), lambda j, i, kk: (i, kk)), pl.BlockSpec((None, tn, n), lambda j, i, kk: (kk, j, 0)),
                   pl.BlockSpec((tm, tn), lambda j, i, kk: (i, j)), _sds((m, n_out), out_dtype),
                   (n_out // tn, m // tm, N_DEV), NT, (tm, tn), name, deps)


def _mm_tn(a, b, tm, tn, out_dtype, name, deps=()):
    s, m = a.shape
    n = b.shape[1]
    return _matmul(a, b, pl.BlockSpec((s, tm), lambda j, i, kk: (0, i)), pl.BlockSpec((s, tn), lambda j, i, kk: (0, j)),
                   pl.BlockSpec((tm, tn), lambda j, i, kk: (i, j)), _sds((m, n), out_dtype),
                   (n // tn, m // tm, 1), TN, (tm, tn), name, deps)


def _mm_tn_dm(a, b, tm, out_dtype, name):
    s, m = a.shape
    n = b.shape[1] // N_DEV
    return _matmul(a, b, pl.BlockSpec((s, tm), lambda j, i, kk: (0, i)), pl.BlockSpec((s, n), lambda j, i, kk: (0, j)),
                   pl.BlockSpec((None, tm, n), lambda j, i, kk: (j, i, 0)), _sds((N_DEV, m, n), out_dtype),
                   (N_DEV, m // tm, 1), TN, (tm, n), name)


def _row_spec():
    return pl.BlockSpec((TR, D), lambda i: (i, 0))


def _vec_spec(k=0):
    return pl.BlockSpec((1, D), lambda i: (0, k))


def _acc_rows(ref, first, val):
    @pl.when(first)
    def _():
        ref[...] = val

    @pl.when(jnp.logical_not(first))
    def _():
        ref[...] += val


def _pre_fwd(x, g, mod, k_scale, k_shift, name):
    s = x.shape[0]

    def body(x_ref, g_ref, sc_ref, sh_ref, h_ref):
        xv = x_ref[...]
        r = lax.rsqrt(jnp.mean(xv * xv, axis=-1, keepdims=True) + EPS)
        n = xv * r * g_ref[...]
        h_ref[...] = (n * (1.0 + sc_ref[...]) + sh_ref[...]).astype(h_ref.dtype)

    return pl.pallas_call(body, grid=(s // TR,), in_specs=[_row_spec(), _vec_spec(), _vec_spec(k_scale), _vec_spec(k_shift)],
                          out_specs=_row_spec(), out_shape=_sds((s, D), BF16), compiler_params=_params(1), name=name)(x, g, mod, mod)


def _post_fwd(x, y, g, mod, k_gate, name):
    s = x.shape[0]

    def body(x_ref, y_ref, g_ref, gt_ref, o_ref):
        yv = y_ref[...]
        r = lax.rsqrt(jnp.mean(yv * yv, axis=-1, keepdims=True) + EPS)
        o_ref[...] = x_ref[...] + gt_ref[...] * (yv * r * g_ref[...])

    return pl.pallas_call(body, grid=(s // TR,), in_specs=[_row_spec(), _row_spec(), _vec_spec(), _vec_spec(k_gate)],
                          out_specs=_row_spec(), out_shape=_sds((s, D), F32), compiler_params=_params(1), name=name)(x, y, g, mod)


def _post_loss_bwd(x, y, g, mod, k_gate, tgt, name):
    s = x.shape[0]

    def body(x_ref, y_ref, g_ref, gt_ref, t_ref, e_ref, loss_ref, dy_ref, dgt_ref, dg_ref):
        first = pl.program_id(0) == 0
        yv, gv, gate = y_ref[...], g_ref[...], gt_ref[...]
        r = lax.rsqrt(jnp.mean(yv * yv, axis=-1, keepdims=True) + EPS)
        yh = yv * r
        err = x_ref[...] + gate * (yh * gv) - t_ref[...]
        e = err * (1.0 / D)
        e_ref[...] = e
        _acc_rows(loss_ref, first, 0.5 * jnp.sum(jnp.mean(err * err, axis=-1, keepdims=True), axis=0, keepdims=True))
        dn = e * gate
        dgn = dn * gv
        dy_ref[...] = (r * (dgn - yh * jnp.mean(dgn * yh, axis=-1, keepdims=True))).astype(dy_ref.dtype)
        _acc_rows(dgt_ref, first, jnp.sum(e * (yh * gv), axis=0, keepdims=True))
        _acc_rows(dg_ref, first, jnp.sum(dn * yh, axis=0, keepdims=True))

    return pl.pallas_call(body, grid=(s // TR,),
                          in_specs=[_row_spec(), _row_spec(), _vec_spec(), _vec_spec(k_gate), _row_spec()],
                          out_specs=[_row_spec(), pl.BlockSpec((1, 1), lambda i: (0, 0)), _row_spec(), _vec_spec(), _vec_spec()],
                          out_shape=[_sds((s, D), F32), _sds((1, 1), F32), _sds((s, D), BF16), _sds((1, D), F32), _sds((1, D), F32)],
                          compiler_params=_params(1), name=name)(x, y, g, mod, tgt)


def _pre_bwd(dh, x, res, g, mod, k_scale, name):
    s = x.shape[0]

    def body(dh_ref, x_ref, res_ref, g_ref, sc_ref, dx_ref, dsh_ref, dsc_ref, dg_ref):
        first = pl.program_id(0) == 0
        xv, dh_v, gv = x_ref[...], dh_ref[...], g_ref[...]
        r = lax.rsqrt(jnp.mean(xv * xv, axis=-1, keepdims=True) + EPS)
        xh = xv * r
        dn = dh_v * (1.0 + sc_ref[...])
        dgn = dn * gv
        dx_ref[...] = res_ref[...] + r * (dgn - xh * jnp.mean(dgn * xh, axis=-1, keepdims=True))
        _acc_rows(dsh_ref, first, jnp.sum(dh_v, axis=0, keepdims=True))
        _acc_rows(dsc_ref, first, jnp.sum(dh_v * (xh * gv), axis=0, keepdims=True))
        _acc_rows(dg_ref, first, jnp.sum(dn * xh, axis=0, keepdims=True))

    return pl.pallas_call(body, grid=(s // TR,),
                          in_specs=[_row_spec(), _row_spec(), _row_spec(), _vec_spec(), _vec_spec(k_scale)],
                          out_specs=[_row_spec(), _vec_spec(), _vec_spec(), _vec_spec()],
                          out_shape=[_sds((s, D), F32)] + [_sds((1, D), F32)] * 3,
                          compiler_params=_params(1), name=name)(dh, x, res, g, mod)


def _post_bwd(dx, y, g, mod, k_gate, name):
    s = y.shape[0]

    def body(dx_ref, y_ref, g_ref, gt_ref, dy_ref, dgt_ref, dg_ref):
        first = pl.program_id(0) == 0
        yv, dxv, gv = y_ref[...], dx_ref[...], g_ref[...]
        r = lax.rsqrt(jnp.mean(yv * yv, axis=-1, keepdims=True) + EPS)
        yh = yv * r
        dn = dxv * gt_ref[...]
        dgn = dn * gv
        dy_ref[...] = (r * (dgn - yh * jnp.mean(dgn * yh, axis=-1, keepdims=True))).astype(dy_ref.dtype)
        _acc_rows(dgt_ref, first, jnp.sum(dxv * (yh * gv), axis=0, keepdims=True))
        _acc_rows(dg_ref, first, jnp.sum(dn * yh, axis=0, keepdims=True))

    return pl.pallas_call(body, grid=(s // TR,), in_specs=[_row_spec(), _row_spec(), _vec_spec(), _vec_spec(k_gate)],
                          out_specs=[_row_spec(), _vec_spec(), _vec_spec()],
                          out_shape=[_sds((s, D), BF16), _sds((1, D), F32), _sds((1, D), F32)],
                          compiler_params=_params(1), name=name)(dx, y, g, mod)


SW_TN = 1408
SW_TR = 512
TALL = 1024


def _swiglu_fwd(gu):
    s = gu.shape[0]
    nb = FFN // SW_TN

    def body(g_ref, u_ref, a_ref):
        gv = g_ref[...]
        a_ref[...] = (gv * _sig(gv) * u_ref[...]).astype(a_ref.dtype)

    return pl.pallas_call(body, grid=(s // SW_TR, nb),
                          in_specs=[pl.BlockSpec((SW_TR, SW_TN), lambda i, j: (i, j)), pl.BlockSpec((SW_TR, SW_TN), lambda i, j: (i, j + nb))],
                          out_specs=pl.BlockSpec((SW_TR, SW_TN), lambda i, j: (i, j)), out_shape=_sds((s, FFN), BF16),
                          compiler_params=_params(2, 48 << 20), name="swiglu_fwd")(gu, gu)


def _swiglu_bwd(dact, gu):
    s = gu.shape[0]
    nb = FFN // SW_TN
    n_steps = (s // SW_TR) * nb

    def body(da_ref, g_ref, u_ref, o_ref, buf, sems):
        i, j = pl.program_id(0), pl.program_id(1)
        step = i * nb + j
        slot = step % 2

        def tiles(sl):
            rows = pl.ds(pl.multiple_of(i * SW_TR, SW_TR), SW_TR)
            return [pltpu.make_async_copy(buf.at[sl, h], o_ref.at[rows, pl.ds(pl.multiple_of((j + nb * h) * SW_TN, LANE), SW_TN)], sems.at[sl, h])
                    for h in range(2)]

        @pl.when(step >= 2)
        def _():
            for cp in tiles(slot):
                cp.wait()

        gv, da = g_ref[...], da_ref[...]
        sg = _sig(gv)
        buf[slot, 0] = (da * u_ref[...] * (sg * (1.0 + gv * (1.0 - sg)))).astype(buf.dtype)
        buf[slot, 1] = (da * (gv * sg)).astype(buf.dtype)
        for cp in tiles(slot):
            cp.start()

        @pl.when(step == n_steps - 1)
        def _():
            for cp in tiles(slot) + (tiles(1 - slot) if n_steps > 1 else []):
                cp.wait()

    blk = lambda f: pl.BlockSpec((SW_TR, SW_TN), f)
    return pl.pallas_call(body, grid=(s // SW_TR, nb),
                          in_specs=[blk(lambda i, j: (i, j)), blk(lambda i, j: (i, j)), blk(lambda i, j: (i, j + nb))],
                          out_specs=pl.BlockSpec(memory_space=pl.ANY), out_shape=_sds((s, 2 * FFN), BF16),
                          scratch_shapes=[pltpu.VMEM((2, 2, SW_TR, SW_TN), BF16), pltpu.SemaphoreType.DMA((2, 2))],
                          compiler_params=_params(2, 48 << 20), name="swiglu_bwd")(dact, gu, gu)


MG_TN = 256


def _merge_fwd(y_a, y_h, proj):
    s = y_a.shape[0]
    tn = MG_TN
    ba, bh = GT_A // tn, GT_H // tn

    def body(ya_ref, yh_ref, ga_ref, gh_ref, m_ref):
        m_ref[...] = (_sig(ga_ref[...]) * ya_ref[...] + _sig(gh_ref[...]) * yh_ref[...]).astype(m_ref.dtype)

    tr = min(s, TALL)
    blk = lambda f: pl.BlockSpec((tr, tn), f)
    return pl.pallas_call(body, grid=(s // tr, D // tn),
                          in_specs=[blk(lambda i, j: (i, j)), blk(lambda i, j: (i, j)), blk(lambda i, j: (i, j + ba)), blk(lambda i, j: (i, j + bh))],
                          out_specs=blk(lambda i, j: (i, j)), out_shape=_sds((s, D), BF16),
                          compiler_params=_params(2), name="merge_fwd")(y_a, y_h, proj, proj)


def _merge_bwd(dm, y_a, y_h, proj):
    s = y_a.shape[0]
    tn = MG_TN
    ba, bh = GT_A // tn, GT_H // tn

    def body(dm_ref, ya_ref, yh_ref, ga_ref, gh_ref, dya_ref, dyh_ref, dga_ref, dgh_ref):
        dmv = dm_ref[...]
        sa, sh = _sig(ga_ref[...]), _sig(gh_ref[...])
        dya_ref[...] = (dmv * sa).astype(BF16)
        dyh_ref[...] = (dmv * sh).astype(BF16)
        dga_ref[...] = (dmv * ya_ref[...] * (sa * (1.0 - sa))).astype(BF16)
        dgh_ref[...] = (dmv * yh_ref[...] * (sh * (1.0 - sh))).astype(BF16)

    tr = min(s, TALL)
    blk = lambda f: pl.BlockSpec((tr, tn), f)
    nat = blk(lambda i, j: (i, j))
    return pl.pallas_call(body, grid=(s // tr, D // tn),
                          in_specs=[nat, nat, nat, blk(lambda i, j: (i, j + ba)), blk(lambda i, j: (i, j + bh))],
                          out_specs=[nat] * 4, out_shape=[_sds((s, D), BF16)] * 4,
                          compiler_params=_params(2), name="merge_bwd")(dm, y_a, y_h, proj, proj)


def _hgout_fwd(o_raw, proj, hg_norm):
    s = o_raw.shape[0]
    bg = G_H // LANE

    def body(o_ref, g_ref, n_ref, out_ref):
        ov = o_ref[...]
        r = lax.rsqrt(jnp.mean(ov * ov, axis=-1, keepdims=True) + EPS)
        out_ref[...] = (ov * r * n_ref[...] * _sig(g_ref[...])).astype(out_ref.dtype)

    tr = min(s, TALL)
    blk = lambda f: pl.BlockSpec((tr, LANE), f)
    return pl.pallas_call(body, grid=(s // tr, HG_HEADS),
                          in_specs=[blk(lambda i, h: (i, h)), blk(lambda i, h: (i, h + bg)), pl.BlockSpec((1, LANE), lambda i, h: (0, 0))],
                          out_specs=blk(lambda i, h: (i, h)), out_shape=_sds((s, HG_W), BF16),
                          compiler_params=_params(2), name="hgout_fwd")(o_raw, proj, hg_norm)


def _hgout_bwd(d_out, o_raw, proj, hg_norm):
    s = o_raw.shape[0]
    bg = G_H // LANE

    def body(d_ref, o_ref, g_ref, n_ref, do_ref, dg_ref, dn_ref):
        first = jnp.logical_and(pl.program_id(0) == 0, pl.program_id(1) == 0)
        ov, dv, nv = o_ref[...], d_ref[...], n_ref[...]
        sg = _sig(g_ref[...])
        r = lax.rsqrt(jnp.mean(ov * ov, axis=-1, keepdims=True) + EPS)
        oh = ov * r
        d_on = dv * sg
        dg_ref[...] = (dv * (oh * nv) * (sg * (1.0 - sg))).astype(dg_ref.dtype)
        t = d_on * nv
        do_ref[...] = r * (t - oh * jnp.mean(t * oh, axis=-1, keepdims=True))
        _acc_rows(dn_ref, first, jnp.sum(d_on * oh, axis=0, keepdims=True))

    tr = min(s, TALL)
    blk = lambda f: pl.BlockSpec((tr, LANE), f)
    vec = pl.BlockSpec((1, LANE), lambda i, h: (0, 0))
    return pl.pallas_call(body, grid=(s // tr, HG_HEADS),
                          in_specs=[blk(lambda i, h: (i, h)), blk(lambda i, h: (i, h)), blk(lambda i, h: (i, h + bg)), vec],
                          out_specs=[blk(lambda i, h: (i, h)), blk(lambda i, h: (i, h)), vec],
                          out_shape=[_sds((s, HG_W), F32), _sds((s, HG_W), BF16), _sds((1, LANE), F32)],
                          compiler_params=_params(2), name="hgout_bwd")(d_out, o_raw, proj, hg_norm)


def _rope(t, cos, s_lo, s_hi):
    return t * cos + pltpu.roll(t, LANE - ROT // 2, 1) * s_lo + pltpu.roll(t, ROT // 2, 1) * s_hi


def _rope_wide(t, cos, s_lo, s_hi):
    return jnp.concatenate([_rope(t[:, k * LANE:(k + 1) * LANE], cos, s_lo, s_hi) for k in range(t.shape[1] // LANE)], axis=1)


def _attn_mask(has_prev):
    kj = lax.broadcasted_iota(jnp.int32, (2 * BLK, BLK), 0)
    qi = lax.broadcasted_iota(jnp.int32, (2 * BLK, BLK), 1)
    rel = BLK + qi - kj
    band = jnp.logical_and(rel >= 0, rel < BLK)
    return jnp.logical_and(band, jnp.logical_or(has_prev, kj >= BLK))


def _attn_specs():
    prev = lambda i: jnp.maximum(i - 1, 0)
    kb, vb = K_A // LANE, V_A // LANE
    blk = lambda f: pl.BlockSpec((BLK, LANE), f)
    tabs = [blk(lambda i: (i, 0))] * 3 + [blk(lambda i: (prev(i), 0))] * 3
    return [pl.BlockSpec((BLK, ATT_W), lambda i: (i, 0)), blk(lambda i: (i, kb)), blk(lambda i: (prev(i), kb)),
            blk(lambda i: (i, vb)), blk(lambda i: (prev(i), vb))] + tabs + [pl.BlockSpec((1, LANE), lambda i: (0, 0))]


def _attn_logits(qh, kg):
    return _dot(kg, qh, NT)


def _attn_probs(raw, mask, sk):
    logits = jnp.where(mask, raw * (HEAD_DIM ** -0.5), -jnp.inf)
    m = jnp.maximum(jnp.max(logits, axis=0, keepdims=True), sk)
    p = jnp.exp(logits - m)
    e_sink = jnp.exp(sk - m)
    inv = 1.0 / (jnp.sum(p, axis=0, keepdims=True) + e_sink)
    return p, inv, e_sink * inv


def _attn_fwd(proj, tabs, sinks):
    s = proj.shape[0]

    def body(q_ref, kc_ref, kp_ref, vc_ref, vp_ref, c0, l0, h0, c1, l1, h1, sk_ref, o_ref):
        i = pl.program_id(0)
        mask = _attn_mask(i > 0)
        q = _rope_wide(q_ref[...], c0[...], l0[...], h0[...]).astype(BF16)
        kk = jnp.concatenate([_rope(kp_ref[...], c1[...], l1[...], h1[...]), _rope(kc_ref[...], c0[...], l0[...], h0[...])], axis=0).astype(BF16)
        v_t = jnp.concatenate([vp_ref[...], vc_ref[...]], axis=0).T.astype(BF16)
        part = lambda t, h: t[:, h * HEAD_DIM:(h + 1) * HEAD_DIM]
        k_heads = [part(kk, g) for g in range(KV_HEADS)]

        def head(h):
            g = h // GROUP
            raw = _attn_logits(part(q, h), k_heads[g])
            yield
            p, inv, _ = _attn_probs(raw, mask, sk_ref[:, h:h + 1])
            yield
            out_t = _dot(v_t[g * HEAD_DIM:(g + 1) * HEAD_DIM], p.astype(BF16), NN)
            yield
            return out_t * inv

        o_ref[...] = jnp.concatenate(_interleave([head(h) for h in range(ATT_HEADS)]), axis=0).T.astype(o_ref.dtype)

    return pl.pallas_call(body, grid=(s // BLK,), in_specs=_attn_specs(),
                          out_specs=pl.BlockSpec((BLK, ATT_W), lambda i: (i, 0)), out_shape=_sds((s, ATT_W), BF16),
                          compiler_params=_params(1), name="attn_fwd")(proj, proj, proj, proj, proj, *tabs, *tabs, sinks)


def _attn_bwd(proj, tabs, sinks, d_att):
    s = proj.shape[0]

    def body(q_ref, kc_ref, kp_ref, vc_ref, vp_ref, c0, l0, h0, c1, l1, h1, sk_ref, do_ref, dq_ref, dk_ref, dv_ref, ds_ref):
        i = pl.program_id(0)

        @pl.when(i == 0)
        def _():
            dk_ref[...] = jnp.zeros_like(dk_ref)
            dv_ref[...] = jnp.zeros_like(dv_ref)
            ds_ref[...] = jnp.zeros_like(ds_ref)

        mask = _attn_mask(i > 0)
        q = _rope_wide(q_ref[...], c0[...], l0[...], h0[...]).astype(BF16)
        kk = jnp.concatenate([_rope(kp_ref[...], c1[...], l1[...], h1[...]), _rope(kc_ref[...], c0[...], l0[...], h0[...])], axis=0).astype(BF16)
        k_f32 = jnp.concatenate([_rope(kp_ref[...], c1[...], l1[...], h1[...]), _rope(kc_ref[...], c0[...], l0[...], h0[...])], axis=0)
        k_t = k_f32.T.astype(BF16)
        vv = jnp.concatenate([vp_ref[...], vc_ref[...]], axis=0).astype(BF16)
        d_o = do_ref[...].astype(BF16)
        lane = lax.broadcasted_iota(jnp.int32, (1, LANE), 1)
        part = lambda t, h: t[:, h * HEAD_DIM:(h + 1) * HEAD_DIM]
        k_heads = [part(kk, g) for g in range(KV_HEADS)]
        v_heads = [part(vv, g) for g in range(KV_HEADS)]

        def head(h):
            g = h // GROUP
            qh, doh = part(q, h), part(d_o, h)
            raw = _attn_logits(qh, k_heads[g])
            d_p = _dot(v_heads[g], doh, NT)
            yield
            p, inv, p_sink = _attn_probs(raw, mask, sk_ref[:, h:h + 1])
            prob = p * inv
            dv = _dot(prob.astype(BF16), doh, NN)
            yield
            dd = jnp.sum(prob * d_p, axis=0, keepdims=True)
            d_s = (prob * (d_p - dd)).astype(BF16)
            d_sink = jnp.where(lane == h, -jnp.sum(p_sink * dd, axis=1, keepdims=True), 0.0)
            dq_t = _dot(k_t[g * HEAD_DIM:(g + 1) * HEAD_DIM], d_s, NN)
            dk = _dot(d_s, qh, NN)
            yield
            return dq_t * (HEAD_DIM ** -0.5), dk * (HEAD_DIM ** -0.5), dv, d_sink

        per_head = _interleave([head(h) for h in range(ATT_HEADS)])
        dqs = [jnp.concatenate([t[0] for t in per_head], axis=0).T]
        group_sum = lambda k, g: functools.reduce(jnp.add, [t[k] for t in per_head[g * GROUP:(g + 1) * GROUP]])
        dks = [group_sum(1, g) for g in range(KV_HEADS)]
        dvs = [group_sum(2, g) for g in range(KV_HEADS)]
        d_sink = functools.reduce(jnp.add, [t[3] for t in per_head])
        dq_ref[...] = _rope_wide(jnp.concatenate(dqs, axis=1), c0[...], -l0[...], -h0[...]).astype(dq_ref.dtype)
        d_k = jnp.concatenate(dks, axis=1)
        d_v = jnp.concatenate(dvs, axis=1)
        cur = pl.ds(pl.multiple_of(i * BLK, BLK), BLK)
        prv = pl.ds(pl.multiple_of(jnp.maximum(i - 1, 0) * BLK, BLK), BLK)
        dk_ref[prv, :] += _rope(d_k[:BLK], c1[...], -l1[...], -h1[...])
        dk_ref[cur, :] += _rope(d_k[BLK:], c0[...], -l0[...], -h0[...])
        dv_ref[prv, :] += d_v[:BLK]
        dv_ref[cur, :] += d_v[BLK:]
        ds_ref[...] += d_sink

    full = pl.BlockSpec((s, LANE), lambda i: (0, 0))
    return pl.pallas_call(body, grid=(s // BLK,), in_specs=_attn_specs() + [pl.BlockSpec((BLK, ATT_W), lambda i: (i, 0))],
                          out_specs=[pl.BlockSpec((BLK, ATT_W), lambda i: (i, 0)), full, full, pl.BlockSpec((1, LANE), lambda i: (0, 0))],
                          out_shape=[_sds((s, ATT_W), BF16), _sds((s, LANE), F32), _sds((s, LANE), F32), _sds((1, LANE), F32)],
                          compiler_params=_params(1), name="attn_bwd")(proj, proj, proj, proj, proj, *tabs, *tabs, sinks, d_att)


def _tri_matmul(tri, t):
    hi = t.astype(BF16)
    r1 = t - hi.astype(F32)
    mid = r1.astype(BF16)
    lo = (r1 - mid.astype(F32)).astype(BF16)
    return _dot(tri, hi, NN) + _dot(tri, mid, NN) + _dot(tri, lo, NN)


def _lower_bound(hl):
    a, b = hl[0:1, :], hl[1:2, :]
    mx = jnp.maximum(a, b)
    ea, eb = jnp.exp(a - mx), jnp.exp(b - mx)
    return ea / (ea + eb)


def _hg_gates(q_raw, f_raw, lb, tri_lower):
    sg = _sig(f_raw)
    f = lb + (1.0 - lb) * sg
    sq = _sig(q_raw)
    b = _tri_matmul(tri_lower, jnp.log(f))
    return sg, f, 1.0 - f, sq, q_raw * sq, b


HG_PAIR_FWD = 8
HG_PAIR_BWD = 8


def _hg_specs(n_map, pair):
    blk = lambda off, p: pl.BlockSpec((HG_TB, LANE), lambda h, n: (n_map(n), off // LANE + pair * h + p))
    return [blk(off, p) for off in (Q_H, F_H, I_H) for p in range(pair)] + [pl.BlockSpec((2, pair * LANE), lambda h, n: (0, h))]


def _interleave(gens):
    out = [None] * len(gens)
    live = list(range(len(gens)))
    while live:
        for k in list(live):
            try:
                next(gens[k])
            except StopIteration as stop:
                out[k] = stop.value
                live.remove(k)
    return out


def _hg_spread():
    c = lax.broadcasted_iota(jnp.int32, (CHUNK, SUB * SUB), 0)
    l = lax.broadcasted_iota(jnp.int32, (CHUNK, SUB * SUB), 1)
    r = lax.broadcasted_iota(jnp.int32, (SUB, SUB * SUB), 0)
    lr = lax.broadcasted_iota(jnp.int32, (SUB, SUB * SUB), 1)
    cols = [(c == lo + (l >> 4)).astype(BF16) for lo in range(0, CHUNK, SUB)]
    tile = [(c == lo + (l & (SUB - 1))).astype(BF16) for lo in range(0, CHUNK, SUB)]
    return cols, tile, (lr & (SUB - 1)) == r, (lr >> 4) == r


def _hg_intra(qs, kk, b, grad=None):
    lane = lax.broadcasted_iota(jnp.int32, (SUB, CHUNK), 1)
    row1 = lax.broadcasted_iota(jnp.int32, (SUB, 1), 0)
    kk_b = kk.astype(BF16)
    if grad is not None:
        d_a, d_at, (cols, tile, diag, block) = grad
    a_blocks, dq_blocks, dk_blocks, db_blocks = [], [], [], []
    dk_left = None
    for j in range(CHUNK // SUB):
        lo = j * SUB
        q_j, k_j, b_j = qs[lo:lo + SUB], kk[lo:lo + SUB], b[lo:lo + SUB]
        es = [jnp.where(row1 >= sx, jnp.exp(jnp.minimum(b_j - b_j[sx:sx + 1], 0.0)), 0.0) for sx in range(SUB)]
        pes = [q_j * e for e in es]
        pe = jnp.concatenate(pes, axis=0).astype(BF16)
        pairs = _dot(pe, kk_b, NT)
        yield
        a_j = jnp.zeros((SUB, CHUNK), F32)
        for sx in range(SUB):
            a_j = jnp.where(lane == lo + sx, pairs[sx * SUB:(sx + 1) * SUB], a_j)
        if grad is not None:
            da_j = d_a[lo:lo + SUB]
            ek = jnp.concatenate([e * k_j[sx:sx + 1] for sx, e in enumerate(es)], axis=0).astype(BF16)
            sel_t = jnp.where(diag, _dot(da_j.astype(BF16), cols[j], NN), 0.0).astype(BF16)
            sel_s = jnp.where(block, _dot(d_at[lo:lo + SUB].astype(BF16), tile[j], NN), 0.0).astype(BF16)
            pek = jnp.concatenate([p * k_j[sx:sx + 1] for sx, p in enumerate(pes)], axis=0).astype(BF16)
            yield
            dq_j = _dot(sel_t, ek, NN)
            dk_j = _dot(sel_s, pe, NN)
            db_j = _dot(sel_t, pek, NN) - _dot(sel_s, pek, NN)
            yield
        if j > 0:
            ref = b[lo - 1:lo]
            sc_q = jnp.exp(b_j - ref)
            sc_k = jnp.exp(jnp.minimum(ref - b, 0.0))
            qt = (q_j * sc_q).astype(BF16)
            kt = (kk * sc_k).astype(BF16)
            left = _dot(qt, kt, NT)
            yield
            a_j = a_j + jnp.where(lane < lo, left, 0.0)
            if grad is not None:
                da_left = jnp.where(lane < lo, da_j, 0.0).astype(BF16)
                dq_left = _dot(da_left, kt, NN) * sc_q
                dq_j = dq_j + dq_left
                db_j = db_j + q_j * dq_left
                t = _dot(da_left, qt, TN)
                yield
                t = t * sc_k
                dk_left = t if dk_left is None else dk_left + t
        a_blocks.append(a_j)
        if grad is not None:
            dq_blocks.append(dq_j)
            dk_blocks.append(dk_j)
            db_blocks.append(db_j)
    a = jnp.concatenate(a_blocks, axis=0)
    if grad is None:
        return a
    return a, jnp.concatenate(dq_blocks, axis=0), jnp.concatenate(dk_blocks, axis=0) + dk_left, jnp.concatenate(db_blocks, axis=0) - kk * dk_left


def _hgrn_fwd(proj, hl):
    s = proj.shape[0]
    n_chunk = HG_TB // CHUNK
    pair = HG_PAIR_FWD

    def body(*refs):
        q_refs, f_refs, i_refs = refs[:pair], refs[pair:2 * pair], refs[2 * pair:3 * pair]
        hl_ref, o_ref, st_out_ref, st_ref = refs[3 * pair:]

        @pl.when(pl.program_id(1) == 0)
        def _():
            st_ref[...] = jnp.zeros_like(st_ref)

        r_i = lax.broadcasted_iota(jnp.int32, (CHUNK, CHUNK), 0)
        c_i = lax.broadcasted_iota(jnp.int32, (CHUNK, CHUNK), 1)
        tri_lower = (r_i >= c_i).astype(BF16)

        def chunk(c, carry):
            rows = pl.ds(pl.multiple_of(c * CHUNK, CHUNK), CHUNK)
            def head(p):
                cols = slice(p * LANE, (p + 1) * LANE)
                lb = _lower_bound(hl_ref[:, cols])
                v = i_refs[p][rows, :].astype(BF16)
                _, _, kk, _, qs, b = _hg_gates(q_refs[p][rows, :], f_refs[p][rows, :], lb, tri_lower)
                yield
                st = st_ref[p]
                st_b = st.astype(BF16)
                st_out_ref[p, c] = st_b
                o_state = _dot((qs * jnp.exp(b)).astype(BF16), st_b, NT)
                b_last = b[CHUNK - 1:CHUNK, :]
                st_new = _dot(v, (kk * jnp.exp(b_last - b)).astype(BF16), TN)
                a = yield from _hg_intra(qs, kk, b)
                st_ref[p] = st * jnp.exp(b_last) + st_new
                o_ref[rows, cols] = o_state + _dot(a.astype(BF16), v, NN)

            _interleave([head(p) for p in range(pair)])
            return carry

        lax.fori_loop(0, n_chunk, chunk, 0)

    return pl.pallas_call(
        body, grid=(HG_HEADS // pair, s // HG_TB), in_specs=_hg_specs(lambda n: n, pair),
        out_specs=[pl.BlockSpec((HG_TB, pair * LANE), lambda h, n: (n, h)), pl.BlockSpec((pair, n_chunk, HG_K, HG_K), lambda h, n: (h, n, 0, 0))],
        out_shape=[_sds((s, HG_W), F32), _sds((HG_HEADS, s // CHUNK, HG_K, HG_K), BF16)],
        scratch_shapes=[pltpu.VMEM((pair, HG_K, HG_K), F32)],
        compiler_params=_params(2), name="hgrn_fwd")(*[proj] * (3 * pair), hl)


def _hgrn_bwd(proj, hl, states, d_o):
    s = proj.shape[0]
    n_chunk = HG_TB // CHUNK
    n_blk = s // HG_TB
    pair = HG_PAIR_BWD
    rev = lambda n: n_blk - 1 - n

    def body(*refs):
        q_refs, f_refs, i_refs = refs[:pair], refs[pair:2 * pair], refs[2 * pair:3 * pair]
        hl_ref, st_in_ref, do_ref, dq_ref, df_ref, di_ref, dhl_ref, dst_ref, dlb_ref = refs[3 * pair:]
        n = pl.program_id(1)

        @pl.when(n == 0)
        def _():
            dst_ref[...] = jnp.zeros_like(dst_ref)
            dlb_ref[...] = jnp.zeros_like(dlb_ref)

        r_i = lax.broadcasted_iota(jnp.int32, (CHUNK, CHUNK), 0)
        c_i = lax.broadcasted_iota(jnp.int32, (CHUNK, CHUNK), 1)
        tri_lower = (r_i >= c_i).astype(BF16)
        tri_upper = (r_i <= c_i).astype(BF16)
        row = lax.broadcasted_iota(jnp.int32, (CHUNK, 1), 0)
        spread = _hg_spread()

        def chunk(cc, carry):
            c = n_chunk - 1 - cc
            rows = pl.ds(pl.multiple_of(c * CHUNK, CHUNK), CHUNK)
            def head(p):
                cols = slice(p * LANE, (p + 1) * LANE)
                lb = _lower_bound(hl_ref[:, cols])
                q_raw = q_refs[p][rows, :]
                vb = i_refs[p][rows, :].astype(BF16)
                sg, f, kk, sq, qs, b = _hg_gates(q_raw, f_refs[p][rows, :], lb, tri_lower)
                yield
                e_b = jnp.exp(b)
                qe = qs * e_b
                b_last = b[CHUNK - 1:CHUNK, :]
                e_last = jnp.exp(b_last)
                e_kd = jnp.exp(b_last - b)
                kd = kk * e_kd
                st0 = st_in_ref[p, c]
                d_ob = do_ref[rows, cols].astype(BF16)
                dst = dst_ref[p]
                dst_b = dst.astype(BF16)
                d_a = jnp.where(r_i >= c_i, _dot(d_ob, vb, NT), 0.0)
                d_at = jnp.where(r_i <= c_i, _dot(vb, d_ob, NT), 0.0)
                d_v_st = _dot(kd.astype(BF16), dst_b, NT)
                d_kd = _dot(vb, dst_b, NN)
                d_qe = _dot(d_ob, st0, NN)
                dst_new = _dot(d_ob, qe.astype(BF16), TN)
                yield
                a, dqs, dkk, d_b = yield from _hg_intra(qs, kk, b, (d_a, d_at, spread))
                d_v = _dot(a.astype(BF16), d_ob, TN) + d_v_st
                dqs_st = d_qe * e_b
                dkk_st = d_kd * e_kd
                dqs = dqs + dqs_st
                dkk = dkk + dkk_st
                d_b_last = jnp.sum(d_kd * kd, axis=0, keepdims=True) + jnp.sum(dst * st0.astype(F32), axis=0, keepdims=True) * e_last
                d_b = d_b + qs * dqs_st - kk * dkk_st + jnp.where(row == CHUNK - 1, d_b_last, 0.0)
                d_g = _tri_matmul(tri_upper, d_b)
                dst_ref[p] = dst_new + dst * e_last
                yield
                d_f = d_g / f - dkk
                dlb_ref[:, cols] += jnp.sum(d_f * (1.0 - sg), axis=0, keepdims=True)
                dq_ref[rows, cols] = (dqs * (sq * (1.0 + q_raw * (1.0 - sq)))).astype(dq_ref.dtype)
                df_ref[rows, cols] = (d_f * (1.0 - lb) * (sg * (1.0 - sg))).astype(df_ref.dtype)
                di_ref[rows, cols] = d_v.astype(di_ref.dtype)

            _interleave([head(p) for p in range(pair)])
            return carry

        lax.fori_loop(0, n_chunk, chunk, 0)

        @pl.when(n == n_blk - 1)
        def _():
            lb = _lower_bound(hl_ref[...])
            d_hl0 = dlb_ref[...] * (lb * (1.0 - lb))
            dhl_ref[...] = jnp.concatenate([d_hl0, -d_hl0], axis=0)

    out_blk = pl.BlockSpec((HG_TB, pair * LANE), lambda h, n: (rev(n), h))
    return pl.pallas_call(
        body, grid=(HG_HEADS // pair, n_blk),
        in_specs=_hg_specs(rev, pair) + [pl.BlockSpec((pair, n_chunk, HG_K, HG_K), lambda h, n: (h, rev(n), 0, 0)), out_blk],
        out_specs=[out_blk, out_blk, out_blk, pl.BlockSpec((2, pair * LANE), lambda h, n: (0, h))],
        out_shape=[_sds((s, HG_W), BF16)] * 3 + [_sds((2, HG_W), F32)],
        scratch_shapes=[pltpu.VMEM((pair, HG_K, HG_K), F32), pltpu.VMEM((1, pair * LANE), F32)],
        compiler_params=_params(2), name="hgrn_bwd")(*[proj] * (3 * pair), hl, states, d_o)


def _mod_part(c_all, w_shard, b_shard):
    n = w_shard.shape[1]
    tn = 512

    def body(c_ref, w_ref, b_ref, o_ref):
        o_ref[...] = _dot(c_ref[...].astype(BF16), w_ref[...].astype(BF16), NN) + b_ref[...]

    return pl.pallas_call(body, grid=(n // tn,),
                          in_specs=[pl.BlockSpec((N_DEV, D), lambda j: (0, 0)), pl.BlockSpec((D, tn), lambda j: (0, j)), pl.BlockSpec((1, tn), lambda j: (0, j))],
                          out_specs=pl.BlockSpec((N_DEV, tn), lambda j: (0, j)), out_shape=_sds((N_DEV, n), F32),
                          compiler_params=_params(1, 32 << 20), name="mod_part")(c_all, w_shard, b_shard)


def _grad_w_ada(c_all_t, dmod_cols):
    n = dmod_cols.shape[1]
    tn = 512

    def body(c_ref, d_ref, o_ref):
        cv = c_ref[...].astype(BF16).astype(F32)
        dv = d_ref[...].astype(BF16).astype(F32)
        acc = cv[:, 0:1] * dv[0:1, :]
        for k in range(1, N_DEV):
            acc = acc + cv[:, k:k + 1] * dv[k:k + 1, :]
        o_ref[...] = acc

    return pl.pallas_call(body, grid=(n // tn,),
                          in_specs=[pl.BlockSpec((D, N_DEV), lambda j: (0, 0)), pl.BlockSpec((N_DEV, tn), lambda j: (0, j))],
                          out_specs=pl.BlockSpec((D, tn), lambda j: (0, j)), out_shape=_sds((D, n), F32),
                          compiler_params=_params(1, 32 << 20), name="grad_w_ada")(c_all_t, dmod_cols)


def _row_tile(r, c, max_elems=1 << 18):
    if r * c <= max_elems or r % 8:
        return r
    best = 8
    for t in range(8, r + 1, 8):
        if r % t == 0 and t * c <= max_elems:
            best = t
    return best


WIDE_TILE = 5 << 17


def _adamw(pieces, w, m, v, name, emit_grad=True, own=None):
    p, r, c = pieces.shape
    tr = _row_tile(r, c)
    c1 = 1.0 / (1.0 - ADAM_B1 ** ADAM_STEP)
    c2 = 1.0 / (1.0 - ADAM_B2 ** ADAM_STEP)

    def body(*refs):
        if own is None:
            p_ref, w_ref, m_ref, v_ref, *outs = refs
            g = p_ref[0].astype(F32)
        else:
            o_ref, p_ref, w_ref, m_ref, v_ref, *outs = refs
            g = o_ref[...].astype(F32) + p_ref[0].astype(F32)
        for k in range(1, p):
            g = g + p_ref[k].astype(F32)
        m2 = ADAM_B1 * m_ref[...] + (1.0 - ADAM_B1) * g
        v2 = ADAM_B2 * v_ref[...] + (1.0 - ADAM_B2) * (g * g)
        delta = -ADAM_LR * ((m2 * c1) / (jnp.sqrt(v2 * c2) + ADAM_EPS) + ADAM_WD * w_ref[...])
        if emit_grad:
            outs[0][...] = g
        outs[-3][...] = delta
        outs[-2][...] = m2
        outs[-1][...] = v2

    blk = pl.BlockSpec((tr, c), lambda i: (i, 0))
    n_out = 4 if emit_grad else 3
    lead = [] if own is None else [own]
    return pl.pallas_call(body, grid=(r // tr,), in_specs=[blk] * len(lead) + [pl.BlockSpec((p, tr, c), lambda i: (0, i, 0)), blk, blk, blk],
                          out_specs=[blk] * n_out, out_shape=[_sds((r, c), F32)] * n_out,
                          compiler_params=_params(1, 48 << 20), name=name)(*lead, pieces, w, m, v)


def _my_coords():
    return lax.axis_index("x"), lax.axis_index("y"), lax.axis_index("c")


def _flip(coords, k):
    x, y, c = coords
    return (1 - x if k & 4 else x, 1 - y if k & 2 else y, 1 - c if k & 1 else c)


def _lin(coords):
    return 4 * coords[0] + 2 * coords[1] + coords[2]


def _exchange_small(x3, bcast, name):
    n = x3.shape[2]

    def body(x_ref, o_ref, send_sems, recv_sems):
        me = _my_coords()
        my_id = _lin(me)
        o_ref[pl.ds(my_id, 1)] = x_ref[pl.ds(0 if bcast else my_id, 1)]
        copies = []
        for k in range(1, N_DEV):
            peer = _flip(me, k)
            src = x_ref.at[0 if bcast else _lin(peer)]
            cp = pltpu.make_async_remote_copy(src_ref=src, dst_ref=o_ref.at[my_id], send_sem=send_sems.at[k], recv_sem=recv_sems.at[k],
                                              device_id=peer, device_id_type=MESH)
            cp.start()
            copies.append(cp)
        for k in range(1, N_DEV):
            peer = _flip(me, k)
            pltpu.make_async_remote_copy(src_ref=x_ref.at[0], dst_ref=o_ref.at[_lin(peer)], send_sem=send_sems.at[k], recv_sem=recv_sems.at[k],
                                         device_id=peer, device_id_type=MESH).wait_recv()
        for cp in copies:
            cp.wait_send()

    vm = pl.BlockSpec(memory_space=pltpu.VMEM)
    return pl.pallas_call(body, in_specs=[vm], out_specs=vm, out_shape=_sds((N_DEV, 1, n), F32),
                          scratch_shapes=[pltpu.SemaphoreType.DMA((N_DEV,)), pltpu.SemaphoreType.DMA((N_DEV,))], name=name)(x3)


HBM_SPEC = pl.BlockSpec(memory_space=pltpu.HBM)
SEM_SPEC = pl.BlockSpec(memory_space=pltpu.SEMAPHORE)
ANY_SPEC = pl.BlockSpec(memory_space=pl.ANY)
DATAFLOW = pltpu.SideEffectType.DATAFLOW_SIDE_EFFECTING
GATHER_FLIPS = (1, 2, 4, 6)
PASS_FLIPS = (2, 4, 6)
TOKEN = (8, LANE)


def _hbm(t):
    return pltpu.with_memory_space_constraint(t, pltpu.HBM)


def _hbm_like(ts):
    return [pltpu.HBM(t.shape, t.dtype) for t in ts]


def _split_start(issue, srcs, lands, n_sem, name, deps=()):
    n, nd = len(srcs), len(deps)

    def body(*refs):
        issue(refs[:n], refs[n:2 * n], refs[2 * n + nd], refs[2 * n + nd + 1])
        refs[-1][...] = jnp.zeros(TOKEN, F32)

    outs = pl.pallas_call(
        body, name=name,
        out_shape=(pltpu.SemaphoreType.DMA((n_sem,)), pltpu.SemaphoreType.DMA((n_sem,)), *_hbm_like(srcs), *_hbm_like(lands), _sds(TOKEN, F32)),
        in_specs=[HBM_SPEC] * (2 * n) + [ANY_SPEC] * nd,
        out_specs=(SEM_SPEC, SEM_SPEC, *[HBM_SPEC] * (2 * n), pl.BlockSpec(memory_space=pltpu.VMEM)),
        input_output_aliases={i: 2 + i for i in range(2 * n)},
        compiler_params=pltpu.CompilerParams(has_side_effects=DATAFLOW))(*[_hbm(t) for t in srcs], *[_hbm(t) for t in lands], *deps)
    return dict(sems=outs[:2], thru=list(outs[2:2 + 2 * n]), token=outs[-1], n=n)


def _split_wait(finish, handle, after, name):
    n = handle["n"]
    thru = handle["thru"]

    def body(*refs):
        finish(refs[:n], refs[n:2 * n], refs[2 * n], refs[2 * n + 1])

    outs = pl.pallas_call(
        body, name=name, out_shape=_hbm_like(thru), in_specs=[HBM_SPEC] * (2 * n) + [SEM_SPEC, SEM_SPEC] + [ANY_SPEC] * len(after),
        out_specs=[HBM_SPEC] * (2 * n), input_output_aliases={i: i for i in range(2 * n)},
        compiler_params=pltpu.CompilerParams(has_side_effects=DATAFLOW))(*thru, *handle["sems"], *after)
    return list(outs[:n]), list(outs[n:])


def _gather_start(shards, name, deps=()):
    n = len(shards)
    my_id = _lin(_my_coords())
    lands = [lax.dynamic_update_slice(lax.empty((N_DEV,) + t.shape, t.dtype), t[None], (my_id, 0, 0)) for t in shards]

    def issue(src, land, send_sems, recv_sems):
        me = _my_coords()
        for w in range(n):
            for j, k in enumerate(GATHER_FLIPS):
                q = len(GATHER_FLIPS) * w + j
                pltpu.make_async_remote_copy(src_ref=src[w], dst_ref=land[w].at[_lin(me)], send_sem=send_sems.at[q], recv_sem=recv_sems.at[q],
                                             device_id=_flip(me, k), device_id_type=MESH).start()

    return _split_start(issue, shards, lands, len(GATHER_FLIPS) * n, name, deps)


def _gather_wait(handle, after, name):
    n = handle["n"]

    def finish(src, land, send_sems, recv_sems):
        me = _my_coords()
        for w in range(n):
            for j, k in enumerate(GATHER_FLIPS):
                q = len(GATHER_FLIPS) * w + j
                peer = _flip(me, k)
                cp = pltpu.make_async_remote_copy(src_ref=src[w], dst_ref=land[w].at[_lin(peer)], send_sem=send_sems.at[q], recv_sem=recv_sems.at[q],
                                                  device_id=peer, device_id_type=MESH)
                cp.wait_send()
                cp.wait_recv()

    return _split_wait(finish, handle, after, name)[1]


def _gather_pass(lands, name):
    n = len(lands)
    n_p = len(PASS_FLIPS)

    def body(*refs):
        land = refs[n:2 * n]
        send_sems, recv_sems = refs[2 * n:]
        me = _my_coords()
        sibling = _flip(me, 1)
        sent = []
        for w in range(n):
            for j, k in enumerate(PASS_FLIPS):
                blk = land[w].at[_lin(_flip(me, k))]
                cp = pltpu.make_async_remote_copy(src_ref=blk, dst_ref=blk, send_sem=send_sems.at[n_p * w + j], recv_sem=recv_sems.at[n_p * w + j],
                                                  device_id=sibling, device_id_type=MESH)
                cp.start()
                sent.append(cp)
        for w in range(n):
            for j, k in enumerate(PASS_FLIPS):
                blk = land[w].at[_lin(_flip(me, k + 1))]
                pltpu.make_async_remote_copy(src_ref=blk, dst_ref=blk, send_sem=send_sems.at[n_p * w + j], recv_sem=recv_sems.at[n_p * w + j],
                                             device_id=sibling, device_id_type=MESH).wait_recv()
        for cp in sent:
            cp.wait_send()

    return pl.pallas_call(body, in_specs=[ANY_SPEC] * n, out_specs=[ANY_SPEC] * n, out_shape=[_sds(t.shape, t.dtype) for t in lands],
                          input_output_aliases={i: i for i in range(n)},
                          scratch_shapes=[pltpu.SemaphoreType.DMA((n_p * n,)), pltpu.SemaphoreType.DMA((n_p * n,))], name=name)(*lands)


CHIP_FLIPS = (0, 2, 4, 6)


def _pair_copy(src, land, send_sems, recv_sems, w, j):
    me = _my_coords()
    q = len(CHIP_FLIPS) * w + j
    return pltpu.make_async_remote_copy(src_ref=src[w].at[_lin(_flip(me, CHIP_FLIPS[j] + 1))], dst_ref=land[w].at[j], send_sem=send_sems.at[q],
                                        recv_sem=recv_sems.at[q], device_id=_flip(me, 1), device_id_type=MESH)


def _pair_exchange(grads, name):
    n = len(grads)

    def body(*refs):
        src, land = refs[:n], refs[n:2 * n]
        send_sems, recv_sems = refs[2 * n:]
        sent = [_pair_copy(src, land, send_sems, recv_sems, w, j) for w in range(n) for j in range(len(CHIP_FLIPS))]
        for cp in sent:
            cp.start()
        for cp in sent:
            cp.wait_recv()
        for cp in sent:
            cp.wait_send()

    outs = pl.pallas_call(body, in_specs=[ANY_SPEC] * n, out_specs=[ANY_SPEC] * n,
                          out_shape=[_sds((len(CHIP_FLIPS),) + g.shape[1:], g.dtype) for g in grads],
                          scratch_shapes=[pltpu.SemaphoreType.DMA((len(CHIP_FLIPS) * n,))] * 2, name=name)(*grads)
    return list(outs)


def _pair_start(grads, name, deps=()):
    n = len(grads)
    lands = [lax.empty((len(CHIP_FLIPS),) + g.shape[1:], g.dtype) for g in grads]

    def issue(src, land, send_sems, recv_sems):
        for w in range(n):
            for j in range(len(CHIP_FLIPS)):
                _pair_copy(src, land, send_sems, recv_sems, w, j).start()

    return _split_start(issue, grads, lands, len(CHIP_FLIPS) * n, name, deps)


def _pair_wait(handle, after, name):
    n = handle["n"]

    def finish(src, land, send_sems, recv_sems):
        for w in range(n):
            for j in range(len(CHIP_FLIPS)):
                cp = _pair_copy(src, land, send_sems, recv_sems, w, j)
                cp.wait_send()
                cp.wait_recv()

    return _split_wait(finish, handle, after, name)


def _pair_add(grad, theirs, name):
    p, r, c = theirs.shape
    tr = _row_tile(r, c, WIDE_TILE)
    me = _my_coords()
    ids = jnp.stack([_lin(_flip(me, k)) for k in CHIP_FLIPS]).astype(jnp.int32)

    def body(ids_ref, a_ref, b_ref, o_ref):
        o_ref[...] = (a_ref[...].astype(F32) + b_ref[...].astype(F32)).astype(o_ref.dtype)

    blk = pl.BlockSpec((None, tr, c), lambda j, i, ids_ref: (j, i, 0))
    return pl.pallas_call(
        body, out_shape=_sds((p, r, c), theirs.dtype), compiler_params=_params(2), name=name,
        grid_spec=pltpu.PrefetchScalarGridSpec(
            num_scalar_prefetch=1, grid=(p, r // tr),
            in_specs=[pl.BlockSpec((None, tr, c), lambda j, i, ids_ref: (ids_ref[j], i, 0)), blk], out_specs=blk))(ids, grad, theirs)


def _chips_start(parts, name, deps=()):
    n = len(parts)
    n_c = len(CHIP_FLIPS) - 1
    lands = [lax.empty((n_c,) + t.shape[1:], t.dtype) for t in parts]

    def issue(src, land, send_sems, recv_sems):
        me = _my_coords()
        for w in range(n):
            for j in range(1, n_c + 1):
                q = n_c * w + j - 1
                pltpu.make_async_remote_copy(src_ref=src[w].at[j], dst_ref=land[w].at[j - 1], send_sem=send_sems.at[q], recv_sem=recv_sems.at[q],
                                             device_id=_flip(me, CHIP_FLIPS[j]), device_id_type=MESH).start()

    return _split_start(issue, parts, lands, n_c * n, name, deps)


def _chips_wait(handle, after, name):
    n = handle["n"]
    n_c = len(CHIP_FLIPS) - 1

    def finish(src, land, send_sems, recv_sems):
        me = _my_coords()
        for w in range(n):
            for j in range(1, n_c + 1):
                q = n_c * w + j - 1
                cp = pltpu.make_async_remote_copy(src_ref=src[w].at[j], dst_ref=land[w].at[j - 1], send_sem=send_sems.at[q], recv_sem=recv_sems.at[q],
                                                  device_id=_flip(me, CHIP_FLIPS[j]), device_id_type=MESH)
                cp.wait_send()
                cp.wait_recv()

    return _split_wait(finish, handle, after, name)


def _after(t, *tokens):
    for tok in tokens:
        t = t + tok[0:1, 0:1]
    return t


def _rope_tables(positions):
    half = ROT // 2
    inv_freq = ROPE_THETA ** (-jnp.arange(0, ROT, 2, dtype=F32) / ROT)
    ang = positions.astype(F32).reshape(-1, 1) * inv_freq
    cos, sin = jnp.cos(ang), jnp.sin(ang)
    s = ang.shape[0]
    pad = jnp.zeros((s, HEAD_DIM - ROT), F32)
    zero = jnp.zeros((s, half), F32)
    two = lambda t: jnp.concatenate([t, t], axis=1)
    return (two(jnp.concatenate([cos, cos, pad + 1.0], axis=1)), two(jnp.concatenate([-sin, zero, pad], axis=1)),
            two(jnp.concatenate([zero, sin, pad], axis=1)))


def _local_step(x, tgt, tabs, mod, sinks_pad, hl, hg_norm, g_pre_mix, g_post_mix, g_pre_ffn, g_post_ffn, weights, scatter, scatter_on):
    s = x.shape[0]
    h1 = _pre_fwd(x, g_pre_mix, mod, 1, 0, "pre_mix_fwd")
    (w_in_a,) = weights("in_a", h1)
    proj = _mm_nt(h1[:, :D // 2], w_in_a, 256, IN_COLS // 2, D // 2, F32, "proj_mm_a")
    (w_in_b,) = weights("in_b", proj)
    proj = _mm_nt(h1[:, D // 2:], w_in_b, 256, IN_COLS // 2, D // 2, F32, "proj_mm_b", add=proj)
    att = _attn_fwd(proj, tabs, sinks_pad)
    o_raw, states = _hgrn_fwd(proj, hl)
    ohg = _hgout_fwd(o_raw, proj, hg_norm)
    w_attn_dm, w_hgrn_dm, w_out = weights("mix", ohg)
    y_a = _mm_nn_dm(att, w_attn_dm, s, F32, "attn_proj_mm")
    y_h = _mm_nn_dm(ohg, w_hgrn_dm, s, F32, "hgrn_proj_mm")
    merged = _merge_fwd(y_a, y_h, proj)
    y = _mm_nn(merged, w_out, s, 512, D, F32, "out_mm")
    x1 = _post_fwd(x, y, g_post_mix, mod, 2, "post_mix_fwd")
    h2 = _pre_fwd(x1, g_pre_ffn, mod, 4, 3, "pre_ffn_fwd")
    (w_ffn_in_dm,) = weights("ffn_in", h2)
    gu = _mm_nn_dm(h2, w_ffn_in_dm, s // 2, F32, "ffn_in_mm")
    act = _swiglu_fwd(gu)
    (w_ffn_out,) = weights("ffn_out", act)
    y2 = _mm_nn(act, w_ffn_out, 512, 512, FFN, F32, "ffn_out_mm")
    err, loss, dy2, d_gate2, dg_post_ffn = _post_loss_bwd(x1, y2, g_post_ffn, mod, 5, tgt, "post_ffn_loss_bwd")
    gw_ffn_out = _mm_tn(act, dy2, 512, D, BF16, "ffn_out_dw")
    t_pair = scatter([gw_ffn_out.reshape(N_DEV, FFN // N_DEV, D)], "ffn_out")
    d_act = _mm_nt(dy2, w_ffn_out, s, 512, D, F32, "ffn_out_dx", deps=[t_pair])
    dgu = _swiglu_bwd(d_act, gu)
    t_out = scatter_on("ffn_out", dgu)
    gw_ffn_in = _mm_tn_dm(h2, dgu, 1024, BF16, "ffn_in_dw")
    t_pair = scatter([gw_ffn_in], "ffn_in")
    dh2 = _mm_nt_dm(dgu, w_ffn_in_dm, s, 1024, F32, "ffn_in_dx", deps=[t_pair])
    mod = _after(mod, t_out)
    dx1, d_shift2, d_scale2, dg_pre_ffn = _pre_bwd(dh2, x1, err, g_pre_ffn, mod, 4, "pre_ffn_bwd")
    dy, d_gate1, dg_post_mix = _post_bwd(dx1, y, g_post_mix, mod, 2, "post_mix_bwd")
    t_in = scatter_on("ffn_in", dy)
    d_merged = _mm_nt(dy, w_out, s, 512, D, F32, "out_dx")
    gw_out = _mm_tn(merged, dy, 512, D, BF16, "out_dw")
    dy_a, dy_h, d_gate_a, d_gate_h = _merge_bwd(d_merged, y_a, y_h, proj)
    gw_attn = _mm_tn_dm(att, dy_a, ATT_W, BF16, "attn_proj_dw")
    gw_hgrn = _mm_tn_dm(ohg, dy_h, HG_W, BF16, "hgrn_proj_dw")
    t_pair = scatter([gw_attn, gw_hgrn, gw_out.reshape(N_DEV, D // N_DEV, D)], "mix")
    d_att = _mm_nt_dm(dy_a, w_attn_dm, s, ATT_W, F32, "attn_proj_dx")
    d_ohg = _mm_nt_dm(dy_h, w_hgrn_dm, s, HG_W, F32, "hgrn_proj_dx", deps=[t_pair])
    d_o, d_gh, d_hg_norm = _hgout_bwd(d_ohg, o_raw, proj, _after(hg_norm, t_in))
    d_qh, d_fh, d_ih, d_hl = _hgrn_bwd(proj, hl, states, d_o)
    t_mix = scatter_on("mix", d_qh)
    d_qa, d_ka, d_va, d_sinks = _attn_bwd(proj, tabs, _after(sinks_pad, t_mix), d_att)
    d_proj = jnp.concatenate([d_qa, d_ka.astype(BF16), d_va.astype(BF16), d_qh, d_fh, d_ih, d_gh, d_gate_a, d_gate_h], axis=1)
    dh1 = jnp.concatenate([_mm_nn(d_proj, w_half, s // 2, 512, IN_COLS // 2, F32, "proj_dx_" + tag)
                           for tag, w_half in (("a", w_in_a), ("b", w_in_b))], axis=1)
    grad_x, d_shift1, d_scale1, dg_pre_mix = _pre_bwd(dh1, x, dx1, g_pre_mix, mod, 1, "pre_mix_bwd")
    d_mod = jnp.concatenate([d_shift1, d_scale1, d_gate1, d_shift2, d_scale2, d_gate2], axis=1)
    small = [d_mod, dg_pre_mix, dg_post_mix, dg_pre_ffn, dg_post_ffn, d_hl.reshape(1, 2 * HG_W), d_hg_norm, d_sinks]
    return loss, grad_x, small, h1, d_proj


def kernel(x, c, positions, w_ada, b_ada, g_pre_mix, g_post_mix, g_pre_ffn, g_post_ffn, w_in, attn_sinks, w_attn_proj, hg_lower_bounds, hg_norm, w_hgrn_proj, w_out, w_ffn_in, w_ffn_out, loss_target, m_w_ada, m_b_ada, m_g_pre_mix, m_g_post_mix, m_g_pre_ffn, m_g_post_ffn, m_w_in, m_attn_sinks, m_w_attn_proj, m_hg_lower_bounds, m_hg_norm, m_w_hgrn_proj, m_w_out, m_w_ffn_in, m_w_ffn_out, v_w_ada, v_b_ada, v_g_pre_mix, v_g_post_mix, v_g_pre_ffn, v_g_post_ffn, v_w_in, v_attn_sinks, v_w_attn_proj, v_hg_lower_bounds, v_hg_norm, v_w_hgrn_proj, v_w_out, v_w_ffn_in, v_w_ffn_out):
    my_id = _lin(_my_coords())
    s = x.shape[1]
    n_ada = w_ada.shape[2]

    c_all = _exchange_small(c.reshape(1, 1, D), True, "gather_c").reshape(N_DEV, D)
    b_cols = lax.dynamic_slice(b_ada, (0, my_id * n_ada), (1, n_ada))
    mod_part = _mod_part(c_all, w_ada[0], b_cols)
    mod = _exchange_small(mod_part.reshape(N_DEV, 1, n_ada), False, "scatter_mod").reshape(1, N_MOD * D)
    groups = {"in_a": [w_in[0].T[:, :D // 2]], "in_b": [w_in[0].T[:, D // 2:]], "mix": [w_attn_proj[0], w_hgrn_proj[0], w_out[0]],
              "ffn_in": [w_ffn_in[0]], "ffn_out": [w_ffn_out[0]]}

    def start(group, dep):
        shards, dep = lax.optimization_barrier((groups[group], dep))
        return _gather_start([t.astype(BF16) for t in shards], "gather_start_" + group, deps=[dep])

    gathers = {"in_a": start("in_a", mod)}
    gathers["in_b"] = start("in_b", gathers["in_a"]["token"])
    gathers["mix"] = start("mix", gathers["in_b"]["token"])
    gathers["ffn_in"] = start("ffn_in", gathers["mix"]["token"])
    gathers["ffn_out"] = start("ffn_out", gathers["ffn_in"]["token"])

    def weights(group, after):
        after = [after, gathers["ffn_out"]["token"]]
        lands = _gather_pass(_gather_wait(gathers[group], after, "gather_wait_" + group), "gather_pass_" + group)
        if group in ("in_a", "in_b"):
            return (lands[0].reshape(IN_COLS, D // 2),)
        if group == "mix":
            return lands[0], lands[1], lands[2].reshape(D, D)
        return (lands[0],) if group == "ffn_in" else (lands[0].reshape(FFN, D),)

    pairs, scatters = {}, {}

    def scatter(grads, group):
        pairs[group] = _pair_start(grads, "scatter_pair_" + group)
        return pairs[group]["token"]

    def scatter_on(group, after):
        if group in pairs:
            local, theirs = _pair_wait(pairs[group], [after], "scatter_pair_wait_" + group)
        else:
            local, theirs = after, _pair_exchange(after, "scatter_pair_" + group)
        parts = [_pair_add(g, t, "scatter_pair_add_%s_%d" % (group, k)) for k, (g, t) in enumerate(zip(local, theirs))]
        scatters[group] = _chips_start(parts, "scatter_start_" + group)
        return scatters[group]["token"]

    sinks_pad = jnp.pad(attn_sinks, ((0, 0), (0, LANE - ATT_HEADS)))
    loss, grad_x, small, h1, d_proj = _local_step(
        x[0], loss_target[0], _rope_tables(positions), mod, sinks_pad, hg_lower_bounds, hg_norm, g_pre_mix, g_post_mix, g_pre_ffn, g_post_ffn,
        weights, scatter, scatter_on)
    loss = lax.psum(loss[0, 0], ("x", "y", "c"))

    sizes = [t.shape[1] for t in small]
    parts = _exchange_small(jnp.concatenate(small, axis=1).reshape(1, 1, sum(sizes)), True, "gather_small_grads")
    dep = parts
    for half, cols in (("in_a", slice(0, D // 2)), ("in_b", slice(D // 2, D))):
        gw_half = _mm_tn(d_proj, h1[:, cols], 256, D // 2, BF16, "proj_dw_" + half, deps=[dep])
        dep = scatter_on(half, [gw_half.reshape(N_DEV, IN_COLS // N_DEV, D // 2)])
    offs = [sum(sizes[:k]) for k in range(len(sizes))]
    piece = lambda k, n=None: parts[:, :, offs[k]:offs[k] + (sizes[k] if n is None else n)]
    small_w = [(piece(0), b_ada, m_b_ada, v_b_ada), (piece(1), g_pre_mix, m_g_pre_mix, v_g_pre_mix),
               (piece(2), g_post_mix, m_g_post_mix, v_g_post_mix), (piece(3), g_pre_ffn, m_g_pre_ffn, v_g_pre_ffn),
               (piece(4), g_post_ffn, m_g_post_ffn, v_g_post_ffn),
               (piece(5).reshape(N_DEV, 2, HG_W), hg_lower_bounds, m_hg_lower_bounds, v_hg_lower_bounds),
               (piece(6), hg_norm, m_hg_norm, v_hg_norm), (piece(7, ATT_HEADS), attn_sinks, m_attn_sinks, v_attn_sinks)]
    names = ["b_ada", "g_pre_mix", "g_post_mix", "g_pre_ffn", "g_post_ffn", "hg_lower_bounds", "hg_norm", "attn_sinks"]
    res = {n: _adamw(p, w, m, v, "adamw_" + n) for n, (p, w, m, v) in zip(names, small_w)}

    dmod_cols = lax.dynamic_slice(parts.reshape(N_DEV, -1), (0, my_id * n_ada), (N_DEV, n_ada))
    g_w_ada = _grad_w_ada(c_all.T, dmod_cols)
    res["w_ada"] = [g_w_ada] + list(_adamw(g_w_ada[None], w_ada[0], m_w_ada[0], v_w_ada[0], "adamw_w_ada", emit_grad=False))

    big = {"ffn_out": [("w_ffn_out", w_ffn_out, m_w_ffn_out, v_w_ffn_out)], "ffn_in": [("w_ffn_in", w_ffn_in, m_w_ffn_in, v_w_ffn_in)],
           "mix": [("w_attn_proj", w_attn_proj, m_w_attn_proj, v_w_attn_proj), ("w_hgrn_proj", w_hgrn_proj, m_w_hgrn_proj, v_w_hgrn_proj),
                   ("w_out", w_out, m_w_out, v_w_out)]}
    after = [scatters["in_b"]["token"]]
    for group, members in big.items():
        local, lands = _chips_wait(scatters[group], after, "scatter_wait_" + group)
        for (n, w, m, v), mine, land in zip(members, local, lands):
            res[n] = _adamw(land, w[0], m[0], v[0], "adamw_" + n, own=mine[0])
            after = after + [res[n][1]]
    after = [res[n][1] for n in res]
    halves = [_chips_wait(scatters[half], after, "scatter_wait_" + half) for half in ("in_a", "in_b")]
    own = jnp.concatenate([local[0][0] for local, _ in halves], axis=1)
    land = jnp.concatenate([lands[0] for _, lands in halves], axis=2)
    res["w_in"] = [t.T for t in _adamw(land, w_in[0].T, m_w_in[0].T, v_w_in[0].T, "adamw_w_in", own=own)]

    order = ["w_ada", "b_ada", "g_pre_mix", "g_post_mix", "g_pre_ffn", "g_post_ffn", "w_in", "attn_sinks", "w_attn_proj",
             "hg_lower_bounds", "hg_norm", "w_hgrn_proj", "w_out", "w_ffn_in", "w_ffn_out"]
    lead = {"w_ada", "w_in", "w_attn_proj", "w_hgrn_proj", "w_out", "w_ffn_in", "w_ffn_out"}
    outs = [loss, grad_x[None]]
    for k in range(4):
        outs += [res[n][k][None] if n in lead else res[n][k] for n in order]
    return tuple(outs)
```

```python
import functools

import jax
import jax.numpy as jnp
from jax import lax
from jax.experimental import pallas as pl
from jax.experimental.pallas import tpu as pltpu

F32 = jnp.float32
BF16 = jnp.bfloat16

N_DEV = 8
D = 2048
ATT_HEADS = 16
KV_HEADS = 2
HEAD_DIM = 64
GROUP = ATT_HEADS // KV_HEADS
ATT_W = ATT_HEADS * HEAD_DIM
BLK = 128
ROT = HEAD_DIM // 4
ROPE_THETA = 500000.0
HG_HEADS = 8
HG_K = 128
HG_W = HG_HEADS * HG_K
CHUNK = 64
SUB = 16
FFN = 5632
N_MOD = 6
EPS = 1e-6
LANE = 128
Q_A, K_A, V_A, Q_H, F_H, I_H, G_H, GT_A, GT_H, IN_COLS = 0, 1024, 1152, 1280, 2304, 3328, 4352, 5376, 7424, 9472

ADAM_LR, ADAM_B1, ADAM_B2, ADAM_EPS, ADAM_WD, ADAM_STEP = 0.001, 0.9, 0.999, 1e-08, 0.01, 10

TR = 256
HG_TB = 512
VMEM_BIG = 56 << 20
MESH = pl.DeviceIdType.MESH


def _sds(shape, dtype):
    return jax.ShapeDtypeStruct(shape, dtype)


def _params(n_axes, vmem=None):
    return pltpu.CompilerParams(dimension_semantics=("arbitrary",) * n_axes, vmem_limit_bytes=vmem)


def _sig(t):
    return 1.0 / (1.0 + jnp.exp(-t))


def _dot(a, b, dims):
    return lax.dot_general(a, b, (dims, ((), ())), preferred_element_type=F32)


NN = ((1,), (0,))
NT = ((1,), (1,))
TN = ((0,), (0,))


def _matmul(a, b, a_spec, b_spec, o_spec, out_shape, grid, dims, acc_shape, name, deps=(), add=None):
    nk = grid[2]
    nd = len(deps)
    extra = [] if add is None else [add]

    def body(a_ref, b_ref, *rest):
        o_ref, scratch = rest[nd + len(extra)], rest[nd + len(extra) + 1:]
        part = _dot(a_ref[...], b_ref[...], dims)
        if add is not None:
            assert nk == 1
            part = part + rest[nd][...]
        if nk == 1:
            o_ref[...] = part.astype(o_ref.dtype)
        else:
            acc = scratch[0]
            k = pl.program_id(2)

            @pl.when(k == 0)
            def _():
                acc[...] = part

            @pl.when(k > 0)
            def _():
                acc[...] += part

            @pl.when(k == nk - 1)
            def _():
                o_ref[...] = acc[...].astype(o_ref.dtype)

    return pl.pallas_call(
        body, grid=grid, in_specs=[a_spec, b_spec] + [pl.BlockSpec(memory_space=pl.ANY)] * nd + [o_spec] * len(extra),
        out_specs=o_spec, out_shape=out_shape, scratch_shapes=[pltpu.VMEM(acc_shape, F32)] if nk > 1 else [],
        input_output_aliases={2 + nd: 0} if extra else {},
        compiler_params=_params(3, VMEM_BIG), name=name)(a, b, *deps, *extra)


def _mm_nn(a, b, tm, tn, tk, out_dtype, name):
    m, k = a.shape
    n = b.shape[1]
    return _matmul(a, b, pl.BlockSpec((tm, tk), lambda j, i, kk: (i, kk)), pl.BlockSpec((tk, tn), lambda j, i, kk: (kk, j)),
                   pl.BlockSpec((tm, tn), lambda j, i, kk: (i, j)), _sds((m, n), out_dtype),
                   (n // tn, m // tm, k // tk), NN, (tm, tn), name)


def _mm_nn_dm(a, b, tm, out_dtype, name):
    m, k = a.shape
    n = b.shape[2]
    return _matmul(a, b, pl.BlockSpec((tm, k), lambda j, i, kk: (i, 0)), pl.BlockSpec((None, k, n), lambda j, i, kk: (j, 0, 0)),
                   pl.BlockSpec((tm, n), lambda j, i, kk: (i, j)), _sds((m, N_DEV * n), out_dtype),
                   (N_DEV, m // tm, 1), NN, (tm, n), name)


def _mm_nt(a, b, tm, tn, tk, out_dtype, name, deps=(), add=None):
    m, k = a.shape
    n = b.shape[0]
    return _matmul(a, b, pl.BlockSpec((tm, tk), lambda j, i, kk: (i, kk)), pl.BlockSpec((tn, tk), lambda j, i, kk: (j, kk)),
                   pl.BlockSpec((tm, tn), lambda j, i, kk: (i, j)), _sds((m, n), out_dtype),
                   (n // tn, m // tm, k // tk), NT, (tm, tn), name, deps, add)


def _mm_nt_dm(a, b, tm, tn, out_dtype, name, deps=()):
    m = a.shape[0]
    n_out, n = b.shape[1], b.shape[2]
    return _matmul(a, b, pl.BlockSpec((tm, n), lambda j, i, kk: (i, kk)), pl.BlockSpec((None, tn, n), lambda j, i, kk: (kk, j, 0)),
                   pl.BlockSpec((tm, tn), lambda j, i, kk: (i, j)), _sds((m, n_out), out_dtype),
                   (n_out // tn, m // tm, N_DEV), NT, (tm, tn), name, deps)


def _mm_tn(a, b, tm, tn, out_dtype, name, deps=()):
    s, m = a.shape
    n = b.shape[1]
    return _matmul(a, b, pl.BlockSpec((s, tm), lambda j, i, kk: (0, i)), pl.BlockSpec((s, tn), lambda j, i, kk: (0, j)),
                   pl.BlockSpec((tm, tn), lambda j, i, kk: (i, j)), _sds((m, n), out_dtype),
                   (n // tn, m // tm, 1), TN, (tm, tn), name, deps)


def _mm_tn_dm(a, b, tm, out_dtype, name):
    s, m = a.shape
    n = b.shape[1] // N_DEV
    return _matmul(a, b, pl.BlockSpec((s, tm), lambda j, i, kk: (0, i)), pl.BlockSpec((s, n), lambda j, i, kk: (0, j)),
                   pl.BlockSpec((None, tm, n), lambda j, i, kk: (j, i, 0)), _sds((N_DEV, m, n), out_dtype),
                   (N_DEV, m // tm, 1), TN, (tm, n), name)


def _row_spec():
    return pl.BlockSpec((TR, D), lambda i: (i, 0))


def _vec_spec(k=0):
    return pl.BlockSpec((1, D), lambda i: (0, k))


def _acc_rows(ref, first, val):
    @pl.when(first)
    def _():
        ref[...] = val

    @pl.when(jnp.logical_not(first))
    def _():
        ref[...] += val


def _pre_fwd(x, g, mod, k_scale, k_shift, name):
    s = x.shape[0]

    def body(x_ref, g_ref, sc_ref, sh_ref, h_ref):
        xv = x_ref[...]
        r = lax.rsqrt(jnp.mean(xv * xv, axis=-1, keepdims=True) + EPS)
        n = xv * r * g_ref[...]
        h_ref[...] = (n * (1.0 + sc_ref[...]) + sh_ref[...]).astype(h_ref.dtype)

    return pl.pallas_call(body, grid=(s // TR,), in_specs=[_row_spec(), _vec_spec(), _vec_spec(k_scale), _vec_spec(k_shift)],
                          out_specs=_row_spec(), out_shape=_sds((s, D), BF16), compiler_params=_params(1), name=name)(x, g, mod, mod)


def _post_fwd(x, y, g, mod, k_gate, name):
    s = x.shape[0]

    def body(x_ref, y_ref, g_ref, gt_ref, o_ref):
        yv = y_ref[...]
        r = lax.rsqrt(jnp.mean(yv * yv, axis=-1, keepdims=True) + EPS)
        o_ref[...] = x_ref[...] + gt_ref[...] * (yv * r * g_ref[...])

    return pl.pallas_call(body, grid=(s // TR,), in_specs=[_row_spec(), _row_spec(), _vec_spec(), _vec_spec(k_gate)],
                          out_specs=_row_spec(), out_shape=_sds((s, D), F32), compiler_params=_params(1), name=name)(x, y, g, mod)


def _post_loss_bwd(x, y, g, mod, k_gate, tgt, name):
    s = x.shape[0]

    def body(x_ref, y_ref, g_ref, gt_ref, t_ref, e_ref, loss_ref, dy_ref, dgt_ref, dg_ref):
        first = pl.program_id(0) == 0
        yv, gv, gate = y_ref[...], g_ref[...], gt_ref[...]
        r = lax.rsqrt(jnp.mean(yv * yv, axis=-1, keepdims=True) + EPS)
        yh = yv * r
        err = x_ref[...] + gate * (yh * gv) - t_ref[...]
        e = err * (1.0 / D)
        e_ref[...] = e
        _acc_rows(loss_ref, first, 0.5 * jnp.sum(jnp.mean(err * err, axis=-1, keepdims=True), axis=0, keepdims=True))
        dn = e * gate
        dgn = dn * gv
        dy_ref[...] = (r * (dgn - yh * jnp.mean(dgn * yh, axis=-1, keepdims=True))).astype(dy_ref.dtype)
        _acc_rows(dgt_ref, first, jnp.sum(e * (yh * gv), axis=0, keepdims=True))
        _acc_rows(dg_ref, first, jnp.sum(dn * yh, axis=0, keepdims=True))

    return pl.pallas_call(body, grid=(s // TR,),
                          in_specs=[_row_spec(), _row_spec(), _vec_spec(), _vec_spec(k_gate), _row_spec()],
                          out_specs=[_row_spec(), pl.BlockSpec((1, 1), lambda i: (0, 0)), _row_spec(), _vec_spec(), _vec_spec()],
                          out_shape=[_sds((s, D), F32), _sds((1, 1), F32), _sds((s, D), BF16), _sds((1, D), F32), _sds((1, D), F32)],
                          compiler_params=_params(1), name=name)(x, y, g, mod, tgt)


def _pre_bwd(dh, x, res, g, mod, k_scale, name):
    s = x.shape[0]

    def body(dh_ref, x_ref, res_ref, g_ref, sc_ref, dx_ref, dsh_ref, dsc_ref, dg_ref):
        first = pl.program_id(0) == 0
        xv, dh_v, gv = x_ref[...], dh_ref[...], g_ref[...]
        r = lax.rsqrt(jnp.mean(xv * xv, axis=-1, keepdims=True) + EPS)
        xh = xv * r
        dn = dh_v * (1.0 + sc_ref[...])
        dgn = dn * gv
        dx_ref[...] = res_ref[...] + r * (dgn - xh * jnp.mean(dgn * xh, axis=-1, keepdims=True))
        _acc_rows(dsh_ref, first, jnp.sum(dh_v, axis=0, keepdims=True))
        _acc_rows(dsc_ref, first, jnp.sum(dh_v * (xh * gv), axis=0, keepdims=True))
        _acc_rows(dg_ref, first, jnp.sum(dn * xh, axis=0, keepdims=True))

    return pl.pallas_call(body, grid=(s // TR,),
                          in_specs=[_row_spec(), _row_spec(), _row_spec(), _vec_spec(), _vec_spec(k_scale)],
                          out_specs=[_row_spec(), _vec_spec(), _vec_spec(), _vec_spec()],
                          out_shape=[_sds((s, D), F32)] + [_sds((1, D), F32)] * 3,
                          compiler_params=_params(1), name=name)(dh, x, res, g, mod)


def _post_bwd(dx, y, g, mod, k_gate, name):
    s = y.shape[0]

    def body(dx_ref, y_ref, g_ref, gt_ref, dy_ref, dgt_ref, dg_ref):
        first = pl.program_id(0) == 0
        yv, dxv, gv = y_ref[...], dx_ref[...], g_ref[...]
        r = lax.rsqrt(jnp.mean(yv * yv, axis=-1, keepdims=True) + EPS)
        yh = yv * r
        dn = dxv * gt_ref[...]
        dgn = dn * gv
        dy_ref[...] = (r * (dgn - yh * jnp.mean(dgn * yh, axis=-1, keepdims=True))).astype(dy_ref.dtype)
        _acc_rows(dgt_ref, first, jnp.sum(dxv * (yh * gv), axis=0, keepdims=True))
        _acc_rows(dg_ref, first, jnp.sum(dn * yh, axis=0, keepdims=True))

    return pl.pallas_call(body, grid=(s // TR,), in_specs=[_row_spec(), _row_spec(), _vec_spec(), _vec_spec(k_gate)],
                          out_specs=[_row_spec(), _vec_spec(), _vec_spec()],
                          out_shape=[_sds((s, D), BF16), _sds((1, D), F32), _sds((1, D), F32)],
                          compiler_params=_params(1), name=name)(dx, y, g, mod)


SW_TN = 1408
SW_TR = 512
TALL = 1024


def _swiglu_fwd(gu):
    s = gu.shape[0]
    nb = FFN // SW_TN

    def body(g_ref, u_ref, a_ref):
        gv = g_ref[...]
        a_ref[...] = (gv * _sig(gv) * u_ref[...]).astype(a_ref.dtype)

    return pl.pallas_call(body, grid=(s // SW_TR, nb),
                          in_specs=[pl.BlockSpec((SW_TR, SW_TN), lambda i, j: (i, j)), pl.BlockSpec((SW_TR, SW_TN), lambda i, j: (i, j + nb))],
                          out_specs=pl.BlockSpec((SW_TR, SW_TN), lambda i, j: (i, j)), out_shape=_sds((s, FFN), BF16),
                          compiler_params=_params(2, 48 << 20), name="swiglu_fwd")(gu, gu)


def _swiglu_bwd(dact, gu):
    s = gu.shape[0]
    nb = FFN // SW_TN
    n_steps = (s // SW_TR) * nb

    def body(da_ref, g_ref, u_ref, o_ref, buf, sems):
        i, j = pl.program_id(0), pl.program_id(1)
        step = i * nb + j
        slot = step % 2

        def tiles(sl):
            rows = pl.ds(pl.multiple_of(i * SW_TR, SW_TR), SW_TR)
            return [pltpu.make_async_copy(buf.at[sl, h], o_ref.at[rows, pl.ds(pl.multiple_of((j + nb * h) * SW_TN, LANE), SW_TN)], sems.at[sl, h])
                    for h in range(2)]

        @pl.when(step >= 2)
        def _():
            for cp in tiles(slot):
                cp.wait()

        gv, da = g_ref[...], da_ref[...]
        sg = _sig(gv)
        buf[slot, 0] = (da * u_ref[...] * (sg * (1.0 + gv * (1.0 - sg)))).astype(buf.dtype)
        buf[slot, 1] = (da * (gv * sg)).astype(buf.dtype)
        for cp in tiles(slot):
            cp.start()

        @pl.when(step == n_steps - 1)
        def _():
            for cp in tiles(slot) + (tiles(1 - slot) if n_steps > 1 else []):
                cp.wait()

    blk = lambda f: pl.BlockSpec((SW_TR, SW_TN), f)
    return pl.pallas_call(body, grid=(s // SW_TR, nb),
                          in_specs=[blk(lambda i, j: (i, j)), blk(lambda i, j: (i, j)), blk(lambda i, j: (i, j + nb))],
                          out_specs=pl.BlockSpec(memory_space=pl.ANY), out_shape=_sds((s, 2 * FFN), BF16),
                          scratch_shapes=[pltpu.VMEM((2, 2, SW_TR, SW_TN), BF16), pltpu.SemaphoreType.DMA((2, 2))],
                          compiler_params=_params(2, 48 << 20), name="swiglu_bwd")(dact, gu, gu)


MG_TN = 256


def _merge_fwd(y_a, y_h, proj):
    s = y_a.shape[0]
    tn = MG_TN
    ba, bh = GT_A // tn, GT_H // tn

    def body(ya_ref, yh_ref, ga_ref, gh_ref, m_ref):
        m_ref[...] = (_sig(ga_ref[...]) * ya_ref[...] + _sig(gh_ref[...]) * yh_ref[...]).astype(m_ref.dtype)

    tr = min(s, TALL)
    blk = lambda f: pl.BlockSpec((tr, tn), f)
    return pl.pallas_call(body, grid=(s // tr, D // tn),
                          in_specs=[blk(lambda i, j: (i, j)), blk(lambda i, j: (i, j)), blk(lambda i, j: (i, j + ba)), blk(lambda i, j: (i, j + bh))],
                          out_specs=blk(lambda i, j: (i, j)), out_shape=_sds((s, D), BF16),
                          compiler_params=_params(2), name="merge_fwd")(y_a, y_h, proj, proj)


def _merge_bwd(dm, y_a, y_h, proj):
    s = y_a.shape[0]
    tn = MG_TN
    ba, bh = GT_A // tn, GT_H // tn

    def body(dm_ref, ya_ref, yh_ref, ga_ref, gh_ref, dya_ref, dyh_ref, dga_ref, dgh_ref):
        dmv = dm_ref[...]
        sa, sh = _sig(ga_ref[...]), _sig(gh_ref[...])
        dya_ref[...] = (dmv * sa).astype(BF16)
        dyh_ref[...] = (dmv * sh).astype(BF16)
        dga_ref[...] = (dmv * ya_ref[...] * (sa * (1.0 - sa))).astype(BF16)
        dgh_ref[...] = (dmv * yh_ref[...] * (sh * (1.0 - sh))).astype(BF16)

    tr = min(s, TALL)
    blk = lambda f: pl.BlockSpec((tr, tn), f)
    nat = blk(lambda i, j: (i, j))
    return pl.pallas_call(body, grid=(s // tr, D // tn),
                          in_specs=[nat, nat, nat, blk(lambda i, j: (i, j + ba)), blk(lambda i, j: (i, j + bh))],
                          out_specs=[nat] * 4, out_shape=[_sds((s, D), BF16)] * 4,
                          compiler_params=_params(2), name="merge_bwd")(dm, y_a, y_h, proj, proj)


def _hgout_fwd(o_raw, proj, hg_norm):
    s = o_raw.shape[0]
    bg = G_H // LANE

    def body(o_ref, g_ref, n_ref, out_ref):
        ov = o_ref[...]
        r = lax.rsqrt(jnp.mean(ov * ov, axis=-1, keepdims=True) + EPS)
        out_ref[...] = (ov * r * n_ref[...] * _sig(g_ref[...])).astype(out_ref.dtype)

    tr = min(s, TALL)
    blk = lambda f: pl.BlockSpec((tr, LANE), f)
    return pl.pallas_call(body, grid=(s // tr, HG_HEADS),
                          in_specs=[blk(lambda i, h: (i, h)), blk(lambda i, h: (i, h + bg)), pl.BlockSpec((1, LANE), lambda i, h: (0, 0))],
                          out_specs=blk(lambda i, h: (i, h)), out_shape=_sds((s, HG_W), BF16),
                          compiler_params=_params(2), name="hgout_fwd")(o_raw, proj, hg_norm)


def _hgout_bwd(d_out, o_raw, proj, hg_norm):
    s = o_raw.shape[0]
    bg = G_H // LANE

    def body(d_ref, o_ref, g_ref, n_ref, do_ref, dg_ref, dn_ref):
        first = jnp.logical_and(pl.program_id(0) == 0, pl.program_id(1) == 0)
        ov, dv, nv = o_ref[...], d_ref[...], n_ref[...]
        sg = _sig(g_ref[...])
        r = lax.rsqrt(jnp.mean(ov * ov, axis=-1, keepdims=True) + EPS)
        oh = ov * r
        d_on = dv * sg
        dg_ref[...] = (dv * (oh * nv) * (sg * (1.0 - sg))).astype(dg_ref.dtype)
        t = d_on * nv
        do_ref[...] = r * (t - oh * jnp.mean(t * oh, axis=-1, keepdims=True))
        _acc_rows(dn_ref, first, jnp.sum(d_on * oh, axis=0, keepdims=True))

    tr = min(s, TALL)
    blk = lambda f: pl.BlockSpec((tr, LANE), f)
    vec = pl.BlockSpec((1, LANE), lambda i, h: (0, 0))
    return pl.pallas_call(body, grid=(s // tr, HG_HEADS),
                          in_specs=[blk(lambda i, h: (i, h)), blk(lambda i, h: (i, h)), blk(lambda i, h: (i, h + bg)), vec],
                          out_specs=[blk(lambda i, h: (i, h)), blk(lambda i, h: (i, h)), vec],
                          out_shape=[_sds((s, HG_W), F32), _sds((s, HG_W), BF16), _sds((1, LANE), F32)],
                          compiler_params=_params(2), name="hgout_bwd")(d_out, o_raw, proj, hg_norm)


def _rope(t, cos, s_lo, s_hi):
    return t * cos + pltpu.roll(t, LANE - ROT // 2, 1) * s_lo + pltpu.roll(t, ROT // 2, 1) * s_hi


def _rope_wide(t, cos, s_lo, s_hi):
    return jnp.concatenate([_rope(t[:, k * LANE:(k + 1) * LANE], cos, s_lo, s_hi) for k in range(t.shape[1] // LANE)], axis=1)


def _attn_mask(has_prev):
    kj = lax.broadcasted_iota(jnp.int32, (2 * BLK, BLK), 0)
    qi = lax.broadcasted_iota(jnp.int32, (2 * BLK, BLK), 1)
    rel = BLK + qi - kj
    band = jnp.logical_and(rel >= 0, rel < BLK)
    return jnp.logical_and(band, jnp.logical_or(has_prev, kj >= BLK))


def _attn_specs():
    prev = lambda i: jnp.maximum(i - 1, 0)
    kb, vb = K_A // LANE, V_A // LANE
    blk = lambda f: pl.BlockSpec((BLK, LANE), f)
    tabs = [blk(lambda i: (i, 0))] * 3 + [blk(lambda i: (prev(i), 0))] * 3
    return [pl.BlockSpec((BLK, ATT_W), lambda i: (i, 0)), blk(lambda i: (i, kb)), blk(lambda i: (prev(i), kb)),
            blk(lambda i: (i, vb)), blk(lambda i: (prev(i), vb))] + tabs + [pl.BlockSpec((1, LANE), lambda i: (0, 0))]


def _attn_logits(qh, kg):
    return _dot(kg, qh, NT)


def _attn_probs(raw, mask, sk):
    logits = jnp.where(mask, raw * (HEAD_DIM ** -0.5), -jnp.inf)
    m = jnp.maximum(jnp.max(logits, axis=0, keepdims=True), sk)
    p = jnp.exp(logits - m)
    e_sink = jnp.exp(sk - m)
    inv = 1.0 / (jnp.sum(p, axis=0, keepdims=True) + e_sink)
    return p, inv, e_sink * inv


def _attn_fwd(proj, tabs, sinks):
    s = proj.shape[0]

    def body(q_ref, kc_ref, kp_ref, vc_ref, vp_ref, c0, l0, h0, c1, l1, h1, sk_ref, o_ref):
        i = pl.program_id(0)
        mask = _attn_mask(i > 0)
        q = _rope_wide(q_ref[...], c0[...], l0[...], h0[...]).astype(BF16)
        kk = jnp.concatenate([_rope(kp_ref[...], c1[...], l1[...], h1[...]), _rope(kc_ref[...], c0[...], l0[...], h0[...])], axis=0).astype(BF16)
        v_t = jnp.concatenate([vp_ref[...], vc_ref[...]], axis=0).T.astype(BF16)
        part = lambda t, h: t[:, h * HEAD_DIM:(h + 1) * HEAD_DIM]
        k_heads = [part(kk, g) for g in range(KV_HEADS)]

        def head(h):
            g = h // GROUP
            raw = _attn_logits(part(q, h), k_heads[g])
            yield
            p, inv, _ = _attn_probs(raw, mask, sk_ref[:, h:h + 1])
            yield
            out_t = _dot(v_t[g * HEAD_DIM:(g + 1) * HEAD_DIM], p.astype(BF16), NN)
            yield
            return out_t * inv

        o_ref[...] = jnp.concatenate(_interleave([head(h) for h in range(ATT_HEADS)]), axis=0).T.astype(o_ref.dtype)

    return pl.pallas_call(body, grid=(s // BLK,), in_specs=_attn_specs(),
                          out_specs=pl.BlockSpec((BLK, ATT_W), lambda i: (i, 0)), out_shape=_sds((s, ATT_W), BF16),
                          compiler_params=_params(1), name="attn_fwd")(proj, proj, proj, proj, proj, *tabs, *tabs, sinks)


def _attn_bwd(proj, tabs, sinks, d_att):
    s = proj.shape[0]

    def body(q_ref, kc_ref, kp_ref, vc_ref, vp_ref, c0, l0, h0, c1, l1, h1, sk_ref, do_ref, dq_ref, dk_ref, dv_ref, ds_ref):
        i = pl.program_id(0)

        @pl.when(i == 0)
        def _():
            dk_ref[...] = jnp.zeros_like(dk_ref)
            dv_ref[...] = jnp.zeros_like(dv_ref)
            ds_ref[...] = jnp.zeros_like(ds_ref)

        mask = _attn_mask(i > 0)
        q = _rope_wide(q_ref[...], c0[...], l0[...], h0[...]).astype(BF16)
        kk = jnp.concatenate([_rope(kp_ref[...], c1[...], l1[...], h1[...]), _rope(kc_ref[...], c0[...], l0[...], h0[...])], axis=0).astype(BF16)
        k_f32 = jnp.concatenate([_rope(kp_ref[...], c1[...], l1[...], h1[...]), _rope(kc_ref[...], c0[...], l0[...], h0[...])], axis=0)
        k_t = k_f32.T.astype(BF16)
        vv = jnp.concatenate([vp_ref[...], vc_ref[...]], axis=0).astype(BF16)
        d_o = do_ref[...].astype(BF16)
        lane = lax.broadcasted_iota(jnp.int32, (1, LANE), 1)
        part = lambda t, h: t[:, h * HEAD_DIM:(h + 1) * HEAD_DIM]
        k_heads = [part(kk, g) for g in range(KV_HEADS)]
        v_heads = [part(vv, g) for g in range(KV_HEADS)]

        def head(h):
            g = h // GROUP
            qh, doh = part(q, h), part(d_o, h)
            raw = _attn_logits(qh, k_heads[g])
            d_p = _dot(v_heads[g], doh, NT)
            yield
            p, inv, p_sink = _attn_probs(raw, mask, sk_ref[:, h:h + 1])
            prob = p * inv
            dv = _dot(prob.astype(BF16), doh, NN)
            yield
            dd = jnp.sum(prob * d_p, axis=0, keepdims=True)
            d_s = (prob * (d_p - dd)).astype(BF16)
            d_sink = jnp.where(lane == h, -jnp.sum(p_sink * dd, axis=1, keepdims=True), 0.0)
            dq_t = _dot(k_t[g * HEAD_DIM:(g + 1) * HEAD_DIM], d_s, NN)
            dk = _dot(d_s, qh, NN)
            yield
            return dq_t * (HEAD_DIM ** -0.5), dk * (HEAD_DIM ** -0.5), dv, d_sink

        per_head = _interleave([head(h) for h in range(ATT_HEADS)])
        dqs = [jnp.concatenate([t[0] for t in per_head], axis=0).T]
        group_sum = lambda k, g: functools.reduce(jnp.add, [t[k] for t in per_head[g * GROUP:(g + 1) * GROUP]])
        dks = [group_sum(1, g) for g in range(KV_HEADS)]
        dvs = [group_sum(2, g) for g in range(KV_HEADS)]
        d_sink = functools.reduce(jnp.add, [t[3] for t in per_head])
        dq_ref[...] = _rope_wide(jnp.concatenate(dqs, axis=1), c0[...], -l0[...], -h0[...]).astype(dq_ref.dtype)
        d_k = jnp.concatenate(dks, axis=1)
        d_v = jnp.concatenate(dvs, axis=1)
        cur = pl.ds(pl.multiple_of(i * BLK, BLK), BLK)
        prv = pl.ds(pl.multiple_of(jnp.maximum(i - 1, 0) * BLK, BLK), BLK)
        dk_ref[prv, :] += _rope(d_k[:BLK], c1[...], -l1[...], -h1[...])
        dk_ref[cur, :] += _rope(d_k[BLK:], c0[...], -l0[...], -h0[...])
        dv_ref[prv, :] += d_v[:BLK]
        dv_ref[cur, :] += d_v[BLK:]
        ds_ref[...] += d_sink

    full = pl.BlockSpec((s, LANE), lambda i: (0, 0))
    return pl.pallas_call(body, grid=(s // BLK,), in_specs=_attn_specs() + [pl.BlockSpec((BLK, ATT_W), lambda i: (i, 0))],
                          out_specs=[pl.BlockSpec((BLK, ATT_W), lambda i: (i, 0)), full, full, pl.BlockSpec((1, LANE), lambda i: (0, 0))],
                          out_shape=[_sds((s, ATT_W), BF16), _sds((s, LANE), F32), _sds((s, LANE), F32), _sds((1, LANE), F32)],
                          compiler_params=_params(1), name="attn_bwd")(proj, proj, proj, proj, proj, *tabs, *tabs, sinks, d_att)


def _tri_matmul(tri, t):
    hi = t.astype(BF16)
    r1 = t - hi.astype(F32)
    mid = r1.astype(BF16)
    lo = (r1 - mid.astype(F32)).astype(BF16)
    return _dot(tri, hi, NN) + _dot(tri, mid, NN) + _dot(tri, lo, NN)


def _lower_bound(hl):
    a, b = hl[0:1, :], hl[1:2, :]
    mx = jnp.maximum(a, b)
    ea, eb = jnp.exp(a - mx), jnp.exp(b - mx)
    return ea / (ea + eb)


def _hg_gates(q_raw, f_raw, lb, tri_lower):
    sg = _sig(f_raw)
    f = lb + (1.0 - lb) * sg
    sq = _sig(q_raw)
    b = _tri_matmul(tri_lower, jnp.log(f))
    return sg, f, 1.0 - f, sq, q_raw * sq, b


HG_PAIR_FWD = 8
HG_PAIR_BWD = 8


def _hg_specs(n_map, pair):
    blk = lambda off, p: pl.BlockSpec((HG_TB, LANE), lambda h, n: (n_map(n), off // LANE + pair * h + p))
    return [blk(off, p) for off in (Q_H, F_H, I_H) for p in range(pair)] + [pl.BlockSpec((2, pair * LANE), lambda h, n: (0, h))]


def _interleave(gens):
    out = [None] * len(gens)
    live = list(range(len(gens)))
    while live:
        for k in list(live):
            try:
                next(gens[k])
            except StopIteration as stop:
                out[k] = stop.value
                live.remove(k)
    return out


def _hg_spread():
    c = lax.broadcasted_iota(jnp.int32, (CHUNK, SUB * SUB), 0)
    l = lax.broadcasted_iota(jnp.int32, (CHUNK, SUB * SUB), 1)
    r = lax.broadcasted_iota(jnp.int32, (SUB, SUB * SUB), 0)
    lr = lax.broadcasted_iota(jnp.int32, (SUB, SUB * SUB), 1)
    shift = SUB.bit_length() - 1
    cols = [(c == lo + (l >> shift)).astype(BF16) for lo in range(0, CHUNK, SUB)]
    tile = [(c == lo + (l & (SUB - 1))).astype(BF16) for lo in range(0, CHUNK, SUB)]
    return cols, tile, (lr & (SUB - 1)) == r, (lr >> shift) == r


def _hg_intra(qs, kk, b, grad=None):
    lane = lax.broadcasted_iota(jnp.int32, (SUB, CHUNK), 1)
    row1 = lax.broadcasted_iota(jnp.int32, (SUB, 1), 0)
    kk_b = kk.astype(BF16)
    if grad is not None:
        d_a, d_at, (cols, tile, diag, block) = grad
    a_blocks, dq_blocks, dk_blocks, db_blocks = [], [], [], []
    dk_left = None
    for j in range(CHUNK // SUB):
        lo = j * SUB
        q_j, k_j, b_j = qs[lo:lo + SUB], kk[lo:lo + SUB], b[lo:lo + SUB]
        es = [jnp.where(row1 >= sx, jnp.exp(jnp.minimum(b_j - b_j[sx:sx + 1], 0.0)), 0.0) for sx in range(SUB)]
        pes = [q_j * e for e in es]
        pe = jnp.concatenate(pes, axis=0).astype(BF16)
        pairs = _dot(pe, kk_b, NT)
        yield
        a_j = jnp.zeros((SUB, CHUNK), F32)
        for sx in range(SUB):
            a_j = jnp.where(lane == lo + sx, pairs[sx * SUB:(sx + 1) * SUB], a_j)
        if grad is not None:
            da_j = d_a[lo:lo + SUB]
            ek = jnp.concatenate([e * k_j[sx:sx + 1] for sx, e in enumerate(es)], axis=0).astype(BF16)
            sel_t = jnp.where(diag, _dot(da_j.astype(BF16), cols[j], NN), 0.0).astype(BF16)
            sel_s = jnp.where(block, _dot(d_at[lo:lo + SUB].astype(BF16), tile[j], NN), 0.0).astype(BF16)
            pek = jnp.concatenate([p * k_j[sx:sx + 1] for sx, p in enumerate(pes)], axis=0).astype(BF16)
            yield
            dq_j = _dot(sel_t, ek, NN)
            dk_j = _dot(sel_s, pe, NN)
            db_j = _dot(sel_t, pek, NN) - _dot(sel_s, pek, NN)
            yield
        if j > 0:
            ref = b[lo - 1:lo]
            sc_q = jnp.exp(b_j - ref)
            sc_k = jnp.exp(jnp.minimum(ref - b, 0.0))
            qt = (q_j * sc_q).astype(BF16)
            kt = (kk * sc_k).astype(BF16)
            left = _dot(qt, kt, NT)
            yield
            a_j = a_j + jnp.where(lane < lo, left, 0.0)
            if grad is not None:
                da_left = jnp.where(lane < lo, da_j, 0.0).astype(BF16)
                dq_left = _dot(da_left, kt, NN) * sc_q
                dq_j = dq_j + dq_left
                db_j = db_j + q_j * dq_left
                t = _dot(da_left, qt, TN)
                yield
                t = t * sc_k
                dk_left = t if dk_left is None else dk_left + t
        a_blocks.append(a_j)
        if grad is not None:
            dq_blocks.append(dq_j)
            dk_blocks.append(dk_j)
            db_blocks.append(db_j)
    a = jnp.concatenate(a_blocks, axis=0)
    if grad is None:
        return a
    return a, jnp.concatenate(dq_blocks, axis=0), jnp.concatenate(dk_blocks, axis=0) + dk_left, jnp.concatenate(db_blocks, axis=0) - kk * dk_left


def _hgrn_fwd(proj, hl):
    s = proj.shape[0]
    n_chunk = HG_TB // CHUNK
    pair = HG_PAIR_FWD

    def body(*refs):
        q_refs, f_refs, i_refs = refs[:pair], refs[pair:2 * pair], refs[2 * pair:3 * pair]
        hl_ref, o_ref, st_out_ref, st_ref = refs[3 * pair:]

        @pl.when(pl.program_id(1) == 0)
        def _():
            st_ref[...] = jnp.zeros_like(st_ref)

        r_i = lax.broadcasted_iota(jnp.int32, (CHUNK, CHUNK), 0)
        c_i = lax.broadcasted_iota(jnp.int32, (CHUNK, CHUNK), 1)
        tri_lower = (r_i >= c_i).astype(BF16)

        def chunk(c, carry):
            rows = pl.ds(pl.multiple_of(c * CHUNK, CHUNK), CHUNK)
            def head(p):
                cols = slice(p * LANE, (p + 1) * LANE)
                lb = _lower_bound(hl_ref[:, cols])
                v = i_refs[p][rows, :].astype(BF16)
                _, _, kk, _, qs, b = _hg_gates(q_refs[p][rows, :], f_refs[p][rows, :], lb, tri_lower)
                yield
                st = st_ref[p]
                st_b = st.astype(BF16)
                st_out_ref[p, c] = st_b
                o_state = _dot((qs * jnp.exp(b)).astype(BF16), st_b, NT)
                b_last = b[CHUNK - 1:CHUNK, :]
                st_new = _dot(v, (kk * jnp.exp(b_last - b)).astype(BF16), TN)
                a = yield from _hg_intra(qs, kk, b)
                st_ref[p] = st * jnp.exp(b_last) + st_new
                o_ref[rows, cols] = o_state + _dot(a.astype(BF16), v, NN)

            _interleave([head(p) for p in range(pair)])
            return carry

        lax.fori_loop(0, n_chunk, chunk, 0)

    return pl.pallas_call(
        body, grid=(HG_HEADS // pair, s // HG_TB), in_specs=_hg_specs(lambda n: n, pair),
        out_specs=[pl.BlockSpec((HG_TB, pair * LANE), lambda h, n: (n, h)), pl.BlockSpec((pair, n_chunk, HG_K, HG_K), lambda h, n: (h, n, 0, 0))],
        out_shape=[_sds((s, HG_W), F32), _sds((HG_HEADS, s // CHUNK, HG_K, HG_K), BF16)],
        scratch_shapes=[pltpu.VMEM((pair, HG_K, HG_K), F32)],
        compiler_params=_params(2), name="hgrn_fwd")(*[proj] * (3 * pair), hl)


def _hgrn_bwd(proj, hl, states, d_o):
    s = proj.shape[0]
    n_chunk = HG_TB // CHUNK
    n_blk = s // HG_TB
    pair = HG_PAIR_BWD
    rev = lambda n: n_blk - 1 - n

    def body(*refs):
        q_refs, f_refs, i_refs = refs[:pair], refs[pair:2 * pair], refs[2 * pair:3 * pair]
        hl_ref, st_in_ref, do_ref, dq_ref, df_ref, di_ref, dhl_ref, dst_ref, dlb_ref = refs[3 * pair:]
        n = pl.program_id(1)

        @pl.when(n == 0)
        def _():
            dst_ref[...] = jnp.zeros_like(dst_ref)
            dlb_ref[...] = jnp.zeros_like(dlb_ref)

        r_i = lax.broadcasted_iota(jnp.int32, (CHUNK, CHUNK), 0)
        c_i = lax.broadcasted_iota(jnp.int32, (CHUNK, CHUNK), 1)
        tri_lower = (r_i >= c_i).astype(BF16)
        tri_upper = (r_i <= c_i).astype(BF16)
        row = lax.broadcasted_iota(jnp.int32, (CHUNK, 1), 0)
        spread = _hg_spread()

        def chunk(cc, carry):
            c = n_chunk - 1 - cc
            rows = pl.ds(pl.multiple_of(c * CHUNK, CHUNK), CHUNK)
            def head(p):
                cols = slice(p * LANE, (p + 1) * LANE)
                lb = _lower_bound(hl_ref[:, cols])
                q_raw = q_refs[p][rows, :]
                vb = i_refs[p][rows, :].astype(BF16)
                sg, f, kk, sq, qs, b = _hg_gates(q_raw, f_refs[p][rows, :], lb, tri_lower)
                yield
                e_b = jnp.exp(b)
                qe = qs * e_b
                b_last = b[CHUNK - 1:CHUNK, :]
                e_last = jnp.exp(b_last)
                e_kd = jnp.exp(b_last - b)
                kd = kk * e_kd
                st0 = st_in_ref[p, c]
                d_ob = do_ref[rows, cols].astype(BF16)
                dst = dst_ref[p]
                dst_b = dst.astype(BF16)
                d_a = jnp.where(r_i >= c_i, _dot(d_ob, vb, NT), 0.0)
                d_at = jnp.where(r_i <= c_i, _dot(vb, d_ob, NT), 0.0)
                d_v_st = _dot(kd.astype(BF16), dst_b, NT)
                d_kd = _dot(vb, dst_b, NN)
                d_qe = _dot(d_ob, st0, NN)
                dst_new = _dot(d_ob, qe.astype(BF16), TN)
                yield
                a, dqs, dkk, d_b = yield from _hg_intra(qs, kk, b, (d_a, d_at, spread))
                d_v = _dot(a.astype(BF16), d_ob, TN) + d_v_st
                dqs_st = d_qe * e_b
                dkk_st = d_kd * e_kd
                dqs = dqs + dqs_st
                dkk = dkk + dkk_st
                d_b_last = jnp.sum(d_kd * kd, axis=0, keepdims=True) + jnp.sum(dst * st0.astype(F32), axis=0, keepdims=True) * e_last
                d_b = d_b + qs * dqs_st - kk * dkk_st + jnp.where(row == CHUNK - 1, d_b_last, 0.0)
                d_g = _tri_matmul(tri_upper, d_b)
                dst_ref[p] = dst_new + dst * e_last
                yield
                d_f = d_g / f - dkk
                dlb_ref[:, cols] += jnp.sum(d_f * (1.0 - sg), axis=0, keepdims=True)
                dq_ref[rows, cols] = (dqs * (sq * (1.0 + q_raw * (1.0 - sq)))).astype(dq_ref.dtype)
                df_ref[rows, cols] = (d_f * (1.0 - lb) * (sg * (1.0 - sg))).astype(df_ref.dtype)
                di_ref[rows, cols] = d_v.astype(di_ref.dtype)

            _interleave([head(p) for p in range(pair)])
            return carry

        lax.fori_loop(0, n_chunk, chunk, 0)

        @pl.when(n == n_blk - 1)
        def _():
            lb = _lower_bound(hl_ref[...])
            d_hl0 = dlb_ref[...] * (lb * (1.0 - lb))
            dhl_ref[...] = jnp.concatenate([d_hl0, -d_hl0], axis=0)

    out_blk = pl.BlockSpec((HG_TB, pair * LANE), lambda h, n: (rev(n), h))
    return pl.pallas_call(
        body, grid=(HG_HEADS // pair, n_blk),
        in_specs=_hg_specs(rev, pair) + [pl.BlockSpec((pair, n_chunk, HG_K, HG_K), lambda h, n: (h, rev(n), 0, 0)), out_blk],
        out_specs=[out_blk, out_blk, out_blk, pl.BlockSpec((2, pair * LANE), lambda h, n: (0, h))],
        out_shape=[_sds((s, HG_W), BF16)] * 3 + [_sds((2, HG_W), F32)],
        scratch_shapes=[pltpu.VMEM((pair, HG_K, HG_K), F32), pltpu.VMEM((1, pair * LANE), F32)],
        compiler_params=_params(2), name="hgrn_bwd")(*[proj] * (3 * pair), hl, states, d_o)


def _mod_part(c_all, w_shard, b_shard):
    n = w_shard.shape[1]
    tn = 512

    def body(c_ref, w_ref, b_ref, o_ref):
        o_ref[...] = _dot(c_ref[...].astype(BF16), w_ref[...].astype(BF16), NN) + b_ref[...]

    return pl.pallas_call(body, grid=(n // tn,),
                          in_specs=[pl.BlockSpec((N_DEV, D), lambda j: (0, 0)), pl.BlockSpec((D, tn), lambda j: (0, j)), pl.BlockSpec((1, tn), lambda j: (0, j))],
                          out_specs=pl.BlockSpec((N_DEV, tn), lambda j: (0, j)), out_shape=_sds((N_DEV, n), F32),
                          compiler_params=_params(1, 32 << 20), name="mod_part")(c_all, w_shard, b_shard)


def _grad_w_ada(c_all_t, dmod_cols):
    n = dmod_cols.shape[1]
    tn = 512

    def body(c_ref, d_ref, o_ref):
        cv = c_ref[...].astype(BF16).astype(F32)
        dv = d_ref[...].astype(BF16).astype(F32)
        acc = cv[:, 0:1] * dv[0:1, :]
        for k in range(1, N_DEV):
            acc = acc + cv[:, k:k + 1] * dv[k:k + 1, :]
        o_ref[...] = acc

    return pl.pallas_call(body, grid=(n // tn,),
                          in_specs=[pl.BlockSpec((D, N_DEV), lambda j: (0, 0)), pl.BlockSpec((N_DEV, tn), lambda j: (0, j))],
                          out_specs=pl.BlockSpec((D, tn), lambda j: (0, j)), out_shape=_sds((D, n), F32),
                          compiler_params=_params(1, 32 << 20), name="grad_w_ada")(c_all_t, dmod_cols)


def _row_tile(r, c, max_elems=1 << 18):
    if r * c <= max_elems or r % 8:
        return r
    best = 8
    for t in range(8, r + 1, 8):
        if r % t == 0 and t * c <= max_elems:
            best = t
    return best


WIDE_TILE = 5 << 17


def _adamw(pieces, w, m, v, name, emit_grad=True, own=None):
    p, r, c = pieces.shape
    tr = _row_tile(r, c)
    c1 = 1.0 / (1.0 - ADAM_B1 ** ADAM_STEP)
    c2 = 1.0 / (1.0 - ADAM_B2 ** ADAM_STEP)

    def body(*refs):
        if own is None:
            p_ref, w_ref, m_ref, v_ref, *outs = refs
            g = p_ref[0].astype(F32)
        else:
            o_ref, p_ref, w_ref, m_ref, v_ref, *outs = refs
            g = o_ref[...].astype(F32) + p_ref[0].astype(F32)
        for k in range(1, p):
            g = g + p_ref[k].astype(F32)
        m2 = ADAM_B1 * m_ref[...] + (1.0 - ADAM_B1) * g
        v2 = ADAM_B2 * v_ref[...] + (1.0 - ADAM_B2) * (g * g)
        delta = -ADAM_LR * ((m2 * c1) / (jnp.sqrt(v2 * c2) + ADAM_EPS) + ADAM_WD * w_ref[...])
        if emit_grad:
            outs[0][...] = g
        outs[-3][...] = delta
        outs[-2][...] = m2
        outs[-1][...] = v2

    blk = pl.BlockSpec((tr, c), lambda i: (i, 0))
    n_out = 4 if emit_grad else 3
    lead = [] if own is None else [own]
    return pl.pallas_call(body, grid=(r // tr,), in_specs=[blk] * len(lead) + [pl.BlockSpec((p, tr, c), lambda i: (0, i, 0)), blk, blk, blk],
                          out_specs=[blk] * n_out, out_shape=[_sds((r, c), F32)] * n_out,
                          compiler_params=_params(1, 48 << 20), name=name)(*lead, pieces, w, m, v)


def _my_coords():
    return lax.axis_index("x"), lax.axis_index("y"), lax.axis_index("c")


def _flip(coords, k):
    x, y, c = coords
    return (1 - x if k & 4 else x, 1 - y if k & 2 else y, 1 - c if k & 1 else c)


def _lin(coords):
    return 4 * coords[0] + 2 * coords[1] + coords[2]


def _exchange_small(x3, bcast, name):
    n = x3.shape[2]

    def body(x_ref, o_ref, send_sems, recv_sems):
        me = _my_coords()
        my_id = _lin(me)
        o_ref[pl.ds(my_id, 1)] = x_ref[pl.ds(0 if bcast else my_id, 1)]
        copies = []
        for k in range(1, N_DEV):
            peer = _flip(me, k)
            src = x_ref.at[0 if bcast else _lin(peer)]
            cp = pltpu.make_async_remote_copy(src_ref=src, dst_ref=o_ref.at[my_id], send_sem=send_sems.at[k], recv_sem=recv_sems.at[k],
                                              device_id=peer, device_id_type=MESH)
            cp.start()
            copies.append(cp)
        for k in range(1, N_DEV):
            peer = _flip(me, k)
            pltpu.make_async_remote_copy(src_ref=x_ref.at[0], dst_ref=o_ref.at[_lin(peer)], send_sem=send_sems.at[k], recv_sem=recv_sems.at[k],
                                         device_id=peer, device_id_type=MESH).wait_recv()
        for cp in copies:
            cp.wait_send()

    vm = pl.BlockSpec(memory_space=pltpu.VMEM)
    return pl.pallas_call(body, in_specs=[vm], out_specs=vm, out_shape=_sds((N_DEV, 1, n), F32),
                          scratch_shapes=[pltpu.SemaphoreType.DMA((N_DEV,)), pltpu.SemaphoreType.DMA((N_DEV,))], name=name)(x3)


HBM_SPEC = pl.BlockSpec(memory_space=pltpu.HBM)
SEM_SPEC = pl.BlockSpec(memory_space=pltpu.SEMAPHORE)
ANY_SPEC = pl.BlockSpec(memory_space=pl.ANY)
DATAFLOW = pltpu.SideEffectType.DATAFLOW_SIDE_EFFECTING
GATHER_FLIPS = (1, 2, 4, 6)
PASS_FLIPS = (2, 4, 6)
TOKEN = (8, LANE)


def _hbm(t):
    return pltpu.with_memory_space_constraint(t, pltpu.HBM)


def _hbm_like(ts):
    return [pltpu.HBM(t.shape, t.dtype) for t in ts]


def _split_start(issue, srcs, lands, n_sem, name, deps=()):
    n, nd = len(srcs), len(deps)

    def body(*refs):
        issue(refs[:n], refs[n:2 * n], refs[2 * n + nd], refs[2 * n + nd + 1])
        refs[-1][...] = jnp.zeros(TOKEN, F32)

    outs = pl.pallas_call(
        body, name=name,
        out_shape=(pltpu.SemaphoreType.DMA((n_sem,)), pltpu.SemaphoreType.DMA((n_sem,)), *_hbm_like(srcs), *_hbm_like(lands), _sds(TOKEN, F32)),
        in_specs=[HBM_SPEC] * (2 * n) + [ANY_SPEC] * nd,
        out_specs=(SEM_SPEC, SEM_SPEC, *[HBM_SPEC] * (2 * n), pl.BlockSpec(memory_space=pltpu.VMEM)),
        input_output_aliases={i: 2 + i for i in range(2 * n)},
        compiler_params=pltpu.CompilerParams(has_side_effects=DATAFLOW))(*[_hbm(t) for t in srcs], *[_hbm(t) for t in lands], *deps)
    return dict(sems=outs[:2], thru=list(outs[2:2 + 2 * n]), token=outs[-1], n=n)


def _split_wait(finish, handle, after, name):
    n = handle["n"]
    thru = handle["thru"]

    def body(*refs):
        finish(refs[:n], refs[n:2 * n], refs[2 * n], refs[2 * n + 1])

    outs = pl.pallas_call(
        body, name=name, out_shape=_hbm_like(thru), in_specs=[HBM_SPEC] * (2 * n) + [SEM_SPEC, SEM_SPEC] + [ANY_SPEC] * len(after),
        out_specs=[HBM_SPEC] * (2 * n), input_output_aliases={i: i for i in range(2 * n)},
        compiler_params=pltpu.CompilerParams(has_side_effects=DATAFLOW))(*thru, *handle["sems"], *after)
    return list(outs[:n]), list(outs[n:])


def _gather_start(shards, name, deps=()):
    n = len(shards)
    my_id = _lin(_my_coords())
    lands = [lax.dynamic_update_slice(lax.empty((N_DEV,) + t.shape, t.dtype), t[None], (my_id, 0, 0)) for t in shards]

    def issue(src, land, send_sems, recv_sems):
        me = _my_coords()
        for w in range(n):
            for j, k in enumerate(GATHER_FLIPS):
                q = len(GATHER_FLIPS) * w + j
                pltpu.make_async_remote_copy(src_ref=src[w], dst_ref=land[w].at[_lin(me)], send_sem=send_sems.at[q], recv_sem=recv_sems.at[q],
                                             device_id=_flip(me, k), device_id_type=MESH).start()

    return _split_start(issue, shards, lands, len(GATHER_FLIPS) * n, name, deps)


def _gather_wait(handle, after, name):
    n = handle["n"]

    def finish(src, land, send_sems, recv_sems):
        me = _my_coords()
        for w in range(n):
            for j, k in enumerate(GATHER_FLIPS):
                q = len(GATHER_FLIPS) * w + j
                peer = _flip(me, k)
                cp = pltpu.make_async_remote_copy(src_ref=src[w], dst_ref=land[w].at[_lin(peer)], send_sem=send_sems.at[q], recv_sem=recv_sems.at[q],
                                                  device_id=peer, device_id_type=MESH)
                cp.wait_send()
                cp.wait_recv()

    return _split_wait(finish, handle, after, name)[1]


def _gather_pass(lands, name):
    n = len(lands)
    n_p = len(PASS_FLIPS)

    def body(*refs):
        land = refs[n:2 * n]
        send_sems, recv_sems = refs[2 * n:]
        me = _my_coords()
        sibling = _flip(me, 1)
        sent = []
        for w in range(n):
            for j, k in enumerate(PASS_FLIPS):
                blk = land[w].at[_lin(_flip(me, k))]
                cp = pltpu.make_async_remote_copy(src_ref=blk, dst_ref=blk, send_sem=send_sems.at[n_p * w + j], recv_sem=recv_sems.at[n_p * w + j],
                                                  device_id=sibling, device_id_type=MESH)
                cp.start()
                sent.append(cp)
        for w in range(n):
            for j, k in enumerate(PASS_FLIPS):
                blk = land[w].at[_lin(_flip(me, k + 1))]
                pltpu.make_async_remote_copy(src_ref=blk, dst_ref=blk, send_sem=send_sems.at[n_p * w + j], recv_sem=recv_sems.at[n_p * w + j],
                                             device_id=sibling, device_id_type=MESH).wait_recv()
        for cp in sent:
            cp.wait_send()

    return pl.pallas_call(body, in_specs=[ANY_SPEC] * n, out_specs=[ANY_SPEC] * n, out_shape=[_sds(t.shape, t.dtype) for t in lands],
                          input_output_aliases={i: i for i in range(n)},
                          scratch_shapes=[pltpu.SemaphoreType.DMA((n_p * n,)), pltpu.SemaphoreType.DMA((n_p * n,))], name=name)(*lands)


CHIP_FLIPS = (0, 2, 4, 6)


def _pair_copy(src, land, send_sems, recv_sems, w, j):
    me = _my_coords()
    q = len(CHIP_FLIPS) * w + j
    return pltpu.make_async_remote_copy(src_ref=src[w].at[_lin(_flip(me, CHIP_FLIPS[j] + 1))], dst_ref=land[w].at[j], send_sem=send_sems.at[q],
                                        recv_sem=recv_sems.at[q], device_id=_flip(me, 1), device_id_type=MESH)


def _pair_exchange(grads, name):
    n = len(grads)

    def body(*refs):
        src, land = refs[:n], refs[n:2 * n]
        send_sems, recv_sems = refs[2 * n:]
        sent = [_pair_copy(src, land, send_sems, recv_sems, w, j) for w in range(n) for j in range(len(CHIP_FLIPS))]
        for cp in sent:
            cp.start()
        for cp in sent:
            cp.wait_recv()
        for cp in sent:
            cp.wait_send()

    outs = pl.pallas_call(body, in_specs=[ANY_SPEC] * n, out_specs=[ANY_SPEC] * n,
                          out_shape=[_sds((len(CHIP_FLIPS),) + g.shape[1:], g.dtype) for g in grads],
                          scratch_shapes=[pltpu.SemaphoreType.DMA((len(CHIP_FLIPS) * n,))] * 2, name=name)(*grads)
    return list(outs)


def _pair_start(grads, name, deps=()):
    n = len(grads)
    lands = [lax.empty((len(CHIP_FLIPS),) + g.shape[1:], g.dtype) for g in grads]

    def issue(src, land, send_sems, recv_sems):
        for w in range(n):
            for j in range(len(CHIP_FLIPS)):
                _pair_copy(src, land, send_sems, recv_sems, w, j).start()

    return _split_start(issue, grads, lands, len(CHIP_FLIPS) * n, name, deps)


def _pair_wait(handle, after, name):
    n = handle["n"]

    def finish(src, land, send_sems, recv_sems):
        for w in range(n):
            for j in range(len(CHIP_FLIPS)):
                cp = _pair_copy(src, land, send_sems, recv_sems, w, j)
                cp.wait_send()
                cp.wait_recv()

    return _split_wait(finish, handle, after, name)


def _pair_add(grad, theirs, name):
    p, r, c = theirs.shape
    tr = _row_tile(r, c, WIDE_TILE)
    me = _my_coords()
    ids = jnp.stack([_lin(_flip(me, k)) for k in CHIP_FLIPS]).astype(jnp.int32)

    def body(ids_ref, a_ref, b_ref, o_ref):
        o_ref[...] = (a_ref[...].astype(F32) + b_ref[...].astype(F32)).astype(o_ref.dtype)

    blk = pl.BlockSpec((None, tr, c), lambda j, i, ids_ref: (j, i, 0))
    return pl.pallas_call(
        body, out_shape=_sds((p, r, c), theirs.dtype), compiler_params=_params(2), name=name,
        grid_spec=pltpu.PrefetchScalarGridSpec(
            num_scalar_prefetch=1, grid=(p, r // tr),
            in_specs=[pl.BlockSpec((None, tr, c), lambda j, i, ids_ref: (ids_ref[j], i, 0)), blk], out_specs=blk))(ids, grad, theirs)


def _chips_start(parts, name, deps=()):
    n = len(parts)
    n_c = len(CHIP_FLIPS) - 1
    lands = [lax.empty((n_c,) + t.shape[1:], t.dtype) for t in parts]

    def issue(src, land, send_sems, recv_sems):
        me = _my_coords()
        for w in range(n):
            for j in range(1, n_c + 1):
                q = n_c * w + j - 1
                pltpu.make_async_remote_copy(src_ref=src[w].at[j], dst_ref=land[w].at[j - 1], send_sem=send_sems.at[q], recv_sem=recv_sems.at[q],
                                             device_id=_flip(me, CHIP_FLIPS[j]), device_id_type=MESH).start()

    return _split_start(issue, parts, lands, n_c * n, name, deps)


def _chips_wait(handle, after, name):
    n = handle["n"]
    n_c = len(CHIP_FLIPS) - 1

    def finish(src, land, send_sems, recv_sems):
        me = _my_coords()
        for w in range(n):
            for j in range(1, n_c + 1):
                q = n_c * w + j - 1
                cp = pltpu.make_async_remote_copy(src_ref=src[w].at[j], dst_ref=land[w].at[j - 1], send_sem=send_sems.at[q], recv_sem=recv_sems.at[q],
                                                  device_id=_flip(me, CHIP_FLIPS[j]), device_id_type=MESH)
                cp.wait_send()
                cp.wait_recv()

    return _split_wait(finish, handle, after, name)


def _after(t, *tokens):
    for tok in tokens:
        t = t + tok[0:1, 0:1]
    return t


def _rope_tables(positions):
    half = ROT // 2
    inv_freq = ROPE_THETA ** (-jnp.arange(0, ROT, 2, dtype=F32) / ROT)
    ang = positions.astype(F32).reshape(-1, 1) * inv_freq
    cos, sin = jnp.cos(ang), jnp.sin(ang)
    s = ang.shape[0]
    pad = jnp.zeros((s, HEAD_DIM - ROT), F32)
    zero = jnp.zeros((s, half), F32)
    two = lambda t: jnp.concatenate([t, t], axis=1)
    return (two(jnp.concatenate([cos, cos, pad + 1.0], axis=1)), two(jnp.concatenate([-sin, zero, pad], axis=1)),
            two(jnp.concatenate([zero, sin, pad], axis=1)))


def _local_step(x, tgt, tabs, mod, sinks_pad, hl, hg_norm, g_pre_mix, g_post_mix, g_pre_ffn, g_post_ffn, weights, scatter, scatter_on):
    s = x.shape[0]
    h1 = _pre_fwd(x, g_pre_mix, mod, 1, 0, "pre_mix_fwd")
    (w_in_a,) = weights("in_a", h1)
    proj = _mm_nt(h1[:, :D // 2], w_in_a, 256, IN_COLS // 2, D // 2, F32, "proj_mm_a")
    (w_in_b,) = weights("in_b", proj)
    proj = _mm_nt(h1[:, D // 2:], w_in_b, 256, IN_COLS // 2, D // 2, F32, "proj_mm_b", add=proj)
    att = _attn_fwd(proj, tabs, sinks_pad)
    o_raw, states = _hgrn_fwd(proj, hl)
    ohg = _hgout_fwd(o_raw, proj, hg_norm)
    w_attn_dm, w_hgrn_dm, w_out = weights("mix", ohg)
    natural = lambda w_dm: w_dm.transpose(1, 0, 2).reshape(w_dm.shape[1], D)
    pieces = lambda g: g.reshape(g.shape[0], N_DEV, D // N_DEV).transpose(1, 0, 2)
    w_attn, w_hgrn = natural(w_attn_dm), natural(w_hgrn_dm)
    y_a = _mm_nn(att, w_attn, s, 512, ATT_W, F32, "attn_proj_mm")
    y_h = _mm_nn(ohg, w_hgrn, s, 512, HG_W, F32, "hgrn_proj_mm")
    merged = _merge_fwd(y_a, y_h, proj)
    y = _mm_nn(merged, w_out, s, 512, D, F32, "out_mm")
    x1 = _post_fwd(x, y, g_post_mix, mod, 2, "post_mix_fwd")
    h2 = _pre_fwd(x1, g_pre_ffn, mod, 4, 3, "pre_ffn_fwd")
    (w_ffn_in_dm,) = weights("ffn_in", h2)
    gu = _mm_nn_dm(h2, w_ffn_in_dm, s // 2, F32, "ffn_in_mm")
    act = _swiglu_fwd(gu)
    (w_ffn_out,) = weights("ffn_out", act)
    y2 = _mm_nn(act, w_ffn_out, 512, 512, FFN, F32, "ffn_out_mm")
    err, loss, dy2, d_gate2, dg_post_ffn = _post_loss_bwd(x1, y2, g_post_ffn, mod, 5, tgt, "post_ffn_loss_bwd")
    gw_ffn_out = _mm_tn(act, dy2, 512, D, BF16, "ffn_out_dw")
    t_pair = scatter([gw_ffn_out.reshape(N_DEV, FFN // N_DEV, D)], "ffn_out")
    d_act = _mm_nt(dy2, w_ffn_out, s, 512, D, F32, "ffn_out_dx", deps=[t_pair])
    dgu = _swiglu_bwd(d_act, gu)
    t_out = scatter_on("ffn_out", dgu)
    gw_ffn_in = _mm_tn_dm(h2, dgu, 1024, BF16, "ffn_in_dw")
    t_pair = scatter([gw_ffn_in], "ffn_in")
    dh2 = _mm_nt_dm(dgu, w_ffn_in_dm, s, 1024, F32, "ffn_in_dx", deps=[t_pair])
    mod = _after(mod, t_out)
    dx1, d_shift2, d_scale2, dg_pre_ffn = _pre_bwd(dh2, x1, err, g_pre_ffn, mod, 4, "pre_ffn_bwd")
    dy, d_gate1, dg_post_mix = _post_bwd(dx1, y, g_post_mix, mod, 2, "post_mix_bwd")
    t_in = scatter_on("ffn_in", dy)
    d_merged = _mm_nt(dy, w_out, s, 512, D, F32, "out_dx")
    gw_out = _mm_tn(merged, dy, 512, D, BF16, "out_dw")
    dy_a, dy_h, d_gate_a, d_gate_h = _merge_bwd(d_merged, y_a, y_h, proj)
    gw_attn = pieces(_mm_tn(att, dy_a, 512, D, BF16, "attn_proj_dw"))
    gw_hgrn = pieces(_mm_tn(ohg, dy_h, 512, D, BF16, "hgrn_proj_dw"))
    t_pair = scatter([gw_attn, gw_hgrn, gw_out.reshape(N_DEV, D // N_DEV, D)], "mix")
    d_att = _mm_nt(dy_a, w_attn, s, 512, D, F32, "attn_proj_dx")
    d_ohg = _mm_nt(dy_h, w_hgrn, s, 512, D, F32, "hgrn_proj_dx", deps=[t_pair])
    d_o, d_gh, d_hg_norm = _hgout_bwd(d_ohg, o_raw, proj, _after(hg_norm, t_in))
    d_qh, d_fh, d_ih, d_hl = _hgrn_bwd(proj, hl, states, d_o)
    t_mix = scatter_on("mix", d_qh)
    d_qa, d_ka, d_va, d_sinks = _attn_bwd(proj, tabs, _after(sinks_pad, t_mix), d_att)
    d_proj = jnp.concatenate([d_qa, d_ka.astype(BF16), d_va.astype(BF16), d_qh, d_fh, d_ih, d_gh, d_gate_a, d_gate_h], axis=1)
    dh1 = jnp.concatenate([_mm_nn(d_proj, w_half, s // 2, 512, IN_COLS // 2, F32, "proj_dx_" + tag)
                           for tag, w_half in (("a", w_in_a), ("b", w_in_b))], axis=1)
    grad_x, d_shift1, d_scale1, dg_pre_mix = _pre_bwd(dh1, x, dx1, g_pre_mix, mod, 1, "pre_mix_bwd")
    d_mod = jnp.concatenate([d_shift1, d_scale1, d_gate1, d_shift2, d_scale2, d_gate2], axis=1)
    small = [d_mod, dg_pre_mix, dg_post_mix, dg_pre_ffn, dg_post_ffn, d_hl.reshape(1, 2 * HG_W), d_hg_norm, d_sinks]
    return loss, grad_x, small, h1, d_proj


def kernel(x, c, positions, w_ada, b_ada, g_pre_mix, g_post_mix, g_pre_ffn, g_post_ffn, w_in, attn_sinks, w_attn_proj, hg_lower_bounds, hg_norm, w_hgrn_proj, w_out, w_ffn_in, w_ffn_out, loss_target, m_w_ada, m_b_ada, m_g_pre_mix, m_g_post_mix, m_g_pre_ffn, m_g_post_ffn, m_w_in, m_attn_sinks, m_w_attn_proj, m_hg_lower_bounds, m_hg_norm, m_w_hgrn_proj, m_w_out, m_w_ffn_in, m_w_ffn_out, v_w_ada, v_b_ada, v_g_pre_mix, v_g_post_mix, v_g_pre_ffn, v_g_post_ffn, v_w_in, v_attn_sinks, v_w_attn_proj, v_hg_lower_bounds, v_hg_norm, v_w_hgrn_proj, v_w_out, v_w_ffn_in, v_w_ffn_out):
    my_id = _lin(_my_coords())
    s = x.shape[1]
    n_ada = w_ada.shape[2]

    c_all = _exchange_small(c.reshape(1, 1, D), True, "gather_c").reshape(N_DEV, D)
    b_cols = lax.dynamic_slice(b_ada, (0, my_id * n_ada), (1, n_ada))
    mod_part = _mod_part(c_all, w_ada[0], b_cols)
    mod = _exchange_small(mod_part.reshape(N_DEV, 1, n_ada), False, "scatter_mod").reshape(1, N_MOD * D)
    groups = {"in_a": [w_in[0].T[:, :D // 2]], "in_b": [w_in[0].T[:, D // 2:]], "mix": [w_attn_proj[0], w_hgrn_proj[0], w_out[0]],
              "ffn_in": [w_ffn_in[0]], "ffn_out": [w_ffn_out[0]]}

    def start(group, dep):
        shards, dep = lax.optimization_barrier((groups[group], dep))
        return _gather_start([t.astype(BF16) for t in shards], "gather_start_" + group, deps=[dep])

    gathers = {"in_a": start("in_a", mod)}
    gathers["in_b"] = start("in_b", gathers["in_a"]["token"])
    gathers["mix"] = start("mix", gathers["in_b"]["token"])
    gathers["ffn_in"] = start("ffn_in", gathers["mix"]["token"])
    gathers["ffn_out"] = start("ffn_out", gathers["ffn_in"]["token"])

    def weights(group, after):
        after = [after, gathers["ffn_out"]["token"]]
        lands = _gather_pass(_gather_wait(gathers[group], after, "gather_wait_" + group), "gather_pass_" + group)
        if group in ("in_a", "in_b"):
            return (lands[0].reshape(IN_COLS, D // 2),)
        if group == "mix":
            return lands[0], lands[1], lands[2].reshape(D, D)
        return (lands[0],) if group == "ffn_in" else (lands[0].reshape(FFN, D),)

    pairs, scatters = {}, {}

    def scatter(grads, group):
        pairs[group] = _pair_start(grads, "scatter_pair_" + group)
        return pairs[group]["token"]

    def scatter_on(group, after):
        if group in pairs:
            local, theirs = _pair_wait(pairs[group], [after], "scatter_pair_wait_" + group)
        else:
            local, theirs = after, _pair_exchange(after, "scatter_pair_" + group)
        parts = [_pair_add(g, t, "scatter_pair_add_%s_%d" % (group, k)) for k, (g, t) in enumerate(zip(local, theirs))]
        scatters[group] = _chips_start(parts, "scatter_start_" + group)
        return scatters[group]["token"]

    sinks_pad = jnp.pad(attn_sinks, ((0, 0), (0, LANE - ATT_HEADS)))
    loss, grad_x, small, h1, d_proj = _local_step(
        x[0], loss_target[0], _rope_tables(positions), mod, sinks_pad, hg_lower_bounds, hg_norm, g_pre_mix, g_post_mix, g_pre_ffn, g_post_ffn,
        weights, scatter, scatter_on)
    loss = lax.psum(loss[0, 0], ("x", "y", "c"))

    sizes = [t.shape[1] for t in small]
    parts = _exchange_small(jnp.concatenate(small, axis=1).reshape(1, 1, sum(sizes)), True, "gather_small_grads")
    dep = parts
    for half, cols in (("in_a", slice(0, D // 2)), ("in_b", slice(D // 2, D))):
        gw_half = _mm_tn(d_proj, h1[:, cols], 256, D // 2, BF16, "proj_dw_" + half, deps=[dep])
        dep = scatter_on(half, [gw_half.reshape(N_DEV, IN_COLS // N_DEV, D // 2)])
    offs = [sum(sizes[:k]) for k in range(len(sizes))]
    piece = lambda k, n=None: parts[:, :, offs[k]:offs[k] + (sizes[k] if n is None else n)]
    small_w = [(piece(0), b_ada, m_b_ada, v_b_ada), (piece(1), g_pre_mix, m_g_pre_mix, v_g_pre_mix),
               (piece(2), g_post_mix, m_g_post_mix, v_g_post_mix), (piece(3), g_pre_ffn, m_g_pre_ffn, v_g_pre_ffn),
               (piece(4), g_post_ffn, m_g_post_ffn, v_g_post_ffn),
               (piece(5).reshape(N_DEV, 2, HG_W), hg_lower_bounds, m_hg_lower_bounds, v_hg_lower_bounds),
               (piece(6), hg_norm, m_hg_norm, v_hg_norm), (piece(7, ATT_HEADS), attn_sinks, m_attn_sinks, v_attn_sinks)]
    names = ["b_ada", "g_pre_mix", "g_post_mix", "g_pre_ffn", "g_post_ffn", "hg_lower_bounds", "hg_norm", "attn_sinks"]
    res = {n: _adamw(p, w, m, v, "adamw_" + n) for n, (p, w, m, v) in zip(names, small_w)}

    dmod_cols = lax.dynamic_slice(parts.reshape(N_DEV, -1), (0, my_id * n_ada), (N_DEV, n_ada))
    g_w_ada = _grad_w_ada(c_all.T, dmod_cols)
    res["w_ada"] = [g_w_ada] + list(_adamw(g_w_ada[None], w_ada[0], m_w_ada[0], v_w_ada[0], "adamw_w_ada", emit_grad=False))

    big = {"ffn_out": [("w_ffn_out", w_ffn_out, m_w_ffn_out, v_w_ffn_out)], "ffn_in": [("w_ffn_in", w_ffn_in, m_w_ffn_in, v_w_ffn_in)],
           "mix": [("w_attn_proj", w_attn_proj, m_w_attn_proj, v_w_attn_proj), ("w_hgrn_proj", w_hgrn_proj, m_w_hgrn_proj, v_w_hgrn_proj),
                   ("w_out", w_out, m_w_out, v_w_out)]}
    after = [scatters["in_b"]["token"]]
    for group, members in big.items():
        local, lands = _chips_wait(scatters[group], after, "scatter_wait_" + group)
        for (n, w, m, v), mine, land in zip(members, local, lands):
            res[n] = _adamw(land, w[0], m[0], v[0], "adamw_" + n, own=mine[0])
            after = after + [res[n][1]]
    after = [res[n][1] for n in res]
    halves = [_chips_wait(scatters[half], after, "scatter_wait_" + half) for half in ("in_a", "in_b")]
    own = jnp.concatenate([local[0][0] for local, _ in halves], axis=1)
    land = jnp.concatenate([lands[0] for _, lands in halves], axis=2)
    res["w_in"] = [t.T for t in _adamw(land, w_in[0].T, m_w_in[0].T, v_w_in[0].T, "adamw_w_in", own=own)]

    order = ["w_ada", "b_ada", "g_pre_mix", "g_post_mix", "g_pre_ffn", "g_post_ffn", "w_in", "attn_sinks", "w_attn_proj",
             "hg_lower_bounds", "hg_norm", "w_hgrn_proj", "w_out", "w_ffn_in", "w_ffn_out"]
    lead = {"w_ada", "w_in", "w_attn_proj", "w_hgrn_proj", "w_out", "w_ffn_in", "w_ffn_out"}
    outs = [loss, grad_x[None]]
    for k in range(4):
        outs += [res[n][k][None] if n in lead else res[n][k] for n in order]
    return tuple(outs)
```

```python
import functools

import jax
import jax.numpy as jnp
from jax import lax
from jax.experimental import pallas as pl
from jax.experimental.pallas import tpu as pltpu

F32 = jnp.float32
BF16 = jnp.bfloat16

N_DEV = 8
D = 2048
ATT_HEADS = 16
KV_HEADS = 2
HEAD_DIM = 64
GROUP = ATT_HEADS // KV_HEADS
ATT_W = ATT_HEADS * HEAD_DIM
BLK = 128
ROT = HEAD_DIM // 4
ROPE_THETA = 500000.0
HG_HEADS = 8
HG_K = 128
HG_W = HG_HEADS * HG_K
CHUNK = 64
SUB = 16
FFN = 5632
N_MOD = 6
EPS = 1e-6
LANE = 128
Q_A, K_A, V_A, Q_H, F_H, I_H, G_H, GT_A, GT_H, IN_COLS = 0, 1024, 1152, 1280, 2304, 3328, 4352, 5376, 7424, 9472

ADAM_LR, ADAM_B1, ADAM_B2, ADAM_EPS, ADAM_WD, ADAM_STEP = 0.001, 0.9, 0.999, 1e-08, 0.01, 10

TR = 256
HG_TB = 512
VMEM_BIG = 56 << 20
MESH = pl.DeviceIdType.MESH


def _sds(shape, dtype):
    return jax.ShapeDtypeStruct(shape, dtype)


def _params(n_axes, vmem=None):
    return pltpu.CompilerParams(dimension_semantics=("arbitrary",) * n_axes, vmem_limit_bytes=vmem)


def _sig(t):
    return 1.0 / (1.0 + jnp.exp(-t))


def _dot(a, b, dims):
    return lax.dot_general(a, b, (dims, ((), ())), preferred_element_type=F32)


NN = ((1,), (0,))
NT = ((1,), (1,))
TN = ((0,), (0,))


def _matmul(a, b, a_spec, b_spec, o_spec, out_shape, grid, dims, acc_shape, name, deps=(), add=None):
    nk = grid[2]
    nd = len(deps)
    extra = [] if add is None else [add]

    def body(a_ref, b_ref, *rest):
        o_ref, scratch = rest[nd + len(extra)], rest[nd + len(extra) + 1:]
        part = _dot(a_ref[...], b_ref[...], dims)
        if add is not None:
            assert nk == 1
            part = part + rest[nd][...]
        if nk == 1:
            o_ref[...] = part.astype(o_ref.dtype)
        else:
            acc = scratch[0]
            k = pl.program_id(2)

            @pl.when(k == 0)
            def _():
                acc[...] = part

            @pl.when(k > 0)
            def _():
                acc[...] += part

            @pl.when(k == nk - 1)
            def _():
                o_ref[...] = acc[...].astype(o_ref.dtype)

    return pl.pallas_call(
        body, grid=grid, in_specs=[a_spec, b_spec] + [pl.BlockSpec(memory_space=pl.ANY)] * nd + [o_spec] * len(extra),
        out_specs=o_spec, out_shape=out_shape, scratch_shapes=[pltpu.VMEM(acc_shape, F32)] if nk > 1 else [],
        input_output_aliases={2 + nd: 0} if extra else {},
        compiler_params=_params(3, VMEM_BIG), name=name)(a, b, *deps, *extra)


def _mm_nn(a, b, tm, tn, tk, out_dtype, name):
    m, k = a.shape
    n = b.shape[1]
    return _matmul(a, b, pl.BlockSpec((tm, tk), lambda j, i, kk: (i, kk)), pl.BlockSpec((tk, tn), lambda j, i, kk: (kk, j)),
                   pl.BlockSpec((tm, tn), lambda j, i, kk: (i, j)), _sds((m, n), out_dtype),
                   (n // tn, m // tm, k // tk), NN, (tm, tn), name)


def _mm_nn_dm(a, b, tm, out_dtype, name):
    m, k = a.shape
    n = b.shape[2]
    return _matmul(a, b, pl.BlockSpec((tm, k), lambda j, i, kk: (i, 0)), pl.BlockSpec((None, k, n), lambda j, i, kk: (j, 0, 0)),
                   pl.BlockSpec((tm, n), lambda j, i, kk: (i, j)), _sds((m, N_DEV * n), out_dtype),
                   (N_DEV, m // tm, 1), NN, (tm, n), name)


def _mm_nt(a, b, tm, tn, tk, out_dtype, name, deps=(), add=None):
    m, k = a.shape
    n = b.shape[0]
    return _matmul(a, b, pl.BlockSpec((tm, tk), lambda j, i, kk: (i, kk)), pl.BlockSpec((tn, tk), lambda j, i, kk: (j, kk)),
                   pl.BlockSpec((tm, tn), lambda j, i, kk: (i, j)), _sds((m, n), out_dtype),
                   (n // tn, m // tm, k // tk), NT, (tm, tn), name, deps, add)


def _mm_nt_dm(a, b, tm, tn, out_dtype, name, deps=()):
    m = a.shape[0]
    n_out, n = b.shape[1], b.shape[2]
    return _matmul(a, b, pl.BlockSpec((tm, n), lambda j, i, kk: (i, kk)), pl.BlockSpec((None, tn, n), lambda j, i, kk: (kk, j, 0)),
                   pl.BlockSpec((tm, tn), lambda j, i, kk: (i, j)), _sds((m, n_out), out_dtype),
                   (n_out // tn, m // tm, N_DEV), NT, (tm, tn), name, deps)


def _mm_tn(a, b, tm, tn, out_dtype, name, deps=()):
    s, m = a.shape
    n = b.shape[1]
    return _matmul(a, b, pl.BlockSpec((s, tm), lambda j, i, kk: (0, i)), pl.BlockSpec((s, tn), lambda j, i, kk: (0, j)),
                   pl.BlockSpec((tm, tn), lambda j, i, kk: (i, j)), _sds((m, n), out_dtype),
                   (n // tn, m // tm, 1), TN, (tm, tn), name, deps)


def _mm_tn_dm(a, b, tm, out_dtype, name):
    s, m = a.shape
    n = b.shape[1] // N_DEV
    return _matmul(a, b, pl.BlockSpec((s, tm), lambda j, i, kk: (0, i)), pl.BlockSpec((s, n), lambda j, i, kk: (0, j)),
                   pl.BlockSpec((None, tm, n), lambda j, i, kk: (j, i, 0)), _sds((N_DEV, m, n), out_dtype),
                   (N_DEV, m // tm, 1), TN, (tm, n), name)


def _row_spec():
    return pl.BlockSpec((TR, D), lambda i: (i, 0))


def _vec_spec(k=0):
    return pl.BlockSpec((1, D), lambda i: (0, k))


def _acc_rows(ref, first, val):
    @pl.when(first)
    def _():
        ref[...] = val

    @pl.when(jnp.logical_not(first))
    def _():
        ref[...] += val


def _pre_fwd(x, g, mod, k_scale, k_shift, name):
    s = x.shape[0]

    def body(x_ref, g_ref, sc_ref, sh_ref, h_ref):
        xv = x_ref[...]
        r = lax.rsqrt(jnp.mean(xv * xv, axis=-1, keepdims=True) + EPS)
        n = xv * r * g_ref[...]
        h_ref[...] = (n * (1.0 + sc_ref[...]) + sh_ref[...]).astype(h_ref.dtype)

    return pl.pallas_call(body, grid=(s // TR,), in_specs=[_row_spec(), _vec_spec(), _vec_spec(k_scale), _vec_spec(k_shift)],
                          out_specs=_row_spec(), out_shape=_sds((s, D), BF16), compiler_params=_params(1), name=name)(x, g, mod, mod)


def _post_fwd(x, y, g, mod, k_gate, name):
    s = x.shape[0]

    def body(x_ref, y_ref, g_ref, gt_ref, o_ref):
        yv = y_ref[...]
        r = lax.rsqrt(jnp.mean(yv * yv, axis=-1, keepdims=True) + EPS)
        o_ref[...] = x_ref[...] + gt_ref[...] * (yv * r * g_ref[...])

    return pl.pallas_call(body, grid=(s // TR,), in_specs=[_row_spec(), _row_spec(), _vec_spec(), _vec_spec(k_gate)],
                          out_specs=_row_spec(), out_shape=_sds((s, D), F32), compiler_params=_params(1), name=name)(x, y, g, mod)


def _post_loss_bwd(x, y, g, mod, k_gate, tgt, name):
    s = x.shape[0]

    def body(x_ref, y_ref, g_ref, gt_ref, t_ref, e_ref, loss_ref, dy_ref, dgt_ref, dg_ref):
        first = pl.program_id(0) == 0
        yv, gv, gate = y_ref[...], g_ref[...], gt_ref[...]
        r = lax.rsqrt(jnp.mean(yv * yv, axis=-1, keepdims=True) + EPS)
        yh = yv * r
        err = x_ref[...] + gate * (yh * gv) - t_ref[...]
        e = err * (1.0 / D)
        e_ref[...] = e
        _acc_rows(loss_ref, first, 0.5 * jnp.sum(jnp.mean(err * err, axis=-1, keepdims=True), axis=0, keepdims=True))
        dn = e * gate
        dgn = dn * gv
        dy_ref[...] = (r * (dgn - yh * jnp.mean(dgn * yh, axis=-1, keepdims=True))).astype(dy_ref.dtype)
        _acc_rows(dgt_ref, first, jnp.sum(e * (yh * gv), axis=0, keepdims=True))
        _acc_rows(dg_ref, first, jnp.sum(dn * yh, axis=0, keepdims=True))

    return pl.pallas_call(body, grid=(s // TR,),
                          in_specs=[_row_spec(), _row_spec(), _vec_spec(), _vec_spec(k_gate), _row_spec()],
                          out_specs=[_row_spec(), pl.BlockSpec((1, 1), lambda i: (0, 0)), _row_spec(), _vec_spec(), _vec_spec()],
                          out_shape=[_sds((s, D), F32), _sds((1, 1), F32), _sds((s, D), BF16), _sds((1, D), F32), _sds((1, D), F32)],
                          compiler_params=_params(1), name=name)(x, y, g, mod, tgt)


def _pre_bwd(dh, x, res, g, mod, k_scale, name):
    s = x.shape[0]

    def body(dh_ref, x_ref, res_ref, g_ref, sc_ref, dx_ref, dsh_ref, dsc_ref, dg_ref):
        first = pl.program_id(0) == 0
        xv, dh_v, gv = x_ref[...], dh_ref[...], g_ref[...]
        r = lax.rsqrt(jnp.mean(xv * xv, axis=-1, keepdims=True) + EPS)
        xh = xv * r
        dn = dh_v * (1.0 + sc_ref[...])
        dgn = dn * gv
        dx_ref[...] = res_ref[...] + r * (dgn - xh * jnp.mean(dgn * xh, axis=-1, keepdims=True))
        _acc_rows(dsh_ref, first, jnp.sum(dh_v, axis=0, keepdims=True))
        _acc_rows(dsc_ref, first, jnp.sum(dh_v * (xh * gv), axis=0, keepdims=True))
        _acc_rows(dg_ref, first, jnp.sum(dn * xh, axis=0, keepdims=True))

    return pl.pallas_call(body, grid=(s // TR,),
                          in_specs=[_row_spec(), _row_spec(), _row_spec(), _vec_spec(), _vec_spec(k_scale)],
                          out_specs=[_row_spec(), _vec_spec(), _vec_spec(), _vec_spec()],
                          out_shape=[_sds((s, D), F32)] + [_sds((1, D), F32)] * 3,
                          compiler_params=_params(1), name=name)(dh, x, res, g, mod)


def _post_bwd(dx, y, g, mod, k_gate, name):
    s = y.shape[0]

    def body(dx_ref, y_ref, g_ref, gt_ref, dy_ref, dgt_ref, dg_ref):
        first = pl.program_id(0) == 0
        yv, dxv, gv = y_ref[...], dx_ref[...], g_ref[...]
        r = lax.rsqrt(jnp.mean(yv * yv, axis=-1, keepdims=True) + EPS)
        yh = yv * r
        dn = dxv * gt_ref[...]
        dgn = dn * gv
        dy_ref[...] = (r * (dgn - yh * jnp.mean(dgn * yh, axis=-1, keepdims=True))).astype(dy_ref.dtype)
        _acc_rows(dgt_ref, first, jnp.sum(dxv * (yh * gv), axis=0, keepdims=True))
        _acc_rows(dg_ref, first, jnp.sum(dn * yh, axis=0, keepdims=True))

    return pl.pallas_call(body, grid=(s // TR,), in_specs=[_row_spec(), _row_spec(), _vec_spec(), _vec_spec(k_gate)],
                          out_specs=[_row_spec(), _vec_spec(), _vec_spec()],
                          out_shape=[_sds((s, D), BF16), _sds((1, D), F32), _sds((1, D), F32)],
                          compiler_params=_params(1), name=name)(dx, y, g, mod)


SW_TN = 1408
SW_TR = 512
TALL = 1024


def _swiglu_fwd(gu, deps=()):
    s = gu.shape[0]
    nb = FFN // SW_TN

    def body(g_ref, u_ref, *rest):
        a_ref = rest[len(deps)]
        gv = g_ref[...]
        a_ref[...] = (gv * _sig(gv) * u_ref[...]).astype(a_ref.dtype)

    return pl.pallas_call(body, grid=(s // SW_TR, nb),
                          in_specs=[pl.BlockSpec((SW_TR, SW_TN), lambda i, j: (i, j)), pl.BlockSpec((SW_TR, SW_TN), lambda i, j: (i, j + nb))]
                          + [pl.BlockSpec(memory_space=pl.ANY)] * len(deps),
                          out_specs=pl.BlockSpec((SW_TR, SW_TN), lambda i, j: (i, j)), out_shape=_sds((s, FFN), BF16),
                          compiler_params=_params(2, 48 << 20), name="swiglu_fwd")(gu, gu, *deps)


def _swiglu_bwd(dact, gu):
    s = gu.shape[0]
    nb = FFN // SW_TN
    n_steps = (s // SW_TR) * nb

    def body(da_ref, g_ref, u_ref, o_ref, buf, sems):
        i, j = pl.program_id(0), pl.program_id(1)
        step = i * nb + j
        slot = step % 2

        def tiles(sl):
            rows = pl.ds(pl.multiple_of(i * SW_TR, SW_TR), SW_TR)
            return [pltpu.make_async_copy(buf.at[sl, h], o_ref.at[rows, pl.ds(pl.multiple_of((j + nb * h) * SW_TN, LANE), SW_TN)], sems.at[sl, h])
                    for h in range(2)]

        @pl.when(step >= 2)
        def _():
            for cp in tiles(slot):
                cp.wait()

        gv, da = g_ref[...], da_ref[...]
        sg = _sig(gv)
        buf[slot, 0] = (da * u_ref[...] * (sg * (1.0 + gv * (1.0 - sg)))).astype(buf.dtype)
        buf[slot, 1] = (da * (gv * sg)).astype(buf.dtype)
        for cp in tiles(slot):
            cp.start()

        @pl.when(step == n_steps - 1)
        def _():
            for cp in tiles(slot) + (tiles(1 - slot) if n_steps > 1 else []):
                cp.wait()

    blk = lambda f: pl.BlockSpec((SW_TR, SW_TN), f)
    return pl.pallas_call(body, grid=(s // SW_TR, nb),
                          in_specs=[blk(lambda i, j: (i, j)), blk(lambda i, j: (i, j)), blk(lambda i, j: (i, j + nb))],
                          out_specs=pl.BlockSpec(memory_space=pl.ANY), out_shape=_sds((s, 2 * FFN), BF16),
                          scratch_shapes=[pltpu.VMEM((2, 2, SW_TR, SW_TN), BF16), pltpu.SemaphoreType.DMA((2, 2))],
                          compiler_params=_params(2, 48 << 20), name="swiglu_bwd")(dact, gu, gu)


MG_TN = 256


def _merge_fwd(y_a, y_h, proj):
    s = y_a.shape[0]
    tn = MG_TN
    ba, bh = GT_A // tn, GT_H // tn

    def body(ya_ref, yh_ref, ga_ref, gh_ref, m_ref):
        m_ref[...] = (_sig(ga_ref[...]) * ya_ref[...] + _sig(gh_ref[...]) * yh_ref[...]).astype(m_ref.dtype)

    tr = min(s, TALL)
    blk = lambda f: pl.BlockSpec((tr, tn), f)
    return pl.pallas_call(body, grid=(s // tr, D // tn),
                          in_specs=[blk(lambda i, j: (i, j)), blk(lambda i, j: (i, j)), blk(lambda i, j: (i, j + ba)), blk(lambda i, j: (i, j + bh))],
                          out_specs=blk(lambda i, j: (i, j)), out_shape=_sds((s, D), BF16),
                          compiler_params=_params(2), name="merge_fwd")(y_a, y_h, proj, proj)


def _merge_bwd(dm, y_a, y_h, proj):
    s = y_a.shape[0]
    tn = MG_TN
    ba, bh = GT_A // tn, GT_H // tn

    def body(dm_ref, ya_ref, yh_ref, ga_ref, gh_ref, dya_ref, dyh_ref, dga_ref, dgh_ref):
        dmv = dm_ref[...]
        sa, sh = _sig(ga_ref[...]), _sig(gh_ref[...])
        dya_ref[...] = (dmv * sa).astype(BF16)
        dyh_ref[...] = (dmv * sh).astype(BF16)
        dga_ref[...] = (dmv * ya_ref[...] * (sa * (1.0 - sa))).astype(BF16)
        dgh_ref[...] = (dmv * yh_ref[...] * (sh * (1.0 - sh))).astype(BF16)

    tr = min(s, TALL)
    blk = lambda f: pl.BlockSpec((tr, tn), f)
    nat = blk(lambda i, j: (i, j))
    return pl.pallas_call(body, grid=(s // tr, D // tn),
                          in_specs=[nat, nat, nat, blk(lambda i, j: (i, j + ba)), blk(lambda i, j: (i, j + bh))],
                          out_specs=[nat] * 4, out_shape=[_sds((s, D), BF16)] * 4,
                          compiler_params=_params(2), name="merge_bwd")(dm, y_a, y_h, proj, proj)


def _hgout_fwd(o_raw, proj, hg_norm):
    s = o_raw.shape[0]
    bg = G_H // LANE

    def body(o_ref, g_ref, n_ref, out_ref):
        ov = o_ref[...]
        r = lax.rsqrt(jnp.mean(ov * ov, axis=-1, keepdims=True) + EPS)
        out_ref[...] = (ov * r * n_ref[...] * _sig(g_ref[...])).astype(out_ref.dtype)

    tr = min(s, TALL)
    blk = lambda f: pl.BlockSpec((tr, LANE), f)
    return pl.pallas_call(body, grid=(s // tr, HG_HEADS),
                          in_specs=[blk(lambda i, h: (i, h)), blk(lambda i, h: (i, h + bg)), pl.BlockSpec((1, LANE), lambda i, h: (0, 0))],
                          out_specs=blk(lambda i, h: (i, h)), out_shape=_sds((s, HG_W), BF16),
                          compiler_params=_params(2), name="hgout_fwd")(o_raw, proj, hg_norm)


def _hgout_bwd(d_out, o_raw, proj, hg_norm):
    s = o_raw.shape[0]
    bg = G_H // LANE

    def body(d_ref, o_ref, g_ref, n_ref, do_ref, dg_ref, dn_ref):
        first = jnp.logical_and(pl.program_id(0) == 0, pl.program_id(1) == 0)
        ov, dv, nv = o_ref[...], d_ref[...], n_ref[...]
        sg = _sig(g_ref[...])
        r = lax.rsqrt(jnp.mean(ov * ov, axis=-1, keepdims=True) + EPS)
        oh = ov * r
        d_on = dv * sg
        dg_ref[...] = (dv * (oh * nv) * (sg * (1.0 - sg))).astype(dg_ref.dtype)
        t = d_on * nv
        do_ref[...] = r * (t - oh * jnp.mean(t * oh, axis=-1, keepdims=True))
        _acc_rows(dn_ref, first, jnp.sum(d_on * oh, axis=0, keepdims=True))

    tr = min(s, TALL)
    blk = lambda f: pl.BlockSpec((tr, LANE), f)
    vec = pl.BlockSpec((1, LANE), lambda i, h: (0, 0))
    return pl.pallas_call(body, grid=(s // tr, HG_HEADS),
                          in_specs=[blk(lambda i, h: (i, h)), blk(lambda i, h: (i, h)), blk(lambda i, h: (i, h + bg)), vec],
                          out_specs=[blk(lambda i, h: (i, h)), blk(lambda i, h: (i, h)), vec],
                          out_shape=[_sds((s, HG_W), F32), _sds((s, HG_W), BF16), _sds((1, LANE), F32)],
                          compiler_params=_params(2), name="hgout_bwd")(d_out, o_raw, proj, hg_norm)


def _rope(t, cos, s_lo, s_hi):
    return t * cos + pltpu.roll(t, LANE - ROT // 2, 1) * s_lo + pltpu.roll(t, ROT // 2, 1) * s_hi


def _rope_wide(t, cos, s_lo, s_hi):
    return jnp.concatenate([_rope(t[:, k * LANE:(k + 1) * LANE], cos, s_lo, s_hi) for k in range(t.shape[1] // LANE)], axis=1)


def _attn_mask(has_prev):
    kj = lax.broadcasted_iota(jnp.int32, (2 * BLK, BLK), 0)
    qi = lax.broadcasted_iota(jnp.int32, (2 * BLK, BLK), 1)
    rel = BLK + qi - kj
    band = jnp.logical_and(rel >= 0, rel < BLK)
    return jnp.logical_and(band, jnp.logical_or(has_prev, kj >= BLK))


def _attn_specs():
    prev = lambda i: jnp.maximum(i - 1, 0)
    kb, vb = K_A // LANE, V_A // LANE
    blk = lambda f: pl.BlockSpec((BLK, LANE), f)
    tabs = [blk(lambda i: (i, 0))] * 3 + [blk(lambda i: (prev(i), 0))] * 3
    return [pl.BlockSpec((BLK, ATT_W), lambda i: (i, 0)), blk(lambda i: (i, kb)), blk(lambda i: (prev(i), kb)),
            blk(lambda i: (i, vb)), blk(lambda i: (prev(i), vb))] + tabs + [pl.BlockSpec((1, LANE), lambda i: (0, 0))]


def _attn_logits(qh, kg):
    return _dot(kg, qh, NT)


def _attn_probs(raw, mask, sk):
    logits = jnp.where(mask, raw * (HEAD_DIM ** -0.5), -jnp.inf)
    m = jnp.maximum(jnp.max(logits, axis=0, keepdims=True), sk)
    p = jnp.exp(logits - m)
    e_sink = jnp.exp(sk - m)
    inv = 1.0 / (jnp.sum(p, axis=0, keepdims=True) + e_sink)
    return p, inv, e_sink * inv


def _attn_fwd(proj, tabs, sinks):
    s = proj.shape[0]

    def body(q_ref, kc_ref, kp_ref, vc_ref, vp_ref, c0, l0, h0, c1, l1, h1, sk_ref, o_ref):
        i = pl.program_id(0)
        mask = _attn_mask(i > 0)
        q = _rope_wide(q_ref[...], c0[...], l0[...], h0[...]).astype(BF16)
        kk = jnp.concatenate([_rope(kp_ref[...], c1[...], l1[...], h1[...]), _rope(kc_ref[...], c0[...], l0[...], h0[...])], axis=0).astype(BF16)
        v_t = jnp.concatenate([vp_ref[...], vc_ref[...]], axis=0).T.astype(BF16)
        part = lambda t, h: t[:, h * HEAD_DIM:(h + 1) * HEAD_DIM]
        k_heads = [part(kk, g) for g in range(KV_HEADS)]

        def head(h):
            g = h // GROUP
            raw = _attn_logits(part(q, h), k_heads[g])
            yield
            p, inv, _ = _attn_probs(raw, mask, sk_ref[:, h:h + 1])
            yield
            out_t = _dot(v_t[g * HEAD_DIM:(g + 1) * HEAD_DIM], p.astype(BF16), NN)
            yield
            return out_t * inv

        o_ref[...] = jnp.concatenate(_interleave([head(h) for h in range(ATT_HEADS)]), axis=0).T.astype(o_ref.dtype)

    return pl.pallas_call(body, grid=(s // BLK,), in_specs=_attn_specs(),
                          out_specs=pl.BlockSpec((BLK, ATT_W), lambda i: (i, 0)), out_shape=_sds((s, ATT_W), BF16),
                          compiler_params=_params(1), name="attn_fwd")(proj, proj, proj, proj, proj, *tabs, *tabs, sinks)


def _attn_bwd(proj, tabs, sinks, d_att):
    s = proj.shape[0]

    def body(q_ref, kc_ref, kp_ref, vc_ref, vp_ref, c0, l0, h0, c1, l1, h1, sk_ref, do_ref, dq_ref, dk_ref, dv_ref, ds_ref):
        i = pl.program_id(0)

        @pl.when(i == 0)
        def _():
            dk_ref[...] = jnp.zeros_like(dk_ref)
            dv_ref[...] = jnp.zeros_like(dv_ref)
            ds_ref[...] = jnp.zeros_like(ds_ref)

        mask = _attn_mask(i > 0)
        q = _rope_wide(q_ref[...], c0[...], l0[...], h0[...]).astype(BF16)
        kk = jnp.concatenate([_rope(kp_ref[...], c1[...], l1[...], h1[...]), _rope(kc_ref[...], c0[...], l0[...], h0[...])], axis=0).astype(BF16)
        k_f32 = jnp.concatenate([_rope(kp_ref[...], c1[...], l1[...], h1[...]), _rope(kc_ref[...], c0[...], l0[...], h0[...])], axis=0)
        k_t = k_f32.T.astype(BF16)
        vv = jnp.concatenate([vp_ref[...], vc_ref[...]], axis=0).astype(BF16)
        d_o = do_ref[...].astype(BF16)
        lane = lax.broadcasted_iota(jnp.int32, (1, LANE), 1)
        part = lambda t, h: t[:, h * HEAD_DIM:(h + 1) * HEAD_DIM]
        k_heads = [part(kk, g) for g in range(KV_HEADS)]
        v_heads = [part(vv, g) for g in range(KV_HEADS)]

        def head(h):
            g = h // GROUP
            qh, doh = part(q, h), part(d_o, h)
            raw = _attn_logits(qh, k_heads[g])
            d_p = _dot(v_heads[g], doh, NT)
            yield
            p, inv, p_sink = _attn_probs(raw, mask, sk_ref[:, h:h + 1])
            prob = p * inv
            dv = _dot(prob.astype(BF16), doh, NN)
            yield
            dd = jnp.sum(prob * d_p, axis=0, keepdims=True)
            d_s = (prob * (d_p - dd)).astype(BF16)
            d_sink = jnp.where(lane == h, -jnp.sum(p_sink * dd, axis=1, keepdims=True), 0.0)
            dq_t = _dot(k_t[g * HEAD_DIM:(g + 1) * HEAD_DIM], d_s, NN)
            dk = _dot(d_s, qh, NN)
            yield
            return dq_t * (HEAD_DIM ** -0.5), dk * (HEAD_DIM ** -0.5), dv, d_sink

        per_head = _interleave([head(h) for h in range(ATT_HEADS)])
        dqs = [jnp.concatenate([t[0] for t in per_head], axis=0).T]
        group_sum = lambda k, g: functools.reduce(jnp.add, [t[k] for t in per_head[g * GROUP:(g + 1) * GROUP]])
        dks = [group_sum(1, g) for g in range(KV_HEADS)]
        dvs = [group_sum(2, g) for g in range(KV_HEADS)]
        d_sink = functools.reduce(jnp.add, [t[3] for t in per_head])
        dq_ref[...] = _rope_wide(jnp.concatenate(dqs, axis=1), c0[...], -l0[...], -h0[...]).astype(dq_ref.dtype)
        d_k = jnp.concatenate(dks, axis=1)
        d_v = jnp.concatenate(dvs, axis=1)
        cur = pl.ds(pl.multiple_of(i * BLK, BLK), BLK)
        prv = pl.ds(pl.multiple_of(jnp.maximum(i - 1, 0) * BLK, BLK), BLK)
        dk_ref[prv, :] += _rope(d_k[:BLK], c1[...], -l1[...], -h1[...])
        dk_ref[cur, :] += _rope(d_k[BLK:], c0[...], -l0[...], -h0[...])
        dv_ref[prv, :] += d_v[:BLK]
        dv_ref[cur, :] += d_v[BLK:]
        ds_ref[...] += d_sink

    full = pl.BlockSpec((s, LANE), lambda i: (0, 0))
    return pl.pallas_call(body, grid=(s // BLK,), in_specs=_attn_specs() + [pl.BlockSpec((BLK, ATT_W), lambda i: (i, 0))],
                          out_specs=[pl.BlockSpec((BLK, ATT_W), lambda i: (i, 0)), full, full, pl.BlockSpec((1, LANE), lambda i: (0, 0))],
                          out_shape=[_sds((s, ATT_W), BF16), _sds((s, LANE), F32), _sds((s, LANE), F32), _sds((1, LANE), F32)],
                          compiler_params=_params(1), name="attn_bwd")(proj, proj, proj, proj, proj, *tabs, *tabs, sinks, d_att)


def _tri_matmul(tri, t):
    hi = t.astype(BF16)
    r1 = t - hi.astype(F32)
    mid = r1.astype(BF16)
    lo = (r1 - mid.astype(F32)).astype(BF16)
    return _dot(tri, hi, NN) + _dot(tri, mid, NN) + _dot(tri, lo, NN)


def _lower_bound(hl):
    a, b = hl[0:1, :], hl[1:2, :]
    mx = jnp.maximum(a, b)
    ea, eb = jnp.exp(a - mx), jnp.exp(b - mx)
    return ea / (ea + eb)


def _hg_gates(q_raw, f_raw, lb, tri_lower):
    sg = _sig(f_raw)
    f = lb + (1.0 - lb) * sg
    sq = _sig(q_raw)
    b = _tri_matmul(tri_lower, jnp.log(f))
    return sg, f, 1.0 - f, sq, q_raw * sq, b


HG_PAIR_FWD = 8
HG_PAIR_BWD = 8


def _hg_specs(n_map, pair):
    blk = lambda off, p: pl.BlockSpec((HG_TB, LANE), lambda h, n: (n_map(n), off // LANE + pair * h + p))
    return [blk(off, p) for off in (Q_H, F_H, I_H) for p in range(pair)] + [pl.BlockSpec((2, pair * LANE), lambda h, n: (0, h))]


def _interleave(gens):
    out = [None] * len(gens)
    live = list(range(len(gens)))
    while live:
        for k in list(live):
            try:
                next(gens[k])
            except StopIteration as stop:
                out[k] = stop.value
                live.remove(k)
    return out


def _hg_spread():
    c = lax.broadcasted_iota(jnp.int32, (CHUNK, SUB * SUB), 0)
    l = lax.broadcasted_iota(jnp.int32, (CHUNK, SUB * SUB), 1)
    r = lax.broadcasted_iota(jnp.int32, (SUB, SUB * SUB), 0)
    lr = lax.broadcasted_iota(jnp.int32, (SUB, SUB * SUB), 1)
    shift = SUB.bit_length() - 1
    cols = [(c == lo + (l >> shift)).astype(BF16) for lo in range(0, CHUNK, SUB)]
    tile = [(c == lo + (l & (SUB - 1))).astype(BF16) for lo in range(0, CHUNK, SUB)]
    return cols, tile, (lr & (SUB - 1)) == r, (lr >> shift) == r


def _hg_intra(qs, kk, b, grad=None):
    lane = lax.broadcasted_iota(jnp.int32, (SUB, CHUNK), 1)
    row1 = lax.broadcasted_iota(jnp.int32, (SUB, 1), 0)
    kk_b = kk.astype(BF16)
    if grad is not None:
        d_a, d_at, (cols, tile, diag, block) = grad
    a_blocks, dq_blocks, dk_blocks, db_blocks = [], [], [], []
    dk_left = None
    for j in range(CHUNK // SUB):
        lo = j * SUB
        q_j, k_j, b_j = qs[lo:lo + SUB], kk[lo:lo + SUB], b[lo:lo + SUB]
        es = [jnp.where(row1 >= sx, jnp.exp(jnp.minimum(b_j - b_j[sx:sx + 1], 0.0)), 0.0) for sx in range(SUB)]
        pes = [q_j * e for e in es]
        pe = jnp.concatenate(pes, axis=0).astype(BF16)
        pairs = _dot(pe, kk_b, NT)
        yield
        a_j = jnp.zeros((SUB, CHUNK), F32)
        for sx in range(SUB):
            a_j = jnp.where(lane == lo + sx, pairs[sx * SUB:(sx + 1) * SUB], a_j)
        if grad is not None:
            da_j = d_a[lo:lo + SUB]
            ek = jnp.concatenate([e * k_j[sx:sx + 1] for sx, e in enumerate(es)], axis=0).astype(BF16)
            sel_t = jnp.where(diag, _dot(da_j.astype(BF16), cols[j], NN), 0.0).astype(BF16)
            sel_s = jnp.where(block, _dot(d_at[lo:lo + SUB].astype(BF16), tile[j], NN), 0.0).astype(BF16)
            pek = jnp.concatenate([p * k_j[sx:sx + 1] for sx, p in enumerate(pes)], axis=0).astype(BF16)
            yield
            dq_j = _dot(sel_t, ek, NN)
            dk_j = _dot(sel_s, pe, NN)
            db_j = _dot(sel_t, pek, NN) - _dot(sel_s, pek, NN)
            yield
        if j > 0:
            ref = b[lo - 1:lo]
            sc_q = jnp.exp(b_j - ref)
            sc_k = jnp.exp(jnp.minimum(ref - b, 0.0))
            qt = (q_j * sc_q).astype(BF16)
            kt = (kk * sc_k).astype(BF16)
            left = _dot(qt, kt, NT)
            yield
            a_j = a_j + jnp.where(lane < lo, left, 0.0)
            if grad is not None:
                da_left = jnp.where(lane < lo, da_j, 0.0).astype(BF16)
                dq_left = _dot(da_left, kt, NN) * sc_q
                dq_j = dq_j + dq_left
                db_j = db_j + q_j * dq_left
                t = _dot(da_left, qt, TN)
                yield
                t = t * sc_k
                dk_left = t if dk_left is None else dk_left + t
        a_blocks.append(a_j)
        if grad is not None:
            dq_blocks.append(dq_j)
            dk_blocks.append(dk_j)
            db_blocks.append(db_j)
    a = jnp.concatenate(a_blocks, axis=0)
    if grad is None:
        return a
    return a, jnp.concatenate(dq_blocks, axis=0), jnp.concatenate(dk_blocks, axis=0) + dk_left, jnp.concatenate(db_blocks, axis=0) - kk * dk_left


def _hgrn_fwd(proj, hl):
    s = proj.shape[0]
    n_chunk = HG_TB // CHUNK
    pair = HG_PAIR_FWD

    def body(*refs):
        q_refs, f_refs, i_refs = refs[:pair], refs[pair:2 * pair], refs[2 * pair:3 * pair]
        hl_ref, o_ref, st_out_ref, st_ref = refs[3 * pair:]

        @pl.when(pl.program_id(1) == 0)
        def _():
            st_ref[...] = jnp.zeros_like(st_ref)

        r_i = lax.broadcasted_iota(jnp.int32, (CHUNK, CHUNK), 0)
        c_i = lax.broadcasted_iota(jnp.int32, (CHUNK, CHUNK), 1)
        tri_lower = (r_i >= c_i).astype(BF16)

        def chunk(c, carry):
            rows = pl.ds(pl.multiple_of(c * CHUNK, CHUNK), CHUNK)
            def head(p):
                cols = slice(p * LANE, (p + 1) * LANE)
                lb = _lower_bound(hl_ref[:, cols])
                v = i_refs[p][rows, :].astype(BF16)
                _, _, kk, _, qs, b = _hg_gates(q_refs[p][rows, :], f_refs[p][rows, :], lb, tri_lower)
                yield
                st = st_ref[p]
                st_b = st.astype(BF16)
                st_out_ref[p, c] = st_b
                o_state = _dot((qs * jnp.exp(b)).astype(BF16), st_b, NT)
                b_last = b[CHUNK - 1:CHUNK, :]
                st_new = _dot(v, (kk * jnp.exp(b_last - b)).astype(BF16), TN)
                a = yield from _hg_intra(qs, kk, b)
                st_ref[p] = st * jnp.exp(b_last) + st_new
                o_ref[rows, cols] = o_state + _dot(a.astype(BF16), v, NN)

            _interleave([head(p) for p in range(pair)])
            return carry

        lax.fori_loop(0, n_chunk, chunk, 0)

    return pl.pallas_call(
        body, grid=(HG_HEADS // pair, s // HG_TB), in_specs=_hg_specs(lambda n: n, pair),
        out_specs=[pl.BlockSpec((HG_TB, pair * LANE), lambda h, n: (n, h)), pl.BlockSpec((pair, n_chunk, HG_K, HG_K), lambda h, n: (h, n, 0, 0))],
        out_shape=[_sds((s, HG_W), F32), _sds((HG_HEADS, s // CHUNK, HG_K, HG_K), BF16)],
        scratch_shapes=[pltpu.VMEM((pair, HG_K, HG_K), F32)],
        compiler_params=_params(2), name="hgrn_fwd")(*[proj] * (3 * pair), hl)


def _hgrn_bwd(proj, hl, states, d_o):
    s = proj.shape[0]
    n_chunk = HG_TB // CHUNK
    n_blk = s // HG_TB
    pair = HG_PAIR_BWD
    rev = lambda n: n_blk - 1 - n

    def body(*refs):
        q_refs, f_refs, i_refs = refs[:pair], refs[pair:2 * pair], refs[2 * pair:3 * pair]
        hl_ref, st_in_ref, do_ref, dq_ref, df_ref, di_ref, dhl_ref, dst_ref, dlb_ref = refs[3 * pair:]
        n = pl.program_id(1)

        @pl.when(n == 0)
        def _():
            dst_ref[...] = jnp.zeros_like(dst_ref)
            dlb_ref[...] = jnp.zeros_like(dlb_ref)

        r_i = lax.broadcasted_iota(jnp.int32, (CHUNK, CHUNK), 0)
        c_i = lax.broadcasted_iota(jnp.int32, (CHUNK, CHUNK), 1)
        tri_lower = (r_i >= c_i).astype(BF16)
        tri_upper = (r_i <= c_i).astype(BF16)
        row = lax.broadcasted_iota(jnp.int32, (CHUNK, 1), 0)
        spread = _hg_spread()

        def chunk(cc, carry):
            c = n_chunk - 1 - cc
            rows = pl.ds(pl.multiple_of(c * CHUNK, CHUNK), CHUNK)
            def head(p):
                cols = slice(p * LANE, (p + 1) * LANE)
                lb = _lower_bound(hl_ref[:, cols])
                q_raw = q_refs[p][rows, :]
                vb = i_refs[p][rows, :].astype(BF16)
                sg, f, kk, sq, qs, b = _hg_gates(q_raw, f_refs[p][rows, :], lb, tri_lower)
                yield
                e_b = jnp.exp(b)
                qe = qs * e_b
                b_last = b[CHUNK - 1:CHUNK, :]
                e_last = jnp.exp(b_last)
                e_kd = jnp.exp(b_last - b)
                kd = kk * e_kd
                st0 = st_in_ref[p, c]
                d_ob = do_ref[rows, cols].astype(BF16)
                dst = dst_ref[p]
                dst_b = dst.astype(BF16)
                d_a = jnp.where(r_i >= c_i, _dot(d_ob, vb, NT), 0.0)
                d_at = jnp.where(r_i <= c_i, _dot(vb, d_ob, NT), 0.0)
                d_v_st = _dot(kd.astype(BF16), dst_b, NT)
                d_kd = _dot(vb, dst_b, NN)
                d_qe = _dot(d_ob, st0, NN)
                dst_new = _dot(d_ob, qe.astype(BF16), TN)
                yield
                a, dqs, dkk, d_b = yield from _hg_intra(qs, kk, b, (d_a, d_at, spread))
                d_v = _dot(a.astype(BF16), d_ob, TN) + d_v_st
                dqs_st = d_qe * e_b
                dkk_st = d_kd * e_kd
                dqs = dqs + dqs_st
                dkk = dkk + dkk_st
                d_b_last = jnp.sum(d_kd * kd, axis=0, keepdims=True) + jnp.sum(dst * st0.astype(F32), axis=0, keepdims=True) * e_last
                d_b = d_b + qs * dqs_st - kk * dkk_st + jnp.where(row == CHUNK - 1, d_b_last, 0.0)
                d_g = _tri_matmul(tri_upper, d_b)
                dst_ref[p] = dst_new + dst * e_last
                yield
                d_f = d_g / f - dkk
                dlb_ref[:, cols] += jnp.sum(d_f * (1.0 - sg), axis=0, keepdims=True)
                dq_ref[rows, cols] = (dqs * (sq * (1.0 + q_raw * (1.0 - sq)))).astype(dq_ref.dtype)
                df_ref[rows, cols] = (d_f * (1.0 - lb) * (sg * (1.0 - sg))).astype(df_ref.dtype)
                di_ref[rows, cols] = d_v.astype(di_ref.dtype)

            _interleave([head(p) for p in range(pair)])
            return carry

        lax.fori_loop(0, n_chunk, chunk, 0)

        @pl.when(n == n_blk - 1)
        def _():
            lb = _lower_bound(hl_ref[...])
            d_hl0 = dlb_ref[...] * (lb * (1.0 - lb))
            dhl_ref[...] = jnp.concatenate([d_hl0, -d_hl0], axis=0)

    out_blk = pl.BlockSpec((HG_TB, pair * LANE), lambda h, n: (rev(n), h))
    return pl.pallas_call(
        body, grid=(HG_HEADS // pair, n_blk),
        in_specs=_hg_specs(rev, pair) + [pl.BlockSpec((pair, n_chunk, HG_K, HG_K), lambda h, n: (h, rev(n), 0, 0)), out_blk],
        out_specs=[out_blk, out_blk, out_blk, pl.BlockSpec((2, pair * LANE), lambda h, n: (0, h))],
        out_shape=[_sds((s, HG_W), BF16)] * 3 + [_sds((2, HG_W), F32)],
        scratch_shapes=[pltpu.VMEM((pair, HG_K, HG_K), F32), pltpu.VMEM((1, pair * LANE), F32)],
        compiler_params=_params(2), name="hgrn_bwd")(*[proj] * (3 * pair), hl, states, d_o)


def _mod_part(c_all, w_shard, b_shard):
    n = w_shard.shape[1]
    tn = 512

    def body(c_ref, w_ref, b_ref, o_ref):
        o_ref[...] = _dot(c_ref[...].astype(BF16), w_ref[...].astype(BF16), NN) + b_ref[...]

    return pl.pallas_call(body, grid=(n // tn,),
                          in_specs=[pl.BlockSpec((N_DEV, D), lambda j: (0, 0)), pl.BlockSpec((D, tn), lambda j: (0, j)), pl.BlockSpec((1, tn), lambda j: (0, j))],
                          out_specs=pl.BlockSpec((N_DEV, tn), lambda j: (0, j)), out_shape=_sds((N_DEV, n), F32),
                          compiler_params=_params(1, 32 << 20), name="mod_part")(c_all, w_shard, b_shard)


def _grad_w_ada(c_all_t, dmod_cols):
    n = dmod_cols.shape[1]
    tn = 512

    def body(c_ref, d_ref, o_ref):
        cv = c_ref[...].astype(BF16).astype(F32)
        dv = d_ref[...].astype(BF16).astype(F32)
        acc = cv[:, 0:1] * dv[0:1, :]
        for k in range(1, N_DEV):
            acc = acc + cv[:, k:k + 1] * dv[k:k + 1, :]
        o_ref[...] = acc

    return pl.pallas_call(body, grid=(n // tn,),
                          in_specs=[pl.BlockSpec((D, N_DEV), lambda j: (0, 0)), pl.BlockSpec((N_DEV, tn), lambda j: (0, j))],
                          out_specs=pl.BlockSpec((D, tn), lambda j: (0, j)), out_shape=_sds((D, n), F32),
                          compiler_params=_params(1, 32 << 20), name="grad_w_ada")(c_all_t, dmod_cols)


def _row_tile(r, c, max_elems=1 << 18):
    if r * c <= max_elems or r % 8:
        return r
    best = 8
    for t in range(8, r + 1, 8):
        if r % t == 0 and t * c <= max_elems:
            best = t
    return best


WIDE_TILE = 5 << 17


def _adamw(pieces, w, m, v, name, emit_grad=True, own=None):
    p, r, c = pieces.shape
    tr = _row_tile(r, c)
    c1 = 1.0 / (1.0 - ADAM_B1 ** ADAM_STEP)
    c2 = 1.0 / (1.0 - ADAM_B2 ** ADAM_STEP)

    def body(*refs):
        if own is None:
            p_ref, w_ref, m_ref, v_ref, *outs = refs
            g = p_ref[0].astype(F32)
        else:
            o_ref, p_ref, w_ref, m_ref, v_ref, *outs = refs
            g = o_ref[...].astype(F32) + p_ref[0].astype(F32)
        for k in range(1, p):
            g = g + p_ref[k].astype(F32)
        m2 = ADAM_B1 * m_ref[...] + (1.0 - ADAM_B1) * g
        v2 = ADAM_B2 * v_ref[...] + (1.0 - ADAM_B2) * (g * g)
        delta = -ADAM_LR * ((m2 * c1) / (jnp.sqrt(v2 * c2) + ADAM_EPS) + ADAM_WD * w_ref[...])
        if emit_grad:
            outs[0][...] = g
        outs[-3][...] = delta
        outs[-2][...] = m2
        outs[-1][...] = v2

    blk = pl.BlockSpec((tr, c), lambda i: (i, 0))
    n_out = 4 if emit_grad else 3
    lead = [] if own is None else [own]
    return pl.pallas_call(body, grid=(r // tr,), in_specs=[blk] * len(lead) + [pl.BlockSpec((p, tr, c), lambda i: (0, i, 0)), blk, blk, blk],
                          out_specs=[blk] * n_out, out_shape=[_sds((r, c), F32)] * n_out,
                          compiler_params=_params(1, 48 << 20), name=name)(*lead, pieces, w, m, v)


def _my_coords():
    return lax.axis_index("x"), lax.axis_index("y"), lax.axis_index("c")


def _flip(coords, k):
    x, y, c = coords
    return (1 - x if k & 4 else x, 1 - y if k & 2 else y, 1 - c if k & 1 else c)


def _lin(coords):
    return 4 * coords[0] + 2 * coords[1] + coords[2]


def _exchange_small(x3, bcast, name):
    n = x3.shape[2]

    def body(x_ref, o_ref, send_sems, recv_sems):
        me = _my_coords()
        my_id = _lin(me)
        o_ref[pl.ds(my_id, 1)] = x_ref[pl.ds(0 if bcast else my_id, 1)]
        copies = []
        for k in range(1, N_DEV):
            peer = _flip(me, k)
            src = x_ref.at[0 if bcast else _lin(peer)]
            cp = pltpu.make_async_remote_copy(src_ref=src, dst_ref=o_ref.at[my_id], send_sem=send_sems.at[k], recv_sem=recv_sems.at[k],
                                              device_id=peer, device_id_type=MESH)
            cp.start()
            copies.append(cp)
        for k in range(1, N_DEV):
            peer = _flip(me, k)
            pltpu.make_async_remote_copy(src_ref=x_ref.at[0], dst_ref=o_ref.at[_lin(peer)], send_sem=send_sems.at[k], recv_sem=recv_sems.at[k],
                                         device_id=peer, device_id_type=MESH).wait_recv()
        for cp in copies:
            cp.wait_send()

    vm = pl.BlockSpec(memory_space=pltpu.VMEM)
    return pl.pallas_call(body, in_specs=[vm], out_specs=vm, out_shape=_sds((N_DEV, 1, n), F32),
                          scratch_shapes=[pltpu.SemaphoreType.DMA((N_DEV,)), pltpu.SemaphoreType.DMA((N_DEV,))], name=name)(x3)


HBM_SPEC = pl.BlockSpec(memory_space=pltpu.HBM)
SEM_SPEC = pl.BlockSpec(memory_space=pltpu.SEMAPHORE)
ANY_SPEC = pl.BlockSpec(memory_space=pl.ANY)
DATAFLOW = pltpu.SideEffectType.DATAFLOW_SIDE_EFFECTING
GATHER_FLIPS = (1, 2, 4, 6)
PASS_FLIPS = (2, 4, 6)
TOKEN = (8, LANE)


def _hbm(t):
    return pltpu.with_memory_space_constraint(t, pltpu.HBM)


def _hbm_like(ts):
    return [pltpu.HBM(t.shape, t.dtype) for t in ts]


def _split_start(issue, srcs, lands, n_sem, name, deps=()):
    n, nb, nd = len(srcs), len(srcs) + len(lands), len(deps)

    def body(*refs):
        issue(refs[:n], refs[n:nb], refs[nb + nd], refs[nb + nd + 1])
        refs[-1][...] = jnp.zeros(TOKEN, F32)

    outs = pl.pallas_call(
        body, name=name,
        out_shape=(pltpu.SemaphoreType.DMA((n_sem,)), pltpu.SemaphoreType.DMA((n_sem,)), *_hbm_like(srcs), *_hbm_like(lands), _sds(TOKEN, F32)),
        in_specs=[HBM_SPEC] * nb + [ANY_SPEC] * nd,
        out_specs=(SEM_SPEC, SEM_SPEC, *[HBM_SPEC] * nb, pl.BlockSpec(memory_space=pltpu.VMEM)),
        input_output_aliases={i: 2 + i for i in range(nb)},
        compiler_params=pltpu.CompilerParams(has_side_effects=DATAFLOW))(*[_hbm(t) for t in srcs], *[_hbm(t) for t in lands], *deps)
    return dict(sems=outs[:2], thru=list(outs[2:2 + nb]), token=outs[-1], n=n)


def _split_wait(finish, handle, after, name):
    n = handle["n"]
    thru = handle["thru"]
    nb = len(thru)

    def body(*refs):
        finish(refs[:n], refs[n:nb], refs[nb], refs[nb + 1])

    outs = pl.pallas_call(
        body, name=name, out_shape=_hbm_like(thru), in_specs=[HBM_SPEC] * nb + [SEM_SPEC, SEM_SPEC] + [ANY_SPEC] * len(after),
        out_specs=[HBM_SPEC] * nb, input_output_aliases={i: i for i in range(nb)},
        compiler_params=pltpu.CompilerParams(has_side_effects=DATAFLOW))(*thru, *handle["sems"], *after)
    return list(outs[:n]), list(outs[n:])


def _gather_start(shards, name, deps=()):
    n = len(shards)
    my_id = _lin(_my_coords())
    lands = [lax.dynamic_update_slice(lax.empty((N_DEV,) + t.shape, t.dtype), t[None], (my_id, 0, 0)) for t in shards]

    def issue(src, land, send_sems, recv_sems):
        me = _my_coords()
        for w in range(n):
            for j, k in enumerate(GATHER_FLIPS):
                q = len(GATHER_FLIPS) * w + j
                pltpu.make_async_remote_copy(src_ref=src[w], dst_ref=land[w].at[_lin(me)], send_sem=send_sems.at[q], recv_sem=recv_sems.at[q],
                                             device_id=_flip(me, k), device_id_type=MESH).start()

    return _split_start(issue, shards, lands, len(GATHER_FLIPS) * n, name, deps)


def _gather_wait(handle, after, name):
    n = handle["n"]

    def finish(src, land, send_sems, recv_sems):
        me = _my_coords()
        for w in range(n):
            for j, k in enumerate(GATHER_FLIPS):
                q = len(GATHER_FLIPS) * w + j
                peer = _flip(me, k)
                cp = pltpu.make_async_remote_copy(src_ref=src[w], dst_ref=land[w].at[_lin(peer)], send_sem=send_sems.at[q], recv_sem=recv_sems.at[q],
                                                  device_id=peer, device_id_type=MESH)
                cp.wait_send()
                cp.wait_recv()

    return _split_wait(finish, handle, after, name)[1]


def _pass_copy(land, send_sems, recv_sems, w, j, arriving):
    me = _my_coords()
    blk = land[w].at[_lin(_flip(me, PASS_FLIPS[j] + (1 if arriving else 0)))]
    q = len(PASS_FLIPS) * w + j
    return pltpu.make_async_remote_copy(src_ref=blk, dst_ref=blk, send_sem=send_sems.at[q], recv_sem=recv_sems.at[q],
                                        device_id=_flip(me, 1), device_id_type=MESH)


def _pass_start(lands, name, deps=()):
    def issue(_, land, send_sems, recv_sems):
        for w in range(len(lands)):
            for j in range(len(PASS_FLIPS)):
                _pass_copy(land, send_sems, recv_sems, w, j, False).start()

    return _split_start(issue, [], lands, len(PASS_FLIPS) * len(lands), name, deps)


def _pass_wait(handle, after, name):
    def finish(_, land, send_sems, recv_sems):
        for w in range(len(handle["thru"])):
            for j in range(len(PASS_FLIPS)):
                _pass_copy(land, send_sems, recv_sems, w, j, False).wait_send()
                _pass_copy(land, send_sems, recv_sems, w, j, True).wait_recv()

    return _split_wait(finish, handle, after, name)[1]


def _gather_pass(lands, name):
    n = len(lands)
    n_p = len(PASS_FLIPS)

    def body(*refs):
        land = refs[n:2 * n]
        send_sems, recv_sems = refs[2 * n:]
        me = _my_coords()
        sibling = _flip(me, 1)
        sent = []
        for w in range(n):
            for j, k in enumerate(PASS_FLIPS):
                blk = land[w].at[_lin(_flip(me, k))]
                cp = pltpu.make_async_remote_copy(src_ref=blk, dst_ref=blk, send_sem=send_sems.at[n_p * w + j], recv_sem=recv_sems.at[n_p * w + j],
                                                  device_id=sibling, device_id_type=MESH)
                cp.start()
                sent.append(cp)
        for w in range(n):
            for j, k in enumerate(PASS_FLIPS):
                blk = land[w].at[_lin(_flip(me, k + 1))]
                pltpu.make_async_remote_copy(src_ref=blk, dst_ref=blk, send_sem=send_sems.at[n_p * w + j], recv_sem=recv_sems.at[n_p * w + j],
                                             device_id=sibling, device_id_type=MESH).wait_recv()
        for cp in sent:
            cp.wait_send()

    return pl.pallas_call(body, in_specs=[ANY_SPEC] * n, out_specs=[ANY_SPEC] * n, out_shape=[_sds(t.shape, t.dtype) for t in lands],
                          input_output_aliases={i: i for i in range(n)},
                          scratch_shapes=[pltpu.SemaphoreType.DMA((n_p * n,)), pltpu.SemaphoreType.DMA((n_p * n,))], name=name)(*lands)


CHIP_FLIPS = (0, 2, 4, 6)


def _pair_copy(src, land, send_sems, recv_sems, w, j):
    me = _my_coords()
    q = len(CHIP_FLIPS) * w + j
    return pltpu.make_async_remote_copy(src_ref=src[w].at[_lin(_flip(me, CHIP_FLIPS[j] + 1))], dst_ref=land[w].at[j], send_sem=send_sems.at[q],
                                        recv_sem=recv_sems.at[q], device_id=_flip(me, 1), device_id_type=MESH)


def _pair_exchange(grads, name):
    n = len(grads)

    def body(*refs):
        src, land = refs[:n], refs[n:2 * n]
        send_sems, recv_sems = refs[2 * n:]
        sent = [_pair_copy(src, land, send_sems, recv_sems, w, j) for w in range(n) for j in range(len(CHIP_FLIPS))]
        for cp in sent:
            cp.start()
        for cp in sent:
            cp.wait_recv()
        for cp in sent:
            cp.wait_send()

    outs = pl.pallas_call(body, in_specs=[ANY_SPEC] * n, out_specs=[ANY_SPEC] * n,
                          out_shape=[_sds((len(CHIP_FLIPS),) + g.shape[1:], g.dtype) for g in grads],
                          scratch_shapes=[pltpu.SemaphoreType.DMA((len(CHIP_FLIPS) * n,))] * 2, name=name)(*grads)
    return list(outs)


def _pair_start(grads, name, deps=()):
    n = len(grads)
    lands = [lax.empty((len(CHIP_FLIPS),) + g.shape[1:], g.dtype) for g in grads]

    def issue(src, land, send_sems, recv_sems):
        for w in range(n):
            for j in range(len(CHIP_FLIPS)):
                _pair_copy(src, land, send_sems, recv_sems, w, j).start()

    return _split_start(issue, grads, lands, len(CHIP_FLIPS) * n, name, deps)


def _pair_wait(handle, after, name):
    n = handle["n"]

    def finish(src, land, send_sems, recv_sems):
        for w in range(n):
            for j in range(len(CHIP_FLIPS)):
                cp = _pair_copy(src, land, send_sems, recv_sems, w, j)
                cp.wait_send()
                cp.wait_recv()

    return _split_wait(finish, handle, after, name)


def _pair_add(grad, theirs, name):
    p, r, c = theirs.shape
    tr = _row_tile(r, c, WIDE_TILE)
    me = _my_coords()
    ids = jnp.stack([_lin(_flip(me, k)) for k in CHIP_FLIPS]).astype(jnp.int32)

    def body(ids_ref, a_ref, b_ref, o_ref):
        o_ref[...] = (a_ref[...].astype(F32) + b_ref[...].astype(F32)).astype(o_ref.dtype)

    blk = pl.BlockSpec((None, tr, c), lambda j, i, ids_ref: (j, i, 0))
    return pl.pallas_call(
        body, out_shape=_sds((p, r, c), theirs.dtype), compiler_params=_params(2), name=name,
        grid_spec=pltpu.PrefetchScalarGridSpec(
            num_scalar_prefetch=1, grid=(p, r // tr),
            in_specs=[pl.BlockSpec((None, tr, c), lambda j, i, ids_ref: (ids_ref[j], i, 0)), blk], out_specs=blk))(ids, grad, theirs)


def _chips_start(parts, name, deps=()):
    n = len(parts)
    n_c = len(CHIP_FLIPS) - 1
    lands = [lax.empty((n_c,) + t.shape[1:], t.dtype) for t in parts]

    def issue(src, land, send_sems, recv_sems):
        me = _my_coords()
        for w in range(n):
            for j in range(1, n_c + 1):
                q = n_c * w + j - 1
                pltpu.make_async_remote_copy(src_ref=src[w].at[j], dst_ref=land[w].at[j - 1], send_sem=send_sems.at[q], recv_sem=recv_sems.at[q],
                                             device_id=_flip(me, CHIP_FLIPS[j]), device_id_type=MESH).start()

    return _split_start(issue, parts, lands, n_c * n, name, deps)


def _chips_wait(handle, after, name):
    n = handle["n"]
    n_c = len(CHIP_FLIPS) - 1

    def finish(src, land, send_sems, recv_sems):
        me = _my_coords()
        for w in range(n):
            for j in range(1, n_c + 1):
                q = n_c * w + j - 1
                cp = pltpu.make_async_remote_copy(src_ref=src[w].at[j], dst_ref=land[w].at[j - 1], send_sem=send_sems.at[q], recv_sem=recv_sems.at[q],
                                                  device_id=_flip(me, CHIP_FLIPS[j]), device_id_type=MESH)
                cp.wait_send()
                cp.wait_recv()

    return _split_wait(finish, handle, after, name)


def _after(t, *tokens):
    for tok in tokens:
        t = t + tok[0:1, 0:1]
    return t


def _rope_tables(positions):
    half = ROT // 2
    inv_freq = ROPE_THETA ** (-jnp.arange(0, ROT, 2, dtype=F32) / ROT)
    ang = positions.astype(F32).reshape(-1, 1) * inv_freq
    cos, sin = jnp.cos(ang), jnp.sin(ang)
    s = ang.shape[0]
    pad = jnp.zeros((s, HEAD_DIM - ROT), F32)
    zero = jnp.zeros((s, half), F32)
    two = lambda t: jnp.concatenate([t, t], axis=1)
    return (two(jnp.concatenate([cos, cos, pad + 1.0], axis=1)), two(jnp.concatenate([-sin, zero, pad], axis=1)),
            two(jnp.concatenate([zero, sin, pad], axis=1)))


def _local_step(x, tgt, tabs, mod, sinks_pad, hl, hg_norm, g_pre_mix, g_post_mix, g_pre_ffn, g_post_ffn, weights, prefetch, scatter, scatter_on):
    s = x.shape[0]
    h1 = _pre_fwd(x, g_pre_mix, mod, 1, 0, "pre_mix_fwd")
    (w_in_a,) = weights("in_a", h1)
    proj = _mm_nt(h1[:, :D // 2], w_in_a, 256, IN_COLS // 2, D // 2, F32, "proj_mm_a")
    (w_in_b,) = weights("in_b", proj)
    proj = _mm_nt(h1[:, D // 2:], w_in_b, 256, IN_COLS // 2, D // 2, F32, "proj_mm_b", add=proj)
    att = _attn_fwd(proj, tabs, _after(sinks_pad, prefetch("mix", proj)))
    o_raw, states = _hgrn_fwd(proj, hl)
    ohg = _hgout_fwd(o_raw, proj, hg_norm)
    w_attn_dm, w_hgrn_dm, w_out = weights("mix", ohg)
    natural = lambda w_dm: w_dm.transpose(1, 0, 2).reshape(w_dm.shape[1], D)
    pieces = lambda g: g.reshape(g.shape[0], N_DEV, D // N_DEV).transpose(1, 0, 2)
    w_attn, w_hgrn = natural(w_attn_dm), natural(w_hgrn_dm)
    y_a = _mm_nn(att, w_attn, s, 512, ATT_W, F32, "attn_proj_mm")
    y_h = _mm_nn(ohg, w_hgrn, s, 512, HG_W, F32, "hgrn_proj_mm")
    merged = _merge_fwd(y_a, y_h, proj)
    y = _mm_nn(merged, w_out, s, 512, D, F32, "out_mm")
    x1 = _post_fwd(x, y, g_post_mix, mod, 2, "post_mix_fwd")
    h2 = _pre_fwd(x1, g_pre_ffn, mod, 4, 3, "pre_ffn_fwd")
    (w_ffn_in_dm,) = weights("ffn_in", h2)
    gu = _mm_nn_dm(h2, w_ffn_in_dm, s // 2, F32, "ffn_in_mm")
    act = _swiglu_fwd(gu, deps=[prefetch("ffn_out", gu)])
    (w_ffn_out,) = weights("ffn_out", act)
    y2 = _mm_nn(act, w_ffn_out, 512, 512, FFN, F32, "ffn_out_mm")
    err, loss, dy2, d_gate2, dg_post_ffn = _post_loss_bwd(x1, y2, g_post_ffn, mod, 5, tgt, "post_ffn_loss_bwd")
    gw_ffn_out = _mm_tn(act, dy2, 512, D, BF16, "ffn_out_dw")
    t_pair = scatter([gw_ffn_out.reshape(N_DEV, FFN // N_DEV, D)], "ffn_out")
    d_act = _mm_nt(dy2, w_ffn_out, s, 512, D, F32, "ffn_out_dx", deps=[t_pair])
    dgu = _swiglu_bwd(d_act, gu)
    t_out = scatter_on("ffn_out", dgu)
    gw_ffn_in = _mm_tn_dm(h2, dgu, 1024, BF16, "ffn_in_dw")
    t_pair = scatter([gw_ffn_in], "ffn_in")
    dh2 = _mm_nt_dm(dgu, w_ffn_in_dm, s, 1024, F32, "ffn_in_dx", deps=[t_pair])
    mod = _after(mod, t_out)
    dx1, d_shift2, d_scale2, dg_pre_ffn = _pre_bwd(dh2, x1, err, g_pre_ffn, mod, 4, "pre_ffn_bwd")
    dy, d_gate1, dg_post_mix = _post_bwd(dx1, y, g_post_mix, mod, 2, "post_mix_bwd")
    t_in = scatter_on("ffn_in", dy)
    d_merged = _mm_nt(dy, w_out, s, 512, D, F32, "out_dx")
    gw_out = _mm_tn(merged, dy, 512, D, BF16, "out_dw")
    dy_a, dy_h, d_gate_a, d_gate_h = _merge_bwd(d_merged, y_a, y_h, proj)
    gw_attn = pieces(_mm_tn(att, dy_a, 512, D, BF16, "attn_proj_dw"))
    gw_hgrn = pieces(_mm_tn(ohg, dy_h, 512, D, BF16, "hgrn_proj_dw"))
    t_pair = scatter([gw_attn, gw_hgrn, gw_out.reshape(N_DEV, D // N_DEV, D)], "mix")
    d_att = _mm_nt(dy_a, w_attn, s, 512, D, F32, "attn_proj_dx")
    d_ohg = _mm_nt(dy_h, w_hgrn, s, 512, D, F32, "hgrn_proj_dx", deps=[t_pair])
    d_o, d_gh, d_hg_norm = _hgout_bwd(d_ohg, o_raw, proj, _after(hg_norm, t_in))
    d_qh, d_fh, d_ih, d_hl = _hgrn_bwd(proj, hl, states, d_o)
    t_mix = scatter_on("mix", d_qh)
    d_qa, d_ka, d_va, d_sinks = _attn_bwd(proj, tabs, _after(sinks_pad, t_mix), d_att)
    d_proj = jnp.concatenate([d_qa, d_ka.astype(BF16), d_va.astype(BF16), d_qh, d_fh, d_ih, d_gh, d_gate_a, d_gate_h], axis=1)
    dh1 = jnp.concatenate([_mm_nn(d_proj, w_half, s // 2, 512, IN_COLS // 2, F32, "proj_dx_" + tag)
                           for tag, w_half in (("a", w_in_a), ("b", w_in_b))], axis=1)
    grad_x, d_shift1, d_scale1, dg_pre_mix = _pre_bwd(dh1, x, dx1, g_pre_mix, mod, 1, "pre_mix_bwd")
    d_mod = jnp.concatenate([d_shift1, d_scale1, d_gate1, d_shift2, d_scale2, d_gate2], axis=1)
    small = [d_mod, dg_pre_mix, dg_post_mix, dg_pre_ffn, dg_post_ffn, d_hl.reshape(1, 2 * HG_W), d_hg_norm, d_sinks]
    return loss, grad_x, small, h1, d_proj


def kernel(x, c, positions, w_ada, b_ada, g_pre_mix, g_post_mix, g_pre_ffn, g_post_ffn, w_in, attn_sinks, w_attn_proj, hg_lower_bounds, hg_norm, w_hgrn_proj, w_out, w_ffn_in, w_ffn_out, loss_target, m_w_ada, m_b_ada, m_g_pre_mix, m_g_post_mix, m_g_pre_ffn, m_g_post_ffn, m_w_in, m_attn_sinks, m_w_attn_proj, m_hg_lower_bounds, m_hg_norm, m_w_hgrn_proj, m_w_out, m_w_ffn_in, m_w_ffn_out, v_w_ada, v_b_ada, v_g_pre_mix, v_g_post_mix, v_g_pre_ffn, v_g_post_ffn, v_w_in, v_attn_sinks, v_w_attn_proj, v_hg_lower_bounds, v_hg_norm, v_w_hgrn_proj, v_w_out, v_w_ffn_in, v_w_ffn_out):
    my_id = _lin(_my_coords())
    s = x.shape[1]
    n_ada = w_ada.shape[2]

    c_all = _exchange_small(c.reshape(1, 1, D), True, "gather_c").reshape(N_DEV, D)
    b_cols = lax.dynamic_slice(b_ada, (0, my_id * n_ada), (1, n_ada))
    mod_part = _mod_part(c_all, w_ada[0], b_cols)
    mod = _exchange_small(mod_part.reshape(N_DEV, 1, n_ada), False, "scatter_mod").reshape(1, N_MOD * D)
    groups = {"in_a": [w_in[0].T[:, :D // 2]], "in_b": [w_in[0].T[:, D // 2:]], "mix": [w_attn_proj[0], w_hgrn_proj[0], w_out[0]],
              "ffn_in": [w_ffn_in[0]], "ffn_out": [w_ffn_out[0]]}

    def start(group, dep):
        shards, dep = lax.optimization_barrier((groups[group], dep))
        return _gather_start([t.astype(BF16) for t in shards], "gather_start_" + group, deps=[dep])

    gathers = {"in_a": start("in_a", mod)}
    gathers["in_b"] = start("in_b", gathers["in_a"]["token"])
    gathers["mix"] = start("mix", gathers["in_b"]["token"])
    gathers["ffn_in"] = start("ffn_in", gathers["mix"]["token"])
    gathers["ffn_out"] = start("ffn_out", gathers["ffn_in"]["token"])

    passes = {}

    def prefetch(group, after):
        lands = _gather_wait(gathers[group], [after], "gather_wait_" + group)
        passes[group] = _pass_start(lands, "gather_pass_start_" + group)
        return passes[group]["token"]

    def weights(group, after):
        if group in passes:
            lands = _pass_wait(passes[group], [after], "gather_pass_wait_" + group)
        else:
            after = [after, gathers["ffn_out"]["token"]]
            lands = _gather_pass(_gather_wait(gathers[group], after, "gather_wait_" + group), "gather_pass_" + group)
        if group in ("in_a", "in_b"):
            return (lands[0].reshape(IN_COLS, D // 2),)
        if group == "mix":
            return lands[0], lands[1], lands[2].reshape(D, D)
        return (lands[0],) if group == "ffn_in" else (lands[0].reshape(FFN, D),)

    pairs, scatters = {}, {}

    def scatter(grads, group):
        pairs[group] = _pair_start(grads, "scatter_pair_" + group)
        return pairs[group]["token"]

    def scatter_on(group, after):
        if group in pairs:
            local, theirs = _pair_wait(pairs[group], [after], "scatter_pair_wait_" + group)
        else:
            local, theirs = after, _pair_exchange(after, "scatter_pair_" + group)
        parts = [_pair_add(g, t, "scatter_pair_add_%s_%d" % (group, k)) for k, (g, t) in enumerate(zip(local, theirs))]
        scatters[group] = _chips_start(parts, "scatter_start_" + group)
        return scatters[group]["token"]

    sinks_pad = jnp.pad(attn_sinks, ((0, 0), (0, LANE - ATT_HEADS)))
    loss, grad_x, small, h1, d_proj = _local_step(
        x[0], loss_target[0], _rope_tables(positions), mod, sinks_pad, hg_lower_bounds, hg_norm, g_pre_mix, g_post_mix, g_pre_ffn, g_post_ffn,
        weights, prefetch, scatter, scatter_on)
    loss = lax.psum(loss[0, 0], ("x", "y", "c"))

    sizes = [t.shape[1] for t in small]
    parts = _exchange_small(jnp.concatenate(small, axis=1).reshape(1, 1, sum(sizes)), True, "gather_small_grads")
    dep = parts
    for half, cols in (("in_a", slice(0, D // 2)), ("in_b", slice(D // 2, D))):
        gw_half = _mm_tn(d_proj, h1[:, cols], 256, D // 2, BF16, "proj_dw_" + half, deps=[dep])
        dep = scatter_on(half, [gw_half.reshape(N_DEV, IN_COLS // N_DEV, D // 2)])
    offs = [sum(sizes[:k]) for k in range(len(sizes))]
    piece = lambda k, n=None: parts[:, :, offs[k]:offs[k] + (sizes[k] if n is None else n)]
    small_w = [(piece(0), b_ada, m_b_ada, v_b_ada), (piece(1), g_pre_mix, m_g_pre_mix, v_g_pre_mix),
               (piece(2), g_post_mix, m_g_post_mix, v_g_post_mix), (piece(3), g_pre_ffn, m_g_pre_ffn, v_g_pre_ffn),
               (piece(4), g_post_ffn, m_g_post_ffn, v_g_post_ffn),
               (piece(5).reshape(N_DEV, 2, HG_W), hg_lower_bounds, m_hg_lower_bounds, v_hg_lower_bounds),
               (piece(6), hg_norm, m_hg_norm, v_hg_norm), (piece(7, ATT_HEADS), attn_sinks, m_attn_sinks, v_attn_sinks)]
    names = ["b_ada", "g_pre_mix", "g_post_mix", "g_pre_ffn", "g_post_ffn", "hg_lower_bounds", "hg_norm", "attn_sinks"]
    res = {n: _adamw(p, w, m, v, "adamw_" + n) for n, (p, w, m, v) in zip(names, small_w)}

    dmod_cols = lax.dynamic_slice(parts.reshape(N_DEV, -1), (0, my_id * n_ada), (N_DEV, n_ada))
    g_w_ada = _grad_w_ada(c_all.T, dmod_cols)
    res["w_ada"] = [g_w_ada] + list(_adamw(g_w_ada[None], w_ada[0], m_w_ada[0], v_w_ada[0], "adamw_w_ada", emit_grad=False))

    big = {"ffn_out": [("w_ffn_out", w_ffn_out, m_w_ffn_out, v_w_ffn_out)], "ffn_in": [("w_ffn_in", w_ffn_in, m_w_ffn_in, v_w_ffn_in)],
           "mix": [("w_attn_proj", w_attn_proj, m_w_attn_proj, v_w_attn_proj), ("w_hgrn_proj", w_hgrn_proj, m_w_hgrn_proj, v_w_hgrn_proj),
                   ("w_out", w_out, m_w_out, v_w_out)]}
    after = [scatters["in_b"]["token"]]
    for group, members in big.items():
        local, lands = _chips_wait(scatters[group], after, "scatter_wait_" + group)
        for (n, w, m, v), mine, land in zip(members, local, lands):
            res[n] = _adamw(land, w[0], m[0], v[0], "adamw_" + n, own=mine[0])
            after = after + [res[n][1]]
    after = [res[n][1] for n in res]
    halves = [_chips_wait(scatters[half], after, "scatter_wait_" + half) for half in ("in_a", "in_b")]
    own = jnp.concatenate([local[0][0] for local, _ in halves], axis=1)
    land = jnp.concatenate([lands[0] for _, lands in halves], axis=2)
    res["w_in"] = [t.T for t in _adamw(land, w_in[0].T, m_w_in[0].T, v_w_in[0].T, "adamw_w_in", own=own)]

    order = ["w_ada", "b_ada", "g_pre_mix", "g_post_mix", "g_pre_ffn", "g_post_ffn", "w_in", "attn_sinks", "w_attn_proj",
             "hg_lower_bounds", "hg_norm", "w_hgrn_proj", "w_out", "w_ffn_in", "w_ffn_out"]
    lead = {"w_ada", "w_in", "w_attn_proj", "w_hgrn_proj", "w_out", "w_ffn_in", "w_ffn_out"}
    outs = [loss, grad_x[None]]
    for k in range(4):
        outs += [res[n][k][None] if n in lead else res[n][k] for n in order]
    return tuple(outs)
```

```python
import functools

import jax
import jax.numpy as jnp
from jax import lax
from jax.experimental import pallas as pl
from jax.experimental.pallas import tpu as pltpu

F32 = jnp.float32
BF16 = jnp.bfloat16

N_DEV = 8
D = 2048
ATT_HEADS = 16
KV_HEADS = 2
HEAD_DIM = 64
GROUP = ATT_HEADS // KV_HEADS
ATT_W = ATT_HEADS * HEAD_DIM
BLK = 128
ROT = HEAD_DIM // 4
ROPE_THETA = 500000.0
HG_HEADS = 8
HG_K = 128
HG_W = HG_HEADS * HG_K
CHUNK = 64
SUB = 16
FFN = 5632
N_MOD = 6
EPS = 1e-6
LANE = 128
Q_A, K_A, V_A, Q_H, F_H, I_H, G_H, GT_A, GT_H, IN_COLS = 0, 1024, 1152, 1280, 2304, 3328, 4352, 5376, 7424, 9472

ADAM_LR, ADAM_B1, ADAM_B2, ADAM_EPS, ADAM_WD, ADAM_STEP = 0.001, 0.9, 0.999, 1e-08, 0.01, 10

TR = 256
HG_TB = 512
VMEM_BIG = 56 << 20
MESH = pl.DeviceIdType.MESH


def _sds(shape, dtype):
    return jax.ShapeDtypeStruct(shape, dtype)


def _params(n_axes, vmem=None):
    return pltpu.CompilerParams(dimension_semantics=("arbitrary",) * n_axes, vmem_limit_bytes=vmem)


def _sig(t):
    return 1.0 / (1.0 + jnp.exp(-t))


def _dot(a, b, dims):
    return lax.dot_general(a, b, (dims, ((), ())), preferred_element_type=F32)


NN = ((1,), (0,))
NT = ((1,), (1,))
TN = ((0,), (0,))


def _matmul(a, b, a_spec, b_spec, o_spec, out_shape, grid, dims, acc_shape, name, deps=(), add=None):
    nk = grid[2]
    nd = len(deps)
    extra = [] if add is None else [add]

    def body(a_ref, b_ref, *rest):
        o_ref, scratch = rest[nd + len(extra)], rest[nd + len(extra) + 1:]
        part = _dot(a_ref[...], b_ref[...], dims)
        if add is not None:
            assert nk == 1
            part = part + rest[nd][...]
        if nk == 1:
            o_ref[...] = part.astype(o_ref.dtype)
        else:
            acc = scratch[0]
            k = pl.program_id(2)

            @pl.when(k == 0)
            def _():
                acc[...] = part

            @pl.when(k > 0)
            def _():
                acc[...] += part

            @pl.when(k == nk - 1)
            def _():
                o_ref[...] = acc[...].astype(o_ref.dtype)

    return pl.pallas_call(
        body, grid=grid, in_specs=[a_spec, b_spec] + [pl.BlockSpec(memory_space=pl.ANY)] * nd + [o_spec] * len(extra),
        out_specs=o_spec, out_shape=out_shape, scratch_shapes=[pltpu.VMEM(acc_shape, F32)] if nk > 1 else [],
        input_output_aliases={2 + nd: 0} if extra else {},
        compiler_params=_params(3, VMEM_BIG), name=name)(a, b, *deps, *extra)


def _mm_nn(a, b, tm, tn, tk, out_dtype, name):
    m, k = a.shape
    n = b.shape[1]
    return _matmul(a, b, pl.BlockSpec((tm, tk), lambda j, i, kk: (i, kk)), pl.BlockSpec((tk, tn), lambda j, i, kk: (kk, j)),
                   pl.BlockSpec((tm, tn), lambda j, i, kk: (i, j)), _sds((m, n), out_dtype),
                   (n // tn, m // tm, k // tk), NN, (tm, tn), name)


def _mm_nn_dm(a, b, tm, out_dtype, name):
    m, k = a.shape
    n = b.shape[2]
    return _matmul(a, b, pl.BlockSpec((tm, k), lambda j, i, kk: (i, 0)), pl.BlockSpec((None, k, n), lambda j, i, kk: (j, 0, 0)),
                   pl.BlockSpec((tm, n), lambda j, i, kk: (i, j)), _sds((m, N_DEV * n), out_dtype),
                   (N_DEV, m // tm, 1), NN, (tm, n), name)


def _mm_nt(a, b, tm, tn, tk, out_dtype, name, deps=(), add=None):
    m, k = a.shape
    n = b.shape[0]
    return _matmul(a, b, pl.BlockSpec((tm, tk), lambda j, i, kk: (i, kk)), pl.BlockSpec((tn, tk), lambda j, i, kk: (j, kk)),
                   pl.BlockSpec((tm, tn), lambda j, i, kk: (i, j)), _sds((m, n), out_dtype),
                   (n // tn, m // tm, k // tk), NT, (tm, tn), name, deps, add)


def _mm_nt_dm(a, b, tm, tn, out_dtype, name, deps=()):
    m = a.shape[0]
    n_out, n = b.shape[1], b.shape[2]
    return _matmul(a, b, pl.BlockSpec((tm, n), lambda j, i, kk: (i, kk)), pl.BlockSpec((None, tn, n), lambda j, i, kk: (kk, j, 0)),
                   pl.BlockSpec((tm, tn), lambda j, i, kk: (i, j)), _sds((m, n_out), out_dtype),
                   (n_out // tn, m // tm, N_DEV), NT, (tm, tn), name, deps)


def _mm_tn(a, b, tm, tn, out_dtype, name, deps=()):
    s, m = a.shape
    n = b.shape[1]
    return _matmul(a, b, pl.BlockSpec((s, tm), lambda j, i, kk: (0, i)), pl.BlockSpec((s, tn), lambda j, i, kk: (0, j)),
                   pl.BlockSpec((tm, tn), lambda j, i, kk: (i, j)), _sds((m, n), out_dtype),
                   (n // tn, m // tm, 1), TN, (tm, tn), name, deps)


def _mm_tn_dm(a, b, tm, out_dtype, name):
    s, m = a.shape
    n = b.shape[1] // N_DEV
    return _matmul(a, b, pl.BlockSpec((s, tm), lambda j, i, kk: (0, i)), pl.BlockSpec((s, n), lambda j, i, kk: (0, j)),
                   pl.BlockSpec((None, tm, n), lambda j, i, kk: (j, i, 0)), _sds((N_DEV, m, n), out_dtype),
                   (N_DEV, m // tm, 1), TN, (tm, n), name)


def _row_spec():
    return pl.BlockSpec((TR, D), lambda i: (i, 0))


def _vec_spec(k=0):
    return pl.BlockSpec((1, D), lambda i: (0, k))


def _acc_rows(ref, first, val):
    @pl.when(first)
    def _():
        ref[...] = val

    @pl.when(jnp.logical_not(first))
    def _():
        ref[...] += val


def _pre_fwd(x, g, mod, k_scale, k_shift, name):
    s = x.shape[0]

    def body(x_ref, g_ref, sc_ref, sh_ref, h_ref):
        xv = x_ref[...]
        r = lax.rsqrt(jnp.mean(xv * xv, axis=-1, keepdims=True) + EPS)
        n = xv * r * g_ref[...]
        h_ref[...] = (n * (1.0 + sc_ref[...]) + sh_ref[...]).astype(h_ref.dtype)

    return pl.pallas_call(body, grid=(s // TR,), in_specs=[_row_spec(), _vec_spec(), _vec_spec(k_scale), _vec_spec(k_shift)],
                          out_specs=_row_spec(), out_shape=_sds((s, D), BF16), compiler_params=_params(1), name=name)(x, g, mod, mod)


def _post_fwd(x, y, g, mod, k_gate, name):
    s = x.shape[0]

    def body(x_ref, y_ref, g_ref, gt_ref, o_ref):
        yv = y_ref[...]
        r = lax.rsqrt(jnp.mean(yv * yv, axis=-1, keepdims=True) + EPS)
        o_ref[...] = x_ref[...] + gt_ref[...] * (yv * r * g_ref[...])

    return pl.pallas_call(body, grid=(s // TR,), in_specs=[_row_spec(), _row_spec(), _vec_spec(), _vec_spec(k_gate)],
                          out_specs=_row_spec(), out_shape=_sds((s, D), F32), compiler_params=_params(1), name=name)(x, y, g, mod)


def _post_loss_bwd(x, y, g, mod, k_gate, tgt, name):
    s = x.shape[0]

    def body(x_ref, y_ref, g_ref, gt_ref, t_ref, e_ref, loss_ref, dy_ref, dgt_ref, dg_ref):
        first = pl.program_id(0) == 0
        yv, gv, gate = y_ref[...], g_ref[...], gt_ref[...]
        r = lax.rsqrt(jnp.mean(yv * yv, axis=-1, keepdims=True) + EPS)
        yh = yv * r
        err = x_ref[...] + gate * (yh * gv) - t_ref[...]
        e = err * (1.0 / D)
        e_ref[...] = e
        _acc_rows(loss_ref, first, 0.5 * jnp.sum(jnp.mean(err * err, axis=-1, keepdims=True), axis=0, keepdims=True))
        dn = e * gate
        dgn = dn * gv
        dy_ref[...] = (r * (dgn - yh * jnp.mean(dgn * yh, axis=-1, keepdims=True))).astype(dy_ref.dtype)
        _acc_rows(dgt_ref, first, jnp.sum(e * (yh * gv), axis=0, keepdims=True))
        _acc_rows(dg_ref, first, jnp.sum(dn * yh, axis=0, keepdims=True))

    return pl.pallas_call(body, grid=(s // TR,),
                          in_specs=[_row_spec(), _row_spec(), _vec_spec(), _vec_spec(k_gate), _row_spec()],
                          out_specs=[_row_spec(), pl.BlockSpec((1, 1), lambda i: (0, 0)), _row_spec(), _vec_spec(), _vec_spec()],
                          out_shape=[_sds((s, D), F32), _sds((1, 1), F32), _sds((s, D), BF16), _sds((1, D), F32), _sds((1, D), F32)],
                          compiler_params=_params(1), name=name)(x, y, g, mod, tgt)


def _pre_bwd(dh, x, res, g, mod, k_scale, name):
    s = x.shape[0]

    def body(dh_ref, x_ref, res_ref, g_ref, sc_ref, dx_ref, dsh_ref, dsc_ref, dg_ref):
        first = pl.program_id(0) == 0
        xv, dh_v, gv = x_ref[...], dh_ref[...], g_ref[...]
        r = lax.rsqrt(jnp.mean(xv * xv, axis=-1, keepdims=True) + EPS)
        xh = xv * r
        dn = dh_v * (1.0 + sc_ref[...])
        dgn = dn * gv
        dx_ref[...] = res_ref[...] + r * (dgn - xh * jnp.mean(dgn * xh, axis=-1, keepdims=True))
        _acc_rows(dsh_ref, first, jnp.sum(dh_v, axis=0, keepdims=True))
        _acc_rows(dsc_ref, first, jnp.sum(dh_v * (xh * gv), axis=0, keepdims=True))
        _acc_rows(dg_ref, first, jnp.sum(dn * xh, axis=0, keepdims=True))

    return pl.pallas_call(body, grid=(s // TR,),
                          in_specs=[_row_spec(), _row_spec(), _row_spec(), _vec_spec(), _vec_spec(k_scale)],
                          out_specs=[_row_spec(), _vec_spec(), _vec_spec(), _vec_spec()],
                          out_shape=[_sds((s, D), F32)] + [_sds((1, D), F32)] * 3,
                          compiler_params=_params(1), name=name)(dh, x, res, g, mod)


def _post_bwd(dx, y, g, mod, k_gate, name):
    s = y.shape[0]

    def body(dx_ref, y_ref, g_ref, gt_ref, dy_ref, dgt_ref, dg_ref):
        first = pl.program_id(0) == 0
        yv, dxv, gv = y_ref[...], dx_ref[...], g_ref[...]
        r = lax.rsqrt(jnp.mean(yv * yv, axis=-1, keepdims=True) + EPS)
        yh = yv * r
        dn = dxv * gt_ref[...]
        dgn = dn * gv
        dy_ref[...] = (r * (dgn - yh * jnp.mean(dgn * yh, axis=-1, keepdims=True))).astype(dy_ref.dtype)
        _acc_rows(dgt_ref, first, jnp.sum(dxv * (yh * gv), axis=0, keepdims=True))
        _acc_rows(dg_ref, first, jnp.sum(dn * yh, axis=0, keepdims=True))

    return pl.pallas_call(body, grid=(s // TR,), in_specs=[_row_spec(), _row_spec(), _vec_spec(), _vec_spec(k_gate)],
                          out_specs=[_row_spec(), _vec_spec(), _vec_spec()],
                          out_shape=[_sds((s, D), BF16), _sds((1, D), F32), _sds((1, D), F32)],
                          compiler_params=_params(1), name=name)(dx, y, g, mod)


SW_TN = 1408
SW_TR = 512
TALL = 1024


def _swiglu_fwd(gu, deps=()):
    s = gu.shape[0]
    nb = FFN // SW_TN

    def body(g_ref, u_ref, *rest):
        a_ref = rest[len(deps)]
        gv = g_ref[...]
        a_ref[...] = (gv * _sig(gv) * u_ref[...]).astype(a_ref.dtype)

    return pl.pallas_call(body, grid=(s // SW_TR, nb),
                          in_specs=[pl.BlockSpec((SW_TR, SW_TN), lambda i, j: (i, j)), pl.BlockSpec((SW_TR, SW_TN), lambda i, j: (i, j + nb))]
                          + [pl.BlockSpec(memory_space=pl.ANY)] * len(deps),
                          out_specs=pl.BlockSpec((SW_TR, SW_TN), lambda i, j: (i, j)), out_shape=_sds((s, FFN), BF16),
                          compiler_params=_params(2, 48 << 20), name="swiglu_fwd")(gu, gu, *deps)


def _swiglu_bwd(dact, gu):
    s = gu.shape[0]
    nb = FFN // SW_TN
    n_steps = (s // SW_TR) * nb

    def body(da_ref, g_ref, u_ref, o_ref, buf, sems):
        i, j = pl.program_id(0), pl.program_id(1)
        step = i * nb + j
        slot = step % 2

        def tiles(sl):
            rows = pl.ds(pl.multiple_of(i * SW_TR, SW_TR), SW_TR)
            return [pltpu.make_async_copy(buf.at[sl, h], o_ref.at[rows, pl.ds(pl.multiple_of((j + nb * h) * SW_TN, LANE), SW_TN)], sems.at[sl, h])
                    for h in range(2)]

        @pl.when(step >= 2)
        def _():
            for cp in tiles(slot):
                cp.wait()

        gv, da = g_ref[...], da_ref[...]
        sg = _sig(gv)
        buf[slot, 0] = (da * u_ref[...] * (sg * (1.0 + gv * (1.0 - sg)))).astype(buf.dtype)
        buf[slot, 1] = (da * (gv * sg)).astype(buf.dtype)
        for cp in tiles(slot):
            cp.start()

        @pl.when(step == n_steps - 1)
        def _():
            for cp in tiles(slot) + (tiles(1 - slot) if n_steps > 1 else []):
                cp.wait()

    blk = lambda f: pl.BlockSpec((SW_TR, SW_TN), f)
    return pl.pallas_call(body, grid=(s // SW_TR, nb),
                          in_specs=[blk(lambda i, j: (i, j)), blk(lambda i, j: (i, j)), blk(lambda i, j: (i, j + nb))],
                          out_specs=pl.BlockSpec(memory_space=pl.ANY), out_shape=_sds((s, 2 * FFN), BF16),
                          scratch_shapes=[pltpu.VMEM((2, 2, SW_TR, SW_TN), BF16), pltpu.SemaphoreType.DMA((2, 2))],
                          compiler_params=_params(2, 48 << 20), name="swiglu_bwd")(dact, gu, gu)


MG_TN = 256


def _merge_fwd(y_a, y_h, proj):
    s = y_a.shape[0]
    tn = MG_TN
    ba, bh = GT_A // tn, GT_H // tn

    def body(ya_ref, yh_ref, ga_ref, gh_ref, m_ref):
        m_ref[...] = (_sig(ga_ref[...]) * ya_ref[...] + _sig(gh_ref[...]) * yh_ref[...]).astype(m_ref.dtype)

    tr = min(s, TALL)
    blk = lambda f: pl.BlockSpec((tr, tn), f)
    return pl.pallas_call(body, grid=(s // tr, D // tn),
                          in_specs=[blk(lambda i, j: (i, j)), blk(lambda i, j: (i, j)), blk(lambda i, j: (i, j + ba)), blk(lambda i, j: (i, j + bh))],
                          out_specs=blk(lambda i, j: (i, j)), out_shape=_sds((s, D), BF16),
                          compiler_params=_params(2), name="merge_fwd")(y_a, y_h, proj, proj)


def _merge_bwd(dm, y_a, y_h, proj):
    s = y_a.shape[0]
    tn = MG_TN
    ba, bh = GT_A // tn, GT_H // tn

    def body(dm_ref, ya_ref, yh_ref, ga_ref, gh_ref, dya_ref, dyh_ref, dga_ref, dgh_ref):
        dmv = dm_ref[...]
        sa, sh = _sig(ga_ref[...]), _sig(gh_ref[...])
        dya_ref[...] = (dmv * sa).astype(BF16)
        dyh_ref[...] = (dmv * sh).astype(BF16)
        dga_ref[...] = (dmv * ya_ref[...] * (sa * (1.0 - sa))).astype(BF16)
        dgh_ref[...] = (dmv * yh_ref[...] * (sh * (1.0 - sh))).astype(BF16)

    tr = min(s, TALL)
    blk = lambda f: pl.BlockSpec((tr, tn), f)
    nat = blk(lambda i, j: (i, j))
    return pl.pallas_call(body, grid=(s // tr, D // tn),
                          in_specs=[nat, nat, nat, blk(lambda i, j: (i, j + ba)), blk(lambda i, j: (i, j + bh))],
                          out_specs=[nat] * 4, out_shape=[_sds((s, D), BF16)] * 4,
                          compiler_params=_params(2), name="merge_bwd")(dm, y_a, y_h, proj, proj)


def _hgout_fwd(o_raw, proj, hg_norm):
    s = o_raw.shape[0]
    bg = G_H // LANE

    def body(o_ref, g_ref, n_ref, out_ref):
        ov = o_ref[...]
        r = lax.rsqrt(jnp.mean(ov * ov, axis=-1, keepdims=True) + EPS)
        out_ref[...] = (ov * r * n_ref[...] * _sig(g_ref[...])).astype(out_ref.dtype)

    tr = min(s, TALL)
    blk = lambda f: pl.BlockSpec((tr, LANE), f)
    return pl.pallas_call(body, grid=(s // tr, HG_HEADS),
                          in_specs=[blk(lambda i, h: (i, h)), blk(lambda i, h: (i, h + bg)), pl.BlockSpec((1, LANE), lambda i, h: (0, 0))],
                          out_specs=blk(lambda i, h: (i, h)), out_shape=_sds((s, HG_W), BF16),
                          compiler_params=_params(2), name="hgout_fwd")(o_raw, proj, hg_norm)


def _hgout_bwd(d_out, o_raw, proj, hg_norm):
    s = o_raw.shape[0]
    bg = G_H // LANE

    def body(d_ref, o_ref, g_ref, n_ref, do_ref, dg_ref, dn_ref):
        first = jnp.logical_and(pl.program_id(0) == 0, pl.program_id(1) == 0)
        ov, dv, nv = o_ref[...], d_ref[...], n_ref[...]
        sg = _sig(g_ref[...])
        r = lax.rsqrt(jnp.mean(ov * ov, axis=-1, keepdims=True) + EPS)
        oh = ov * r
        d_on = dv * sg
        dg_ref[...] = (dv * (oh * nv) * (sg * (1.0 - sg))).astype(dg_ref.dtype)
        t = d_on * nv
        do_ref[...] = r * (t - oh * jnp.mean(t * oh, axis=-1, keepdims=True))
        _acc_rows(dn_ref, first, jnp.sum(d_on * oh, axis=0, keepdims=True))

    tr = min(s, TALL)
    blk = lambda f: pl.BlockSpec((tr, LANE), f)
    vec = pl.BlockSpec((1, LANE), lambda i, h: (0, 0))
    return pl.pallas_call(body, grid=(s // tr, HG_HEADS),
                          in_specs=[blk(lambda i, h: (i, h)), blk(lambda i, h: (i, h)), blk(lambda i, h: (i, h + bg)), vec],
                          out_specs=[blk(lambda i, h: (i, h)), blk(lambda i, h: (i, h)), vec],
                          out_shape=[_sds((s, HG_W), F32), _sds((s, HG_W), BF16), _sds((1, LANE), F32)],
                          compiler_params=_params(2), name="hgout_bwd")(d_out, o_raw, proj, hg_norm)


def _rope(t, cos, s_lo, s_hi):
    return t * cos + pltpu.roll(t, LANE - ROT // 2, 1) * s_lo + pltpu.roll(t, ROT // 2, 1) * s_hi


def _rope_wide(t, cos, s_lo, s_hi):
    return jnp.concatenate([_rope(t[:, k * LANE:(k + 1) * LANE], cos, s_lo, s_hi) for k in range(t.shape[1] // LANE)], axis=1)


def _attn_mask(has_prev):
    kj = lax.broadcasted_iota(jnp.int32, (2 * BLK, BLK), 0)
    qi = lax.broadcasted_iota(jnp.int32, (2 * BLK, BLK), 1)
    rel = BLK + qi - kj
    band = jnp.logical_and(rel >= 0, rel < BLK)
    return jnp.logical_and(band, jnp.logical_or(has_prev, kj >= BLK))


def _attn_specs():
    prev = lambda i: jnp.maximum(i - 1, 0)
    kb, vb = K_A // LANE, V_A // LANE
    blk = lambda f: pl.BlockSpec((BLK, LANE), f)
    tabs = [blk(lambda i: (i, 0))] * 3 + [blk(lambda i: (prev(i), 0))] * 3
    return [pl.BlockSpec((BLK, ATT_W), lambda i: (i, 0)), blk(lambda i: (i, kb)), blk(lambda i: (prev(i), kb)),
            blk(lambda i: (i, vb)), blk(lambda i: (prev(i), vb))] + tabs + [pl.BlockSpec((1, LANE), lambda i: (0, 0))]


def _attn_logits(qh, kg):
    return _dot(kg, qh, NT)


def _attn_probs(raw, mask, sk):
    logits = jnp.where(mask, raw * (HEAD_DIM ** -0.5), -jnp.inf)
    m = jnp.maximum(jnp.max(logits, axis=0, keepdims=True), sk)
    p = jnp.exp(logits - m)
    e_sink = jnp.exp(sk - m)
    inv = 1.0 / (jnp.sum(p, axis=0, keepdims=True) + e_sink)
    return p, inv, e_sink * inv


def _attn_fwd(proj, tabs, sinks):
    s = proj.shape[0]

    def body(q_ref, kc_ref, kp_ref, vc_ref, vp_ref, c0, l0, h0, c1, l1, h1, sk_ref, o_ref):
        i = pl.program_id(0)
        mask = _attn_mask(i > 0)
        q = _rope_wide(q_ref[...], c0[...], l0[...], h0[...]).astype(BF16)
        kk = jnp.concatenate([_rope(kp_ref[...], c1[...], l1[...], h1[...]), _rope(kc_ref[...], c0[...], l0[...], h0[...])], axis=0).astype(BF16)
        v_t = jnp.concatenate([vp_ref[...], vc_ref[...]], axis=0).T.astype(BF16)
        part = lambda t, h: t[:, h * HEAD_DIM:(h + 1) * HEAD_DIM]
        k_heads = [part(kk, g) for g in range(KV_HEADS)]

        def head(h):
            g = h // GROUP
            raw = _attn_logits(part(q, h), k_heads[g])
            yield
            p, inv, _ = _attn_probs(raw, mask, sk_ref[:, h:h + 1])
            yield
            out_t = _dot(v_t[g * HEAD_DIM:(g + 1) * HEAD_DIM], p.astype(BF16), NN)
            yield
            return out_t * inv

        o_ref[...] = jnp.concatenate(_interleave([head(h) for h in range(ATT_HEADS)]), axis=0).T.astype(o_ref.dtype)

    return pl.pallas_call(body, grid=(s // BLK,), in_specs=_attn_specs(),
                          out_specs=pl.BlockSpec((BLK, ATT_W), lambda i: (i, 0)), out_shape=_sds((s, ATT_W), BF16),
                          compiler_params=_params(1), name="attn_fwd")(proj, proj, proj, proj, proj, *tabs, *tabs, sinks)


def _attn_bwd(proj, tabs, sinks, d_att):
    s = proj.shape[0]

    def body(q_ref, kc_ref, kp_ref, vc_ref, vp_ref, c0, l0, h0, c1, l1, h1, sk_ref, do_ref, dq_ref, dk_ref, dv_ref, ds_ref):
        i = pl.program_id(0)

        @pl.when(i == 0)
        def _():
            dk_ref[...] = jnp.zeros_like(dk_ref)
            dv_ref[...] = jnp.zeros_like(dv_ref)
            ds_ref[...] = jnp.zeros_like(ds_ref)

        mask = _attn_mask(i > 0)
        q = _rope_wide(q_ref[...], c0[...], l0[...], h0[...]).astype(BF16)
        kk = jnp.concatenate([_rope(kp_ref[...], c1[...], l1[...], h1[...]), _rope(kc_ref[...], c0[...], l0[...], h0[...])], axis=0).astype(BF16)
        k_f32 = jnp.concatenate([_rope(kp_ref[...], c1[...], l1[...], h1[...]), _rope(kc_ref[...], c0[...], l0[...], h0[...])], axis=0)
        k_t = k_f32.T.astype(BF16)
        vv = jnp.concatenate([vp_ref[...], vc_ref[...]], axis=0).astype(BF16)
        d_o = do_ref[...].astype(BF16)
        lane = lax.broadcasted_iota(jnp.int32, (1, LANE), 1)
        part = lambda t, h: t[:, h * HEAD_DIM:(h + 1) * HEAD_DIM]
        k_heads = [part(kk, g) for g in range(KV_HEADS)]
        v_heads = [part(vv, g) for g in range(KV_HEADS)]

        def head(h):
            g = h // GROUP
            qh, doh = part(q, h), part(d_o, h)
            raw = _attn_logits(qh, k_heads[g])
            d_p = _dot(v_heads[g], doh, NT)
            yield
            p, inv, p_sink = _attn_probs(raw, mask, sk_ref[:, h:h + 1])
            prob = p * inv
            dv = _dot(prob.astype(BF16), doh, NN)
            yield
            dd = jnp.sum(prob * d_p, axis=0, keepdims=True)
            d_s = (prob * (d_p - dd)).astype(BF16)
            d_sink = jnp.where(lane == h, -jnp.sum(p_sink * dd, axis=1, keepdims=True), 0.0)
            dq_t = _dot(k_t[g * HEAD_DIM:(g + 1) * HEAD_DIM], d_s, NN)
            dk = _dot(d_s, qh, NN)
            yield
            return dq_t * (HEAD_DIM ** -0.5), dk * (HEAD_DIM ** -0.5), dv, d_sink

        per_head = _interleave([head(h) for h in range(ATT_HEADS)])
        dqs = [jnp.concatenate([t[0] for t in per_head], axis=0).T]
        group_sum = lambda k, g: functools.reduce(jnp.add, [t[k] for t in per_head[g * GROUP:(g + 1) * GROUP]])
        dks = [group_sum(1, g) for g in range(KV_HEADS)]
        dvs = [group_sum(2, g) for g in range(KV_HEADS)]
        d_sink = functools.reduce(jnp.add, [t[3] for t in per_head])
        dq_ref[...] = _rope_wide(jnp.concatenate(dqs, axis=1), c0[...], -l0[...], -h0[...]).astype(dq_ref.dtype)
        d_k = jnp.concatenate(dks, axis=1)
        d_v = jnp.concatenate(dvs, axis=1)
        cur = pl.ds(pl.multiple_of(i * BLK, BLK), BLK)
        prv = pl.ds(pl.multiple_of(jnp.maximum(i - 1, 0) * BLK, BLK), BLK)
        dk_ref[prv, :] += _rope(d_k[:BLK], c1[...], -l1[...], -h1[...])
        dk_ref[cur, :] += _rope(d_k[BLK:], c0[...], -l0[...], -h0[...])
        dv_ref[prv, :] += d_v[:BLK]
        dv_ref[cur, :] += d_v[BLK:]
        ds_ref[...] += d_sink

    full = pl.BlockSpec((s, LANE), lambda i: (0, 0))
    return pl.pallas_call(body, grid=(s // BLK,), in_specs=_attn_specs() + [pl.BlockSpec((BLK, ATT_W), lambda i: (i, 0))],
                          out_specs=[pl.BlockSpec((BLK, ATT_W), lambda i: (i, 0)), full, full, pl.BlockSpec((1, LANE), lambda i: (0, 0))],
                          out_shape=[_sds((s, ATT_W), BF16), _sds((s, LANE), F32), _sds((s, LANE), F32), _sds((1, LANE), F32)],
                          compiler_params=_params(1), name="attn_bwd")(proj, proj, proj, proj, proj, *tabs, *tabs, sinks, d_att)


def _tri_matmul(tri, t):
    hi = t.astype(BF16)
    r1 = t - hi.astype(F32)
    mid = r1.astype(BF16)
    lo = (r1 - mid.astype(F32)).astype(BF16)
    return _dot(tri, hi, NN) + _dot(tri, mid, NN) + _dot(tri, lo, NN)


def _lower_bound(hl):
    a, b = hl[0:1, :], hl[1:2, :]
    mx = jnp.maximum(a, b)
    ea, eb = jnp.exp(a - mx), jnp.exp(b - mx)
    return ea / (ea + eb)


def _hg_gates(q_raw, f_raw, lb, tri_lower):
    sg = _sig(f_raw)
    f = lb + (1.0 - lb) * sg
    sq = _sig(q_raw)
    b = _tri_matmul(tri_lower, jnp.log(f))
    return sg, f, 1.0 - f, sq, q_raw * sq, b


HG_PAIR_FWD = 8
HG_PAIR_BWD = 8


def _hg_specs(n_map, pair):
    blk = lambda off, p: pl.BlockSpec((HG_TB, LANE), lambda h, n: (n_map(n), off // LANE + pair * h + p))
    return [blk(off, p) for off in (Q_H, F_H, I_H) for p in range(pair)] + [pl.BlockSpec((2, pair * LANE), lambda h, n: (0, h))]


def _interleave(gens):
    out = [None] * len(gens)
    live = list(range(len(gens)))
    while live:
        for k in list(live):
            try:
                next(gens[k])
            except StopIteration as stop:
                out[k] = stop.value
                live.remove(k)
    return out


def _hg_spread():
    c = lax.broadcasted_iota(jnp.int32, (CHUNK, SUB * SUB), 0)
    l = lax.broadcasted_iota(jnp.int32, (CHUNK, SUB * SUB), 1)
    r = lax.broadcasted_iota(jnp.int32, (SUB, SUB * SUB), 0)
    lr = lax.broadcasted_iota(jnp.int32, (SUB, SUB * SUB), 1)
    shift = SUB.bit_length() - 1
    cols = [(c == lo + (l >> shift)).astype(BF16) for lo in range(0, CHUNK, SUB)]
    tile = [(c == lo + (l & (SUB - 1))).astype(BF16) for lo in range(0, CHUNK, SUB)]
    return cols, tile, (lr & (SUB - 1)) == r, (lr >> shift) == r


def _hg_intra(qs, kk, b, grad=None):
    lane = lax.broadcasted_iota(jnp.int32, (SUB, CHUNK), 1)
    row1 = lax.broadcasted_iota(jnp.int32, (SUB, 1), 0)
    kk_b = kk.astype(BF16)
    if grad is not None:
        d_a, d_at, (cols, tile, diag, block) = grad
    a_blocks, dq_blocks, dk_blocks, db_blocks = [], [], [], []
    dk_left = None
    for j in range(CHUNK // SUB):
        lo = j * SUB
        q_j, k_j, b_j = qs[lo:lo + SUB], kk[lo:lo + SUB], b[lo:lo + SUB]
        es = [jnp.where(row1 >= sx, jnp.exp(jnp.minimum(b_j - b_j[sx:sx + 1], 0.0)), 0.0) for sx in range(SUB)]
        pes = [q_j * e for e in es]
        pe = jnp.concatenate(pes, axis=0).astype(BF16)
        pairs = _dot(pe, kk_b, NT)
        yield
        a_j = jnp.zeros((SUB, CHUNK), F32)
        for sx in range(SUB):
            a_j = jnp.where(lane == lo + sx, pairs[sx * SUB:(sx + 1) * SUB], a_j)
        if grad is not None:
            da_j = d_a[lo:lo + SUB]
            ek = jnp.concatenate([e * k_j[sx:sx + 1] for sx, e in enumerate(es)], axis=0).astype(BF16)
            sel_t = jnp.where(diag, _dot(da_j.astype(BF16), cols[j], NN), 0.0).astype(BF16)
            sel_s = jnp.where(block, _dot(d_at[lo:lo + SUB].astype(BF16), tile[j], NN), 0.0).astype(BF16)
            pek = jnp.concatenate([p * k_j[sx:sx + 1] for sx, p in enumerate(pes)], axis=0).astype(BF16)
            yield
            dq_j = _dot(sel_t, ek, NN)
            dk_j = _dot(sel_s, pe, NN)
            db_j = _dot(sel_t, pek, NN) - _dot(sel_s, pek, NN)
            yield
        if j > 0:
            ref = b[lo - 1:lo]
            sc_q = jnp.exp(b_j - ref)
            sc_k = jnp.exp(jnp.minimum(ref - b, 0.0))
            qt = (q_j * sc_q).astype(BF16)
            kt = (kk * sc_k).astype(BF16)
            left = _dot(qt, kt, NT)
            yield
            a_j = a_j + jnp.where(lane < lo, left, 0.0)
            if grad is not None:
                da_left = jnp.where(lane < lo, da_j, 0.0).astype(BF16)
                dq_left = _dot(da_left, kt, NN) * sc_q
                dq_j = dq_j + dq_left
                db_j = db_j + q_j * dq_left
                t = _dot(da_left, qt, TN)
                yield
                t = t * sc_k
                dk_left = t if dk_left is None else dk_left + t
        a_blocks.append(a_j)
        if grad is not None:
            dq_blocks.append(dq_j)
            dk_blocks.append(dk_j)
            db_blocks.append(db_j)
    a = jnp.concatenate(a_blocks, axis=0)
    if grad is None:
        return a
    return a, jnp.concatenate(dq_blocks, axis=0), jnp.concatenate(dk_blocks, axis=0) + dk_left, jnp.concatenate(db_blocks, axis=0) - kk * dk_left


def _hgrn_fwd(proj, hl):
    s = proj.shape[0]
    n_chunk = HG_TB // CHUNK
    pair = HG_PAIR_FWD

    def body(*refs):
        q_refs, f_refs, i_refs = refs[:pair], refs[pair:2 * pair], refs[2 * pair:3 * pair]
        hl_ref, o_ref, st_out_ref, st_ref = refs[3 * pair:]

        @pl.when(pl.program_id(1) == 0)
        def _():
            st_ref[...] = jnp.zeros_like(st_ref)

        r_i = lax.broadcasted_iota(jnp.int32, (CHUNK, CHUNK), 0)
        c_i = lax.broadcasted_iota(jnp.int32, (CHUNK, CHUNK), 1)
        tri_lower = (r_i >= c_i).astype(BF16)

        def chunk(c, carry):
            rows = pl.ds(pl.multiple_of(c * CHUNK, CHUNK), CHUNK)
            def head(p):
                cols = slice(p * LANE, (p + 1) * LANE)
                lb = _lower_bound(hl_ref[:, cols])
                v = i_refs[p][rows, :].astype(BF16)
                _, _, kk, _, qs, b = _hg_gates(q_refs[p][rows, :], f_refs[p][rows, :], lb, tri_lower)
                yield
                st = st_ref[p]
                st_b = st.astype(BF16)
                st_out_ref[p, c] = st_b
                o_state = _dot((qs * jnp.exp(b)).astype(BF16), st_b, NT)
                b_last = b[CHUNK - 1:CHUNK, :]
                st_new = _dot(v, (kk * jnp.exp(b_last - b)).astype(BF16), TN)
                a = yield from _hg_intra(qs, kk, b)
                st_ref[p] = st * jnp.exp(b_last) + st_new
                o_ref[rows, cols] = o_state + _dot(a.astype(BF16), v, NN)

            _interleave([head(p) for p in range(pair)])
            return carry

        lax.fori_loop(0, n_chunk, chunk, 0)

    return pl.pallas_call(
        body, grid=(HG_HEADS // pair, s // HG_TB), in_specs=_hg_specs(lambda n: n, pair),
        out_specs=[pl.BlockSpec((HG_TB, pair * LANE), lambda h, n: (n, h)), pl.BlockSpec((pair, n_chunk, HG_K, HG_K), lambda h, n: (h, n, 0, 0))],
        out_shape=[_sds((s, HG_W), F32), _sds((HG_HEADS, s // CHUNK, HG_K, HG_K), BF16)],
        scratch_shapes=[pltpu.VMEM((pair, HG_K, HG_K), F32)],
        compiler_params=_params(2), name="hgrn_fwd")(*[proj] * (3 * pair), hl)


def _hgrn_bwd(proj, hl, states, d_o):
    s = proj.shape[0]
    n_chunk = HG_TB // CHUNK
    n_blk = s // HG_TB
    pair = HG_PAIR_BWD
    rev = lambda n: n_blk - 1 - n

    def body(*refs):
        q_refs, f_refs, i_refs = refs[:pair], refs[pair:2 * pair], refs[2 * pair:3 * pair]
        hl_ref, st_in_ref, do_ref, dq_ref, df_ref, di_ref, dhl_ref, dst_ref, dlb_ref = refs[3 * pair:]
        n = pl.program_id(1)

        @pl.when(n == 0)
        def _():
            dst_ref[...] = jnp.zeros_like(dst_ref)
            dlb_ref[...] = jnp.zeros_like(dlb_ref)

        r_i = lax.broadcasted_iota(jnp.int32, (CHUNK, CHUNK), 0)
        c_i = lax.broadcasted_iota(jnp.int32, (CHUNK, CHUNK), 1)
        tri_lower = (r_i >= c_i).astype(BF16)
        tri_upper = (r_i <= c_i).astype(BF16)
        row = lax.broadcasted_iota(jnp.int32, (CHUNK, 1), 0)
        spread = _hg_spread()

        def chunk(cc, carry):
            c = n_chunk - 1 - cc
            rows = pl.ds(pl.multiple_of(c * CHUNK, CHUNK), CHUNK)
            def head(p):
                cols = slice(p * LANE, (p + 1) * LANE)
                lb = _lower_bound(hl_ref[:, cols])
                q_raw = q_refs[p][rows, :]
                vb = i_refs[p][rows, :].astype(BF16)
                sg, f, kk, sq, qs, b = _hg_gates(q_raw, f_refs[p][rows, :], lb, tri_lower)
                yield
                e_b = jnp.exp(b)
                qe = qs * e_b
                b_last = b[CHUNK - 1:CHUNK, :]
                e_last = jnp.exp(b_last)
                e_kd = jnp.exp(b_last - b)
                kd = kk * e_kd
                st0 = st_in_ref[p, c]
                d_ob = do_ref[rows, cols].astype(BF16)
                dst = dst_ref[p]
                dst_b = dst.astype(BF16)
                d_a = jnp.where(r_i >= c_i, _dot(d_ob, vb, NT), 0.0)
                d_at = jnp.where(r_i <= c_i, _dot(vb, d_ob, NT), 0.0)
                d_v_st = _dot(kd.astype(BF16), dst_b, NT)
                d_kd = _dot(vb, dst_b, NN)
                d_qe = _dot(d_ob, st0, NN)
                dst_new = _dot(d_ob, qe.astype(BF16), TN)
                yield
                a, dqs, dkk, d_b = yield from _hg_intra(qs, kk, b, (d_a, d_at, spread))
                d_v = _dot(a.astype(BF16), d_ob, TN) + d_v_st
                dqs_st = d_qe * e_b
                dkk_st = d_kd * e_kd
                dqs = dqs + dqs_st
                dkk = dkk + dkk_st
                d_b_last = jnp.sum(d_kd * kd, axis=0, keepdims=True) + jnp.sum(dst * st0.astype(F32), axis=0, keepdims=True) * e_last
                d_b = d_b + qs * dqs_st - kk * dkk_st + jnp.where(row == CHUNK - 1, d_b_last, 0.0)
                d_g = _tri_matmul(tri_upper, d_b)
                dst_ref[p] = dst_new + dst * e_last
                yield
                d_f = d_g / f - dkk
                dlb_ref[:, cols] += jnp.sum(d_f * (1.0 - sg), axis=0, keepdims=True)
                dq_ref[rows, cols] = (dqs * (sq * (1.0 + q_raw * (1.0 - sq)))).astype(dq_ref.dtype)
                df_ref[rows, cols] = (d_f * (1.0 - lb) * (sg * (1.0 - sg))).astype(df_ref.dtype)
                di_ref[rows, cols] = d_v.astype(di_ref.dtype)

            _interleave([head(p) for p in range(pair)])
            return carry

        lax.fori_loop(0, n_chunk, chunk, 0)

        @pl.when(n == n_blk - 1)
        def _():
            lb = _lower_bound(hl_ref[...])
            d_hl0 = dlb_ref[...] * (lb * (1.0 - lb))
            dhl_ref[...] = jnp.concatenate([d_hl0, -d_hl0], axis=0)

    out_blk = pl.BlockSpec((HG_TB, pair * LANE), lambda h, n: (rev(n), h))
    return pl.pallas_call(
        body, grid=(HG_HEADS // pair, n_blk),
        in_specs=_hg_specs(rev, pair) + [pl.BlockSpec((pair, n_chunk, HG_K, HG_K), lambda h, n: (h, rev(n), 0, 0)), out_blk],
        out_specs=[out_blk, out_blk, out_blk, pl.BlockSpec((2, pair * LANE), lambda h, n: (0, h))],
        out_shape=[_sds((s, HG_W), BF16)] * 3 + [_sds((2, HG_W), F32)],
        scratch_shapes=[pltpu.VMEM((pair, HG_K, HG_K), F32), pltpu.VMEM((1, pair * LANE), F32)],
        compiler_params=_params(2), name="hgrn_bwd")(*[proj] * (3 * pair), hl, states, d_o)


def _mod_part(c_all, w_shard, b_shard):
    n = w_shard.shape[1]
    tn = 512

    def body(c_ref, w_ref, b_ref, o_ref):
        o_ref[...] = _dot(c_ref[...].astype(BF16), w_ref[...].astype(BF16), NN) + b_ref[...]

    return pl.pallas_call(body, grid=(n // tn,),
                          in_specs=[pl.BlockSpec((N_DEV, D), lambda j: (0, 0)), pl.BlockSpec((D, tn), lambda j: (0, j)), pl.BlockSpec((1, tn), lambda j: (0, j))],
                          out_specs=pl.BlockSpec((N_DEV, tn), lambda j: (0, j)), out_shape=_sds((N_DEV, n), F32),
                          compiler_params=_params(1, 32 << 20), name="mod_part")(c_all, w_shard, b_shard)


def _grad_w_ada(c_all_t, dmod_cols):
    n = dmod_cols.shape[1]
    tn = 512

    def body(c_ref, d_ref, o_ref):
        cv = c_ref[...].astype(BF16).astype(F32)
        dv = d_ref[...].astype(BF16).astype(F32)
        acc = cv[:, 0:1] * dv[0:1, :]
        for k in range(1, N_DEV):
            acc = acc + cv[:, k:k + 1] * dv[k:k + 1, :]
        o_ref[...] = acc

    return pl.pallas_call(body, grid=(n // tn,),
                          in_specs=[pl.BlockSpec((D, N_DEV), lambda j: (0, 0)), pl.BlockSpec((N_DEV, tn), lambda j: (0, j))],
                          out_specs=pl.BlockSpec((D, tn), lambda j: (0, j)), out_shape=_sds((D, n), F32),
                          compiler_params=_params(1, 32 << 20), name="grad_w_ada")(c_all_t, dmod_cols)


def _row_tile(r, c, max_elems=1 << 18):
    if r * c <= max_elems or r % 8:
        return r
    best = 8
    for t in range(8, r + 1, 8):
        if r % t == 0 and t * c <= max_elems:
            best = t
    return best


WIDE_TILE = 5 << 17


def _adamw(pieces, w, m, v, name, emit_grad=True, own=None, max_elems=1 << 18):
    p, r, c = pieces.shape
    tr = _row_tile(r, c, max_elems)
    c1 = 1.0 / (1.0 - ADAM_B1 ** ADAM_STEP)
    c2 = 1.0 / (1.0 - ADAM_B2 ** ADAM_STEP)

    def body(*refs):
        if own is None:
            p_ref, w_ref, m_ref, v_ref, *outs = refs
            g = p_ref[0].astype(F32)
        else:
            o_ref, p_ref, w_ref, m_ref, v_ref, *outs = refs
            g = o_ref[...].astype(F32) + p_ref[0].astype(F32)
        for k in range(1, p):
            g = g + p_ref[k].astype(F32)
        m2 = ADAM_B1 * m_ref[...] + (1.0 - ADAM_B1) * g
        v2 = ADAM_B2 * v_ref[...] + (1.0 - ADAM_B2) * (g * g)
        delta = -ADAM_LR * ((m2 * c1) / (jnp.sqrt(v2 * c2) + ADAM_EPS) + ADAM_WD * w_ref[...])
        if emit_grad:
            outs[0][...] = g
        outs[-3][...] = delta
        outs[-2][...] = m2
        outs[-1][...] = v2

    blk = pl.BlockSpec((tr, c), lambda i: (i, 0))
    n_out = 4 if emit_grad else 3
    lead = [] if own is None else [own]
    return pl.pallas_call(body, grid=(r // tr,), in_specs=[blk] * len(lead) + [pl.BlockSpec((p, tr, c), lambda i: (0, i, 0)), blk, blk, blk],
                          out_specs=[blk] * n_out, out_shape=[_sds((r, c), F32)] * n_out,
                          compiler_params=_params(1, 48 << 20), name=name)(*lead, pieces, w, m, v)


def _my_coords():
    return lax.axis_index("x"), lax.axis_index("y"), lax.axis_index("c")


def _flip(coords, k):
    x, y, c = coords
    return (1 - x if k & 4 else x, 1 - y if k & 2 else y, 1 - c if k & 1 else c)


def _lin(coords):
    return 4 * coords[0] + 2 * coords[1] + coords[2]


def _exchange_small(x3, bcast, name):
    n = x3.shape[2]

    def body(x_ref, o_ref, send_sems, recv_sems):
        me = _my_coords()
        my_id = _lin(me)
        o_ref[pl.ds(my_id, 1)] = x_ref[pl.ds(0 if bcast else my_id, 1)]
        copies = []
        for k in range(1, N_DEV):
            peer = _flip(me, k)
            src = x_ref.at[0 if bcast else _lin(peer)]
            cp = pltpu.make_async_remote_copy(src_ref=src, dst_ref=o_ref.at[my_id], send_sem=send_sems.at[k], recv_sem=recv_sems.at[k],
                                              device_id=peer, device_id_type=MESH)
            cp.start()
            copies.append(cp)
        for k in range(1, N_DEV):
            peer = _flip(me, k)
            pltpu.make_async_remote_copy(src_ref=x_ref.at[0], dst_ref=o_ref.at[_lin(peer)], send_sem=send_sems.at[k], recv_sem=recv_sems.at[k],
                                         device_id=peer, device_id_type=MESH).wait_recv()
        for cp in copies:
            cp.wait_send()

    vm = pl.BlockSpec(memory_space=pltpu.VMEM)
    return pl.pallas_call(body, in_specs=[vm], out_specs=vm, out_shape=_sds((N_DEV, 1, n), F32),
                          scratch_shapes=[pltpu.SemaphoreType.DMA((N_DEV,)), pltpu.SemaphoreType.DMA((N_DEV,))], name=name)(x3)


HBM_SPEC = pl.BlockSpec(memory_space=pltpu.HBM)
SEM_SPEC = pl.BlockSpec(memory_space=pltpu.SEMAPHORE)
ANY_SPEC = pl.BlockSpec(memory_space=pl.ANY)
DATAFLOW = pltpu.SideEffectType.DATAFLOW_SIDE_EFFECTING
GATHER_FLIPS = (1, 2, 4, 6)
PASS_FLIPS = (2, 4, 6)
TOKEN = (8, LANE)


def _hbm(t):
    return pltpu.with_memory_space_constraint(t, pltpu.HBM)


def _hbm_like(ts):
    return [pltpu.HBM(t.shape, t.dtype) for t in ts]


def _split_start(issue, srcs, lands, n_sem, name, deps=()):
    n, nb, nd = len(srcs), len(srcs) + len(lands), len(deps)

    def body(*refs):
        issue(refs[:n], refs[n:nb], refs[nb + nd], refs[nb + nd + 1])
        refs[-1][...] = jnp.zeros(TOKEN, F32)

    outs = pl.pallas_call(
        body, name=name,
        out_shape=(pltpu.SemaphoreType.DMA((n_sem,)), pltpu.SemaphoreType.DMA((n_sem,)), *_hbm_like(srcs), *_hbm_like(lands), _sds(TOKEN, F32)),
        in_specs=[HBM_SPEC] * nb + [ANY_SPEC] * nd,
        out_specs=(SEM_SPEC, SEM_SPEC, *[HBM_SPEC] * nb, pl.BlockSpec(memory_space=pltpu.VMEM)),
        input_output_aliases={i: 2 + i for i in range(nb)},
        compiler_params=pltpu.CompilerParams(has_side_effects=DATAFLOW))(*[_hbm(t) for t in srcs], *[_hbm(t) for t in lands], *deps)
    return dict(sems=outs[:2], thru=list(outs[2:2 + nb]), token=outs[-1], n=n)


def _split_wait(finish, handle, after, name):
    n = handle["n"]
    thru = handle["thru"]
    nb = len(thru)

    def body(*refs):
        finish(refs[:n], refs[n:nb], refs[nb], refs[nb + 1])

    outs = pl.pallas_call(
        body, name=name, out_shape=_hbm_like(thru), in_specs=[HBM_SPEC] * nb + [SEM_SPEC, SEM_SPEC] + [ANY_SPEC] * len(after),
        out_specs=[HBM_SPEC] * nb, input_output_aliases={i: i for i in range(nb)},
        compiler_params=pltpu.CompilerParams(has_side_effects=DATAFLOW))(*thru, *handle["sems"], *after)
    return list(outs[:n]), list(outs[n:])


def _gather_start(shards, name, deps=()):
    n = len(shards)
    my_id = _lin(_my_coords())
    lands = [lax.dynamic_update_slice(lax.empty((N_DEV,) + t.shape, t.dtype), t[None], (my_id, 0, 0)) for t in shards]

    def issue(src, land, send_sems, recv_sems):
        me = _my_coords()
        for w in range(n):
            for j, k in enumerate(GATHER_FLIPS):
                q = len(GATHER_FLIPS) * w + j
                pltpu.make_async_remote_copy(src_ref=src[w], dst_ref=land[w].at[_lin(me)], send_sem=send_sems.at[q], recv_sem=recv_sems.at[q],
                                             device_id=_flip(me, k), device_id_type=MESH).start()

    return _split_start(issue, shards, lands, len(GATHER_FLIPS) * n, name, deps)


def _gather_wait(handle, after, name):
    n = handle["n"]

    def finish(src, land, send_sems, recv_sems):
        me = _my_coords()
        for w in range(n):
            for j, k in enumerate(GATHER_FLIPS):
                q = len(GATHER_FLIPS) * w + j
                peer = _flip(me, k)
                cp = pltpu.make_async_remote_copy(src_ref=src[w], dst_ref=land[w].at[_lin(peer)], send_sem=send_sems.at[q], recv_sem=recv_sems.at[q],
                                                  device_id=peer, device_id_type=MESH)
                cp.wait_send()
                cp.wait_recv()

    return _split_wait(finish, handle, after, name)[1]


def _pass_copy(land, send_sems, recv_sems, w, j, arriving):
    me = _my_coords()
    blk = land[w].at[_lin(_flip(me, PASS_FLIPS[j] + (1 if arriving else 0)))]
    q = len(PASS_FLIPS) * w + j
    return pltpu.make_async_remote_copy(src_ref=blk, dst_ref=blk, send_sem=send_sems.at[q], recv_sem=recv_sems.at[q],
                                        device_id=_flip(me, 1), device_id_type=MESH)


def _pass_start(lands, name, deps=()):
    def issue(_, land, send_sems, recv_sems):
        for w in range(len(lands)):
            for j in range(len(PASS_FLIPS)):
                _pass_copy(land, send_sems, recv_sems, w, j, False).start()

    return _split_start(issue, [], lands, len(PASS_FLIPS) * len(lands), name, deps)


def _pass_wait(handle, after, name):
    def finish(_, land, send_sems, recv_sems):
        for w in range(len(handle["thru"])):
            for j in range(len(PASS_FLIPS)):
                _pass_copy(land, send_sems, recv_sems, w, j, False).wait_send()
                _pass_copy(land, send_sems, recv_sems, w, j, True).wait_recv()

    return _split_wait(finish, handle, after, name)[1]


def _gather_pass(lands, name):
    n = len(lands)
    n_p = len(PASS_FLIPS)

    def body(*refs):
        land = refs[n:2 * n]
        send_sems, recv_sems = refs[2 * n:]
        me = _my_coords()
        sibling = _flip(me, 1)
        sent = []
        for w in range(n):
            for j, k in enumerate(PASS_FLIPS):
                blk = land[w].at[_lin(_flip(me, k))]
                cp = pltpu.make_async_remote_copy(src_ref=blk, dst_ref=blk, send_sem=send_sems.at[n_p * w + j], recv_sem=recv_sems.at[n_p * w + j],
                                                  device_id=sibling, device_id_type=MESH)
                cp.start()
                sent.append(cp)
        for w in range(n):
            for j, k in enumerate(PASS_FLIPS):
                blk = land[w].at[_lin(_flip(me, k + 1))]
                pltpu.make_async_remote_copy(src_ref=blk, dst_ref=blk, send_sem=send_sems.at[n_p * w + j], recv_sem=recv_sems.at[n_p * w + j],
                                             device_id=sibling, device_id_type=MESH).wait_recv()
        for cp in sent:
            cp.wait_send()

    return pl.pallas_call(body, in_specs=[ANY_SPEC] * n, out_specs=[ANY_SPEC] * n, out_shape=[_sds(t.shape, t.dtype) for t in lands],
                          input_output_aliases={i: i for i in range(n)},
                          scratch_shapes=[pltpu.SemaphoreType.DMA((n_p * n,)), pltpu.SemaphoreType.DMA((n_p * n,))], name=name)(*lands)


CHIP_FLIPS = (0, 2, 4, 6)


def _pair_copy(src, land, send_sems, recv_sems, w, j):
    me = _my_coords()
    q = len(CHIP_FLIPS) * w + j
    return pltpu.make_async_remote_copy(src_ref=src[w].at[_lin(_flip(me, CHIP_FLIPS[j] + 1))], dst_ref=land[w].at[j], send_sem=send_sems.at[q],
                                        recv_sem=recv_sems.at[q], device_id=_flip(me, 1), device_id_type=MESH)


def _pair_exchange(grads, name):
    n = len(grads)

    def body(*refs):
        src, land = refs[:n], refs[n:2 * n]
        send_sems, recv_sems = refs[2 * n:]
        sent = [_pair_copy(src, land, send_sems, recv_sems, w, j) for w in range(n) for j in range(len(CHIP_FLIPS))]
        for cp in sent:
            cp.start()
        for cp in sent:
            cp.wait_recv()
        for cp in sent:
            cp.wait_send()

    outs = pl.pallas_call(body, in_specs=[ANY_SPEC] * n, out_specs=[ANY_SPEC] * n,
                          out_shape=[_sds((len(CHIP_FLIPS),) + g.shape[1:], g.dtype) for g in grads],
                          scratch_shapes=[pltpu.SemaphoreType.DMA((len(CHIP_FLIPS) * n,))] * 2, name=name)(*grads)
    return list(outs)


def _pair_start(grads, name, deps=()):
    n = len(grads)
    lands = [lax.empty((len(CHIP_FLIPS),) + g.shape[1:], g.dtype) for g in grads]

    def issue(src, land, send_sems, recv_sems):
        for w in range(n):
            for j in range(len(CHIP_FLIPS)):
                _pair_copy(src, land, send_sems, recv_sems, w, j).start()

    return _split_start(issue, grads, lands, len(CHIP_FLIPS) * n, name, deps)


def _pair_wait(handle, after, name):
    n = handle["n"]

    def finish(src, land, send_sems, recv_sems):
        for w in range(n):
            for j in range(len(CHIP_FLIPS)):
                cp = _pair_copy(src, land, send_sems, recv_sems, w, j)
                cp.wait_send()
                cp.wait_recv()

    return _split_wait(finish, handle, after, name)


def _pair_add(grad, theirs, name):
    p, r, c = theirs.shape
    tr = _row_tile(r, c, WIDE_TILE)
    me = _my_coords()
    ids = jnp.stack([_lin(_flip(me, k)) for k in CHIP_FLIPS]).astype(jnp.int32)

    def body(ids_ref, a_ref, b_ref, o_ref):
        o_ref[...] = (a_ref[...].astype(F32) + b_ref[...].astype(F32)).astype(o_ref.dtype)

    blk = pl.BlockSpec((None, tr, c), lambda j, i, ids_ref: (j, i, 0))
    return pl.pallas_call(
        body, out_shape=_sds((p, r, c), theirs.dtype), compiler_params=_params(2), name=name,
        grid_spec=pltpu.PrefetchScalarGridSpec(
            num_scalar_prefetch=1, grid=(p, r // tr),
            in_specs=[pl.BlockSpec((None, tr, c), lambda j, i, ids_ref: (ids_ref[j], i, 0)), blk], out_specs=blk))(ids, grad, theirs)


def _chips_start(parts, name, deps=()):
    n = len(parts)
    n_c = len(CHIP_FLIPS) - 1
    lands = [lax.empty((n_c,) + t.shape[1:], t.dtype) for t in parts]

    def issue(src, land, send_sems, recv_sems):
        me = _my_coords()
        for w in range(n):
            for j in range(1, n_c + 1):
                q = n_c * w + j - 1
                pltpu.make_async_remote_copy(src_ref=src[w].at[j], dst_ref=land[w].at[j - 1], send_sem=send_sems.at[q], recv_sem=recv_sems.at[q],
                                             device_id=_flip(me, CHIP_FLIPS[j]), device_id_type=MESH).start()

    return _split_start(issue, parts, lands, n_c * n, name, deps)


def _chips_wait(handle, after, name):
    n = handle["n"]
    n_c = len(CHIP_FLIPS) - 1

    def finish(src, land, send_sems, recv_sems):
        me = _my_coords()
        for w in range(n):
            for j in range(1, n_c + 1):
                q = n_c * w + j - 1
                cp = pltpu.make_async_remote_copy(src_ref=src[w].at[j], dst_ref=land[w].at[j - 1], send_sem=send_sems.at[q], recv_sem=recv_sems.at[q],
                                                  device_id=_flip(me, CHIP_FLIPS[j]), device_id_type=MESH)
                cp.wait_send()
                cp.wait_recv()

    return _split_wait(finish, handle, after, name)


def _after(t, *tokens):
    for tok in tokens:
        t = t + tok[0:1, 0:1]
    return t


def _rope_tables(positions):
    half = ROT // 2
    inv_freq = ROPE_THETA ** (-jnp.arange(0, ROT, 2, dtype=F32) / ROT)
    ang = positions.astype(F32).reshape(-1, 1) * inv_freq
    cos, sin = jnp.cos(ang), jnp.sin(ang)
    s = ang.shape[0]
    pad = jnp.zeros((s, HEAD_DIM - ROT), F32)
    zero = jnp.zeros((s, half), F32)
    two = lambda t: jnp.concatenate([t, t], axis=1)
    return (two(jnp.concatenate([cos, cos, pad + 1.0], axis=1)), two(jnp.concatenate([-sin, zero, pad], axis=1)),
            two(jnp.concatenate([zero, sin, pad], axis=1)))


def _local_step(x, tgt, tabs, mod, sinks_pad, hl, hg_norm, g_pre_mix, g_post_mix, g_pre_ffn, g_post_ffn, weights, prefetch, scatter, scatter_on):
    s = x.shape[0]
    h1 = _pre_fwd(x, g_pre_mix, mod, 1, 0, "pre_mix_fwd")
    (w_in_a,) = weights("in_a", h1)
    proj = _mm_nt(h1[:, :D // 2], w_in_a, 256, IN_COLS // 2, D // 2, F32, "proj_mm_a")
    (w_in_b,) = weights("in_b", proj)
    proj = _mm_nt(h1[:, D // 2:], w_in_b, 256, IN_COLS // 2, D // 2, F32, "proj_mm_b", add=proj)
    att = _attn_fwd(proj, tabs, _after(sinks_pad, prefetch("mix", proj)))
    o_raw, states = _hgrn_fwd(proj, hl)
    ohg = _hgout_fwd(o_raw, proj, hg_norm)
    w_attn_dm, w_hgrn_dm, w_out = weights("mix", ohg)
    natural = lambda w_dm: w_dm.transpose(1, 0, 2).reshape(w_dm.shape[1], D)
    pieces = lambda g: g.reshape(g.shape[0], N_DEV, D // N_DEV).transpose(1, 0, 2)
    w_attn, w_hgrn = natural(w_attn_dm), natural(w_hgrn_dm)
    y_a = _mm_nn(att, w_attn, s, 512, ATT_W, F32, "attn_proj_mm")
    y_h = _mm_nn(ohg, w_hgrn, s, 512, HG_W, F32, "hgrn_proj_mm")
    merged = _merge_fwd(y_a, y_h, proj)
    y = _mm_nn(merged, w_out, s, 512, D, F32, "out_mm")
    x1 = _post_fwd(x, y, g_post_mix, _after(mod, prefetch("ffn_in", y)), 2, "post_mix_fwd")
    h2 = _pre_fwd(x1, g_pre_ffn, mod, 4, 3, "pre_ffn_fwd")
    (w_ffn_in_dm,) = weights("ffn_in", h2)
    gu = _mm_nn_dm(h2, w_ffn_in_dm, s // 2, F32, "ffn_in_mm")
    act = _swiglu_fwd(gu, deps=[prefetch("ffn_out", gu)])
    (w_ffn_out,) = weights("ffn_out", act)
    y2 = _mm_nn(act, w_ffn_out, 512, 512, FFN, F32, "ffn_out_mm")
    err, loss, dy2, d_gate2, dg_post_ffn = _post_loss_bwd(x1, y2, g_post_ffn, mod, 5, tgt, "post_ffn_loss_bwd")
    gw_ffn_out = _mm_tn(act, dy2, 512, D, BF16, "ffn_out_dw")
    t_pair = scatter([gw_ffn_out.reshape(N_DEV, FFN // N_DEV, D)], "ffn_out")
    d_act = _mm_nt(dy2, w_ffn_out, s, 512, D, F32, "ffn_out_dx", deps=[t_pair])
    dgu = _swiglu_bwd(d_act, gu)
    t_out = scatter_on("ffn_out", dgu)
    gw_ffn_in = _mm_tn_dm(h2, dgu, 1024, BF16, "ffn_in_dw")
    t_pair = scatter([gw_ffn_in], "ffn_in")
    dh2 = _mm_nt_dm(dgu, w_ffn_in_dm, s, 1024, F32, "ffn_in_dx", deps=[t_pair])
    mod = _after(mod, t_out)
    dx1, d_shift2, d_scale2, dg_pre_ffn = _pre_bwd(dh2, x1, err, g_pre_ffn, mod, 4, "pre_ffn_bwd")
    dy, d_gate1, dg_post_mix = _post_bwd(dx1, y, g_post_mix, mod, 2, "post_mix_bwd")
    t_in = scatter_on("ffn_in", dy)
    d_merged = _mm_nt(dy, w_out, s, 512, D, F32, "out_dx")
    gw_out = _mm_tn(merged, dy, 512, D, BF16, "out_dw")
    dy_a, dy_h, d_gate_a, d_gate_h = _merge_bwd(d_merged, y_a, y_h, proj)
    gw_attn = pieces(_mm_tn(att, dy_a, 512, D, BF16, "attn_proj_dw"))
    gw_hgrn = pieces(_mm_tn(ohg, dy_h, 512, D, BF16, "hgrn_proj_dw"))
    t_pair = scatter([gw_attn, gw_hgrn, gw_out.reshape(N_DEV, D // N_DEV, D)], "mix")
    d_att = _mm_nt(dy_a, w_attn, s, 512, D, F32, "attn_proj_dx")
    d_ohg = _mm_nt(dy_h, w_hgrn, s, 512, D, F32, "hgrn_proj_dx", deps=[t_pair])
    d_o, d_gh, d_hg_norm = _hgout_bwd(d_ohg, o_raw, proj, _after(hg_norm, t_in))
    d_qh, d_fh, d_ih, d_hl = _hgrn_bwd(proj, hl, states, d_o)
    t_mix = scatter_on("mix", d_qh)
    d_qa, d_ka, d_va, d_sinks = _attn_bwd(proj, tabs, _after(sinks_pad, t_mix), d_att)
    d_proj = jnp.concatenate([d_qa, d_ka.astype(BF16), d_va.astype(BF16), d_qh, d_fh, d_ih, d_gh, d_gate_a, d_gate_h], axis=1)
    dh1 = jnp.concatenate([_mm_nn(d_proj, w_half, s // 2, 512, IN_COLS // 2, F32, "proj_dx_" + tag)
                           for tag, w_half in (("a", w_in_a), ("b", w_in_b))], axis=1)
    grad_x, d_shift1, d_scale1, dg_pre_mix = _pre_bwd(dh1, x, dx1, g_pre_mix, mod, 1, "pre_mix_bwd")
    d_mod = jnp.concatenate([d_shift1, d_scale1, d_gate1, d_shift2, d_scale2, d_gate2], axis=1)
    small = [d_mod, dg_pre_mix, dg_post_mix, dg_pre_ffn, dg_post_ffn, d_hl.reshape(1, 2 * HG_W), d_hg_norm, d_sinks]
    return loss, grad_x, small, h1, d_proj


def kernel(x, c, positions, w_ada, b_ada, g_pre_mix, g_post_mix, g_pre_ffn, g_post_ffn, w_in, attn_sinks, w_attn_proj, hg_lower_bounds, hg_norm, w_hgrn_proj, w_out, w_ffn_in, w_ffn_out, loss_target, m_w_ada, m_b_ada, m_g_pre_mix, m_g_post_mix, m_g_pre_ffn, m_g_post_ffn, m_w_in, m_attn_sinks, m_w_attn_proj, m_hg_lower_bounds, m_hg_norm, m_w_hgrn_proj, m_w_out, m_w_ffn_in, m_w_ffn_out, v_w_ada, v_b_ada, v_g_pre_mix, v_g_post_mix, v_g_pre_ffn, v_g_post_ffn, v_w_in, v_attn_sinks, v_w_attn_proj, v_hg_lower_bounds, v_hg_norm, v_w_hgrn_proj, v_w_out, v_w_ffn_in, v_w_ffn_out):
    my_id = _lin(_my_coords())
    s = x.shape[1]
    n_ada = w_ada.shape[2]

    c_all = _exchange_small(c.reshape(1, 1, D), True, "gather_c").reshape(N_DEV, D)
    b_cols = lax.dynamic_slice(b_ada, (0, my_id * n_ada), (1, n_ada))
    mod_part = _mod_part(c_all, w_ada[0], b_cols)
    mod = _exchange_small(mod_part.reshape(N_DEV, 1, n_ada), False, "scatter_mod").reshape(1, N_MOD * D)
    groups = {"in_a": [w_in[0].T[:, :D // 2]], "in_b": [w_in[0].T[:, D // 2:]], "mix": [w_attn_proj[0], w_hgrn_proj[0], w_out[0]],
              "ffn_in": [w_ffn_in[0]], "ffn_out": [w_ffn_out[0]]}

    def start(group, dep):
        shards, dep = lax.optimization_barrier((groups[group], dep))
        return _gather_start([t.astype(BF16) for t in shards], "gather_start_" + group, deps=[dep])

    gathers = {"in_a": start("in_a", mod)}
    gathers["in_b"] = start("in_b", gathers["in_a"]["token"])
    gathers["mix"] = start("mix", gathers["in_b"]["token"])
    gathers["ffn_in"] = start("ffn_in", gathers["mix"]["token"])
    gathers["ffn_out"] = start("ffn_out", gathers["ffn_in"]["token"])

    passes = {}

    def prefetch(group, after):
        lands = _gather_wait(gathers[group], [after], "gather_wait_" + group)
        passes[group] = _pass_start(lands, "gather_pass_start_" + group)
        return passes[group]["token"]

    def weights(group, after):
        if group in passes:
            lands = _pass_wait(passes[group], [after], "gather_pass_wait_" + group)
        else:
            after = [after, gathers["ffn_out"]["token"]]
            lands = _gather_pass(_gather_wait(gathers[group], after, "gather_wait_" + group), "gather_pass_" + group)
        if group in ("in_a", "in_b"):
            return (lands[0].reshape(IN_COLS, D // 2),)
        if group == "mix":
            return lands[0], lands[1], lands[2].reshape(D, D)
        return (lands[0],) if group == "ffn_in" else (lands[0].reshape(FFN, D),)

    pairs, scatters = {}, {}

    def scatter(grads, group):
        pairs[group] = _pair_start(grads, "scatter_pair_" + group)
        return pairs[group]["token"]

    def scatter_on(group, after):
        if group in pairs:
            local, theirs = _pair_wait(pairs[group], [after], "scatter_pair_wait_" + group)
        else:
            local, theirs = after, _pair_exchange(after, "scatter_pair_" + group)
        parts = [_pair_add(g, t, "scatter_pair_add_%s_%d" % (group, k)) for k, (g, t) in enumerate(zip(local, theirs))]
        scatters[group] = _chips_start(parts, "scatter_start_" + group)
        return scatters[group]["token"]

    sinks_pad = jnp.pad(attn_sinks, ((0, 0), (0, LANE - ATT_HEADS)))
    loss, grad_x, small, h1, d_proj = _local_step(
        x[0], loss_target[0], _rope_tables(positions), mod, sinks_pad, hg_lower_bounds, hg_norm, g_pre_mix, g_post_mix, g_pre_ffn, g_post_ffn,
        weights, prefetch, scatter, scatter_on)
    loss = lax.psum(loss[0, 0], ("x", "y", "c"))

    sizes = [t.shape[1] for t in small]
    parts = _exchange_small(jnp.concatenate(small, axis=1).reshape(1, 1, sum(sizes)), True, "gather_small_grads")
    dep = parts
    for half, cols in (("in_a", slice(0, D // 2)), ("in_b", slice(D // 2, D))):
        gw_half = _mm_tn(d_proj, h1[:, cols], 256, D // 2, BF16, "proj_dw_" + half, deps=[dep])
        dep = scatter_on(half, [gw_half.reshape(N_DEV, IN_COLS // N_DEV, D // 2)])
    offs = [sum(sizes[:k]) for k in range(len(sizes))]
    piece = lambda k, n=None: parts[:, :, offs[k]:offs[k] + (sizes[k] if n is None else n)]
    small_w = [(piece(0), b_ada, m_b_ada, v_b_ada), (piece(1), g_pre_mix, m_g_pre_mix, v_g_pre_mix),
               (piece(2), g_post_mix, m_g_post_mix, v_g_post_mix), (piece(3), g_pre_ffn, m_g_pre_ffn, v_g_pre_ffn),
               (piece(4), g_post_ffn, m_g_post_ffn, v_g_post_ffn),
               (piece(5).reshape(N_DEV, 2, HG_W), hg_lower_bounds, m_hg_lower_bounds, v_hg_lower_bounds),
               (piece(6), hg_norm, m_hg_norm, v_hg_norm), (piece(7, ATT_HEADS), attn_sinks, m_attn_sinks, v_attn_sinks)]
    names = ["b_ada", "g_pre_mix", "g_post_mix", "g_pre_ffn", "g_post_ffn", "hg_lower_bounds", "hg_norm", "attn_sinks"]
    res = {n: _adamw(p, w, m, v, "adamw_" + n) for n, (p, w, m, v) in zip(names, small_w)}

    dmod_cols = lax.dynamic_slice(parts.reshape(N_DEV, -1), (0, my_id * n_ada), (N_DEV, n_ada))
    g_w_ada = _grad_w_ada(c_all.T, dmod_cols)
    res["w_ada"] = [g_w_ada] + list(_adamw(g_w_ada[None], w_ada[0], m_w_ada[0], v_w_ada[0], "adamw_w_ada", emit_grad=False))

    big = {"ffn_out": [("w_ffn_out", w_ffn_out, m_w_ffn_out, v_w_ffn_out)], "ffn_in": [("w_ffn_in", w_ffn_in, m_w_ffn_in, v_w_ffn_in)],
           "mix": [("w_attn_proj", w_attn_proj, m_w_attn_proj, v_w_attn_proj), ("w_hgrn_proj", w_hgrn_proj, m_w_hgrn_proj, v_w_hgrn_proj),
                   ("w_out", w_out, m_w_out, v_w_out)]}
    after = [scatters["in_b"]["token"]]
    for group, members in big.items():
        local, lands = _chips_wait(scatters[group], after, "scatter_wait_" + group)
        for (n, w, m, v), mine, land in zip(members, local, lands):
            res[n] = _adamw(land, w[0], m[0], v[0], "adamw_" + n, own=mine[0])
            after = after + [res[n][1]]
    after = [res[n][1] for n in res]
    halves = [_chips_wait(scatters[half], after, "scatter_wait_" + half) for half in ("in_a", "in_b")]
    own = jnp.concatenate([local[0][0] for local, _ in halves], axis=1)
    land = jnp.concatenate([lands[0] for _, lands in halves], axis=2)
    res["w_in"] = [t.T for t in _adamw(land, w_in[0].T, m_w_in[0].T, v_w_in[0].T, "adamw_w_in", own=own, max_elems=WIDE_TILE)]

    order = ["w_ada", "b_ada", "g_pre_mix", "g_post_mix", "g_pre_ffn", "g_post_ffn", "w_in", "attn_sinks", "w_attn_proj",
             "hg_lower_bounds", "hg_norm", "w_hgrn_proj", "w_out", "w_ffn_in", "w_ffn_out"]
    lead = {"w_ada", "w_in", "w_attn_proj", "w_hgrn_proj", "w_out", "w_ffn_in", "w_ffn_out"}
    outs = [loss, grad_x[None]]
    for k in range(4):
        outs += [res[n][k][None] if n in lead else res[n][k] for n in order]
    return tuple(outs)
```

```python
import functools

import jax
import jax.numpy as jnp
from jax import lax
from jax.experimental import pallas as pl
from jax.experimental.pallas import tpu as pltpu

F32 = jnp.float32
BF16 = jnp.bfloat16

N_DEV = 8
D = 2048
ATT_HEADS = 16
KV_HEADS = 2
HEAD_DIM = 64
GROUP = ATT_HEADS // KV_HEADS
ATT_W = ATT_HEADS * HEAD_DIM
BLK = 128
ROT = HEAD_DIM // 4
ROPE_THETA = 500000.0
HG_HEADS = 8
HG_K = 128
HG_W = HG_HEADS * HG_K
CHUNK = 64
SUB = 16
FFN = 5632
N_MOD = 6
EPS = 1e-6
LANE = 128
Q_A, K_A, V_A, Q_H, F_H, I_H, G_H, GT_A, GT_H, IN_COLS = 0, 1024, 1152, 1280, 2304, 3328, 4352, 5376, 7424, 9472

ADAM_LR, ADAM_B1, ADAM_B2, ADAM_EPS, ADAM_WD, ADAM_STEP = 0.001, 0.9, 0.999, 1e-08, 0.01, 10

TR = 256
HG_TB = 512
VMEM_BIG = 56 << 20
MESH = pl.DeviceIdType.MESH


def _sds(shape, dtype):
    return jax.ShapeDtypeStruct(shape, dtype)


def _params(n_axes, vmem=None):
    return pltpu.CompilerParams(dimension_semantics=("arbitrary",) * n_axes, vmem_limit_bytes=vmem)


def _sig(t):
    return 1.0 / (1.0 + jnp.exp(-t))


def _dot(a, b, dims):
    return lax.dot_general(a, b, (dims, ((), ())), preferred_element_type=F32)


NN = ((1,), (0,))
NT = ((1,), (1,))
TN = ((0,), (0,))


def _matmul(a, b, a_spec, b_spec, o_spec, out_shape, grid, dims, acc_shape, name, deps=(), add=None):
    nk = grid[2]
    nd = len(deps)
    extra = [] if add is None else [add]

    def body(a_ref, b_ref, *rest):
        o_ref, scratch = rest[nd + len(extra)], rest[nd + len(extra) + 1:]
        part = _dot(a_ref[...], b_ref[...], dims)
        if add is not None:
            assert nk == 1
            part = part + rest[nd][...]
        if nk == 1:
            o_ref[...] = part.astype(o_ref.dtype)
        else:
            acc = scratch[0]
            k = pl.program_id(2)

            @pl.when(k == 0)
            def _():
                acc[...] = part

            @pl.when(k > 0)
            def _():
                acc[...] += part

            @pl.when(k == nk - 1)
            def _():
                o_ref[...] = acc[...].astype(o_ref.dtype)

    return pl.pallas_call(
        body, grid=grid, in_specs=[a_spec, b_spec] + [pl.BlockSpec(memory_space=pl.ANY)] * nd + [o_spec] * len(extra),
        out_specs=o_spec, out_shape=out_shape, scratch_shapes=[pltpu.VMEM(acc_shape, F32)] if nk > 1 else [],
        input_output_aliases={2 + nd: 0} if extra else {},
        compiler_params=_params(3, VMEM_BIG), name=name)(a, b, *deps, *extra)


def _mm_nn(a, b, tm, tn, tk, out_dtype, name):
    m, k = a.shape
    n = b.shape[1]
    return _matmul(a, b, pl.BlockSpec((tm, tk), lambda j, i, kk: (i, kk)), pl.BlockSpec((tk, tn), lambda j, i, kk: (kk, j)),
                   pl.BlockSpec((tm, tn), lambda j, i, kk: (i, j)), _sds((m, n), out_dtype),
                   (n // tn, m // tm, k // tk), NN, (tm, tn), name)


def _mm_nn_dm(a, b, tm, out_dtype, name):
    m, k = a.shape
    n = b.shape[2]
    return _matmul(a, b, pl.BlockSpec((tm, k), lambda j, i, kk: (i, 0)), pl.BlockSpec((None, k, n), lambda j, i, kk: (j, 0, 0)),
                   pl.BlockSpec((tm, n), lambda j, i, kk: (i, j)), _sds((m, N_DEV * n), out_dtype),
                   (N_DEV, m // tm, 1), NN, (tm, n), name)


def _mm_nt(a, b, tm, tn, tk, out_dtype, name, deps=(), add=None):
    m, k = a.shape
    n = b.shape[0]
    return _matmul(a, b, pl.BlockSpec((tm, tk), lambda j, i, kk: (i, kk)), pl.BlockSpec((tn, tk), lambda j, i, kk: (j, kk)),
                   pl.BlockSpec((tm, tn), lambda j, i, kk: (i, j)), _sds((m, n), out_dtype),
                   (n // tn, m // tm, k // tk), NT, (tm, tn), name, deps, add)


def _mm_nt_dm(a, b, tm, tn, out_dtype, name, deps=()):
    m = a.shape[0]
    n_out, n = b.shape[1], b.shape[2]
    return _matmul(a, b, pl.BlockSpec((tm, n), lambda j, i, kk: (i, kk)), pl.BlockSpec((None, tn, n), lambda j, i, kk: (kk, j, 0)),
                   pl.BlockSpec((tm, tn), lambda j, i, kk: (i, j)), _sds((m, n_out), out_dtype),
                   (n_out // tn, m // tm, N_DEV), NT, (tm, tn), name, deps)


def _mm_tn(a, b, tm, tn, out_dtype, name, deps=()):
    s, m = a.shape
    n = b.shape[1]
    return _matmul(a, b, pl.BlockSpec((s, tm), lambda j, i, kk: (0, i)), pl.BlockSpec((s, tn), lambda j, i, kk: (0, j)),
                   pl.BlockSpec((tm, tn), lambda j, i, kk: (i, j)), _sds((m, n), out_dtype),
                   (n // tn, m // tm, 1), TN, (tm, tn), name, deps)


def _mm_tn_dm(a, b, tm, out_dtype, name):
    s, m = a.shape
    n = b.shape[1] // N_DEV
    return _matmul(a, b, pl.BlockSpec((s, tm), lambda j, i, kk: (0, i)), pl.BlockSpec((s, n), lambda j, i, kk: (0, j)),
                   pl.BlockSpec((None, tm, n), lambda j, i, kk: (j, i, 0)), _sds((N_DEV, m, n), out_dtype),
                   (N_DEV, m // tm, 1), TN, (tm, n), name)


def _row_spec():
    return pl.BlockSpec((TR, D), lambda i: (i, 0))


def _vec_spec(k=0):
    return pl.BlockSpec((1, D), lambda i: (0, k))


def _acc_rows(ref, first, val):
    @pl.when(first)
    def _():
        ref[...] = val

    @pl.when(jnp.logical_not(first))
    def _():
        ref[...] += val


def _pre_fwd(x, g, mod, k_scale, k_shift, name):
    s = x.shape[0]

    def body(x_ref, g_ref, sc_ref, sh_ref, h_ref):
        xv = x_ref[...]
        r = lax.rsqrt(jnp.mean(xv * xv, axis=-1, keepdims=True) + EPS)
        n = xv * r * g_ref[...]
        h_ref[...] = (n * (1.0 + sc_ref[...]) + sh_ref[...]).astype(h_ref.dtype)

    return pl.pallas_call(body, grid=(s // TR,), in_specs=[_row_spec(), _vec_spec(), _vec_spec(k_scale), _vec_spec(k_shift)],
                          out_specs=_row_spec(), out_shape=_sds((s, D), BF16), compiler_params=_params(1), name=name)(x, g, mod, mod)


def _post_fwd(x, y, g, mod, k_gate, name):
    s = x.shape[0]

    def body(x_ref, y_ref, g_ref, gt_ref, o_ref):
        yv = y_ref[...]
        r = lax.rsqrt(jnp.mean(yv * yv, axis=-1, keepdims=True) + EPS)
        o_ref[...] = x_ref[...] + gt_ref[...] * (yv * r * g_ref[...])

    return pl.pallas_call(body, grid=(s // TR,), in_specs=[_row_spec(), _row_spec(), _vec_spec(), _vec_spec(k_gate)],
                          out_specs=_row_spec(), out_shape=_sds((s, D), F32), compiler_params=_params(1), name=name)(x, y, g, mod)


def _post_loss_bwd(x, y, g, mod, k_gate, tgt, name):
    s = x.shape[0]

    def body(x_ref, y_ref, g_ref, gt_ref, t_ref, e_ref, loss_ref, dy_ref, dgt_ref, dg_ref):
        first = pl.program_id(0) == 0
        yv, gv, gate = y_ref[...], g_ref[...], gt_ref[...]
        r = lax.rsqrt(jnp.mean(yv * yv, axis=-1, keepdims=True) + EPS)
        yh = yv * r
        err = x_ref[...] + gate * (yh * gv) - t_ref[...]
        e = err * (1.0 / D)
        e_ref[...] = e
        _acc_rows(loss_ref, first, 0.5 * jnp.sum(jnp.mean(err * err, axis=-1, keepdims=True), axis=0, keepdims=True))
        dn = e * gate
        dgn = dn * gv
        dy_ref[...] = (r * (dgn - yh * jnp.mean(dgn * yh, axis=-1, keepdims=True))).astype(dy_ref.dtype)
        _acc_rows(dgt_ref, first, jnp.sum(e * (yh * gv), axis=0, keepdims=True))
        _acc_rows(dg_ref, first, jnp.sum(dn * yh, axis=0, keepdims=True))

    return pl.pallas_call(body, grid=(s // TR,),
                          in_specs=[_row_spec(), _row_spec(), _vec_spec(), _vec_spec(k_gate), _row_spec()],
                          out_specs=[_row_spec(), pl.BlockSpec((1, 1), lambda i: (0, 0)), _row_spec(), _vec_spec(), _vec_spec()],
                          out_shape=[_sds((s, D), F32), _sds((1, 1), F32), _sds((s, D), BF16), _sds((1, D), F32), _sds((1, D), F32)],
                          compiler_params=_params(1), name=name)(x, y, g, mod, tgt)


def _pre_bwd(dh, x, res, g, mod, k_scale, name):
    s = x.shape[0]

    def body(dh_ref, x_ref, res_ref, g_ref, sc_ref, dx_ref, dsh_ref, dsc_ref, dg_ref):
        first = pl.program_id(0) == 0
        xv, dh_v, gv = x_ref[...], dh_ref[...], g_ref[...]
        r = lax.rsqrt(jnp.mean(xv * xv, axis=-1, keepdims=True) + EPS)
        xh = xv * r
        dn = dh_v * (1.0 + sc_ref[...])
        dgn = dn * gv
        dx_ref[...] = res_ref[...] + r * (dgn - xh * jnp.mean(dgn * xh, axis=-1, keepdims=True))
        _acc_rows(dsh_ref, first, jnp.sum(dh_v, axis=0, keepdims=True))
        _acc_rows(dsc_ref, first, jnp.sum(dh_v * (xh * gv), axis=0, keepdims=True))
        _acc_rows(dg_ref, first, jnp.sum(dn * xh, axis=0, keepdims=True))

    return pl.pallas_call(body, grid=(s // TR,),
                          in_specs=[_row_spec(), _row_spec(), _row_spec(), _vec_spec(), _vec_spec(k_scale)],
                          out_specs=[_row_spec(), _vec_spec(), _vec_spec(), _vec_spec()],
                          out_shape=[_sds((s, D), F32)] + [_sds((1, D), F32)] * 3,
                          compiler_params=_params(1), name=name)(dh, x, res, g, mod)


def _post_bwd(dx, y, g, mod, k_gate, name):
    s = y.shape[0]

    def body(dx_ref, y_ref, g_ref, gt_ref, dy_ref, dgt_ref, dg_ref):
        first = pl.program_id(0) == 0
        yv, dxv, gv = y_ref[...], dx_ref[...], g_ref[...]
        r = lax.rsqrt(jnp.mean(yv * yv, axis=-1, keepdims=True) + EPS)
        yh = yv * r
        dn = dxv * gt_ref[...]
        dgn = dn * gv
        dy_ref[...] = (r * (dgn - yh * jnp.mean(dgn * yh, axis=-1, keepdims=True))).astype(dy_ref.dtype)
        _acc_rows(dgt_ref, first, jnp.sum(dxv * (yh * gv), axis=0, keepdims=True))
        _acc_rows(dg_ref, first, jnp.sum(dn * yh, axis=0, keepdims=True))

    return pl.pallas_call(body, grid=(s // TR,), in_specs=[_row_spec(), _row_spec(), _vec_spec(), _vec_spec(k_gate)],
                          out_specs=[_row_spec(), _vec_spec(), _vec_spec()],
                          out_shape=[_sds((s, D), BF16), _sds((1, D), F32), _sds((1, D), F32)],
                          compiler_params=_params(1), name=name)(dx, y, g, mod)


SW_TN = 1408
SW_TR = 512
TALL = 1024


def _swiglu_fwd(gu, deps=()):
    s = gu.shape[0]
    nb = FFN // SW_TN

    def body(g_ref, u_ref, *rest):
        a_ref = rest[len(deps)]
        gv = g_ref[...]
        a_ref[...] = (gv * _sig(gv) * u_ref[...]).astype(a_ref.dtype)

    return pl.pallas_call(body, grid=(s // SW_TR, nb),
                          in_specs=[pl.BlockSpec((SW_TR, SW_TN), lambda i, j: (i, j)), pl.BlockSpec((SW_TR, SW_TN), lambda i, j: (i, j + nb))]
                          + [pl.BlockSpec(memory_space=pl.ANY)] * len(deps),
                          out_specs=pl.BlockSpec((SW_TR, SW_TN), lambda i, j: (i, j)), out_shape=_sds((s, FFN), BF16),
                          compiler_params=_params(2, 48 << 20), name="swiglu_fwd")(gu, gu, *deps)


def _swiglu_bwd(dact, gu):
    s = gu.shape[0]
    nb = FFN // SW_TN
    n_steps = (s // SW_TR) * nb

    def body(da_ref, g_ref, u_ref, o_ref, buf, sems):
        i, j = pl.program_id(0), pl.program_id(1)
        step = i * nb + j
        slot = step % 2

        def tiles(sl):
            rows = pl.ds(pl.multiple_of(i * SW_TR, SW_TR), SW_TR)
            return [pltpu.make_async_copy(buf.at[sl, h], o_ref.at[rows, pl.ds(pl.multiple_of((j + nb * h) * SW_TN, LANE), SW_TN)], sems.at[sl, h])
                    for h in range(2)]

        @pl.when(step >= 2)
        def _():
            for cp in tiles(slot):
                cp.wait()

        gv, da = g_ref[...], da_ref[...]
        sg = _sig(gv)
        buf[slot, 0] = (da * u_ref[...] * (sg * (1.0 + gv * (1.0 - sg)))).astype(buf.dtype)
        buf[slot, 1] = (da * (gv * sg)).astype(buf.dtype)
        for cp in tiles(slot):
            cp.start()

        @pl.when(step == n_steps - 1)
        def _():
            for cp in tiles(slot) + (tiles(1 - slot) if n_steps > 1 else []):
                cp.wait()

    blk = lambda f: pl.BlockSpec((SW_TR, SW_TN), f)
    return pl.pallas_call(body, grid=(s // SW_TR, nb),
                          in_specs=[blk(lambda i, j: (i, j)), blk(lambda i, j: (i, j)), blk(lambda i, j: (i, j + nb))],
                          out_specs=pl.BlockSpec(memory_space=pl.ANY), out_shape=_sds((s, 2 * FFN), BF16),
                          scratch_shapes=[pltpu.VMEM((2, 2, SW_TR, SW_TN), BF16), pltpu.SemaphoreType.DMA((2, 2))],
                          compiler_params=_params(2, 48 << 20), name="swiglu_bwd")(dact, gu, gu)


MG_TN = 256


def _merge_fwd(y_a, y_h, proj):
    s = y_a.shape[0]
    tn = MG_TN
    ba, bh = GT_A // tn, GT_H // tn

    def body(ya_ref, yh_ref, ga_ref, gh_ref, m_ref):
        m_ref[...] = (_sig(ga_ref[...]) * ya_ref[...] + _sig(gh_ref[...]) * yh_ref[...]).astype(m_ref.dtype)

    tr = min(s, TALL)
    blk = lambda f: pl.BlockSpec((tr, tn), f)
    return pl.pallas_call(body, grid=(s // tr, D // tn),
                          in_specs=[blk(lambda i, j: (i, j)), blk(lambda i, j: (i, j)), blk(lambda i, j: (i, j + ba)), blk(lambda i, j: (i, j + bh))],
                          out_specs=blk(lambda i, j: (i, j)), out_shape=_sds((s, D), BF16),
                          compiler_params=_params(2), name="merge_fwd")(y_a, y_h, proj, proj)


def _merge_bwd(dm, y_a, y_h, proj):
    s = y_a.shape[0]
    tn = MG_TN
    ba, bh = GT_A // tn, GT_H // tn

    def body(dm_ref, ya_ref, yh_ref, ga_ref, gh_ref, dya_ref, dyh_ref, dga_ref, dgh_ref):
        dmv = dm_ref[...]
        sa, sh = _sig(ga_ref[...]), _sig(gh_ref[...])
        dya_ref[...] = (dmv * sa).astype(BF16)
        dyh_ref[...] = (dmv * sh).astype(BF16)
        dga_ref[...] = (dmv * ya_ref[...] * (sa * (1.0 - sa))).astype(BF16)
        dgh_ref[...] = (dmv * yh_ref[...] * (sh * (1.0 - sh))).astype(BF16)

    tr = min(s, TALL)
    blk = lambda f: pl.BlockSpec((tr, tn), f)
    nat = blk(lambda i, j: (i, j))
    return pl.pallas_call(body, grid=(s // tr, D // tn),
                          in_specs=[nat, nat, nat, blk(lambda i, j: (i, j + ba)), blk(lambda i, j: (i, j + bh))],
                          out_specs=[nat] * 4, out_shape=[_sds((s, D), BF16)] * 4,
                          compiler_params=_params(2), name="merge_bwd")(dm, y_a, y_h, proj, proj)


def _hgout_fwd(o_raw, proj, hg_norm):
    s = o_raw.shape[0]
    bg = G_H // LANE

    def body(o_ref, g_ref, n_ref, out_ref):
        ov = o_ref[...]
        r = lax.rsqrt(jnp.mean(ov * ov, axis=-1, keepdims=True) + EPS)
        out_ref[...] = (ov * r * n_ref[...] * _sig(g_ref[...])).astype(out_ref.dtype)

    tr = min(s, TALL)
    blk = lambda f: pl.BlockSpec((tr, LANE), f)
    return pl.pallas_call(body, grid=(s // tr, HG_HEADS),
                          in_specs=[blk(lambda i, h: (i, h)), blk(lambda i, h: (i, h + bg)), pl.BlockSpec((1, LANE), lambda i, h: (0, 0))],
                          out_specs=blk(lambda i, h: (i, h)), out_shape=_sds((s, HG_W), BF16),
                          compiler_params=_params(2), name="hgout_fwd")(o_raw, proj, hg_norm)


def _hgout_bwd(d_out, o_raw, proj, hg_norm):
    s = o_raw.shape[0]
    bg = G_H // LANE

    def body(d_ref, o_ref, g_ref, n_ref, do_ref, dg_ref, dn_ref):
        first = jnp.logical_and(pl.program_id(0) == 0, pl.program_id(1) == 0)
        ov, dv, nv = o_ref[...], d_ref[...], n_ref[...]
        sg = _sig(g_ref[...])
        r = lax.rsqrt(jnp.mean(ov * ov, axis=-1, keepdims=True) + EPS)
        oh = ov * r
        d_on = dv * sg
        dg_ref[...] = (dv * (oh * nv) * (sg * (1.0 - sg))).astype(dg_ref.dtype)
        t = d_on * nv
        do_ref[...] = r * (t - oh * jnp.mean(t * oh, axis=-1, keepdims=True))
        _acc_rows(dn_ref, first, jnp.sum(d_on * oh, axis=0, keepdims=True))

    tr = min(s, TALL)
    blk = lambda f: pl.BlockSpec((tr, LANE), f)
    vec = pl.BlockSpec((1, LANE), lambda i, h: (0, 0))
    return pl.pallas_call(body, grid=(s // tr, HG_HEADS),
                          in_specs=[blk(lambda i, h: (i, h)), blk(lambda i, h: (i, h)), blk(lambda i, h: (i, h + bg)), vec],
                          out_specs=[blk(lambda i, h: (i, h)), blk(lambda i, h: (i, h)), vec],
                          out_shape=[_sds((s, HG_W), F32), _sds((s, HG_W), BF16), _sds((1, LANE), F32)],
                          compiler_params=_params(2), name="hgout_bwd")(d_out, o_raw, proj, hg_norm)


def _rope(t, cos, s_lo, s_hi):
    return t * cos + pltpu.roll(t, LANE - ROT // 2, 1) * s_lo + pltpu.roll(t, ROT // 2, 1) * s_hi


def _rope_wide(t, cos, s_lo, s_hi):
    return jnp.concatenate([_rope(t[:, k * LANE:(k + 1) * LANE], cos, s_lo, s_hi) for k in range(t.shape[1] // LANE)], axis=1)


def _attn_mask(has_prev):
    kj = lax.broadcasted_iota(jnp.int32, (2 * BLK, BLK), 0)
    qi = lax.broadcasted_iota(jnp.int32, (2 * BLK, BLK), 1)
    rel = BLK + qi - kj
    band = jnp.logical_and(rel >= 0, rel < BLK)
    return jnp.logical_and(band, jnp.logical_or(has_prev, kj >= BLK))


def _attn_specs():
    prev = lambda i: jnp.maximum(i - 1, 0)
    kb, vb = K_A // LANE, V_A // LANE
    blk = lambda f: pl.BlockSpec((BLK, LANE), f)
    tabs = [blk(lambda i: (i, 0))] * 3 + [blk(lambda i: (prev(i), 0))] * 3
    return [pl.BlockSpec((BLK, ATT_W), lambda i: (i, 0)), blk(lambda i: (i, kb)), blk(lambda i: (prev(i), kb)),
            blk(lambda i: (i, vb)), blk(lambda i: (prev(i), vb))] + tabs + [pl.BlockSpec((1, LANE), lambda i: (0, 0))]


def _attn_logits(qh, kg):
    return _dot(kg, qh, NT)


def _attn_probs(raw, mask, sk):
    logits = jnp.where(mask, raw * (HEAD_DIM ** -0.5), -jnp.inf)
    m = jnp.maximum(jnp.max(logits, axis=0, keepdims=True), sk)
    p = jnp.exp(logits - m)
    e_sink = jnp.exp(sk - m)
    inv = 1.0 / (jnp.sum(p, axis=0, keepdims=True) + e_sink)
    return p, inv, e_sink * inv


def _attn_fwd(proj, tabs, sinks):
    s = proj.shape[0]

    def body(q_ref, kc_ref, kp_ref, vc_ref, vp_ref, c0, l0, h0, c1, l1, h1, sk_ref, o_ref):
        i = pl.program_id(0)
        mask = _attn_mask(i > 0)
        q = _rope_wide(q_ref[...], c0[...], l0[...], h0[...]).astype(BF16)
        kk = jnp.concatenate([_rope(kp_ref[...], c1[...], l1[...], h1[...]), _rope(kc_ref[...], c0[...], l0[...], h0[...])], axis=0).astype(BF16)
        v_t = jnp.concatenate([vp_ref[...], vc_ref[...]], axis=0).T.astype(BF16)
        part = lambda t, h: t[:, h * HEAD_DIM:(h + 1) * HEAD_DIM]
        k_heads = [part(kk, g) for g in range(KV_HEADS)]

        def head(h):
            g = h // GROUP
            raw = _attn_logits(part(q, h), k_heads[g])
            yield
            p, inv, _ = _attn_probs(raw, mask, sk_ref[:, h:h + 1])
            yield
            out_t = _dot(v_t[g * HEAD_DIM:(g + 1) * HEAD_DIM], p.astype(BF16), NN)
            yield
            return out_t * inv

        o_ref[...] = jnp.concatenate(_interleave([head(h) for h in range(ATT_HEADS)]), axis=0).T.astype(o_ref.dtype)

    return pl.pallas_call(body, grid=(s // BLK,), in_specs=_attn_specs(),
                          out_specs=pl.BlockSpec((BLK, ATT_W), lambda i: (i, 0)), out_shape=_sds((s, ATT_W), BF16),
                          compiler_params=_params(1), name="attn_fwd")(proj, proj, proj, proj, proj, *tabs, *tabs, sinks)


def _attn_bwd(proj, tabs, sinks, d_att):
    s = proj.shape[0]

    def body(q_ref, kc_ref, kp_ref, vc_ref, vp_ref, c0, l0, h0, c1, l1, h1, sk_ref, do_ref, dq_ref, dk_ref, dv_ref, ds_ref):
        i = pl.program_id(0)

        @pl.when(i == 0)
        def _():
            dk_ref[...] = jnp.zeros_like(dk_ref)
            dv_ref[...] = jnp.zeros_like(dv_ref)
            ds_ref[...] = jnp.zeros_like(ds_ref)

        mask = _attn_mask(i > 0)
        q = _rope_wide(q_ref[...], c0[...], l0[...], h0[...]).astype(BF16)
        kk = jnp.concatenate([_rope(kp_ref[...], c1[...], l1[...], h1[...]), _rope(kc_ref[...], c0[...], l0[...], h0[...])], axis=0).astype(BF16)
        k_f32 = jnp.concatenate([_rope(kp_ref[...], c1[...], l1[...], h1[...]), _rope(kc_ref[...], c0[...], l0[...], h0[...])], axis=0)
        k_t = k_f32.T.astype(BF16)
        vv = jnp.concatenate([vp_ref[...], vc_ref[...]], axis=0).astype(BF16)
        d_o = do_ref[...].astype(BF16)
        lane = lax.broadcasted_iota(jnp.int32, (1, LANE), 1)
        part = lambda t, h: t[:, h * HEAD_DIM:(h + 1) * HEAD_DIM]
        k_heads = [part(kk, g) for g in range(KV_HEADS)]
        v_heads = [part(vv, g) for g in range(KV_HEADS)]

        def head(h):
            g = h // GROUP
            qh, doh = part(q, h), part(d_o, h)
            raw = _attn_logits(qh, k_heads[g])
            d_p = _dot(v_heads[g], doh, NT)
            yield
            p, inv, p_sink = _attn_probs(raw, mask, sk_ref[:, h:h + 1])
            prob = p * inv
            dv = _dot(prob.astype(BF16), doh, NN)
            yield
            dd = jnp.sum(prob * d_p, axis=0, keepdims=True)
            d_s = (prob * (d_p - dd)).astype(BF16)
            d_sink = jnp.where(lane == h, -jnp.sum(p_sink * dd, axis=1, keepdims=True), 0.0)
            dq_t = _dot(k_t[g * HEAD_DIM:(g + 1) * HEAD_DIM], d_s, NN)
            dk = _dot(d_s, qh, NN)
            yield
            return dq_t * (HEAD_DIM ** -0.5), dk * (HEAD_DIM ** -0.5), dv, d_sink

        per_head = _interleave([head(h) for h in range(ATT_HEADS)])
        dqs = [jnp.concatenate([t[0] for t in per_head], axis=0).T]
        group_sum = lambda k, g: functools.reduce(jnp.add, [t[k] for t in per_head[g * GROUP:(g + 1) * GROUP]])
        dks = [group_sum(1, g) for g in range(KV_HEADS)]
        dvs = [group_sum(2, g) for g in range(KV_HEADS)]
        d_sink = functools.reduce(jnp.add, [t[3] for t in per_head])
        dq_ref[...] = _rope_wide(jnp.concatenate(dqs, axis=1), c0[...], -l0[...], -h0[...]).astype(dq_ref.dtype)
        d_k = jnp.concatenate(dks, axis=1)
        d_v = jnp.concatenate(dvs, axis=1)
        cur = pl.ds(pl.multiple_of(i * BLK, BLK), BLK)
        prv = pl.ds(pl.multiple_of(jnp.maximum(i - 1, 0) * BLK, BLK), BLK)
        dk_ref[prv, :] += _rope(d_k[:BLK], c1[...], -l1[...], -h1[...])
        dk_ref[cur, :] += _rope(d_k[BLK:], c0[...], -l0[...], -h0[...])
        dv_ref[prv, :] += d_v[:BLK]
        dv_ref[cur, :] += d_v[BLK:]
        ds_ref[...] += d_sink

    full = pl.BlockSpec((s, LANE), lambda i: (0, 0))
    return pl.pallas_call(body, grid=(s // BLK,), in_specs=_attn_specs() + [pl.BlockSpec((BLK, ATT_W), lambda i: (i, 0))],
                          out_specs=[pl.BlockSpec((BLK, ATT_W), lambda i: (i, 0)), full, full, pl.BlockSpec((1, LANE), lambda i: (0, 0))],
                          out_shape=[_sds((s, ATT_W), BF16), _sds((s, LANE), F32), _sds((s, LANE), F32), _sds((1, LANE), F32)],
                          compiler_params=_params(1), name="attn_bwd")(proj, proj, proj, proj, proj, *tabs, *tabs, sinks, d_att)


def _tri_matmul(tri, t):
    hi = t.astype(BF16)
    r1 = t - hi.astype(F32)
    mid = r1.astype(BF16)
    lo = (r1 - mid.astype(F32)).astype(BF16)
    return _dot(tri, hi, NN) + _dot(tri, mid, NN) + _dot(tri, lo, NN)


def _lower_bound(hl):
    a, b = hl[0:1, :], hl[1:2, :]
    mx = jnp.maximum(a, b)
    ea, eb = jnp.exp(a - mx), jnp.exp(b - mx)
    return ea / (ea + eb)


def _hg_gates(q_raw, f_raw, lb, tri_lower):
    sg = _sig(f_raw)
    f = lb + (1.0 - lb) * sg
    sq = _sig(q_raw)
    b = _tri_matmul(tri_lower, jnp.log(f))
    return sg, f, 1.0 - f, sq, q_raw * sq, b


HG_PAIR_FWD = 8
HG_PAIR_BWD = 8


def _hg_specs(n_map, pair):
    blk = lambda off, p: pl.BlockSpec((HG_TB, LANE), lambda h, n: (n_map(n), off // LANE + pair * h + p))
    return [blk(off, p) for off in (Q_H, F_H, I_H) for p in range(pair)] + [pl.BlockSpec((2, pair * LANE), lambda h, n: (0, h))]


def _interleave(gens):
    out = [None] * len(gens)
    live = list(range(len(gens)))
    while live:
        for k in list(live):
            try:
                next(gens[k])
            except StopIteration as stop:
                out[k] = stop.value
                live.remove(k)
    return out


def _hg_spread():
    c = lax.broadcasted_iota(jnp.int32, (CHUNK, SUB * SUB), 0)
    l = lax.broadcasted_iota(jnp.int32, (CHUNK, SUB * SUB), 1)
    r = lax.broadcasted_iota(jnp.int32, (SUB, SUB * SUB), 0)
    lr = lax.broadcasted_iota(jnp.int32, (SUB, SUB * SUB), 1)
    shift = SUB.bit_length() - 1
    cols = [(c == lo + (l >> shift)).astype(BF16) for lo in range(0, CHUNK, SUB)]
    tile = [(c == lo + (l & (SUB - 1))).astype(BF16) for lo in range(0, CHUNK, SUB)]
    return cols, tile, (lr & (SUB - 1)) == r, (lr >> shift) == r


def _hg_intra(qs, kk, b, grad=None):
    lane = lax.broadcasted_iota(jnp.int32, (SUB, CHUNK), 1)
    row1 = lax.broadcasted_iota(jnp.int32, (SUB, 1), 0)
    kk_b = kk.astype(BF16)
    if grad is not None:
        d_a, d_at, (cols, tile, diag, block) = grad
    a_blocks, dq_blocks, dk_blocks, db_blocks = [], [], [], []
    dk_left = None
    for j in range(CHUNK // SUB):
        lo = j * SUB
        q_j, k_j, b_j = qs[lo:lo + SUB], kk[lo:lo + SUB], b[lo:lo + SUB]
        es = [jnp.where(row1 >= sx, jnp.exp(jnp.minimum(b_j - b_j[sx:sx + 1], 0.0)), 0.0) for sx in range(SUB)]
        pes = [q_j * e for e in es]
        pe = jnp.concatenate(pes, axis=0).astype(BF16)
        pairs = _dot(pe, kk_b, NT)
        yield
        a_j = jnp.zeros((SUB, CHUNK), F32)
        for sx in range(SUB):
            a_j = jnp.where(lane == lo + sx, pairs[sx * SUB:(sx + 1) * SUB], a_j)
        if grad is not None:
            da_j = d_a[lo:lo + SUB]
            ek = jnp.concatenate([e * k_j[sx:sx + 1] for sx, e in enumerate(es)], axis=0).astype(BF16)
            sel_t = jnp.where(diag, _dot(da_j.astype(BF16), cols[j], NN), 0.0).astype(BF16)
            sel_s = jnp.where(block, _dot(d_at[lo:lo + SUB].astype(BF16), tile[j], NN), 0.0).astype(BF16)
            pek = jnp.concatenate([p * k_j[sx:sx + 1] for sx, p in enumerate(pes)], axis=0).astype(BF16)
            yield
            dq_j = _dot(sel_t, ek, NN)
            dk_j = _dot(sel_s, pe, NN)
            db_j = _dot(sel_t, pek, NN) - _dot(sel_s, pek, NN)
            yield
        if j > 0:
            ref = b[lo - 1:lo]
            sc_q = jnp.exp(b_j - ref)
            sc_k = jnp.exp(jnp.minimum(ref - b, 0.0))
            qt = (q_j * sc_q).astype(BF16)
            kt = (kk * sc_k).astype(BF16)
            left = _dot(qt, kt, NT)
            yield
            a_j = a_j + jnp.where(lane < lo, left, 0.0)
            if grad is not None:
                da_left = jnp.where(lane < lo, da_j, 0.0).astype(BF16)
                dq_left = _dot(da_left, kt, NN) * sc_q
                dq_j = dq_j + dq_left
                db_j = db_j + q_j * dq_left
                t = _dot(da_left, qt, TN)
                yield
                t = t * sc_k
                dk_left = t if dk_left is None else dk_left + t
        a_blocks.append(a_j)
        if grad is not None:
            dq_blocks.append(dq_j)
            dk_blocks.append(dk_j)
            db_blocks.append(db_j)
    a = jnp.concatenate(a_blocks, axis=0)
    if grad is None:
        return a
    return a, jnp.concatenate(dq_blocks, axis=0), jnp.concatenate(dk_blocks, axis=0) + dk_left, jnp.concatenate(db_blocks, axis=0) - kk * dk_left


def _hgrn_fwd(proj, hl):
    s = proj.shape[0]
    n_chunk = HG_TB // CHUNK
    pair = HG_PAIR_FWD

    def body(*refs):
        q_refs, f_refs, i_refs = refs[:pair], refs[pair:2 * pair], refs[2 * pair:3 * pair]
        hl_ref, o_ref, st_out_ref, st_ref = refs[3 * pair:]

        @pl.when(pl.program_id(1) == 0)
        def _():
            st_ref[...] = jnp.zeros_like(st_ref)

        r_i = lax.broadcasted_iota(jnp.int32, (CHUNK, CHUNK), 0)
        c_i = lax.broadcasted_iota(jnp.int32, (CHUNK, CHUNK), 1)
        tri_lower = (r_i >= c_i).astype(BF16)

        def chunk(c, carry):
            rows = pl.ds(pl.multiple_of(c * CHUNK, CHUNK), CHUNK)
            def head(p):
                cols = slice(p * LANE, (p + 1) * LANE)
                lb = _lower_bound(hl_ref[:, cols])
                v = i_refs[p][rows, :].astype(BF16)
                _, _, kk, _, qs, b = _hg_gates(q_refs[p][rows, :], f_refs[p][rows, :], lb, tri_lower)
                yield
                st = st_ref[p]
                st_b = st.astype(BF16)
                st_out_ref[p, c] = st_b
                o_state = _dot((qs * jnp.exp(b)).astype(BF16), st_b, NT)
                b_last = b[CHUNK - 1:CHUNK, :]
                st_new = _dot(v, (kk * jnp.exp(b_last - b)).astype(BF16), TN)
                a = yield from _hg_intra(qs, kk, b)
                st_ref[p] = st * jnp.exp(b_last) + st_new
                o_ref[rows, cols] = o_state + _dot(a.astype(BF16), v, NN)

            _interleave([head(p) for p in range(pair)])
            return carry

        lax.fori_loop(0, n_chunk, chunk, 0)

    return pl.pallas_call(
        body, grid=(HG_HEADS // pair, s // HG_TB), in_specs=_hg_specs(lambda n: n, pair),
        out_specs=[pl.BlockSpec((HG_TB, pair * LANE), lambda h, n: (n, h)), pl.BlockSpec((pair, n_chunk, HG_K, HG_K), lambda h, n: (h, n, 0, 0))],
        out_shape=[_sds((s, HG_W), F32), _sds((HG_HEADS, s // CHUNK, HG_K, HG_K), BF16)],
        scratch_shapes=[pltpu.VMEM((pair, HG_K, HG_K), F32)],
        compiler_params=_params(2), name="hgrn_fwd")(*[proj] * (3 * pair), hl)


def _hgrn_bwd(proj, hl, states, d_o):
    s = proj.shape[0]
    n_chunk = HG_TB // CHUNK
    n_blk = s // HG_TB
    pair = HG_PAIR_BWD
    rev = lambda n: n_blk - 1 - n

    def body(*refs):
        q_refs, f_refs, i_refs = refs[:pair], refs[pair:2 * pair], refs[2 * pair:3 * pair]
        hl_ref, st_in_ref, do_ref, dq_ref, df_ref, di_ref, dhl_ref, dst_ref, dlb_ref = refs[3 * pair:]
        n = pl.program_id(1)

        @pl.when(n == 0)
        def _():
            dst_ref[...] = jnp.zeros_like(dst_ref)
            dlb_ref[...] = jnp.zeros_like(dlb_ref)

        r_i = lax.broadcasted_iota(jnp.int32, (CHUNK, CHUNK), 0)
        c_i = lax.broadcasted_iota(jnp.int32, (CHUNK, CHUNK), 1)
        tri_lower = (r_i >= c_i).astype(BF16)
        tri_upper = (r_i <= c_i).astype(BF16)
        row = lax.broadcasted_iota(jnp.int32, (CHUNK, 1), 0)
        spread = _hg_spread()

        def chunk(cc, carry):
            c = n_chunk - 1 - cc
            rows = pl.ds(pl.multiple_of(c * CHUNK, CHUNK), CHUNK)
            def head(p):
                cols = slice(p * LANE, (p + 1) * LANE)
                lb = _lower_bound(hl_ref[:, cols])
                q_raw = q_refs[p][rows, :]
                vb = i_refs[p][rows, :].astype(BF16)
                sg, f, kk, sq, qs, b = _hg_gates(q_raw, f_refs[p][rows, :], lb, tri_lower)
                yield
                e_b = jnp.exp(b)
                qe = qs * e_b
                b_last = b[CHUNK - 1:CHUNK, :]
                e_last = jnp.exp(b_last)
                e_kd = jnp.exp(b_last - b)
                kd = kk * e_kd
                st0 = st_in_ref[p, c]
                d_ob = do_ref[rows, cols].astype(BF16)
                dst = dst_ref[p]
                dst_b = dst.astype(BF16)
                d_a = jnp.where(r_i >= c_i, _dot(d_ob, vb, NT), 0.0)
                d_at = jnp.where(r_i <= c_i, _dot(vb, d_ob, NT), 0.0)
                d_v_st = _dot(kd.astype(BF16), dst_b, NT)
                d_kd = _dot(vb, dst_b, NN)
                d_qe = _dot(d_ob, st0, NN)
                dst_new = _dot(d_ob, qe.astype(BF16), TN)
                yield
                a, dqs, dkk, d_b = yield from _hg_intra(qs, kk, b, (d_a, d_at, spread))
                d_v = _dot(a.astype(BF16), d_ob, TN) + d_v_st
                dqs_st = d_qe * e_b
                dkk_st = d_kd * e_kd
                dqs = dqs + dqs_st
                dkk = dkk + dkk_st
                d_b_last = jnp.sum(d_kd * kd, axis=0, keepdims=True) + jnp.sum(dst * st0.astype(F32), axis=0, keepdims=True) * e_last
                d_b = d_b + qs * dqs_st - kk * dkk_st + jnp.where(row == CHUNK - 1, d_b_last, 0.0)
                d_g = _tri_matmul(tri_upper, d_b)
                dst_ref[p] = dst_new + dst * e_last
                yield
                d_f = d_g / f - dkk
                dlb_ref[:, cols] += jnp.sum(d_f * (1.0 - sg), axis=0, keepdims=True)
                dq_ref[rows, cols] = (dqs * (sq * (1.0 + q_raw * (1.0 - sq)))).astype(dq_ref.dtype)
                df_ref[rows, cols] = (d_f * (1.0 - lb) * (sg * (1.0 - sg))).astype(df_ref.dtype)
                di_ref[rows, cols] = d_v.astype(di_ref.dtype)

            _interleave([head(p) for p in range(pair)])
            return carry

        lax.fori_loop(0, n_chunk, chunk, 0)

        @pl.when(n == n_blk - 1)
        def _():
            lb = _lower_bound(hl_ref[...])
            d_hl0 = dlb_ref[...] * (lb * (1.0 - lb))
            dhl_ref[...] = jnp.concatenate([d_hl0, -d_hl0], axis=0)

    out_blk = pl.BlockSpec((HG_TB, pair * LANE), lambda h, n: (rev(n), h))
    return pl.pallas_call(
        body, grid=(HG_HEADS // pair, n_blk),
        in_specs=_hg_specs(rev, pair) + [pl.BlockSpec((pair, n_chunk, HG_K, HG_K), lambda h, n: (h, rev(n), 0, 0)), out_blk],
        out_specs=[out_blk, out_blk, out_blk, pl.BlockSpec((2, pair * LANE), lambda h, n: (0, h))],
        out_shape=[_sds((s, HG_W), BF16)] * 3 + [_sds((2, HG_W), F32)],
        scratch_shapes=[pltpu.VMEM((pair, HG_K, HG_K), F32), pltpu.VMEM((1, pair * LANE), F32)],
        compiler_params=_params(2), name="hgrn_bwd")(*[proj] * (3 * pair), hl, states, d_o)


def _mod_part(c_all, w_shard, b_shard):
    n = w_shard.shape[1]
    tn = 512

    def body(c_ref, w_ref, b_ref, o_ref):
        o_ref[...] = _dot(c_ref[...].astype(BF16), w_ref[...].astype(BF16), NN) + b_ref[...]

    return pl.pallas_call(body, grid=(n // tn,),
                          in_specs=[pl.BlockSpec((N_DEV, D), lambda j: (0, 0)), pl.BlockSpec((D, tn), lambda j: (0, j)), pl.BlockSpec((1, tn), lambda j: (0, j))],
                          out_specs=pl.BlockSpec((N_DEV, tn), lambda j: (0, j)), out_shape=_sds((N_DEV, n), F32),
                          compiler_params=_params(1, 32 << 20), name="mod_part")(c_all, w_shard, b_shard)


def _grad_w_ada(c_all_t, dmod_cols):
    n = dmod_cols.shape[1]
    tn = 512

    def body(c_ref, d_ref, o_ref):
        cv = c_ref[...].astype(BF16).astype(F32)
        dv = d_ref[...].astype(BF16).astype(F32)
        acc = cv[:, 0:1] * dv[0:1, :]
        for k in range(1, N_DEV):
            acc = acc + cv[:, k:k + 1] * dv[k:k + 1, :]
        o_ref[...] = acc

    return pl.pallas_call(body, grid=(n // tn,),
                          in_specs=[pl.BlockSpec((D, N_DEV), lambda j: (0, 0)), pl.BlockSpec((N_DEV, tn), lambda j: (0, j))],
                          out_specs=pl.BlockSpec((D, tn), lambda j: (0, j)), out_shape=_sds((D, n), F32),
                          compiler_params=_params(1, 32 << 20), name="grad_w_ada")(c_all_t, dmod_cols)


def _row_tile(r, c, max_elems=1 << 18):
    if r * c <= max_elems or r % 8:
        return r
    best = 8
    for t in range(8, r + 1, 8):
        if r % t == 0 and t * c <= max_elems:
            best = t
    return best


WIDE_TILE = 5 << 17
PAIR_TILE = 3 << 19


def _adamw(pieces, w, m, v, name, emit_grad=True, own=None, max_elems=1 << 18):
    p, r, c = pieces.shape
    tr = _row_tile(r, c, max_elems)
    c1 = 1.0 / (1.0 - ADAM_B1 ** ADAM_STEP)
    c2 = 1.0 / (1.0 - ADAM_B2 ** ADAM_STEP)

    def body(*refs):
        if own is None:
            p_ref, w_ref, m_ref, v_ref, *outs = refs
            g = p_ref[0].astype(F32)
        else:
            o_ref, p_ref, w_ref, m_ref, v_ref, *outs = refs
            g = o_ref[...].astype(F32) + p_ref[0].astype(F32)
        for k in range(1, p):
            g = g + p_ref[k].astype(F32)
        m2 = ADAM_B1 * m_ref[...] + (1.0 - ADAM_B1) * g
        v2 = ADAM_B2 * v_ref[...] + (1.0 - ADAM_B2) * (g * g)
        delta = -ADAM_LR * ((m2 * c1) / (jnp.sqrt(v2 * c2) + ADAM_EPS) + ADAM_WD * w_ref[...])
        if emit_grad:
            outs[0][...] = g
        outs[-3][...] = delta
        outs[-2][...] = m2
        outs[-1][...] = v2

    blk = pl.BlockSpec((tr, c), lambda i: (i, 0))
    n_out = 4 if emit_grad else 3
    lead = [] if own is None else [own]
    return pl.pallas_call(body, grid=(r // tr,), in_specs=[blk] * len(lead) + [pl.BlockSpec((p, tr, c), lambda i: (0, i, 0)), blk, blk, blk],
                          out_specs=[blk] * n_out, out_shape=[_sds((r, c), F32)] * n_out,
                          compiler_params=_params(1, 48 << 20), name=name)(*lead, pieces, w, m, v)


def _my_coords():
    return lax.axis_index("x"), lax.axis_index("y"), lax.axis_index("c")


def _flip(coords, k):
    x, y, c = coords
    return (1 - x if k & 4 else x, 1 - y if k & 2 else y, 1 - c if k & 1 else c)


def _lin(coords):
    return 4 * coords[0] + 2 * coords[1] + coords[2]


def _exchange_small(x3, bcast, name):
    n = x3.shape[2]

    def body(x_ref, o_ref, send_sems, recv_sems):
        me = _my_coords()
        my_id = _lin(me)
        o_ref[pl.ds(my_id, 1)] = x_ref[pl.ds(0 if bcast else my_id, 1)]
        copies = []
        for k in range(1, N_DEV):
            peer = _flip(me, k)
            src = x_ref.at[0 if bcast else _lin(peer)]
            cp = pltpu.make_async_remote_copy(src_ref=src, dst_ref=o_ref.at[my_id], send_sem=send_sems.at[k], recv_sem=recv_sems.at[k],
                                              device_id=peer, device_id_type=MESH)
            cp.start()
            copies.append(cp)
        for k in range(1, N_DEV):
            peer = _flip(me, k)
            pltpu.make_async_remote_copy(src_ref=x_ref.at[0], dst_ref=o_ref.at[_lin(peer)], send_sem=send_sems.at[k], recv_sem=recv_sems.at[k],
                                         device_id=peer, device_id_type=MESH).wait_recv()
        for cp in copies:
            cp.wait_send()

    vm = pl.BlockSpec(memory_space=pltpu.VMEM)
    return pl.pallas_call(body, in_specs=[vm], out_specs=vm, out_shape=_sds((N_DEV, 1, n), F32),
                          scratch_shapes=[pltpu.SemaphoreType.DMA((N_DEV,)), pltpu.SemaphoreType.DMA((N_DEV,))], name=name)(x3)


HBM_SPEC = pl.BlockSpec(memory_space=pltpu.HBM)
SEM_SPEC = pl.BlockSpec(memory_space=pltpu.SEMAPHORE)
ANY_SPEC = pl.BlockSpec(memory_space=pl.ANY)
DATAFLOW = pltpu.SideEffectType.DATAFLOW_SIDE_EFFECTING
GATHER_FLIPS = (1, 2, 4, 6)
PASS_FLIPS = (2, 4, 6)
TOKEN = (8, LANE)


def _hbm(t):
    return pltpu.with_memory_space_constraint(t, pltpu.HBM)


def _hbm_like(ts):
    return [pltpu.HBM(t.shape, t.dtype) for t in ts]


def _split_start(issue, srcs, lands, n_sem, name, deps=()):
    n, nb, nd = len(srcs), len(srcs) + len(lands), len(deps)

    def body(*refs):
        issue(refs[:n], refs[n:nb], refs[nb + nd], refs[nb + nd + 1])
        refs[-1][...] = jnp.zeros(TOKEN, F32)

    outs = pl.pallas_call(
        body, name=name,
        out_shape=(pltpu.SemaphoreType.DMA((n_sem,)), pltpu.SemaphoreType.DMA((n_sem,)), *_hbm_like(srcs), *_hbm_like(lands), _sds(TOKEN, F32)),
        in_specs=[HBM_SPEC] * nb + [ANY_SPEC] * nd,
        out_specs=(SEM_SPEC, SEM_SPEC, *[HBM_SPEC] * nb, pl.BlockSpec(memory_space=pltpu.VMEM)),
        input_output_aliases={i: 2 + i for i in range(nb)},
        compiler_params=pltpu.CompilerParams(has_side_effects=DATAFLOW))(*[_hbm(t) for t in srcs], *[_hbm(t) for t in lands], *deps)
    return dict(sems=outs[:2], thru=list(outs[2:2 + nb]), token=outs[-1], n=n)


def _split_wait(finish, handle, after, name):
    n = handle["n"]
    thru = handle["thru"]
    nb = len(thru)

    def body(*refs):
        finish(refs[:n], refs[n:nb], refs[nb], refs[nb + 1])

    outs = pl.pallas_call(
        body, name=name, out_shape=_hbm_like(thru), in_specs=[HBM_SPEC] * nb + [SEM_SPEC, SEM_SPEC] + [ANY_SPEC] * len(after),
        out_specs=[HBM_SPEC] * nb, input_output_aliases={i: i for i in range(nb)},
        compiler_params=pltpu.CompilerParams(has_side_effects=DATAFLOW))(*thru, *handle["sems"], *after)
    return list(outs[:n]), list(outs[n:])


def _gather_start(shards, name, deps=()):
    n = len(shards)
    my_id = _lin(_my_coords())
    lands = [lax.dynamic_update_slice(lax.empty((N_DEV,) + t.shape, t.dtype), t[None], (my_id, 0, 0)) for t in shards]

    def issue(src, land, send_sems, recv_sems):
        me = _my_coords()
        for w in range(n):
            for j, k in enumerate(GATHER_FLIPS):
                q = len(GATHER_FLIPS) * w + j
                pltpu.make_async_remote_copy(src_ref=src[w], dst_ref=land[w].at[_lin(me)], send_sem=send_sems.at[q], recv_sem=recv_sems.at[q],
                                             device_id=_flip(me, k), device_id_type=MESH).start()

    return _split_start(issue, shards, lands, len(GATHER_FLIPS) * n, name, deps)


def _gather_wait(handle, after, name):
    n = handle["n"]

    def finish(src, land, send_sems, recv_sems):
        me = _my_coords()
        for w in range(n):
            for j, k in enumerate(GATHER_FLIPS):
                q = len(GATHER_FLIPS) * w + j
                peer = _flip(me, k)
                cp = pltpu.make_async_remote_copy(src_ref=src[w], dst_ref=land[w].at[_lin(peer)], send_sem=send_sems.at[q], recv_sem=recv_sems.at[q],
                                                  device_id=peer, device_id_type=MESH)
                cp.wait_send()
                cp.wait_recv()

    return _split_wait(finish, handle, after, name)[1]


def _pass_copy(land, send_sems, recv_sems, w, j, arriving):
    me = _my_coords()
    blk = land[w].at[_lin(_flip(me, PASS_FLIPS[j] + (1 if arriving else 0)))]
    q = len(PASS_FLIPS) * w + j
    return pltpu.make_async_remote_copy(src_ref=blk, dst_ref=blk, send_sem=send_sems.at[q], recv_sem=recv_sems.at[q],
                                        device_id=_flip(me, 1), device_id_type=MESH)


def _pass_start(lands, name, deps=()):
    def issue(_, land, send_sems, recv_sems):
        for w in range(len(lands)):
            for j in range(len(PASS_FLIPS)):
                _pass_copy(land, send_sems, recv_sems, w, j, False).start()

    return _split_start(issue, [], lands, len(PASS_FLIPS) * len(lands), name, deps)


def _pass_wait(handle, after, name):
    def finish(_, land, send_sems, recv_sems):
        for w in range(len(handle["thru"])):
            for j in range(len(PASS_FLIPS)):
                _pass_copy(land, send_sems, recv_sems, w, j, False).wait_send()
                _pass_copy(land, send_sems, recv_sems, w, j, True).wait_recv()

    return _split_wait(finish, handle, after, name)[1]


def _gather_pass(lands, name):
    n = len(lands)
    n_p = len(PASS_FLIPS)

    def body(*refs):
        land = refs[n:2 * n]
        send_sems, recv_sems = refs[2 * n:]
        me = _my_coords()
        sibling = _flip(me, 1)
        sent = []
        for w in range(n):
            for j, k in enumerate(PASS_FLIPS):
                blk = land[w].at[_lin(_flip(me, k))]
                cp = pltpu.make_async_remote_copy(src_ref=blk, dst_ref=blk, send_sem=send_sems.at[n_p * w + j], recv_sem=recv_sems.at[n_p * w + j],
                                                  device_id=sibling, device_id_type=MESH)
                cp.start()
                sent.append(cp)
        for w in range(n):
            for j, k in enumerate(PASS_FLIPS):
                blk = land[w].at[_lin(_flip(me, k + 1))]
                pltpu.make_async_remote_copy(src_ref=blk, dst_ref=blk, send_sem=send_sems.at[n_p * w + j], recv_sem=recv_sems.at[n_p * w + j],
                                             device_id=sibling, device_id_type=MESH).wait_recv()
        for cp in sent:
            cp.wait_send()

    return pl.pallas_call(body, in_specs=[ANY_SPEC] * n, out_specs=[ANY_SPEC] * n, out_shape=[_sds(t.shape, t.dtype) for t in lands],
                          input_output_aliases={i: i for i in range(n)},
                          scratch_shapes=[pltpu.SemaphoreType.DMA((n_p * n,)), pltpu.SemaphoreType.DMA((n_p * n,))], name=name)(*lands)


CHIP_FLIPS = (0, 2, 4, 6)


def _pair_copy(src, land, send_sems, recv_sems, w, j):
    me = _my_coords()
    q = len(CHIP_FLIPS) * w + j
    return pltpu.make_async_remote_copy(src_ref=src[w].at[_lin(_flip(me, CHIP_FLIPS[j] + 1))], dst_ref=land[w].at[j], send_sem=send_sems.at[q],
                                        recv_sem=recv_sems.at[q], device_id=_flip(me, 1), device_id_type=MESH)


def _pair_exchange(grads, name):
    n = len(grads)

    def body(*refs):
        src, land = refs[:n], refs[n:2 * n]
        send_sems, recv_sems = refs[2 * n:]
        sent = [_pair_copy(src, land, send_sems, recv_sems, w, j) for w in range(n) for j in range(len(CHIP_FLIPS))]
        for cp in sent:
            cp.start()
        for cp in sent:
            cp.wait_recv()
        for cp in sent:
            cp.wait_send()

    outs = pl.pallas_call(body, in_specs=[ANY_SPEC] * n, out_specs=[ANY_SPEC] * n,
                          out_shape=[_sds((len(CHIP_FLIPS),) + g.shape[1:], g.dtype) for g in grads],
                          scratch_shapes=[pltpu.SemaphoreType.DMA((len(CHIP_FLIPS) * n,))] * 2, name=name)(*grads)
    return list(outs)


def _pair_start(grads, name, deps=()):
    n = len(grads)
    lands = [lax.empty((len(CHIP_FLIPS),) + g.shape[1:], g.dtype) for g in grads]

    def issue(src, land, send_sems, recv_sems):
        for w in range(n):
            for j in range(len(CHIP_FLIPS)):
                _pair_copy(src, land, send_sems, recv_sems, w, j).start()

    return _split_start(issue, grads, lands, len(CHIP_FLIPS) * n, name, deps)


def _pair_wait(handle, after, name):
    n = handle["n"]

    def finish(src, land, send_sems, recv_sems):
        for w in range(n):
            for j in range(len(CHIP_FLIPS)):
                cp = _pair_copy(src, land, send_sems, recv_sems, w, j)
                cp.wait_send()
                cp.wait_recv()

    return _split_wait(finish, handle, after, name)


def _pair_add(grad, theirs, name):
    p, r, c = theirs.shape
    tr = _row_tile(r, c, PAIR_TILE)
    me = _my_coords()
    ids = jnp.stack([_lin(_flip(me, k)) for k in CHIP_FLIPS]).astype(jnp.int32)

    def body(ids_ref, a_ref, b_ref, o_ref):
        o_ref[...] = (a_ref[...].astype(F32) + b_ref[...].astype(F32)).astype(o_ref.dtype)

    blk = pl.BlockSpec((None, tr, c), lambda j, i, ids_ref: (j, i, 0))
    return pl.pallas_call(
        body, out_shape=_sds((p, r, c), theirs.dtype), compiler_params=_params(2), name=name,
        grid_spec=pltpu.PrefetchScalarGridSpec(
            num_scalar_prefetch=1, grid=(p, r // tr),
            in_specs=[pl.BlockSpec((None, tr, c), lambda j, i, ids_ref: (ids_ref[j], i, 0)), blk], out_specs=blk))(ids, grad, theirs)


def _chips_start(parts, name, deps=()):
    n = len(parts)
    n_c = len(CHIP_FLIPS) - 1
    lands = [lax.empty((n_c,) + t.shape[1:], t.dtype) for t in parts]

    def issue(src, land, send_sems, recv_sems):
        me = _my_coords()
        for w in range(n):
            for j in range(1, n_c + 1):
                q = n_c * w + j - 1
                pltpu.make_async_remote_copy(src_ref=src[w].at[j], dst_ref=land[w].at[j - 1], send_sem=send_sems.at[q], recv_sem=recv_sems.at[q],
                                             device_id=_flip(me, CHIP_FLIPS[j]), device_id_type=MESH).start()

    return _split_start(issue, parts, lands, n_c * n, name, deps)


def _chips_wait(handle, after, name):
    n = handle["n"]
    n_c = len(CHIP_FLIPS) - 1

    def finish(src, land, send_sems, recv_sems):
        me = _my_coords()
        for w in range(n):
            for j in range(1, n_c + 1):
                q = n_c * w + j - 1
                cp = pltpu.make_async_remote_copy(src_ref=src[w].at[j], dst_ref=land[w].at[j - 1], send_sem=send_sems.at[q], recv_sem=recv_sems.at[q],
                                                  device_id=_flip(me, CHIP_FLIPS[j]), device_id_type=MESH)
                cp.wait_send()
                cp.wait_recv()

    return _split_wait(finish, handle, after, name)


def _after(t, *tokens):
    for tok in tokens:
        t = t + tok[0:1, 0:1]
    return t


def _rope_tables(positions):
    half = ROT // 2
    inv_freq = ROPE_THETA ** (-jnp.arange(0, ROT, 2, dtype=F32) / ROT)
    ang = positions.astype(F32).reshape(-1, 1) * inv_freq
    cos, sin = jnp.cos(ang), jnp.sin(ang)
    s = ang.shape[0]
    pad = jnp.zeros((s, HEAD_DIM - ROT), F32)
    zero = jnp.zeros((s, half), F32)
    two = lambda t: jnp.concatenate([t, t], axis=1)
    return (two(jnp.concatenate([cos, cos, pad + 1.0], axis=1)), two(jnp.concatenate([-sin, zero, pad], axis=1)),
            two(jnp.concatenate([zero, sin, pad], axis=1)))


def _local_step(x, tgt, tabs, mod, sinks_pad, hl, hg_norm, g_pre_mix, g_post_mix, g_pre_ffn, g_post_ffn, weights, prefetch, scatter, scatter_on):
    s = x.shape[0]
    h1 = _pre_fwd(x, g_pre_mix, mod, 1, 0, "pre_mix_fwd")
    (w_in_a,) = weights("in_a", h1)
    proj = _mm_nt(h1[:, :D // 2], w_in_a, 256, IN_COLS // 2, D // 2, F32, "proj_mm_a")
    (w_in_b,) = weights("in_b", proj)
    proj = _mm_nt(h1[:, D // 2:], w_in_b, 256, IN_COLS // 2, D // 2, F32, "proj_mm_b", add=proj)
    att = _attn_fwd(proj, tabs, _after(sinks_pad, prefetch("mix", proj)))
    o_raw, states = _hgrn_fwd(proj, hl)
    ohg = _hgout_fwd(o_raw, proj, hg_norm)
    w_attn_dm, w_hgrn_dm, w_out = weights("mix", ohg)
    natural = lambda w_dm: w_dm.transpose(1, 0, 2).reshape(w_dm.shape[1], D)
    pieces = lambda g: g.reshape(g.shape[0], N_DEV, D // N_DEV).transpose(1, 0, 2)
    w_attn, w_hgrn = natural(w_attn_dm), natural(w_hgrn_dm)
    y_a = _mm_nn(att, w_attn, s, 512, ATT_W, F32, "attn_proj_mm")
    y_h = _mm_nn(ohg, w_hgrn, s, 512, HG_W, F32, "hgrn_proj_mm")
    merged = _merge_fwd(y_a, y_h, proj)
    y = _mm_nn(merged, w_out, s, 512, D, F32, "out_mm")
    x1 = _post_fwd(x, y, g_post_mix, mod, 2, "post_mix_fwd")
    h2 = _pre_fwd(x1, g_pre_ffn, _after(mod, prefetch("ffn_in", x1)), 4, 3, "pre_ffn_fwd")
    (w_ffn_in_dm,) = weights("ffn_in", h2)
    gu = _mm_nn_dm(h2, w_ffn_in_dm, s // 2, F32, "ffn_in_mm")
    act = _swiglu_fwd(gu, deps=[prefetch("ffn_out", gu)])
    (w_ffn_out,) = weights("ffn_out", act)
    y2 = _mm_nn(act, w_ffn_out, 512, 512, FFN, F32, "ffn_out_mm")
    err, loss, dy2, d_gate2, dg_post_ffn = _post_loss_bwd(x1, y2, g_post_ffn, mod, 5, tgt, "post_ffn_loss_bwd")
    gw_ffn_out = _mm_tn(act, dy2, 512, D, BF16, "ffn_out_dw")
    t_pair = scatter([gw_ffn_out.reshape(N_DEV, FFN // N_DEV, D)], "ffn_out")
    d_act = _mm_nt(dy2, w_ffn_out, s, 512, D, F32, "ffn_out_dx", deps=[t_pair])
    dgu = _swiglu_bwd(d_act, gu)
    t_out = scatter_on("ffn_out", dgu)
    gw_ffn_in = _mm_tn_dm(h2, dgu, 1024, BF16, "ffn_in_dw")
    t_pair = scatter([gw_ffn_in], "ffn_in")
    dh2 = _mm_nt_dm(dgu, w_ffn_in_dm, s, 1024, F32, "ffn_in_dx", deps=[t_pair])
    mod = _after(mod, t_out)
    dx1, d_shift2, d_scale2, dg_pre_ffn = _pre_bwd(dh2, x1, err, g_pre_ffn, mod, 4, "pre_ffn_bwd")
    dy, d_gate1, dg_post_mix = _post_bwd(dx1, y, g_post_mix, mod, 2, "post_mix_bwd")
    t_in = scatter_on("ffn_in", dy)
    d_merged = _mm_nt(dy, w_out, s, 512, D, F32, "out_dx")
    gw_out = _mm_tn(merged, dy, 512, D, BF16, "out_dw")
    dy_a, dy_h, d_gate_a, d_gate_h = _merge_bwd(d_merged, y_a, y_h, proj)
    gw_attn = pieces(_mm_tn(att, dy_a, 512, D, BF16, "attn_proj_dw"))
    gw_hgrn = pieces(_mm_tn(ohg, dy_h, 512, D, BF16, "hgrn_proj_dw"))
    t_pair = scatter([gw_attn, gw_hgrn, gw_out.reshape(N_DEV, D // N_DEV, D)], "mix")
    d_att = _mm_nt(dy_a, w_attn, s, 512, D, F32, "attn_proj_dx")
    d_ohg = _mm_nt(dy_h, w_hgrn, s, 512, D, F32, "hgrn_proj_dx", deps=[t_pair])
    d_o, d_gh, d_hg_norm = _hgout_bwd(d_ohg, o_raw, proj, _after(hg_norm, t_in))
    d_qh, d_fh, d_ih, d_hl = _hgrn_bwd(proj, hl, states, d_o)
    t_mix = scatter_on("mix", d_qh)
    d_qa, d_ka, d_va, d_sinks = _attn_bwd(proj, tabs, _after(sinks_pad, t_mix), d_att)
    d_proj = jnp.concatenate([d_qa, d_ka.astype(BF16), d_va.astype(BF16), d_qh, d_fh, d_ih, d_gh, d_gate_a, d_gate_h], axis=1)
    dh1 = jnp.concatenate([_mm_nn(d_proj, w_half, s // 2, 512, IN_COLS // 2, F32, "proj_dx_" + tag)
                           for tag, w_half in (("a", w_in_a), ("b", w_in_b))], axis=1)
    grad_x, d_shift1, d_scale1, dg_pre_mix = _pre_bwd(dh1, x, dx1, g_pre_mix, mod, 1, "pre_mix_bwd")
    d_mod = jnp.concatenate([d_shift1, d_scale1, d_gate1, d_shift2, d_scale2, d_gate2], axis=1)
    small = [d_mod, dg_pre_mix, dg_post_mix, dg_pre_ffn, dg_post_ffn, d_hl.reshape(1, 2 * HG_W), d_hg_norm, d_sinks]
    return loss, grad_x, small, h1, d_proj


def kernel(x, c, positions, w_ada, b_ada, g_pre_mix, g_post_mix, g_pre_ffn, g_post_ffn, w_in, attn_sinks, w_attn_proj, hg_lower_bounds, hg_norm, w_hgrn_proj, w_out, w_ffn_in, w_ffn_out, loss_target, m_w_ada, m_b_ada, m_g_pre_mix, m_g_post_mix, m_g_pre_ffn, m_g_post_ffn, m_w_in, m_attn_sinks, m_w_attn_proj, m_hg_lower_bounds, m_hg_norm, m_w_hgrn_proj, m_w_out, m_w_ffn_in, m_w_ffn_out, v_w_ada, v_b_ada, v_g_pre_mix, v_g_post_mix, v_g_pre_ffn, v_g_post_ffn, v_w_in, v_attn_sinks, v_w_attn_proj, v_hg_lower_bounds, v_hg_norm, v_w_hgrn_proj, v_w_out, v_w_ffn_in, v_w_ffn_out):
    my_id = _lin(_my_coords())
    s = x.shape[1]
    n_ada = w_ada.shape[2]

    c_all = _exchange_small(c.reshape(1, 1, D), True, "gather_c").reshape(N_DEV, D)
    b_cols = lax.dynamic_slice(b_ada, (0, my_id * n_ada), (1, n_ada))
    mod_part = _mod_part(c_all, w_ada[0], b_cols)
    mod = _exchange_small(mod_part.reshape(N_DEV, 1, n_ada), False, "scatter_mod").reshape(1, N_MOD * D)
    groups = {"in_a": [w_in[0].T[:, :D // 2]], "in_b": [w_in[0].T[:, D // 2:]], "mix": [w_attn_proj[0], w_hgrn_proj[0], w_out[0]],
              "ffn_in": [w_ffn_in[0]], "ffn_out": [w_ffn_out[0]]}

    def start(group, dep):
        shards, dep = lax.optimization_barrier((groups[group], dep))
        return _gather_start([t.astype(BF16) for t in shards], "gather_start_" + group, deps=[dep])

    gathers = {"in_a": start("in_a", mod)}
    gathers["in_b"] = start("in_b", gathers["in_a"]["token"])
    gathers["mix"] = start("mix", gathers["in_b"]["token"])
    gathers["ffn_in"] = start("ffn_in", gathers["mix"]["token"])
    gathers["ffn_out"] = start("ffn_out", gathers["ffn_in"]["token"])

    passes = {}

    def prefetch(group, after):
        lands = _gather_wait(gathers[group], [after], "gather_wait_" + group)
        passes[group] = _pass_start(lands, "gather_pass_start_" + group)
        return passes[group]["token"]

    def weights(group, after):
        if group in passes:
            lands = _pass_wait(passes[group], [after], "gather_pass_wait_" + group)
        else:
            after = [after, gathers["ffn_out"]["token"]]
            lands = _gather_pass(_gather_wait(gathers[group], after, "gather_wait_" + group), "gather_pass_" + group)
        if group in ("in_a", "in_b"):
            return (lands[0].reshape(IN_COLS, D // 2),)
        if group == "mix":
            return lands[0], lands[1], lands[2].reshape(D, D)
        return (lands[0],) if group == "ffn_in" else (lands[0].reshape(FFN, D),)

    pairs, scatters = {}, {}

    def scatter(grads, group):
        pairs[group] = _pair_start(grads, "scatter_pair_" + group)
        return pairs[group]["token"]

    def scatter_on(group, after):
        if group in pairs:
            local, theirs = _pair_wait(pairs[group], [after], "scatter_pair_wait_" + group)
        else:
            local, theirs = after, _pair_exchange(after, "scatter_pair_" + group)
        parts = [_pair_add(g, t, "scatter_pair_add_%s_%d" % (group, k)) for k, (g, t) in enumerate(zip(local, theirs))]
        scatters[group] = _chips_start(parts, "scatter_start_" + group)
        return scatters[group]["token"]

    sinks_pad = jnp.pad(attn_sinks, ((0, 0), (0, LANE - ATT_HEADS)))
    loss, grad_x, small, h1, d_proj = _local_step(
        x[0], loss_target[0], _rope_tables(positions), mod, sinks_pad, hg_lower_bounds, hg_norm, g_pre_mix, g_post_mix, g_pre_ffn, g_post_ffn,
        weights, prefetch, scatter, scatter_on)
    loss = lax.psum(loss[0, 0], ("x", "y", "c"))

    sizes = [t.shape[1] for t in small]
    parts = _exchange_small(jnp.concatenate(small, axis=1).reshape(1, 1, sum(sizes)), True, "gather_small_grads")
    dep = parts
    for half, cols in (("in_a", slice(0, D // 2)), ("in_b", slice(D // 2, D))):
        gw_half = _mm_tn(d_proj, h1[:, cols], 256, D // 2, BF16, "proj_dw_" + half, deps=[dep])
        dep = scatter_on(half, [gw_half.reshape(N_DEV, IN_COLS // N_DEV, D // 2)])
    offs = [sum(sizes[:k]) for k in range(len(sizes))]
    piece = lambda k, n=None: parts[:, :, offs[k]:offs[k] + (sizes[k] if n is None else n)]
    small_w = [(piece(0), b_ada, m_b_ada, v_b_ada), (piece(1), g_pre_mix, m_g_pre_mix, v_g_pre_mix),
               (piece(2), g_post_mix, m_g_post_mix, v_g_post_mix), (piece(3), g_pre_ffn, m_g_pre_ffn, v_g_pre_ffn),
               (piece(4), g_post_ffn, m_g_post_ffn, v_g_post_ffn),
               (piece(5).reshape(N_DEV, 2, HG_W), hg_lower_bounds, m_hg_lower_bounds, v_hg_lower_bounds),
               (piece(6), hg_norm, m_hg_norm, v_hg_norm), (piece(7, ATT_HEADS), attn_sinks, m_attn_sinks, v_attn_sinks)]
    names = ["b_ada", "g_pre_mix", "g_post_mix", "g_pre_ffn", "g_post_ffn", "hg_lower_bounds", "hg_norm", "attn_sinks"]
    res = {n: _adamw(p, w, m, v, "adamw_" + n) for n, (p, w, m, v) in zip(names, small_w)}

    dmod_cols = lax.dynamic_slice(parts.reshape(N_DEV, -1), (0, my_id * n_ada), (N_DEV, n_ada))
    g_w_ada = _grad_w_ada(c_all.T, dmod_cols)
    res["w_ada"] = [g_w_ada] + list(_adamw(g_w_ada[None], w_ada[0], m_w_ada[0], v_w_ada[0], "adamw_w_ada", emit_grad=False))

    big = {"ffn_out": [("w_ffn_out", w_ffn_out, m_w_ffn_out, v_w_ffn_out)], "ffn_in": [("w_ffn_in", w_ffn_in, m_w_ffn_in, v_w_ffn_in)],
           "mix": [("w_attn_proj", w_attn_proj, m_w_attn_proj, v_w_attn_proj), ("w_hgrn_proj", w_hgrn_proj, m_w_hgrn_proj, v_w_hgrn_proj),
                   ("w_out", w_out, m_w_out, v_w_out)]}
    after = [scatters["in_b"]["token"]]
    for group, members in big.items():
        local, lands = _chips_wait(scatters[group], after, "scatter_wait_" + group)
        for (n, w, m, v), mine, land in zip(members, local, lands):
            res[n] = _adamw(land, w[0], m[0], v[0], "adamw_" + n, own=mine[0])
            after = after + [res[n][1]]
    after = [res[n][1] for n in res]
    halves = [_chips_wait(scatters[half], after, "scatter_wait_" + half) for half in ("in_a", "in_b")]
    own = jnp.concatenate([local[0][0] for local, _ in halves], axis=1)
    land = jnp.concatenate([lands[0] for _, lands in halves], axis=2)
    res["w_in"] = [t.T for t in _adamw(land, w_in[0].T, m_w_in[0].T, v_w_in[0].T, "adamw_w_in", own=own, max_elems=WIDE_TILE)]

    order = ["w_ada", "b_ada", "g_pre_mix", "g_post_mix", "g_pre_ffn", "g_post_ffn", "w_in", "attn_sinks", "w_attn_proj",
             "hg_lower_bounds", "hg_norm", "w_hgrn_proj", "w_out", "w_ffn_in", "w_ffn_out"]
    lead = {"w_ada", "w_in", "w_attn_proj", "w_hgrn_proj", "w_out", "w_ffn_in", "w_ffn_out"}
    outs = [loss, grad_x[None]]
    for k in range(4):
        outs += [res[n][k][None] if n in lead else res[n][k] for n in order]
    return tuple(outs)
```

```python
import functools

import jax
import jax.numpy as jnp
from jax import lax
from jax.experimental import pallas as pl
from jax.experimental.pallas import tpu as pltpu

F32 = jnp.float32
BF16 = jnp.bfloat16

N_DEV = 8
D = 2048
ATT_HEADS = 16
KV_HEADS = 2
HEAD_DIM = 64
GROUP = ATT_HEADS // KV_HEADS
ATT_W = ATT_HEADS * HEAD_DIM
BLK = 128
ROT = HEAD_DIM // 4
ROPE_THETA = 500000.0
HG_HEADS = 8
HG_K = 128
HG_W = HG_HEADS * HG_K
CHUNK = 64
SUB = 16
FFN = 5632
N_MOD = 6
EPS = 1e-6
LANE = 128
Q_A, K_A, V_A, Q_H, F_H, I_H, G_H, GT_A, GT_H, IN_COLS = 0, 1024, 1152, 1280, 2304, 3328, 4352, 5376, 7424, 9472

ADAM_LR, ADAM_B1, ADAM_B2, ADAM_EPS, ADAM_WD, ADAM_STEP = 0.001, 0.9, 0.999, 1e-08, 0.01, 10

TR = 256
HG_TB = 512
VMEM_BIG = 56 << 20
MESH = pl.DeviceIdType.MESH


def _sds(shape, dtype):
    return jax.ShapeDtypeStruct(shape, dtype)


def _params(n_axes, vmem=None):
    return pltpu.CompilerParams(dimension_semantics=("arbitrary",) * n_axes, vmem_limit_bytes=vmem)


def _sig(t):
    return 1.0 / (1.0 + jnp.exp(-t))


def _dot(a, b, dims):
    return lax.dot_general(a, b, (dims, ((), ())), preferred_element_type=F32)


NN = ((1,), (0,))
NT = ((1,), (1,))
TN = ((0,), (0,))


def _matmul(a, b, a_spec, b_spec, o_spec, out_shape, grid, dims, acc_shape, name, deps=(), add=None):
    nk = grid[2]
    nd = len(deps)
    extra = [] if add is None else [add]

    def body(a_ref, b_ref, *rest):
        o_ref, scratch = rest[nd + len(extra)], rest[nd + len(extra) + 1:]
        part = _dot(a_ref[...], b_ref[...], dims)
        if add is not None:
            assert nk == 1
            part = part + rest[nd][...]
        if nk == 1:
            o_ref[...] = part.astype(o_ref.dtype)
        else:
            acc = scratch[0]
            k = pl.program_id(2)

            @pl.when(k == 0)
            def _():
                acc[...] = part

            @pl.when(k > 0)
            def _():
                acc[...] += part

            @pl.when(k == nk - 1)
            def _():
                o_ref[...] = acc[...].astype(o_ref.dtype)

    return pl.pallas_call(
        body, grid=grid, in_specs=[a_spec, b_spec] + [pl.BlockSpec(memory_space=pl.ANY)] * nd + [o_spec] * len(extra),
        out_specs=o_spec, out_shape=out_shape, scratch_shapes=[pltpu.VMEM(acc_shape, F32)] if nk > 1 else [],
        input_output_aliases={2 + nd: 0} if extra else {},
        compiler_params=_params(3, VMEM_BIG), name=name)(a, b, *deps, *extra)


def _mm_nn(a, b, tm, tn, tk, out_dtype, name):
    m, k = a.shape
    n = b.shape[1]
    return _matmul(a, b, pl.BlockSpec((tm, tk), lambda j, i, kk: (i, kk)), pl.BlockSpec((tk, tn), lambda j, i, kk: (kk, j)),
                   pl.BlockSpec((tm, tn), lambda j, i, kk: (i, j)), _sds((m, n), out_dtype),
                   (n // tn, m // tm, k // tk), NN, (tm, tn), name)


def _mm_nn_dm(a, b, tm, out_dtype, name):
    m, k = a.shape
    n = b.shape[2]
    return _matmul(a, b, pl.BlockSpec((tm, k), lambda j, i, kk: (i, 0)), pl.BlockSpec((None, k, n), lambda j, i, kk: (j, 0, 0)),
                   pl.BlockSpec((tm, n), lambda j, i, kk: (i, j)), _sds((m, N_DEV * n), out_dtype),
                   (N_DEV, m // tm, 1), NN, (tm, n), name)


def _mm_nt(a, b, tm, tn, tk, out_dtype, name, deps=(), add=None):
    m, k = a.shape
    n = b.shape[0]
    return _matmul(a, b, pl.BlockSpec((tm, tk), lambda j, i, kk: (i, kk)), pl.BlockSpec((tn, tk), lambda j, i, kk: (j, kk)),
                   pl.BlockSpec((tm, tn), lambda j, i, kk: (i, j)), _sds((m, n), out_dtype),
                   (n // tn, m // tm, k // tk), NT, (tm, tn), name, deps, add)


def _mm_nt_dm(a, b, tm, tn, out_dtype, name, deps=()):
    m = a.shape[0]
    n_out, n = b.shape[1], b.shape[2]
    return _matmul(a, b, pl.BlockSpec((tm, n), lambda j, i, kk: (i, kk)), pl.BlockSpec((None, tn, n), lambda j, i, kk: (kk, j, 0)),
                   pl.BlockSpec((tm, tn), lambda j, i, kk: (i, j)), _sds((m, n_out), out_dtype),
                   (n_out // tn, m // tm, N_DEV), NT, (tm, tn), name, deps)


def _mm_tn(a, b, tm, tn, out_dtype, name, deps=()):
    s, m = a.shape
    n = b.shape[1]
    return _matmul(a, b, pl.BlockSpec((s, tm), lambda j, i, kk: (0, i)), pl.BlockSpec((s, tn), lambda j, i, kk: (0, j)),
                   pl.BlockSpec((tm, tn), lambda j, i, kk: (i, j)), _sds((m, n), out_dtype),
                   (n // tn, m // tm, 1), TN, (tm, tn), name, deps)


def _mm_tn_dm(a, b, tm, out_dtype, name):
    s, m = a.shape
    n = b.shape[1] // N_DEV
    return _matmul(a, b, pl.BlockSpec((s, tm), lambda j, i, kk: (0, i)), pl.BlockSpec((s, n), lambda j, i, kk: (0, j)),
                   pl.BlockSpec((None, tm, n), lambda j, i, kk: (j, i, 0)), _sds((N_DEV, m, n), out_dtype),
                   (N_DEV, m // tm, 1), TN, (tm, n), name)


def _row_spec():
    return pl.BlockSpec((TR, D), lambda i: (i, 0))


def _vec_spec(k=0):
    return pl.BlockSpec((1, D), lambda i: (0, k))


def _acc_rows(ref, first, val):
    @pl.when(first)
    def _():
        ref[...] = val

    @pl.when(jnp.logical_not(first))
    def _():
        ref[...] += val


def _pre_fwd(x, g, mod, k_scale, k_shift, name):
    s = x.shape[0]

    def body(x_ref, g_ref, sc_ref, sh_ref, h_ref):
        xv = x_ref[...]
        r = lax.rsqrt(jnp.mean(xv * xv, axis=-1, keepdims=True) + EPS)
        n = xv * r * g_ref[...]
        h_ref[...] = (n * (1.0 + sc_ref[...]) + sh_ref[...]).astype(h_ref.dtype)

    return pl.pallas_call(body, grid=(s // TR,), in_specs=[_row_spec(), _vec_spec(), _vec_spec(k_scale), _vec_spec(k_shift)],
                          out_specs=_row_spec(), out_shape=_sds((s, D), BF16), compiler_params=_params(1), name=name)(x, g, mod, mod)


def _post_fwd(x, y, g, mod, k_gate, name):
    s = x.shape[0]

    def body(x_ref, y_ref, g_ref, gt_ref, o_ref):
        yv = y_ref[...]
        r = lax.rsqrt(jnp.mean(yv * yv, axis=-1, keepdims=True) + EPS)
        o_ref[...] = x_ref[...] + gt_ref[...] * (yv * r * g_ref[...])

    return pl.pallas_call(body, grid=(s // TR,), in_specs=[_row_spec(), _row_spec(), _vec_spec(), _vec_spec(k_gate)],
                          out_specs=_row_spec(), out_shape=_sds((s, D), F32), compiler_params=_params(1), name=name)(x, y, g, mod)


def _post_loss_bwd(x, y, g, mod, k_gate, tgt, name):
    s = x.shape[0]

    def body(x_ref, y_ref, g_ref, gt_ref, t_ref, e_ref, loss_ref, dy_ref, dgt_ref, dg_ref):
        first = pl.program_id(0) == 0
        yv, gv, gate = y_ref[...], g_ref[...], gt_ref[...]
        r = lax.rsqrt(jnp.mean(yv * yv, axis=-1, keepdims=True) + EPS)
        yh = yv * r
        err = x_ref[...] + gate * (yh * gv) - t_ref[...]
        e = err * (1.0 / D)
        e_ref[...] = e
        _acc_rows(loss_ref, first, 0.5 * jnp.sum(jnp.mean(err * err, axis=-1, keepdims=True), axis=0, keepdims=True))
        dn = e * gate
        dgn = dn * gv
        dy_ref[...] = (r * (dgn - yh * jnp.mean(dgn * yh, axis=-1, keepdims=True))).astype(dy_ref.dtype)
        _acc_rows(dgt_ref, first, jnp.sum(e * (yh * gv), axis=0, keepdims=True))
        _acc_rows(dg_ref, first, jnp.sum(dn * yh, axis=0, keepdims=True))

    return pl.pallas_call(body, grid=(s // TR,),
                          in_specs=[_row_spec(), _row_spec(), _vec_spec(), _vec_spec(k_gate), _row_spec()],
                          out_specs=[_row_spec(), pl.BlockSpec((1, 1), lambda i: (0, 0)), _row_spec(), _vec_spec(), _vec_spec()],
                          out_shape=[_sds((s, D), F32), _sds((1, 1), F32), _sds((s, D), BF16), _sds((1, D), F32), _sds((1, D), F32)],
                          compiler_params=_params(1), name=name)(x, y, g, mod, tgt)


def _pre_bwd(dh, x, res, g, mod, k_scale, name):
    s = x.shape[0]

    def body(dh_ref, x_ref, res_ref, g_ref, sc_ref, dx_ref, dsh_ref, dsc_ref, dg_ref):
        first = pl.program_id(0) == 0
        xv, dh_v, gv = x_ref[...], dh_ref[...], g_ref[...]
        r = lax.rsqrt(jnp.mean(xv * xv, axis=-1, keepdims=True) + EPS)
        xh = xv * r
        dn = dh_v * (1.0 + sc_ref[...])
        dgn = dn * gv
        dx_ref[...] = res_ref[...] + r * (dgn - xh * jnp.mean(dgn * xh, axis=-1, keepdims=True))
        _acc_rows(dsh_ref, first, jnp.sum(dh_v, axis=0, keepdims=True))
        _acc_rows(dsc_ref, first, jnp.sum(dh_v * (xh * gv), axis=0, keepdims=True))
        _acc_rows(dg_ref, first, jnp.sum(dn * xh, axis=0, keepdims=True))

    return pl.pallas_call(body, grid=(s // TR,),
                          in_specs=[_row_spec(), _row_spec(), _row_spec(), _vec_spec(), _vec_spec(k_scale)],
                          out_specs=[_row_spec(), _vec_spec(), _vec_spec(), _vec_spec()],
                          out_shape=[_sds((s, D), F32)] + [_sds((1, D), F32)] * 3,
                          compiler_params=_params(1), name=name)(dh, x, res, g, mod)


def _post_bwd(dx, y, g, mod, k_gate, name):
    s = y.shape[0]

    def body(dx_ref, y_ref, g_ref, gt_ref, dy_ref, dgt_ref, dg_ref):
        first = pl.program_id(0) == 0
        yv, dxv, gv = y_ref[...], dx_ref[...], g_ref[...]
        r = lax.rsqrt(jnp.mean(yv * yv, axis=-1, keepdims=True) + EPS)
        yh = yv * r
        dn = dxv * gt_ref[...]
        dgn = dn * gv
        dy_ref[...] = (r * (dgn - yh * jnp.mean(dgn * yh, axis=-1, keepdims=True))).astype(dy_ref.dtype)
        _acc_rows(dgt_ref, first, jnp.sum(dxv * (yh * gv), axis=0, keepdims=True))
        _acc_rows(dg_ref, first, jnp.sum(dn * yh, axis=0, keepdims=True))

    return pl.pallas_call(body, grid=(s // TR,), in_specs=[_row_spec(), _row_spec(), _vec_spec(), _vec_spec(k_gate)],
                          out_specs=[_row_spec(), _vec_spec(), _vec_spec()],
                          out_shape=[_sds((s, D), BF16), _sds((1, D), F32), _sds((1, D), F32)],
                          compiler_params=_params(1), name=name)(dx, y, g, mod)


SW_TN = 1408
SW_TR = 512
TALL = 2048


def _swiglu_fwd(gu, deps=()):
    s = gu.shape[0]
    nb = FFN // SW_TN

    def body(g_ref, u_ref, *rest):
        a_ref = rest[len(deps)]
        gv = g_ref[...]
        a_ref[...] = (gv * _sig(gv) * u_ref[...]).astype(a_ref.dtype)

    return pl.pallas_call(body, grid=(s // SW_TR, nb),
                          in_specs=[pl.BlockSpec((SW_TR, SW_TN), lambda i, j: (i, j)), pl.BlockSpec((SW_TR, SW_TN), lambda i, j: (i, j + nb))]
                          + [pl.BlockSpec(memory_space=pl.ANY)] * len(deps),
                          out_specs=pl.BlockSpec((SW_TR, SW_TN), lambda i, j: (i, j)), out_shape=_sds((s, FFN), BF16),
                          compiler_params=_params(2, 48 << 20), name="swiglu_fwd")(gu, gu, *deps)


def _swiglu_bwd(dact, gu):
    s = gu.shape[0]
    nb = FFN // SW_TN
    n_steps = (s // SW_TR) * nb

    def body(da_ref, g_ref, u_ref, o_ref, buf, sems):
        i, j = pl.program_id(0), pl.program_id(1)
        step = i * nb + j
        slot = step % 2

        def tiles(sl):
            rows = pl.ds(pl.multiple_of(i * SW_TR, SW_TR), SW_TR)
            return [pltpu.make_async_copy(buf.at[sl, h], o_ref.at[rows, pl.ds(pl.multiple_of((j + nb * h) * SW_TN, LANE), SW_TN)], sems.at[sl, h])
                    for h in range(2)]

        @pl.when(step >= 2)
        def _():
            for cp in tiles(slot):
                cp.wait()

        gv, da = g_ref[...], da_ref[...]
        sg = _sig(gv)
        buf[slot, 0] = (da * u_ref[...] * (sg * (1.0 + gv * (1.0 - sg)))).astype(buf.dtype)
        buf[slot, 1] = (da * (gv * sg)).astype(buf.dtype)
        for cp in tiles(slot):
            cp.start()

        @pl.when(step == n_steps - 1)
        def _():
            for cp in tiles(slot) + (tiles(1 - slot) if n_steps > 1 else []):
                cp.wait()

    blk = lambda f: pl.BlockSpec((SW_TR, SW_TN), f)
    return pl.pallas_call(body, grid=(s // SW_TR, nb),
                          in_specs=[blk(lambda i, j: (i, j)), blk(lambda i, j: (i, j)), blk(lambda i, j: (i, j + nb))],
                          out_specs=pl.BlockSpec(memory_space=pl.ANY), out_shape=_sds((s, 2 * FFN), BF16),
                          scratch_shapes=[pltpu.VMEM((2, 2, SW_TR, SW_TN), BF16), pltpu.SemaphoreType.DMA((2, 2))],
                          compiler_params=_params(2, 48 << 20), name="swiglu_bwd")(dact, gu, gu)


MG_TN = 256


def _merge_fwd(y_a, y_h, proj):
    s = y_a.shape[0]
    tn = MG_TN
    ba, bh = GT_A // tn, GT_H // tn

    def body(ya_ref, yh_ref, ga_ref, gh_ref, m_ref):
        m_ref[...] = (_sig(ga_ref[...]) * ya_ref[...] + _sig(gh_ref[...]) * yh_ref[...]).astype(m_ref.dtype)

    tr = min(s, TALL)
    blk = lambda f: pl.BlockSpec((tr, tn), f)
    return pl.pallas_call(body, grid=(s // tr, D // tn),
                          in_specs=[blk(lambda i, j: (i, j)), blk(lambda i, j: (i, j)), blk(lambda i, j: (i, j + ba)), blk(lambda i, j: (i, j + bh))],
                          out_specs=blk(lambda i, j: (i, j)), out_shape=_sds((s, D), BF16),
                          compiler_params=_params(2), name="merge_fwd")(y_a, y_h, proj, proj)


def _merge_bwd(dm, y_a, y_h, proj):
    s = y_a.shape[0]
    tn = MG_TN
    ba, bh = GT_A // tn, GT_H // tn

    def body(dm_ref, ya_ref, yh_ref, ga_ref, gh_ref, dya_ref, dyh_ref, dga_ref, dgh_ref):
        dmv = dm_ref[...]
        sa, sh = _sig(ga_ref[...]), _sig(gh_ref[...])
        dya_ref[...] = (dmv * sa).astype(BF16)
        dyh_ref[...] = (dmv * sh).astype(BF16)
        dga_ref[...] = (dmv * ya_ref[...] * (sa * (1.0 - sa))).astype(BF16)
        dgh_ref[...] = (dmv * yh_ref[...] * (sh * (1.0 - sh))).astype(BF16)

    tr = min(s, TALL)
    blk = lambda f: pl.BlockSpec((tr, tn), f)
    nat = blk(lambda i, j: (i, j))
    return pl.pallas_call(body, grid=(s // tr, D // tn),
                          in_specs=[nat, nat, nat, blk(lambda i, j: (i, j + ba)), blk(lambda i, j: (i, j + bh))],
                          out_specs=[nat] * 4, out_shape=[_sds((s, D), BF16)] * 4,
                          compiler_params=_params(2), name="merge_bwd")(dm, y_a, y_h, proj, proj)


def _hgout_fwd(o_raw, proj, hg_norm):
    s = o_raw.shape[0]
    bg = G_H // LANE

    def body(o_ref, g_ref, n_ref, out_ref):
        ov = o_ref[...]
        r = lax.rsqrt(jnp.mean(ov * ov, axis=-1, keepdims=True) + EPS)
        out_ref[...] = (ov * r * n_ref[...] * _sig(g_ref[...])).astype(out_ref.dtype)

    tr = min(s, TALL)
    blk = lambda f: pl.BlockSpec((tr, LANE), f)
    return pl.pallas_call(body, grid=(s // tr, HG_HEADS),
                          in_specs=[blk(lambda i, h: (i, h)), blk(lambda i, h: (i, h + bg)), pl.BlockSpec((1, LANE), lambda i, h: (0, 0))],
                          out_specs=blk(lambda i, h: (i, h)), out_shape=_sds((s, HG_W), BF16),
                          compiler_params=_params(2), name="hgout_fwd")(o_raw, proj, hg_norm)


def _hgout_bwd(d_out, o_raw, proj, hg_norm):
    s = o_raw.shape[0]
    bg = G_H // LANE

    def body(d_ref, o_ref, g_ref, n_ref, do_ref, dg_ref, dn_ref):
        first = jnp.logical_and(pl.program_id(0) == 0, pl.program_id(1) == 0)
        ov, dv, nv = o_ref[...], d_ref[...], n_ref[...]
        sg = _sig(g_ref[...])
        r = lax.rsqrt(jnp.mean(ov * ov, axis=-1, keepdims=True) + EPS)
        oh = ov * r
        d_on = dv * sg
        dg_ref[...] = (dv * (oh * nv) * (sg * (1.0 - sg))).astype(dg_ref.dtype)
        t = d_on * nv
        do_ref[...] = r * (t - oh * jnp.mean(t * oh, axis=-1, keepdims=True))
        _acc_rows(dn_ref, first, jnp.sum(d_on * oh, axis=0, keepdims=True))

    tr = min(s, TALL)
    blk = lambda f: pl.BlockSpec((tr, LANE), f)
    vec = pl.BlockSpec((1, LANE), lambda i, h: (0, 0))
    return pl.pallas_call(body, grid=(s // tr, HG_HEADS),
                          in_specs=[blk(lambda i, h: (i, h)), blk(lambda i, h: (i, h)), blk(lambda i, h: (i, h + bg)), vec],
                          out_specs=[blk(lambda i, h: (i, h)), blk(lambda i, h: (i, h)), vec],
                          out_shape=[_sds((s, HG_W), F32), _sds((s, HG_W), BF16), _sds((1, LANE), F32)],
                          compiler_params=_params(2), name="hgout_bwd")(d_out, o_raw, proj, hg_norm)


def _rope(t, cos, s_lo, s_hi):
    return t * cos + pltpu.roll(t, LANE - ROT // 2, 1) * s_lo + pltpu.roll(t, ROT // 2, 1) * s_hi


def _rope_wide(t, cos, s_lo, s_hi):
    return jnp.concatenate([_rope(t[:, k * LANE:(k + 1) * LANE], cos, s_lo, s_hi) for k in range(t.shape[1] // LANE)], axis=1)


def _attn_mask(has_prev):
    kj = lax.broadcasted_iota(jnp.int32, (2 * BLK, BLK), 0)
    qi = lax.broadcasted_iota(jnp.int32, (2 * BLK, BLK), 1)
    rel = BLK + qi - kj
    band = jnp.logical_and(rel >= 0, rel < BLK)
    return jnp.logical_and(band, jnp.logical_or(has_prev, kj >= BLK))


def _attn_specs():
    prev = lambda i: jnp.maximum(i - 1, 0)
    kb, vb = K_A // LANE, V_A // LANE
    blk = lambda f: pl.BlockSpec((BLK, LANE), f)
    tabs = [blk(lambda i: (i, 0))] * 3 + [blk(lambda i: (prev(i), 0))] * 3
    return [pl.BlockSpec((BLK, ATT_W), lambda i: (i, 0)), blk(lambda i: (i, kb)), blk(lambda i: (prev(i), kb)),
            blk(lambda i: (i, vb)), blk(lambda i: (prev(i), vb))] + tabs + [pl.BlockSpec((1, LANE), lambda i: (0, 0))]


def _attn_logits(qh, kg):
    return _dot(kg, qh, NT)


def _attn_probs(raw, mask, sk):
    logits = jnp.where(mask, raw * (HEAD_DIM ** -0.5), -jnp.inf)
    m = jnp.maximum(jnp.max(logits, axis=0, keepdims=True), sk)
    p = jnp.exp(logits - m)
    e_sink = jnp.exp(sk - m)
    inv = 1.0 / (jnp.sum(p, axis=0, keepdims=True) + e_sink)
    return p, inv, e_sink * inv


def _attn_fwd(proj, tabs, sinks):
    s = proj.shape[0]

    def body(q_ref, kc_ref, kp_ref, vc_ref, vp_ref, c0, l0, h0, c1, l1, h1, sk_ref, o_ref):
        i = pl.program_id(0)
        mask = _attn_mask(i > 0)
        q = _rope_wide(q_ref[...], c0[...], l0[...], h0[...]).astype(BF16)
        kk = jnp.concatenate([_rope(kp_ref[...], c1[...], l1[...], h1[...]), _rope(kc_ref[...], c0[...], l0[...], h0[...])], axis=0).astype(BF16)
        v_t = jnp.concatenate([vp_ref[...], vc_ref[...]], axis=0).T.astype(BF16)
        part = lambda t, h: t[:, h * HEAD_DIM:(h + 1) * HEAD_DIM]
        k_heads = [part(kk, g) for g in range(KV_HEADS)]

        def head(h):
            g = h // GROUP
            raw = _attn_logits(part(q, h), k_heads[g])
            yield
            p, inv, _ = _attn_probs(raw, mask, sk_ref[:, h:h + 1])
            yield
            out_t = _dot(v_t[g * HEAD_DIM:(g + 1) * HEAD_DIM], p.astype(BF16), NN)
            yield
            return out_t * inv

        o_ref[...] = jnp.concatenate(_interleave([head(h) for h in range(ATT_HEADS)]), axis=0).T.astype(o_ref.dtype)

    return pl.pallas_call(body, grid=(s // BLK,), in_specs=_attn_specs(),
                          out_specs=pl.BlockSpec((BLK, ATT_W), lambda i: (i, 0)), out_shape=_sds((s, ATT_W), BF16),
                          compiler_params=_params(1), name="attn_fwd")(proj, proj, proj, proj, proj, *tabs, *tabs, sinks)


def _attn_bwd(proj, tabs, sinks, d_att):
    s = proj.shape[0]

    def body(q_ref, kc_ref, kp_ref, vc_ref, vp_ref, c0, l0, h0, c1, l1, h1, sk_ref, do_ref, dq_ref, dk_ref, dv_ref, ds_ref):
        i = pl.program_id(0)

        @pl.when(i == 0)
        def _():
            dk_ref[...] = jnp.zeros_like(dk_ref)
            dv_ref[...] = jnp.zeros_like(dv_ref)
            ds_ref[...] = jnp.zeros_like(ds_ref)

        mask = _attn_mask(i > 0)
        q = _rope_wide(q_ref[...], c0[...], l0[...], h0[...]).astype(BF16)
        kk = jnp.concatenate([_rope(kp_ref[...], c1[...], l1[...], h1[...]), _rope(kc_ref[...], c0[...], l0[...], h0[...])], axis=0).astype(BF16)
        k_f32 = jnp.concatenate([_rope(kp_ref[...], c1[...], l1[...], h1[...]), _rope(kc_ref[...], c0[...], l0[...], h0[...])], axis=0)
        k_t = k_f32.T.astype(BF16)
        vv = jnp.concatenate([vp_ref[...], vc_ref[...]], axis=0).astype(BF16)
        d_o = do_ref[...].astype(BF16)
        lane = lax.broadcasted_iota(jnp.int32, (1, LANE), 1)
        part = lambda t, h: t[:, h * HEAD_DIM:(h + 1) * HEAD_DIM]
        k_heads = [part(kk, g) for g in range(KV_HEADS)]
        v_heads = [part(vv, g) for g in range(KV_HEADS)]

        def head(h):
            g = h // GROUP
            qh, doh = part(q, h), part(d_o, h)
            raw = _attn_logits(qh, k_heads[g])
            d_p = _dot(v_heads[g], doh, NT)
            yield
            p, inv, p_sink = _attn_probs(raw, mask, sk_ref[:, h:h + 1])
            prob = p * inv
            dv = _dot(prob.astype(BF16), doh, NN)
            yield
            dd = jnp.sum(prob * d_p, axis=0, keepdims=True)
            d_s = (prob * (d_p - dd)).astype(BF16)
            d_sink = jnp.where(lane == h, -jnp.sum(p_sink * dd, axis=1, keepdims=True), 0.0)
            dq_t = _dot(k_t[g * HEAD_DIM:(g + 1) * HEAD_DIM], d_s, NN)
            dk = _dot(d_s, qh, NN)
            yield
            return dq_t * (HEAD_DIM ** -0.5), dk * (HEAD_DIM ** -0.5), dv, d_sink

        per_head = _interleave([head(h) for h in range(ATT_HEADS)])
        dqs = [jnp.concatenate([t[0] for t in per_head], axis=0).T]
        group_sum = lambda k, g: functools.reduce(jnp.add, [t[k] for t in per_head[g * GROUP:(g + 1) * GROUP]])
        dks = [group_sum(1, g) for g in range(KV_HEADS)]
        dvs = [group_sum(2, g) for g in range(KV_HEADS)]
        d_sink = functools.reduce(jnp.add, [t[3] for t in per_head])
        dq_ref[...] = _rope_wide(jnp.concatenate(dqs, axis=1), c0[...], -l0[...], -h0[...]).astype(dq_ref.dtype)
        d_k = jnp.concatenate(dks, axis=1)
        d_v = jnp.concatenate(dvs, axis=1)
        cur = pl.ds(pl.multiple_of(i * BLK, BLK), BLK)
        prv = pl.ds(pl.multiple_of(jnp.maximum(i - 1, 0) * BLK, BLK), BLK)
        dk_ref[prv, :] += _rope(d_k[:BLK], c1[...], -l1[...], -h1[...])
        dk_ref[cur, :] += _rope(d_k[BLK:], c0[...], -l0[...], -h0[...])
        dv_ref[prv, :] += d_v[:BLK]
        dv_ref[cur, :] += d_v[BLK:]
        ds_ref[...] += d_sink

    full = pl.BlockSpec((s, LANE), lambda i: (0, 0))
    return pl.pallas_call(body, grid=(s // BLK,), in_specs=_attn_specs() + [pl.BlockSpec((BLK, ATT_W), lambda i: (i, 0))],
                          out_specs=[pl.BlockSpec((BLK, ATT_W), lambda i: (i, 0)), full, full, pl.BlockSpec((1, LANE), lambda i: (0, 0))],
                          out_shape=[_sds((s, ATT_W), BF16), _sds((s, LANE), F32), _sds((s, LANE), F32), _sds((1, LANE), F32)],
                          compiler_params=_params(1), name="attn_bwd")(proj, proj, proj, proj, proj, *tabs, *tabs, sinks, d_att)


def _tri_matmul(tri, t):
    hi = t.astype(BF16)
    r1 = t - hi.astype(F32)
    mid = r1.astype(BF16)
    lo = (r1 - mid.astype(F32)).astype(BF16)
    return _dot(tri, hi, NN) + _dot(tri, mid, NN) + _dot(tri, lo, NN)


def _lower_bound(hl):
    a, b = hl[0:1, :], hl[1:2, :]
    mx = jnp.maximum(a, b)
    ea, eb = jnp.exp(a - mx), jnp.exp(b - mx)
    return ea / (ea + eb)


def _hg_gates(q_raw, f_raw, lb, tri_lower):
    sg = _sig(f_raw)
    f = lb + (1.0 - lb) * sg
    sq = _sig(q_raw)
    b = _tri_matmul(tri_lower, jnp.log(f))
    return sg, f, 1.0 - f, sq, q_raw * sq, b


HG_PAIR_FWD = 8
HG_PAIR_BWD = 8


def _hg_specs(n_map, pair):
    blk = lambda off, p: pl.BlockSpec((HG_TB, LANE), lambda h, n: (n_map(n), off // LANE + pair * h + p))
    return [blk(off, p) for off in (Q_H, F_H, I_H) for p in range(pair)] + [pl.BlockSpec((2, pair * LANE), lambda h, n: (0, h))]


def _interleave(gens):
    out = [None] * len(gens)
    live = list(range(len(gens)))
    while live:
        for k in list(live):
            try:
                next(gens[k])
            except StopIteration as stop:
                out[k] = stop.value
                live.remove(k)
    return out


def _hg_spread():
    c = lax.broadcasted_iota(jnp.int32, (CHUNK, SUB * SUB), 0)
    l = lax.broadcasted_iota(jnp.int32, (CHUNK, SUB * SUB), 1)
    r = lax.broadcasted_iota(jnp.int32, (SUB, SUB * SUB), 0)
    lr = lax.broadcasted_iota(jnp.int32, (SUB, SUB * SUB), 1)
    shift = SUB.bit_length() - 1
    cols = [(c == lo + (l >> shift)).astype(BF16) for lo in range(0, CHUNK, SUB)]
    tile = [(c == lo + (l & (SUB - 1))).astype(BF16) for lo in range(0, CHUNK, SUB)]
    return cols, tile, (lr & (SUB - 1)) == r, (lr >> shift) == r


def _hg_intra(qs, kk, b, grad=None):
    lane = lax.broadcasted_iota(jnp.int32, (SUB, CHUNK), 1)
    row1 = lax.broadcasted_iota(jnp.int32, (SUB, 1), 0)
    kk_b = kk.astype(BF16)
    if grad is not None:
        d_a, d_at, (cols, tile, diag, block) = grad
    a_blocks, dq_blocks, dk_blocks, db_blocks = [], [], [], []
    dk_left = None
    for j in range(CHUNK // SUB):
        lo = j * SUB
        q_j, k_j, b_j = qs[lo:lo + SUB], kk[lo:lo + SUB], b[lo:lo + SUB]
        es = [jnp.where(row1 >= sx, jnp.exp(jnp.minimum(b_j - b_j[sx:sx + 1], 0.0)), 0.0) for sx in range(SUB)]
        pes = [q_j * e for e in es]
        pe = jnp.concatenate(pes, axis=0).astype(BF16)
        pairs = _dot(pe, kk_b, NT)
        yield
        a_j = jnp.zeros((SUB, CHUNK), F32)
        for sx in range(SUB):
            a_j = jnp.where(lane == lo + sx, pairs[sx * SUB:(sx + 1) * SUB], a_j)
        if grad is not None:
            da_j = d_a[lo:lo + SUB]
            ek = jnp.concatenate([e * k_j[sx:sx + 1] for sx, e in enumerate(es)], axis=0).astype(BF16)
            sel_t = jnp.where(diag, _dot(da_j.astype(BF16), cols[j], NN), 0.0).astype(BF16)
            sel_s = jnp.where(block, _dot(d_at[lo:lo + SUB].astype(BF16), tile[j], NN), 0.0).astype(BF16)
            pek = jnp.concatenate([p * k_j[sx:sx + 1] for sx, p in enumerate(pes)], axis=0).astype(BF16)
            yield
            dq_j = _dot(sel_t, ek, NN)
            dk_j = _dot(sel_s, pe, NN)
            db_j = _dot(sel_t, pek, NN) - _dot(sel_s, pek, NN)
            yield
        if j > 0:
            ref = b[lo - 1:lo]
            sc_q = jnp.exp(b_j - ref)
            sc_k = jnp.exp(jnp.minimum(ref - b, 0.0))
            qt = (q_j * sc_q).astype(BF16)
            kt = (kk * sc_k).astype(BF16)
            left = _dot(qt, kt, NT)
            yield
            a_j = a_j + jnp.where(lane < lo, left, 0.0)
            if grad is not None:
                da_left = jnp.where(lane < lo, da_j, 0.0).astype(BF16)
                dq_left = _dot(da_left, kt, NN) * sc_q
                dq_j = dq_j + dq_left
                db_j = db_j + q_j * dq_left
                t = _dot(da_left, qt, TN)
                yield
                t = t * sc_k
                dk_left = t if dk_left is None else dk_left + t
        a_blocks.append(a_j)
        if grad is not None:
            dq_blocks.append(dq_j)
            dk_blocks.append(dk_j)
            db_blocks.append(db_j)
    a = jnp.concatenate(a_blocks, axis=0)
    if grad is None:
        return a
    return a, jnp.concatenate(dq_blocks, axis=0), jnp.concatenate(dk_blocks, axis=0) + dk_left, jnp.concatenate(db_blocks, axis=0) - kk * dk_left


def _hgrn_fwd(proj, hl):
    s = proj.shape[0]
    n_chunk = HG_TB // CHUNK
    pair = HG_PAIR_FWD

    def body(*refs):
        q_refs, f_refs, i_refs = refs[:pair], refs[pair:2 * pair], refs[2 * pair:3 * pair]
        hl_ref, o_ref, st_out_ref, st_ref = refs[3 * pair:]

        @pl.when(pl.program_id(1) == 0)
        def _():
            st_ref[...] = jnp.zeros_like(st_ref)

        r_i = lax.broadcasted_iota(jnp.int32, (CHUNK, CHUNK), 0)
        c_i = lax.broadcasted_iota(jnp.int32, (CHUNK, CHUNK), 1)
        tri_lower = (r_i >= c_i).astype(BF16)

        def chunk(c, carry):
            rows = pl.ds(pl.multiple_of(c * CHUNK, CHUNK), CHUNK)
            def head(p):
                cols = slice(p * LANE, (p + 1) * LANE)
                lb = _lower_bound(hl_ref[:, cols])
                v = i_refs[p][rows, :].astype(BF16)
                _, _, kk, _, qs, b = _hg_gates(q_refs[p][rows, :], f_refs[p][rows, :], lb, tri_lower)
                yield
                st = st_ref[p]
                st_b = st.astype(BF16)
                st_out_ref[p, c] = st_b
                o_state = _dot((qs * jnp.exp(b)).astype(BF16), st_b, NT)
                b_last = b[CHUNK - 1:CHUNK, :]
                st_new = _dot(v, (kk * jnp.exp(b_last - b)).astype(BF16), TN)
                a = yield from _hg_intra(qs, kk, b)
                st_ref[p] = st * jnp.exp(b_last) + st_new
                o_ref[rows, cols] = o_state + _dot(a.astype(BF16), v, NN)

            _interleave([head(p) for p in range(pair)])
            return carry

        lax.fori_loop(0, n_chunk, chunk, 0)

    return pl.pallas_call(
        body, grid=(HG_HEADS // pair, s // HG_TB), in_specs=_hg_specs(lambda n: n, pair),
        out_specs=[pl.BlockSpec((HG_TB, pair * LANE), lambda h, n: (n, h)), pl.BlockSpec((pair, n_chunk, HG_K, HG_K), lambda h, n: (h, n, 0, 0))],
        out_shape=[_sds((s, HG_W), F32), _sds((HG_HEADS, s // CHUNK, HG_K, HG_K), BF16)],
        scratch_shapes=[pltpu.VMEM((pair, HG_K, HG_K), F32)],
        compiler_params=_params(2), name="hgrn_fwd")(*[proj] * (3 * pair), hl)


def _hgrn_bwd(proj, hl, states, d_o):
    s = proj.shape[0]
    n_chunk = HG_TB // CHUNK
    n_blk = s // HG_TB
    pair = HG_PAIR_BWD
    rev = lambda n: n_blk - 1 - n

    def body(*refs):
        q_refs, f_refs, i_refs = refs[:pair], refs[pair:2 * pair], refs[2 * pair:3 * pair]
        hl_ref, st_in_ref, do_ref, dq_ref, df_ref, di_ref, dhl_ref, dst_ref, dlb_ref = refs[3 * pair:]
        n = pl.program_id(1)

        @pl.when(n == 0)
        def _():
            dst_ref[...] = jnp.zeros_like(dst_ref)
            dlb_ref[...] = jnp.zeros_like(dlb_ref)

        r_i = lax.broadcasted_iota(jnp.int32, (CHUNK, CHUNK), 0)
        c_i = lax.broadcasted_iota(jnp.int32, (CHUNK, CHUNK), 1)
        tri_lower = (r_i >= c_i).astype(BF16)
        tri_upper = (r_i <= c_i).astype(BF16)
        row = lax.broadcasted_iota(jnp.int32, (CHUNK, 1), 0)
        spread = _hg_spread()

        def chunk(cc, carry):
            c = n_chunk - 1 - cc
            rows = pl.ds(pl.multiple_of(c * CHUNK, CHUNK), CHUNK)
            def head(p):
                cols = slice(p * LANE, (p + 1) * LANE)
                lb = _lower_bound(hl_ref[:, cols])
                q_raw = q_refs[p][rows, :]
                vb = i_refs[p][rows, :].astype(BF16)
                sg, f, kk, sq, qs, b = _hg_gates(q_raw, f_refs[p][rows, :], lb, tri_lower)
                yield
                e_b = jnp.exp(b)
                qe = qs * e_b
                b_last = b[CHUNK - 1:CHUNK, :]
                e_last = jnp.exp(b_last)
                e_kd = jnp.exp(b_last - b)
                kd = kk * e_kd
                st0 = st_in_ref[p, c]
                d_ob = do_ref[rows, cols].astype(BF16)
                dst = dst_ref[p]
                dst_b = dst.astype(BF16)
                d_a = jnp.where(r_i >= c_i, _dot(d_ob, vb, NT), 0.0)
                d_at = jnp.where(r_i <= c_i, _dot(vb, d_ob, NT), 0.0)
                d_v_st = _dot(kd.astype(BF16), dst_b, NT)
                d_kd = _dot(vb, dst_b, NN)
                d_qe = _dot(d_ob, st0, NN)
                dst_new = _dot(d_ob, qe.astype(BF16), TN)
                yield
                a, dqs, dkk, d_b = yield from _hg_intra(qs, kk, b, (d_a, d_at, spread))
                d_v = _dot(a.astype(BF16), d_ob, TN) + d_v_st
                dqs_st = d_qe * e_b
                dkk_st = d_kd * e_kd
                dqs = dqs + dqs_st
                dkk = dkk + dkk_st
                d_b_last = jnp.sum(d_kd * kd, axis=0, keepdims=True) + jnp.sum(dst * st0.astype(F32), axis=0, keepdims=True) * e_last
                d_b = d_b + qs * dqs_st - kk * dkk_st + jnp.where(row == CHUNK - 1, d_b_last, 0.0)
                d_g = _tri_matmul(tri_upper, d_b)
                dst_ref[p] = dst_new + dst * e_last
                yield
                d_f = d_g / f - dkk
                dlb_ref[:, cols] += jnp.sum(d_f * (1.0 - sg), axis=0, keepdims=True)
                dq_ref[rows, cols] = (dqs * (sq * (1.0 + q_raw * (1.0 - sq)))).astype(dq_ref.dtype)
                df_ref[rows, cols] = (d_f * (1.0 - lb) * (sg * (1.0 - sg))).astype(df_ref.dtype)
                di_ref[rows, cols] = d_v.astype(di_ref.dtype)

            _interleave([head(p) for p in range(pair)])
            return carry

        lax.fori_loop(0, n_chunk, chunk, 0)

        @pl.when(n == n_blk - 1)
        def _():
            lb = _lower_bound(hl_ref[...])
            d_hl0 = dlb_ref[...] * (lb * (1.0 - lb))
            dhl_ref[...] = jnp.concatenate([d_hl0, -d_hl0], axis=0)

    out_blk = pl.BlockSpec((HG_TB, pair * LANE), lambda h, n: (rev(n), h))
    return pl.pallas_call(
        body, grid=(HG_HEADS // pair, n_blk),
        in_specs=_hg_specs(rev, pair) + [pl.BlockSpec((pair, n_chunk, HG_K, HG_K), lambda h, n: (h, rev(n), 0, 0)), out_blk],
        out_specs=[out_blk, out_blk, out_blk, pl.BlockSpec((2, pair * LANE), lambda h, n: (0, h))],
        out_shape=[_sds((s, HG_W), BF16)] * 3 + [_sds((2, HG_W), F32)],
        scratch_shapes=[pltpu.VMEM((pair, HG_K, HG_K), F32), pltpu.VMEM((1, pair * LANE), F32)],
        compiler_params=_params(2), name="hgrn_bwd")(*[proj] * (3 * pair), hl, states, d_o)


def _mod_part(c_all, w_shard, b_shard):
    n = w_shard.shape[1]
    tn = 512

    def body(c_ref, w_ref, b_ref, o_ref):
        o_ref[...] = _dot(c_ref[...].astype(BF16), w_ref[...].astype(BF16), NN) + b_ref[...]

    return pl.pallas_call(body, grid=(n // tn,),
                          in_specs=[pl.BlockSpec((N_DEV, D), lambda j: (0, 0)), pl.BlockSpec((D, tn), lambda j: (0, j)), pl.BlockSpec((1, tn), lambda j: (0, j))],
                          out_specs=pl.BlockSpec((N_DEV, tn), lambda j: (0, j)), out_shape=_sds((N_DEV, n), F32),
                          compiler_params=_params(1, 32 << 20), name="mod_part")(c_all, w_shard, b_shard)


def _grad_w_ada(c_all_t, dmod_cols):
    n = dmod_cols.shape[1]
    tn = 512

    def body(c_ref, d_ref, o_ref):
        cv = c_ref[...].astype(BF16).astype(F32)
        dv = d_ref[...].astype(BF16).astype(F32)
        acc = cv[:, 0:1] * dv[0:1, :]
        for k in range(1, N_DEV):
            acc = acc + cv[:, k:k + 1] * dv[k:k + 1, :]
        o_ref[...] = acc

    return pl.pallas_call(body, grid=(n // tn,),
                          in_specs=[pl.BlockSpec((D, N_DEV), lambda j: (0, 0)), pl.BlockSpec((N_DEV, tn), lambda j: (0, j))],
                          out_specs=pl.BlockSpec((D, tn), lambda j: (0, j)), out_shape=_sds((D, n), F32),
                          compiler_params=_params(1, 32 << 20), name="grad_w_ada")(c_all_t, dmod_cols)


def _row_tile(r, c, max_elems=1 << 18):
    if r * c <= max_elems or r % 8:
        return r
    best = 8
    for t in range(8, r + 1, 8):
        if r % t == 0 and t * c <= max_elems:
            best = t
    return best


WIDE_TILE = 5 << 17
PAIR_TILE = 3 << 19


def _adamw(pieces, w, m, v, name, emit_grad=True, own=None, max_elems=1 << 18):
    p, r, c = pieces.shape
    tr = _row_tile(r, c, max_elems)
    c1 = 1.0 / (1.0 - ADAM_B1 ** ADAM_STEP)
    c2 = 1.0 / (1.0 - ADAM_B2 ** ADAM_STEP)

    def body(*refs):
        if own is None:
            p_ref, w_ref, m_ref, v_ref, *outs = refs
            g = p_ref[0].astype(F32)
        else:
            o_ref, p_ref, w_ref, m_ref, v_ref, *outs = refs
            g = o_ref[...].astype(F32) + p_ref[0].astype(F32)
        for k in range(1, p):
            g = g + p_ref[k].astype(F32)
        m2 = ADAM_B1 * m_ref[...] + (1.0 - ADAM_B1) * g
        v2 = ADAM_B2 * v_ref[...] + (1.0 - ADAM_B2) * (g * g)
        delta = -ADAM_LR * ((m2 * c1) / (jnp.sqrt(v2 * c2) + ADAM_EPS) + ADAM_WD * w_ref[...])
        if emit_grad:
            outs[0][...] = g
        outs[-3][...] = delta
        outs[-2][...] = m2
        outs[-1][...] = v2

    blk = pl.BlockSpec((tr, c), lambda i: (i, 0))
    n_out = 4 if emit_grad else 3
    lead = [] if own is None else [own]
    return pl.pallas_call(body, grid=(r // tr,), in_specs=[blk] * len(lead) + [pl.BlockSpec((p, tr, c), lambda i: (0, i, 0)), blk, blk, blk],
                          out_specs=[blk] * n_out, out_shape=[_sds((r, c), F32)] * n_out,
                          compiler_params=_params(1, 48 << 20), name=name)(*lead, pieces, w, m, v)


def _my_coords():
    return lax.axis_index("x"), lax.axis_index("y"), lax.axis_index("c")


def _flip(coords, k):
    x, y, c = coords
    return (1 - x if k & 4 else x, 1 - y if k & 2 else y, 1 - c if k & 1 else c)


def _lin(coords):
    return 4 * coords[0] + 2 * coords[1] + coords[2]


def _exchange_small(x3, bcast, name):
    n = x3.shape[2]

    def body(x_ref, o_ref, send_sems, recv_sems):
        me = _my_coords()
        my_id = _lin(me)
        o_ref[pl.ds(my_id, 1)] = x_ref[pl.ds(0 if bcast else my_id, 1)]
        copies = []
        for k in range(1, N_DEV):
            peer = _flip(me, k)
            src = x_ref.at[0 if bcast else _lin(peer)]
            cp = pltpu.make_async_remote_copy(src_ref=src, dst_ref=o_ref.at[my_id], send_sem=send_sems.at[k], recv_sem=recv_sems.at[k],
                                              device_id=peer, device_id_type=MESH)
            cp.start()
            copies.append(cp)
        for k in range(1, N_DEV):
            peer = _flip(me, k)
            pltpu.make_async_remote_copy(src_ref=x_ref.at[0], dst_ref=o_ref.at[_lin(peer)], send_sem=send_sems.at[k], recv_sem=recv_sems.at[k],
                                         device_id=peer, device_id_type=MESH).wait_recv()
        for cp in copies:
            cp.wait_send()

    vm = pl.BlockSpec(memory_space=pltpu.VMEM)
    return pl.pallas_call(body, in_specs=[vm], out_specs=vm, out_shape=_sds((N_DEV, 1, n), F32),
                          scratch_shapes=[pltpu.SemaphoreType.DMA((N_DEV,)), pltpu.SemaphoreType.DMA((N_DEV,))], name=name)(x3)


HBM_SPEC = pl.BlockSpec(memory_space=pltpu.HBM)
SEM_SPEC = pl.BlockSpec(memory_space=pltpu.SEMAPHORE)
ANY_SPEC = pl.BlockSpec(memory_space=pl.ANY)
DATAFLOW = pltpu.SideEffectType.DATAFLOW_SIDE_EFFECTING
GATHER_FLIPS = (1, 2, 4, 6)
PASS_FLIPS = (2, 4, 6)
TOKEN = (8, LANE)


def _hbm(t):
    return pltpu.with_memory_space_constraint(t, pltpu.HBM)


def _hbm_like(ts):
    return [pltpu.HBM(t.shape, t.dtype) for t in ts]


def _split_start(issue, srcs, lands, n_sem, name, deps=()):
    n, nb, nd = len(srcs), len(srcs) + len(lands), len(deps)

    def body(*refs):
        issue(refs[:n], refs[n:nb], refs[nb + nd], refs[nb + nd + 1])
        refs[-1][...] = jnp.zeros(TOKEN, F32)

    outs = pl.pallas_call(
        body, name=name,
        out_shape=(pltpu.SemaphoreType.DMA((n_sem,)), pltpu.SemaphoreType.DMA((n_sem,)), *_hbm_like(srcs), *_hbm_like(lands), _sds(TOKEN, F32)),
        in_specs=[HBM_SPEC] * nb + [ANY_SPEC] * nd,
        out_specs=(SEM_SPEC, SEM_SPEC, *[HBM_SPEC] * nb, pl.BlockSpec(memory_space=pltpu.VMEM)),
        input_output_aliases={i: 2 + i for i in range(nb)},
        compiler_params=pltpu.CompilerParams(has_side_effects=DATAFLOW))(*[_hbm(t) for t in srcs], *[_hbm(t) for t in lands], *deps)
    return dict(sems=outs[:2], thru=list(outs[2:2 + nb]), token=outs[-1], n=n)


def _split_wait(finish, handle, after, name):
    n = handle["n"]
    thru = handle["thru"]
    nb = len(thru)

    def body(*refs):
        finish(refs[:n], refs[n:nb], refs[nb], refs[nb + 1])

    outs = pl.pallas_call(
        body, name=name, out_shape=_hbm_like(thru), in_specs=[HBM_SPEC] * nb + [SEM_SPEC, SEM_SPEC] + [ANY_SPEC] * len(after),
        out_specs=[HBM_SPEC] * nb, input_output_aliases={i: i for i in range(nb)},
        compiler_params=pltpu.CompilerParams(has_side_effects=DATAFLOW))(*thru, *handle["sems"], *after)
    return list(outs[:n]), list(outs[n:])


def _gather_start(shards, name, deps=()):
    n = len(shards)
    my_id = _lin(_my_coords())
    lands = [lax.dynamic_update_slice(lax.empty((N_DEV,) + t.shape, t.dtype), t[None], (my_id, 0, 0)) for t in shards]

    def issue(src, land, send_sems, recv_sems):
        me = _my_coords()
        for w in range(n):
            for j, k in enumerate(GATHER_FLIPS):
                q = len(GATHER_FLIPS) * w + j
                pltpu.make_async_remote_copy(src_ref=src[w], dst_ref=land[w].at[_lin(me)], send_sem=send_sems.at[q], recv_sem=recv_sems.at[q],
                                             device_id=_flip(me, k), device_id_type=MESH).start()

    return _split_start(issue, shards, lands, len(GATHER_FLIPS) * n, name, deps)


def _gather_wait(handle, after, name):
    n = handle["n"]

    def finish(src, land, send_sems, recv_sems):
        me = _my_coords()
        for w in range(n):
            for j, k in enumerate(GATHER_FLIPS):
                q = len(GATHER_FLIPS) * w + j
                peer = _flip(me, k)
                cp = pltpu.make_async_remote_copy(src_ref=src[w], dst_ref=land[w].at[_lin(peer)], send_sem=send_sems.at[q], recv_sem=recv_sems.at[q],
                                                  device_id=peer, device_id_type=MESH)
                cp.wait_send()
                cp.wait_recv()

    return _split_wait(finish, handle, after, name)[1]


def _pass_copy(land, send_sems, recv_sems, w, j, arriving):
    me = _my_coords()
    blk = land[w].at[_lin(_flip(me, PASS_FLIPS[j] + (1 if arriving else 0)))]
    q = len(PASS_FLIPS) * w + j
    return pltpu.make_async_remote_copy(src_ref=blk, dst_ref=blk, send_sem=send_sems.at[q], recv_sem=recv_sems.at[q],
                                        device_id=_flip(me, 1), device_id_type=MESH)


def _pass_start(lands, name, deps=()):
    def issue(_, land, send_sems, recv_sems):
        for w in range(len(lands)):
            for j in range(len(PASS_FLIPS)):
                _pass_copy(land, send_sems, recv_sems, w, j, False).start()

    return _split_start(issue, [], lands, len(PASS_FLIPS) * len(lands), name, deps)


def _pass_wait(handle, after, name):
    def finish(_, land, send_sems, recv_sems):
        for w in range(len(handle["thru"])):
            for j in range(len(PASS_FLIPS)):
                _pass_copy(land, send_sems, recv_sems, w, j, False).wait_send()
                _pass_copy(land, send_sems, recv_sems, w, j, True).wait_recv()

    return _split_wait(finish, handle, after, name)[1]


def _gather_pass(lands, name):
    n = len(lands)
    n_p = len(PASS_FLIPS)

    def body(*refs):
        land = refs[n:2 * n]
        send_sems, recv_sems = refs[2 * n:]
        me = _my_coords()
        sibling = _flip(me, 1)
        sent = []
        for w in range(n):
            for j, k in enumerate(PASS_FLIPS):
                blk = land[w].at[_lin(_flip(me, k))]
                cp = pltpu.make_async_remote_copy(src_ref=blk, dst_ref=blk, send_sem=send_sems.at[n_p * w + j], recv_sem=recv_sems.at[n_p * w + j],
                                                  device_id=sibling, device_id_type=MESH)
                cp.start()
                sent.append(cp)
        for w in range(n):
            for j, k in enumerate(PASS_FLIPS):
                blk = land[w].at[_lin(_flip(me, k + 1))]
                pltpu.make_async_remote_copy(src_ref=blk, dst_ref=blk, send_sem=send_sems.at[n_p * w + j], recv_sem=recv_sems.at[n_p * w + j],
                                             device_id=sibling, device_id_type=MESH).wait_recv()
        for cp in sent:
            cp.wait_send()

    return pl.pallas_call(body, in_specs=[ANY_SPEC] * n, out_specs=[ANY_SPEC] * n, out_shape=[_sds(t.shape, t.dtype) for t in lands],
                          input_output_aliases={i: i for i in range(n)},
                          scratch_shapes=[pltpu.SemaphoreType.DMA((n_p * n,)), pltpu.SemaphoreType.DMA((n_p * n,))], name=name)(*lands)


CHIP_FLIPS = (0, 2, 4, 6)


def _pair_copy(src, land, send_sems, recv_sems, w, j):
    me = _my_coords()
    q = len(CHIP_FLIPS) * w + j
    return pltpu.make_async_remote_copy(src_ref=src[w].at[_lin(_flip(me, CHIP_FLIPS[j] + 1))], dst_ref=land[w].at[j], send_sem=send_sems.at[q],
                                        recv_sem=recv_sems.at[q], device_id=_flip(me, 1), device_id_type=MESH)


def _pair_exchange(grads, name):
    n = len(grads)

    def body(*refs):
        src, land = refs[:n], refs[n:2 * n]
        send_sems, recv_sems = refs[2 * n:]
        sent = [_pair_copy(src, land, send_sems, recv_sems, w, j) for w in range(n) for j in range(len(CHIP_FLIPS))]
        for cp in sent:
            cp.start()
        for cp in sent:
            cp.wait_recv()
        for cp in sent:
            cp.wait_send()

    outs = pl.pallas_call(body, in_specs=[ANY_SPEC] * n, out_specs=[ANY_SPEC] * n,
                          out_shape=[_sds((len(CHIP_FLIPS),) + g.shape[1:], g.dtype) for g in grads],
                          scratch_shapes=[pltpu.SemaphoreType.DMA((len(CHIP_FLIPS) * n,))] * 2, name=name)(*grads)
    return list(outs)


def _pair_start(grads, name, deps=()):
    n = len(grads)
    lands = [lax.empty((len(CHIP_FLIPS),) + g.shape[1:], g.dtype) for g in grads]

    def issue(src, land, send_sems, recv_sems):
        for w in range(n):
            for j in range(len(CHIP_FLIPS)):
                _pair_copy(src, land, send_sems, recv_sems, w, j).start()

    return _split_start(issue, grads, lands, len(CHIP_FLIPS) * n, name, deps)


def _pair_wait(handle, after, name):
    n = handle["n"]

    def finish(src, land, send_sems, recv_sems):
        for w in range(n):
            for j in range(len(CHIP_FLIPS)):
                cp = _pair_copy(src, land, send_sems, recv_sems, w, j)
                cp.wait_send()
                cp.wait_recv()

    return _split_wait(finish, handle, after, name)


def _pair_add(grad, theirs, name):
    p, r, c = theirs.shape
    tr = _row_tile(r, c, PAIR_TILE)
    me = _my_coords()
    ids = jnp.stack([_lin(_flip(me, k)) for k in CHIP_FLIPS]).astype(jnp.int32)

    def body(ids_ref, a_ref, b_ref, o_ref):
        o_ref[...] = (a_ref[...].astype(F32) + b_ref[...].astype(F32)).astype(o_ref.dtype)

    blk = pl.BlockSpec((None, tr, c), lambda j, i, ids_ref: (j, i, 0))
    return pl.pallas_call(
        body, out_shape=_sds((p, r, c), theirs.dtype), compiler_params=_params(2), name=name,
        grid_spec=pltpu.PrefetchScalarGridSpec(
            num_scalar_prefetch=1, grid=(p, r // tr),
            in_specs=[pl.BlockSpec((None, tr, c), lambda j, i, ids_ref: (ids_ref[j], i, 0)), blk], out_specs=blk))(ids, grad, theirs)


def _chips_start(parts, name, deps=()):
    n = len(parts)
    n_c = len(CHIP_FLIPS) - 1
    lands = [lax.empty((n_c,) + t.shape[1:], t.dtype) for t in parts]

    def issue(src, land, send_sems, recv_sems):
        me = _my_coords()
        for w in range(n):
            for j in range(1, n_c + 1):
                q = n_c * w + j - 1
                pltpu.make_async_remote_copy(src_ref=src[w].at[j], dst_ref=land[w].at[j - 1], send_sem=send_sems.at[q], recv_sem=recv_sems.at[q],
                                             device_id=_flip(me, CHIP_FLIPS[j]), device_id_type=MESH).start()

    return _split_start(issue, parts, lands, n_c * n, name, deps)


def _chips_wait(handle, after, name):
    n = handle["n"]
    n_c = len(CHIP_FLIPS) - 1

    def finish(src, land, send_sems, recv_sems):
        me = _my_coords()
        for w in range(n):
            for j in range(1, n_c + 1):
                q = n_c * w + j - 1
                cp = pltpu.make_async_remote_copy(src_ref=src[w].at[j], dst_ref=land[w].at[j - 1], send_sem=send_sems.at[q], recv_sem=recv_sems.at[q],
                                                  device_id=_flip(me, CHIP_FLIPS[j]), device_id_type=MESH)
                cp.wait_send()
                cp.wait_recv()

    return _split_wait(finish, handle, after, name)


def _after(t, *tokens):
    for tok in tokens:
        t = t + tok[0:1, 0:1]
    return t


def _rope_tables(positions):
    half = ROT // 2
    inv_freq = ROPE_THETA ** (-jnp.arange(0, ROT, 2, dtype=F32) / ROT)
    ang = positions.astype(F32).reshape(-1, 1) * inv_freq
    cos, sin = jnp.cos(ang), jnp.sin(ang)
    s = ang.shape[0]
    pad = jnp.zeros((s, HEAD_DIM - ROT), F32)
    zero = jnp.zeros((s, half), F32)
    two = lambda t: jnp.concatenate([t, t], axis=1)
    return (two(jnp.concatenate([cos, cos, pad + 1.0], axis=1)), two(jnp.concatenate([-sin, zero, pad], axis=1)),
            two(jnp.concatenate([zero, sin, pad], axis=1)))


def _local_step(x, tgt, tabs, mod, sinks_pad, hl, hg_norm, g_pre_mix, g_post_mix, g_pre_ffn, g_post_ffn, weights, prefetch, scatter, scatter_on):
    s = x.shape[0]
    h1 = _pre_fwd(x, g_pre_mix, mod, 1, 0, "pre_mix_fwd")
    (w_in_a,) = weights("in_a", h1)
    proj = _mm_nt(h1[:, :D // 2], w_in_a, 256, IN_COLS // 2, D // 2, F32, "proj_mm_a")
    (w_in_b,) = weights("in_b", proj)
    proj = _mm_nt(h1[:, D // 2:], w_in_b, 256, IN_COLS // 2, D // 2, F32, "proj_mm_b", add=proj)
    att = _attn_fwd(proj, tabs, _after(sinks_pad, prefetch("mix", proj)))
    o_raw, states = _hgrn_fwd(proj, hl)
    ohg = _hgout_fwd(o_raw, proj, hg_norm)
    w_attn_dm, w_hgrn_dm, w_out = weights("mix", ohg)
    natural = lambda w_dm: w_dm.transpose(1, 0, 2).reshape(w_dm.shape[1], D)
    pieces = lambda g: g.reshape(g.shape[0], N_DEV, D // N_DEV).transpose(1, 0, 2)
    w_attn, w_hgrn = natural(w_attn_dm), natural(w_hgrn_dm)
    y_a = _mm_nn(att, w_attn, s, 512, ATT_W, F32, "attn_proj_mm")
    y_h = _mm_nn(ohg, w_hgrn, s, 512, HG_W, F32, "hgrn_proj_mm")
    merged = _merge_fwd(y_a, y_h, proj)
    y = _mm_nn(merged, w_out, s, 512, D, F32, "out_mm")
    x1 = _post_fwd(x, y, g_post_mix, mod, 2, "post_mix_fwd")
    h2 = _pre_fwd(x1, g_pre_ffn, _after(mod, prefetch("ffn_in", x1)), 4, 3, "pre_ffn_fwd")
    (w_ffn_in_dm,) = weights("ffn_in", h2)
    gu = _mm_nn_dm(h2, w_ffn_in_dm, s // 2, F32, "ffn_in_mm")
    act = _swiglu_fwd(gu, deps=[prefetch("ffn_out", gu)])
    (w_ffn_out,) = weights("ffn_out", act)
    y2 = _mm_nn(act, w_ffn_out, 512, 512, FFN, F32, "ffn_out_mm")
    err, loss, dy2, d_gate2, dg_post_ffn = _post_loss_bwd(x1, y2, g_post_ffn, mod, 5, tgt, "post_ffn_loss_bwd")
    gw_ffn_out = _mm_tn(act, dy2, 512, D, BF16, "ffn_out_dw")
    t_pair = scatter([gw_ffn_out.reshape(N_DEV, FFN // N_DEV, D)], "ffn_out")
    d_act = _mm_nt(dy2, w_ffn_out, s, 512, D, F32, "ffn_out_dx", deps=[t_pair])
    dgu = _swiglu_bwd(d_act, gu)
    t_out = scatter_on("ffn_out", dgu)
    gw_ffn_in = _mm_tn_dm(h2, dgu, 1024, BF16, "ffn_in_dw")
    t_pair = scatter([gw_ffn_in], "ffn_in")
    dh2 = _mm_nt_dm(dgu, w_ffn_in_dm, s, 1024, F32, "ffn_in_dx", deps=[t_pair])
    mod = _after(mod, t_out)
    dx1, d_shift2, d_scale2, dg_pre_ffn = _pre_bwd(dh2, x1, err, g_pre_ffn, mod, 4, "pre_ffn_bwd")
    dy, d_gate1, dg_post_mix = _post_bwd(dx1, y, g_post_mix, mod, 2, "post_mix_bwd")
    t_in = scatter_on("ffn_in", dy)
    d_merged = _mm_nt(dy, w_out, s, 512, D, F32, "out_dx")
    gw_out = _mm_tn(merged, dy, 512, D, BF16, "out_dw")
    dy_a, dy_h, d_gate_a, d_gate_h = _merge_bwd(d_merged, y_a, y_h, proj)
    gw_attn = pieces(_mm_tn(att, dy_a, 512, D, BF16, "attn_proj_dw"))
    gw_hgrn = pieces(_mm_tn(ohg, dy_h, 512, D, BF16, "hgrn_proj_dw"))
    t_pair = scatter([gw_attn, gw_hgrn, gw_out.reshape(N_DEV, D // N_DEV, D)], "mix")
    d_att = _mm_nt(dy_a, w_attn, s, 512, D, F32, "attn_proj_dx")
    d_ohg = _mm_nt(dy_h, w_hgrn, s, 512, D, F32, "hgrn_proj_dx", deps=[t_pair])
    d_o, d_gh, d_hg_norm = _hgout_bwd(d_ohg, o_raw, proj, _after(hg_norm, t_in))
    d_qh, d_fh, d_ih, d_hl = _hgrn_bwd(proj, hl, states, d_o)
    t_mix = scatter_on("mix", d_qh)
    d_qa, d_ka, d_va, d_sinks = _attn_bwd(proj, tabs, _after(sinks_pad, t_mix), d_att)
    d_proj = jnp.concatenate([d_qa, d_ka.astype(BF16), d_va.astype(BF16), d_qh, d_fh, d_ih, d_gh, d_gate_a, d_gate_h], axis=1)
    dh1 = jnp.concatenate([_mm_nn(d_proj, w_half, s // 2, 512, IN_COLS // 2, F32, "proj_dx_" + tag)
                           for tag, w_half in (("a", w_in_a), ("b", w_in_b))], axis=1)
    grad_x, d_shift1, d_scale1, dg_pre_mix = _pre_bwd(dh1, x, dx1, g_pre_mix, mod, 1, "pre_mix_bwd")
    d_mod = jnp.concatenate([d_shift1, d_scale1, d_gate1, d_shift2, d_scale2, d_gate2], axis=1)
    small = [d_mod, dg_pre_mix, dg_post_mix, dg_pre_ffn, dg_post_ffn, d_hl.reshape(1, 2 * HG_W), d_hg_norm, d_sinks]
    return loss, grad_x, small, h1, d_proj


def kernel(x, c, positions, w_ada, b_ada, g_pre_mix, g_post_mix, g_pre_ffn, g_post_ffn, w_in, attn_sinks, w_attn_proj, hg_lower_bounds, hg_norm, w_hgrn_proj, w_out, w_ffn_in, w_ffn_out, loss_target, m_w_ada, m_b_ada, m_g_pre_mix, m_g_post_mix, m_g_pre_ffn, m_g_post_ffn, m_w_in, m_attn_sinks, m_w_attn_proj, m_hg_lower_bounds, m_hg_norm, m_w_hgrn_proj, m_w_out, m_w_ffn_in, m_w_ffn_out, v_w_ada, v_b_ada, v_g_pre_mix, v_g_post_mix, v_g_pre_ffn, v_g_post_ffn, v_w_in, v_attn_sinks, v_w_attn_proj, v_hg_lower_bounds, v_hg_norm, v_w_hgrn_proj, v_w_out, v_w_ffn_in, v_w_ffn_out):
    my_id = _lin(_my_coords())
    s = x.shape[1]
    n_ada = w_ada.shape[2]

    c_all = _exchange_small(c.reshape(1, 1, D), True, "gather_c").reshape(N_DEV, D)
    b_cols = lax.dynamic_slice(b_ada, (0, my_id * n_ada), (1, n_ada))
    mod_part = _mod_part(c_all, w_ada[0], b_cols)
    mod = _exchange_small(mod_part.reshape(N_DEV, 1, n_ada), False, "scatter_mod").reshape(1, N_MOD * D)
    groups = {"in_a": [w_in[0].T[:, :D // 2]], "in_b": [w_in[0].T[:, D // 2:]], "mix": [w_attn_proj[0], w_hgrn_proj[0], w_out[0]],
              "ffn_in": [w_ffn_in[0]], "ffn_out": [w_ffn_out[0]]}

    def start(group, dep):
        shards, dep = lax.optimization_barrier((groups[group], dep))
        return _gather_start([t.astype(BF16) for t in shards], "gather_start_" + group, deps=[dep])

    gathers = {"in_a": start("in_a", mod)}
    gathers["in_b"] = start("in_b", gathers["in_a"]["token"])
    gathers["mix"] = start("mix", gathers["in_b"]["token"])
    gathers["ffn_in"] = start("ffn_in", gathers["mix"]["token"])
    gathers["ffn_out"] = start("ffn_out", gathers["ffn_in"]["token"])

    passes = {}

    def prefetch(group, after):
        lands = _gather_wait(gathers[group], [after], "gather_wait_" + group)
        passes[group] = _pass_start(lands, "gather_pass_start_" + group)
        return passes[group]["token"]

    def weights(group, after):
        if group in passes:
            lands = _pass_wait(passes[group], [after], "gather_pass_wait_" + group)
        else:
            after = [after, gathers["ffn_out"]["token"]]
            lands = _gather_pass(_gather_wait(gathers[group], after, "gather_wait_" + group), "gather_pass_" + group)
        if group in ("in_a", "in_b"):
            return (lands[0].reshape(IN_COLS, D // 2),)
        if group == "mix":
            return lands[0], lands[1], lands[2].reshape(D, D)
        return (lands[0],) if group == "ffn_in" else (lands[0].reshape(FFN, D),)

    pairs, scatters = {}, {}

    def scatter(grads, group):
        pairs[group] = _pair_start(grads, "scatter_pair_" + group)
        return pairs[group]["token"]

    def scatter_on(group, after):
        if group in pairs:
            local, theirs = _pair_wait(pairs[group], [after], "scatter_pair_wait_" + group)
        else:
            local, theirs = after, _pair_exchange(after, "scatter_pair_" + group)
        parts = [_pair_add(g, t, "scatter_pair_add_%s_%d" % (group, k)) for k, (g, t) in enumerate(zip(local, theirs))]
        scatters[group] = _chips_start(parts, "scatter_start_" + group)
        return scatters[group]["token"]

    sinks_pad = jnp.pad(attn_sinks, ((0, 0), (0, LANE - ATT_HEADS)))
    loss, grad_x, small, h1, d_proj = _local_step(
        x[0], loss_target[0], _rope_tables(positions), mod, sinks_pad, hg_lower_bounds, hg_norm, g_pre_mix, g_post_mix, g_pre_ffn, g_post_ffn,
        weights, prefetch, scatter, scatter_on)
    loss = lax.psum(loss[0, 0], ("x", "y", "c"))

    sizes = [t.shape[1] for t in small]
    parts = _exchange_small(jnp.concatenate(small, axis=1).reshape(1, 1, sum(sizes)), True, "gather_small_grads")
    dep = parts
    for half, cols in (("in_a", slice(0, D // 2)), ("in_b", slice(D // 2, D))):
        gw_half = _mm_tn(d_proj, h1[:, cols], 256, D // 2, BF16, "proj_dw_" + half, deps=[dep])
        dep = scatter_on(half, [gw_half.reshape(N_DEV, IN_COLS // N_DEV, D // 2)])
    offs = [sum(sizes[:k]) for k in range(len(sizes))]
    piece = lambda k, n=None: parts[:, :, offs[k]:offs[k] + (sizes[k] if n is None else n)]
    small_w = [(piece(0), b_ada, m_b_ada, v_b_ada), (piece(1), g_pre_mix, m_g_pre_mix, v_g_pre_mix),
               (piece(2), g_post_mix, m_g_post_mix, v_g_post_mix), (piece(3), g_pre_ffn, m_g_pre_ffn, v_g_pre_ffn),
               (piece(4), g_post_ffn, m_g_post_ffn, v_g_post_ffn),
               (piece(5).reshape(N_DEV, 2, HG_W), hg_lower_bounds, m_hg_lower_bounds, v_hg_lower_bounds),
               (piece(6), hg_norm, m_hg_norm, v_hg_norm), (piece(7, ATT_HEADS), attn_sinks, m_attn_sinks, v_attn_sinks)]
    names = ["b_ada", "g_pre_mix", "g_post_mix", "g_pre_ffn", "g_post_ffn", "hg_lower_bounds", "hg_norm", "attn_sinks"]
    res = {n: _adamw(p, w, m, v, "adamw_" + n) for n, (p, w, m, v) in zip(names, small_w)}

    dmod_cols = lax.dynamic_slice(parts.reshape(N_DEV, -1), (0, my_id * n_ada), (N_DEV, n_ada))
    g_w_ada = _grad_w_ada(c_all.T, dmod_cols)
    res["w_ada"] = [g_w_ada] + list(_adamw(g_w_ada[None], w_ada[0], m_w_ada[0], v_w_ada[0], "adamw_w_ada", emit_grad=False))

    big = {"ffn_out": [("w_ffn_out", w_ffn_out, m_w_ffn_out, v_w_ffn_out)], "ffn_in": [("w_ffn_in", w_ffn_in, m_w_ffn_in, v_w_ffn_in)],
           "mix": [("w_attn_proj", w_attn_proj, m_w_attn_proj, v_w_attn_proj), ("w_hgrn_proj", w_hgrn_proj, m_w_hgrn_proj, v_w_hgrn_proj),
                   ("w_out", w_out, m_w_out, v_w_out)]}
    after = [scatters["in_b"]["token"]]
    for group, members in big.items():
        local, lands = _chips_wait(scatters[group], after, "scatter_wait_" + group)
        for (n, w, m, v), mine, land in zip(members, local, lands):
            res[n] = _adamw(land, w[0], m[0], v[0], "adamw_" + n, own=mine[0])
            after = after + [res[n][1]]
    after = [res[n][1] for n in res]
    halves = [_chips_wait(scatters[half], after, "scatter_wait_" + half) for half in ("in_a", "in_b")]
    own = jnp.concatenate([local[0][0] for local, _ in halves], axis=1)
    land = jnp.concatenate([lands[0] for _, lands in halves], axis=2)
    res["w_in"] = [t.T for t in _adamw(land, w_in[0].T, m_w_in[0].T, v_w_in[0].T, "adamw_w_in", own=own, max_elems=WIDE_TILE)]

    order = ["w_ada", "b_ada", "g_pre_mix", "g_post_mix", "g_pre_ffn", "g_post_ffn", "w_in", "attn_sinks", "w_attn_proj",
             "hg_lower_bounds", "hg_norm", "w_hgrn_proj", "w_out", "w_ffn_in", "w_ffn_out"]
    lead = {"w_ada", "w_in", "w_attn_proj", "w_hgrn_proj", "w_out", "w_ffn_in", "w_ffn_out"}
    outs = [loss, grad_x[None]]
    for k in range(4):
        outs += [res[n][k][None] if n in lead else res[n][k] for n in order]
    return tuple(outs)
```

```python
import functools

import jax
import jax.numpy as jnp
from jax import lax
from jax.experimental import pallas as pl
from jax.experimental.pallas import tpu as pltpu

F32 = jnp.float32
BF16 = jnp.bfloat16

N_DEV = 8
D = 2048
ATT_HEADS = 16
KV_HEADS = 2
HEAD_DIM = 64
GROUP = ATT_HEADS // KV_HEADS
ATT_W = ATT_HEADS * HEAD_DIM
BLK = 128
ROT = HEAD_DIM // 4
ROPE_THETA = 500000.0
HG_HEADS = 8
HG_K = 128
HG_W = HG_HEADS * HG_K
CHUNK = 64
SUB = 16
FFN = 5632
N_MOD = 6
EPS = 1e-6
LANE = 128
Q_A, K_A, V_A, Q_H, F_H, I_H, G_H, GT_A, GT_H, IN_COLS = 0, 1024, 1152, 1280, 2304, 3328, 4352, 5376, 7424, 9472

ADAM_LR, ADAM_B1, ADAM_B2, ADAM_EPS, ADAM_WD, ADAM_STEP = 0.001, 0.9, 0.999, 1e-08, 0.01, 10

TR = 256
HG_TB = 512
VMEM_BIG = 56 << 20
MESH = pl.DeviceIdType.MESH


def _sds(shape, dtype):
    return jax.ShapeDtypeStruct(shape, dtype)


def _params(n_axes, vmem=None):
    return pltpu.CompilerParams(dimension_semantics=("arbitrary",) * n_axes, vmem_limit_bytes=vmem)


def _sig(t):
    return 1.0 / (1.0 + jnp.exp(-t))


def _dot(a, b, dims):
    return lax.dot_general(a, b, (dims, ((), ())), preferred_element_type=F32)


NN = ((1,), (0,))
NT = ((1,), (1,))
TN = ((0,), (0,))


def _matmul(a, b, a_spec, b_spec, o_spec, out_shape, grid, dims, acc_shape, name, deps=(), add=None):
    nk = grid[2]
    nd = len(deps)
    extra = [] if add is None else [add]

    def body(a_ref, b_ref, *rest):
        o_ref, scratch = rest[nd + len(extra)], rest[nd + len(extra) + 1:]
        part = _dot(a_ref[...], b_ref[...], dims)
        if add is not None:
            assert nk == 1
            part = part + rest[nd][...]
        if nk == 1:
            o_ref[...] = part.astype(o_ref.dtype)
        else:
            acc = scratch[0]
            k = pl.program_id(2)

            @pl.when(k == 0)
            def _():
                acc[...] = part

            @pl.when(k > 0)
            def _():
                acc[...] += part

            @pl.when(k == nk - 1)
            def _():
                o_ref[...] = acc[...].astype(o_ref.dtype)

    return pl.pallas_call(
        body, grid=grid, in_specs=[a_spec, b_spec] + [pl.BlockSpec(memory_space=pl.ANY)] * nd + [o_spec] * len(extra),
        out_specs=o_spec, out_shape=out_shape, scratch_shapes=[pltpu.VMEM(acc_shape, F32)] if nk > 1 else [],
        input_output_aliases={2 + nd: 0} if extra else {},
        compiler_params=_params(3, VMEM_BIG), name=name)(a, b, *deps, *extra)


def _mm_nn(a, b, tm, tn, tk, out_dtype, name):
    m, k = a.shape
    n = b.shape[1]
    return _matmul(a, b, pl.BlockSpec((tm, tk), lambda j, i, kk: (i, kk)), pl.BlockSpec((tk, tn), lambda j, i, kk: (kk, j)),
                   pl.BlockSpec((tm, tn), lambda j, i, kk: (i, j)), _sds((m, n), out_dtype),
                   (n // tn, m // tm, k // tk), NN, (tm, tn), name)


def _mm_nn_dm(a, b, tm, out_dtype, name):
    m, k = a.shape
    n = b.shape[2]
    return _matmul(a, b, pl.BlockSpec((tm, k), lambda j, i, kk: (i, 0)), pl.BlockSpec((None, k, n), lambda j, i, kk: (j, 0, 0)),
                   pl.BlockSpec((tm, n), lambda j, i, kk: (i, j)), _sds((m, N_DEV * n), out_dtype),
                   (N_DEV, m // tm, 1), NN, (tm, n), name)


def _mm_nt(a, b, tm, tn, tk, out_dtype, name, deps=(), add=None):
    m, k = a.shape
    n = b.shape[0]
    return _matmul(a, b, pl.BlockSpec((tm, tk), lambda j, i, kk: (i, kk)), pl.BlockSpec((tn, tk), lambda j, i, kk: (j, kk)),
                   pl.BlockSpec((tm, tn), lambda j, i, kk: (i, j)), _sds((m, n), out_dtype),
                   (n // tn, m // tm, k // tk), NT, (tm, tn), name, deps, add)


def _mm_nt_dm(a, b, tm, tn, out_dtype, name, deps=()):
    m = a.shape[0]
    n_out, n = b.shape[1], b.shape[2]
    return _matmul(a, b, pl.BlockSpec((tm, n), lambda j, i, kk: (i, kk)), pl.BlockSpec((None, tn, n), lambda j, i, kk: (kk, j, 0)),
                   pl.BlockSpec((tm, tn), lambda j, i, kk: (i, j)), _sds((m, n_out), out_dtype),
                   (n_out // tn, m // tm, N_DEV), NT, (tm, tn), name, deps)


def _mm_tn(a, b, tm, tn, out_dtype, name, deps=()):
    s, m = a.shape
    n = b.shape[1]
    return _matmul(a, b, pl.BlockSpec((s, tm), lambda j, i, kk: (0, i)), pl.BlockSpec((s, tn), lambda j, i, kk: (0, j)),
                   pl.BlockSpec((tm, tn), lambda j, i, kk: (i, j)), _sds((m, n), out_dtype),
                   (n // tn, m // tm, 1), TN, (tm, tn), name, deps)


def _mm_tn_dm(a, b, tm, out_dtype, name):
    s, m = a.shape
    n = b.shape[1] // N_DEV
    return _matmul(a, b, pl.BlockSpec((s, tm), lambda j, i, kk: (0, i)), pl.BlockSpec((s, n), lambda j, i, kk: (0, j)),
                   pl.BlockSpec((None, tm, n), lambda j, i, kk: (j, i, 0)), _sds((N_DEV, m, n), out_dtype),
                   (N_DEV, m // tm, 1), TN, (tm, n), name)


def _row_spec():
    return pl.BlockSpec((TR, D), lambda i: (i, 0))


def _vec_spec(k=0):
    return pl.BlockSpec((1, D), lambda i: (0, k))


def _acc_rows(ref, first, val):
    @pl.when(first)
    def _():
        ref[...] = val

    @pl.when(jnp.logical_not(first))
    def _():
        ref[...] += val


def _pre_fwd(x, g, mod, k_scale, k_shift, name):
    s = x.shape[0]

    def body(x_ref, g_ref, sc_ref, sh_ref, h_ref):
        xv = x_ref[...]
        r = lax.rsqrt(jnp.mean(xv * xv, axis=-1, keepdims=True) + EPS)
        n = xv * r * g_ref[...]
        h_ref[...] = (n * (1.0 + sc_ref[...]) + sh_ref[...]).astype(h_ref.dtype)

    return pl.pallas_call(body, grid=(s // TR,), in_specs=[_row_spec(), _vec_spec(), _vec_spec(k_scale), _vec_spec(k_shift)],
                          out_specs=_row_spec(), out_shape=_sds((s, D), BF16), compiler_params=_params(1), name=name)(x, g, mod, mod)


def _post_fwd(x, y, g, mod, k_gate, name):
    s = x.shape[0]

    def body(x_ref, y_ref, g_ref, gt_ref, o_ref):
        yv = y_ref[...]
        r = lax.rsqrt(jnp.mean(yv * yv, axis=-1, keepdims=True) + EPS)
        o_ref[...] = x_ref[...] + gt_ref[...] * (yv * r * g_ref[...])

    return pl.pallas_call(body, grid=(s // TR,), in_specs=[_row_spec(), _row_spec(), _vec_spec(), _vec_spec(k_gate)],
                          out_specs=_row_spec(), out_shape=_sds((s, D), F32), compiler_params=_params(1), name=name)(x, y, g, mod)


def _post_loss_bwd(x, y, g, mod, k_gate, tgt, name):
    s = x.shape[0]

    def body(x_ref, y_ref, g_ref, gt_ref, t_ref, e_ref, loss_ref, dy_ref, dgt_ref, dg_ref):
        first = pl.program_id(0) == 0
        yv, gv, gate = y_ref[...], g_ref[...], gt_ref[...]
        r = lax.rsqrt(jnp.mean(yv * yv, axis=-1, keepdims=True) + EPS)
        yh = yv * r
        err = x_ref[...] + gate * (yh * gv) - t_ref[...]
        e = err * (1.0 / D)
        e_ref[...] = e
        _acc_rows(loss_ref, first, 0.5 * jnp.sum(jnp.mean(err * err, axis=-1, keepdims=True), axis=0, keepdims=True))
        dn = e * gate
        dgn = dn * gv
        dy_ref[...] = (r * (dgn - yh * jnp.mean(dgn * yh, axis=-1, keepdims=True))).astype(dy_ref.dtype)
        _acc_rows(dgt_ref, first, jnp.sum(e * (yh * gv), axis=0, keepdims=True))
        _acc_rows(dg_ref, first, jnp.sum(dn * yh, axis=0, keepdims=True))

    return pl.pallas_call(body, grid=(s // TR,),
                          in_specs=[_row_spec(), _row_spec(), _vec_spec(), _vec_spec(k_gate), _row_spec()],
                          out_specs=[_row_spec(), pl.BlockSpec((1, 1), lambda i: (0, 0)), _row_spec(), _vec_spec(), _vec_spec()],
                          out_shape=[_sds((s, D), F32), _sds((1, 1), F32), _sds((s, D), BF16), _sds((1, D), F32), _sds((1, D), F32)],
                          compiler_params=_params(1), name=name)(x, y, g, mod, tgt)


def _pre_bwd(dh, x, res, g, mod, k_scale, name):
    s = x.shape[0]

    def body(dh_ref, x_ref, res_ref, g_ref, sc_ref, dx_ref, dsh_ref, dsc_ref, dg_ref):
        first = pl.program_id(0) == 0
        xv, dh_v, gv = x_ref[...], dh_ref[...], g_ref[...]
        r = lax.rsqrt(jnp.mean(xv * xv, axis=-1, keepdims=True) + EPS)
        xh = xv * r
        dn = dh_v * (1.0 + sc_ref[...])
        dgn = dn * gv
        dx_ref[...] = res_ref[...] + r * (dgn - xh * jnp.mean(dgn * xh, axis=-1, keepdims=True))
        _acc_rows(dsh_ref, first, jnp.sum(dh_v, axis=0, keepdims=True))
        _acc_rows(dsc_ref, first, jnp.sum(dh_v * (xh * gv), axis=0, keepdims=True))
        _acc_rows(dg_ref, first, jnp.sum(dn * xh, axis=0, keepdims=True))

    return pl.pallas_call(body, grid=(s // TR,),
                          in_specs=[_row_spec(), _row_spec(), _row_spec(), _vec_spec(), _vec_spec(k_scale)],
                          out_specs=[_row_spec(), _vec_spec(), _vec_spec(), _vec_spec()],
                          out_shape=[_sds((s, D), F32)] + [_sds((1, D), F32)] * 3,
                          compiler_params=_params(1), name=name)(dh, x, res, g, mod)


def _post_bwd(dx, y, g, mod, k_gate, name):
    s = y.shape[0]

    def body(dx_ref, y_ref, g_ref, gt_ref, dy_ref, dgt_ref, dg_ref):
        first = pl.program_id(0) == 0
        yv, dxv, gv = y_ref[...], dx_ref[...], g_ref[...]
        r = lax.rsqrt(jnp.mean(yv * yv, axis=-1, keepdims=True) + EPS)
        yh = yv * r
        dn = dxv * gt_ref[...]
        dgn = dn * gv
        dy_ref[...] = (r * (dgn - yh * jnp.mean(dgn * yh, axis=-1, keepdims=True))).astype(dy_ref.dtype)
        _acc_rows(dgt_ref, first, jnp.sum(dxv * (yh * gv), axis=0, keepdims=True))
        _acc_rows(dg_ref, first, jnp.sum(dn * yh, axis=0, keepdims=True))

    return pl.pallas_call(body, grid=(s // TR,), in_specs=[_row_spec(), _row_spec(), _vec_spec(), _vec_spec(k_gate)],
                          out_specs=[_row_spec(), _vec_spec(), _vec_spec()],
                          out_shape=[_sds((s, D), BF16), _sds((1, D), F32), _sds((1, D), F32)],
                          compiler_params=_params(1), name=name)(dx, y, g, mod)


SW_TN = 1408
SW_TR = 512
TALL = 2048


def _swiglu_fwd(gu, deps=()):
    s = gu.shape[0]
    nb = FFN // SW_TN

    def body(g_ref, u_ref, *rest):
        a_ref = rest[len(deps)]
        gv = g_ref[...]
        a_ref[...] = (gv * _sig(gv) * u_ref[...]).astype(a_ref.dtype)

    return pl.pallas_call(body, grid=(s // SW_TR, nb),
                          in_specs=[pl.BlockSpec((SW_TR, SW_TN), lambda i, j: (i, j)), pl.BlockSpec((SW_TR, SW_TN), lambda i, j: (i, j + nb))]
                          + [pl.BlockSpec(memory_space=pl.ANY)] * len(deps),
                          out_specs=pl.BlockSpec((SW_TR, SW_TN), lambda i, j: (i, j)), out_shape=_sds((s, FFN), BF16),
                          compiler_params=_params(2, 48 << 20), name="swiglu_fwd")(gu, gu, *deps)


def _swiglu_bwd(dact, gu):
    s = gu.shape[0]
    nb = FFN // SW_TN
    n_steps = (s // SW_TR) * nb

    def body(da_ref, g_ref, u_ref, o_ref, buf, sems):
        i, j = pl.program_id(0), pl.program_id(1)
        step = i * nb + j
        slot = step % 2

        def tiles(sl):
            rows = pl.ds(pl.multiple_of(i * SW_TR, SW_TR), SW_TR)
            return [pltpu.make_async_copy(buf.at[sl, h], o_ref.at[rows, pl.ds(pl.multiple_of((j + nb * h) * SW_TN, LANE), SW_TN)], sems.at[sl, h])
                    for h in range(2)]

        @pl.when(step >= 2)
        def _():
            for cp in tiles(slot):
                cp.wait()

        gv, da = g_ref[...], da_ref[...]
        sg = _sig(gv)
        buf[slot, 0] = (da * u_ref[...] * (sg * (1.0 + gv * (1.0 - sg)))).astype(buf.dtype)
        buf[slot, 1] = (da * (gv * sg)).astype(buf.dtype)
        for cp in tiles(slot):
            cp.start()

        @pl.when(step == n_steps - 1)
        def _():
            for cp in tiles(slot) + (tiles(1 - slot) if n_steps > 1 else []):
                cp.wait()

    blk = lambda f: pl.BlockSpec((SW_TR, SW_TN), f)
    return pl.pallas_call(body, grid=(s // SW_TR, nb),
                          in_specs=[blk(lambda i, j: (i, j)), blk(lambda i, j: (i, j)), blk(lambda i, j: (i, j + nb))],
                          out_specs=pl.BlockSpec(memory_space=pl.ANY), out_shape=_sds((s, 2 * FFN), BF16),
                          scratch_shapes=[pltpu.VMEM((2, 2, SW_TR, SW_TN), BF16), pltpu.SemaphoreType.DMA((2, 2))],
                          compiler_params=_params(2, 48 << 20), name="swiglu_bwd")(dact, gu, gu)


MG_TN = 256


def _merge_fwd(y_a, y_h, proj):
    s = y_a.shape[0]
    tn = MG_TN
    ba, bh = GT_A // tn, GT_H // tn

    def body(ya_ref, yh_ref, ga_ref, gh_ref, m_ref):
        m_ref[...] = (_sig(ga_ref[...]) * ya_ref[...] + _sig(gh_ref[...]) * yh_ref[...]).astype(m_ref.dtype)

    tr = min(s, TALL)
    blk = lambda f: pl.BlockSpec((tr, tn), f)
    return pl.pallas_call(body, grid=(s // tr, D // tn),
                          in_specs=[blk(lambda i, j: (i, j)), blk(lambda i, j: (i, j)), blk(lambda i, j: (i, j + ba)), blk(lambda i, j: (i, j + bh))],
                          out_specs=blk(lambda i, j: (i, j)), out_shape=_sds((s, D), BF16),
                          compiler_params=_params(2), name="merge_fwd")(y_a, y_h, proj, proj)


def _merge_bwd(dm, y_a, y_h, proj):
    s = y_a.shape[0]
    tn = MG_TN
    ba, bh = GT_A // tn, GT_H // tn

    def body(dm_ref, ya_ref, yh_ref, ga_ref, gh_ref, dya_ref, dyh_ref, dga_ref, dgh_ref):
        dmv = dm_ref[...]
        sa, sh = _sig(ga_ref[...]), _sig(gh_ref[...])
        dya_ref[...] = (dmv * sa).astype(BF16)
        dyh_ref[...] = (dmv * sh).astype(BF16)
        dga_ref[...] = (dmv * ya_ref[...] * (sa * (1.0 - sa))).astype(BF16)
        dgh_ref[...] = (dmv * yh_ref[...] * (sh * (1.0 - sh))).astype(BF16)

    tr = min(s, TALL)
    blk = lambda f: pl.BlockSpec((tr, tn), f)
    nat = blk(lambda i, j: (i, j))
    return pl.pallas_call(body, grid=(s // tr, D // tn),
                          in_specs=[nat, nat, nat, blk(lambda i, j: (i, j + ba)), blk(lambda i, j: (i, j + bh))],
                          out_specs=[nat] * 4, out_shape=[_sds((s, D), BF16)] * 4,
                          compiler_params=_params(2), name="merge_bwd")(dm, y_a, y_h, proj, proj)


def _hgout_fwd(o_raw, proj, hg_norm):
    s = o_raw.shape[0]
    bg = G_H // LANE

    def body(o_ref, g_ref, n_ref, out_ref):
        ov = o_ref[...]
        r = lax.rsqrt(jnp.mean(ov * ov, axis=-1, keepdims=True) + EPS)
        out_ref[...] = (ov * r * n_ref[...] * _sig(g_ref[...])).astype(out_ref.dtype)

    tr = min(s, TALL)
    blk = lambda f: pl.BlockSpec((tr, LANE), f)
    return pl.pallas_call(body, grid=(s // tr, HG_HEADS),
                          in_specs=[blk(lambda i, h: (i, h)), blk(lambda i, h: (i, h + bg)), pl.BlockSpec((1, LANE), lambda i, h: (0, 0))],
                          out_specs=blk(lambda i, h: (i, h)), out_shape=_sds((s, HG_W), BF16),
                          compiler_params=_params(2), name="hgout_fwd")(o_raw, proj, hg_norm)


def _hgout_bwd(d_out, o_raw, proj, hg_norm):
    s = o_raw.shape[0]
    bg = G_H // LANE

    def body(d_ref, o_ref, g_ref, n_ref, do_ref, dg_ref, dn_ref):
        first = jnp.logical_and(pl.program_id(0) == 0, pl.program_id(1) == 0)
        ov, dv, nv = o_ref[...], d_ref[...], n_ref[...]
        sg = _sig(g_ref[...])
        r = lax.rsqrt(jnp.mean(ov * ov, axis=-1, keepdims=True) + EPS)
        oh = ov * r
        d_on = dv * sg
        dg_ref[...] = (dv * (oh * nv) * (sg * (1.0 - sg))).astype(dg_ref.dtype)
        t = d_on * nv
        do_ref[...] = r * (t - oh * jnp.mean(t * oh, axis=-1, keepdims=True))
        _acc_rows(dn_ref, first, jnp.sum(d_on * oh, axis=0, keepdims=True))

    tr = min(s, TALL)
    blk = lambda f: pl.BlockSpec((tr, LANE), f)
    vec = pl.BlockSpec((1, LANE), lambda i, h: (0, 0))
    return pl.pallas_call(body, grid=(s // tr, HG_HEADS),
                          in_specs=[blk(lambda i, h: (i, h)), blk(lambda i, h: (i, h)), blk(lambda i, h: (i, h + bg)), vec],
                          out_specs=[blk(lambda i, h: (i, h)), blk(lambda i, h: (i, h)), vec],
                          out_shape=[_sds((s, HG_W), F32), _sds((s, HG_W), BF16), _sds((1, LANE), F32)],
                          compiler_params=_params(2), name="hgout_bwd")(d_out, o_raw, proj, hg_norm)


def _rope(t, cos, s_lo, s_hi):
    return t * cos + pltpu.roll(t, LANE - ROT // 2, 1) * s_lo + pltpu.roll(t, ROT // 2, 1) * s_hi


def _rope_wide(t, cos, s_lo, s_hi):
    return jnp.concatenate([_rope(t[:, k * LANE:(k + 1) * LANE], cos, s_lo, s_hi) for k in range(t.shape[1] // LANE)], axis=1)


def _attn_mask(has_prev):
    kj = lax.broadcasted_iota(jnp.int32, (2 * BLK, BLK), 0)
    qi = lax.broadcasted_iota(jnp.int32, (2 * BLK, BLK), 1)
    rel = BLK + qi - kj
    band = jnp.logical_and(rel >= 0, rel < BLK)
    return jnp.logical_and(band, jnp.logical_or(has_prev, kj >= BLK))


def _attn_specs():
    prev = lambda i: jnp.maximum(i - 1, 0)
    kb, vb = K_A // LANE, V_A // LANE
    blk = lambda f: pl.BlockSpec((BLK, LANE), f)
    tabs = [blk(lambda i: (i, 0))] * 3 + [blk(lambda i: (prev(i), 0))] * 3
    return [pl.BlockSpec((BLK, ATT_W), lambda i: (i, 0)), blk(lambda i: (i, kb)), blk(lambda i: (prev(i), kb)),
            blk(lambda i: (i, vb)), blk(lambda i: (prev(i), vb))] + tabs + [pl.BlockSpec((1, LANE), lambda i: (0, 0))]


def _attn_logits(qh, kg):
    return _dot(kg, qh, NT)


def _attn_probs(raw, mask, sk):
    logits = jnp.where(mask, raw * (HEAD_DIM ** -0.5), -jnp.inf)
    m = jnp.maximum(jnp.max(logits, axis=0, keepdims=True), sk)
    p = jnp.exp(logits - m)
    e_sink = jnp.exp(sk - m)
    inv = 1.0 / (jnp.sum(p, axis=0, keepdims=True) + e_sink)
    return p, inv, e_sink * inv


def _attn_fwd(proj, tabs, sinks):
    s = proj.shape[0]

    def body(q_ref, kc_ref, kp_ref, vc_ref, vp_ref, c0, l0, h0, c1, l1, h1, sk_ref, o_ref):
        i = pl.program_id(0)
        mask = _attn_mask(i > 0)
        q = _rope_wide(q_ref[...], c0[...], l0[...], h0[...]).astype(BF16)
        kk = jnp.concatenate([_rope(kp_ref[...], c1[...], l1[...], h1[...]), _rope(kc_ref[...], c0[...], l0[...], h0[...])], axis=0).astype(BF16)
        v_t = jnp.concatenate([vp_ref[...], vc_ref[...]], axis=0).T.astype(BF16)
        part = lambda t, h: t[:, h * HEAD_DIM:(h + 1) * HEAD_DIM]
        k_heads = [part(kk, g) for g in range(KV_HEADS)]

        def head(h):
            g = h // GROUP
            raw = _attn_logits(part(q, h), k_heads[g])
            yield
            p, inv, _ = _attn_probs(raw, mask, sk_ref[:, h:h + 1])
            yield
            out_t = _dot(v_t[g * HEAD_DIM:(g + 1) * HEAD_DIM], p.astype(BF16), NN)
            yield
            return out_t * inv

        o_ref[...] = jnp.concatenate(_interleave([head(h) for h in range(ATT_HEADS)]), axis=0).T.astype(o_ref.dtype)

    return pl.pallas_call(body, grid=(s // BLK,), in_specs=_attn_specs(),
                          out_specs=pl.BlockSpec((BLK, ATT_W), lambda i: (i, 0)), out_shape=_sds((s, ATT_W), BF16),
                          compiler_params=_params(1), name="attn_fwd")(proj, proj, proj, proj, proj, *tabs, *tabs, sinks)


def _attn_bwd(proj, tabs, sinks, d_att):
    s = proj.shape[0]

    def body(q_ref, kc_ref, kp_ref, vc_ref, vp_ref, c0, l0, h0, c1, l1, h1, sk_ref, do_ref, dq_ref, dk_ref, dv_ref, ds_ref):
        i = pl.program_id(0)

        @pl.when(i == 0)
        def _():
            dk_ref[...] = jnp.zeros_like(dk_ref)
            dv_ref[...] = jnp.zeros_like(dv_ref)
            ds_ref[...] = jnp.zeros_like(ds_ref)

        mask = _attn_mask(i > 0)
        q = _rope_wide(q_ref[...], c0[...], l0[...], h0[...]).astype(BF16)
        kk = jnp.concatenate([_rope(kp_ref[...], c1[...], l1[...], h1[...]), _rope(kc_ref[...], c0[...], l0[...], h0[...])], axis=0).astype(BF16)
        k_f32 = jnp.concatenate([_rope(kp_ref[...], c1[...], l1[...], h1[...]), _rope(kc_ref[...], c0[...], l0[...], h0[...])], axis=0)
        k_t = k_f32.T.astype(BF16)
        vv = jnp.concatenate([vp_ref[...], vc_ref[...]], axis=0).astype(BF16)
        d_o = do_ref[...].astype(BF16)
        lane = lax.broadcasted_iota(jnp.int32, (1, LANE), 1)
        part = lambda t, h: t[:, h * HEAD_DIM:(h + 1) * HEAD_DIM]
        k_heads = [part(kk, g) for g in range(KV_HEADS)]
        v_heads = [part(vv, g) for g in range(KV_HEADS)]

        def head(h):
            g = h // GROUP
            qh, doh = part(q, h), part(d_o, h)
            raw = _attn_logits(qh, k_heads[g])
            d_p = _dot(v_heads[g], doh, NT)
            yield
            p, inv, p_sink = _attn_probs(raw, mask, sk_ref[:, h:h + 1])
            prob = p * inv
            dv = _dot(prob.astype(BF16), doh, NN)
            yield
            dd = jnp.sum(prob * d_p, axis=0, keepdims=True)
            d_s = (prob * (d_p - dd)).astype(BF16)
            d_sink = jnp.where(lane == h, -jnp.sum(p_sink * dd, axis=1, keepdims=True), 0.0)
            dq_t = _dot(k_t[g * HEAD_DIM:(g + 1) * HEAD_DIM], d_s, NN)
            dk = _dot(d_s, qh, NN)
            yield
            return dq_t * (HEAD_DIM ** -0.5), dk * (HEAD_DIM ** -0.5), dv, d_sink

        per_head = _interleave([head(h) for h in range(ATT_HEADS)])
        dqs = [jnp.concatenate([t[0] for t in per_head], axis=0).T]
        group_sum = lambda k, g: functools.reduce(jnp.add, [t[k] for t in per_head[g * GROUP:(g + 1) * GROUP]])
        dks = [group_sum(1, g) for g in range(KV_HEADS)]
        dvs = [group_sum(2, g) for g in range(KV_HEADS)]
        d_sink = functools.reduce(jnp.add, [t[3] for t in per_head])
        dq_ref[...] = _rope_wide(jnp.concatenate(dqs, axis=1), c0[...], -l0[...], -h0[...]).astype(dq_ref.dtype)
        d_k = jnp.concatenate(dks, axis=1)
        d_v = jnp.concatenate(dvs, axis=1)
        cur = pl.ds(pl.multiple_of(i * BLK, BLK), BLK)
        prv = pl.ds(pl.multiple_of(jnp.maximum(i - 1, 0) * BLK, BLK), BLK)
        dk_ref[prv, :] += _rope(d_k[:BLK], c1[...], -l1[...], -h1[...])
        dk_ref[cur, :] += _rope(d_k[BLK:], c0[...], -l0[...], -h0[...])
        dv_ref[prv, :] += d_v[:BLK]
        dv_ref[cur, :] += d_v[BLK:]
        ds_ref[...] += d_sink

    full = pl.BlockSpec((s, LANE), lambda i: (0, 0))
    return pl.pallas_call(body, grid=(s // BLK,), in_specs=_attn_specs() + [pl.BlockSpec((BLK, ATT_W), lambda i: (i, 0))],
                          out_specs=[pl.BlockSpec((BLK, ATT_W), lambda i: (i, 0)), full, full, pl.BlockSpec((1, LANE), lambda i: (0, 0))],
                          out_shape=[_sds((s, ATT_W), BF16), _sds((s, LANE), F32), _sds((s, LANE), F32), _sds((1, LANE), F32)],
                          compiler_params=_params(1), name="attn_bwd")(proj, proj, proj, proj, proj, *tabs, *tabs, sinks, d_att)


def _tri_matmul(tri, t):
    hi = t.astype(BF16)
    r1 = t - hi.astype(F32)
    mid = r1.astype(BF16)
    lo = (r1 - mid.astype(F32)).astype(BF16)
    return _dot(tri, hi, NN) + _dot(tri, mid, NN) + _dot(tri, lo, NN)


def _lower_bound(hl):
    a, b = hl[0:1, :], hl[1:2, :]
    mx = jnp.maximum(a, b)
    ea, eb = jnp.exp(a - mx), jnp.exp(b - mx)
    return ea / (ea + eb)


def _hg_gates(q_raw, f_raw, lb, tri_lower):
    sg = _sig(f_raw)
    f = lb + (1.0 - lb) * sg
    sq = _sig(q_raw)
    b = _tri_matmul(tri_lower, jnp.log(f))
    return sg, f, 1.0 - f, sq, q_raw * sq, b


HG_PAIR_FWD = 8
HG_PAIR_BWD = 8


def _hg_specs(n_map, pair):
    blk = lambda off, p: pl.BlockSpec((HG_TB, LANE), lambda h, n: (n_map(n), off // LANE + pair * h + p))
    return [blk(off, p) for off in (Q_H, F_H, I_H) for p in range(pair)] + [pl.BlockSpec((2, pair * LANE), lambda h, n: (0, h))]


def _interleave(gens):
    out = [None] * len(gens)
    live = list(range(len(gens)))
    while live:
        for k in list(live):
            try:
                next(gens[k])
            except StopIteration as stop:
                out[k] = stop.value
                live.remove(k)
    return out


def _hg_spread():
    c = lax.broadcasted_iota(jnp.int32, (CHUNK, SUB * SUB), 0)
    l = lax.broadcasted_iota(jnp.int32, (CHUNK, SUB * SUB), 1)
    r = lax.broadcasted_iota(jnp.int32, (SUB, SUB * SUB), 0)
    lr = lax.broadcasted_iota(jnp.int32, (SUB, SUB * SUB), 1)
    shift = SUB.bit_length() - 1
    cols = [(c == lo + (l >> shift)).astype(BF16) for lo in range(0, CHUNK, SUB)]
    tile = [(c == lo + (l & (SUB - 1))).astype(BF16) for lo in range(0, CHUNK, SUB)]
    return cols, tile, (lr & (SUB - 1)) == r, (lr >> shift) == r


def _hg_intra(qs, kk, b, grad=None):
    lane = lax.broadcasted_iota(jnp.int32, (SUB, CHUNK), 1)
    row1 = lax.broadcasted_iota(jnp.int32, (SUB, 1), 0)
    kk_b = kk.astype(BF16)
    if grad is not None:
        d_a, d_at, (cols, tile, diag, block) = grad
    a_blocks, dq_blocks, dk_blocks, db_blocks = [], [], [], []
    dk_left = None
    for j in range(CHUNK // SUB):
        lo = j * SUB
        q_j, k_j, b_j = qs[lo:lo + SUB], kk[lo:lo + SUB], b[lo:lo + SUB]
        es = [jnp.where(row1 >= sx, jnp.exp(jnp.minimum(b_j - b_j[sx:sx + 1], 0.0)), 0.0) for sx in range(SUB)]
        pes = [q_j * e for e in es]
        pe = jnp.concatenate(pes, axis=0).astype(BF16)
        pairs = _dot(pe, kk_b, NT)
        yield
        a_j = jnp.zeros((SUB, CHUNK), F32)
        for sx in range(SUB):
            a_j = jnp.where(lane == lo + sx, pairs[sx * SUB:(sx + 1) * SUB], a_j)
        if grad is not None:
            da_j = d_a[lo:lo + SUB]
            ek = jnp.concatenate([e * k_j[sx:sx + 1] for sx, e in enumerate(es)], axis=0).astype(BF16)
            sel_t = jnp.where(diag, _dot(da_j.astype(BF16), cols[j], NN), 0.0).astype(BF16)
            sel_s = jnp.where(block, _dot(d_at[lo:lo + SUB].astype(BF16), tile[j], NN), 0.0).astype(BF16)
            pek = jnp.concatenate([p * k_j[sx:sx + 1] for sx, p in enumerate(pes)], axis=0).astype(BF16)
            yield
            dq_j = _dot(sel_t, ek, NN)
            dk_j = _dot(sel_s, pe, NN)
            db_j = _dot(sel_t, pek, NN) - _dot(sel_s, pek, NN)
            yield
        if j > 0:
            ref = b[lo - 1:lo]
            sc_q = jnp.exp(b_j - ref)
            sc_k = jnp.exp(jnp.minimum(ref - b, 0.0))
            qt = (q_j * sc_q).astype(BF16)
            kt = (kk * sc_k).astype(BF16)
            left = _dot(qt, kt, NT)
            yield
            a_j = a_j + jnp.where(lane < lo, left, 0.0)
            if grad is not None:
                da_left = jnp.where(lane < lo, da_j, 0.0).astype(BF16)
                dq_left = _dot(da_left, kt, NN) * sc_q
                dq_j = dq_j + dq_left
                db_j = db_j + q_j * dq_left
                t = _dot(da_left, qt, TN)
                yield
                t = t * sc_k
                dk_left = t if dk_left is None else dk_left + t
        a_blocks.append(a_j)
        if grad is not None:
            dq_blocks.append(dq_j)
            dk_blocks.append(dk_j)
            db_blocks.append(db_j)
    a = jnp.concatenate(a_blocks, axis=0)
    if grad is None:
        return a
    return a, jnp.concatenate(dq_blocks, axis=0), jnp.concatenate(dk_blocks, axis=0) + dk_left, jnp.concatenate(db_blocks, axis=0) - kk * dk_left


def _hgrn_fwd(proj, hl):
    s = proj.shape[0]
    n_chunk = HG_TB // CHUNK
    pair = HG_PAIR_FWD

    def body(*refs):
        q_refs, f_refs, i_refs = refs[:pair], refs[pair:2 * pair], refs[2 * pair:3 * pair]
        hl_ref, o_ref, st_out_ref, st_ref = refs[3 * pair:]

        @pl.when(pl.program_id(1) == 0)
        def _():
            st_ref[...] = jnp.zeros_like(st_ref)

        r_i = lax.broadcasted_iota(jnp.int32, (CHUNK, CHUNK), 0)
        c_i = lax.broadcasted_iota(jnp.int32, (CHUNK, CHUNK), 1)
        tri_lower = (r_i >= c_i).astype(BF16)

        def chunk(c, carry):
            rows = pl.ds(pl.multiple_of(c * CHUNK, CHUNK), CHUNK)
            def head(p):
                cols = slice(p * LANE, (p + 1) * LANE)
                lb = _lower_bound(hl_ref[:, cols])
                v = i_refs[p][rows, :].astype(BF16)
                _, _, kk, _, qs, b = _hg_gates(q_refs[p][rows, :], f_refs[p][rows, :], lb, tri_lower)
                yield
                st = st_ref[p]
                st_b = st.astype(BF16)
                st_out_ref[p, c] = st_b
                o_state = _dot((qs * jnp.exp(b)).astype(BF16), st_b, NT)
                b_last = b[CHUNK - 1:CHUNK, :]
                st_new = _dot(v, (kk * jnp.exp(b_last - b)).astype(BF16), TN)
                a = yield from _hg_intra(qs, kk, b)
                st_ref[p] = st * jnp.exp(b_last) + st_new
                o_ref[rows, cols] = o_state + _dot(a.astype(BF16), v, NN)

            _interleave([head(p) for p in range(pair)])
            return carry

        lax.fori_loop(0, n_chunk, chunk, 0)

    return pl.pallas_call(
        body, grid=(HG_HEADS // pair, s // HG_TB), in_specs=_hg_specs(lambda n: n, pair),
        out_specs=[pl.BlockSpec((HG_TB, pair * LANE), lambda h, n: (n, h)), pl.BlockSpec((pair, n_chunk, HG_K, HG_K), lambda h, n: (h, n, 0, 0))],
        out_shape=[_sds((s, HG_W), F32), _sds((HG_HEADS, s // CHUNK, HG_K, HG_K), BF16)],
        scratch_shapes=[pltpu.VMEM((pair, HG_K, HG_K), F32)],
        compiler_params=_params(2), name="hgrn_fwd")(*[proj] * (3 * pair), hl)


def _hgrn_bwd(proj, hl, states, d_o):
    s = proj.shape[0]
    n_chunk = HG_TB // CHUNK
    n_blk = s // HG_TB
    pair = HG_PAIR_BWD
    rev = lambda n: n_blk - 1 - n

    def body(*refs):
        q_refs, f_refs, i_refs = refs[:pair], refs[pair:2 * pair], refs[2 * pair:3 * pair]
        hl_ref, st_in_ref, do_ref, dq_ref, df_ref, di_ref, dhl_ref, dst_ref, dlb_ref = refs[3 * pair:]
        n = pl.program_id(1)

        @pl.when(n == 0)
        def _():
            dst_ref[...] = jnp.zeros_like(dst_ref)
            dlb_ref[...] = jnp.zeros_like(dlb_ref)

        r_i = lax.broadcasted_iota(jnp.int32, (CHUNK, CHUNK), 0)
        c_i = lax.broadcasted_iota(jnp.int32, (CHUNK, CHUNK), 1)
        tri_lower = (r_i >= c_i).astype(BF16)
        tri_upper = (r_i <= c_i).astype(BF16)
        row = lax.broadcasted_iota(jnp.int32, (CHUNK, 1), 0)
        spread = _hg_spread()

        def chunk(cc, carry):
            c = n_chunk - 1 - cc
            rows = pl.ds(pl.multiple_of(c * CHUNK, CHUNK), CHUNK)
            def head(p):
                cols = slice(p * LANE, (p + 1) * LANE)
                lb = _lower_bound(hl_ref[:, cols])
                q_raw = q_refs[p][rows, :]
                vb = i_refs[p][rows, :].astype(BF16)
                sg, f, kk, sq, qs, b = _hg_gates(q_raw, f_refs[p][rows, :], lb, tri_lower)
                yield
                e_b = jnp.exp(b)
                qe = qs * e_b
                b_last = b[CHUNK - 1:CHUNK, :]
                e_last = jnp.exp(b_last)
                e_kd = jnp.exp(b_last - b)
                kd = kk * e_kd
                st0 = st_in_ref[p, c]
                d_ob = do_ref[rows, cols].astype(BF16)
                dst = dst_ref[p]
                dst_b = dst.astype(BF16)
                d_a = jnp.where(r_i >= c_i, _dot(d_ob, vb, NT), 0.0)
                d_at = jnp.where(r_i <= c_i, _dot(vb, d_ob, NT), 0.0)
                d_v_st = _dot(kd.astype(BF16), dst_b, NT)
                d_kd = _dot(vb, dst_b, NN)
                d_qe = _dot(d_ob, st0, NN)
                dst_new = _dot(d_ob, qe.astype(BF16), TN)
                yield
                a, dqs, dkk, d_b = yield from _hg_intra(qs, kk, b, (d_a, d_at, spread))
                d_v = _dot(a.astype(BF16), d_ob, TN) + d_v_st
                dqs_st = d_qe * e_b
                dkk_st = d_kd * e_kd
                dqs = dqs + dqs_st
                dkk = dkk + dkk_st
                d_b_last = jnp.sum(d_kd * kd, axis=0, keepdims=True) + jnp.sum(dst * st0.astype(F32), axis=0, keepdims=True) * e_last
                d_b = d_b + qs * dqs_st - kk * dkk_st + jnp.where(row == CHUNK - 1, d_b_last, 0.0)
                d_g = _tri_matmul(tri_upper, d_b)
                dst_ref[p] = dst_new + dst * e_last
                yield
                d_f = d_g / f - dkk
                dlb_ref[:, cols] += jnp.sum(d_f * (1.0 - sg), axis=0, keepdims=True)
                dq_ref[rows, cols] = (dqs * (sq * (1.0 + q_raw * (1.0 - sq)))).astype(dq_ref.dtype)
                df_ref[rows, cols] = (d_f * (1.0 - lb) * (sg * (1.0 - sg))).astype(df_ref.dtype)
                di_ref[rows, cols] = d_v.astype(di_ref.dtype)

            _interleave([head(p) for p in range(pair)])
            return carry

        lax.fori_loop(0, n_chunk, chunk, 0)

        @pl.when(n == n_blk - 1)
        def _():
            lb = _lower_bound(hl_ref[...])
            d_hl0 = dlb_ref[...] * (lb * (1.0 - lb))
            dhl_ref[...] = jnp.concatenate([d_hl0, -d_hl0], axis=0)

    out_blk = pl.BlockSpec((HG_TB, pair * LANE), lambda h, n: (rev(n), h))
    return pl.pallas_call(
        body, grid=(HG_HEADS // pair, n_blk),
        in_specs=_hg_specs(rev, pair) + [pl.BlockSpec((pair, n_chunk, HG_K, HG_K), lambda h, n: (h, rev(n), 0, 0)), out_blk],
        out_specs=[out_blk, out_blk, out_blk, pl.BlockSpec((2, pair * LANE), lambda h, n: (0, h))],
        out_shape=[_sds((s, HG_W), BF16)] * 3 + [_sds((2, HG_W), F32)],
        scratch_shapes=[pltpu.VMEM((pair, HG_K, HG_K), F32), pltpu.VMEM((1, pair * LANE), F32)],
        compiler_params=_params(2), name="hgrn_bwd")(*[proj] * (3 * pair), hl, states, d_o)


def _mod_part(c_all, w_shard, b_shard):
    n = w_shard.shape[1]
    tn = 512

    def body(c_ref, w_ref, b_ref, o_ref):
        o_ref[...] = _dot(c_ref[...].astype(BF16), w_ref[...].astype(BF16), NN) + b_ref[...]

    return pl.pallas_call(body, grid=(n // tn,),
                          in_specs=[pl.BlockSpec((N_DEV, D), lambda j: (0, 0)), pl.BlockSpec((D, tn), lambda j: (0, j)), pl.BlockSpec((1, tn), lambda j: (0, j))],
                          out_specs=pl.BlockSpec((N_DEV, tn), lambda j: (0, j)), out_shape=_sds((N_DEV, n), F32),
                          compiler_params=_params(1, 32 << 20), name="mod_part")(c_all, w_shard, b_shard)


def _adam_math(g, w, m, v):
    c1 = 1.0 / (1.0 - ADAM_B1 ** ADAM_STEP)
    c2 = 1.0 / (1.0 - ADAM_B2 ** ADAM_STEP)
    m2 = ADAM_B1 * m + (1.0 - ADAM_B1) * g
    v2 = ADAM_B2 * v + (1.0 - ADAM_B2) * (g * g)
    return -ADAM_LR * ((m2 * c1) / (jnp.sqrt(v2 * c2) + ADAM_EPS) + ADAM_WD * w), m2, v2


def _update_w_ada(c_all_t, dmod_cols, w, m, v):
    n = dmod_cols.shape[1]
    tn = 256

    def body(c_ref, d_ref, w_ref, m_ref, v_ref, g_ref, dl_ref, m2_ref, v2_ref):
        cv = c_ref[...].astype(BF16).astype(F32)
        dv = d_ref[...].astype(BF16).astype(F32)
        g = cv[:, 0:1] * dv[0:1, :]
        for k in range(1, N_DEV):
            g = g + cv[:, k:k + 1] * dv[k:k + 1, :]
        g_ref[...] = g
        dl_ref[...], m2_ref[...], v2_ref[...] = _adam_math(g, w_ref[...], m_ref[...], v_ref[...])

    blk = pl.BlockSpec((D, tn), lambda j: (0, j))
    return pl.pallas_call(body, grid=(n // tn,),
                          in_specs=[pl.BlockSpec((D, N_DEV), lambda j: (0, 0)), pl.BlockSpec((N_DEV, tn), lambda j: (0, j)), blk, blk, blk],
                          out_specs=[blk] * 4, out_shape=[_sds((D, n), F32)] * 4,
                          compiler_params=_params(1, 48 << 20), name="adamw_w_ada")(c_all_t, dmod_cols, w, m, v)


def _row_tile(r, c, max_elems=1 << 18):
    if r * c <= max_elems or r % 8:
        return r
    best = 8
    for t in range(8, r + 1, 8):
        if r % t == 0 and t * c <= max_elems:
            best = t
    return best


WIDE_TILE = 5 << 17
PAIR_TILE = 3 << 19


def _adamw(pieces, w, m, v, name, own=None, max_elems=1 << 18):
    p, r, c = pieces.shape
    tr = _row_tile(r, c, max_elems)

    def body(*refs):
        if own is None:
            p_ref, w_ref, m_ref, v_ref, *outs = refs
            g = p_ref[0].astype(F32)
        else:
            o_ref, p_ref, w_ref, m_ref, v_ref, *outs = refs
            g = o_ref[...].astype(F32) + p_ref[0].astype(F32)
        for k in range(1, p):
            g = g + p_ref[k].astype(F32)
        outs[0][...] = g
        outs[1][...], outs[2][...], outs[3][...] = _adam_math(g, w_ref[...], m_ref[...], v_ref[...])

    blk = pl.BlockSpec((tr, c), lambda i: (i, 0))
    lead = [] if own is None else [own]
    return pl.pallas_call(body, grid=(r // tr,), in_specs=[blk] * len(lead) + [pl.BlockSpec((p, tr, c), lambda i: (0, i, 0)), blk, blk, blk],
                          out_specs=[blk] * 4, out_shape=[_sds((r, c), F32)] * 4,
                          compiler_params=_params(1, 48 << 20), name=name)(*lead, pieces, w, m, v)


def _my_coords():
    return lax.axis_index("x"), lax.axis_index("y"), lax.axis_index("c")


def _flip(coords, k):
    x, y, c = coords
    return (1 - x if k & 4 else x, 1 - y if k & 2 else y, 1 - c if k & 1 else c)


def _lin(coords):
    return 4 * coords[0] + 2 * coords[1] + coords[2]


def _exchange_small(x3, bcast, name):
    n = x3.shape[2]

    def body(x_ref, o_ref, send_sems, recv_sems):
        me = _my_coords()
        my_id = _lin(me)
        o_ref[pl.ds(my_id, 1)] = x_ref[pl.ds(0 if bcast else my_id, 1)]
        copies = []
        for k in range(1, N_DEV):
            peer = _flip(me, k)
            src = x_ref.at[0 if bcast else _lin(peer)]
            cp = pltpu.make_async_remote_copy(src_ref=src, dst_ref=o_ref.at[my_id], send_sem=send_sems.at[k], recv_sem=recv_sems.at[k],
                                              device_id=peer, device_id_type=MESH)
            cp.start()
            copies.append(cp)
        for k in range(1, N_DEV):
            peer = _flip(me, k)
            pltpu.make_async_remote_copy(src_ref=x_ref.at[0], dst_ref=o_ref.at[_lin(peer)], send_sem=send_sems.at[k], recv_sem=recv_sems.at[k],
                                         device_id=peer, device_id_type=MESH).wait_recv()
        for cp in copies:
            cp.wait_send()

    vm = pl.BlockSpec(memory_space=pltpu.VMEM)
    return pl.pallas_call(body, in_specs=[vm], out_specs=vm, out_shape=_sds((N_DEV, 1, n), F32),
                          scratch_shapes=[pltpu.SemaphoreType.DMA((N_DEV,)), pltpu.SemaphoreType.DMA((N_DEV,))], name=name)(x3)


HBM_SPEC = pl.BlockSpec(memory_space=pltpu.HBM)
SEM_SPEC = pl.BlockSpec(memory_space=pltpu.SEMAPHORE)
ANY_SPEC = pl.BlockSpec(memory_space=pl.ANY)
DATAFLOW = pltpu.SideEffectType.DATAFLOW_SIDE_EFFECTING
GATHER_FLIPS = (1, 2, 4, 6)
PASS_FLIPS = (2, 4, 6)
TOKEN = (8, LANE)


def _hbm(t):
    return pltpu.with_memory_space_constraint(t, pltpu.HBM)


def _hbm_like(ts):
    return [pltpu.HBM(t.shape, t.dtype) for t in ts]


def _split_start(issue, srcs, lands, n_sem, name, deps=()):
    n, nb, nd = len(srcs), len(srcs) + len(lands), len(deps)

    def body(*refs):
        issue(refs[:n], refs[n:nb], refs[nb + nd], refs[nb + nd + 1])
        refs[-1][...] = jnp.zeros(TOKEN, F32)

    outs = pl.pallas_call(
        body, name=name,
        out_shape=(pltpu.SemaphoreType.DMA((n_sem,)), pltpu.SemaphoreType.DMA((n_sem,)), *_hbm_like(srcs), *_hbm_like(lands), _sds(TOKEN, F32)),
        in_specs=[HBM_SPEC] * nb + [ANY_SPEC] * nd,
        out_specs=(SEM_SPEC, SEM_SPEC, *[HBM_SPEC] * nb, pl.BlockSpec(memory_space=pltpu.VMEM)),
        input_output_aliases={i: 2 + i for i in range(nb)},
        compiler_params=pltpu.CompilerParams(has_side_effects=DATAFLOW))(*[_hbm(t) for t in srcs], *[_hbm(t) for t in lands], *deps)
    return dict(sems=outs[:2], thru=list(outs[2:2 + nb]), token=outs[-1], n=n)


def _split_wait(finish, handle, after, name):
    n = handle["n"]
    thru = handle["thru"]
    nb = len(thru)

    def body(*refs):
        finish(refs[:n], refs[n:nb], refs[nb], refs[nb + 1])

    outs = pl.pallas_call(
        body, name=name, out_shape=_hbm_like(thru), in_specs=[HBM_SPEC] * nb + [SEM_SPEC, SEM_SPEC] + [ANY_SPEC] * len(after),
        out_specs=[HBM_SPEC] * nb, input_output_aliases={i: i for i in range(nb)},
        compiler_params=pltpu.CompilerParams(has_side_effects=DATAFLOW))(*thru, *handle["sems"], *after)
    return list(outs[:n]), list(outs[n:])


def _gather_start(shards, name, deps=()):
    n = len(shards)
    my_id = _lin(_my_coords())
    lands = [lax.dynamic_update_slice(lax.empty((N_DEV,) + t.shape, t.dtype), t[None], (my_id, 0, 0)) for t in shards]

    def issue(src, land, send_sems, recv_sems):
        me = _my_coords()
        for w in range(n):
            for j, k in enumerate(GATHER_FLIPS):
                q = len(GATHER_FLIPS) * w + j
                pltpu.make_async_remote_copy(src_ref=src[w], dst_ref=land[w].at[_lin(me)], send_sem=send_sems.at[q], recv_sem=recv_sems.at[q],
                                             device_id=_flip(me, k), device_id_type=MESH).start()

    return _split_start(issue, shards, lands, len(GATHER_FLIPS) * n, name, deps)


def _gather_wait(handle, after, name):
    n = handle["n"]

    def finish(src, land, send_sems, recv_sems):
        me = _my_coords()
        for w in range(n):
            for j, k in enumerate(GATHER_FLIPS):
                q = len(GATHER_FLIPS) * w + j
                peer = _flip(me, k)
                cp = pltpu.make_async_remote_copy(src_ref=src[w], dst_ref=land[w].at[_lin(peer)], send_sem=send_sems.at[q], recv_sem=recv_sems.at[q],
                                                  device_id=peer, device_id_type=MESH)
                cp.wait_send()
                cp.wait_recv()

    return _split_wait(finish, handle, after, name)[1]


def _pass_copy(land, send_sems, recv_sems, w, j, arriving):
    me = _my_coords()
    blk = land[w].at[_lin(_flip(me, PASS_FLIPS[j] + (1 if arriving else 0)))]
    q = len(PASS_FLIPS) * w + j
    return pltpu.make_async_remote_copy(src_ref=blk, dst_ref=blk, send_sem=send_sems.at[q], recv_sem=recv_sems.at[q],
                                        device_id=_flip(me, 1), device_id_type=MESH)


def _pass_start(lands, name, deps=()):
    def issue(_, land, send_sems, recv_sems):
        for w in range(len(lands)):
            for j in range(len(PASS_FLIPS)):
                _pass_copy(land, send_sems, recv_sems, w, j, False).start()

    return _split_start(issue, [], lands, len(PASS_FLIPS) * len(lands), name, deps)


def _pass_wait(handle, after, name):
    def finish(_, land, send_sems, recv_sems):
        for w in range(len(handle["thru"])):
            for j in range(len(PASS_FLIPS)):
                _pass_copy(land, send_sems, recv_sems, w, j, False).wait_send()
                _pass_copy(land, send_sems, recv_sems, w, j, True).wait_recv()

    return _split_wait(finish, handle, after, name)[1]


def _gather_pass(lands, name):
    n = len(lands)
    n_p = len(PASS_FLIPS)

    def body(*refs):
        land = refs[n:2 * n]
        send_sems, recv_sems = refs[2 * n:]
        me = _my_coords()
        sibling = _flip(me, 1)
        sent = []
        for w in range(n):
            for j, k in enumerate(PASS_FLIPS):
                blk = land[w].at[_lin(_flip(me, k))]
                cp = pltpu.make_async_remote_copy(src_ref=blk, dst_ref=blk, send_sem=send_sems.at[n_p * w + j], recv_sem=recv_sems.at[n_p * w + j],
                                                  device_id=sibling, device_id_type=MESH)
                cp.start()
                sent.append(cp)
        for w in range(n):
            for j, k in enumerate(PASS_FLIPS):
                blk = land[w].at[_lin(_flip(me, k + 1))]
                pltpu.make_async_remote_copy(src_ref=blk, dst_ref=blk, send_sem=send_sems.at[n_p * w + j], recv_sem=recv_sems.at[n_p * w + j],
                                             device_id=sibling, device_id_type=MESH).wait_recv()
        for cp in sent:
            cp.wait_send()

    return pl.pallas_call(body, in_specs=[ANY_SPEC] * n, out_specs=[ANY_SPEC] * n, out_shape=[_sds(t.shape, t.dtype) for t in lands],
                          input_output_aliases={i: i for i in range(n)},
                          scratch_shapes=[pltpu.SemaphoreType.DMA((n_p * n,)), pltpu.SemaphoreType.DMA((n_p * n,))], name=name)(*lands)


CHIP_FLIPS = (0, 2, 4, 6)


def _pair_copy(src, land, send_sems, recv_sems, w, j):
    me = _my_coords()
    q = len(CHIP_FLIPS) * w + j
    return pltpu.make_async_remote_copy(src_ref=src[w].at[_lin(_flip(me, CHIP_FLIPS[j] + 1))], dst_ref=land[w].at[j], send_sem=send_sems.at[q],
                                        recv_sem=recv_sems.at[q], device_id=_flip(me, 1), device_id_type=MESH)


def _pair_exchange(grads, name):
    n = len(grads)

    def body(*refs):
        src, land = refs[:n], refs[n:2 * n]
        send_sems, recv_sems = refs[2 * n:]
        sent = [_pair_copy(src, land, send_sems, recv_sems, w, j) for w in range(n) for j in range(len(CHIP_FLIPS))]
        for cp in sent:
            cp.start()
        for cp in sent:
            cp.wait_recv()
        for cp in sent:
            cp.wait_send()

    outs = pl.pallas_call(body, in_specs=[ANY_SPEC] * n, out_specs=[ANY_SPEC] * n,
                          out_shape=[_sds((len(CHIP_FLIPS),) + g.shape[1:], g.dtype) for g in grads],
                          scratch_shapes=[pltpu.SemaphoreType.DMA((len(CHIP_FLIPS) * n,))] * 2, name=name)(*grads)
    return list(outs)


def _pair_start(grads, name, deps=()):
    n = len(grads)
    lands = [lax.empty((len(CHIP_FLIPS),) + g.shape[1:], g.dtype) for g in grads]

    def issue(src, land, send_sems, recv_sems):
        for w in range(n):
            for j in range(len(CHIP_FLIPS)):
                _pair_copy(src, land, send_sems, recv_sems, w, j).start()

    return _split_start(issue, grads, lands, len(CHIP_FLIPS) * n, name, deps)


def _pair_wait(handle, after, name):
    n = handle["n"]

    def finish(src, land, send_sems, recv_sems):
        for w in range(n):
            for j in range(len(CHIP_FLIPS)):
                cp = _pair_copy(src, land, send_sems, recv_sems, w, j)
                cp.wait_send()
                cp.wait_recv()

    return _split_wait(finish, handle, after, name)


def _pair_add(grad, theirs, name):
    p, r, c = theirs.shape
    tr = _row_tile(r, c, PAIR_TILE)
    me = _my_coords()
    ids = jnp.stack([_lin(_flip(me, k)) for k in CHIP_FLIPS]).astype(jnp.int32)

    def body(ids_ref, a_ref, b_ref, o_ref):
        o_ref[...] = (a_ref[...].astype(F32) + b_ref[...].astype(F32)).astype(o_ref.dtype)

    blk = pl.BlockSpec((None, tr, c), lambda j, i, ids_ref: (j, i, 0))
    return pl.pallas_call(
        body, out_shape=_sds((p, r, c), theirs.dtype), compiler_params=_params(2), name=name,
        grid_spec=pltpu.PrefetchScalarGridSpec(
            num_scalar_prefetch=1, grid=(p, r // tr),
            in_specs=[pl.BlockSpec((None, tr, c), lambda j, i, ids_ref: (ids_ref[j], i, 0)), blk], out_specs=blk))(ids, grad, theirs)


def _chips_start(parts, name, deps=()):
    n = len(parts)
    n_c = len(CHIP_FLIPS) - 1
    lands = [lax.empty((n_c,) + t.shape[1:], t.dtype) for t in parts]

    def issue(src, land, send_sems, recv_sems):
        me = _my_coords()
        for w in range(n):
            for j in range(1, n_c + 1):
                q = n_c * w + j - 1
                pltpu.make_async_remote_copy(src_ref=src[w].at[j], dst_ref=land[w].at[j - 1], send_sem=send_sems.at[q], recv_sem=recv_sems.at[q],
                                             device_id=_flip(me, CHIP_FLIPS[j]), device_id_type=MESH).start()

    return _split_start(issue, parts, lands, n_c * n, name, deps)


def _chips_wait(handle, after, name):
    n = handle["n"]
    n_c = len(CHIP_FLIPS) - 1

    def finish(src, land, send_sems, recv_sems):
        me = _my_coords()
        for w in range(n):
            for j in range(1, n_c + 1):
                q = n_c * w + j - 1
                cp = pltpu.make_async_remote_copy(src_ref=src[w].at[j], dst_ref=land[w].at[j - 1], send_sem=send_sems.at[q], recv_sem=recv_sems.at[q],
                                                  device_id=_flip(me, CHIP_FLIPS[j]), device_id_type=MESH)
                cp.wait_send()
                cp.wait_recv()

    return _split_wait(finish, handle, after, name)


def _after(t, *tokens):
    for tok in tokens:
        t = t + tok[0:1, 0:1]
    return t


def _rope_tables(positions):
    half = ROT // 2
    inv_freq = ROPE_THETA ** (-jnp.arange(0, ROT, 2, dtype=F32) / ROT)
    ang = positions.astype(F32).reshape(-1, 1) * inv_freq
    cos, sin = jnp.cos(ang), jnp.sin(ang)
    s = ang.shape[0]
    pad = jnp.zeros((s, HEAD_DIM - ROT), F32)
    zero = jnp.zeros((s, half), F32)
    two = lambda t: jnp.concatenate([t, t], axis=1)
    return (two(jnp.concatenate([cos, cos, pad + 1.0], axis=1)), two(jnp.concatenate([-sin, zero, pad], axis=1)),
            two(jnp.concatenate([zero, sin, pad], axis=1)))


def _local_step(x, tgt, tabs, mod, sinks_pad, hl, hg_norm, g_pre_mix, g_post_mix, g_pre_ffn, g_post_ffn, weights, prefetch, scatter, scatter_on):
    s = x.shape[0]
    h1 = _pre_fwd(x, g_pre_mix, mod, 1, 0, "pre_mix_fwd")
    (w_in_a,) = weights("in_a", h1)
    proj = _mm_nt(h1[:, :D // 2], w_in_a, 256, IN_COLS // 2, D // 2, F32, "proj_mm_a")
    (w_in_b,) = weights("in_b", proj)
    proj = _mm_nt(h1[:, D // 2:], w_in_b, 256, IN_COLS // 2, D // 2, F32, "proj_mm_b", add=proj)
    att = _attn_fwd(proj, tabs, _after(sinks_pad, prefetch("mix", proj)))
    o_raw, states = _hgrn_fwd(proj, hl)
    ohg = _hgout_fwd(o_raw, proj, hg_norm)
    w_attn_dm, w_hgrn_dm, w_out = weights("mix", ohg)
    natural = lambda w_dm: w_dm.transpose(1, 0, 2).reshape(w_dm.shape[1], D)
    pieces = lambda g: g.reshape(g.shape[0], N_DEV, D // N_DEV).transpose(1, 0, 2)
    w_attn, w_hgrn = natural(w_attn_dm), natural(w_hgrn_dm)
    y_a = _mm_nn(att, w_attn, s, 512, ATT_W, F32, "attn_proj_mm")
    y_h = _mm_nn(ohg, w_hgrn, s, 512, HG_W, F32, "hgrn_proj_mm")
    merged = _merge_fwd(y_a, y_h, proj)
    y = _mm_nn(merged, w_out, s, 512, D, F32, "out_mm")
    x1 = _post_fwd(x, y, g_post_mix, mod, 2, "post_mix_fwd")
    h2 = _pre_fwd(x1, g_pre_ffn, _after(mod, prefetch("ffn_in", x1)), 4, 3, "pre_ffn_fwd")
    (w_ffn_in_dm,) = weights("ffn_in", h2)
    gu = _mm_nn_dm(h2, w_ffn_in_dm, s // 2, F32, "ffn_in_mm")
    act = _swiglu_fwd(gu, deps=[prefetch("ffn_out", gu)])
    (w_ffn_out,) = weights("ffn_out", act)
    y2 = _mm_nn(act, w_ffn_out, 512, 512, FFN, F32, "ffn_out_mm")
    err, loss, dy2, d_gate2, dg_post_ffn = _post_loss_bwd(x1, y2, g_post_ffn, mod, 5, tgt, "post_ffn_loss_bwd")
    gw_ffn_out = _mm_tn(act, dy2, 512, D, BF16, "ffn_out_dw")
    t_pair = scatter([gw_ffn_out.reshape(N_DEV, FFN // N_DEV, D)], "ffn_out")
    d_act = _mm_nt(dy2, w_ffn_out, s, 512, D, F32, "ffn_out_dx", deps=[t_pair])
    dgu = _swiglu_bwd(d_act, gu)
    t_out = scatter_on("ffn_out", dgu)
    gw_ffn_in = _mm_tn_dm(h2, dgu, 1024, BF16, "ffn_in_dw")
    t_pair = scatter([gw_ffn_in], "ffn_in")
    dh2 = _mm_nt_dm(dgu, w_ffn_in_dm, s, 1024, F32, "ffn_in_dx", deps=[t_pair])
    mod = _after(mod, t_out)
    dx1, d_shift2, d_scale2, dg_pre_ffn = _pre_bwd(dh2, x1, err, g_pre_ffn, mod, 4, "pre_ffn_bwd")
    dy, d_gate1, dg_post_mix = _post_bwd(dx1, y, g_post_mix, mod, 2, "post_mix_bwd")
    t_in = scatter_on("ffn_in", dy)
    d_merged = _mm_nt(dy, w_out, s, 512, D, F32, "out_dx")
    gw_out = _mm_tn(merged, dy, 512, D, BF16, "out_dw")
    dy_a, dy_h, d_gate_a, d_gate_h = _merge_bwd(d_merged, y_a, y_h, proj)
    gw_attn = pieces(_mm_tn(att, dy_a, 512, D, BF16, "attn_proj_dw"))
    gw_hgrn = pieces(_mm_tn(ohg, dy_h, 512, D, BF16, "hgrn_proj_dw"))
    t_pair = scatter([gw_attn, gw_hgrn, gw_out.reshape(N_DEV, D // N_DEV, D)], "mix")
    d_att = _mm_nt(dy_a, w_attn, s, 512, D, F32, "attn_proj_dx")
    d_ohg = _mm_nt(dy_h, w_hgrn, s, 512, D, F32, "hgrn_proj_dx", deps=[t_pair])
    d_o, d_gh, d_hg_norm = _hgout_bwd(d_ohg, o_raw, proj, _after(hg_norm, t_in))
    d_qh, d_fh, d_ih, d_hl = _hgrn_bwd(proj, hl, states, d_o)
    t_mix = scatter_on("mix", d_qh)
    d_qa, d_ka, d_va, d_sinks = _attn_bwd(proj, tabs, _after(sinks_pad, t_mix), d_att)
    d_proj = jnp.concatenate([d_qa, d_ka.astype(BF16), d_va.astype(BF16), d_qh, d_fh, d_ih, d_gh, d_gate_a, d_gate_h], axis=1)
    dh1 = jnp.concatenate([_mm_nn(d_proj, w_half, s // 2, 512, IN_COLS // 2, F32, "proj_dx_" + tag)
                           for tag, w_half in (("a", w_in_a), ("b", w_in_b))], axis=1)
    grad_x, d_shift1, d_scale1, dg_pre_mix = _pre_bwd(dh1, x, dx1, g_pre_mix, mod, 1, "pre_mix_bwd")
    d_mod = jnp.concatenate([d_shift1, d_scale1, d_gate1, d_shift2, d_scale2, d_gate2], axis=1)
    small = [d_mod, dg_pre_mix, dg_post_mix, dg_pre_ffn, dg_post_ffn, d_hl.reshape(1, 2 * HG_W), d_hg_norm, d_sinks]
    return loss, grad_x, small, h1, d_proj


def kernel(x, c, positions, w_ada, b_ada, g_pre_mix, g_post_mix, g_pre_ffn, g_post_ffn, w_in, attn_sinks, w_attn_proj, hg_lower_bounds, hg_norm, w_hgrn_proj, w_out, w_ffn_in, w_ffn_out, loss_target, m_w_ada, m_b_ada, m_g_pre_mix, m_g_post_mix, m_g_pre_ffn, m_g_post_ffn, m_w_in, m_attn_sinks, m_w_attn_proj, m_hg_lower_bounds, m_hg_norm, m_w_hgrn_proj, m_w_out, m_w_ffn_in, m_w_ffn_out, v_w_ada, v_b_ada, v_g_pre_mix, v_g_post_mix, v_g_pre_ffn, v_g_post_ffn, v_w_in, v_attn_sinks, v_w_attn_proj, v_hg_lower_bounds, v_hg_norm, v_w_hgrn_proj, v_w_out, v_w_ffn_in, v_w_ffn_out):
    my_id = _lin(_my_coords())
    s = x.shape[1]
    n_ada = w_ada.shape[2]

    c_all = _exchange_small(c.reshape(1, 1, D), True, "gather_c").reshape(N_DEV, D)
    b_cols = lax.dynamic_slice(b_ada, (0, my_id * n_ada), (1, n_ada))
    mod_part = _mod_part(c_all, w_ada[0], b_cols)
    mod = _exchange_small(mod_part.reshape(N_DEV, 1, n_ada), False, "scatter_mod").reshape(1, N_MOD * D)
    groups = {"in_a": [w_in[0].T[:, :D // 2]], "in_b": [w_in[0].T[:, D // 2:]], "mix": [w_attn_proj[0], w_hgrn_proj[0], w_out[0]],
              "ffn_in": [w_ffn_in[0]], "ffn_out": [w_ffn_out[0]]}

    def start(group, dep):
        shards, dep = lax.optimization_barrier((groups[group], dep))
        return _gather_start([t.astype(BF16) for t in shards], "gather_start_" + group, deps=[dep])

    gathers = {"in_a": start("in_a", mod)}
    gathers["in_b"] = start("in_b", gathers["in_a"]["token"])
    gathers["mix"] = start("mix", gathers["in_b"]["token"])
    gathers["ffn_in"] = start("ffn_in", gathers["mix"]["token"])
    gathers["ffn_out"] = start("ffn_out", gathers["ffn_in"]["token"])

    passes = {}

    def prefetch(group, after):
        lands = _gather_wait(gathers[group], [after], "gather_wait_" + group)
        passes[group] = _pass_start(lands, "gather_pass_start_" + group)
        return passes[group]["token"]

    def weights(group, after):
        if group in passes:
            lands = _pass_wait(passes[group], [after], "gather_pass_wait_" + group)
        else:
            after = [after, gathers["ffn_out"]["token"]]
            lands = _gather_pass(_gather_wait(gathers[group], after, "gather_wait_" + group), "gather_pass_" + group)
        if group in ("in_a", "in_b"):
            return (lands[0].reshape(IN_COLS, D // 2),)
        if group == "mix":
            return lands[0], lands[1], lands[2].reshape(D, D)
        return (lands[0],) if group == "ffn_in" else (lands[0].reshape(FFN, D),)

    pairs, scatters = {}, {}

    def scatter(grads, group):
        pairs[group] = _pair_start(grads, "scatter_pair_" + group)
        return pairs[group]["token"]

    def scatter_on(group, after):
        if group in pairs:
            local, theirs = _pair_wait(pairs[group], [after], "scatter_pair_wait_" + group)
        else:
            local, theirs = after, _pair_exchange(after, "scatter_pair_" + group)
        parts = [_pair_add(g, t, "scatter_pair_add_%s_%d" % (group, k)) for k, (g, t) in enumerate(zip(local, theirs))]
        scatters[group] = _chips_start(parts, "scatter_start_" + group)
        return scatters[group]["token"]

    sinks_pad = jnp.pad(attn_sinks, ((0, 0), (0, LANE - ATT_HEADS)))
    loss, grad_x, small, h1, d_proj = _local_step(
        x[0], loss_target[0], _rope_tables(positions), mod, sinks_pad, hg_lower_bounds, hg_norm, g_pre_mix, g_post_mix, g_pre_ffn, g_post_ffn,
        weights, prefetch, scatter, scatter_on)
    loss = lax.psum(loss[0, 0], ("x", "y", "c"))

    sizes = [t.shape[1] for t in small]
    parts = _exchange_small(jnp.concatenate(small, axis=1).reshape(1, 1, sum(sizes)), True, "gather_small_grads")
    dep = parts
    for half, cols in (("in_a", slice(0, D // 2)), ("in_b", slice(D // 2, D))):
        gw_half = _mm_tn(d_proj, h1[:, cols], 256, D // 2, BF16, "proj_dw_" + half, deps=[dep])
        dep = scatter_on(half, [gw_half.reshape(N_DEV, IN_COLS // N_DEV, D // 2)])
    offs = [sum(sizes[:k]) for k in range(len(sizes))]
    piece = lambda k, n=None: parts[:, :, offs[k]:offs[k] + (sizes[k] if n is None else n)]
    small_w = [(piece(0), b_ada, m_b_ada, v_b_ada), (piece(1), g_pre_mix, m_g_pre_mix, v_g_pre_mix),
               (piece(2), g_post_mix, m_g_post_mix, v_g_post_mix), (piece(3), g_pre_ffn, m_g_pre_ffn, v_g_pre_ffn),
               (piece(4), g_post_ffn, m_g_post_ffn, v_g_post_ffn),
               (piece(5).reshape(N_DEV, 2, HG_W), hg_lower_bounds, m_hg_lower_bounds, v_hg_lower_bounds),
               (piece(6), hg_norm, m_hg_norm, v_hg_norm), (piece(7, ATT_HEADS), attn_sinks, m_attn_sinks, v_attn_sinks)]
    names = ["b_ada", "g_pre_mix", "g_post_mix", "g_pre_ffn", "g_post_ffn", "hg_lower_bounds", "hg_norm", "attn_sinks"]
    res = {n: _adamw(p, w, m, v, "adamw_" + n) for n, (p, w, m, v) in zip(names, small_w)}

    dmod_cols = lax.dynamic_slice(parts.reshape(N_DEV, -1), (0, my_id * n_ada), (N_DEV, n_ada))
    res["w_ada"] = list(_update_w_ada(c_all.T, dmod_cols, w_ada[0], m_w_ada[0], v_w_ada[0]))

    big = {"ffn_out": [("w_ffn_out", w_ffn_out, m_w_ffn_out, v_w_ffn_out)], "ffn_in": [("w_ffn_in", w_ffn_in, m_w_ffn_in, v_w_ffn_in)],
           "mix": [("w_attn_proj", w_attn_proj, m_w_attn_proj, v_w_attn_proj), ("w_hgrn_proj", w_hgrn_proj, m_w_hgrn_proj, v_w_hgrn_proj),
                   ("w_out", w_out, m_w_out, v_w_out)]}
    after = [scatters["in_b"]["token"]]
    for group, members in big.items():
        local, lands = _chips_wait(scatters[group], after, "scatter_wait_" + group)
        for (n, w, m, v), mine, land in zip(members, local, lands):
            res[n] = _adamw(land, w[0], m[0], v[0], "adamw_" + n, own=mine[0])
            after = after + [res[n][1]]
    after = [res[n][1] for n in res]
    halves = [_chips_wait(scatters[half], after, "scatter_wait_" + half) for half in ("in_a", "in_b")]
    own = jnp.concatenate([local[0][0] for local, _ in halves], axis=1)
    land = jnp.concatenate([lands[0] for _, lands in halves], axis=2)
    res["w_in"] = [t.T for t in _adamw(land, w_in[0].T, m_w_in[0].T, v_w_in[0].T, "adamw_w_in", own=own, max_elems=WIDE_TILE)]

    order = ["w_ada", "b_ada", "g_pre_mix", "g_post_mix", "g_pre_ffn", "g_post_ffn", "w_in", "attn_sinks", "w_attn_proj",
             "hg_lower_bounds", "hg_norm", "w_hgrn_proj", "w_out", "w_ffn_in", "w_ffn_out"]
    lead = {"w_ada", "w_in", "w_attn_proj", "w_hgrn_proj", "w_out", "w_ffn_in", "w_ffn_out"}
    outs = [loss, grad_x[None]]
    for k in range(4):
        outs += [res[n][k][None] if n in lead else res[n][k] for n in order]
    return tuple(outs)
```

```python
import functools

import jax
import jax.numpy as jnp
from jax import lax
from jax.experimental import pallas as pl
from jax.experimental.pallas import tpu as pltpu

F32 = jnp.float32
BF16 = jnp.bfloat16

N_DEV = 8
D = 2048
ATT_HEADS = 16
KV_HEADS = 2
HEAD_DIM = 64
GROUP = ATT_HEADS // KV_HEADS
ATT_W = ATT_HEADS * HEAD_DIM
BLK = 128
ROT = HEAD_DIM // 4
ROPE_THETA = 500000.0
HG_HEADS = 8
HG_K = 128
HG_W = HG_HEADS * HG_K
CHUNK = 64
SUB = 16
FFN = 5632
N_MOD = 6
EPS = 1e-6
LANE = 128
Q_A, K_A, V_A, Q_H, F_H, I_H, G_H, GT_A, GT_H, IN_COLS = 0, 1024, 1152, 1280, 2304, 3328, 4352, 5376, 7424, 9472

ADAM_LR, ADAM_B1, ADAM_B2, ADAM_EPS, ADAM_WD, ADAM_STEP = 0.001, 0.9, 0.999, 1e-08, 0.01, 10

TR = 256
HG_TB = 512
VMEM_BIG = 56 << 20
MESH = pl.DeviceIdType.MESH


def _sds(shape, dtype):
    return jax.ShapeDtypeStruct(shape, dtype)


def _params(n_axes, vmem=None):
    return pltpu.CompilerParams(dimension_semantics=("arbitrary",) * n_axes, vmem_limit_bytes=vmem)


def _sig(t):
    return 1.0 / (1.0 + jnp.exp(-t))


def _dot(a, b, dims):
    return lax.dot_general(a, b, (dims, ((), ())), preferred_element_type=F32)


NN = ((1,), (0,))
NT = ((1,), (1,))
TN = ((0,), (0,))


def _matmul(a, b, a_spec, b_spec, o_spec, out_shape, grid, dims, acc_shape, name, deps=(), add=None):
    nk = grid[2]
    nd = len(deps)
    extra = [] if add is None else [add]

    def body(a_ref, b_ref, *rest):
        o_ref, scratch = rest[nd + len(extra)], rest[nd + len(extra) + 1:]
        part = _dot(a_ref[...], b_ref[...], dims)
        if add is not None:
            assert nk == 1
            part = part + rest[nd][...]
        if nk == 1:
            o_ref[...] = part.astype(o_ref.dtype)
        else:
            acc = scratch[0]
            k = pl.program_id(2)

            @pl.when(k == 0)
            def _():
                acc[...] = part

            @pl.when(k > 0)
            def _():
                acc[...] += part

            @pl.when(k == nk - 1)
            def _():
                o_ref[...] = acc[...].astype(o_ref.dtype)

    return pl.pallas_call(
        body, grid=grid, in_specs=[a_spec, b_spec] + [pl.BlockSpec(memory_space=pl.ANY)] * nd + [o_spec] * len(extra),
        out_specs=o_spec, out_shape=out_shape, scratch_shapes=[pltpu.VMEM(acc_shape, F32)] if nk > 1 else [],
        input_output_aliases={2 + nd: 0} if extra else {},
        compiler_params=_params(3, VMEM_BIG), name=name)(a, b, *deps, *extra)


def _mm_nn(a, b, tm, tn, tk, out_dtype, name):
    m, k = a.shape
    n = b.shape[1]
    return _matmul(a, b, pl.BlockSpec((tm, tk), lambda j, i, kk: (i, kk)), pl.BlockSpec((tk, tn), lambda j, i, kk: (kk, j)),
                   pl.BlockSpec((tm, tn), lambda j, i, kk: (i, j)), _sds((m, n), out_dtype),
                   (n // tn, m // tm, k // tk), NN, (tm, tn), name)


def _mm_nn_dm(a, b, tm, out_dtype, name):
    m, k = a.shape
    n = b.shape[2]
    return _matmul(a, b, pl.BlockSpec((tm, k), lambda j, i, kk: (i, 0)), pl.BlockSpec((None, k, n), lambda j, i, kk: (j, 0, 0)),
                   pl.BlockSpec((tm, n), lambda j, i, kk: (i, j)), _sds((m, N_DEV * n), out_dtype),
                   (N_DEV, m // tm, 1), NN, (tm, n), name)


def _mm_nt(a, b, tm, tn, tk, out_dtype, name, deps=(), add=None):
    m, k = a.shape
    n = b.shape[0]
    return _matmul(a, b, pl.BlockSpec((tm, tk), lambda j, i, kk: (i, kk)), pl.BlockSpec((tn, tk), lambda j, i, kk: (j, kk)),
                   pl.BlockSpec((tm, tn), lambda j, i, kk: (i, j)), _sds((m, n), out_dtype),
                   (n // tn, m // tm, k // tk), NT, (tm, tn), name, deps, add)


def _mm_nt_dm(a, b, tm, tn, out_dtype, name, deps=()):
    m = a.shape[0]
    n_out, n = b.shape[1], b.shape[2]
    return _matmul(a, b, pl.BlockSpec((tm, n), lambda j, i, kk: (i, kk)), pl.BlockSpec((None, tn, n), lambda j, i, kk: (kk, j, 0)),
                   pl.BlockSpec((tm, tn), lambda j, i, kk: (i, j)), _sds((m, n_out), out_dtype),
                   (n_out // tn, m // tm, N_DEV), NT, (tm, tn), name, deps)


def _mm_tn(a, b, tm, tn, out_dtype, name, deps=()):
    s, m = a.shape
    n = b.shape[1]
    return _matmul(a, b, pl.BlockSpec((s, tm), lambda j, i, kk: (0, i)), pl.BlockSpec((s, tn), lambda j, i, kk: (0, j)),
                   pl.BlockSpec((tm, tn), lambda j, i, kk: (i, j)), _sds((m, n), out_dtype),
                   (n // tn, m // tm, 1), TN, (tm, tn), name, deps)


def _mm_tn_dm(a, b, tm, out_dtype, name):
    s, m = a.shape
    n = b.shape[1] // N_DEV
    return _matmul(a, b, pl.BlockSpec((s, tm), lambda j, i, kk: (0, i)), pl.BlockSpec((s, n), lambda j, i, kk: (0, j)),
                   pl.BlockSpec((None, tm, n), lambda j, i, kk: (j, i, 0)), _sds((N_DEV, m, n), out_dtype),
                   (N_DEV, m // tm, 1), TN, (tm, n), name)


def _row_spec():
    return pl.BlockSpec((TR, D), lambda i: (i, 0))


def _vec_spec(k=0):
    return pl.BlockSpec((1, D), lambda i: (0, k))


def _acc_rows(ref, first, val):
    @pl.when(first)
    def _():
        ref[...] = val

    @pl.when(jnp.logical_not(first))
    def _():
        ref[...] += val


def _pre_fwd(x, g, mod, k_scale, k_shift, name):
    s = x.shape[0]

    def body(x_ref, g_ref, sc_ref, sh_ref, h_ref):
        xv = x_ref[...]
        r = lax.rsqrt(jnp.mean(xv * xv, axis=-1, keepdims=True) + EPS)
        n = xv * r * g_ref[...]
        h_ref[...] = (n * (1.0 + sc_ref[...]) + sh_ref[...]).astype(h_ref.dtype)

    return pl.pallas_call(body, grid=(s // TR,), in_specs=[_row_spec(), _vec_spec(), _vec_spec(k_scale), _vec_spec(k_shift)],
                          out_specs=_row_spec(), out_shape=_sds((s, D), BF16), compiler_params=_params(1), name=name)(x, g, mod, mod)


def _post_fwd(x, y, g, mod, k_gate, name):
    s = x.shape[0]

    def body(x_ref, y_ref, g_ref, gt_ref, o_ref):
        yv = y_ref[...]
        r = lax.rsqrt(jnp.mean(yv * yv, axis=-1, keepdims=True) + EPS)
        o_ref[...] = x_ref[...] + gt_ref[...] * (yv * r * g_ref[...])

    return pl.pallas_call(body, grid=(s // TR,), in_specs=[_row_spec(), _row_spec(), _vec_spec(), _vec_spec(k_gate)],
                          out_specs=_row_spec(), out_shape=_sds((s, D), F32), compiler_params=_params(1), name=name)(x, y, g, mod)


def _post_loss_bwd(x, y, g, mod, k_gate, tgt, name):
    s = x.shape[0]

    def body(x_ref, y_ref, g_ref, gt_ref, t_ref, e_ref, loss_ref, dy_ref, dgt_ref, dg_ref):
        first = pl.program_id(0) == 0
        yv, gv, gate = y_ref[...], g_ref[...], gt_ref[...]
        r = lax.rsqrt(jnp.mean(yv * yv, axis=-1, keepdims=True) + EPS)
        yh = yv * r
        err = x_ref[...] + gate * (yh * gv) - t_ref[...]
        e = err * (1.0 / D)
        e_ref[...] = e
        _acc_rows(loss_ref, first, 0.5 * jnp.sum(jnp.mean(err * err, axis=-1, keepdims=True), axis=0, keepdims=True))
        dn = e * gate
        dgn = dn * gv
        dy_ref[...] = (r * (dgn - yh * jnp.mean(dgn * yh, axis=-1, keepdims=True))).astype(dy_ref.dtype)
        _acc_rows(dgt_ref, first, jnp.sum(e * (yh * gv), axis=0, keepdims=True))
        _acc_rows(dg_ref, first, jnp.sum(dn * yh, axis=0, keepdims=True))

    return pl.pallas_call(body, grid=(s // TR,),
                          in_specs=[_row_spec(), _row_spec(), _vec_spec(), _vec_spec(k_gate), _row_spec()],
                          out_specs=[_row_spec(), pl.BlockSpec((1, 1), lambda i: (0, 0)), _row_spec(), _vec_spec(), _vec_spec()],
                          out_shape=[_sds((s, D), F32), _sds((1, 1), F32), _sds((s, D), BF16), _sds((1, D), F32), _sds((1, D), F32)],
                          compiler_params=_params(1), name=name)(x, y, g, mod, tgt)


def _pre_bwd(dh, x, res, g, mod, k_scale, name):
    s = x.shape[0]

    def body(dh_ref, x_ref, res_ref, g_ref, sc_ref, dx_ref, dsh_ref, dsc_ref, dg_ref):
        first = pl.program_id(0) == 0
        xv, dh_v, gv = x_ref[...], dh_ref[...], g_ref[...]
        r = lax.rsqrt(jnp.mean(xv * xv, axis=-1, keepdims=True) + EPS)
        xh = xv * r
        dn = dh_v * (1.0 + sc_ref[...])
        dgn = dn * gv
        dx_ref[...] = res_ref[...] + r * (dgn - xh * jnp.mean(dgn * xh, axis=-1, keepdims=True))
        _acc_rows(dsh_ref, first, jnp.sum(dh_v, axis=0, keepdims=True))
        _acc_rows(dsc_ref, first, jnp.sum(dh_v * (xh * gv), axis=0, keepdims=True))
        _acc_rows(dg_ref, first, jnp.sum(dn * xh, axis=0, keepdims=True))

    return pl.pallas_call(body, grid=(s // TR,),
                          in_specs=[_row_spec(), _row_spec(), _row_spec(), _vec_spec(), _vec_spec(k_scale)],
                          out_specs=[_row_spec(), _vec_spec(), _vec_spec(), _vec_spec()],
                          out_shape=[_sds((s, D), F32)] + [_sds((1, D), F32)] * 3,
                          compiler_params=_params(1), name=name)(dh, x, res, g, mod)


def _post_bwd(dx, y, g, mod, k_gate, name):
    s = y.shape[0]

    def body(dx_ref, y_ref, g_ref, gt_ref, dy_ref, dgt_ref, dg_ref):
        first = pl.program_id(0) == 0
        yv, dxv, gv = y_ref[...], dx_ref[...], g_ref[...]
        r = lax.rsqrt(jnp.mean(yv * yv, axis=-1, keepdims=True) + EPS)
        yh = yv * r
        dn = dxv * gt_ref[...]
        dgn = dn * gv
        dy_ref[...] = (r * (dgn - yh * jnp.mean(dgn * yh, axis=-1, keepdims=True))).astype(dy_ref.dtype)
        _acc_rows(dgt_ref, first, jnp.sum(dxv * (yh * gv), axis=0, keepdims=True))
        _acc_rows(dg_ref, first, jnp.sum(dn * yh, axis=0, keepdims=True))

    return pl.pallas_call(body, grid=(s // TR,), in_specs=[_row_spec(), _row_spec(), _vec_spec(), _vec_spec(k_gate)],
                          out_specs=[_row_spec(), _vec_spec(), _vec_spec()],
                          out_shape=[_sds((s, D), BF16), _sds((1, D), F32), _sds((1, D), F32)],
                          compiler_params=_params(1), name=name)(dx, y, g, mod)


SW_TN = 1408
SW_TR = 512
TALL = 2048


def _swiglu_fwd(gu, deps=()):
    s = gu.shape[0]
    nb = FFN // SW_TN

    def body(g_ref, u_ref, *rest):
        a_ref = rest[len(deps)]
        gv = g_ref[...]
        a_ref[...] = (gv * _sig(gv) * u_ref[...]).astype(a_ref.dtype)

    return pl.pallas_call(body, grid=(s // SW_TR, nb),
                          in_specs=[pl.BlockSpec((SW_TR, SW_TN), lambda i, j: (i, j)), pl.BlockSpec((SW_TR, SW_TN), lambda i, j: (i, j + nb))]
                          + [pl.BlockSpec(memory_space=pl.ANY)] * len(deps),
                          out_specs=pl.BlockSpec((SW_TR, SW_TN), lambda i, j: (i, j)), out_shape=_sds((s, FFN), BF16),
                          compiler_params=_params(2, 48 << 20), name="swiglu_fwd")(gu, gu, *deps)


def _swiglu_bwd(dact, gu):
    s = gu.shape[0]
    nb = FFN // SW_TN
    n_steps = (s // SW_TR) * nb

    def body(da_ref, g_ref, u_ref, o_ref, buf, sems):
        i, j = pl.program_id(0), pl.program_id(1)
        step = i * nb + j
        slot = step % 2

        def tiles(sl):
            rows = pl.ds(pl.multiple_of(i * SW_TR, SW_TR), SW_TR)
            return [pltpu.make_async_copy(buf.at[sl, h], o_ref.at[rows, pl.ds(pl.multiple_of((j + nb * h) * SW_TN, LANE), SW_TN)], sems.at[sl, h])
                    for h in range(2)]

        @pl.when(step >= 2)
        def _():
            for cp in tiles(slot):
                cp.wait()

        gv, da = g_ref[...], da_ref[...]
        sg = _sig(gv)
        buf[slot, 0] = (da * u_ref[...] * (sg * (1.0 + gv * (1.0 - sg)))).astype(buf.dtype)
        buf[slot, 1] = (da * (gv * sg)).astype(buf.dtype)
        for cp in tiles(slot):
            cp.start()

        @pl.when(step == n_steps - 1)
        def _():
            for cp in tiles(slot) + (tiles(1 - slot) if n_steps > 1 else []):
                cp.wait()

    blk = lambda f: pl.BlockSpec((SW_TR, SW_TN), f)
    return pl.pallas_call(body, grid=(s // SW_TR, nb),
                          in_specs=[blk(lambda i, j: (i, j)), blk(lambda i, j: (i, j)), blk(lambda i, j: (i, j + nb))],
                          out_specs=pl.BlockSpec(memory_space=pl.ANY), out_shape=_sds((s, 2 * FFN), BF16),
                          scratch_shapes=[pltpu.VMEM((2, 2, SW_TR, SW_TN), BF16), pltpu.SemaphoreType.DMA((2, 2))],
                          compiler_params=_params(2, 48 << 20), name="swiglu_bwd")(dact, gu, gu)


MG_TN = 256


def _merge_fwd(y_a, y_h, proj):
    s = y_a.shape[0]
    tn = MG_TN
    ba, bh = GT_A // tn, GT_H // tn

    def body(ya_ref, yh_ref, ga_ref, gh_ref, m_ref):
        m_ref[...] = (_sig(ga_ref[...]) * ya_ref[...] + _sig(gh_ref[...]) * yh_ref[...]).astype(m_ref.dtype)

    tr = min(s, TALL)
    blk = lambda f: pl.BlockSpec((tr, tn), f)
    return pl.pallas_call(body, grid=(s // tr, D // tn),
                          in_specs=[blk(lambda i, j: (i, j)), blk(lambda i, j: (i, j)), blk(lambda i, j: (i, j + ba)), blk(lambda i, j: (i, j + bh))],
                          out_specs=blk(lambda i, j: (i, j)), out_shape=_sds((s, D), BF16),
                          compiler_params=_params(2), name="merge_fwd")(y_a, y_h, proj, proj)


def _merge_bwd(dm, y_a, y_h, proj):
    s = y_a.shape[0]
    tn = MG_TN
    ba, bh = GT_A // tn, GT_H // tn

    def body(dm_ref, ya_ref, yh_ref, ga_ref, gh_ref, dya_ref, dyh_ref, dga_ref, dgh_ref):
        dmv = dm_ref[...]
        sa, sh = _sig(ga_ref[...]), _sig(gh_ref[...])
        dya_ref[...] = (dmv * sa).astype(BF16)
        dyh_ref[...] = (dmv * sh).astype(BF16)
        dga_ref[...] = (dmv * ya_ref[...] * (sa * (1.0 - sa))).astype(BF16)
        dgh_ref[...] = (dmv * yh_ref[...] * (sh * (1.0 - sh))).astype(BF16)

    tr = min(s, TALL)
    blk = lambda f: pl.BlockSpec((tr, tn), f)
    nat = blk(lambda i, j: (i, j))
    return pl.pallas_call(body, grid=(s // tr, D // tn),
                          in_specs=[nat, nat, nat, blk(lambda i, j: (i, j + ba)), blk(lambda i, j: (i, j + bh))],
                          out_specs=[nat] * 4, out_shape=[_sds((s, D), BF16)] * 4,
                          compiler_params=_params(2), name="merge_bwd")(dm, y_a, y_h, proj, proj)


def _hgout_fwd(o_raw, proj, hg_norm):
    s = o_raw.shape[0]
    bg = G_H // LANE

    def body(o_ref, g_ref, n_ref, out_ref):
        ov = o_ref[...]
        r = lax.rsqrt(jnp.mean(ov * ov, axis=-1, keepdims=True) + EPS)
        out_ref[...] = (ov * r * n_ref[...] * _sig(g_ref[...])).astype(out_ref.dtype)

    tr = min(s, TALL)
    blk = lambda f: pl.BlockSpec((tr, LANE), f)
    return pl.pallas_call(body, grid=(s // tr, HG_HEADS),
                          in_specs=[blk(lambda i, h: (i, h)), blk(lambda i, h: (i, h + bg)), pl.BlockSpec((1, LANE), lambda i, h: (0, 0))],
                          out_specs=blk(lambda i, h: (i, h)), out_shape=_sds((s, HG_W), BF16),
                          compiler_params=_params(2), name="hgout_fwd")(o_raw, proj, hg_norm)


def _hgout_bwd(d_out, o_raw, proj, hg_norm):
    s = o_raw.shape[0]
    bg = G_H // LANE

    def body(d_ref, o_ref, g_ref, n_ref, do_ref, dg_ref, dn_ref):
        first = jnp.logical_and(pl.program_id(0) == 0, pl.program_id(1) == 0)
        ov, dv, nv = o_ref[...], d_ref[...], n_ref[...]
        sg = _sig(g_ref[...])
        r = lax.rsqrt(jnp.mean(ov * ov, axis=-1, keepdims=True) + EPS)
        oh = ov * r
        d_on = dv * sg
        dg_ref[...] = (dv * (oh * nv) * (sg * (1.0 - sg))).astype(dg_ref.dtype)
        t = d_on * nv
        do_ref[...] = r * (t - oh * jnp.mean(t * oh, axis=-1, keepdims=True))
        _acc_rows(dn_ref, first, jnp.sum(d_on * oh, axis=0, keepdims=True))

    tr = min(s, TALL)
    blk = lambda f: pl.BlockSpec((tr, LANE), f)
    vec = pl.BlockSpec((1, LANE), lambda i, h: (0, 0))
    return pl.pallas_call(body, grid=(s // tr, HG_HEADS),
                          in_specs=[blk(lambda i, h: (i, h)), blk(lambda i, h: (i, h)), blk(lambda i, h: (i, h + bg)), vec],
                          out_specs=[blk(lambda i, h: (i, h)), blk(lambda i, h: (i, h)), vec],
                          out_shape=[_sds((s, HG_W), F32), _sds((s, HG_W), BF16), _sds((1, LANE), F32)],
                          compiler_params=_params(2), name="hgout_bwd")(d_out, o_raw, proj, hg_norm)


def _rope(t, cos, s_lo, s_hi):
    return t * cos + pltpu.roll(t, LANE - ROT // 2, 1) * s_lo + pltpu.roll(t, ROT // 2, 1) * s_hi


def _rope_wide(t, cos, s_lo, s_hi):
    return jnp.concatenate([_rope(t[:, k * LANE:(k + 1) * LANE], cos, s_lo, s_hi) for k in range(t.shape[1] // LANE)], axis=1)


def _attn_mask(has_prev):
    kj = lax.broadcasted_iota(jnp.int32, (2 * BLK, BLK), 0)
    qi = lax.broadcasted_iota(jnp.int32, (2 * BLK, BLK), 1)
    rel = BLK + qi - kj
    band = jnp.logical_and(rel >= 0, rel < BLK)
    return jnp.logical_and(band, jnp.logical_or(has_prev, kj >= BLK))


def _attn_specs():
    prev = lambda i: jnp.maximum(i - 1, 0)
    kb, vb = K_A // LANE, V_A // LANE
    blk = lambda f: pl.BlockSpec((BLK, LANE), f)
    tabs = [blk(lambda i: (i, 0))] * 3 + [blk(lambda i: (prev(i), 0))] * 3
    return [pl.BlockSpec((BLK, ATT_W), lambda i: (i, 0)), blk(lambda i: (i, kb)), blk(lambda i: (prev(i), kb)),
            blk(lambda i: (i, vb)), blk(lambda i: (prev(i), vb))] + tabs + [pl.BlockSpec((1, LANE), lambda i: (0, 0))]


def _attn_logits(qh, kg):
    return _dot(kg, qh, NT)


def _attn_probs(raw, mask, sk):
    logits = jnp.where(mask, raw * (HEAD_DIM ** -0.5), -jnp.inf)
    m = jnp.maximum(jnp.max(logits, axis=0, keepdims=True), sk)
    p = jnp.exp(logits - m)
    e_sink = jnp.exp(sk - m)
    inv = 1.0 / (jnp.sum(p, axis=0, keepdims=True) + e_sink)
    return p, inv, e_sink * inv


def _attn_fwd(proj, tabs, sinks):
    s = proj.shape[0]

    def body(q_ref, kc_ref, kp_ref, vc_ref, vp_ref, c0, l0, h0, c1, l1, h1, sk_ref, o_ref):
        i = pl.program_id(0)
        mask = _attn_mask(i > 0)
        q = _rope_wide(q_ref[...], c0[...], l0[...], h0[...]).astype(BF16)
        kk = jnp.concatenate([_rope(kp_ref[...], c1[...], l1[...], h1[...]), _rope(kc_ref[...], c0[...], l0[...], h0[...])], axis=0).astype(BF16)
        v_t = jnp.concatenate([vp_ref[...], vc_ref[...]], axis=0).T.astype(BF16)
        part = lambda t, h: t[:, h * HEAD_DIM:(h + 1) * HEAD_DIM]
        k_heads = [part(kk, g) for g in range(KV_HEADS)]

        def head(h):
            g = h // GROUP
            raw = _attn_logits(part(q, h), k_heads[g])
            yield
            p, inv, _ = _attn_probs(raw, mask, sk_ref[:, h:h + 1])
            yield
            out_t = _dot(v_t[g * HEAD_DIM:(g + 1) * HEAD_DIM], p.astype(BF16), NN)
            yield
            return out_t * inv

        o_ref[...] = jnp.concatenate(_interleave([head(h) for h in range(ATT_HEADS)]), axis=0).T.astype(o_ref.dtype)

    return pl.pallas_call(body, grid=(s // BLK,), in_specs=_attn_specs(),
                          out_specs=pl.BlockSpec((BLK, ATT_W), lambda i: (i, 0)), out_shape=_sds((s, ATT_W), BF16),
                          compiler_params=_params(1), name="attn_fwd")(proj, proj, proj, proj, proj, *tabs, *tabs, sinks)


def _attn_bwd(proj, tabs, sinks, d_att):
    s = proj.shape[0]

    def body(q_ref, kc_ref, kp_ref, vc_ref, vp_ref, c0, l0, h0, c1, l1, h1, sk_ref, do_ref, dq_ref, dk_ref, dv_ref, ds_ref):
        i = pl.program_id(0)

        @pl.when(i == 0)
        def _():
            dk_ref[...] = jnp.zeros_like(dk_ref)
            dv_ref[...] = jnp.zeros_like(dv_ref)
            ds_ref[...] = jnp.zeros_like(ds_ref)

        mask = _attn_mask(i > 0)
        q = _rope_wide(q_ref[...], c0[...], l0[...], h0[...]).astype(BF16)
        kk = jnp.concatenate([_rope(kp_ref[...], c1[...], l1[...], h1[...]), _rope(kc_ref[...], c0[...], l0[...], h0[...])], axis=0).astype(BF16)
        k_f32 = jnp.concatenate([_rope(kp_ref[...], c1[...], l1[...], h1[...]), _rope(kc_ref[...], c0[...], l0[...], h0[...])], axis=0)
        k_t = k_f32.T.astype(BF16)
        vv = jnp.concatenate([vp_ref[...], vc_ref[...]], axis=0).astype(BF16)
        d_o = do_ref[...].astype(BF16)
        lane = lax.broadcasted_iota(jnp.int32, (1, LANE), 1)
        part = lambda t, h: t[:, h * HEAD_DIM:(h + 1) * HEAD_DIM]
        k_heads = [part(kk, g) for g in range(KV_HEADS)]
        v_heads = [part(vv, g) for g in range(KV_HEADS)]

        def head(h):
            g = h // GROUP
            qh, doh = part(q, h), part(d_o, h)
            raw = _attn_logits(qh, k_heads[g])
            d_p = _dot(v_heads[g], doh, NT)
            yield
            p, inv, p_sink = _attn_probs(raw, mask, sk_ref[:, h:h + 1])
            prob = p * inv
            dv = _dot(prob.astype(BF16), doh, NN)
            yield
            dd = jnp.sum(prob * d_p, axis=0, keepdims=True)
            d_s = (prob * (d_p - dd)).astype(BF16)
            d_sink = jnp.where(lane == h, -jnp.sum(p_sink * dd, axis=1, keepdims=True), 0.0)
            dq_t = _dot(k_t[g * HEAD_DIM:(g + 1) * HEAD_DIM], d_s, NN)
            dk = _dot(d_s, qh, NN)
            yield
            return dq_t * (HEAD_DIM ** -0.5), dk * (HEAD_DIM ** -0.5), dv, d_sink

        per_head = _interleave([head(h) for h in range(ATT_HEADS)])
        dqs = [jnp.concatenate([t[0] for t in per_head], axis=0).T]
        group_sum = lambda k, g: functools.reduce(jnp.add, [t[k] for t in per_head[g * GROUP:(g + 1) * GROUP]])
        dks = [group_sum(1, g) for g in range(KV_HEADS)]
        dvs = [group_sum(2, g) for g in range(KV_HEADS)]
        d_sink = functools.reduce(jnp.add, [t[3] for t in per_head])
        dq_ref[...] = _rope_wide(jnp.concatenate(dqs, axis=1), c0[...], -l0[...], -h0[...]).astype(dq_ref.dtype)
        d_k = jnp.concatenate(dks, axis=1)
        d_v = jnp.concatenate(dvs, axis=1)
        cur = pl.ds(pl.multiple_of(i * BLK, BLK), BLK)
        prv = pl.ds(pl.multiple_of(jnp.maximum(i - 1, 0) * BLK, BLK), BLK)
        dk_ref[prv, :] += _rope(d_k[:BLK], c1[...], -l1[...], -h1[...])
        dk_ref[cur, :] += _rope(d_k[BLK:], c0[...], -l0[...], -h0[...])
        dv_ref[prv, :] += d_v[:BLK]
        dv_ref[cur, :] += d_v[BLK:]
        ds_ref[...] += d_sink

    full = pl.BlockSpec((s, LANE), lambda i: (0, 0))
    return pl.pallas_call(body, grid=(s // BLK,), in_specs=_attn_specs() + [pl.BlockSpec((BLK, ATT_W), lambda i: (i, 0))],
                          out_specs=[pl.BlockSpec((BLK, ATT_W), lambda i: (i, 0)), full, full, pl.BlockSpec((1, LANE), lambda i: (0, 0))],
                          out_shape=[_sds((s, ATT_W), BF16), _sds((s, LANE), F32), _sds((s, LANE), F32), _sds((1, LANE), F32)],
                          compiler_params=_params(1), name="attn_bwd")(proj, proj, proj, proj, proj, *tabs, *tabs, sinks, d_att)


def _tri_matmul(tri, t):
    hi = t.astype(BF16)
    r1 = t - hi.astype(F32)
    mid = r1.astype(BF16)
    lo = (r1 - mid.astype(F32)).astype(BF16)
    return _dot(tri, hi, NN) + _dot(tri, mid, NN) + _dot(tri, lo, NN)


def _lower_bound(hl):
    a, b = hl[0:1, :], hl[1:2, :]
    mx = jnp.maximum(a, b)
    ea, eb = jnp.exp(a - mx), jnp.exp(b - mx)
    return ea / (ea + eb)


def _hg_gates(q_raw, f_raw, lb, tri_lower):
    sg = _sig(f_raw)
    f = lb + (1.0 - lb) * sg
    sq = _sig(q_raw)
    b = _tri_matmul(tri_lower, jnp.log(f))
    return sg, f, 1.0 - f, sq, q_raw * sq, b


HG_PAIR_FWD = 8
HG_PAIR_BWD = 8


def _hg_specs(n_map, pair):
    blk = lambda off, p: pl.BlockSpec((HG_TB, LANE), lambda h, n: (n_map(n), off // LANE + pair * h + p))
    return [blk(off, p) for off in (Q_H, F_H, I_H) for p in range(pair)] + [pl.BlockSpec((2, pair * LANE), lambda h, n: (0, h))]


def _interleave(gens):
    out = [None] * len(gens)
    live = list(range(len(gens)))
    while live:
        for k in list(live):
            try:
                next(gens[k])
            except StopIteration as stop:
                out[k] = stop.value
                live.remove(k)
    return out


def _hg_spread():
    c = lax.broadcasted_iota(jnp.int32, (CHUNK, SUB * SUB), 0)
    l = lax.broadcasted_iota(jnp.int32, (CHUNK, SUB * SUB), 1)
    r = lax.broadcasted_iota(jnp.int32, (SUB, SUB * SUB), 0)
    lr = lax.broadcasted_iota(jnp.int32, (SUB, SUB * SUB), 1)
    shift = SUB.bit_length() - 1
    cols = [(c == lo + (l >> shift)).astype(BF16) for lo in range(0, CHUNK, SUB)]
    tile = [(c == lo + (l & (SUB - 1))).astype(BF16) for lo in range(0, CHUNK, SUB)]
    return cols, tile, (lr & (SUB - 1)) == r, (lr >> shift) == r


def _hg_intra(qs, kk, b, grad=None):
    lane = lax.broadcasted_iota(jnp.int32, (SUB, CHUNK), 1)
    row1 = lax.broadcasted_iota(jnp.int32, (SUB, 1), 0)
    kk_b = kk.astype(BF16)
    if grad is not None:
        d_a, d_at, (cols, tile, diag, block) = grad
    a_blocks, dq_blocks, dk_blocks, db_blocks = [], [], [], []
    dk_left = None
    for j in range(CHUNK // SUB):
        lo = j * SUB
        q_j, k_j, b_j = qs[lo:lo + SUB], kk[lo:lo + SUB], b[lo:lo + SUB]
        es = [jnp.where(row1 >= sx, jnp.exp(jnp.minimum(b_j - b_j[sx:sx + 1], 0.0)), 0.0) for sx in range(SUB)]
        pes = [q_j * e for e in es]
        pe = jnp.concatenate(pes, axis=0).astype(BF16)
        pairs = _dot(pe, kk_b, NT)
        yield
        a_j = jnp.zeros((SUB, CHUNK), F32)
        for sx in range(SUB):
            a_j = jnp.where(lane == lo + sx, pairs[sx * SUB:(sx + 1) * SUB], a_j)
        if grad is not None:
            da_j = d_a[lo:lo + SUB]
            ek = jnp.concatenate([e * k_j[sx:sx + 1] for sx, e in enumerate(es)], axis=0).astype(BF16)
            sel_t = jnp.where(diag, _dot(da_j.astype(BF16), cols[j], NN), 0.0).astype(BF16)
            sel_s = jnp.where(block, _dot(d_at[lo:lo + SUB].astype(BF16), tile[j], NN), 0.0).astype(BF16)
            pek = jnp.concatenate([p * k_j[sx:sx + 1] for sx, p in enumerate(pes)], axis=0).astype(BF16)
            yield
            dq_j = _dot(sel_t, ek, NN)
            dk_j = _dot(sel_s, pe, NN)
            db_j = _dot(sel_t, pek, NN) - _dot(sel_s, pek, NN)
            yield
        if j > 0:
            ref = b[lo - 1:lo]
            sc_q = jnp.exp(b_j - ref)
            sc_k = jnp.exp(jnp.minimum(ref - b, 0.0))
            qt = (q_j * sc_q).astype(BF16)
            kt = (kk * sc_k).astype(BF16)
            left = _dot(qt, kt, NT)
            yield
            a_j = a_j + jnp.where(lane < lo, left, 0.0)
            if grad is not None:
                da_left = jnp.where(lane < lo, da_j, 0.0).astype(BF16)
                dq_left = _dot(da_left, kt, NN) * sc_q
                dq_j = dq_j + dq_left
                db_j = db_j + q_j * dq_left
                t = _dot(da_left, qt, TN)
                yield
                t = t * sc_k
                dk_left = t if dk_left is None else dk_left + t
        a_blocks.append(a_j)
        if grad is not None:
            dq_blocks.append(dq_j)
            dk_blocks.append(dk_j)
            db_blocks.append(db_j)
    a = jnp.concatenate(a_blocks, axis=0)
    if grad is None:
        return a
    return a, jnp.concatenate(dq_blocks, axis=0), jnp.concatenate(dk_blocks, axis=0) + dk_left, jnp.concatenate(db_blocks, axis=0) - kk * dk_left


def _hgrn_fwd(proj, hl):
    s = proj.shape[0]
    n_chunk = HG_TB // CHUNK
    pair = HG_PAIR_FWD

    def body(*refs):
        q_refs, f_refs, i_refs = refs[:pair], refs[pair:2 * pair], refs[2 * pair:3 * pair]
        hl_ref, o_ref, st_out_ref, st_ref = refs[3 * pair:]

        @pl.when(pl.program_id(1) == 0)
        def _():
            st_ref[...] = jnp.zeros_like(st_ref)

        r_i = lax.broadcasted_iota(jnp.int32, (CHUNK, CHUNK), 0)
        c_i = lax.broadcasted_iota(jnp.int32, (CHUNK, CHUNK), 1)
        tri_lower = (r_i >= c_i).astype(BF16)

        def chunk(c, carry):
            rows = pl.ds(pl.multiple_of(c * CHUNK, CHUNK), CHUNK)
            def head(p):
                cols = slice(p * LANE, (p + 1) * LANE)
                lb = _lower_bound(hl_ref[:, cols])
                v = i_refs[p][rows, :].astype(BF16)
                _, _, kk, _, qs, b = _hg_gates(q_refs[p][rows, :], f_refs[p][rows, :], lb, tri_lower)
                yield
                st = st_ref[p]
                st_b = st.astype(BF16)
                st_out_ref[p, c] = st_b
                o_state = _dot((qs * jnp.exp(b)).astype(BF16), st_b, NT)
                b_last = b[CHUNK - 1:CHUNK, :]
                st_new = _dot(v, (kk * jnp.exp(b_last - b)).astype(BF16), TN)
                a = yield from _hg_intra(qs, kk, b)
                st_ref[p] = st * jnp.exp(b_last) + st_new
                o_ref[rows, cols] = o_state + _dot(a.astype(BF16), v, NN)

            _interleave([head(p) for p in range(pair)])
            return carry

        lax.fori_loop(0, n_chunk, chunk, 0)

    return pl.pallas_call(
        body, grid=(HG_HEADS // pair, s // HG_TB), in_specs=_hg_specs(lambda n: n, pair),
        out_specs=[pl.BlockSpec((HG_TB, pair * LANE), lambda h, n: (n, h)), pl.BlockSpec((pair, n_chunk, HG_K, HG_K), lambda h, n: (h, n, 0, 0))],
        out_shape=[_sds((s, HG_W), F32), _sds((HG_HEADS, s // CHUNK, HG_K, HG_K), BF16)],
        scratch_shapes=[pltpu.VMEM((pair, HG_K, HG_K), F32)],
        compiler_params=_params(2), name="hgrn_fwd")(*[proj] * (3 * pair), hl)


def _hgrn_bwd(proj, hl, states, d_o):
    s = proj.shape[0]
    n_chunk = HG_TB // CHUNK
    n_blk = s // HG_TB
    pair = HG_PAIR_BWD
    rev = lambda n: n_blk - 1 - n

    def body(*refs):
        q_refs, f_refs, i_refs = refs[:pair], refs[pair:2 * pair], refs[2 * pair:3 * pair]
        hl_ref, st_in_ref, do_ref, dq_ref, df_ref, di_ref, dhl_ref, dst_ref, dlb_ref = refs[3 * pair:]
        n = pl.program_id(1)

        @pl.when(n == 0)
        def _():
            dst_ref[...] = jnp.zeros_like(dst_ref)
            dlb_ref[...] = jnp.zeros_like(dlb_ref)

        r_i = lax.broadcasted_iota(jnp.int32, (CHUNK, CHUNK), 0)
        c_i = lax.broadcasted_iota(jnp.int32, (CHUNK, CHUNK), 1)
        tri_lower = (r_i >= c_i).astype(BF16)
        tri_upper = (r_i <= c_i).astype(BF16)
        row = lax.broadcasted_iota(jnp.int32, (CHUNK, 1), 0)
        spread = _hg_spread()

        def chunk(cc, carry):
            c = n_chunk - 1 - cc
            rows = pl.ds(pl.multiple_of(c * CHUNK, CHUNK), CHUNK)
            def head(p):
                cols = slice(p * LANE, (p + 1) * LANE)
                lb = _lower_bound(hl_ref[:, cols])
                q_raw = q_refs[p][rows, :]
                vb = i_refs[p][rows, :].astype(BF16)
                sg, f, kk, sq, qs, b = _hg_gates(q_raw, f_refs[p][rows, :], lb, tri_lower)
                yield
                e_b = jnp.exp(b)
                qe = qs * e_b
                b_last = b[CHUNK - 1:CHUNK, :]
                e_last = jnp.exp(b_last)
                e_kd = jnp.exp(b_last - b)
                kd = kk * e_kd
                st0 = st_in_ref[p, c]
                d_ob = do_ref[rows, cols].astype(BF16)
                dst = dst_ref[p]
                dst_b = dst.astype(BF16)
                d_a = jnp.where(r_i >= c_i, _dot(d_ob, vb, NT), 0.0)
                d_at = jnp.where(r_i <= c_i, _dot(vb, d_ob, NT), 0.0)
                d_v_st = _dot(kd.astype(BF16), dst_b, NT)
                d_kd = _dot(vb, dst_b, NN)
                d_qe = _dot(d_ob, st0, NN)
                dst_new = _dot(d_ob, qe.astype(BF16), TN)
                yield
                a, dqs, dkk, d_b = yield from _hg_intra(qs, kk, b, (d_a, d_at, spread))
                d_v = _dot(a.astype(BF16), d_ob, TN) + d_v_st
                dqs_st = d_qe * e_b
                dkk_st = d_kd * e_kd
                dqs = dqs + dqs_st
                dkk = dkk + dkk_st
                d_b_last = jnp.sum(d_kd * kd, axis=0, keepdims=True) + jnp.sum(dst * st0.astype(F32), axis=0, keepdims=True) * e_last
                d_b = d_b + qs * dqs_st - kk * dkk_st + jnp.where(row == CHUNK - 1, d_b_last, 0.0)
                d_g = _tri_matmul(tri_upper, d_b)
                dst_ref[p] = dst_new + dst * e_last
                yield
                d_f = d_g / f - dkk
                dlb_ref[:, cols] += jnp.sum(d_f * (1.0 - sg), axis=0, keepdims=True)
                dq_ref[rows, cols] = (dqs * (sq * (1.0 + q_raw * (1.0 - sq)))).astype(dq_ref.dtype)
                df_ref[rows, cols] = (d_f * (1.0 - lb) * (sg * (1.0 - sg))).astype(df_ref.dtype)
                di_ref[rows, cols] = d_v.astype(di_ref.dtype)

            _interleave([head(p) for p in range(pair)])
            return carry

        lax.fori_loop(0, n_chunk, chunk, 0)

        @pl.when(n == n_blk - 1)
        def _():
            lb = _lower_bound(hl_ref[...])
            d_hl0 = dlb_ref[...] * (lb * (1.0 - lb))
            dhl_ref[...] = jnp.concatenate([d_hl0, -d_hl0], axis=0)

    out_blk = pl.BlockSpec((HG_TB, pair * LANE), lambda h, n: (rev(n), h))
    return pl.pallas_call(
        body, grid=(HG_HEADS // pair, n_blk),
        in_specs=_hg_specs(rev, pair) + [pl.BlockSpec((pair, n_chunk, HG_K, HG_K), lambda h, n: (h, rev(n), 0, 0)), out_blk],
        out_specs=[out_blk, out_blk, out_blk, pl.BlockSpec((2, pair * LANE), lambda h, n: (0, h))],
        out_shape=[_sds((s, HG_W), BF16)] * 3 + [_sds((2, HG_W), F32)],
        scratch_shapes=[pltpu.VMEM((pair, HG_K, HG_K), F32), pltpu.VMEM((1, pair * LANE), F32)],
        compiler_params=_params(2), name="hgrn_bwd")(*[proj] * (3 * pair), hl, states, d_o)


def _mod_part(c_all, w_shard, b_shard):
    n = w_shard.shape[1]
    tn = 512

    def body(c_ref, w_ref, b_ref, o_ref):
        o_ref[...] = _dot(c_ref[...].astype(BF16), w_ref[...].astype(BF16), NN) + b_ref[...]

    return pl.pallas_call(body, grid=(n // tn,),
                          in_specs=[pl.BlockSpec((N_DEV, D), lambda j: (0, 0)), pl.BlockSpec((D, tn), lambda j: (0, j)), pl.BlockSpec((1, tn), lambda j: (0, j))],
                          out_specs=pl.BlockSpec((N_DEV, tn), lambda j: (0, j)), out_shape=_sds((N_DEV, n), F32),
                          compiler_params=_params(1, 32 << 20), name="mod_part")(c_all, w_shard, b_shard)


def _adam_math(g, w, m, v):
    c1 = 1.0 / (1.0 - ADAM_B1 ** ADAM_STEP)
    c2 = 1.0 / (1.0 - ADAM_B2 ** ADAM_STEP)
    m2 = ADAM_B1 * m + (1.0 - ADAM_B1) * g
    v2 = ADAM_B2 * v + (1.0 - ADAM_B2) * (g * g)
    return -ADAM_LR * ((m2 * c1) / (jnp.sqrt(v2 * c2) + ADAM_EPS) + ADAM_WD * w), m2, v2


def _update_w_ada(c_all_t, dmod_cols, w, m, v, deps=()):
    n = dmod_cols.shape[1]
    tn = 256

    def body(c_ref, d_ref, w_ref, m_ref, v_ref, *rest):
        g_ref, dl_ref, m2_ref, v2_ref = rest[len(deps):]
        cv = c_ref[...].astype(BF16).astype(F32)
        dv = d_ref[...].astype(BF16).astype(F32)
        g = cv[:, 0:1] * dv[0:1, :]
        for k in range(1, N_DEV):
            g = g + cv[:, k:k + 1] * dv[k:k + 1, :]
        g_ref[...] = g
        dl_ref[...], m2_ref[...], v2_ref[...] = _adam_math(g, w_ref[...], m_ref[...], v_ref[...])

    blk = pl.BlockSpec((D, tn), lambda j: (0, j))
    return pl.pallas_call(body, grid=(n // tn,),
                          in_specs=[pl.BlockSpec((D, N_DEV), lambda j: (0, 0)), pl.BlockSpec((N_DEV, tn), lambda j: (0, j)), blk, blk, blk]
                          + [pl.BlockSpec(memory_space=pl.ANY)] * len(deps),
                          out_specs=[blk] * 4, out_shape=[_sds((D, n), F32)] * 4,
                          compiler_params=_params(1, 48 << 20), name="adamw_w_ada")(c_all_t, dmod_cols, w, m, v, *deps)


def _row_tile(r, c, max_elems=1 << 18):
    if r * c <= max_elems or r % 8:
        return r
    best = 8
    for t in range(8, r + 1, 8):
        if r % t == 0 and t * c <= max_elems:
            best = t
    return best


WIDE_TILE = 5 << 17
PAIR_TILE = 3 << 19


def _adamw(pieces, w, m, v, name, own=None, max_elems=1 << 18):
    p, r, c = pieces.shape
    tr = _row_tile(r, c, max_elems)

    def body(*refs):
        if own is None:
            p_ref, w_ref, m_ref, v_ref, *outs = refs
            g = p_ref[0].astype(F32)
        else:
            o_ref, p_ref, w_ref, m_ref, v_ref, *outs = refs
            g = o_ref[...].astype(F32) + p_ref[0].astype(F32)
        for k in range(1, p):
            g = g + p_ref[k].astype(F32)
        outs[0][...] = g
        outs[1][...], outs[2][...], outs[3][...] = _adam_math(g, w_ref[...], m_ref[...], v_ref[...])

    blk = pl.BlockSpec((tr, c), lambda i: (i, 0))
    lead = [] if own is None else [own]
    return pl.pallas_call(body, grid=(r // tr,), in_specs=[blk] * len(lead) + [pl.BlockSpec((p, tr, c), lambda i: (0, i, 0)), blk, blk, blk],
                          out_specs=[blk] * 4, out_shape=[_sds((r, c), F32)] * 4,
                          compiler_params=_params(1, 48 << 20), name=name)(*lead, pieces, w, m, v)


def _my_coords():
    return lax.axis_index("x"), lax.axis_index("y"), lax.axis_index("c")


def _flip(coords, k):
    x, y, c = coords
    return (1 - x if k & 4 else x, 1 - y if k & 2 else y, 1 - c if k & 1 else c)


def _lin(coords):
    return 4 * coords[0] + 2 * coords[1] + coords[2]


def _exchange_small(x3, bcast, name):
    n = x3.shape[2]

    def body(x_ref, o_ref, send_sems, recv_sems):
        me = _my_coords()
        my_id = _lin(me)
        o_ref[pl.ds(my_id, 1)] = x_ref[pl.ds(0 if bcast else my_id, 1)]
        copies = []
        for k in range(1, N_DEV):
            peer = _flip(me, k)
            src = x_ref.at[0 if bcast else _lin(peer)]
            cp = pltpu.make_async_remote_copy(src_ref=src, dst_ref=o_ref.at[my_id], send_sem=send_sems.at[k], recv_sem=recv_sems.at[k],
                                              device_id=peer, device_id_type=MESH)
            cp.start()
            copies.append(cp)
        for k in range(1, N_DEV):
            peer = _flip(me, k)
            pltpu.make_async_remote_copy(src_ref=x_ref.at[0], dst_ref=o_ref.at[_lin(peer)], send_sem=send_sems.at[k], recv_sem=recv_sems.at[k],
                                         device_id=peer, device_id_type=MESH).wait_recv()
        for cp in copies:
            cp.wait_send()

    vm = pl.BlockSpec(memory_space=pltpu.VMEM)
    return pl.pallas_call(body, in_specs=[vm], out_specs=vm, out_shape=_sds((N_DEV, 1, n), F32),
                          scratch_shapes=[pltpu.SemaphoreType.DMA((N_DEV,)), pltpu.SemaphoreType.DMA((N_DEV,))], name=name)(x3)


HBM_SPEC = pl.BlockSpec(memory_space=pltpu.HBM)
SEM_SPEC = pl.BlockSpec(memory_space=pltpu.SEMAPHORE)
ANY_SPEC = pl.BlockSpec(memory_space=pl.ANY)
DATAFLOW = pltpu.SideEffectType.DATAFLOW_SIDE_EFFECTING
GATHER_FLIPS = (1, 2, 4, 6)
PASS_FLIPS = (2, 4, 6)
TOKEN = (8, LANE)


def _hbm(t):
    return pltpu.with_memory_space_constraint(t, pltpu.HBM)


def _hbm_like(ts):
    return [pltpu.HBM(t.shape, t.dtype) for t in ts]


def _split_start(issue, srcs, lands, n_sem, name, deps=()):
    n, nb, nd = len(srcs), len(srcs) + len(lands), len(deps)

    def body(*refs):
        issue(refs[:n], refs[n:nb], refs[nb + nd], refs[nb + nd + 1])
        refs[-1][...] = jnp.zeros(TOKEN, F32)

    outs = pl.pallas_call(
        body, name=name,
        out_shape=(pltpu.SemaphoreType.DMA((n_sem,)), pltpu.SemaphoreType.DMA((n_sem,)), *_hbm_like(srcs), *_hbm_like(lands), _sds(TOKEN, F32)),
        in_specs=[HBM_SPEC] * nb + [ANY_SPEC] * nd,
        out_specs=(SEM_SPEC, SEM_SPEC, *[HBM_SPEC] * nb, pl.BlockSpec(memory_space=pltpu.VMEM)),
        input_output_aliases={i: 2 + i for i in range(nb)},
        compiler_params=pltpu.CompilerParams(has_side_effects=DATAFLOW))(*[_hbm(t) for t in srcs], *[_hbm(t) for t in lands], *deps)
    return dict(sems=outs[:2], thru=list(outs[2:2 + nb]), token=outs[-1], n=n)


def _split_wait(finish, handle, after, name):
    n = handle["n"]
    thru = handle["thru"]
    nb = len(thru)

    def body(*refs):
        finish(refs[:n], refs[n:nb], refs[nb], refs[nb + 1])

    outs = pl.pallas_call(
        body, name=name, out_shape=_hbm_like(thru), in_specs=[HBM_SPEC] * nb + [SEM_SPEC, SEM_SPEC] + [ANY_SPEC] * len(after),
        out_specs=[HBM_SPEC] * nb, input_output_aliases={i: i for i in range(nb)},
        compiler_params=pltpu.CompilerParams(has_side_effects=DATAFLOW))(*thru, *handle["sems"], *after)
    return list(outs[:n]), list(outs[n:])


def _gather_start(shards, name, deps=()):
    n = len(shards)
    my_id = _lin(_my_coords())
    lands = [lax.dynamic_update_slice(lax.empty((N_DEV,) + t.shape, t.dtype), t[None], (my_id, 0, 0)) for t in shards]

    def issue(src, land, send_sems, recv_sems):
        me = _my_coords()
        for w in range(n):
            for j, k in enumerate(GATHER_FLIPS):
                q = len(GATHER_FLIPS) * w + j
                pltpu.make_async_remote_copy(src_ref=src[w], dst_ref=land[w].at[_lin(me)], send_sem=send_sems.at[q], recv_sem=recv_sems.at[q],
                                             device_id=_flip(me, k), device_id_type=MESH).start()

    return _split_start(issue, shards, lands, len(GATHER_FLIPS) * n, name, deps)


def _gather_wait(handle, after, name):
    n = handle["n"]

    def finish(src, land, send_sems, recv_sems):
        me = _my_coords()
        for w in range(n):
            for j, k in enumerate(GATHER_FLIPS):
                q = len(GATHER_FLIPS) * w + j
                peer = _flip(me, k)
                cp = pltpu.make_async_remote_copy(src_ref=src[w], dst_ref=land[w].at[_lin(peer)], send_sem=send_sems.at[q], recv_sem=recv_sems.at[q],
                                                  device_id=peer, device_id_type=MESH)
                cp.wait_send()
                cp.wait_recv()

    return _split_wait(finish, handle, after, name)[1]


def _pass_copy(land, send_sems, recv_sems, w, j, arriving):
    me = _my_coords()
    blk = land[w].at[_lin(_flip(me, PASS_FLIPS[j] + (1 if arriving else 0)))]
    q = len(PASS_FLIPS) * w + j
    return pltpu.make_async_remote_copy(src_ref=blk, dst_ref=blk, send_sem=send_sems.at[q], recv_sem=recv_sems.at[q],
                                        device_id=_flip(me, 1), device_id_type=MESH)


def _pass_start(lands, name, deps=()):
    def issue(_, land, send_sems, recv_sems):
        for w in range(len(lands)):
            for j in range(len(PASS_FLIPS)):
                _pass_copy(land, send_sems, recv_sems, w, j, False).start()

    return _split_start(issue, [], lands, len(PASS_FLIPS) * len(lands), name, deps)


def _pass_wait(handle, after, name):
    def finish(_, land, send_sems, recv_sems):
        for w in range(len(handle["thru"])):
            for j in range(len(PASS_FLIPS)):
                _pass_copy(land, send_sems, recv_sems, w, j, False).wait_send()
                _pass_copy(land, send_sems, recv_sems, w, j, True).wait_recv()

    return _split_wait(finish, handle, after, name)[1]


def _gather_pass(lands, name):
    n = len(lands)
    n_p = len(PASS_FLIPS)

    def body(*refs):
        land = refs[n:2 * n]
        send_sems, recv_sems = refs[2 * n:]
        me = _my_coords()
        sibling = _flip(me, 1)
        sent = []
        for w in range(n):
            for j, k in enumerate(PASS_FLIPS):
                blk = land[w].at[_lin(_flip(me, k))]
                cp = pltpu.make_async_remote_copy(src_ref=blk, dst_ref=blk, send_sem=send_sems.at[n_p * w + j], recv_sem=recv_sems.at[n_p * w + j],
                                                  device_id=sibling, device_id_type=MESH)
                cp.start()
                sent.append(cp)
        for w in range(n):
            for j, k in enumerate(PASS_FLIPS):
                blk = land[w].at[_lin(_flip(me, k + 1))]
                pltpu.make_async_remote_copy(src_ref=blk, dst_ref=blk, send_sem=send_sems.at[n_p * w + j], recv_sem=recv_sems.at[n_p * w + j],
                                             device_id=sibling, device_id_type=MESH).wait_recv()
        for cp in sent:
            cp.wait_send()

    return pl.pallas_call(body, in_specs=[ANY_SPEC] * n, out_specs=[ANY_SPEC] * n, out_shape=[_sds(t.shape, t.dtype) for t in lands],
                          input_output_aliases={i: i for i in range(n)},
                          scratch_shapes=[pltpu.SemaphoreType.DMA((n_p * n,)), pltpu.SemaphoreType.DMA((n_p * n,))], name=name)(*lands)


CHIP_FLIPS = (0, 2, 4, 6)


def _pair_copy(src, land, send_sems, recv_sems, w, j):
    me = _my_coords()
    q = len(CHIP_FLIPS) * w + j
    return pltpu.make_async_remote_copy(src_ref=src[w].at[_lin(_flip(me, CHIP_FLIPS[j] + 1))], dst_ref=land[w].at[j], send_sem=send_sems.at[q],
                                        recv_sem=recv_sems.at[q], device_id=_flip(me, 1), device_id_type=MESH)


def _pair_exchange(grads, name):
    n = len(grads)

    def body(*refs):
        src, land = refs[:n], refs[n:2 * n]
        send_sems, recv_sems = refs[2 * n:]
        sent = [_pair_copy(src, land, send_sems, recv_sems, w, j) for w in range(n) for j in range(len(CHIP_FLIPS))]
        for cp in sent:
            cp.start()
        for cp in sent:
            cp.wait_recv()
        for cp in sent:
            cp.wait_send()

    outs = pl.pallas_call(body, in_specs=[ANY_SPEC] * n, out_specs=[ANY_SPEC] * n,
                          out_shape=[_sds((len(CHIP_FLIPS),) + g.shape[1:], g.dtype) for g in grads],
                          scratch_shapes=[pltpu.SemaphoreType.DMA((len(CHIP_FLIPS) * n,))] * 2, name=name)(*grads)
    return list(outs)


def _pair_start(grads, name, deps=()):
    n = len(grads)
    lands = [lax.empty((len(CHIP_FLIPS),) + g.shape[1:], g.dtype) for g in grads]

    def issue(src, land, send_sems, recv_sems):
        for w in range(n):
            for j in range(len(CHIP_FLIPS)):
                _pair_copy(src, land, send_sems, recv_sems, w, j).start()

    return _split_start(issue, grads, lands, len(CHIP_FLIPS) * n, name, deps)


def _pair_wait(handle, after, name):
    n = handle["n"]

    def finish(src, land, send_sems, recv_sems):
        for w in range(n):
            for j in range(len(CHIP_FLIPS)):
                cp = _pair_copy(src, land, send_sems, recv_sems, w, j)
                cp.wait_send()
                cp.wait_recv()

    return _split_wait(finish, handle, after, name)


def _pair_add(grad, theirs, name):
    p, r, c = theirs.shape
    tr = _row_tile(r, c, PAIR_TILE)
    me = _my_coords()
    ids = jnp.stack([_lin(_flip(me, k)) for k in CHIP_FLIPS]).astype(jnp.int32)

    def body(ids_ref, a_ref, b_ref, o_ref):
        o_ref[...] = (a_ref[...].astype(F32) + b_ref[...].astype(F32)).astype(o_ref.dtype)

    blk = pl.BlockSpec((None, tr, c), lambda j, i, ids_ref: (j, i, 0))
    return pl.pallas_call(
        body, out_shape=_sds((p, r, c), theirs.dtype), compiler_params=_params(2), name=name,
        grid_spec=pltpu.PrefetchScalarGridSpec(
            num_scalar_prefetch=1, grid=(p, r // tr),
            in_specs=[pl.BlockSpec((None, tr, c), lambda j, i, ids_ref: (ids_ref[j], i, 0)), blk], out_specs=blk))(ids, grad, theirs)


def _chips_start(parts, name, deps=()):
    n = len(parts)
    n_c = len(CHIP_FLIPS) - 1
    lands = [lax.empty((n_c,) + t.shape[1:], t.dtype) for t in parts]

    def issue(src, land, send_sems, recv_sems):
        me = _my_coords()
        for w in range(n):
            for j in range(1, n_c + 1):
                q = n_c * w + j - 1
                pltpu.make_async_remote_copy(src_ref=src[w].at[j], dst_ref=land[w].at[j - 1], send_sem=send_sems.at[q], recv_sem=recv_sems.at[q],
                                             device_id=_flip(me, CHIP_FLIPS[j]), device_id_type=MESH).start()

    return _split_start(issue, parts, lands, n_c * n, name, deps)


def _chips_wait(handle, after, name):
    n = handle["n"]
    n_c = len(CHIP_FLIPS) - 1

    def finish(src, land, send_sems, recv_sems):
        me = _my_coords()
        for w in range(n):
            for j in range(1, n_c + 1):
                q = n_c * w + j - 1
                cp = pltpu.make_async_remote_copy(src_ref=src[w].at[j], dst_ref=land[w].at[j - 1], send_sem=send_sems.at[q], recv_sem=recv_sems.at[q],
                                                  device_id=_flip(me, CHIP_FLIPS[j]), device_id_type=MESH)
                cp.wait_send()
                cp.wait_recv()

    return _split_wait(finish, handle, after, name)


def _after(t, *tokens):
    for tok in tokens:
        t = t + tok[0:1, 0:1]
    return t


def _rope_tables(positions):
    half = ROT // 2
    inv_freq = ROPE_THETA ** (-jnp.arange(0, ROT, 2, dtype=F32) / ROT)
    ang = positions.astype(F32).reshape(-1, 1) * inv_freq
    cos, sin = jnp.cos(ang), jnp.sin(ang)
    s = ang.shape[0]
    pad = jnp.zeros((s, HEAD_DIM - ROT), F32)
    zero = jnp.zeros((s, half), F32)
    two = lambda t: jnp.concatenate([t, t], axis=1)
    return (two(jnp.concatenate([cos, cos, pad + 1.0], axis=1)), two(jnp.concatenate([-sin, zero, pad], axis=1)),
            two(jnp.concatenate([zero, sin, pad], axis=1)))


def _local_step(x, tgt, tabs, mod, sinks_pad, hl, hg_norm, g_pre_mix, g_post_mix, g_pre_ffn, g_post_ffn, weights, prefetch, scatter, scatter_on):
    s = x.shape[0]
    h1 = _pre_fwd(x, g_pre_mix, mod, 1, 0, "pre_mix_fwd")
    (w_in_a,) = weights("in_a", h1)
    proj = _mm_nt(h1[:, :D // 2], w_in_a, 256, IN_COLS // 2, D // 2, F32, "proj_mm_a")
    (w_in_b,) = weights("in_b", proj)
    proj = _mm_nt(h1[:, D // 2:], w_in_b, 256, IN_COLS // 2, D // 2, F32, "proj_mm_b", add=proj)
    att = _attn_fwd(proj, tabs, _after(sinks_pad, prefetch("mix", proj)))
    o_raw, states = _hgrn_fwd(proj, hl)
    ohg = _hgout_fwd(o_raw, proj, hg_norm)
    w_attn_dm, w_hgrn_dm, w_out = weights("mix", ohg)
    natural = lambda w_dm: w_dm.transpose(1, 0, 2).reshape(w_dm.shape[1], D)
    pieces = lambda g: g.reshape(g.shape[0], N_DEV, D // N_DEV).transpose(1, 0, 2)
    w_attn, w_hgrn = natural(w_attn_dm), natural(w_hgrn_dm)
    y_a = _mm_nn(att, w_attn, s, 512, ATT_W, F32, "attn_proj_mm")
    y_h = _mm_nn(ohg, w_hgrn, s, 512, HG_W, F32, "hgrn_proj_mm")
    merged = _merge_fwd(y_a, y_h, proj)
    y = _mm_nn(merged, w_out, s, 512, D, F32, "out_mm")
    x1 = _post_fwd(x, y, g_post_mix, mod, 2, "post_mix_fwd")
    h2 = _pre_fwd(x1, g_pre_ffn, _after(mod, prefetch("ffn_in", x1)), 4, 3, "pre_ffn_fwd")
    (w_ffn_in_dm,) = weights("ffn_in", h2)
    gu = _mm_nn_dm(h2, w_ffn_in_dm, s // 2, F32, "ffn_in_mm")
    act = _swiglu_fwd(gu, deps=[prefetch("ffn_out", gu)])
    (w_ffn_out,) = weights("ffn_out", act)
    y2 = _mm_nn(act, w_ffn_out, 512, 512, FFN, F32, "ffn_out_mm")
    err, loss, dy2, d_gate2, dg_post_ffn = _post_loss_bwd(x1, y2, g_post_ffn, mod, 5, tgt, "post_ffn_loss_bwd")
    gw_ffn_out = _mm_tn(act, dy2, 512, D, BF16, "ffn_out_dw")
    t_pair = scatter([gw_ffn_out.reshape(N_DEV, FFN // N_DEV, D)], "ffn_out")
    d_act = _mm_nt(dy2, w_ffn_out, s, 512, D, F32, "ffn_out_dx", deps=[t_pair])
    dgu = _swiglu_bwd(d_act, gu)
    t_out = scatter_on("ffn_out", dgu)
    gw_ffn_in = _mm_tn_dm(h2, dgu, 1024, BF16, "ffn_in_dw")
    t_pair = scatter([gw_ffn_in], "ffn_in")
    dh2 = _mm_nt_dm(dgu, w_ffn_in_dm, s, 1024, F32, "ffn_in_dx", deps=[t_pair])
    mod = _after(mod, t_out)
    dx1, d_shift2, d_scale2, dg_pre_ffn = _pre_bwd(dh2, x1, err, g_pre_ffn, mod, 4, "pre_ffn_bwd")
    dy, d_gate1, dg_post_mix = _post_bwd(dx1, y, g_post_mix, mod, 2, "post_mix_bwd")
    t_in = scatter_on("ffn_in", dy)
    d_merged = _mm_nt(dy, w_out, s, 512, D, F32, "out_dx")
    gw_out = _mm_tn(merged, dy, 512, D, BF16, "out_dw")
    dy_a, dy_h, d_gate_a, d_gate_h = _merge_bwd(d_merged, y_a, y_h, proj)
    gw_attn = pieces(_mm_tn(att, dy_a, 512, D, BF16, "attn_proj_dw"))
    gw_hgrn = pieces(_mm_tn(ohg, dy_h, 512, D, BF16, "hgrn_proj_dw"))
    t_pair = scatter([gw_attn, gw_hgrn, gw_out.reshape(N_DEV, D // N_DEV, D)], "mix")
    d_att = _mm_nt(dy_a, w_attn, s, 512, D, F32, "attn_proj_dx")
    d_ohg = _mm_nt(dy_h, w_hgrn, s, 512, D, F32, "hgrn_proj_dx", deps=[t_pair])
    d_o, d_gh, d_hg_norm = _hgout_bwd(d_ohg, o_raw, proj, _after(hg_norm, t_in))
    d_qh, d_fh, d_ih, d_hl = _hgrn_bwd(proj, hl, states, d_o)
    t_mix = scatter_on("mix", d_qh)
    d_qa, d_ka, d_va, d_sinks = _attn_bwd(proj, tabs, _after(sinks_pad, t_mix), d_att)
    d_proj = jnp.concatenate([d_qa, d_ka.astype(BF16), d_va.astype(BF16), d_qh, d_fh, d_ih, d_gh, d_gate_a, d_gate_h], axis=1)
    dh1 = jnp.concatenate([_mm_nn(d_proj, w_half, s // 2, 512, IN_COLS // 2, F32, "proj_dx_" + tag)
                           for tag, w_half in (("a", w_in_a), ("b", w_in_b))], axis=1)
    grad_x, d_shift1, d_scale1, dg_pre_mix = _pre_bwd(dh1, x, dx1, g_pre_mix, mod, 1, "pre_mix_bwd")
    d_mod = jnp.concatenate([d_shift1, d_scale1, d_gate1, d_shift2, d_scale2, d_gate2], axis=1)
    small = [d_mod, dg_pre_mix, dg_post_mix, dg_pre_ffn, dg_post_ffn, d_hl.reshape(1, 2 * HG_W), d_hg_norm, d_sinks]
    return loss, grad_x, small, h1, d_proj


def kernel(x, c, positions, w_ada, b_ada, g_pre_mix, g_post_mix, g_pre_ffn, g_post_ffn, w_in, attn_sinks, w_attn_proj, hg_lower_bounds, hg_norm, w_hgrn_proj, w_out, w_ffn_in, w_ffn_out, loss_target, m_w_ada, m_b_ada, m_g_pre_mix, m_g_post_mix, m_g_pre_ffn, m_g_post_ffn, m_w_in, m_attn_sinks, m_w_attn_proj, m_hg_lower_bounds, m_hg_norm, m_w_hgrn_proj, m_w_out, m_w_ffn_in, m_w_ffn_out, v_w_ada, v_b_ada, v_g_pre_mix, v_g_post_mix, v_g_pre_ffn, v_g_post_ffn, v_w_in, v_attn_sinks, v_w_attn_proj, v_hg_lower_bounds, v_hg_norm, v_w_hgrn_proj, v_w_out, v_w_ffn_in, v_w_ffn_out):
    my_id = _lin(_my_coords())
    s = x.shape[1]
    n_ada = w_ada.shape[2]

    c_all = _exchange_small(c.reshape(1, 1, D), True, "gather_c").reshape(N_DEV, D)
    b_cols = lax.dynamic_slice(b_ada, (0, my_id * n_ada), (1, n_ada))
    mod_part = _mod_part(c_all, w_ada[0], b_cols)
    mod = _exchange_small(mod_part.reshape(N_DEV, 1, n_ada), False, "scatter_mod").reshape(1, N_MOD * D)
    groups = {"in_a": [w_in[0].T[:, :D // 2]], "in_b": [w_in[0].T[:, D // 2:]], "mix": [w_attn_proj[0], w_hgrn_proj[0], w_out[0]],
              "ffn_in": [w_ffn_in[0]], "ffn_out": [w_ffn_out[0]]}

    def start(group, dep):
        shards, dep = lax.optimization_barrier((groups[group], dep))
        return _gather_start([t.astype(BF16) for t in shards], "gather_start_" + group, deps=[dep])

    gathers = {"in_a": start("in_a", mod)}
    gathers["in_b"] = start("in_b", gathers["in_a"]["token"])
    gathers["mix"] = start("mix", gathers["in_b"]["token"])
    gathers["ffn_in"] = start("ffn_in", gathers["mix"]["token"])
    gathers["ffn_out"] = start("ffn_out", gathers["ffn_in"]["token"])

    passes = {}

    def prefetch(group, after):
        lands = _gather_wait(gathers[group], [after], "gather_wait_" + group)
        passes[group] = _pass_start(lands, "gather_pass_start_" + group)
        return passes[group]["token"]

    def weights(group, after):
        if group in passes:
            lands = _pass_wait(passes[group], [after], "gather_pass_wait_" + group)
        else:
            after = [after, gathers["ffn_out"]["token"]]
            lands = _gather_pass(_gather_wait(gathers[group], after, "gather_wait_" + group), "gather_pass_" + group)
        if group in ("in_a", "in_b"):
            return (lands[0].reshape(IN_COLS, D // 2),)
        if group == "mix":
            return lands[0], lands[1], lands[2].reshape(D, D)
        return (lands[0],) if group == "ffn_in" else (lands[0].reshape(FFN, D),)

    pairs, scatters = {}, {}

    def scatter(grads, group):
        pairs[group] = _pair_start(grads, "scatter_pair_" + group)
        return pairs[group]["token"]

    def scatter_on(group, after):
        if group in pairs:
            local, theirs = _pair_wait(pairs[group], [after], "scatter_pair_wait_" + group)
        else:
            local, theirs = after, _pair_exchange(after, "scatter_pair_" + group)
        parts = [_pair_add(g, t, "scatter_pair_add_%s_%d" % (group, k)) for k, (g, t) in enumerate(zip(local, theirs))]
        scatters[group] = _chips_start(parts, "scatter_start_" + group)
        return scatters[group]["token"]

    sinks_pad = jnp.pad(attn_sinks, ((0, 0), (0, LANE - ATT_HEADS)))
    loss, grad_x, small, h1, d_proj = _local_step(
        x[0], loss_target[0], _rope_tables(positions), mod, sinks_pad, hg_lower_bounds, hg_norm, g_pre_mix, g_post_mix, g_pre_ffn, g_post_ffn,
        weights, prefetch, scatter, scatter_on)
    loss = lax.psum(loss[0, 0], ("x", "y", "c"))

    sizes = [t.shape[1] for t in small]
    parts = _exchange_small(jnp.concatenate(small, axis=1).reshape(1, 1, sum(sizes)), True, "gather_small_grads")
    dep = parts
    for half, cols in (("in_a", slice(0, D // 2)), ("in_b", slice(D // 2, D))):
        gw_half = _mm_tn(d_proj, h1[:, cols], 256, D // 2, BF16, "proj_dw_" + half, deps=[dep])
        dep = scatter_on(half, [gw_half.reshape(N_DEV, IN_COLS // N_DEV, D // 2)])
    offs = [sum(sizes[:k]) for k in range(len(sizes))]
    piece = lambda k, n=None: parts[:, :, offs[k]:offs[k] + (sizes[k] if n is None else n)]
    small_w = [(piece(0), b_ada, m_b_ada, v_b_ada), (piece(1), g_pre_mix, m_g_pre_mix, v_g_pre_mix),
               (piece(2), g_post_mix, m_g_post_mix, v_g_post_mix), (piece(3), g_pre_ffn, m_g_pre_ffn, v_g_pre_ffn),
               (piece(4), g_post_ffn, m_g_post_ffn, v_g_post_ffn),
               (piece(5).reshape(N_DEV, 2, HG_W), hg_lower_bounds, m_hg_lower_bounds, v_hg_lower_bounds),
               (piece(6), hg_norm, m_hg_norm, v_hg_norm), (piece(7, ATT_HEADS), attn_sinks, m_attn_sinks, v_attn_sinks)]
    names = ["b_ada", "g_pre_mix", "g_post_mix", "g_pre_ffn", "g_post_ffn", "hg_lower_bounds", "hg_norm", "attn_sinks"]
    res = {n: _adamw(p, w, m, v, "adamw_" + n) for n, (p, w, m, v) in zip(names, small_w)}

    dmod_cols = lax.dynamic_slice(parts.reshape(N_DEV, -1), (0, my_id * n_ada), (N_DEV, n_ada))
    res["w_ada"] = list(_update_w_ada(c_all.T, dmod_cols, w_ada[0], m_w_ada[0], v_w_ada[0], deps=[scatters["in_b"]["token"]]))

    big = {"ffn_out": [("w_ffn_out", w_ffn_out, m_w_ffn_out, v_w_ffn_out)], "ffn_in": [("w_ffn_in", w_ffn_in, m_w_ffn_in, v_w_ffn_in)],
           "mix": [("w_attn_proj", w_attn_proj, m_w_attn_proj, v_w_attn_proj), ("w_hgrn_proj", w_hgrn_proj, m_w_hgrn_proj, v_w_hgrn_proj),
                   ("w_out", w_out, m_w_out, v_w_out)]}
    after = [scatters["in_b"]["token"]]
    for group, members in big.items():
        local, lands = _chips_wait(scatters[group], after, "scatter_wait_" + group)
        for (n, w, m, v), mine, land in zip(members, local, lands):
            res[n] = _adamw(land, w[0], m[0], v[0], "adamw_" + n, own=mine[0])
            after = after + [res[n][1]]
    after = [res[n][1] for n in res]
    halves = [_chips_wait(scatters[half], after, "scatter_wait_" + half) for half in ("in_a", "in_b")]
    own = jnp.concatenate([local[0][0] for local, _ in halves], axis=1)
    land = jnp.concatenate([lands[0] for _, lands in halves], axis=2)
    res["w_in"] = [t.T for t in _adamw(land, w_in[0].T, m_w_in[0].T, v_w_in[0].T, "adamw_w_in", own=own, max_elems=WIDE_TILE)]

    order = ["w_ada", "b_ada", "g_pre_mix", "g_post_mix", "g_pre_ffn", "g_post_ffn", "w_in", "attn_sinks", "w_attn_proj",
             "hg_lower_bounds", "hg_norm", "w_hgrn_proj", "w_out", "w_ffn_in", "w_ffn_out"]
    lead = {"w_ada", "w_in", "w_attn_proj", "w_hgrn_proj", "w_out", "w_ffn_in", "w_ffn_out"}
    outs = [loss, grad_x[None]]
    for k in range(4):
        outs += [res[n][k][None] if n in lead else res[n][k] for n in order]
    return tuple(outs)
```

```python
import functools

import jax
import jax.numpy as jnp
from jax import lax
from jax.experimental import pallas as pl
from jax.experimental.pallas import tpu as pltpu

F32 = jnp.float32
BF16 = jnp.bfloat16

N_DEV = 8
D = 2048
ATT_HEADS = 16
KV_HEADS = 2
HEAD_DIM = 64
GROUP = ATT_HEADS // KV_HEADS
ATT_W = ATT_HEADS * HEAD_DIM
BLK = 128
ROT = HEAD_DIM // 4
ROPE_THETA = 500000.0
HG_HEADS = 8
HG_K = 128
HG_W = HG_HEADS * HG_K
CHUNK = 64
SUB = 16
FFN = 5632
N_MOD = 6
EPS = 1e-6
LANE = 128
Q_A, K_A, V_A, Q_H, F_H, I_H, G_H, GT_A, GT_H, IN_COLS = 0, 1024, 1152, 1280, 2304, 3328, 4352, 5376, 7424, 9472

ADAM_LR, ADAM_B1, ADAM_B2, ADAM_EPS, ADAM_WD, ADAM_STEP = 0.001, 0.9, 0.999, 1e-08, 0.01, 10

TR = 256
HG_TB = 512
VMEM_BIG = 56 << 20
MESH = pl.DeviceIdType.MESH


def _sds(shape, dtype):
    return jax.ShapeDtypeStruct(shape, dtype)


def _params(n_axes, vmem=None):
    return pltpu.CompilerParams(dimension_semantics=("arbitrary",) * n_axes, vmem_limit_bytes=vmem)


def _sig(t):
    return 1.0 / (1.0 + jnp.exp(-t))


def _dot(a, b, dims):
    return lax.dot_general(a, b, (dims, ((), ())), preferred_element_type=F32)


NN = ((1,), (0,))
NT = ((1,), (1,))
TN = ((0,), (0,))


def _matmul(a, b, a_spec, b_spec, o_spec, out_shape, grid, dims, acc_shape, name, deps=(), add=None):
    nk = grid[2]
    nd = len(deps)
    extra = [] if add is None else [add]

    def body(a_ref, b_ref, *rest):
        o_ref, scratch = rest[nd + len(extra)], rest[nd + len(extra) + 1:]
        part = _dot(a_ref[...], b_ref[...], dims)
        if add is not None:
            assert nk == 1
            part = part + rest[nd][...]
        if nk == 1:
            o_ref[...] = part.astype(o_ref.dtype)
        else:
            acc = scratch[0]
            k = pl.program_id(2)

            @pl.when(k == 0)
            def _():
                acc[...] = part

            @pl.when(k > 0)
            def _():
                acc[...] += part

            @pl.when(k == nk - 1)
            def _():
                o_ref[...] = acc[...].astype(o_ref.dtype)

    return pl.pallas_call(
        body, grid=grid, in_specs=[a_spec, b_spec] + [pl.BlockSpec(memory_space=pl.ANY)] * nd + [o_spec] * len(extra),
        out_specs=o_spec, out_shape=out_shape, scratch_shapes=[pltpu.VMEM(acc_shape, F32)] if nk > 1 else [],
        input_output_aliases={2 + nd: 0} if extra else {},
        compiler_params=_params(3, VMEM_BIG), name=name)(a, b, *deps, *extra)


def _mm_nn(a, b, tm, tn, tk, out_dtype, name):
    m, k = a.shape
    n = b.shape[1]
    return _matmul(a, b, pl.BlockSpec((tm, tk), lambda j, i, kk: (i, kk)), pl.BlockSpec((tk, tn), lambda j, i, kk: (kk, j)),
                   pl.BlockSpec((tm, tn), lambda j, i, kk: (i, j)), _sds((m, n), out_dtype),
                   (n // tn, m // tm, k // tk), NN, (tm, tn), name)


def _mm_nn_dm(a, b, tm, out_dtype, name):
    m, k = a.shape
    n = b.shape[2]
    return _matmul(a, b, pl.BlockSpec((tm, k), lambda j, i, kk: (i, 0)), pl.BlockSpec((None, k, n), lambda j, i, kk: (j, 0, 0)),
                   pl.BlockSpec((tm, n), lambda j, i, kk: (i, j)), _sds((m, N_DEV * n), out_dtype),
                   (N_DEV, m // tm, 1), NN, (tm, n), name)


def _mm_nt(a, b, tm, tn, tk, out_dtype, name, deps=(), add=None, a_col=0):
    m = a.shape[0]
    n, k = b.shape
    return _matmul(a, b, pl.BlockSpec((tm, tk), lambda j, i, kk: (i, kk + a_col * (k // tk))), pl.BlockSpec((tn, tk), lambda j, i, kk: (j, kk)),
                   pl.BlockSpec((tm, tn), lambda j, i, kk: (i, j)), _sds((m, n), out_dtype),
                   (n // tn, m // tm, k // tk), NT, (tm, tn), name, deps, add)


def _mm_nt_dm(a, b, tm, tn, out_dtype, name, deps=()):
    m = a.shape[0]
    n_out, n = b.shape[1], b.shape[2]
    return _matmul(a, b, pl.BlockSpec((tm, n), lambda j, i, kk: (i, kk)), pl.BlockSpec((None, tn, n), lambda j, i, kk: (kk, j, 0)),
                   pl.BlockSpec((tm, tn), lambda j, i, kk: (i, j)), _sds((m, n_out), out_dtype),
                   (n_out // tn, m // tm, N_DEV), NT, (tm, tn), name, deps)


def _mm_tn(a, b, tm, tn, out_dtype, name, deps=(), b_col=None):
    s, m = a.shape
    n = b.shape[1] if b_col is None else tn
    first = 0 if b_col is None else b_col
    return _matmul(a, b, pl.BlockSpec((s, tm), lambda j, i, kk: (0, i)), pl.BlockSpec((s, tn), lambda j, i, kk: (0, j + first)),
                   pl.BlockSpec((tm, tn), lambda j, i, kk: (i, j)), _sds((m, n), out_dtype),
                   (n // tn, m // tm, 1), TN, (tm, tn), name, deps)


def _mm_tn_dm(a, b, tm, out_dtype, name):
    s, m = a.shape
    n = b.shape[1] // N_DEV
    return _matmul(a, b, pl.BlockSpec((s, tm), lambda j, i, kk: (0, i)), pl.BlockSpec((s, n), lambda j, i, kk: (0, j)),
                   pl.BlockSpec((None, tm, n), lambda j, i, kk: (j, i, 0)), _sds((N_DEV, m, n), out_dtype),
                   (N_DEV, m // tm, 1), TN, (tm, n), name)


def _row_spec():
    return pl.BlockSpec((TR, D), lambda i: (i, 0))


def _vec_spec(k=0):
    return pl.BlockSpec((1, D), lambda i: (0, k))


def _acc_rows(ref, first, val):
    @pl.when(first)
    def _():
        ref[...] = val

    @pl.when(jnp.logical_not(first))
    def _():
        ref[...] += val


def _pre_fwd(x, g, mod, k_scale, k_shift, name):
    s = x.shape[0]

    def body(x_ref, g_ref, sc_ref, sh_ref, h_ref):
        xv = x_ref[...]
        r = lax.rsqrt(jnp.mean(xv * xv, axis=-1, keepdims=True) + EPS)
        n = xv * r * g_ref[...]
        h_ref[...] = (n * (1.0 + sc_ref[...]) + sh_ref[...]).astype(h_ref.dtype)

    return pl.pallas_call(body, grid=(s // TR,), in_specs=[_row_spec(), _vec_spec(), _vec_spec(k_scale), _vec_spec(k_shift)],
                          out_specs=_row_spec(), out_shape=_sds((s, D), BF16), compiler_params=_params(1), name=name)(x, g, mod, mod)


def _post_fwd(x, y, g, mod, k_gate, name):
    s = x.shape[0]

    def body(x_ref, y_ref, g_ref, gt_ref, o_ref):
        yv = y_ref[...]
        r = lax.rsqrt(jnp.mean(yv * yv, axis=-1, keepdims=True) + EPS)
        o_ref[...] = x_ref[...] + gt_ref[...] * (yv * r * g_ref[...])

    return pl.pallas_call(body, grid=(s // TR,), in_specs=[_row_spec(), _row_spec(), _vec_spec(), _vec_spec(k_gate)],
                          out_specs=_row_spec(), out_shape=_sds((s, D), F32), compiler_params=_params(1), name=name)(x, y, g, mod)


def _post_loss_bwd(x, y, g, mod, k_gate, tgt, name):
    s = x.shape[0]

    def body(x_ref, y_ref, g_ref, gt_ref, t_ref, e_ref, loss_ref, dy_ref, dgt_ref, dg_ref):
        first = pl.program_id(0) == 0
        yv, gv, gate = y_ref[...], g_ref[...], gt_ref[...]
        r = lax.rsqrt(jnp.mean(yv * yv, axis=-1, keepdims=True) + EPS)
        yh = yv * r
        err = x_ref[...] + gate * (yh * gv) - t_ref[...]
        e = err * (1.0 / D)
        e_ref[...] = e
        _acc_rows(loss_ref, first, 0.5 * jnp.sum(jnp.mean(err * err, axis=-1, keepdims=True), axis=0, keepdims=True))
        dn = e * gate
        dgn = dn * gv
        dy_ref[...] = (r * (dgn - yh * jnp.mean(dgn * yh, axis=-1, keepdims=True))).astype(dy_ref.dtype)
        _acc_rows(dgt_ref, first, jnp.sum(e * (yh * gv), axis=0, keepdims=True))
        _acc_rows(dg_ref, first, jnp.sum(dn * yh, axis=0, keepdims=True))

    return pl.pallas_call(body, grid=(s // TR,),
                          in_specs=[_row_spec(), _row_spec(), _vec_spec(), _vec_spec(k_gate), _row_spec()],
                          out_specs=[_row_spec(), pl.BlockSpec((1, 1), lambda i: (0, 0)), _row_spec(), _vec_spec(), _vec_spec()],
                          out_shape=[_sds((s, D), F32), _sds((1, 1), F32), _sds((s, D), BF16), _sds((1, D), F32), _sds((1, D), F32)],
                          compiler_params=_params(1), name=name)(x, y, g, mod, tgt)


def _pre_bwd(dh_parts, x, res, g, mod, k_scale, name):
    s = x.shape[0]
    n_p = len(dh_parts)

    def body(*refs):
        x_ref, res_ref, g_ref, sc_ref, dx_ref, dsh_ref, dsc_ref, dg_ref = refs[n_p:]
        first = pl.program_id(0) == 0
        dh_v = jnp.concatenate([r[...] for r in refs[:n_p]], axis=1)
        xv, gv = x_ref[...], g_ref[...]
        r = lax.rsqrt(jnp.mean(xv * xv, axis=-1, keepdims=True) + EPS)
        xh = xv * r
        dn = dh_v * (1.0 + sc_ref[...])
        dgn = dn * gv
        dx_ref[...] = res_ref[...] + r * (dgn - xh * jnp.mean(dgn * xh, axis=-1, keepdims=True))
        _acc_rows(dsh_ref, first, jnp.sum(dh_v, axis=0, keepdims=True))
        _acc_rows(dsc_ref, first, jnp.sum(dh_v * (xh * gv), axis=0, keepdims=True))
        _acc_rows(dg_ref, first, jnp.sum(dn * xh, axis=0, keepdims=True))

    return pl.pallas_call(body, grid=(s // TR,),
                          in_specs=[pl.BlockSpec((TR, t.shape[1]), lambda i: (i, 0)) for t in dh_parts]
                          + [_row_spec(), _row_spec(), _vec_spec(), _vec_spec(k_scale)],
                          out_specs=[_row_spec(), _vec_spec(), _vec_spec(), _vec_spec()],
                          out_shape=[_sds((s, D), F32)] + [_sds((1, D), F32)] * 3,
                          compiler_params=_params(1), name=name)(*dh_parts, x, res, g, mod)


def _post_bwd(dx, y, g, mod, k_gate, name):
    s = y.shape[0]

    def body(dx_ref, y_ref, g_ref, gt_ref, dy_ref, dgt_ref, dg_ref):
        first = pl.program_id(0) == 0
        yv, dxv, gv = y_ref[...], dx_ref[...], g_ref[...]
        r = lax.rsqrt(jnp.mean(yv * yv, axis=-1, keepdims=True) + EPS)
        yh = yv * r
        dn = dxv * gt_ref[...]
        dgn = dn * gv
        dy_ref[...] = (r * (dgn - yh * jnp.mean(dgn * yh, axis=-1, keepdims=True))).astype(dy_ref.dtype)
        _acc_rows(dgt_ref, first, jnp.sum(dxv * (yh * gv), axis=0, keepdims=True))
        _acc_rows(dg_ref, first, jnp.sum(dn * yh, axis=0, keepdims=True))

    return pl.pallas_call(body, grid=(s // TR,), in_specs=[_row_spec(), _row_spec(), _vec_spec(), _vec_spec(k_gate)],
                          out_specs=[_row_spec(), _vec_spec(), _vec_spec()],
                          out_shape=[_sds((s, D), BF16), _sds((1, D), F32), _sds((1, D), F32)],
                          compiler_params=_params(1), name=name)(dx, y, g, mod)


SW_TN = 1408
SW_TR = 512
TALL = 2048


def _swiglu_fwd(gu, deps=()):
    s = gu.shape[0]
    nb = FFN // SW_TN

    def body(g_ref, u_ref, *rest):
        a_ref = rest[len(deps)]
        gv = g_ref[...]
        a_ref[...] = (gv * _sig(gv) * u_ref[...]).astype(a_ref.dtype)

    return pl.pallas_call(body, grid=(s // SW_TR, nb),
                          in_specs=[pl.BlockSpec((SW_TR, SW_TN), lambda i, j: (i, j)), pl.BlockSpec((SW_TR, SW_TN), lambda i, j: (i, j + nb))]
                          + [pl.BlockSpec(memory_space=pl.ANY)] * len(deps),
                          out_specs=pl.BlockSpec((SW_TR, SW_TN), lambda i, j: (i, j)), out_shape=_sds((s, FFN), BF16),
                          compiler_params=_params(2, 48 << 20), name="swiglu_fwd")(gu, gu, *deps)


def _swiglu_bwd(dact, gu):
    s = gu.shape[0]
    nb = FFN // SW_TN
    n_steps = (s // SW_TR) * nb

    def body(da_ref, g_ref, u_ref, o_ref, buf, sems):
        i, j = pl.program_id(0), pl.program_id(1)
        step = i * nb + j
        slot = step % 2

        def tiles(sl):
            rows = pl.ds(pl.multiple_of(i * SW_TR, SW_TR), SW_TR)
            return [pltpu.make_async_copy(buf.at[sl, h], o_ref.at[rows, pl.ds(pl.multiple_of((j + nb * h) * SW_TN, LANE), SW_TN)], sems.at[sl, h])
                    for h in range(2)]

        @pl.when(step >= 2)
        def _():
            for cp in tiles(slot):
                cp.wait()

        gv, da = g_ref[...], da_ref[...]
        sg = _sig(gv)
        buf[slot, 0] = (da * u_ref[...] * (sg * (1.0 + gv * (1.0 - sg)))).astype(buf.dtype)
        buf[slot, 1] = (da * (gv * sg)).astype(buf.dtype)
        for cp in tiles(slot):
            cp.start()

        @pl.when(step == n_steps - 1)
        def _():
            for cp in tiles(slot) + (tiles(1 - slot) if n_steps > 1 else []):
                cp.wait()

    blk = lambda f: pl.BlockSpec((SW_TR, SW_TN), f)
    return pl.pallas_call(body, grid=(s // SW_TR, nb),
                          in_specs=[blk(lambda i, j: (i, j)), blk(lambda i, j: (i, j)), blk(lambda i, j: (i, j + nb))],
                          out_specs=pl.BlockSpec(memory_space=pl.ANY), out_shape=_sds((s, 2 * FFN), BF16),
                          scratch_shapes=[pltpu.VMEM((2, 2, SW_TR, SW_TN), BF16), pltpu.SemaphoreType.DMA((2, 2))],
                          compiler_params=_params(2, 48 << 20), name="swiglu_bwd")(dact, gu, gu)


MG_TN = 256


def _merge_fwd(y_a, y_h, proj):
    s = y_a.shape[0]
    tn = MG_TN
    ba, bh = GT_A // tn, GT_H // tn

    def body(ya_ref, yh_ref, ga_ref, gh_ref, m_ref):
        m_ref[...] = (_sig(ga_ref[...]) * ya_ref[...] + _sig(gh_ref[...]) * yh_ref[...]).astype(m_ref.dtype)

    tr = min(s, TALL)
    blk = lambda f: pl.BlockSpec((tr, tn), f)
    return pl.pallas_call(body, grid=(s // tr, D // tn),
                          in_specs=[blk(lambda i, j: (i, j)), blk(lambda i, j: (i, j)), blk(lambda i, j: (i, j + ba)), blk(lambda i, j: (i, j + bh))],
                          out_specs=blk(lambda i, j: (i, j)), out_shape=_sds((s, D), BF16),
                          compiler_params=_params(2), name="merge_fwd")(y_a, y_h, proj, proj)


def _merge_bwd(dm, y_a, y_h, proj):
    s = y_a.shape[0]
    tn = MG_TN
    ba, bh = GT_A // tn, GT_H // tn

    def body(dm_ref, ya_ref, yh_ref, ga_ref, gh_ref, dya_ref, dyh_ref, dga_ref, dgh_ref):
        dmv = dm_ref[...]
        sa, sh = _sig(ga_ref[...]), _sig(gh_ref[...])
        dya_ref[...] = (dmv * sa).astype(BF16)
        dyh_ref[...] = (dmv * sh).astype(BF16)
        dga_ref[...] = (dmv * ya_ref[...] * (sa * (1.0 - sa))).astype(BF16)
        dgh_ref[...] = (dmv * yh_ref[...] * (sh * (1.0 - sh))).astype(BF16)

    tr = min(s, TALL)
    blk = lambda f: pl.BlockSpec((tr, tn), f)
    nat = blk(lambda i, j: (i, j))
    return pl.pallas_call(body, grid=(s // tr, D // tn),
                          in_specs=[nat, nat, nat, blk(lambda i, j: (i, j + ba)), blk(lambda i, j: (i, j + bh))],
                          out_specs=[nat] * 4, out_shape=[_sds((s, D), BF16)] * 4,
                          compiler_params=_params(2), name="merge_bwd")(dm, y_a, y_h, proj, proj)


def _hgout_fwd(o_raw, proj, hg_norm):
    s = o_raw.shape[0]
    bg = G_H // LANE

    def body(o_ref, g_ref, n_ref, out_ref):
        ov = o_ref[...]
        r = lax.rsqrt(jnp.mean(ov * ov, axis=-1, keepdims=True) + EPS)
        out_ref[...] = (ov * r * n_ref[...] * _sig(g_ref[...])).astype(out_ref.dtype)

    tr = min(s, TALL)
    blk = lambda f: pl.BlockSpec((tr, LANE), f)
    return pl.pallas_call(body, grid=(s // tr, HG_HEADS),
                          in_specs=[blk(lambda i, h: (i, h)), blk(lambda i, h: (i, h + bg)), pl.BlockSpec((1, LANE), lambda i, h: (0, 0))],
                          out_specs=blk(lambda i, h: (i, h)), out_shape=_sds((s, HG_W), BF16),
                          compiler_params=_params(2), name="hgout_fwd")(o_raw, proj, hg_norm)


def _hgout_bwd(d_out, o_raw, proj, hg_norm):
    s = o_raw.shape[0]
    bg = G_H // LANE

    def body(d_ref, o_ref, g_ref, n_ref, do_ref, dg_ref, dn_ref):
        first = jnp.logical_and(pl.program_id(0) == 0, pl.program_id(1) == 0)
        ov, dv, nv = o_ref[...], d_ref[...], n_ref[...]
        sg = _sig(g_ref[...])
        r = lax.rsqrt(jnp.mean(ov * ov, axis=-1, keepdims=True) + EPS)
        oh = ov * r
        d_on = dv * sg
        dg_ref[...] = (dv * (oh * nv) * (sg * (1.0 - sg))).astype(dg_ref.dtype)
        t = d_on * nv
        do_ref[...] = r * (t - oh * jnp.mean(t * oh, axis=-1, keepdims=True))
        _acc_rows(dn_ref, first, jnp.sum(d_on * oh, axis=0, keepdims=True))

    tr = min(s, TALL)
    blk = lambda f: pl.BlockSpec((tr, LANE), f)
    vec = pl.BlockSpec((1, LANE), lambda i, h: (0, 0))
    return pl.pallas_call(body, grid=(s // tr, HG_HEADS),
                          in_specs=[blk(lambda i, h: (i, h)), blk(lambda i, h: (i, h)), blk(lambda i, h: (i, h + bg)), vec],
                          out_specs=[blk(lambda i, h: (i, h)), blk(lambda i, h: (i, h)), vec],
                          out_shape=[_sds((s, HG_W), F32), _sds((s, HG_W), BF16), _sds((1, LANE), F32)],
                          compiler_params=_params(2), name="hgout_bwd")(d_out, o_raw, proj, hg_norm)


def _rope(t, cos, s_lo, s_hi):
    return t * cos + pltpu.roll(t, LANE - ROT // 2, 1) * s_lo + pltpu.roll(t, ROT // 2, 1) * s_hi


def _rope_wide(t, cos, s_lo, s_hi):
    return jnp.concatenate([_rope(t[:, k * LANE:(k + 1) * LANE], cos, s_lo, s_hi) for k in range(t.shape[1] // LANE)], axis=1)


def _attn_mask(has_prev):
    kj = lax.broadcasted_iota(jnp.int32, (2 * BLK, BLK), 0)
    qi = lax.broadcasted_iota(jnp.int32, (2 * BLK, BLK), 1)
    rel = BLK + qi - kj
    band = jnp.logical_and(rel >= 0, rel < BLK)
    return jnp.logical_and(band, jnp.logical_or(has_prev, kj >= BLK))


def _attn_specs():
    prev = lambda i: jnp.maximum(i - 1, 0)
    kb, vb = K_A // LANE, V_A // LANE
    blk = lambda f: pl.BlockSpec((BLK, LANE), f)
    tabs = [blk(lambda i: (i, 0))] * 3 + [blk(lambda i: (prev(i), 0))] * 3
    return [pl.BlockSpec((BLK, ATT_W), lambda i: (i, 0)), blk(lambda i: (i, kb)), blk(lambda i: (prev(i), kb)),
            blk(lambda i: (i, vb)), blk(lambda i: (prev(i), vb))] + tabs + [pl.BlockSpec((1, LANE), lambda i: (0, 0))]


def _attn_logits(qh, kg):
    return _dot(kg, qh, NT)


def _attn_probs(raw, mask, sk):
    logits = jnp.where(mask, raw * (HEAD_DIM ** -0.5), -jnp.inf)
    m = jnp.maximum(jnp.max(logits, axis=0, keepdims=True), sk)
    p = jnp.exp(logits - m)
    e_sink = jnp.exp(sk - m)
    inv = 1.0 / (jnp.sum(p, axis=0, keepdims=True) + e_sink)
    return p, inv, e_sink * inv


def _attn_fwd(proj, tabs, sinks):
    s = proj.shape[0]

    def body(q_ref, kc_ref, kp_ref, vc_ref, vp_ref, c0, l0, h0, c1, l1, h1, sk_ref, o_ref):
        i = pl.program_id(0)
        mask = _attn_mask(i > 0)
        q = _rope_wide(q_ref[...], c0[...], l0[...], h0[...]).astype(BF16)
        kk = jnp.concatenate([_rope(kp_ref[...], c1[...], l1[...], h1[...]), _rope(kc_ref[...], c0[...], l0[...], h0[...])], axis=0).astype(BF16)
        v_t = jnp.concatenate([vp_ref[...], vc_ref[...]], axis=0).T.astype(BF16)
        part = lambda t, h: t[:, h * HEAD_DIM:(h + 1) * HEAD_DIM]
        k_heads = [part(kk, g) for g in range(KV_HEADS)]

        def head(h):
            g = h // GROUP
            raw = _attn_logits(part(q, h), k_heads[g])
            yield
            p, inv, _ = _attn_probs(raw, mask, sk_ref[:, h:h + 1])
            yield
            out_t = _dot(v_t[g * HEAD_DIM:(g + 1) * HEAD_DIM], p.astype(BF16), NN)
            yield
            return out_t * inv

        o_ref[...] = jnp.concatenate(_interleave([head(h) for h in range(ATT_HEADS)]), axis=0).T.astype(o_ref.dtype)

    return pl.pallas_call(body, grid=(s // BLK,), in_specs=_attn_specs(),
                          out_specs=pl.BlockSpec((BLK, ATT_W), lambda i: (i, 0)), out_shape=_sds((s, ATT_W), BF16),
                          compiler_params=_params(1), name="attn_fwd")(proj, proj, proj, proj, proj, *tabs, *tabs, sinks)


def _attn_bwd(proj, tabs, sinks, d_att):
    s = proj.shape[0]

    def body(q_ref, kc_ref, kp_ref, vc_ref, vp_ref, c0, l0, h0, c1, l1, h1, sk_ref, do_ref, dq_ref, dk_ref, dv_ref, ds_ref):
        i = pl.program_id(0)

        @pl.when(i == 0)
        def _():
            dk_ref[...] = jnp.zeros_like(dk_ref)
            dv_ref[...] = jnp.zeros_like(dv_ref)
            ds_ref[...] = jnp.zeros_like(ds_ref)

        mask = _attn_mask(i > 0)
        q = _rope_wide(q_ref[...], c0[...], l0[...], h0[...]).astype(BF16)
        kk = jnp.concatenate([_rope(kp_ref[...], c1[...], l1[...], h1[...]), _rope(kc_ref[...], c0[...], l0[...], h0[...])], axis=0).astype(BF16)
        k_f32 = jnp.concatenate([_rope(kp_ref[...], c1[...], l1[...], h1[...]), _rope(kc_ref[...], c0[...], l0[...], h0[...])], axis=0)
        k_t = k_f32.T.astype(BF16)
        vv = jnp.concatenate([vp_ref[...], vc_ref[...]], axis=0).astype(BF16)
        d_o = do_ref[...].astype(BF16)
        lane = lax.broadcasted_iota(jnp.int32, (1, LANE), 1)
        part = lambda t, h: t[:, h * HEAD_DIM:(h + 1) * HEAD_DIM]
        k_heads = [part(kk, g) for g in range(KV_HEADS)]
        v_heads = [part(vv, g) for g in range(KV_HEADS)]

        def head(h):
            g = h // GROUP
            qh, doh = part(q, h), part(d_o, h)
            raw = _attn_logits(qh, k_heads[g])
            d_p = _dot(v_heads[g], doh, NT)
            yield
            p, inv, p_sink = _attn_probs(raw, mask, sk_ref[:, h:h + 1])
            prob = p * inv
            dv = _dot(prob.astype(BF16), doh, NN)
            yield
            dd = jnp.sum(prob * d_p, axis=0, keepdims=True)
            d_s = (prob * (d_p - dd)).astype(BF16)
            d_sink = jnp.where(lane == h, -jnp.sum(p_sink * dd, axis=1, keepdims=True), 0.0)
            dq_t = _dot(k_t[g * HEAD_DIM:(g + 1) * HEAD_DIM], d_s, NN)
            dk = _dot(d_s, qh, NN)
            yield
            return dq_t * (HEAD_DIM ** -0.5), dk * (HEAD_DIM ** -0.5), dv, d_sink

        per_head = _interleave([head(h) for h in range(ATT_HEADS)])
        dqs = [jnp.concatenate([t[0] for t in per_head], axis=0).T]
        group_sum = lambda k, g: functools.reduce(jnp.add, [t[k] for t in per_head[g * GROUP:(g + 1) * GROUP]])
        dks = [group_sum(1, g) for g in range(KV_HEADS)]
        dvs = [group_sum(2, g) for g in range(KV_HEADS)]
        d_sink = functools.reduce(jnp.add, [t[3] for t in per_head])
        dq_ref[...] = _rope_wide(jnp.concatenate(dqs, axis=1), c0[...], -l0[...], -h0[...]).astype(dq_ref.dtype)
        d_k = jnp.concatenate(dks, axis=1)
        d_v = jnp.concatenate(dvs, axis=1)
        cur = pl.ds(pl.multiple_of(i * BLK, BLK), BLK)
        prv = pl.ds(pl.multiple_of(jnp.maximum(i - 1, 0) * BLK, BLK), BLK)
        dk_ref[prv, :] += _rope(d_k[:BLK], c1[...], -l1[...], -h1[...])
        dk_ref[cur, :] += _rope(d_k[BLK:], c0[...], -l0[...], -h0[...])
        dv_ref[prv, :] += d_v[:BLK]
        dv_ref[cur, :] += d_v[BLK:]
        ds_ref[...] += d_sink

    full = pl.BlockSpec((s, LANE), lambda i: (0, 0))
    return pl.pallas_call(body, grid=(s // BLK,), in_specs=_attn_specs() + [pl.BlockSpec((BLK, ATT_W), lambda i: (i, 0))],
                          out_specs=[pl.BlockSpec((BLK, ATT_W), lambda i: (i, 0)), full, full, pl.BlockSpec((1, LANE), lambda i: (0, 0))],
                          out_shape=[_sds((s, ATT_W), BF16), _sds((s, LANE), F32), _sds((s, LANE), F32), _sds((1, LANE), F32)],
                          compiler_params=_params(1), name="attn_bwd")(proj, proj, proj, proj, proj, *tabs, *tabs, sinks, d_att)


def _tri_matmul(tri, t):
    hi = t.astype(BF16)
    r1 = t - hi.astype(F32)
    mid = r1.astype(BF16)
    lo = (r1 - mid.astype(F32)).astype(BF16)
    return _dot(tri, hi, NN) + _dot(tri, mid, NN) + _dot(tri, lo, NN)


def _lower_bound(hl):
    a, b = hl[0:1, :], hl[1:2, :]
    mx = jnp.maximum(a, b)
    ea, eb = jnp.exp(a - mx), jnp.exp(b - mx)
    return ea / (ea + eb)


def _hg_gates(q_raw, f_raw, lb, tri_lower):
    sg = _sig(f_raw)
    f = lb + (1.0 - lb) * sg
    sq = _sig(q_raw)
    b = _tri_matmul(tri_lower, jnp.log(f))
    return sg, f, 1.0 - f, sq, q_raw * sq, b


HG_PAIR_FWD = 8
HG_PAIR_BWD = 8


def _hg_specs(n_map, pair):
    blk = lambda off, p: pl.BlockSpec((HG_TB, LANE), lambda h, n: (n_map(n), off // LANE + pair * h + p))
    return [blk(off, p) for off in (Q_H, F_H, I_H) for p in range(pair)] + [pl.BlockSpec((2, pair * LANE), lambda h, n: (0, h))]


def _interleave(gens):
    out = [None] * len(gens)
    live = list(range(len(gens)))
    while live:
        for k in list(live):
            try:
                next(gens[k])
            except StopIteration as stop:
                out[k] = stop.value
                live.remove(k)
    return out


def _hg_spread():
    c = lax.broadcasted_iota(jnp.int32, (CHUNK, SUB * SUB), 0)
    l = lax.broadcasted_iota(jnp.int32, (CHUNK, SUB * SUB), 1)
    r = lax.broadcasted_iota(jnp.int32, (SUB, SUB * SUB), 0)
    lr = lax.broadcasted_iota(jnp.int32, (SUB, SUB * SUB), 1)
    shift = SUB.bit_length() - 1
    cols = [(c == lo + (l >> shift)).astype(BF16) for lo in range(0, CHUNK, SUB)]
    tile = [(c == lo + (l & (SUB - 1))).astype(BF16) for lo in range(0, CHUNK, SUB)]
    return cols, tile, (lr & (SUB - 1)) == r, (lr >> shift) == r


def _hg_intra(qs, kk, b, grad=None):
    lane = lax.broadcasted_iota(jnp.int32, (SUB, CHUNK), 1)
    row1 = lax.broadcasted_iota(jnp.int32, (SUB, 1), 0)
    kk_b = kk.astype(BF16)
    if grad is not None:
        d_a, d_at, (cols, tile, diag, block) = grad
    a_blocks, dq_blocks, dk_blocks, db_blocks = [], [], [], []
    dk_left = None
    for j in range(CHUNK // SUB):
        lo = j * SUB
        q_j, k_j, b_j = qs[lo:lo + SUB], kk[lo:lo + SUB], b[lo:lo + SUB]
        es = [jnp.where(row1 >= sx, jnp.exp(jnp.minimum(b_j - b_j[sx:sx + 1], 0.0)), 0.0) for sx in range(SUB)]
        pes = [q_j * e for e in es]
        pe = jnp.concatenate(pes, axis=0).astype(BF16)
        pairs = _dot(pe, kk_b, NT)
        yield
        a_j = jnp.zeros((SUB, CHUNK), F32)
        for sx in range(SUB):
            a_j = jnp.where(lane == lo + sx, pairs[sx * SUB:(sx + 1) * SUB], a_j)
        if grad is not None:
            da_j = d_a[lo:lo + SUB]
            ek = jnp.concatenate([e * k_j[sx:sx + 1] for sx, e in enumerate(es)], axis=0).astype(BF16)
            sel_t = jnp.where(diag, _dot(da_j.astype(BF16), cols[j], NN), 0.0).astype(BF16)
            sel_s = jnp.where(block, _dot(d_at[lo:lo + SUB].astype(BF16), tile[j], NN), 0.0).astype(BF16)
            pek = jnp.concatenate([p * k_j[sx:sx + 1] for sx, p in enumerate(pes)], axis=0).astype(BF16)
            yield
            dq_j = _dot(sel_t, ek, NN)
            dk_j = _dot(sel_s, pe, NN)
            db_j = _dot(sel_t, pek, NN) - _dot(sel_s, pek, NN)
            yield
        if j > 0:
            ref = b[lo - 1:lo]
            sc_q = jnp.exp(b_j - ref)
            sc_k = jnp.exp(jnp.minimum(ref - b, 0.0))
            qt = (q_j * sc_q).astype(BF16)
            kt = (kk * sc_k).astype(BF16)
            left = _dot(qt, kt, NT)
            yield
            a_j = a_j + jnp.where(lane < lo, left, 0.0)
            if grad is not None:
                da_left = jnp.where(lane < lo, da_j, 0.0).astype(BF16)
                dq_left = _dot(da_left, kt, NN) * sc_q
                dq_j = dq_j + dq_left
                db_j = db_j + q_j * dq_left
                t = _dot(da_left, qt, TN)
                yield
                t = t * sc_k
                dk_left = t if dk_left is None else dk_left + t
        a_blocks.append(a_j)
        if grad is not None:
            dq_blocks.append(dq_j)
            dk_blocks.append(dk_j)
            db_blocks.append(db_j)
    a = jnp.concatenate(a_blocks, axis=0)
    if grad is None:
        return a
    return a, jnp.concatenate(dq_blocks, axis=0), jnp.concatenate(dk_blocks, axis=0) + dk_left, jnp.concatenate(db_blocks, axis=0) - kk * dk_left


def _hgrn_fwd(proj, hl):
    s = proj.shape[0]
    n_chunk = HG_TB // CHUNK
    pair = HG_PAIR_FWD

    def body(*refs):
        q_refs, f_refs, i_refs = refs[:pair], refs[pair:2 * pair], refs[2 * pair:3 * pair]
        hl_ref, o_ref, st_out_ref, st_ref = refs[3 * pair:]

        @pl.when(pl.program_id(1) == 0)
        def _():
            st_ref[...] = jnp.zeros_like(st_ref)

        r_i = lax.broadcasted_iota(jnp.int32, (CHUNK, CHUNK), 0)
        c_i = lax.broadcasted_iota(jnp.int32, (CHUNK, CHUNK), 1)
        tri_lower = (r_i >= c_i).astype(BF16)

        def chunk(c, carry):
            rows = pl.ds(pl.multiple_of(c * CHUNK, CHUNK), CHUNK)
            def head(p):
                cols = slice(p * LANE, (p + 1) * LANE)
                lb = _lower_bound(hl_ref[:, cols])
                v = i_refs[p][rows, :].astype(BF16)
                _, _, kk, _, qs, b = _hg_gates(q_refs[p][rows, :], f_refs[p][rows, :], lb, tri_lower)
                yield
                st = st_ref[p]
                st_b = st.astype(BF16)
                st_out_ref[p, c] = st_b
                o_state = _dot((qs * jnp.exp(b)).astype(BF16), st_b, NT)
                b_last = b[CHUNK - 1:CHUNK, :]
                st_new = _dot(v, (kk * jnp.exp(b_last - b)).astype(BF16), TN)
                a = yield from _hg_intra(qs, kk, b)
                st_ref[p] = st * jnp.exp(b_last) + st_new
                o_ref[rows, cols] = o_state + _dot(a.astype(BF16), v, NN)

            _interleave([head(p) for p in range(pair)])
            return carry

        lax.fori_loop(0, n_chunk, chunk, 0)

    return pl.pallas_call(
        body, grid=(HG_HEADS // pair, s // HG_TB), in_specs=_hg_specs(lambda n: n, pair),
        out_specs=[pl.BlockSpec((HG_TB, pair * LANE), lambda h, n: (n, h)), pl.BlockSpec((pair, n_chunk, HG_K, HG_K), lambda h, n: (h, n, 0, 0))],
        out_shape=[_sds((s, HG_W), F32), _sds((HG_HEADS, s // CHUNK, HG_K, HG_K), BF16)],
        scratch_shapes=[pltpu.VMEM((pair, HG_K, HG_K), F32)],
        compiler_params=_params(2), name="hgrn_fwd")(*[proj] * (3 * pair), hl)


def _hgrn_bwd(proj, hl, states, d_o):
    s = proj.shape[0]
    n_chunk = HG_TB // CHUNK
    n_blk = s // HG_TB
    pair = HG_PAIR_BWD
    rev = lambda n: n_blk - 1 - n

    def body(*refs):
        q_refs, f_refs, i_refs = refs[:pair], refs[pair:2 * pair], refs[2 * pair:3 * pair]
        hl_ref, st_in_ref, do_ref, dq_ref, df_ref, di_ref, dhl_ref, dst_ref, dlb_ref = refs[3 * pair:]
        n = pl.program_id(1)

        @pl.when(n == 0)
        def _():
            dst_ref[...] = jnp.zeros_like(dst_ref)
            dlb_ref[...] = jnp.zeros_like(dlb_ref)

        r_i = lax.broadcasted_iota(jnp.int32, (CHUNK, CHUNK), 0)
        c_i = lax.broadcasted_iota(jnp.int32, (CHUNK, CHUNK), 1)
        tri_lower = (r_i >= c_i).astype(BF16)
        tri_upper = (r_i <= c_i).astype(BF16)
        row = lax.broadcasted_iota(jnp.int32, (CHUNK, 1), 0)
        spread = _hg_spread()

        def chunk(cc, carry):
            c = n_chunk - 1 - cc
            rows = pl.ds(pl.multiple_of(c * CHUNK, CHUNK), CHUNK)
            def head(p):
                cols = slice(p * LANE, (p + 1) * LANE)
                lb = _lower_bound(hl_ref[:, cols])
                q_raw = q_refs[p][rows, :]
                vb = i_refs[p][rows, :].astype(BF16)
                sg, f, kk, sq, qs, b = _hg_gates(q_raw, f_refs[p][rows, :], lb, tri_lower)
                yield
                e_b = jnp.exp(b)
                qe = qs * e_b
                b_last = b[CHUNK - 1:CHUNK, :]
                e_last = jnp.exp(b_last)
                e_kd = jnp.exp(b_last - b)
                kd = kk * e_kd
                st0 = st_in_ref[p, c]
                d_ob = do_ref[rows, cols].astype(BF16)
                dst = dst_ref[p]
                dst_b = dst.astype(BF16)
                d_a = jnp.where(r_i >= c_i, _dot(d_ob, vb, NT), 0.0)
                d_at = jnp.where(r_i <= c_i, _dot(vb, d_ob, NT), 0.0)
                d_v_st = _dot(kd.astype(BF16), dst_b, NT)
                d_kd = _dot(vb, dst_b, NN)
                d_qe = _dot(d_ob, st0, NN)
                dst_new = _dot(d_ob, qe.astype(BF16), TN)
                yield
                a, dqs, dkk, d_b = yield from _hg_intra(qs, kk, b, (d_a, d_at, spread))
                d_v = _dot(a.astype(BF16), d_ob, TN) + d_v_st
                dqs_st = d_qe * e_b
                dkk_st = d_kd * e_kd
                dqs = dqs + dqs_st
                dkk = dkk + dkk_st
                d_b_last = jnp.sum(d_kd * kd, axis=0, keepdims=True) + jnp.sum(dst * st0.astype(F32), axis=0, keepdims=True) * e_last
                d_b = d_b + qs * dqs_st - kk * dkk_st + jnp.where(row == CHUNK - 1, d_b_last, 0.0)
                d_g = _tri_matmul(tri_upper, d_b)
                dst_ref[p] = dst_new + dst * e_last
                yield
                d_f = d_g / f - dkk
                dlb_ref[:, cols] += jnp.sum(d_f * (1.0 - sg), axis=0, keepdims=True)
                dq_ref[rows, cols] = (dqs * (sq * (1.0 + q_raw * (1.0 - sq)))).astype(dq_ref.dtype)
                df_ref[rows, cols] = (d_f * (1.0 - lb) * (sg * (1.0 - sg))).astype(df_ref.dtype)
                di_ref[rows, cols] = d_v.astype(di_ref.dtype)

            _interleave([head(p) for p in range(pair)])
            return carry

        lax.fori_loop(0, n_chunk, chunk, 0)

        @pl.when(n == n_blk - 1)
        def _():
            lb = _lower_bound(hl_ref[...])
            d_hl0 = dlb_ref[...] * (lb * (1.0 - lb))
            dhl_ref[...] = jnp.concatenate([d_hl0, -d_hl0], axis=0)

    out_blk = pl.BlockSpec((HG_TB, pair * LANE), lambda h, n: (rev(n), h))
    return pl.pallas_call(
        body, grid=(HG_HEADS // pair, n_blk),
        in_specs=_hg_specs(rev, pair) + [pl.BlockSpec((pair, n_chunk, HG_K, HG_K), lambda h, n: (h, rev(n), 0, 0)), out_blk],
        out_specs=[out_blk, out_blk, out_blk, pl.BlockSpec((2, pair * LANE), lambda h, n: (0, h))],
        out_shape=[_sds((s, HG_W), BF16)] * 3 + [_sds((2, HG_W), F32)],
        scratch_shapes=[pltpu.VMEM((pair, HG_K, HG_K), F32), pltpu.VMEM((1, pair * LANE), F32)],
        compiler_params=_params(2), name="hgrn_bwd")(*[proj] * (3 * pair), hl, states, d_o)


def _mod_part(c_all, w_shard, b_shard):
    n = w_shard.shape[1]
    tn = 512

    def body(c_ref, w_ref, b_ref, o_ref):
        o_ref[...] = _dot(c_ref[...].astype(BF16), w_ref[...].astype(BF16), NN) + b_ref[...]

    return pl.pallas_call(body, grid=(n // tn,),
                          in_specs=[pl.BlockSpec((N_DEV, D), lambda j: (0, 0)), pl.BlockSpec((D, tn), lambda j: (0, j)), pl.BlockSpec((1, tn), lambda j: (0, j))],
                          out_specs=pl.BlockSpec((N_DEV, tn), lambda j: (0, j)), out_shape=_sds((N_DEV, n), F32),
                          compiler_params=_params(1, 32 << 20), name="mod_part")(c_all, w_shard, b_shard)


def _adam_math(g, w, m, v):
    c1 = 1.0 / (1.0 - ADAM_B1 ** ADAM_STEP)
    c2 = 1.0 / (1.0 - ADAM_B2 ** ADAM_STEP)
    m2 = ADAM_B1 * m + (1.0 - ADAM_B1) * g
    v2 = ADAM_B2 * v + (1.0 - ADAM_B2) * (g * g)
    return -ADAM_LR * ((m2 * c1) / (jnp.sqrt(v2 * c2) + ADAM_EPS) + ADAM_WD * w), m2, v2


def _update_w_ada(c_all_t, dmod_cols, w, m, v, deps=()):
    n = dmod_cols.shape[1]
    tn = 256

    def body(c_ref, d_ref, w_ref, m_ref, v_ref, *rest):
        g_ref, dl_ref, m2_ref, v2_ref = rest[len(deps):]
        cv = c_ref[...].astype(BF16).astype(F32)
        dv = d_ref[...].astype(BF16).astype(F32)
        g = cv[:, 0:1] * dv[0:1, :]
        for k in range(1, N_DEV):
            g = g + cv[:, k:k + 1] * dv[k:k + 1, :]
        g_ref[...] = g
        dl_ref[...], m2_ref[...], v2_ref[...] = _adam_math(g, w_ref[...], m_ref[...], v_ref[...])

    blk = pl.BlockSpec((D, tn), lambda j: (0, j))
    return pl.pallas_call(body, grid=(n // tn,),
                          in_specs=[pl.BlockSpec((D, N_DEV), lambda j: (0, 0)), pl.BlockSpec((N_DEV, tn), lambda j: (0, j)), blk, blk, blk]
                          + [pl.BlockSpec(memory_space=pl.ANY)] * len(deps),
                          out_specs=[blk] * 4, out_shape=[_sds((D, n), F32)] * 4,
                          compiler_params=_params(1, 48 << 20), name="adamw_w_ada")(c_all_t, dmod_cols, w, m, v, *deps)


def _row_tile(r, c, max_elems=1 << 18):
    if r * c <= max_elems or r % 8:
        return r
    best = 8
    for t in range(8, r + 1, 8):
        if r % t == 0 and t * c <= max_elems:
            best = t
    return best


WIDE_TILE = 5 << 17
PAIR_TILE = 3 << 19


def _adamw(pieces, w, m, v, name, own=None, max_elems=1 << 18):
    p, r, c = pieces.shape
    tr = _row_tile(r, c, max_elems)

    def body(*refs):
        if own is None:
            p_ref, w_ref, m_ref, v_ref, *outs = refs
            g = p_ref[0].astype(F32)
        else:
            o_ref, p_ref, w_ref, m_ref, v_ref, *outs = refs
            g = o_ref[...].astype(F32) + p_ref[0].astype(F32)
        for k in range(1, p):
            g = g + p_ref[k].astype(F32)
        outs[0][...] = g
        outs[1][...], outs[2][...], outs[3][...] = _adam_math(g, w_ref[...], m_ref[...], v_ref[...])

    blk = pl.BlockSpec((tr, c), lambda i: (i, 0))
    lead = [] if own is None else [own]
    return pl.pallas_call(body, grid=(r // tr,), in_specs=[blk] * len(lead) + [pl.BlockSpec((p, tr, c), lambda i: (0, i, 0)), blk, blk, blk],
                          out_specs=[blk] * 4, out_shape=[_sds((r, c), F32)] * 4,
                          compiler_params=_params(1, 48 << 20), name=name)(*lead, pieces, w, m, v)


def _my_coords():
    return lax.axis_index("x"), lax.axis_index("y"), lax.axis_index("c")


def _flip(coords, k):
    x, y, c = coords
    return (1 - x if k & 4 else x, 1 - y if k & 2 else y, 1 - c if k & 1 else c)


def _lin(coords):
    return 4 * coords[0] + 2 * coords[1] + coords[2]


def _exchange_small(x3, bcast, name):
    n = x3.shape[2]

    def body(x_ref, o_ref, send_sems, recv_sems):
        me = _my_coords()
        my_id = _lin(me)
        o_ref[pl.ds(my_id, 1)] = x_ref[pl.ds(0 if bcast else my_id, 1)]
        copies = []
        for k in range(1, N_DEV):
            peer = _flip(me, k)
            src = x_ref.at[0 if bcast else _lin(peer)]
            cp = pltpu.make_async_remote_copy(src_ref=src, dst_ref=o_ref.at[my_id], send_sem=send_sems.at[k], recv_sem=recv_sems.at[k],
                                              device_id=peer, device_id_type=MESH)
            cp.start()
            copies.append(cp)
        for k in range(1, N_DEV):
            peer = _flip(me, k)
            pltpu.make_async_remote_copy(src_ref=x_ref.at[0], dst_ref=o_ref.at[_lin(peer)], send_sem=send_sems.at[k], recv_sem=recv_sems.at[k],
                                         device_id=peer, device_id_type=MESH).wait_recv()
        for cp in copies:
            cp.wait_send()

    vm = pl.BlockSpec(memory_space=pltpu.VMEM)
    return pl.pallas_call(body, in_specs=[vm], out_specs=vm, out_shape=_sds((N_DEV, 1, n), F32),
                          scratch_shapes=[pltpu.SemaphoreType.DMA((N_DEV,)), pltpu.SemaphoreType.DMA((N_DEV,))], name=name)(x3)


HBM_SPEC = pl.BlockSpec(memory_space=pltpu.HBM)
SEM_SPEC = pl.BlockSpec(memory_space=pltpu.SEMAPHORE)
ANY_SPEC = pl.BlockSpec(memory_space=pl.ANY)
DATAFLOW = pltpu.SideEffectType.DATAFLOW_SIDE_EFFECTING
GATHER_FLIPS = (1, 2, 4, 6)
PASS_FLIPS = (2, 4, 6)
TOKEN = (8, LANE)


def _hbm(t):
    return pltpu.with_memory_space_constraint(t, pltpu.HBM)


def _hbm_like(ts):
    return [pltpu.HBM(t.shape, t.dtype) for t in ts]


def _split_start(issue, srcs, lands, n_sem, name, deps=()):
    n, nb, nd = len(srcs), len(srcs) + len(lands), len(deps)

    def body(*refs):
        issue(refs[:n], refs[n:nb], refs[nb + nd], refs[nb + nd + 1])
        refs[-1][...] = jnp.zeros(TOKEN, F32)

    outs = pl.pallas_call(
        body, name=name,
        out_shape=(pltpu.SemaphoreType.DMA((n_sem,)), pltpu.SemaphoreType.DMA((n_sem,)), *_hbm_like(srcs), *_hbm_like(lands), _sds(TOKEN, F32)),
        in_specs=[HBM_SPEC] * nb + [ANY_SPEC] * nd,
        out_specs=(SEM_SPEC, SEM_SPEC, *[HBM_SPEC] * nb, pl.BlockSpec(memory_space=pltpu.VMEM)),
        input_output_aliases={i: 2 + i for i in range(nb)},
        compiler_params=pltpu.CompilerParams(has_side_effects=DATAFLOW))(*[_hbm(t) for t in srcs], *[_hbm(t) for t in lands], *deps)
    return dict(sems=outs[:2], thru=list(outs[2:2 + nb]), token=outs[-1], n=n)


def _split_wait(finish, handle, after, name):
    n = handle["n"]
    thru = handle["thru"]
    nb = len(thru)

    def body(*refs):
        finish(refs[:n], refs[n:nb], refs[nb], refs[nb + 1])

    outs = pl.pallas_call(
        body, name=name, out_shape=_hbm_like(thru), in_specs=[HBM_SPEC] * nb + [SEM_SPEC, SEM_SPEC] + [ANY_SPEC] * len(after),
        out_specs=[HBM_SPEC] * nb, input_output_aliases={i: i for i in range(nb)},
        compiler_params=pltpu.CompilerParams(has_side_effects=DATAFLOW))(*thru, *handle["sems"], *after)
    return list(outs[:n]), list(outs[n:])


def _gather_start(shards, name, deps=()):
    n = len(shards)
    my_id = _lin(_my_coords())
    lands = [lax.dynamic_update_slice(lax.empty((N_DEV,) + t.shape, t.dtype), t[None], (my_id, 0, 0)) for t in shards]

    def issue(src, land, send_sems, recv_sems):
        me = _my_coords()
        for w in range(n):
            for j, k in enumerate(GATHER_FLIPS):
                q = len(GATHER_FLIPS) * w + j
                pltpu.make_async_remote_copy(src_ref=src[w], dst_ref=land[w].at[_lin(me)], send_sem=send_sems.at[q], recv_sem=recv_sems.at[q],
                                             device_id=_flip(me, k), device_id_type=MESH).start()

    return _split_start(issue, shards, lands, len(GATHER_FLIPS) * n, name, deps)


def _gather_wait(handle, after, name):
    n = handle["n"]

    def finish(src, land, send_sems, recv_sems):
        me = _my_coords()
        for w in range(n):
            for j, k in enumerate(GATHER_FLIPS):
                q = len(GATHER_FLIPS) * w + j
                peer = _flip(me, k)
                cp = pltpu.make_async_remote_copy(src_ref=src[w], dst_ref=land[w].at[_lin(peer)], send_sem=send_sems.at[q], recv_sem=recv_sems.at[q],
                                                  device_id=peer, device_id_type=MESH)
                cp.wait_send()
                cp.wait_recv()

    return _split_wait(finish, handle, after, name)[1]


def _pass_copy(land, send_sems, recv_sems, w, j, arriving):
    me = _my_coords()
    blk = land[w].at[_lin(_flip(me, PASS_FLIPS[j] + (1 if arriving else 0)))]
    q = len(PASS_FLIPS) * w + j
    return pltpu.make_async_remote_copy(src_ref=blk, dst_ref=blk, send_sem=send_sems.at[q], recv_sem=recv_sems.at[q],
                                        device_id=_flip(me, 1), device_id_type=MESH)


def _pass_start(lands, name, deps=()):
    def issue(_, land, send_sems, recv_sems):
        for w in range(len(lands)):
            for j in range(len(PASS_FLIPS)):
                _pass_copy(land, send_sems, recv_sems, w, j, False).start()

    return _split_start(issue, [], lands, len(PASS_FLIPS) * len(lands), name, deps)


def _pass_wait(handle, after, name):
    def finish(_, land, send_sems, recv_sems):
        for w in range(len(handle["thru"])):
            for j in range(len(PASS_FLIPS)):
                _pass_copy(land, send_sems, recv_sems, w, j, False).wait_send()
                _pass_copy(land, send_sems, recv_sems, w, j, True).wait_recv()

    return _split_wait(finish, handle, after, name)[1]


def _gather_pass(lands, name):
    n = len(lands)
    n_p = len(PASS_FLIPS)

    def body(*refs):
        land = refs[n:2 * n]
        send_sems, recv_sems = refs[2 * n:]
        me = _my_coords()
        sibling = _flip(me, 1)
        sent = []
        for w in range(n):
            for j, k in enumerate(PASS_FLIPS):
                blk = land[w].at[_lin(_flip(me, k))]
                cp = pltpu.make_async_remote_copy(src_ref=blk, dst_ref=blk, send_sem=send_sems.at[n_p * w + j], recv_sem=recv_sems.at[n_p * w + j],
                                                  device_id=sibling, device_id_type=MESH)
                cp.start()
                sent.append(cp)
        for w in range(n):
            for j, k in enumerate(PASS_FLIPS):
                blk = land[w].at[_lin(_flip(me, k + 1))]
                pltpu.make_async_remote_copy(src_ref=blk, dst_ref=blk, send_sem=send_sems.at[n_p * w + j], recv_sem=recv_sems.at[n_p * w + j],
                                             device_id=sibling, device_id_type=MESH).wait_recv()
        for cp in sent:
            cp.wait_send()

    return pl.pallas_call(body, in_specs=[ANY_SPEC] * n, out_specs=[ANY_SPEC] * n, out_shape=[_sds(t.shape, t.dtype) for t in lands],
                          input_output_aliases={i: i for i in range(n)},
                          scratch_shapes=[pltpu.SemaphoreType.DMA((n_p * n,)), pltpu.SemaphoreType.DMA((n_p * n,))], name=name)(*lands)


CHIP_FLIPS = (0, 2, 4, 6)


def _pair_copy(src, land, send_sems, recv_sems, w, j):
    me = _my_coords()
    q = len(CHIP_FLIPS) * w + j
    return pltpu.make_async_remote_copy(src_ref=src[w].at[_lin(_flip(me, CHIP_FLIPS[j] + 1))], dst_ref=land[w].at[j], send_sem=send_sems.at[q],
                                        recv_sem=recv_sems.at[q], device_id=_flip(me, 1), device_id_type=MESH)


def _pair_exchange(grads, name):
    n = len(grads)

    def body(*refs):
        src, land = refs[:n], refs[n:2 * n]
        send_sems, recv_sems = refs[2 * n:]
        sent = [_pair_copy(src, land, send_sems, recv_sems, w, j) for w in range(n) for j in range(len(CHIP_FLIPS))]
        for cp in sent:
            cp.start()
        for cp in sent:
            cp.wait_recv()
        for cp in sent:
            cp.wait_send()

    outs = pl.pallas_call(body, in_specs=[ANY_SPEC] * n, out_specs=[ANY_SPEC] * n,
                          out_shape=[_sds((len(CHIP_FLIPS),) + g.shape[1:], g.dtype) for g in grads],
                          scratch_shapes=[pltpu.SemaphoreType.DMA((len(CHIP_FLIPS) * n,))] * 2, name=name)(*grads)
    return list(outs)


def _pair_start(grads, name, deps=()):
    n = len(grads)
    lands = [lax.empty((len(CHIP_FLIPS),) + g.shape[1:], g.dtype) for g in grads]

    def issue(src, land, send_sems, recv_sems):
        for w in range(n):
            for j in range(len(CHIP_FLIPS)):
                _pair_copy(src, land, send_sems, recv_sems, w, j).start()

    return _split_start(issue, grads, lands, len(CHIP_FLIPS) * n, name, deps)


def _pair_wait(handle, after, name):
    n = handle["n"]

    def finish(src, land, send_sems, recv_sems):
        for w in range(n):
            for j in range(len(CHIP_FLIPS)):
                cp = _pair_copy(src, land, send_sems, recv_sems, w, j)
                cp.wait_send()
                cp.wait_recv()

    return _split_wait(finish, handle, after, name)


def _pair_add(grad, theirs, name):
    p, r, c = theirs.shape
    tr = _row_tile(r, c, PAIR_TILE)
    me = _my_coords()
    ids = jnp.stack([_lin(_flip(me, k)) for k in CHIP_FLIPS]).astype(jnp.int32)

    def body(ids_ref, a_ref, b_ref, o_ref):
        o_ref[...] = (a_ref[...].astype(F32) + b_ref[...].astype(F32)).astype(o_ref.dtype)

    blk = pl.BlockSpec((None, tr, c), lambda j, i, ids_ref: (j, i, 0))
    return pl.pallas_call(
        body, out_shape=_sds((p, r, c), theirs.dtype), compiler_params=_params(2), name=name,
        grid_spec=pltpu.PrefetchScalarGridSpec(
            num_scalar_prefetch=1, grid=(p, r // tr),
            in_specs=[pl.BlockSpec((None, tr, c), lambda j, i, ids_ref: (ids_ref[j], i, 0)), blk], out_specs=blk))(ids, grad, theirs)


def _chips_start(parts, name, deps=()):
    n = len(parts)
    n_c = len(CHIP_FLIPS) - 1
    lands = [lax.empty((n_c,) + t.shape[1:], t.dtype) for t in parts]

    def issue(src, land, send_sems, recv_sems):
        me = _my_coords()
        for w in range(n):
            for j in range(1, n_c + 1):
                q = n_c * w + j - 1
                pltpu.make_async_remote_copy(src_ref=src[w].at[j], dst_ref=land[w].at[j - 1], send_sem=send_sems.at[q], recv_sem=recv_sems.at[q],
                                             device_id=_flip(me, CHIP_FLIPS[j]), device_id_type=MESH).start()

    return _split_start(issue, parts, lands, n_c * n, name, deps)


def _chips_wait(handle, after, name):
    n = handle["n"]
    n_c = len(CHIP_FLIPS) - 1

    def finish(src, land, send_sems, recv_sems):
        me = _my_coords()
        for w in range(n):
            for j in range(1, n_c + 1):
                q = n_c * w + j - 1
                cp = pltpu.make_async_remote_copy(src_ref=src[w].at[j], dst_ref=land[w].at[j - 1], send_sem=send_sems.at[q], recv_sem=recv_sems.at[q],
                                                  device_id=_flip(me, CHIP_FLIPS[j]), device_id_type=MESH)
                cp.wait_send()
                cp.wait_recv()

    return _split_wait(finish, handle, after, name)


def _after(t, *tokens):
    for tok in tokens:
        t = t + tok[0:1, 0:1]
    return t


def _rope_tables(positions):
    half = ROT // 2
    inv_freq = ROPE_THETA ** (-jnp.arange(0, ROT, 2, dtype=F32) / ROT)
    ang = positions.astype(F32).reshape(-1, 1) * inv_freq
    cos, sin = jnp.cos(ang), jnp.sin(ang)
    s = ang.shape[0]
    pad = jnp.zeros((s, HEAD_DIM - ROT), F32)
    zero = jnp.zeros((s, half), F32)
    two = lambda t: jnp.concatenate([t, t], axis=1)
    return (two(jnp.concatenate([cos, cos, pad + 1.0], axis=1)), two(jnp.concatenate([-sin, zero, pad], axis=1)),
            two(jnp.concatenate([zero, sin, pad], axis=1)))


def _local_step(x, tgt, tabs, mod, sinks_pad, hl, hg_norm, g_pre_mix, g_post_mix, g_pre_ffn, g_post_ffn, weights, prefetch, scatter, scatter_on):
    s = x.shape[0]
    h1 = _pre_fwd(x, g_pre_mix, mod, 1, 0, "pre_mix_fwd")
    (w_in_a,) = weights("in_a", h1)
    proj = _mm_nt(h1, w_in_a, 256, IN_COLS // 2, D // 2, F32, "proj_mm_a", a_col=0)
    (w_in_b,) = weights("in_b", proj)
    proj = _mm_nt(h1, w_in_b, 256, IN_COLS // 2, D // 2, F32, "proj_mm_b", add=proj, a_col=1)
    att = _attn_fwd(proj, tabs, _after(sinks_pad, prefetch("mix", proj)))
    o_raw, states = _hgrn_fwd(proj, hl)
    ohg = _hgout_fwd(o_raw, proj, hg_norm)
    w_attn_dm, w_hgrn_dm, w_out = weights("mix", ohg)
    natural = lambda w_dm: w_dm.transpose(1, 0, 2).reshape(w_dm.shape[1], D)
    pieces = lambda g: g.reshape(g.shape[0], N_DEV, D // N_DEV).transpose(1, 0, 2)
    w_attn, w_hgrn = natural(w_attn_dm), natural(w_hgrn_dm)
    y_a = _mm_nn(att, w_attn, s, 512, ATT_W, F32, "attn_proj_mm")
    y_h = _mm_nn(ohg, w_hgrn, s, 512, HG_W, F32, "hgrn_proj_mm")
    merged = _merge_fwd(y_a, y_h, proj)
    y = _mm_nn(merged, w_out, s, 512, D, F32, "out_mm")
    x1 = _post_fwd(x, y, g_post_mix, mod, 2, "post_mix_fwd")
    h2 = _pre_fwd(x1, g_pre_ffn, _after(mod, prefetch("ffn_in", x1)), 4, 3, "pre_ffn_fwd")
    (w_ffn_in_dm,) = weights("ffn_in", h2)
    gu = _mm_nn_dm(h2, w_ffn_in_dm, s // 2, F32, "ffn_in_mm")
    act = _swiglu_fwd(gu, deps=[prefetch("ffn_out", gu)])
    (w_ffn_out,) = weights("ffn_out", act)
    y2 = _mm_nn(act, w_ffn_out, 512, 512, FFN, F32, "ffn_out_mm")
    err, loss, dy2, d_gate2, dg_post_ffn = _post_loss_bwd(x1, y2, g_post_ffn, mod, 5, tgt, "post_ffn_loss_bwd")
    gw_ffn_out = _mm_tn(act, dy2, 512, D, BF16, "ffn_out_dw")
    t_pair = scatter([gw_ffn_out.reshape(N_DEV, FFN // N_DEV, D)], "ffn_out")
    d_act = _mm_nt(dy2, w_ffn_out, s, 512, D, F32, "ffn_out_dx", deps=[t_pair])
    dgu = _swiglu_bwd(d_act, gu)
    t_out = scatter_on("ffn_out", dgu)
    gw_ffn_in = _mm_tn_dm(h2, dgu, 1024, BF16, "ffn_in_dw")
    t_pair = scatter([gw_ffn_in], "ffn_in")
    dh2 = _mm_nt_dm(dgu, w_ffn_in_dm, s, 1024, F32, "ffn_in_dx", deps=[t_pair])
    mod = _after(mod, t_out)
    dx1, d_shift2, d_scale2, dg_pre_ffn = _pre_bwd([dh2], x1, err, g_pre_ffn, mod, 4, "pre_ffn_bwd")
    dy, d_gate1, dg_post_mix = _post_bwd(dx1, y, g_post_mix, mod, 2, "post_mix_bwd")
    t_in = scatter_on("ffn_in", dy)
    d_merged = _mm_nt(dy, w_out, s, 512, D, F32, "out_dx")
    gw_out = _mm_tn(merged, dy, 512, D, BF16, "out_dw")
    dy_a, dy_h, d_gate_a, d_gate_h = _merge_bwd(d_merged, y_a, y_h, proj)
    gw_attn = pieces(_mm_tn(att, dy_a, 512, D, BF16, "attn_proj_dw"))
    gw_hgrn = pieces(_mm_tn(ohg, dy_h, 512, D, BF16, "hgrn_proj_dw"))
    t_pair = scatter([gw_attn, gw_hgrn, gw_out.reshape(N_DEV, D // N_DEV, D)], "mix")
    d_att = _mm_nt(dy_a, w_attn, s, 512, D, F32, "attn_proj_dx")
    d_ohg = _mm_nt(dy_h, w_hgrn, s, 512, D, F32, "hgrn_proj_dx", deps=[t_pair])
    d_o, d_gh, d_hg_norm = _hgout_bwd(d_ohg, o_raw, proj, _after(hg_norm, t_in))
    d_qh, d_fh, d_ih, d_hl = _hgrn_bwd(proj, hl, states, d_o)
    t_mix = scatter_on("mix", d_qh)
    d_qa, d_ka, d_va, d_sinks = _attn_bwd(proj, tabs, _after(sinks_pad, t_mix), d_att)
    d_proj = jnp.concatenate([d_qa, d_ka.astype(BF16), d_va.astype(BF16), d_qh, d_fh, d_ih, d_gh, d_gate_a, d_gate_h], axis=1)
    dh1 = [_mm_nn(d_proj, w_half, s // 2, 512, IN_COLS // 2, F32, "proj_dx_" + tag) for tag, w_half in (("a", w_in_a), ("b", w_in_b))]
    grad_x, d_shift1, d_scale1, dg_pre_mix = _pre_bwd(dh1, x, dx1, g_pre_mix, mod, 1, "pre_mix_bwd")
    d_mod = jnp.concatenate([d_shift1, d_scale1, d_gate1, d_shift2, d_scale2, d_gate2], axis=1)
    small = [d_mod, dg_pre_mix, dg_post_mix, dg_pre_ffn, dg_post_ffn, d_hl.reshape(1, 2 * HG_W), d_hg_norm, d_sinks]
    return loss, grad_x, small, h1, d_proj


def kernel(x, c, positions, w_ada, b_ada, g_pre_mix, g_post_mix, g_pre_ffn, g_post_ffn, w_in, attn_sinks, w_attn_proj, hg_lower_bounds, hg_norm, w_hgrn_proj, w_out, w_ffn_in, w_ffn_out, loss_target, m_w_ada, m_b_ada, m_g_pre_mix, m_g_post_mix, m_g_pre_ffn, m_g_post_ffn, m_w_in, m_attn_sinks, m_w_attn_proj, m_hg_lower_bounds, m_hg_norm, m_w_hgrn_proj, m_w_out, m_w_ffn_in, m_w_ffn_out, v_w_ada, v_b_ada, v_g_pre_mix, v_g_post_mix, v_g_pre_ffn, v_g_post_ffn, v_w_in, v_attn_sinks, v_w_attn_proj, v_hg_lower_bounds, v_hg_norm, v_w_hgrn_proj, v_w_out, v_w_ffn_in, v_w_ffn_out):
    my_id = _lin(_my_coords())
    s = x.shape[1]
    n_ada = w_ada.shape[2]

    c_all = _exchange_small(c.reshape(1, 1, D), True, "gather_c").reshape(N_DEV, D)
    b_cols = lax.dynamic_slice(b_ada, (0, my_id * n_ada), (1, n_ada))
    mod_part = _mod_part(c_all, w_ada[0], b_cols)
    mod = _exchange_small(mod_part.reshape(N_DEV, 1, n_ada), False, "scatter_mod").reshape(1, N_MOD * D)
    groups = {"in_a": [w_in[0].T[:, :D // 2]], "in_b": [w_in[0].T[:, D // 2:]], "mix": [w_attn_proj[0], w_hgrn_proj[0], w_out[0]],
              "ffn_in": [w_ffn_in[0]], "ffn_out": [w_ffn_out[0]]}

    def start(group, dep):
        shards, dep = lax.optimization_barrier((groups[group], dep))
        return _gather_start([t.astype(BF16) for t in shards], "gather_start_" + group, deps=[dep])

    gathers = {"in_a": start("in_a", mod)}
    gathers["in_b"] = start("in_b", gathers["in_a"]["token"])
    gathers["mix"] = start("mix", gathers["in_b"]["token"])
    gathers["ffn_in"] = start("ffn_in", gathers["mix"]["token"])
    gathers["ffn_out"] = start("ffn_out", gathers["ffn_in"]["token"])

    passes = {}

    def prefetch(group, after):
        lands = _gather_wait(gathers[group], [after], "gather_wait_" + group)
        passes[group] = _pass_start(lands, "gather_pass_start_" + group)
        return passes[group]["token"]

    def weights(group, after):
        if group in passes:
            lands = _pass_wait(passes[group], [after], "gather_pass_wait_" + group)
        else:
            after = [after, gathers["ffn_out"]["token"]]
            lands = _gather_pass(_gather_wait(gathers[group], after, "gather_wait_" + group), "gather_pass_" + group)
        if group in ("in_a", "in_b"):
            return (lands[0].reshape(IN_COLS, D // 2),)
        if group == "mix":
            return lands[0], lands[1], lands[2].reshape(D, D)
        return (lands[0],) if group == "ffn_in" else (lands[0].reshape(FFN, D),)

    pairs, scatters = {}, {}

    def scatter(grads, group):
        pairs[group] = _pair_start(grads, "scatter_pair_" + group)
        return pairs[group]["token"]

    def scatter_on(group, after):
        if group in pairs:
            local, theirs = _pair_wait(pairs[group], [after], "scatter_pair_wait_" + group)
        else:
            local, theirs = after, _pair_exchange(after, "scatter_pair_" + group)
        parts = [_pair_add(g, t, "scatter_pair_add_%s_%d" % (group, k)) for k, (g, t) in enumerate(zip(local, theirs))]
        scatters[group] = _chips_start(parts, "scatter_start_" + group)
        return scatters[group]["token"]

    sinks_pad = jnp.pad(attn_sinks, ((0, 0), (0, LANE - ATT_HEADS)))
    loss, grad_x, small, h1, d_proj = _local_step(
        x[0], loss_target[0], _rope_tables(positions), mod, sinks_pad, hg_lower_bounds, hg_norm, g_pre_mix, g_post_mix, g_pre_ffn, g_post_ffn,
        weights, prefetch, scatter, scatter_on)
    loss = lax.psum(loss[0, 0], ("x", "y", "c"))

    sizes = [t.shape[1] for t in small]
    parts = _exchange_small(jnp.concatenate(small, axis=1).reshape(1, 1, sum(sizes)), True, "gather_small_grads")
    dep = parts
    for b_col, half in enumerate(("in_a", "in_b")):
        gw_half = _mm_tn(d_proj, h1, 256, D // 2, BF16, "proj_dw_" + half, deps=[dep], b_col=b_col)
        dep = scatter_on(half, [gw_half.reshape(N_DEV, IN_COLS // N_DEV, D // 2)])
    offs = [sum(sizes[:k]) for k in range(len(sizes))]
    piece = lambda k, n=None: parts[:, :, offs[k]:offs[k] + (sizes[k] if n is None else n)]
    small_w = [(piece(0), b_ada, m_b_ada, v_b_ada), (piece(1), g_pre_mix, m_g_pre_mix, v_g_pre_mix),
               (piece(2), g_post_mix, m_g_post_mix, v_g_post_mix), (piece(3), g_pre_ffn, m_g_pre_ffn, v_g_pre_ffn),
               (piece(4), g_post_ffn, m_g_post_ffn, v_g_post_ffn),
               (piece(5).reshape(N_DEV, 2, HG_W), hg_lower_bounds, m_hg_lower_bounds, v_hg_lower_bounds),
               (piece(6), hg_norm, m_hg_norm, v_hg_norm), (piece(7, ATT_HEADS), attn_sinks, m_attn_sinks, v_attn_sinks)]
    names = ["b_ada", "g_pre_mix", "g_post_mix", "g_pre_ffn", "g_post_ffn", "hg_lower_bounds", "hg_norm", "attn_sinks"]
    res = {n: _adamw(p, w, m, v, "adamw_" + n) for n, (p, w, m, v) in zip(names, small_w)}

    dmod_cols = lax.dynamic_slice(parts.reshape(N_DEV, -1), (0, my_id * n_ada), (N_DEV, n_ada))
    res["w_ada"] = list(_update_w_ada(c_all.T, dmod_cols, w_ada[0], m_w_ada[0], v_w_ada[0], deps=[scatters["in_b"]["token"]]))

    big = {"ffn_out": [("w_ffn_out", w_ffn_out, m_w_ffn_out, v_w_ffn_out)], "ffn_in": [("w_ffn_in", w_ffn_in, m_w_ffn_in, v_w_ffn_in)],
           "mix": [("w_attn_proj", w_attn_proj, m_w_attn_proj, v_w_attn_proj), ("w_hgrn_proj", w_hgrn_proj, m_w_hgrn_proj, v_w_hgrn_proj),
                   ("w_out", w_out, m_w_out, v_w_out)]}
    after = [scatters["in_b"]["token"]]
    for group, members in big.items():
        local, lands = _chips_wait(scatters[group], after, "scatter_wait_" + group)
        for (n, w, m, v), mine, land in zip(members, local, lands):
            res[n] = _adamw(land, w[0], m[0], v[0], "adamw_" + n, own=mine[0])
            after = after + [res[n][1]]
    after = [res[n][1] for n in res]
    halves = [_chips_wait(scatters[half], after, "scatter_wait_" + half) for half in ("in_a", "in_b")]
    own = jnp.concatenate([local[0][0] for local, _ in halves], axis=1)
    land = jnp.concatenate([lands[0] for _, lands in halves], axis=2)
    res["w_in"] = [t.T for t in _adamw(land, w_in[0].T, m_w_in[0].T, v_w_in[0].T, "adamw_w_in", own=own, max_elems=WIDE_TILE)]

    order = ["w_ada", "b_ada", "g_pre_mix", "g_post_mix", "g_pre_ffn", "g_post_ffn", "w_in", "attn_sinks", "w_attn_proj",
             "hg_lower_bounds", "hg_norm", "w_hgrn_proj", "w_out", "w_ffn_in", "w_ffn_out"]
    lead = {"w_ada", "w_in", "w_attn_proj", "w_hgrn_proj", "w_out", "w_ffn_in", "w_ffn_out"}
    outs = [loss, grad_x[None]]
    for k in range(4):
        outs += [res[n][k][None] if n in lead else res[n][k] for n in order]
    return tuple(outs)
```

```python
import functools

import jax
import jax.numpy as jnp
from jax import lax
from jax.experimental import pallas as pl
from jax.experimental.pallas import tpu as pltpu

F32 = jnp.float32
BF16 = jnp.bfloat16

N_DEV = 8
D = 2048
ATT_HEADS = 16
KV_HEADS = 2
HEAD_DIM = 64
GROUP = ATT_HEADS // KV_HEADS
ATT_W = ATT_HEADS * HEAD_DIM
BLK = 128
ROT = HEAD_DIM // 4
ROPE_THETA = 500000.0
HG_HEADS = 8
HG_K = 128
HG_W = HG_HEADS * HG_K
CHUNK = 64
SUB = 16
FFN = 5632
N_MOD = 6
EPS = 1e-6
LANE = 128
Q_A, K_A, V_A, Q_H, F_H, I_H, G_H, GT_A, GT_H, IN_COLS = 0, 1024, 1152, 1280, 2304, 3328, 4352, 5376, 7424, 9472

ADAM_LR, ADAM_B1, ADAM_B2, ADAM_EPS, ADAM_WD, ADAM_STEP = 0.001, 0.9, 0.999, 1e-08, 0.01, 10

TR = 256
HG_TB = 512
VMEM_BIG = 56 << 20
MESH = pl.DeviceIdType.MESH


def _sds(shape, dtype):
    return jax.ShapeDtypeStruct(shape, dtype)


def _params(n_axes, vmem=None):
    return pltpu.CompilerParams(dimension_semantics=("arbitrary",) * n_axes, vmem_limit_bytes=vmem)


def _sig(t):
    return 1.0 / (1.0 + jnp.exp(-t))


def _dot(a, b, dims):
    return lax.dot_general(a, b, (dims, ((), ())), preferred_element_type=F32)


NN = ((1,), (0,))
NT = ((1,), (1,))
TN = ((0,), (0,))


def _matmul(a, b, a_spec, b_spec, o_spec, out_shape, grid, dims, acc_shape, name, deps=(), add=None):
    nk = grid[2]
    nd = len(deps)
    extra = [] if add is None else [add]

    def body(a_ref, b_ref, *rest):
        o_ref, scratch = rest[nd + len(extra)], rest[nd + len(extra) + 1:]
        part = _dot(a_ref[...], b_ref[...], dims)
        if add is not None:
            assert nk == 1
            part = part + rest[nd][...]
        if nk == 1:
            o_ref[...] = part.astype(o_ref.dtype)
        else:
            acc = scratch[0]
            k = pl.program_id(2)

            @pl.when(k == 0)
            def _():
                acc[...] = part

            @pl.when(k > 0)
            def _():
                acc[...] += part

            @pl.when(k == nk - 1)
            def _():
                o_ref[...] = acc[...].astype(o_ref.dtype)

    return pl.pallas_call(
        body, grid=grid, in_specs=[a_spec, b_spec] + [pl.BlockSpec(memory_space=pl.ANY)] * nd + [o_spec] * len(extra),
        out_specs=o_spec, out_shape=out_shape, scratch_shapes=[pltpu.VMEM(acc_shape, F32)] if nk > 1 else [],
        input_output_aliases={2 + nd: 0} if extra else {},
        compiler_params=_params(3, VMEM_BIG), name=name)(a, b, *deps, *extra)


def _mm_nn(a, b, tm, tn, tk, out_dtype, name):
    m, k = a.shape
    n = b.shape[1]
    return _matmul(a, b, pl.BlockSpec((tm, tk), lambda j, i, kk: (i, kk)), pl.BlockSpec((tk, tn), lambda j, i, kk: (kk, j)),
                   pl.BlockSpec((tm, tn), lambda j, i, kk: (i, j)), _sds((m, n), out_dtype),
                   (n // tn, m // tm, k // tk), NN, (tm, tn), name)


def _mm_nn_dm(a, b, tm, out_dtype, name):
    m, k = a.shape
    n = b.shape[2]
    return _matmul(a, b, pl.BlockSpec((tm, k), lambda j, i, kk: (i, 0)), pl.BlockSpec((None, k, n), lambda j, i, kk: (j, 0, 0)),
                   pl.BlockSpec((tm, n), lambda j, i, kk: (i, j)), _sds((m, N_DEV * n), out_dtype),
                   (N_DEV, m // tm, 1), NN, (tm, n), name)


def _mm_nt(a, b, tm, tn, tk, out_dtype, name, deps=(), add=None, a_col=0):
    m = a.shape[0]
    n, k = b.shape
    return _matmul(a, b, pl.BlockSpec((tm, tk), lambda j, i, kk: (i, kk + a_col * (k // tk))), pl.BlockSpec((tn, tk), lambda j, i, kk: (j, kk)),
                   pl.BlockSpec((tm, tn), lambda j, i, kk: (i, j)), _sds((m, n), out_dtype),
                   (n // tn, m // tm, k // tk), NT, (tm, tn), name, deps, add)


def _mm_nt_dm(a, b, tm, tn, out_dtype, name, deps=()):
    m = a.shape[0]
    n_out, n = b.shape[1], b.shape[2]
    return _matmul(a, b, pl.BlockSpec((tm, n), lambda j, i, kk: (i, kk)), pl.BlockSpec((None, tn, n), lambda j, i, kk: (kk, j, 0)),
                   pl.BlockSpec((tm, tn), lambda j, i, kk: (i, j)), _sds((m, n_out), out_dtype),
                   (n_out // tn, m // tm, N_DEV), NT, (tm, tn), name, deps)


def _mm_tn(a, b, tm, tn, out_dtype, name, deps=(), b_col=None):
    s, m = a.shape
    n = b.shape[1] if b_col is None else tn
    first = 0 if b_col is None else b_col
    return _matmul(a, b, pl.BlockSpec((s, tm), lambda j, i, kk: (0, i)), pl.BlockSpec((s, tn), lambda j, i, kk: (0, j + first)),
                   pl.BlockSpec((tm, tn), lambda j, i, kk: (i, j)), _sds((m, n), out_dtype),
                   (n // tn, m // tm, 1), TN, (tm, tn), name, deps)


def _mm_tn_dm(a, b, tm, out_dtype, name):
    s, m = a.shape
    n = b.shape[1] // N_DEV
    return _matmul(a, b, pl.BlockSpec((s, tm), lambda j, i, kk: (0, i)), pl.BlockSpec((s, n), lambda j, i, kk: (0, j)),
                   pl.BlockSpec((None, tm, n), lambda j, i, kk: (j, i, 0)), _sds((N_DEV, m, n), out_dtype),
                   (N_DEV, m // tm, 1), TN, (tm, n), name)


def _row_spec():
    return pl.BlockSpec((TR, D), lambda i: (i, 0))


def _vec_spec(k=0):
    return pl.BlockSpec((1, D), lambda i: (0, k))


def _acc_rows(ref, first, val):
    @pl.when(first)
    def _():
        ref[...] = val

    @pl.when(jnp.logical_not(first))
    def _():
        ref[...] += val


def _pre_fwd(x, g, mod, k_scale, k_shift, name):
    s = x.shape[0]

    def body(x_ref, g_ref, sc_ref, sh_ref, h_ref):
        xv = x_ref[...]
        r = lax.rsqrt(jnp.mean(xv * xv, axis=-1, keepdims=True) + EPS)
        n = xv * r * g_ref[...]
        h_ref[...] = (n * (1.0 + sc_ref[...]) + sh_ref[...]).astype(h_ref.dtype)

    return pl.pallas_call(body, grid=(s // TR,), in_specs=[_row_spec(), _vec_spec(), _vec_spec(k_scale), _vec_spec(k_shift)],
                          out_specs=_row_spec(), out_shape=_sds((s, D), BF16), compiler_params=_params(1), name=name)(x, g, mod, mod)


def _post_fwd(x, y, g, mod, k_gate, name):
    s = x.shape[0]

    def body(x_ref, y_ref, g_ref, gt_ref, o_ref):
        yv = y_ref[...]
        r = lax.rsqrt(jnp.mean(yv * yv, axis=-1, keepdims=True) + EPS)
        o_ref[...] = x_ref[...] + gt_ref[...] * (yv * r * g_ref[...])

    return pl.pallas_call(body, grid=(s // TR,), in_specs=[_row_spec(), _row_spec(), _vec_spec(), _vec_spec(k_gate)],
                          out_specs=_row_spec(), out_shape=_sds((s, D), F32), compiler_params=_params(1), name=name)(x, y, g, mod)


def _post_loss_bwd(x, y, g, mod, k_gate, tgt, name):
    s = x.shape[0]

    def body(x_ref, y_ref, g_ref, gt_ref, t_ref, e_ref, loss_ref, dy_ref, dgt_ref, dg_ref):
        first = pl.program_id(0) == 0
        yv, gv, gate = y_ref[...], g_ref[...], gt_ref[...]
        r = lax.rsqrt(jnp.mean(yv * yv, axis=-1, keepdims=True) + EPS)
        yh = yv * r
        err = x_ref[...] + gate * (yh * gv) - t_ref[...]
        e = err * (1.0 / D)
        e_ref[...] = e
        _acc_rows(loss_ref, first, 0.5 * jnp.sum(jnp.mean(err * err, axis=-1, keepdims=True), axis=0, keepdims=True))
        dn = e * gate
        dgn = dn * gv
        dy_ref[...] = (r * (dgn - yh * jnp.mean(dgn * yh, axis=-1, keepdims=True))).astype(dy_ref.dtype)
        _acc_rows(dgt_ref, first, jnp.sum(e * (yh * gv), axis=0, keepdims=True))
        _acc_rows(dg_ref, first, jnp.sum(dn * yh, axis=0, keepdims=True))

    return pl.pallas_call(body, grid=(s // TR,),
                          in_specs=[_row_spec(), _row_spec(), _vec_spec(), _vec_spec(k_gate), _row_spec()],
                          out_specs=[_row_spec(), pl.BlockSpec((1, 1), lambda i: (0, 0)), _row_spec(), _vec_spec(), _vec_spec()],
                          out_shape=[_sds((s, D), F32), _sds((1, 1), F32), _sds((s, D), BF16), _sds((1, D), F32), _sds((1, D), F32)],
                          compiler_params=_params(1), name=name)(x, y, g, mod, tgt)


def _pre_bwd(dh_parts, x, res, g, mod, k_scale, name):
    s = x.shape[0]
    n_p = len(dh_parts)

    def body(*refs):
        x_ref, res_ref, g_ref, sc_ref, dx_ref, dsh_ref, dsc_ref, dg_ref = refs[n_p:]
        first = pl.program_id(0) == 0
        dh_v = jnp.concatenate([r[...] for r in refs[:n_p]], axis=1)
        xv, gv = x_ref[...], g_ref[...]
        r = lax.rsqrt(jnp.mean(xv * xv, axis=-1, keepdims=True) + EPS)
        xh = xv * r
        dn = dh_v * (1.0 + sc_ref[...])
        dgn = dn * gv
        dx_ref[...] = res_ref[...] + r * (dgn - xh * jnp.mean(dgn * xh, axis=-1, keepdims=True))
        _acc_rows(dsh_ref, first, jnp.sum(dh_v, axis=0, keepdims=True))
        _acc_rows(dsc_ref, first, jnp.sum(dh_v * (xh * gv), axis=0, keepdims=True))
        _acc_rows(dg_ref, first, jnp.sum(dn * xh, axis=0, keepdims=True))

    return pl.pallas_call(body, grid=(s // TR,),
                          in_specs=[pl.BlockSpec((TR, t.shape[1]), lambda i: (i, 0)) for t in dh_parts]
                          + [_row_spec(), _row_spec(), _vec_spec(), _vec_spec(k_scale)],
                          out_specs=[_row_spec(), _vec_spec(), _vec_spec(), _vec_spec()],
                          out_shape=[_sds((s, D), F32)] + [_sds((1, D), F32)] * 3,
                          compiler_params=_params(1), name=name)(*dh_parts, x, res, g, mod)


def _post_bwd(dx, y, g, mod, k_gate, name):
    s = y.shape[0]

    def body(dx_ref, y_ref, g_ref, gt_ref, dy_ref, dgt_ref, dg_ref):
        first = pl.program_id(0) == 0
        yv, dxv, gv = y_ref[...], dx_ref[...], g_ref[...]
        r = lax.rsqrt(jnp.mean(yv * yv, axis=-1, keepdims=True) + EPS)
        yh = yv * r
        dn = dxv * gt_ref[...]
        dgn = dn * gv
        dy_ref[...] = (r * (dgn - yh * jnp.mean(dgn * yh, axis=-1, keepdims=True))).astype(dy_ref.dtype)
        _acc_rows(dgt_ref, first, jnp.sum(dxv * (yh * gv), axis=0, keepdims=True))
        _acc_rows(dg_ref, first, jnp.sum(dn * yh, axis=0, keepdims=True))

    return pl.pallas_call(body, grid=(s // TR,), in_specs=[_row_spec(), _row_spec(), _vec_spec(), _vec_spec(k_gate)],
                          out_specs=[_row_spec(), _vec_spec(), _vec_spec()],
                          out_shape=[_sds((s, D), BF16), _sds((1, D), F32), _sds((1, D), F32)],
                          compiler_params=_params(1), name=name)(dx, y, g, mod)


SW_TN = 1408
SW_TR = 512
TALL = 2048


def _swiglu_fwd(gu, deps=()):
    s = gu.shape[0]
    nb = FFN // SW_TN

    def body(g_ref, u_ref, *rest):
        a_ref = rest[len(deps)]
        gv = g_ref[...]
        a_ref[...] = (gv * _sig(gv) * u_ref[...]).astype(a_ref.dtype)

    return pl.pallas_call(body, grid=(s // SW_TR, nb),
                          in_specs=[pl.BlockSpec((SW_TR, SW_TN), lambda i, j: (i, j)), pl.BlockSpec((SW_TR, SW_TN), lambda i, j: (i, j + nb))]
                          + [pl.BlockSpec(memory_space=pl.ANY)] * len(deps),
                          out_specs=pl.BlockSpec((SW_TR, SW_TN), lambda i, j: (i, j)), out_shape=_sds((s, FFN), BF16),
                          compiler_params=_params(2, 48 << 20), name="swiglu_fwd")(gu, gu, *deps)


def _swiglu_bwd(dact, gu):
    s = gu.shape[0]
    nb = FFN // SW_TN
    n_steps = (s // SW_TR) * nb

    def body(da_ref, g_ref, u_ref, o_ref, buf, sems):
        i, j = pl.program_id(0), pl.program_id(1)
        step = i * nb + j
        slot = step % 2

        def tiles(sl):
            rows = pl.ds(pl.multiple_of(i * SW_TR, SW_TR), SW_TR)
            return [pltpu.make_async_copy(buf.at[sl, h], o_ref.at[rows, pl.ds(pl.multiple_of((j + nb * h) * SW_TN, LANE), SW_TN)], sems.at[sl, h])
                    for h in range(2)]

        @pl.when(step >= 2)
        def _():
            for cp in tiles(slot):
                cp.wait()

        gv, da = g_ref[...], da_ref[...]
        sg = _sig(gv)
        buf[slot, 0] = (da * u_ref[...] * (sg * (1.0 + gv * (1.0 - sg)))).astype(buf.dtype)
        buf[slot, 1] = (da * (gv * sg)).astype(buf.dtype)
        for cp in tiles(slot):
            cp.start()

        @pl.when(step == n_steps - 1)
        def _():
            for cp in tiles(slot) + (tiles(1 - slot) if n_steps > 1 else []):
                cp.wait()

    blk = lambda f: pl.BlockSpec((SW_TR, SW_TN), f)
    return pl.pallas_call(body, grid=(s // SW_TR, nb),
                          in_specs=[blk(lambda i, j: (i, j)), blk(lambda i, j: (i, j)), blk(lambda i, j: (i, j + nb))],
                          out_specs=pl.BlockSpec(memory_space=pl.ANY), out_shape=_sds((s, 2 * FFN), BF16),
                          scratch_shapes=[pltpu.VMEM((2, 2, SW_TR, SW_TN), BF16), pltpu.SemaphoreType.DMA((2, 2))],
                          compiler_params=_params(2, 48 << 20), name="swiglu_bwd")(dact, gu, gu)


MG_TN = 256


def _merge_fwd(y_a, y_h, proj):
    s = y_a.shape[0]
    tn = MG_TN
    ba, bh = GT_A // tn, GT_H // tn

    def body(ya_ref, yh_ref, ga_ref, gh_ref, m_ref):
        m_ref[...] = (_sig(ga_ref[...]) * ya_ref[...] + _sig(gh_ref[...]) * yh_ref[...]).astype(m_ref.dtype)

    tr = min(s, TALL)
    blk = lambda f: pl.BlockSpec((tr, tn), f)
    return pl.pallas_call(body, grid=(s // tr, D // tn),
                          in_specs=[blk(lambda i, j: (i, j)), blk(lambda i, j: (i, j)), blk(lambda i, j: (i, j + ba)), blk(lambda i, j: (i, j + bh))],
                          out_specs=blk(lambda i, j: (i, j)), out_shape=_sds((s, D), BF16),
                          compiler_params=_params(2), name="merge_fwd")(y_a, y_h, proj, proj)


def _merge_bwd(dm, y_a, y_h, proj):
    s = y_a.shape[0]
    tn = MG_TN
    ba, bh = GT_A // tn, GT_H // tn

    def body(dm_ref, ya_ref, yh_ref, ga_ref, gh_ref, dya_ref, dyh_ref, dga_ref, dgh_ref):
        dmv = dm_ref[...]
        sa, sh = _sig(ga_ref[...]), _sig(gh_ref[...])
        dya_ref[...] = (dmv * sa).astype(BF16)
        dyh_ref[...] = (dmv * sh).astype(BF16)
        dga_ref[...] = (dmv * ya_ref[...] * (sa * (1.0 - sa))).astype(BF16)
        dgh_ref[...] = (dmv * yh_ref[...] * (sh * (1.0 - sh))).astype(BF16)

    tr = min(s, TALL)
    blk = lambda f: pl.BlockSpec((tr, tn), f)
    nat = blk(lambda i, j: (i, j))
    return pl.pallas_call(body, grid=(s // tr, D // tn),
                          in_specs=[nat, nat, nat, blk(lambda i, j: (i, j + ba)), blk(lambda i, j: (i, j + bh))],
                          out_specs=[nat] * 4, out_shape=[_sds((s, D), BF16)] * 4,
                          compiler_params=_params(2), name="merge_bwd")(dm, y_a, y_h, proj, proj)


def _hgout_fwd(o_raw, proj, hg_norm):
    s = o_raw.shape[0]
    bg = G_H // LANE

    def body(o_ref, g_ref, n_ref, out_ref):
        ov = o_ref[...]
        r = lax.rsqrt(jnp.mean(ov * ov, axis=-1, keepdims=True) + EPS)
        out_ref[...] = (ov * r * n_ref[...] * _sig(g_ref[...])).astype(out_ref.dtype)

    tr = min(s, TALL)
    blk = lambda f: pl.BlockSpec((tr, LANE), f)
    return pl.pallas_call(body, grid=(s // tr, HG_HEADS),
                          in_specs=[blk(lambda i, h: (i, h)), blk(lambda i, h: (i, h + bg)), pl.BlockSpec((1, LANE), lambda i, h: (0, 0))],
                          out_specs=blk(lambda i, h: (i, h)), out_shape=_sds((s, HG_W), BF16),
                          compiler_params=_params(2), name="hgout_fwd")(o_raw, proj, hg_norm)


def _hgout_bwd(d_out, o_raw, proj, hg_norm):
    s = o_raw.shape[0]
    bg = G_H // LANE

    def body(d_ref, o_ref, g_ref, n_ref, do_ref, dg_ref, dn_ref):
        first = jnp.logical_and(pl.program_id(0) == 0, pl.program_id(1) == 0)
        ov, dv, nv = o_ref[...], d_ref[...], n_ref[...]
        sg = _sig(g_ref[...])
        r = lax.rsqrt(jnp.mean(ov * ov, axis=-1, keepdims=True) + EPS)
        oh = ov * r
        d_on = dv * sg
        dg_ref[...] = (dv * (oh * nv) * (sg * (1.0 - sg))).astype(dg_ref.dtype)
        t = d_on * nv
        do_ref[...] = r * (t - oh * jnp.mean(t * oh, axis=-1, keepdims=True))
        _acc_rows(dn_ref, first, jnp.sum(d_on * oh, axis=0, keepdims=True))

    tr = min(s, TALL)
    blk = lambda f: pl.BlockSpec((tr, LANE), f)
    vec = pl.BlockSpec((1, LANE), lambda i, h: (0, 0))
    return pl.pallas_call(body, grid=(s // tr, HG_HEADS),
                          in_specs=[blk(lambda i, h: (i, h)), blk(lambda i, h: (i, h)), blk(lambda i, h: (i, h + bg)), vec],
                          out_specs=[blk(lambda i, h: (i, h)), blk(lambda i, h: (i, h)), vec],
                          out_shape=[_sds((s, HG_W), F32), _sds((s, HG_W), BF16), _sds((1, LANE), F32)],
                          compiler_params=_params(2), name="hgout_bwd")(d_out, o_raw, proj, hg_norm)


def _rope(t, cos, s_lo, s_hi):
    return t * cos + pltpu.roll(t, LANE - ROT // 2, 1) * s_lo + pltpu.roll(t, ROT // 2, 1) * s_hi


def _rope_wide(t, cos, s_lo, s_hi):
    return jnp.concatenate([_rope(t[:, k * LANE:(k + 1) * LANE], cos, s_lo, s_hi) for k in range(t.shape[1] // LANE)], axis=1)


def _attn_mask(has_prev):
    kj = lax.broadcasted_iota(jnp.int32, (2 * BLK, BLK), 0)
    qi = lax.broadcasted_iota(jnp.int32, (2 * BLK, BLK), 1)
    rel = BLK + qi - kj
    band = jnp.logical_and(rel >= 0, rel < BLK)
    return jnp.logical_and(band, jnp.logical_or(has_prev, kj >= BLK))


def _attn_specs():
    prev = lambda i: jnp.maximum(i - 1, 0)
    kb, vb = K_A // LANE, V_A // LANE
    blk = lambda f: pl.BlockSpec((BLK, LANE), f)
    tabs = [blk(lambda i: (i, 0))] * 3 + [blk(lambda i: (prev(i), 0))] * 3
    return [pl.BlockSpec((BLK, ATT_W), lambda i: (i, 0)), blk(lambda i: (i, kb)), blk(lambda i: (prev(i), kb)),
            blk(lambda i: (i, vb)), blk(lambda i: (prev(i), vb))] + tabs + [pl.BlockSpec((1, LANE), lambda i: (0, 0))]


def _attn_logits(qh, kg):
    return _dot(kg, qh, NT)


def _attn_probs(raw, mask, sk):
    logits = jnp.where(mask, raw * (HEAD_DIM ** -0.5), -jnp.inf)
    m = jnp.maximum(jnp.max(logits, axis=0, keepdims=True), sk)
    p = jnp.exp(logits - m)
    e_sink = jnp.exp(sk - m)
    inv = 1.0 / (jnp.sum(p, axis=0, keepdims=True) + e_sink)
    return p, inv, e_sink * inv


def _attn_fwd(proj, tabs, sinks):
    s = proj.shape[0]

    def body(q_ref, kc_ref, kp_ref, vc_ref, vp_ref, c0, l0, h0, c1, l1, h1, sk_ref, o_ref):
        i = pl.program_id(0)
        mask = _attn_mask(i > 0)
        q = _rope_wide(q_ref[...], c0[...], l0[...], h0[...]).astype(BF16)
        kk = jnp.concatenate([_rope(kp_ref[...], c1[...], l1[...], h1[...]), _rope(kc_ref[...], c0[...], l0[...], h0[...])], axis=0).astype(BF16)
        v_t = jnp.concatenate([vp_ref[...], vc_ref[...]], axis=0).T.astype(BF16)
        part = lambda t, h: t[:, h * HEAD_DIM:(h + 1) * HEAD_DIM]
        k_heads = [part(kk, g) for g in range(KV_HEADS)]

        def head(h):
            g = h // GROUP
            raw = _attn_logits(part(q, h), k_heads[g])
            yield
            p, inv, _ = _attn_probs(raw, mask, sk_ref[:, h:h + 1])
            yield
            out_t = _dot(v_t[g * HEAD_DIM:(g + 1) * HEAD_DIM], p.astype(BF16), NN)
            yield
            return out_t * inv

        o_ref[...] = jnp.concatenate(_interleave([head(h) for h in range(ATT_HEADS)]), axis=0).T.astype(o_ref.dtype)

    return pl.pallas_call(body, grid=(s // BLK,), in_specs=_attn_specs(),
                          out_specs=pl.BlockSpec((BLK, ATT_W), lambda i: (i, 0)), out_shape=_sds((s, ATT_W), BF16),
                          compiler_params=_params(1), name="attn_fwd")(proj, proj, proj, proj, proj, *tabs, *tabs, sinks)


def _attn_bwd(proj, tabs, sinks, d_att):
    s = proj.shape[0]

    def body(q_ref, kc_ref, kp_ref, vc_ref, vp_ref, c0, l0, h0, c1, l1, h1, sk_ref, do_ref, dq_ref, dk_ref, dv_ref, ds_ref):
        i = pl.program_id(0)

        @pl.when(i == 0)
        def _():
            dk_ref[...] = jnp.zeros_like(dk_ref)
            dv_ref[...] = jnp.zeros_like(dv_ref)
            ds_ref[...] = jnp.zeros_like(ds_ref)

        mask = _attn_mask(i > 0)
        q = _rope_wide(q_ref[...], c0[...], l0[...], h0[...]).astype(BF16)
        kk = jnp.concatenate([_rope(kp_ref[...], c1[...], l1[...], h1[...]), _rope(kc_ref[...], c0[...], l0[...], h0[...])], axis=0).astype(BF16)
        k_f32 = jnp.concatenate([_rope(kp_ref[...], c1[...], l1[...], h1[...]), _rope(kc_ref[...], c0[...], l0[...], h0[...])], axis=0)
        k_t = k_f32.T.astype(BF16)
        vv = jnp.concatenate([vp_ref[...], vc_ref[...]], axis=0).astype(BF16)
        d_o = do_ref[...].astype(BF16)
        lane = lax.broadcasted_iota(jnp.int32, (1, LANE), 1)
        part = lambda t, h: t[:, h * HEAD_DIM:(h + 1) * HEAD_DIM]
        k_heads = [part(kk, g) for g in range(KV_HEADS)]
        v_heads = [part(vv, g) for g in range(KV_HEADS)]

        def head(h):
            g = h // GROUP
            qh, doh = part(q, h), part(d_o, h)
            raw = _attn_logits(qh, k_heads[g])
            d_p = _dot(v_heads[g], doh, NT)
            yield
            p, inv, p_sink = _attn_probs(raw, mask, sk_ref[:, h:h + 1])
            prob = p * inv
            dv = _dot(prob.astype(BF16), doh, NN)
            yield
            dd = jnp.sum(prob * d_p, axis=0, keepdims=True)
            d_s = (prob * (d_p - dd)).astype(BF16)
            d_sink = jnp.where(lane == h, -jnp.sum(p_sink * dd, axis=1, keepdims=True), 0.0)
            dq_t = _dot(k_t[g * HEAD_DIM:(g + 1) * HEAD_DIM], d_s, NN)
            dk = _dot(d_s, qh, NN)
            yield
            return dq_t * (HEAD_DIM ** -0.5), dk * (HEAD_DIM ** -0.5), dv, d_sink

        per_head = _interleave([head(h) for h in range(ATT_HEADS)])
        dqs = [jnp.concatenate([t[0] for t in per_head], axis=0).T]
        group_sum = lambda k, g: functools.reduce(jnp.add, [t[k] for t in per_head[g * GROUP:(g + 1) * GROUP]])
        dks = [group_sum(1, g) for g in range(KV_HEADS)]
        dvs = [group_sum(2, g) for g in range(KV_HEADS)]
        d_sink = functools.reduce(jnp.add, [t[3] for t in per_head])
        dq_ref[...] = _rope_wide(jnp.concatenate(dqs, axis=1), c0[...], -l0[...], -h0[...]).astype(dq_ref.dtype)
        d_k = jnp.concatenate(dks, axis=1)
        d_v = jnp.concatenate(dvs, axis=1)
        cur = pl.ds(pl.multiple_of(i * BLK, BLK), BLK)
        prv = pl.ds(pl.multiple_of(jnp.maximum(i - 1, 0) * BLK, BLK), BLK)
        dk_ref[prv, :] += _rope(d_k[:BLK], c1[...], -l1[...], -h1[...])
        dk_ref[cur, :] += _rope(d_k[BLK:], c0[...], -l0[...], -h0[...])
        dv_ref[prv, :] += d_v[:BLK]
        dv_ref[cur, :] += d_v[BLK:]
        ds_ref[...] += d_sink

    full = pl.BlockSpec((s, LANE), lambda i: (0, 0))
    return pl.pallas_call(body, grid=(s // BLK,), in_specs=_attn_specs() + [pl.BlockSpec((BLK, ATT_W), lambda i: (i, 0))],
                          out_specs=[pl.BlockSpec((BLK, ATT_W), lambda i: (i, 0)), full, full, pl.BlockSpec((1, LANE), lambda i: (0, 0))],
                          out_shape=[_sds((s, ATT_W), BF16), _sds((s, LANE), F32), _sds((s, LANE), F32), _sds((1, LANE), F32)],
                          compiler_params=_params(1), name="attn_bwd")(proj, proj, proj, proj, proj, *tabs, *tabs, sinks, d_att)


def _tri_matmul(tri, t):
    hi = t.astype(BF16)
    r1 = t - hi.astype(F32)
    mid = r1.astype(BF16)
    lo = (r1 - mid.astype(F32)).astype(BF16)
    return _dot(tri, hi, NN) + _dot(tri, mid, NN) + _dot(tri, lo, NN)


def _lower_bound(hl):
    a, b = hl[0:1, :], hl[1:2, :]
    mx = jnp.maximum(a, b)
    ea, eb = jnp.exp(a - mx), jnp.exp(b - mx)
    return ea / (ea + eb)


def _hg_gates(q_raw, f_raw, lb, tri_lower):
    sg = _sig(f_raw)
    f = lb + (1.0 - lb) * sg
    sq = _sig(q_raw)
    b = _tri_matmul(tri_lower, jnp.log(f))
    return sg, f, 1.0 - f, sq, q_raw * sq, b


HG_PAIR_FWD = 8
HG_PAIR_BWD = 8


def _hg_specs(n_map, pair):
    blk = lambda off, p: pl.BlockSpec((HG_TB, LANE), lambda h, n: (n_map(n), off // LANE + pair * h + p))
    return [blk(off, p) for off in (Q_H, F_H, I_H) for p in range(pair)] + [pl.BlockSpec((2, pair * LANE), lambda h, n: (0, h))]


def _interleave(gens):
    out = [None] * len(gens)
    live = list(range(len(gens)))
    while live:
        for k in list(live):
            try:
                next(gens[k])
            except StopIteration as stop:
                out[k] = stop.value
                live.remove(k)
    return out


def _hg_spread():
    c = lax.broadcasted_iota(jnp.int32, (CHUNK, SUB * SUB), 0)
    l = lax.broadcasted_iota(jnp.int32, (CHUNK, SUB * SUB), 1)
    r = lax.broadcasted_iota(jnp.int32, (SUB, SUB * SUB), 0)
    lr = lax.broadcasted_iota(jnp.int32, (SUB, SUB * SUB), 1)
    shift = SUB.bit_length() - 1
    cols = [(c == lo + (l >> shift)).astype(BF16) for lo in range(0, CHUNK, SUB)]
    tile = [(c == lo + (l & (SUB - 1))).astype(BF16) for lo in range(0, CHUNK, SUB)]
    return cols, tile, (lr & (SUB - 1)) == r, (lr >> shift) == r


def _hg_intra(qs, kk, b, grad=None):
    lane = lax.broadcasted_iota(jnp.int32, (SUB, CHUNK), 1)
    row1 = lax.broadcasted_iota(jnp.int32, (SUB, 1), 0)
    kk_b = kk.astype(BF16)
    if grad is not None:
        d_a, d_at, (cols, tile, diag, block) = grad
    a_blocks, dq_blocks, dk_blocks, db_blocks = [], [], [], []
    dk_left = None
    for j in range(CHUNK // SUB):
        lo = j * SUB
        q_j, k_j, b_j = qs[lo:lo + SUB], kk[lo:lo + SUB], b[lo:lo + SUB]
        es = [jnp.where(row1 >= sx, jnp.exp(jnp.minimum(b_j - b_j[sx:sx + 1], 0.0)), 0.0) for sx in range(SUB)]
        pes = [q_j * e for e in es]
        pe = jnp.concatenate(pes, axis=0).astype(BF16)
        pairs = _dot(pe, kk_b, NT)
        yield
        a_j = jnp.zeros((SUB, CHUNK), F32)
        for sx in range(SUB):
            a_j = jnp.where(lane == lo + sx, pairs[sx * SUB:(sx + 1) * SUB], a_j)
        if grad is not None:
            da_j = d_a[lo:lo + SUB]
            ek = jnp.concatenate([e * k_j[sx:sx + 1] for sx, e in enumerate(es)], axis=0).astype(BF16)
            sel_t = jnp.where(diag, _dot(da_j.astype(BF16), cols[j], NN), 0.0).astype(BF16)
            sel_s = jnp.where(block, _dot(d_at[lo:lo + SUB].astype(BF16), tile[j], NN), 0.0).astype(BF16)
            pek = jnp.concatenate([p * k_j[sx:sx + 1] for sx, p in enumerate(pes)], axis=0).astype(BF16)
            yield
            dq_j = _dot(sel_t, ek, NN)
            dk_j = _dot(sel_s, pe, NN)
            db_j = _dot(sel_t, pek, NN) - _dot(sel_s, pek, NN)
            yield
        if j > 0:
            ref = b[lo - 1:lo]
            sc_q = jnp.exp(b_j - ref)
            sc_k = jnp.exp(jnp.minimum(ref - b, 0.0))
            qt = (q_j * sc_q).astype(BF16)
            kt = (kk * sc_k).astype(BF16)
            left = _dot(qt, kt, NT)
            yield
            a_j = a_j + jnp.where(lane < lo, left, 0.0)
            if grad is not None:
                da_left = jnp.where(lane < lo, da_j, 0.0).astype(BF16)
                dq_left = _dot(da_left, kt, NN) * sc_q
                dq_j = dq_j + dq_left
                db_j = db_j + q_j * dq_left
                t = _dot(da_left, qt, TN)
                yield
                t = t * sc_k
                dk_left = t if dk_left is None else dk_left + t
        a_blocks.append(a_j)
        if grad is not None:
            dq_blocks.append(dq_j)
            dk_blocks.append(dk_j)
            db_blocks.append(db_j)
    a = jnp.concatenate(a_blocks, axis=0)
    if grad is None:
        return a
    return a, jnp.concatenate(dq_blocks, axis=0), jnp.concatenate(dk_blocks, axis=0) + dk_left, jnp.concatenate(db_blocks, axis=0) - kk * dk_left


def _hgrn_fwd(proj, hl):
    s = proj.shape[0]
    n_chunk = HG_TB // CHUNK
    pair = HG_PAIR_FWD

    def body(*refs):
        q_refs, f_refs, i_refs = refs[:pair], refs[pair:2 * pair], refs[2 * pair:3 * pair]
        hl_ref, o_ref, st_out_ref, st_ref = refs[3 * pair:]

        @pl.when(pl.program_id(1) == 0)
        def _():
            st_ref[...] = jnp.zeros_like(st_ref)

        r_i = lax.broadcasted_iota(jnp.int32, (CHUNK, CHUNK), 0)
        c_i = lax.broadcasted_iota(jnp.int32, (CHUNK, CHUNK), 1)
        tri_lower = (r_i >= c_i).astype(BF16)

        def chunk(c, carry):
            rows = pl.ds(pl.multiple_of(c * CHUNK, CHUNK), CHUNK)
            def head(p):
                cols = slice(p * LANE, (p + 1) * LANE)
                lb = _lower_bound(hl_ref[:, cols])
                v = i_refs[p][rows, :].astype(BF16)
                _, _, kk, _, qs, b = _hg_gates(q_refs[p][rows, :], f_refs[p][rows, :], lb, tri_lower)
                yield
                st = st_ref[p]
                st_b = st.astype(BF16)
                st_out_ref[p, c] = st_b
                o_state = _dot((qs * jnp.exp(b)).astype(BF16), st_b, NT)
                b_last = b[CHUNK - 1:CHUNK, :]
                st_new = _dot(v, (kk * jnp.exp(b_last - b)).astype(BF16), TN)
                a = yield from _hg_intra(qs, kk, b)
                st_ref[p] = st * jnp.exp(b_last) + st_new
                o_ref[rows, cols] = o_state + _dot(a.astype(BF16), v, NN)

            _interleave([head(p) for p in range(pair)])
            return carry

        lax.fori_loop(0, n_chunk, chunk, 0)

    return pl.pallas_call(
        body, grid=(HG_HEADS // pair, s // HG_TB), in_specs=_hg_specs(lambda n: n, pair),
        out_specs=[pl.BlockSpec((HG_TB, pair * LANE), lambda h, n: (n, h)), pl.BlockSpec((pair, n_chunk, HG_K, HG_K), lambda h, n: (h, n, 0, 0))],
        out_shape=[_sds((s, HG_W), F32), _sds((HG_HEADS, s // CHUNK, HG_K, HG_K), BF16)],
        scratch_shapes=[pltpu.VMEM((pair, HG_K, HG_K), F32)],
        compiler_params=_params(2), name="hgrn_fwd")(*[proj] * (3 * pair), hl)


def _hgrn_bwd(proj, hl, states, d_o):
    s = proj.shape[0]
    n_chunk = HG_TB // CHUNK
    n_blk = s // HG_TB
    pair = HG_PAIR_BWD
    rev = lambda n: n_blk - 1 - n

    def body(*refs):
        q_refs, f_refs, i_refs = refs[:pair], refs[pair:2 * pair], refs[2 * pair:3 * pair]
        hl_ref, st_in_ref, do_ref, dq_ref, df_ref, di_ref, dhl_ref, dst_ref, dlb_ref = refs[3 * pair:]
        n = pl.program_id(1)

        @pl.when(n == 0)
        def _():
            dst_ref[...] = jnp.zeros_like(dst_ref)
            dlb_ref[...] = jnp.zeros_like(dlb_ref)

        r_i = lax.broadcasted_iota(jnp.int32, (CHUNK, CHUNK), 0)
        c_i = lax.broadcasted_iota(jnp.int32, (CHUNK, CHUNK), 1)
        tri_lower = (r_i >= c_i).astype(BF16)
        tri_upper = (r_i <= c_i).astype(BF16)
        row = lax.broadcasted_iota(jnp.int32, (CHUNK, 1), 0)
        spread = _hg_spread()

        def chunk(cc, carry):
            c = n_chunk - 1 - cc
            rows = pl.ds(pl.multiple_of(c * CHUNK, CHUNK), CHUNK)
            def head(p):
                cols = slice(p * LANE, (p + 1) * LANE)
                lb = _lower_bound(hl_ref[:, cols])
                q_raw = q_refs[p][rows, :]
                vb = i_refs[p][rows, :].astype(BF16)
                sg, f, kk, sq, qs, b = _hg_gates(q_raw, f_refs[p][rows, :], lb, tri_lower)
                yield
                e_b = jnp.exp(b)
                qe = qs * e_b
                b_last = b[CHUNK - 1:CHUNK, :]
                e_last = jnp.exp(b_last)
                e_kd = jnp.exp(b_last - b)
                kd = kk * e_kd
                st0 = st_in_ref[p, c]
                d_ob = do_ref[rows, cols].astype(BF16)
                dst = dst_ref[p]
                dst_b = dst.astype(BF16)
                d_a = jnp.where(r_i >= c_i, _dot(d_ob, vb, NT), 0.0)
                d_at = jnp.where(r_i <= c_i, _dot(vb, d_ob, NT), 0.0)
                d_v_st = _dot(kd.astype(BF16), dst_b, NT)
                d_kd = _dot(vb, dst_b, NN)
                d_qe = _dot(d_ob, st0, NN)
                dst_new = _dot(d_ob, qe.astype(BF16), TN)
                yield
                a, dqs, dkk, d_b = yield from _hg_intra(qs, kk, b, (d_a, d_at, spread))
                d_v = _dot(a.astype(BF16), d_ob, TN) + d_v_st
                dqs_st = d_qe * e_b
                dkk_st = d_kd * e_kd
                dqs = dqs + dqs_st
                dkk = dkk + dkk_st
                d_b_last = jnp.sum(d_kd * kd, axis=0, keepdims=True) + jnp.sum(dst * st0.astype(F32), axis=0, keepdims=True) * e_last
                d_b = d_b + qs * dqs_st - kk * dkk_st + jnp.where(row == CHUNK - 1, d_b_last, 0.0)
                d_g = _tri_matmul(tri_upper, d_b)
                dst_ref[p] = dst_new + dst * e_last
                yield
                d_f = d_g / f - dkk
                dlb_ref[:, cols] += jnp.sum(d_f * (1.0 - sg), axis=0, keepdims=True)
                dq_ref[rows, cols] = (dqs * (sq * (1.0 + q_raw * (1.0 - sq)))).astype(dq_ref.dtype)
                df_ref[rows, cols] = (d_f * (1.0 - lb) * (sg * (1.0 - sg))).astype(df_ref.dtype)
                di_ref[rows, cols] = d_v.astype(di_ref.dtype)

            _interleave([head(p) for p in range(pair)])
            return carry

        lax.fori_loop(0, n_chunk, chunk, 0)

        @pl.when(n == n_blk - 1)
        def _():
            lb = _lower_bound(hl_ref[...])
            d_hl0 = dlb_ref[...] * (lb * (1.0 - lb))
            dhl_ref[...] = jnp.concatenate([d_hl0, -d_hl0], axis=0)

    out_blk = pl.BlockSpec((HG_TB, pair * LANE), lambda h, n: (rev(n), h))
    return pl.pallas_call(
        body, grid=(HG_HEADS // pair, n_blk),
        in_specs=_hg_specs(rev, pair) + [pl.BlockSpec((pair, n_chunk, HG_K, HG_K), lambda h, n: (h, rev(n), 0, 0)), out_blk],
        out_specs=[out_blk, out_blk, out_blk, pl.BlockSpec((2, pair * LANE), lambda h, n: (0, h))],
        out_shape=[_sds((s, HG_W), BF16)] * 3 + [_sds((2, HG_W), F32)],
        scratch_shapes=[pltpu.VMEM((pair, HG_K, HG_K), F32), pltpu.VMEM((1, pair * LANE), F32)],
        compiler_params=_params(2), name="hgrn_bwd")(*[proj] * (3 * pair), hl, states, d_o)


def _mod_part(c_all, w_shard, b_shard):
    n = w_shard.shape[1]
    tn = 512

    def body(c_ref, w_ref, b_ref, o_ref):
        o_ref[...] = _dot(c_ref[...].astype(BF16), w_ref[...].astype(BF16), NN) + b_ref[...]

    return pl.pallas_call(body, grid=(n // tn,),
                          in_specs=[pl.BlockSpec((N_DEV, D), lambda j: (0, 0)), pl.BlockSpec((D, tn), lambda j: (0, j)), pl.BlockSpec((1, tn), lambda j: (0, j))],
                          out_specs=pl.BlockSpec((N_DEV, tn), lambda j: (0, j)), out_shape=_sds((N_DEV, n), F32),
                          compiler_params=_params(1, 32 << 20), name="mod_part")(c_all, w_shard, b_shard)


def _adam_math(g, w, m, v):
    c1 = 1.0 / (1.0 - ADAM_B1 ** ADAM_STEP)
    c2 = 1.0 / (1.0 - ADAM_B2 ** ADAM_STEP)
    m2 = ADAM_B1 * m + (1.0 - ADAM_B1) * g
    v2 = ADAM_B2 * v + (1.0 - ADAM_B2) * (g * g)
    return -ADAM_LR * ((m2 * c1) / (jnp.sqrt(v2 * c2) + ADAM_EPS) + ADAM_WD * w), m2, v2


def _update_w_ada(c_all_t, dmod_cols, w, m, v, deps=()):
    n = dmod_cols.shape[1]
    tn = 256

    def body(c_ref, d_ref, w_ref, m_ref, v_ref, *rest):
        g_ref, dl_ref, m2_ref, v2_ref = rest[len(deps):]
        cv = c_ref[...].astype(BF16).astype(F32)
        dv = d_ref[...].astype(BF16).astype(F32)
        g = cv[:, 0:1] * dv[0:1, :]
        for k in range(1, N_DEV):
            g = g + cv[:, k:k + 1] * dv[k:k + 1, :]
        g_ref[...] = g
        dl_ref[...], m2_ref[...], v2_ref[...] = _adam_math(g, w_ref[...], m_ref[...], v_ref[...])

    blk = pl.BlockSpec((D, tn), lambda j: (0, j))
    return pl.pallas_call(body, grid=(n // tn,),
                          in_specs=[pl.BlockSpec((D, N_DEV), lambda j: (0, 0)), pl.BlockSpec((N_DEV, tn), lambda j: (0, j)), blk, blk, blk]
                          + [pl.BlockSpec(memory_space=pl.ANY)] * len(deps),
                          out_specs=[blk] * 4, out_shape=[_sds((D, n), F32)] * 4,
                          compiler_params=_params(1, 48 << 20), name="adamw_w_ada")(c_all_t, dmod_cols, w, m, v, *deps)


def _row_tile(r, c, max_elems=1 << 18):
    if r * c <= max_elems or r % 8:
        return r
    best = 8
    for t in range(8, r + 1, 8):
        if r % t == 0 and t * c <= max_elems:
            best = t
    return best


WIDE_TILE = 5 << 17
PAIR_TILE = 3 << 19


def _adamw(pieces, w, m, v, name, own=None, max_elems=1 << 18):
    parts = list(pieces) if isinstance(pieces, (list, tuple)) else [pieces]
    owns = [] if own is None else (list(own) if isinstance(own, (list, tuple)) else [own])
    n_o, n_p = len(owns), len(parts)
    p, r = parts[0].shape[:2]
    c = sum(t.shape[2] for t in parts)
    tr = _row_tile(r, c, max_elems)

    def body(*refs):
        w_ref, m_ref, v_ref, *outs = refs[n_o + n_p:]
        cols = []
        for j, p_ref in enumerate(refs[n_o:n_o + n_p]):
            g = p_ref[0].astype(F32)
            if owns:
                g = refs[j][...].astype(F32) + g
            for k in range(1, p):
                g = g + p_ref[k].astype(F32)
            cols.append(g)
        g = cols[0] if n_p == 1 else jnp.concatenate(cols, axis=1)
        outs[0][...] = g
        outs[1][...], outs[2][...], outs[3][...] = _adam_math(g, w_ref[...], m_ref[...], v_ref[...])

    blk = pl.BlockSpec((tr, c), lambda i: (i, 0))
    in_specs = ([pl.BlockSpec((tr, t.shape[1]), lambda i: (i, 0)) for t in owns]
                + [pl.BlockSpec((p, tr, t.shape[2]), lambda i: (0, i, 0)) for t in parts] + [blk, blk, blk])
    return pl.pallas_call(body, grid=(r // tr,), in_specs=in_specs,
                          out_specs=[blk] * 4, out_shape=[_sds((r, c), F32)] * 4,
                          compiler_params=_params(1, 48 << 20), name=name)(*owns, *parts, w, m, v)


def _my_coords():
    return lax.axis_index("x"), lax.axis_index("y"), lax.axis_index("c")


def _flip(coords, k):
    x, y, c = coords
    return (1 - x if k & 4 else x, 1 - y if k & 2 else y, 1 - c if k & 1 else c)


def _lin(coords):
    return 4 * coords[0] + 2 * coords[1] + coords[2]


def _exchange_small(x3, bcast, name):
    n = x3.shape[2]

    def body(x_ref, o_ref, send_sems, recv_sems):
        me = _my_coords()
        my_id = _lin(me)
        o_ref[pl.ds(my_id, 1)] = x_ref[pl.ds(0 if bcast else my_id, 1)]
        copies = []
        for k in range(1, N_DEV):
            peer = _flip(me, k)
            src = x_ref.at[0 if bcast else _lin(peer)]
            cp = pltpu.make_async_remote_copy(src_ref=src, dst_ref=o_ref.at[my_id], send_sem=send_sems.at[k], recv_sem=recv_sems.at[k],
                                              device_id=peer, device_id_type=MESH)
            cp.start()
            copies.append(cp)
        for k in range(1, N_DEV):
            peer = _flip(me, k)
            pltpu.make_async_remote_copy(src_ref=x_ref.at[0], dst_ref=o_ref.at[_lin(peer)], send_sem=send_sems.at[k], recv_sem=recv_sems.at[k],
                                         device_id=peer, device_id_type=MESH).wait_recv()
        for cp in copies:
            cp.wait_send()

    vm = pl.BlockSpec(memory_space=pltpu.VMEM)
    return pl.pallas_call(body, in_specs=[vm], out_specs=vm, out_shape=_sds((N_DEV, 1, n), F32),
                          scratch_shapes=[pltpu.SemaphoreType.DMA((N_DEV,)), pltpu.SemaphoreType.DMA((N_DEV,))], name=name)(x3)


HBM_SPEC = pl.BlockSpec(memory_space=pltpu.HBM)
SEM_SPEC = pl.BlockSpec(memory_space=pltpu.SEMAPHORE)
ANY_SPEC = pl.BlockSpec(memory_space=pl.ANY)
DATAFLOW = pltpu.SideEffectType.DATAFLOW_SIDE_EFFECTING
GATHER_FLIPS = (1, 2, 4, 6)
PASS_FLIPS = (2, 4, 6)
TOKEN = (8, LANE)


def _hbm(t):
    return pltpu.with_memory_space_constraint(t, pltpu.HBM)


def _hbm_like(ts):
    return [pltpu.HBM(t.shape, t.dtype) for t in ts]


def _split_start(issue, srcs, lands, n_sem, name, deps=()):
    n, nb, nd = len(srcs), len(srcs) + len(lands), len(deps)

    def body(*refs):
        issue(refs[:n], refs[n:nb], refs[nb + nd], refs[nb + nd + 1])
        refs[-1][...] = jnp.zeros(TOKEN, F32)

    outs = pl.pallas_call(
        body, name=name,
        out_shape=(pltpu.SemaphoreType.DMA((n_sem,)), pltpu.SemaphoreType.DMA((n_sem,)), *_hbm_like(srcs), *_hbm_like(lands), _sds(TOKEN, F32)),
        in_specs=[HBM_SPEC] * nb + [ANY_SPEC] * nd,
        out_specs=(SEM_SPEC, SEM_SPEC, *[HBM_SPEC] * nb, pl.BlockSpec(memory_space=pltpu.VMEM)),
        input_output_aliases={i: 2 + i for i in range(nb)},
        compiler_params=pltpu.CompilerParams(has_side_effects=DATAFLOW))(*[_hbm(t) for t in srcs], *[_hbm(t) for t in lands], *deps)
    return dict(sems=outs[:2], thru=list(outs[2:2 + nb]), token=outs[-1], n=n)


def _split_wait(finish, handle, after, name):
    n = handle["n"]
    thru = handle["thru"]
    nb = len(thru)

    def body(*refs):
        finish(refs[:n], refs[n:nb], refs[nb], refs[nb + 1])

    outs = pl.pallas_call(
        body, name=name, out_shape=_hbm_like(thru), in_specs=[HBM_SPEC] * nb + [SEM_SPEC, SEM_SPEC] + [ANY_SPEC] * len(after),
        out_specs=[HBM_SPEC] * nb, input_output_aliases={i: i for i in range(nb)},
        compiler_params=pltpu.CompilerParams(has_side_effects=DATAFLOW))(*thru, *handle["sems"], *after)
    return list(outs[:n]), list(outs[n:])


def _gather_start(shards, name, deps=()):
    n = len(shards)
    my_id = _lin(_my_coords())
    lands = [lax.dynamic_update_slice(lax.empty((N_DEV,) + t.shape, t.dtype), t[None], (my_id, 0, 0)) for t in shards]

    def issue(src, land, send_sems, recv_sems):
        me = _my_coords()
        for w in range(n):
            for j, k in enumerate(GATHER_FLIPS):
                q = len(GATHER_FLIPS) * w + j
                pltpu.make_async_remote_copy(src_ref=src[w], dst_ref=land[w].at[_lin(me)], send_sem=send_sems.at[q], recv_sem=recv_sems.at[q],
                                             device_id=_flip(me, k), device_id_type=MESH).start()

    return _split_start(issue, shards, lands, len(GATHER_FLIPS) * n, name, deps)


def _gather_wait(handle, after, name):
    n = handle["n"]

    def finish(src, land, send_sems, recv_sems):
        me = _my_coords()
        for w in range(n):
            for j, k in enumerate(GATHER_FLIPS):
                q = len(GATHER_FLIPS) * w + j
                peer = _flip(me, k)
                cp = pltpu.make_async_remote_copy(src_ref=src[w], dst_ref=land[w].at[_lin(peer)], send_sem=send_sems.at[q], recv_sem=recv_sems.at[q],
                                                  device_id=peer, device_id_type=MESH)
                cp.wait_send()
                cp.wait_recv()

    return _split_wait(finish, handle, after, name)[1]


def _pass_copy(land, send_sems, recv_sems, w, j, arriving):
    me = _my_coords()
    blk = land[w].at[_lin(_flip(me, PASS_FLIPS[j] + (1 if arriving else 0)))]
    q = len(PASS_FLIPS) * w + j
    return pltpu.make_async_remote_copy(src_ref=blk, dst_ref=blk, send_sem=send_sems.at[q], recv_sem=recv_sems.at[q],
                                        device_id=_flip(me, 1), device_id_type=MESH)


def _pass_start(lands, name, deps=()):
    def issue(_, land, send_sems, recv_sems):
        for w in range(len(lands)):
            for j in range(len(PASS_FLIPS)):
                _pass_copy(land, send_sems, recv_sems, w, j, False).start()

    return _split_start(issue, [], lands, len(PASS_FLIPS) * len(lands), name, deps)


def _pass_wait(handle, after, name):
    def finish(_, land, send_sems, recv_sems):
        for w in range(len(handle["thru"])):
            for j in range(len(PASS_FLIPS)):
                _pass_copy(land, send_sems, recv_sems, w, j, False).wait_send()
                _pass_copy(land, send_sems, recv_sems, w, j, True).wait_recv()

    return _split_wait(finish, handle, after, name)[1]


def _gather_pass(lands, name):
    n = len(lands)
    n_p = len(PASS_FLIPS)

    def body(*refs):
        land = refs[n:2 * n]
        send_sems, recv_sems = refs[2 * n:]
        me = _my_coords()
        sibling = _flip(me, 1)
        sent = []
        for w in range(n):
            for j, k in enumerate(PASS_FLIPS):
                blk = land[w].at[_lin(_flip(me, k))]
                cp = pltpu.make_async_remote_copy(src_ref=blk, dst_ref=blk, send_sem=send_sems.at[n_p * w + j], recv_sem=recv_sems.at[n_p * w + j],
                                                  device_id=sibling, device_id_type=MESH)
                cp.start()
                sent.append(cp)
        for w in range(n):
            for j, k in enumerate(PASS_FLIPS):
                blk = land[w].at[_lin(_flip(me, k + 1))]
                pltpu.make_async_remote_copy(src_ref=blk, dst_ref=blk, send_sem=send_sems.at[n_p * w + j], recv_sem=recv_sems.at[n_p * w + j],
                                             device_id=sibling, device_id_type=MESH).wait_recv()
        for cp in sent:
            cp.wait_send()

    return pl.pallas_call(body, in_specs=[ANY_SPEC] * n, out_specs=[ANY_SPEC] * n, out_shape=[_sds(t.shape, t.dtype) for t in lands],
                          input_output_aliases={i: i for i in range(n)},
                          scratch_shapes=[pltpu.SemaphoreType.DMA((n_p * n,)), pltpu.SemaphoreType.DMA((n_p * n,))], name=name)(*lands)


CHIP_FLIPS = (0, 2, 4, 6)


def _pair_copy(src, land, send_sems, recv_sems, w, j):
    me = _my_coords()
    q = len(CHIP_FLIPS) * w + j
    return pltpu.make_async_remote_copy(src_ref=src[w].at[_lin(_flip(me, CHIP_FLIPS[j] + 1))], dst_ref=land[w].at[j], send_sem=send_sems.at[q],
                                        recv_sem=recv_sems.at[q], device_id=_flip(me, 1), device_id_type=MESH)


def _pair_exchange(grads, name):
    n = len(grads)

    def body(*refs):
        src, land = refs[:n], refs[n:2 * n]
        send_sems, recv_sems = refs[2 * n:]
        sent = [_pair_copy(src, land, send_sems, recv_sems, w, j) for w in range(n) for j in range(len(CHIP_FLIPS))]
        for cp in sent:
            cp.start()
        for cp in sent:
            cp.wait_recv()
        for cp in sent:
            cp.wait_send()

    outs = pl.pallas_call(body, in_specs=[ANY_SPEC] * n, out_specs=[ANY_SPEC] * n,
                          out_shape=[_sds((len(CHIP_FLIPS),) + g.shape[1:], g.dtype) for g in grads],
                          scratch_shapes=[pltpu.SemaphoreType.DMA((len(CHIP_FLIPS) * n,))] * 2, name=name)(*grads)
    return list(outs)


def _pair_start(grads, name, deps=()):
    n = len(grads)
    lands = [lax.empty((len(CHIP_FLIPS),) + g.shape[1:], g.dtype) for g in grads]

    def issue(src, land, send_sems, recv_sems):
        for w in range(n):
            for j in range(len(CHIP_FLIPS)):
                _pair_copy(src, land, send_sems, recv_sems, w, j).start()

    return _split_start(issue, grads, lands, len(CHIP_FLIPS) * n, name, deps)


def _pair_wait(handle, after, name):
    n = handle["n"]

    def finish(src, land, send_sems, recv_sems):
        for w in range(n):
            for j in range(len(CHIP_FLIPS)):
                cp = _pair_copy(src, land, send_sems, recv_sems, w, j)
                cp.wait_send()
                cp.wait_recv()

    return _split_wait(finish, handle, after, name)


def _pair_add(grad, theirs, name):
    p, r, c = theirs.shape
    tr = _row_tile(r, c, PAIR_TILE)
    me = _my_coords()
    ids = jnp.stack([_lin(_flip(me, k)) for k in CHIP_FLIPS]).astype(jnp.int32)

    def body(ids_ref, a_ref, b_ref, o_ref):
        o_ref[...] = (a_ref[...].astype(F32) + b_ref[...].astype(F32)).astype(o_ref.dtype)

    blk = pl.BlockSpec((None, tr, c), lambda j, i, ids_ref: (j, i, 0))
    return pl.pallas_call(
        body, out_shape=_sds((p, r, c), theirs.dtype), compiler_params=_params(2), name=name,
        grid_spec=pltpu.PrefetchScalarGridSpec(
            num_scalar_prefetch=1, grid=(p, r // tr),
            in_specs=[pl.BlockSpec((None, tr, c), lambda j, i, ids_ref: (ids_ref[j], i, 0)), blk], out_specs=blk))(ids, grad, theirs)


def _chips_start(parts, name, deps=()):
    n = len(parts)
    n_c = len(CHIP_FLIPS) - 1
    lands = [lax.empty((n_c,) + t.shape[1:], t.dtype) for t in parts]

    def issue(src, land, send_sems, recv_sems):
        me = _my_coords()
        for w in range(n):
            for j in range(1, n_c + 1):
                q = n_c * w + j - 1
                pltpu.make_async_remote_copy(src_ref=src[w].at[j], dst_ref=land[w].at[j - 1], send_sem=send_sems.at[q], recv_sem=recv_sems.at[q],
                                             device_id=_flip(me, CHIP_FLIPS[j]), device_id_type=MESH).start()

    return _split_start(issue, parts, lands, n_c * n, name, deps)


def _chips_wait(handle, after, name):
    n = handle["n"]
    n_c = len(CHIP_FLIPS) - 1

    def finish(src, land, send_sems, recv_sems):
        me = _my_coords()
        for w in range(n):
            for j in range(1, n_c + 1):
                q = n_c * w + j - 1
                cp = pltpu.make_async_remote_copy(src_ref=src[w].at[j], dst_ref=land[w].at[j - 1], send_sem=send_sems.at[q], recv_sem=recv_sems.at[q],
                                                  device_id=_flip(me, CHIP_FLIPS[j]), device_id_type=MESH)
                cp.wait_send()
                cp.wait_recv()

    return _split_wait(finish, handle, after, name)


def _after(t, *tokens):
    for tok in tokens:
        t = t + tok[0:1, 0:1]
    return t


def _rope_tables(positions):
    half = ROT // 2
    inv_freq = ROPE_THETA ** (-jnp.arange(0, ROT, 2, dtype=F32) / ROT)
    ang = positions.astype(F32).reshape(-1, 1) * inv_freq
    cos, sin = jnp.cos(ang), jnp.sin(ang)
    s = ang.shape[0]
    pad = jnp.zeros((s, HEAD_DIM - ROT), F32)
    zero = jnp.zeros((s, half), F32)
    two = lambda t: jnp.concatenate([t, t], axis=1)
    return (two(jnp.concatenate([cos, cos, pad + 1.0], axis=1)), two(jnp.concatenate([-sin, zero, pad], axis=1)),
            two(jnp.concatenate([zero, sin, pad], axis=1)))


def _local_step(x, tgt, tabs, mod, sinks_pad, hl, hg_norm, g_pre_mix, g_post_mix, g_pre_ffn, g_post_ffn, weights, prefetch, scatter, scatter_on):
    s = x.shape[0]
    h1 = _pre_fwd(x, g_pre_mix, mod, 1, 0, "pre_mix_fwd")
    (w_in_a,) = weights("in_a", h1)
    proj = _mm_nt(h1, w_in_a, 256, IN_COLS // 2, D // 2, F32, "proj_mm_a", a_col=0)
    (w_in_b,) = weights("in_b", proj)
    proj = _mm_nt(h1, w_in_b, 256, IN_COLS // 2, D // 2, F32, "proj_mm_b", add=proj, a_col=1)
    att = _attn_fwd(proj, tabs, _after(sinks_pad, prefetch("mix", proj)))
    o_raw, states = _hgrn_fwd(proj, hl)
    ohg = _hgout_fwd(o_raw, proj, hg_norm)
    w_attn_dm, w_hgrn_dm, w_out = weights("mix", ohg)
    natural = lambda w_dm: w_dm.transpose(1, 0, 2).reshape(w_dm.shape[1], D)
    pieces = lambda g: g.reshape(g.shape[0], N_DEV, D // N_DEV).transpose(1, 0, 2)
    w_attn, w_hgrn = natural(w_attn_dm), natural(w_hgrn_dm)
    y_a = _mm_nn(att, w_attn, s, 512, ATT_W, F32, "attn_proj_mm")
    y_h = _mm_nn(ohg, w_hgrn, s, 512, HG_W, F32, "hgrn_proj_mm")
    merged = _merge_fwd(y_a, y_h, proj)
    y = _mm_nn(merged, w_out, s, 512, D, F32, "out_mm")
    x1 = _post_fwd(x, y, g_post_mix, mod, 2, "post_mix_fwd")
    h2 = _pre_fwd(x1, g_pre_ffn, _after(mod, prefetch("ffn_in", x1)), 4, 3, "pre_ffn_fwd")
    (w_ffn_in_dm,) = weights("ffn_in", h2)
    gu = _mm_nn_dm(h2, w_ffn_in_dm, s // 2, F32, "ffn_in_mm")
    act = _swiglu_fwd(gu, deps=[prefetch("ffn_out", gu)])
    (w_ffn_out,) = weights("ffn_out", act)
    y2 = _mm_nn(act, w_ffn_out, 512, 512, FFN, F32, "ffn_out_mm")
    err, loss, dy2, d_gate2, dg_post_ffn = _post_loss_bwd(x1, y2, g_post_ffn, mod, 5, tgt, "post_ffn_loss_bwd")
    gw_ffn_out = _mm_tn(act, dy2, 512, D, BF16, "ffn_out_dw")
    t_pair = scatter([gw_ffn_out.reshape(N_DEV, FFN // N_DEV, D)], "ffn_out")
    d_act = _mm_nt(dy2, w_ffn_out, s, 512, D, F32, "ffn_out_dx", deps=[t_pair])
    dgu = _swiglu_bwd(d_act, gu)
    t_out = scatter_on("ffn_out", dgu)
    gw_ffn_in = _mm_tn_dm(h2, dgu, 1024, BF16, "ffn_in_dw")
    t_pair = scatter([gw_ffn_in], "ffn_in")
    dh2 = _mm_nt_dm(dgu, w_ffn_in_dm, s, 1024, F32, "ffn_in_dx", deps=[t_pair])
    mod = _after(mod, t_out)
    dx1, d_shift2, d_scale2, dg_pre_ffn = _pre_bwd([dh2], x1, err, g_pre_ffn, mod, 4, "pre_ffn_bwd")
    dy, d_gate1, dg_post_mix = _post_bwd(dx1, y, g_post_mix, mod, 2, "post_mix_bwd")
    t_in = scatter_on("ffn_in", dy)
    d_merged = _mm_nt(dy, w_out, s, 512, D, F32, "out_dx")
    gw_out = _mm_tn(merged, dy, 512, D, BF16, "out_dw")
    dy_a, dy_h, d_gate_a, d_gate_h = _merge_bwd(d_merged, y_a, y_h, proj)
    gw_attn = pieces(_mm_tn(att, dy_a, 512, D, BF16, "attn_proj_dw"))
    gw_hgrn = pieces(_mm_tn(ohg, dy_h, 512, D, BF16, "hgrn_proj_dw"))
    t_pair = scatter([gw_attn, gw_hgrn, gw_out.reshape(N_DEV, D // N_DEV, D)], "mix")
    d_att = _mm_nt(dy_a, w_attn, s, 512, D, F32, "attn_proj_dx")
    d_ohg = _mm_nt(dy_h, w_hgrn, s, 512, D, F32, "hgrn_proj_dx", deps=[t_pair])
    d_o, d_gh, d_hg_norm = _hgout_bwd(d_ohg, o_raw, proj, _after(hg_norm, t_in))
    d_qh, d_fh, d_ih, d_hl = _hgrn_bwd(proj, hl, states, d_o)
    t_mix = scatter_on("mix", d_qh)
    d_qa, d_ka, d_va, d_sinks = _attn_bwd(proj, tabs, _after(sinks_pad, t_mix), d_att)
    d_proj = jnp.concatenate([d_qa, d_ka.astype(BF16), d_va.astype(BF16), d_qh, d_fh, d_ih, d_gh, d_gate_a, d_gate_h], axis=1)
    dh1 = [_mm_nn(d_proj, w_half, s // 2, 512, IN_COLS // 2, F32, "proj_dx_" + tag) for tag, w_half in (("a", w_in_a), ("b", w_in_b))]
    grad_x, d_shift1, d_scale1, dg_pre_mix = _pre_bwd(dh1, x, dx1, g_pre_mix, mod, 1, "pre_mix_bwd")
    d_mod = jnp.concatenate([d_shift1, d_scale1, d_gate1, d_shift2, d_scale2, d_gate2], axis=1)
    small = [d_mod, dg_pre_mix, dg_post_mix, dg_pre_ffn, dg_post_ffn, d_hl.reshape(1, 2 * HG_W), d_hg_norm, d_sinks]
    return loss, grad_x, small, h1, d_proj


def kernel(x, c, positions, w_ada, b_ada, g_pre_mix, g_post_mix, g_pre_ffn, g_post_ffn, w_in, attn_sinks, w_attn_proj, hg_lower_bounds, hg_norm, w_hgrn_proj, w_out, w_ffn_in, w_ffn_out, loss_target, m_w_ada, m_b_ada, m_g_pre_mix, m_g_post_mix, m_g_pre_ffn, m_g_post_ffn, m_w_in, m_attn_sinks, m_w_attn_proj, m_hg_lower_bounds, m_hg_norm, m_w_hgrn_proj, m_w_out, m_w_ffn_in, m_w_ffn_out, v_w_ada, v_b_ada, v_g_pre_mix, v_g_post_mix, v_g_pre_ffn, v_g_post_ffn, v_w_in, v_attn_sinks, v_w_attn_proj, v_hg_lower_bounds, v_hg_norm, v_w_hgrn_proj, v_w_out, v_w_ffn_in, v_w_ffn_out):
    my_id = _lin(_my_coords())
    s = x.shape[1]
    n_ada = w_ada.shape[2]

    c_all = _exchange_small(c.reshape(1, 1, D), True, "gather_c").reshape(N_DEV, D)
    b_cols = lax.dynamic_slice(b_ada, (0, my_id * n_ada), (1, n_ada))
    mod_part = _mod_part(c_all, w_ada[0], b_cols)
    mod = _exchange_small(mod_part.reshape(N_DEV, 1, n_ada), False, "scatter_mod").reshape(1, N_MOD * D)
    groups = {"in_a": [w_in[0].T[:, :D // 2]], "in_b": [w_in[0].T[:, D // 2:]], "mix": [w_attn_proj[0], w_hgrn_proj[0], w_out[0]],
              "ffn_in": [w_ffn_in[0]], "ffn_out": [w_ffn_out[0]]}

    def start(group, dep):
        shards, dep = lax.optimization_barrier((groups[group], dep))
        return _gather_start([t.astype(BF16) for t in shards], "gather_start_" + group, deps=[dep])

    gathers = {"in_a": start("in_a", mod)}
    gathers["in_b"] = start("in_b", gathers["in_a"]["token"])
    gathers["mix"] = start("mix", gathers["in_b"]["token"])
    gathers["ffn_in"] = start("ffn_in", gathers["mix"]["token"])
    gathers["ffn_out"] = start("ffn_out", gathers["ffn_in"]["token"])

    passes = {}

    def prefetch(group, after):
        lands = _gather_wait(gathers[group], [after], "gather_wait_" + group)
        passes[group] = _pass_start(lands, "gather_pass_start_" + group)
        return passes[group]["token"]

    def weights(group, after):
        if group in passes:
            lands = _pass_wait(passes[group], [after], "gather_pass_wait_" + group)
        else:
            after = [after, gathers["ffn_out"]["token"]]
            lands = _gather_pass(_gather_wait(gathers[group], after, "gather_wait_" + group), "gather_pass_" + group)
        if group in ("in_a", "in_b"):
            return (lands[0].reshape(IN_COLS, D // 2),)
        if group == "mix":
            return lands[0], lands[1], lands[2].reshape(D, D)
        return (lands[0],) if group == "ffn_in" else (lands[0].reshape(FFN, D),)

    pairs, scatters = {}, {}

    def scatter(grads, group):
        pairs[group] = _pair_start(grads, "scatter_pair_" + group)
        return pairs[group]["token"]

    def scatter_on(group, after):
        if group in pairs:
            local, theirs = _pair_wait(pairs[group], [after], "scatter_pair_wait_" + group)
        else:
            local, theirs = after, _pair_exchange(after, "scatter_pair_" + group)
        parts = [_pair_add(g, t, "scatter_pair_add_%s_%d" % (group, k)) for k, (g, t) in enumerate(zip(local, theirs))]
        scatters[group] = _chips_start(parts, "scatter_start_" + group)
        return scatters[group]["token"]

    sinks_pad = jnp.pad(attn_sinks, ((0, 0), (0, LANE - ATT_HEADS)))
    loss, grad_x, small, h1, d_proj = _local_step(
        x[0], loss_target[0], _rope_tables(positions), mod, sinks_pad, hg_lower_bounds, hg_norm, g_pre_mix, g_post_mix, g_pre_ffn, g_post_ffn,
        weights, prefetch, scatter, scatter_on)
    small = small + [jnp.pad(loss, ((0, 0), (0, LANE - 1)))]

    sizes = [t.shape[1] for t in small]
    parts = _exchange_small(jnp.concatenate(small, axis=1).reshape(1, 1, sum(sizes)), True, "gather_small_grads")
    dep = parts
    for b_col, half in enumerate(("in_a", "in_b")):
        gw_half = _mm_tn(d_proj, h1, 256, D // 2, BF16, "proj_dw_" + half, deps=[dep], b_col=b_col)
        dep = scatter_on(half, [gw_half.reshape(N_DEV, IN_COLS // N_DEV, D // 2)])
    offs = [sum(sizes[:k]) for k in range(len(sizes))]
    piece = lambda k, n=None: parts[:, :, offs[k]:offs[k] + (sizes[k] if n is None else n)]
    loss = jnp.sum(piece(8, 1))
    small_w = [(piece(0), b_ada, m_b_ada, v_b_ada), (piece(1), g_pre_mix, m_g_pre_mix, v_g_pre_mix),
               (piece(2), g_post_mix, m_g_post_mix, v_g_post_mix), (piece(3), g_pre_ffn, m_g_pre_ffn, v_g_pre_ffn),
               (piece(4), g_post_ffn, m_g_post_ffn, v_g_post_ffn),
               (piece(5).reshape(N_DEV, 2, HG_W), hg_lower_bounds, m_hg_lower_bounds, v_hg_lower_bounds),
               (piece(6), hg_norm, m_hg_norm, v_hg_norm), (piece(7, ATT_HEADS), attn_sinks, m_attn_sinks, v_attn_sinks)]
    names = ["b_ada", "g_pre_mix", "g_post_mix", "g_pre_ffn", "g_post_ffn", "hg_lower_bounds", "hg_norm", "attn_sinks"]
    res = {n: _adamw(p, w, m, v, "adamw_" + n) for n, (p, w, m, v) in zip(names, small_w)}

    dmod_cols = lax.dynamic_slice(parts.reshape(N_DEV, -1), (0, my_id * n_ada), (N_DEV, n_ada))
    res["w_ada"] = list(_update_w_ada(c_all.T, dmod_cols, w_ada[0], m_w_ada[0], v_w_ada[0], deps=[scatters["in_b"]["token"]]))

    big = {"ffn_out": [("w_ffn_out", w_ffn_out, m_w_ffn_out, v_w_ffn_out)], "ffn_in": [("w_ffn_in", w_ffn_in, m_w_ffn_in, v_w_ffn_in)],
           "mix": [("w_attn_proj", w_attn_proj, m_w_attn_proj, v_w_attn_proj), ("w_hgrn_proj", w_hgrn_proj, m_w_hgrn_proj, v_w_hgrn_proj),
                   ("w_out", w_out, m_w_out, v_w_out)]}
    after = [scatters["in_b"]["token"]]
    for group, members in big.items():
        local, lands = _chips_wait(scatters[group], after, "scatter_wait_" + group)
        for (n, w, m, v), mine, land in zip(members, local, lands):
            res[n] = _adamw(land, w[0], m[0], v[0], "adamw_" + n, own=mine[0])
            after = after + [res[n][1]]
    after = [res[n][1] for n in res]
    halves = [_chips_wait(scatters[half], after, "scatter_wait_" + half) for half in ("in_a", "in_b")]
    own = [local[0][0] for local, _ in halves]
    land = [lands[0] for _, lands in halves]
    res["w_in"] = [t.T for t in _adamw(land, w_in[0].T, m_w_in[0].T, v_w_in[0].T, "adamw_w_in", own=own, max_elems=WIDE_TILE)]

    order = ["w_ada", "b_ada", "g_pre_mix", "g_post_mix", "g_pre_ffn", "g_post_ffn", "w_in", "attn_sinks", "w_attn_proj",
             "hg_lower_bounds", "hg_norm", "w_hgrn_proj", "w_out", "w_ffn_in", "w_ffn_out"]
    lead = {"w_ada", "w_in", "w_attn_proj", "w_hgrn_proj", "w_out", "w_ffn_in", "w_ffn_out"}
    outs = [loss, grad_x[None]]
    for k in range(4):
        outs += [res[n][k][None] if n in lead else res[n][k] for n in order]
    return tuple(outs)
```

```python
import functools

import jax
import jax.numpy as jnp
from jax import lax
from jax.experimental import pallas as pl
from jax.experimental.pallas import tpu as pltpu

F32 = jnp.float32
BF16 = jnp.bfloat16

N_DEV = 8
D = 2048
ATT_HEADS = 16
KV_HEADS = 2
HEAD_DIM = 64
GROUP = ATT_HEADS // KV_HEADS
ATT_W = ATT_HEADS * HEAD_DIM
BLK = 128
ROT = HEAD_DIM // 4
ROPE_THETA = 500000.0
HG_HEADS = 8
HG_K = 128
HG_W = HG_HEADS * HG_K
CHUNK = 64
SUB = 16
FFN = 5632
N_MOD = 6
EPS = 1e-6
LANE = 128
Q_A, K_A, V_A, Q_H, F_H, I_H, G_H, GT_A, GT_H, IN_COLS = 0, 1024, 1152, 1280, 2304, 3328, 4352, 5376, 7424, 9472

ADAM_LR, ADAM_B1, ADAM_B2, ADAM_EPS, ADAM_WD, ADAM_STEP = 0.001, 0.9, 0.999, 1e-08, 0.01, 10

TR = 256
HG_TB = 512
VMEM_BIG = 56 << 20
MESH = pl.DeviceIdType.MESH


def _sds(shape, dtype):
    return jax.ShapeDtypeStruct(shape, dtype)


def _params(n_axes, vmem=None):
    return pltpu.CompilerParams(dimension_semantics=("arbitrary",) * n_axes, vmem_limit_bytes=vmem)


def _sig(t):
    return 1.0 / (1.0 + jnp.exp(-t))


def _dot(a, b, dims):
    return lax.dot_general(a, b, (dims, ((), ())), preferred_element_type=F32)


NN = ((1,), (0,))
NT = ((1,), (1,))
TN = ((0,), (0,))


def _matmul(a, b, a_spec, b_spec, o_spec, out_shape, grid, dims, acc_shape, name, deps=(), add=None):
    nk = grid[2]
    nd = len(deps)
    extra = [] if add is None else [add]

    def body(a_ref, b_ref, *rest):
        o_ref, scratch = rest[nd + len(extra)], rest[nd + len(extra) + 1:]
        part = _dot(a_ref[...], b_ref[...], dims)
        if add is not None:
            assert nk == 1
            part = part + rest[nd][...]
        if nk == 1:
            o_ref[...] = part.astype(o_ref.dtype)
        else:
            acc = scratch[0]
            k = pl.program_id(2)

            @pl.when(k == 0)
            def _():
                acc[...] = part

            @pl.when(k > 0)
            def _():
                acc[...] += part

            @pl.when(k == nk - 1)
            def _():
                o_ref[...] = acc[...].astype(o_ref.dtype)

    return pl.pallas_call(
        body, grid=grid, in_specs=[a_spec, b_spec] + [pl.BlockSpec(memory_space=pl.ANY)] * nd + [o_spec] * len(extra),
        out_specs=o_spec, out_shape=out_shape, scratch_shapes=[pltpu.VMEM(acc_shape, F32)] if nk > 1 else [],
        input_output_aliases={2 + nd: 0} if extra else {},
        compiler_params=_params(3, VMEM_BIG), name=name)(a, b, *deps, *extra)


def _mm_nn(a, b, tm, tn, tk, out_dtype, name):
    m, k = a.shape
    n = b.shape[1]
    return _matmul(a, b, pl.BlockSpec((tm, tk), lambda j, i, kk: (i, kk)), pl.BlockSpec((tk, tn), lambda j, i, kk: (kk, j)),
                   pl.BlockSpec((tm, tn), lambda j, i, kk: (i, j)), _sds((m, n), out_dtype),
                   (n // tn, m // tm, k // tk), NN, (tm, tn), name)


def _mm_nn_dm(a, b, tm, out_dtype, name):
    m, k = a.shape
    n = b.shape[2]
    return _matmul(a, b, pl.BlockSpec((tm, k), lambda j, i, kk: (i, 0)), pl.BlockSpec((None, k, n), lambda j, i, kk: (j, 0, 0)),
                   pl.BlockSpec((tm, n), lambda j, i, kk: (i, j)), _sds((m, N_DEV * n), out_dtype),
                   (N_DEV, m // tm, 1), NN, (tm, n), name)


def _mm_nt(a, b, tm, tn, tk, out_dtype, name, deps=(), add=None, a_col=0):
    m = a.shape[0]
    n, k = b.shape
    return _matmul(a, b, pl.BlockSpec((tm, tk), lambda j, i, kk: (i, kk + a_col * (k // tk))), pl.BlockSpec((tn, tk), lambda j, i, kk: (j, kk)),
                   pl.BlockSpec((tm, tn), lambda j, i, kk: (i, j)), _sds((m, n), out_dtype),
                   (n // tn, m // tm, k // tk), NT, (tm, tn), name, deps, add)


def _mm_nt_dm(a, b, tm, tn, out_dtype, name, deps=()):
    m = a.shape[0]
    n_out, n = b.shape[1], b.shape[2]
    return _matmul(a, b, pl.BlockSpec((tm, n), lambda j, i, kk: (i, kk)), pl.BlockSpec((None, tn, n), lambda j, i, kk: (kk, j, 0)),
                   pl.BlockSpec((tm, tn), lambda j, i, kk: (i, j)), _sds((m, n_out), out_dtype),
                   (n_out // tn, m // tm, N_DEV), NT, (tm, tn), name, deps)


def _mm_tn(a, b, tm, tn, out_dtype, name, deps=(), b_col=None):
    s, m = a.shape
    n = b.shape[1] if b_col is None else tn
    first = 0 if b_col is None else b_col
    return _matmul(a, b, pl.BlockSpec((s, tm), lambda j, i, kk: (0, i)), pl.BlockSpec((s, tn), lambda j, i, kk: (0, j + first)),
                   pl.BlockSpec((tm, tn), lambda j, i, kk: (i, j)), _sds((m, n), out_dtype),
                   (n // tn, m // tm, 1), TN, (tm, tn), name, deps)


def _mm_tn_dm(a, b, tm, out_dtype, name):
    s, m = a.shape
    n = b.shape[1] // N_DEV
    return _matmul(a, b, pl.BlockSpec((s, tm), lambda j, i, kk: (0, i)), pl.BlockSpec((s, n), lambda j, i, kk: (0, j)),
                   pl.BlockSpec((None, tm, n), lambda j, i, kk: (j, i, 0)), _sds((N_DEV, m, n), out_dtype),
                   (N_DEV, m // tm, 1), TN, (tm, n), name)


def _row_spec():
    return pl.BlockSpec((TR, D), lambda i: (i, 0))


def _vec_spec(k=0):
    return pl.BlockSpec((1, D), lambda i: (0, k))


def _acc_rows(ref, first, val):
    @pl.when(first)
    def _():
        ref[...] = val

    @pl.when(jnp.logical_not(first))
    def _():
        ref[...] += val


def _pre_fwd(x, g, mod, k_scale, k_shift, name):
    s = x.shape[0]

    def body(x_ref, g_ref, sc_ref, sh_ref, h_ref):
        xv = x_ref[...]
        r = lax.rsqrt(jnp.mean(xv * xv, axis=-1, keepdims=True) + EPS)
        n = xv * r * g_ref[...]
        h_ref[...] = (n * (1.0 + sc_ref[...]) + sh_ref[...]).astype(h_ref.dtype)

    return pl.pallas_call(body, grid=(s // TR,), in_specs=[_row_spec(), _vec_spec(), _vec_spec(k_scale), _vec_spec(k_shift)],
                          out_specs=_row_spec(), out_shape=_sds((s, D), BF16), compiler_params=_params(1), name=name)(x, g, mod, mod)


def _post_fwd(x, y, g, mod, k_gate, name):
    s = x.shape[0]

    def body(x_ref, y_ref, g_ref, gt_ref, o_ref):
        yv = y_ref[...]
        r = lax.rsqrt(jnp.mean(yv * yv, axis=-1, keepdims=True) + EPS)
        o_ref[...] = x_ref[...] + gt_ref[...] * (yv * r * g_ref[...])

    return pl.pallas_call(body, grid=(s // TR,), in_specs=[_row_spec(), _row_spec(), _vec_spec(), _vec_spec(k_gate)],
                          out_specs=_row_spec(), out_shape=_sds((s, D), F32), compiler_params=_params(1), name=name)(x, y, g, mod)


def _post_loss_bwd(x, y, g, mod, k_gate, tgt, name):
    s = x.shape[0]

    def body(x_ref, y_ref, g_ref, gt_ref, t_ref, e_ref, loss_ref, dy_ref, dgt_ref, dg_ref):
        first = pl.program_id(0) == 0
        yv, gv, gate = y_ref[...], g_ref[...], gt_ref[...]
        r = lax.rsqrt(jnp.mean(yv * yv, axis=-1, keepdims=True) + EPS)
        yh = yv * r
        err = x_ref[...] + gate * (yh * gv) - t_ref[...]
        e = err * (1.0 / D)
        e_ref[...] = e
        _acc_rows(loss_ref, first, 0.5 * jnp.sum(jnp.mean(err * err, axis=-1, keepdims=True), axis=0, keepdims=True))
        dn = e * gate
        dgn = dn * gv
        dy_ref[...] = (r * (dgn - yh * jnp.mean(dgn * yh, axis=-1, keepdims=True))).astype(dy_ref.dtype)
        _acc_rows(dgt_ref, first, jnp.sum(e * (yh * gv), axis=0, keepdims=True))
        _acc_rows(dg_ref, first, jnp.sum(dn * yh, axis=0, keepdims=True))

    return pl.pallas_call(body, grid=(s // TR,),
                          in_specs=[_row_spec(), _row_spec(), _vec_spec(), _vec_spec(k_gate), _row_spec()],
                          out_specs=[_row_spec(), pl.BlockSpec((1, 1), lambda i: (0, 0)), _row_spec(), _vec_spec(), _vec_spec()],
                          out_shape=[_sds((s, D), F32), _sds((1, 1), F32), _sds((s, D), BF16), _sds((1, D), F32), _sds((1, D), F32)],
                          compiler_params=_params(1), name=name)(x, y, g, mod, tgt)


def _pre_bwd(dh_parts, x, res, g, mod, k_scale, name):
    s = x.shape[0]
    n_p = len(dh_parts)

    def body(*refs):
        x_ref, res_ref, g_ref, sc_ref, dx_ref, dsh_ref, dsc_ref, dg_ref = refs[n_p:]
        first = pl.program_id(0) == 0
        dh_v = jnp.concatenate([r[...] for r in refs[:n_p]], axis=1)
        xv, gv = x_ref[...], g_ref[...]
        r = lax.rsqrt(jnp.mean(xv * xv, axis=-1, keepdims=True) + EPS)
        xh = xv * r
        dn = dh_v * (1.0 + sc_ref[...])
        dgn = dn * gv
        dx_ref[...] = res_ref[...] + r * (dgn - xh * jnp.mean(dgn * xh, axis=-1, keepdims=True))
        _acc_rows(dsh_ref, first, jnp.sum(dh_v, axis=0, keepdims=True))
        _acc_rows(dsc_ref, first, jnp.sum(dh_v * (xh * gv), axis=0, keepdims=True))
        _acc_rows(dg_ref, first, jnp.sum(dn * xh, axis=0, keepdims=True))

    return pl.pallas_call(body, grid=(s // TR,),
                          in_specs=[pl.BlockSpec((TR, t.shape[1]), lambda i: (i, 0)) for t in dh_parts]
                          + [_row_spec(), _row_spec(), _vec_spec(), _vec_spec(k_scale)],
                          out_specs=[_row_spec(), _vec_spec(), _vec_spec(), _vec_spec()],
                          out_shape=[_sds((s, D), F32)] + [_sds((1, D), F32)] * 3,
                          compiler_params=_params(1), name=name)(*dh_parts, x, res, g, mod)


def _post_bwd(dx, y, g, mod, k_gate, name):
    s = y.shape[0]

    def body(dx_ref, y_ref, g_ref, gt_ref, dy_ref, dgt_ref, dg_ref):
        first = pl.program_id(0) == 0
        yv, dxv, gv = y_ref[...], dx_ref[...], g_ref[...]
        r = lax.rsqrt(jnp.mean(yv * yv, axis=-1, keepdims=True) + EPS)
        yh = yv * r
        dn = dxv * gt_ref[...]
        dgn = dn * gv
        dy_ref[...] = (r * (dgn - yh * jnp.mean(dgn * yh, axis=-1, keepdims=True))).astype(dy_ref.dtype)
        _acc_rows(dgt_ref, first, jnp.sum(dxv * (yh * gv), axis=0, keepdims=True))
        _acc_rows(dg_ref, first, jnp.sum(dn * yh, axis=0, keepdims=True))

    return pl.pallas_call(body, grid=(s // TR,), in_specs=[_row_spec(), _row_spec(), _vec_spec(), _vec_spec(k_gate)],
                          out_specs=[_row_spec(), _vec_spec(), _vec_spec()],
                          out_shape=[_sds((s, D), BF16), _sds((1, D), F32), _sds((1, D), F32)],
                          compiler_params=_params(1), name=name)(dx, y, g, mod)


SW_TN = 1408
SW_TR = 512
TALL = 2048


def _swiglu_fwd(gu, deps=()):
    s = gu.shape[0]
    nb = FFN // SW_TN

    def body(g_ref, u_ref, *rest):
        a_ref = rest[len(deps)]
        gv = g_ref[...]
        a_ref[...] = (gv * _sig(gv) * u_ref[...]).astype(a_ref.dtype)

    return pl.pallas_call(body, grid=(s // SW_TR, nb),
                          in_specs=[pl.BlockSpec((SW_TR, SW_TN), lambda i, j: (i, j)), pl.BlockSpec((SW_TR, SW_TN), lambda i, j: (i, j + nb))]
                          + [pl.BlockSpec(memory_space=pl.ANY)] * len(deps),
                          out_specs=pl.BlockSpec((SW_TR, SW_TN), lambda i, j: (i, j)), out_shape=_sds((s, FFN), BF16),
                          compiler_params=_params(2, 48 << 20), name="swiglu_fwd")(gu, gu, *deps)


def _swiglu_bwd(dact, gu):
    s = gu.shape[0]
    nb = FFN // SW_TN
    n_steps = (s // SW_TR) * nb

    def body(da_ref, g_ref, u_ref, o_ref, buf, sems):
        i, j = pl.program_id(0), pl.program_id(1)
        step = i * nb + j
        slot = step % 2

        def tiles(sl):
            rows = pl.ds(pl.multiple_of(i * SW_TR, SW_TR), SW_TR)
            return [pltpu.make_async_copy(buf.at[sl, h], o_ref.at[rows, pl.ds(pl.multiple_of((j + nb * h) * SW_TN, LANE), SW_TN)], sems.at[sl, h])
                    for h in range(2)]

        @pl.when(step >= 2)
        def _():
            for cp in tiles(slot):
                cp.wait()

        gv, da = g_ref[...], da_ref[...]
        sg = _sig(gv)
        buf[slot, 0] = (da * u_ref[...] * (sg * (1.0 + gv * (1.0 - sg)))).astype(buf.dtype)
        buf[slot, 1] = (da * (gv * sg)).astype(buf.dtype)
        for cp in tiles(slot):
            cp.start()

        @pl.when(step == n_steps - 1)
        def _():
            for cp in tiles(slot) + (tiles(1 - slot) if n_steps > 1 else []):
                cp.wait()

    blk = lambda f: pl.BlockSpec((SW_TR, SW_TN), f)
    return pl.pallas_call(body, grid=(s // SW_TR, nb),
                          in_specs=[blk(lambda i, j: (i, j)), blk(lambda i, j: (i, j)), blk(lambda i, j: (i, j + nb))],
                          out_specs=pl.BlockSpec(memory_space=pl.ANY), out_shape=_sds((s, 2 * FFN), BF16),
                          scratch_shapes=[pltpu.VMEM((2, 2, SW_TR, SW_TN), BF16), pltpu.SemaphoreType.DMA((2, 2))],
                          compiler_params=_params(2, 48 << 20), name="swiglu_bwd")(dact, gu, gu)


MG_TN = 256


def _merge_fwd(y_a, y_h, proj):
    s = y_a.shape[0]
    tn = MG_TN
    ba, bh = GT_A // tn, GT_H // tn

    def body(ya_ref, yh_ref, ga_ref, gh_ref, m_ref):
        m_ref[...] = (_sig(ga_ref[...]) * ya_ref[...] + _sig(gh_ref[...]) * yh_ref[...]).astype(m_ref.dtype)

    tr = min(s, TALL)
    blk = lambda f: pl.BlockSpec((tr, tn), f)
    return pl.pallas_call(body, grid=(s // tr, D // tn),
                          in_specs=[blk(lambda i, j: (i, j)), blk(lambda i, j: (i, j)), blk(lambda i, j: (i, j + ba)), blk(lambda i, j: (i, j + bh))],
                          out_specs=blk(lambda i, j: (i, j)), out_shape=_sds((s, D), BF16),
                          compiler_params=_params(2), name="merge_fwd")(y_a, y_h, proj, proj)


def _merge_bwd(dm, y_a, y_h, proj):
    s = y_a.shape[0]
    tn = MG_TN
    ba, bh = GT_A // tn, GT_H // tn

    def body(dm_ref, ya_ref, yh_ref, ga_ref, gh_ref, dya_ref, dyh_ref, dga_ref, dgh_ref):
        dmv = dm_ref[...]
        sa, sh = _sig(ga_ref[...]), _sig(gh_ref[...])
        dya_ref[...] = (dmv * sa).astype(BF16)
        dyh_ref[...] = (dmv * sh).astype(BF16)
        dga_ref[...] = (dmv * ya_ref[...] * (sa * (1.0 - sa))).astype(BF16)
        dgh_ref[...] = (dmv * yh_ref[...] * (sh * (1.0 - sh))).astype(BF16)

    tr = min(s, TALL)
    blk = lambda f: pl.BlockSpec((tr, tn), f)
    nat = blk(lambda i, j: (i, j))
    return pl.pallas_call(body, grid=(s // tr, D // tn),
                          in_specs=[nat, nat, nat, blk(lambda i, j: (i, j + ba)), blk(lambda i, j: (i, j + bh))],
                          out_specs=[nat] * 4, out_shape=[_sds((s, D), BF16)] * 4,
                          compiler_params=_params(2), name="merge_bwd")(dm, y_a, y_h, proj, proj)


def _hgout_fwd(o_raw, proj, hg_norm):
    s = o_raw.shape[0]
    bg = G_H // LANE

    def body(o_ref, g_ref, n_ref, out_ref):
        ov = o_ref[...]
        r = lax.rsqrt(jnp.mean(ov * ov, axis=-1, keepdims=True) + EPS)
        out_ref[...] = (ov * r * n_ref[...] * _sig(g_ref[...])).astype(out_ref.dtype)

    tr = min(s, TALL)
    blk = lambda f: pl.BlockSpec((tr, LANE), f)
    return pl.pallas_call(body, grid=(s // tr, HG_HEADS),
                          in_specs=[blk(lambda i, h: (i, h)), blk(lambda i, h: (i, h + bg)), pl.BlockSpec((1, LANE), lambda i, h: (0, 0))],
                          out_specs=blk(lambda i, h: (i, h)), out_shape=_sds((s, HG_W), BF16),
                          compiler_params=_params(2), name="hgout_fwd")(o_raw, proj, hg_norm)


def _hgout_bwd(d_out, o_raw, proj, hg_norm):
    s = o_raw.shape[0]
    bg = G_H // LANE

    def body(d_ref, o_ref, g_ref, n_ref, do_ref, dg_ref, dn_ref):
        first = jnp.logical_and(pl.program_id(0) == 0, pl.program_id(1) == 0)
        ov, dv, nv = o_ref[...], d_ref[...], n_ref[...]
        sg = _sig(g_ref[...])
        r = lax.rsqrt(jnp.mean(ov * ov, axis=-1, keepdims=True) + EPS)
        oh = ov * r
        d_on = dv * sg
        dg_ref[...] = (dv * (oh * nv) * (sg * (1.0 - sg))).astype(dg_ref.dtype)
        t = d_on * nv
        do_ref[...] = r * (t - oh * jnp.mean(t * oh, axis=-1, keepdims=True))
        _acc_rows(dn_ref, first, jnp.sum(d_on * oh, axis=0, keepdims=True))

    tr = min(s, TALL)
    blk = lambda f: pl.BlockSpec((tr, LANE), f)
    vec = pl.BlockSpec((1, LANE), lambda i, h: (0, 0))
    return pl.pallas_call(body, grid=(s // tr, HG_HEADS),
                          in_specs=[blk(lambda i, h: (i, h)), blk(lambda i, h: (i, h)), blk(lambda i, h: (i, h + bg)), vec],
                          out_specs=[blk(lambda i, h: (i, h)), blk(lambda i, h: (i, h)), vec],
                          out_shape=[_sds((s, HG_W), F32), _sds((s, HG_W), BF16), _sds((1, LANE), F32)],
                          compiler_params=_params(2), name="hgout_bwd")(d_out, o_raw, proj, hg_norm)


def _rope(t, cos, s_lo, s_hi):
    return t * cos + pltpu.roll(t, LANE - ROT // 2, 1) * s_lo + pltpu.roll(t, ROT // 2, 1) * s_hi


def _rope_wide(t, cos, s_lo, s_hi):
    return jnp.concatenate([_rope(t[:, k * LANE:(k + 1) * LANE], cos, s_lo, s_hi) for k in range(t.shape[1] // LANE)], axis=1)


def _attn_mask(has_prev):
    kj = lax.broadcasted_iota(jnp.int32, (2 * BLK, BLK), 0)
    qi = lax.broadcasted_iota(jnp.int32, (2 * BLK, BLK), 1)
    rel = BLK + qi - kj
    band = jnp.logical_and(rel >= 0, rel < BLK)
    return jnp.logical_and(band, jnp.logical_or(has_prev, kj >= BLK))


def _attn_specs():
    prev = lambda i: jnp.maximum(i - 1, 0)
    kb, vb = K_A // LANE, V_A // LANE
    blk = lambda f: pl.BlockSpec((BLK, LANE), f)
    tabs = [blk(lambda i: (i, 0))] * 3 + [blk(lambda i: (prev(i), 0))] * 3
    return [pl.BlockSpec((BLK, ATT_W), lambda i: (i, 0)), blk(lambda i: (i, kb)), blk(lambda i: (prev(i), kb)),
            blk(lambda i: (i, vb)), blk(lambda i: (prev(i), vb))] + tabs + [pl.BlockSpec((1, LANE), lambda i: (0, 0))]


def _attn_logits(qh, kg):
    return _dot(kg, qh, NT)


def _attn_probs(raw, mask, sk):
    logits = jnp.where(mask, raw * (HEAD_DIM ** -0.5), -jnp.inf)
    m = jnp.maximum(jnp.max(logits, axis=0, keepdims=True), sk)
    p = jnp.exp(logits - m)
    e_sink = jnp.exp(sk - m)
    inv = 1.0 / (jnp.sum(p, axis=0, keepdims=True) + e_sink)
    return p, inv, e_sink * inv


def _attn_fwd(proj, tabs, sinks):
    s = proj.shape[0]

    def body(q_ref, kc_ref, kp_ref, vc_ref, vp_ref, c0, l0, h0, c1, l1, h1, sk_ref, o_ref):
        i = pl.program_id(0)
        mask = _attn_mask(i > 0)
        q = _rope_wide(q_ref[...], c0[...], l0[...], h0[...]).astype(BF16)
        kk = jnp.concatenate([_rope(kp_ref[...], c1[...], l1[...], h1[...]), _rope(kc_ref[...], c0[...], l0[...], h0[...])], axis=0).astype(BF16)
        v_t = jnp.concatenate([vp_ref[...], vc_ref[...]], axis=0).T.astype(BF16)
        part = lambda t, h: t[:, h * HEAD_DIM:(h + 1) * HEAD_DIM]
        k_heads = [part(kk, g) for g in range(KV_HEADS)]

        def head(h):
            g = h // GROUP
            raw = _attn_logits(part(q, h), k_heads[g])
            yield
            p, inv, _ = _attn_probs(raw, mask, sk_ref[:, h:h + 1])
            yield
            out_t = _dot(v_t[g * HEAD_DIM:(g + 1) * HEAD_DIM], p.astype(BF16), NN)
            yield
            return out_t * inv

        o_ref[...] = jnp.concatenate(_interleave([head(h) for h in range(ATT_HEADS)]), axis=0).T.astype(o_ref.dtype)

    return pl.pallas_call(body, grid=(s // BLK,), in_specs=_attn_specs(),
                          out_specs=pl.BlockSpec((BLK, ATT_W), lambda i: (i, 0)), out_shape=_sds((s, ATT_W), BF16),
                          compiler_params=_params(1), name="attn_fwd")(proj, proj, proj, proj, proj, *tabs, *tabs, sinks)


def _attn_bwd(proj, tabs, sinks, d_att):
    s = proj.shape[0]

    def body(q_ref, kc_ref, kp_ref, vc_ref, vp_ref, c0, l0, h0, c1, l1, h1, sk_ref, do_ref, dq_ref, dk_ref, dv_ref, ds_ref):
        i = pl.program_id(0)

        @pl.when(i == 0)
        def _():
            dk_ref[...] = jnp.zeros_like(dk_ref)
            dv_ref[...] = jnp.zeros_like(dv_ref)
            ds_ref[...] = jnp.zeros_like(ds_ref)

        mask = _attn_mask(i > 0)
        q = _rope_wide(q_ref[...], c0[...], l0[...], h0[...]).astype(BF16)
        kk = jnp.concatenate([_rope(kp_ref[...], c1[...], l1[...], h1[...]), _rope(kc_ref[...], c0[...], l0[...], h0[...])], axis=0).astype(BF16)
        k_f32 = jnp.concatenate([_rope(kp_ref[...], c1[...], l1[...], h1[...]), _rope(kc_ref[...], c0[...], l0[...], h0[...])], axis=0)
        k_t = k_f32.T.astype(BF16)
        vv = jnp.concatenate([vp_ref[...], vc_ref[...]], axis=0).astype(BF16)
        d_o = do_ref[...].astype(BF16)
        lane = lax.broadcasted_iota(jnp.int32, (1, LANE), 1)
        part = lambda t, h: t[:, h * HEAD_DIM:(h + 1) * HEAD_DIM]
        k_heads = [part(kk, g) for g in range(KV_HEADS)]
        v_heads = [part(vv, g) for g in range(KV_HEADS)]

        def head(h):
            g = h // GROUP
            qh, doh = part(q, h), part(d_o, h)
            raw = _attn_logits(qh, k_heads[g])
            d_p = _dot(v_heads[g], doh, NT)
            yield
            p, inv, p_sink = _attn_probs(raw, mask, sk_ref[:, h:h + 1])
            prob = p * inv
            dv = _dot(prob.astype(BF16), doh, NN)
            yield
            dd = jnp.sum(prob * d_p, axis=0, keepdims=True)
            d_s = (prob * (d_p - dd)).astype(BF16)
            d_sink = jnp.where(lane == h, -jnp.sum(p_sink * dd, axis=1, keepdims=True), 0.0)
            dq_t = _dot(k_t[g * HEAD_DIM:(g + 1) * HEAD_DIM], d_s, NN)
            dk = _dot(d_s, qh, NN)
            yield
            return dq_t * (HEAD_DIM ** -0.5), dk * (HEAD_DIM ** -0.5), dv, d_sink

        per_head = _interleave([head(h) for h in range(ATT_HEADS)])
        dqs = [jnp.concatenate([t[0] for t in per_head], axis=0).T]
        group_sum = lambda k, g: functools.reduce(jnp.add, [t[k] for t in per_head[g * GROUP:(g + 1) * GROUP]])
        dks = [group_sum(1, g) for g in range(KV_HEADS)]
        dvs = [group_sum(2, g) for g in range(KV_HEADS)]
        d_sink = functools.reduce(jnp.add, [t[3] for t in per_head])
        dq_ref[...] = _rope_wide(jnp.concatenate(dqs, axis=1), c0[...], -l0[...], -h0[...]).astype(dq_ref.dtype)
        d_k = jnp.concatenate(dks, axis=1)
        d_v = jnp.concatenate(dvs, axis=1)
        cur = pl.ds(pl.multiple_of(i * BLK, BLK), BLK)
        prv = pl.ds(pl.multiple_of(jnp.maximum(i - 1, 0) * BLK, BLK), BLK)
        dk_ref[prv, :] += _rope(d_k[:BLK], c1[...], -l1[...], -h1[...])
        dk_ref[cur, :] += _rope(d_k[BLK:], c0[...], -l0[...], -h0[...])
        dv_ref[prv, :] += d_v[:BLK]
        dv_ref[cur, :] += d_v[BLK:]
        ds_ref[...] += d_sink

    full = pl.BlockSpec((s, LANE), lambda i: (0, 0))
    return pl.pallas_call(body, grid=(s // BLK,), in_specs=_attn_specs() + [pl.BlockSpec((BLK, ATT_W), lambda i: (i, 0))],
                          out_specs=[pl.BlockSpec((BLK, ATT_W), lambda i: (i, 0)), full, full, pl.BlockSpec((1, LANE), lambda i: (0, 0))],
                          out_shape=[_sds((s, ATT_W), BF16), _sds((s, LANE), F32), _sds((s, LANE), F32), _sds((1, LANE), F32)],
                          compiler_params=_params(1), name="attn_bwd")(proj, proj, proj, proj, proj, *tabs, *tabs, sinks, d_att)


def _tri_matmul(tri, t):
    hi = t.astype(BF16)
    r1 = t - hi.astype(F32)
    mid = r1.astype(BF16)
    lo = (r1 - mid.astype(F32)).astype(BF16)
    return _dot(tri, hi, NN) + _dot(tri, mid, NN) + _dot(tri, lo, NN)


def _lower_bound(hl):
    a, b = hl[0:1, :], hl[1:2, :]
    mx = jnp.maximum(a, b)
    ea, eb = jnp.exp(a - mx), jnp.exp(b - mx)
    return ea / (ea + eb)


def _hg_gates(q_raw, f_raw, lb, tri_lower):
    sg = _sig(f_raw)
    f = lb + (1.0 - lb) * sg
    sq = _sig(q_raw)
    b = _tri_matmul(tri_lower, jnp.log(f))
    return sg, f, 1.0 - f, sq, q_raw * sq, b


HG_PAIR_FWD = 8
HG_PAIR_BWD = 8


def _hg_specs(n_map, pair):
    blk = lambda off, p: pl.BlockSpec((HG_TB, LANE), lambda h, n: (n_map(n), off // LANE + pair * h + p))
    return [blk(off, p) for off in (Q_H, F_H, I_H) for p in range(pair)] + [pl.BlockSpec((2, pair * LANE), lambda h, n: (0, h))]


def _interleave(gens):
    out = [None] * len(gens)
    live = list(range(len(gens)))
    while live:
        for k in list(live):
            try:
                next(gens[k])
            except StopIteration as stop:
                out[k] = stop.value
                live.remove(k)
    return out


def _hg_spread():
    c = lax.broadcasted_iota(jnp.int32, (CHUNK, SUB * SUB), 0)
    l = lax.broadcasted_iota(jnp.int32, (CHUNK, SUB * SUB), 1)
    r = lax.broadcasted_iota(jnp.int32, (SUB, SUB * SUB), 0)
    lr = lax.broadcasted_iota(jnp.int32, (SUB, SUB * SUB), 1)
    shift = SUB.bit_length() - 1
    cols = [(c == lo + (l >> shift)).astype(BF16) for lo in range(0, CHUNK, SUB)]
    tile = [(c == lo + (l & (SUB - 1))).astype(BF16) for lo in range(0, CHUNK, SUB)]
    return cols, tile, (lr & (SUB - 1)) == r, (lr >> shift) == r


def _hg_intra(qs, kk, b, grad=None):
    lane = lax.broadcasted_iota(jnp.int32, (SUB, CHUNK), 1)
    row1 = lax.broadcasted_iota(jnp.int32, (SUB, 1), 0)
    kk_b = kk.astype(BF16)
    if grad is not None:
        d_a, d_at, (cols, tile, diag, block) = grad
    a_blocks, dq_blocks, dk_blocks, db_blocks = [], [], [], []
    dk_left = None
    for j in range(CHUNK // SUB):
        lo = j * SUB
        q_j, k_j, b_j = qs[lo:lo + SUB], kk[lo:lo + SUB], b[lo:lo + SUB]
        es = [jnp.where(row1 >= sx, jnp.exp(jnp.minimum(b_j - b_j[sx:sx + 1], 0.0)), 0.0) for sx in range(SUB)]
        pes = [q_j * e for e in es]
        pe = jnp.concatenate(pes, axis=0).astype(BF16)
        pairs = _dot(pe, kk_b, NT)
        yield
        a_j = jnp.zeros((SUB, CHUNK), F32)
        for sx in range(SUB):
            a_j = jnp.where(lane == lo + sx, pairs[sx * SUB:(sx + 1) * SUB], a_j)
        if grad is not None:
            da_j = d_a[lo:lo + SUB]
            ek = jnp.concatenate([e * k_j[sx:sx + 1] for sx, e in enumerate(es)], axis=0).astype(BF16)
            sel_t = jnp.where(diag, _dot(da_j.astype(BF16), cols[j], NN), 0.0).astype(BF16)
            sel_s = jnp.where(block, _dot(d_at[lo:lo + SUB].astype(BF16), tile[j], NN), 0.0).astype(BF16)
            pek = jnp.concatenate([p * k_j[sx:sx + 1] for sx, p in enumerate(pes)], axis=0).astype(BF16)
            yield
            dq_j = _dot(sel_t, ek, NN)
            dk_j = _dot(sel_s, pe, NN)
            db_j = _dot(sel_t, pek, NN) - _dot(sel_s, pek, NN)
            yield
        if j > 0:
            ref = b[lo - 1:lo]
            sc_q = jnp.exp(b_j - ref)
            sc_k = jnp.exp(jnp.minimum(ref - b, 0.0))
            qt = (q_j * sc_q).astype(BF16)
            kt = (kk * sc_k).astype(BF16)
            left = _dot(qt, kt, NT)
            yield
            a_j = a_j + jnp.where(lane < lo, left, 0.0)
            if grad is not None:
                da_left = jnp.where(lane < lo, da_j, 0.0).astype(BF16)
                dq_left = _dot(da_left, kt, NN) * sc_q
                dq_j = dq_j + dq_left
                db_j = db_j + q_j * dq_left
                t = _dot(da_left, qt, TN)
                yield
                t = t * sc_k
                dk_left = t if dk_left is None else dk_left + t
        a_blocks.append(a_j)
        if grad is not None:
            dq_blocks.append(dq_j)
            dk_blocks.append(dk_j)
            db_blocks.append(db_j)
    a = jnp.concatenate(a_blocks, axis=0)
    if grad is None:
        return a
    return a, jnp.concatenate(dq_blocks, axis=0), jnp.concatenate(dk_blocks, axis=0) + dk_left, jnp.concatenate(db_blocks, axis=0) - kk * dk_left


def _hgrn_fwd(proj, hl):
    s = proj.shape[0]
    n_chunk = HG_TB // CHUNK
    pair = HG_PAIR_FWD

    def body(*refs):
        q_refs, f_refs, i_refs = refs[:pair], refs[pair:2 * pair], refs[2 * pair:3 * pair]
        hl_ref, o_ref, st_out_ref, st_ref = refs[3 * pair:]

        @pl.when(pl.program_id(1) == 0)
        def _():
            st_ref[...] = jnp.zeros_like(st_ref)

        r_i = lax.broadcasted_iota(jnp.int32, (CHUNK, CHUNK), 0)
        c_i = lax.broadcasted_iota(jnp.int32, (CHUNK, CHUNK), 1)
        tri_lower = (r_i >= c_i).astype(BF16)

        def chunk(c, carry):
            rows = pl.ds(pl.multiple_of(c * CHUNK, CHUNK), CHUNK)
            def head(p):
                cols = slice(p * LANE, (p + 1) * LANE)
                lb = _lower_bound(hl_ref[:, cols])
                v = i_refs[p][rows, :].astype(BF16)
                _, _, kk, _, qs, b = _hg_gates(q_refs[p][rows, :], f_refs[p][rows, :], lb, tri_lower)
                yield
                st = st_ref[p]
                st_b = st.astype(BF16)
                st_out_ref[p, c] = st_b
                o_state = _dot((qs * jnp.exp(b)).astype(BF16), st_b, NT)
                b_last = b[CHUNK - 1:CHUNK, :]
                st_new = _dot(v, (kk * jnp.exp(b_last - b)).astype(BF16), TN)
                a = yield from _hg_intra(qs, kk, b)
                st_ref[p] = st * jnp.exp(b_last) + st_new
                o_ref[rows, cols] = o_state + _dot(a.astype(BF16), v, NN)

            _interleave([head(p) for p in range(pair)])
            return carry

        lax.fori_loop(0, n_chunk, chunk, 0)

    return pl.pallas_call(
        body, grid=(HG_HEADS // pair, s // HG_TB), in_specs=_hg_specs(lambda n: n, pair),
        out_specs=[pl.BlockSpec((HG_TB, pair * LANE), lambda h, n: (n, h)), pl.BlockSpec((pair, n_chunk, HG_K, HG_K), lambda h, n: (h, n, 0, 0))],
        out_shape=[_sds((s, HG_W), F32), _sds((HG_HEADS, s // CHUNK, HG_K, HG_K), BF16)],
        scratch_shapes=[pltpu.VMEM((pair, HG_K, HG_K), F32)],
        compiler_params=_params(2), name="hgrn_fwd")(*[proj] * (3 * pair), hl)


def _hgrn_bwd(proj, hl, states, d_o):
    s = proj.shape[0]
    n_chunk = HG_TB // CHUNK
    n_blk = s // HG_TB
    pair = HG_PAIR_BWD
    rev = lambda n: n_blk - 1 - n

    def body(*refs):
        q_refs, f_refs, i_refs = refs[:pair], refs[pair:2 * pair], refs[2 * pair:3 * pair]
        hl_ref, st_in_ref, do_ref, dq_ref, df_ref, di_ref, dhl_ref, dst_ref, dlb_ref = refs[3 * pair:]
        n = pl.program_id(1)

        @pl.when(n == 0)
        def _():
            dst_ref[...] = jnp.zeros_like(dst_ref)
            dlb_ref[...] = jnp.zeros_like(dlb_ref)

        r_i = lax.broadcasted_iota(jnp.int32, (CHUNK, CHUNK), 0)
        c_i = lax.broadcasted_iota(jnp.int32, (CHUNK, CHUNK), 1)
        tri_lower = (r_i >= c_i).astype(BF16)
        tri_upper = (r_i <= c_i).astype(BF16)
        row = lax.broadcasted_iota(jnp.int32, (CHUNK, 1), 0)
        spread = _hg_spread()

        def chunk(cc, carry):
            c = n_chunk - 1 - cc
            rows = pl.ds(pl.multiple_of(c * CHUNK, CHUNK), CHUNK)
            def head(p):
                cols = slice(p * LANE, (p + 1) * LANE)
                lb = _lower_bound(hl_ref[:, cols])
                q_raw = q_refs[p][rows, :]
                vb = i_refs[p][rows, :].astype(BF16)
                sg, f, kk, sq, qs, b = _hg_gates(q_raw, f_refs[p][rows, :], lb, tri_lower)
                yield
                e_b = jnp.exp(b)
                qe = qs * e_b
                b_last = b[CHUNK - 1:CHUNK, :]
                e_last = jnp.exp(b_last)
                e_kd = jnp.exp(b_last - b)
                kd = kk * e_kd
                st0 = st_in_ref[p, c]
                d_ob = do_ref[rows, cols].astype(BF16)
                dst = dst_ref[p]
                dst_b = dst.astype(BF16)
                d_a = jnp.where(r_i >= c_i, _dot(d_ob, vb, NT), 0.0)
                d_at = jnp.where(r_i <= c_i, _dot(vb, d_ob, NT), 0.0)
                d_v_st = _dot(kd.astype(BF16), dst_b, NT)
                d_kd = _dot(vb, dst_b, NN)
                d_qe = _dot(d_ob, st0, NN)
                dst_new = _dot(d_ob, qe.astype(BF16), TN)
                yield
                a, dqs, dkk, d_b = yield from _hg_intra(qs, kk, b, (d_a, d_at, spread))
                d_v = _dot(a.astype(BF16), d_ob, TN) + d_v_st
                dqs_st = d_qe * e_b
                dkk_st = d_kd * e_kd
                dqs = dqs + dqs_st
                dkk = dkk + dkk_st
                d_b_last = jnp.sum(d_kd * kd, axis=0, keepdims=True) + jnp.sum(dst * st0.astype(F32), axis=0, keepdims=True) * e_last
                d_b = d_b + qs * dqs_st - kk * dkk_st + jnp.where(row == CHUNK - 1, d_b_last, 0.0)
                d_g = _tri_matmul(tri_upper, d_b)
                dst_ref[p] = dst_new + dst * e_last
                yield
                d_f = d_g / f - dkk
                dlb_ref[:, cols] += jnp.sum(d_f * (1.0 - sg), axis=0, keepdims=True)
                dq_ref[rows, cols] = (dqs * (sq * (1.0 + q_raw * (1.0 - sq)))).astype(dq_ref.dtype)
                df_ref[rows, cols] = (d_f * (1.0 - lb) * (sg * (1.0 - sg))).astype(df_ref.dtype)
                di_ref[rows, cols] = d_v.astype(di_ref.dtype)

            _interleave([head(p) for p in range(pair)])
            return carry

        lax.fori_loop(0, n_chunk, chunk, 0)

        @pl.when(n == n_blk - 1)
        def _():
            lb = _lower_bound(hl_ref[...])
            d_hl0 = dlb_ref[...] * (lb * (1.0 - lb))
            dhl_ref[...] = jnp.concatenate([d_hl0, -d_hl0], axis=0)

    out_blk = pl.BlockSpec((HG_TB, pair * LANE), lambda h, n: (rev(n), h))
    return pl.pallas_call(
        body, grid=(HG_HEADS // pair, n_blk),
        in_specs=_hg_specs(rev, pair) + [pl.BlockSpec((pair, n_chunk, HG_K, HG_K), lambda h, n: (h, rev(n), 0, 0)), out_blk],
        out_specs=[out_blk, out_blk, out_blk, pl.BlockSpec((2, pair * LANE), lambda h, n: (0, h))],
        out_shape=[_sds((s, HG_W), BF16)] * 3 + [_sds((2, HG_W), F32)],
        scratch_shapes=[pltpu.VMEM((pair, HG_K, HG_K), F32), pltpu.VMEM((1, pair * LANE), F32)],
        compiler_params=_params(2), name="hgrn_bwd")(*[proj] * (3 * pair), hl, states, d_o)


def _mod_part(c_all, w_shard, b_shard):
    n = w_shard.shape[1]
    tn = 512

    def body(c_ref, w_ref, b_ref, o_ref):
        o_ref[...] = _dot(c_ref[...].astype(BF16), w_ref[...].astype(BF16), NN) + b_ref[...]

    return pl.pallas_call(body, grid=(n // tn,),
                          in_specs=[pl.BlockSpec((N_DEV, D), lambda j: (0, 0)), pl.BlockSpec((D, tn), lambda j: (0, j)), pl.BlockSpec((1, tn), lambda j: (0, j))],
                          out_specs=pl.BlockSpec((N_DEV, tn), lambda j: (0, j)), out_shape=_sds((N_DEV, n), F32),
                          compiler_params=_params(1, 32 << 20), name="mod_part")(c_all, w_shard, b_shard)


def _adam_math(g, w, m, v):
    c1 = 1.0 / (1.0 - ADAM_B1 ** ADAM_STEP)
    c2 = 1.0 / (1.0 - ADAM_B2 ** ADAM_STEP)
    m2 = ADAM_B1 * m + (1.0 - ADAM_B1) * g
    v2 = ADAM_B2 * v + (1.0 - ADAM_B2) * (g * g)
    return -ADAM_LR * ((m2 * c1) / (jnp.sqrt(v2 * c2) + ADAM_EPS) + ADAM_WD * w), m2, v2


def _update_w_ada(c_all_t, dmod_cols, w, m, v, deps=()):
    n = dmod_cols.shape[1]
    tn = 256

    def body(c_ref, d_ref, w_ref, m_ref, v_ref, *rest):
        g_ref, dl_ref, m2_ref, v2_ref = rest[len(deps):]
        cv = c_ref[...].astype(BF16).astype(F32)
        dv = d_ref[...].astype(BF16).astype(F32)
        g = cv[:, 0:1] * dv[0:1, :]
        for k in range(1, N_DEV):
            g = g + cv[:, k:k + 1] * dv[k:k + 1, :]
        g_ref[...] = g
        dl_ref[...], m2_ref[...], v2_ref[...] = _adam_math(g, w_ref[...], m_ref[...], v_ref[...])

    blk = pl.BlockSpec((D, tn), lambda j: (0, j))
    return pl.pallas_call(body, grid=(n // tn,),
                          in_specs=[pl.BlockSpec((D, N_DEV), lambda j: (0, 0)), pl.BlockSpec((N_DEV, tn), lambda j: (0, j)), blk, blk, blk]
                          + [pl.BlockSpec(memory_space=pl.ANY)] * len(deps),
                          out_specs=[blk] * 4, out_shape=[_sds((D, n), F32)] * 4,
                          compiler_params=_params(1, 48 << 20), name="adamw_w_ada")(c_all_t, dmod_cols, w, m, v, *deps)


def _row_tile(r, c, max_elems=1 << 18):
    if r * c <= max_elems or r % 8:
        return r
    best = 8
    for t in range(8, r + 1, 8):
        if r % t == 0 and t * c <= max_elems:
            best = t
    return best


WIDE_TILE = 5 << 17
PAIR_TILE = 3 << 19


def _adamw(pieces, w, m, v, name, own=None, max_elems=1 << 18):
    parts = list(pieces) if isinstance(pieces, (list, tuple)) else [pieces]
    owns = [] if own is None else (list(own) if isinstance(own, (list, tuple)) else [own])
    n_o, n_p = len(owns), len(parts)
    p, r = parts[0].shape[:2]
    c = sum(t.shape[2] for t in parts)
    tr = _row_tile(r, c, max_elems)

    def body(*refs):
        w_ref, m_ref, v_ref, *outs = refs[n_o + n_p:]
        cols = []
        for j, p_ref in enumerate(refs[n_o:n_o + n_p]):
            g = p_ref[0].astype(F32)
            if owns:
                g = refs[j][...].astype(F32) + g
            for k in range(1, p):
                g = g + p_ref[k].astype(F32)
            cols.append(g)
        g = cols[0] if n_p == 1 else jnp.concatenate(cols, axis=1)
        outs[0][...] = g
        outs[1][...], outs[2][...], outs[3][...] = _adam_math(g, w_ref[...], m_ref[...], v_ref[...])

    blk = pl.BlockSpec((tr, c), lambda i: (i, 0))
    in_specs = ([pl.BlockSpec((tr, t.shape[1]), lambda i: (i, 0)) for t in owns]
                + [pl.BlockSpec((p, tr, t.shape[2]), lambda i: (0, i, 0)) for t in parts] + [blk, blk, blk])
    return pl.pallas_call(body, grid=(r // tr,), in_specs=in_specs,
                          out_specs=[blk] * 4, out_shape=[_sds((r, c), F32)] * 4,
                          compiler_params=_params(1, 48 << 20), name=name)(*owns, *parts, w, m, v)


def _my_coords():
    return lax.axis_index("x"), lax.axis_index("y"), lax.axis_index("c")


def _flip(coords, k):
    x, y, c = coords
    return (1 - x if k & 4 else x, 1 - y if k & 2 else y, 1 - c if k & 1 else c)


def _lin(coords):
    return 4 * coords[0] + 2 * coords[1] + coords[2]


def _exchange_small(x3, bcast, name):
    n = x3.shape[2]

    def body(x_ref, o_ref, send_sems, recv_sems):
        me = _my_coords()
        my_id = _lin(me)
        o_ref[pl.ds(my_id, 1)] = x_ref[pl.ds(0 if bcast else my_id, 1)]
        copies = []
        for k in range(1, N_DEV):
            peer = _flip(me, k)
            src = x_ref.at[0 if bcast else _lin(peer)]
            cp = pltpu.make_async_remote_copy(src_ref=src, dst_ref=o_ref.at[my_id], send_sem=send_sems.at[k], recv_sem=recv_sems.at[k],
                                              device_id=peer, device_id_type=MESH)
            cp.start()
            copies.append(cp)
        for k in range(1, N_DEV):
            peer = _flip(me, k)
            pltpu.make_async_remote_copy(src_ref=x_ref.at[0], dst_ref=o_ref.at[_lin(peer)], send_sem=send_sems.at[k], recv_sem=recv_sems.at[k],
                                         device_id=peer, device_id_type=MESH).wait_recv()
        for cp in copies:
            cp.wait_send()

    vm = pl.BlockSpec(memory_space=pltpu.VMEM)
    return pl.pallas_call(body, in_specs=[vm], out_specs=vm, out_shape=_sds((N_DEV, 1, n), F32),
                          scratch_shapes=[pltpu.SemaphoreType.DMA((N_DEV,)), pltpu.SemaphoreType.DMA((N_DEV,))], name=name)(x3)


HBM_SPEC = pl.BlockSpec(memory_space=pltpu.HBM)
SEM_SPEC = pl.BlockSpec(memory_space=pltpu.SEMAPHORE)
ANY_SPEC = pl.BlockSpec(memory_space=pl.ANY)
DATAFLOW = pltpu.SideEffectType.DATAFLOW_SIDE_EFFECTING
GATHER_FLIPS = (1, 2, 4, 6)
PASS_FLIPS = (2, 4, 6)
TOKEN = (8, LANE)


def _hbm(t):
    return pltpu.with_memory_space_constraint(t, pltpu.HBM)


def _hbm_like(ts):
    return [pltpu.HBM(t.shape, t.dtype) for t in ts]


def _split_start(issue, srcs, lands, n_sem, name, deps=()):
    n, nb, nd = len(srcs), len(srcs) + len(lands), len(deps)

    def body(*refs):
        issue(refs[:n], refs[n:nb], refs[nb + nd], refs[nb + nd + 1])
        refs[-1][...] = jnp.zeros(TOKEN, F32)

    outs = pl.pallas_call(
        body, name=name,
        out_shape=(pltpu.SemaphoreType.DMA((n_sem,)), pltpu.SemaphoreType.DMA((n_sem,)), *_hbm_like(srcs), *_hbm_like(lands), _sds(TOKEN, F32)),
        in_specs=[HBM_SPEC] * nb + [ANY_SPEC] * nd,
        out_specs=(SEM_SPEC, SEM_SPEC, *[HBM_SPEC] * nb, pl.BlockSpec(memory_space=pltpu.VMEM)),
        input_output_aliases={i: 2 + i for i in range(nb)},
        compiler_params=pltpu.CompilerParams(has_side_effects=DATAFLOW))(*[_hbm(t) for t in srcs], *[_hbm(t) for t in lands], *deps)
    return dict(sems=outs[:2], thru=list(outs[2:2 + nb]), token=outs[-1], n=n)


def _split_wait(finish, handle, after, name):
    n = handle["n"]
    thru = handle["thru"]
    nb = len(thru)

    def body(*refs):
        finish(refs[:n], refs[n:nb], refs[nb], refs[nb + 1])

    outs = pl.pallas_call(
        body, name=name, out_shape=_hbm_like(thru), in_specs=[HBM_SPEC] * nb + [SEM_SPEC, SEM_SPEC] + [ANY_SPEC] * len(after),
        out_specs=[HBM_SPEC] * nb, input_output_aliases={i: i for i in range(nb)},
        compiler_params=pltpu.CompilerParams(has_side_effects=DATAFLOW))(*thru, *handle["sems"], *after)
    return list(outs[:n]), list(outs[n:])


def _gather_start(shards, name, deps=()):
    n = len(shards)
    my_id = _lin(_my_coords())
    lands = [lax.dynamic_update_slice(lax.empty((N_DEV,) + t.shape, t.dtype), t[None], (my_id, 0, 0)) for t in shards]

    def issue(src, land, send_sems, recv_sems):
        me = _my_coords()
        for w in range(n):
            for j, k in enumerate(GATHER_FLIPS):
                q = len(GATHER_FLIPS) * w + j
                pltpu.make_async_remote_copy(src_ref=src[w], dst_ref=land[w].at[_lin(me)], send_sem=send_sems.at[q], recv_sem=recv_sems.at[q],
                                             device_id=_flip(me, k), device_id_type=MESH).start()

    return _split_start(issue, shards, lands, len(GATHER_FLIPS) * n, name, deps)


def _gather_wait(handle, after, name):
    n = handle["n"]

    def finish(src, land, send_sems, recv_sems):
        me = _my_coords()
        for w in range(n):
            for j, k in enumerate(GATHER_FLIPS):
                q = len(GATHER_FLIPS) * w + j
                peer = _flip(me, k)
                cp = pltpu.make_async_remote_copy(src_ref=src[w], dst_ref=land[w].at[_lin(peer)], send_sem=send_sems.at[q], recv_sem=recv_sems.at[q],
                                                  device_id=peer, device_id_type=MESH)
                cp.wait_send()
                cp.wait_recv()

    return _split_wait(finish, handle, after, name)[1]


def _pass_copy(land, send_sems, recv_sems, w, j, arriving):
    me = _my_coords()
    blk = land[w].at[_lin(_flip(me, PASS_FLIPS[j] + (1 if arriving else 0)))]
    q = len(PASS_FLIPS) * w + j
    return pltpu.make_async_remote_copy(src_ref=blk, dst_ref=blk, send_sem=send_sems.at[q], recv_sem=recv_sems.at[q],
                                        device_id=_flip(me, 1), device_id_type=MESH)


def _pass_start(lands, name, deps=()):
    def issue(_, land, send_sems, recv_sems):
        for w in range(len(lands)):
            for j in range(len(PASS_FLIPS)):
                _pass_copy(land, send_sems, recv_sems, w, j, False).start()

    return _split_start(issue, [], lands, len(PASS_FLIPS) * len(lands), name, deps)


def _pass_wait(handle, after, name):
    def finish(_, land, send_sems, recv_sems):
        for w in range(len(handle["thru"])):
            for j in range(len(PASS_FLIPS)):
                _pass_copy(land, send_sems, recv_sems, w, j, False).wait_send()
                _pass_copy(land, send_sems, recv_sems, w, j, True).wait_recv()

    return _split_wait(finish, handle, after, name)[1]


def _gather_pass(lands, name):
    n = len(lands)
    n_p = len(PASS_FLIPS)

    def body(*refs):
        land = refs[n:2 * n]
        send_sems, recv_sems = refs[2 * n:]
        me = _my_coords()
        sibling = _flip(me, 1)
        sent = []
        for w in range(n):
            for j, k in enumerate(PASS_FLIPS):
                blk = land[w].at[_lin(_flip(me, k))]
                cp = pltpu.make_async_remote_copy(src_ref=blk, dst_ref=blk, send_sem=send_sems.at[n_p * w + j], recv_sem=recv_sems.at[n_p * w + j],
                                                  device_id=sibling, device_id_type=MESH)
                cp.start()
                sent.append(cp)
        for w in range(n):
            for j, k in enumerate(PASS_FLIPS):
                blk = land[w].at[_lin(_flip(me, k + 1))]
                pltpu.make_async_remote_copy(src_ref=blk, dst_ref=blk, send_sem=send_sems.at[n_p * w + j], recv_sem=recv_sems.at[n_p * w + j],
                                             device_id=sibling, device_id_type=MESH).wait_recv()
        for cp in sent:
            cp.wait_send()

    return pl.pallas_call(body, in_specs=[ANY_SPEC] * n, out_specs=[ANY_SPEC] * n, out_shape=[_sds(t.shape, t.dtype) for t in lands],
                          input_output_aliases={i: i for i in range(n)},
                          scratch_shapes=[pltpu.SemaphoreType.DMA((n_p * n,)), pltpu.SemaphoreType.DMA((n_p * n,))], name=name)(*lands)


CHIP_FLIPS = (0, 2, 4, 6)


def _pair_copy(src, land, send_sems, recv_sems, w, j):
    me = _my_coords()
    q = len(CHIP_FLIPS) * w + j
    return pltpu.make_async_remote_copy(src_ref=src[w].at[_lin(_flip(me, CHIP_FLIPS[j] + 1))], dst_ref=land[w].at[j], send_sem=send_sems.at[q],
                                        recv_sem=recv_sems.at[q], device_id=_flip(me, 1), device_id_type=MESH)


def _pair_exchange(grads, name):
    n = len(grads)

    def body(*refs):
        src, land = refs[:n], refs[n:2 * n]
        send_sems, recv_sems = refs[2 * n:]
        sent = [_pair_copy(src, land, send_sems, recv_sems, w, j) for w in range(n) for j in range(len(CHIP_FLIPS))]
        for cp in sent:
            cp.start()
        for cp in sent:
            cp.wait_recv()
        for cp in sent:
            cp.wait_send()

    outs = pl.pallas_call(body, in_specs=[ANY_SPEC] * n, out_specs=[ANY_SPEC] * n,
                          out_shape=[_sds((len(CHIP_FLIPS),) + g.shape[1:], g.dtype) for g in grads],
                          scratch_shapes=[pltpu.SemaphoreType.DMA((len(CHIP_FLIPS) * n,))] * 2, name=name)(*grads)
    return list(outs)


def _pair_start(grads, name, deps=()):
    n = len(grads)
    lands = [lax.empty((len(CHIP_FLIPS),) + g.shape[1:], g.dtype) for g in grads]

    def issue(src, land, send_sems, recv_sems):
        for w in range(n):
            for j in range(len(CHIP_FLIPS)):
                _pair_copy(src, land, send_sems, recv_sems, w, j).start()

    return _split_start(issue, grads, lands, len(CHIP_FLIPS) * n, name, deps)


def _pair_wait(handle, after, name):
    n = handle["n"]

    def finish(src, land, send_sems, recv_sems):
        for w in range(n):
            for j in range(len(CHIP_FLIPS)):
                cp = _pair_copy(src, land, send_sems, recv_sems, w, j)
                cp.wait_send()
                cp.wait_recv()

    return _split_wait(finish, handle, after, name)


def _pair_add(grad, theirs, name):
    p, r, c = theirs.shape
    tr = _row_tile(r, c, PAIR_TILE)
    me = _my_coords()
    ids = jnp.stack([_lin(_flip(me, k)) for k in CHIP_FLIPS]).astype(jnp.int32)

    def body(ids_ref, a_ref, b_ref, o_ref):
        o_ref[...] = (a_ref[...].astype(F32) + b_ref[...].astype(F32)).astype(o_ref.dtype)

    blk = pl.BlockSpec((None, tr, c), lambda j, i, ids_ref: (j, i, 0))
    return pl.pallas_call(
        body, out_shape=_sds((p, r, c), theirs.dtype), compiler_params=_params(2), name=name,
        grid_spec=pltpu.PrefetchScalarGridSpec(
            num_scalar_prefetch=1, grid=(p, r // tr),
            in_specs=[pl.BlockSpec((None, tr, c), lambda j, i, ids_ref: (ids_ref[j], i, 0)), blk], out_specs=blk))(ids, grad, theirs)


def _chips_start(parts, name, deps=()):
    n = len(parts)
    n_c = len(CHIP_FLIPS) - 1
    lands = [lax.empty((n_c,) + t.shape[1:], t.dtype) for t in parts]

    def issue(src, land, send_sems, recv_sems):
        me = _my_coords()
        for w in range(n):
            for j in range(1, n_c + 1):
                q = n_c * w + j - 1
                pltpu.make_async_remote_copy(src_ref=src[w].at[j], dst_ref=land[w].at[j - 1], send_sem=send_sems.at[q], recv_sem=recv_sems.at[q],
                                             device_id=_flip(me, CHIP_FLIPS[j]), device_id_type=MESH).start()

    return _split_start(issue, parts, lands, n_c * n, name, deps)


def _chips_wait(handle, after, name):
    n = handle["n"]
    n_c = len(CHIP_FLIPS) - 1

    def finish(src, land, send_sems, recv_sems):
        me = _my_coords()
        for w in range(n):
            for j in range(1, n_c + 1):
                q = n_c * w + j - 1
                cp = pltpu.make_async_remote_copy(src_ref=src[w].at[j], dst_ref=land[w].at[j - 1], send_sem=send_sems.at[q], recv_sem=recv_sems.at[q],
                                                  device_id=_flip(me, CHIP_FLIPS[j]), device_id_type=MESH)
                cp.wait_send()
                cp.wait_recv()

    return _split_wait(finish, handle, after, name)


def _after(t, *tokens):
    for tok in tokens:
        t = t + tok[0:1, 0:1]
    return t


def _rope_tables(positions):
    half = ROT // 2
    inv_freq = ROPE_THETA ** (-jnp.arange(0, ROT, 2, dtype=F32) / ROT)
    ang = positions.astype(F32).reshape(-1, 1) * inv_freq
    cos, sin = jnp.cos(ang), jnp.sin(ang)
    s = ang.shape[0]
    pad = jnp.zeros((s, HEAD_DIM - ROT), F32)
    zero = jnp.zeros((s, half), F32)
    two = lambda t: jnp.concatenate([t, t], axis=1)
    return (two(jnp.concatenate([cos, cos, pad + 1.0], axis=1)), two(jnp.concatenate([-sin, zero, pad], axis=1)),
            two(jnp.concatenate([zero, sin, pad], axis=1)))


def _local_step(x, tgt, tabs, mod, sinks_pad, hl, hg_norm, g_pre_mix, g_post_mix, g_pre_ffn, g_post_ffn, weights, prefetch, scatter, scatter_on):
    s = x.shape[0]
    h1 = _pre_fwd(x, g_pre_mix, mod, 1, 0, "pre_mix_fwd")
    (w_in_a,) = weights("in_a", h1)
    proj = _mm_nt(h1, w_in_a, 256, IN_COLS // 2, D // 2, F32, "proj_mm_a", a_col=0)
    (w_in_b,) = weights("in_b", proj)
    proj = _mm_nt(h1, w_in_b, 256, IN_COLS // 2, D // 2, F32, "proj_mm_b", add=proj, a_col=1)
    att = _attn_fwd(proj, tabs, _after(sinks_pad, prefetch("mix", proj)))
    o_raw, states = _hgrn_fwd(proj, hl)
    ohg = _hgout_fwd(o_raw, proj, hg_norm)
    w_attn_dm, w_hgrn_dm, w_out = weights("mix", ohg)
    natural = lambda w_dm: w_dm.transpose(1, 0, 2).reshape(w_dm.shape[1], D)
    pieces = lambda g: g.reshape(g.shape[0], N_DEV, D // N_DEV).transpose(1, 0, 2)
    w_attn, w_hgrn = natural(w_attn_dm), natural(w_hgrn_dm)
    y_a = _mm_nn(att, w_attn, s, 512, ATT_W, F32, "attn_proj_mm")
    y_h = _mm_nn(ohg, w_hgrn, s, 512, HG_W, F32, "hgrn_proj_mm")
    merged = _merge_fwd(y_a, y_h, proj)
    y = _mm_nn(merged, w_out, s, 512, D, F32, "out_mm")
    x1 = _post_fwd(x, y, g_post_mix, mod, 2, "post_mix_fwd")
    h2 = _pre_fwd(x1, g_pre_ffn, _after(mod, prefetch("ffn_in", x1)), 4, 3, "pre_ffn_fwd")
    (w_ffn_in_dm,) = weights("ffn_in", h2)
    gu = _mm_nn_dm(h2, w_ffn_in_dm, s // 2, F32, "ffn_in_mm")
    act = _swiglu_fwd(gu, deps=[prefetch("ffn_out", gu)])
    (w_ffn_out,) = weights("ffn_out", act)
    y2 = _mm_nn(act, w_ffn_out, 512, 512, FFN, F32, "ffn_out_mm")
    err, loss, dy2, d_gate2, dg_post_ffn = _post_loss_bwd(x1, y2, g_post_ffn, mod, 5, tgt, "post_ffn_loss_bwd")
    gw_ffn_out = _mm_tn(act, dy2, 512, D, BF16, "ffn_out_dw")
    t_pair = scatter([gw_ffn_out.reshape(N_DEV, FFN // N_DEV, D)], "ffn_out")
    d_act = _mm_nt(dy2, w_ffn_out, s, 512, D, F32, "ffn_out_dx", deps=[t_pair])
    dgu = _swiglu_bwd(d_act, gu)
    t_out = scatter_on("ffn_out", dgu)
    gw_ffn_in = _mm_tn_dm(h2, dgu, 1024, BF16, "ffn_in_dw")
    t_pair = scatter([gw_ffn_in], "ffn_in")
    dh2 = _mm_nt_dm(dgu, w_ffn_in_dm, s, 1024, F32, "ffn_in_dx", deps=[t_pair])
    mod = _after(mod, t_out)
    dx1, d_shift2, d_scale2, dg_pre_ffn = _pre_bwd([dh2], x1, err, g_pre_ffn, mod, 4, "pre_ffn_bwd")
    dy, d_gate1, dg_post_mix = _post_bwd(dx1, y, g_post_mix, mod, 2, "post_mix_bwd")
    t_in = scatter_on("ffn_in", dy)
    d_merged = _mm_nt(dy, w_out, s, 512, D, F32, "out_dx")
    gw_out = _mm_tn(merged, dy, 512, D, BF16, "out_dw")
    dy_a, dy_h, d_gate_a, d_gate_h = _merge_bwd(d_merged, y_a, y_h, proj)
    gw_attn = pieces(_mm_tn(att, dy_a, 512, D, BF16, "attn_proj_dw"))
    gw_hgrn = pieces(_mm_tn(ohg, dy_h, 512, D, BF16, "hgrn_proj_dw"))
    t_pair = scatter([gw_attn, gw_hgrn, gw_out.reshape(N_DEV, D // N_DEV, D)], "mix")
    d_att = _mm_nt(dy_a, w_attn, s, 512, D, F32, "attn_proj_dx")
    d_ohg = _mm_nt(dy_h, w_hgrn, s, 512, D, F32, "hgrn_proj_dx", deps=[t_pair])
    d_o, d_gh, d_hg_norm = _hgout_bwd(d_ohg, o_raw, proj, _after(hg_norm, t_in))
    d_qh, d_fh, d_ih, d_hl = _hgrn_bwd(proj, hl, states, d_o)
    t_mix = scatter_on("mix", d_qh)
    d_qa, d_ka, d_va, d_sinks = _attn_bwd(proj, tabs, _after(sinks_pad, t_mix), d_att)
    d_proj = jnp.concatenate([d_qa, d_ka.astype(BF16), d_va.astype(BF16), d_qh, d_fh, d_ih, d_gh, d_gate_a, d_gate_h], axis=1)
    dh1 = [_mm_nn(d_proj, w_half, s // 2, 512, IN_COLS // 2, F32, "proj_dx_" + tag) for tag, w_half in (("a", w_in_a), ("b", w_in_b))]
    grad_x, d_shift1, d_scale1, dg_pre_mix = _pre_bwd(dh1, x, dx1, g_pre_mix, mod, 1, "pre_mix_bwd")
    d_mod = jnp.concatenate([d_shift1, d_scale1, d_gate1, d_shift2, d_scale2, d_gate2], axis=1)
    small = [d_mod, dg_pre_mix, dg_post_mix, dg_pre_ffn, dg_post_ffn, d_hl.reshape(1, 2 * HG_W), d_hg_norm, d_sinks]
    return loss, grad_x, small, h1, d_proj


def kernel(x, c, positions, w_ada, b_ada, g_pre_mix, g_post_mix, g_pre_ffn, g_post_ffn, w_in, attn_sinks, w_attn_proj, hg_lower_bounds, hg_norm, w_hgrn_proj, w_out, w_ffn_in, w_ffn_out, loss_target, m_w_ada, m_b_ada, m_g_pre_mix, m_g_post_mix, m_g_pre_ffn, m_g_post_ffn, m_w_in, m_attn_sinks, m_w_attn_proj, m_hg_lower_bounds, m_hg_norm, m_w_hgrn_proj, m_w_out, m_w_ffn_in, m_w_ffn_out, v_w_ada, v_b_ada, v_g_pre_mix, v_g_post_mix, v_g_pre_ffn, v_g_post_ffn, v_w_in, v_attn_sinks, v_w_attn_proj, v_hg_lower_bounds, v_hg_norm, v_w_hgrn_proj, v_w_out, v_w_ffn_in, v_w_ffn_out):
    my_id = _lin(_my_coords())
    s = x.shape[1]
    n_ada = w_ada.shape[2]

    c_all = _exchange_small(c.reshape(1, 1, D), True, "gather_c").reshape(N_DEV, D)
    b_cols = lax.dynamic_slice(b_ada, (0, my_id * n_ada), (1, n_ada))
    mod_part = _mod_part(c_all, w_ada[0], b_cols)
    mod = _exchange_small(mod_part.reshape(N_DEV, 1, n_ada), False, "scatter_mod").reshape(1, N_MOD * D)
    groups = {"in_a": [w_in[0].T[:, :D // 2]], "in_b": [w_in[0].T[:, D // 2:]], "mix": [w_attn_proj[0], w_hgrn_proj[0], w_out[0]],
              "ffn_in": [w_ffn_in[0]], "ffn_out": [w_ffn_out[0]]}

    def start(group, dep):
        shards, dep = lax.optimization_barrier((groups[group], dep))
        return _gather_start([t.astype(BF16) for t in shards], "gather_start_" + group, deps=[dep])

    gathers = {"in_a": start("in_a", mod)}
    gathers["in_b"] = start("in_b", gathers["in_a"]["token"])
    gathers["mix"] = start("mix", gathers["in_b"]["token"])
    gathers["ffn_in"] = start("ffn_in", gathers["mix"]["token"])
    gathers["ffn_out"] = start("ffn_out", gathers["ffn_in"]["token"])

    passes = {}

    def prefetch(group, after):
        lands = _gather_wait(gathers[group], [after], "gather_wait_" + group)
        passes[group] = _pass_start(lands, "gather_pass_start_" + group)
        return passes[group]["token"]

    def weights(group, after):
        if group in passes:
            lands = _pass_wait(passes[group], [after], "gather_pass_wait_" + group)
        else:
            after = [after, gathers["ffn_out"]["token"]]
            lands = _gather_pass(_gather_wait(gathers[group], after, "gather_wait_" + group), "gather_pass_" + group)
        if group in ("in_a", "in_b"):
            return (lands[0].reshape(IN_COLS, D // 2),)
        if group == "mix":
            return lands[0], lands[1], lands[2].reshape(D, D)
        return (lands[0],) if group == "ffn_in" else (lands[0].reshape(FFN, D),)

    pairs, scatters = {}, {}

    def scatter(grads, group):
        pairs[group] = _pair_start(grads, "scatter_pair_" + group)
        return pairs[group]["token"]

    def scatter_on(group, after):
        if group in pairs:
            local, theirs = _pair_wait(pairs[group], [after], "scatter_pair_wait_" + group)
        else:
            local, theirs = after, _pair_exchange(after, "scatter_pair_" + group)
        parts = [_pair_add(g, t, "scatter_pair_add_%s_%d" % (group, k)) for k, (g, t) in enumerate(zip(local, theirs))]
        scatters[group] = _chips_start(parts, "scatter_start_" + group)
        return scatters[group]["token"]

    sinks_pad = jnp.pad(attn_sinks, ((0, 0), (0, LANE - ATT_HEADS)))
    loss, grad_x, small, h1, d_proj = _local_step(
        x[0], loss_target[0], _rope_tables(positions), _after(mod, gathers["ffn_out"]["token"]), sinks_pad, hg_lower_bounds, hg_norm, g_pre_mix, g_post_mix, g_pre_ffn, g_post_ffn,
        weights, prefetch, scatter, scatter_on)
    small = small + [jnp.pad(loss, ((0, 0), (0, LANE - 1)))]

    sizes = [t.shape[1] for t in small]
    parts = _exchange_small(jnp.concatenate(small, axis=1).reshape(1, 1, sum(sizes)), True, "gather_small_grads")
    dep = parts
    for b_col, half in enumerate(("in_a", "in_b")):
        gw_half = _mm_tn(d_proj, h1, 256, D // 2, BF16, "proj_dw_" + half, deps=[dep], b_col=b_col)
        dep = scatter_on(half, [gw_half.reshape(N_DEV, IN_COLS // N_DEV, D // 2)])
    offs = [sum(sizes[:k]) for k in range(len(sizes))]
    piece = lambda k, n=None: parts[:, :, offs[k]:offs[k] + (sizes[k] if n is None else n)]
    loss = jnp.sum(piece(8, 1))
    small_w = [(piece(0), b_ada, m_b_ada, v_b_ada), (piece(1), g_pre_mix, m_g_pre_mix, v_g_pre_mix),
               (piece(2), g_post_mix, m_g_post_mix, v_g_post_mix), (piece(3), g_pre_ffn, m_g_pre_ffn, v_g_pre_ffn),
               (piece(4), g_post_ffn, m_g_post_ffn, v_g_post_ffn),
               (piece(5).reshape(N_DEV, 2, HG_W), hg_lower_bounds, m_hg_lower_bounds, v_hg_lower_bounds),
               (piece(6), hg_norm, m_hg_norm, v_hg_norm), (piece(7, ATT_HEADS), attn_sinks, m_attn_sinks, v_attn_sinks)]
    names = ["b_ada", "g_pre_mix", "g_post_mix", "g_pre_ffn", "g_post_ffn", "hg_lower_bounds", "hg_norm", "attn_sinks"]
    res = {n: _adamw(p, w, m, v, "adamw_" + n) for n, (p, w, m, v) in zip(names, small_w)}

    dmod_cols = lax.dynamic_slice(parts.reshape(N_DEV, -1), (0, my_id * n_ada), (N_DEV, n_ada))
    res["w_ada"] = list(_update_w_ada(c_all.T, dmod_cols, w_ada[0], m_w_ada[0], v_w_ada[0], deps=[scatters["in_b"]["token"]]))

    big = {"ffn_out": [("w_ffn_out", w_ffn_out, m_w_ffn_out, v_w_ffn_out)], "ffn_in": [("w_ffn_in", w_ffn_in, m_w_ffn_in, v_w_ffn_in)],
           "mix": [("w_attn_proj", w_attn_proj, m_w_attn_proj, v_w_attn_proj), ("w_hgrn_proj", w_hgrn_proj, m_w_hgrn_proj, v_w_hgrn_proj),
                   ("w_out", w_out, m_w_out, v_w_out)]}
    after = [scatters["in_b"]["token"]]
    for group, members in big.items():
        local, lands = _chips_wait(scatters[group], after, "scatter_wait_" + group)
        for (n, w, m, v), mine, land in zip(members, local, lands):
            res[n] = _adamw(land, w[0], m[0], v[0], "adamw_" + n, own=mine[0])
            after = after + [res[n][1]]
    after = [res[n][1] for n in res]
    halves = [_chips_wait(scatters[half], after, "scatter_wait_" + half) for half in ("in_a", "in_b")]
    own = [local[0][0] for local, _ in halves]
    land = [lands[0] for _, lands in halves]
    res["w_in"] = [t.T for t in _adamw(land, w_in[0].T, m_w_in[0].T, v_w_in[0].T, "adamw_w_in", own=own, max_elems=WIDE_TILE)]

    order = ["w_ada", "b_ada", "g_pre_mix", "g_post_mix", "g_pre_ffn", "g_post_ffn", "w_in", "attn_sinks", "w_attn_proj",
             "hg_lower_bounds", "hg_norm", "w_hgrn_proj", "w_out", "w_ffn_in", "w_ffn_out"]
    lead = {"w_ada", "w_in", "w_attn_proj", "w_hgrn_proj", "w_out", "w_ffn_in", "w_ffn_out"}
    outs = [loss, grad_x[None]]
    for k in range(4):
        outs += [res[n][k][None] if n in lead else res[n][k] for n in order]
    return tuple(outs)
```

```python
import functools

import jax
import jax.numpy as jnp
from jax import lax
from jax.experimental import pallas as pl
from jax.experimental.pallas import tpu as pltpu

F32 = jnp.float32
BF16 = jnp.bfloat16

N_DEV = 8
D = 2048
ATT_HEADS = 16
KV_HEADS = 2
HEAD_DIM = 64
GROUP = ATT_HEADS // KV_HEADS
ATT_W = ATT_HEADS * HEAD_DIM
BLK = 128
ROT = HEAD_DIM // 4
ROPE_THETA = 500000.0
HG_HEADS = 8
HG_K = 128
HG_W = HG_HEADS * HG_K
CHUNK = 64
SUB = 16
FFN = 5632
N_MOD = 6
EPS = 1e-6
LANE = 128
Q_A, K_A, V_A, Q_H, F_H, I_H, G_H, GT_A, GT_H, IN_COLS = 0, 1024, 1152, 1280, 2304, 3328, 4352, 5376, 7424, 9472

ADAM_LR, ADAM_B1, ADAM_B2, ADAM_EPS, ADAM_WD, ADAM_STEP = 0.001, 0.9, 0.999, 1e-08, 0.01, 10

TR = 256
HG_TB = 512
VMEM_BIG = 56 << 20
MESH = pl.DeviceIdType.MESH


def _sds(shape, dtype):
    return jax.ShapeDtypeStruct(shape, dtype)


def _params(n_axes, vmem=None):
    return pltpu.CompilerParams(dimension_semantics=("arbitrary",) * n_axes, vmem_limit_bytes=vmem)


def _sig(t):
    return 1.0 / (1.0 + jnp.exp(-t))


def _dot(a, b, dims):
    return lax.dot_general(a, b, (dims, ((), ())), preferred_element_type=F32)


NN = ((1,), (0,))
NT = ((1,), (1,))
TN = ((0,), (0,))


def _matmul(a, b, a_spec, b_spec, o_spec, out_shape, grid, dims, acc_shape, name, deps=(), add=None):
    nk = grid[2]
    nd = len(deps)
    extra = [] if add is None else [add]

    def body(a_ref, b_ref, *rest):
        o_ref, scratch = rest[nd + len(extra)], rest[nd + len(extra) + 1:]
        part = _dot(a_ref[...], b_ref[...], dims)
        if add is not None:
            assert nk == 1
            part = part + rest[nd][...]
        if nk == 1:
            o_ref[...] = part.astype(o_ref.dtype)
        else:
            acc = scratch[0]
            k = pl.program_id(2)

            @pl.when(k == 0)
            def _():
                acc[...] = part

            @pl.when(k > 0)
            def _():
                acc[...] += part

            @pl.when(k == nk - 1)
            def _():
                o_ref[...] = acc[...].astype(o_ref.dtype)

    return pl.pallas_call(
        body, grid=grid, in_specs=[a_spec, b_spec] + [pl.BlockSpec(memory_space=pl.ANY)] * nd + [o_spec] * len(extra),
        out_specs=o_spec, out_shape=out_shape, scratch_shapes=[pltpu.VMEM(acc_shape, F32)] if nk > 1 else [],
        input_output_aliases={2 + nd: 0} if extra else {},
        compiler_params=_params(3, VMEM_BIG), name=name)(a, b, *deps, *extra)


def _mm_nn(a, b, tm, tn, tk, out_dtype, name):
    m, k = a.shape
    n = b.shape[1]
    return _matmul(a, b, pl.BlockSpec((tm, tk), lambda j, i, kk: (i, kk)), pl.BlockSpec((tk, tn), lambda j, i, kk: (kk, j)),
                   pl.BlockSpec((tm, tn), lambda j, i, kk: (i, j)), _sds((m, n), out_dtype),
                   (n // tn, m // tm, k // tk), NN, (tm, tn), name)


def _mm_nn_dm(a, b, tm, out_dtype, name):
    m, k = a.shape
    n = b.shape[2]
    return _matmul(a, b, pl.BlockSpec((tm, k), lambda j, i, kk: (i, 0)), pl.BlockSpec((None, k, n), lambda j, i, kk: (j, 0, 0)),
                   pl.BlockSpec((tm, n), lambda j, i, kk: (i, j)), _sds((m, N_DEV * n), out_dtype),
                   (N_DEV, m // tm, 1), NN, (tm, n), name)


def _mm_nt(a, b, tm, tn, tk, out_dtype, name, deps=(), add=None, a_col=0):
    m = a.shape[0]
    n, k = b.shape
    return _matmul(a, b, pl.BlockSpec((tm, tk), lambda j, i, kk: (i, kk + a_col * (k // tk))), pl.BlockSpec((tn, tk), lambda j, i, kk: (j, kk)),
                   pl.BlockSpec((tm, tn), lambda j, i, kk: (i, j)), _sds((m, n), out_dtype),
                   (n // tn, m // tm, k // tk), NT, (tm, tn), name, deps, add)


def _mm_nt_dm(a, b, tm, tn, out_dtype, name, deps=()):
    m = a.shape[0]
    n_out, n = b.shape[1], b.shape[2]
    return _matmul(a, b, pl.BlockSpec((tm, n), lambda j, i, kk: (i, kk)), pl.BlockSpec((None, tn, n), lambda j, i, kk: (kk, j, 0)),
                   pl.BlockSpec((tm, tn), lambda j, i, kk: (i, j)), _sds((m, n_out), out_dtype),
                   (n_out // tn, m // tm, N_DEV), NT, (tm, tn), name, deps)


def _mm_tn(a, b, tm, tn, out_dtype, name, deps=(), b_col=None):
    s, m = a.shape
    n = b.shape[1] if b_col is None else tn
    first = 0 if b_col is None else b_col
    return _matmul(a, b, pl.BlockSpec((s, tm), lambda j, i, kk: (0, i)), pl.BlockSpec((s, tn), lambda j, i, kk: (0, j + first)),
                   pl.BlockSpec((tm, tn), lambda j, i, kk: (i, j)), _sds((m, n), out_dtype),
                   (n // tn, m // tm, 1), TN, (tm, tn), name, deps)


def _mm_tn_dm(a, b, tm, out_dtype, name):
    s, m = a.shape
    n = b.shape[1] // N_DEV
    return _matmul(a, b, pl.BlockSpec((s, tm), lambda j, i, kk: (0, i)), pl.BlockSpec((s, n), lambda j, i, kk: (0, j)),
                   pl.BlockSpec((None, tm, n), lambda j, i, kk: (j, i, 0)), _sds((N_DEV, m, n), out_dtype),
                   (N_DEV, m // tm, 1), TN, (tm, n), name)


def _row_spec():
    return pl.BlockSpec((TR, D), lambda i: (i, 0))


def _vec_spec(k=0):
    return pl.BlockSpec((1, D), lambda i: (0, k))


def _acc_rows(ref, first, val):
    @pl.when(first)
    def _():
        ref[...] = val

    @pl.when(jnp.logical_not(first))
    def _():
        ref[...] += val


def _pre_fwd(x, g, mod, k_scale, k_shift, name):
    s = x.shape[0]

    def body(x_ref, g_ref, sc_ref, sh_ref, h_ref):
        xv = x_ref[...]
        r = lax.rsqrt(jnp.mean(xv * xv, axis=-1, keepdims=True) + EPS)
        n = xv * r * g_ref[...]
        h_ref[...] = (n * (1.0 + sc_ref[...]) + sh_ref[...]).astype(h_ref.dtype)

    return pl.pallas_call(body, grid=(s // TR,), in_specs=[_row_spec(), _vec_spec(), _vec_spec(k_scale), _vec_spec(k_shift)],
                          out_specs=_row_spec(), out_shape=_sds((s, D), BF16), compiler_params=_params(1), name=name)(x, g, mod, mod)


def _post_fwd(x, y, g, mod, k_gate, name):
    s = x.shape[0]

    def body(x_ref, y_ref, g_ref, gt_ref, o_ref):
        yv = y_ref[...]
        r = lax.rsqrt(jnp.mean(yv * yv, axis=-1, keepdims=True) + EPS)
        o_ref[...] = x_ref[...] + gt_ref[...] * (yv * r * g_ref[...])

    return pl.pallas_call(body, grid=(s // TR,), in_specs=[_row_spec(), _row_spec(), _vec_spec(), _vec_spec(k_gate)],
                          out_specs=_row_spec(), out_shape=_sds((s, D), F32), compiler_params=_params(1), name=name)(x, y, g, mod)


def _post_loss_bwd(x, y, g, mod, k_gate, tgt, name):
    s = x.shape[0]

    def body(x_ref, y_ref, g_ref, gt_ref, t_ref, e_ref, loss_ref, dy_ref, dgt_ref, dg_ref):
        first = pl.program_id(0) == 0
        yv, gv, gate = y_ref[...], g_ref[...], gt_ref[...]
        r = lax.rsqrt(jnp.mean(yv * yv, axis=-1, keepdims=True) + EPS)
        yh = yv * r
        err = x_ref[...] + gate * (yh * gv) - t_ref[...]
        e = err * (1.0 / D)
        e_ref[...] = e
        _acc_rows(loss_ref, first, 0.5 * jnp.sum(jnp.mean(err * err, axis=-1, keepdims=True), axis=0, keepdims=True))
        dn = e * gate
        dgn = dn * gv
        dy_ref[...] = (r * (dgn - yh * jnp.mean(dgn * yh, axis=-1, keepdims=True))).astype(dy_ref.dtype)
        _acc_rows(dgt_ref, first, jnp.sum(e * (yh * gv), axis=0, keepdims=True))
        _acc_rows(dg_ref, first, jnp.sum(dn * yh, axis=0, keepdims=True))

    return pl.pallas_call(body, grid=(s // TR,),
                          in_specs=[_row_spec(), _row_spec(), _vec_spec(), _vec_spec(k_gate), _row_spec()],
                          out_specs=[_row_spec(), pl.BlockSpec((1, 1), lambda i: (0, 0)), _row_spec(), _vec_spec(), _vec_spec()],
                          out_shape=[_sds((s, D), F32), _sds((1, 1), F32), _sds((s, D), BF16), _sds((1, D), F32), _sds((1, D), F32)],
                          compiler_params=_params(1), name=name)(x, y, g, mod, tgt)


def _pre_bwd(dh_parts, x, res, g, mod, k_scale, name):
    s = x.shape[0]
    n_p = len(dh_parts)

    def body(*refs):
        x_ref, res_ref, g_ref, sc_ref, dx_ref, dsh_ref, dsc_ref, dg_ref = refs[n_p:]
        first = pl.program_id(0) == 0
        dh_v = jnp.concatenate([r[...] for r in refs[:n_p]], axis=1)
        xv, gv = x_ref[...], g_ref[...]
        r = lax.rsqrt(jnp.mean(xv * xv, axis=-1, keepdims=True) + EPS)
        xh = xv * r
        dn = dh_v * (1.0 + sc_ref[...])
        dgn = dn * gv
        dx_ref[...] = res_ref[...] + r * (dgn - xh * jnp.mean(dgn * xh, axis=-1, keepdims=True))
        _acc_rows(dsh_ref, first, jnp.sum(dh_v, axis=0, keepdims=True))
        _acc_rows(dsc_ref, first, jnp.sum(dh_v * (xh * gv), axis=0, keepdims=True))
        _acc_rows(dg_ref, first, jnp.sum(dn * xh, axis=0, keepdims=True))

    return pl.pallas_call(body, grid=(s // TR,),
                          in_specs=[pl.BlockSpec((TR, t.shape[1]), lambda i: (i, 0)) for t in dh_parts]
                          + [_row_spec(), _row_spec(), _vec_spec(), _vec_spec(k_scale)],
                          out_specs=[_row_spec(), _vec_spec(), _vec_spec(), _vec_spec()],
                          out_shape=[_sds((s, D), F32)] + [_sds((1, D), F32)] * 3,
                          compiler_params=_params(1), name=name)(*dh_parts, x, res, g, mod)


def _post_bwd(dx, y, g, mod, k_gate, name):
    s = y.shape[0]

    def body(dx_ref, y_ref, g_ref, gt_ref, dy_ref, dgt_ref, dg_ref):
        first = pl.program_id(0) == 0
        yv, dxv, gv = y_ref[...], dx_ref[...], g_ref[...]
        r = lax.rsqrt(jnp.mean(yv * yv, axis=-1, keepdims=True) + EPS)
        yh = yv * r
        dn = dxv * gt_ref[...]
        dgn = dn * gv
        dy_ref[...] = (r * (dgn - yh * jnp.mean(dgn * yh, axis=-1, keepdims=True))).astype(dy_ref.dtype)
        _acc_rows(dgt_ref, first, jnp.sum(dxv * (yh * gv), axis=0, keepdims=True))
        _acc_rows(dg_ref, first, jnp.sum(dn * yh, axis=0, keepdims=True))

    return pl.pallas_call(body, grid=(s // TR,), in_specs=[_row_spec(), _row_spec(), _vec_spec(), _vec_spec(k_gate)],
                          out_specs=[_row_spec(), _vec_spec(), _vec_spec()],
                          out_shape=[_sds((s, D), BF16), _sds((1, D), F32), _sds((1, D), F32)],
                          compiler_params=_params(1), name=name)(dx, y, g, mod)


SW_TN = 1408
SW_TR = 512
TALL = 2048


def _swiglu_fwd(gu, deps=()):
    s = gu.shape[0]
    nb = FFN // SW_TN

    def body(g_ref, u_ref, *rest):
        a_ref = rest[len(deps)]
        gv = g_ref[...]
        a_ref[...] = (gv * _sig(gv) * u_ref[...]).astype(a_ref.dtype)

    return pl.pallas_call(body, grid=(s // SW_TR, nb),
                          in_specs=[pl.BlockSpec((SW_TR, SW_TN), lambda i, j: (i, j)), pl.BlockSpec((SW_TR, SW_TN), lambda i, j: (i, j + nb))]
                          + [pl.BlockSpec(memory_space=pl.ANY)] * len(deps),
                          out_specs=pl.BlockSpec((SW_TR, SW_TN), lambda i, j: (i, j)), out_shape=_sds((s, FFN), BF16),
                          compiler_params=_params(2, 48 << 20), name="swiglu_fwd")(gu, gu, *deps)


def _swiglu_bwd(dact, gu):
    s = gu.shape[0]
    nb = FFN // SW_TN
    n_steps = (s // SW_TR) * nb

    def body(da_ref, g_ref, u_ref, o_ref, buf, sems):
        i, j = pl.program_id(0), pl.program_id(1)
        step = i * nb + j
        slot = step % 2

        def tiles(sl):
            rows = pl.ds(pl.multiple_of(i * SW_TR, SW_TR), SW_TR)
            return [pltpu.make_async_copy(buf.at[sl, h], o_ref.at[rows, pl.ds(pl.multiple_of((j + nb * h) * SW_TN, LANE), SW_TN)], sems.at[sl, h])
                    for h in range(2)]

        @pl.when(step >= 2)
        def _():
            for cp in tiles(slot):
                cp.wait()

        gv, da = g_ref[...], da_ref[...]
        sg = _sig(gv)
        buf[slot, 0] = (da * u_ref[...] * (sg * (1.0 + gv * (1.0 - sg)))).astype(buf.dtype)
        buf[slot, 1] = (da * (gv * sg)).astype(buf.dtype)
        for cp in tiles(slot):
            cp.start()

        @pl.when(step == n_steps - 1)
        def _():
            for cp in tiles(slot) + (tiles(1 - slot) if n_steps > 1 else []):
                cp.wait()

    blk = lambda f: pl.BlockSpec((SW_TR, SW_TN), f)
    return pl.pallas_call(body, grid=(s // SW_TR, nb),
                          in_specs=[blk(lambda i, j: (i, j)), blk(lambda i, j: (i, j)), blk(lambda i, j: (i, j + nb))],
                          out_specs=pl.BlockSpec(memory_space=pl.ANY), out_shape=_sds((s, 2 * FFN), BF16),
                          scratch_shapes=[pltpu.VMEM((2, 2, SW_TR, SW_TN), BF16), pltpu.SemaphoreType.DMA((2, 2))],
                          compiler_params=_params(2, 48 << 20), name="swiglu_bwd")(dact, gu, gu)


MG_TN = 256


def _merge_fwd(y_a, y_h, proj):
    s = y_a.shape[0]
    tn = MG_TN
    ba, bh = GT_A // tn, GT_H // tn

    def body(ya_ref, yh_ref, ga_ref, gh_ref, m_ref):
        m_ref[...] = (_sig(ga_ref[...]) * ya_ref[...] + _sig(gh_ref[...]) * yh_ref[...]).astype(m_ref.dtype)

    tr = min(s, TALL)
    blk = lambda f: pl.BlockSpec((tr, tn), f)
    return pl.pallas_call(body, grid=(s // tr, D // tn),
                          in_specs=[blk(lambda i, j: (i, j)), blk(lambda i, j: (i, j)), blk(lambda i, j: (i, j + ba)), blk(lambda i, j: (i, j + bh))],
                          out_specs=blk(lambda i, j: (i, j)), out_shape=_sds((s, D), BF16),
                          compiler_params=_params(2), name="merge_fwd")(y_a, y_h, proj, proj)


def _merge_bwd(dm, y_a, y_h, proj):
    s = y_a.shape[0]
    tn = MG_TN
    ba, bh = GT_A // tn, GT_H // tn

    def body(dm_ref, ya_ref, yh_ref, ga_ref, gh_ref, dya_ref, dyh_ref, dga_ref, dgh_ref):
        dmv = dm_ref[...]
        sa, sh = _sig(ga_ref[...]), _sig(gh_ref[...])
        dya_ref[...] = (dmv * sa).astype(BF16)
        dyh_ref[...] = (dmv * sh).astype(BF16)
        dga_ref[...] = (dmv * ya_ref[...] * (sa * (1.0 - sa))).astype(BF16)
        dgh_ref[...] = (dmv * yh_ref[...] * (sh * (1.0 - sh))).astype(BF16)

    tr = min(s, TALL)
    blk = lambda f: pl.BlockSpec((tr, tn), f)
    nat = blk(lambda i, j: (i, j))
    return pl.pallas_call(body, grid=(s // tr, D // tn),
                          in_specs=[nat, nat, nat, blk(lambda i, j: (i, j + ba)), blk(lambda i, j: (i, j + bh))],
                          out_specs=[nat] * 4, out_shape=[_sds((s, D), BF16)] * 4,
                          compiler_params=_params(2), name="merge_bwd")(dm, y_a, y_h, proj, proj)


def _hgout_fwd(o_raw, proj, hg_norm):
    s = o_raw.shape[0]
    bg = G_H // LANE

    def body(o_ref, g_ref, n_ref, out_ref):
        ov = o_ref[...]
        r = lax.rsqrt(jnp.mean(ov * ov, axis=-1, keepdims=True) + EPS)
        out_ref[...] = (ov * r * n_ref[...] * _sig(g_ref[...])).astype(out_ref.dtype)

    tr = min(s, TALL)
    blk = lambda f: pl.BlockSpec((tr, LANE), f)
    return pl.pallas_call(body, grid=(s // tr, HG_HEADS),
                          in_specs=[blk(lambda i, h: (i, h)), blk(lambda i, h: (i, h + bg)), pl.BlockSpec((1, LANE), lambda i, h: (0, 0))],
                          out_specs=blk(lambda i, h: (i, h)), out_shape=_sds((s, HG_W), BF16),
                          compiler_params=_params(2), name="hgout_fwd")(o_raw, proj, hg_norm)


def _hgout_bwd(d_out, o_raw, proj, hg_norm):
    s = o_raw.shape[0]
    bg = G_H // LANE

    def body(d_ref, o_ref, g_ref, n_ref, do_ref, dg_ref, dn_ref):
        first = jnp.logical_and(pl.program_id(0) == 0, pl.program_id(1) == 0)
        ov, dv, nv = o_ref[...], d_ref[...], n_ref[...]
        sg = _sig(g_ref[...])
        r = lax.rsqrt(jnp.mean(ov * ov, axis=-1, keepdims=True) + EPS)
        oh = ov * r
        d_on = dv * sg
        dg_ref[...] = (dv * (oh * nv) * (sg * (1.0 - sg))).astype(dg_ref.dtype)
        t = d_on * nv
        do_ref[...] = r * (t - oh * jnp.mean(t * oh, axis=-1, keepdims=True))
        _acc_rows(dn_ref, first, jnp.sum(d_on * oh, axis=0, keepdims=True))

    tr = min(s, TALL)
    blk = lambda f: pl.BlockSpec((tr, LANE), f)
    vec = pl.BlockSpec((1, LANE), lambda i, h: (0, 0))
    return pl.pallas_call(body, grid=(s // tr, HG_HEADS),
                          in_specs=[blk(lambda i, h: (i, h)), blk(lambda i, h: (i, h)), blk(lambda i, h: (i, h + bg)), vec],
                          out_specs=[blk(lambda i, h: (i, h)), blk(lambda i, h: (i, h)), vec],
                          out_shape=[_sds((s, HG_W), F32), _sds((s, HG_W), BF16), _sds((1, LANE), F32)],
                          compiler_params=_params(2), name="hgout_bwd")(d_out, o_raw, proj, hg_norm)


def _rope(t, cos, s_lo, s_hi):
    return t * cos + pltpu.roll(t, LANE - ROT // 2, 1) * s_lo + pltpu.roll(t, ROT // 2, 1) * s_hi


def _rope_wide(t, cos, s_lo, s_hi):
    return jnp.concatenate([_rope(t[:, k * LANE:(k + 1) * LANE], cos, s_lo, s_hi) for k in range(t.shape[1] // LANE)], axis=1)


def _attn_mask(has_prev):
    kj = lax.broadcasted_iota(jnp.int32, (2 * BLK, BLK), 0)
    qi = lax.broadcasted_iota(jnp.int32, (2 * BLK, BLK), 1)
    rel = BLK + qi - kj
    band = jnp.logical_and(rel >= 0, rel < BLK)
    return jnp.logical_and(band, jnp.logical_or(has_prev, kj >= BLK))


def _attn_specs():
    prev = lambda i: jnp.maximum(i - 1, 0)
    kb, vb = K_A // LANE, V_A // LANE
    blk = lambda f: pl.BlockSpec((BLK, LANE), f)
    tabs = [blk(lambda i: (i, 0))] * 3 + [blk(lambda i: (prev(i), 0))] * 3
    return [pl.BlockSpec((BLK, ATT_W), lambda i: (i, 0)), blk(lambda i: (i, kb)), blk(lambda i: (prev(i), kb)),
            blk(lambda i: (i, vb)), blk(lambda i: (prev(i), vb))] + tabs + [pl.BlockSpec((1, LANE), lambda i: (0, 0))]


def _attn_logits(qh, kg):
    return _dot(kg, qh, NT)


def _attn_probs(raw, mask, sk):
    logits = jnp.where(mask, raw * (HEAD_DIM ** -0.5), -jnp.inf)
    m = jnp.maximum(jnp.max(logits, axis=0, keepdims=True), sk)
    p = jnp.exp(logits - m)
    e_sink = jnp.exp(sk - m)
    inv = 1.0 / (jnp.sum(p, axis=0, keepdims=True) + e_sink)
    return p, inv, e_sink * inv


def _attn_fwd(proj, tabs, sinks):
    s = proj.shape[0]

    def body(q_ref, kc_ref, kp_ref, vc_ref, vp_ref, c0, l0, h0, c1, l1, h1, sk_ref, o_ref):
        i = pl.program_id(0)
        mask = _attn_mask(i > 0)
        q = _rope_wide(q_ref[...], c0[...], l0[...], h0[...]).astype(BF16)
        kk = jnp.concatenate([_rope(kp_ref[...], c1[...], l1[...], h1[...]), _rope(kc_ref[...], c0[...], l0[...], h0[...])], axis=0).astype(BF16)
        v_t = jnp.concatenate([vp_ref[...], vc_ref[...]], axis=0).T.astype(BF16)
        part = lambda t, h: t[:, h * HEAD_DIM:(h + 1) * HEAD_DIM]
        k_heads = [part(kk, g) for g in range(KV_HEADS)]

        def head(h):
            g = h // GROUP
            raw = _attn_logits(part(q, h), k_heads[g])
            yield
            p, inv, _ = _attn_probs(raw, mask, sk_ref[:, h:h + 1])
            yield
            out_t = _dot(v_t[g * HEAD_DIM:(g + 1) * HEAD_DIM], p.astype(BF16), NN)
            yield
            return out_t * inv

        o_ref[...] = jnp.concatenate(_interleave([head(h) for h in range(ATT_HEADS)]), axis=0).T.astype(o_ref.dtype)

    return pl.pallas_call(body, grid=(s // BLK,), in_specs=_attn_specs(),
                          out_specs=pl.BlockSpec((BLK, ATT_W), lambda i: (i, 0)), out_shape=_sds((s, ATT_W), BF16),
                          compiler_params=_params(1), name="attn_fwd")(proj, proj, proj, proj, proj, *tabs, *tabs, sinks)


def _attn_bwd(proj, tabs, sinks, d_att):
    s = proj.shape[0]

    def body(q_ref, kc_ref, kp_ref, vc_ref, vp_ref, c0, l0, h0, c1, l1, h1, sk_ref, do_ref, dq_ref, dk_ref, dv_ref, ds_ref):
        i = pl.program_id(0)

        @pl.when(i == 0)
        def _():
            dk_ref[...] = jnp.zeros_like(dk_ref)
            dv_ref[...] = jnp.zeros_like(dv_ref)
            ds_ref[...] = jnp.zeros_like(ds_ref)

        mask = _attn_mask(i > 0)
        q = _rope_wide(q_ref[...], c0[...], l0[...], h0[...]).astype(BF16)
        kk = jnp.concatenate([_rope(kp_ref[...], c1[...], l1[...], h1[...]), _rope(kc_ref[...], c0[...], l0[...], h0[...])], axis=0).astype(BF16)
        k_f32 = jnp.concatenate([_rope(kp_ref[...], c1[...], l1[...], h1[...]), _rope(kc_ref[...], c0[...], l0[...], h0[...])], axis=0)
        k_t = k_f32.T.astype(BF16)
        vv = jnp.concatenate([vp_ref[...], vc_ref[...]], axis=0).astype(BF16)
        d_o = do_ref[...].astype(BF16)
        lane = lax.broadcasted_iota(jnp.int32, (1, LANE), 1)
        part = lambda t, h: t[:, h * HEAD_DIM:(h + 1) * HEAD_DIM]
        k_heads = [part(kk, g) for g in range(KV_HEADS)]
        v_heads = [part(vv, g) for g in range(KV_HEADS)]

        def head(h):
            g = h // GROUP
            qh, doh = part(q, h), part(d_o, h)
            raw = _attn_logits(qh, k_heads[g])
            d_p = _dot(v_heads[g], doh, NT)
            yield
            p, inv, p_sink = _attn_probs(raw, mask, sk_ref[:, h:h + 1])
            prob = p * inv
            dv = _dot(prob.astype(BF16), doh, NN)
            yield
            dd = jnp.sum(prob * d_p, axis=0, keepdims=True)
            d_s = (prob * (d_p - dd)).astype(BF16)
            d_sink = jnp.where(lane == h, -jnp.sum(p_sink * dd, axis=1, keepdims=True), 0.0)
            dq_t = _dot(k_t[g * HEAD_DIM:(g + 1) * HEAD_DIM], d_s, NN)
            dk = _dot(d_s, qh, NN)
            yield
            return dq_t * (HEAD_DIM ** -0.5), dk * (HEAD_DIM ** -0.5), dv, d_sink

        per_head = _interleave([head(h) for h in range(ATT_HEADS)])
        dqs = [jnp.concatenate([t[0] for t in per_head], axis=0).T]
        group_sum = lambda k, g: functools.reduce(jnp.add, [t[k] for t in per_head[g * GROUP:(g + 1) * GROUP]])
        dks = [group_sum(1, g) for g in range(KV_HEADS)]
        dvs = [group_sum(2, g) for g in range(KV_HEADS)]
        d_sink = functools.reduce(jnp.add, [t[3] for t in per_head])
        dq_ref[...] = _rope_wide(jnp.concatenate(dqs, axis=1), c0[...], -l0[...], -h0[...]).astype(dq_ref.dtype)
        d_k = jnp.concatenate(dks, axis=1)
        d_v = jnp.concatenate(dvs, axis=1)
        cur = pl.ds(pl.multiple_of(i * BLK, BLK), BLK)
        prv = pl.ds(pl.multiple_of(jnp.maximum(i - 1, 0) * BLK, BLK), BLK)
        dk_ref[prv, :] += _rope(d_k[:BLK], c1[...], -l1[...], -h1[...])
        dk_ref[cur, :] += _rope(d_k[BLK:], c0[...], -l0[...], -h0[...])
        dv_ref[prv, :] += d_v[:BLK]
        dv_ref[cur, :] += d_v[BLK:]
        ds_ref[...] += d_sink

    full = pl.BlockSpec((s, LANE), lambda i: (0, 0))
    return pl.pallas_call(body, grid=(s // BLK,), in_specs=_attn_specs() + [pl.BlockSpec((BLK, ATT_W), lambda i: (i, 0))],
                          out_specs=[pl.BlockSpec((BLK, ATT_W), lambda i: (i, 0)), full, full, pl.BlockSpec((1, LANE), lambda i: (0, 0))],
                          out_shape=[_sds((s, ATT_W), BF16), _sds((s, LANE), F32), _sds((s, LANE), F32), _sds((1, LANE), F32)],
                          compiler_params=_params(1), name="attn_bwd")(proj, proj, proj, proj, proj, *tabs, *tabs, sinks, d_att)


def _tri_matmul(tri, t):
    hi = t.astype(BF16)
    r1 = t - hi.astype(F32)
    mid = r1.astype(BF16)
    lo = (r1 - mid.astype(F32)).astype(BF16)
    return _dot(tri, hi, NN) + _dot(tri, mid, NN) + _dot(tri, lo, NN)


def _lower_bound(hl):
    a, b = hl[0:1, :], hl[1:2, :]
    mx = jnp.maximum(a, b)
    ea, eb = jnp.exp(a - mx), jnp.exp(b - mx)
    return ea / (ea + eb)


def _hg_gates(q_raw, f_raw, lb, tri_lower):
    sg = _sig(f_raw)
    f = lb + (1.0 - lb) * sg
    sq = _sig(q_raw)
    b = _tri_matmul(tri_lower, jnp.log(f))
    return sg, f, 1.0 - f, sq, q_raw * sq, b


HG_PAIR_FWD = 8
HG_PAIR_BWD = 8


def _hg_specs(n_map, pair):
    blk = lambda off, p: pl.BlockSpec((HG_TB, LANE), lambda h, n: (n_map(n), off // LANE + pair * h + p))
    return [blk(off, p) for off in (Q_H, F_H, I_H) for p in range(pair)] + [pl.BlockSpec((2, pair * LANE), lambda h, n: (0, h))]


def _interleave(gens):
    out = [None] * len(gens)
    live = list(range(len(gens)))
    while live:
        for k in list(live):
            try:
                next(gens[k])
            except StopIteration as stop:
                out[k] = stop.value
                live.remove(k)
    return out


def _hg_spread():
    c = lax.broadcasted_iota(jnp.int32, (CHUNK, SUB * SUB), 0)
    l = lax.broadcasted_iota(jnp.int32, (CHUNK, SUB * SUB), 1)
    r = lax.broadcasted_iota(jnp.int32, (SUB, SUB * SUB), 0)
    lr = lax.broadcasted_iota(jnp.int32, (SUB, SUB * SUB), 1)
    shift = SUB.bit_length() - 1
    cols = [(c == lo + (l >> shift)).astype(BF16) for lo in range(0, CHUNK, SUB)]
    tile = [(c == lo + (l & (SUB - 1))).astype(BF16) for lo in range(0, CHUNK, SUB)]
    return cols, tile, (lr & (SUB - 1)) == r, (lr >> shift) == r


def _hg_intra(qs, kk, b, grad=None):
    lane = lax.broadcasted_iota(jnp.int32, (SUB, CHUNK), 1)
    row1 = lax.broadcasted_iota(jnp.int32, (SUB, 1), 0)
    kk_b = kk.astype(BF16)
    if grad is not None:
        d_a, d_at, (cols, tile, diag, block) = grad
    a_blocks, dq_blocks, dk_blocks, db_blocks = [], [], [], []
    dk_left = None
    for j in range(CHUNK // SUB):
        lo = j * SUB
        q_j, k_j, b_j = qs[lo:lo + SUB], kk[lo:lo + SUB], b[lo:lo + SUB]
        es = [jnp.where(row1 >= sx, jnp.exp(jnp.minimum(b_j - b_j[sx:sx + 1], 0.0)), 0.0) for sx in range(SUB)]
        pes = [q_j * e for e in es]
        pe = jnp.concatenate(pes, axis=0).astype(BF16)
        pairs = _dot(pe, kk_b, NT)
        yield
        a_j = jnp.zeros((SUB, CHUNK), F32)
        for sx in range(SUB):
            a_j = jnp.where(lane == lo + sx, pairs[sx * SUB:(sx + 1) * SUB], a_j)
        if grad is not None:
            da_j = d_a[lo:lo + SUB]
            ek = jnp.concatenate([e * k_j[sx:sx + 1] for sx, e in enumerate(es)], axis=0).astype(BF16)
            sel_t = jnp.where(diag, _dot(da_j.astype(BF16), cols[j], NN), 0.0).astype(BF16)
            sel_s = jnp.where(block, _dot(d_at[lo:lo + SUB].astype(BF16), tile[j], NN), 0.0).astype(BF16)
            pek = jnp.concatenate([p * k_j[sx:sx + 1] for sx, p in enumerate(pes)], axis=0).astype(BF16)
            yield
            dq_j = _dot(sel_t, ek, NN)
            dk_j = _dot(sel_s, pe, NN)
            db_j = _dot(sel_t, pek, NN) - _dot(sel_s, pek, NN)
            yield
        if j > 0:
            ref = b[lo - 1:lo]
            sc_q = jnp.exp(b_j - ref)
            sc_k = jnp.exp(jnp.minimum(ref - b, 0.0))
            qt = (q_j * sc_q).astype(BF16)
            kt = (kk * sc_k).astype(BF16)
            left = _dot(qt, kt, NT)
            yield
            a_j = a_j + jnp.where(lane < lo, left, 0.0)
            if grad is not None:
                da_left = jnp.where(lane < lo, da_j, 0.0).astype(BF16)
                dq_left = _dot(da_left, kt, NN) * sc_q
                dq_j = dq_j + dq_left
                db_j = db_j + q_j * dq_left
                t = _dot(da_left, qt, TN)
                yield
                t = t * sc_k
                dk_left = t if dk_left is None else dk_left + t
        a_blocks.append(a_j)
        if grad is not None:
            dq_blocks.append(dq_j)
            dk_blocks.append(dk_j)
            db_blocks.append(db_j)
    a = jnp.concatenate(a_blocks, axis=0)
    if grad is None:
        return a
    return a, jnp.concatenate(dq_blocks, axis=0), jnp.concatenate(dk_blocks, axis=0) + dk_left, jnp.concatenate(db_blocks, axis=0) - kk * dk_left


def _hgrn_fwd(proj, hl):
    s = proj.shape[0]
    n_chunk = HG_TB // CHUNK
    pair = HG_PAIR_FWD

    def body(*refs):
        q_refs, f_refs, i_refs = refs[:pair], refs[pair:2 * pair], refs[2 * pair:3 * pair]
        hl_ref, o_ref, st_out_ref, st_ref = refs[3 * pair:]

        @pl.when(pl.program_id(1) == 0)
        def _():
            st_ref[...] = jnp.zeros_like(st_ref)

        r_i = lax.broadcasted_iota(jnp.int32, (CHUNK, CHUNK), 0)
        c_i = lax.broadcasted_iota(jnp.int32, (CHUNK, CHUNK), 1)
        tri_lower = (r_i >= c_i).astype(BF16)

        def chunk(c, carry):
            rows = pl.ds(pl.multiple_of(c * CHUNK, CHUNK), CHUNK)
            def head(p):
                cols = slice(p * LANE, (p + 1) * LANE)
                lb = _lower_bound(hl_ref[:, cols])
                v = i_refs[p][rows, :].astype(BF16)
                _, _, kk, _, qs, b = _hg_gates(q_refs[p][rows, :], f_refs[p][rows, :], lb, tri_lower)
                yield
                st = st_ref[p]
                st_b = st.astype(BF16)
                st_out_ref[p, c] = st_b
                o_state = _dot((qs * jnp.exp(b)).astype(BF16), st_b, NT)
                b_last = b[CHUNK - 1:CHUNK, :]
                st_new = _dot(v, (kk * jnp.exp(b_last - b)).astype(BF16), TN)
                a = yield from _hg_intra(qs, kk, b)
                st_ref[p] = st * jnp.exp(b_last) + st_new
                o_ref[rows, cols] = o_state + _dot(a.astype(BF16), v, NN)

            _interleave([head(p) for p in range(pair)])
            return carry

        lax.fori_loop(0, n_chunk, chunk, 0)

    return pl.pallas_call(
        body, grid=(HG_HEADS // pair, s // HG_TB), in_specs=_hg_specs(lambda n: n, pair),
        out_specs=[pl.BlockSpec((HG_TB, pair * LANE), lambda h, n: (n, h)), pl.BlockSpec((pair, n_chunk, HG_K, HG_K), lambda h, n: (h, n, 0, 0))],
        out_shape=[_sds((s, HG_W), F32), _sds((HG_HEADS, s // CHUNK, HG_K, HG_K), BF16)],
        scratch_shapes=[pltpu.VMEM((pair, HG_K, HG_K), F32)],
        compiler_params=_params(2), name="hgrn_fwd")(*[proj] * (3 * pair), hl)


def _hgrn_bwd(proj, hl, states, d_o):
    s = proj.shape[0]
    n_chunk = HG_TB // CHUNK
    n_blk = s // HG_TB
    pair = HG_PAIR_BWD
    rev = lambda n: n_blk - 1 - n

    def body(*refs):
        q_refs, f_refs, i_refs = refs[:pair], refs[pair:2 * pair], refs[2 * pair:3 * pair]
        hl_ref, st_in_ref, do_ref, dq_ref, df_ref, di_ref, dhl_ref, dst_ref, dlb_ref = refs[3 * pair:]
        n = pl.program_id(1)

        @pl.when(n == 0)
        def _():
            dst_ref[...] = jnp.zeros_like(dst_ref)
            dlb_ref[...] = jnp.zeros_like(dlb_ref)

        r_i = lax.broadcasted_iota(jnp.int32, (CHUNK, CHUNK), 0)
        c_i = lax.broadcasted_iota(jnp.int32, (CHUNK, CHUNK), 1)
        tri_lower = (r_i >= c_i).astype(BF16)
        tri_upper = (r_i <= c_i).astype(BF16)
        row = lax.broadcasted_iota(jnp.int32, (CHUNK, 1), 0)
        spread = _hg_spread()

        def chunk(cc, carry):
            c = n_chunk - 1 - cc
            rows = pl.ds(pl.multiple_of(c * CHUNK, CHUNK), CHUNK)
            def head(p):
                cols = slice(p * LANE, (p + 1) * LANE)
                lb = _lower_bound(hl_ref[:, cols])
                q_raw = q_refs[p][rows, :]
                vb = i_refs[p][rows, :].astype(BF16)
                sg, f, kk, sq, qs, b = _hg_gates(q_raw, f_refs[p][rows, :], lb, tri_lower)
                yield
                e_b = jnp.exp(b)
                qe = qs * e_b
                b_last = b[CHUNK - 1:CHUNK, :]
                e_last = jnp.exp(b_last)
                e_kd = jnp.exp(b_last - b)
                kd = kk * e_kd
                st0 = st_in_ref[p, c]
                d_ob = do_ref[rows, cols].astype(BF16)
                dst = dst_ref[p]
                dst_b = dst.astype(BF16)
                d_a = jnp.where(r_i >= c_i, _dot(d_ob, vb, NT), 0.0)
                d_at = jnp.where(r_i <= c_i, _dot(vb, d_ob, NT), 0.0)
                d_v_st = _dot(kd.astype(BF16), dst_b, NT)
                d_kd = _dot(vb, dst_b, NN)
                d_qe = _dot(d_ob, st0, NN)
                dst_new = _dot(d_ob, qe.astype(BF16), TN)
                yield
                a, dqs, dkk, d_b = yield from _hg_intra(qs, kk, b, (d_a, d_at, spread))
                d_v = _dot(a.astype(BF16), d_ob, TN) + d_v_st
                dqs_st = d_qe * e_b
                dkk_st = d_kd * e_kd
                dqs = dqs + dqs_st
                dkk = dkk + dkk_st
                d_b_last = jnp.sum(d_kd * kd, axis=0, keepdims=True) + jnp.sum(dst * st0.astype(F32), axis=0, keepdims=True) * e_last
                d_b = d_b + qs * dqs_st - kk * dkk_st + jnp.where(row == CHUNK - 1, d_b_last, 0.0)
                d_g = _tri_matmul(tri_upper, d_b)
                dst_ref[p] = dst_new + dst * e_last
                yield
                d_f = d_g / f - dkk
                dlb_ref[:, cols] += jnp.sum(d_f * (1.0 - sg), axis=0, keepdims=True)
                dq_ref[rows, cols] = (dqs * (sq * (1.0 + q_raw * (1.0 - sq)))).astype(dq_ref.dtype)
                df_ref[rows, cols] = (d_f * (1.0 - lb) * (sg * (1.0 - sg))).astype(df_ref.dtype)
                di_ref[rows, cols] = d_v.astype(di_ref.dtype)

            _interleave([head(p) for p in range(pair)])
            return carry

        lax.fori_loop(0, n_chunk, chunk, 0)

        @pl.when(n == n_blk - 1)
        def _():
            lb = _lower_bound(hl_ref[...])
            d_hl0 = dlb_ref[...] * (lb * (1.0 - lb))
            dhl_ref[...] = jnp.concatenate([d_hl0, -d_hl0], axis=0)

    out_blk = pl.BlockSpec((HG_TB, pair * LANE), lambda h, n: (rev(n), h))
    return pl.pallas_call(
        body, grid=(HG_HEADS // pair, n_blk),
        in_specs=_hg_specs(rev, pair) + [pl.BlockSpec((pair, n_chunk, HG_K, HG_K), lambda h, n: (h, rev(n), 0, 0)), out_blk],
        out_specs=[out_blk, out_blk, out_blk, pl.BlockSpec((2, pair * LANE), lambda h, n: (0, h))],
        out_shape=[_sds((s, HG_W), BF16)] * 3 + [_sds((2, HG_W), F32)],
        scratch_shapes=[pltpu.VMEM((pair, HG_K, HG_K), F32), pltpu.VMEM((1, pair * LANE), F32)],
        compiler_params=_params(2), name="hgrn_bwd")(*[proj] * (3 * pair), hl, states, d_o)


def _mod_part(c_all, w_shard, b_shard):
    n = w_shard.shape[1]
    tn = 512

    def body(c_ref, w_ref, b_ref, o_ref):
        o_ref[...] = _dot(c_ref[...].astype(BF16), w_ref[...].astype(BF16), NN) + b_ref[...]

    return pl.pallas_call(body, grid=(n // tn,),
                          in_specs=[pl.BlockSpec((N_DEV, D), lambda j: (0, 0)), pl.BlockSpec((D, tn), lambda j: (0, j)), pl.BlockSpec((1, tn), lambda j: (0, j))],
                          out_specs=pl.BlockSpec((N_DEV, tn), lambda j: (0, j)), out_shape=_sds((N_DEV, n), F32),
                          compiler_params=_params(1, 32 << 20), name="mod_part")(c_all, w_shard, b_shard)


def _adam_math(g, w, m, v):
    c1 = 1.0 / (1.0 - ADAM_B1 ** ADAM_STEP)
    c2 = 1.0 / (1.0 - ADAM_B2 ** ADAM_STEP)
    m2 = ADAM_B1 * m + (1.0 - ADAM_B1) * g
    v2 = ADAM_B2 * v + (1.0 - ADAM_B2) * (g * g)
    return -ADAM_LR * ((m2 * c1) / (jnp.sqrt(v2 * c2) + ADAM_EPS) + ADAM_WD * w), m2, v2


def _update_w_ada(c_all_t, dmod_cols, w, m, v, deps=()):
    n = dmod_cols.shape[1]
    tn = 256

    def body(c_ref, d_ref, w_ref, m_ref, v_ref, *rest):
        g_ref, dl_ref, m2_ref, v2_ref = rest[len(deps):]
        cv = c_ref[...].astype(BF16).astype(F32)
        dv = d_ref[...].astype(BF16).astype(F32)
        g = cv[:, 0:1] * dv[0:1, :]
        for k in range(1, N_DEV):
            g = g + cv[:, k:k + 1] * dv[k:k + 1, :]
        g_ref[...] = g
        dl_ref[...], m2_ref[...], v2_ref[...] = _adam_math(g, w_ref[...], m_ref[...], v_ref[...])

    blk = pl.BlockSpec((D, tn), lambda j: (0, j))
    return pl.pallas_call(body, grid=(n // tn,),
                          in_specs=[pl.BlockSpec((D, N_DEV), lambda j: (0, 0)), pl.BlockSpec((N_DEV, tn), lambda j: (0, j)), blk, blk, blk]
                          + [pl.BlockSpec(memory_space=pl.ANY)] * len(deps),
                          out_specs=[blk] * 4, out_shape=[_sds((D, n), F32)] * 4,
                          compiler_params=_params(1, 48 << 20), name="adamw_w_ada")(c_all_t, dmod_cols, w, m, v, *deps)


def _row_tile(r, c, max_elems=1 << 18):
    if r * c <= max_elems or r % 8:
        return r
    best = 8
    for t in range(8, r + 1, 8):
        if r % t == 0 and t * c <= max_elems:
            best = t
    return best


WIDE_TILE = 5 << 17
PAIR_TILE = 3 << 19


def _adamw(pieces, w, m, v, name, own=None, max_elems=1 << 18):
    parts = list(pieces) if isinstance(pieces, (list, tuple)) else [pieces]
    owns = [] if own is None else (list(own) if isinstance(own, (list, tuple)) else [own])
    n_o, n_p = len(owns), len(parts)
    p, r = parts[0].shape[:2]
    c = sum(t.shape[2] for t in parts)
    tr = _row_tile(r, c, max_elems)

    def body(*refs):
        w_ref, m_ref, v_ref, *outs = refs[n_o + n_p:]
        cols = []
        for j, p_ref in enumerate(refs[n_o:n_o + n_p]):
            g = p_ref[0].astype(F32)
            if owns:
                g = refs[j][...].astype(F32) + g
            for k in range(1, p):
                g = g + p_ref[k].astype(F32)
            cols.append(g)
        g = cols[0] if n_p == 1 else jnp.concatenate(cols, axis=1)
        outs[0][...] = g
        outs[1][...], outs[2][...], outs[3][...] = _adam_math(g, w_ref[...], m_ref[...], v_ref[...])

    blk = pl.BlockSpec((tr, c), lambda i: (i, 0))
    in_specs = ([pl.BlockSpec((tr, t.shape[1]), lambda i: (i, 0)) for t in owns]
                + [pl.BlockSpec((p, tr, t.shape[2]), lambda i: (0, i, 0)) for t in parts] + [blk, blk, blk])
    return pl.pallas_call(body, grid=(r // tr,), in_specs=in_specs,
                          out_specs=[blk] * 4, out_shape=[_sds((r, c), F32)] * 4,
                          compiler_params=_params(1, 48 << 20), name=name)(*owns, *parts, w, m, v)


def _my_coords():
    return lax.axis_index("x"), lax.axis_index("y"), lax.axis_index("c")


def _flip(coords, k):
    x, y, c = coords
    return (1 - x if k & 4 else x, 1 - y if k & 2 else y, 1 - c if k & 1 else c)


def _lin(coords):
    return 4 * coords[0] + 2 * coords[1] + coords[2]


def _exchange_small(x3, bcast, name):
    n = x3.shape[2]

    def body(x_ref, o_ref, send_sems, recv_sems):
        me = _my_coords()
        my_id = _lin(me)
        o_ref[pl.ds(my_id, 1)] = x_ref[pl.ds(0 if bcast else my_id, 1)]
        copies = []
        for k in range(1, N_DEV):
            peer = _flip(me, k)
            src = x_ref.at[0 if bcast else _lin(peer)]
            cp = pltpu.make_async_remote_copy(src_ref=src, dst_ref=o_ref.at[my_id], send_sem=send_sems.at[k], recv_sem=recv_sems.at[k],
                                              device_id=peer, device_id_type=MESH)
            cp.start()
            copies.append(cp)
        for k in range(1, N_DEV):
            peer = _flip(me, k)
            pltpu.make_async_remote_copy(src_ref=x_ref.at[0], dst_ref=o_ref.at[_lin(peer)], send_sem=send_sems.at[k], recv_sem=recv_sems.at[k],
                                         device_id=peer, device_id_type=MESH).wait_recv()
        for cp in copies:
            cp.wait_send()

    vm = pl.BlockSpec(memory_space=pltpu.VMEM)
    return pl.pallas_call(body, in_specs=[vm], out_specs=vm, out_shape=_sds((N_DEV, 1, n), F32),
                          scratch_shapes=[pltpu.SemaphoreType.DMA((N_DEV,)), pltpu.SemaphoreType.DMA((N_DEV,))], name=name)(x3)


HBM_SPEC = pl.BlockSpec(memory_space=pltpu.HBM)
SEM_SPEC = pl.BlockSpec(memory_space=pltpu.SEMAPHORE)
ANY_SPEC = pl.BlockSpec(memory_space=pl.ANY)
DATAFLOW = pltpu.SideEffectType.DATAFLOW_SIDE_EFFECTING
GATHER_FLIPS = (1, 2, 4, 6)
PASS_FLIPS = (2, 4, 6)
TOKEN = (8, LANE)


def _hbm(t):
    return pltpu.with_memory_space_constraint(t, pltpu.HBM)


def _hbm_like(ts):
    return [pltpu.HBM(t.shape, t.dtype) for t in ts]


def _split_start(issue, srcs, lands, n_sem, name, deps=()):
    n, nb, nd = len(srcs), len(srcs) + len(lands), len(deps)

    def body(*refs):
        issue(refs[:n], refs[n:nb], refs[nb + nd], refs[nb + nd + 1])
        refs[-1][...] = jnp.zeros(TOKEN, F32)

    outs = pl.pallas_call(
        body, name=name,
        out_shape=(pltpu.SemaphoreType.DMA((n_sem,)), pltpu.SemaphoreType.DMA((n_sem,)), *_hbm_like(srcs), *_hbm_like(lands), _sds(TOKEN, F32)),
        in_specs=[HBM_SPEC] * nb + [ANY_SPEC] * nd,
        out_specs=(SEM_SPEC, SEM_SPEC, *[HBM_SPEC] * nb, pl.BlockSpec(memory_space=pltpu.VMEM)),
        input_output_aliases={i: 2 + i for i in range(nb)},
        compiler_params=pltpu.CompilerParams(has_side_effects=DATAFLOW))(*[_hbm(t) for t in srcs], *[_hbm(t) for t in lands], *deps)
    return dict(sems=outs[:2], thru=list(outs[2:2 + nb]), token=outs[-1], n=n)


def _split_wait(finish, handle, after, name):
    n = handle["n"]
    thru = handle["thru"]
    nb = len(thru)

    def body(*refs):
        finish(refs[:n], refs[n:nb], refs[nb], refs[nb + 1])

    outs = pl.pallas_call(
        body, name=name, out_shape=_hbm_like(thru), in_specs=[HBM_SPEC] * nb + [SEM_SPEC, SEM_SPEC] + [ANY_SPEC] * len(after),
        out_specs=[HBM_SPEC] * nb, input_output_aliases={i: i for i in range(nb)},
        compiler_params=pltpu.CompilerParams(has_side_effects=DATAFLOW))(*thru, *handle["sems"], *after)
    return list(outs[:n]), list(outs[n:])


def _gather_start(shards, name, deps=()):
    n = len(shards)
    my_id = _lin(_my_coords())
    lands = [lax.dynamic_update_slice(lax.empty((N_DEV,) + t.shape, t.dtype), t[None], (my_id, 0, 0)) for t in shards]

    def issue(src, land, send_sems, recv_sems):
        me = _my_coords()
        for w in range(n):
            for j, k in enumerate(GATHER_FLIPS):
                q = len(GATHER_FLIPS) * w + j
                pltpu.make_async_remote_copy(src_ref=src[w], dst_ref=land[w].at[_lin(me)], send_sem=send_sems.at[q], recv_sem=recv_sems.at[q],
                                             device_id=_flip(me, k), device_id_type=MESH).start()

    return _split_start(issue, shards, lands, len(GATHER_FLIPS) * n, name, deps)


def _gather_wait(handle, after, name):
    n = handle["n"]

    def finish(src, land, send_sems, recv_sems):
        me = _my_coords()
        for w in range(n):
            for j, k in enumerate(GATHER_FLIPS):
                q = len(GATHER_FLIPS) * w + j
                peer = _flip(me, k)
                cp = pltpu.make_async_remote_copy(src_ref=src[w], dst_ref=land[w].at[_lin(peer)], send_sem=send_sems.at[q], recv_sem=recv_sems.at[q],
                                                  device_id=peer, device_id_type=MESH)
                cp.wait_send()
                cp.wait_recv()

    return _split_wait(finish, handle, after, name)[1]


def _pass_copy(land, send_sems, recv_sems, w, j, arriving):
    me = _my_coords()
    blk = land[w].at[_lin(_flip(me, PASS_FLIPS[j] + (1 if arriving else 0)))]
    q = len(PASS_FLIPS) * w + j
    return pltpu.make_async_remote_copy(src_ref=blk, dst_ref=blk, send_sem=send_sems.at[q], recv_sem=recv_sems.at[q],
                                        device_id=_flip(me, 1), device_id_type=MESH)


def _pass_start(lands, name, deps=()):
    def issue(_, land, send_sems, recv_sems):
        for w in range(len(lands)):
            for j in range(len(PASS_FLIPS)):
                _pass_copy(land, send_sems, recv_sems, w, j, False).start()

    return _split_start(issue, [], lands, len(PASS_FLIPS) * len(lands), name, deps)


def _pass_wait(handle, after, name):
    def finish(_, land, send_sems, recv_sems):
        for w in range(len(handle["thru"])):
            for j in range(len(PASS_FLIPS)):
                _pass_copy(land, send_sems, recv_sems, w, j, False).wait_send()
                _pass_copy(land, send_sems, recv_sems, w, j, True).wait_recv()

    return _split_wait(finish, handle, after, name)[1]


def _gather_pass(lands, name):
    n = len(lands)
    n_p = len(PASS_FLIPS)

    def body(*refs):
        land = refs[n:2 * n]
        send_sems, recv_sems = refs[2 * n:]
        me = _my_coords()
        sibling = _flip(me, 1)
        sent = []
        for w in range(n):
            for j, k in enumerate(PASS_FLIPS):
                blk = land[w].at[_lin(_flip(me, k))]
                cp = pltpu.make_async_remote_copy(src_ref=blk, dst_ref=blk, send_sem=send_sems.at[n_p * w + j], recv_sem=recv_sems.at[n_p * w + j],
                                                  device_id=sibling, device_id_type=MESH)
                cp.start()
                sent.append(cp)
        for w in range(n):
            for j, k in enumerate(PASS_FLIPS):
                blk = land[w].at[_lin(_flip(me, k + 1))]
                pltpu.make_async_remote_copy(src_ref=blk, dst_ref=blk, send_sem=send_sems.at[n_p * w + j], recv_sem=recv_sems.at[n_p * w + j],
                                             device_id=sibling, device_id_type=MESH).wait_recv()
        for cp in sent:
            cp.wait_send()

    return pl.pallas_call(body, in_specs=[ANY_SPEC] * n, out_specs=[ANY_SPEC] * n, out_shape=[_sds(t.shape, t.dtype) for t in lands],
                          input_output_aliases={i: i for i in range(n)},
                          scratch_shapes=[pltpu.SemaphoreType.DMA((n_p * n,)), pltpu.SemaphoreType.DMA((n_p * n,))], name=name)(*lands)


CHIP_FLIPS = (0, 2, 4, 6)


def _pair_copy(src, land, send_sems, recv_sems, w, j):
    me = _my_coords()
    q = len(CHIP_FLIPS) * w + j
    return pltpu.make_async_remote_copy(src_ref=src[w].at[_lin(_flip(me, CHIP_FLIPS[j] + 1))], dst_ref=land[w].at[j], send_sem=send_sems.at[q],
                                        recv_sem=recv_sems.at[q], device_id=_flip(me, 1), device_id_type=MESH)


def _pair_exchange(grads, name):
    n = len(grads)

    def body(*refs):
        src, land = refs[:n], refs[n:2 * n]
        send_sems, recv_sems = refs[2 * n:]
        sent = [_pair_copy(src, land, send_sems, recv_sems, w, j) for w in range(n) for j in range(len(CHIP_FLIPS))]
        for cp in sent:
            cp.start()
        for cp in sent:
            cp.wait_recv()
        for cp in sent:
            cp.wait_send()

    outs = pl.pallas_call(body, in_specs=[ANY_SPEC] * n, out_specs=[ANY_SPEC] * n,
                          out_shape=[_sds((len(CHIP_FLIPS),) + g.shape[1:], g.dtype) for g in grads],
                          scratch_shapes=[pltpu.SemaphoreType.DMA((len(CHIP_FLIPS) * n,))] * 2, name=name)(*grads)
    return list(outs)


def _pair_start(grads, name, deps=()):
    n = len(grads)
    lands = [lax.empty((len(CHIP_FLIPS),) + g.shape[1:], g.dtype) for g in grads]

    def issue(src, land, send_sems, recv_sems):
        for w in range(n):
            for j in range(len(CHIP_FLIPS)):
                _pair_copy(src, land, send_sems, recv_sems, w, j).start()

    return _split_start(issue, grads, lands, len(CHIP_FLIPS) * n, name, deps)


def _pair_wait(handle, after, name):
    n = handle["n"]

    def finish(src, land, send_sems, recv_sems):
        for w in range(n):
            for j in range(len(CHIP_FLIPS)):
                cp = _pair_copy(src, land, send_sems, recv_sems, w, j)
                cp.wait_send()
                cp.wait_recv()

    return _split_wait(finish, handle, after, name)


def _pair_add(grad, theirs, name):
    p, r, c = theirs.shape
    tr = _row_tile(r, c, PAIR_TILE)
    me = _my_coords()
    ids = jnp.stack([_lin(_flip(me, k)) for k in CHIP_FLIPS]).astype(jnp.int32)

    def body(ids_ref, a_ref, b_ref, o_ref):
        o_ref[...] = (a_ref[...].astype(F32) + b_ref[...].astype(F32)).astype(o_ref.dtype)

    blk = pl.BlockSpec((None, tr, c), lambda j, i, ids_ref: (j, i, 0))
    return pl.pallas_call(
        body, out_shape=_sds((p, r, c), theirs.dtype), compiler_params=_params(2), name=name,
        grid_spec=pltpu.PrefetchScalarGridSpec(
            num_scalar_prefetch=1, grid=(p, r // tr),
            in_specs=[pl.BlockSpec((None, tr, c), lambda j, i, ids_ref: (ids_ref[j], i, 0)), blk], out_specs=blk))(ids, grad, theirs)


def _chips_start(parts, name, deps=()):
    n = len(parts)
    n_c = len(CHIP_FLIPS) - 1
    lands = [lax.empty((n_c,) + t.shape[1:], t.dtype) for t in parts]

    def issue(src, land, send_sems, recv_sems):
        me = _my_coords()
        for w in range(n):
            for j in range(1, n_c + 1):
                q = n_c * w + j - 1
                pltpu.make_async_remote_copy(src_ref=src[w].at[j], dst_ref=land[w].at[j - 1], send_sem=send_sems.at[q], recv_sem=recv_sems.at[q],
                                             device_id=_flip(me, CHIP_FLIPS[j]), device_id_type=MESH).start()

    return _split_start(issue, parts, lands, n_c * n, name, deps)


def _chips_wait(handle, after, name):
    n = handle["n"]
    n_c = len(CHIP_FLIPS) - 1

    def finish(src, land, send_sems, recv_sems):
        me = _my_coords()
        for w in range(n):
            for j in range(1, n_c + 1):
                q = n_c * w + j - 1
                cp = pltpu.make_async_remote_copy(src_ref=src[w].at[j], dst_ref=land[w].at[j - 1], send_sem=send_sems.at[q], recv_sem=recv_sems.at[q],
                                                  device_id=_flip(me, CHIP_FLIPS[j]), device_id_type=MESH)
                cp.wait_send()
                cp.wait_recv()

    return _split_wait(finish, handle, after, name)


def _after(t, *tokens):
    for tok in tokens:
        t = t + tok[0:1, 0:1]
    return t


def _rope_tables(positions):
    half = ROT // 2
    inv_freq = ROPE_THETA ** (-jnp.arange(0, ROT, 2, dtype=F32) / ROT)
    ang = positions.astype(F32).reshape(-1, 1) * inv_freq
    cos, sin = jnp.cos(ang), jnp.sin(ang)
    s = ang.shape[0]
    pad = jnp.zeros((s, HEAD_DIM - ROT), F32)
    zero = jnp.zeros((s, half), F32)
    two = lambda t: jnp.concatenate([t, t], axis=1)
    return (two(jnp.concatenate([cos, cos, pad + 1.0], axis=1)), two(jnp.concatenate([-sin, zero, pad], axis=1)),
            two(jnp.concatenate([zero, sin, pad], axis=1)))


def _local_step(x, tgt, tabs, mod, sinks_pad, hl, hg_norm, g_pre_mix, g_post_mix, g_pre_ffn, g_post_ffn, weights, prefetch, scatter, scatter_on):
    s = x.shape[0]
    h1 = _pre_fwd(x, g_pre_mix, mod, 1, 0, "pre_mix_fwd")
    h1, tabs = lax.optimization_barrier((h1, tabs))
    (w_in_a,) = weights("in_a", h1)
    proj = _mm_nt(h1, w_in_a, 256, IN_COLS // 2, D // 2, F32, "proj_mm_a", a_col=0)
    (w_in_b,) = weights("in_b", proj)
    proj = _mm_nt(h1, w_in_b, 256, IN_COLS // 2, D // 2, F32, "proj_mm_b", add=proj, a_col=1)
    att = _attn_fwd(proj, tabs, _after(sinks_pad, prefetch("mix", proj)))
    o_raw, states = _hgrn_fwd(proj, hl)
    ohg = _hgout_fwd(o_raw, proj, hg_norm)
    w_attn_dm, w_hgrn_dm, w_out = weights("mix", ohg)
    natural = lambda w_dm: w_dm.transpose(1, 0, 2).reshape(w_dm.shape[1], D)
    pieces = lambda g: g.reshape(g.shape[0], N_DEV, D // N_DEV).transpose(1, 0, 2)
    w_attn, w_hgrn = natural(w_attn_dm), natural(w_hgrn_dm)
    y_a = _mm_nn(att, w_attn, s, 512, ATT_W, F32, "attn_proj_mm")
    y_h = _mm_nn(ohg, w_hgrn, s, 512, HG_W, F32, "hgrn_proj_mm")
    merged = _merge_fwd(y_a, y_h, proj)
    y = _mm_nn(merged, w_out, s, 512, D, F32, "out_mm")
    x1 = _post_fwd(x, y, g_post_mix, mod, 2, "post_mix_fwd")
    h2 = _pre_fwd(x1, g_pre_ffn, _after(mod, prefetch("ffn_in", x1)), 4, 3, "pre_ffn_fwd")
    (w_ffn_in_dm,) = weights("ffn_in", h2)
    gu = _mm_nn_dm(h2, w_ffn_in_dm, s // 2, F32, "ffn_in_mm")
    act = _swiglu_fwd(gu, deps=[prefetch("ffn_out", gu)])
    (w_ffn_out,) = weights("ffn_out", act)
    y2 = _mm_nn(act, w_ffn_out, 512, 512, FFN, F32, "ffn_out_mm")
    err, loss, dy2, d_gate2, dg_post_ffn = _post_loss_bwd(x1, y2, g_post_ffn, mod, 5, tgt, "post_ffn_loss_bwd")
    gw_ffn_out = _mm_tn(act, dy2, 512, D, BF16, "ffn_out_dw")
    t_pair = scatter([gw_ffn_out.reshape(N_DEV, FFN // N_DEV, D)], "ffn_out")
    d_act = _mm_nt(dy2, w_ffn_out, s, 512, D, F32, "ffn_out_dx", deps=[t_pair])
    dgu = _swiglu_bwd(d_act, gu)
    t_out = scatter_on("ffn_out", dgu)
    gw_ffn_in = _mm_tn_dm(h2, dgu, 1024, BF16, "ffn_in_dw")
    t_pair = scatter([gw_ffn_in], "ffn_in")
    dh2 = _mm_nt_dm(dgu, w_ffn_in_dm, s, 1024, F32, "ffn_in_dx", deps=[t_pair])
    mod = _after(mod, t_out)
    dx1, d_shift2, d_scale2, dg_pre_ffn = _pre_bwd([dh2], x1, err, g_pre_ffn, mod, 4, "pre_ffn_bwd")
    dy, d_gate1, dg_post_mix = _post_bwd(dx1, y, g_post_mix, mod, 2, "post_mix_bwd")
    t_in = scatter_on("ffn_in", dy)
    d_merged = _mm_nt(dy, w_out, s, 512, D, F32, "out_dx")
    gw_out = _mm_tn(merged, dy, 512, D, BF16, "out_dw")
    dy_a, dy_h, d_gate_a, d_gate_h = _merge_bwd(d_merged, y_a, y_h, proj)
    gw_attn = pieces(_mm_tn(att, dy_a, 512, D, BF16, "attn_proj_dw"))
    gw_hgrn = pieces(_mm_tn(ohg, dy_h, 512, D, BF16, "hgrn_proj_dw"))
    t_pair = scatter([gw_attn, gw_hgrn, gw_out.reshape(N_DEV, D // N_DEV, D)], "mix")
    d_att = _mm_nt(dy_a, w_attn, s, 512, D, F32, "attn_proj_dx")
    d_ohg = _mm_nt(dy_h, w_hgrn, s, 512, D, F32, "hgrn_proj_dx", deps=[t_pair])
    d_o, d_gh, d_hg_norm = _hgout_bwd(d_ohg, o_raw, proj, _after(hg_norm, t_in))
    d_qh, d_fh, d_ih, d_hl = _hgrn_bwd(proj, hl, states, d_o)
    t_mix = scatter_on("mix", d_qh)
    d_qa, d_ka, d_va, d_sinks = _attn_bwd(proj, tabs, _after(sinks_pad, t_mix), d_att)
    d_proj = jnp.concatenate([d_qa, d_ka.astype(BF16), d_va.astype(BF16), d_qh, d_fh, d_ih, d_gh, d_gate_a, d_gate_h], axis=1)
    dh1 = [_mm_nn(d_proj, w_half, s // 2, 512, IN_COLS // 2, F32, "proj_dx_" + tag) for tag, w_half in (("a", w_in_a), ("b", w_in_b))]
    grad_x, d_shift1, d_scale1, dg_pre_mix = _pre_bwd(dh1, x, dx1, g_pre_mix, mod, 1, "pre_mix_bwd")
    d_mod = jnp.concatenate([d_shift1, d_scale1, d_gate1, d_shift2, d_scale2, d_gate2], axis=1)
    small = [d_mod, dg_pre_mix, dg_post_mix, dg_pre_ffn, dg_post_ffn, d_hl.reshape(1, 2 * HG_W), d_hg_norm, d_sinks]
    return loss, grad_x, small, h1, d_proj


def kernel(x, c, positions, w_ada, b_ada, g_pre_mix, g_post_mix, g_pre_ffn, g_post_ffn, w_in, attn_sinks, w_attn_proj, hg_lower_bounds, hg_norm, w_hgrn_proj, w_out, w_ffn_in, w_ffn_out, loss_target, m_w_ada, m_b_ada, m_g_pre_mix, m_g_post_mix, m_g_pre_ffn, m_g_post_ffn, m_w_in, m_attn_sinks, m_w_attn_proj, m_hg_lower_bounds, m_hg_norm, m_w_hgrn_proj, m_w_out, m_w_ffn_in, m_w_ffn_out, v_w_ada, v_b_ada, v_g_pre_mix, v_g_post_mix, v_g_pre_ffn, v_g_post_ffn, v_w_in, v_attn_sinks, v_w_attn_proj, v_hg_lower_bounds, v_hg_norm, v_w_hgrn_proj, v_w_out, v_w_ffn_in, v_w_ffn_out):
    my_id = _lin(_my_coords())
    s = x.shape[1]
    n_ada = w_ada.shape[2]

    c_all = _exchange_small(c.reshape(1, 1, D), True, "gather_c").reshape(N_DEV, D)
    b_cols = lax.dynamic_slice(b_ada, (0, my_id * n_ada), (1, n_ada))
    mod_part = _mod_part(c_all, w_ada[0], b_cols)
    mod = _exchange_small(mod_part.reshape(N_DEV, 1, n_ada), False, "scatter_mod").reshape(1, N_MOD * D)
    groups = {"in_a": [w_in[0].T[:, :D // 2]], "in_b": [w_in[0].T[:, D // 2:]], "mix": [w_attn_proj[0], w_hgrn_proj[0], w_out[0]],
              "ffn_in": [w_ffn_in[0]], "ffn_out": [w_ffn_out[0]]}

    def start(group, dep):
        shards, dep = lax.optimization_barrier((groups[group], dep))
        return _gather_start([t.astype(BF16) for t in shards], "gather_start_" + group, deps=[dep])

    gathers = {"in_a": start("in_a", mod)}
    gathers["in_b"] = start("in_b", gathers["in_a"]["token"])
    gathers["mix"] = start("mix", gathers["in_b"]["token"])
    gathers["ffn_in"] = start("ffn_in", gathers["mix"]["token"])
    gathers["ffn_out"] = start("ffn_out", gathers["ffn_in"]["token"])

    passes = {}

    def prefetch(group, after):
        lands = _gather_wait(gathers[group], [after], "gather_wait_" + group)
        passes[group] = _pass_start(lands, "gather_pass_start_" + group)
        return passes[group]["token"]

    def weights(group, after):
        if group in passes:
            lands = _pass_wait(passes[group], [after], "gather_pass_wait_" + group)
        else:
            after = [after, gathers["ffn_out"]["token"]]
            lands = _gather_pass(_gather_wait(gathers[group], after, "gather_wait_" + group), "gather_pass_" + group)
        if group in ("in_a", "in_b"):
            return (lands[0].reshape(IN_COLS, D // 2),)
        if group == "mix":
            return lands[0], lands[1], lands[2].reshape(D, D)
        return (lands[0],) if group == "ffn_in" else (lands[0].reshape(FFN, D),)

    pairs, scatters = {}, {}

    def scatter(grads, group):
        pairs[group] = _pair_start(grads, "scatter_pair_" + group)
        return pairs[group]["token"]

    def scatter_on(group, after):
        if group in pairs:
            local, theirs = _pair_wait(pairs[group], [after], "scatter_pair_wait_" + group)
        else:
            local, theirs = after, _pair_exchange(after, "scatter_pair_" + group)
        parts = [_pair_add(g, t, "scatter_pair_add_%s_%d" % (group, k)) for k, (g, t) in enumerate(zip(local, theirs))]
        scatters[group] = _chips_start(parts, "scatter_start_" + group)
        return scatters[group]["token"]

    sinks_pad = jnp.pad(attn_sinks, ((0, 0), (0, LANE - ATT_HEADS)))
    started = gathers["ffn_out"]["token"]
    loss, grad_x, small, h1, d_proj = _local_step(
        x[0], loss_target[0], _rope_tables(positions + started[0:1, 0:1].astype(positions.dtype)), _after(mod, started), sinks_pad, hg_lower_bounds, hg_norm, g_pre_mix, g_post_mix, g_pre_ffn, g_post_ffn,
        weights, prefetch, scatter, scatter_on)
    small = small + [jnp.pad(loss, ((0, 0), (0, LANE - 1)))]

    sizes = [t.shape[1] for t in small]
    parts = _exchange_small(jnp.concatenate(small, axis=1).reshape(1, 1, sum(sizes)), True, "gather_small_grads")
    dep = parts
    for b_col, half in enumerate(("in_a", "in_b")):
        gw_half = _mm_tn(d_proj, h1, 256, D // 2, BF16, "proj_dw_" + half, deps=[dep], b_col=b_col)
        dep = scatter_on(half, [gw_half.reshape(N_DEV, IN_COLS // N_DEV, D // 2)])
    offs = [sum(sizes[:k]) for k in range(len(sizes))]
    piece = lambda k, n=None: parts[:, :, offs[k]:offs[k] + (sizes[k] if n is None else n)]
    loss = jnp.sum(piece(8, 1))
    small_w = [(piece(0), b_ada, m_b_ada, v_b_ada), (piece(1), g_pre_mix, m_g_pre_mix, v_g_pre_mix),
               (piece(2), g_post_mix, m_g_post_mix, v_g_post_mix), (piece(3), g_pre_ffn, m_g_pre_ffn, v_g_pre_ffn),
               (piece(4), g_post_ffn, m_g_post_ffn, v_g_post_ffn),
               (piece(5).reshape(N_DEV, 2, HG_W), hg_lower_bounds, m_hg_lower_bounds, v_hg_lower_bounds),
               (piece(6), hg_norm, m_hg_norm, v_hg_norm), (piece(7, ATT_HEADS), attn_sinks, m_attn_sinks, v_attn_sinks)]
    names = ["b_ada", "g_pre_mix", "g_post_mix", "g_pre_ffn", "g_post_ffn", "hg_lower_bounds", "hg_norm", "attn_sinks"]
    res = {n: _adamw(p, w, m, v, "adamw_" + n) for n, (p, w, m, v) in zip(names, small_w)}

    dmod_cols = lax.dynamic_slice(parts.reshape(N_DEV, -1), (0, my_id * n_ada), (N_DEV, n_ada))
    res["w_ada"] = list(_update_w_ada(c_all.T, dmod_cols, w_ada[0], m_w_ada[0], v_w_ada[0], deps=[scatters["in_b"]["token"]]))

    big = {"ffn_out": [("w_ffn_out", w_ffn_out, m_w_ffn_out, v_w_ffn_out)], "ffn_in": [("w_ffn_in", w_ffn_in, m_w_ffn_in, v_w_ffn_in)],
           "mix": [("w_attn_proj", w_attn_proj, m_w_attn_proj, v_w_attn_proj), ("w_hgrn_proj", w_hgrn_proj, m_w_hgrn_proj, v_w_hgrn_proj),
                   ("w_out", w_out, m_w_out, v_w_out)]}
    after = [scatters["in_b"]["token"]]
    for group, members in big.items():
        local, lands = _chips_wait(scatters[group], after, "scatter_wait_" + group)
        for (n, w, m, v), mine, land in zip(members, local, lands):
            res[n] = _adamw(land, w[0], m[0], v[0], "adamw_" + n, own=mine[0])
            after = after + [res[n][1]]
    after = [res[n][1] for n in res]
    halves = [_chips_wait(scatters[half], after, "scatter_wait_" + half) for half in ("in_a", "in_b")]
    own = [local[0][0] for local, _ in halves]
    land = [lands[0] for _, lands in halves]
    res["w_in"] = [t.T for t in _adamw(land, w_in[0].T, m_w_in[0].T, v_w_in[0].T, "adamw_w_in", own=own, max_elems=WIDE_TILE)]

    order = ["w_ada", "b_ada", "g_pre_mix", "g_post_mix", "g_pre_ffn", "g_post_ffn", "w_in", "attn_sinks", "w_attn_proj",
             "hg_lower_bounds", "hg_norm", "w_hgrn_proj", "w_out", "w_ffn_in", "w_ffn_out"]
    lead = {"w_ada", "w_in", "w_attn_proj", "w_hgrn_proj", "w_out", "w_ffn_in", "w_ffn_out"}
    outs = [loss, grad_x[None]]
    for k in range(4):
        outs += [res[n][k][None] if n in lead else res[n][k] for n in order]
    return tuple(outs)
```

```python
import functools

import jax
import jax.numpy as jnp
from jax import lax
from jax.experimental import pallas as pl
from jax.experimental.pallas import tpu as pltpu

F32 = jnp.float32
BF16 = jnp.bfloat16

N_DEV = 8
D = 2048
ATT_HEADS = 16
KV_HEADS = 2
HEAD_DIM = 64
GROUP = ATT_HEADS // KV_HEADS
ATT_W = ATT_HEADS * HEAD_DIM
BLK = 128
ROT = HEAD_DIM // 4
ROPE_THETA = 500000.0
HG_HEADS = 8
HG_K = 128
HG_W = HG_HEADS * HG_K
CHUNK = 64
SUB = 16
FFN = 5632
N_MOD = 6
EPS = 1e-6
LANE = 128
Q_A, K_A, V_A, Q_H, F_H, I_H, G_H, GT_A, GT_H, IN_COLS = 0, 1024, 1152, 1280, 2304, 3328, 4352, 5376, 7424, 9472

ADAM_LR, ADAM_B1, ADAM_B2, ADAM_EPS, ADAM_WD, ADAM_STEP = 0.001, 0.9, 0.999, 1e-08, 0.01, 10

TR = 256
HG_TB = 512
VMEM_BIG = 56 << 20
MESH = pl.DeviceIdType.MESH


def _sds(shape, dtype):
    return jax.ShapeDtypeStruct(shape, dtype)


def _params(n_axes, vmem=None):
    return pltpu.CompilerParams(dimension_semantics=("arbitrary",) * n_axes, vmem_limit_bytes=vmem)


def _sig(t):
    return 1.0 / (1.0 + jnp.exp(-t))


def _dot(a, b, dims):
    return lax.dot_general(a, b, (dims, ((), ())), preferred_element_type=F32)


NN = ((1,), (0,))
NT = ((1,), (1,))
TN = ((0,), (0,))


def _matmul(a, b, a_spec, b_spec, o_spec, out_shape, grid, dims, acc_shape, name, deps=(), add=None):
    nk = grid[2]
    nd = len(deps)
    extra = [] if add is None else [add]

    def body(a_ref, b_ref, *rest):
        o_ref, scratch = rest[nd + len(extra)], rest[nd + len(extra) + 1:]
        part = _dot(a_ref[...], b_ref[...], dims)
        if add is not None:
            assert nk == 1
            part = part + rest[nd][...]
        if nk == 1:
            o_ref[...] = part.astype(o_ref.dtype)
        else:
            acc = scratch[0]
            k = pl.program_id(2)

            @pl.when(k == 0)
            def _():
                acc[...] = part

            @pl.when(k > 0)
            def _():
                acc[...] += part

            @pl.when(k == nk - 1)
            def _():
                o_ref[...] = acc[...].astype(o_ref.dtype)

    return pl.pallas_call(
        body, grid=grid, in_specs=[a_spec, b_spec] + [pl.BlockSpec(memory_space=pl.ANY)] * nd + [o_spec] * len(extra),
        out_specs=o_spec, out_shape=out_shape, scratch_shapes=[pltpu.VMEM(acc_shape, F32)] if nk > 1 else [],
        input_output_aliases={2 + nd: 0} if extra else {},
        compiler_params=_params(3, VMEM_BIG), name=name)(a, b, *deps, *extra)


def _mm_nn(a, b, tm, tn, tk, out_dtype, name):
    m, k = a.shape
    n = b.shape[1]
    return _matmul(a, b, pl.BlockSpec((tm, tk), lambda j, i, kk: (i, kk)), pl.BlockSpec((tk, tn), lambda j, i, kk: (kk, j)),
                   pl.BlockSpec((tm, tn), lambda j, i, kk: (i, j)), _sds((m, n), out_dtype),
                   (n // tn, m // tm, k // tk), NN, (tm, tn), name)


def _mm_nn_dm(a, b, tm, out_dtype, name):
    m, k = a.shape
    n = b.shape[2]
    return _matmul(a, b, pl.BlockSpec((tm, k), lambda j, i, kk: (i, 0)), pl.BlockSpec((None, k, n), lambda j, i, kk: (j, 0, 0)),
                   pl.BlockSpec((tm, n), lambda j, i, kk: (i, j)), _sds((m, N_DEV * n), out_dtype),
                   (N_DEV, m // tm, 1), NN, (tm, n), name)


def _mm_nt(a, b, tm, tn, tk, out_dtype, name, deps=(), add=None, a_col=0):
    m = a.shape[0]
    n, k = b.shape
    return _matmul(a, b, pl.BlockSpec((tm, tk), lambda j, i, kk: (i, kk + a_col * (k // tk))), pl.BlockSpec((tn, tk), lambda j, i, kk: (j, kk)),
                   pl.BlockSpec((tm, tn), lambda j, i, kk: (i, j)), _sds((m, n), out_dtype),
                   (n // tn, m // tm, k // tk), NT, (tm, tn), name, deps, add)


def _mm_nt_dm(a, b, tm, tn, out_dtype, name, deps=()):
    m = a.shape[0]
    n_out, n = b.shape[1], b.shape[2]
    return _matmul(a, b, pl.BlockSpec((tm, n), lambda j, i, kk: (i, kk)), pl.BlockSpec((None, tn, n), lambda j, i, kk: (kk, j, 0)),
                   pl.BlockSpec((tm, tn), lambda j, i, kk: (i, j)), _sds((m, n_out), out_dtype),
                   (n_out // tn, m // tm, N_DEV), NT, (tm, tn), name, deps)


def _mm_tn(a, b, tm, tn, out_dtype, name, deps=(), b_col=None):
    s, m = a.shape
    n = b.shape[1] if b_col is None else tn
    first = 0 if b_col is None else b_col
    return _matmul(a, b, pl.BlockSpec((s, tm), lambda j, i, kk: (0, i)), pl.BlockSpec((s, tn), lambda j, i, kk: (0, j + first)),
                   pl.BlockSpec((tm, tn), lambda j, i, kk: (i, j)), _sds((m, n), out_dtype),
                   (n // tn, m // tm, 1), TN, (tm, tn), name, deps)


def _mm_tn_dm(a, b, tm, out_dtype, name):
    s, m = a.shape
    n = b.shape[1] // N_DEV
    return _matmul(a, b, pl.BlockSpec((s, tm), lambda j, i, kk: (0, i)), pl.BlockSpec((s, n), lambda j, i, kk: (0, j)),
                   pl.BlockSpec((None, tm, n), lambda j, i, kk: (j, i, 0)), _sds((N_DEV, m, n), out_dtype),
                   (N_DEV, m // tm, 1), TN, (tm, n), name)


def _row_spec():
    return pl.BlockSpec((TR, D), lambda i: (i, 0))


def _vec_spec(k=0):
    return pl.BlockSpec((1, D), lambda i: (0, k))


def _acc_rows(ref, first, val):
    @pl.when(first)
    def _():
        ref[...] = val

    @pl.when(jnp.logical_not(first))
    def _():
        ref[...] += val


def _pre_fwd(x, g, mod, k_scale, k_shift, name):
    s = x.shape[0]

    def body(x_ref, g_ref, sc_ref, sh_ref, h_ref):
        xv = x_ref[...]
        r = lax.rsqrt(jnp.mean(xv * xv, axis=-1, keepdims=True) + EPS)
        n = xv * r * g_ref[...]
        h_ref[...] = (n * (1.0 + sc_ref[...]) + sh_ref[...]).astype(h_ref.dtype)

    return pl.pallas_call(body, grid=(s // TR,), in_specs=[_row_spec(), _vec_spec(), _vec_spec(k_scale), _vec_spec(k_shift)],
                          out_specs=_row_spec(), out_shape=_sds((s, D), BF16), compiler_params=_params(1), name=name)(x, g, mod, mod)


def _post_fwd(x, y, g, mod, k_gate, name):
    s = x.shape[0]

    def body(x_ref, y_ref, g_ref, gt_ref, o_ref):
        yv = y_ref[...]
        r = lax.rsqrt(jnp.mean(yv * yv, axis=-1, keepdims=True) + EPS)
        o_ref[...] = x_ref[...] + gt_ref[...] * (yv * r * g_ref[...])

    return pl.pallas_call(body, grid=(s // TR,), in_specs=[_row_spec(), _row_spec(), _vec_spec(), _vec_spec(k_gate)],
                          out_specs=_row_spec(), out_shape=_sds((s, D), F32), compiler_params=_params(1), name=name)(x, y, g, mod)


def _post_loss_bwd(x, y, g, mod, k_gate, tgt, name):
    s = x.shape[0]

    def body(x_ref, y_ref, g_ref, gt_ref, t_ref, e_ref, loss_ref, dy_ref, dgt_ref, dg_ref):
        first = pl.program_id(0) == 0
        yv, gv, gate = y_ref[...], g_ref[...], gt_ref[...]
        r = lax.rsqrt(jnp.mean(yv * yv, axis=-1, keepdims=True) + EPS)
        yh = yv * r
        err = x_ref[...] + gate * (yh * gv) - t_ref[...]
        e = err * (1.0 / D)
        e_ref[...] = e
        _acc_rows(loss_ref, first, 0.5 * jnp.sum(jnp.mean(err * err, axis=-1, keepdims=True), axis=0, keepdims=True))
        dn = e * gate
        dgn = dn * gv
        dy_ref[...] = (r * (dgn - yh * jnp.mean(dgn * yh, axis=-1, keepdims=True))).astype(dy_ref.dtype)
        _acc_rows(dgt_ref, first, jnp.sum(e * (yh * gv), axis=0, keepdims=True))
        _acc_rows(dg_ref, first, jnp.sum(dn * yh, axis=0, keepdims=True))

    return pl.pallas_call(body, grid=(s // TR,),
                          in_specs=[_row_spec(), _row_spec(), _vec_spec(), _vec_spec(k_gate), _row_spec()],
                          out_specs=[_row_spec(), pl.BlockSpec((1, 1), lambda i: (0, 0)), _row_spec(), _vec_spec(), _vec_spec()],
                          out_shape=[_sds((s, D), F32), _sds((1, 1), F32), _sds((s, D), BF16), _sds((1, D), F32), _sds((1, D), F32)],
                          compiler_params=_params(1), name=name)(x, y, g, mod, tgt)


def _pre_bwd(dh_parts, x, res, g, mod, k_scale, name):
    s = x.shape[0]
    n_p = len(dh_parts)

    def body(*refs):
        x_ref, res_ref, g_ref, sc_ref, dx_ref, dsh_ref, dsc_ref, dg_ref = refs[n_p:]
        first = pl.program_id(0) == 0
        dh_v = jnp.concatenate([r[...] for r in refs[:n_p]], axis=1)
        xv, gv = x_ref[...], g_ref[...]
        r = lax.rsqrt(jnp.mean(xv * xv, axis=-1, keepdims=True) + EPS)
        xh = xv * r
        dn = dh_v * (1.0 + sc_ref[...])
        dgn = dn * gv
        dx_ref[...] = res_ref[...] + r * (dgn - xh * jnp.mean(dgn * xh, axis=-1, keepdims=True))
        _acc_rows(dsh_ref, first, jnp.sum(dh_v, axis=0, keepdims=True))
        _acc_rows(dsc_ref, first, jnp.sum(dh_v * (xh * gv), axis=0, keepdims=True))
        _acc_rows(dg_ref, first, jnp.sum(dn * xh, axis=0, keepdims=True))

    return pl.pallas_call(body, grid=(s // TR,),
                          in_specs=[pl.BlockSpec((TR, t.shape[1]), lambda i: (i, 0)) for t in dh_parts]
                          + [_row_spec(), _row_spec(), _vec_spec(), _vec_spec(k_scale)],
                          out_specs=[_row_spec(), _vec_spec(), _vec_spec(), _vec_spec()],
                          out_shape=[_sds((s, D), F32)] + [_sds((1, D), F32)] * 3,
                          compiler_params=_params(1), name=name)(*dh_parts, x, res, g, mod)


def _post_bwd(dx, y, g, mod, k_gate, name):
    s = y.shape[0]

    def body(dx_ref, y_ref, g_ref, gt_ref, dy_ref, dgt_ref, dg_ref):
        first = pl.program_id(0) == 0
        yv, dxv, gv = y_ref[...], dx_ref[...], g_ref[...]
        r = lax.rsqrt(jnp.mean(yv * yv, axis=-1, keepdims=True) + EPS)
        yh = yv * r
        dn = dxv * gt_ref[...]
        dgn = dn * gv
        dy_ref[...] = (r * (dgn - yh * jnp.mean(dgn * yh, axis=-1, keepdims=True))).astype(dy_ref.dtype)
        _acc_rows(dgt_ref, first, jnp.sum(dxv * (yh * gv), axis=0, keepdims=True))
        _acc_rows(dg_ref, first, jnp.sum(dn * yh, axis=0, keepdims=True))

    return pl.pallas_call(body, grid=(s // TR,), in_specs=[_row_spec(), _row_spec(), _vec_spec(), _vec_spec(k_gate)],
                          out_specs=[_row_spec(), _vec_spec(), _vec_spec()],
                          out_shape=[_sds((s, D), BF16), _sds((1, D), F32), _sds((1, D), F32)],
                          compiler_params=_params(1), name=name)(dx, y, g, mod)


SW_TN = 1408
SW_TR = 512
TALL = 2048


def _swiglu_fwd(gu, deps=()):
    s = gu.shape[0]
    nb = FFN // SW_TN

    def body(g_ref, u_ref, *rest):
        a_ref = rest[len(deps)]
        gv = g_ref[...]
        a_ref[...] = (gv * _sig(gv) * u_ref[...]).astype(a_ref.dtype)

    return pl.pallas_call(body, grid=(s // SW_TR, nb),
                          in_specs=[pl.BlockSpec((SW_TR, SW_TN), lambda i, j: (i, j)), pl.BlockSpec((SW_TR, SW_TN), lambda i, j: (i, j + nb))]
                          + [pl.BlockSpec(memory_space=pl.ANY)] * len(deps),
                          out_specs=pl.BlockSpec((SW_TR, SW_TN), lambda i, j: (i, j)), out_shape=_sds((s, FFN), BF16),
                          compiler_params=_params(2, 48 << 20), name="swiglu_fwd")(gu, gu, *deps)


def _swiglu_bwd(dact, gu):
    s = gu.shape[0]
    nb = FFN // SW_TN
    n_steps = (s // SW_TR) * nb

    def body(da_ref, g_ref, u_ref, o_ref, buf, sems):
        i, j = pl.program_id(0), pl.program_id(1)
        step = i * nb + j
        slot = step % 2

        def tiles(sl):
            rows = pl.ds(pl.multiple_of(i * SW_TR, SW_TR), SW_TR)
            return [pltpu.make_async_copy(buf.at[sl, h], o_ref.at[rows, pl.ds(pl.multiple_of((j + nb * h) * SW_TN, LANE), SW_TN)], sems.at[sl, h])
                    for h in range(2)]

        @pl.when(step >= 2)
        def _():
            for cp in tiles(slot):
                cp.wait()

        gv, da = g_ref[...], da_ref[...]
        sg = _sig(gv)
        buf[slot, 0] = (da * u_ref[...] * (sg * (1.0 + gv * (1.0 - sg)))).astype(buf.dtype)
        buf[slot, 1] = (da * (gv * sg)).astype(buf.dtype)
        for cp in tiles(slot):
            cp.start()

        @pl.when(step == n_steps - 1)
        def _():
            for cp in tiles(slot) + (tiles(1 - slot) if n_steps > 1 else []):
                cp.wait()

    blk = lambda f: pl.BlockSpec((SW_TR, SW_TN), f)
    return pl.pallas_call(body, grid=(s // SW_TR, nb),
                          in_specs=[blk(lambda i, j: (i, j)), blk(lambda i, j: (i, j)), blk(lambda i, j: (i, j + nb))],
                          out_specs=pl.BlockSpec(memory_space=pl.ANY), out_shape=_sds((s, 2 * FFN), BF16),
                          scratch_shapes=[pltpu.VMEM((2, 2, SW_TR, SW_TN), BF16), pltpu.SemaphoreType.DMA((2, 2))],
                          compiler_params=_params(2, 48 << 20), name="swiglu_bwd")(dact, gu, gu)


MG_TN = 256


def _merge_fwd(y_a, y_h, proj):
    s = y_a.shape[0]
    tn = MG_TN
    ba, bh = GT_A // tn, GT_H // tn

    def body(ya_ref, yh_ref, ga_ref, gh_ref, m_ref):
        m_ref[...] = (_sig(ga_ref[...]) * ya_ref[...] + _sig(gh_ref[...]) * yh_ref[...]).astype(m_ref.dtype)

    tr = min(s, TALL)
    blk = lambda f: pl.BlockSpec((tr, tn), f)
    return pl.pallas_call(body, grid=(s // tr, D // tn),
                          in_specs=[blk(lambda i, j: (i, j)), blk(lambda i, j: (i, j)), blk(lambda i, j: (i, j + ba)), blk(lambda i, j: (i, j + bh))],
                          out_specs=blk(lambda i, j: (i, j)), out_shape=_sds((s, D), BF16),
                          compiler_params=_params(2), name="merge_fwd")(y_a, y_h, proj, proj)


def _merge_bwd(dm, y_a, y_h, proj):
    s = y_a.shape[0]
    tn = MG_TN
    ba, bh = GT_A // tn, GT_H // tn

    def body(dm_ref, ya_ref, yh_ref, ga_ref, gh_ref, dya_ref, dyh_ref, dga_ref, dgh_ref):
        dmv = dm_ref[...]
        sa, sh = _sig(ga_ref[...]), _sig(gh_ref[...])
        dya_ref[...] = (dmv * sa).astype(BF16)
        dyh_ref[...] = (dmv * sh).astype(BF16)
        dga_ref[...] = (dmv * ya_ref[...] * (sa * (1.0 - sa))).astype(BF16)
        dgh_ref[...] = (dmv * yh_ref[...] * (sh * (1.0 - sh))).astype(BF16)

    tr = min(s, TALL)
    blk = lambda f: pl.BlockSpec((tr, tn), f)
    nat = blk(lambda i, j: (i, j))
    return pl.pallas_call(body, grid=(s // tr, D // tn),
                          in_specs=[nat, nat, nat, blk(lambda i, j: (i, j + ba)), blk(lambda i, j: (i, j + bh))],
                          out_specs=[nat] * 4, out_shape=[_sds((s, D), BF16)] * 4,
                          compiler_params=_params(2), name="merge_bwd")(dm, y_a, y_h, proj, proj)


def _hgout_fwd(o_raw, proj, hg_norm):
    s = o_raw.shape[0]
    bg = G_H // LANE

    def body(o_ref, g_ref, n_ref, out_ref):
        ov = o_ref[...]
        r = lax.rsqrt(jnp.mean(ov * ov, axis=-1, keepdims=True) + EPS)
        out_ref[...] = (ov * r * n_ref[...] * _sig(g_ref[...])).astype(out_ref.dtype)

    tr = min(s, TALL)
    blk = lambda f: pl.BlockSpec((tr, LANE), f)
    return pl.pallas_call(body, grid=(s // tr, HG_HEADS),
                          in_specs=[blk(lambda i, h: (i, h)), blk(lambda i, h: (i, h + bg)), pl.BlockSpec((1, LANE), lambda i, h: (0, 0))],
                          out_specs=blk(lambda i, h: (i, h)), out_shape=_sds((s, HG_W), BF16),
                          compiler_params=_params(2), name="hgout_fwd")(o_raw, proj, hg_norm)


def _hgout_bwd(d_out, o_raw, proj, hg_norm):
    s = o_raw.shape[0]
    bg = G_H // LANE

    def body(d_ref, o_ref, g_ref, n_ref, do_ref, dg_ref, dn_ref):
        first = jnp.logical_and(pl.program_id(0) == 0, pl.program_id(1) == 0)
        ov, dv, nv = o_ref[...], d_ref[...], n_ref[...]
        sg = _sig(g_ref[...])
        r = lax.rsqrt(jnp.mean(ov * ov, axis=-1, keepdims=True) + EPS)
        oh = ov * r
        d_on = dv * sg
        dg_ref[...] = (dv * (oh * nv) * (sg * (1.0 - sg))).astype(dg_ref.dtype)
        t = d_on * nv
        do_ref[...] = r * (t - oh * jnp.mean(t * oh, axis=-1, keepdims=True))
        _acc_rows(dn_ref, first, jnp.sum(d_on * oh, axis=0, keepdims=True))

    tr = min(s, TALL)
    blk = lambda f: pl.BlockSpec((tr, LANE), f)
    vec = pl.BlockSpec((1, LANE), lambda i, h: (0, 0))
    return pl.pallas_call(body, grid=(s // tr, HG_HEADS),
                          in_specs=[blk(lambda i, h: (i, h)), blk(lambda i, h: (i, h)), blk(lambda i, h: (i, h + bg)), vec],
                          out_specs=[blk(lambda i, h: (i, h)), blk(lambda i, h: (i, h)), vec],
                          out_shape=[_sds((s, HG_W), F32), _sds((s, HG_W), BF16), _sds((1, LANE), F32)],
                          compiler_params=_params(2), name="hgout_bwd")(d_out, o_raw, proj, hg_norm)


def _rope(t, cos, s_lo, s_hi):
    return t * cos + pltpu.roll(t, LANE - ROT // 2, 1) * s_lo + pltpu.roll(t, ROT // 2, 1) * s_hi


def _rope_wide(t, cos, s_lo, s_hi):
    return jnp.concatenate([_rope(t[:, k * LANE:(k + 1) * LANE], cos, s_lo, s_hi) for k in range(t.shape[1] // LANE)], axis=1)


def _attn_mask(has_prev):
    kj = lax.broadcasted_iota(jnp.int32, (2 * BLK, BLK), 0)
    qi = lax.broadcasted_iota(jnp.int32, (2 * BLK, BLK), 1)
    rel = BLK + qi - kj
    band = jnp.logical_and(rel >= 0, rel < BLK)
    return jnp.logical_and(band, jnp.logical_or(has_prev, kj >= BLK))


def _attn_specs():
    prev = lambda i: jnp.maximum(i - 1, 0)
    kb, vb = K_A // LANE, V_A // LANE
    blk = lambda f: pl.BlockSpec((BLK, LANE), f)
    tabs = [blk(lambda i: (i, 0))] * 3 + [blk(lambda i: (prev(i), 0))] * 3
    return [pl.BlockSpec((BLK, ATT_W), lambda i: (i, 0)), blk(lambda i: (i, kb)), blk(lambda i: (prev(i), kb)),
            blk(lambda i: (i, vb)), blk(lambda i: (prev(i), vb))] + tabs + [pl.BlockSpec((1, LANE), lambda i: (0, 0))]


def _attn_logits(qh, kg):
    return _dot(kg, qh, NT)


def _attn_probs(raw, mask, sk):
    logits = jnp.where(mask, raw * (HEAD_DIM ** -0.5), -jnp.inf)
    m = jnp.maximum(jnp.max(logits, axis=0, keepdims=True), sk)
    p = jnp.exp(logits - m)
    e_sink = jnp.exp(sk - m)
    inv = 1.0 / (jnp.sum(p, axis=0, keepdims=True) + e_sink)
    return p, inv, e_sink * inv


def _attn_fwd(proj, tabs, sinks):
    s = proj.shape[0]

    def body(q_ref, kc_ref, kp_ref, vc_ref, vp_ref, c0, l0, h0, c1, l1, h1, sk_ref, o_ref):
        i = pl.program_id(0)
        mask = _attn_mask(i > 0)
        q = _rope_wide(q_ref[...], c0[...], l0[...], h0[...]).astype(BF16)
        kk = jnp.concatenate([_rope(kp_ref[...], c1[...], l1[...], h1[...]), _rope(kc_ref[...], c0[...], l0[...], h0[...])], axis=0).astype(BF16)
        v_t = jnp.concatenate([vp_ref[...], vc_ref[...]], axis=0).T.astype(BF16)
        part = lambda t, h: t[:, h * HEAD_DIM:(h + 1) * HEAD_DIM]
        k_heads = [part(kk, g) for g in range(KV_HEADS)]

        def head(h):
            g = h // GROUP
            raw = _attn_logits(part(q, h), k_heads[g])
            yield
            p, inv, _ = _attn_probs(raw, mask, sk_ref[:, h:h + 1])
            yield
            out_t = _dot(v_t[g * HEAD_DIM:(g + 1) * HEAD_DIM], p.astype(BF16), NN)
            yield
            return out_t * inv

        o_ref[...] = jnp.concatenate(_interleave([head(h) for h in range(ATT_HEADS)]), axis=0).T.astype(o_ref.dtype)

    return pl.pallas_call(body, grid=(s // BLK,), in_specs=_attn_specs(),
                          out_specs=pl.BlockSpec((BLK, ATT_W), lambda i: (i, 0)), out_shape=_sds((s, ATT_W), BF16),
                          compiler_params=_params(1), name="attn_fwd")(proj, proj, proj, proj, proj, *tabs, *tabs, sinks)


def _attn_bwd(proj, tabs, sinks, d_att):
    s = proj.shape[0]

    def body(q_ref, kc_ref, kp_ref, vc_ref, vp_ref, c0, l0, h0, c1, l1, h1, sk_ref, do_ref, dq_ref, dk_ref, dv_ref, ds_ref):
        i = pl.program_id(0)

        @pl.when(i == 0)
        def _():
            dk_ref[...] = jnp.zeros_like(dk_ref)
            dv_ref[...] = jnp.zeros_like(dv_ref)
            ds_ref[...] = jnp.zeros_like(ds_ref)

        mask = _attn_mask(i > 0)
        q = _rope_wide(q_ref[...], c0[...], l0[...], h0[...]).astype(BF16)
        kk = jnp.concatenate([_rope(kp_ref[...], c1[...], l1[...], h1[...]), _rope(kc_ref[...], c0[...], l0[...], h0[...])], axis=0).astype(BF16)
        k_f32 = jnp.concatenate([_rope(kp_ref[...], c1[...], l1[...], h1[...]), _rope(kc_ref[...], c0[...], l0[...], h0[...])], axis=0)
        k_t = k_f32.T.astype(BF16)
        vv = jnp.concatenate([vp_ref[...], vc_ref[...]], axis=0).astype(BF16)
        d_o = do_ref[...].astype(BF16)
        lane = lax.broadcasted_iota(jnp.int32, (1, LANE), 1)
        part = lambda t, h: t[:, h * HEAD_DIM:(h + 1) * HEAD_DIM]
        k_heads = [part(kk, g) for g in range(KV_HEADS)]
        v_heads = [part(vv, g) for g in range(KV_HEADS)]

        def head(h):
            g = h // GROUP
            qh, doh = part(q, h), part(d_o, h)
            raw = _attn_logits(qh, k_heads[g])
            d_p = _dot(v_heads[g], doh, NT)
            yield
            p, inv, p_sink = _attn_probs(raw, mask, sk_ref[:, h:h + 1])
            prob = p * inv
            dv = _dot(prob.astype(BF16), doh, NN)
            yield
            dd = jnp.sum(prob * d_p, axis=0, keepdims=True)
            d_s = (prob * (d_p - dd)).astype(BF16)
            d_sink = jnp.where(lane == h, -jnp.sum(p_sink * dd, axis=1, keepdims=True), 0.0)
            dq_t = _dot(k_t[g * HEAD_DIM:(g + 1) * HEAD_DIM], d_s, NN)
            dk = _dot(d_s, qh, NN)
            yield
            return dq_t * (HEAD_DIM ** -0.5), dk * (HEAD_DIM ** -0.5), dv, d_sink

        per_head = _interleave([head(h) for h in range(ATT_HEADS)])
        dqs = [jnp.concatenate([t[0] for t in per_head], axis=0).T]
        group_sum = lambda k, g: functools.reduce(jnp.add, [t[k] for t in per_head[g * GROUP:(g + 1) * GROUP]])
        dks = [group_sum(1, g) for g in range(KV_HEADS)]
        dvs = [group_sum(2, g) for g in range(KV_HEADS)]
        d_sink = functools.reduce(jnp.add, [t[3] for t in per_head])
        dq_ref[...] = _rope_wide(jnp.concatenate(dqs, axis=1), c0[...], -l0[...], -h0[...]).astype(dq_ref.dtype)
        d_k = jnp.concatenate(dks, axis=1)
        d_v = jnp.concatenate(dvs, axis=1)
        cur = pl.ds(pl.multiple_of(i * BLK, BLK), BLK)
        prv = pl.ds(pl.multiple_of(jnp.maximum(i - 1, 0) * BLK, BLK), BLK)
        dk_ref[prv, :] += _rope(d_k[:BLK], c1[...], -l1[...], -h1[...])
        dk_ref[cur, :] += _rope(d_k[BLK:], c0[...], -l0[...], -h0[...])
        dv_ref[prv, :] += d_v[:BLK]
        dv_ref[cur, :] += d_v[BLK:]
        ds_ref[...] += d_sink

    full = pl.BlockSpec((s, LANE), lambda i: (0, 0))
    return pl.pallas_call(body, grid=(s // BLK,), in_specs=_attn_specs() + [pl.BlockSpec((BLK, ATT_W), lambda i: (i, 0))],
                          out_specs=[pl.BlockSpec((BLK, ATT_W), lambda i: (i, 0)), full, full, pl.BlockSpec((1, LANE), lambda i: (0, 0))],
                          out_shape=[_sds((s, ATT_W), BF16), _sds((s, LANE), F32), _sds((s, LANE), F32), _sds((1, LANE), F32)],
                          compiler_params=_params(1), name="attn_bwd")(proj, proj, proj, proj, proj, *tabs, *tabs, sinks, d_att)


def _tri_matmul(tri, t):
    hi = t.astype(BF16)
    r1 = t - hi.astype(F32)
    mid = r1.astype(BF16)
    lo = (r1 - mid.astype(F32)).astype(BF16)
    return _dot(tri, hi, NN) + _dot(tri, mid, NN) + _dot(tri, lo, NN)


def _lower_bound(hl):
    a, b = hl[0:1, :], hl[1:2, :]
    mx = jnp.maximum(a, b)
    ea, eb = jnp.exp(a - mx), jnp.exp(b - mx)
    return ea / (ea + eb)


def _hg_gates(q_raw, f_raw, lb, tri_lower):
    sg = _sig(f_raw)
    f = lb + (1.0 - lb) * sg
    sq = _sig(q_raw)
    b = _tri_matmul(tri_lower, jnp.log(f))
    return sg, f, 1.0 - f, sq, q_raw * sq, b


HG_PAIR_FWD = 8
HG_PAIR_BWD = 8


def _hg_specs(n_map, pair):
    blk = lambda off, p: pl.BlockSpec((HG_TB, LANE), lambda h, n: (n_map(n), off // LANE + pair * h + p))
    return [blk(off, p) for off in (Q_H, F_H, I_H) for p in range(pair)] + [pl.BlockSpec((2, pair * LANE), lambda h, n: (0, h))]


def _interleave(gens):
    out = [None] * len(gens)
    live = list(range(len(gens)))
    while live:
        for k in list(live):
            try:
                next(gens[k])
            except StopIteration as stop:
                out[k] = stop.value
                live.remove(k)
    return out


def _hg_spread():
    c = lax.broadcasted_iota(jnp.int32, (CHUNK, SUB * SUB), 0)
    l = lax.broadcasted_iota(jnp.int32, (CHUNK, SUB * SUB), 1)
    r = lax.broadcasted_iota(jnp.int32, (SUB, SUB * SUB), 0)
    lr = lax.broadcasted_iota(jnp.int32, (SUB, SUB * SUB), 1)
    shift = SUB.bit_length() - 1
    cols = [(c == lo + (l >> shift)).astype(BF16) for lo in range(0, CHUNK, SUB)]
    tile = [(c == lo + (l & (SUB - 1))).astype(BF16) for lo in range(0, CHUNK, SUB)]
    return cols, tile, (lr & (SUB - 1)) == r, (lr >> shift) == r


def _hg_intra(qs, kk, b, grad=None):
    lane = lax.broadcasted_iota(jnp.int32, (SUB, CHUNK), 1)
    row1 = lax.broadcasted_iota(jnp.int32, (SUB, 1), 0)
    kk_b = kk.astype(BF16)
    if grad is not None:
        d_a, d_at, (cols, tile, diag, block) = grad
    a_blocks, dq_blocks, dk_blocks, db_blocks = [], [], [], []
    dk_left = None
    for j in range(CHUNK // SUB):
        lo = j * SUB
        q_j, k_j, b_j = qs[lo:lo + SUB], kk[lo:lo + SUB], b[lo:lo + SUB]
        es = [jnp.where(row1 >= sx, jnp.exp(jnp.minimum(b_j - b_j[sx:sx + 1], 0.0)), 0.0) for sx in range(SUB)]
        pes = [q_j * e for e in es]
        pe = jnp.concatenate(pes, axis=0).astype(BF16)
        pairs = _dot(pe, kk_b, NT)
        yield
        a_j = jnp.zeros((SUB, CHUNK), F32)
        for sx in range(SUB):
            a_j = jnp.where(lane == lo + sx, pairs[sx * SUB:(sx + 1) * SUB], a_j)
        if grad is not None:
            da_j = d_a[lo:lo + SUB]
            ek = jnp.concatenate([e * k_j[sx:sx + 1] for sx, e in enumerate(es)], axis=0).astype(BF16)
            sel_t = jnp.where(diag, _dot(da_j.astype(BF16), cols[j], NN), 0.0).astype(BF16)
            sel_s = jnp.where(block, _dot(d_at[lo:lo + SUB].astype(BF16), tile[j], NN), 0.0).astype(BF16)
            pek = jnp.concatenate([p * k_j[sx:sx + 1] for sx, p in enumerate(pes)], axis=0).astype(BF16)
            yield
            dq_j = _dot(sel_t, ek, NN)
            dk_j = _dot(sel_s, pe, NN)
            db_j = _dot(sel_t, pek, NN) - _dot(sel_s, pek, NN)
            yield
        if j > 0:
            ref = b[lo - 1:lo]
            sc_q = jnp.exp(b_j - ref)
            sc_k = jnp.exp(jnp.minimum(ref - b, 0.0))
            qt = (q_j * sc_q).astype(BF16)
            kt = (kk * sc_k).astype(BF16)
            left = _dot(qt, kt, NT)
            yield
            a_j = a_j + jnp.where(lane < lo, left, 0.0)
            if grad is not None:
                da_left = jnp.where(lane < lo, da_j, 0.0).astype(BF16)
                dq_left = _dot(da_left, kt, NN) * sc_q
                dq_j = dq_j + dq_left
                db_j = db_j + q_j * dq_left
                t = _dot(da_left, qt, TN)
                yield
                t = t * sc_k
                dk_left = t if dk_left is None else dk_left + t
        a_blocks.append(a_j)
        if grad is not None:
            dq_blocks.append(dq_j)
            dk_blocks.append(dk_j)
            db_blocks.append(db_j)
    a = jnp.concatenate(a_blocks, axis=0)
    if grad is None:
        return a
    return a, jnp.concatenate(dq_blocks, axis=0), jnp.concatenate(dk_blocks, axis=0) + dk_left, jnp.concatenate(db_blocks, axis=0) - kk * dk_left


def _hgrn_fwd(proj, hl):
    s = proj.shape[0]
    n_chunk = HG_TB // CHUNK
    pair = HG_PAIR_FWD

    def body(*refs):
        q_refs, f_refs, i_refs = refs[:pair], refs[pair:2 * pair], refs[2 * pair:3 * pair]
        hl_ref, o_ref, st_out_ref, st_ref = refs[3 * pair:]

        @pl.when(pl.program_id(1) == 0)
        def _():
            st_ref[...] = jnp.zeros_like(st_ref)

        r_i = lax.broadcasted_iota(jnp.int32, (CHUNK, CHUNK), 0)
        c_i = lax.broadcasted_iota(jnp.int32, (CHUNK, CHUNK), 1)
        tri_lower = (r_i >= c_i).astype(BF16)

        def chunk(c, carry):
            rows = pl.ds(pl.multiple_of(c * CHUNK, CHUNK), CHUNK)
            def head(p):
                cols = slice(p * LANE, (p + 1) * LANE)
                lb = _lower_bound(hl_ref[:, cols])
                v = i_refs[p][rows, :].astype(BF16)
                _, _, kk, _, qs, b = _hg_gates(q_refs[p][rows, :], f_refs[p][rows, :], lb, tri_lower)
                yield
                st = st_ref[p]
                st_b = st.astype(BF16)
                st_out_ref[p, c] = st_b
                o_state = _dot((qs * jnp.exp(b)).astype(BF16), st_b, NT)
                b_last = b[CHUNK - 1:CHUNK, :]
                st_new = _dot(v, (kk * jnp.exp(b_last - b)).astype(BF16), TN)
                a = yield from _hg_intra(qs, kk, b)
                st_ref[p] = st * jnp.exp(b_last) + st_new
                o_ref[rows, cols] = o_state + _dot(a.astype(BF16), v, NN)

            _interleave([head(p) for p in range(pair)])
            return carry

        lax.fori_loop(0, n_chunk, chunk, 0)

    return pl.pallas_call(
        body, grid=(HG_HEADS // pair, s // HG_TB), in_specs=_hg_specs(lambda n: n, pair),
        out_specs=[pl.BlockSpec((HG_TB, pair * LANE), lambda h, n: (n, h)), pl.BlockSpec((pair, n_chunk, HG_K, HG_K), lambda h, n: (h, n, 0, 0))],
        out_shape=[_sds((s, HG_W), F32), _sds((HG_HEADS, s // CHUNK, HG_K, HG_K), BF16)],
        scratch_shapes=[pltpu.VMEM((pair, HG_K, HG_K), F32)],
        compiler_params=_params(2), name="hgrn_fwd")(*[proj] * (3 * pair), hl)


def _hgrn_bwd(proj, hl, states, d_o):
    s = proj.shape[0]
    n_chunk = HG_TB // CHUNK
    n_blk = s // HG_TB
    pair = HG_PAIR_BWD
    rev = lambda n: n_blk - 1 - n

    def body(*refs):
        q_refs, f_refs, i_refs = refs[:pair], refs[pair:2 * pair], refs[2 * pair:3 * pair]
        hl_ref, st_in_ref, do_ref, dq_ref, df_ref, di_ref, dhl_ref, dst_ref, dlb_ref = refs[3 * pair:]
        n = pl.program_id(1)

        @pl.when(n == 0)
        def _():
            dst_ref[...] = jnp.zeros_like(dst_ref)
            dlb_ref[...] = jnp.zeros_like(dlb_ref)

        r_i = lax.broadcasted_iota(jnp.int32, (CHUNK, CHUNK), 0)
        c_i = lax.broadcasted_iota(jnp.int32, (CHUNK, CHUNK), 1)
        tri_lower = (r_i >= c_i).astype(BF16)
        tri_upper = (r_i <= c_i).astype(BF16)
        row = lax.broadcasted_iota(jnp.int32, (CHUNK, 1), 0)
        spread = _hg_spread()

        def chunk(cc, carry):
            c = n_chunk - 1 - cc
            rows = pl.ds(pl.multiple_of(c * CHUNK, CHUNK), CHUNK)
            def head(p):
                cols = slice(p * LANE, (p + 1) * LANE)
                lb = _lower_bound(hl_ref[:, cols])
                q_raw = q_refs[p][rows, :]
                vb = i_refs[p][rows, :].astype(BF16)
                sg, f, kk, sq, qs, b = _hg_gates(q_raw, f_refs[p][rows, :], lb, tri_lower)
                yield
                e_b = jnp.exp(b)
                qe = qs * e_b
                b_last = b[CHUNK - 1:CHUNK, :]
                e_last = jnp.exp(b_last)
                e_kd = jnp.exp(b_last - b)
                kd = kk * e_kd
                st0 = st_in_ref[p, c]
                d_ob = do_ref[rows, cols].astype(BF16)
                dst = dst_ref[p]
                dst_b = dst.astype(BF16)
                d_a = jnp.where(r_i >= c_i, _dot(d_ob, vb, NT), 0.0)
                d_at = jnp.where(r_i <= c_i, _dot(vb, d_ob, NT), 0.0)
                d_v_st = _dot(kd.astype(BF16), dst_b, NT)
                d_kd = _dot(vb, dst_b, NN)
                d_qe = _dot(d_ob, st0, NN)
                dst_new = _dot(d_ob, qe.astype(BF16), TN)
                yield
                a, dqs, dkk, d_b = yield from _hg_intra(qs, kk, b, (d_a, d_at, spread))
                d_v = _dot(a.astype(BF16), d_ob, TN) + d_v_st
                dqs_st = d_qe * e_b
                dkk_st = d_kd * e_kd
                dqs = dqs + dqs_st
                dkk = dkk + dkk_st
                d_b_last = jnp.sum(d_kd * kd, axis=0, keepdims=True) + jnp.sum(dst * st0.astype(F32), axis=0, keepdims=True) * e_last
                d_b = d_b + qs * dqs_st - kk * dkk_st + jnp.where(row == CHUNK - 1, d_b_last, 0.0)
                d_g = _tri_matmul(tri_upper, d_b)
                dst_ref[p] = dst_new + dst * e_last
                yield
                d_f = d_g / f - dkk
                dlb_ref[:, cols] += jnp.sum(d_f * (1.0 - sg), axis=0, keepdims=True)
                dq_ref[rows, cols] = (dqs * (sq * (1.0 + q_raw * (1.0 - sq)))).astype(dq_ref.dtype)
                df_ref[rows, cols] = (d_f * (1.0 - lb) * (sg * (1.0 - sg))).astype(df_ref.dtype)
                di_ref[rows, cols] = d_v.astype(di_ref.dtype)

            _interleave([head(p) for p in range(pair)])
            return carry

        lax.fori_loop(0, n_chunk, chunk, 0)

        @pl.when(n == n_blk - 1)
        def _():
            lb = _lower_bound(hl_ref[...])
            d_hl0 = dlb_ref[...] * (lb * (1.0 - lb))
            dhl_ref[...] = jnp.concatenate([d_hl0, -d_hl0], axis=0)

    out_blk = pl.BlockSpec((HG_TB, pair * LANE), lambda h, n: (rev(n), h))
    return pl.pallas_call(
        body, grid=(HG_HEADS // pair, n_blk),
        in_specs=_hg_specs(rev, pair) + [pl.BlockSpec((pair, n_chunk, HG_K, HG_K), lambda h, n: (h, rev(n), 0, 0)), out_blk],
        out_specs=[out_blk, out_blk, out_blk, pl.BlockSpec((2, pair * LANE), lambda h, n: (0, h))],
        out_shape=[_sds((s, HG_W), BF16)] * 3 + [_sds((2, HG_W), F32)],
        scratch_shapes=[pltpu.VMEM((pair, HG_K, HG_K), F32), pltpu.VMEM((1, pair * LANE), F32)],
        compiler_params=_params(2), name="hgrn_bwd")(*[proj] * (3 * pair), hl, states, d_o)


def _mod_part(c_all, w_shard, b_shard):
    n = w_shard.shape[1]
    tn = 512

    def body(c_ref, w_ref, b_ref, o_ref):
        o_ref[...] = _dot(c_ref[...].astype(BF16), w_ref[...].astype(BF16), NN) + b_ref[...]

    return pl.pallas_call(body, grid=(n // tn,),
                          in_specs=[pl.BlockSpec((N_DEV, D), lambda j: (0, 0)), pl.BlockSpec((D, tn), lambda j: (0, j)), pl.BlockSpec((1, tn), lambda j: (0, j))],
                          out_specs=pl.BlockSpec((N_DEV, tn), lambda j: (0, j)), out_shape=_sds((N_DEV, n), F32),
                          compiler_params=_params(1, 32 << 20), name="mod_part")(c_all, w_shard, b_shard)


def _adam_math(g, w, m, v):
    c1 = 1.0 / (1.0 - ADAM_B1 ** ADAM_STEP)
    c2 = 1.0 / (1.0 - ADAM_B2 ** ADAM_STEP)
    m2 = ADAM_B1 * m + (1.0 - ADAM_B1) * g
    v2 = ADAM_B2 * v + (1.0 - ADAM_B2) * (g * g)
    return -ADAM_LR * ((m2 * c1) / (jnp.sqrt(v2 * c2) + ADAM_EPS) + ADAM_WD * w), m2, v2


def _update_w_ada(c_all_t, dmod_cols, w, m, v, deps=()):
    n = dmod_cols.shape[1]
    tn = 256

    def body(c_ref, d_ref, w_ref, m_ref, v_ref, *rest):
        g_ref, dl_ref, m2_ref, v2_ref = rest[len(deps):]
        cv = c_ref[...].astype(BF16).astype(F32)
        dv = d_ref[...].astype(BF16).astype(F32)
        g = cv[:, 0:1] * dv[0:1, :]
        for k in range(1, N_DEV):
            g = g + cv[:, k:k + 1] * dv[k:k + 1, :]
        g_ref[...] = g
        dl_ref[...], m2_ref[...], v2_ref[...] = _adam_math(g, w_ref[...], m_ref[...], v_ref[...])

    blk = pl.BlockSpec((D, tn), lambda j: (0, j))
    return pl.pallas_call(body, grid=(n // tn,),
                          in_specs=[pl.BlockSpec((D, N_DEV), lambda j: (0, 0)), pl.BlockSpec((N_DEV, tn), lambda j: (0, j)), blk, blk, blk]
                          + [pl.BlockSpec(memory_space=pl.ANY)] * len(deps),
                          out_specs=[blk] * 4, out_shape=[_sds((D, n), F32)] * 4,
                          compiler_params=_params(1, 48 << 20), name="adamw_w_ada")(c_all_t, dmod_cols, w, m, v, *deps)


def _row_tile(r, c, max_elems=1 << 18):
    if r * c <= max_elems or r % 8:
        return r
    best = 8
    for t in range(8, r + 1, 8):
        if r % t == 0 and t * c <= max_elems:
            best = t
    return best


WIDE_TILE = 5 << 17
PAIR_TILE = 3 << 19


def _adamw(pieces, w, m, v, name, own=None, max_elems=1 << 18):
    parts = list(pieces) if isinstance(pieces, (list, tuple)) else [pieces]
    owns = [] if own is None else (list(own) if isinstance(own, (list, tuple)) else [own])
    n_o, n_p = len(owns), len(parts)
    p, r = parts[0].shape[:2]
    c = sum(t.shape[2] for t in parts)
    tr = _row_tile(r, c, max_elems)

    def body(*refs):
        w_ref, m_ref, v_ref, *outs = refs[n_o + n_p:]
        cols = []
        for j, p_ref in enumerate(refs[n_o:n_o + n_p]):
            g = p_ref[0].astype(F32)
            if owns:
                g = refs[j][...].astype(F32) + g
            for k in range(1, p):
                g = g + p_ref[k].astype(F32)
            cols.append(g)
        g = cols[0] if n_p == 1 else jnp.concatenate(cols, axis=1)
        outs[0][...] = g
        outs[1][...], outs[2][...], outs[3][...] = _adam_math(g, w_ref[...], m_ref[...], v_ref[...])

    blk = pl.BlockSpec((tr, c), lambda i: (i, 0))
    in_specs = ([pl.BlockSpec((tr, t.shape[1]), lambda i: (i, 0)) for t in owns]
                + [pl.BlockSpec((p, tr, t.shape[2]), lambda i: (0, i, 0)) for t in parts] + [blk, blk, blk])
    return pl.pallas_call(body, grid=(r // tr,), in_specs=in_specs,
                          out_specs=[blk] * 4, out_shape=[_sds((r, c), F32)] * 4,
                          compiler_params=_params(1, 48 << 20), name=name)(*owns, *parts, w, m, v)


def _my_coords():
    return lax.axis_index("x"), lax.axis_index("y"), lax.axis_index("c")


def _flip(coords, k):
    x, y, c = coords
    return (1 - x if k & 4 else x, 1 - y if k & 2 else y, 1 - c if k & 1 else c)


def _lin(coords):
    return 4 * coords[0] + 2 * coords[1] + coords[2]


def _exchange_small(x3, bcast, name):
    n = x3.shape[2]

    def body(x_ref, o_ref, send_sems, recv_sems):
        me = _my_coords()
        my_id = _lin(me)
        o_ref[pl.ds(my_id, 1)] = x_ref[pl.ds(0 if bcast else my_id, 1)]
        copies = []
        for k in range(1, N_DEV):
            peer = _flip(me, k)
            src = x_ref.at[0 if bcast else _lin(peer)]
            cp = pltpu.make_async_remote_copy(src_ref=src, dst_ref=o_ref.at[my_id], send_sem=send_sems.at[k], recv_sem=recv_sems.at[k],
                                              device_id=peer, device_id_type=MESH)
            cp.start()
            copies.append(cp)
        for k in range(1, N_DEV):
            peer = _flip(me, k)
            pltpu.make_async_remote_copy(src_ref=x_ref.at[0], dst_ref=o_ref.at[_lin(peer)], send_sem=send_sems.at[k], recv_sem=recv_sems.at[k],
                                         device_id=peer, device_id_type=MESH).wait_recv()
        for cp in copies:
            cp.wait_send()

    vm = pl.BlockSpec(memory_space=pltpu.VMEM)
    return pl.pallas_call(body, in_specs=[vm], out_specs=vm, out_shape=_sds((N_DEV, 1, n), F32),
                          scratch_shapes=[pltpu.SemaphoreType.DMA((N_DEV,)), pltpu.SemaphoreType.DMA((N_DEV,))], name=name)(x3)


HBM_SPEC = pl.BlockSpec(memory_space=pltpu.HBM)
SEM_SPEC = pl.BlockSpec(memory_space=pltpu.SEMAPHORE)
ANY_SPEC = pl.BlockSpec(memory_space=pl.ANY)
DATAFLOW = pltpu.SideEffectType.DATAFLOW_SIDE_EFFECTING
GATHER_FLIPS = (1, 2, 4, 6)
PASS_FLIPS = (2, 4, 6)
TOKEN = (8, LANE)


def _hbm(t):
    return pltpu.with_memory_space_constraint(t, pltpu.HBM)


def _hbm_like(ts):
    return [pltpu.HBM(t.shape, t.dtype) for t in ts]


def _split_start(issue, srcs, lands, n_sem, name, deps=()):
    n, nb, nd = len(srcs), len(srcs) + len(lands), len(deps)

    def body(*refs):
        issue(refs[:n], refs[n:nb], refs[nb + nd], refs[nb + nd + 1])
        refs[-1][...] = jnp.zeros(TOKEN, F32)

    outs = pl.pallas_call(
        body, name=name,
        out_shape=(pltpu.SemaphoreType.DMA((n_sem,)), pltpu.SemaphoreType.DMA((n_sem,)), *_hbm_like(srcs), *_hbm_like(lands), _sds(TOKEN, F32)),
        in_specs=[HBM_SPEC] * nb + [ANY_SPEC] * nd,
        out_specs=(SEM_SPEC, SEM_SPEC, *[HBM_SPEC] * nb, pl.BlockSpec(memory_space=pltpu.VMEM)),
        input_output_aliases={i: 2 + i for i in range(nb)},
        compiler_params=pltpu.CompilerParams(has_side_effects=DATAFLOW))(*[_hbm(t) for t in srcs], *[_hbm(t) for t in lands], *deps)
    return dict(sems=outs[:2], thru=list(outs[2:2 + nb]), token=outs[-1], n=n)


def _split_wait(finish, handle, after, name):
    n = handle["n"]
    thru = handle["thru"]
    nb = len(thru)

    def body(*refs):
        finish(refs[:n], refs[n:nb], refs[nb], refs[nb + 1])

    outs = pl.pallas_call(
        body, name=name, out_shape=_hbm_like(thru), in_specs=[HBM_SPEC] * nb + [SEM_SPEC, SEM_SPEC] + [ANY_SPEC] * len(after),
        out_specs=[HBM_SPEC] * nb, input_output_aliases={i: i for i in range(nb)},
        compiler_params=pltpu.CompilerParams(has_side_effects=DATAFLOW))(*thru, *handle["sems"], *after)
    return list(outs[:n]), list(outs[n:])


def _gather_start(shards, name, deps=()):
    n = len(shards)
    my_id = _lin(_my_coords())
    lands = [lax.dynamic_update_slice(lax.empty((N_DEV,) + t.shape, t.dtype), t[None], (my_id, 0, 0)) for t in shards]

    def issue(src, land, send_sems, recv_sems):
        me = _my_coords()
        for w in range(n):
            for j, k in enumerate(GATHER_FLIPS):
                q = len(GATHER_FLIPS) * w + j
                pltpu.make_async_remote_copy(src_ref=src[w], dst_ref=land[w].at[_lin(me)], send_sem=send_sems.at[q], recv_sem=recv_sems.at[q],
                                             device_id=_flip(me, k), device_id_type=MESH).start()

    return _split_start(issue, shards, lands, len(GATHER_FLIPS) * n, name, deps)


def _gather_wait(handle, after, name):
    n = handle["n"]

    def finish(src, land, send_sems, recv_sems):
        me = _my_coords()
        for w in range(n):
            for j, k in enumerate(GATHER_FLIPS):
                q = len(GATHER_FLIPS) * w + j
                peer = _flip(me, k)
                cp = pltpu.make_async_remote_copy(src_ref=src[w], dst_ref=land[w].at[_lin(peer)], send_sem=send_sems.at[q], recv_sem=recv_sems.at[q],
                                                  device_id=peer, device_id_type=MESH)
                cp.wait_send()
                cp.wait_recv()

    return _split_wait(finish, handle, after, name)[1]


def _pass_copy(land, send_sems, recv_sems, w, j, arriving):
    me = _my_coords()
    blk = land[w].at[_lin(_flip(me, PASS_FLIPS[j] + (1 if arriving else 0)))]
    q = len(PASS_FLIPS) * w + j
    return pltpu.make_async_remote_copy(src_ref=blk, dst_ref=blk, send_sem=send_sems.at[q], recv_sem=recv_sems.at[q],
                                        device_id=_flip(me, 1), device_id_type=MESH)


def _pass_start(lands, name, deps=()):
    def issue(_, land, send_sems, recv_sems):
        for w in range(len(lands)):
            for j in range(len(PASS_FLIPS)):
                _pass_copy(land, send_sems, recv_sems, w, j, False).start()

    return _split_start(issue, [], lands, len(PASS_FLIPS) * len(lands), name, deps)


def _pass_wait(handle, after, name):
    def finish(_, land, send_sems, recv_sems):
        for w in range(len(handle["thru"])):
            for j in range(len(PASS_FLIPS)):
                _pass_copy(land, send_sems, recv_sems, w, j, False).wait_send()
                _pass_copy(land, send_sems, recv_sems, w, j, True).wait_recv()

    return _split_wait(finish, handle, after, name)[1]


def _gather_pass(lands, name):
    n = len(lands)
    n_p = len(PASS_FLIPS)

    def body(*refs):
        land = refs[n:2 * n]
        send_sems, recv_sems = refs[2 * n:]
        me = _my_coords()
        sibling = _flip(me, 1)
        sent = []
        for w in range(n):
            for j, k in enumerate(PASS_FLIPS):
                blk = land[w].at[_lin(_flip(me, k))]
                cp = pltpu.make_async_remote_copy(src_ref=blk, dst_ref=blk, send_sem=send_sems.at[n_p * w + j], recv_sem=recv_sems.at[n_p * w + j],
                                                  device_id=sibling, device_id_type=MESH)
                cp.start()
                sent.append(cp)
        for w in range(n):
            for j, k in enumerate(PASS_FLIPS):
                blk = land[w].at[_lin(_flip(me, k + 1))]
                pltpu.make_async_remote_copy(src_ref=blk, dst_ref=blk, send_sem=send_sems.at[n_p * w + j], recv_sem=recv_sems.at[n_p * w + j],
                                             device_id=sibling, device_id_type=MESH).wait_recv()
        for cp in sent:
            cp.wait_send()

    return pl.pallas_call(body, in_specs=[ANY_SPEC] * n, out_specs=[ANY_SPEC] * n, out_shape=[_sds(t.shape, t.dtype) for t in lands],
                          input_output_aliases={i: i for i in range(n)},
                          scratch_shapes=[pltpu.SemaphoreType.DMA((n_p * n,)), pltpu.SemaphoreType.DMA((n_p * n,))], name=name)(*lands)


CHIP_FLIPS = (0, 2, 4, 6)


def _pair_copy(src, land, send_sems, recv_sems, w, j):
    me = _my_coords()
    q = len(CHIP_FLIPS) * w + j
    return pltpu.make_async_remote_copy(src_ref=src[w].at[_lin(_flip(me, CHIP_FLIPS[j] + 1))], dst_ref=land[w].at[j], send_sem=send_sems.at[q],
                                        recv_sem=recv_sems.at[q], device_id=_flip(me, 1), device_id_type=MESH)


def _pair_exchange(grads, name):
    n = len(grads)

    def body(*refs):
        src, land = refs[:n], refs[n:2 * n]
        send_sems, recv_sems = refs[2 * n:]
        sent = [_pair_copy(src, land, send_sems, recv_sems, w, j) for w in range(n) for j in range(len(CHIP_FLIPS))]
        for cp in sent:
            cp.start()
        for cp in sent:
            cp.wait_recv()
        for cp in sent:
            cp.wait_send()

    outs = pl.pallas_call(body, in_specs=[ANY_SPEC] * n, out_specs=[ANY_SPEC] * n,
                          out_shape=[_sds((len(CHIP_FLIPS),) + g.shape[1:], g.dtype) for g in grads],
                          scratch_shapes=[pltpu.SemaphoreType.DMA((len(CHIP_FLIPS) * n,))] * 2, name=name)(*grads)
    return list(outs)


def _pair_start(grads, name, deps=()):
    n = len(grads)
    lands = [lax.empty((len(CHIP_FLIPS),) + g.shape[1:], g.dtype) for g in grads]

    def issue(src, land, send_sems, recv_sems):
        for w in range(n):
            for j in range(len(CHIP_FLIPS)):
                _pair_copy(src, land, send_sems, recv_sems, w, j).start()

    return _split_start(issue, grads, lands, len(CHIP_FLIPS) * n, name, deps)


def _pair_wait(handle, after, name):
    n = handle["n"]

    def finish(src, land, send_sems, recv_sems):
        for w in range(n):
            for j in range(len(CHIP_FLIPS)):
                cp = _pair_copy(src, land, send_sems, recv_sems, w, j)
                cp.wait_send()
                cp.wait_recv()

    return _split_wait(finish, handle, after, name)


def _pair_add(grad, theirs, name):
    p, r, c = theirs.shape
    tr = _row_tile(r, c, PAIR_TILE)
    me = _my_coords()
    ids = jnp.stack([_lin(_flip(me, k)) for k in CHIP_FLIPS]).astype(jnp.int32)

    def body(ids_ref, a_ref, b_ref, o_ref):
        o_ref[...] = (a_ref[...].astype(F32) + b_ref[...].astype(F32)).astype(o_ref.dtype)

    blk = pl.BlockSpec((None, tr, c), lambda j, i, ids_ref: (j, i, 0))
    return pl.pallas_call(
        body, out_shape=_sds((p, r, c), theirs.dtype), compiler_params=_params(2), name=name,
        grid_spec=pltpu.PrefetchScalarGridSpec(
            num_scalar_prefetch=1, grid=(p, r // tr),
            in_specs=[pl.BlockSpec((None, tr, c), lambda j, i, ids_ref: (ids_ref[j], i, 0)), blk], out_specs=blk))(ids, grad, theirs)


def _chips_start(parts, name, deps=()):
    n = len(parts)
    n_c = len(CHIP_FLIPS) - 1
    lands = [lax.empty((n_c,) + t.shape[1:], t.dtype) for t in parts]

    def issue(src, land, send_sems, recv_sems):
        me = _my_coords()
        for w in range(n):
            for j in range(1, n_c + 1):
                q = n_c * w + j - 1
                pltpu.make_async_remote_copy(src_ref=src[w].at[j], dst_ref=land[w].at[j - 1], send_sem=send_sems.at[q], recv_sem=recv_sems.at[q],
                                             device_id=_flip(me, CHIP_FLIPS[j]), device_id_type=MESH).start()

    return _split_start(issue, parts, lands, n_c * n, name, deps)


def _chips_wait(handle, after, name):
    n = handle["n"]
    n_c = len(CHIP_FLIPS) - 1

    def finish(src, land, send_sems, recv_sems):
        me = _my_coords()
        for w in range(n):
            for j in range(1, n_c + 1):
                q = n_c * w + j - 1
                cp = pltpu.make_async_remote_copy(src_ref=src[w].at[j], dst_ref=land[w].at[j - 1], send_sem=send_sems.at[q], recv_sem=recv_sems.at[q],
                                                  device_id=_flip(me, CHIP_FLIPS[j]), device_id_type=MESH)
                cp.wait_send()
                cp.wait_recv()

    return _split_wait(finish, handle, after, name)


def _after(t, *tokens):
    for tok in tokens:
        t = t + tok[0:1, 0:1]
    return t


def _rope_tables(positions):
    half = ROT // 2
    inv_freq = ROPE_THETA ** (-jnp.arange(0, ROT, 2, dtype=F32) / ROT)
    ang = positions.astype(F32).reshape(-1, 1) * inv_freq
    cos, sin = jnp.cos(ang), jnp.sin(ang)
    s = ang.shape[0]
    pad = jnp.zeros((s, HEAD_DIM - ROT), F32)
    zero = jnp.zeros((s, half), F32)
    two = lambda t: jnp.concatenate([t, t], axis=1)
    return (two(jnp.concatenate([cos, cos, pad + 1.0], axis=1)), two(jnp.concatenate([-sin, zero, pad], axis=1)),
            two(jnp.concatenate([zero, sin, pad], axis=1)))


def _local_step(x, tgt, tabs, mod, sinks_pad, hl, hg_norm, g_pre_mix, g_post_mix, g_pre_ffn, g_post_ffn, weights, prefetch, scatter, scatter_on):
    s = x.shape[0]
    h1 = _pre_fwd(x, g_pre_mix, mod, 1, 0, "pre_mix_fwd")
    (w_in_a,) = weights("in_a", h1)
    proj = _mm_nt(h1, w_in_a, 256, IN_COLS // 2, D // 2, F32, "proj_mm_a", a_col=0)
    (w_in_b,) = weights("in_b", proj)
    proj = _mm_nt(h1, w_in_b, 256, IN_COLS // 2, D // 2, F32, "proj_mm_b", add=proj, a_col=1)
    att = _attn_fwd(proj, tabs, _after(sinks_pad, prefetch("mix", proj)))
    o_raw, states = _hgrn_fwd(proj, hl)
    ohg = _hgout_fwd(o_raw, proj, hg_norm)
    w_attn_dm, w_hgrn_dm, w_out = weights("mix", ohg)
    natural = lambda w_dm: w_dm.transpose(1, 0, 2).reshape(w_dm.shape[1], D)
    w_attn, w_hgrn = natural(w_attn_dm), natural(w_hgrn_dm)
    y_a = _mm_nn(att, w_attn, s, 512, ATT_W, F32, "attn_proj_mm")
    y_h = _mm_nn(ohg, w_hgrn, s, 512, HG_W, F32, "hgrn_proj_mm")
    merged = _merge_fwd(y_a, y_h, proj)
    y = _mm_nn(merged, w_out, s, 512, D, F32, "out_mm")
    x1 = _post_fwd(x, y, g_post_mix, mod, 2, "post_mix_fwd")
    h2 = _pre_fwd(x1, g_pre_ffn, _after(mod, prefetch("ffn_in", x1)), 4, 3, "pre_ffn_fwd")
    (w_ffn_in_dm,) = weights("ffn_in", h2)
    gu = _mm_nn_dm(h2, w_ffn_in_dm, s // 2, F32, "ffn_in_mm")
    act = _swiglu_fwd(gu, deps=[prefetch("ffn_out", gu)])
    (w_ffn_out,) = weights("ffn_out", act)
    y2 = _mm_nn(act, w_ffn_out, 512, 512, FFN, F32, "ffn_out_mm")
    err, loss, dy2, d_gate2, dg_post_ffn = _post_loss_bwd(x1, y2, g_post_ffn, mod, 5, tgt, "post_ffn_loss_bwd")
    gw_ffn_out = _mm_tn(act, dy2, 512, D, BF16, "ffn_out_dw")
    t_pair = scatter([gw_ffn_out.reshape(N_DEV, FFN // N_DEV, D)], "ffn_out")
    d_act = _mm_nt(dy2, w_ffn_out, s, 512, D, F32, "ffn_out_dx", deps=[t_pair])
    dgu = _swiglu_bwd(d_act, gu)
    t_out = scatter_on("ffn_out", dgu)
    gw_ffn_in = _mm_tn_dm(h2, dgu, 1024, BF16, "ffn_in_dw")
    t_pair = scatter([gw_ffn_in], "ffn_in")
    dh2 = _mm_nt_dm(dgu, w_ffn_in_dm, s, 1024, F32, "ffn_in_dx", deps=[t_pair])
    mod = _after(mod, t_out)
    dx1, d_shift2, d_scale2, dg_pre_ffn = _pre_bwd([dh2], x1, err, g_pre_ffn, mod, 4, "pre_ffn_bwd")
    dy, d_gate1, dg_post_mix = _post_bwd(dx1, y, g_post_mix, mod, 2, "post_mix_bwd")
    t_in = scatter_on("ffn_in", dy)
    d_merged = _mm_nt(dy, w_out, s, 512, D, F32, "out_dx")
    gw_out = _mm_tn(merged, dy, 512, D, BF16, "out_dw")
    dy_a, dy_h, d_gate_a, d_gate_h = _merge_bwd(d_merged, y_a, y_h, proj)
    gw_attn = _mm_tn_dm(att, dy_a, ATT_W, BF16, "attn_proj_dw")
    gw_hgrn = _mm_tn_dm(ohg, dy_h, HG_W, BF16, "hgrn_proj_dw")
    t_pair = scatter([gw_attn, gw_hgrn, gw_out.reshape(N_DEV, D // N_DEV, D)], "mix")
    d_att = _mm_nt(dy_a, w_attn, s, 512, D, F32, "attn_proj_dx")
    d_ohg = _mm_nt(dy_h, w_hgrn, s, 512, D, F32, "hgrn_proj_dx", deps=[t_pair])
    d_o, d_gh, d_hg_norm = _hgout_bwd(d_ohg, o_raw, proj, _after(hg_norm, t_in))
    d_qh, d_fh, d_ih, d_hl = _hgrn_bwd(proj, hl, states, d_o)
    t_mix = scatter_on("mix", d_qh)
    d_qa, d_ka, d_va, d_sinks = _attn_bwd(proj, tabs, _after(sinks_pad, t_mix), d_att)
    d_proj = jnp.concatenate([d_qa, d_ka.astype(BF16), d_va.astype(BF16), d_qh, d_fh, d_ih, d_gh, d_gate_a, d_gate_h], axis=1)
    dh1 = [_mm_nn(d_proj, w_half, s // 2, 512, IN_COLS // 2, F32, "proj_dx_" + tag) for tag, w_half in (("a", w_in_a), ("b", w_in_b))]
    grad_x, d_shift1, d_scale1, dg_pre_mix = _pre_bwd(dh1, x, dx1, g_pre_mix, mod, 1, "pre_mix_bwd")
    d_mod = jnp.concatenate([d_shift1, d_scale1, d_gate1, d_shift2, d_scale2, d_gate2], axis=1)
    small = [d_mod, dg_pre_mix, dg_post_mix, dg_pre_ffn, dg_post_ffn, d_hl.reshape(1, 2 * HG_W), d_hg_norm, d_sinks]
    return loss, grad_x, small, h1, d_proj


def kernel(x, c, positions, w_ada, b_ada, g_pre_mix, g_post_mix, g_pre_ffn, g_post_ffn, w_in, attn_sinks, w_attn_proj, hg_lower_bounds, hg_norm, w_hgrn_proj, w_out, w_ffn_in, w_ffn_out, loss_target, m_w_ada, m_b_ada, m_g_pre_mix, m_g_post_mix, m_g_pre_ffn, m_g_post_ffn, m_w_in, m_attn_sinks, m_w_attn_proj, m_hg_lower_bounds, m_hg_norm, m_w_hgrn_proj, m_w_out, m_w_ffn_in, m_w_ffn_out, v_w_ada, v_b_ada, v_g_pre_mix, v_g_post_mix, v_g_pre_ffn, v_g_post_ffn, v_w_in, v_attn_sinks, v_w_attn_proj, v_hg_lower_bounds, v_hg_norm, v_w_hgrn_proj, v_w_out, v_w_ffn_in, v_w_ffn_out):
    my_id = _lin(_my_coords())
    s = x.shape[1]
    n_ada = w_ada.shape[2]

    c_all = _exchange_small(c.reshape(1, 1, D), True, "gather_c").reshape(N_DEV, D)
    b_cols = lax.dynamic_slice(b_ada, (0, my_id * n_ada), (1, n_ada))
    mod_part = _mod_part(c_all, w_ada[0], b_cols)
    mod = _exchange_small(mod_part.reshape(N_DEV, 1, n_ada), False, "scatter_mod").reshape(1, N_MOD * D)
    groups = {"in_a": [w_in[0].T[:, :D // 2]], "in_b": [w_in[0].T[:, D // 2:]], "mix": [w_attn_proj[0], w_hgrn_proj[0], w_out[0]],
              "ffn_in": [w_ffn_in[0]], "ffn_out": [w_ffn_out[0]]}

    def start(group, dep):
        shards, dep = lax.optimization_barrier((groups[group], dep))
        return _gather_start([t.astype(BF16) for t in shards], "gather_start_" + group, deps=[dep])

    gathers = {"in_a": start("in_a", mod)}
    gathers["in_b"] = start("in_b", gathers["in_a"]["token"])
    gathers["mix"] = start("mix", gathers["in_b"]["token"])
    gathers["ffn_in"] = start("ffn_in", gathers["mix"]["token"])
    gathers["ffn_out"] = start("ffn_out", gathers["ffn_in"]["token"])

    passes = {}

    def prefetch(group, after):
        lands = _gather_wait(gathers[group], [after], "gather_wait_" + group)
        passes[group] = _pass_start(lands, "gather_pass_start_" + group)
        return passes[group]["token"]

    def weights(group, after):
        if group in passes:
            lands = _pass_wait(passes[group], [after], "gather_pass_wait_" + group)
        else:
            after = [after, gathers["ffn_out"]["token"]]
            lands = _gather_pass(_gather_wait(gathers[group], after, "gather_wait_" + group), "gather_pass_" + group)
        if group in ("in_a", "in_b"):
            return (lands[0].reshape(IN_COLS, D // 2),)
        if group == "mix":
            return lands[0], lands[1], lands[2].reshape(D, D)
        return (lands[0],) if group == "ffn_in" else (lands[0].reshape(FFN, D),)

    pairs, scatters = {}, {}

    def scatter(grads, group):
        pairs[group] = _pair_start(grads, "scatter_pair_" + group)
        return pairs[group]["token"]

    def scatter_on(group, after):
        if group in pairs:
            local, theirs = _pair_wait(pairs[group], [after], "scatter_pair_wait_" + group)
        else:
            local, theirs = after, _pair_exchange(after, "scatter_pair_" + group)
        parts = [_pair_add(g, t, "scatter_pair_add_%s_%d" % (group, k)) for k, (g, t) in enumerate(zip(local, theirs))]
        scatters[group] = _chips_start(parts, "scatter_start_" + group)
        return scatters[group]["token"]

    sinks_pad = jnp.pad(attn_sinks, ((0, 0), (0, LANE - ATT_HEADS)))
    loss, grad_x, small, h1, d_proj = _local_step(
        x[0], loss_target[0], _rope_tables(positions), _after(mod, gathers["ffn_out"]["token"]), sinks_pad, hg_lower_bounds, hg_norm, g_pre_mix, g_post_mix, g_pre_ffn, g_post_ffn,
        weights, prefetch, scatter, scatter_on)
    small = small + [jnp.pad(loss, ((0, 0), (0, LANE - 1)))]

    sizes = [t.shape[1] for t in small]
    parts = _exchange_small(jnp.concatenate(small, axis=1).reshape(1, 1, sum(sizes)), True, "gather_small_grads")
    dep = parts
    for b_col, half in enumerate(("in_a", "in_b")):
        gw_half = _mm_tn(d_proj, h1, 256, D // 2, BF16, "proj_dw_" + half, deps=[dep], b_col=b_col)
        dep = scatter_on(half, [gw_half.reshape(N_DEV, IN_COLS // N_DEV, D // 2)])
    offs = [sum(sizes[:k]) for k in range(len(sizes))]
    piece = lambda k, n=None: parts[:, :, offs[k]:offs[k] + (sizes[k] if n is None else n)]
    loss = jnp.sum(piece(8, 1))
    small_w = [(piece(0), b_ada, m_b_ada, v_b_ada), (piece(1), g_pre_mix, m_g_pre_mix, v_g_pre_mix),
               (piece(2), g_post_mix, m_g_post_mix, v_g_post_mix), (piece(3), g_pre_ffn, m_g_pre_ffn, v_g_pre_ffn),
               (piece(4), g_post_ffn, m_g_post_ffn, v_g_post_ffn),
               (piece(5).reshape(N_DEV, 2, HG_W), hg_lower_bounds, m_hg_lower_bounds, v_hg_lower_bounds),
               (piece(6), hg_norm, m_hg_norm, v_hg_norm), (piece(7, ATT_HEADS), attn_sinks, m_attn_sinks, v_attn_sinks)]
    names = ["b_ada", "g_pre_mix", "g_post_mix", "g_pre_ffn", "g_post_ffn", "hg_lower_bounds", "hg_norm", "attn_sinks"]
    res = {n: _adamw(p, w, m, v, "adamw_" + n) for n, (p, w, m, v) in zip(names, small_w)}

    dmod_cols = lax.dynamic_slice(parts.reshape(N_DEV, -1), (0, my_id * n_ada), (N_DEV, n_ada))
    res["w_ada"] = list(_update_w_ada(c_all.T, dmod_cols, w_ada[0], m_w_ada[0], v_w_ada[0], deps=[scatters["in_b"]["token"]]))

    big = {"ffn_out": [("w_ffn_out", w_ffn_out, m_w_ffn_out, v_w_ffn_out)], "ffn_in": [("w_ffn_in", w_ffn_in, m_w_ffn_in, v_w_ffn_in)],
           "mix": [("w_attn_proj", w_attn_proj, m_w_attn_proj, v_w_attn_proj), ("w_hgrn_proj", w_hgrn_proj, m_w_hgrn_proj, v_w_hgrn_proj),
                   ("w_out", w_out, m_w_out, v_w_out)]}
    after = [scatters["in_b"]["token"]]
    for group, members in big.items():
        local, lands = _chips_wait(scatters[group], after, "scatter_wait_" + group)
        for (n, w, m, v), mine, land in zip(members, local, lands):
            res[n] = _adamw(land, w[0], m[0], v[0], "adamw_" + n, own=mine[0])
            after = after + [res[n][1]]
    after = [res[n][1] for n in res]
    halves = [_chips_wait(scatters[half], after, "scatter_wait_" + half) for half in ("in_a", "in_b")]
    own = [local[0][0] for local, _ in halves]
    land = [lands[0] for _, lands in halves]
    res["w_in"] = [t.T for t in _adamw(land, w_in[0].T, m_w_in[0].T, v_w_in[0].T, "adamw_w_in", own=own, max_elems=WIDE_TILE)]

    order = ["w_ada", "b_ada", "g_pre_mix", "g_post_mix", "g_pre_ffn", "g_post_ffn", "w_in", "attn_sinks", "w_attn_proj",
             "hg_lower_bounds", "hg_norm", "w_hgrn_proj", "w_out", "w_ffn_in", "w_ffn_out"]
    lead = {"w_ada", "w_in", "w_attn_proj", "w_hgrn_proj", "w_out", "w_ffn_in", "w_ffn_out"}
    outs = [loss, grad_x[None]]
    for k in range(4):
        outs += [res[n][k][None] if n in lead else res[n][k] for n in order]
    return tuple(outs)
```

```python
import functools

import jax
import jax.numpy as jnp
from jax import lax
from jax.experimental import pallas as pl
from jax.experimental.pallas import tpu as pltpu

F32 = jnp.float32
BF16 = jnp.bfloat16

N_DEV = 8
D = 2048
ATT_HEADS = 16
KV_HEADS = 2
HEAD_DIM = 64
GROUP = ATT_HEADS // KV_HEADS
ATT_W = ATT_HEADS * HEAD_DIM
BLK = 128
ROT = HEAD_DIM // 4
ROPE_THETA = 500000.0
HG_HEADS = 8
HG_K = 128
HG_W = HG_HEADS * HG_K
CHUNK = 64
SUB = 16
FFN = 5632
N_MOD = 6
EPS = 1e-6
LANE = 128
Q_A, K_A, V_A, Q_H, F_H, I_H, G_H, GT_A, GT_H, IN_COLS = 0, 1024, 1152, 1280, 2304, 3328, 4352, 5376, 7424, 9472

ADAM_LR, ADAM_B1, ADAM_B2, ADAM_EPS, ADAM_WD, ADAM_STEP = 0.001, 0.9, 0.999, 1e-08, 0.01, 10

TR = 256
HG_TB = 512
VMEM_BIG = 56 << 20
MESH = pl.DeviceIdType.MESH


def _sds(shape, dtype):
    return jax.ShapeDtypeStruct(shape, dtype)


def _params(n_axes, vmem=None):
    return pltpu.CompilerParams(dimension_semantics=("arbitrary",) * n_axes, vmem_limit_bytes=vmem)


def _sig(t):
    return 1.0 / (1.0 + jnp.exp(-t))


def _dot(a, b, dims):
    return lax.dot_general(a, b, (dims, ((), ())), preferred_element_type=F32)


NN = ((1,), (0,))
NT = ((1,), (1,))
TN = ((0,), (0,))


def _matmul(a, b, a_spec, b_spec, o_spec, out_shape, grid, dims, acc_shape, name, deps=(), add=None):
    nk = grid[2]
    nd = len(deps)
    extra = [] if add is None else [add]

    def body(a_ref, b_ref, *rest):
        o_ref, scratch = rest[nd + len(extra)], rest[nd + len(extra) + 1:]
        part = _dot(a_ref[...], b_ref[...], dims)
        if add is not None:
            assert nk == 1
            part = part + rest[nd][...]
        if nk == 1:
            o_ref[...] = part.astype(o_ref.dtype)
        else:
            acc = scratch[0]
            k = pl.program_id(2)

            @pl.when(k == 0)
            def _():
                acc[...] = part

            @pl.when(k > 0)
            def _():
                acc[...] += part

            @pl.when(k == nk - 1)
            def _():
                o_ref[...] = acc[...].astype(o_ref.dtype)

    return pl.pallas_call(
        body, grid=grid, in_specs=[a_spec, b_spec] + [pl.BlockSpec(memory_space=pl.ANY)] * nd + [o_spec] * len(extra),
        out_specs=o_spec, out_shape=out_shape, scratch_shapes=[pltpu.VMEM(acc_shape, F32)] if nk > 1 else [],
        input_output_aliases={2 + nd: 0} if extra else {},
        compiler_params=_params(3, VMEM_BIG), name=name)(a, b, *deps, *extra)


def _mm_nn(a, b, tm, tn, tk, out_dtype, name):
    m, k = a.shape
    n = b.shape[1]
    return _matmul(a, b, pl.BlockSpec((tm, tk), lambda j, i, kk: (i, kk)), pl.BlockSpec((tk, tn), lambda j, i, kk: (kk, j)),
                   pl.BlockSpec((tm, tn), lambda j, i, kk: (i, j)), _sds((m, n), out_dtype),
                   (n // tn, m // tm, k // tk), NN, (tm, tn), name)


def _mm_nn_dm(a, b, tm, out_dtype, name):
    m, k = a.shape
    n = b.shape[2]
    return _matmul(a, b, pl.BlockSpec((tm, k), lambda j, i, kk: (i, 0)), pl.BlockSpec((None, k, n), lambda j, i, kk: (j, 0, 0)),
                   pl.BlockSpec((tm, n), lambda j, i, kk: (i, j)), _sds((m, N_DEV * n), out_dtype),
                   (N_DEV, m // tm, 1), NN, (tm, n), name)


def _mm_nt(a, b, tm, tn, tk, out_dtype, name, deps=(), add=None, a_col=0):
    m = a.shape[0]
    n, k = b.shape
    return _matmul(a, b, pl.BlockSpec((tm, tk), lambda j, i, kk: (i, kk + a_col * (k // tk))), pl.BlockSpec((tn, tk), lambda j, i, kk: (j, kk)),
                   pl.BlockSpec((tm, tn), lambda j, i, kk: (i, j)), _sds((m, n), out_dtype),
                   (n // tn, m // tm, k // tk), NT, (tm, tn), name, deps, add)


def _mm_nt_dm(a, b, tm, tn, out_dtype, name, deps=()):
    m = a.shape[0]
    n_out, n = b.shape[1], b.shape[2]
    return _matmul(a, b, pl.BlockSpec((tm, n), lambda j, i, kk: (i, kk)), pl.BlockSpec((None, tn, n), lambda j, i, kk: (kk, j, 0)),
                   pl.BlockSpec((tm, tn), lambda j, i, kk: (i, j)), _sds((m, n_out), out_dtype),
                   (n_out // tn, m // tm, N_DEV), NT, (tm, tn), name, deps)


def _mm_tn(a, b, tm, tn, out_dtype, name, deps=(), b_col=None):
    s, m = a.shape
    n = b.shape[1] if b_col is None else tn
    first = 0 if b_col is None else b_col
    return _matmul(a, b, pl.BlockSpec((s, tm), lambda j, i, kk: (0, i)), pl.BlockSpec((s, tn), lambda j, i, kk: (0, j + first)),
                   pl.BlockSpec((tm, tn), lambda j, i, kk: (i, j)), _sds((m, n), out_dtype),
                   (n // tn, m // tm, 1), TN, (tm, tn), name, deps)


def _mm_tn_dm(a, b, tm, out_dtype, name):
    s, m = a.shape
    n = b.shape[1] // N_DEV
    return _matmul(a, b, pl.BlockSpec((s, tm), lambda j, i, kk: (0, i)), pl.BlockSpec((s, n), lambda j, i, kk: (0, j)),
                   pl.BlockSpec((None, tm, n), lambda j, i, kk: (j, i, 0)), _sds((N_DEV, m, n), out_dtype),
                   (N_DEV, m // tm, 1), TN, (tm, n), name)


def _row_spec():
    return pl.BlockSpec((TR, D), lambda i: (i, 0))


def _vec_spec(k=0):
    return pl.BlockSpec((1, D), lambda i: (0, k))


def _acc_rows(ref, first, val):
    @pl.when(first)
    def _():
        ref[...] = val

    @pl.when(jnp.logical_not(first))
    def _():
        ref[...] += val


def _pre_fwd(x, g, mod, k_scale, k_shift, name):
    s = x.shape[0]

    def body(x_ref, g_ref, sc_ref, sh_ref, h_ref):
        xv = x_ref[...]
        r = lax.rsqrt(jnp.mean(xv * xv, axis=-1, keepdims=True) + EPS)
        n = xv * r * g_ref[...]
        h_ref[...] = (n * (1.0 + sc_ref[...]) + sh_ref[...]).astype(h_ref.dtype)

    return pl.pallas_call(body, grid=(s // TR,), in_specs=[_row_spec(), _vec_spec(), _vec_spec(k_scale), _vec_spec(k_shift)],
                          out_specs=_row_spec(), out_shape=_sds((s, D), BF16), compiler_params=_params(1), name=name)(x, g, mod, mod)


def _post_fwd(x, y, g, mod, k_gate, name):
    s = x.shape[0]

    def body(x_ref, y_ref, g_ref, gt_ref, o_ref):
        yv = y_ref[...]
        r = lax.rsqrt(jnp.mean(yv * yv, axis=-1, keepdims=True) + EPS)
        o_ref[...] = x_ref[...] + gt_ref[...] * (yv * r * g_ref[...])

    return pl.pallas_call(body, grid=(s // TR,), in_specs=[_row_spec(), _row_spec(), _vec_spec(), _vec_spec(k_gate)],
                          out_specs=_row_spec(), out_shape=_sds((s, D), F32), compiler_params=_params(1), name=name)(x, y, g, mod)


def _post_loss_bwd(x, y, g, mod, k_gate, tgt, name):
    s = x.shape[0]

    def body(x_ref, y_ref, g_ref, gt_ref, t_ref, e_ref, loss_ref, dy_ref, dgt_ref, dg_ref):
        first = pl.program_id(0) == 0
        yv, gv, gate = y_ref[...], g_ref[...], gt_ref[...]
        r = lax.rsqrt(jnp.mean(yv * yv, axis=-1, keepdims=True) + EPS)
        yh = yv * r
        err = x_ref[...] + gate * (yh * gv) - t_ref[...]
        e = err * (1.0 / D)
        e_ref[...] = e
        _acc_rows(loss_ref, first, 0.5 * jnp.sum(jnp.mean(err * err, axis=-1, keepdims=True), axis=0, keepdims=True))
        dn = e * gate
        dgn = dn * gv
        dy_ref[...] = (r * (dgn - yh * jnp.mean(dgn * yh, axis=-1, keepdims=True))).astype(dy_ref.dtype)
        _acc_rows(dgt_ref, first, jnp.sum(e * (yh * gv), axis=0, keepdims=True))
        _acc_rows(dg_ref, first, jnp.sum(dn * yh, axis=0, keepdims=True))

    return pl.pallas_call(body, grid=(s // TR,),
                          in_specs=[_row_spec(), _row_spec(), _vec_spec(), _vec_spec(k_gate), _row_spec()],
                          out_specs=[_row_spec(), pl.BlockSpec((1, 1), lambda i: (0, 0)), _row_spec(), _vec_spec(), _vec_spec()],
                          out_shape=[_sds((s, D), F32), _sds((1, 1), F32), _sds((s, D), BF16), _sds((1, D), F32), _sds((1, D), F32)],
                          compiler_params=_params(1), name=name)(x, y, g, mod, tgt)


def _pre_bwd(dh_parts, x, res, g, mod, k_scale, name):
    s = x.shape[0]
    n_p = len(dh_parts)

    def body(*refs):
        x_ref, res_ref, g_ref, sc_ref, dx_ref, dsh_ref, dsc_ref, dg_ref = refs[n_p:]
        first = pl.program_id(0) == 0
        dh_v = jnp.concatenate([r[...] for r in refs[:n_p]], axis=1)
        xv, gv = x_ref[...], g_ref[...]
        r = lax.rsqrt(jnp.mean(xv * xv, axis=-1, keepdims=True) + EPS)
        xh = xv * r
        dn = dh_v * (1.0 + sc_ref[...])
        dgn = dn * gv
        dx_ref[...] = res_ref[...] + r * (dgn - xh * jnp.mean(dgn * xh, axis=-1, keepdims=True))
        _acc_rows(dsh_ref, first, jnp.sum(dh_v, axis=0, keepdims=True))
        _acc_rows(dsc_ref, first, jnp.sum(dh_v * (xh * gv), axis=0, keepdims=True))
        _acc_rows(dg_ref, first, jnp.sum(dn * xh, axis=0, keepdims=True))

    return pl.pallas_call(body, grid=(s // TR,),
                          in_specs=[pl.BlockSpec((TR, t.shape[1]), lambda i: (i, 0)) for t in dh_parts]
                          + [_row_spec(), _row_spec(), _vec_spec(), _vec_spec(k_scale)],
                          out_specs=[_row_spec(), _vec_spec(), _vec_spec(), _vec_spec()],
                          out_shape=[_sds((s, D), F32)] + [_sds((1, D), F32)] * 3,
                          compiler_params=_params(1), name=name)(*dh_parts, x, res, g, mod)


def _post_bwd(dx, y, g, mod, k_gate, name):
    s = y.shape[0]

    def body(dx_ref, y_ref, g_ref, gt_ref, dy_ref, dgt_ref, dg_ref):
        first = pl.program_id(0) == 0
        yv, dxv, gv = y_ref[...], dx_ref[...], g_ref[...]
        r = lax.rsqrt(jnp.mean(yv * yv, axis=-1, keepdims=True) + EPS)
        yh = yv * r
        dn = dxv * gt_ref[...]
        dgn = dn * gv
        dy_ref[...] = (r * (dgn - yh * jnp.mean(dgn * yh, axis=-1, keepdims=True))).astype(dy_ref.dtype)
        _acc_rows(dgt_ref, first, jnp.sum(dxv * (yh * gv), axis=0, keepdims=True))
        _acc_rows(dg_ref, first, jnp.sum(dn * yh, axis=0, keepdims=True))

    return pl.pallas_call(body, grid=(s // TR,), in_specs=[_row_spec(), _row_spec(), _vec_spec(), _vec_spec(k_gate)],
                          out_specs=[_row_spec(), _vec_spec(), _vec_spec()],
                          out_shape=[_sds((s, D), BF16), _sds((1, D), F32), _sds((1, D), F32)],
                          compiler_params=_params(1), name=name)(dx, y, g, mod)


SW_TN = 1408
SW_TR = 512
TALL = 2048


def _swiglu_fwd(gu, deps=()):
    s = gu.shape[0]
    nb = FFN // SW_TN

    def body(g_ref, u_ref, *rest):
        a_ref = rest[len(deps)]
        gv = g_ref[...]
        a_ref[...] = (gv * _sig(gv) * u_ref[...]).astype(a_ref.dtype)

    return pl.pallas_call(body, grid=(s // SW_TR, nb),
                          in_specs=[pl.BlockSpec((SW_TR, SW_TN), lambda i, j: (i, j)), pl.BlockSpec((SW_TR, SW_TN), lambda i, j: (i, j + nb))]
                          + [pl.BlockSpec(memory_space=pl.ANY)] * len(deps),
                          out_specs=pl.BlockSpec((SW_TR, SW_TN), lambda i, j: (i, j)), out_shape=_sds((s, FFN), BF16),
                          compiler_params=_params(2, 48 << 20), name="swiglu_fwd")(gu, gu, *deps)


def _swiglu_bwd(dact, gu):
    s = gu.shape[0]
    nb = FFN // SW_TN
    n_steps = (s // SW_TR) * nb

    def body(da_ref, g_ref, u_ref, o_ref, buf, sems):
        i, j = pl.program_id(0), pl.program_id(1)
        step = i * nb + j
        slot = step % 2

        def tiles(sl):
            rows = pl.ds(pl.multiple_of(i * SW_TR, SW_TR), SW_TR)
            return [pltpu.make_async_copy(buf.at[sl, h], o_ref.at[rows, pl.ds(pl.multiple_of((j + nb * h) * SW_TN, LANE), SW_TN)], sems.at[sl, h])
                    for h in range(2)]

        @pl.when(step >= 2)
        def _():
            for cp in tiles(slot):
                cp.wait()

        gv, da = g_ref[...], da_ref[...]
        sg = _sig(gv)
        buf[slot, 0] = (da * u_ref[...] * (sg * (1.0 + gv * (1.0 - sg)))).astype(buf.dtype)
        buf[slot, 1] = (da * (gv * sg)).astype(buf.dtype)
        for cp in tiles(slot):
            cp.start()

        @pl.when(step == n_steps - 1)
        def _():
            for cp in tiles(slot) + (tiles(1 - slot) if n_steps > 1 else []):
                cp.wait()

    blk = lambda f: pl.BlockSpec((SW_TR, SW_TN), f)
    return pl.pallas_call(body, grid=(s // SW_TR, nb),
                          in_specs=[blk(lambda i, j: (i, j)), blk(lambda i, j: (i, j)), blk(lambda i, j: (i, j + nb))],
                          out_specs=pl.BlockSpec(memory_space=pl.ANY), out_shape=_sds((s, 2 * FFN), BF16),
                          scratch_shapes=[pltpu.VMEM((2, 2, SW_TR, SW_TN), BF16), pltpu.SemaphoreType.DMA((2, 2))],
                          compiler_params=_params(2, 48 << 20), name="swiglu_bwd")(dact, gu, gu)


MG_TN = 256


def _merge_fwd(y_a, y_h, proj):
    s = y_a.shape[0]
    tn = MG_TN
    ba, bh = GT_A // tn, GT_H // tn

    def body(ya_ref, yh_ref, ga_ref, gh_ref, m_ref):
        m_ref[...] = (_sig(ga_ref[...]) * ya_ref[...] + _sig(gh_ref[...]) * yh_ref[...]).astype(m_ref.dtype)

    tr = min(s, TALL)
    blk = lambda f: pl.BlockSpec((tr, tn), f)
    return pl.pallas_call(body, grid=(s // tr, D // tn),
                          in_specs=[blk(lambda i, j: (i, j)), blk(lambda i, j: (i, j)), blk(lambda i, j: (i, j + ba)), blk(lambda i, j: (i, j + bh))],
                          out_specs=blk(lambda i, j: (i, j)), out_shape=_sds((s, D), BF16),
                          compiler_params=_params(2), name="merge_fwd")(y_a, y_h, proj, proj)


def _merge_bwd(dm, y_a, y_h, proj):
    s = y_a.shape[0]
    tn = MG_TN
    ba, bh = GT_A // tn, GT_H // tn

    def body(dm_ref, ya_ref, yh_ref, ga_ref, gh_ref, dya_ref, dyh_ref, dga_ref, dgh_ref):
        dmv = dm_ref[...]
        sa, sh = _sig(ga_ref[...]), _sig(gh_ref[...])
        dya_ref[...] = (dmv * sa).astype(BF16)
        dyh_ref[...] = (dmv * sh).astype(BF16)
        dga_ref[...] = (dmv * ya_ref[...] * (sa * (1.0 - sa))).astype(BF16)
        dgh_ref[...] = (dmv * yh_ref[...] * (sh * (1.0 - sh))).astype(BF16)

    tr = min(s, TALL)
    blk = lambda f: pl.BlockSpec((tr, tn), f)
    nat = blk(lambda i, j: (i, j))
    return pl.pallas_call(body, grid=(s // tr, D // tn),
                          in_specs=[nat, nat, nat, blk(lambda i, j: (i, j + ba)), blk(lambda i, j: (i, j + bh))],
                          out_specs=[nat] * 4, out_shape=[_sds((s, D), BF16)] * 4,
                          compiler_params=_params(2), name="merge_bwd")(dm, y_a, y_h, proj, proj)


def _hgout_fwd(o_raw, proj, hg_norm):
    s = o_raw.shape[0]
    bg = G_H // LANE

    def body(o_ref, g_ref, n_ref, out_ref):
        ov = o_ref[...]
        r = lax.rsqrt(jnp.mean(ov * ov, axis=-1, keepdims=True) + EPS)
        out_ref[...] = (ov * r * n_ref[...] * _sig(g_ref[...])).astype(out_ref.dtype)

    tr = min(s, TALL)
    blk = lambda f: pl.BlockSpec((tr, LANE), f)
    return pl.pallas_call(body, grid=(s // tr, HG_HEADS),
                          in_specs=[blk(lambda i, h: (i, h)), blk(lambda i, h: (i, h + bg)), pl.BlockSpec((1, LANE), lambda i, h: (0, 0))],
                          out_specs=blk(lambda i, h: (i, h)), out_shape=_sds((s, HG_W), BF16),
                          compiler_params=_params(2), name="hgout_fwd")(o_raw, proj, hg_norm)


def _hgout_bwd(d_out, o_raw, proj, hg_norm):
    s = o_raw.shape[0]
    bg = G_H // LANE

    def body(d_ref, o_ref, g_ref, n_ref, do_ref, dg_ref, dn_ref):
        first = jnp.logical_and(pl.program_id(0) == 0, pl.program_id(1) == 0)
        ov, dv, nv = o_ref[...], d_ref[...], n_ref[...]
        sg = _sig(g_ref[...])
        r = lax.rsqrt(jnp.mean(ov * ov, axis=-1, keepdims=True) + EPS)
        oh = ov * r
        d_on = dv * sg
        dg_ref[...] = (dv * (oh * nv) * (sg * (1.0 - sg))).astype(dg_ref.dtype)
        t = d_on * nv
        do_ref[...] = r * (t - oh * jnp.mean(t * oh, axis=-1, keepdims=True))
        _acc_rows(dn_ref, first, jnp.sum(d_on * oh, axis=0, keepdims=True))

    tr = min(s, TALL)
    blk = lambda f: pl.BlockSpec((tr, LANE), f)
    vec = pl.BlockSpec((1, LANE), lambda i, h: (0, 0))
    return pl.pallas_call(body, grid=(s // tr, HG_HEADS),
                          in_specs=[blk(lambda i, h: (i, h)), blk(lambda i, h: (i, h)), blk(lambda i, h: (i, h + bg)), vec],
                          out_specs=[blk(lambda i, h: (i, h)), blk(lambda i, h: (i, h)), vec],
                          out_shape=[_sds((s, HG_W), F32), _sds((s, HG_W), BF16), _sds((1, LANE), F32)],
                          compiler_params=_params(2), name="hgout_bwd")(d_out, o_raw, proj, hg_norm)


def _rope(t, cos, s_lo, s_hi):
    return t * cos + pltpu.roll(t, LANE - ROT // 2, 1) * s_lo + pltpu.roll(t, ROT // 2, 1) * s_hi


def _rope_wide(t, cos, s_lo, s_hi):
    return jnp.concatenate([_rope(t[:, k * LANE:(k + 1) * LANE], cos, s_lo, s_hi) for k in range(t.shape[1] // LANE)], axis=1)


def _attn_mask(has_prev):
    kj = lax.broadcasted_iota(jnp.int32, (2 * BLK, BLK), 0)
    qi = lax.broadcasted_iota(jnp.int32, (2 * BLK, BLK), 1)
    rel = BLK + qi - kj
    band = jnp.logical_and(rel >= 0, rel < BLK)
    return jnp.logical_and(band, jnp.logical_or(has_prev, kj >= BLK))


def _attn_specs():
    prev = lambda i: jnp.maximum(i - 1, 0)
    kb, vb = K_A // LANE, V_A // LANE
    blk = lambda f: pl.BlockSpec((BLK, LANE), f)
    tabs = [blk(lambda i: (i, 0))] * 3 + [blk(lambda i: (prev(i), 0))] * 3
    return [pl.BlockSpec((BLK, ATT_W), lambda i: (i, 0)), blk(lambda i: (i, kb)), blk(lambda i: (prev(i), kb)),
            blk(lambda i: (i, vb)), blk(lambda i: (prev(i), vb))] + tabs + [pl.BlockSpec((1, LANE), lambda i: (0, 0))]


def _attn_logits(qh, kg):
    return _dot(kg, qh, NT)


def _attn_probs(raw, mask, sk):
    logits = jnp.where(mask, raw * (HEAD_DIM ** -0.5), -jnp.inf)
    m = jnp.maximum(jnp.max(logits, axis=0, keepdims=True), sk)
    p = jnp.exp(logits - m)
    e_sink = jnp.exp(sk - m)
    inv = 1.0 / (jnp.sum(p, axis=0, keepdims=True) + e_sink)
    return p, inv, e_sink * inv


def _attn_fwd(proj, tabs, sinks):
    s = proj.shape[0]

    def body(q_ref, kc_ref, kp_ref, vc_ref, vp_ref, c0, l0, h0, c1, l1, h1, sk_ref, o_ref):
        i = pl.program_id(0)
        mask = _attn_mask(i > 0)
        q = _rope_wide(q_ref[...], c0[...], l0[...], h0[...]).astype(BF16)
        kk = jnp.concatenate([_rope(kp_ref[...], c1[...], l1[...], h1[...]), _rope(kc_ref[...], c0[...], l0[...], h0[...])], axis=0).astype(BF16)
        v_t = jnp.concatenate([vp_ref[...], vc_ref[...]], axis=0).T.astype(BF16)
        part = lambda t, h: t[:, h * HEAD_DIM:(h + 1) * HEAD_DIM]
        k_heads = [part(kk, g) for g in range(KV_HEADS)]

        def head(h):
            g = h // GROUP
            raw = _attn_logits(part(q, h), k_heads[g])
            yield
            p, inv, _ = _attn_probs(raw, mask, sk_ref[:, h:h + 1])
            yield
            out_t = _dot(v_t[g * HEAD_DIM:(g + 1) * HEAD_DIM], p.astype(BF16), NN)
            yield
            return out_t * inv

        o_ref[...] = jnp.concatenate(_interleave([head(h) for h in range(ATT_HEADS)]), axis=0).T.astype(o_ref.dtype)

    return pl.pallas_call(body, grid=(s // BLK,), in_specs=_attn_specs(),
                          out_specs=pl.BlockSpec((BLK, ATT_W), lambda i: (i, 0)), out_shape=_sds((s, ATT_W), BF16),
                          compiler_params=_params(1), name="attn_fwd")(proj, proj, proj, proj, proj, *tabs, *tabs, sinks)


def _attn_bwd(proj, tabs, sinks, d_att):
    s = proj.shape[0]

    def body(q_ref, kc_ref, kp_ref, vc_ref, vp_ref, c0, l0, h0, c1, l1, h1, sk_ref, do_ref, dq_ref, dk_ref, dv_ref, ds_ref):
        i = pl.program_id(0)

        @pl.when(i == 0)
        def _():
            dk_ref[...] = jnp.zeros_like(dk_ref)
            dv_ref[...] = jnp.zeros_like(dv_ref)
            ds_ref[...] = jnp.zeros_like(ds_ref)

        mask = _attn_mask(i > 0)
        q = _rope_wide(q_ref[...], c0[...], l0[...], h0[...]).astype(BF16)
        kk = jnp.concatenate([_rope(kp_ref[...], c1[...], l1[...], h1[...]), _rope(kc_ref[...], c0[...], l0[...], h0[...])], axis=0).astype(BF16)
        k_f32 = jnp.concatenate([_rope(kp_ref[...], c1[...], l1[...], h1[...]), _rope(kc_ref[...], c0[...], l0[...], h0[...])], axis=0)
        k_t = k_f32.T.astype(BF16)
        vv = jnp.concatenate([vp_ref[...], vc_ref[...]], axis=0).astype(BF16)
        d_o = do_ref[...].astype(BF16)
        lane = lax.broadcasted_iota(jnp.int32, (1, LANE), 1)
        part = lambda t, h: t[:, h * HEAD_DIM:(h + 1) * HEAD_DIM]
        k_heads = [part(kk, g) for g in range(KV_HEADS)]
        v_heads = [part(vv, g) for g in range(KV_HEADS)]

        def head(h):
            g = h // GROUP
            qh, doh = part(q, h), part(d_o, h)
            raw = _attn_logits(qh, k_heads[g])
            d_p = _dot(v_heads[g], doh, NT)
            yield
            p, inv, p_sink = _attn_probs(raw, mask, sk_ref[:, h:h + 1])
            prob = p * inv
            dv = _dot(prob.astype(BF16), doh, NN)
            yield
            dd = jnp.sum(prob * d_p, axis=0, keepdims=True)
            d_s = (prob * (d_p - dd)).astype(BF16)
            d_sink = jnp.where(lane == h, -jnp.sum(p_sink * dd, axis=1, keepdims=True), 0.0)
            dq_t = _dot(k_t[g * HEAD_DIM:(g + 1) * HEAD_DIM], d_s, NN)
            dk = _dot(d_s, qh, NN)
            yield
            return dq_t * (HEAD_DIM ** -0.5), dk * (HEAD_DIM ** -0.5), dv, d_sink

        per_head = _interleave([head(h) for h in range(ATT_HEADS)])
        dqs = [jnp.concatenate([t[0] for t in per_head], axis=0).T]
        group_sum = lambda k, g: functools.reduce(jnp.add, [t[k] for t in per_head[g * GROUP:(g + 1) * GROUP]])
        dks = [group_sum(1, g) for g in range(KV_HEADS)]
        dvs = [group_sum(2, g) for g in range(KV_HEADS)]
        d_sink = functools.reduce(jnp.add, [t[3] for t in per_head])
        dq_ref[...] = _rope_wide(jnp.concatenate(dqs, axis=1), c0[...], -l0[...], -h0[...]).astype(dq_ref.dtype)
        d_k = jnp.concatenate(dks, axis=1)
        d_v = jnp.concatenate(dvs, axis=1)
        cur = pl.ds(pl.multiple_of(i * BLK, BLK), BLK)
        prv = pl.ds(pl.multiple_of(jnp.maximum(i - 1, 0) * BLK, BLK), BLK)
        dk_ref[prv, :] += _rope(d_k[:BLK], c1[...], -l1[...], -h1[...])
        dk_ref[cur, :] += _rope(d_k[BLK:], c0[...], -l0[...], -h0[...])
        dv_ref[prv, :] += d_v[:BLK]
        dv_ref[cur, :] += d_v[BLK:]
        ds_ref[...] += d_sink

    full = pl.BlockSpec((s, LANE), lambda i: (0, 0))
    return pl.pallas_call(body, grid=(s // BLK,), in_specs=_attn_specs() + [pl.BlockSpec((BLK, ATT_W), lambda i: (i, 0))],
                          out_specs=[pl.BlockSpec((BLK, ATT_W), lambda i: (i, 0)), full, full, pl.BlockSpec((1, LANE), lambda i: (0, 0))],
                          out_shape=[_sds((s, ATT_W), BF16), _sds((s, LANE), F32), _sds((s, LANE), F32), _sds((1, LANE), F32)],
                          compiler_params=_params(1), name="attn_bwd")(proj, proj, proj, proj, proj, *tabs, *tabs, sinks, d_att)


def _tri_matmul(tri, t):
    hi = t.astype(BF16)
    r1 = t - hi.astype(F32)
    mid = r1.astype(BF16)
    lo = (r1 - mid.astype(F32)).astype(BF16)
    return _dot(tri, hi, NN) + _dot(tri, mid, NN) + _dot(tri, lo, NN)


def _lower_bound(hl):
    a, b = hl[0:1, :], hl[1:2, :]
    mx = jnp.maximum(a, b)
    ea, eb = jnp.exp(a - mx), jnp.exp(b - mx)
    return ea / (ea + eb)


def _hg_gates(q_raw, f_raw, lb, tri_lower):
    sg = _sig(f_raw)
    f = lb + (1.0 - lb) * sg
    sq = _sig(q_raw)
    b = _tri_matmul(tri_lower, jnp.log(f))
    return sg, f, 1.0 - f, sq, q_raw * sq, b


HG_PAIR_FWD = 8
HG_PAIR_BWD = 8


def _hg_specs(n_map, pair):
    blk = lambda off, p: pl.BlockSpec((HG_TB, LANE), lambda h, n: (n_map(n), off // LANE + pair * h + p))
    return [blk(off, p) for off in (Q_H, F_H, I_H) for p in range(pair)] + [pl.BlockSpec((2, pair * LANE), lambda h, n: (0, h))]


def _interleave(gens):
    out = [None] * len(gens)
    live = list(range(len(gens)))
    while live:
        for k in list(live):
            try:
                next(gens[k])
            except StopIteration as stop:
                out[k] = stop.value
                live.remove(k)
    return out


def _hg_spread():
    c = lax.broadcasted_iota(jnp.int32, (CHUNK, SUB * SUB), 0)
    l = lax.broadcasted_iota(jnp.int32, (CHUNK, SUB * SUB), 1)
    r = lax.broadcasted_iota(jnp.int32, (SUB, SUB * SUB), 0)
    lr = lax.broadcasted_iota(jnp.int32, (SUB, SUB * SUB), 1)
    shift = SUB.bit_length() - 1
    cols = [(c == lo + (l >> shift)).astype(BF16) for lo in range(0, CHUNK, SUB)]
    tile = [(c == lo + (l & (SUB - 1))).astype(BF16) for lo in range(0, CHUNK, SUB)]
    return cols, tile, (lr & (SUB - 1)) == r, (lr >> shift) == r


def _hg_intra(qs, kk, b, grad=None):
    lane = lax.broadcasted_iota(jnp.int32, (SUB, CHUNK), 1)
    row1 = lax.broadcasted_iota(jnp.int32, (SUB, 1), 0)
    kk_b = kk.astype(BF16)
    if grad is not None:
        d_a, d_at, (cols, tile, diag, block) = grad
    a_blocks, dq_blocks, dk_blocks, db_blocks = [], [], [], []
    dk_left = None
    for j in range(CHUNK // SUB):
        lo = j * SUB
        q_j, k_j, b_j = qs[lo:lo + SUB], kk[lo:lo + SUB], b[lo:lo + SUB]
        es = [jnp.where(row1 >= sx, jnp.exp(jnp.minimum(b_j - b_j[sx:sx + 1], 0.0)), 0.0) for sx in range(SUB)]
        pes = [q_j * e for e in es]
        pe = jnp.concatenate(pes, axis=0).astype(BF16)
        pairs = _dot(pe, kk_b, NT)
        yield
        a_j = jnp.zeros((SUB, CHUNK), F32)
        for sx in range(SUB):
            a_j = jnp.where(lane == lo + sx, pairs[sx * SUB:(sx + 1) * SUB], a_j)
        if grad is not None:
            da_j = d_a[lo:lo + SUB]
            ek = jnp.concatenate([e * k_j[sx:sx + 1] for sx, e in enumerate(es)], axis=0).astype(BF16)
            sel_t = jnp.where(diag, _dot(da_j.astype(BF16), cols[j], NN), 0.0).astype(BF16)
            sel_s = jnp.where(block, _dot(d_at[lo:lo + SUB].astype(BF16), tile[j], NN), 0.0).astype(BF16)
            pek = jnp.concatenate([p * k_j[sx:sx + 1] for sx, p in enumerate(pes)], axis=0).astype(BF16)
            yield
            dq_j = _dot(sel_t, ek, NN)
            dk_j = _dot(sel_s, pe, NN)
            db_j = _dot(sel_t, pek, NN) - _dot(sel_s, pek, NN)
            yield
        if j > 0:
            ref = b[lo - 1:lo]
            sc_q = jnp.exp(b_j - ref)
            sc_k = jnp.exp(jnp.minimum(ref - b, 0.0))
            qt = (q_j * sc_q).astype(BF16)
            kt = (kk * sc_k).astype(BF16)
            left = _dot(qt, kt, NT)
            yield
            a_j = a_j + jnp.where(lane < lo, left, 0.0)
            if grad is not None:
                da_left = jnp.where(lane < lo, da_j, 0.0).astype(BF16)
                dq_left = _dot(da_left, kt, NN) * sc_q
                dq_j = dq_j + dq_left
                db_j = db_j + q_j * dq_left
                t = _dot(da_left, qt, TN)
                yield
                t = t * sc_k
                dk_left = t if dk_left is None else dk_left + t
        a_blocks.append(a_j)
        if grad is not None:
            dq_blocks.append(dq_j)
            dk_blocks.append(dk_j)
            db_blocks.append(db_j)
    a = jnp.concatenate(a_blocks, axis=0)
    if grad is None:
        return a
    return a, jnp.concatenate(dq_blocks, axis=0), jnp.concatenate(dk_blocks, axis=0) + dk_left, jnp.concatenate(db_blocks, axis=0) - kk * dk_left


def _hgrn_fwd(proj, hl):
    s = proj.shape[0]
    n_chunk = HG_TB // CHUNK
    pair = HG_PAIR_FWD

    def body(*refs):
        q_refs, f_refs, i_refs = refs[:pair], refs[pair:2 * pair], refs[2 * pair:3 * pair]
        hl_ref, o_ref, st_out_ref, st_ref = refs[3 * pair:]

        @pl.when(pl.program_id(1) == 0)
        def _():
            st_ref[...] = jnp.zeros_like(st_ref)

        r_i = lax.broadcasted_iota(jnp.int32, (CHUNK, CHUNK), 0)
        c_i = lax.broadcasted_iota(jnp.int32, (CHUNK, CHUNK), 1)
        tri_lower = (r_i >= c_i).astype(BF16)

        def chunk(c, carry):
            rows = pl.ds(pl.multiple_of(c * CHUNK, CHUNK), CHUNK)
            def head(p):
                cols = slice(p * LANE, (p + 1) * LANE)
                lb = _lower_bound(hl_ref[:, cols])
                v = i_refs[p][rows, :].astype(BF16)
                _, _, kk, _, qs, b = _hg_gates(q_refs[p][rows, :], f_refs[p][rows, :], lb, tri_lower)
                yield
                st = st_ref[p]
                st_b = st.astype(BF16)
                st_out_ref[p, c] = st_b
                o_state = _dot((qs * jnp.exp(b)).astype(BF16), st_b, NT)
                b_last = b[CHUNK - 1:CHUNK, :]
                st_new = _dot(v, (kk * jnp.exp(b_last - b)).astype(BF16), TN)
                a = yield from _hg_intra(qs, kk, b)
                st_ref[p] = st * jnp.exp(b_last) + st_new
                o_ref[rows, cols] = o_state + _dot(a.astype(BF16), v, NN)

            _interleave([head(p) for p in range(pair)])
            return carry

        lax.fori_loop(0, n_chunk, chunk, 0)

    return pl.pallas_call(
        body, grid=(HG_HEADS // pair, s // HG_TB), in_specs=_hg_specs(lambda n: n, pair),
        out_specs=[pl.BlockSpec((HG_TB, pair * LANE), lambda h, n: (n, h)), pl.BlockSpec((pair, n_chunk, HG_K, HG_K), lambda h, n: (h, n, 0, 0))],
        out_shape=[_sds((s, HG_W), F32), _sds((HG_HEADS, s // CHUNK, HG_K, HG_K), BF16)],
        scratch_shapes=[pltpu.VMEM((pair, HG_K, HG_K), F32)],
        compiler_params=_params(2), name="hgrn_fwd")(*[proj] * (3 * pair), hl)


def _hgrn_bwd(proj, hl, states, d_o):
    s = proj.shape[0]
    n_chunk = HG_TB // CHUNK
    n_blk = s // HG_TB
    pair = HG_PAIR_BWD
    rev = lambda n: n_blk - 1 - n

    def body(*refs):
        q_refs, f_refs, i_refs = refs[:pair], refs[pair:2 * pair], refs[2 * pair:3 * pair]
        hl_ref, st_in_ref, do_ref, dq_ref, df_ref, di_ref, dhl_ref, dst_ref, dlb_ref = refs[3 * pair:]
        n = pl.program_id(1)

        @pl.when(n == 0)
        def _():
            dst_ref[...] = jnp.zeros_like(dst_ref)
            dlb_ref[...] = jnp.zeros_like(dlb_ref)

        r_i = lax.broadcasted_iota(jnp.int32, (CHUNK, CHUNK), 0)
        c_i = lax.broadcasted_iota(jnp.int32, (CHUNK, CHUNK), 1)
        tri_lower = (r_i >= c_i).astype(BF16)
        tri_upper = (r_i <= c_i).astype(BF16)
        row = lax.broadcasted_iota(jnp.int32, (CHUNK, 1), 0)
        spread = _hg_spread()

        def chunk(cc, carry):
            c = n_chunk - 1 - cc
            rows = pl.ds(pl.multiple_of(c * CHUNK, CHUNK), CHUNK)
            def head(p):
                cols = slice(p * LANE, (p + 1) * LANE)
                lb = _lower_bound(hl_ref[:, cols])
                q_raw = q_refs[p][rows, :]
                vb = i_refs[p][rows, :].astype(BF16)
                sg, f, kk, sq, qs, b = _hg_gates(q_raw, f_refs[p][rows, :], lb, tri_lower)
                yield
                e_b = jnp.exp(b)
                qe = qs * e_b
                b_last = b[CHUNK - 1:CHUNK, :]
                e_last = jnp.exp(b_last)
                e_kd = jnp.exp(b_last - b)
                kd = kk * e_kd
                st0 = st_in_ref[p, c]
                d_ob = do_ref[rows, cols].astype(BF16)
                dst = dst_ref[p]
                dst_b = dst.astype(BF16)
                d_a = jnp.where(r_i >= c_i, _dot(d_ob, vb, NT), 0.0)
                d_at = jnp.where(r_i <= c_i, _dot(vb, d_ob, NT), 0.0)
                d_v_st = _dot(kd.astype(BF16), dst_b, NT)
                d_kd = _dot(vb, dst_b, NN)
                d_qe = _dot(d_ob, st0, NN)
                dst_new = _dot(d_ob, qe.astype(BF16), TN)
                yield
                a, dqs, dkk, d_b = yield from _hg_intra(qs, kk, b, (d_a, d_at, spread))
                d_v = _dot(a.astype(BF16), d_ob, TN) + d_v_st
                dqs_st = d_qe * e_b
                dkk_st = d_kd * e_kd
                dqs = dqs + dqs_st
                dkk = dkk + dkk_st
                d_b_last = jnp.sum(d_kd * kd, axis=0, keepdims=True) + jnp.sum(dst * st0.astype(F32), axis=0, keepdims=True) * e_last
                d_b = d_b + qs * dqs_st - kk * dkk_st + jnp.where(row == CHUNK - 1, d_b_last, 0.0)
                d_g = _tri_matmul(tri_upper, d_b)
                dst_ref[p] = dst_new + dst * e_last
                yield
                d_f = d_g / f - dkk
                dlb_ref[:, cols] += jnp.sum(d_f * (1.0 - sg), axis=0, keepdims=True)
                dq_ref[rows, cols] = (dqs * (sq * (1.0 + q_raw * (1.0 - sq)))).astype(dq_ref.dtype)
                df_ref[rows, cols] = (d_f * (1.0 - lb) * (sg * (1.0 - sg))).astype(df_ref.dtype)
                di_ref[rows, cols] = d_v.astype(di_ref.dtype)

            _interleave([head(p) for p in range(pair)])
            return carry

        lax.fori_loop(0, n_chunk, chunk, 0)

        @pl.when(n == n_blk - 1)
        def _():
            lb = _lower_bound(hl_ref[...])
            d_hl0 = dlb_ref[...] * (lb * (1.0 - lb))
            dhl_ref[...] = jnp.concatenate([d_hl0, -d_hl0], axis=0)

    out_blk = pl.BlockSpec((HG_TB, pair * LANE), lambda h, n: (rev(n), h))
    return pl.pallas_call(
        body, grid=(HG_HEADS // pair, n_blk),
        in_specs=_hg_specs(rev, pair) + [pl.BlockSpec((pair, n_chunk, HG_K, HG_K), lambda h, n: (h, rev(n), 0, 0)), out_blk],
        out_specs=[out_blk, out_blk, out_blk, pl.BlockSpec((2, pair * LANE), lambda h, n: (0, h))],
        out_shape=[_sds((s, HG_W), BF16)] * 3 + [_sds((2, HG_W), F32)],
        scratch_shapes=[pltpu.VMEM((pair, HG_K, HG_K), F32), pltpu.VMEM((1, pair * LANE), F32)],
        compiler_params=_params(2), name="hgrn_bwd")(*[proj] * (3 * pair), hl, states, d_o)


def _mod_part(c_all, w_shard, b_shard):
    n = w_shard.shape[1]
    tn = 512

    def body(c_ref, w_ref, b_ref, o_ref):
        o_ref[...] = _dot(c_ref[...].astype(BF16), w_ref[...].astype(BF16), NN) + b_ref[...]

    return pl.pallas_call(body, grid=(n // tn,),
                          in_specs=[pl.BlockSpec((N_DEV, D), lambda j: (0, 0)), pl.BlockSpec((D, tn), lambda j: (0, j)), pl.BlockSpec((1, tn), lambda j: (0, j))],
                          out_specs=pl.BlockSpec((N_DEV, tn), lambda j: (0, j)), out_shape=_sds((N_DEV, n), F32),
                          compiler_params=_params(1, 32 << 20), name="mod_part")(c_all, w_shard, b_shard)


def _adam_math(g, w, m, v):
    c1 = 1.0 / (1.0 - ADAM_B1 ** ADAM_STEP)
    c2 = 1.0 / (1.0 - ADAM_B2 ** ADAM_STEP)
    m2 = ADAM_B1 * m + (1.0 - ADAM_B1) * g
    v2 = ADAM_B2 * v + (1.0 - ADAM_B2) * (g * g)
    return -ADAM_LR * ((m2 * c1) / (jnp.sqrt(v2 * c2) + ADAM_EPS) + ADAM_WD * w), m2, v2


def _update_w_ada(c_all_t, dmod_cols, w, m, v, deps=()):
    n = dmod_cols.shape[1]
    tn = 256

    def body(c_ref, d_ref, w_ref, m_ref, v_ref, *rest):
        g_ref, dl_ref, m2_ref, v2_ref = rest[len(deps):]
        cv = c_ref[...].astype(BF16).astype(F32)
        dv = d_ref[...].astype(BF16).astype(F32)
        g = cv[:, 0:1] * dv[0:1, :]
        for k in range(1, N_DEV):
            g = g + cv[:, k:k + 1] * dv[k:k + 1, :]
        g_ref[...] = g
        dl_ref[...], m2_ref[...], v2_ref[...] = _adam_math(g, w_ref[...], m_ref[...], v_ref[...])

    blk = pl.BlockSpec((D, tn), lambda j: (0, j))
    return pl.pallas_call(body, grid=(n // tn,),
                          in_specs=[pl.BlockSpec((D, N_DEV), lambda j: (0, 0)), pl.BlockSpec((N_DEV, tn), lambda j: (0, j)), blk, blk, blk]
                          + [pl.BlockSpec(memory_space=pl.ANY)] * len(deps),
                          out_specs=[blk] * 4, out_shape=[_sds((D, n), F32)] * 4,
                          compiler_params=_params(1, 48 << 20), name="adamw_w_ada")(c_all_t, dmod_cols, w, m, v, *deps)


def _row_tile(r, c, max_elems=1 << 18):
    if r * c <= max_elems or r % 8:
        return r
    best = 8
    for t in range(8, r + 1, 8):
        if r % t == 0 and t * c <= max_elems:
            best = t
    return best


WIDE_TILE = 5 << 17
PAIR_TILE = 3 << 19


def _adamw(pieces, w, m, v, name, own=None, max_elems=1 << 18):
    parts = list(pieces) if isinstance(pieces, (list, tuple)) else [pieces]
    owns = [] if own is None else (list(own) if isinstance(own, (list, tuple)) else [own])
    n_o, n_p = len(owns), len(parts)
    p, r = parts[0].shape[:2]
    c = sum(t.shape[2] for t in parts)
    tr = _row_tile(r, c, max_elems)

    def body(*refs):
        w_ref, m_ref, v_ref, *outs = refs[n_o + n_p:]
        cols = []
        for j, p_ref in enumerate(refs[n_o:n_o + n_p]):
            g = p_ref[0].astype(F32)
            if owns:
                g = refs[j][...].astype(F32) + g
            for k in range(1, p):
                g = g + p_ref[k].astype(F32)
            cols.append(g)
        g = cols[0] if n_p == 1 else jnp.concatenate(cols, axis=1)
        outs[0][...] = g
        outs[1][...], outs[2][...], outs[3][...] = _adam_math(g, w_ref[...], m_ref[...], v_ref[...])

    blk = pl.BlockSpec((tr, c), lambda i: (i, 0))
    in_specs = ([pl.BlockSpec((tr, t.shape[1]), lambda i: (i, 0)) for t in owns]
                + [pl.BlockSpec((p, tr, t.shape[2]), lambda i: (0, i, 0)) for t in parts] + [blk, blk, blk])
    return pl.pallas_call(body, grid=(r // tr,), in_specs=in_specs,
                          out_specs=[blk] * 4, out_shape=[_sds((r, c), F32)] * 4,
                          compiler_params=_params(1, 48 << 20), name=name)(*owns, *parts, w, m, v)


def _my_coords():
    return lax.axis_index("x"), lax.axis_index("y"), lax.axis_index("c")


def _flip(coords, k):
    x, y, c = coords
    return (1 - x if k & 4 else x, 1 - y if k & 2 else y, 1 - c if k & 1 else c)


def _lin(coords):
    return 4 * coords[0] + 2 * coords[1] + coords[2]


def _exchange_small(x3, bcast, name):
    n = x3.shape[2]

    def body(x_ref, o_ref, send_sems, recv_sems):
        me = _my_coords()
        my_id = _lin(me)
        o_ref[pl.ds(my_id, 1)] = x_ref[pl.ds(0 if bcast else my_id, 1)]
        copies = []
        for k in range(1, N_DEV):
            peer = _flip(me, k)
            src = x_ref.at[0 if bcast else _lin(peer)]
            cp = pltpu.make_async_remote_copy(src_ref=src, dst_ref=o_ref.at[my_id], send_sem=send_sems.at[k], recv_sem=recv_sems.at[k],
                                              device_id=peer, device_id_type=MESH)
            cp.start()
            copies.append(cp)
        for k in range(1, N_DEV):
            peer = _flip(me, k)
            pltpu.make_async_remote_copy(src_ref=x_ref.at[0], dst_ref=o_ref.at[_lin(peer)], send_sem=send_sems.at[k], recv_sem=recv_sems.at[k],
                                         device_id=peer, device_id_type=MESH).wait_recv()
        for cp in copies:
            cp.wait_send()

    vm = pl.BlockSpec(memory_space=pltpu.VMEM)
    return pl.pallas_call(body, in_specs=[vm], out_specs=vm, out_shape=_sds((N_DEV, 1, n), F32),
                          scratch_shapes=[pltpu.SemaphoreType.DMA((N_DEV,)), pltpu.SemaphoreType.DMA((N_DEV,))], name=name)(x3)


HBM_SPEC = pl.BlockSpec(memory_space=pltpu.HBM)
SEM_SPEC = pl.BlockSpec(memory_space=pltpu.SEMAPHORE)
ANY_SPEC = pl.BlockSpec(memory_space=pl.ANY)
DATAFLOW = pltpu.SideEffectType.DATAFLOW_SIDE_EFFECTING
GATHER_FLIPS = (1, 2, 4, 6)
PASS_FLIPS = (2, 4, 6)
TOKEN = (8, LANE)


def _hbm(t):
    return pltpu.with_memory_space_constraint(t, pltpu.HBM)


def _hbm_like(ts):
    return [pltpu.HBM(t.shape, t.dtype) for t in ts]


def _split_start(issue, srcs, lands, n_sem, name, deps=()):
    n, nb, nd = len(srcs), len(srcs) + len(lands), len(deps)

    def body(*refs):
        issue(refs[:n], refs[n:nb], refs[nb + nd], refs[nb + nd + 1])
        refs[-1][...] = jnp.zeros(TOKEN, F32)

    outs = pl.pallas_call(
        body, name=name,
        out_shape=(pltpu.SemaphoreType.DMA((n_sem,)), pltpu.SemaphoreType.DMA((n_sem,)), *_hbm_like(srcs), *_hbm_like(lands), _sds(TOKEN, F32)),
        in_specs=[HBM_SPEC] * nb + [ANY_SPEC] * nd,
        out_specs=(SEM_SPEC, SEM_SPEC, *[HBM_SPEC] * nb, pl.BlockSpec(memory_space=pltpu.VMEM)),
        input_output_aliases={i: 2 + i for i in range(nb)},
        compiler_params=pltpu.CompilerParams(has_side_effects=DATAFLOW))(*[_hbm(t) for t in srcs], *[_hbm(t) for t in lands], *deps)
    return dict(sems=outs[:2], thru=list(outs[2:2 + nb]), token=outs[-1], n=n)


def _split_wait(finish, handle, after, name):
    n = handle["n"]
    thru = handle["thru"]
    nb = len(thru)

    def body(*refs):
        finish(refs[:n], refs[n:nb], refs[nb], refs[nb + 1])

    outs = pl.pallas_call(
        body, name=name, out_shape=_hbm_like(thru), in_specs=[HBM_SPEC] * nb + [SEM_SPEC, SEM_SPEC] + [ANY_SPEC] * len(after),
        out_specs=[HBM_SPEC] * nb, input_output_aliases={i: i for i in range(nb)},
        compiler_params=pltpu.CompilerParams(has_side_effects=DATAFLOW))(*thru, *handle["sems"], *after)
    return list(outs[:n]), list(outs[n:])


def _gather_start(shards, name, deps=()):
    n = len(shards)
    my_id = _lin(_my_coords())
    lands = [lax.dynamic_update_slice(lax.empty((N_DEV,) + t.shape, t.dtype), t[None], (my_id, 0, 0)) for t in shards]

    def issue(src, land, send_sems, recv_sems):
        me = _my_coords()
        for w in range(n):
            for j, k in enumerate(GATHER_FLIPS):
                q = len(GATHER_FLIPS) * w + j
                pltpu.make_async_remote_copy(src_ref=src[w], dst_ref=land[w].at[_lin(me)], send_sem=send_sems.at[q], recv_sem=recv_sems.at[q],
                                             device_id=_flip(me, k), device_id_type=MESH).start()

    return _split_start(issue, shards, lands, len(GATHER_FLIPS) * n, name, deps)


def _gather_wait(handle, after, name):
    n = handle["n"]

    def finish(src, land, send_sems, recv_sems):
        me = _my_coords()
        for w in range(n):
            for j, k in enumerate(GATHER_FLIPS):
                q = len(GATHER_FLIPS) * w + j
                peer = _flip(me, k)
                cp = pltpu.make_async_remote_copy(src_ref=src[w], dst_ref=land[w].at[_lin(peer)], send_sem=send_sems.at[q], recv_sem=recv_sems.at[q],
                                                  device_id=peer, device_id_type=MESH)
                cp.wait_send()
                cp.wait_recv()

    return _split_wait(finish, handle, after, name)[1]


def _pass_copy(land, send_sems, recv_sems, w, j, arriving):
    me = _my_coords()
    blk = land[w].at[_lin(_flip(me, PASS_FLIPS[j] + (1 if arriving else 0)))]
    q = len(PASS_FLIPS) * w + j
    return pltpu.make_async_remote_copy(src_ref=blk, dst_ref=blk, send_sem=send_sems.at[q], recv_sem=recv_sems.at[q],
                                        device_id=_flip(me, 1), device_id_type=MESH)


def _pass_start(lands, name, deps=()):
    def issue(_, land, send_sems, recv_sems):
        for w in range(len(lands)):
            for j in range(len(PASS_FLIPS)):
                _pass_copy(land, send_sems, recv_sems, w, j, False).start()

    return _split_start(issue, [], lands, len(PASS_FLIPS) * len(lands), name, deps)


def _pass_wait(handle, after, name):
    def finish(_, land, send_sems, recv_sems):
        for w in range(len(handle["thru"])):
            for j in range(len(PASS_FLIPS)):
                _pass_copy(land, send_sems, recv_sems, w, j, False).wait_send()
                _pass_copy(land, send_sems, recv_sems, w, j, True).wait_recv()

    return _split_wait(finish, handle, after, name)[1]


def _gather_pass(lands, name):
    n = len(lands)
    n_p = len(PASS_FLIPS)

    def body(*refs):
        land = refs[n:2 * n]
        send_sems, recv_sems = refs[2 * n:]
        me = _my_coords()
        sibling = _flip(me, 1)
        sent = []
        for w in range(n):
            for j, k in enumerate(PASS_FLIPS):
                blk = land[w].at[_lin(_flip(me, k))]
                cp = pltpu.make_async_remote_copy(src_ref=blk, dst_ref=blk, send_sem=send_sems.at[n_p * w + j], recv_sem=recv_sems.at[n_p * w + j],
                                                  device_id=sibling, device_id_type=MESH)
                cp.start()
                sent.append(cp)
        for w in range(n):
            for j, k in enumerate(PASS_FLIPS):
                blk = land[w].at[_lin(_flip(me, k + 1))]
                pltpu.make_async_remote_copy(src_ref=blk, dst_ref=blk, send_sem=send_sems.at[n_p * w + j], recv_sem=recv_sems.at[n_p * w + j],
                                             device_id=sibling, device_id_type=MESH).wait_recv()
        for cp in sent:
            cp.wait_send()

    return pl.pallas_call(body, in_specs=[ANY_SPEC] * n, out_specs=[ANY_SPEC] * n, out_shape=[_sds(t.shape, t.dtype) for t in lands],
                          input_output_aliases={i: i for i in range(n)},
                          scratch_shapes=[pltpu.SemaphoreType.DMA((n_p * n,)), pltpu.SemaphoreType.DMA((n_p * n,))], name=name)(*lands)


CHIP_FLIPS = (0, 2, 4, 6)


def _pair_copy(src, land, send_sems, recv_sems, w, j):
    me = _my_coords()
    q = len(CHIP_FLIPS) * w + j
    return pltpu.make_async_remote_copy(src_ref=src[w].at[_lin(_flip(me, CHIP_FLIPS[j] + 1))], dst_ref=land[w].at[j], send_sem=send_sems.at[q],
                                        recv_sem=recv_sems.at[q], device_id=_flip(me, 1), device_id_type=MESH)


def _pair_exchange(grads, name):
    n = len(grads)

    def body(*refs):
        src, land = refs[:n], refs[n:2 * n]
        send_sems, recv_sems = refs[2 * n:]
        sent = [_pair_copy(src, land, send_sems, recv_sems, w, j) for w in range(n) for j in range(len(CHIP_FLIPS))]
        for cp in sent:
            cp.start()
        for cp in sent:
            cp.wait_recv()
        for cp in sent:
            cp.wait_send()

    outs = pl.pallas_call(body, in_specs=[ANY_SPEC] * n, out_specs=[ANY_SPEC] * n,
                          out_shape=[_sds((len(CHIP_FLIPS),) + g.shape[1:], g.dtype) for g in grads],
                          scratch_shapes=[pltpu.SemaphoreType.DMA((len(CHIP_FLIPS) * n,))] * 2, name=name)(*grads)
    return list(outs)


def _pair_start(grads, name, deps=()):
    n = len(grads)
    lands = [lax.empty((len(CHIP_FLIPS),) + g.shape[1:], g.dtype) for g in grads]

    def issue(src, land, send_sems, recv_sems):
        for w in range(n):
            for j in range(len(CHIP_FLIPS)):
                _pair_copy(src, land, send_sems, recv_sems, w, j).start()

    return _split_start(issue, grads, lands, len(CHIP_FLIPS) * n, name, deps)


def _pair_wait(handle, after, name):
    n = handle["n"]

    def finish(src, land, send_sems, recv_sems):
        for w in range(n):
            for j in range(len(CHIP_FLIPS)):
                cp = _pair_copy(src, land, send_sems, recv_sems, w, j)
                cp.wait_send()
                cp.wait_recv()

    return _split_wait(finish, handle, after, name)


def _pair_add(grad, theirs, name):
    p, r, c = theirs.shape
    tr = _row_tile(r, c, PAIR_TILE)
    me = _my_coords()
    ids = jnp.stack([_lin(_flip(me, k)) for k in CHIP_FLIPS]).astype(jnp.int32)

    def body(ids_ref, a_ref, b_ref, o_ref):
        o_ref[...] = (a_ref[...].astype(F32) + b_ref[...].astype(F32)).astype(o_ref.dtype)

    blk = pl.BlockSpec((None, tr, c), lambda j, i, ids_ref: (j, i, 0))
    return pl.pallas_call(
        body, out_shape=_sds((p, r, c), theirs.dtype), compiler_params=_params(2), name=name,
        grid_spec=pltpu.PrefetchScalarGridSpec(
            num_scalar_prefetch=1, grid=(p, r // tr),
            in_specs=[pl.BlockSpec((None, tr, c), lambda j, i, ids_ref: (ids_ref[j], i, 0)), blk], out_specs=blk))(ids, grad, theirs)


def _chips_start(parts, name, deps=()):
    n = len(parts)
    n_c = len(CHIP_FLIPS) - 1
    lands = [lax.empty((n_c,) + t.shape[1:], t.dtype) for t in parts]

    def issue(src, land, send_sems, recv_sems):
        me = _my_coords()
        for w in range(n):
            for j in range(1, n_c + 1):
                q = n_c * w + j - 1
                pltpu.make_async_remote_copy(src_ref=src[w].at[j], dst_ref=land[w].at[j - 1], send_sem=send_sems.at[q], recv_sem=recv_sems.at[q],
                                             device_id=_flip(me, CHIP_FLIPS[j]), device_id_type=MESH).start()

    return _split_start(issue, parts, lands, n_c * n, name, deps)


def _chips_wait(handle, after, name):
    n = handle["n"]
    n_c = len(CHIP_FLIPS) - 1

    def finish(src, land, send_sems, recv_sems):
        me = _my_coords()
        for w in range(n):
            for j in range(1, n_c + 1):
                q = n_c * w + j - 1
                cp = pltpu.make_async_remote_copy(src_ref=src[w].at[j], dst_ref=land[w].at[j - 1], send_sem=send_sems.at[q], recv_sem=recv_sems.at[q],
                                                  device_id=_flip(me, CHIP_FLIPS[j]), device_id_type=MESH)
                cp.wait_send()
                cp.wait_recv()

    return _split_wait(finish, handle, after, name)


def _after(t, *tokens):
    for tok in tokens:
        t = t + tok[0:1, 0:1]
    return t


def _rope_tables(positions):
    half = ROT // 2
    inv_freq = ROPE_THETA ** (-jnp.arange(0, ROT, 2, dtype=F32) / ROT)
    ang = positions.astype(F32).reshape(-1, 1) * inv_freq
    cos, sin = jnp.cos(ang), jnp.sin(ang)
    s = ang.shape[0]
    pad = jnp.zeros((s, HEAD_DIM - ROT), F32)
    zero = jnp.zeros((s, half), F32)
    two = lambda t: jnp.concatenate([t, t], axis=1)
    return (two(jnp.concatenate([cos, cos, pad + 1.0], axis=1)), two(jnp.concatenate([-sin, zero, pad], axis=1)),
            two(jnp.concatenate([zero, sin, pad], axis=1)))


def _local_step(x, tgt, tabs, mod, sinks_pad, hl, hg_norm, g_pre_mix, g_post_mix, g_pre_ffn, g_post_ffn, weights, prefetch, scatter, scatter_on):
    s = x.shape[0]
    h1 = _pre_fwd(x, g_pre_mix, mod, 1, 0, "pre_mix_fwd")
    (w_in_a,) = weights("in_a", h1)
    proj = _mm_nt(h1, w_in_a, 256, IN_COLS // 2, D // 2, F32, "proj_mm_a", a_col=0)
    (w_in_b,) = weights("in_b", proj)
    proj = _mm_nt(h1, w_in_b, 256, IN_COLS // 2, D // 2, F32, "proj_mm_b", add=proj, a_col=1)
    att = _attn_fwd(proj, tabs, _after(sinks_pad, prefetch("mix", proj)))
    o_raw, states = _hgrn_fwd(proj, hl)
    ohg = _hgout_fwd(o_raw, proj, hg_norm)
    w_attn_dm, w_hgrn_dm, w_out = weights("mix", ohg)
    natural = lambda w_dm: w_dm.transpose(1, 0, 2).reshape(w_dm.shape[1], D)
    w_attn, w_hgrn = natural(w_attn_dm), natural(w_hgrn_dm)
    y_a = _mm_nn(att, w_attn, s, 512, ATT_W, F32, "attn_proj_mm")
    y_h = _mm_nn(ohg, w_hgrn, s, 512, HG_W, F32, "hgrn_proj_mm")
    merged = _merge_fwd(y_a, y_h, proj)
    y = _mm_nn(merged, w_out, s, 512, D, F32, "out_mm")
    x1 = _post_fwd(x, y, g_post_mix, mod, 2, "post_mix_fwd")
    h2 = _pre_fwd(x1, g_pre_ffn, _after(mod, prefetch("ffn_in", x1)), 4, 3, "pre_ffn_fwd")
    (w_ffn_in_dm,) = weights("ffn_in", h2)
    gu = _mm_nn_dm(h2, w_ffn_in_dm, s // 2, F32, "ffn_in_mm")
    act = _swiglu_fwd(gu, deps=[prefetch("ffn_out", gu)])
    (w_ffn_out,) = weights("ffn_out", act)
    y2 = _mm_nn(act, w_ffn_out, 512, 512, FFN, F32, "ffn_out_mm")
    err, loss, dy2, d_gate2, dg_post_ffn = _post_loss_bwd(x1, y2, g_post_ffn, mod, 5, tgt, "post_ffn_loss_bwd")
    gw_ffn_out = _mm_tn(act, dy2, 512, D, BF16, "ffn_out_dw")
    t_pair = scatter([gw_ffn_out.reshape(N_DEV, FFN // N_DEV, D)], "ffn_out")
    d_act = _mm_nt(dy2, w_ffn_out, s, 512, D, F32, "ffn_out_dx", deps=[t_pair])
    dgu = _swiglu_bwd(d_act, gu)
    t_out = scatter_on("ffn_out", dgu)
    gw_ffn_in = _mm_tn_dm(h2, dgu, 1024, BF16, "ffn_in_dw")
    t_pair = scatter([gw_ffn_in], "ffn_in")
    dh2 = _mm_nt_dm(dgu, w_ffn_in_dm, s, 1024, F32, "ffn_in_dx", deps=[t_pair])
    mod = _after(mod, t_out)
    dx1, d_shift2, d_scale2, dg_pre_ffn = _pre_bwd([dh2], x1, err, g_pre_ffn, mod, 4, "pre_ffn_bwd")
    dy, d_gate1, dg_post_mix = _post_bwd(dx1, y, g_post_mix, mod, 2, "post_mix_bwd")
    t_in = scatter_on("ffn_in", dy)
    d_merged = _mm_nt(dy, w_out, s, 512, D, F32, "out_dx")
    gw_out = _mm_tn(merged, dy, 512, D, BF16, "out_dw")
    dy_a, dy_h, d_gate_a, d_gate_h = _merge_bwd(d_merged, y_a, y_h, proj)
    gw_attn = _mm_tn_dm(att, dy_a, ATT_W, BF16, "attn_proj_dw")
    gw_hgrn = _mm_tn_dm(ohg, dy_h, HG_W, BF16, "hgrn_proj_dw")
    t_pair = scatter([gw_attn, gw_hgrn, gw_out.reshape(N_DEV, D // N_DEV, D)], "mix")
    d_att = _mm_nt(dy_a, w_attn, s, 512, D, F32, "attn_proj_dx")
    d_ohg = _mm_nt(dy_h, w_hgrn, s, 512, D, F32, "hgrn_proj_dx", deps=[t_pair])
    d_o, d_gh, d_hg_norm = _hgout_bwd(d_ohg, o_raw, proj, _after(hg_norm, t_in))
    d_qh, d_fh, d_ih, d_hl = _hgrn_bwd(proj, hl, states, d_o)
    t_mix = scatter_on("mix", d_qh)
    d_qa, d_ka, d_va, d_sinks = _attn_bwd(proj, tabs, _after(sinks_pad, t_mix), d_att)
    d_proj = jnp.concatenate([d_qa, d_ka.astype(BF16), d_va.astype(BF16), d_qh, d_fh, d_ih, d_gh, d_gate_a, d_gate_h], axis=1)
    dh1 = [_mm_nn(d_proj, w_half, s // 2, 512, IN_COLS // 2, F32, "proj_dx_" + tag) for tag, w_half in (("a", w_in_a), ("b", w_in_b))]
    grad_x, d_shift1, d_scale1, dg_pre_mix = _pre_bwd(dh1, x, dx1, g_pre_mix, mod, 1, "pre_mix_bwd")
    d_mod = jnp.concatenate([d_shift1, d_scale1, d_gate1, d_shift2, d_scale2, d_gate2], axis=1)
    small = [d_mod, dg_pre_mix, dg_post_mix, dg_pre_ffn, dg_post_ffn, d_hl.reshape(1, 2 * HG_W), d_hg_norm, d_sinks]
    return loss, grad_x, small, h1, d_proj


def kernel(x, c, positions, w_ada, b_ada, g_pre_mix, g_post_mix, g_pre_ffn, g_post_ffn, w_in, attn_sinks, w_attn_proj, hg_lower_bounds, hg_norm, w_hgrn_proj, w_out, w_ffn_in, w_ffn_out, loss_target, m_w_ada, m_b_ada, m_g_pre_mix, m_g_post_mix, m_g_pre_ffn, m_g_post_ffn, m_w_in, m_attn_sinks, m_w_attn_proj, m_hg_lower_bounds, m_hg_norm, m_w_hgrn_proj, m_w_out, m_w_ffn_in, m_w_ffn_out, v_w_ada, v_b_ada, v_g_pre_mix, v_g_post_mix, v_g_pre_ffn, v_g_post_ffn, v_w_in, v_attn_sinks, v_w_attn_proj, v_hg_lower_bounds, v_hg_norm, v_w_hgrn_proj, v_w_out, v_w_ffn_in, v_w_ffn_out):
    my_id = _lin(_my_coords())
    s = x.shape[1]
    n_ada = w_ada.shape[2]

    c_all = _exchange_small(c.reshape(1, 1, D), True, "gather_c").reshape(N_DEV, D)
    b_cols = lax.dynamic_slice(b_ada, (0, my_id * n_ada), (1, n_ada))
    mod_part = _mod_part(c_all, w_ada[0], b_cols)
    mod = _exchange_small(mod_part.reshape(N_DEV, 1, n_ada), False, "scatter_mod").reshape(1, N_MOD * D)
    groups = {"in_a": [w_in[0].T[:, :D // 2]], "in_b": [w_in[0].T[:, D // 2:]], "mix": [w_attn_proj[0], w_hgrn_proj[0], w_out[0]],
              "ffn_in": [w_ffn_in[0]], "ffn_out": [w_ffn_out[0]]}

    def start(group, dep):
        shards, dep = lax.optimization_barrier((groups[group], dep))
        return _gather_start([t.astype(BF16) for t in shards], "gather_start_" + group, deps=[dep])

    gathers = {"in_a": start("in_a", mod)}
    gathers["in_b"] = start("in_b", gathers["in_a"]["token"])
    gathers["mix"] = start("mix", gathers["in_b"]["token"])
    gathers["ffn_in"] = start("ffn_in", gathers["mix"]["token"])
    gathers["ffn_out"] = start("ffn_out", gathers["ffn_in"]["token"])

    passes = {}

    def prefetch(group, after):
        lands = _gather_wait(gathers[group], [after], "gather_wait_" + group)
        passes[group] = _pass_start(lands, "gather_pass_start_" + group)
        return passes[group]["token"]

    def weights(group, after):
        if group in passes:
            lands = _pass_wait(passes[group], [after], "gather_pass_wait_" + group)
        else:
            after = [after, gathers["ffn_out"]["token"]]
            lands = _gather_pass(_gather_wait(gathers[group], after, "gather_wait_" + group), "gather_pass_" + group)
        if group in ("in_a", "in_b"):
            return (lands[0].reshape(IN_COLS, D // 2),)
        if group == "mix":
            return lands[0], lands[1], lands[2].reshape(D, D)
        return (lands[0],) if group == "ffn_in" else (lands[0].reshape(FFN, D),)

    pairs, scatters = {}, {}

    def scatter(grads, group):
        pairs[group] = _pair_start(grads, "scatter_pair_" + group)
        return pairs[group]["token"]

    def scatter_on(group, after):
        if group in pairs:
            local, theirs = _pair_wait(pairs[group], [after], "scatter_pair_wait_" + group)
        else:
            local, theirs = after, _pair_exchange(after, "scatter_pair_" + group)
        parts = [_pair_add(g, t, "scatter_pair_add_%s_%d" % (group, k)) for k, (g, t) in enumerate(zip(local, theirs))]
        scatters[group] = _chips_start(parts, "scatter_start_" + group)
        return scatters[group]["token"]

    sinks_pad = jnp.pad(attn_sinks, ((0, 0), (0, LANE - ATT_HEADS)))
    loss, grad_x, small, h1, d_proj = _local_step(
        x[0], loss_target[0], _rope_tables(positions), _after(mod, gathers["ffn_out"]["token"]), sinks_pad, hg_lower_bounds, hg_norm, g_pre_mix, g_post_mix, g_pre_ffn, g_post_ffn,
        weights, prefetch, scatter, scatter_on)
    small = small + [jnp.pad(loss, ((0, 0), (0, LANE - 1)))]

    sizes = [t.shape[1] for t in small]
    parts = _exchange_small(jnp.concatenate(small, axis=1).reshape(1, 1, sum(sizes)), True, "gather_small_grads")
    dep = parts
    for b_col, half in enumerate(("in_a", "in_b")):
        gw_half = _mm_tn(d_proj, h1, 256, D // 2, BF16, "proj_dw_" + half, deps=[dep], b_col=b_col)
        dep = scatter_on(half, [gw_half.reshape(N_DEV, IN_COLS // N_DEV, D // 2)])
    offs = [sum(sizes[:k]) for k in range(len(sizes))]
    piece = lambda k, n=None: parts[:, :, offs[k]:offs[k] + (sizes[k] if n is None else n)]
    loss = jnp.sum(piece(8, 1))
    small_w = [(piece(0), b_ada, m_b_ada, v_b_ada), (piece(1), g_pre_mix, m_g_pre_mix, v_g_pre_mix),
               (piece(2), g_post_mix, m_g_post_mix, v_g_post_mix), (piece(3), g_pre_ffn, m_g_pre_ffn, v_g_pre_ffn),
               (piece(4), g_post_ffn, m_g_post_ffn, v_g_post_ffn),
               (piece(5).reshape(N_DEV, 2, HG_W), hg_lower_bounds, m_hg_lower_bounds, v_hg_lower_bounds),
               (piece(6), hg_norm, m_hg_norm, v_hg_norm), (piece(7, ATT_HEADS), attn_sinks, m_attn_sinks, v_attn_sinks)]
    names = ["b_ada", "g_pre_mix", "g_post_mix", "g_pre_ffn", "g_post_ffn", "hg_lower_bounds", "hg_norm", "attn_sinks"]
    res = {n: _adamw(p, w, m, v, "adamw_" + n) for n, (p, w, m, v) in zip(names, small_w)}

    dmod_cols = lax.dynamic_slice(parts.reshape(N_DEV, -1), (0, my_id * n_ada), (N_DEV, n_ada))
    res["w_ada"] = list(_update_w_ada(c_all.T, dmod_cols, w_ada[0], m_w_ada[0], v_w_ada[0], deps=[scatters["in_b"]["token"]]))

    big = {"ffn_out": [("w_ffn_out", w_ffn_out, m_w_ffn_out, v_w_ffn_out)], "ffn_in": [("w_ffn_in", w_ffn_in, m_w_ffn_in, v_w_ffn_in)],
           "mix": [("w_attn_proj", w_attn_proj, m_w_attn_proj, v_w_attn_proj), ("w_hgrn_proj", w_hgrn_proj, m_w_hgrn_proj, v_w_hgrn_proj),
                   ("w_out", w_out, m_w_out, v_w_out)]}
    after = [scatters["in_b"]["token"]]
    for group, members in big.items():
        local, lands = _chips_wait(scatters[group], after, "scatter_wait_" + group)
        for (n, w, m, v), mine, land in zip(members, local, lands):
            res[n] = _adamw(land, w[0], m[0], v[0], "adamw_" + n, own=mine[0], max_elems=WIDE_TILE)
            after = after + [res[n][1]]
    after = [res[n][1] for n in res]
    halves = [_chips_wait(scatters[half], after, "scatter_wait_" + half) for half in ("in_a", "in_b")]
    own = [local[0][0] for local, _ in halves]
    land = [lands[0] for _, lands in halves]
    res["w_in"] = [t.T for t in _adamw(land, w_in[0].T, m_w_in[0].T, v_w_in[0].T, "adamw_w_in", own=own, max_elems=WIDE_TILE)]

    order = ["w_ada", "b_ada", "g_pre_mix", "g_post_mix", "g_pre_ffn", "g_post_ffn", "w_in", "attn_sinks", "w_attn_proj",
             "hg_lower_bounds", "hg_norm", "w_hgrn_proj", "w_out", "w_ffn_in", "w_ffn_out"]
    lead = {"w_ada", "w_in", "w_attn_proj", "w_hgrn_proj", "w_out", "w_ffn_in", "w_ffn_out"}
    outs = [loss, grad_x[None]]
    for k in range(4):
        outs += [res[n][k][None] if n in lead else res[n][k] for n in order]
    return tuple(outs)
```
